```python
import jax, jax.numpy as jnp
from jax import lax
import numpy as np

D_MODEL = 1024
BATCH = 8
SEQ = 2048
DEPTH = 1

GRID_W = 64
CTX_LEN = 256
D_MIX = D_MODEL
D_ATTN = D_MIX // 2
D_CONV = D_MIX - D_ATTN
HEAD_DIM = 64
N_HEADS = D_ATTN // HEAD_DIM
WIN_H = 8
WIN_W = 16
CONV_K = 3
ROPE_THETA = 10000.0
RMS_EPS = 1e-6
SPLIT_POINTS = (D_ATTN, 2 * D_ATTN, 3 * D_ATTN, 4 * D_ATTN,
                4 * D_ATTN + D_CONV, 4 * D_ATTN + 2 * D_CONV, 4 * D_ATTN + 3 * D_CONV)
D_IN = 4 * D_ATTN + 4 * D_CONV

kernel_name = "hybrid_natten_shortconv_dit_block"


def rmsnorm(x, g):
    xf = x.astype(jnp.float32)
    y = xf * lax.rsqrt(jnp.mean(xf * xf, axis=-1, keepdims=True) + RMS_EPS)
    return (y * g.astype(jnp.float32)).astype(x.dtype)


def adaln(cond, w_ada, b_ada):
    m = jax.nn.silu(cond) @ w_ada + b_ada
    return jnp.split(m, 3, axis=-1)


def heads(t):
    return t.reshape(*t.shape[:-1], N_HEADS, HEAD_DIM)


def axial_rope(t, n_cols):
    S = t.shape[1]
    nf = HEAD_DIM // 4
    inv = ROPE_THETA ** (-jnp.arange(nf, dtype=jnp.float32) / nf)
    pos = jnp.arange(S, dtype=jnp.int32)
    row = (pos // n_cols).astype(jnp.float32)
    col = (pos % n_cols).astype(jnp.float32)
    ang = jnp.stack([row[:, None] * inv, col[:, None] * inv], axis=1)
    cos = jnp.cos(ang)[:, None, :, None, :]
    sin = jnp.sin(ang)[:, None, :, None, :]
    tf = t.astype(jnp.float32).reshape(*t.shape[:-1], 2, 2, nf)
    rot = jnp.stack([-tf[..., 1, :], tf[..., 0, :]], axis=-2)
    return (tf * cos + rot * sin).reshape(t.shape).astype(t.dtype)


def neighbourhood_attention(q, k, v, kc, vc, rpb):
    Bn, S, H, Dh = q.shape
    rows = S // GRID_W
    wh = min(WIN_H, rows)
    scale = Dh ** -0.5
    q_plain = q.reshape(Bn, rows, GRID_W, H, Dh)
    q_rot = axial_rope(q, GRID_W).reshape(Bn, rows, GRID_W, H, Dh)
    k_rot = axial_rope(k, GRID_W).reshape(Bn, rows, GRID_W, H, Dh)
    v_g = v.reshape(Bn, rows, GRID_W, H, Dh)
    qi = jnp.arange(rows)
    row_start = jnp.clip(qi - wh // 2, 0, rows - wh)
    row_idx = row_start[:, None] + jnp.arange(wh)[None, :]
    k_rows = k_rot[:, row_idx]
    v_rows = v_g[:, row_idx]
    cols = jnp.arange(GRID_W)
    col_start = jnp.clip(cols - WIN_W // 2, 0, GRID_W - WIN_W)
    col_valid = (cols[None, :] >= col_start[:, None]) & (cols[None, :] < col_start[:, None] + WIN_W)
    dr = row_idx - qi[:, None] + (WIN_H - 1)
    dc = jnp.clip(cols[None, :] - cols[:, None] + (WIN_W - 1), 0, 2 * WIN_W - 2)
    bias = rpb[:, dr[:, None, :, None], dc[None, :, None, :]].astype(jnp.float32)
    s_lat = jnp.einsum('biqhd,birkhd->bhiqrk', q_rot, k_rows,
                       preferred_element_type=jnp.float32) * scale + bias
    s_lat = jnp.where(col_valid[:, None, :], s_lat, -jnp.inf)
    s_ctx = jnp.einsum('biqhd,blhd->bhiql', q_plain, kc,
                       preferred_element_type=jnp.float32) * scale
    n_lat = wh * GRID_W
    s = jnp.concatenate([s_lat.reshape(*s_lat.shape[:4], n_lat), s_ctx], axis=-1)
    p = jax.nn.softmax(s, axis=-1).astype(v.dtype)
    p_lat = p[..., :n_lat].reshape(s_lat.shape)
    p_ctx = p[..., n_lat:]
    o = (jnp.einsum('bhiqrk,birkhd->biqhd', p_lat, v_rows)
         + jnp.einsum('bhiql,blhd->biqhd', p_ctx, vc))
    return o.reshape(Bn, S, H * Dh)


def context_attention(qc, kc, vc):
    Bn, L, H, Dh = qc.shape
    s = jnp.einsum('blhd,bmhd->bhlm', qc, kc, preferred_element_type=jnp.float32) * (Dh ** -0.5)
    p = jax.nn.softmax(s, axis=-1).astype(vc.dtype)
    return jnp.einsum('bhlm,bmhd->blhd', p, vc).reshape(Bn, L, H * Dh)


def centred_short_conv(u, w, b):
    L = u.shape[1]
    pad = CONV_K // 2
    up = jnp.pad(u, ((0, 0), (pad, pad), (0, 0)))
    y = b
    for i in range(CONV_K):
        y = y + up[:, i:i + L] * w[i]
    return y


def gated_short_conv(u, bg, cg, zc, conv_w, conv_b):
    return bg * centred_short_conv(cg * u, conv_w, conv_b) * jax.nn.silu(zc)


def hybrid_layer(x, ctx, c, c_ctx, w_ada, b_ada, norm_g, w_in, q_norm_g, k_norm_g,
                 rpb, conv_w, conv_b, w_out, update_ctx):
    shift, scale, gate = adaln(c, w_ada, b_ada)
    shift_c, scale_c, gate_c = adaln(c_ctx, w_ada, b_ada)
    h = rmsnorm(x, norm_g) * (1 + scale[:, None]) + shift[:, None]
    hc = rmsnorm(ctx, norm_g) * (1 + scale_c) + shift_c
    q, k, v, za, u, bg, cg, zc = jnp.split(h @ w_in, SPLIT_POINTS, axis=-1)
    q = rmsnorm(heads(q), q_norm_g)
    k = rmsnorm(heads(k), k_norm_g)
    if update_ctx:
        qc, kc, vc, zac, uc, bgc, cgc, zcc = jnp.split(hc @ w_in, SPLIT_POINTS, axis=-1)
    else:
        kc, vc = jnp.split(hc @ w_in[:, D_ATTN:3 * D_ATTN], 2, axis=-1)
    kc = rmsnorm(heads(kc), k_norm_g)
    vc = heads(vc)
    attn = neighbourhood_attention(q, k, heads(v), kc, vc, rpb) * jax.nn.silu(za)
    conv = gated_short_conv(u, bg, cg, zc, conv_w, conv_b)
    x_new = x + gate[:, None] * (jnp.concatenate([attn, conv], axis=-1) @ w_out)
    if update_ctx:
        qc = rmsnorm(heads(qc), q_norm_g)
        attn_c = context_attention(qc, kc, vc) * jax.nn.silu(zac)
        conv_c = gated_short_conv(uc, bgc, cgc, zcc, conv_w, conv_b)
        ctx_new = ctx + gate_c * (jnp.concatenate([attn_c, conv_c], axis=-1) @ w_out)
    else:
        ctx_new = ctx
    return x_new, ctx_new


def _fwd_setup_inputs(seed: int = 0) -> dict:
    key = jax.random.key(seed)
    ks = jax.random.split(key, 14)
    f32 = jnp.float32
    x = jax.random.normal(ks[0], (BATCH, SEQ, D_MODEL), f32)
    c = jax.random.normal(ks[1], (BATCH, D_MODEL), f32)
    ctx = jax.random.normal(ks[2], (BATCH, CTX_LEN, D_MODEL), f32)
    c_ctx = jax.random.normal(ks[3], (D_MODEL,), f32)
    w_ada = jax.random.normal(ks[4], (DEPTH, D_MODEL, 3 * D_MODEL), f32) * (0.5 * D_MODEL ** -0.5)
    b_ada = jax.random.normal(ks[5], (DEPTH, 3 * D_MODEL), f32) * 0.01
    norm_g = 1.0 + 0.1 * jax.random.normal(ks[6], (DEPTH, D_MODEL), f32)
    w_in = jax.random.normal(ks[7], (DEPTH, D_MODEL, D_IN), f32) * D_MODEL ** -0.5
    q_norm_g = 1.0 + 0.1 * jax.random.normal(ks[8], (DEPTH, HEAD_DIM), f32)
    k_norm_g = 1.0 + 0.1 * jax.random.normal(ks[9], (DEPTH, HEAD_DIM), f32)
    rpb = 0.1 * jax.random.normal(ks[10], (DEPTH, N_HEADS, 2 * WIN_H - 1, 2 * WIN_W - 1), f32)
    conv_w = jax.random.normal(ks[11], (DEPTH, CONV_K, D_CONV), f32) * CONV_K ** -0.5
    conv_b = 0.01 * jax.random.normal(ks[12], (DEPTH, D_CONV), f32)
    w_out = jax.random.normal(ks[13], (DEPTH, D_MIX, D_MODEL), f32) * D_MIX ** -0.5
    return {"x": x, "c": c, "ctx": ctx, "c_ctx": c_ctx, "w_ada": w_ada, "b_ada": b_ada,
            "norm_g": norm_g, "w_in": w_in, "q_norm_g": q_norm_g, "k_norm_g": k_norm_g,
            "rpb": rpb, "conv_w": conv_w, "conv_b": conv_b, "w_out": w_out}


def _fwd_reference(x, c, ctx, c_ctx, w_ada, b_ada, norm_g, w_in, q_norm_g, k_norm_g,
              rpb, conv_w, conv_b, w_out):
    for l in range(DEPTH):
        x, ctx = hybrid_layer(x, ctx, c, c_ctx, w_ada[l], b_ada[l], norm_g[l], w_in[l],
                              q_norm_g[l], k_norm_g[l], rpb[l], conv_w[l], conv_b[l], w_out[l],
                              update_ctx=(l < DEPTH - 1))
    return x


import jax as _jax
import jax.numpy as _jnp

TWIN_FORMAT = 'train_step'
FWD_PARAMS = ['x', 'c', 'ctx', 'c_ctx', 'w_ada', 'b_ada', 'norm_g', 'w_in', 'q_norm_g', 'k_norm_g', 'rpb', 'conv_w', 'conv_b', 'w_out']
TWIN_WEIGHTS = ['c_ctx', 'w_ada', 'b_ada', 'norm_g', 'w_in', 'q_norm_g', 'k_norm_g', 'rpb', 'conv_w', 'conv_b', 'w_out']
TWIN_DIFF_INPUT = 'x'
TWIN_INPUTS = ['x', 'c', 'ctx', 'c_ctx', 'w_ada', 'b_ada', 'norm_g', 'w_in', 'q_norm_g', 'k_norm_g', 'rpb', 'conv_w', 'conv_b', 'w_out', 'loss_target', 'm_c_ctx', 'm_w_ada', 'm_b_ada', 'm_norm_g', 'm_w_in', 'm_q_norm_g', 'm_k_norm_g', 'm_rpb', 'm_conv_w', 'm_conv_b', 'm_w_out', 'v_c_ctx', 'v_w_ada', 'v_b_ada', 'v_norm_g', 'v_w_in', 'v_q_norm_g', 'v_k_norm_g', 'v_rpb', 'v_conv_w', 'v_conv_b', 'v_w_out']
TWIN_OUTPUTS = ['loss', 'grad_x', 'grad_c_ctx', 'grad_w_ada', 'grad_b_ada', 'grad_norm_g', 'grad_w_in', 'grad_q_norm_g', 'grad_k_norm_g', 'grad_rpb', 'grad_conv_w', 'grad_conv_b', 'grad_w_out', 'delta_c_ctx', 'delta_w_ada', 'delta_b_ada', 'delta_norm_g', 'delta_w_in', 'delta_q_norm_g', 'delta_k_norm_g', 'delta_rpb', 'delta_conv_w', 'delta_conv_b', 'delta_w_out', 'new_m_c_ctx', 'new_m_w_ada', 'new_m_b_ada', 'new_m_norm_g', 'new_m_w_in', 'new_m_q_norm_g', 'new_m_k_norm_g', 'new_m_rpb', 'new_m_conv_w', 'new_m_conv_b', 'new_m_w_out', 'new_v_c_ctx', 'new_v_w_ada', 'new_v_b_ada', 'new_v_norm_g', 'new_v_w_in', 'new_v_q_norm_g', 'new_v_k_norm_g', 'new_v_rpb', 'new_v_conv_w', 'new_v_conv_b', 'new_v_w_out']
TWIN_LEAF_KINDS = {'loss': 'loss', 'grad_x': 'grad_x', 'grad_c_ctx': 'grad_w', 'grad_w_ada': 'grad_w', 'grad_b_ada': 'grad_w', 'grad_norm_g': 'grad_w', 'grad_w_in': 'grad_w', 'grad_q_norm_g': 'grad_w', 'grad_k_norm_g': 'grad_w', 'grad_rpb': 'grad_w', 'grad_conv_w': 'grad_w', 'grad_conv_b': 'grad_w', 'grad_w_out': 'grad_w', 'delta_c_ctx': 'delta_w', 'delta_w_ada': 'delta_w', 'delta_b_ada': 'delta_w', 'delta_norm_g': 'delta_w', 'delta_w_in': 'delta_w', 'delta_q_norm_g': 'delta_w', 'delta_k_norm_g': 'delta_w', 'delta_rpb': 'delta_w', 'delta_conv_w': 'delta_w', 'delta_conv_b': 'delta_w', 'delta_w_out': 'delta_w', 'new_m_c_ctx': 'new_m', 'new_m_w_ada': 'new_m', 'new_m_b_ada': 'new_m', 'new_m_norm_g': 'new_m', 'new_m_w_in': 'new_m', 'new_m_q_norm_g': 'new_m', 'new_m_k_norm_g': 'new_m', 'new_m_rpb': 'new_m', 'new_m_conv_w': 'new_m', 'new_m_conv_b': 'new_m', 'new_m_w_out': 'new_m', 'new_v_c_ctx': 'new_v', 'new_v_w_ada': 'new_v', 'new_v_b_ada': 'new_v', 'new_v_norm_g': 'new_v', 'new_v_w_in': 'new_v', 'new_v_q_norm_g': 'new_v', 'new_v_k_norm_g': 'new_v', 'new_v_rpb': 'new_v', 'new_v_conv_w': 'new_v', 'new_v_conv_b': 'new_v', 'new_v_w_out': 'new_v'}


def _forward(args):
    return _fwd_reference(*[args[k] for k in FWD_PARAMS])


def _output_shape():
    out = _jax.eval_shape(lambda: _forward(_fwd_setup_inputs(0)))
    return out.shape, out.dtype

N_MICROBATCH = 1
ADAM_LR = 0.001
ADAM_B1 = 0.9
ADAM_B2 = 0.999
ADAM_EPS = 1e-08
ADAM_WD = 0.01
ADAM_STEP = 10
PER_EXAMPLE_BATCH_AXIS = {'x': 0, 'c': 0, 'ctx': 0, 'loss_target': 0}
SHARED_INPUTS = []
_WEIGHT_DTYPES = {'c_ctx': _jnp.float32, 'w_ada': _jnp.float32, 'b_ada': _jnp.float32, 'norm_g': _jnp.float32, 'w_in': _jnp.float32, 'q_norm_g': _jnp.float32, 'k_norm_g': _jnp.float32, 'rpb': _jnp.float32, 'conv_w': _jnp.float32, 'conv_b': _jnp.float32, 'w_out': _jnp.float32}
MOMENT_SCALE = {'c_ctx': 2.470116e-02, 'w_ada': 6.040373e-01, 'b_ada': 1.480770e+00, 'norm_g': 2.376630e+00, 'w_in': 1.005540e-01, 'q_norm_g': 3.457249e-02, 'k_norm_g': 3.510704e-02, 'rpb': 7.840336e-04, 'conv_w': 6.764617e-01, 'conv_b': 6.782535e-02, 'w_out': 4.755213e-02}


def _to_microbatches(a, axis):
    t = _jnp.moveaxis(a, axis, 0)
    t = t.reshape((N_MICROBATCH, t.shape[0] // N_MICROBATCH) + t.shape[1:])
    return _jnp.moveaxis(t, 1, axis + 1)


def setup_inputs(seed: int = 0) -> dict:
    inp = _fwd_setup_inputs(seed)
    key = _jax.random.fold_in(_jax.random.key(seed), 7919)
    shape, _ = _output_shape()
    out = dict(inp)
    out["loss_target"] = _jax.random.normal(_jax.random.fold_in(key, 0), shape, _jnp.float32)
    for i, name in enumerate(TWIN_WEIGHTS):
        w = inp[name].astype(_jnp.float32)
        if MOMENT_SCALE is None:
            s = _jnp.sqrt(_jnp.mean(_jnp.square(w)) + 1e-30)
        else:
            s = MOMENT_SCALE[name]
        km, kv = _jax.random.split(_jax.random.fold_in(key, i + 1))
        out[name] = w
        out["m_" + name] = s * _jax.random.normal(km, w.shape, _jnp.float32)
        out["v_" + name] = (s * s) * _jax.random.uniform(kv, w.shape, _jnp.float32, 0.5, 1.5)
    if N_MICROBATCH > 1:
        for name, axis in PER_EXAMPLE_BATCH_AXIS.items():
            out[name] = _to_microbatches(out[name], axis)
    return {'x': out['x'], 'c': out['c'], 'ctx': out['ctx'], 'c_ctx': out['c_ctx'], 'w_ada': out['w_ada'], 'b_ada': out['b_ada'], 'norm_g': out['norm_g'], 'w_in': out['w_in'], 'q_norm_g': out['q_norm_g'], 'k_norm_g': out['k_norm_g'], 'rpb': out['rpb'], 'conv_w': out['conv_w'], 'conv_b': out['conv_b'], 'w_out': out['w_out'], 'loss_target': out['loss_target'], 'm_c_ctx': out['m_c_ctx'], 'm_w_ada': out['m_w_ada'], 'm_b_ada': out['m_b_ada'], 'm_norm_g': out['m_norm_g'], 'm_w_in': out['m_w_in'], 'm_q_norm_g': out['m_q_norm_g'], 'm_k_norm_g': out['m_k_norm_g'], 'm_rpb': out['m_rpb'], 'm_conv_w': out['m_conv_w'], 'm_conv_b': out['m_conv_b'], 'm_w_out': out['m_w_out'], 'v_c_ctx': out['v_c_ctx'], 'v_w_ada': out['v_w_ada'], 'v_b_ada': out['v_b_ada'], 'v_norm_g': out['v_norm_g'], 'v_w_in': out['v_w_in'], 'v_q_norm_g': out['v_q_norm_g'], 'v_k_norm_g': out['v_k_norm_g'], 'v_rpb': out['v_rpb'], 'v_conv_w': out['v_conv_w'], 'v_conv_b': out['v_conv_b'], 'v_w_out': out['v_w_out']}


def _loss(weights, diff, rest, loss_target):
    with _jax.named_scope("forward"):
        args = {**rest, TWIN_DIFF_INPUT: diff, **{k: w.astype(_WEIGHT_DTYPES[k]) for k, w in weights.items()}}
        y = _forward(args)
    with _jax.named_scope("loss_head"):
        err = _jnp.square(y.astype(_jnp.float32) - loss_target)
        return 0.5 * _jnp.sum(_jnp.mean(err, axis=-1)) if err.ndim else 0.5 * err


def _adamw(w, g, m, v):
    m = ADAM_B1 * m + (1.0 - ADAM_B1) * g
    v = ADAM_B2 * v + (1.0 - ADAM_B2) * _jnp.square(g)
    m_hat = m / (1.0 - ADAM_B1 ** ADAM_STEP)
    v_hat = v / (1.0 - ADAM_B2 ** ADAM_STEP)
    delta = -ADAM_LR * (m_hat / (_jnp.sqrt(v_hat) + ADAM_EPS) + ADAM_WD * w)
    return delta, m, v


def reference(x, c, ctx, c_ctx, w_ada, b_ada, norm_g, w_in, q_norm_g, k_norm_g, rpb, conv_w, conv_b, w_out, loss_target, m_c_ctx, m_w_ada, m_b_ada, m_norm_g, m_w_in, m_q_norm_g, m_k_norm_g, m_rpb, m_conv_w, m_conv_b, m_w_out, v_c_ctx, v_w_ada, v_b_ada, v_norm_g, v_w_in, v_q_norm_g, v_k_norm_g, v_rpb, v_conv_w, v_conv_b, v_w_out):
    given = dict(x=x, c=c, ctx=ctx, c_ctx=c_ctx, w_ada=w_ada, b_ada=b_ada, norm_g=norm_g, w_in=w_in, q_norm_g=q_norm_g, k_norm_g=k_norm_g, rpb=rpb, conv_w=conv_w, conv_b=conv_b, w_out=w_out, loss_target=loss_target, m_c_ctx=m_c_ctx, m_w_ada=m_w_ada, m_b_ada=m_b_ada, m_norm_g=m_norm_g, m_w_in=m_w_in, m_q_norm_g=m_q_norm_g, m_k_norm_g=m_k_norm_g, m_rpb=m_rpb, m_conv_w=m_conv_w, m_conv_b=m_conv_b, m_w_out=m_w_out, v_c_ctx=v_c_ctx, v_w_ada=v_w_ada, v_b_ada=v_b_ada, v_norm_g=v_norm_g, v_w_in=v_w_in, v_q_norm_g=v_q_norm_g, v_k_norm_g=v_k_norm_g, v_rpb=v_rpb, v_conv_w=v_conv_w, v_conv_b=v_conv_b, v_w_out=v_w_out)
    weights = {n: given[n] for n in TWIN_WEIGHTS}
    shared = {n: given[n] for n in SHARED_INPUTS}
    per_example = {n: given[n] for n in ['x', 'c', 'ctx']}
    grad_fn = _jax.value_and_grad(_loss, argnums=(0, 1))

    def one_microbatch(ex, loss_target):
        ex = dict(ex)
        diff = ex.pop(TWIN_DIFF_INPUT)
        return grad_fn(weights, diff, {**shared, **ex}, loss_target)

    if N_MICROBATCH == 1:
        loss, (grad_w, grad_x) = one_microbatch(per_example, given["loss_target"])
    else:
        def body(carry, xs):
            loss_sum, grad_sum = carry
            l_k, (gw_k, gx_k) = one_microbatch(xs[0], xs[1])
            with _jax.named_scope("update"):
                return (loss_sum + l_k, _jax.tree.map(_jnp.add, grad_sum, gw_k)), gx_k

        init = (_jnp.zeros((), _jnp.float32), _jax.tree.map(_jnp.zeros_like, weights))
        (loss, grad_w), grad_x = _jax.lax.scan(body, init, (per_example, given["loss_target"]))
    with _jax.named_scope("update"):
        delta_w, new_m, new_v = {}, {}, {}
        for n in TWIN_WEIGHTS:
            delta_w[n], new_m[n], new_v[n] = _adamw(weights[n], grad_w[n], given["m_" + n], given["v_" + n])
    return (loss, grad_x, *[grad_w[n] for n in TWIN_WEIGHTS], *[delta_w[n] for n in TWIN_WEIGHTS],
            *[new_m[n] for n in TWIN_WEIGHTS], *[new_v[n] for n in TWIN_WEIGHTS])
```

```python
import functools

import jax
import jax.numpy as jnp
from jax import lax
from jax.experimental import pallas as pl
from jax.experimental.pallas import tpu as pltpu

f32, bf16, i32 = jnp.float32, jnp.bfloat16, jnp.int32
MESH = pl.DeviceIdType.MESH
HIGHEST = lax.Precision.HIGHEST

D = 1024
S = 2048
L = 256
GW = 64
ROWS = S // GW
H = 8
DH = 64
DA = H * DH
DC = 512
WIN_H, WIN_W = 8, 16
N_DR, N_DC = 2 * WIN_H - 1, 2 * WIN_W - 1
RMS_EPS = 1e-6
ROPE_THETA = 10000.0
QK_SCALE = DH ** -0.5
NEG = -1e30

QB = 128
NQB = S // QB
KR = 9
KB = KR * GW
TILE_GEOM = ((0, 0), (2, 0), (4, 0), (28, 23), (30, 23))
NT = len(TILE_GEOM)

ADAM_LR, ADAM_B1, ADAM_B2, ADAM_EPS, ADAM_WD, ADAM_STEP = 0.001, 0.9, 0.999, 1e-08, 0.01, 10

VMEM_SPEC = pl.BlockSpec(memory_space=pltpu.VMEM)
ANY_SPEC = pl.BlockSpec(memory_space=pl.ANY)
SMEM_SPEC = pl.BlockSpec(memory_space=pltpu.SMEM)
SDS = jax.ShapeDtypeStruct


def _cp(vmem_mb=None, **kw):
    if vmem_mb is not None:
        kw["vmem_limit_bytes"] = vmem_mb << 20
    return pltpu.CompilerParams(**kw)


def _silu(z):
    return z * jax.nn.sigmoid(z)


def _dsilu(z):
    sg = jax.nn.sigmoid(z)
    return sg * (1.0 + z * (1.0 - sg))


def _row_start(i):
    return min(max(i - WIN_H // 2, 0), ROWS - WIN_H)


def _my_pos():
    return lax.axis_index("x"), lax.axis_index("y"), lax.axis_index("c")


def _flip(v, bit):
    return 1 - v if bit else v


def _all_gather8(xin, name):
    R, N = xin.shape

    def body(x_ref, o_ref, ssem, rsem, lsem):
        x, y, c = _my_pos()
        me = 4 * x + 2 * y + c
        own = pltpu.make_async_copy(x_ref, o_ref.at[me], lsem)
        own.start()
        sends = []
        for k in range(1, 8):
            tgt = (_flip(x, (k >> 2) & 1), _flip(y, (k >> 1) & 1), _flip(c, k & 1))
            cp = pltpu.make_async_remote_copy(src_ref=x_ref, dst_ref=o_ref.at[me], send_sem=ssem.at[k - 1],
                                              recv_sem=rsem.at[k - 1], device_id=tgt, device_id_type=MESH)
            cp.start()
            sends.append(cp)
        for k in range(1, 8):
            tgt = (_flip(x, (k >> 2) & 1), _flip(y, (k >> 1) & 1), _flip(c, k & 1))
            peer = 4 * tgt[0] + 2 * tgt[1] + tgt[2]
            pltpu.make_async_remote_copy(src_ref=x_ref, dst_ref=o_ref.at[peer], send_sem=ssem.at[k - 1],
                                         recv_sem=rsem.at[k - 1], device_id=tgt, device_id_type=MESH).wait_recv()
        for cp in sends:
            cp.wait_send()
        own.wait()

    return pl.pallas_call(
        body, name=name, out_shape=SDS((8, R, N), xin.dtype), in_specs=[VMEM_SPEC], out_specs=VMEM_SPEC,
        scratch_shapes=[pltpu.SemaphoreType.DMA((7,)), pltpu.SemaphoreType.DMA((7,)), pltpu.SemaphoreType.DMA],
    )(xin)


def _chip_gather(smalls, bigs, name):
    ns, nb = len(smalls), len(bigs)
    n_sem = 3 * ns + 6 * nb

    def body(*refs):
        s_in, b_in = refs[:ns], refs[ns:ns + nb]
        s_out, b_out = refs[ns + nb:2 * ns + nb], refs[2 * ns + nb:2 * (ns + nb)]
        ssem, rsem, lsem = refs[2 * (ns + nb):]
        x, y, c = _my_pos()
        j = 2 * x + y
        sib = (x, y, 1 - c)
        chips = []
        for k in range(1, 4):
            px, py = _flip(x, (k >> 1) & 1), _flip(y, k & 1)
            chips.append(((px, py, c), 2 * px + py))
        local = []
        for a in range(ns):
            local.append(pltpu.make_async_copy(s_in[a], s_out[a].at[j], lsem.at[a]))
        for a in range(nb):
            local.append(pltpu.make_async_copy(b_in[a], b_out[a].at[j], lsem.at[ns + a]))
        for cp in local:
            cp.start()
        sends = []
        sem = 0
        for a in range(ns):
            for k in range(3):
                cp = pltpu.make_async_remote_copy(src_ref=s_in[a], dst_ref=s_out[a].at[j], send_sem=ssem.at[sem + k],
                                                  recv_sem=rsem.at[sem + k], device_id=chips[k][0], device_id_type=MESH)
                cp.start()
                sends.append(cp)
            sem += 3
        big_first = sem
        for a in range(nb):
            half = b_in[a].shape[0] // 2
            mine = pl.ds(pl.multiple_of(c * half, 8), half)
            for k in range(3):
                cp = pltpu.make_async_remote_copy(src_ref=b_in[a].at[mine], dst_ref=b_out[a].at[j, mine],
                                                  send_sem=ssem.at[sem + k], recv_sem=rsem.at[sem + k],
                                                  device_id=chips[k][0], device_id_type=MESH)
                cp.start()
                sends.append(cp)
            sem += 6
        sem = big_first
        for a in range(nb):
            half = b_in[a].shape[0] // 2
            mine = pl.ds(pl.multiple_of(c * half, 8), half)
            for k in range(3):
                pj = chips[k][1]
                pltpu.make_async_remote_copy(src_ref=b_in[a].at[mine], dst_ref=b_out[a].at[pj, mine],
                                             send_sem=ssem.at[sem + k], recv_sem=rsem.at[sem + k],
                                             device_id=chips[k][0], device_id_type=MESH).wait_recv()
                cp = pltpu.make_async_remote_copy(src_ref=b_out[a].at[pj, mine], dst_ref=b_out[a].at[pj, mine],
                                                  send_sem=ssem.at[sem + 3 + k], recv_sem=rsem.at[sem + 3 + k],
                                                  device_id=sib, device_id_type=MESH)
                cp.start()
                sends.append(cp)
            sem += 6
        sem = 0
        for a in range(ns):
            for k in range(3):
                pj = chips[k][1]
                pltpu.make_async_remote_copy(src_ref=s_in[a], dst_ref=s_out[a].at[pj], send_sem=ssem.at[sem + k],
                                             recv_sem=rsem.at[sem + k], device_id=chips[k][0], device_id_type=MESH).wait_recv()
            sem += 3
        for a in range(nb):
            half = b_in[a].shape[0] // 2
            other = pl.ds(pl.multiple_of((1 - c) * half, 8), half)
            for k in range(3):
                pj = chips[k][1]
                pltpu.make_async_remote_copy(src_ref=b_out[a].at[pj, other], dst_ref=b_out[a].at[pj, other],
                                             send_sem=ssem.at[sem + 3 + k], recv_sem=rsem.at[sem + 3 + k],
                                             device_id=sib, device_id_type=MESH).wait_recv()
            sem += 6
        for cp in sends:
            cp.wait_send()
        for cp in local:
            cp.wait()

    out_shape = [SDS((4,) + a.shape, a.dtype) for a in smalls] + [SDS((4,) + a.shape, a.dtype) for a in bigs]
    return pl.pallas_call(
        body, name=name, out_shape=out_shape,
        in_specs=[VMEM_SPEC] * ns + [ANY_SPEC] * nb, out_specs=[VMEM_SPEC] * ns + [ANY_SPEC] * nb,
        scratch_shapes=[pltpu.SemaphoreType.DMA((n_sem,)), pltpu.SemaphoreType.DMA((n_sem,)),
                        pltpu.SemaphoreType.DMA((ns + nb,))],
    )(*smalls, *bigs)


def _sibling_exchange(mats, n_blocks, name):
    n = len(mats)
    n_cp = sum(n_blocks)

    def body(*refs):
        ins, outs = refs[:n], refs[n:2 * n]
        ssem, rsem = refs[2 * n:]
        x, y, c = _my_pos()
        sib = (x, y, 1 - c)
        sends = []
        sem = 0
        for a in range(n):
            rb = mats[a].shape[0] // n_blocks[a]
            half = rb // 2
            for jb in range(n_blocks[a]):
                theirs = pl.ds(pl.multiple_of(jb * rb + (1 - c) * half, 8), half)
                cp = pltpu.make_async_remote_copy(src_ref=ins[a].at[theirs], dst_ref=outs[a].at[jb],
                                                  send_sem=ssem.at[sem], recv_sem=rsem.at[sem],
                                                  device_id=sib, device_id_type=MESH)
                cp.start()
                sends.append(cp)
                sem += 1
        for cp in sends:
            cp.wait_recv()
        for cp in sends:
            cp.wait_send()

    out_shape = [SDS((n_blocks[a], mats[a].shape[0] // n_blocks[a] // 2, mats[a].shape[1]), mats[a].dtype)
                 for a in range(n)]
    return pl.pallas_call(
        body, name=name, out_shape=out_shape, in_specs=[ANY_SPEC] * n, out_specs=[ANY_SPEC] * n,
        scratch_shapes=[pltpu.SemaphoreType.DMA((n_cp,)), pltpu.SemaphoreType.DMA((n_cp,))],
    )(*mats)


def _chip_scatter(parts, name):
    n = len(parts)

    def body(*refs):
        ins, outs = refs[:n], refs[n:2 * n]
        ssem, rsem = refs[2 * n:]
        x, y, c = _my_pos()
        sends = []
        sem = 0
        for a in range(n):
            for k in range(1, 4):
                px, py = _flip(x, (k >> 1) & 1), _flip(y, k & 1)
                cp = pltpu.make_async_remote_copy(src_ref=ins[a].at[2 * px + py], dst_ref=outs[a].at[k - 1],
                                                  send_sem=ssem.at[sem], recv_sem=rsem.at[sem],
                                                  device_id=(px, py, c), device_id_type=MESH)
                cp.start()
                sends.append(cp)
                sem += 1
        for cp in sends:
            cp.wait_recv()
        for cp in sends:
            cp.wait_send()

    out_shape = [SDS((3,) + p.shape[1:], p.dtype) for p in parts]
    return pl.pallas_call(
        body, name=name, out_shape=out_shape, in_specs=[ANY_SPEC] * n, out_specs=[ANY_SPEC] * n,
        scratch_shapes=[pltpu.SemaphoreType.DMA((3 * n,)), pltpu.SemaphoreType.DMA((3 * n,))],
    )(*parts)


def _sibling_join(halves, name):
    n = len(halves)

    def body(*refs):
        ins, outs = refs[:n], refs[n:2 * n]
        ssem, rsem, lsem = refs[2 * n:]
        x, y, c = _my_pos()
        sib = (x, y, 1 - c)
        cps, loc = [], []
        for a in range(n):
            r = halves[a].shape[0]
            mine = pl.ds(pl.multiple_of(c * r, 8), r)
            lc = pltpu.make_async_copy(ins[a], outs[a].at[mine], lsem.at[a])
            lc.start()
            loc.append(lc)
            cp = pltpu.make_async_remote_copy(src_ref=ins[a], dst_ref=outs[a].at[mine], send_sem=ssem.at[a],
                                              recv_sem=rsem.at[a], device_id=sib, device_id_type=MESH)
            cp.start()
            cps.append(cp)
        for a in range(n):
            r = halves[a].shape[0]
            other = pl.ds(pl.multiple_of((1 - c) * r, 8), r)
            pltpu.make_async_remote_copy(src_ref=ins[a], dst_ref=outs[a].at[other], send_sem=ssem.at[a],
                                         recv_sem=rsem.at[a], device_id=sib, device_id_type=MESH).wait_recv()
        for cp in cps:
            cp.wait_send()
        for lc in loc:
            lc.wait()

    out_shape = [SDS((2 * h.shape[0], h.shape[1]), h.dtype) for h in halves]
    return pl.pallas_call(
        body, name=name, out_shape=out_shape, in_specs=[ANY_SPEC] * n, out_specs=[ANY_SPEC] * n,
        scratch_shapes=[pltpu.SemaphoreType.DMA((n,)), pltpu.SemaphoreType.DMA((n,)), pltpu.SemaphoreType.DMA((n,))],
    )(*halves)


def _adaln_shard(cc, w_ada_shard):
    def body(c_ref, w_ref, m_ref, sc_ref):
        sc = _silu(c_ref[...])
        sc_ref[...] = sc
        m_ref[...] = jnp.dot(sc, w_ref[...], precision=HIGHEST, preferred_element_type=f32)

    return pl.pallas_call(
        body, name="adaln_shard", out_shape=(SDS((16, w_ada_shard.shape[1]), f32), SDS((16, D), f32)),
        in_specs=[VMEM_SPEC, VMEM_SPEC], out_specs=(VMEM_SPEC, VMEM_SPEC), compiler_params=_cp(32),
    )(cc, w_ada_shard)


def _prenorm(xx, norm_g, mrow, b_ada, tm, name):
    n = xx.shape[0]

    def body(x_ref, g_ref, m_ref, b_ref, h_ref):
        x = x_ref[...]
        shift = m_ref[:, 0:D] + b_ref[:, 0:D]
        scale = m_ref[:, D:2 * D] + b_ref[:, D:2 * D]
        r = lax.rsqrt(jnp.mean(x * x, axis=-1, keepdims=True) + RMS_EPS)
        y = (x * r) * g_ref[...]
        h_ref[...] = (y * (1.0 + scale) + shift).astype(bf16)

    row = lambda i: (i, 0)
    fixed = lambda i: (0, 0)
    return pl.pallas_call(
        body, name=name, out_shape=SDS((n, D), bf16), grid=(n // tm,),
        in_specs=[pl.BlockSpec((tm, D), row), pl.BlockSpec((1, D), fixed), pl.BlockSpec((1, 3 * D), fixed),
                  pl.BlockSpec((1, 3 * D), fixed)],
        out_specs=pl.BlockSpec((tm, D), row),
    )(xx, norm_g, mrow, b_ada)


def _in_proj(h, w4):
    tm = 512

    def body(h_ref, w_ref, p_ref):
        p_ref[...] = jnp.dot(h_ref[...], w_ref[...], preferred_element_type=f32)

    return pl.pallas_call(
        body, name="in_proj", out_shape=SDS((S, 4 * D), f32), grid=(4, S // tm),
        in_specs=[pl.BlockSpec((tm, D), lambda j, k: (k, 0)), pl.BlockSpec((None, D, D), lambda j, k: (j, 0, 0))],
        out_specs=pl.BlockSpec((tm, D), lambda j, k: (k, j)),
    )(h, w4)


def _ctx_proj(hc, w4):
    def body(h_ref, w0_ref, w1_ref, p_ref):
        hv = h_ref[...]
        p_ref[:, 0:DA] = jnp.dot(hv, w0_ref[:, DA:2 * DA], preferred_element_type=f32)
        p_ref[:, DA:2 * DA] = jnp.dot(hv, w1_ref[:, 0:DA], preferred_element_type=f32)

    return pl.pallas_call(
        body, name="ctx_proj", out_shape=SDS((L, 2 * DA), f32), grid=(1,),
        in_specs=[pl.BlockSpec((L, D), lambda i: (0, 0)), pl.BlockSpec((None, D, D), lambda i: (0, 0, 0)),
                  pl.BlockSpec((None, D, D), lambda i: (1, 0, 0))],
        out_specs=pl.BlockSpec((L, 2 * DA), lambda i: (0, 0)),
    )(hc, w4, w4)


def _head_ones():
    r = lax.broadcasted_iota(i32, (DA, DA), 0) // DH
    c = lax.broadcasted_iota(i32, (DA, DA), 1) // DH
    return (r == c).astype(f32)


def _head_sum(v, ones_bd):
    return jnp.dot(v, ones_bd, precision=HIGHEST, preferred_element_type=f32)


def _swap16(v):
    lane = lax.broadcasted_iota(i32, v.shape, 1)
    return jnp.where((lane & 31) < 16, pltpu.roll(v, DA - 16, 1), pltpu.roll(v, 16, 1))


def _qk_prep(p, gq, gk, cos, sins):
    tm = 256

    def body(qk_ref, v_ref, gq_ref, gk_ref, cos_ref, sin_ref, qr_ref, qp_ref, kr_ref, vh_ref):
        ones_bd = _head_ones()
        cs, sn = cos_ref[...], sin_ref[...]
        q = qk_ref[:, 0:DA]
        k = qk_ref[:, DA:2 * DA]
        yq = (q * lax.rsqrt(_head_sum(q * q, ones_bd) * (1.0 / DH) + RMS_EPS)) * gq_ref[...]
        yk = (k * lax.rsqrt(_head_sum(k * k, ones_bd) * (1.0 / DH) + RMS_EPS)) * gk_ref[...]
        qr = (yq * cs + _swap16(yq) * sn) * QK_SCALE
        qp = yq * QK_SCALE
        kr = yk * cs + _swap16(yk) * sn
        vv = v_ref[...]
        for hh in range(H):
            sl = slice(hh * DH, (hh + 1) * DH)
            qr_ref[hh] = qr[:, sl].astype(bf16)
            qp_ref[hh] = qp[:, sl].astype(bf16)
            kr_ref[hh] = kr[:, sl].astype(bf16)
            vh_ref[hh] = vv[:, sl].astype(bf16)

    hm = SDS((H, S, DH), bf16)
    hspec = pl.BlockSpec((H, tm, DH), lambda i: (0, i, 0))
    fixed = lambda i: (0, 0)
    return pl.pallas_call(
        body, name="qk_prep", out_shape=(hm, hm, hm, hm), grid=(S // tm,),
        in_specs=[pl.BlockSpec((tm, 2 * DA), lambda i: (i, 0)), pl.BlockSpec((tm, DA), lambda i: (i, 2)),
                  pl.BlockSpec((1, DA), fixed), pl.BlockSpec((1, DA), fixed),
                  pl.BlockSpec((tm, DA), lambda i: (i, 0)), pl.BlockSpec((tm, DA), lambda i: (i, 0))],
        out_specs=(hspec, hspec, hspec, hspec),
    )(p, p, gq, gk, cos, sins)


def _ctx_prep(pc, gk):
    def body(p_ref, gk_ref, kc_ref, vc_ref):
        ones_bd = _head_ones()
        k = p_ref[:, 0:DA]
        yk = (k * lax.rsqrt(_head_sum(k * k, ones_bd) * (1.0 / DH) + RMS_EPS)) * gk_ref[...]
        vv = p_ref[:, DA:2 * DA]
        for hh in range(H):
            sl = slice(hh * DH, (hh + 1) * DH)
            kc_ref[hh] = yk[:, sl].astype(bf16)
            vc_ref[hh] = vv[:, sl].astype(bf16)

    hm = SDS((H, L, DH), bf16)
    return pl.pallas_call(
        body, name="ctx_prep", out_shape=(hm, hm), in_specs=[VMEM_SPEC, VMEM_SPEC], out_specs=(VMEM_SPEC, VMEM_SPEC),
    )(pc, gk)


def _tile_pieces():
    out = []
    for (i0, u0) in TILE_GEOM:
        rows = []
        for j in range(2):
            i = i0 + j
            rs = _row_start(i)
            rows.append([(u0 + u - i + WIN_H - 1) if rs <= u0 + u < rs + WIN_H else None for u in range(KR)])
        out.append(rows)
    return out


def _bias_prep(rpb_pad):
    pieces = _tile_pieces()

    def body(r_ref, o_ref):
        rp = r_ref[...]
        xs = jnp.broadcast_to(rp[:, None, :], (N_DR, GW, 128)).reshape(N_DR * GW, 128)
        row = lax.broadcasted_iota(i32, xs.shape, 0)
        lane = lax.broadcasted_iota(i32, xs.shape, 1)
        for b in range(6):
            xs = jnp.where(((row >> b) & 1) == 1, pltpu.roll(xs, 1 << b, 1), xs)
        xs = pltpu.roll(xs, 128 - (WIN_W - 1), 1)
        q = row & (GW - 1)
        c0 = jnp.clip(q - WIN_W // 2, 0, GW - WIN_W)
        xs = jnp.where((lane >= c0) & (lane < c0 + WIN_W), xs, NEG)
        neg = jnp.full((GW, GW), NEG, f32)
        for t in range(NT):
            for j in range(2):
                for u in range(KR):
                    dr = pieces[t][j][u]
                    piece = neg if dr is None else xs[dr * GW:(dr + 1) * GW, 0:GW]
                    o_ref[t, j * GW:(j + 1) * GW, u * GW:(u + 1) * GW] = piece

    return pl.pallas_call(
        body, name="bias_prep", out_shape=SDS((H, NT, QB, KB), f32), grid=(H,),
        in_specs=[pl.BlockSpec((None, N_DR, 128), lambda h: (h, 0, 0))],
        out_specs=pl.BlockSpec((None, NT, QB, KB), lambda h: (h, 0, 0, 0)),
    )(rpb_pad)


def _block_geom(b):
    qs = pl.multiple_of(b * QB, QB)
    ks = pl.multiple_of(jnp.clip(2 * b - 4, 0, ROWS - KR) * GW, GW)
    t = jnp.where(b < 2, b, jnp.where(b > NQB - 3, b - (NQB - NT), 2))
    return qs, ks, t


def _tt(a, b):
    return lax.dot_general(a, b, (((1,), (1,)), ((), ())), preferred_element_type=f32)


def _tn(a, b):
    return lax.dot_general(a, b, (((0,), (0,)), ((), ())), preferred_element_type=f32)


def _softmax_block(qr, qp, k, kc, bias):
    s_lat = _tt(qr, k) + bias
    s_ctx = _tt(qp, kc)
    m = jnp.maximum(jnp.max(s_lat, axis=-1, keepdims=True), jnp.max(s_ctx, axis=-1, keepdims=True))
    e_lat = jnp.exp(s_lat - m)
    e_ctx = jnp.exp(s_ctx - m)
    inv = 1.0 / (jnp.sum(e_lat, axis=-1, keepdims=True) + jnp.sum(e_ctx, axis=-1, keepdims=True))
    return e_lat * inv, e_ctx * inv


def _attn_fwd(qr, qp, kr, vh, kc, vc, btb):
    def body(qr_ref, qp_ref, kr_ref, v_ref, kc_ref, vc_ref, bt_ref, o_ref):
        kcv, vcv = kc_ref[...], vc_ref[...]

        def blk(b, carry):
            qs, ks, t = _block_geom(b)
            p_lat, p_ctx = _softmax_block(qr_ref[pl.ds(qs, QB), :], qp_ref[pl.ds(qs, QB), :],
                                          kr_ref[pl.ds(ks, KB), :], kcv, bt_ref[t])
            o = jnp.dot(p_lat.astype(bf16), v_ref[pl.ds(ks, KB), :], preferred_element_type=f32)
            o_ref[pl.ds(qs, QB), :] = o + jnp.dot(p_ctx.astype(bf16), vcv, preferred_element_type=f32)
            return carry

        lax.fori_loop(0, NQB, blk, 0)

    sq = pl.BlockSpec((None, S, DH), lambda h: (h, 0, 0))
    sc = pl.BlockSpec((None, L, DH), lambda h: (h, 0, 0))
    return pl.pallas_call(
        body, name="attn_fwd", out_shape=SDS((H, S, DH), f32), grid=(H,),
        in_specs=[sq, sq, sq, sq, sc, sc, pl.BlockSpec((None, NT, QB, KB), lambda h: (h, 0, 0, 0))],
        out_specs=sq, compiler_params=_cp(40),
    )(qr, qp, kr, vh, kc, vc, btb)


def _attn_gate(o, p):
    tm = 256

    def body(o_ref, za_ref, a_ref):
        sz = _silu(za_ref[...])
        for hh in range(H):
            sl = slice(hh * DH, (hh + 1) * DH)
            a_ref[:, sl] = (o_ref[hh] * sz[:, sl]).astype(bf16)

    return pl.pallas_call(
        body, name="attn_gate", out_shape=SDS((S, DA), bf16), grid=(S // tm,),
        in_specs=[pl.BlockSpec((H, tm, DH), lambda i: (0, i, 0)), pl.BlockSpec((tm, DA), lambda i: (i, 3))],
        out_specs=pl.BlockSpec((tm, DA), lambda i: (i, 0)),
    )(o, p)


def _shift_rows(v, down):
    n = v.shape[0]
    row = lax.broadcasted_iota(i32, v.shape, 0)
    if down:
        return jnp.where(row == 0, 0.0, pltpu.roll(v, 1, 0))
    return jnp.where(row == n - 1, 0.0, pltpu.roll(v, n - 1, 0))


def _conv_specs():
    col = lambda off: pl.BlockSpec((S, 128), lambda i, off=off: (0, off + i))
    return [col(16), col(20), col(24), col(28), pl.BlockSpec((3, 128), lambda i: (0, i)),
            pl.BlockSpec((1, 128), lambda i: (0, i))]


def _conv_fwd(p, conv_w, conv_b):
    def body(u_ref, bg_ref, cg_ref, zc_ref, w_ref, b_ref, o_ref):
        cu = cg_ref[...] * u_ref[...]
        cv = b_ref[...] + _shift_rows(cu, True) * w_ref[0:1, :]
        cv = cv + cu * w_ref[1:2, :]
        cv = cv + _shift_rows(cu, False) * w_ref[2:3, :]
        o_ref[...] = ((bg_ref[...] * cv) * _silu(zc_ref[...])).astype(bf16)

    return pl.pallas_call(
        body, name="conv_fwd", out_shape=SDS((S, DC), bf16), grid=(DC // 128,),
        in_specs=_conv_specs(), out_specs=pl.BlockSpec((S, 128), lambda i: (0, i)), compiler_params=_cp(40),
    )(p, p, p, p, conv_w, conv_b)


def _out_proj_loss(attn_g, conv_g, w_out, xx, tgt, mrow, b_ada):
    tm = 256

    def body(a_ref, c_ref, w_ref, x_ref, t_ref, m_ref, b_ref, dy_ref, dmix_ref, gwo_ref, dgate_ref, loss_ref):
        k = pl.program_id(0)

        @pl.when(k == 0)
        def _():
            gwo_ref[...] = jnp.zeros_like(gwo_ref)
            dgate_ref[...] = jnp.zeros_like(dgate_ref)
            loss_ref[0, 0] = 0.0

        gate = m_ref[:, 2 * D:3 * D] + b_ref[:, 2 * D:3 * D]
        av, cv = a_ref[...], c_ref[...]
        mo = jnp.dot(av, w_ref[0:DA, :], preferred_element_type=f32)
        mo = mo + jnp.dot(cv, w_ref[DA:DA + DC, :], preferred_element_type=f32)
        y = x_ref[...] + gate * mo
        diff = y - t_ref[...]
        loss_ref[0, 0] += jnp.sum(diff * diff)
        dy = diff * (1.0 / D)
        dy_ref[...] = dy
        dgate_ref[...] += jnp.sum(dy * mo, axis=0, keepdims=True)
        dmo = (dy * gate).astype(bf16)
        dmix_ref[...] = _tt(dmo, w_ref[...])
        gwo_ref[0:DA, :] += _tn(av, dmo)
        gwo_ref[DA:DA + DC, :] += _tn(cv, dmo)

    row = lambda i: (i, 0)
    fixed = lambda i: (0, 0)
    return pl.pallas_call(
        body, name="out_proj_loss",
        out_shape=(SDS((S, D), f32), SDS((S, D), f32), SDS((D, D), f32), SDS((1, D), f32), SDS((1, 1), f32)),
        grid=(S // tm,),
        in_specs=[pl.BlockSpec((tm, DA), row), pl.BlockSpec((tm, DC), row), pl.BlockSpec((D, D), fixed),
                  pl.BlockSpec((tm, D), row), pl.BlockSpec((tm, D), row), pl.BlockSpec((1, 3 * D), fixed),
                  pl.BlockSpec((1, 3 * D), fixed)],
        out_specs=(pl.BlockSpec((tm, D), row), pl.BlockSpec((tm, D), row), pl.BlockSpec((D, D), fixed),
                   pl.BlockSpec((1, D), fixed), SMEM_SPEC),
        compiler_params=_cp(48, dimension_semantics=("arbitrary",)),
    )(attn_g, conv_g, w_out, xx, tgt, mrow, b_ada)


def _conv_bwd(dmix, p, conv_w, conv_b):
    def body(d_ref, u_ref, bg_ref, cg_ref, zc_ref, w_ref, b_ref, du_ref, dbg_ref, dcg_ref, dzc_ref, gw_ref, gb_ref):
        dconv = d_ref[...]
        u, bg, cg, zc = u_ref[...], bg_ref[...], cg_ref[...], zc_ref[...]
        w0, w1, w2 = w_ref[0:1, :], w_ref[1:2, :], w_ref[2:3, :]
        cu = cg * u
        cu_m, cu_p = _shift_rows(cu, True), _shift_rows(cu, False)
        cv = b_ref[...] + cu_m * w0
        cv = cv + cu * w1
        cv = cv + cu_p * w2
        sz = _silu(zc)
        dbg_ref[...] = ((dconv * sz) * cv).astype(bf16)
        dzc_ref[...] = ((dconv * (bg * cv)) * _dsilu(zc)).astype(bf16)
        dcv = (dconv * sz) * bg
        gb_ref[...] = jnp.sum(dcv, axis=0, keepdims=True)
        gw_ref[0:1, :] = jnp.sum(dcv * cu_m, axis=0, keepdims=True)
        gw_ref[1:2, :] = jnp.sum(dcv * cu, axis=0, keepdims=True)
        gw_ref[2:3, :] = jnp.sum(dcv * cu_p, axis=0, keepdims=True)
        gw_ref[3:8, :] = jnp.zeros((5, 128), f32)
        dcu = _shift_rows(dcv, False) * w0 + dcv * w1 + _shift_rows(dcv, True) * w2
        dcg_ref[...] = (dcu * u).astype(bf16)
        du_ref[...] = (dcu * cg).astype(bf16)

    piece = SDS((S, DC), bf16)
    ospec = pl.BlockSpec((S, 128), lambda i: (0, i))
    return pl.pallas_call(
        body, name="conv_bwd", out_shape=(piece, piece, piece, piece, SDS((8, DC), f32), SDS((1, DC), f32)),
        grid=(DC // 128,),
        in_specs=[pl.BlockSpec((S, 128), lambda i: (0, 4 + i))] + _conv_specs(),
        out_specs=(ospec, ospec, ospec, ospec, pl.BlockSpec((8, 128), lambda i: (0, i)),
                   pl.BlockSpec((1, 128), lambda i: (0, i))),
        compiler_params=_cp(48),
    )(dmix, p, p, p, p, conv_w, conv_b)


def _gate_bwd(dmix, p, o):
    tm = 256

    def body(d_ref, za_ref, o_ref, do_ref, dza_ref):
        za = za_ref[...]
        dattn = d_ref[...]
        a = dattn * _silu(za)
        bb = dattn * _dsilu(za)
        for hh in range(H):
            sl = slice(hh * DH, (hh + 1) * DH)
            do_ref[hh] = a[:, sl].astype(bf16)
            dza_ref[:, sl] = (bb[:, sl] * o_ref[hh]).astype(bf16)

    hspec = pl.BlockSpec((H, tm, DH), lambda i: (0, i, 0))
    return pl.pallas_call(
        body, name="gate_bwd", out_shape=(SDS((H, S, DH), bf16), SDS((S, DA), bf16)), grid=(S // tm,),
        in_specs=[pl.BlockSpec((tm, DA), lambda i: (i, 0)), pl.BlockSpec((tm, DA), lambda i: (i, 3)), hspec],
        out_specs=(hspec, pl.BlockSpec((tm, DA), lambda i: (i, 0))),
    )(dmix, p, o)


def _attn_bwd(qr, qp, kr, vh, kc, vc, btb, do):
    def body(qr_ref, qp_ref, kr_ref, v_ref, kc_ref, vc_ref, bt_ref, do_ref,
             dqr_ref, dqp_ref, dkr_ref, dv_ref, dkc_ref, dvc_ref, dbt_ref):
        kcv, vcv = kc_ref[...], vc_ref[...]
        dkr_ref[...] = jnp.zeros_like(dkr_ref)
        dv_ref[...] = jnp.zeros_like(dv_ref)
        dkc_ref[...] = jnp.zeros_like(dkc_ref)
        dvc_ref[...] = jnp.zeros_like(dvc_ref)
        dbt_ref[...] = jnp.zeros_like(dbt_ref)

        def blk(b, carry):
            qs, ks, t = _block_geom(b)
            qrb, qpb = qr_ref[pl.ds(qs, QB), :], qp_ref[pl.ds(qs, QB), :]
            kb, vb = kr_ref[pl.ds(ks, KB), :], v_ref[pl.ds(ks, KB), :]
            dob = do_ref[pl.ds(qs, QB), :]
            p_lat, p_ctx = _softmax_block(qrb, qpb, kb, kcv, bt_ref[t])
            dp_lat = _tt(dob, vb)
            dp_ctx = _tt(dob, vcv)
            delta = jnp.sum(p_lat * dp_lat, axis=-1, keepdims=True) + jnp.sum(p_ctx * dp_ctx, axis=-1, keepdims=True)
            ds_lat = p_lat * (dp_lat - delta)
            ds_ctx = p_ctx * (dp_ctx - delta)
            dbt_ref[t] += ds_lat
            dsb_lat, dsb_ctx = ds_lat.astype(bf16), ds_ctx.astype(bf16)
            dqr_ref[pl.ds(qs, QB), :] = jnp.dot(dsb_lat, kb, preferred_element_type=f32)
            dqp_ref[pl.ds(qs, QB), :] = jnp.dot(dsb_ctx, kcv, preferred_element_type=f32)
            dkr_ref[pl.ds(ks, KB), :] += _tn(dsb_lat, qrb)
            dv_ref[pl.ds(ks, KB), :] += _tn(p_lat.astype(bf16), dob)
            dkc_ref[...] += _tn(dsb_ctx, qpb)
            dvc_ref[...] += _tn(p_ctx.astype(bf16), dob)
            return carry

        lax.fori_loop(0, NQB, blk, 0)

    sq = pl.BlockSpec((None, S, DH), lambda h: (h, 0, 0))
    sc = pl.BlockSpec((None, L, DH), lambda h: (h, 0, 0))
    sb = pl.BlockSpec((None, NT, QB, KB), lambda h: (h, 0, 0, 0))
    big, ctxs = SDS((H, S, DH), f32), SDS((H, L, DH), f32)
    return pl.pallas_call(
        body, name="attn_bwd", out_shape=(big, big, big, big, ctxs, ctxs, SDS((H, NT, QB, KB), f32)), grid=(H,),
        in_specs=[sq, sq, sq, sq, sc, sc, sb, sq], out_specs=(sq, sq, sq, sq, sc, sc, sb), compiler_params=_cp(48),
    )(qr, qp, kr, vh, kc, vc, btb, do)


def _bias_bwd(dbtb):
    pieces = _tile_pieces()

    def body(d_ref, o_ref, scr):
        scr[...] = jnp.zeros_like(scr)
        acc = [None] * N_DR
        for t in range(NT):
            for j in range(2):
                for u in range(KR):
                    dr = pieces[t][j][u]
                    if dr is None:
                        continue
                    piece = d_ref[t, j * GW:(j + 1) * GW, u * GW:(u + 1) * GW]
                    acc[dr] = piece if acc[dr] is None else acc[dr] + piece
        for dr in range(N_DR):
            scr[dr * GW:(dr + 1) * GW, 0:GW] = acc[dr]
        xs = pltpu.roll(scr[...], WIN_W - 1, 1)
        row = lax.broadcasted_iota(i32, xs.shape, 0)
        for b in range(6):
            xs = jnp.where(((row >> b) & 1) == 1, pltpu.roll(xs, 128 - (1 << b), 1), xs)
        o_ref[...] = jnp.sum(xs.reshape(N_DR, GW, 128), axis=1)

    return pl.pallas_call(
        body, name="bias_bwd", out_shape=SDS((H, N_DR, 128), f32), grid=(H,),
        in_specs=[pl.BlockSpec((None, NT, QB, KB), lambda h: (h, 0, 0, 0))],
        out_specs=pl.BlockSpec((None, N_DR, 128), lambda h: (h, 0, 0)),
        scratch_shapes=[pltpu.VMEM((N_DR * GW, 128), f32)],
    )(dbtb)


def _merge_heads(ref):
    return jnp.concatenate([ref[hh] for hh in range(H)], axis=1)


def _head_norm_bwd(xraw, gain, dy, ones_bd):
    r = lax.rsqrt(_head_sum(xraw * xraw, ones_bd) * (1.0 / DH) + RMS_EPS)
    xh = xraw * r
    gdy = dy * gain
    dx = r * (gdy - xh * (_head_sum(xh * gdy, ones_bd) * (1.0 / DH)))
    return dx, jnp.sum(dy * xh, axis=0, keepdims=True)


def _qk_bwd(dqr, dqp, dkr, dvh, p, gq, gk, cos, sins):
    tm = 256

    def body(dqr_ref, dqp_ref, dkr_ref, dv_ref, qk_ref, gq_ref, gk_ref, cos_ref, sin_ref,
             dq_ref, dk_ref, dvo_ref, ggq_ref, ggk_ref):
        @pl.when(pl.program_id(0) == 0)
        def _():
            ggq_ref[...] = jnp.zeros_like(ggq_ref)
            ggk_ref[...] = jnp.zeros_like(ggk_ref)

        ones_bd = _head_ones()
        cs, sn = cos_ref[...], sin_ref[...]
        a = _merge_heads(dqr_ref)
        dyq = ((a * cs - _swap16(a) * sn) + _merge_heads(dqp_ref)) * QK_SCALE
        bk = _merge_heads(dkr_ref)
        dyk = bk * cs - _swap16(bk) * sn
        dq, gq_part = _head_norm_bwd(qk_ref[:, 0:DA], gq_ref[...], dyq, ones_bd)
        dk, gk_part = _head_norm_bwd(qk_ref[:, DA:2 * DA], gk_ref[...], dyk, ones_bd)
        dq_ref[...] = dq.astype(bf16)
        dk_ref[...] = dk.astype(bf16)
        dvo_ref[...] = _merge_heads(dv_ref).astype(bf16)
        ggq_ref[...] += gq_part
        ggk_ref[...] += gk_part

    hspec = pl.BlockSpec((H, tm, DH), lambda i: (0, i, 0))
    row = pl.BlockSpec((tm, DA), lambda i: (i, 0))
    fixed = pl.BlockSpec((1, DA), lambda i: (0, 0))
    piece = SDS((S, DA), bf16)
    return pl.pallas_call(
        body, name="qk_bwd", out_shape=(piece, piece, piece, SDS((1, DA), f32), SDS((1, DA), f32)), grid=(S // tm,),
        in_specs=[hspec, hspec, hspec, hspec, pl.BlockSpec((tm, 2 * DA), lambda i: (i, 0)), fixed, fixed, row, row],
        out_specs=(row, row, row, fixed, fixed), compiler_params=_cp(40, dimension_semantics=("arbitrary",)),
    )(dqr, dqp, dkr, dvh, p, gq, gk, cos, sins)


def _ctx_bwd(dkc, dvc, pc, gk):
    def body(dkc_ref, dvc_ref, p_ref, gk_ref, dk_ref, dv_ref, ggk_ref):
        ones_bd = _head_ones()
        dk, gk_part = _head_norm_bwd(p_ref[:, 0:DA], gk_ref[...], _merge_heads(dkc_ref), ones_bd)
        dk_ref[...] = dk.astype(bf16)
        dv_ref[...] = _merge_heads(dvc_ref).astype(bf16)
        ggk_ref[...] = gk_part

    piece = SDS((L, DA), bf16)
    return pl.pallas_call(
        body, name="ctx_bwd", out_shape=(piece, piece, SDS((1, DA), f32)), in_specs=[VMEM_SPEC] * 4,
        out_specs=(VMEM_SPEC,) * 3,
    )(dkc, dvc, pc, gk)


def _grad_w_in_block(h, pa, pb, name, hc=None, pc=None, ctx_cols=None):
    tm = 512

    def body(*refs):
        if hc is None:
            h_ref, a_ref, b_ref, g_ref = refs
        else:
            h_ref, a_ref, b_ref, hc_ref, pc_ref, g_ref = refs

        @pl.when(pl.program_id(0) == 0)
        def _():
            g_ref[...] = jnp.zeros_like(g_ref)
            if hc is not None:
                g_ref[:, ctx_cols[0]:ctx_cols[1]] = _tn(hc_ref[...], pc_ref[...])

        hv = h_ref[...]
        g_ref[:, 0:DA] += _tn(hv, a_ref[...])
        g_ref[:, DA:2 * DA] += _tn(hv, b_ref[...])

    row = lambda i: (i, 0)
    fixed = lambda i: (0, 0)
    in_specs = [pl.BlockSpec((tm, D), row), pl.BlockSpec((tm, DA), row), pl.BlockSpec((tm, DA), row)]
    args = [h, pa, pb]
    if hc is not None:
        in_specs += [pl.BlockSpec((L, D), fixed), pl.BlockSpec((L, DA), fixed)]
        args += [hc, pc]
    return pl.pallas_call(
        body, name=name, out_shape=SDS((D, D), f32), grid=(S // tm,), in_specs=in_specs,
        out_specs=pl.BlockSpec((D, D), fixed), compiler_params=_cp(40, dimension_semantics=("arbitrary",)),
    )(*args)


def _norm_mod_bwd(x, dh, g, scale):
    r = lax.rsqrt(jnp.mean(x * x, axis=-1, keepdims=True) + RMS_EPS)
    xh = x * r
    y = xh * g
    dshift = jnp.sum(dh, axis=0, keepdims=True)
    dscale = jnp.sum(dh * y, axis=0, keepdims=True)
    dyn = dh * (1.0 + scale)
    dg = jnp.sum(dyn * xh, axis=0, keepdims=True)
    gdy = dyn * g
    dx = r * (gdy - xh * jnp.mean(xh * gdy, axis=-1, keepdims=True))
    return dx, dshift, dscale, dg


def _dh_grad_x(pieces, w4, xx, dy, norm_g, mrow, b_ada):
    tm = 256

    def body(*refs):
        p_refs = refs[:8]
        w_ref, x_ref, dy_ref, g_ref, m_ref, b_ref, gx_ref, dsh_ref, dsc_ref, dg_ref = refs[8:]

        @pl.when(pl.program_id(0) == 0)
        def _():
            dsh_ref[...] = jnp.zeros_like(dsh_ref)
            dsc_ref[...] = jnp.zeros_like(dsc_ref)
            dg_ref[...] = jnp.zeros_like(dg_ref)

        dh = None
        for j in range(4):
            for half in range(2):
                term = _tt(p_refs[2 * j + half][...], w_ref[j, :, half * DA:(half + 1) * DA])
                dh = term if dh is None else dh + term
        scale = m_ref[:, D:2 * D] + b_ref[:, D:2 * D]
        dx, dshift, dscale, dg = _norm_mod_bwd(x_ref[...], dh, g_ref[...], scale)
        gx_ref[...] = dy_ref[...] + dx
        dsh_ref[...] += dshift
        dsc_ref[...] += dscale
        dg_ref[...] += dg

    row = lambda i: (i, 0)
    fixed = lambda i: (0, 0)
    vec = SDS((1, D), f32)
    return pl.pallas_call(
        body, name="dh_grad_x", out_shape=(SDS((S, D), f32), vec, vec, vec), grid=(S // tm,),
        in_specs=[pl.BlockSpec((tm, DA), row)] * 8 + [pl.BlockSpec((4, D, D), lambda i: (0, 0, 0)),
                                                      pl.BlockSpec((tm, D), row), pl.BlockSpec((tm, D), row),
                                                      pl.BlockSpec((1, D), fixed), pl.BlockSpec((1, 3 * D), fixed),
                                                      pl.BlockSpec((1, 3 * D), fixed)],
        out_specs=(pl.BlockSpec((tm, D), row), pl.BlockSpec((1, D), fixed), pl.BlockSpec((1, D), fixed),
                   pl.BlockSpec((1, D), fixed)),
        compiler_params=_cp(56, dimension_semantics=("arbitrary",)),
    )(*pieces, w4, xx, dy, norm_g, mrow, b_ada)


def _dhc_sums(dkc_raw, dvc_m, w4, ctx2, norm_g, mrow_c, b_ada):
    def body(dk_ref, dv_ref, w0_ref, w1_ref, x_ref, g_ref, m_ref, b_ref, dsh_ref, dsc_ref, dg_ref):
        dh = _tt(dk_ref[...], w0_ref[:, DA:2 * DA]) + _tt(dv_ref[...], w1_ref[:, 0:DA])
        scale = m_ref[:, D:2 * D] + b_ref[:, D:2 * D]
        _, dshift, dscale, dg = _norm_mod_bwd(x_ref[...], dh, g_ref[...], scale)
        dsh_ref[...] = dshift
        dsc_ref[...] = dscale
        dg_ref[...] = dg

    fixed = lambda i: (0, 0)
    vec = SDS((1, D), f32)
    vspec = pl.BlockSpec((1, D), fixed)
    return pl.pallas_call(
        body, name="dhc_sums", out_shape=(vec, vec, vec), grid=(1,),
        in_specs=[pl.BlockSpec((L, DA), fixed), pl.BlockSpec((L, DA), fixed),
                  pl.BlockSpec((None, D, D), lambda i: (0, 0, 0)), pl.BlockSpec((None, D, D), lambda i: (1, 0, 0)),
                  pl.BlockSpec((L, D), fixed), vspec, pl.BlockSpec((1, 3 * D), fixed), pl.BlockSpec((1, 3 * D), fixed)],
        out_specs=(vspec, vspec, vspec), compiler_params=_cp(32),
    )(dkc_raw, dvc_m, w4, w4, ctx2, norm_g, mrow_c, b_ada)


def _rope_tables():
    nf = DH // 4
    inv = ROPE_THETA ** (-jnp.arange(nf, dtype=f32) / nf)
    pos = jnp.arange(S, dtype=i32)
    ang_r = (pos // GW).astype(f32)[:, None] * inv
    ang_c = (pos % GW).astype(f32)[:, None] * inv
    cos64 = jnp.concatenate([jnp.cos(ang_r), jnp.cos(ang_r), jnp.cos(ang_c), jnp.cos(ang_c)], axis=1)
    sin64 = jnp.concatenate([-jnp.sin(ang_r), jnp.sin(ang_r), -jnp.sin(ang_c), jnp.sin(ang_c)], axis=1)
    return jnp.tile(cos64, (1, H)), jnp.tile(sin64, (1, H))


def _local_step(xx, ctx2, tgt, mrow, mrow_c, b_ada, norm_g, w4, w_out_full, q_norm_g, k_norm_g, rpb2, conv_w_full,
                conv_b):
    gq = jnp.tile(q_norm_g, (1, H))
    gk = jnp.tile(k_norm_g, (1, H))
    cos, sins = _rope_tables()
    rpb_pad = jnp.pad(rpb2, ((0, 0), (0, 0), (0, 128 - N_DC)))

    h = _prenorm(xx, norm_g, mrow, b_ada, 256, "prenorm_x")
    hc = _prenorm(ctx2, norm_g, mrow_c, b_ada, L, "prenorm_ctx")
    p = _in_proj(h, w4)
    pc = _ctx_proj(hc, w4)
    qr, qp, kr, vh = _qk_prep(p, gq, gk, cos, sins)
    kc, vc = _ctx_prep(pc, gk)
    btb = _bias_prep(rpb_pad)
    o = _attn_fwd(qr, qp, kr, vh, kc, vc, btb)
    attn_g = _attn_gate(o, p)
    conv_g = _conv_fwd(p, conv_w_full, conv_b)
    dy, dmix, g_w_out, dgate, loss_sum = _out_proj_loss(attn_g, conv_g, w_out_full, xx, tgt, mrow, b_ada)

    du, dbg, dcg, dzc, g_conv_w, g_conv_b = _conv_bwd(dmix, p, conv_w_full, conv_b)
    do, dza = _gate_bwd(dmix, p, o)
    dqr, dqp, dkr, dvh, dkc, dvc, dbtb = _attn_bwd(qr, qp, kr, vh, kc, vc, btb, do)
    g_rpb = _bias_bwd(dbtb)
    dq, dk, dv, g_gq, g_gk = _qk_bwd(dqr, dqp, dkr, dvh, p, gq, gk, cos, sins)
    dkc_raw, dvc_m, g_gk_c = _ctx_bwd(dkc, dvc, pc, gk)
    g_w_in = [
        _grad_w_in_block(h, dq, dk, "grad_w_in_0", hc, dkc_raw, (DA, 2 * DA)),
        _grad_w_in_block(h, dv, dza, "grad_w_in_1", hc, dvc_m, (0, DA)),
        _grad_w_in_block(h, du, dbg, "grad_w_in_2"),
        _grad_w_in_block(h, dcg, dzc, "grad_w_in_3"),
    ]
    grad_x, dshift, dscale, dng = _dh_grad_x([dq, dk, dv, dza, du, dbg, dcg, dzc], w4, xx, dy, norm_g, mrow, b_ada)
    dshift_c, dscale_c, dng_c = _dhc_sums(dkc_raw, dvc_m, w4, ctx2, norm_g, mrow_c, b_ada)
    return dict(loss_sum=loss_sum, grad_x=grad_x, g_w_in=g_w_in, g_w_out=g_w_out, g_conv_w=g_conv_w,
                g_conv_b=g_conv_b, g_rpb=g_rpb, g_gq=g_gq, g_gk=g_gk, g_gk_c=g_gk_c, dshift=dshift, dscale=dscale,
                dgate=dgate, dng=dng, dshift_c=dshift_c, dscale_c=dscale_c, dng_c=dng_c)


def _pair_sum_w_in(gs, rs, cvec):
    tr = 128

    def body(c_ref, g0, g1, g2, g3, r0, r1, r2, r3, t32_ref, tb_ref):
        for q, (g_ref, r_ref) in enumerate(((g0, r0), (g1, r1), (g2, r2), (g3, r3))):
            t = g_ref[...] + r_ref[...]
            t32_ref[q] = t
            tb_ref[q] = t.astype(bf16)

    half = D // 2
    gs_spec = pl.BlockSpec((tr, D), lambda i, c: (c[0] * (half // tr) + i, 0))
    rs_spec = pl.BlockSpec((None, tr, D), lambda i, c: (0, i, 0))
    ospec = pl.BlockSpec((4, tr, D), lambda i, c: (0, i, 0))
    grid_spec = pltpu.PrefetchScalarGridSpec(num_scalar_prefetch=1, grid=(half // tr,), in_specs=[gs_spec] * 4 + [rs_spec] * 4,
                                             out_specs=(ospec, ospec))
    return pl.pallas_call(body, name="pair_sum_w_in", out_shape=(SDS((4, half, D), f32), SDS((4, half, D), bf16)),
                          grid_spec=grid_spec, compiler_params=_cp(40))(cvec, *gs, *rs)


def _pair_sum_w_out(g, r, cvec):
    hr = D // 8

    def body(c_ref, g0, g1, g2, g3, r_ref, t32_ref, tb_ref):
        for q, g_ref in enumerate((g0, g1, g2, g3)):
            t = g_ref[...] + r_ref[q]
            t32_ref[q] = t
            tb_ref[q] = t.astype(bf16)

    gspecs = [pl.BlockSpec((hr, D), lambda i, c, q=q: (2 * q + c[0], 0)) for q in range(4)]
    full = pl.BlockSpec((4, hr, D), lambda i, c: (0, 0, 0))
    grid_spec = pltpu.PrefetchScalarGridSpec(num_scalar_prefetch=1, grid=(1,), in_specs=gspecs + [full],
                                             out_specs=(full, full))
    return pl.pallas_call(body, name="pair_sum_w_out", out_shape=(SDS((4, hr, D), f32), SDS((4, hr, D), bf16)),
                          grid_spec=grid_spec)(cvec, g, g, g, g, r)


def _chip_sum(t32, r2, jvec, name):
    rows = t32.shape[1]
    tr = min(rows, 128)

    def body(j_ref, t_ref, r_ref, u_ref):
        u_ref[...] = ((t_ref[...] + r_ref[0].astype(f32)) + r_ref[1].astype(f32)) + r_ref[2].astype(f32)

    grid_spec = pltpu.PrefetchScalarGridSpec(
        num_scalar_prefetch=1, grid=(rows // tr,),
        in_specs=[pl.BlockSpec((None, tr, D), lambda i, j: (j[0], i, 0)), pl.BlockSpec((3, tr, D), lambda i, j: (0, i, 0))],
        out_specs=pl.BlockSpec((tr, D), lambda i, j: (i, 0)))
    return pl.pallas_call(body, name=name, out_shape=SDS((rows, D), f32), grid_spec=grid_spec)(jvec, t32, r2)


_PK = {}
_off = 0
for _name, _rows in (("dm", 24), ("dmc", 24), ("dng", 8), ("dng_c", 8), ("gq", 8), ("gk", 8), ("gk_c", 8),
                     ("rpb", H * N_DR), ("conv_b", 8), ("conv_w", 16)):
    _PK[_name] = (_off, _off + _rows)
    _off += _rows
PK_ROWS = _off
RS_B_ADA, RS_NORM_G, RS_GQ, RS_GK, RS_RPB, RS_CONV_B, RS_CONV_W, RS_DMC, RS_ROWS = 0, 24, 32, 40, 48, 168, 176, 192, 216


def _small_reduce(gathered):
    def body(g_ref, o_ref):
        tot = g_ref[0]
        for b in range(1, 8):
            tot = tot + g_ref[b]

        def rows(name):
            a, z = _PK[name]
            return tot[a:z]

        o_ref[RS_B_ADA:RS_B_ADA + 24] = rows("dm") + rows("dmc")
        o_ref[RS_NORM_G:RS_NORM_G + 8] = rows("dng") + rows("dng_c")
        gq = jnp.broadcast_to(jnp.sum(rows("gq"), axis=0, keepdims=True), (8, 128))
        gk = jnp.broadcast_to(jnp.sum(rows("gk") + rows("gk_c"), axis=0, keepdims=True), (8, 128))
        o_ref[RS_GQ:RS_GQ + 8] = gq + pltpu.roll(gq, DH, 1)
        o_ref[RS_GK:RS_GK + 8] = gk + pltpu.roll(gk, DH, 1)
        o_ref[RS_RPB:RS_RPB + H * N_DR] = rows("rpb")
        o_ref[RS_CONV_B:RS_CONV_B + 8] = rows("conv_b")
        o_ref[RS_CONV_W:RS_CONV_W + 16] = rows("conv_w")
        o_ref[RS_DMC:RS_DMC + 24] = rows("dmc")

    return pl.pallas_call(body, name="small_reduce", out_shape=SDS((RS_ROWS, 128), f32), in_specs=[VMEM_SPEC],
                          out_specs=VMEM_SPEC)(gathered)


def _w_ada_grad(sc16, dm16s, w_ada_shard):
    def body(sc_ref, dm_ref, w_ref, g_ref, part_ref):
        dm = dm_ref[...]
        g_ref[...] = lax.dot_general(sc_ref[...], dm, (((0,), (0,)), ((), ())), precision=HIGHEST,
                                     preferred_element_type=f32)
        part_ref[...] = lax.dot_general(dm[8:16], w_ref[...], (((1,), (1,)), ((), ())), precision=HIGHEST,
                                        preferred_element_type=f32)

    ncol = w_ada_shard.shape[1]
    return pl.pallas_call(body, name="w_ada_grad", out_shape=(SDS((D, ncol), f32), SDS((8, D), f32)),
                          in_specs=[VMEM_SPEC] * 3, out_specs=(VMEM_SPEC, VMEM_SPEC), compiler_params=_cp(40),
                          )(sc16, dm16s, w_ada_shard)


def _c_ctx_grad(parts4, c_ctx_row):
    def body(p_ref, c_ref, o_ref):
        tot = ((p_ref[0] + p_ref[1]) + p_ref[2]) + p_ref[3]
        o_ref[...] = tot[0:1] * _dsilu(c_ref[...])

    return pl.pallas_call(body, name="c_ctx_grad", out_shape=SDS((1, D), f32), in_specs=[VMEM_SPEC, VMEM_SPEC],
                          out_specs=VMEM_SPEC)(parts4, c_ctx_row)


def _adamw(w, g, m, v, name):
    rows, cols = w.shape
    tr = 256 if rows % 256 == 0 else rows

    def body(w_ref, g_ref, m_ref, v_ref, d_ref, m2_ref, v2_ref):
        gv = g_ref[...]
        m2 = ADAM_B1 * m_ref[...] + (1.0 - ADAM_B1) * gv
        v2 = ADAM_B2 * v_ref[...] + (1.0 - ADAM_B2) * jnp.square(gv)
        m_hat = m2 / (1.0 - ADAM_B1 ** ADAM_STEP)
        v_hat = v2 / (1.0 - ADAM_B2 ** ADAM_STEP)
        d_ref[...] = -ADAM_LR * (m_hat / (jnp.sqrt(v_hat) + ADAM_EPS) + ADAM_WD * w_ref[...])
        m2_ref[...] = m2
        v2_ref[...] = v2

    spec = pl.BlockSpec((tr, cols), lambda i: (i, 0))
    shp = SDS((rows, cols), f32)
    return pl.pallas_call(body, name=name, out_shape=(shp, shp, shp), grid=(rows // tr,), in_specs=[spec] * 4,
                          out_specs=(spec, spec, spec))(w, g, m, v)


def _rows128(a):
    return a.reshape(-1, 128)


def _pad_lanes(a):
    a2 = a.reshape(-1, a.shape[-1])
    return jnp.pad(a2, ((0, 0), (0, 128 - a2.shape[1])))


def kernel(x, c, ctx, c_ctx, w_ada, b_ada, norm_g, w_in, q_norm_g, k_norm_g, rpb, conv_w, conv_b, w_out, loss_target, m_c_ctx, m_w_ada, m_b_ada, m_norm_g, m_w_in, m_q_norm_g, m_k_norm_g, m_rpb, m_conv_w, m_conv_b, m_w_out, v_c_ctx, v_w_ada, v_b_ada, v_norm_g, v_w_in, v_q_norm_g, v_k_norm_g, v_rpb, v_conv_w, v_conv_b, v_w_out):
    xi, yi, ci = lax.axis_index("x"), lax.axis_index("y"), lax.axis_index("c")
    dev = 4 * xi + 2 * yi + ci
    chip = 2 * xi + yi
    cvec = jnp.reshape(ci, (1,)).astype(i32)
    jvec = jnp.reshape(chip, (1,)).astype(i32)
    w_ada_s = w_ada[0]
    ncol = w_ada_s.shape[1]

    c8 = _all_gather8(c.reshape(8, 128), "gather_c").reshape(8, D)
    cc = jnp.concatenate([c8, c_ctx.reshape(1, D), jnp.zeros((7, D), f32)], axis=0)
    m_shard, sc16 = _adaln_shard(cc, w_ada_s)

    conv_w_pad = jnp.pad(conv_w[0], ((0, 5), (0, 0)))
    m4, cw4, w4, wo4 = _chip_gather([m_shard, conv_w_pad], [w_in[0].astype(bf16), w_out[0].astype(bf16)],
                                    "gather_weights")
    m_full = jnp.transpose(m4, (1, 0, 2)).reshape(16, 4 * ncol)
    mrow = lax.dynamic_slice(m_full, (dev, 0), (1, 3 * D))
    mrow_c = m_full[8:9]
    conv_w_full = jnp.transpose(cw4[:, 0:3, :], (1, 0, 2)).reshape(3, DC)
    w_out_full = wo4.reshape(D, D)

    r = _local_step(x[0], ctx[0], loss_target[0], mrow, mrow_c, b_ada, norm_g, w4, w_out_full, q_norm_g, k_norm_g,
                    rpb[0], conv_w_full, conv_b)
    loss = lax.psum(r["loss_sum"][0, 0] * (0.5 / D), ("x", "y", "c"))

    ex = _sibling_exchange(r["g_w_in"] + [r["g_w_out"]], [1, 1, 1, 1, 4], "grad_pair_exchange")
    t32, tb = _pair_sum_w_in(r["g_w_in"], ex[0:4], cvec)
    to32, tob = _pair_sum_w_out(r["g_w_out"], ex[4], cvec)
    r2, ro2 = _chip_scatter([tb, tob], "grad_chip_scatter")
    u_in = _chip_sum(t32, r2, jvec, "chip_sum_w_in")
    u_out = _chip_sum(to32, ro2, jvec, "chip_sum_w_out")
    g_w_in_s, g_w_out_s = _sibling_join([u_in, u_out], "grad_pair_join")

    dm = jnp.concatenate([r["dshift"], r["dscale"], r["dgate"]], axis=1)
    dmc = jnp.concatenate([r["dshift_c"], r["dscale_c"], jnp.zeros((1, D), f32)], axis=1)
    pack_parts = [_rows128(dm), _rows128(dmc), _rows128(r["dng"]), _rows128(r["dng_c"]), _rows128(r["g_gq"]),
                  _rows128(r["g_gk"]), _rows128(r["g_gk_c"]), r["g_rpb"].reshape(H * N_DR, 128),
                  _rows128(r["g_conv_b"]), _rows128(r["g_conv_w"][0:3])]
    pack = jnp.concatenate([jnp.pad(p, ((0, -p.shape[0] % 8), (0, 0))) for p in pack_parts], axis=0)
    gathered = _all_gather8(pack, "gather_small")
    red = _small_reduce(gathered)
    g_b_ada = red[RS_B_ADA:RS_B_ADA + 24].reshape(1, 3 * D)
    g_norm_g = red[RS_NORM_G:RS_NORM_G + 8].reshape(1, D)
    g_q = red[RS_GQ:RS_GQ + 1, 0:DH]
    g_k = red[RS_GK:RS_GK + 1, 0:DH]
    g_rpb = red[RS_RPB:RS_RPB + H * N_DR]
    g_conv_b = red[RS_CONV_B:RS_CONV_B + 4].reshape(1, DC)
    g_conv_w_full = red[RS_CONV_W:RS_CONV_W + 12].reshape(3, DC)
    assert pack.shape[0] == PK_ROWS
    g_conv_w_s = lax.dynamic_slice(g_conv_w_full, (0, chip * 128), (3, 128))
    dmc_tot = red[RS_DMC:RS_DMC + 24].reshape(1, 3 * D)

    a0, a1 = _PK["dm"]
    dm8 = gathered[:, a0:a1, :].reshape(8, 3 * D)
    dm16 = jnp.concatenate([dm8, dmc_tot, jnp.zeros((7, 3 * D), f32)], axis=0)
    dm16s = lax.dynamic_slice(dm16, (0, chip * ncol), (16, ncol))
    g_w_ada_s, cpart = _w_ada_grad(sc16, dm16s, w_ada_s)
    (cparts4,) = _chip_gather([cpart], [], "gather_c_ctx_parts")
    g_c_ctx = _c_ctx_grad(cparts4, c_ctx.reshape(1, D))

    d_w_in, nm_w_in, nv_w_in = _adamw(w_in[0], g_w_in_s, m_w_in[0], v_w_in[0], "adamw_w_in")
    d_w_ada, nm_w_ada, nv_w_ada = _adamw(w_ada_s, g_w_ada_s, m_w_ada[0], v_w_ada[0], "adamw_w_ada")
    d_w_out, nm_w_out, nv_w_out = _adamw(w_out[0], g_w_out_s, m_w_out[0], v_w_out[0], "adamw_w_out")

    def small_pack(a_c_ctx, a_b_ada, a_norm_g, a_q, a_k, a_rpb, a_conv_w, a_conv_b):
        parts = [_rows128(a_c_ctx), _rows128(a_b_ada), _rows128(a_norm_g), _pad_lanes(a_q), _pad_lanes(a_k),
                 a_rpb, _rows128(a_conv_w), _rows128(a_conv_b)]
        return jnp.concatenate([jnp.pad(p, ((0, -p.shape[0] % 8), (0, 0))) for p in parts], axis=0)

    w_pk = small_pack(c_ctx, b_ada, norm_g, q_norm_g, k_norm_g, _pad_lanes(rpb[0]), conv_w[0], conv_b)
    g_pk = small_pack(g_c_ctx, g_b_ada, g_norm_g, g_q, g_k, g_rpb, g_conv_w_s, g_conv_b)
    m_pk = small_pack(m_c_ctx, m_b_ada, m_norm_g, m_q_norm_g, m_k_norm_g, _pad_lanes(m_rpb[0]), m_conv_w[0], m_conv_b)
    v_pk = small_pack(v_c_ctx, v_b_ada, v_norm_g, v_q_norm_g, v_k_norm_g, _pad_lanes(v_rpb[0]), v_conv_w[0], v_conv_b)
    d_pk, nm_pk, nv_pk = _adamw(w_pk, g_pk, m_pk, v_pk, "adamw_small")

    def small_unpack(pk):
        o = 0
        out = []
        for rows, fn in ((8, lambda a: a.reshape(D)), (24, lambda a: a.reshape(1, 3 * D)), (8, lambda a: a.reshape(1, D)),
                         (1, lambda a: a[:, 0:DH]), (1, lambda a: a[:, 0:DH]),
                         (H * N_DR, lambda a: a[:, 0:N_DC].reshape(1, H, N_DR, N_DC)),
                         (3, lambda a: a.reshape(1, 3, 128)), (4, lambda a: a.reshape(1, DC))):
            out.append(fn(pk[o:o + rows]))
            o += rows + (-rows % 8)
        return out

    def ordered(small, big_w_ada, big_w_in, big_w_out):
        s_c_ctx, s_b_ada, s_norm_g, s_q, s_k, s_rpb, s_conv_w, s_conv_b = small
        return [s_c_ctx, big_w_ada[None], s_b_ada, s_norm_g, big_w_in[None], s_q, s_k, s_rpb, s_conv_w, s_conv_b,
                big_w_out[None]]

    grads = ordered(small_unpack(g_pk), g_w_ada_s, g_w_in_s, g_w_out_s)
    deltas = ordered(small_unpack(d_pk), d_w_ada, d_w_in, d_w_out)
    new_m = ordered(small_unpack(nm_pk), nm_w_ada, nm_w_in, nm_w_out)
    new_v = ordered(small_unpack(nv_pk), nv_w_ada, nv_w_in, nv_w_out)
    return (loss, r["grad_x"][None], *grads, *deltas, *new_m, *new_v)
```

```python
import functools

import jax
import jax.numpy as jnp
from jax import lax
from jax.experimental import pallas as pl
from jax.experimental.pallas import tpu as pltpu

f32, bf16, i32 = jnp.float32, jnp.bfloat16, jnp.int32
MESH = pl.DeviceIdType.MESH
HIGHEST = lax.Precision.HIGHEST

D = 1024
S = 2048
L = 256
GW = 64
ROWS = S // GW
H = 8
DH = 64
DA = H * DH
DC = 512
WIN_H, WIN_W = 8, 16
N_DR, N_DC = 2 * WIN_H - 1, 2 * WIN_W - 1
RMS_EPS = 1e-6
ROPE_THETA = 10000.0
QK_SCALE = DH ** -0.5
NEG = -1e30

QB = 128
NQB = S // QB
KR = 9
KB = KR * GW
TILE_GEOM = ((0, 0), (2, 0), (4, 0), (28, 23), (30, 23))
NT = len(TILE_GEOM)

ADAM_LR, ADAM_B1, ADAM_B2, ADAM_EPS, ADAM_WD, ADAM_STEP = 0.001, 0.9, 0.999, 1e-08, 0.01, 10

VMEM_SPEC = pl.BlockSpec(memory_space=pltpu.VMEM)
ANY_SPEC = pl.BlockSpec(memory_space=pl.ANY)
SMEM_SPEC = pl.BlockSpec(memory_space=pltpu.SMEM)
SDS = jax.ShapeDtypeStruct


def _cp(vmem_mb=None, **kw):
    if vmem_mb is not None:
        kw["vmem_limit_bytes"] = vmem_mb << 20
    return pltpu.CompilerParams(**kw)


def _silu(z):
    return z * jax.nn.sigmoid(z)


def _dsilu(z):
    sg = jax.nn.sigmoid(z)
    return sg * (1.0 + z * (1.0 - sg))


def _row_start(i):
    return min(max(i - WIN_H // 2, 0), ROWS - WIN_H)


def _my_pos():
    return lax.axis_index("x"), lax.axis_index("y"), lax.axis_index("c")


def _flip(v, bit):
    return 1 - v if bit else v


def _all_gather8(xin, name):
    R, N = xin.shape

    def body(x_ref, o_ref, ssem, rsem, lsem):
        x, y, c = _my_pos()
        me = 4 * x + 2 * y + c
        own = pltpu.make_async_copy(x_ref, o_ref.at[me], lsem)
        own.start()
        sends = []
        for k in range(1, 8):
            tgt = (_flip(x, (k >> 2) & 1), _flip(y, (k >> 1) & 1), _flip(c, k & 1))
            cp = pltpu.make_async_remote_copy(src_ref=x_ref, dst_ref=o_ref.at[me], send_sem=ssem.at[k - 1],
                                              recv_sem=rsem.at[k - 1], device_id=tgt, device_id_type=MESH)
            cp.start()
            sends.append(cp)
        for k in range(1, 8):
            tgt = (_flip(x, (k >> 2) & 1), _flip(y, (k >> 1) & 1), _flip(c, k & 1))
            peer = 4 * tgt[0] + 2 * tgt[1] + tgt[2]
            pltpu.make_async_remote_copy(src_ref=x_ref, dst_ref=o_ref.at[peer], send_sem=ssem.at[k - 1],
                                         recv_sem=rsem.at[k - 1], device_id=tgt, device_id_type=MESH).wait_recv()
        for cp in sends:
            cp.wait_send()
        own.wait()

    return pl.pallas_call(
        body, name=name, out_shape=SDS((8, R, N), xin.dtype), in_specs=[VMEM_SPEC], out_specs=VMEM_SPEC,
        scratch_shapes=[pltpu.SemaphoreType.DMA((7,)), pltpu.SemaphoreType.DMA((7,)), pltpu.SemaphoreType.DMA],
    )(xin)


def _chip_gather(smalls, bigs, name):
    ns, nb = len(smalls), len(bigs)
    n_sem = 3 * ns + 6 * nb

    def body(*refs):
        s_in, b_in = refs[:ns], refs[ns:ns + nb]
        s_out, b_out = refs[ns + nb:2 * ns + nb], refs[2 * ns + nb:2 * (ns + nb)]
        ssem, rsem, lsem = refs[2 * (ns + nb):]
        x, y, c = _my_pos()
        j = 2 * x + y
        sib = (x, y, 1 - c)
        chips = []
        for k in range(1, 4):
            px, py = _flip(x, (k >> 1) & 1), _flip(y, k & 1)
            chips.append(((px, py, c), 2 * px + py))
        local = [pltpu.make_async_copy(s_in[a], s_out[a].at[j], lsem.at[a]) for a in range(ns)]
        for cp in local:
            cp.start()
        sends = []
        sem = 0
        for a in range(ns):
            for k in range(3):
                cp = pltpu.make_async_remote_copy(src_ref=s_in[a], dst_ref=s_out[a].at[j], send_sem=ssem.at[sem + k],
                                                  recv_sem=rsem.at[sem + k], device_id=chips[k][0], device_id_type=MESH)
                cp.start()
                sends.append(cp)
            sem += 3
        big_first = sem
        for a in range(nb):
            half = b_in[a].shape[1] // 2
            mine = pl.ds(pl.multiple_of(c * half, 8), half)
            for k in range(3):
                cp = pltpu.make_async_remote_copy(src_ref=b_in[a].at[j, mine], dst_ref=b_out[a].at[j, mine],
                                                  send_sem=ssem.at[sem + k], recv_sem=rsem.at[sem + k],
                                                  device_id=chips[k][0], device_id_type=MESH)
                cp.start()
                sends.append(cp)
            sem += 6
        sem = big_first
        for a in range(nb):
            half = b_in[a].shape[1] // 2
            mine = pl.ds(pl.multiple_of(c * half, 8), half)
            for k in range(3):
                pj = chips[k][1]
                pltpu.make_async_remote_copy(src_ref=b_in[a].at[j, mine], dst_ref=b_out[a].at[pj, mine],
                                             send_sem=ssem.at[sem + k], recv_sem=rsem.at[sem + k],
                                             device_id=chips[k][0], device_id_type=MESH).wait_recv()
                cp = pltpu.make_async_remote_copy(src_ref=b_out[a].at[pj, mine], dst_ref=b_out[a].at[pj, mine],
                                                  send_sem=ssem.at[sem + 3 + k], recv_sem=rsem.at[sem + 3 + k],
                                                  device_id=sib, device_id_type=MESH)
                cp.start()
                sends.append(cp)
            sem += 6
        sem = 0
        for a in range(ns):
            for k in range(3):
                pj = chips[k][1]
                pltpu.make_async_remote_copy(src_ref=s_in[a], dst_ref=s_out[a].at[pj], send_sem=ssem.at[sem + k],
                                             recv_sem=rsem.at[sem + k], device_id=chips[k][0], device_id_type=MESH).wait_recv()
            sem += 3
        for a in range(nb):
            half = b_in[a].shape[1] // 2
            other = pl.ds(pl.multiple_of((1 - c) * half, 8), half)
            for k in range(3):
                pj = chips[k][1]
                pltpu.make_async_remote_copy(src_ref=b_out[a].at[pj, other], dst_ref=b_out[a].at[pj, other],
                                             send_sem=ssem.at[sem + 3 + k], recv_sem=rsem.at[sem + 3 + k],
                                             device_id=sib, device_id_type=MESH).wait_recv()
            sem += 6
        for cp in sends:
            cp.wait_send()
        for cp in local:
            cp.wait()

    out_shape = [SDS((4,) + a.shape, a.dtype) for a in smalls] + [SDS(a.shape, a.dtype) for a in bigs]
    return pl.pallas_call(
        body, name=name, out_shape=out_shape,
        in_specs=[VMEM_SPEC] * ns + [ANY_SPEC] * nb, out_specs=[VMEM_SPEC] * ns + [ANY_SPEC] * nb,
        input_output_aliases={ns + a: ns + a for a in range(nb)},
        scratch_shapes=[pltpu.SemaphoreType.DMA((n_sem,)), pltpu.SemaphoreType.DMA((n_sem,)),
                        pltpu.SemaphoreType.DMA((max(ns, 1),))],
    )(*smalls, *bigs)


def _cast_to_slot(w, jvec, name):
    rows, cols = w.shape
    tr = 256

    def body(j_ref, w_ref, o_ref):
        o_ref[...] = w_ref[...].astype(bf16)

    grid_spec = pltpu.PrefetchScalarGridSpec(
        num_scalar_prefetch=1, grid=(rows // tr,), in_specs=[pl.BlockSpec((tr, cols), lambda i, j: (i, 0))],
        out_specs=pl.BlockSpec((None, tr, cols), lambda i, j: (j[0], i, 0)))
    return pl.pallas_call(body, name=name, out_shape=SDS((4, rows, cols), bf16), grid_spec=grid_spec)(jvec, w)


def _sibling_exchange(mats, n_blocks, name):
    n = len(mats)
    n_cp = sum(n_blocks)

    def body(*refs):
        ins, outs = refs[:n], refs[n:2 * n]
        ssem, rsem = refs[2 * n:]
        x, y, c = _my_pos()
        sib = (x, y, 1 - c)
        sends = []
        sem = 0
        for a in range(n):
            rb = mats[a].shape[0] // n_blocks[a]
            half = rb // 2
            for jb in range(n_blocks[a]):
                theirs = pl.ds(pl.multiple_of(jb * rb + (1 - c) * half, 8), half)
                cp = pltpu.make_async_remote_copy(src_ref=ins[a].at[theirs], dst_ref=outs[a].at[jb],
                                                  send_sem=ssem.at[sem], recv_sem=rsem.at[sem],
                                                  device_id=sib, device_id_type=MESH)
                cp.start()
                sends.append(cp)
                sem += 1
        for cp in sends:
            cp.wait_recv()
        for cp in sends:
            cp.wait_send()

    out_shape = [SDS((n_blocks[a], mats[a].shape[0] // n_blocks[a] // 2, mats[a].shape[1]), mats[a].dtype)
                 for a in range(n)]
    return pl.pallas_call(
        body, name=name, out_shape=out_shape, in_specs=[ANY_SPEC] * n, out_specs=[ANY_SPEC] * n,
        scratch_shapes=[pltpu.SemaphoreType.DMA((n_cp,)), pltpu.SemaphoreType.DMA((n_cp,))],
    )(*mats)


def _chip_scatter(parts, name):
    n = len(parts)

    def body(*refs):
        ins, outs = refs[:n], refs[n:2 * n]
        ssem, rsem = refs[2 * n:]
        x, y, c = _my_pos()
        sends = []
        sem = 0
        for a in range(n):
            for k in range(1, 4):
                px, py = _flip(x, (k >> 1) & 1), _flip(y, k & 1)
                cp = pltpu.make_async_remote_copy(src_ref=ins[a].at[2 * px + py], dst_ref=outs[a].at[k - 1],
                                                  send_sem=ssem.at[sem], recv_sem=rsem.at[sem],
                                                  device_id=(px, py, c), device_id_type=MESH)
                cp.start()
                sends.append(cp)
                sem += 1
        for cp in sends:
            cp.wait_recv()
        for cp in sends:
            cp.wait_send()

    out_shape = [SDS((3,) + p.shape[1:], p.dtype) for p in parts]
    return pl.pallas_call(
        body, name=name, out_shape=out_shape, in_specs=[ANY_SPEC] * n, out_specs=[ANY_SPEC] * n,
        scratch_shapes=[pltpu.SemaphoreType.DMA((3 * n,)), pltpu.SemaphoreType.DMA((3 * n,))],
    )(*parts)


def _sibling_send(halves, name):
    n = len(halves)

    def body(*refs):
        ins, outs = refs[:n], refs[n:2 * n]
        ssem, rsem = refs[2 * n:]
        x, y, c = _my_pos()
        cps = []
        for a in range(n):
            cp = pltpu.make_async_remote_copy(src_ref=ins[a], dst_ref=outs[a], send_sem=ssem.at[a],
                                              recv_sem=rsem.at[a], device_id=(x, y, 1 - c), device_id_type=MESH)
            cp.start()
            cps.append(cp)
        for cp in cps:
            cp.wait_recv()
        for cp in cps:
            cp.wait_send()

    out_shape = [SDS(h.shape, h.dtype) for h in halves]
    return pl.pallas_call(
        body, name=name, out_shape=out_shape, in_specs=[ANY_SPEC] * n, out_specs=[ANY_SPEC] * n,
        scratch_shapes=[pltpu.SemaphoreType.DMA((n,)), pltpu.SemaphoreType.DMA((n,))],
    )(*halves)


def _adaln_shard(cc, w_ada_shard):
    def body(c_ref, w_ref, m_ref, sc_ref):
        sc = _silu(c_ref[...])
        sc_ref[...] = sc
        m_ref[...] = jnp.dot(sc, w_ref[...], precision=HIGHEST, preferred_element_type=f32)

    return pl.pallas_call(
        body, name="adaln_shard", out_shape=(SDS((16, w_ada_shard.shape[1]), f32), SDS((16, D), f32)),
        in_specs=[VMEM_SPEC, VMEM_SPEC], out_specs=(VMEM_SPEC, VMEM_SPEC), compiler_params=_cp(32),
    )(cc, w_ada_shard)


def _prenorm(xx, norm_g, mrow, b_ada, tm, name):
    n = xx.shape[0]

    def body(x_ref, g_ref, m_ref, b_ref, h_ref):
        x = x_ref[...]
        shift = m_ref[:, 0:D] + b_ref[:, 0:D]
        scale = m_ref[:, D:2 * D] + b_ref[:, D:2 * D]
        r = lax.rsqrt(jnp.mean(x * x, axis=-1, keepdims=True) + RMS_EPS)
        y = (x * r) * g_ref[...]
        h_ref[...] = (y * (1.0 + scale) + shift).astype(bf16)

    row = lambda i: (i, 0)
    fixed = lambda i: (0, 0)
    return pl.pallas_call(
        body, name=name, out_shape=SDS((n, D), bf16), grid=(n // tm,),
        in_specs=[pl.BlockSpec((tm, D), row), pl.BlockSpec((1, D), fixed), pl.BlockSpec((1, 3 * D), fixed),
                  pl.BlockSpec((1, 3 * D), fixed)],
        out_specs=pl.BlockSpec((tm, D), row),
    )(xx, norm_g, mrow, b_ada)


def _in_proj(h, w4):
    tm = 512

    def body(h_ref, w_ref, p_ref):
        p_ref[...] = jnp.dot(h_ref[...], w_ref[...], preferred_element_type=f32)

    return pl.pallas_call(
        body, name="in_proj", out_shape=SDS((S, 4 * D), f32), grid=(4, S // tm),
        in_specs=[pl.BlockSpec((tm, D), lambda j, k: (k, 0)), pl.BlockSpec((None, D, D), lambda j, k: (j, 0, 0))],
        out_specs=pl.BlockSpec((tm, D), lambda j, k: (k, j)),
    )(h, w4)


def _ctx_proj(hc, w4):
    def body(h_ref, w0_ref, w1_ref, p_ref):
        hv = h_ref[...]
        p_ref[:, 0:DA] = jnp.dot(hv, w0_ref[:, DA:2 * DA], preferred_element_type=f32)
        p_ref[:, DA:2 * DA] = jnp.dot(hv, w1_ref[:, 0:DA], preferred_element_type=f32)

    return pl.pallas_call(
        body, name="ctx_proj", out_shape=SDS((L, 2 * DA), f32), grid=(1,),
        in_specs=[pl.BlockSpec((L, D), lambda i: (0, 0)), pl.BlockSpec((None, D, D), lambda i: (0, 0, 0)),
                  pl.BlockSpec((None, D, D), lambda i: (1, 0, 0))],
        out_specs=pl.BlockSpec((L, 2 * DA), lambda i: (0, 0)),
    )(hc, w4, w4)


def _head_ones():
    r = lax.broadcasted_iota(i32, (DA, DA), 0) // DH
    c = lax.broadcasted_iota(i32, (DA, DA), 1) // DH
    return (r == c).astype(f32)


def _head_sum(v, ones_bd):
    return jnp.dot(v, ones_bd, precision=HIGHEST, preferred_element_type=f32)


def _swap16(v):
    lane = lax.broadcasted_iota(i32, v.shape, 1)
    return jnp.where((lane & 31) < 16, pltpu.roll(v, DA - 16, 1), pltpu.roll(v, 16, 1))


def _rope_block(ct_ref, rt_ref, tm):
    rows = [jnp.tile(rt_ref[8 * j:8 * j + 8, :], (GW // 8, 1)) for j in range(tm // GW)]
    return jnp.tile(ct_ref[...], (tm // GW, 1)) + jnp.concatenate(rows, axis=0)


def _rope_specs(tm):
    col = pl.BlockSpec((GW, DA), lambda i: (0, 0))
    row = pl.BlockSpec((8 * tm // GW, DA), lambda i: (i, 0))
    return [col, row, col, row]


def _qk_prep(p, gq, gk, rope):
    tm = 256

    def body(qk_ref, v_ref, gq_ref, gk_ref, cc_ref, cr_ref, sc_ref, sr_ref, qr_ref, qp_ref, kr_ref, vh_ref):
        ones_bd = _head_ones()
        cs, sn = _rope_block(cc_ref, cr_ref, tm), _rope_block(sc_ref, sr_ref, tm)
        q = qk_ref[:, 0:DA]
        k = qk_ref[:, DA:2 * DA]
        yq = (q * lax.rsqrt(_head_sum(q * q, ones_bd) * (1.0 / DH) + RMS_EPS)) * gq_ref[...]
        yk = (k * lax.rsqrt(_head_sum(k * k, ones_bd) * (1.0 / DH) + RMS_EPS)) * gk_ref[...]
        qr = (yq * cs + _swap16(yq) * sn) * QK_SCALE
        qp = yq * QK_SCALE
        kr = yk * cs + _swap16(yk) * sn
        vv = v_ref[...]
        for hh in range(H):
            sl = slice(hh * DH, (hh + 1) * DH)
            qr_ref[hh] = qr[:, sl].astype(bf16)
            qp_ref[hh] = qp[:, sl].astype(bf16)
            kr_ref[hh] = kr[:, sl].astype(bf16)
            vh_ref[hh] = vv[:, sl].astype(bf16)

    hm = SDS((H, S, DH), bf16)
    hspec = pl.BlockSpec((H, tm, DH), lambda i: (0, i, 0))
    fixed = lambda i: (0, 0)
    return pl.pallas_call(
        body, name="qk_prep", out_shape=(hm, hm, hm, hm), grid=(S // tm,),
        in_specs=[pl.BlockSpec((tm, 2 * DA), lambda i: (i, 0)), pl.BlockSpec((tm, DA), lambda i: (i, 2)),
                  pl.BlockSpec((1, DA), fixed), pl.BlockSpec((1, DA), fixed)] + _rope_specs(tm),
        out_specs=(hspec, hspec, hspec, hspec),
    )(p, p, gq, gk, *rope)


def _ctx_prep(pc, gk):
    def body(p_ref, gk_ref, kc_ref, vc_ref):
        ones_bd = _head_ones()
        k = p_ref[:, 0:DA]
        yk = (k * lax.rsqrt(_head_sum(k * k, ones_bd) * (1.0 / DH) + RMS_EPS)) * gk_ref[...]
        vv = p_ref[:, DA:2 * DA]
        for hh in range(H):
            sl = slice(hh * DH, (hh + 1) * DH)
            kc_ref[hh] = yk[:, sl].astype(bf16)
            vc_ref[hh] = vv[:, sl].astype(bf16)

    hm = SDS((H, L, DH), bf16)
    return pl.pallas_call(
        body, name="ctx_prep", out_shape=(hm, hm), in_specs=[VMEM_SPEC, VMEM_SPEC], out_specs=(VMEM_SPEC, VMEM_SPEC),
    )(pc, gk)


def _tile_pieces():
    out = []
    for (i0, u0) in TILE_GEOM:
        rows = []
        for j in range(2):
            i = i0 + j
            rs = _row_start(i)
            rows.append([(u0 + u - i + WIN_H - 1) if rs <= u0 + u < rs + WIN_H else None for u in range(KR)])
        out.append(rows)
    return out


def _bias_prep(rpb_pad):
    pieces = _tile_pieces()

    def body(r_ref, o_ref):
        rp = r_ref[...]
        xs = jnp.broadcast_to(rp[:, None, :], (N_DR, GW, 128)).reshape(N_DR * GW, 128)
        row = lax.broadcasted_iota(i32, xs.shape, 0)
        lane = lax.broadcasted_iota(i32, xs.shape, 1)
        for b in range(6):
            xs = jnp.where(((row >> b) & 1) == 1, pltpu.roll(xs, 1 << b, 1), xs)
        xs = pltpu.roll(xs, 128 - (WIN_W - 1), 1)
        q = row & (GW - 1)
        c0 = jnp.clip(q - WIN_W // 2, 0, GW - WIN_W)
        xs = jnp.where((lane >= c0) & (lane < c0 + WIN_W), xs, NEG)
        neg = jnp.full((GW, GW), NEG, f32)
        for t in range(NT):
            for j in range(2):
                for u in range(KR):
                    dr = pieces[t][j][u]
                    piece = neg if dr is None else xs[dr * GW:(dr + 1) * GW, 0:GW]
                    o_ref[t, j * GW:(j + 1) * GW, u * GW:(u + 1) * GW] = piece

    return pl.pallas_call(
        body, name="bias_prep", out_shape=SDS((H, NT, QB, KB), f32), grid=(H,),
        in_specs=[pl.BlockSpec((None, N_DR, 128), lambda h: (h, 0, 0))],
        out_specs=pl.BlockSpec((None, NT, QB, KB), lambda h: (h, 0, 0, 0)),
    )(rpb_pad)


def _block_geom(b):
    qs = b * QB
    ks = min(max(2 * b - 4, 0), ROWS - KR) * GW
    t = b if b < 2 else (b - (NQB - NT) if b > NQB - 3 else 2)
    return qs, ks, t


def _tt(a, b):
    return lax.dot_general(a, b, (((1,), (1,)), ((), ())), preferred_element_type=f32)


def _tn(a, b):
    return lax.dot_general(a, b, (((0,), (0,)), ((), ())), preferred_element_type=f32)


def _softmax_block(qr, qp, k, kc, bias):
    s_lat = _tt(qr, k) + bias
    s_ctx = _tt(qp, kc)
    m = jnp.maximum(jnp.max(s_lat, axis=-1, keepdims=True), jnp.max(s_ctx, axis=-1, keepdims=True))
    e_lat = jnp.exp(s_lat - m)
    e_ctx = jnp.exp(s_ctx - m)
    inv = 1.0 / (jnp.sum(e_lat, axis=-1, keepdims=True) + jnp.sum(e_ctx, axis=-1, keepdims=True))
    return e_lat * inv, e_ctx * inv


def _attn_fwd(qr, qp, kr, vh, kc, vc, btb):
    def body(qr_ref, qp_ref, kr_ref, v_ref, kc_ref, vc_ref, bt_ref, o_ref):
        kcv, vcv = kc_ref[...], vc_ref[...]

        def blk(b, carry):
            qs, ks, t = _block_geom(b)
            p_lat, p_ctx = _softmax_block(qr_ref[pl.ds(qs, QB), :], qp_ref[pl.ds(qs, QB), :],
                                          kr_ref[pl.ds(ks, KB), :], kcv, bt_ref[t])
            o = jnp.dot(p_lat.astype(bf16), v_ref[pl.ds(ks, KB), :], preferred_element_type=f32)
            o_ref[pl.ds(qs, QB), :] = o + jnp.dot(p_ctx.astype(bf16), vcv, preferred_element_type=f32)
            return carry

        for b in range(NQB):
            blk(b, 0)

    sq = pl.BlockSpec((None, S, DH), lambda h: (h, 0, 0))
    sc = pl.BlockSpec((None, L, DH), lambda h: (h, 0, 0))
    return pl.pallas_call(
        body, name="attn_fwd", out_shape=SDS((H, S, DH), f32), grid=(H,),
        in_specs=[sq, sq, sq, sq, sc, sc, pl.BlockSpec((None, NT, QB, KB), lambda h: (h, 0, 0, 0))],
        out_specs=sq, compiler_params=_cp(40),
    )(qr, qp, kr, vh, kc, vc, btb)


def _attn_gate(o, p):
    tm = 256

    def body(o_ref, za_ref, a_ref):
        sz = _silu(za_ref[...])
        for hh in range(H):
            sl = slice(hh * DH, (hh + 1) * DH)
            a_ref[:, sl] = (o_ref[hh] * sz[:, sl]).astype(bf16)

    return pl.pallas_call(
        body, name="attn_gate", out_shape=SDS((S, DA), bf16), grid=(S // tm,),
        in_specs=[pl.BlockSpec((H, tm, DH), lambda i: (0, i, 0)), pl.BlockSpec((tm, DA), lambda i: (i, 3))],
        out_specs=pl.BlockSpec((tm, DA), lambda i: (i, 0)),
    )(o, p)


def _shift_rows(v, down):
    n = v.shape[0]
    row = lax.broadcasted_iota(i32, v.shape, 0)
    if down:
        return jnp.where(row == 0, 0.0, pltpu.roll(v, 1, 0))
    return jnp.where(row == n - 1, 0.0, pltpu.roll(v, n - 1, 0))


def _conv_specs():
    col = lambda off: pl.BlockSpec((S, 128), lambda i, off=off: (0, off + i))
    return [col(16), col(20), col(24), col(28), pl.BlockSpec((3, 128), lambda i: (0, i)),
            pl.BlockSpec((1, 128), lambda i: (0, i))]


def _conv_fwd(p, conv_w, conv_b):
    def body(u_ref, bg_ref, cg_ref, zc_ref, w_ref, b_ref, o_ref):
        cu = cg_ref[...] * u_ref[...]
        cv = b_ref[...] + _shift_rows(cu, True) * w_ref[0:1, :]
        cv = cv + cu * w_ref[1:2, :]
        cv = cv + _shift_rows(cu, False) * w_ref[2:3, :]
        o_ref[...] = ((bg_ref[...] * cv) * _silu(zc_ref[...])).astype(bf16)

    return pl.pallas_call(
        body, name="conv_fwd", out_shape=SDS((S, DC), bf16), grid=(DC // 128,),
        in_specs=_conv_specs(), out_specs=pl.BlockSpec((S, 128), lambda i: (0, i)), compiler_params=_cp(40),
    )(p, p, p, p, conv_w, conv_b)


def _out_proj_loss(attn_g, conv_g, w_out, xx, tgt, mrow, b_ada):
    tm = 256

    def body(a_ref, c_ref, w_ref, x_ref, t_ref, m_ref, b_ref, dy_ref, dmix_ref, gwo_ref, dgate_ref, loss_ref):
        k = pl.program_id(0)

        @pl.when(k == 0)
        def _():
            gwo_ref[...] = jnp.zeros_like(gwo_ref)
            dgate_ref[...] = jnp.zeros_like(dgate_ref)
            loss_ref[0, 0] = 0.0

        gate = m_ref[:, 2 * D:3 * D] + b_ref[:, 2 * D:3 * D]
        av, cv = a_ref[...], c_ref[...]
        mo = jnp.dot(av, w_ref[0:DA, :], preferred_element_type=f32)
        mo = mo + jnp.dot(cv, w_ref[DA:DA + DC, :], preferred_element_type=f32)
        y = x_ref[...] + gate * mo
        diff = y - t_ref[...]
        loss_ref[0, 0] += jnp.sum(diff * diff)
        dy = diff * (1.0 / D)
        dy_ref[...] = dy
        dgate_ref[...] += jnp.sum(dy * mo, axis=0, keepdims=True)
        dmo = (dy * gate).astype(bf16)
        dmix_ref[...] = _tt(dmo, w_ref[...])
        gwo_ref[0:DA, :] += _tn(av, dmo)
        gwo_ref[DA:DA + DC, :] += _tn(cv, dmo)

    row = lambda i: (i, 0)
    fixed = lambda i: (0, 0)
    return pl.pallas_call(
        body, name="out_proj_loss",
        out_shape=(SDS((S, D), f32), SDS((S, D), f32), SDS((D, D), f32), SDS((1, D), f32), SDS((1, 1), f32)),
        grid=(S // tm,),
        in_specs=[pl.BlockSpec((tm, DA), row), pl.BlockSpec((tm, DC), row), pl.BlockSpec((D, D), fixed),
                  pl.BlockSpec((tm, D), row), pl.BlockSpec((tm, D), row), pl.BlockSpec((1, 3 * D), fixed),
                  pl.BlockSpec((1, 3 * D), fixed)],
        out_specs=(pl.BlockSpec((tm, D), row), pl.BlockSpec((tm, D), row), pl.BlockSpec((D, D), fixed),
                   pl.BlockSpec((1, D), fixed), SMEM_SPEC),
        compiler_params=_cp(48, dimension_semantics=("arbitrary",)),
    )(attn_g, conv_g, w_out, xx, tgt, mrow, b_ada)


def _conv_bwd(dmix, p, conv_w, conv_b):
    def body(d_ref, u_ref, bg_ref, cg_ref, zc_ref, w_ref, b_ref, du_ref, dbg_ref, dcg_ref, dzc_ref, gw_ref, gb_ref):
        dconv = d_ref[...]
        u, bg, cg, zc = u_ref[...], bg_ref[...], cg_ref[...], zc_ref[...]
        w0, w1, w2 = w_ref[0:1, :], w_ref[1:2, :], w_ref[2:3, :]
        cu = cg * u
        cu_m, cu_p = _shift_rows(cu, True), _shift_rows(cu, False)
        cv = b_ref[...] + cu_m * w0
        cv = cv + cu * w1
        cv = cv + cu_p * w2
        sz = _silu(zc)
        dbg_ref[...] = ((dconv * sz) * cv).astype(bf16)
        dzc_ref[...] = ((dconv * (bg * cv)) * _dsilu(zc)).astype(bf16)
        dcv = (dconv * sz) * bg
        gb_ref[...] = jnp.sum(dcv, axis=0, keepdims=True)
        gw_ref[0:1, :] = jnp.sum(dcv * cu_m, axis=0, keepdims=True)
        gw_ref[1:2, :] = jnp.sum(dcv * cu, axis=0, keepdims=True)
        gw_ref[2:3, :] = jnp.sum(dcv * cu_p, axis=0, keepdims=True)
        gw_ref[3:8, :] = jnp.zeros((5, 128), f32)
        dcu = _shift_rows(dcv, False) * w0 + dcv * w1 + _shift_rows(dcv, True) * w2
        dcg_ref[...] = (dcu * u).astype(bf16)
        du_ref[...] = (dcu * cg).astype(bf16)

    piece = SDS((S, DC), bf16)
    ospec = pl.BlockSpec((S, 128), lambda i: (0, i))
    return pl.pallas_call(
        body, name="conv_bwd", out_shape=(piece, piece, piece, piece, SDS((8, DC), f32), SDS((1, DC), f32)),
        grid=(DC // 128,),
        in_specs=[pl.BlockSpec((S, 128), lambda i: (0, 4 + i))] + _conv_specs(),
        out_specs=(ospec, ospec, ospec, ospec, pl.BlockSpec((8, 128), lambda i: (0, i)),
                   pl.BlockSpec((1, 128), lambda i: (0, i))),
        compiler_params=_cp(48),
    )(dmix, p, p, p, p, conv_w, conv_b)


def _gate_bwd(dmix, p, o):
    tm = 256

    def body(d_ref, za_ref, o_ref, do_ref, dza_ref):
        za = za_ref[...]
        dattn = d_ref[...]
        a = dattn * _silu(za)
        bb = dattn * _dsilu(za)
        for hh in range(H):
            sl = slice(hh * DH, (hh + 1) * DH)
            do_ref[hh] = a[:, sl].astype(bf16)
            dza_ref[:, sl] = (bb[:, sl] * o_ref[hh]).astype(bf16)

    hspec = pl.BlockSpec((H, tm, DH), lambda i: (0, i, 0))
    return pl.pallas_call(
        body, name="gate_bwd", out_shape=(SDS((H, S, DH), bf16), SDS((S, DA), bf16)), grid=(S // tm,),
        in_specs=[pl.BlockSpec((tm, DA), lambda i: (i, 0)), pl.BlockSpec((tm, DA), lambda i: (i, 3)), hspec],
        out_specs=(hspec, pl.BlockSpec((tm, DA), lambda i: (i, 0))),
    )(dmix, p, o)


def _attn_bwd(qr, qp, kr, vh, kc, vc, btb, do):
    def body(qr_ref, qp_ref, kr_ref, v_ref, kc_ref, vc_ref, bt_ref, do_ref,
             dqr_ref, dqp_ref, dkr_ref, dv_ref, dkc_ref, dvc_ref, dbt_ref):
        kcv, vcv = kc_ref[...], vc_ref[...]
        dkr_ref[...] = jnp.zeros_like(dkr_ref)
        dv_ref[...] = jnp.zeros_like(dv_ref)
        dkc_ref[...] = jnp.zeros_like(dkc_ref)
        dvc_ref[...] = jnp.zeros_like(dvc_ref)
        dbt_ref[...] = jnp.zeros_like(dbt_ref)

        def blk(b, carry):
            qs, ks, t = _block_geom(b)
            qrb, qpb = qr_ref[pl.ds(qs, QB), :], qp_ref[pl.ds(qs, QB), :]
            kb, vb = kr_ref[pl.ds(ks, KB), :], v_ref[pl.ds(ks, KB), :]
            dob = do_ref[pl.ds(qs, QB), :]
            p_lat, p_ctx = _softmax_block(qrb, qpb, kb, kcv, bt_ref[t])
            dp_lat = _tt(dob, vb)
            dp_ctx = _tt(dob, vcv)
            delta = jnp.sum(p_lat * dp_lat, axis=-1, keepdims=True) + jnp.sum(p_ctx * dp_ctx, axis=-1, keepdims=True)
            ds_lat = p_lat * (dp_lat - delta)
            ds_ctx = p_ctx * (dp_ctx - delta)
            dbt_ref[t] += ds_lat
            dsb_lat, dsb_ctx = ds_lat.astype(bf16), ds_ctx.astype(bf16)
            dqr_ref[pl.ds(qs, QB), :] = jnp.dot(dsb_lat, kb, preferred_element_type=f32)
            dqp_ref[pl.ds(qs, QB), :] = jnp.dot(dsb_ctx, kcv, preferred_element_type=f32)
            dkr_ref[pl.ds(ks, KB), :] += _tn(dsb_lat, qrb)
            dv_ref[pl.ds(ks, KB), :] += _tn(p_lat.astype(bf16), dob)
            dkc_ref[...] += _tn(dsb_ctx, qpb)
            dvc_ref[...] += _tn(p_ctx.astype(bf16), dob)
            return carry

        for b in range(NQB):
            blk(b, 0)

    sq = pl.BlockSpec((None, S, DH), lambda h: (h, 0, 0))
    sc = pl.BlockSpec((None, L, DH), lambda h: (h, 0, 0))
    sb = pl.BlockSpec((None, NT, QB, KB), lambda h: (h, 0, 0, 0))
    big, ctxs = SDS((H, S, DH), f32), SDS((H, L, DH), f32)
    return pl.pallas_call(
        body, name="attn_bwd", out_shape=(big, big, big, big, ctxs, ctxs, SDS((H, NT, QB, KB), f32)), grid=(H,),
        in_specs=[sq, sq, sq, sq, sc, sc, sb, sq], out_specs=(sq, sq, sq, sq, sc, sc, sb), compiler_params=_cp(48),
    )(qr, qp, kr, vh, kc, vc, btb, do)


def _bias_bwd(dbtb):
    pieces = _tile_pieces()

    def body(d_ref, o_ref, scr):
        scr[...] = jnp.zeros_like(scr)
        acc = [None] * N_DR
        for t in range(NT):
            for j in range(2):
                for u in range(KR):
                    dr = pieces[t][j][u]
                    if dr is None:
                        continue
                    piece = d_ref[t, j * GW:(j + 1) * GW, u * GW:(u + 1) * GW]
                    acc[dr] = piece if acc[dr] is None else acc[dr] + piece
        for dr in range(N_DR):
            scr[dr * GW:(dr + 1) * GW, 0:GW] = acc[dr]
        xs = pltpu.roll(scr[...], WIN_W - 1, 1)
        row = lax.broadcasted_iota(i32, xs.shape, 0)
        for b in range(6):
            xs = jnp.where(((row >> b) & 1) == 1, pltpu.roll(xs, 128 - (1 << b), 1), xs)
        o_ref[...] = jnp.sum(xs.reshape(N_DR, GW, 128), axis=1)

    return pl.pallas_call(
        body, name="bias_bwd", out_shape=SDS((H, N_DR, 128), f32), grid=(H,),
        in_specs=[pl.BlockSpec((None, NT, QB, KB), lambda h: (h, 0, 0, 0))],
        out_specs=pl.BlockSpec((None, N_DR, 128), lambda h: (h, 0, 0)),
        scratch_shapes=[pltpu.VMEM((N_DR * GW, 128), f32)],
    )(dbtb)


def _merge_heads(ref):
    return jnp.concatenate([ref[hh] for hh in range(H)], axis=1)


def _head_norm_bwd(xraw, gain, dy, ones_bd):
    r = lax.rsqrt(_head_sum(xraw * xraw, ones_bd) * (1.0 / DH) + RMS_EPS)
    xh = xraw * r
    gdy = dy * gain
    dx = r * (gdy - xh * (_head_sum(xh * gdy, ones_bd) * (1.0 / DH)))
    return dx, jnp.sum(dy * xh, axis=0, keepdims=True)


def _qk_bwd(dqr, dqp, dkr, dvh, p, gq, gk, rope):
    tm = 256

    def body(dqr_ref, dqp_ref, dkr_ref, dv_ref, qk_ref, gq_ref, gk_ref, cc_ref, cr_ref, sc_ref, sr_ref,
             dq_ref, dk_ref, dvo_ref, ggq_ref, ggk_ref):
        @pl.when(pl.program_id(0) == 0)
        def _():
            ggq_ref[...] = jnp.zeros_like(ggq_ref)
            ggk_ref[...] = jnp.zeros_like(ggk_ref)

        ones_bd = _head_ones()
        cs, sn = _rope_block(cc_ref, cr_ref, tm), _rope_block(sc_ref, sr_ref, tm)
        a = _merge_heads(dqr_ref)
        dyq = ((a * cs - _swap16(a) * sn) + _merge_heads(dqp_ref)) * QK_SCALE
        bk = _merge_heads(dkr_ref)
        dyk = bk * cs - _swap16(bk) * sn
        dq, gq_part = _head_norm_bwd(qk_ref[:, 0:DA], gq_ref[...], dyq, ones_bd)
        dk, gk_part = _head_norm_bwd(qk_ref[:, DA:2 * DA], gk_ref[...], dyk, ones_bd)
        dq_ref[...] = dq.astype(bf16)
        dk_ref[...] = dk.astype(bf16)
        dvo_ref[...] = _merge_heads(dv_ref).astype(bf16)
        ggq_ref[...] += gq_part
        ggk_ref[...] += gk_part

    hspec = pl.BlockSpec((H, tm, DH), lambda i: (0, i, 0))
    row = pl.BlockSpec((tm, DA), lambda i: (i, 0))
    fixed = pl.BlockSpec((1, DA), lambda i: (0, 0))
    piece = SDS((S, DA), bf16)
    return pl.pallas_call(
        body, name="qk_bwd", out_shape=(piece, piece, piece, SDS((1, DA), f32), SDS((1, DA), f32)), grid=(S // tm,),
        in_specs=[hspec, hspec, hspec, hspec, pl.BlockSpec((tm, 2 * DA), lambda i: (i, 0)), fixed, fixed]
        + _rope_specs(tm),
        out_specs=(row, row, row, fixed, fixed), compiler_params=_cp(40, dimension_semantics=("arbitrary",)),
    )(dqr, dqp, dkr, dvh, p, gq, gk, *rope)


def _ctx_bwd(dkc, dvc, pc, gk):
    def body(dkc_ref, dvc_ref, p_ref, gk_ref, dk_ref, dv_ref, ggk_ref):
        ones_bd = _head_ones()
        dk, gk_part = _head_norm_bwd(p_ref[:, 0:DA], gk_ref[...], _merge_heads(dkc_ref), ones_bd)
        dk_ref[...] = dk.astype(bf16)
        dv_ref[...] = _merge_heads(dvc_ref).astype(bf16)
        ggk_ref[...] = gk_part

    piece = SDS((L, DA), bf16)
    return pl.pallas_call(
        body, name="ctx_bwd", out_shape=(piece, piece, SDS((1, DA), f32)), in_specs=[VMEM_SPEC] * 4,
        out_specs=(VMEM_SPEC,) * 3,
    )(dkc, dvc, pc, gk)


def _grad_w_in_block(h, pa, pb, name, hc=None, pc=None, ctx_cols=None):
    tm = 512

    def body(*refs):
        if hc is None:
            h_ref, a_ref, b_ref, g_ref = refs
        else:
            h_ref, a_ref, b_ref, hc_ref, pc_ref, g_ref = refs

        @pl.when(pl.program_id(0) == 0)
        def _():
            g_ref[...] = jnp.zeros_like(g_ref)
            if hc is not None:
                g_ref[:, ctx_cols[0]:ctx_cols[1]] = _tn(hc_ref[...], pc_ref[...])

        hv = h_ref[...]
        g_ref[:, 0:DA] += _tn(hv, a_ref[...])
        g_ref[:, DA:2 * DA] += _tn(hv, b_ref[...])

    row = lambda i: (i, 0)
    fixed = lambda i: (0, 0)
    in_specs = [pl.BlockSpec((tm, D), row), pl.BlockSpec((tm, DA), row), pl.BlockSpec((tm, DA), row)]
    args = [h, pa, pb]
    if hc is not None:
        in_specs += [pl.BlockSpec((L, D), fixed), pl.BlockSpec((L, DA), fixed)]
        args += [hc, pc]
    return pl.pallas_call(
        body, name=name, out_shape=SDS((D, D), f32), grid=(S // tm,), in_specs=in_specs,
        out_specs=pl.BlockSpec((D, D), fixed), compiler_params=_cp(40, dimension_semantics=("arbitrary",)),
    )(*args)


def _norm_mod_bwd(x, dh, g, scale):
    r = lax.rsqrt(jnp.mean(x * x, axis=-1, keepdims=True) + RMS_EPS)
    xh = x * r
    y = xh * g
    dshift = jnp.sum(dh, axis=0, keepdims=True)
    dscale = jnp.sum(dh * y, axis=0, keepdims=True)
    dyn = dh * (1.0 + scale)
    dg = jnp.sum(dyn * xh, axis=0, keepdims=True)
    gdy = dyn * g
    dx = r * (gdy - xh * jnp.mean(xh * gdy, axis=-1, keepdims=True))
    return dx, dshift, dscale, dg


def _dh_grad_x(pieces, w4, xx, dy, norm_g, mrow, b_ada):
    tm = 256

    def body(*refs):
        p_refs = refs[:8]
        w_ref, x_ref, dy_ref, g_ref, m_ref, b_ref, gx_ref, dsh_ref, dsc_ref, dg_ref = refs[8:]

        @pl.when(pl.program_id(0) == 0)
        def _():
            dsh_ref[...] = jnp.zeros_like(dsh_ref)
            dsc_ref[...] = jnp.zeros_like(dsc_ref)
            dg_ref[...] = jnp.zeros_like(dg_ref)

        dh = None
        for j in range(4):
            for half in range(2):
                term = _tt(p_refs[2 * j + half][...], w_ref[j, :, half * DA:(half + 1) * DA])
                dh = term if dh is None else dh + term
        scale = m_ref[:, D:2 * D] + b_ref[:, D:2 * D]
        dx, dshift, dscale, dg = _norm_mod_bwd(x_ref[...], dh, g_ref[...], scale)
        gx_ref[...] = dy_ref[...] + dx
        dsh_ref[...] += dshift
        dsc_ref[...] += dscale
        dg_ref[...] += dg

    row = lambda i: (i, 0)
    fixed = lambda i: (0, 0)
    vec = SDS((1, D), f32)
    return pl.pallas_call(
        body, name="dh_grad_x", out_shape=(SDS((S, D), f32), vec, vec, vec), grid=(S // tm,),
        in_specs=[pl.BlockSpec((tm, DA), row)] * 8 + [pl.BlockSpec((4, D, D), lambda i: (0, 0, 0)),
                                                      pl.BlockSpec((tm, D), row), pl.BlockSpec((tm, D), row),
                                                      pl.BlockSpec((1, D), fixed), pl.BlockSpec((1, 3 * D), fixed),
                                                      pl.BlockSpec((1, 3 * D), fixed)],
        out_specs=(pl.BlockSpec((tm, D), row), pl.BlockSpec((1, D), fixed), pl.BlockSpec((1, D), fixed),
                   pl.BlockSpec((1, D), fixed)),
        compiler_params=_cp(56, dimension_semantics=("arbitrary",)),
    )(*pieces, w4, xx, dy, norm_g, mrow, b_ada)


def _dhc_sums(dkc_raw, dvc_m, w4, ctx2, norm_g, mrow_c, b_ada):
    def body(dk_ref, dv_ref, w0_ref, w1_ref, x_ref, g_ref, m_ref, b_ref, dsh_ref, dsc_ref, dg_ref):
        dh = _tt(dk_ref[...], w0_ref[:, DA:2 * DA]) + _tt(dv_ref[...], w1_ref[:, 0:DA])
        scale = m_ref[:, D:2 * D] + b_ref[:, D:2 * D]
        _, dshift, dscale, dg = _norm_mod_bwd(x_ref[...], dh, g_ref[...], scale)
        dsh_ref[...] = dshift
        dsc_ref[...] = dscale
        dg_ref[...] = dg

    fixed = lambda i: (0, 0)
    vec = SDS((1, D), f32)
    vspec = pl.BlockSpec((1, D), fixed)
    return pl.pallas_call(
        body, name="dhc_sums", out_shape=(vec, vec, vec), grid=(1,),
        in_specs=[pl.BlockSpec((L, DA), fixed), pl.BlockSpec((L, DA), fixed),
                  pl.BlockSpec((None, D, D), lambda i: (0, 0, 0)), pl.BlockSpec((None, D, D), lambda i: (1, 0, 0)),
                  pl.BlockSpec((L, D), fixed), vspec, pl.BlockSpec((1, 3 * D), fixed), pl.BlockSpec((1, 3 * D), fixed)],
        out_specs=(vspec, vspec, vspec), compiler_params=_cp(32),
    )(dkc_raw, dvc_m, w4, w4, ctx2, norm_g, mrow_c, b_ada)


def _rope_tables():
    nf = DH // 4
    inv = ROPE_THETA ** (-jnp.arange(nf, dtype=f32) / nf)
    ang_c = jnp.arange(GW, dtype=i32).astype(f32)[:, None] * inv
    ang_r = jnp.arange(ROWS, dtype=i32).astype(f32)[:, None] * inv
    zc, zr = jnp.zeros((GW, 2 * nf), f32), jnp.zeros((ROWS, 2 * nf), f32)
    ct_cos = jnp.tile(jnp.concatenate([zc, jnp.cos(ang_c), jnp.cos(ang_c)], axis=1), (1, H))
    ct_sin = jnp.tile(jnp.concatenate([zc, -jnp.sin(ang_c), jnp.sin(ang_c)], axis=1), (1, H))
    rt_cos = jnp.tile(jnp.concatenate([jnp.cos(ang_r), jnp.cos(ang_r), zr], axis=1), (1, H))
    rt_sin = jnp.tile(jnp.concatenate([-jnp.sin(ang_r), jnp.sin(ang_r), zr], axis=1), (1, H))
    rep8 = lambda t: jnp.broadcast_to(t[:, None, :], (ROWS, 8, DA)).reshape(ROWS * 8, DA)
    return ct_cos, rep8(rt_cos), ct_sin, rep8(rt_sin)


def _local_step(xx, ctx2, tgt, mrow, mrow_c, b_ada, norm_g, w4, w_out_full, q_norm_g, k_norm_g, rpb2, conv_w_full,
                conv_b):
    gq = jnp.tile(q_norm_g, (1, H))
    gk = jnp.tile(k_norm_g, (1, H))
    rope = _rope_tables()
    rpb_pad = jnp.pad(rpb2, ((0, 0), (0, 0), (0, 128 - N_DC)))

    h = _prenorm(xx, norm_g, mrow, b_ada, 256, "prenorm_x")
    hc = _prenorm(ctx2, norm_g, mrow_c, b_ada, L, "prenorm_ctx")
    p = _in_proj(h, w4)
    pc = _ctx_proj(hc, w4)
    qr, qp, kr, vh = _qk_prep(p, gq, gk, rope)
    kc, vc = _ctx_prep(pc, gk)
    btb = _bias_prep(rpb_pad)
    o = _attn_fwd(qr, qp, kr, vh, kc, vc, btb)
    attn_g = _attn_gate(o, p)
    conv_g = _conv_fwd(p, conv_w_full, conv_b)
    dy, dmix, g_w_out, dgate, loss_sum = _out_proj_loss(attn_g, conv_g, w_out_full, xx, tgt, mrow, b_ada)

    du, dbg, dcg, dzc, g_conv_w, g_conv_b = _conv_bwd(dmix, p, conv_w_full, conv_b)
    do, dza = _gate_bwd(dmix, p, o)
    dqr, dqp, dkr, dvh, dkc, dvc, dbtb = _attn_bwd(qr, qp, kr, vh, kc, vc, btb, do)
    g_rpb = _bias_bwd(dbtb)
    dq, dk, dv, g_gq, g_gk = _qk_bwd(dqr, dqp, dkr, dvh, p, gq, gk, rope)
    dkc_raw, dvc_m, g_gk_c = _ctx_bwd(dkc, dvc, pc, gk)
    g_w_in = [
        _grad_w_in_block(h, dq, dk, "grad_w_in_0", hc, dkc_raw, (DA, 2 * DA)),
        _grad_w_in_block(h, dv, dza, "grad_w_in_1", hc, dvc_m, (0, DA)),
        _grad_w_in_block(h, du, dbg, "grad_w_in_2"),
        _grad_w_in_block(h, dcg, dzc, "grad_w_in_3"),
    ]
    grad_x, dshift, dscale, dng = _dh_grad_x([dq, dk, dv, dza, du, dbg, dcg, dzc], w4, xx, dy, norm_g, mrow, b_ada)
    dshift_c, dscale_c, dng_c = _dhc_sums(dkc_raw, dvc_m, w4, ctx2, norm_g, mrow_c, b_ada)
    return dict(loss_sum=loss_sum, grad_x=grad_x, g_w_in=g_w_in, g_w_out=g_w_out, g_conv_w=g_conv_w,
                g_conv_b=g_conv_b, g_rpb=g_rpb, g_gq=g_gq, g_gk=g_gk, g_gk_c=g_gk_c, dshift=dshift, dscale=dscale,
                dgate=dgate, dng=dng, dshift_c=dshift_c, dscale_c=dscale_c, dng_c=dng_c)


def _pair_sum_w_in(gs, rs, cvec):
    tr = 128

    def body(c_ref, g0, g1, g2, g3, r0, r1, r2, r3, t32_ref, tb_ref):
        for q, (g_ref, r_ref) in enumerate(((g0, r0), (g1, r1), (g2, r2), (g3, r3))):
            t = g_ref[...] + r_ref[...]
            t32_ref[q] = t
            tb_ref[q] = t.astype(bf16)

    half = D // 2
    gs_spec = pl.BlockSpec((tr, D), lambda i, c: (c[0] * (half // tr) + i, 0))
    rs_spec = pl.BlockSpec((None, tr, D), lambda i, c: (0, i, 0))
    ospec = pl.BlockSpec((4, tr, D), lambda i, c: (0, i, 0))
    grid_spec = pltpu.PrefetchScalarGridSpec(num_scalar_prefetch=1, grid=(half // tr,), in_specs=[gs_spec] * 4 + [rs_spec] * 4,
                                             out_specs=(ospec, ospec))
    return pl.pallas_call(body, name="pair_sum_w_in", out_shape=(SDS((4, half, D), f32), SDS((4, half, D), bf16)),
                          grid_spec=grid_spec, compiler_params=_cp(40))(cvec, *gs, *rs)


def _pair_sum_w_out(g, r, cvec):
    hr = D // 8

    def body(c_ref, g0, g1, g2, g3, r_ref, t32_ref, tb_ref):
        for q, g_ref in enumerate((g0, g1, g2, g3)):
            t = g_ref[...] + r_ref[q]
            t32_ref[q] = t
            tb_ref[q] = t.astype(bf16)

    gspecs = [pl.BlockSpec((hr, D), lambda i, c, q=q: (2 * q + c[0], 0)) for q in range(4)]
    full = pl.BlockSpec((4, hr, D), lambda i, c: (0, 0, 0))
    grid_spec = pltpu.PrefetchScalarGridSpec(num_scalar_prefetch=1, grid=(1,), in_specs=gspecs + [full],
                                             out_specs=(full, full))
    return pl.pallas_call(body, name="pair_sum_w_out", out_shape=(SDS((4, hr, D), f32), SDS((4, hr, D), bf16)),
                          grid_spec=grid_spec)(cvec, g, g, g, g, r)


def _chip_sum(t32, r2, jvec, name):
    rows = t32.shape[1]
    tr = min(rows, 128)

    def body(j_ref, t_ref, r_ref, u_ref):
        u_ref[...] = ((t_ref[...] + r_ref[0].astype(f32)) + r_ref[1].astype(f32)) + r_ref[2].astype(f32)

    grid_spec = pltpu.PrefetchScalarGridSpec(
        num_scalar_prefetch=1, grid=(rows // tr,),
        in_specs=[pl.BlockSpec((None, tr, D), lambda i, j: (j[0], i, 0)), pl.BlockSpec((3, tr, D), lambda i, j: (0, i, 0))],
        out_specs=pl.BlockSpec((tr, D), lambda i, j: (i, 0)))
    return pl.pallas_call(body, name=name, out_shape=SDS((rows, D), f32), grid_spec=grid_spec)(jvec, t32, r2)


_PK = {}
_off = 0
for _name, _rows in (("dm", 24), ("dmc", 24), ("dng", 8), ("dng_c", 8), ("gq", 8), ("gk", 8), ("gk_c", 8),
                     ("rpb", H * N_DR), ("conv_b", 8), ("conv_w", 16), ("loss", 8)):
    _PK[_name] = (_off, _off + _rows)
    _off += _rows
PK_ROWS = _off
RS_B_ADA, RS_NORM_G, RS_GQ, RS_GK, RS_RPB, RS_CONV_B, RS_CONV_W, RS_DMC, RS_LOSS, RS_ROWS = (
    0, 24, 32, 40, 48, 168, 176, 192, 216, 224)


def _small_reduce(gathered):
    def body(g_ref, o_ref):
        tot = g_ref[0]
        for b in range(1, 8):
            tot = tot + g_ref[b]

        def rows(name):
            a, z = _PK[name]
            return tot[a:z]

        o_ref[RS_B_ADA:RS_B_ADA + 24] = rows("dm") + rows("dmc")
        o_ref[RS_NORM_G:RS_NORM_G + 8] = rows("dng") + rows("dng_c")
        gq = jnp.broadcast_to(jnp.sum(rows("gq"), axis=0, keepdims=True), (8, 128))
        gk = jnp.broadcast_to(jnp.sum(rows("gk") + rows("gk_c"), axis=0, keepdims=True), (8, 128))
        o_ref[RS_GQ:RS_GQ + 8] = gq + pltpu.roll(gq, DH, 1)
        o_ref[RS_GK:RS_GK + 8] = gk + pltpu.roll(gk, DH, 1)
        o_ref[RS_RPB:RS_RPB + H * N_DR] = rows("rpb")
        o_ref[RS_CONV_B:RS_CONV_B + 8] = rows("conv_b")
        o_ref[RS_CONV_W:RS_CONV_W + 16] = rows("conv_w")
        o_ref[RS_DMC:RS_DMC + 24] = rows("dmc")
        o_ref[RS_LOSS:RS_LOSS + 8] = rows("loss")

    return pl.pallas_call(body, name="small_reduce", out_shape=SDS((RS_ROWS, 128), f32), in_specs=[VMEM_SPEC],
                          out_specs=VMEM_SPEC)(gathered)


def _w_ada_grad(sc16, dm16s, w_ada_shard):
    def body(sc_ref, dm_ref, w_ref, g_ref, part_ref):
        dm = dm_ref[...]
        g_ref[...] = lax.dot_general(sc_ref[...], dm, (((0,), (0,)), ((), ())), precision=HIGHEST,
                                     preferred_element_type=f32)
        part_ref[...] = lax.dot_general(dm[8:16], w_ref[...], (((1,), (1,)), ((), ())), precision=HIGHEST,
                                        preferred_element_type=f32)

    ncol = w_ada_shard.shape[1]
    return pl.pallas_call(body, name="w_ada_grad", out_shape=(SDS((D, ncol), f32), SDS((8, D), f32)),
                          in_specs=[VMEM_SPEC] * 3, out_specs=(VMEM_SPEC, VMEM_SPEC), compiler_params=_cp(40),
                          )(sc16, dm16s, w_ada_shard)


def _c_ctx_grad(parts4, c_ctx_row):
    def body(p_ref, c_ref, o_ref):
        tot = ((p_ref[0] + p_ref[1]) + p_ref[2]) + p_ref[3]
        o_ref[...] = tot[0:1] * _dsilu(c_ref[...])

    return pl.pallas_call(body, name="c_ctx_grad", out_shape=SDS((1, D), f32), in_specs=[VMEM_SPEC, VMEM_SPEC],
                          out_specs=VMEM_SPEC)(parts4, c_ctx_row)


def _adamw(w, g, m, v, name):
    rows, cols = w.shape
    tr = 256 if rows % 256 == 0 else rows

    def body(w_ref, g_ref, m_ref, v_ref, d_ref, m2_ref, v2_ref):
        gv = g_ref[...]
        m2 = ADAM_B1 * m_ref[...] + (1.0 - ADAM_B1) * gv
        v2 = ADAM_B2 * v_ref[...] + (1.0 - ADAM_B2) * jnp.square(gv)
        m_hat = m2 / (1.0 - ADAM_B1 ** ADAM_STEP)
        v_hat = v2 / (1.0 - ADAM_B2 ** ADAM_STEP)
        d_ref[...] = -ADAM_LR * (m_hat / (jnp.sqrt(v_hat) + ADAM_EPS) + ADAM_WD * w_ref[...])
        m2_ref[...] = m2
        v2_ref[...] = v2

    spec = pl.BlockSpec((tr, cols), lambda i: (i, 0))
    shp = SDS((rows, cols), f32)
    return pl.pallas_call(body, name=name, out_shape=(shp, shp, shp), grid=(rows // tr,), in_specs=[spec] * 4,
                          out_specs=(spec, spec, spec))(w, g, m, v)


def _adamw_halves(w, g_mine, g_other, m, v, cvec, name):
    rows, cols = w.shape
    half = rows // 2
    tr = min(256, half)
    per_half = half // tr

    def body(c_ref, w_ref, ga_ref, gb_ref, m_ref, v_ref, g_ref, d_ref, m2_ref, v2_ref):
        in_my_half = (pl.program_id(0) // per_half) == c_ref[0]
        gv = jnp.where(in_my_half, ga_ref[...], gb_ref[...])
        g_ref[...] = gv
        m2 = ADAM_B1 * m_ref[...] + (1.0 - ADAM_B1) * gv
        v2 = ADAM_B2 * v_ref[...] + (1.0 - ADAM_B2) * jnp.square(gv)
        m_hat = m2 / (1.0 - ADAM_B1 ** ADAM_STEP)
        v_hat = v2 / (1.0 - ADAM_B2 ** ADAM_STEP)
        d_ref[...] = -ADAM_LR * (m_hat / (jnp.sqrt(v_hat) + ADAM_EPS) + ADAM_WD * w_ref[...])
        m2_ref[...] = m2
        v2_ref[...] = v2

    full = pl.BlockSpec((tr, cols), lambda i, c: (i, 0))
    part = pl.BlockSpec((tr, cols), lambda i, c: (i % per_half, 0))
    shp = SDS((rows, cols), f32)
    grid_spec = pltpu.PrefetchScalarGridSpec(num_scalar_prefetch=1, grid=(rows // tr,),
                                             in_specs=[full, part, part, full, full], out_specs=(full,) * 4)
    return pl.pallas_call(body, name=name, out_shape=(shp,) * 4, grid_spec=grid_spec)(cvec, w, g_mine, g_other, m, v)


def _rows128(a):
    return a.reshape(-1, 128)


def _pad_lanes(a):
    a2 = a.reshape(-1, a.shape[-1])
    return jnp.pad(a2, ((0, 0), (0, 128 - a2.shape[1])))


def kernel(x, c, ctx, c_ctx, w_ada, b_ada, norm_g, w_in, q_norm_g, k_norm_g, rpb, conv_w, conv_b, w_out, loss_target, m_c_ctx, m_w_ada, m_b_ada, m_norm_g, m_w_in, m_q_norm_g, m_k_norm_g, m_rpb, m_conv_w, m_conv_b, m_w_out, v_c_ctx, v_w_ada, v_b_ada, v_norm_g, v_w_in, v_q_norm_g, v_k_norm_g, v_rpb, v_conv_w, v_conv_b, v_w_out):
    xi, yi, ci = lax.axis_index("x"), lax.axis_index("y"), lax.axis_index("c")
    dev = 4 * xi + 2 * yi + ci
    chip = 2 * xi + yi
    cvec = jnp.reshape(ci, (1,)).astype(i32)
    jvec = jnp.reshape(chip, (1,)).astype(i32)
    w_ada_s = w_ada[0]
    ncol = w_ada_s.shape[1]

    c8 = _all_gather8(c.reshape(8, 128), "gather_c").reshape(8, D)
    cc = jnp.concatenate([c8, c_ctx.reshape(1, D), jnp.zeros((7, D), f32)], axis=0)
    m_shard, sc16 = _adaln_shard(cc, w_ada_s)

    conv_w_pad = jnp.pad(conv_w[0], ((0, 5), (0, 0)))
    m4, cw4, w4, wo4 = _chip_gather([m_shard, conv_w_pad], [_cast_to_slot(w_in[0], jvec, "cast_w_in"),
                                                            _cast_to_slot(w_out[0], jvec, "cast_w_out")],
                                    "gather_weights")
    m_full = jnp.transpose(m4, (1, 0, 2)).reshape(16, 4 * ncol)
    mrow = lax.dynamic_slice(m_full, (dev, 0), (1, 3 * D))
    mrow_c = m_full[8:9]
    conv_w_full = jnp.transpose(cw4[:, 0:3, :], (1, 0, 2)).reshape(3, DC)
    w_out_full = wo4.reshape(D, D)

    r = _local_step(x[0], ctx[0], loss_target[0], mrow, mrow_c, b_ada, norm_g, w4, w_out_full, q_norm_g, k_norm_g,
                    rpb[0], conv_w_full, conv_b)

    ex = _sibling_exchange(r["g_w_in"] + [r["g_w_out"]], [1, 1, 1, 1, 4], "grad_pair_exchange")
    t32, tb = _pair_sum_w_in(r["g_w_in"], ex[0:4], cvec)
    to32, tob = _pair_sum_w_out(r["g_w_out"], ex[4], cvec)
    r2, ro2 = _chip_scatter([tb, tob], "grad_chip_scatter")
    u_in = _chip_sum(t32, r2, jvec, "chip_sum_w_in")
    u_out = _chip_sum(to32, ro2, jvec, "chip_sum_w_out")
    o_in, o_out = _sibling_send([u_in, u_out], "grad_pair_send")

    dm = jnp.concatenate([r["dshift"], r["dscale"], r["dgate"]], axis=1)
    dmc = jnp.concatenate([r["dshift_c"], r["dscale_c"], jnp.zeros((1, D), f32)], axis=1)
    pack_parts = [_rows128(dm), _rows128(dmc), _rows128(r["dng"]), _rows128(r["dng_c"]), _rows128(r["g_gq"]),
                  _rows128(r["g_gk"]), _rows128(r["g_gk_c"]), r["g_rpb"].reshape(H * N_DR, 128),
                  _rows128(r["g_conv_b"]), _rows128(r["g_conv_w"][0:3]), jnp.pad(r["loss_sum"], ((0, 0), (0, 127)))]
    pack = jnp.concatenate([jnp.pad(p, ((0, -p.shape[0] % 8), (0, 0))) for p in pack_parts], axis=0)
    gathered = _all_gather8(pack, "gather_small")
    red = _small_reduce(gathered)
    g_b_ada = red[RS_B_ADA:RS_B_ADA + 24].reshape(1, 3 * D)
    g_norm_g = red[RS_NORM_G:RS_NORM_G + 8].reshape(1, D)
    g_q = red[RS_GQ:RS_GQ + 1, 0:DH]
    g_k = red[RS_GK:RS_GK + 1, 0:DH]
    g_rpb = red[RS_RPB:RS_RPB + H * N_DR]
    g_conv_b = red[RS_CONV_B:RS_CONV_B + 4].reshape(1, DC)
    g_conv_w_full = red[RS_CONV_W:RS_CONV_W + 12].reshape(3, DC)
    assert pack.shape[0] == PK_ROWS
    g_conv_w_s = lax.dynamic_slice(g_conv_w_full, (0, chip * 128), (3, 128))
    dmc_tot = red[RS_DMC:RS_DMC + 24].reshape(1, 3 * D)
    loss = red[RS_LOSS, 0] * (0.5 / D)

    a0, a1 = _PK["dm"]
    dm8 = gathered[:, a0:a1, :].reshape(8, 3 * D)
    dm16 = jnp.concatenate([dm8, dmc_tot, jnp.zeros((7, 3 * D), f32)], axis=0)
    dm16s = lax.dynamic_slice(dm16, (0, chip * ncol), (16, ncol))
    g_w_ada_s, cpart = _w_ada_grad(sc16, dm16s, w_ada_s)
    (cparts4,) = _chip_gather([cpart], [], "gather_c_ctx_parts")
    g_c_ctx = _c_ctx_grad(cparts4, c_ctx.reshape(1, D))

    g_w_in_s, d_w_in, nm_w_in, nv_w_in = _adamw_halves(w_in[0], u_in, o_in, m_w_in[0], v_w_in[0], cvec, "adamw_w_in")
    d_w_ada, nm_w_ada, nv_w_ada = _adamw(w_ada_s, g_w_ada_s, m_w_ada[0], v_w_ada[0], "adamw_w_ada")
    g_w_out_s, d_w_out, nm_w_out, nv_w_out = _adamw_halves(w_out[0], u_out, o_out, m_w_out[0], v_w_out[0], cvec,
                                                           "adamw_w_out")

    def small_pack(a_c_ctx, a_b_ada, a_norm_g, a_q, a_k, a_rpb, a_conv_w, a_conv_b):
        parts = [_rows128(a_c_ctx), _rows128(a_b_ada), _rows128(a_norm_g), _pad_lanes(a_q), _pad_lanes(a_k),
                 a_rpb, _rows128(a_conv_w), _rows128(a_conv_b)]
        return jnp.concatenate([jnp.pad(p, ((0, -p.shape[0] % 8), (0, 0))) for p in parts], axis=0)

    w_pk = small_pack(c_ctx, b_ada, norm_g, q_norm_g, k_norm_g, _pad_lanes(rpb[0]), conv_w[0], conv_b)
    g_pk = small_pack(g_c_ctx, g_b_ada, g_norm_g, g_q, g_k, g_rpb, g_conv_w_s, g_conv_b)
    m_pk = small_pack(m_c_ctx, m_b_ada, m_norm_g, m_q_norm_g, m_k_norm_g, _pad_lanes(m_rpb[0]), m_conv_w[0], m_conv_b)
    v_pk = small_pack(v_c_ctx, v_b_ada, v_norm_g, v_q_norm_g, v_k_norm_g, _pad_lanes(v_rpb[0]), v_conv_w[0], v_conv_b)
    d_pk, nm_pk, nv_pk = _adamw(w_pk, g_pk, m_pk, v_pk, "adamw_small")

    def small_unpack(pk):
        o = 0
        out = []
        for rows, fn in ((8, lambda a: a.reshape(D)), (24, lambda a: a.reshape(1, 3 * D)), (8, lambda a: a.reshape(1, D)),
                         (1, lambda a: a[:, 0:DH]), (1, lambda a: a[:, 0:DH]),
                         (H * N_DR, lambda a: a[:, 0:N_DC].reshape(1, H, N_DR, N_DC)),
                         (3, lambda a: a.reshape(1, 3, 128)), (4, lambda a: a.reshape(1, DC))):
            out.append(fn(pk[o:o + rows]))
            o += rows + (-rows % 8)
        return out

    def ordered(small, big_w_ada, big_w_in, big_w_out):
        s_c_ctx, s_b_ada, s_norm_g, s_q, s_k, s_rpb, s_conv_w, s_conv_b = small
        return [s_c_ctx, big_w_ada[None], s_b_ada, s_norm_g, big_w_in[None], s_q, s_k, s_rpb, s_conv_w, s_conv_b,
                big_w_out[None]]

    grads = ordered(small_unpack(g_pk), g_w_ada_s, g_w_in_s, g_w_out_s)
    deltas = ordered(small_unpack(d_pk), d_w_ada, d_w_in, d_w_out)
    new_m = ordered(small_unpack(nm_pk), nm_w_ada, nm_w_in, nm_w_out)
    new_v = ordered(small_unpack(nv_pk), nv_w_ada, nv_w_in, nv_w_out)
    return (loss, r["grad_x"][None], *grads, *deltas, *new_m, *new_v)
```

```python
import functools

import jax
import jax.numpy as jnp
from jax import lax
from jax.experimental import pallas as pl
from jax.experimental.pallas import tpu as pltpu

f32, bf16, i32 = jnp.float32, jnp.bfloat16, jnp.int32
MESH = pl.DeviceIdType.MESH
HIGHEST = lax.Precision.HIGHEST

D = 1024
S = 2048
L = 256
GW = 64
ROWS = S // GW
H = 8
DH = 64
DA = H * DH
DC = 512
WIN_H, WIN_W = 8, 16
N_DR, N_DC = 2 * WIN_H - 1, 2 * WIN_W - 1
RMS_EPS = 1e-6
ROPE_THETA = 10000.0
QK_SCALE = DH ** -0.5
NEG = -1e30

QB = 128
NQB = S // QB
KR = 9
KB = KR * GW
TILE_GEOM = ((0, 0), (2, 0), (4, 0), (28, 23), (30, 23))
NT = len(TILE_GEOM)

ADAM_LR, ADAM_B1, ADAM_B2, ADAM_EPS, ADAM_WD, ADAM_STEP = 0.001, 0.9, 0.999, 1e-08, 0.01, 10

VMEM_SPEC = pl.BlockSpec(memory_space=pltpu.VMEM)
ANY_SPEC = pl.BlockSpec(memory_space=pl.ANY)
SMEM_SPEC = pl.BlockSpec(memory_space=pltpu.SMEM)
SDS = jax.ShapeDtypeStruct


def _cp(vmem_mb=None, **kw):
    if vmem_mb is not None:
        kw["vmem_limit_bytes"] = vmem_mb << 20
    return pltpu.CompilerParams(**kw)


def _silu(z):
    return z * jax.nn.sigmoid(z)


def _dsilu(z):
    sg = jax.nn.sigmoid(z)
    return sg * (1.0 + z * (1.0 - sg))


def _row_start(i):
    return min(max(i - WIN_H // 2, 0), ROWS - WIN_H)


def _my_pos():
    return lax.axis_index("x"), lax.axis_index("y"), lax.axis_index("c")


def _flip(v, bit):
    return 1 - v if bit else v


def _all_gather8(xin, name):
    R, N = xin.shape

    def body(x_ref, o_ref, ssem, rsem, lsem):
        x, y, c = _my_pos()
        me = 4 * x + 2 * y + c
        own = pltpu.make_async_copy(x_ref, o_ref.at[me], lsem)
        own.start()
        sends = []
        for k in range(1, 8):
            tgt = (_flip(x, (k >> 2) & 1), _flip(y, (k >> 1) & 1), _flip(c, k & 1))
            cp = pltpu.make_async_remote_copy(src_ref=x_ref, dst_ref=o_ref.at[me], send_sem=ssem.at[k - 1],
                                              recv_sem=rsem.at[k - 1], device_id=tgt, device_id_type=MESH)
            cp.start()
            sends.append(cp)
        for k in range(1, 8):
            tgt = (_flip(x, (k >> 2) & 1), _flip(y, (k >> 1) & 1), _flip(c, k & 1))
            peer = 4 * tgt[0] + 2 * tgt[1] + tgt[2]
            pltpu.make_async_remote_copy(src_ref=x_ref, dst_ref=o_ref.at[peer], send_sem=ssem.at[k - 1],
                                         recv_sem=rsem.at[k - 1], device_id=tgt, device_id_type=MESH).wait_recv()
        for cp in sends:
            cp.wait_send()
        own.wait()

    return pl.pallas_call(
        body, name=name, out_shape=SDS((8, R, N), xin.dtype), in_specs=[VMEM_SPEC], out_specs=VMEM_SPEC,
        scratch_shapes=[pltpu.SemaphoreType.DMA((7,)), pltpu.SemaphoreType.DMA((7,)), pltpu.SemaphoreType.DMA],
    )(xin)


def _chip_gather(smalls, bigs, name):
    ns, nb = len(smalls), len(bigs)
    n_sem = 3 * ns + 6 * nb

    def body(*refs):
        s_in, b_in = refs[:ns], refs[ns:ns + nb]
        s_out, b_out = refs[ns + nb:2 * ns + nb], refs[2 * ns + nb:2 * (ns + nb)]
        ssem, rsem, lsem = refs[2 * (ns + nb):]
        x, y, c = _my_pos()
        j = 2 * x + y
        sib = (x, y, 1 - c)
        chips = []
        for k in range(1, 4):
            px, py = _flip(x, (k >> 1) & 1), _flip(y, k & 1)
            chips.append(((px, py, c), 2 * px + py))
        local = [pltpu.make_async_copy(s_in[a], s_out[a].at[j], lsem.at[a]) for a in range(ns)]
        for cp in local:
            cp.start()
        sends = []
        sem = 0
        for a in range(ns):
            for k in range(3):
                cp = pltpu.make_async_remote_copy(src_ref=s_in[a], dst_ref=s_out[a].at[j], send_sem=ssem.at[sem + k],
                                                  recv_sem=rsem.at[sem + k], device_id=chips[k][0], device_id_type=MESH)
                cp.start()
                sends.append(cp)
            sem += 3
        big_first = sem
        for a in range(nb):
            half = b_in[a].shape[1] // 2
            mine = pl.ds(pl.multiple_of(c * half, 8), half)
            for k in range(3):
                cp = pltpu.make_async_remote_copy(src_ref=b_in[a].at[j, mine], dst_ref=b_out[a].at[j, mine],
                                                  send_sem=ssem.at[sem + k], recv_sem=rsem.at[sem + k],
                                                  device_id=chips[k][0], device_id_type=MESH)
                cp.start()
                sends.append(cp)
            sem += 6
        sem = big_first
        for a in range(nb):
            half = b_in[a].shape[1] // 2
            mine = pl.ds(pl.multiple_of(c * half, 8), half)
            for k in range(3):
                pj = chips[k][1]
                pltpu.make_async_remote_copy(src_ref=b_in[a].at[j, mine], dst_ref=b_out[a].at[pj, mine],
                                             send_sem=ssem.at[sem + k], recv_sem=rsem.at[sem + k],
                                             device_id=chips[k][0], device_id_type=MESH).wait_recv()
                cp = pltpu.make_async_remote_copy(src_ref=b_out[a].at[pj, mine], dst_ref=b_out[a].at[pj, mine],
                                                  send_sem=ssem.at[sem + 3 + k], recv_sem=rsem.at[sem + 3 + k],
                                                  device_id=sib, device_id_type=MESH)
                cp.start()
                sends.append(cp)
            sem += 6
        sem = 0
        for a in range(ns):
            for k in range(3):
                pj = chips[k][1]
                pltpu.make_async_remote_copy(src_ref=s_in[a], dst_ref=s_out[a].at[pj], send_sem=ssem.at[sem + k],
                                             recv_sem=rsem.at[sem + k], device_id=chips[k][0], device_id_type=MESH).wait_recv()
            sem += 3
        for a in range(nb):
            half = b_in[a].shape[1] // 2
            other = pl.ds(pl.multiple_of((1 - c) * half, 8), half)
            for k in range(3):
                pj = chips[k][1]
                pltpu.make_async_remote_copy(src_ref=b_out[a].at[pj, other], dst_ref=b_out[a].at[pj, other],
                                             send_sem=ssem.at[sem + 3 + k], recv_sem=rsem.at[sem + 3 + k],
                                             device_id=sib, device_id_type=MESH).wait_recv()
            sem += 6
        for cp in sends:
            cp.wait_send()
        for cp in local:
            cp.wait()

    out_shape = [SDS((4,) + a.shape, a.dtype) for a in smalls] + [SDS(a.shape, a.dtype) for a in bigs]
    return pl.pallas_call(
        body, name=name, out_shape=out_shape,
        in_specs=[VMEM_SPEC] * ns + [ANY_SPEC] * nb, out_specs=[VMEM_SPEC] * ns + [ANY_SPEC] * nb,
        input_output_aliases={ns + a: ns + a for a in range(nb)},
        scratch_shapes=[pltpu.SemaphoreType.DMA((n_sem,)), pltpu.SemaphoreType.DMA((n_sem,)),
                        pltpu.SemaphoreType.DMA((max(ns, 1),))],
    )(*smalls, *bigs)


HBM_SPEC = pl.BlockSpec(memory_space=pltpu.HBM)
SEM_SPEC = pl.BlockSpec(memory_space=pltpu.SEMAPHORE)
DATAFLOW = pltpu.SideEffectType.DATAFLOW_SIDE_EFFECTING


def _peer_chips(x, y, c):
    out = []
    for k in range(1, 4):
        px, py = _flip(x, (k >> 1) & 1), _flip(y, k & 1)
        out.append(((px, py, c), 2 * px + py))
    return out


def _half_copies(srcs, dsts, ssem, rsem):
    x, y, c = _my_pos()
    j = 2 * x + y
    pairs = []
    for a in range(len(srcs)):
        half = srcs[a].shape[1] // 2
        mine = pl.ds(pl.multiple_of(c * half, 8), half)
        for k, (dev, pj) in enumerate(_peer_chips(x, y, c)):
            sem = 3 * a + k
            send = pltpu.make_async_remote_copy(src_ref=srcs[a].at[j, mine], dst_ref=dsts[a].at[j, mine],
                                                send_sem=ssem.at[sem], recv_sem=rsem.at[sem], device_id=dev,
                                                device_id_type=MESH)
            arrive = pltpu.make_async_remote_copy(src_ref=srcs[a].at[j, mine], dst_ref=dsts[a].at[pj, mine],
                                                  send_sem=ssem.at[sem], recv_sem=rsem.at[sem], device_id=dev,
                                                  device_id_type=MESH)
            pairs.append((send, arrive))
    return pairs


def _halves_start(bigs, name):
    nb = len(bigs)

    def body(*refs):
        b_in = refs[:nb]
        ssem, rsem = refs[nb], refs[nb + 1]
        b_out = refs[nb + 2:2 * nb + 2]
        token = refs[2 * nb + 2]
        for send, _ in _half_copies(b_in, b_out, ssem, rsem):
            send.start()
        token[...] = jnp.zeros_like(token)

    out_shape = (pltpu.SemaphoreType.DMA((3 * nb,)), pltpu.SemaphoreType.DMA((3 * nb,)),
                 *[pltpu.HBM(b.shape, b.dtype) for b in bigs], SDS((8, 128), f32))
    return pl.pallas_call(
        body, name=name, out_shape=out_shape, in_specs=[HBM_SPEC] * nb,
        out_specs=(SEM_SPEC, SEM_SPEC, *[HBM_SPEC] * nb, VMEM_SPEC),
        input_output_aliases={a: 2 + a for a in range(nb)}, compiler_params=_cp(has_side_effects=DATAFLOW),
    )(*[pltpu.with_memory_space_constraint(b, pltpu.HBM) for b in bigs])


def _halves_wait(ssem, rsem, bigs, after, name):
    nb = len(bigs)

    def body(*refs):
        b_in = refs[:nb]
        ssem_ref, rsem_ref = refs[nb], refs[nb + 1]
        for send, arrive in _half_copies(b_in, b_in, ssem_ref, rsem_ref):
            send.wait_send()
            arrive.wait_recv()

    return pl.pallas_call(
        body, name=name, out_shape=tuple(pltpu.HBM(b.shape, b.dtype) for b in bigs),
        in_specs=[HBM_SPEC] * nb + [SEM_SPEC, SEM_SPEC, ANY_SPEC], out_specs=tuple([HBM_SPEC] * nb),
        input_output_aliases={a: a for a in range(nb)}, compiler_params=_cp(has_side_effects=DATAFLOW),
    )(*bigs, ssem, rsem, after)


def _halves_forward(bigs, name):
    nb = len(bigs)

    def body(*refs):
        b_in, b_out = refs[:nb], refs[nb:2 * nb]
        ssem, rsem = refs[2 * nb:]
        x, y, c = _my_pos()
        sib = (x, y, 1 - c)
        sends = []
        for a in range(nb):
            half = b_in[a].shape[1] // 2
            mine = pl.ds(pl.multiple_of(c * half, 8), half)
            for k, (_, pj) in enumerate(_peer_chips(x, y, c)):
                cp = pltpu.make_async_remote_copy(src_ref=b_in[a].at[pj, mine], dst_ref=b_out[a].at[pj, mine],
                                                  send_sem=ssem.at[3 * a + k], recv_sem=rsem.at[3 * a + k],
                                                  device_id=sib, device_id_type=MESH)
                cp.start()
                sends.append(cp)
        for a in range(nb):
            half = b_in[a].shape[1] // 2
            other = pl.ds(pl.multiple_of((1 - c) * half, 8), half)
            for k, (_, pj) in enumerate(_peer_chips(x, y, c)):
                pltpu.make_async_remote_copy(src_ref=b_in[a].at[pj, other], dst_ref=b_out[a].at[pj, other],
                                             send_sem=ssem.at[3 * a + k], recv_sem=rsem.at[3 * a + k],
                                             device_id=sib, device_id_type=MESH).wait_recv()
        for cp in sends:
            cp.wait_send()

    return pl.pallas_call(
        body, name=name, out_shape=[SDS(b.shape, b.dtype) for b in bigs], in_specs=[ANY_SPEC] * nb,
        out_specs=[ANY_SPEC] * nb, input_output_aliases={a: a for a in range(nb)},
        scratch_shapes=[pltpu.SemaphoreType.DMA((3 * nb,)), pltpu.SemaphoreType.DMA((3 * nb,))],
    )(*bigs)


def _cast_to_slot(w, jvec, name):
    rows, cols = w.shape
    tr = 256

    def body(j_ref, w_ref, o_ref):
        o_ref[...] = w_ref[...].astype(bf16)

    grid_spec = pltpu.PrefetchScalarGridSpec(
        num_scalar_prefetch=1, grid=(rows // tr,), in_specs=[pl.BlockSpec((tr, cols), lambda i, j: (i, 0))],
        out_specs=pl.BlockSpec((None, tr, cols), lambda i, j: (j[0], i, 0)))
    return pl.pallas_call(body, name=name, out_shape=SDS((4, rows, cols), bf16), grid_spec=grid_spec)(jvec, w)


def _exchange_copies(n_blocks):
    def make(srcs, lands, ssem, rsem):
        x, y, c = _my_pos()
        cps = []
        sem = 0
        for a in range(len(srcs)):
            rb = srcs[a].shape[0] // n_blocks[a]
            half = rb // 2
            for jb in range(n_blocks[a]):
                theirs = pl.ds(pl.multiple_of(jb * rb + (1 - c) * half, 8), half)
                cps.append(pltpu.make_async_remote_copy(src_ref=srcs[a].at[theirs], dst_ref=lands[a].at[jb],
                                                        send_sem=ssem.at[sem], recv_sem=rsem.at[sem],
                                                        device_id=(x, y, 1 - c), device_id_type=MESH))
                sem += 1
        return cps
    return make


def _scatter_copies(srcs, lands, ssem, rsem):
    x, y, c = _my_pos()
    cps = []
    for a in range(len(srcs)):
        for k, (dev, pj) in enumerate(_peer_chips(x, y, c)):
            cps.append(pltpu.make_async_remote_copy(src_ref=srcs[a].at[pj], dst_ref=lands[a].at[k],
                                                    send_sem=ssem.at[3 * a + k], recv_sem=rsem.at[3 * a + k],
                                                    device_id=dev, device_id_type=MESH))
    return cps


def _split_start(srcs, land_shapes, n_cp, make, name):
    ns, nl = len(srcs), len(land_shapes)

    def body(*refs):
        s_in = refs[:ns]
        ssem, rsem = refs[ns + nl], refs[ns + nl + 1]
        l_out = refs[2 * ns + nl + 2:2 * ns + 2 * nl + 2]
        token = refs[2 * ns + 2 * nl + 2]
        for cp in make(s_in, l_out, ssem, rsem):
            cp.start()
        token[...] = jnp.zeros_like(token)

    lands = [pltpu.with_memory_space_constraint(lax.empty(sh.shape, sh.dtype), pltpu.HBM) for sh in land_shapes]
    out_shape = (pltpu.SemaphoreType.DMA((n_cp,)), pltpu.SemaphoreType.DMA((n_cp,)),
                 *[pltpu.HBM(b.shape, b.dtype) for b in srcs], *[pltpu.HBM(b.shape, b.dtype) for b in land_shapes],
                 SDS((8, 128), f32))
    return pl.pallas_call(
        body, name=name, out_shape=out_shape, in_specs=[HBM_SPEC] * (ns + nl),
        out_specs=(SEM_SPEC, SEM_SPEC, *[HBM_SPEC] * (ns + nl), VMEM_SPEC),
        input_output_aliases={i: 2 + i for i in range(ns + nl)}, compiler_params=_cp(has_side_effects=DATAFLOW),
    )(*[pltpu.with_memory_space_constraint(b, pltpu.HBM) for b in srcs], *lands)


def _split_wait(ssem, rsem, srcs, lands, after, make, name):
    ns, nl = len(srcs), len(lands)

    def body(*refs):
        s_in, l_in = refs[:ns], refs[ns:ns + nl]
        ssem_ref, rsem_ref = refs[ns + nl], refs[ns + nl + 1]
        for cp in make(s_in, l_in, ssem_ref, rsem_ref):
            cp.wait_send()
            cp.wait_recv()

    outs = pl.pallas_call(
        body, name=name, out_shape=tuple(pltpu.HBM(b.shape, b.dtype) for b in (*srcs, *lands)),
        in_specs=[HBM_SPEC] * (ns + nl) + [SEM_SPEC, SEM_SPEC, ANY_SPEC], out_specs=tuple([HBM_SPEC] * (ns + nl)),
        input_output_aliases={i: i for i in range(ns + nl)}, compiler_params=_cp(has_side_effects=DATAFLOW),
    )(*srcs, *lands, ssem, rsem, after)
    return list(outs[:ns]), list(outs[ns:])


def _sibling_send(halves, name):
    n = len(halves)

    def body(*refs):
        ins, outs = refs[:n], refs[n:2 * n]
        ssem, rsem = refs[2 * n:]
        x, y, c = _my_pos()
        cps = []
        for a in range(n):
            cp = pltpu.make_async_remote_copy(src_ref=ins[a], dst_ref=outs[a], send_sem=ssem.at[a],
                                              recv_sem=rsem.at[a], device_id=(x, y, 1 - c), device_id_type=MESH)
            cp.start()
            cps.append(cp)
        for cp in cps:
            cp.wait_recv()
        for cp in cps:
            cp.wait_send()

    out_shape = [SDS(h.shape, h.dtype) for h in halves]
    return pl.pallas_call(
        body, name=name, out_shape=out_shape, in_specs=[ANY_SPEC] * n, out_specs=[ANY_SPEC] * n,
        scratch_shapes=[pltpu.SemaphoreType.DMA((n,)), pltpu.SemaphoreType.DMA((n,))],
    )(*halves)


def _adaln_shard(cc, w_ada_shard):
    def body(c_ref, w_ref, m_ref, sc_ref):
        sc = _silu(c_ref[...])
        sc_ref[...] = sc
        m_ref[...] = jnp.dot(sc, w_ref[...], precision=HIGHEST, preferred_element_type=f32)

    return pl.pallas_call(
        body, name="adaln_shard", out_shape=(SDS((16, w_ada_shard.shape[1]), f32), SDS((16, D), f32)),
        in_specs=[VMEM_SPEC, VMEM_SPEC], out_specs=(VMEM_SPEC, VMEM_SPEC), compiler_params=_cp(32),
    )(cc, w_ada_shard)


def _prenorm(xx, norm_g, mrow, b_ada, tm, name):
    n = xx.shape[0]

    def body(x_ref, g_ref, m_ref, b_ref, h_ref):
        x = x_ref[...]
        shift = m_ref[:, 0:D] + b_ref[:, 0:D]
        scale = m_ref[:, D:2 * D] + b_ref[:, D:2 * D]
        r = lax.rsqrt(jnp.mean(x * x, axis=-1, keepdims=True) + RMS_EPS)
        y = (x * r) * g_ref[...]
        h_ref[...] = (y * (1.0 + scale) + shift).astype(bf16)

    row = lambda i: (i, 0)
    fixed = lambda i: (0, 0)
    return pl.pallas_call(
        body, name=name, out_shape=SDS((n, D), bf16), grid=(n // tm,),
        in_specs=[pl.BlockSpec((tm, D), row), pl.BlockSpec((1, D), fixed), pl.BlockSpec((1, 3 * D), fixed),
                  pl.BlockSpec((1, 3 * D), fixed)],
        out_specs=pl.BlockSpec((tm, D), row),
    )(xx, norm_g, mrow, b_ada)


def _in_proj(h, w4):
    tm = 512

    def body(h_ref, w_ref, p_ref):
        p_ref[...] = jnp.dot(h_ref[...], w_ref[...], preferred_element_type=f32)

    return pl.pallas_call(
        body, name="in_proj", out_shape=SDS((S, 4 * D), f32), grid=(4, S // tm),
        in_specs=[pl.BlockSpec((tm, D), lambda j, k: (k, 0)), pl.BlockSpec((None, D, D), lambda j, k: (j, 0, 0))],
        out_specs=pl.BlockSpec((tm, D), lambda j, k: (k, j)),
    )(h, w4)


def _ctx_proj(hc, w4):
    def body(h_ref, w0_ref, w1_ref, p_ref):
        hv = h_ref[...]
        p_ref[:, 0:DA] = jnp.dot(hv, w0_ref[:, DA:2 * DA], preferred_element_type=f32)
        p_ref[:, DA:2 * DA] = jnp.dot(hv, w1_ref[:, 0:DA], preferred_element_type=f32)

    return pl.pallas_call(
        body, name="ctx_proj", out_shape=SDS((L, 2 * DA), f32), grid=(1,),
        in_specs=[pl.BlockSpec((L, D), lambda i: (0, 0)), pl.BlockSpec((None, D, D), lambda i: (0, 0, 0)),
                  pl.BlockSpec((None, D, D), lambda i: (1, 0, 0))],
        out_specs=pl.BlockSpec((L, 2 * DA), lambda i: (0, 0)),
    )(hc, w4, w4)


def _head_ones():
    r = lax.broadcasted_iota(i32, (DA, DA), 0) // DH
    c = lax.broadcasted_iota(i32, (DA, DA), 1) // DH
    return (r == c).astype(f32)


def _head_sum(v, ones_bd):
    return jnp.dot(v, ones_bd, precision=HIGHEST, preferred_element_type=f32)


def _swap16(v):
    lane = lax.broadcasted_iota(i32, v.shape, 1)
    return jnp.where((lane & 31) < 16, pltpu.roll(v, DA - 16, 1), pltpu.roll(v, 16, 1))


def _rope_block(ct_ref, rt_ref, tm):
    rows = [jnp.tile(rt_ref[8 * j:8 * j + 8, :], (GW // 8, 1)) for j in range(tm // GW)]
    return jnp.tile(ct_ref[...], (tm // GW, 1)) + jnp.concatenate(rows, axis=0)


def _rope_specs(tm):
    col = pl.BlockSpec((GW, DA), lambda i: (0, 0))
    row = pl.BlockSpec((8 * tm // GW, DA), lambda i: (i, 0))
    return [col, row, col, row]


def _qk_prep(p, gq, gk, rope):
    tm = 256

    def body(qk_ref, v_ref, gq_ref, gk_ref, cc_ref, cr_ref, sc_ref, sr_ref, qr_ref, qp_ref, kr_ref, vh_ref):
        ones_bd = _head_ones()
        cs, sn = _rope_block(cc_ref, cr_ref, tm), _rope_block(sc_ref, sr_ref, tm)
        q = qk_ref[:, 0:DA]
        k = qk_ref[:, DA:2 * DA]
        yq = (q * lax.rsqrt(_head_sum(q * q, ones_bd) * (1.0 / DH) + RMS_EPS)) * gq_ref[...]
        yk = (k * lax.rsqrt(_head_sum(k * k, ones_bd) * (1.0 / DH) + RMS_EPS)) * gk_ref[...]
        qr = (yq * cs + _swap16(yq) * sn) * QK_SCALE
        qp = yq * QK_SCALE
        kr = yk * cs + _swap16(yk) * sn
        vv = v_ref[...]
        for hh in range(H):
            sl = slice(hh * DH, (hh + 1) * DH)
            qr_ref[hh] = qr[:, sl].astype(bf16)
            qp_ref[hh] = qp[:, sl].astype(bf16)
            kr_ref[hh] = kr[:, sl].astype(bf16)
            vh_ref[hh] = vv[:, sl].astype(bf16)

    hm = SDS((H, S, DH), bf16)
    hspec = pl.BlockSpec((H, tm, DH), lambda i: (0, i, 0))
    fixed = lambda i: (0, 0)
    return pl.pallas_call(
        body, name="qk_prep", out_shape=(hm, hm, hm, hm), grid=(S // tm,),
        in_specs=[pl.BlockSpec((tm, 2 * DA), lambda i: (i, 0)), pl.BlockSpec((tm, DA), lambda i: (i, 2)),
                  pl.BlockSpec((1, DA), fixed), pl.BlockSpec((1, DA), fixed)] + _rope_specs(tm),
        out_specs=(hspec, hspec, hspec, hspec),
    )(p, p, gq, gk, *rope)


def _ctx_prep(pc, gk):
    def body(p_ref, gk_ref, kc_ref, vc_ref):
        ones_bd = _head_ones()
        k = p_ref[:, 0:DA]
        yk = (k * lax.rsqrt(_head_sum(k * k, ones_bd) * (1.0 / DH) + RMS_EPS)) * gk_ref[...]
        vv = p_ref[:, DA:2 * DA]
        for hh in range(H):
            sl = slice(hh * DH, (hh + 1) * DH)
            kc_ref[hh] = yk[:, sl].astype(bf16)
            vc_ref[hh] = vv[:, sl].astype(bf16)

    hm = SDS((H, L, DH), bf16)
    return pl.pallas_call(
        body, name="ctx_prep", out_shape=(hm, hm), in_specs=[VMEM_SPEC, VMEM_SPEC], out_specs=(VMEM_SPEC, VMEM_SPEC),
    )(pc, gk)


def _tile_pieces():
    out = []
    for (i0, u0) in TILE_GEOM:
        rows = []
        for j in range(2):
            i = i0 + j
            rs = _row_start(i)
            rows.append([(u0 + u - i + WIN_H - 1) if rs <= u0 + u < rs + WIN_H else None for u in range(KR)])
        out.append(rows)
    return out


def _bias_prep(rpb_pad):
    pieces = _tile_pieces()

    def body(r_ref, o_ref):
        rp = r_ref[...]
        xs = jnp.broadcast_to(rp[:, None, :], (N_DR, GW, 128)).reshape(N_DR * GW, 128)
        row = lax.broadcasted_iota(i32, xs.shape, 0)
        lane = lax.broadcasted_iota(i32, xs.shape, 1)
        for b in range(6):
            xs = jnp.where(((row >> b) & 1) == 1, pltpu.roll(xs, 1 << b, 1), xs)
        xs = pltpu.roll(xs, 128 - (WIN_W - 1), 1)
        q = row & (GW - 1)
        c0 = jnp.clip(q - WIN_W // 2, 0, GW - WIN_W)
        xs = jnp.where((lane >= c0) & (lane < c0 + WIN_W), xs, NEG)
        neg = jnp.full((GW, GW), NEG, f32)
        for t in range(NT):
            for j in range(2):
                for u in range(KR):
                    dr = pieces[t][j][u]
                    piece = neg if dr is None else xs[dr * GW:(dr + 1) * GW, 0:GW]
                    o_ref[t, j * GW:(j + 1) * GW, u * GW:(u + 1) * GW] = piece

    return pl.pallas_call(
        body, name="bias_prep", out_shape=SDS((H, NT, QB, KB), f32), grid=(H,),
        in_specs=[pl.BlockSpec((None, N_DR, 128), lambda h: (h, 0, 0))],
        out_specs=pl.BlockSpec((None, NT, QB, KB), lambda h: (h, 0, 0, 0)),
    )(rpb_pad)


def _block_geom(b):
    qs = b * QB
    ks = min(max(2 * b - 4, 0), ROWS - KR) * GW
    t = b if b < 2 else (b - (NQB - NT) if b > NQB - 3 else 2)
    return qs, ks, t


def _tt(a, b):
    return lax.dot_general(a, b, (((1,), (1,)), ((), ())), preferred_element_type=f32)


def _tn(a, b):
    return lax.dot_general(a, b, (((0,), (0,)), ((), ())), preferred_element_type=f32)


def _softmax_block(qr, qp, k, kc, bias):
    s_lat = _tt(qr, k) + bias
    s_ctx = _tt(qp, kc)
    m = jnp.maximum(jnp.max(s_lat, axis=-1, keepdims=True), jnp.max(s_ctx, axis=-1, keepdims=True))
    e_lat = jnp.exp(s_lat - m)
    e_ctx = jnp.exp(s_ctx - m)
    inv = 1.0 / (jnp.sum(e_lat, axis=-1, keepdims=True) + jnp.sum(e_ctx, axis=-1, keepdims=True))
    return e_lat * inv, e_ctx * inv


def _attn_fwd(qr, qp, kr, vh, kc, vc, btb):
    def body(qr_ref, qp_ref, kr_ref, v_ref, kc_ref, vc_ref, bt_ref, o_ref):
        kcv, vcv = kc_ref[...], vc_ref[...]

        def blk(b, carry):
            qs, ks, t = _block_geom(b)
            p_lat, p_ctx = _softmax_block(qr_ref[pl.ds(qs, QB), :], qp_ref[pl.ds(qs, QB), :],
                                          kr_ref[pl.ds(ks, KB), :], kcv, bt_ref[t])
            o = jnp.dot(p_lat.astype(bf16), v_ref[pl.ds(ks, KB), :], preferred_element_type=f32)
            o_ref[pl.ds(qs, QB), :] = o + jnp.dot(p_ctx.astype(bf16), vcv, preferred_element_type=f32)
            return carry

        for b in range(NQB):
            blk(b, 0)

    sq = pl.BlockSpec((None, S, DH), lambda h: (h, 0, 0))
    sc = pl.BlockSpec((None, L, DH), lambda h: (h, 0, 0))
    return pl.pallas_call(
        body, name="attn_fwd", out_shape=SDS((H, S, DH), f32), grid=(H,),
        in_specs=[sq, sq, sq, sq, sc, sc, pl.BlockSpec((None, NT, QB, KB), lambda h: (h, 0, 0, 0))],
        out_specs=sq, compiler_params=_cp(40),
    )(qr, qp, kr, vh, kc, vc, btb)


def _attn_gate(o, p):
    tm = 256

    def body(o_ref, za_ref, a_ref):
        sz = _silu(za_ref[...])
        for hh in range(H):
            sl = slice(hh * DH, (hh + 1) * DH)
            a_ref[:, sl] = (o_ref[hh] * sz[:, sl]).astype(bf16)

    return pl.pallas_call(
        body, name="attn_gate", out_shape=SDS((S, DA), bf16), grid=(S // tm,),
        in_specs=[pl.BlockSpec((H, tm, DH), lambda i: (0, i, 0)), pl.BlockSpec((tm, DA), lambda i: (i, 3))],
        out_specs=pl.BlockSpec((tm, DA), lambda i: (i, 0)),
    )(o, p)


def _shift_rows(v, down):
    n = v.shape[0]
    row = lax.broadcasted_iota(i32, v.shape, 0)
    if down:
        return jnp.where(row == 0, 0.0, pltpu.roll(v, 1, 0))
    return jnp.where(row == n - 1, 0.0, pltpu.roll(v, n - 1, 0))


def _conv_specs():
    col = lambda off: pl.BlockSpec((S, 128), lambda i, off=off: (0, off + i))
    return [col(16), col(20), col(24), col(28), pl.BlockSpec((3, 128), lambda i: (0, i)),
            pl.BlockSpec((1, 128), lambda i: (0, i))]


def _conv_fwd(p, conv_w, conv_b):
    def body(u_ref, bg_ref, cg_ref, zc_ref, w_ref, b_ref, o_ref):
        cu = cg_ref[...] * u_ref[...]
        cv = b_ref[...] + _shift_rows(cu, True) * w_ref[0:1, :]
        cv = cv + cu * w_ref[1:2, :]
        cv = cv + _shift_rows(cu, False) * w_ref[2:3, :]
        o_ref[...] = ((bg_ref[...] * cv) * _silu(zc_ref[...])).astype(bf16)

    return pl.pallas_call(
        body, name="conv_fwd", out_shape=SDS((S, DC), bf16), grid=(DC // 128,),
        in_specs=_conv_specs(), out_specs=pl.BlockSpec((S, 128), lambda i: (0, i)), compiler_params=_cp(40),
    )(p, p, p, p, conv_w, conv_b)


def _out_proj_loss(attn_g, conv_g, w_out, xx, tgt, mrow, b_ada):
    tm = 256

    def body(a_ref, c_ref, w_ref, x_ref, t_ref, m_ref, b_ref, dy_ref, dmix_ref, gwo_ref, dgate_ref, loss_ref):
        k = pl.program_id(0)

        @pl.when(k == 0)
        def _():
            gwo_ref[...] = jnp.zeros_like(gwo_ref)
            dgate_ref[...] = jnp.zeros_like(dgate_ref)
            loss_ref[0, 0] = 0.0

        gate = m_ref[:, 2 * D:3 * D] + b_ref[:, 2 * D:3 * D]
        av, cv = a_ref[...], c_ref[...]
        mo = jnp.dot(av, w_ref[0:DA, :], preferred_element_type=f32)
        mo = mo + jnp.dot(cv, w_ref[DA:DA + DC, :], preferred_element_type=f32)
        y = x_ref[...] + gate * mo
        diff = y - t_ref[...]
        loss_ref[0, 0] += jnp.sum(diff * diff)
        dy = diff * (1.0 / D)
        dy_ref[...] = dy
        dgate_ref[...] += jnp.sum(dy * mo, axis=0, keepdims=True)
        dmo = (dy * gate).astype(bf16)
        dmix_ref[...] = _tt(dmo, w_ref[...])
        gwo_ref[0:DA, :] += _tn(av, dmo)
        gwo_ref[DA:DA + DC, :] += _tn(cv, dmo)

    row = lambda i: (i, 0)
    fixed = lambda i: (0, 0)
    return pl.pallas_call(
        body, name="out_proj_loss",
        out_shape=(SDS((S, D), f32), SDS((S, D), f32), SDS((D, D), f32), SDS((1, D), f32), SDS((1, 1), f32)),
        grid=(S // tm,),
        in_specs=[pl.BlockSpec((tm, DA), row), pl.BlockSpec((tm, DC), row), pl.BlockSpec((D, D), fixed),
                  pl.BlockSpec((tm, D), row), pl.BlockSpec((tm, D), row), pl.BlockSpec((1, 3 * D), fixed),
                  pl.BlockSpec((1, 3 * D), fixed)],
        out_specs=(pl.BlockSpec((tm, D), row), pl.BlockSpec((tm, D), row), pl.BlockSpec((D, D), fixed),
                   pl.BlockSpec((1, D), fixed), SMEM_SPEC),
        compiler_params=_cp(48, dimension_semantics=("arbitrary",)),
    )(attn_g, conv_g, w_out, xx, tgt, mrow, b_ada)


def _conv_bwd(dmix, p, conv_w, conv_b):
    def body(d_ref, u_ref, bg_ref, cg_ref, zc_ref, w_ref, b_ref, du_ref, dbg_ref, dcg_ref, dzc_ref, gw_ref, gb_ref):
        dconv = d_ref[...]
        u, bg, cg, zc = u_ref[...], bg_ref[...], cg_ref[...], zc_ref[...]
        w0, w1, w2 = w_ref[0:1, :], w_ref[1:2, :], w_ref[2:3, :]
        cu = cg * u
        cu_m, cu_p = _shift_rows(cu, True), _shift_rows(cu, False)
        cv = b_ref[...] + cu_m * w0
        cv = cv + cu * w1
        cv = cv + cu_p * w2
        sz = _silu(zc)
        dbg_ref[...] = ((dconv * sz) * cv).astype(bf16)
        dzc_ref[...] = ((dconv * (bg * cv)) * _dsilu(zc)).astype(bf16)
        dcv = (dconv * sz) * bg
        gb_ref[...] = jnp.sum(dcv, axis=0, keepdims=True)
        gw_ref[0:1, :] = jnp.sum(dcv * cu_m, axis=0, keepdims=True)
        gw_ref[1:2, :] = jnp.sum(dcv * cu, axis=0, keepdims=True)
        gw_ref[2:3, :] = jnp.sum(dcv * cu_p, axis=0, keepdims=True)
        gw_ref[3:8, :] = jnp.zeros((5, 128), f32)
        dcu = _shift_rows(dcv, False) * w0 + dcv * w1 + _shift_rows(dcv, True) * w2
        dcg_ref[...] = (dcu * u).astype(bf16)
        du_ref[...] = (dcu * cg).astype(bf16)

    piece = SDS((S, DC), bf16)
    ospec = pl.BlockSpec((S, 128), lambda i: (0, i))
    return pl.pallas_call(
        body, name="conv_bwd", out_shape=(piece, piece, piece, piece, SDS((8, DC), f32), SDS((1, DC), f32)),
        grid=(DC // 128,),
        in_specs=[pl.BlockSpec((S, 128), lambda i: (0, 4 + i))] + _conv_specs(),
        out_specs=(ospec, ospec, ospec, ospec, pl.BlockSpec((8, 128), lambda i: (0, i)),
                   pl.BlockSpec((1, 128), lambda i: (0, i))),
        compiler_params=_cp(48),
    )(dmix, p, p, p, p, conv_w, conv_b)


def _gate_bwd(dmix, p, o):
    tm = 256

    def body(d_ref, za_ref, o_ref, do_ref, dza_ref):
        za = za_ref[...]
        dattn = d_ref[...]
        a = dattn * _silu(za)
        bb = dattn * _dsilu(za)
        for hh in range(H):
            sl = slice(hh * DH, (hh + 1) * DH)
            do_ref[hh] = a[:, sl].astype(bf16)
            dza_ref[:, sl] = (bb[:, sl] * o_ref[hh]).astype(bf16)

    hspec = pl.BlockSpec((H, tm, DH), lambda i: (0, i, 0))
    return pl.pallas_call(
        body, name="gate_bwd", out_shape=(SDS((H, S, DH), bf16), SDS((S, DA), bf16)), grid=(S // tm,),
        in_specs=[pl.BlockSpec((tm, DA), lambda i: (i, 0)), pl.BlockSpec((tm, DA), lambda i: (i, 3)), hspec],
        out_specs=(hspec, pl.BlockSpec((tm, DA), lambda i: (i, 0))),
    )(dmix, p, o)


def _attn_bwd(qr, qp, kr, vh, kc, vc, btb, do):
    def body(qr_ref, qp_ref, kr_ref, v_ref, kc_ref, vc_ref, bt_ref, do_ref,
             dqr_ref, dqp_ref, dkr_ref, dv_ref, dkc_ref, dvc_ref, dbt_ref):
        kcv, vcv = kc_ref[...], vc_ref[...]
        dkr_ref[...] = jnp.zeros_like(dkr_ref)
        dv_ref[...] = jnp.zeros_like(dv_ref)
        dkc_ref[...] = jnp.zeros_like(dkc_ref)
        dvc_ref[...] = jnp.zeros_like(dvc_ref)
        dbt_ref[...] = jnp.zeros_like(dbt_ref)

        def blk(b, carry):
            qs, ks, t = _block_geom(b)
            qrb, qpb = qr_ref[pl.ds(qs, QB), :], qp_ref[pl.ds(qs, QB), :]
            kb, vb = kr_ref[pl.ds(ks, KB), :], v_ref[pl.ds(ks, KB), :]
            dob = do_ref[pl.ds(qs, QB), :]
            p_lat, p_ctx = _softmax_block(qrb, qpb, kb, kcv, bt_ref[t])
            dp_lat = _tt(dob, vb)
            dp_ctx = _tt(dob, vcv)
            delta = jnp.sum(p_lat * dp_lat, axis=-1, keepdims=True) + jnp.sum(p_ctx * dp_ctx, axis=-1, keepdims=True)
            ds_lat = p_lat * (dp_lat - delta)
            ds_ctx = p_ctx * (dp_ctx - delta)
            dbt_ref[t] += ds_lat
            dsb_lat, dsb_ctx = ds_lat.astype(bf16), ds_ctx.astype(bf16)
            dqr_ref[pl.ds(qs, QB), :] = jnp.dot(dsb_lat, kb, preferred_element_type=f32)
            dqp_ref[pl.ds(qs, QB), :] = jnp.dot(dsb_ctx, kcv, preferred_element_type=f32)
            dkr_ref[pl.ds(ks, KB), :] += _tn(dsb_lat, qrb)
            dv_ref[pl.ds(ks, KB), :] += _tn(p_lat.astype(bf16), dob)
            dkc_ref[...] += _tn(dsb_ctx, qpb)
            dvc_ref[...] += _tn(p_ctx.astype(bf16), dob)
            return carry

        for b in range(NQB):
            blk(b, 0)

    sq = pl.BlockSpec((None, S, DH), lambda h: (h, 0, 0))
    sc = pl.BlockSpec((None, L, DH), lambda h: (h, 0, 0))
    sb = pl.BlockSpec((None, NT, QB, KB), lambda h: (h, 0, 0, 0))
    big, ctxs = SDS((H, S, DH), f32), SDS((H, L, DH), f32)
    return pl.pallas_call(
        body, name="attn_bwd", out_shape=(big, big, big, big, ctxs, ctxs, SDS((H, NT, QB, KB), f32)), grid=(H,),
        in_specs=[sq, sq, sq, sq, sc, sc, sb, sq], out_specs=(sq, sq, sq, sq, sc, sc, sb), compiler_params=_cp(48),
    )(qr, qp, kr, vh, kc, vc, btb, do)


def _bias_bwd(dbtb):
    pieces = _tile_pieces()

    def body(d_ref, o_ref, scr):
        scr[...] = jnp.zeros_like(scr)
        acc = [None] * N_DR
        for t in range(NT):
            for j in range(2):
                for u in range(KR):
                    dr = pieces[t][j][u]
                    if dr is None:
                        continue
                    piece = d_ref[t, j * GW:(j + 1) * GW, u * GW:(u + 1) * GW]
                    acc[dr] = piece if acc[dr] is None else acc[dr] + piece
        for dr in range(N_DR):
            scr[dr * GW:(dr + 1) * GW, 0:GW] = acc[dr]
        xs = pltpu.roll(scr[...], WIN_W - 1, 1)
        row = lax.broadcasted_iota(i32, xs.shape, 0)
        for b in range(6):
            xs = jnp.where(((row >> b) & 1) == 1, pltpu.roll(xs, 128 - (1 << b), 1), xs)
        o_ref[...] = jnp.sum(xs.reshape(N_DR, GW, 128), axis=1)

    return pl.pallas_call(
        body, name="bias_bwd", out_shape=SDS((H, N_DR, 128), f32), grid=(H,),
        in_specs=[pl.BlockSpec((None, NT, QB, KB), lambda h: (h, 0, 0, 0))],
        out_specs=pl.BlockSpec((None, N_DR, 128), lambda h: (h, 0, 0)),
        scratch_shapes=[pltpu.VMEM((N_DR * GW, 128), f32)],
    )(dbtb)


def _merge_heads(ref):
    return jnp.concatenate([ref[hh] for hh in range(H)], axis=1)


def _head_norm_bwd(xraw, gain, dy, ones_bd):
    r = lax.rsqrt(_head_sum(xraw * xraw, ones_bd) * (1.0 / DH) + RMS_EPS)
    xh = xraw * r
    gdy = dy * gain
    dx = r * (gdy - xh * (_head_sum(xh * gdy, ones_bd) * (1.0 / DH)))
    return dx, jnp.sum(dy * xh, axis=0, keepdims=True)


def _qk_bwd(dqr, dqp, dkr, dvh, p, gq, gk, rope):
    tm = 256

    def body(dqr_ref, dqp_ref, dkr_ref, dv_ref, qk_ref, gq_ref, gk_ref, cc_ref, cr_ref, sc_ref, sr_ref,
             dq_ref, dk_ref, dvo_ref, ggq_ref, ggk_ref):
        @pl.when(pl.program_id(0) == 0)
        def _():
            ggq_ref[...] = jnp.zeros_like(ggq_ref)
            ggk_ref[...] = jnp.zeros_like(ggk_ref)

        ones_bd = _head_ones()
        cs, sn = _rope_block(cc_ref, cr_ref, tm), _rope_block(sc_ref, sr_ref, tm)
        a = _merge_heads(dqr_ref)
        dyq = ((a * cs - _swap16(a) * sn) + _merge_heads(dqp_ref)) * QK_SCALE
        bk = _merge_heads(dkr_ref)
        dyk = bk * cs - _swap16(bk) * sn
        dq, gq_part = _head_norm_bwd(qk_ref[:, 0:DA], gq_ref[...], dyq, ones_bd)
        dk, gk_part = _head_norm_bwd(qk_ref[:, DA:2 * DA], gk_ref[...], dyk, ones_bd)
        dq_ref[...] = dq.astype(bf16)
        dk_ref[...] = dk.astype(bf16)
        dvo_ref[...] = _merge_heads(dv_ref).astype(bf16)
        ggq_ref[...] += gq_part
        ggk_ref[...] += gk_part

    hspec = pl.BlockSpec((H, tm, DH), lambda i: (0, i, 0))
    row = pl.BlockSpec((tm, DA), lambda i: (i, 0))
    fixed = pl.BlockSpec((1, DA), lambda i: (0, 0))
    piece = SDS((S, DA), bf16)
    return pl.pallas_call(
        body, name="qk_bwd", out_shape=(piece, piece, piece, SDS((1, DA), f32), SDS((1, DA), f32)), grid=(S // tm,),
        in_specs=[hspec, hspec, hspec, hspec, pl.BlockSpec((tm, 2 * DA), lambda i: (i, 0)), fixed, fixed]
        + _rope_specs(tm),
        out_specs=(row, row, row, fixed, fixed), compiler_params=_cp(40, dimension_semantics=("arbitrary",)),
    )(dqr, dqp, dkr, dvh, p, gq, gk, *rope)


def _ctx_bwd(dkc, dvc, pc, gk):
    def body(dkc_ref, dvc_ref, p_ref, gk_ref, dk_ref, dv_ref, ggk_ref):
        ones_bd = _head_ones()
        dk, gk_part = _head_norm_bwd(p_ref[:, 0:DA], gk_ref[...], _merge_heads(dkc_ref), ones_bd)
        dk_ref[...] = dk.astype(bf16)
        dv_ref[...] = _merge_heads(dvc_ref).astype(bf16)
        ggk_ref[...] = gk_part

    piece = SDS((L, DA), bf16)
    return pl.pallas_call(
        body, name="ctx_bwd", out_shape=(piece, piece, SDS((1, DA), f32)), in_specs=[VMEM_SPEC] * 4,
        out_specs=(VMEM_SPEC,) * 3,
    )(dkc, dvc, pc, gk)


def _grad_w_in_block(h, pa, pb, name, hc=None, pc=None, ctx_cols=None):
    tm = 512

    def body(*refs):
        if hc is None:
            h_ref, a_ref, b_ref, g_ref = refs
        else:
            h_ref, a_ref, b_ref, hc_ref, pc_ref, g_ref = refs

        @pl.when(pl.program_id(0) == 0)
        def _():
            g_ref[...] = jnp.zeros_like(g_ref)
            if hc is not None:
                g_ref[:, ctx_cols[0]:ctx_cols[1]] = _tn(hc_ref[...], pc_ref[...])

        hv = h_ref[...]
        g_ref[:, 0:DA] += _tn(hv, a_ref[...])
        g_ref[:, DA:2 * DA] += _tn(hv, b_ref[...])

    row = lambda i: (i, 0)
    fixed = lambda i: (0, 0)
    in_specs = [pl.BlockSpec((tm, D), row), pl.BlockSpec((tm, DA), row), pl.BlockSpec((tm, DA), row)]
    args = [h, pa, pb]
    if hc is not None:
        in_specs += [pl.BlockSpec((L, D), fixed), pl.BlockSpec((L, DA), fixed)]
        args += [hc, pc]
    return pl.pallas_call(
        body, name=name, out_shape=SDS((D, D), f32), grid=(S // tm,), in_specs=in_specs,
        out_specs=pl.BlockSpec((D, D), fixed), compiler_params=_cp(40, dimension_semantics=("arbitrary",)),
    )(*args)


def _norm_mod_bwd(x, dh, g, scale):
    r = lax.rsqrt(jnp.mean(x * x, axis=-1, keepdims=True) + RMS_EPS)
    xh = x * r
    y = xh * g
    dshift = jnp.sum(dh, axis=0, keepdims=True)
    dscale = jnp.sum(dh * y, axis=0, keepdims=True)
    dyn = dh * (1.0 + scale)
    dg = jnp.sum(dyn * xh, axis=0, keepdims=True)
    gdy = dyn * g
    dx = r * (gdy - xh * jnp.mean(xh * gdy, axis=-1, keepdims=True))
    return dx, dshift, dscale, dg


def _dh_grad_x(pieces, w4, xx, dy, norm_g, mrow, b_ada):
    tm = 256

    def body(*refs):
        p_refs = refs[:8]
        w_ref, x_ref, dy_ref, g_ref, m_ref, b_ref, gx_ref, dsh_ref, dsc_ref, dg_ref = refs[8:]

        @pl.when(pl.program_id(0) == 0)
        def _():
            dsh_ref[...] = jnp.zeros_like(dsh_ref)
            dsc_ref[...] = jnp.zeros_like(dsc_ref)
            dg_ref[...] = jnp.zeros_like(dg_ref)

        dh = None
        for j in range(4):
            for half in range(2):
                term = _tt(p_refs[2 * j + half][...], w_ref[j, :, half * DA:(half + 1) * DA])
                dh = term if dh is None else dh + term
        scale = m_ref[:, D:2 * D] + b_ref[:, D:2 * D]
        dx, dshift, dscale, dg = _norm_mod_bwd(x_ref[...], dh, g_ref[...], scale)
        gx_ref[...] = dy_ref[...] + dx
        dsh_ref[...] += dshift
        dsc_ref[...] += dscale
        dg_ref[...] += dg

    row = lambda i: (i, 0)
    fixed = lambda i: (0, 0)
    vec = SDS((1, D), f32)
    return pl.pallas_call(
        body, name="dh_grad_x", out_shape=(SDS((S, D), f32), vec, vec, vec), grid=(S // tm,),
        in_specs=[pl.BlockSpec((tm, DA), row)] * 8 + [pl.BlockSpec((4, D, D), lambda i: (0, 0, 0)),
                                                      pl.BlockSpec((tm, D), row), pl.BlockSpec((tm, D), row),
                                                      pl.BlockSpec((1, D), fixed), pl.BlockSpec((1, 3 * D), fixed),
                                                      pl.BlockSpec((1, 3 * D), fixed)],
        out_specs=(pl.BlockSpec((tm, D), row), pl.BlockSpec((1, D), fixed), pl.BlockSpec((1, D), fixed),
                   pl.BlockSpec((1, D), fixed)),
        compiler_params=_cp(56, dimension_semantics=("arbitrary",)),
    )(*pieces, w4, xx, dy, norm_g, mrow, b_ada)


def _dhc_sums(dkc_raw, dvc_m, w4, ctx2, norm_g, mrow_c, b_ada):
    def body(dk_ref, dv_ref, w0_ref, w1_ref, x_ref, g_ref, m_ref, b_ref, dsh_ref, dsc_ref, dg_ref):
        dh = _tt(dk_ref[...], w0_ref[:, DA:2 * DA]) + _tt(dv_ref[...], w1_ref[:, 0:DA])
        scale = m_ref[:, D:2 * D] + b_ref[:, D:2 * D]
        _, dshift, dscale, dg = _norm_mod_bwd(x_ref[...], dh, g_ref[...], scale)
        dsh_ref[...] = dshift
        dsc_ref[...] = dscale
        dg_ref[...] = dg

    fixed = lambda i: (0, 0)
    vec = SDS((1, D), f32)
    vspec = pl.BlockSpec((1, D), fixed)
    return pl.pallas_call(
        body, name="dhc_sums", out_shape=(vec, vec, vec), grid=(1,),
        in_specs=[pl.BlockSpec((L, DA), fixed), pl.BlockSpec((L, DA), fixed),
                  pl.BlockSpec((None, D, D), lambda i: (0, 0, 0)), pl.BlockSpec((None, D, D), lambda i: (1, 0, 0)),
                  pl.BlockSpec((L, D), fixed), vspec, pl.BlockSpec((1, 3 * D), fixed), pl.BlockSpec((1, 3 * D), fixed)],
        out_specs=(vspec, vspec, vspec), compiler_params=_cp(32),
    )(dkc_raw, dvc_m, w4, w4, ctx2, norm_g, mrow_c, b_ada)


def _rope_tables():
    nf = DH // 4
    inv = ROPE_THETA ** (-jnp.arange(nf, dtype=f32) / nf)
    ang_c = jnp.arange(GW, dtype=i32).astype(f32)[:, None] * inv
    ang_r = jnp.arange(ROWS, dtype=i32).astype(f32)[:, None] * inv
    zc, zr = jnp.zeros((GW, 2 * nf), f32), jnp.zeros((ROWS, 2 * nf), f32)
    ct_cos = jnp.tile(jnp.concatenate([zc, jnp.cos(ang_c), jnp.cos(ang_c)], axis=1), (1, H))
    ct_sin = jnp.tile(jnp.concatenate([zc, -jnp.sin(ang_c), jnp.sin(ang_c)], axis=1), (1, H))
    rt_cos = jnp.tile(jnp.concatenate([jnp.cos(ang_r), jnp.cos(ang_r), zr], axis=1), (1, H))
    rt_sin = jnp.tile(jnp.concatenate([-jnp.sin(ang_r), jnp.sin(ang_r), zr], axis=1), (1, H))
    rep8 = lambda t: jnp.broadcast_to(t[:, None, :], (ROWS, 8, DA)).reshape(ROWS * 8, DA)
    return ct_cos, rep8(rt_cos), ct_sin, rep8(rt_sin)


def _local_step(xx, ctx2, tgt, mrow, mrow_c, b_ada, norm_g, weights, q_norm_g, k_norm_g, rpb2, conv_w_full, conv_b,
                mid=None):
    gq = jnp.tile(q_norm_g, (1, H))
    gk = jnp.tile(k_norm_g, (1, H))
    rope = _rope_tables()
    rpb_pad = jnp.pad(rpb2, ((0, 0), (0, 0), (0, 128 - N_DC)))

    h = _prenorm(xx, norm_g, mrow, b_ada, 256, "prenorm_x")
    hc = _prenorm(ctx2, norm_g, mrow_c, b_ada, L, "prenorm_ctx")
    btb = _bias_prep(rpb_pad)
    w4, w_out_full = weights(btb)
    p = _in_proj(h, w4)
    pc = _ctx_proj(hc, w4)
    qr, qp, kr, vh = _qk_prep(p, gq, gk, rope)
    kc, vc = _ctx_prep(pc, gk)
    o = _attn_fwd(qr, qp, kr, vh, kc, vc, btb)
    attn_g = _attn_gate(o, p)
    conv_g = _conv_fwd(p, conv_w_full, conv_b)
    dy, dmix, g_w_out, dgate, loss_sum = _out_proj_loss(attn_g, conv_g, w_out_full, xx, tgt, mrow, b_ada)

    du, dbg, dcg, dzc, g_conv_w, g_conv_b = _conv_bwd(dmix, p, conv_w_full, conv_b)
    do, dza = _gate_bwd(dmix, p, o)
    dqr, dqp, dkr, dvh, dkc, dvc, dbtb = _attn_bwd(qr, qp, kr, vh, kc, vc, btb, do)
    g_rpb = _bias_bwd(dbtb)
    dq, dk, dv, g_gq, g_gk = _qk_bwd(dqr, dqp, dkr, dvh, p, gq, gk, rope)
    dkc_raw, dvc_m, g_gk_c = _ctx_bwd(dkc, dvc, pc, gk)
    g_w_in = [
        _grad_w_in_block(h, dq, dk, "grad_w_in_0", hc, dkc_raw, (DA, 2 * DA)),
        _grad_w_in_block(h, dv, dza, "grad_w_in_1", hc, dvc_m, (0, DA)),
        _grad_w_in_block(h, du, dbg, "grad_w_in_2"),
        _grad_w_in_block(h, dcg, dzc, "grad_w_in_3"),
    ]
    b_ada_late = b_ada if mid is None else mid(g_w_in, g_w_out, b_ada)
    grad_x, dshift, dscale, dng = _dh_grad_x([dq, dk, dv, dza, du, dbg, dcg, dzc], w4, xx, dy, norm_g, mrow,
                                             b_ada_late)
    dshift_c, dscale_c, dng_c = _dhc_sums(dkc_raw, dvc_m, w4, ctx2, norm_g, mrow_c, b_ada_late)
    return dict(loss_sum=loss_sum, grad_x=grad_x, g_w_in=g_w_in, g_w_out=g_w_out, g_conv_w=g_conv_w,
                g_conv_b=g_conv_b, g_rpb=g_rpb, g_gq=g_gq, g_gk=g_gk, g_gk_c=g_gk_c, dshift=dshift, dscale=dscale,
                dgate=dgate, dng=dng, dshift_c=dshift_c, dscale_c=dscale_c, dng_c=dng_c)


def _pair_sum_w_in(gs, rs, cvec):
    tr = 128

    def body(c_ref, g0, g1, g2, g3, r0, r1, r2, r3, t32_ref, tb_ref):
        for q, (g_ref, r_ref) in enumerate(((g0, r0), (g1, r1), (g2, r2), (g3, r3))):
            t = g_ref[...] + r_ref[...]
            t32_ref[q] = t
            tb_ref[q] = t.astype(bf16)

    half = D // 2
    gs_spec = pl.BlockSpec((tr, D), lambda i, c: (c[0] * (half // tr) + i, 0))
    rs_spec = pl.BlockSpec((None, tr, D), lambda i, c: (0, i, 0))
    ospec = pl.BlockSpec((4, tr, D), lambda i, c: (0, i, 0))
    grid_spec = pltpu.PrefetchScalarGridSpec(num_scalar_prefetch=1, grid=(half // tr,), in_specs=[gs_spec] * 4 + [rs_spec] * 4,
                                             out_specs=(ospec, ospec))
    return pl.pallas_call(body, name="pair_sum_w_in", out_shape=(SDS((4, half, D), f32), SDS((4, half, D), bf16)),
                          grid_spec=grid_spec, compiler_params=_cp(40))(cvec, *gs, *rs)


def _pair_sum_w_out(g, r, cvec):
    hr = D // 8

    def body(c_ref, g0, g1, g2, g3, r_ref, t32_ref, tb_ref):
        for q, g_ref in enumerate((g0, g1, g2, g3)):
            t = g_ref[...] + r_ref[q]
            t32_ref[q] = t
            tb_ref[q] = t.astype(bf16)

    gspecs = [pl.BlockSpec((hr, D), lambda i, c, q=q: (2 * q + c[0], 0)) for q in range(4)]
    full = pl.BlockSpec((4, hr, D), lambda i, c: (0, 0, 0))
    grid_spec = pltpu.PrefetchScalarGridSpec(num_scalar_prefetch=1, grid=(1,), in_specs=gspecs + [full],
                                             out_specs=(full, full))
    return pl.pallas_call(body, name="pair_sum_w_out", out_shape=(SDS((4, hr, D), f32), SDS((4, hr, D), bf16)),
                          grid_spec=grid_spec)(cvec, g, g, g, g, r)


def _chip_sum(t32, r2, jvec, name):
    rows = t32.shape[1]
    tr = min(rows, 128)

    def body(j_ref, t_ref, r_ref, u_ref):
        u_ref[...] = ((t_ref[...] + r_ref[0].astype(f32)) + r_ref[1].astype(f32)) + r_ref[2].astype(f32)

    grid_spec = pltpu.PrefetchScalarGridSpec(
        num_scalar_prefetch=1, grid=(rows // tr,),
        in_specs=[pl.BlockSpec((None, tr, D), lambda i, j: (j[0], i, 0)), pl.BlockSpec((3, tr, D), lambda i, j: (0, i, 0))],
        out_specs=pl.BlockSpec((tr, D), lambda i, j: (i, 0)))
    return pl.pallas_call(body, name=name, out_shape=SDS((rows, D), f32), grid_spec=grid_spec)(jvec, t32, r2)


_PK = {}
_off = 0
for _name, _rows in (("dm", 24), ("dmc", 24), ("dng", 8), ("dng_c", 8), ("gq", 8), ("gk", 8), ("gk_c", 8),
                     ("rpb", H * N_DR), ("conv_b", 8), ("conv_w", 16), ("loss", 8)):
    _PK[_name] = (_off, _off + _rows)
    _off += _rows
PK_ROWS = _off
RS_B_ADA, RS_NORM_G, RS_GQ, RS_GK, RS_RPB, RS_CONV_B, RS_CONV_W, RS_DMC, RS_LOSS, RS_ROWS = (
    0, 24, 32, 40, 48, 168, 176, 192, 216, 224)


def _small_reduce(gathered):
    def body(g_ref, o_ref):
        tot = g_ref[0]
        for b in range(1, 8):
            tot = tot + g_ref[b]

        def rows(name):
            a, z = _PK[name]
            return tot[a:z]

        o_ref[RS_B_ADA:RS_B_ADA + 24] = rows("dm") + rows("dmc")
        o_ref[RS_NORM_G:RS_NORM_G + 8] = rows("dng") + rows("dng_c")
        gq = jnp.broadcast_to(jnp.sum(rows("gq"), axis=0, keepdims=True), (8, 128))
        gk = jnp.broadcast_to(jnp.sum(rows("gk") + rows("gk_c"), axis=0, keepdims=True), (8, 128))
        o_ref[RS_GQ:RS_GQ + 8] = gq + pltpu.roll(gq, DH, 1)
        o_ref[RS_GK:RS_GK + 8] = gk + pltpu.roll(gk, DH, 1)
        o_ref[RS_RPB:RS_RPB + H * N_DR] = rows("rpb")
        o_ref[RS_CONV_B:RS_CONV_B + 8] = rows("conv_b")
        o_ref[RS_CONV_W:RS_CONV_W + 16] = rows("conv_w")
        o_ref[RS_DMC:RS_DMC + 24] = rows("dmc")
        o_ref[RS_LOSS:RS_LOSS + 8] = rows("loss")

    return pl.pallas_call(body, name="small_reduce", out_shape=SDS((RS_ROWS, 128), f32), in_specs=[VMEM_SPEC],
                          out_specs=VMEM_SPEC)(gathered)


def _w_ada_grad(sc16, dm16s, w_ada_shard):
    def body(sc_ref, dm_ref, w_ref, g_ref, part_ref):
        dm = dm_ref[...]
        g_ref[...] = lax.dot_general(sc_ref[...], dm, (((0,), (0,)), ((), ())), precision=HIGHEST,
                                     preferred_element_type=f32)
        part_ref[...] = lax.dot_general(dm[8:16], w_ref[...], (((1,), (1,)), ((), ())), precision=HIGHEST,
                                        preferred_element_type=f32)

    ncol = w_ada_shard.shape[1]
    return pl.pallas_call(body, name="w_ada_grad", out_shape=(SDS((D, ncol), f32), SDS((8, D), f32)),
                          in_specs=[VMEM_SPEC] * 3, out_specs=(VMEM_SPEC, VMEM_SPEC), compiler_params=_cp(40),
                          )(sc16, dm16s, w_ada_shard)


def _c_ctx_grad(parts4, c_ctx_row):
    def body(p_ref, c_ref, o_ref):
        tot = ((p_ref[0] + p_ref[1]) + p_ref[2]) + p_ref[3]
        o_ref[...] = tot[0:1] * _dsilu(c_ref[...])

    return pl.pallas_call(body, name="c_ctx_grad", out_shape=SDS((1, D), f32), in_specs=[VMEM_SPEC, VMEM_SPEC],
                          out_specs=VMEM_SPEC)(parts4, c_ctx_row)


def _adamw(w, g, m, v, name):
    rows, cols = w.shape
    tr = 256 if rows % 256 == 0 else rows

    def body(w_ref, g_ref, m_ref, v_ref, d_ref, m2_ref, v2_ref):
        gv = g_ref[...]
        m2 = ADAM_B1 * m_ref[...] + (1.0 - ADAM_B1) * gv
        v2 = ADAM_B2 * v_ref[...] + (1.0 - ADAM_B2) * jnp.square(gv)
        m_hat = m2 / (1.0 - ADAM_B1 ** ADAM_STEP)
        v_hat = v2 / (1.0 - ADAM_B2 ** ADAM_STEP)
        d_ref[...] = -ADAM_LR * (m_hat / (jnp.sqrt(v_hat) + ADAM_EPS) + ADAM_WD * w_ref[...])
        m2_ref[...] = m2
        v2_ref[...] = v2

    spec = pl.BlockSpec((tr, cols), lambda i: (i, 0))
    shp = SDS((rows, cols), f32)
    return pl.pallas_call(body, name=name, out_shape=(shp, shp, shp), grid=(rows // tr,), in_specs=[spec] * 4,
                          out_specs=(spec, spec, spec))(w, g, m, v)


def _adamw_halves(w, g_mine, g_other, m, v, cvec, name):
    rows, cols = w.shape
    half = rows // 2
    tr = min(256, half)
    per_half = half // tr

    def body(c_ref, w_ref, ga_ref, gb_ref, m_ref, v_ref, g_ref, d_ref, m2_ref, v2_ref):
        in_my_half = (pl.program_id(0) // per_half) == c_ref[0]
        gv = jnp.where(in_my_half, ga_ref[...], gb_ref[...])
        g_ref[...] = gv
        m2 = ADAM_B1 * m_ref[...] + (1.0 - ADAM_B1) * gv
        v2 = ADAM_B2 * v_ref[...] + (1.0 - ADAM_B2) * jnp.square(gv)
        m_hat = m2 / (1.0 - ADAM_B1 ** ADAM_STEP)
        v_hat = v2 / (1.0 - ADAM_B2 ** ADAM_STEP)
        d_ref[...] = -ADAM_LR * (m_hat / (jnp.sqrt(v_hat) + ADAM_EPS) + ADAM_WD * w_ref[...])
        m2_ref[...] = m2
        v2_ref[...] = v2

    full = pl.BlockSpec((tr, cols), lambda i, c: (i, 0))
    part = pl.BlockSpec((tr, cols), lambda i, c: (i % per_half, 0))
    shp = SDS((rows, cols), f32)
    grid_spec = pltpu.PrefetchScalarGridSpec(num_scalar_prefetch=1, grid=(rows // tr,),
                                             in_specs=[full, part, part, full, full], out_specs=(full,) * 4)
    return pl.pallas_call(body, name=name, out_shape=(shp,) * 4, grid_spec=grid_spec)(cvec, w, g_mine, g_other, m, v)


def _rows128(a):
    return a.reshape(-1, 128)


def _pad_lanes(a):
    a2 = a.reshape(-1, a.shape[-1])
    return jnp.pad(a2, ((0, 0), (0, 128 - a2.shape[1])))


def kernel(x, c, ctx, c_ctx, w_ada, b_ada, norm_g, w_in, q_norm_g, k_norm_g, rpb, conv_w, conv_b, w_out, loss_target, m_c_ctx, m_w_ada, m_b_ada, m_norm_g, m_w_in, m_q_norm_g, m_k_norm_g, m_rpb, m_conv_w, m_conv_b, m_w_out, v_c_ctx, v_w_ada, v_b_ada, v_norm_g, v_w_in, v_q_norm_g, v_k_norm_g, v_rpb, v_conv_w, v_conv_b, v_w_out):
    xi, yi, ci = lax.axis_index("x"), lax.axis_index("y"), lax.axis_index("c")
    dev = 4 * xi + 2 * yi + ci
    chip = 2 * xi + yi
    cvec = jnp.reshape(ci, (1,)).astype(i32)
    jvec = jnp.reshape(chip, (1,)).astype(i32)
    w_ada_s = w_ada[0]
    ncol = w_ada_s.shape[1]

    ssem, rsem, w4s, wo4s, token = _halves_start(
        [_cast_to_slot(w_in[0], jvec, "cast_w_in"), _cast_to_slot(w_out[0], jvec, "cast_w_out")], "weights_ici_start")

    c8 = _all_gather8(c.reshape(8, 128) + token, "gather_c").reshape(8, D)
    cc = jnp.concatenate([c8, c_ctx.reshape(1, D), jnp.zeros((7, D), f32)], axis=0)
    m_shard, sc16 = _adaln_shard(cc, w_ada_s)

    conv_w_pad = jnp.pad(conv_w[0], ((0, 5), (0, 0)))
    m4, cw4 = _chip_gather([m_shard, conv_w_pad], [], "gather_mod")
    m_full = jnp.transpose(m4, (1, 0, 2)).reshape(16, 4 * ncol)
    mrow = lax.dynamic_slice(m_full, (dev, 0), (1, 3 * D))
    mrow_c = m_full[8:9]
    conv_w_full = jnp.transpose(cw4[:, 0:3, :], (1, 0, 2)).reshape(3, DC)

    def weights(after):
        w4w, wo4w = _halves_wait(ssem, rsem, [w4s, wo4s], after, "weights_ici_wait")
        w4f, wo4f = _halves_forward([w4w, wo4w], "weights_pair_forward")
        return w4f, wo4f.reshape(D, D)

    exchange = _exchange_copies([1, 1, 1, 1, 4])
    pending = {}

    def mid(g_w_in, g_w_out, b_ada_in):
        srcs = list(g_w_in) + [g_w_out]
        shapes = [SDS((1, D // 2, D), f32)] * 4 + [SDS((4, D // 8, D), f32)]
        out = _split_start(srcs, shapes, 8, exchange, "grad_pair_start")
        pending["ex"] = (out[0], out[1], list(out[2:7]), list(out[7:12]))
        return b_ada_in + out[12][0:1, 0:1]

    r = _local_step(x[0], ctx[0], loss_target[0], mrow, mrow_c, b_ada, norm_g, weights, q_norm_g, k_norm_g,
                    rpb[0], conv_w_full, conv_b, mid)
    ex_ssem, ex_rsem, ex_srcs, ex_lands = pending["ex"]
    ex_g, ex = _split_wait(ex_ssem, ex_rsem, ex_srcs, ex_lands, r["grad_x"], exchange, "grad_pair_wait")
    t32, tb = _pair_sum_w_in(ex_g[0:4], ex[0:4], cvec)
    to32, tob = _pair_sum_w_out(ex_g[4], ex[4], cvec)
    sc_out = _split_start([tb, tob], [SDS((3, D // 2, D), bf16), SDS((3, D // 8, D), bf16)], 6, _scatter_copies,
                          "grad_chip_start")
    sc_token = sc_out[6][0:1, 0:1]

    dm = jnp.concatenate([r["dshift"] + sc_token, r["dscale"], r["dgate"]], axis=1)
    dmc = jnp.concatenate([r["dshift_c"], r["dscale_c"], jnp.zeros((1, D), f32)], axis=1)
    pack_parts = [_rows128(dm), _rows128(dmc), _rows128(r["dng"]), _rows128(r["dng_c"]), _rows128(r["g_gq"]),
                  _rows128(r["g_gk"]), _rows128(r["g_gk_c"]), r["g_rpb"].reshape(H * N_DR, 128),
                  _rows128(r["g_conv_b"]), _rows128(r["g_conv_w"][0:3]), jnp.pad(r["loss_sum"], ((0, 0), (0, 127)))]
    pack = jnp.concatenate([jnp.pad(p, ((0, -p.shape[0] % 8), (0, 0))) for p in pack_parts], axis=0)
    gathered = _all_gather8(pack, "gather_small")
    red = _small_reduce(gathered)
    g_b_ada = red[RS_B_ADA:RS_B_ADA + 24].reshape(1, 3 * D)
    g_norm_g = red[RS_NORM_G:RS_NORM_G + 8].reshape(1, D)
    g_q = red[RS_GQ:RS_GQ + 1, 0:DH]
    g_k = red[RS_GK:RS_GK + 1, 0:DH]
    g_rpb = red[RS_RPB:RS_RPB + H * N_DR]
    g_conv_b = red[RS_CONV_B:RS_CONV_B + 4].reshape(1, DC)
    g_conv_w_full = red[RS_CONV_W:RS_CONV_W + 12].reshape(3, DC)
    assert pack.shape[0] == PK_ROWS
    g_conv_w_s = lax.dynamic_slice(g_conv_w_full, (0, chip * 128), (3, 128))
    dmc_tot = red[RS_DMC:RS_DMC + 24].reshape(1, 3 * D)
    loss = red[RS_LOSS, 0] * (0.5 / D)

    a0, a1 = _PK["dm"]
    dm8 = gathered[:, a0:a1, :].reshape(8, 3 * D)
    dm16 = jnp.concatenate([dm8, dmc_tot, jnp.zeros((7, 3 * D), f32)], axis=0)
    dm16s = lax.dynamic_slice(dm16, (0, chip * ncol), (16, ncol))
    g_w_ada_s, cpart = _w_ada_grad(sc16, dm16s, w_ada_s)
    (cparts4,) = _chip_gather([cpart], [], "gather_c_ctx_parts")
    g_c_ctx = _c_ctx_grad(cparts4, c_ctx.reshape(1, D))

    d_w_ada, nm_w_ada, nv_w_ada = _adamw(w_ada_s, g_w_ada_s, m_w_ada[0], v_w_ada[0], "adamw_w_ada")

    def small_pack(a_c_ctx, a_b_ada, a_norm_g, a_q, a_k, a_rpb, a_conv_w, a_conv_b):
        parts = [_rows128(a_c_ctx), _rows128(a_b_ada), _rows128(a_norm_g), _pad_lanes(a_q), _pad_lanes(a_k),
                 a_rpb, _rows128(a_conv_w), _rows128(a_conv_b)]
        return jnp.concatenate([jnp.pad(p, ((0, -p.shape[0] % 8), (0, 0))) for p in parts], axis=0)

    w_pk = small_pack(c_ctx, b_ada, norm_g, q_norm_g, k_norm_g, _pad_lanes(rpb[0]), conv_w[0], conv_b)
    g_pk = small_pack(g_c_ctx, g_b_ada, g_norm_g, g_q, g_k, g_rpb, g_conv_w_s, g_conv_b)
    m_pk = small_pack(m_c_ctx, m_b_ada, m_norm_g, m_q_norm_g, m_k_norm_g, _pad_lanes(m_rpb[0]), m_conv_w[0], m_conv_b)
    v_pk = small_pack(v_c_ctx, v_b_ada, v_norm_g, v_q_norm_g, v_k_norm_g, _pad_lanes(v_rpb[0]), v_conv_w[0], v_conv_b)
    d_pk, nm_pk, nv_pk = _adamw(w_pk, g_pk, m_pk, v_pk, "adamw_small")

    _, (r2, ro2) = _split_wait(sc_out[0], sc_out[1], [sc_out[2], sc_out[3]], [sc_out[4], sc_out[5]], d_pk,
                               _scatter_copies, "grad_chip_wait")
    u_in = _chip_sum(t32, r2, jvec, "chip_sum_w_in")
    u_out = _chip_sum(to32, ro2, jvec, "chip_sum_w_out")
    o_in, o_out = _sibling_send([u_in, u_out], "grad_pair_send")
    g_w_in_s, d_w_in, nm_w_in, nv_w_in = _adamw_halves(w_in[0], u_in, o_in, m_w_in[0], v_w_in[0], cvec, "adamw_w_in")
    g_w_out_s, d_w_out, nm_w_out, nv_w_out = _adamw_halves(w_out[0], u_out, o_out, m_w_out[0], v_w_out[0], cvec,
                                                           "adamw_w_out")

    def small_unpack(pk):
        o = 0
        out = []
        for rows, fn in ((8, lambda a: a.reshape(D)), (24, lambda a: a.reshape(1, 3 * D)), (8, lambda a: a.reshape(1, D)),
                         (1, lambda a: a[:, 0:DH]), (1, lambda a: a[:, 0:DH]),
                         (H * N_DR, lambda a: a[:, 0:N_DC].reshape(1, H, N_DR, N_DC)),
                         (3, lambda a: a.reshape(1, 3, 128)), (4, lambda a: a.reshape(1, DC))):
            out.append(fn(pk[o:o + rows]))
            o += rows + (-rows % 8)
        return out

    def ordered(small, big_w_ada, big_w_in, big_w_out):
        s_c_ctx, s_b_ada, s_norm_g, s_q, s_k, s_rpb, s_conv_w, s_conv_b = small
        return [s_c_ctx, big_w_ada[None], s_b_ada, s_norm_g, big_w_in[None], s_q, s_k, s_rpb, s_conv_w, s_conv_b,
                big_w_out[None]]

    grads = ordered(small_unpack(g_pk), g_w_ada_s, g_w_in_s, g_w_out_s)
    deltas = ordered(small_unpack(d_pk), d_w_ada, d_w_in, d_w_out)
    new_m = ordered(small_unpack(nm_pk), nm_w_ada, nm_w_in, nm_w_out)
    new_v = ordered(small_unpack(nv_pk), nv_w_ada, nv_w_in, nv_w_out)
    return (loss, r["grad_x"][None], *grads, *deltas, *new_m, *new_v)
```

```python
import functools

import jax
import jax.numpy as jnp
from jax import lax
from jax.experimental import pallas as pl
from jax.experimental.pallas import tpu as pltpu

f32, bf16, i32 = jnp.float32, jnp.bfloat16, jnp.int32
MESH = pl.DeviceIdType.MESH
HIGHEST = lax.Precision.HIGHEST

D = 1024
S = 2048
L = 256
GW = 64
ROWS = S // GW
H = 8
DH = 64
DA = H * DH
DC = 512
WIN_H, WIN_W = 8, 16
N_DR, N_DC = 2 * WIN_H - 1, 2 * WIN_W - 1
RMS_EPS = 1e-6
ROPE_THETA = 10000.0
QK_SCALE = DH ** -0.5
NEG = -1e30

QB = 128
NQB = S // QB
KR = 9
KB = KR * GW
TILE_GEOM = ((0, 0), (2, 0), (4, 0), (28, 23), (30, 23))
NT = len(TILE_GEOM)

ADAM_LR, ADAM_B1, ADAM_B2, ADAM_EPS, ADAM_WD, ADAM_STEP = 0.001, 0.9, 0.999, 1e-08, 0.01, 10

VMEM_SPEC = pl.BlockSpec(memory_space=pltpu.VMEM)
ANY_SPEC = pl.BlockSpec(memory_space=pl.ANY)
SMEM_SPEC = pl.BlockSpec(memory_space=pltpu.SMEM)
SDS = jax.ShapeDtypeStruct


def _cp(vmem_mb=None, **kw):
    if vmem_mb is not None:
        kw["vmem_limit_bytes"] = vmem_mb << 20
    return pltpu.CompilerParams(**kw)


def _silu(z):
    return z * jax.nn.sigmoid(z)


def _dsilu(z):
    sg = jax.nn.sigmoid(z)
    return sg * (1.0 + z * (1.0 - sg))


def _row_start(i):
    return min(max(i - WIN_H // 2, 0), ROWS - WIN_H)


def _my_pos():
    return lax.axis_index("x"), lax.axis_index("y"), lax.axis_index("c")


def _flip(v, bit):
    return 1 - v if bit else v


def _all_gather8(xin, name):
    R, N = xin.shape

    def body(x_ref, o_ref, ssem, rsem, lsem):
        x, y, c = _my_pos()
        me = 4 * x + 2 * y + c
        own = pltpu.make_async_copy(x_ref, o_ref.at[me], lsem)
        own.start()
        sends = []
        for k in range(1, 8):
            tgt = (_flip(x, (k >> 2) & 1), _flip(y, (k >> 1) & 1), _flip(c, k & 1))
            cp = pltpu.make_async_remote_copy(src_ref=x_ref, dst_ref=o_ref.at[me], send_sem=ssem.at[k - 1],
                                              recv_sem=rsem.at[k - 1], device_id=tgt, device_id_type=MESH)
            cp.start()
            sends.append(cp)
        for k in range(1, 8):
            tgt = (_flip(x, (k >> 2) & 1), _flip(y, (k >> 1) & 1), _flip(c, k & 1))
            peer = 4 * tgt[0] + 2 * tgt[1] + tgt[2]
            pltpu.make_async_remote_copy(src_ref=x_ref, dst_ref=o_ref.at[peer], send_sem=ssem.at[k - 1],
                                         recv_sem=rsem.at[k - 1], device_id=tgt, device_id_type=MESH).wait_recv()
        for cp in sends:
            cp.wait_send()
        own.wait()

    return pl.pallas_call(
        body, name=name, out_shape=SDS((8, R, N), xin.dtype), in_specs=[VMEM_SPEC], out_specs=VMEM_SPEC,
        scratch_shapes=[pltpu.SemaphoreType.DMA((7,)), pltpu.SemaphoreType.DMA((7,)), pltpu.SemaphoreType.DMA],
    )(xin)


def _chip_gather(smalls, name, after=None):
    ns = len(smalls)

    def body(*refs):
        s_in, s_out = refs[:ns], refs[ns + 1:2 * ns + 1]
        ssem, rsem, lsem = refs[2 * ns + 1:]
        x, y, c = _my_pos()
        j = 2 * x + y
        chips = _peer_chips(x, y, c)
        local = [pltpu.make_async_copy(s_in[a], s_out[a].at[j], lsem.at[a]) for a in range(ns)]
        for cp in local:
            cp.start()
        sends = []
        for a in range(ns):
            for k in range(3):
                cp = pltpu.make_async_remote_copy(src_ref=s_in[a], dst_ref=s_out[a].at[j], send_sem=ssem.at[3 * a + k],
                                                  recv_sem=rsem.at[3 * a + k], device_id=chips[k][0], device_id_type=MESH)
                cp.start()
                sends.append(cp)
        for a in range(ns):
            for k in range(3):
                pltpu.make_async_remote_copy(src_ref=s_in[a], dst_ref=s_out[a].at[chips[k][1]], send_sem=ssem.at[3 * a + k],
                                             recv_sem=rsem.at[3 * a + k], device_id=chips[k][0],
                                             device_id_type=MESH).wait_recv()
        for cp in sends:
            cp.wait_send()
        for cp in local:
            cp.wait()

    return pl.pallas_call(
        body, name=name, out_shape=[SDS((4,) + a.shape, a.dtype) for a in smalls],
        in_specs=[VMEM_SPEC] * ns + [ANY_SPEC], out_specs=[VMEM_SPEC] * ns,
        scratch_shapes=[pltpu.SemaphoreType.DMA((3 * ns,)), pltpu.SemaphoreType.DMA((3 * ns,)),
                        pltpu.SemaphoreType.DMA((ns,))],
    )(*smalls, smalls[0] if after is None else after)


HBM_SPEC = pl.BlockSpec(memory_space=pltpu.HBM)
SEM_SPEC = pl.BlockSpec(memory_space=pltpu.SEMAPHORE)
DATAFLOW = pltpu.SideEffectType.DATAFLOW_SIDE_EFFECTING


def _peer_chips(x, y, c):
    out = []
    for k in range(1, 4):
        px, py = _flip(x, (k >> 1) & 1), _flip(y, k & 1)
        out.append(((px, py, c), 2 * px + py))
    return out


def _half_copies(srcs, dsts, ssem, rsem):
    x, y, c = _my_pos()
    j = 2 * x + y
    pairs = []
    for a in range(len(srcs)):
        half = srcs[a].shape[1] // 2
        mine = pl.ds(pl.multiple_of(c * half, 8), half)
        for k, (dev, pj) in enumerate(_peer_chips(x, y, c)):
            sem = 3 * a + k
            send = pltpu.make_async_remote_copy(src_ref=srcs[a].at[j, mine], dst_ref=dsts[a].at[j, mine],
                                                send_sem=ssem.at[sem], recv_sem=rsem.at[sem], device_id=dev,
                                                device_id_type=MESH)
            arrive = pltpu.make_async_remote_copy(src_ref=srcs[a].at[j, mine], dst_ref=dsts[a].at[pj, mine],
                                                  send_sem=ssem.at[sem], recv_sem=rsem.at[sem], device_id=dev,
                                                  device_id_type=MESH)
            pairs.append((send, arrive))
    return pairs


def _halves_start(bigs, after, name):
    nb = len(bigs)

    def body(*refs):
        b_in = refs[:nb]
        ssem, rsem = refs[nb + 1], refs[nb + 2]
        b_out = refs[nb + 3:2 * nb + 3]
        token = refs[2 * nb + 3]
        for send, _ in _half_copies(b_in, b_out, ssem, rsem):
            send.start()
        token[...] = jnp.zeros_like(token)

    out_shape = (pltpu.SemaphoreType.DMA((3 * nb,)), pltpu.SemaphoreType.DMA((3 * nb,)),
                 *[pltpu.HBM(b.shape, b.dtype) for b in bigs], SDS((8, 128), f32))
    return pl.pallas_call(
        body, name=name, out_shape=out_shape, in_specs=[HBM_SPEC] * nb + [ANY_SPEC],
        out_specs=(SEM_SPEC, SEM_SPEC, *[HBM_SPEC] * nb, VMEM_SPEC),
        input_output_aliases={a: 2 + a for a in range(nb)}, compiler_params=_cp(has_side_effects=DATAFLOW),
    )(*[pltpu.with_memory_space_constraint(b, pltpu.HBM) for b in bigs], after)


def _halves_wait(ssem, rsem, bigs, after, name):
    nb = len(bigs)

    def body(*refs):
        b_in = refs[:nb]
        ssem_ref, rsem_ref = refs[nb], refs[nb + 1]
        for send, arrive in _half_copies(b_in, b_in, ssem_ref, rsem_ref):
            send.wait_send()
            arrive.wait_recv()

    return pl.pallas_call(
        body, name=name, out_shape=tuple(pltpu.HBM(b.shape, b.dtype) for b in bigs),
        in_specs=[HBM_SPEC] * nb + [SEM_SPEC, SEM_SPEC, ANY_SPEC], out_specs=tuple([HBM_SPEC] * nb),
        input_output_aliases={a: a for a in range(nb)}, compiler_params=_cp(has_side_effects=DATAFLOW),
    )(*bigs, ssem, rsem, after)


def _halves_forward(bigs, name):
    nb = len(bigs)

    def body(*refs):
        b_in, b_out = refs[:nb], refs[nb:2 * nb]
        ssem, rsem = refs[2 * nb:]
        x, y, c = _my_pos()
        sib = (x, y, 1 - c)
        sends = []
        for a in range(nb):
            half = b_in[a].shape[1] // 2
            mine = pl.ds(pl.multiple_of(c * half, 8), half)
            for k, (_, pj) in enumerate(_peer_chips(x, y, c)):
                cp = pltpu.make_async_remote_copy(src_ref=b_in[a].at[pj, mine], dst_ref=b_out[a].at[pj, mine],
                                                  send_sem=ssem.at[3 * a + k], recv_sem=rsem.at[3 * a + k],
                                                  device_id=sib, device_id_type=MESH)
                cp.start()
                sends.append(cp)
        for a in range(nb):
            half = b_in[a].shape[1] // 2
            other = pl.ds(pl.multiple_of((1 - c) * half, 8), half)
            for k, (_, pj) in enumerate(_peer_chips(x, y, c)):
                pltpu.make_async_remote_copy(src_ref=b_in[a].at[pj, other], dst_ref=b_out[a].at[pj, other],
                                             send_sem=ssem.at[3 * a + k], recv_sem=rsem.at[3 * a + k],
                                             device_id=sib, device_id_type=MESH).wait_recv()
        for cp in sends:
            cp.wait_send()

    return pl.pallas_call(
        body, name=name, out_shape=[SDS(b.shape, b.dtype) for b in bigs], in_specs=[ANY_SPEC] * nb,
        out_specs=[ANY_SPEC] * nb, input_output_aliases={a: a for a in range(nb)},
        scratch_shapes=[pltpu.SemaphoreType.DMA((3 * nb,)), pltpu.SemaphoreType.DMA((3 * nb,))],
    )(*bigs)


def _cast_to_slot(w, jvec, name):
    rows, cols = w.shape
    tr = 256

    def body(j_ref, w_ref, o_ref):
        o_ref[...] = w_ref[...].astype(bf16)

    grid_spec = pltpu.PrefetchScalarGridSpec(
        num_scalar_prefetch=1, grid=(rows // tr,), in_specs=[pl.BlockSpec((tr, cols), lambda i, j: (i, 0))],
        out_specs=pl.BlockSpec((None, tr, cols), lambda i, j: (j[0], i, 0)))
    return pl.pallas_call(body, name=name, out_shape=SDS((4, rows, cols), bf16), grid_spec=grid_spec)(jvec, w)


def _exchange_copies(n_blocks):
    def make(srcs, lands, ssem, rsem):
        x, y, c = _my_pos()
        cps = []
        sem = 0
        for a in range(len(srcs)):
            rb = srcs[a].shape[0] // n_blocks[a]
            half = rb // 2
            for jb in range(n_blocks[a]):
                theirs = pl.ds(pl.multiple_of(jb * rb + (1 - c) * half, 8), half)
                cps.append(pltpu.make_async_remote_copy(src_ref=srcs[a].at[theirs], dst_ref=lands[a].at[jb],
                                                        send_sem=ssem.at[sem], recv_sem=rsem.at[sem],
                                                        device_id=(x, y, 1 - c), device_id_type=MESH))
                sem += 1
        return cps
    return make


def _scatter_copies(srcs, lands, ssem, rsem):
    x, y, c = _my_pos()
    cps = []
    for a in range(len(srcs)):
        for k, (dev, pj) in enumerate(_peer_chips(x, y, c)):
            cps.append(pltpu.make_async_remote_copy(src_ref=srcs[a].at[pj], dst_ref=lands[a].at[k],
                                                    send_sem=ssem.at[3 * a + k], recv_sem=rsem.at[3 * a + k],
                                                    device_id=dev, device_id_type=MESH))
    return cps


def _split_start(srcs, land_shapes, n_cp, make, name):
    ns, nl = len(srcs), len(land_shapes)

    def body(*refs):
        s_in = refs[:ns]
        ssem, rsem = refs[ns + nl], refs[ns + nl + 1]
        l_out = refs[2 * ns + nl + 2:2 * ns + 2 * nl + 2]
        token = refs[2 * ns + 2 * nl + 2]
        for cp in make(s_in, l_out, ssem, rsem):
            cp.start()
        token[...] = jnp.zeros_like(token)

    lands = [pltpu.with_memory_space_constraint(lax.empty(sh.shape, sh.dtype), pltpu.HBM) for sh in land_shapes]
    out_shape = (pltpu.SemaphoreType.DMA((n_cp,)), pltpu.SemaphoreType.DMA((n_cp,)),
                 *[pltpu.HBM(b.shape, b.dtype) for b in srcs], *[pltpu.HBM(b.shape, b.dtype) for b in land_shapes],
                 SDS((8, 128), f32))
    return pl.pallas_call(
        body, name=name, out_shape=out_shape, in_specs=[HBM_SPEC] * (ns + nl),
        out_specs=(SEM_SPEC, SEM_SPEC, *[HBM_SPEC] * (ns + nl), VMEM_SPEC),
        input_output_aliases={i: 2 + i for i in range(ns + nl)}, compiler_params=_cp(has_side_effects=DATAFLOW),
    )(*[pltpu.with_memory_space_constraint(b, pltpu.HBM) for b in srcs], *lands)


def _split_wait(ssem, rsem, srcs, lands, after, make, name):
    ns, nl = len(srcs), len(lands)

    def body(*refs):
        s_in, l_in = refs[:ns], refs[ns:ns + nl]
        ssem_ref, rsem_ref = refs[ns + nl], refs[ns + nl + 1]
        for cp in make(s_in, l_in, ssem_ref, rsem_ref):
            cp.wait_send()
            cp.wait_recv()

    outs = pl.pallas_call(
        body, name=name, out_shape=tuple(pltpu.HBM(b.shape, b.dtype) for b in (*srcs, *lands)),
        in_specs=[HBM_SPEC] * (ns + nl) + [SEM_SPEC, SEM_SPEC, ANY_SPEC], out_specs=tuple([HBM_SPEC] * (ns + nl)),
        input_output_aliases={i: i for i in range(ns + nl)}, compiler_params=_cp(has_side_effects=DATAFLOW),
    )(*srcs, *lands, ssem, rsem, after)
    return list(outs[:ns]), list(outs[ns:])


def _sibling_send(halves, name):
    n = len(halves)

    def body(*refs):
        ins, outs = refs[:n], refs[n:2 * n]
        ssem, rsem = refs[2 * n:]
        x, y, c = _my_pos()
        cps = []
        for a in range(n):
            cp = pltpu.make_async_remote_copy(src_ref=ins[a], dst_ref=outs[a], send_sem=ssem.at[a],
                                              recv_sem=rsem.at[a], device_id=(x, y, 1 - c), device_id_type=MESH)
            cp.start()
            cps.append(cp)
        for cp in cps:
            cp.wait_recv()
        for cp in cps:
            cp.wait_send()

    out_shape = [SDS(h.shape, h.dtype) for h in halves]
    return pl.pallas_call(
        body, name=name, out_shape=out_shape, in_specs=[ANY_SPEC] * n, out_specs=[ANY_SPEC] * n,
        scratch_shapes=[pltpu.SemaphoreType.DMA((n,)), pltpu.SemaphoreType.DMA((n,))],
    )(*halves)


def _adaln_shard(cc, w_ada_shard):
    def body(c_ref, w_ref, m_ref, sc_ref):
        sc = _silu(c_ref[...])
        sc_ref[...] = sc
        m_ref[...] = jnp.dot(sc, w_ref[...], precision=HIGHEST, preferred_element_type=f32)

    return pl.pallas_call(
        body, name="adaln_shard", out_shape=(SDS((16, w_ada_shard.shape[1]), f32), SDS((16, D), f32)),
        in_specs=[VMEM_SPEC, VMEM_SPEC], out_specs=(VMEM_SPEC, VMEM_SPEC), compiler_params=_cp(32),
    )(cc, w_ada_shard)


def _prenorm(xx, norm_g, mrow, b_ada, tm, name):
    n = xx.shape[0]

    def body(x_ref, g_ref, m_ref, b_ref, h_ref):
        x = x_ref[...]
        shift = m_ref[:, 0:D] + b_ref[:, 0:D]
        scale = m_ref[:, D:2 * D] + b_ref[:, D:2 * D]
        r = lax.rsqrt(jnp.mean(x * x, axis=-1, keepdims=True) + RMS_EPS)
        y = (x * r) * g_ref[...]
        h_ref[...] = (y * (1.0 + scale) + shift).astype(bf16)

    row = lambda i: (i, 0)
    fixed = lambda i: (0, 0)
    return pl.pallas_call(
        body, name=name, out_shape=SDS((n, D), bf16), grid=(n // tm,),
        in_specs=[pl.BlockSpec((tm, D), row), pl.BlockSpec((1, D), fixed), pl.BlockSpec((1, 3 * D), fixed),
                  pl.BlockSpec((1, 3 * D), fixed)],
        out_specs=pl.BlockSpec((tm, D), row),
    )(xx, norm_g, mrow, b_ada)


def _in_proj(h, w4):
    tm = 512

    def body(h_ref, w_ref, p_ref):
        p_ref[...] = jnp.dot(h_ref[...], w_ref[...], preferred_element_type=f32)

    return pl.pallas_call(
        body, name="in_proj", out_shape=SDS((S, 4 * D), f32), grid=(4, S // tm),
        in_specs=[pl.BlockSpec((tm, D), lambda j, k: (k, 0)), pl.BlockSpec((None, D, D), lambda j, k: (j, 0, 0))],
        out_specs=pl.BlockSpec((tm, D), lambda j, k: (k, j)),
    )(h, w4)


def _ctx_proj(hc, w4):
    def body(h_ref, w0_ref, w1_ref, p_ref):
        hv = h_ref[...]
        p_ref[:, 0:DA] = jnp.dot(hv, w0_ref[:, DA:2 * DA], preferred_element_type=f32)
        p_ref[:, DA:2 * DA] = jnp.dot(hv, w1_ref[:, 0:DA], preferred_element_type=f32)

    return pl.pallas_call(
        body, name="ctx_proj", out_shape=SDS((L, 2 * DA), f32), grid=(1,),
        in_specs=[pl.BlockSpec((L, D), lambda i: (0, 0)), pl.BlockSpec((None, D, D), lambda i: (0, 0, 0)),
                  pl.BlockSpec((None, D, D), lambda i: (1, 0, 0))],
        out_specs=pl.BlockSpec((L, 2 * DA), lambda i: (0, 0)),
    )(hc, w4, w4)


def _head_ones():
    r = lax.broadcasted_iota(i32, (DA, DA), 0) // DH
    c = lax.broadcasted_iota(i32, (DA, DA), 1) // DH
    return (r == c).astype(bf16)


def _head_sum(v, ones_bd):
    hi = v.astype(bf16)
    lo = (v - hi.astype(f32)).astype(bf16)
    return jnp.dot(hi, ones_bd, preferred_element_type=f32) + jnp.dot(lo, ones_bd, preferred_element_type=f32)


def _swap16(v):
    lane = lax.broadcasted_iota(i32, v.shape, 1)
    return jnp.where((lane & 31) < 16, pltpu.roll(v, DA - 16, 1), pltpu.roll(v, 16, 1))


def _rope_block(ct_ref, rt_ref, tm):
    rows = [jnp.tile(rt_ref[8 * j:8 * j + 8, :], (GW // 8, 1)) for j in range(tm // GW)]
    return jnp.tile(ct_ref[...], (tm // GW, 1)) + jnp.concatenate(rows, axis=0)


def _rope_specs(tm):
    col = pl.BlockSpec((GW, DA), lambda i: (0, 0))
    row = pl.BlockSpec((8 * tm // GW, DA), lambda i: (i, 0))
    return [col, row, col, row]


def _qk_prep(p, gq, gk, rope):
    tm = 256

    def body(qk_ref, v_ref, gq_ref, gk_ref, cc_ref, cr_ref, sc_ref, sr_ref, qr_ref, qp_ref, kr_ref, vh_ref):
        ones_bd = _head_ones()
        cs, sn = _rope_block(cc_ref, cr_ref, tm), _rope_block(sc_ref, sr_ref, tm)
        q = qk_ref[:, 0:DA]
        k = qk_ref[:, DA:2 * DA]
        yq = (q * lax.rsqrt(_head_sum(q * q, ones_bd) * (1.0 / DH) + RMS_EPS)) * gq_ref[...]
        yk = (k * lax.rsqrt(_head_sum(k * k, ones_bd) * (1.0 / DH) + RMS_EPS)) * gk_ref[...]
        qr = (yq * cs + _swap16(yq) * sn) * QK_SCALE
        qp = yq * QK_SCALE
        kr = yk * cs + _swap16(yk) * sn
        vv = v_ref[...]
        for hh in range(H):
            sl = slice(hh * DH, (hh + 1) * DH)
            qr_ref[hh] = qr[:, sl].astype(bf16)
            qp_ref[hh] = qp[:, sl].astype(bf16)
            kr_ref[hh] = kr[:, sl].astype(bf16)
            vh_ref[hh] = vv[:, sl].astype(bf16)

    hm = SDS((H, S, DH), bf16)
    hspec = pl.BlockSpec((H, tm, DH), lambda i: (0, i, 0))
    fixed = lambda i: (0, 0)
    return pl.pallas_call(
        body, name="qk_prep", out_shape=(hm, hm, hm, hm), grid=(S // tm,),
        in_specs=[pl.BlockSpec((tm, 2 * DA), lambda i: (i, 0)), pl.BlockSpec((tm, DA), lambda i: (i, 2)),
                  pl.BlockSpec((1, DA), fixed), pl.BlockSpec((1, DA), fixed)] + _rope_specs(tm),
        out_specs=(hspec, hspec, hspec, hspec),
    )(p, p, gq, gk, *rope)


def _ctx_prep(pc, gk):
    def body(p_ref, gk_ref, kc_ref, vc_ref):
        ones_bd = _head_ones()
        k = p_ref[:, 0:DA]
        yk = (k * lax.rsqrt(_head_sum(k * k, ones_bd) * (1.0 / DH) + RMS_EPS)) * gk_ref[...]
        vv = p_ref[:, DA:2 * DA]
        for hh in range(H):
            sl = slice(hh * DH, (hh + 1) * DH)
            kc_ref[hh] = yk[:, sl].astype(bf16)
            vc_ref[hh] = vv[:, sl].astype(bf16)

    hm = SDS((H, L, DH), bf16)
    return pl.pallas_call(
        body, name="ctx_prep", out_shape=(hm, hm), in_specs=[VMEM_SPEC, VMEM_SPEC], out_specs=(VMEM_SPEC, VMEM_SPEC),
    )(pc, gk)


def _tile_pieces():
    out = []
    for (i0, u0) in TILE_GEOM:
        rows = []
        for j in range(2):
            i = i0 + j
            rs = _row_start(i)
            rows.append([(u0 + u - i + WIN_H - 1) if rs <= u0 + u < rs + WIN_H else None for u in range(KR)])
        out.append(rows)
    return out


def _bias_prep(rpb_pad):
    pieces = _tile_pieces()

    def body(r_ref, o_ref):
        rp = r_ref[...]
        xs = jnp.broadcast_to(rp[:, None, :], (N_DR, GW, 128)).reshape(N_DR * GW, 128)
        row = lax.broadcasted_iota(i32, xs.shape, 0)
        lane = lax.broadcasted_iota(i32, xs.shape, 1)
        for b in range(6):
            xs = jnp.where(((row >> b) & 1) == 1, pltpu.roll(xs, 1 << b, 1), xs)
        xs = pltpu.roll(xs, 128 - (WIN_W - 1), 1)
        q = row & (GW - 1)
        c0 = jnp.clip(q - WIN_W // 2, 0, GW - WIN_W)
        xs = jnp.where((lane >= c0) & (lane < c0 + WIN_W), xs, NEG)
        neg = jnp.full((GW, GW), NEG, f32)
        for t in range(NT):
            for j in range(2):
                for u in range(KR):
                    dr = pieces[t][j][u]
                    piece = neg if dr is None else xs[dr * GW:(dr + 1) * GW, 0:GW]
                    o_ref[t, j * GW:(j + 1) * GW, u * GW:(u + 1) * GW] = piece

    return pl.pallas_call(
        body, name="bias_prep", out_shape=SDS((H, NT, QB, KB), f32), grid=(H,),
        in_specs=[pl.BlockSpec((None, N_DR, 128), lambda h: (h, 0, 0))],
        out_specs=pl.BlockSpec((None, NT, QB, KB), lambda h: (h, 0, 0, 0)),
    )(rpb_pad)


def _block_geom(b):
    qs = b * QB
    ks = min(max(2 * b - 4, 0), ROWS - KR) * GW
    t = b if b < 2 else (b - (NQB - NT) if b > NQB - 3 else 2)
    return qs, ks, t


def _tt(a, b):
    return lax.dot_general(a, b, (((1,), (1,)), ((), ())), preferred_element_type=f32)


def _tn(a, b):
    return lax.dot_general(a, b, (((0,), (0,)), ((), ())), preferred_element_type=f32)


def _softmax_block(qr, qp, k, kc, bias):
    s_lat = _tt(qr, k) + bias
    s_ctx = _tt(qp, kc)
    m = jnp.maximum(jnp.max(s_lat, axis=-1, keepdims=True), jnp.max(s_ctx, axis=-1, keepdims=True))
    e_lat = jnp.exp(s_lat - m)
    e_ctx = jnp.exp(s_ctx - m)
    inv = 1.0 / (jnp.sum(e_lat, axis=-1, keepdims=True) + jnp.sum(e_ctx, axis=-1, keepdims=True))
    return e_lat * inv, e_ctx * inv


def _attn_fwd(qr, qp, kr, vh, kc, vc, btb):
    def body(qr_ref, qp_ref, kr_ref, v_ref, kc_ref, vc_ref, bt_ref, o_ref):
        kcv, vcv = kc_ref[...], vc_ref[...]

        def blk(b, carry):
            qs, ks, t = _block_geom(b)
            p_lat, p_ctx = _softmax_block(qr_ref[pl.ds(qs, QB), :], qp_ref[pl.ds(qs, QB), :],
                                          kr_ref[pl.ds(ks, KB), :], kcv, bt_ref[t])
            o = jnp.dot(p_lat.astype(bf16), v_ref[pl.ds(ks, KB), :], preferred_element_type=f32)
            o_ref[pl.ds(qs, QB), :] = o + jnp.dot(p_ctx.astype(bf16), vcv, preferred_element_type=f32)
            return carry

        for b in range(NQB):
            blk(b, 0)

    sq = pl.BlockSpec((None, S, DH), lambda h: (h, 0, 0))
    sc = pl.BlockSpec((None, L, DH), lambda h: (h, 0, 0))
    return pl.pallas_call(
        body, name="attn_fwd", out_shape=SDS((H, S, DH), f32), grid=(H,),
        in_specs=[sq, sq, sq, sq, sc, sc, pl.BlockSpec((None, NT, QB, KB), lambda h: (h, 0, 0, 0))],
        out_specs=sq, compiler_params=_cp(40),
    )(qr, qp, kr, vh, kc, vc, btb)


def _attn_gate(o, p):
    tm = 256

    def body(o_ref, za_ref, a_ref):
        sz = _silu(za_ref[...])
        for hh in range(H):
            sl = slice(hh * DH, (hh + 1) * DH)
            a_ref[:, sl] = (o_ref[hh] * sz[:, sl]).astype(bf16)

    return pl.pallas_call(
        body, name="attn_gate", out_shape=SDS((S, DA), bf16), grid=(S // tm,),
        in_specs=[pl.BlockSpec((H, tm, DH), lambda i: (0, i, 0)), pl.BlockSpec((tm, DA), lambda i: (i, 3))],
        out_specs=pl.BlockSpec((tm, DA), lambda i: (i, 0)),
    )(o, p)


def _shift_rows(v, down):
    n = v.shape[0]
    row = lax.broadcasted_iota(i32, v.shape, 0)
    if down:
        return jnp.where(row == 0, 0.0, pltpu.roll(v, 1, 0))
    return jnp.where(row == n - 1, 0.0, pltpu.roll(v, n - 1, 0))


def _conv_specs():
    col = lambda off: pl.BlockSpec((S, 128), lambda i, off=off: (0, off + i))
    return [col(16), col(20), col(24), col(28), pl.BlockSpec((3, 128), lambda i: (0, i)),
            pl.BlockSpec((1, 128), lambda i: (0, i))]


def _conv_fwd(p, conv_w, conv_b):
    def body(u_ref, bg_ref, cg_ref, zc_ref, w_ref, b_ref, o_ref):
        cu = cg_ref[...] * u_ref[...]
        cv = b_ref[...] + _shift_rows(cu, True) * w_ref[0:1, :]
        cv = cv + cu * w_ref[1:2, :]
        cv = cv + _shift_rows(cu, False) * w_ref[2:3, :]
        o_ref[...] = ((bg_ref[...] * cv) * _silu(zc_ref[...])).astype(bf16)

    return pl.pallas_call(
        body, name="conv_fwd", out_shape=SDS((S, DC), bf16), grid=(DC // 128,),
        in_specs=_conv_specs(), out_specs=pl.BlockSpec((S, 128), lambda i: (0, i)), compiler_params=_cp(40),
    )(p, p, p, p, conv_w, conv_b)


def _out_proj_loss(attn_g, conv_g, w_out, xx, tgt, mrow, b_ada):
    tm = 256

    def body(a_ref, c_ref, w_ref, x_ref, t_ref, m_ref, b_ref, dy_ref, dmix_ref, gwo_ref, dgate_ref, loss_ref):
        k = pl.program_id(0)

        @pl.when(k == 0)
        def _():
            gwo_ref[...] = jnp.zeros_like(gwo_ref)
            dgate_ref[...] = jnp.zeros_like(dgate_ref)
            loss_ref[0, 0] = 0.0

        gate = m_ref[:, 2 * D:3 * D] + b_ref[:, 2 * D:3 * D]
        av, cv = a_ref[...], c_ref[...]
        mo = jnp.dot(av, w_ref[0:DA, :], preferred_element_type=f32)
        mo = mo + jnp.dot(cv, w_ref[DA:DA + DC, :], preferred_element_type=f32)
        y = x_ref[...] + gate * mo
        diff = y - t_ref[...]
        loss_ref[0, 0] += jnp.sum(diff * diff)
        dy = diff * (1.0 / D)
        dy_ref[...] = dy
        dgate_ref[...] += jnp.sum(dy * mo, axis=0, keepdims=True)
        dmo = (dy * gate).astype(bf16)
        dmix_ref[...] = _tt(dmo, w_ref[...])
        gwo_ref[0:DA, :] += _tn(av, dmo)
        gwo_ref[DA:DA + DC, :] += _tn(cv, dmo)

    row = lambda i: (i, 0)
    fixed = lambda i: (0, 0)
    return pl.pallas_call(
        body, name="out_proj_loss",
        out_shape=(SDS((S, D), f32), SDS((S, D), f32), SDS((D, D), f32), SDS((1, D), f32), SDS((1, 1), f32)),
        grid=(S // tm,),
        in_specs=[pl.BlockSpec((tm, DA), row), pl.BlockSpec((tm, DC), row), pl.BlockSpec((D, D), fixed),
                  pl.BlockSpec((tm, D), row), pl.BlockSpec((tm, D), row), pl.BlockSpec((1, 3 * D), fixed),
                  pl.BlockSpec((1, 3 * D), fixed)],
        out_specs=(pl.BlockSpec((tm, D), row), pl.BlockSpec((tm, D), row), pl.BlockSpec((D, D), fixed),
                   pl.BlockSpec((1, D), fixed), SMEM_SPEC),
        compiler_params=_cp(48, dimension_semantics=("arbitrary",)),
    )(attn_g, conv_g, w_out, xx, tgt, mrow, b_ada)


def _conv_bwd(dmix, p, conv_w, conv_b):
    def body(d_ref, u_ref, bg_ref, cg_ref, zc_ref, w_ref, b_ref, du_ref, dbg_ref, dcg_ref, dzc_ref, gw_ref, gb_ref):
        dconv = d_ref[...]
        u, bg, cg, zc = u_ref[...], bg_ref[...], cg_ref[...], zc_ref[...]
        w0, w1, w2 = w_ref[0:1, :], w_ref[1:2, :], w_ref[2:3, :]
        cu = cg * u
        cu_m, cu_p = _shift_rows(cu, True), _shift_rows(cu, False)
        cv = b_ref[...] + cu_m * w0
        cv = cv + cu * w1
        cv = cv + cu_p * w2
        sz = _silu(zc)
        dbg_ref[...] = ((dconv * sz) * cv).astype(bf16)
        dzc_ref[...] = ((dconv * (bg * cv)) * _dsilu(zc)).astype(bf16)
        dcv = (dconv * sz) * bg
        gb_ref[...] = jnp.sum(dcv, axis=0, keepdims=True)
        gw_ref[0:1, :] = jnp.sum(dcv * cu_m, axis=0, keepdims=True)
        gw_ref[1:2, :] = jnp.sum(dcv * cu, axis=0, keepdims=True)
        gw_ref[2:3, :] = jnp.sum(dcv * cu_p, axis=0, keepdims=True)
        gw_ref[3:8, :] = jnp.zeros((5, 128), f32)
        dcu = _shift_rows(dcv, False) * w0 + dcv * w1 + _shift_rows(dcv, True) * w2
        dcg_ref[...] = (dcu * u).astype(bf16)
        du_ref[...] = (dcu * cg).astype(bf16)

    piece = SDS((S, DC), bf16)
    ospec = pl.BlockSpec((S, 128), lambda i: (0, i))
    return pl.pallas_call(
        body, name="conv_bwd", out_shape=(piece, piece, piece, piece, SDS((8, DC), f32), SDS((1, DC), f32)),
        grid=(DC // 128,),
        in_specs=[pl.BlockSpec((S, 128), lambda i: (0, 4 + i))] + _conv_specs(),
        out_specs=(ospec, ospec, ospec, ospec, pl.BlockSpec((8, 128), lambda i: (0, i)),
                   pl.BlockSpec((1, 128), lambda i: (0, i))),
        compiler_params=_cp(48),
    )(dmix, p, p, p, p, conv_w, conv_b)


def _gate_bwd(dmix, p, o):
    tm = 256

    def body(d_ref, za_ref, o_ref, do_ref, dza_ref):
        za = za_ref[...]
        dattn = d_ref[...]
        a = dattn * _silu(za)
        bb = dattn * _dsilu(za)
        for hh in range(H):
            sl = slice(hh * DH, (hh + 1) * DH)
            do_ref[hh] = a[:, sl].astype(bf16)
            dza_ref[:, sl] = (bb[:, sl] * o_ref[hh]).astype(bf16)

    hspec = pl.BlockSpec((H, tm, DH), lambda i: (0, i, 0))
    return pl.pallas_call(
        body, name="gate_bwd", out_shape=(SDS((H, S, DH), bf16), SDS((S, DA), bf16)), grid=(S // tm,),
        in_specs=[pl.BlockSpec((tm, DA), lambda i: (i, 0)), pl.BlockSpec((tm, DA), lambda i: (i, 3)), hspec],
        out_specs=(hspec, pl.BlockSpec((tm, DA), lambda i: (i, 0))),
    )(dmix, p, o)


def _attn_bwd(qr, qp, kr, vh, kc, vc, btb, do):
    def body(qr_ref, qp_ref, kr_ref, v_ref, kc_ref, vc_ref, bt_ref, do_ref,
             dqr_ref, dqp_ref, dkr_ref, dv_ref, dkc_ref, dvc_ref, dbt_ref):
        kcv, vcv = kc_ref[...], vc_ref[...]
        dkr_ref[...] = jnp.zeros_like(dkr_ref)
        dv_ref[...] = jnp.zeros_like(dv_ref)
        dkc_ref[...] = jnp.zeros_like(dkc_ref)
        dvc_ref[...] = jnp.zeros_like(dvc_ref)
        dbt_ref[...] = jnp.zeros_like(dbt_ref)

        def blk(b, carry):
            qs, ks, t = _block_geom(b)
            qrb, qpb = qr_ref[pl.ds(qs, QB), :], qp_ref[pl.ds(qs, QB), :]
            kb, vb = kr_ref[pl.ds(ks, KB), :], v_ref[pl.ds(ks, KB), :]
            dob = do_ref[pl.ds(qs, QB), :]
            p_lat, p_ctx = _softmax_block(qrb, qpb, kb, kcv, bt_ref[t])
            dp_lat = _tt(dob, vb)
            dp_ctx = _tt(dob, vcv)
            delta = jnp.sum(p_lat * dp_lat, axis=-1, keepdims=True) + jnp.sum(p_ctx * dp_ctx, axis=-1, keepdims=True)
            ds_lat = p_lat * (dp_lat - delta)
            ds_ctx = p_ctx * (dp_ctx - delta)
            dbt_ref[t] += ds_lat
            dsb_lat, dsb_ctx = ds_lat.astype(bf16), ds_ctx.astype(bf16)
            dqr_ref[pl.ds(qs, QB), :] = jnp.dot(dsb_lat, kb, preferred_element_type=f32)
            dqp_ref[pl.ds(qs, QB), :] = jnp.dot(dsb_ctx, kcv, preferred_element_type=f32)
            dkr_ref[pl.ds(ks, KB), :] += _tn(dsb_lat, qrb)
            dv_ref[pl.ds(ks, KB), :] += _tn(p_lat.astype(bf16), dob)
            dkc_ref[...] += _tn(dsb_ctx, qpb)
            dvc_ref[...] += _tn(p_ctx.astype(bf16), dob)
            return carry

        for b in range(NQB):
            blk(b, 0)

    sq = pl.BlockSpec((None, S, DH), lambda h: (h, 0, 0))
    sc = pl.BlockSpec((None, L, DH), lambda h: (h, 0, 0))
    sb = pl.BlockSpec((None, NT, QB, KB), lambda h: (h, 0, 0, 0))
    big, ctxs = SDS((H, S, DH), f32), SDS((H, L, DH), f32)
    return pl.pallas_call(
        body, name="attn_bwd", out_shape=(big, big, big, big, ctxs, ctxs, SDS((H, NT, QB, KB), f32)), grid=(H,),
        in_specs=[sq, sq, sq, sq, sc, sc, sb, sq], out_specs=(sq, sq, sq, sq, sc, sc, sb), compiler_params=_cp(48),
    )(qr, qp, kr, vh, kc, vc, btb, do)


def _bias_bwd(dbtb):
    pieces = _tile_pieces()

    def body(d_ref, o_ref, scr):
        scr[...] = jnp.zeros_like(scr)
        acc = [None] * N_DR
        for t in range(NT):
            for j in range(2):
                for u in range(KR):
                    dr = pieces[t][j][u]
                    if dr is None:
                        continue
                    piece = d_ref[t, j * GW:(j + 1) * GW, u * GW:(u + 1) * GW]
                    acc[dr] = piece if acc[dr] is None else acc[dr] + piece
        for dr in range(N_DR):
            scr[dr * GW:(dr + 1) * GW, 0:GW] = acc[dr]
        xs = pltpu.roll(scr[...], WIN_W - 1, 1)
        row = lax.broadcasted_iota(i32, xs.shape, 0)
        for b in range(6):
            xs = jnp.where(((row >> b) & 1) == 1, pltpu.roll(xs, 128 - (1 << b), 1), xs)
        o_ref[...] = jnp.sum(xs.reshape(N_DR, GW, 128), axis=1)

    return pl.pallas_call(
        body, name="bias_bwd", out_shape=SDS((H, N_DR, 128), f32), grid=(H,),
        in_specs=[pl.BlockSpec((None, NT, QB, KB), lambda h: (h, 0, 0, 0))],
        out_specs=pl.BlockSpec((None, N_DR, 128), lambda h: (h, 0, 0)),
        scratch_shapes=[pltpu.VMEM((N_DR * GW, 128), f32)],
    )(dbtb)


def _merge_heads(ref):
    return jnp.concatenate([ref[hh] for hh in range(H)], axis=1)


def _head_norm_bwd(xraw, gain, dy, ones_bd):
    r = lax.rsqrt(_head_sum(xraw * xraw, ones_bd) * (1.0 / DH) + RMS_EPS)
    xh = xraw * r
    gdy = dy * gain
    dx = r * (gdy - xh * (_head_sum(xh * gdy, ones_bd) * (1.0 / DH)))
    return dx, jnp.sum(dy * xh, axis=0, keepdims=True)


def _qk_bwd(dqr, dqp, dkr, dvh, p, gq, gk, rope):
    tm = 256

    def body(dqr_ref, dqp_ref, dkr_ref, dv_ref, qk_ref, gq_ref, gk_ref, cc_ref, cr_ref, sc_ref, sr_ref,
             dq_ref, dk_ref, dvo_ref, ggq_ref, ggk_ref):
        @pl.when(pl.program_id(0) == 0)
        def _():
            ggq_ref[...] = jnp.zeros_like(ggq_ref)
            ggk_ref[...] = jnp.zeros_like(ggk_ref)

        ones_bd = _head_ones()
        cs, sn = _rope_block(cc_ref, cr_ref, tm), _rope_block(sc_ref, sr_ref, tm)
        a = _merge_heads(dqr_ref)
        dyq = ((a * cs - _swap16(a) * sn) + _merge_heads(dqp_ref)) * QK_SCALE
        bk = _merge_heads(dkr_ref)
        dyk = bk * cs - _swap16(bk) * sn
        dq, gq_part = _head_norm_bwd(qk_ref[:, 0:DA], gq_ref[...], dyq, ones_bd)
        dk, gk_part = _head_norm_bwd(qk_ref[:, DA:2 * DA], gk_ref[...], dyk, ones_bd)
        dq_ref[...] = dq.astype(bf16)
        dk_ref[...] = dk.astype(bf16)
        dvo_ref[...] = _merge_heads(dv_ref).astype(bf16)
        ggq_ref[...] += gq_part
        ggk_ref[...] += gk_part

    hspec = pl.BlockSpec((H, tm, DH), lambda i: (0, i, 0))
    row = pl.BlockSpec((tm, DA), lambda i: (i, 0))
    fixed = pl.BlockSpec((1, DA), lambda i: (0, 0))
    piece = SDS((S, DA), bf16)
    return pl.pallas_call(
        body, name="qk_bwd", out_shape=(piece, piece, piece, SDS((1, DA), f32), SDS((1, DA), f32)), grid=(S // tm,),
        in_specs=[hspec, hspec, hspec, hspec, pl.BlockSpec((tm, 2 * DA), lambda i: (i, 0)), fixed, fixed]
        + _rope_specs(tm),
        out_specs=(row, row, row, fixed, fixed), compiler_params=_cp(40, dimension_semantics=("arbitrary",)),
    )(dqr, dqp, dkr, dvh, p, gq, gk, *rope)


def _ctx_bwd(dkc, dvc, pc, gk):
    def body(dkc_ref, dvc_ref, p_ref, gk_ref, dk_ref, dv_ref, ggk_ref):
        ones_bd = _head_ones()
        dk, gk_part = _head_norm_bwd(p_ref[:, 0:DA], gk_ref[...], _merge_heads(dkc_ref), ones_bd)
        dk_ref[...] = dk.astype(bf16)
        dv_ref[...] = _merge_heads(dvc_ref).astype(bf16)
        ggk_ref[...] = gk_part

    piece = SDS((L, DA), bf16)
    return pl.pallas_call(
        body, name="ctx_bwd", out_shape=(piece, piece, SDS((1, DA), f32)), in_specs=[VMEM_SPEC] * 4,
        out_specs=(VMEM_SPEC,) * 3,
    )(dkc, dvc, pc, gk)


def _grad_w_in_block(h, pa, pb, name, hc=None, pc=None, ctx_cols=None):
    tm = 512

    def body(*refs):
        if hc is None:
            h_ref, a_ref, b_ref, g_ref = refs
        else:
            h_ref, a_ref, b_ref, hc_ref, pc_ref, g_ref = refs

        @pl.when(pl.program_id(0) == 0)
        def _():
            g_ref[...] = jnp.zeros_like(g_ref)
            if hc is not None:
                g_ref[:, ctx_cols[0]:ctx_cols[1]] = _tn(hc_ref[...], pc_ref[...])

        hv = h_ref[...]
        g_ref[:, 0:DA] += _tn(hv, a_ref[...])
        g_ref[:, DA:2 * DA] += _tn(hv, b_ref[...])

    row = lambda i: (i, 0)
    fixed = lambda i: (0, 0)
    in_specs = [pl.BlockSpec((tm, D), row), pl.BlockSpec((tm, DA), row), pl.BlockSpec((tm, DA), row)]
    args = [h, pa, pb]
    if hc is not None:
        in_specs += [pl.BlockSpec((L, D), fixed), pl.BlockSpec((L, DA), fixed)]
        args += [hc, pc]
    return pl.pallas_call(
        body, name=name, out_shape=SDS((D, D), f32), grid=(S // tm,), in_specs=in_specs,
        out_specs=pl.BlockSpec((D, D), fixed), compiler_params=_cp(40, dimension_semantics=("arbitrary",)),
    )(*args)


def _norm_mod_bwd(x, dh, g, scale):
    r = lax.rsqrt(jnp.mean(x * x, axis=-1, keepdims=True) + RMS_EPS)
    xh = x * r
    y = xh * g
    dshift = jnp.sum(dh, axis=0, keepdims=True)
    dscale = jnp.sum(dh * y, axis=0, keepdims=True)
    dyn = dh * (1.0 + scale)
    dg = jnp.sum(dyn * xh, axis=0, keepdims=True)
    gdy = dyn * g
    dx = r * (gdy - xh * jnp.mean(xh * gdy, axis=-1, keepdims=True))
    return dx, dshift, dscale, dg


def _dh_grad_x(pieces, w4, xx, dy, norm_g, mrow, b_ada):
    tm = 256

    def body(*refs):
        p_refs = refs[:8]
        w_ref, x_ref, dy_ref, g_ref, m_ref, b_ref, gx_ref, dsh_ref, dsc_ref, dg_ref = refs[8:]

        @pl.when(pl.program_id(0) == 0)
        def _():
            dsh_ref[...] = jnp.zeros_like(dsh_ref)
            dsc_ref[...] = jnp.zeros_like(dsc_ref)
            dg_ref[...] = jnp.zeros_like(dg_ref)

        dh = None
        for j in range(4):
            for half in range(2):
                term = _tt(p_refs[2 * j + half][...], w_ref[j, :, half * DA:(half + 1) * DA])
                dh = term if dh is None else dh + term
        scale = m_ref[:, D:2 * D] + b_ref[:, D:2 * D]
        dx, dshift, dscale, dg = _norm_mod_bwd(x_ref[...], dh, g_ref[...], scale)
        gx_ref[...] = dy_ref[...] + dx
        dsh_ref[...] += dshift
        dsc_ref[...] += dscale
        dg_ref[...] += dg

    row = lambda i: (i, 0)
    fixed = lambda i: (0, 0)
    vec = SDS((1, D), f32)
    return pl.pallas_call(
        body, name="dh_grad_x", out_shape=(SDS((S, D), f32), vec, vec, vec), grid=(S // tm,),
        in_specs=[pl.BlockSpec((tm, DA), row)] * 8 + [pl.BlockSpec((4, D, D), lambda i: (0, 0, 0)),
                                                      pl.BlockSpec((tm, D), row), pl.BlockSpec((tm, D), row),
                                                      pl.BlockSpec((1, D), fixed), pl.BlockSpec((1, 3 * D), fixed),
                                                      pl.BlockSpec((1, 3 * D), fixed)],
        out_specs=(pl.BlockSpec((tm, D), row), pl.BlockSpec((1, D), fixed), pl.BlockSpec((1, D), fixed),
                   pl.BlockSpec((1, D), fixed)),
        compiler_params=_cp(56, dimension_semantics=("arbitrary",)),
    )(*pieces, w4, xx, dy, norm_g, mrow, b_ada)


def _dhc_sums(dkc_raw, dvc_m, w4, ctx2, norm_g, mrow_c, b_ada):
    def body(dk_ref, dv_ref, w0_ref, w1_ref, x_ref, g_ref, m_ref, b_ref, dsh_ref, dsc_ref, dg_ref):
        dh = _tt(dk_ref[...], w0_ref[:, DA:2 * DA]) + _tt(dv_ref[...], w1_ref[:, 0:DA])
        scale = m_ref[:, D:2 * D] + b_ref[:, D:2 * D]
        _, dshift, dscale, dg = _norm_mod_bwd(x_ref[...], dh, g_ref[...], scale)
        dsh_ref[...] = dshift
        dsc_ref[...] = dscale
        dg_ref[...] = dg

    fixed = lambda i: (0, 0)
    vec = SDS((1, D), f32)
    vspec = pl.BlockSpec((1, D), fixed)
    return pl.pallas_call(
        body, name="dhc_sums", out_shape=(vec, vec, vec), grid=(1,),
        in_specs=[pl.BlockSpec((L, DA), fixed), pl.BlockSpec((L, DA), fixed),
                  pl.BlockSpec((None, D, D), lambda i: (0, 0, 0)), pl.BlockSpec((None, D, D), lambda i: (1, 0, 0)),
                  pl.BlockSpec((L, D), fixed), vspec, pl.BlockSpec((1, 3 * D), fixed), pl.BlockSpec((1, 3 * D), fixed)],
        out_specs=(vspec, vspec, vspec), compiler_params=_cp(32),
    )(dkc_raw, dvc_m, w4, w4, ctx2, norm_g, mrow_c, b_ada)


def _rope_tables():
    nf = DH // 4
    inv = ROPE_THETA ** (-jnp.arange(nf, dtype=f32) / nf)
    ang_c = jnp.arange(GW, dtype=i32).astype(f32)[:, None] * inv
    ang_r = jnp.arange(ROWS, dtype=i32).astype(f32)[:, None] * inv
    zc, zr = jnp.zeros((GW, 2 * nf), f32), jnp.zeros((ROWS, 2 * nf), f32)
    ct_cos = jnp.tile(jnp.concatenate([zc, jnp.cos(ang_c), jnp.cos(ang_c)], axis=1), (1, H))
    ct_sin = jnp.tile(jnp.concatenate([zc, -jnp.sin(ang_c), jnp.sin(ang_c)], axis=1), (1, H))
    rt_cos = jnp.tile(jnp.concatenate([jnp.cos(ang_r), jnp.cos(ang_r), zr], axis=1), (1, H))
    rt_sin = jnp.tile(jnp.concatenate([-jnp.sin(ang_r), jnp.sin(ang_r), zr], axis=1), (1, H))
    rep8 = lambda t: jnp.broadcast_to(t[:, None, :], (ROWS, 8, DA)).reshape(ROWS * 8, DA)
    return ct_cos, rep8(rt_cos), ct_sin, rep8(rt_sin)


def _local_step(xx, ctx2, tgt, mrow, mrow_c, b_ada, norm_g, weights, q_norm_g, k_norm_g, rpb2, conv_w_full, conv_b,
                mid=None):
    gq = jnp.tile(q_norm_g, (1, H))
    gk = jnp.tile(k_norm_g, (1, H))
    rope = _rope_tables()
    rpb_pad = jnp.pad(rpb2, ((0, 0), (0, 0), (0, 128 - N_DC)))

    h = _prenorm(xx, norm_g, mrow, b_ada, 256, "prenorm_x")
    hc = _prenorm(ctx2, norm_g, mrow_c, b_ada, L, "prenorm_ctx")
    btb = _bias_prep(rpb_pad)
    w4, w_out_full = weights(btb)
    p = _in_proj(h, w4)
    pc = _ctx_proj(hc, w4)
    qr, qp, kr, vh = _qk_prep(p, gq, gk, rope)
    kc, vc = _ctx_prep(pc, gk)
    o = _attn_fwd(qr, qp, kr, vh, kc, vc, btb)
    attn_g = _attn_gate(o, p)
    conv_g = _conv_fwd(p, conv_w_full, conv_b)
    dy, dmix, g_w_out, dgate, loss_sum = _out_proj_loss(attn_g, conv_g, w_out_full, xx, tgt, mrow, b_ada)

    du, dbg, dcg, dzc, g_conv_w, g_conv_b = _conv_bwd(dmix, p, conv_w_full, conv_b)
    do, dza = _gate_bwd(dmix, p, o)
    dqr, dqp, dkr, dvh, dkc, dvc, dbtb = _attn_bwd(qr, qp, kr, vh, kc, vc, btb, do)
    g_rpb = _bias_bwd(dbtb)
    dq, dk, dv, g_gq, g_gk = _qk_bwd(dqr, dqp, dkr, dvh, p, gq, gk, rope)
    dkc_raw, dvc_m, g_gk_c = _ctx_bwd(dkc, dvc, pc, gk)
    g_w_in = [
        _grad_w_in_block(h, dq, dk, "grad_w_in_0", hc, dkc_raw, (DA, 2 * DA)),
        _grad_w_in_block(h, dv, dza, "grad_w_in_1", hc, dvc_m, (0, DA)),
        _grad_w_in_block(h, du, dbg, "grad_w_in_2"),
        _grad_w_in_block(h, dcg, dzc, "grad_w_in_3"),
    ]
    b_ada_late = b_ada if mid is None else mid(g_w_in, g_w_out, b_ada)
    grad_x, dshift, dscale, dng = _dh_grad_x([dq, dk, dv, dza, du, dbg, dcg, dzc], w4, xx, dy, norm_g, mrow,
                                             b_ada_late)
    dshift_c, dscale_c, dng_c = _dhc_sums(dkc_raw, dvc_m, w4, ctx2, norm_g, mrow_c, b_ada_late)
    return dict(loss_sum=loss_sum, grad_x=grad_x, g_w_in=g_w_in, g_w_out=g_w_out, g_conv_w=g_conv_w,
                g_conv_b=g_conv_b, g_rpb=g_rpb, g_gq=g_gq, g_gk=g_gk, g_gk_c=g_gk_c, dshift=dshift, dscale=dscale,
                dgate=dgate, dng=dng, dshift_c=dshift_c, dscale_c=dscale_c, dng_c=dng_c)


def _pair_sum_w_in(gs, rs, cvec):
    tr = 128

    def body(c_ref, g0, g1, g2, g3, r0, r1, r2, r3, t32_ref, tb_ref):
        for q, (g_ref, r_ref) in enumerate(((g0, r0), (g1, r1), (g2, r2), (g3, r3))):
            t = g_ref[...] + r_ref[...]
            t32_ref[q] = t
            tb_ref[q] = t.astype(bf16)

    half = D // 2
    gs_spec = pl.BlockSpec((tr, D), lambda i, c: (c[0] * (half // tr) + i, 0))
    rs_spec = pl.BlockSpec((None, tr, D), lambda i, c: (0, i, 0))
    ospec = pl.BlockSpec((4, tr, D), lambda i, c: (0, i, 0))
    grid_spec = pltpu.PrefetchScalarGridSpec(num_scalar_prefetch=1, grid=(half // tr,), in_specs=[gs_spec] * 4 + [rs_spec] * 4,
                                             out_specs=(ospec, ospec))
    return pl.pallas_call(body, name="pair_sum_w_in", out_shape=(SDS((4, half, D), f32), SDS((4, half, D), bf16)),
                          grid_spec=grid_spec, compiler_params=_cp(40))(cvec, *gs, *rs)


def _pair_sum_w_out(g, r, cvec):
    hr = D // 8

    def body(c_ref, g0, g1, g2, g3, r_ref, t32_ref, tb_ref):
        for q, g_ref in enumerate((g0, g1, g2, g3)):
            t = g_ref[...] + r_ref[q]
            t32_ref[q] = t
            tb_ref[q] = t.astype(bf16)

    gspecs = [pl.BlockSpec((hr, D), lambda i, c, q=q: (2 * q + c[0], 0)) for q in range(4)]
    full = pl.BlockSpec((4, hr, D), lambda i, c: (0, 0, 0))
    grid_spec = pltpu.PrefetchScalarGridSpec(num_scalar_prefetch=1, grid=(1,), in_specs=gspecs + [full],
                                             out_specs=(full, full))
    return pl.pallas_call(body, name="pair_sum_w_out", out_shape=(SDS((4, hr, D), f32), SDS((4, hr, D), bf16)),
                          grid_spec=grid_spec)(cvec, g, g, g, g, r)


def _chip_sum(t32, r2, jvec, name):
    rows = t32.shape[1]
    tr = min(rows, 128)

    def body(j_ref, t_ref, r_ref, u_ref):
        u_ref[...] = ((t_ref[...] + r_ref[0].astype(f32)) + r_ref[1].astype(f32)) + r_ref[2].astype(f32)

    grid_spec = pltpu.PrefetchScalarGridSpec(
        num_scalar_prefetch=1, grid=(rows // tr,),
        in_specs=[pl.BlockSpec((None, tr, D), lambda i, j: (j[0], i, 0)), pl.BlockSpec((3, tr, D), lambda i, j: (0, i, 0))],
        out_specs=pl.BlockSpec((tr, D), lambda i, j: (i, 0)))
    return pl.pallas_call(body, name=name, out_shape=SDS((rows, D), f32), grid_spec=grid_spec)(jvec, t32, r2)


_PK = {}
_off = 0
for _name, _rows in (("dm", 24), ("dmc", 24), ("dng", 8), ("dng_c", 8), ("gq", 8), ("gk", 8), ("gk_c", 8),
                     ("rpb", H * N_DR), ("conv_b", 8), ("conv_w", 16), ("loss", 8)):
    _PK[_name] = (_off, _off + _rows)
    _off += _rows
PK_ROWS = _off
RS_B_ADA, RS_NORM_G, RS_GQ, RS_GK, RS_RPB, RS_CONV_B, RS_CONV_W, RS_DMC, RS_LOSS, RS_ROWS = (
    0, 24, 32, 40, 48, 168, 176, 192, 216, 224)


def _small_reduce(gathered):
    def body(g_ref, o_ref):
        tot = g_ref[0]
        for b in range(1, 8):
            tot = tot + g_ref[b]

        def rows(name):
            a, z = _PK[name]
            return tot[a:z]

        o_ref[RS_B_ADA:RS_B_ADA + 24] = rows("dm") + rows("dmc")
        o_ref[RS_NORM_G:RS_NORM_G + 8] = rows("dng") + rows("dng_c")
        gq = jnp.broadcast_to(jnp.sum(rows("gq"), axis=0, keepdims=True), (8, 128))
        gk = jnp.broadcast_to(jnp.sum(rows("gk") + rows("gk_c"), axis=0, keepdims=True), (8, 128))
        o_ref[RS_GQ:RS_GQ + 8] = gq + pltpu.roll(gq, DH, 1)
        o_ref[RS_GK:RS_GK + 8] = gk + pltpu.roll(gk, DH, 1)
        o_ref[RS_RPB:RS_RPB + H * N_DR] = rows("rpb")
        o_ref[RS_CONV_B:RS_CONV_B + 8] = rows("conv_b")
        o_ref[RS_CONV_W:RS_CONV_W + 16] = rows("conv_w")
        o_ref[RS_DMC:RS_DMC + 24] = rows("dmc")
        o_ref[RS_LOSS:RS_LOSS + 8] = rows("loss")

    return pl.pallas_call(body, name="small_reduce", out_shape=SDS((RS_ROWS, 128), f32), in_specs=[VMEM_SPEC],
                          out_specs=VMEM_SPEC)(gathered)


def _w_ada_grad(sc16, dm16s, w_ada_shard):
    def body(sc_ref, dm_ref, w_ref, g_ref, part_ref):
        dm = dm_ref[...]
        g_ref[...] = lax.dot_general(sc_ref[...], dm, (((0,), (0,)), ((), ())), precision=HIGHEST,
                                     preferred_element_type=f32)
        part_ref[...] = lax.dot_general(dm[8:16], w_ref[...], (((1,), (1,)), ((), ())), precision=HIGHEST,
                                        preferred_element_type=f32)

    ncol = w_ada_shard.shape[1]
    return pl.pallas_call(body, name="w_ada_grad", out_shape=(SDS((D, ncol), f32), SDS((8, D), f32)),
                          in_specs=[VMEM_SPEC] * 3, out_specs=(VMEM_SPEC, VMEM_SPEC), compiler_params=_cp(40),
                          )(sc16, dm16s, w_ada_shard)


def _c_ctx_grad(parts4, c_ctx_row):
    def body(p_ref, c_ref, o_ref):
        tot = ((p_ref[0] + p_ref[1]) + p_ref[2]) + p_ref[3]
        o_ref[...] = tot[0:1] * _dsilu(c_ref[...])

    return pl.pallas_call(body, name="c_ctx_grad", out_shape=SDS((1, D), f32), in_specs=[VMEM_SPEC, VMEM_SPEC],
                          out_specs=VMEM_SPEC)(parts4, c_ctx_row)


def _adamw(w, g, m, v, name):
    rows, cols = w.shape
    tr = 256 if rows % 256 == 0 else rows

    def body(w_ref, g_ref, m_ref, v_ref, d_ref, m2_ref, v2_ref):
        gv = g_ref[...]
        m2 = ADAM_B1 * m_ref[...] + (1.0 - ADAM_B1) * gv
        v2 = ADAM_B2 * v_ref[...] + (1.0 - ADAM_B2) * jnp.square(gv)
        m_hat = m2 / (1.0 - ADAM_B1 ** ADAM_STEP)
        v_hat = v2 / (1.0 - ADAM_B2 ** ADAM_STEP)
        d_ref[...] = -ADAM_LR * (m_hat / (jnp.sqrt(v_hat) + ADAM_EPS) + ADAM_WD * w_ref[...])
        m2_ref[...] = m2
        v2_ref[...] = v2

    spec = pl.BlockSpec((tr, cols), lambda i: (i, 0))
    shp = SDS((rows, cols), f32)
    return pl.pallas_call(body, name=name, out_shape=(shp, shp, shp), grid=(rows // tr,), in_specs=[spec] * 4,
                          out_specs=(spec, spec, spec))(w, g, m, v)


def _adamw_halves(w, g_mine, g_other, m, v, cvec, name):
    rows, cols = w.shape
    half = rows // 2
    tr = min(256, half)
    per_half = half // tr

    def body(c_ref, w_ref, ga_ref, gb_ref, m_ref, v_ref, g_ref, d_ref, m2_ref, v2_ref):
        in_my_half = (pl.program_id(0) // per_half) == c_ref[0]
        gv = jnp.where(in_my_half, ga_ref[...], gb_ref[...])
        g_ref[...] = gv
        m2 = ADAM_B1 * m_ref[...] + (1.0 - ADAM_B1) * gv
        v2 = ADAM_B2 * v_ref[...] + (1.0 - ADAM_B2) * jnp.square(gv)
        m_hat = m2 / (1.0 - ADAM_B1 ** ADAM_STEP)
        v_hat = v2 / (1.0 - ADAM_B2 ** ADAM_STEP)
        d_ref[...] = -ADAM_LR * (m_hat / (jnp.sqrt(v_hat) + ADAM_EPS) + ADAM_WD * w_ref[...])
        m2_ref[...] = m2
        v2_ref[...] = v2

    full = pl.BlockSpec((tr, cols), lambda i, c: (i, 0))
    part = pl.BlockSpec((tr, cols), lambda i, c: (i % per_half, 0))
    shp = SDS((rows, cols), f32)
    grid_spec = pltpu.PrefetchScalarGridSpec(num_scalar_prefetch=1, grid=(rows // tr,),
                                             in_specs=[full, part, part, full, full], out_specs=(full,) * 4)
    return pl.pallas_call(body, name=name, out_shape=(shp,) * 4, grid_spec=grid_spec)(cvec, w, g_mine, g_other, m, v)


def _rows128(a):
    return a.reshape(-1, 128)


def _pad_lanes(a):
    a2 = a.reshape(-1, a.shape[-1])
    return jnp.pad(a2, ((0, 0), (0, 128 - a2.shape[1])))


def kernel(x, c, ctx, c_ctx, w_ada, b_ada, norm_g, w_in, q_norm_g, k_norm_g, rpb, conv_w, conv_b, w_out, loss_target, m_c_ctx, m_w_ada, m_b_ada, m_norm_g, m_w_in, m_q_norm_g, m_k_norm_g, m_rpb, m_conv_w, m_conv_b, m_w_out, v_c_ctx, v_w_ada, v_b_ada, v_norm_g, v_w_in, v_q_norm_g, v_k_norm_g, v_rpb, v_conv_w, v_conv_b, v_w_out):
    xi, yi, ci = lax.axis_index("x"), lax.axis_index("y"), lax.axis_index("c")
    dev = 4 * xi + 2 * yi + ci
    chip = 2 * xi + yi
    cvec = jnp.reshape(ci, (1,)).astype(i32)
    jvec = jnp.reshape(chip, (1,)).astype(i32)
    w_ada_s = w_ada[0]
    ncol = w_ada_s.shape[1]

    c8 = _all_gather8(c.reshape(8, 128), "gather_c").reshape(8, D)
    cc = jnp.concatenate([c8, c_ctx.reshape(1, D), jnp.zeros((7, D), f32)], axis=0)
    m_shard, sc16 = _adaln_shard(cc, w_ada_s)

    conv_w_pad = jnp.pad(conv_w[0], ((0, 5), (0, 0)))
    m4, cw4 = _chip_gather([m_shard, conv_w_pad], "gather_mod")

    ssem, rsem, w4s, wo4s, token = _halves_start(
        [_cast_to_slot(w_in[0], jvec, "cast_w_in"), _cast_to_slot(w_out[0], jvec, "cast_w_out")], m4,
        "weights_ici_start")
    m_full = jnp.transpose(m4, (1, 0, 2)).reshape(16, 4 * ncol) + token[0:1, 0:1]
    mrow = lax.dynamic_slice(m_full, (dev, 0), (1, 3 * D))
    mrow_c = m_full[8:9]
    conv_w_full = jnp.transpose(cw4[:, 0:3, :], (1, 0, 2)).reshape(3, DC)

    def weights(after):
        w4w, wo4w = _halves_wait(ssem, rsem, [w4s, wo4s], after, "weights_ici_wait")
        w4f, wo4f = _halves_forward([w4w, wo4w], "weights_pair_forward")
        return w4f, wo4f.reshape(D, D)

    exchange = _exchange_copies([1, 1, 1, 1, 4])
    pending = {}

    def mid(g_w_in, g_w_out, b_ada_in):
        srcs = list(g_w_in) + [g_w_out]
        shapes = [SDS((1, D // 2, D), f32)] * 4 + [SDS((4, D // 8, D), f32)]
        out = _split_start(srcs, shapes, 8, exchange, "grad_pair_start")
        pending["ex"] = (out[0], out[1], list(out[2:7]), list(out[7:12]))
        return b_ada_in + out[12][0:1, 0:1]

    r = _local_step(x[0], ctx[0], loss_target[0], mrow, mrow_c, b_ada, norm_g, weights, q_norm_g, k_norm_g,
                    rpb[0], conv_w_full, conv_b, mid)
    dm = jnp.concatenate([r["dshift"], r["dscale"], r["dgate"]], axis=1)
    dmc = jnp.concatenate([r["dshift_c"], r["dscale_c"], jnp.zeros((1, D), f32)], axis=1)
    pack_parts = [_rows128(dm), _rows128(dmc), _rows128(r["dng"]), _rows128(r["dng_c"]), _rows128(r["g_gq"]),
                  _rows128(r["g_gk"]), _rows128(r["g_gk_c"]), r["g_rpb"].reshape(H * N_DR, 128),
                  _rows128(r["g_conv_b"]), _rows128(r["g_conv_w"][0:3]), jnp.pad(r["loss_sum"], ((0, 0), (0, 127)))]
    pack = jnp.concatenate([jnp.pad(p, ((0, -p.shape[0] % 8), (0, 0))) for p in pack_parts], axis=0)
    gathered = _all_gather8(pack, "gather_small")

    ex_ssem, ex_rsem, ex_srcs, ex_lands = pending["ex"]
    ex_g, ex = _split_wait(ex_ssem, ex_rsem, ex_srcs, ex_lands, gathered, exchange, "grad_pair_wait")
    t32, tb = _pair_sum_w_in(ex_g[0:4], ex[0:4], cvec)
    to32, tob = _pair_sum_w_out(ex_g[4], ex[4], cvec)
    sc_out = _split_start([tb, tob], [SDS((3, D // 2, D), bf16), SDS((3, D // 8, D), bf16)], 6, _scatter_copies,
                          "grad_chip_start")
    sc16 = sc16 + sc_out[6][0:1, 0:1]

    red = _small_reduce(gathered)
    g_b_ada = red[RS_B_ADA:RS_B_ADA + 24].reshape(1, 3 * D)
    g_norm_g = red[RS_NORM_G:RS_NORM_G + 8].reshape(1, D)
    g_q = red[RS_GQ:RS_GQ + 1, 0:DH]
    g_k = red[RS_GK:RS_GK + 1, 0:DH]
    g_rpb = red[RS_RPB:RS_RPB + H * N_DR]
    g_conv_b = red[RS_CONV_B:RS_CONV_B + 4].reshape(1, DC)
    g_conv_w_full = red[RS_CONV_W:RS_CONV_W + 12].reshape(3, DC)
    assert pack.shape[0] == PK_ROWS
    g_conv_w_s = lax.dynamic_slice(g_conv_w_full, (0, chip * 128), (3, 128))
    dmc_tot = red[RS_DMC:RS_DMC + 24].reshape(1, 3 * D)
    loss = red[RS_LOSS, 0] * (0.5 / D)

    a0, a1 = _PK["dm"]
    dm8 = gathered[:, a0:a1, :].reshape(8, 3 * D)
    dm16 = jnp.concatenate([dm8, dmc_tot, jnp.zeros((7, 3 * D), f32)], axis=0)
    dm16s = lax.dynamic_slice(dm16, (0, chip * ncol), (16, ncol))
    g_w_ada_s, cpart = _w_ada_grad(sc16, dm16s, w_ada_s)
    d_w_ada, nm_w_ada, nv_w_ada = _adamw(w_ada_s, g_w_ada_s, m_w_ada[0], v_w_ada[0], "adamw_w_ada")

    _, (r2, ro2) = _split_wait(sc_out[0], sc_out[1], [sc_out[2], sc_out[3]], [sc_out[4], sc_out[5]], nm_w_ada,
                               _scatter_copies, "grad_chip_wait")
    u_in = _chip_sum(t32, r2, jvec, "chip_sum_w_in")
    u_out = _chip_sum(to32, ro2, jvec, "chip_sum_w_out")

    (cparts4,) = _chip_gather([cpart], "gather_c_ctx_parts", after=u_out)
    g_c_ctx = _c_ctx_grad(cparts4, c_ctx.reshape(1, D))

    def small_pack(a_c_ctx, a_b_ada, a_norm_g, a_q, a_k, a_rpb, a_conv_w, a_conv_b):
        parts = [_rows128(a_c_ctx), _rows128(a_b_ada), _rows128(a_norm_g), _pad_lanes(a_q), _pad_lanes(a_k),
                 a_rpb, _rows128(a_conv_w), _rows128(a_conv_b)]
        return jnp.concatenate([jnp.pad(p, ((0, -p.shape[0] % 8), (0, 0))) for p in parts], axis=0)

    w_pk = small_pack(c_ctx, b_ada, norm_g, q_norm_g, k_norm_g, _pad_lanes(rpb[0]), conv_w[0], conv_b)
    g_pk = small_pack(g_c_ctx, g_b_ada, g_norm_g, g_q, g_k, g_rpb, g_conv_w_s, g_conv_b)
    m_pk = small_pack(m_c_ctx, m_b_ada, m_norm_g, m_q_norm_g, m_k_norm_g, _pad_lanes(m_rpb[0]), m_conv_w[0], m_conv_b)
    v_pk = small_pack(v_c_ctx, v_b_ada, v_norm_g, v_q_norm_g, v_k_norm_g, _pad_lanes(v_rpb[0]), v_conv_w[0], v_conv_b)
    d_pk, nm_pk, nv_pk = _adamw(w_pk, g_pk, m_pk, v_pk, "adamw_small")

    o_in, o_out = _sibling_send([u_in, u_out], "grad_pair_send")
    g_w_in_s, d_w_in, nm_w_in, nv_w_in = _adamw_halves(w_in[0], u_in, o_in, m_w_in[0], v_w_in[0], cvec, "adamw_w_in")
    g_w_out_s, d_w_out, nm_w_out, nv_w_out = _adamw_halves(w_out[0], u_out, o_out, m_w_out[0], v_w_out[0], cvec,
                                                           "adamw_w_out")

    def small_unpack(pk):
        o = 0
        out = []
        for rows, fn in ((8, lambda a: a.reshape(D)), (24, lambda a: a.reshape(1, 3 * D)), (8, lambda a: a.reshape(1, D)),
                         (1, lambda a: a[:, 0:DH]), (1, lambda a: a[:, 0:DH]),
                         (H * N_DR, lambda a: a[:, 0:N_DC].reshape(1, H, N_DR, N_DC)),
                         (3, lambda a: a.reshape(1, 3, 128)), (4, lambda a: a.reshape(1, DC))):
            out.append(fn(pk[o:o + rows]))
            o += rows + (-rows % 8)
        return out

    def ordered(small, big_w_ada, big_w_in, big_w_out):
        s_c_ctx, s_b_ada, s_norm_g, s_q, s_k, s_rpb, s_conv_w, s_conv_b = small
        return [s_c_ctx, big_w_ada[None], s_b_ada, s_norm_g, big_w_in[None], s_q, s_k, s_rpb, s_conv_w, s_conv_b,
                big_w_out[None]]

    grads = ordered(small_unpack(g_pk), g_w_ada_s, g_w_in_s, g_w_out_s)
    deltas = ordered(small_unpack(d_pk), d_w_ada, d_w_in, d_w_out)
    new_m = ordered(small_unpack(nm_pk), nm_w_ada, nm_w_in, nm_w_out)
    new_v = ordered(small_unpack(nv_pk), nv_w_ada, nv_w_in, nv_w_out)
    return (loss, r["grad_x"][None], *grads, *deltas, *new_m, *new_v)
```

```python
import functools

import jax
import jax.numpy as jnp
from jax import lax
from jax.experimental import pallas as pl
from jax.experimental.pallas import tpu as pltpu

f32, bf16, i32 = jnp.float32, jnp.bfloat16, jnp.int32
MESH = pl.DeviceIdType.MESH
HIGHEST = lax.Precision.HIGHEST

D = 1024
S = 2048
L = 256
GW = 64
ROWS = S // GW
H = 8
DH = 64
DA = H * DH
DC = 512
WIN_H, WIN_W = 8, 16
N_DR, N_DC = 2 * WIN_H - 1, 2 * WIN_W - 1
RMS_EPS = 1e-6
ROPE_THETA = 10000.0
QK_SCALE = DH ** -0.5
NEG = -1e30

QB = 128
NQB = S // QB
KR = 9
KB = KR * GW
TILE_GEOM = ((0, 0), (2, 0), (4, 0), (28, 23), (30, 23))
NT = len(TILE_GEOM)

ADAM_LR, ADAM_B1, ADAM_B2, ADAM_EPS, ADAM_WD, ADAM_STEP = 0.001, 0.9, 0.999, 1e-08, 0.01, 10

VMEM_SPEC = pl.BlockSpec(memory_space=pltpu.VMEM)
ANY_SPEC = pl.BlockSpec(memory_space=pl.ANY)
SMEM_SPEC = pl.BlockSpec(memory_space=pltpu.SMEM)
SDS = jax.ShapeDtypeStruct


def _cp(vmem_mb=None, **kw):
    if vmem_mb is not None:
        kw["vmem_limit_bytes"] = vmem_mb << 20
    return pltpu.CompilerParams(**kw)


def _silu(z):
    return z * jax.nn.sigmoid(z)


def _dsilu(z):
    sg = jax.nn.sigmoid(z)
    return sg * (1.0 + z * (1.0 - sg))


def _row_start(i):
    return min(max(i - WIN_H // 2, 0), ROWS - WIN_H)


def _my_pos():
    return lax.axis_index("x"), lax.axis_index("y"), lax.axis_index("c")


def _flip(v, bit):
    return 1 - v if bit else v


def _all_gather8(xin, name):
    R, N = xin.shape

    def body(x_ref, o_ref, ssem, rsem, lsem):
        x, y, c = _my_pos()
        me = 4 * x + 2 * y + c
        own = pltpu.make_async_copy(x_ref, o_ref.at[me], lsem)
        own.start()
        sends = []
        for k in range(1, 8):
            tgt = (_flip(x, (k >> 2) & 1), _flip(y, (k >> 1) & 1), _flip(c, k & 1))
            cp = pltpu.make_async_remote_copy(src_ref=x_ref, dst_ref=o_ref.at[me], send_sem=ssem.at[k - 1],
                                              recv_sem=rsem.at[k - 1], device_id=tgt, device_id_type=MESH)
            cp.start()
            sends.append(cp)
        for k in range(1, 8):
            tgt = (_flip(x, (k >> 2) & 1), _flip(y, (k >> 1) & 1), _flip(c, k & 1))
            peer = 4 * tgt[0] + 2 * tgt[1] + tgt[2]
            pltpu.make_async_remote_copy(src_ref=x_ref, dst_ref=o_ref.at[peer], send_sem=ssem.at[k - 1],
                                         recv_sem=rsem.at[k - 1], device_id=tgt, device_id_type=MESH).wait_recv()
        for cp in sends:
            cp.wait_send()
        own.wait()

    return pl.pallas_call(
        body, name=name, out_shape=SDS((8, R, N), xin.dtype), in_specs=[VMEM_SPEC], out_specs=VMEM_SPEC,
        scratch_shapes=[pltpu.SemaphoreType.DMA((7,)), pltpu.SemaphoreType.DMA((7,)), pltpu.SemaphoreType.DMA],
    )(xin)


def _chip_gather(smalls, name, after=None):
    ns = len(smalls)

    def body(*refs):
        s_in, s_out = refs[:ns], refs[ns + 1:2 * ns + 1]
        ssem, rsem, lsem = refs[2 * ns + 1:]
        x, y, c = _my_pos()
        j = 2 * x + y
        chips = _peer_chips(x, y, c)
        local = [pltpu.make_async_copy(s_in[a], s_out[a].at[j], lsem.at[a]) for a in range(ns)]
        for cp in local:
            cp.start()
        sends = []
        for a in range(ns):
            for k in range(3):
                cp = pltpu.make_async_remote_copy(src_ref=s_in[a], dst_ref=s_out[a].at[j], send_sem=ssem.at[3 * a + k],
                                                  recv_sem=rsem.at[3 * a + k], device_id=chips[k][0], device_id_type=MESH)
                cp.start()
                sends.append(cp)
        for a in range(ns):
            for k in range(3):
                pltpu.make_async_remote_copy(src_ref=s_in[a], dst_ref=s_out[a].at[chips[k][1]], send_sem=ssem.at[3 * a + k],
                                             recv_sem=rsem.at[3 * a + k], device_id=chips[k][0],
                                             device_id_type=MESH).wait_recv()
        for cp in sends:
            cp.wait_send()
        for cp in local:
            cp.wait()

    return pl.pallas_call(
        body, name=name, out_shape=[SDS((4,) + a.shape, a.dtype) for a in smalls],
        in_specs=[VMEM_SPEC] * ns + [ANY_SPEC], out_specs=[VMEM_SPEC] * ns,
        scratch_shapes=[pltpu.SemaphoreType.DMA((3 * ns,)), pltpu.SemaphoreType.DMA((3 * ns,)),
                        pltpu.SemaphoreType.DMA((ns,))],
    )(*smalls, smalls[0] if after is None else after)


HBM_SPEC = pl.BlockSpec(memory_space=pltpu.HBM)
SEM_SPEC = pl.BlockSpec(memory_space=pltpu.SEMAPHORE)
DATAFLOW = pltpu.SideEffectType.DATAFLOW_SIDE_EFFECTING


def _peer_chips(x, y, c):
    out = []
    for k in range(1, 4):
        px, py = _flip(x, (k >> 1) & 1), _flip(y, k & 1)
        out.append(((px, py, c), 2 * px + py))
    return out


def _half_copies(srcs, dsts, ssem, rsem):
    x, y, c = _my_pos()
    j = 2 * x + y
    pairs = []
    for a in range(len(srcs)):
        half = srcs[a].shape[1] // 2
        mine = pl.ds(pl.multiple_of(c * half, 8), half)
        for k, (dev, pj) in enumerate(_peer_chips(x, y, c)):
            sem = 3 * a + k
            send = pltpu.make_async_remote_copy(src_ref=srcs[a].at[j, mine], dst_ref=dsts[a].at[j, mine],
                                                send_sem=ssem.at[sem], recv_sem=rsem.at[sem], device_id=dev,
                                                device_id_type=MESH)
            arrive = pltpu.make_async_remote_copy(src_ref=srcs[a].at[j, mine], dst_ref=dsts[a].at[pj, mine],
                                                  send_sem=ssem.at[sem], recv_sem=rsem.at[sem], device_id=dev,
                                                  device_id_type=MESH)
            pairs.append((send, arrive))
    return pairs


def _halves_start(bigs, after, name):
    nb = len(bigs)

    def body(*refs):
        b_in = refs[:nb]
        ssem, rsem = refs[nb + 1], refs[nb + 2]
        b_out = refs[nb + 3:2 * nb + 3]
        token = refs[2 * nb + 3]
        for send, _ in _half_copies(b_in, b_out, ssem, rsem):
            send.start()
        token[...] = jnp.zeros_like(token)

    out_shape = (pltpu.SemaphoreType.DMA((3 * nb,)), pltpu.SemaphoreType.DMA((3 * nb,)),
                 *[pltpu.HBM(b.shape, b.dtype) for b in bigs], SDS((8, 128), f32))
    return pl.pallas_call(
        body, name=name, out_shape=out_shape, in_specs=[HBM_SPEC] * nb + [ANY_SPEC],
        out_specs=(SEM_SPEC, SEM_SPEC, *[HBM_SPEC] * nb, VMEM_SPEC),
        input_output_aliases={a: 2 + a for a in range(nb)}, compiler_params=_cp(has_side_effects=DATAFLOW),
    )(*[pltpu.with_memory_space_constraint(b, pltpu.HBM) for b in bigs], after)


def _halves_wait(ssem, rsem, bigs, after, name):
    nb = len(bigs)

    def body(*refs):
        b_in = refs[:nb]
        ssem_ref, rsem_ref = refs[nb], refs[nb + 1]
        for send, arrive in _half_copies(b_in, b_in, ssem_ref, rsem_ref):
            send.wait_send()
            arrive.wait_recv()

    return pl.pallas_call(
        body, name=name, out_shape=tuple(pltpu.HBM(b.shape, b.dtype) for b in bigs),
        in_specs=[HBM_SPEC] * nb + [SEM_SPEC, SEM_SPEC, ANY_SPEC], out_specs=tuple([HBM_SPEC] * nb),
        input_output_aliases={a: a for a in range(nb)}, compiler_params=_cp(has_side_effects=DATAFLOW),
    )(*bigs, ssem, rsem, after)


def _halves_forward(bigs, name):
    nb = len(bigs)

    def body(*refs):
        b_in, b_out = refs[:nb], refs[nb:2 * nb]
        ssem, rsem = refs[2 * nb:]
        x, y, c = _my_pos()
        sib = (x, y, 1 - c)
        sends = []
        for a in range(nb):
            half = b_in[a].shape[1] // 2
            mine = pl.ds(pl.multiple_of(c * half, 8), half)
            for k, (_, pj) in enumerate(_peer_chips(x, y, c)):
                cp = pltpu.make_async_remote_copy(src_ref=b_in[a].at[pj, mine], dst_ref=b_out[a].at[pj, mine],
                                                  send_sem=ssem.at[3 * a + k], recv_sem=rsem.at[3 * a + k],
                                                  device_id=sib, device_id_type=MESH)
                cp.start()
                sends.append(cp)
        for a in range(nb):
            half = b_in[a].shape[1] // 2
            other = pl.ds(pl.multiple_of((1 - c) * half, 8), half)
            for k, (_, pj) in enumerate(_peer_chips(x, y, c)):
                pltpu.make_async_remote_copy(src_ref=b_in[a].at[pj, other], dst_ref=b_out[a].at[pj, other],
                                             send_sem=ssem.at[3 * a + k], recv_sem=rsem.at[3 * a + k],
                                             device_id=sib, device_id_type=MESH).wait_recv()
        for cp in sends:
            cp.wait_send()

    return pl.pallas_call(
        body, name=name, out_shape=[SDS(b.shape, b.dtype) for b in bigs], in_specs=[ANY_SPEC] * nb,
        out_specs=[ANY_SPEC] * nb, input_output_aliases={a: a for a in range(nb)},
        scratch_shapes=[pltpu.SemaphoreType.DMA((3 * nb,)), pltpu.SemaphoreType.DMA((3 * nb,))],
    )(*bigs)


def _cast_to_slot(w, jvec, name):
    rows, cols = w.shape
    tr = 256

    def body(j_ref, w_ref, o_ref):
        o_ref[...] = w_ref[...].astype(bf16)

    grid_spec = pltpu.PrefetchScalarGridSpec(
        num_scalar_prefetch=1, grid=(rows // tr,), in_specs=[pl.BlockSpec((tr, cols), lambda i, j: (i, 0))],
        out_specs=pl.BlockSpec((None, tr, cols), lambda i, j: (j[0], i, 0)))
    return pl.pallas_call(body, name=name, out_shape=SDS((4, rows, cols), bf16), grid_spec=grid_spec)(jvec, w)


def _exchange_copies(n_blocks):
    def make(srcs, lands, ssem, rsem):
        x, y, c = _my_pos()
        cps = []
        sem = 0
        for a in range(len(srcs)):
            rb = srcs[a].shape[0] // n_blocks[a]
            half = rb // 2
            for jb in range(n_blocks[a]):
                theirs = pl.ds(pl.multiple_of(jb * rb + (1 - c) * half, 8), half)
                cps.append(pltpu.make_async_remote_copy(src_ref=srcs[a].at[theirs], dst_ref=lands[a].at[jb],
                                                        send_sem=ssem.at[sem], recv_sem=rsem.at[sem],
                                                        device_id=(x, y, 1 - c), device_id_type=MESH))
                sem += 1
        return cps
    return make


def _scatter_copies(srcs, lands, ssem, rsem):
    x, y, c = _my_pos()
    cps = []
    for a in range(len(srcs)):
        for k, (dev, pj) in enumerate(_peer_chips(x, y, c)):
            cps.append(pltpu.make_async_remote_copy(src_ref=srcs[a].at[pj], dst_ref=lands[a].at[k],
                                                    send_sem=ssem.at[3 * a + k], recv_sem=rsem.at[3 * a + k],
                                                    device_id=dev, device_id_type=MESH))
    return cps


def _split_start(srcs, land_shapes, n_cp, make, name):
    ns, nl = len(srcs), len(land_shapes)

    def body(*refs):
        s_in = refs[:ns]
        ssem, rsem = refs[ns + nl], refs[ns + nl + 1]
        l_out = refs[2 * ns + nl + 2:2 * ns + 2 * nl + 2]
        token = refs[2 * ns + 2 * nl + 2]
        for cp in make(s_in, l_out, ssem, rsem):
            cp.start()
        token[...] = jnp.zeros_like(token)

    lands = [pltpu.with_memory_space_constraint(lax.empty(sh.shape, sh.dtype), pltpu.HBM) for sh in land_shapes]
    out_shape = (pltpu.SemaphoreType.DMA((n_cp,)), pltpu.SemaphoreType.DMA((n_cp,)),
                 *[pltpu.HBM(b.shape, b.dtype) for b in srcs], *[pltpu.HBM(b.shape, b.dtype) for b in land_shapes],
                 SDS((8, 128), f32))
    return pl.pallas_call(
        body, name=name, out_shape=out_shape, in_specs=[HBM_SPEC] * (ns + nl),
        out_specs=(SEM_SPEC, SEM_SPEC, *[HBM_SPEC] * (ns + nl), VMEM_SPEC),
        input_output_aliases={i: 2 + i for i in range(ns + nl)}, compiler_params=_cp(has_side_effects=DATAFLOW),
    )(*[pltpu.with_memory_space_constraint(b, pltpu.HBM) for b in srcs], *lands)


def _split_wait(ssem, rsem, srcs, lands, after, make, name):
    ns, nl = len(srcs), len(lands)

    def body(*refs):
        s_in, l_in = refs[:ns], refs[ns:ns + nl]
        ssem_ref, rsem_ref = refs[ns + nl], refs[ns + nl + 1]
        for cp in make(s_in, l_in, ssem_ref, rsem_ref):
            cp.wait_send()
            cp.wait_recv()

    outs = pl.pallas_call(
        body, name=name, out_shape=tuple(pltpu.HBM(b.shape, b.dtype) for b in (*srcs, *lands)),
        in_specs=[HBM_SPEC] * (ns + nl) + [SEM_SPEC, SEM_SPEC, ANY_SPEC], out_specs=tuple([HBM_SPEC] * (ns + nl)),
        input_output_aliases={i: i for i in range(ns + nl)}, compiler_params=_cp(has_side_effects=DATAFLOW),
    )(*srcs, *lands, ssem, rsem, after)
    return list(outs[:ns]), list(outs[ns:])


def _sibling_send(halves, name):
    n = len(halves)

    def body(*refs):
        ins, outs = refs[:n], refs[n:2 * n]
        ssem, rsem = refs[2 * n:]
        x, y, c = _my_pos()
        cps = []
        for a in range(n):
            cp = pltpu.make_async_remote_copy(src_ref=ins[a], dst_ref=outs[a], send_sem=ssem.at[a],
                                              recv_sem=rsem.at[a], device_id=(x, y, 1 - c), device_id_type=MESH)
            cp.start()
            cps.append(cp)
        for cp in cps:
            cp.wait_recv()
        for cp in cps:
            cp.wait_send()

    out_shape = [SDS(h.shape, h.dtype) for h in halves]
    return pl.pallas_call(
        body, name=name, out_shape=out_shape, in_specs=[ANY_SPEC] * n, out_specs=[ANY_SPEC] * n,
        scratch_shapes=[pltpu.SemaphoreType.DMA((n,)), pltpu.SemaphoreType.DMA((n,))],
    )(*halves)


def _adaln_shard(cc, w_ada_shard):
    def body(c_ref, w_ref, m_ref, sc_ref):
        sc = _silu(c_ref[...])
        sc_ref[...] = sc
        m_ref[...] = jnp.dot(sc, w_ref[...], precision=HIGHEST, preferred_element_type=f32)

    return pl.pallas_call(
        body, name="adaln_shard", out_shape=(SDS((16, w_ada_shard.shape[1]), f32), SDS((16, D), f32)),
        in_specs=[VMEM_SPEC, VMEM_SPEC], out_specs=(VMEM_SPEC, VMEM_SPEC), compiler_params=_cp(32),
    )(cc, w_ada_shard)


def _prenorm(xx, norm_g, mrow, b_ada, tm, name):
    n = xx.shape[0]

    def body(x_ref, g_ref, m_ref, b_ref, h_ref):
        x = x_ref[...]
        shift = m_ref[:, 0:D] + b_ref[:, 0:D]
        scale = m_ref[:, D:2 * D] + b_ref[:, D:2 * D]
        r = lax.rsqrt(jnp.mean(x * x, axis=-1, keepdims=True) + RMS_EPS)
        y = (x * r) * g_ref[...]
        h_ref[...] = (y * (1.0 + scale) + shift).astype(bf16)

    row = lambda i: (i, 0)
    fixed = lambda i: (0, 0)
    return pl.pallas_call(
        body, name=name, out_shape=SDS((n, D), bf16), grid=(n // tm,),
        in_specs=[pl.BlockSpec((tm, D), row), pl.BlockSpec((1, D), fixed), pl.BlockSpec((1, 3 * D), fixed),
                  pl.BlockSpec((1, 3 * D), fixed)],
        out_specs=pl.BlockSpec((tm, D), row),
    )(xx, norm_g, mrow, b_ada)


def _in_proj(h, w4):
    tm = 512

    def body(h_ref, w_ref, p_ref):
        p_ref[...] = jnp.dot(h_ref[...], w_ref[...], preferred_element_type=f32)

    return pl.pallas_call(
        body, name="in_proj", out_shape=SDS((S, 4 * D), f32), grid=(4, S // tm),
        in_specs=[pl.BlockSpec((tm, D), lambda j, k: (k, 0)), pl.BlockSpec((None, D, D), lambda j, k: (j, 0, 0))],
        out_specs=pl.BlockSpec((tm, D), lambda j, k: (k, j)),
    )(h, w4)


def _ctx_proj(hc, w4):
    def body(h_ref, w0_ref, w1_ref, p_ref):
        hv = h_ref[...]
        p_ref[:, 0:DA] = jnp.dot(hv, w0_ref[:, DA:2 * DA], preferred_element_type=f32)
        p_ref[:, DA:2 * DA] = jnp.dot(hv, w1_ref[:, 0:DA], preferred_element_type=f32)

    return pl.pallas_call(
        body, name="ctx_proj", out_shape=SDS((L, 2 * DA), f32), grid=(1,),
        in_specs=[pl.BlockSpec((L, D), lambda i: (0, 0)), pl.BlockSpec((None, D, D), lambda i: (0, 0, 0)),
                  pl.BlockSpec((None, D, D), lambda i: (1, 0, 0))],
        out_specs=pl.BlockSpec((L, 2 * DA), lambda i: (0, 0)),
    )(hc, w4, w4)


def _head_ones():
    r = lax.broadcasted_iota(i32, (DA, DA), 0) // DH
    c = lax.broadcasted_iota(i32, (DA, DA), 1) // DH
    return (r == c).astype(bf16)


def _head_sum(v, ones_bd):
    hi = v.astype(bf16)
    lo = (v - hi.astype(f32)).astype(bf16)
    return jnp.dot(hi, ones_bd, preferred_element_type=f32) + jnp.dot(lo, ones_bd, preferred_element_type=f32)


def _swap16(v):
    lane = lax.broadcasted_iota(i32, v.shape, 1)
    return jnp.where((lane & 31) < 16, pltpu.roll(v, DA - 16, 1), pltpu.roll(v, 16, 1))


def _rope_block(ct_ref, rt_ref, tm):
    rows = [jnp.tile(rt_ref[8 * j:8 * j + 8, :], (GW // 8, 1)) for j in range(tm // GW)]
    return jnp.tile(ct_ref[...], (tm // GW, 1)) + jnp.concatenate(rows, axis=0)


def _rope_specs(tm):
    col = pl.BlockSpec((GW, DA), lambda i: (0, 0))
    row = pl.BlockSpec((8 * tm // GW, DA), lambda i: (i, 0))
    return [col, row, col, row]


def _qk_prep(p, gq, gk, rope):
    tm = 256

    def body(qk_ref, v_ref, gq_ref, gk_ref, cc_ref, cr_ref, sc_ref, sr_ref, qr_ref, qp_ref, kr_ref, vh_ref):
        ones_bd = _head_ones()
        cs, sn = _rope_block(cc_ref, cr_ref, tm), _rope_block(sc_ref, sr_ref, tm)
        q = qk_ref[:, 0:DA]
        k = qk_ref[:, DA:2 * DA]
        yq = (q * lax.rsqrt(_head_sum(q * q, ones_bd) * (1.0 / DH) + RMS_EPS)) * gq_ref[...]
        yk = (k * lax.rsqrt(_head_sum(k * k, ones_bd) * (1.0 / DH) + RMS_EPS)) * gk_ref[...]
        qr = (yq * cs + _swap16(yq) * sn) * QK_SCALE
        qp = yq * QK_SCALE
        kr = yk * cs + _swap16(yk) * sn
        vv = v_ref[...]
        for hh in range(H):
            sl = slice(hh * DH, (hh + 1) * DH)
            qr_ref[hh] = qr[:, sl].astype(bf16)
            qp_ref[hh] = qp[:, sl].astype(bf16)
            kr_ref[hh] = kr[:, sl].astype(bf16)
            vh_ref[hh] = vv[:, sl].astype(bf16)

    hm = SDS((H, S, DH), bf16)
    hspec = pl.BlockSpec((H, tm, DH), lambda i: (0, i, 0))
    fixed = lambda i: (0, 0)
    return pl.pallas_call(
        body, name="qk_prep", out_shape=(hm, hm, hm, hm), grid=(S // tm,),
        in_specs=[pl.BlockSpec((tm, 2 * DA), lambda i: (i, 0)), pl.BlockSpec((tm, DA), lambda i: (i, 2)),
                  pl.BlockSpec((1, DA), fixed), pl.BlockSpec((1, DA), fixed)] + _rope_specs(tm),
        out_specs=(hspec, hspec, hspec, hspec),
    )(p, p, gq, gk, *rope)


def _ctx_prep(pc, gk):
    def body(p_ref, gk_ref, kc_ref, vc_ref):
        ones_bd = _head_ones()
        k = p_ref[:, 0:DA]
        yk = (k * lax.rsqrt(_head_sum(k * k, ones_bd) * (1.0 / DH) + RMS_EPS)) * gk_ref[...]
        vv = p_ref[:, DA:2 * DA]
        for hh in range(H):
            sl = slice(hh * DH, (hh + 1) * DH)
            kc_ref[hh] = yk[:, sl].astype(bf16)
            vc_ref[hh] = vv[:, sl].astype(bf16)

    hm = SDS((H, L, DH), bf16)
    return pl.pallas_call(
        body, name="ctx_prep", out_shape=(hm, hm), in_specs=[VMEM_SPEC, VMEM_SPEC], out_specs=(VMEM_SPEC, VMEM_SPEC),
    )(pc, gk)


def _tile_pieces():
    out = []
    for (i0, u0) in TILE_GEOM:
        rows = []
        for j in range(2):
            i = i0 + j
            rs = _row_start(i)
            rows.append([(u0 + u - i + WIN_H - 1) if rs <= u0 + u < rs + WIN_H else None for u in range(KR)])
        out.append(rows)
    return out


def _bias_prep(rpb_rev_pad):
    pieces = _tile_pieces()

    def body(r_ref, o_ref):
        rp = r_ref[...]
        xs = jnp.broadcast_to(rp[:, None, :], (N_DR, GW, 128)).reshape(N_DR * GW, 128)
        row = lax.broadcasted_iota(i32, xs.shape, 0)
        lane = lax.broadcasted_iota(i32, xs.shape, 1)
        for b in range(6):
            xs = jnp.where(((row >> b) & 1) == 1, pltpu.roll(xs, 1 << b, 1), xs)
        xs = pltpu.roll(xs, 128 - (WIN_W - 1), 1)
        k = row & (GW - 1)
        c0 = jnp.clip(lane - WIN_W // 2, 0, GW - WIN_W)
        xs = jnp.where((k >= c0) & (k < c0 + WIN_W), xs, NEG)
        neg = jnp.full((GW, GW), NEG, f32)
        for t in range(NT):
            for j in range(2):
                for u in range(KR):
                    dr = pieces[t][j][u]
                    piece = neg if dr is None else xs[dr * GW:(dr + 1) * GW, 0:GW]
                    o_ref[t, u * GW:(u + 1) * GW, j * GW:(j + 1) * GW] = piece

    return pl.pallas_call(
        body, name="bias_prep", out_shape=SDS((H, NT, KB, QB), f32), grid=(H,),
        in_specs=[pl.BlockSpec((None, N_DR, 128), lambda h: (h, 0, 0))],
        out_specs=pl.BlockSpec((None, NT, KB, QB), lambda h: (h, 0, 0, 0)),
    )(rpb_rev_pad)


def _block_geom(b):
    qs = b * QB
    ks = min(max(2 * b - 4, 0), ROWS - KR) * GW
    t = b if b < 2 else (b - (NQB - NT) if b > NQB - 3 else 2)
    return qs, ks, t


def _tt(a, b):
    return lax.dot_general(a, b, (((1,), (1,)), ((), ())), preferred_element_type=f32)


def _tn(a, b):
    return lax.dot_general(a, b, (((0,), (0,)), ((), ())), preferred_element_type=f32)


def _softmax_t(s_lat, s_ctx):
    m = jnp.maximum(jnp.max(s_lat, axis=0, keepdims=True), jnp.max(s_ctx, axis=0, keepdims=True))
    e_lat = jnp.exp(s_lat - m)
    e_ctx = jnp.exp(s_ctx - m)
    inv = 1.0 / (jnp.sum(e_lat, axis=0, keepdims=True) + jnp.sum(e_ctx, axis=0, keepdims=True))
    return e_lat * inv, e_ctx * inv


def _staged(n_blocks, stages):
    held = [dict() for _ in stages]
    for step in range(n_blocks + len(stages) - 1):
        for s, fn in enumerate(stages):
            b = step - s
            if 0 <= b < n_blocks:
                held[s][b] = fn(b) if s == 0 else fn(b, held[s - 1].pop(b))


def _attn_fwd(qr, qp, kr, vh, kc, vc, btt):
    def body(qr_ref, qp_ref, kr_ref, v_ref, kc_ref, vc_ref, bt_ref, o_ref):
        kcv, vcv = kc_ref[...], vc_ref[...]

        def scores(b):
            qs, ks, t = _block_geom(b)
            return (_tt(kr_ref[ks:ks + KB, :], qr_ref[qs:qs + QB, :]) + bt_ref[t], _tt(kcv, qp_ref[qs:qs + QB, :]))

        def probs(b, sc):
            p_lat, p_ctx = _softmax_t(*sc)
            return p_lat.astype(bf16), p_ctx.astype(bf16)

        def values(b, p):
            qs, ks, _ = _block_geom(b)
            o_ref[qs:qs + QB, :] = _tn(p[0], v_ref[ks:ks + KB, :]) + _tn(p[1], vcv)

        _staged(NQB, (scores, probs, values))

    sq = pl.BlockSpec((None, S, DH), lambda h: (h, 0, 0))
    sc = pl.BlockSpec((None, L, DH), lambda h: (h, 0, 0))
    return pl.pallas_call(
        body, name="attn_fwd", out_shape=SDS((H, S, DH), f32), grid=(H,),
        in_specs=[sq, sq, sq, sq, sc, sc, pl.BlockSpec((None, NT, KB, QB), lambda h: (h, 0, 0, 0))],
        out_specs=sq, compiler_params=_cp(48),
    )(qr, qp, kr, vh, kc, vc, btt)


def _attn_gate(o, p):
    tm = 256

    def body(o_ref, za_ref, a_ref):
        sz = _silu(za_ref[...])
        for hh in range(H):
            sl = slice(hh * DH, (hh + 1) * DH)
            a_ref[:, sl] = (o_ref[hh] * sz[:, sl]).astype(bf16)

    return pl.pallas_call(
        body, name="attn_gate", out_shape=SDS((S, DA), bf16), grid=(S // tm,),
        in_specs=[pl.BlockSpec((H, tm, DH), lambda i: (0, i, 0)), pl.BlockSpec((tm, DA), lambda i: (i, 3))],
        out_specs=pl.BlockSpec((tm, DA), lambda i: (i, 0)),
    )(o, p)


def _shift_rows(v, down):
    n = v.shape[0]
    row = lax.broadcasted_iota(i32, v.shape, 0)
    if down:
        return jnp.where(row == 0, 0.0, pltpu.roll(v, 1, 0))
    return jnp.where(row == n - 1, 0.0, pltpu.roll(v, n - 1, 0))


def _conv_specs():
    col = lambda off: pl.BlockSpec((S, 128), lambda i, off=off: (0, off + i))
    return [col(16), col(20), col(24), col(28), pl.BlockSpec((3, 128), lambda i: (0, i)),
            pl.BlockSpec((1, 128), lambda i: (0, i))]


def _conv_fwd(p, conv_w, conv_b):
    def body(u_ref, bg_ref, cg_ref, zc_ref, w_ref, b_ref, o_ref):
        cu = cg_ref[...] * u_ref[...]
        cv = b_ref[...] + _shift_rows(cu, True) * w_ref[0:1, :]
        cv = cv + cu * w_ref[1:2, :]
        cv = cv + _shift_rows(cu, False) * w_ref[2:3, :]
        o_ref[...] = ((bg_ref[...] * cv) * _silu(zc_ref[...])).astype(bf16)

    return pl.pallas_call(
        body, name="conv_fwd", out_shape=SDS((S, DC), bf16), grid=(DC // 128,),
        in_specs=_conv_specs(), out_specs=pl.BlockSpec((S, 128), lambda i: (0, i)), compiler_params=_cp(40),
    )(p, p, p, p, conv_w, conv_b)


def _out_proj_loss(attn_g, conv_g, w_out, xx, tgt, mrow, b_ada):
    tm = 256

    def body(a_ref, c_ref, w_ref, x_ref, t_ref, m_ref, b_ref, dy_ref, dmix_ref, gwo_ref, dgate_ref, loss_ref):
        k = pl.program_id(0)

        @pl.when(k == 0)
        def _():
            gwo_ref[...] = jnp.zeros_like(gwo_ref)
            dgate_ref[...] = jnp.zeros_like(dgate_ref)
            loss_ref[0, 0] = 0.0

        gate = m_ref[:, 2 * D:3 * D] + b_ref[:, 2 * D:3 * D]
        av, cv = a_ref[...], c_ref[...]
        mo = jnp.dot(av, w_ref[0:DA, :], preferred_element_type=f32)
        mo = mo + jnp.dot(cv, w_ref[DA:DA + DC, :], preferred_element_type=f32)
        y = x_ref[...] + gate * mo
        diff = y - t_ref[...]
        loss_ref[0, 0] += jnp.sum(diff * diff)
        dy = diff * (1.0 / D)
        dy_ref[...] = dy
        dgate_ref[...] += jnp.sum(dy * mo, axis=0, keepdims=True)
        dmo = (dy * gate).astype(bf16)
        dmix_ref[...] = _tt(dmo, w_ref[...])
        gwo_ref[0:DA, :] += _tn(av, dmo)
        gwo_ref[DA:DA + DC, :] += _tn(cv, dmo)

    row = lambda i: (i, 0)
    fixed = lambda i: (0, 0)
    return pl.pallas_call(
        body, name="out_proj_loss",
        out_shape=(SDS((S, D), f32), SDS((S, D), f32), SDS((D, D), f32), SDS((1, D), f32), SDS((1, 1), f32)),
        grid=(S // tm,),
        in_specs=[pl.BlockSpec((tm, DA), row), pl.BlockSpec((tm, DC), row), pl.BlockSpec((D, D), fixed),
                  pl.BlockSpec((tm, D), row), pl.BlockSpec((tm, D), row), pl.BlockSpec((1, 3 * D), fixed),
                  pl.BlockSpec((1, 3 * D), fixed)],
        out_specs=(pl.BlockSpec((tm, D), row), pl.BlockSpec((tm, D), row), pl.BlockSpec((D, D), fixed),
                   pl.BlockSpec((1, D), fixed), SMEM_SPEC),
        compiler_params=_cp(48, dimension_semantics=("arbitrary",)),
    )(attn_g, conv_g, w_out, xx, tgt, mrow, b_ada)


def _conv_bwd(dmix, p, conv_w, conv_b):
    def body(d_ref, u_ref, bg_ref, cg_ref, zc_ref, w_ref, b_ref, du_ref, dbg_ref, dcg_ref, dzc_ref, gw_ref, gb_ref):
        dconv = d_ref[...]
        u, bg, cg, zc = u_ref[...], bg_ref[...], cg_ref[...], zc_ref[...]
        w0, w1, w2 = w_ref[0:1, :], w_ref[1:2, :], w_ref[2:3, :]
        cu = cg * u
        cu_m, cu_p = _shift_rows(cu, True), _shift_rows(cu, False)
        cv = b_ref[...] + cu_m * w0
        cv = cv + cu * w1
        cv = cv + cu_p * w2
        sz = _silu(zc)
        dbg_ref[...] = ((dconv * sz) * cv).astype(bf16)
        dzc_ref[...] = ((dconv * (bg * cv)) * _dsilu(zc)).astype(bf16)
        dcv = (dconv * sz) * bg
        gb_ref[...] = jnp.sum(dcv, axis=0, keepdims=True)
        gw_ref[0:1, :] = jnp.sum(dcv * cu_m, axis=0, keepdims=True)
        gw_ref[1:2, :] = jnp.sum(dcv * cu, axis=0, keepdims=True)
        gw_ref[2:3, :] = jnp.sum(dcv * cu_p, axis=0, keepdims=True)
        gw_ref[3:8, :] = jnp.zeros((5, 128), f32)
        dcu = _shift_rows(dcv, False) * w0 + dcv * w1 + _shift_rows(dcv, True) * w2
        dcg_ref[...] = (dcu * u).astype(bf16)
        du_ref[...] = (dcu * cg).astype(bf16)

    piece = SDS((S, DC), bf16)
    ospec = pl.BlockSpec((S, 128), lambda i: (0, i))
    return pl.pallas_call(
        body, name="conv_bwd", out_shape=(piece, piece, piece, piece, SDS((8, DC), f32), SDS((1, DC), f32)),
        grid=(DC // 128,),
        in_specs=[pl.BlockSpec((S, 128), lambda i: (0, 4 + i))] + _conv_specs(),
        out_specs=(ospec, ospec, ospec, ospec, pl.BlockSpec((8, 128), lambda i: (0, i)),
                   pl.BlockSpec((1, 128), lambda i: (0, i))),
        compiler_params=_cp(48),
    )(dmix, p, p, p, p, conv_w, conv_b)


def _gate_bwd(dmix, p, o):
    tm = 256

    def body(d_ref, za_ref, o_ref, do_ref, dza_ref):
        za = za_ref[...]
        dattn = d_ref[...]
        a = dattn * _silu(za)
        bb = dattn * _dsilu(za)
        for hh in range(H):
            sl = slice(hh * DH, (hh + 1) * DH)
            do_ref[hh] = a[:, sl].astype(bf16)
            dza_ref[:, sl] = (bb[:, sl] * o_ref[hh]).astype(bf16)

    hspec = pl.BlockSpec((H, tm, DH), lambda i: (0, i, 0))
    return pl.pallas_call(
        body, name="gate_bwd", out_shape=(SDS((H, S, DH), bf16), SDS((S, DA), bf16)), grid=(S // tm,),
        in_specs=[pl.BlockSpec((tm, DA), lambda i: (i, 0)), pl.BlockSpec((tm, DA), lambda i: (i, 3)), hspec],
        out_specs=(hspec, pl.BlockSpec((tm, DA), lambda i: (i, 0))),
    )(dmix, p, o)


def _attn_bwd(qr, qp, kr, vh, kc, vc, btt, do):
    def body(qr_ref, qp_ref, kr_ref, v_ref, kc_ref, vc_ref, bt_ref, do_ref,
             dqr_ref, dqp_ref, dkr_ref, dv_ref, dkc_ref, dvc_ref, dbt_ref):
        kcv, vcv = kc_ref[...], vc_ref[...]
        dkr_ref[...] = jnp.zeros_like(dkr_ref)
        dv_ref[...] = jnp.zeros_like(dv_ref)
        dbt_ref[...] = jnp.zeros_like(dbt_ref)
        ctx_acc = {}

        def products(b):
            qs, ks, t = _block_geom(b)
            dob = do_ref[qs:qs + QB, :]
            s_lat = _tt(kr_ref[ks:ks + KB, :], qr_ref[qs:qs + QB, :]) + bt_ref[t]
            s_ctx = _tt(kcv, qp_ref[qs:qs + QB, :])
            return s_lat, s_ctx, _tt(v_ref[ks:ks + KB, :], dob), _tt(vcv, dob)

        def score_grads(b, x):
            s_lat, s_ctx, dp_lat, dp_ctx = x
            p_lat, p_ctx = _softmax_t(s_lat, s_ctx)
            delta = jnp.sum(p_lat * dp_lat, axis=0, keepdims=True) + jnp.sum(p_ctx * dp_ctx, axis=0, keepdims=True)
            ds_lat = p_lat * (dp_lat - delta)
            ds_ctx = p_ctx * (dp_ctx - delta)
            return ds_lat, ds_lat.astype(bf16), ds_ctx.astype(bf16), p_lat.astype(bf16), p_ctx.astype(bf16)

        def operand_grads(b, y):
            qs, ks, t = _block_geom(b)
            ds_lat, dsb_lat, dsb_ctx, pb_lat, pb_ctx = y
            qrb, qpb, dob = qr_ref[qs:qs + QB, :], qp_ref[qs:qs + QB, :], do_ref[qs:qs + QB, :]
            dbt_ref[t] += ds_lat
            dqr_ref[qs:qs + QB, :] = _tn(dsb_lat, kr_ref[ks:ks + KB, :])
            dqp_ref[qs:qs + QB, :] = _tn(dsb_ctx, kcv)
            dkr_ref[ks:ks + KB, :] += jnp.dot(dsb_lat, qrb, preferred_element_type=f32)
            dv_ref[ks:ks + KB, :] += jnp.dot(pb_lat, dob, preferred_element_type=f32)
            dkc = jnp.dot(dsb_ctx, qpb, preferred_element_type=f32)
            dvc = jnp.dot(pb_ctx, dob, preferred_element_type=f32)
            ctx_acc["k"] = dkc if b == 0 else ctx_acc["k"] + dkc
            ctx_acc["v"] = dvc if b == 0 else ctx_acc["v"] + dvc

        _staged(NQB, (products, score_grads, operand_grads))
        dkc_ref[...] = ctx_acc["k"]
        dvc_ref[...] = ctx_acc["v"]

    sq = pl.BlockSpec((None, S, DH), lambda h: (h, 0, 0))
    sc = pl.BlockSpec((None, L, DH), lambda h: (h, 0, 0))
    sb = pl.BlockSpec((None, NT, KB, QB), lambda h: (h, 0, 0, 0))
    big, ctxs = SDS((H, S, DH), f32), SDS((H, L, DH), f32)
    return pl.pallas_call(
        body, name="attn_bwd", out_shape=(big, big, big, big, ctxs, ctxs, SDS((H, NT, KB, QB), f32)), grid=(H,),
        in_specs=[sq, sq, sq, sq, sc, sc, sb, sq], out_specs=(sq, sq, sq, sq, sc, sc, sb), compiler_params=_cp(56),
    )(qr, qp, kr, vh, kc, vc, btt, do)


def _bias_bwd(dbtt):
    pieces = _tile_pieces()

    def body(d_ref, o_ref, scr):
        scr[...] = jnp.zeros_like(scr)
        acc = [None] * N_DR
        for t in range(NT):
            for j in range(2):
                for u in range(KR):
                    dr = pieces[t][j][u]
                    if dr is None:
                        continue
                    piece = d_ref[t, u * GW:(u + 1) * GW, j * GW:(j + 1) * GW]
                    acc[dr] = piece if acc[dr] is None else acc[dr] + piece
        for dr in range(N_DR):
            scr[dr * GW:(dr + 1) * GW, 0:GW] = acc[dr]
        xs = pltpu.roll(scr[...], WIN_W - 1, 1)
        row = lax.broadcasted_iota(i32, xs.shape, 0)
        for b in range(6):
            xs = jnp.where(((row >> b) & 1) == 1, pltpu.roll(xs, 128 - (1 << b), 1), xs)
        o_ref[...] = jnp.sum(xs.reshape(N_DR, GW, 128), axis=1)

    return pl.pallas_call(
        body, name="bias_bwd", out_shape=SDS((H, N_DR, 128), f32), grid=(H,),
        in_specs=[pl.BlockSpec((None, NT, KB, QB), lambda h: (h, 0, 0, 0))],
        out_specs=pl.BlockSpec((None, N_DR, 128), lambda h: (h, 0, 0)),
        scratch_shapes=[pltpu.VMEM((N_DR * GW, 128), f32)],
    )(dbtt)


def _merge_heads(ref):
    return jnp.concatenate([ref[hh] for hh in range(H)], axis=1)


def _head_norm_bwd(xraw, gain, dy, ones_bd):
    r = lax.rsqrt(_head_sum(xraw * xraw, ones_bd) * (1.0 / DH) + RMS_EPS)
    xh = xraw * r
    gdy = dy * gain
    dx = r * (gdy - xh * (_head_sum(xh * gdy, ones_bd) * (1.0 / DH)))
    return dx, jnp.sum(dy * xh, axis=0, keepdims=True)


def _qk_bwd(dqr, dqp, dkr, dvh, p, gq, gk, rope):
    tm = 256

    def body(dqr_ref, dqp_ref, dkr_ref, dv_ref, qk_ref, gq_ref, gk_ref, cc_ref, cr_ref, sc_ref, sr_ref,
             dq_ref, dk_ref, dvo_ref, ggq_ref, ggk_ref):
        @pl.when(pl.program_id(0) == 0)
        def _():
            ggq_ref[...] = jnp.zeros_like(ggq_ref)
            ggk_ref[...] = jnp.zeros_like(ggk_ref)

        ones_bd = _head_ones()
        cs, sn = _rope_block(cc_ref, cr_ref, tm), _rope_block(sc_ref, sr_ref, tm)
        a = _merge_heads(dqr_ref)
        dyq = ((a * cs - _swap16(a) * sn) + _merge_heads(dqp_ref)) * QK_SCALE
        bk = _merge_heads(dkr_ref)
        dyk = bk * cs - _swap16(bk) * sn
        dq, gq_part = _head_norm_bwd(qk_ref[:, 0:DA], gq_ref[...], dyq, ones_bd)
        dk, gk_part = _head_norm_bwd(qk_ref[:, DA:2 * DA], gk_ref[...], dyk, ones_bd)
        dq_ref[...] = dq.astype(bf16)
        dk_ref[...] = dk.astype(bf16)
        dvo_ref[...] = _merge_heads(dv_ref).astype(bf16)
        ggq_ref[...] += gq_part
        ggk_ref[...] += gk_part

    hspec = pl.BlockSpec((H, tm, DH), lambda i: (0, i, 0))
    row = pl.BlockSpec((tm, DA), lambda i: (i, 0))
    fixed = pl.BlockSpec((1, DA), lambda i: (0, 0))
    piece = SDS((S, DA), bf16)
    return pl.pallas_call(
        body, name="qk_bwd", out_shape=(piece, piece, piece, SDS((1, DA), f32), SDS((1, DA), f32)), grid=(S // tm,),
        in_specs=[hspec, hspec, hspec, hspec, pl.BlockSpec((tm, 2 * DA), lambda i: (i, 0)), fixed, fixed]
        + _rope_specs(tm),
        out_specs=(row, row, row, fixed, fixed), compiler_params=_cp(40, dimension_semantics=("arbitrary",)),
    )(dqr, dqp, dkr, dvh, p, gq, gk, *rope)


def _ctx_bwd(dkc, dvc, pc, gk):
    def body(dkc_ref, dvc_ref, p_ref, gk_ref, dk_ref, dv_ref, ggk_ref):
        ones_bd = _head_ones()
        dk, gk_part = _head_norm_bwd(p_ref[:, 0:DA], gk_ref[...], _merge_heads(dkc_ref), ones_bd)
        dk_ref[...] = dk.astype(bf16)
        dv_ref[...] = _merge_heads(dvc_ref).astype(bf16)
        ggk_ref[...] = gk_part

    piece = SDS((L, DA), bf16)
    return pl.pallas_call(
        body, name="ctx_bwd", out_shape=(piece, piece, SDS((1, DA), f32)), in_specs=[VMEM_SPEC] * 4,
        out_specs=(VMEM_SPEC,) * 3,
    )(dkc, dvc, pc, gk)


def _grad_w_in_block(h, pa, pb, name, hc=None, pc=None, ctx_cols=None):
    tm = 512

    def body(*refs):
        if hc is None:
            h_ref, a_ref, b_ref, g_ref = refs
        else:
            h_ref, a_ref, b_ref, hc_ref, pc_ref, g_ref = refs

        @pl.when(pl.program_id(0) == 0)
        def _():
            g_ref[...] = jnp.zeros_like(g_ref)
            if hc is not None:
                g_ref[:, ctx_cols[0]:ctx_cols[1]] = _tn(hc_ref[...], pc_ref[...])

        hv = h_ref[...]
        g_ref[:, 0:DA] += _tn(hv, a_ref[...])
        g_ref[:, DA:2 * DA] += _tn(hv, b_ref[...])

    row = lambda i: (i, 0)
    fixed = lambda i: (0, 0)
    in_specs = [pl.BlockSpec((tm, D), row), pl.BlockSpec((tm, DA), row), pl.BlockSpec((tm, DA), row)]
    args = [h, pa, pb]
    if hc is not None:
        in_specs += [pl.BlockSpec((L, D), fixed), pl.BlockSpec((L, DA), fixed)]
        args += [hc, pc]
    return pl.pallas_call(
        body, name=name, out_shape=SDS((D, D), f32), grid=(S // tm,), in_specs=in_specs,
        out_specs=pl.BlockSpec((D, D), fixed), compiler_params=_cp(40, dimension_semantics=("arbitrary",)),
    )(*args)


def _norm_mod_bwd(x, dh, g, scale):
    r = lax.rsqrt(jnp.mean(x * x, axis=-1, keepdims=True) + RMS_EPS)
    xh = x * r
    y = xh * g
    dshift = jnp.sum(dh, axis=0, keepdims=True)
    dscale = jnp.sum(dh * y, axis=0, keepdims=True)
    dyn = dh * (1.0 + scale)
    dg = jnp.sum(dyn * xh, axis=0, keepdims=True)
    gdy = dyn * g
    dx = r * (gdy - xh * jnp.mean(xh * gdy, axis=-1, keepdims=True))
    return dx, dshift, dscale, dg


def _dh_grad_x(pieces, w4, xx, dy, norm_g, mrow, b_ada):
    tm = 256

    def body(*refs):
        p_refs = refs[:8]
        w_ref, x_ref, dy_ref, g_ref, m_ref, b_ref, gx_ref, dsh_ref, dsc_ref, dg_ref = refs[8:]

        @pl.when(pl.program_id(0) == 0)
        def _():
            dsh_ref[...] = jnp.zeros_like(dsh_ref)
            dsc_ref[...] = jnp.zeros_like(dsc_ref)
            dg_ref[...] = jnp.zeros_like(dg_ref)

        dh = None
        for j in range(4):
            for half in range(2):
                term = _tt(p_refs[2 * j + half][...], w_ref[j, :, half * DA:(half + 1) * DA])
                dh = term if dh is None else dh + term
        scale = m_ref[:, D:2 * D] + b_ref[:, D:2 * D]
        dx, dshift, dscale, dg = _norm_mod_bwd(x_ref[...], dh, g_ref[...], scale)
        gx_ref[...] = dy_ref[...] + dx
        dsh_ref[...] += dshift
        dsc_ref[...] += dscale
        dg_ref[...] += dg

    row = lambda i: (i, 0)
    fixed = lambda i: (0, 0)
    vec = SDS((1, D), f32)
    return pl.pallas_call(
        body, name="dh_grad_x", out_shape=(SDS((S, D), f32), vec, vec, vec), grid=(S // tm,),
        in_specs=[pl.BlockSpec((tm, DA), row)] * 8 + [pl.BlockSpec((4, D, D), lambda i: (0, 0, 0)),
                                                      pl.BlockSpec((tm, D), row), pl.BlockSpec((tm, D), row),
                                                      pl.BlockSpec((1, D), fixed), pl.BlockSpec((1, 3 * D), fixed),
                                                      pl.BlockSpec((1, 3 * D), fixed)],
        out_specs=(pl.BlockSpec((tm, D), row), pl.BlockSpec((1, D), fixed), pl.BlockSpec((1, D), fixed),
                   pl.BlockSpec((1, D), fixed)),
        compiler_params=_cp(56, dimension_semantics=("arbitrary",)),
    )(*pieces, w4, xx, dy, norm_g, mrow, b_ada)


def _dhc_sums(dkc_raw, dvc_m, w4, ctx2, norm_g, mrow_c, b_ada):
    def body(dk_ref, dv_ref, w0_ref, w1_ref, x_ref, g_ref, m_ref, b_ref, dsh_ref, dsc_ref, dg_ref):
        dh = _tt(dk_ref[...], w0_ref[:, DA:2 * DA]) + _tt(dv_ref[...], w1_ref[:, 0:DA])
        scale = m_ref[:, D:2 * D] + b_ref[:, D:2 * D]
        _, dshift, dscale, dg = _norm_mod_bwd(x_ref[...], dh, g_ref[...], scale)
        dsh_ref[...] = dshift
        dsc_ref[...] = dscale
        dg_ref[...] = dg

    fixed = lambda i: (0, 0)
    vec = SDS((1, D), f32)
    vspec = pl.BlockSpec((1, D), fixed)
    return pl.pallas_call(
        body, name="dhc_sums", out_shape=(vec, vec, vec), grid=(1,),
        in_specs=[pl.BlockSpec((L, DA), fixed), pl.BlockSpec((L, DA), fixed),
                  pl.BlockSpec((None, D, D), lambda i: (0, 0, 0)), pl.BlockSpec((None, D, D), lambda i: (1, 0, 0)),
                  pl.BlockSpec((L, D), fixed), vspec, pl.BlockSpec((1, 3 * D), fixed), pl.BlockSpec((1, 3 * D), fixed)],
        out_specs=(vspec, vspec, vspec), compiler_params=_cp(32),
    )(dkc_raw, dvc_m, w4, w4, ctx2, norm_g, mrow_c, b_ada)


def _rope_tables():
    nf = DH // 4
    inv = ROPE_THETA ** (-jnp.arange(nf, dtype=f32) / nf)
    ang_c = jnp.arange(GW, dtype=i32).astype(f32)[:, None] * inv
    ang_r = jnp.arange(ROWS, dtype=i32).astype(f32)[:, None] * inv
    zc, zr = jnp.zeros((GW, 2 * nf), f32), jnp.zeros((ROWS, 2 * nf), f32)
    ct_cos = jnp.tile(jnp.concatenate([zc, jnp.cos(ang_c), jnp.cos(ang_c)], axis=1), (1, H))
    ct_sin = jnp.tile(jnp.concatenate([zc, -jnp.sin(ang_c), jnp.sin(ang_c)], axis=1), (1, H))
    rt_cos = jnp.tile(jnp.concatenate([jnp.cos(ang_r), jnp.cos(ang_r), zr], axis=1), (1, H))
    rt_sin = jnp.tile(jnp.concatenate([-jnp.sin(ang_r), jnp.sin(ang_r), zr], axis=1), (1, H))
    rep8 = lambda t: jnp.broadcast_to(t[:, None, :], (ROWS, 8, DA)).reshape(ROWS * 8, DA)
    return ct_cos, rep8(rt_cos), ct_sin, rep8(rt_sin)


def _local_step(xx, ctx2, tgt, mrow, mrow_c, b_ada, norm_g, weights, q_norm_g, k_norm_g, rpb2, conv_w_full, conv_b,
                mid=None):
    gq = jnp.tile(q_norm_g, (1, H))
    gk = jnp.tile(k_norm_g, (1, H))
    rope = _rope_tables()
    rpb_pad = jnp.pad(rpb2[:, :, ::-1], ((0, 0), (0, 0), (0, 128 - N_DC)))

    h = _prenorm(xx, norm_g, mrow, b_ada, 256, "prenorm_x")
    hc = _prenorm(ctx2, norm_g, mrow_c, b_ada, L, "prenorm_ctx")
    btb = _bias_prep(rpb_pad)
    w4, w_out_full = weights(btb)
    p = _in_proj(h, w4)
    pc = _ctx_proj(hc, w4)
    qr, qp, kr, vh = _qk_prep(p, gq, gk, rope)
    kc, vc = _ctx_prep(pc, gk)
    o = _attn_fwd(qr, qp, kr, vh, kc, vc, btb)
    attn_g = _attn_gate(o, p)
    conv_g = _conv_fwd(p, conv_w_full, conv_b)
    dy, dmix, g_w_out, dgate, loss_sum = _out_proj_loss(attn_g, conv_g, w_out_full, xx, tgt, mrow, b_ada)

    du, dbg, dcg, dzc, g_conv_w, g_conv_b = _conv_bwd(dmix, p, conv_w_full, conv_b)
    do, dza = _gate_bwd(dmix, p, o)
    dqr, dqp, dkr, dvh, dkc, dvc, dbtb = _attn_bwd(qr, qp, kr, vh, kc, vc, btb, do)
    g_rpb = _bias_bwd(dbtb)
    dq, dk, dv, g_gq, g_gk = _qk_bwd(dqr, dqp, dkr, dvh, p, gq, gk, rope)
    dkc_raw, dvc_m, g_gk_c = _ctx_bwd(dkc, dvc, pc, gk)
    g_w_in = [
        _grad_w_in_block(h, dq, dk, "grad_w_in_0", hc, dkc_raw, (DA, 2 * DA)),
        _grad_w_in_block(h, dv, dza, "grad_w_in_1", hc, dvc_m, (0, DA)),
        _grad_w_in_block(h, du, dbg, "grad_w_in_2"),
        _grad_w_in_block(h, dcg, dzc, "grad_w_in_3"),
    ]
    b_ada_late = b_ada if mid is None else mid(g_w_in, g_w_out, b_ada)
    grad_x, dshift, dscale, dng = _dh_grad_x([dq, dk, dv, dza, du, dbg, dcg, dzc], w4, xx, dy, norm_g, mrow,
                                             b_ada_late)
    dshift_c, dscale_c, dng_c = _dhc_sums(dkc_raw, dvc_m, w4, ctx2, norm_g, mrow_c, b_ada_late)
    return dict(loss_sum=loss_sum, grad_x=grad_x, g_w_in=g_w_in, g_w_out=g_w_out, g_conv_w=g_conv_w,
                g_conv_b=g_conv_b, g_rpb=g_rpb, g_gq=g_gq, g_gk=g_gk, g_gk_c=g_gk_c, dshift=dshift, dscale=dscale,
                dgate=dgate, dng=dng, dshift_c=dshift_c, dscale_c=dscale_c, dng_c=dng_c)


def _pair_sum_w_in(gs, rs, cvec):
    tr = 128

    def body(c_ref, g0, g1, g2, g3, r0, r1, r2, r3, t32_ref, tb_ref):
        for q, (g_ref, r_ref) in enumerate(((g0, r0), (g1, r1), (g2, r2), (g3, r3))):
            t = g_ref[...] + r_ref[...]
            t32_ref[q] = t
            tb_ref[q] = t.astype(bf16)

    half = D // 2
    gs_spec = pl.BlockSpec((tr, D), lambda i, c: (c[0] * (half // tr) + i, 0))
    rs_spec = pl.BlockSpec((None, tr, D), lambda i, c: (0, i, 0))
    ospec = pl.BlockSpec((4, tr, D), lambda i, c: (0, i, 0))
    grid_spec = pltpu.PrefetchScalarGridSpec(num_scalar_prefetch=1, grid=(half // tr,), in_specs=[gs_spec] * 4 + [rs_spec] * 4,
                                             out_specs=(ospec, ospec))
    return pl.pallas_call(body, name="pair_sum_w_in", out_shape=(SDS((4, half, D), f32), SDS((4, half, D), bf16)),
                          grid_spec=grid_spec, compiler_params=_cp(40))(cvec, *gs, *rs)


def _pair_sum_w_out(g, r, cvec):
    hr = D // 8

    def body(c_ref, g0, g1, g2, g3, r_ref, t32_ref, tb_ref):
        for q, g_ref in enumerate((g0, g1, g2, g3)):
            t = g_ref[...] + r_ref[q]
            t32_ref[q] = t
            tb_ref[q] = t.astype(bf16)

    gspecs = [pl.BlockSpec((hr, D), lambda i, c, q=q: (2 * q + c[0], 0)) for q in range(4)]
    full = pl.BlockSpec((4, hr, D), lambda i, c: (0, 0, 0))
    grid_spec = pltpu.PrefetchScalarGridSpec(num_scalar_prefetch=1, grid=(1,), in_specs=gspecs + [full],
                                             out_specs=(full, full))
    return pl.pallas_call(body, name="pair_sum_w_out", out_shape=(SDS((4, hr, D), f32), SDS((4, hr, D), bf16)),
                          grid_spec=grid_spec)(cvec, g, g, g, g, r)


def _chip_sum(t32, r2, jvec, name):
    rows = t32.shape[1]
    tr = min(rows, 128)

    def body(j_ref, t_ref, r_ref, u_ref):
        u_ref[...] = ((t_ref[...] + r_ref[0].astype(f32)) + r_ref[1].astype(f32)) + r_ref[2].astype(f32)

    grid_spec = pltpu.PrefetchScalarGridSpec(
        num_scalar_prefetch=1, grid=(rows // tr,),
        in_specs=[pl.BlockSpec((None, tr, D), lambda i, j: (j[0], i, 0)), pl.BlockSpec((3, tr, D), lambda i, j: (0, i, 0))],
        out_specs=pl.BlockSpec((tr, D), lambda i, j: (i, 0)))
    return pl.pallas_call(body, name=name, out_shape=SDS((rows, D), f32), grid_spec=grid_spec)(jvec, t32, r2)


_PK = {}
_off = 0
for _name, _rows in (("dm", 24), ("dmc", 24), ("dng", 8), ("dng_c", 8), ("gq", 8), ("gk", 8), ("gk_c", 8),
                     ("rpb", H * N_DR), ("conv_b", 8), ("conv_w", 16), ("loss", 8)):
    _PK[_name] = (_off, _off + _rows)
    _off += _rows
PK_ROWS = _off
RS_B_ADA, RS_NORM_G, RS_GQ, RS_GK, RS_RPB, RS_CONV_B, RS_CONV_W, RS_DMC, RS_LOSS, RS_ROWS = (
    0, 24, 32, 40, 48, 168, 176, 192, 216, 224)


def _small_reduce(gathered):
    def body(g_ref, o_ref):
        tot = g_ref[0]
        for b in range(1, 8):
            tot = tot + g_ref[b]

        def rows(name):
            a, z = _PK[name]
            return tot[a:z]

        o_ref[RS_B_ADA:RS_B_ADA + 24] = rows("dm") + rows("dmc")
        o_ref[RS_NORM_G:RS_NORM_G + 8] = rows("dng") + rows("dng_c")
        gq = jnp.broadcast_to(jnp.sum(rows("gq"), axis=0, keepdims=True), (8, 128))
        gk = jnp.broadcast_to(jnp.sum(rows("gk") + rows("gk_c"), axis=0, keepdims=True), (8, 128))
        o_ref[RS_GQ:RS_GQ + 8] = gq + pltpu.roll(gq, DH, 1)
        o_ref[RS_GK:RS_GK + 8] = gk + pltpu.roll(gk, DH, 1)
        o_ref[RS_RPB:RS_RPB + H * N_DR] = rows("rpb")
        o_ref[RS_CONV_B:RS_CONV_B + 8] = rows("conv_b")
        o_ref[RS_CONV_W:RS_CONV_W + 16] = rows("conv_w")
        o_ref[RS_DMC:RS_DMC + 24] = rows("dmc")
        o_ref[RS_LOSS:RS_LOSS + 8] = rows("loss")

    return pl.pallas_call(body, name="small_reduce", out_shape=SDS((RS_ROWS, 128), f32), in_specs=[VMEM_SPEC],
                          out_specs=VMEM_SPEC)(gathered)


def _w_ada_grad(sc16, dm16s, w_ada_shard):
    def body(sc_ref, dm_ref, w_ref, g_ref, part_ref):
        dm = dm_ref[...]
        g_ref[...] = lax.dot_general(sc_ref[...], dm, (((0,), (0,)), ((), ())), precision=HIGHEST,
                                     preferred_element_type=f32)
        part_ref[...] = lax.dot_general(dm[8:16], w_ref[...], (((1,), (1,)), ((), ())), precision=HIGHEST,
                                        preferred_element_type=f32)

    ncol = w_ada_shard.shape[1]
    return pl.pallas_call(body, name="w_ada_grad", out_shape=(SDS((D, ncol), f32), SDS((8, D), f32)),
                          in_specs=[VMEM_SPEC] * 3, out_specs=(VMEM_SPEC, VMEM_SPEC), compiler_params=_cp(40),
                          )(sc16, dm16s, w_ada_shard)


def _c_ctx_grad(parts4, c_ctx_row):
    def body(p_ref, c_ref, o_ref):
        tot = ((p_ref[0] + p_ref[1]) + p_ref[2]) + p_ref[3]
        o_ref[...] = tot[0:1] * _dsilu(c_ref[...])

    return pl.pallas_call(body, name="c_ctx_grad", out_shape=SDS((1, D), f32), in_specs=[VMEM_SPEC, VMEM_SPEC],
                          out_specs=VMEM_SPEC)(parts4, c_ctx_row)


def _adamw(w, g, m, v, name):
    rows, cols = w.shape
    tr = 256 if rows % 256 == 0 else rows

    def body(w_ref, g_ref, m_ref, v_ref, d_ref, m2_ref, v2_ref):
        gv = g_ref[...]
        m2 = ADAM_B1 * m_ref[...] + (1.0 - ADAM_B1) * gv
        v2 = ADAM_B2 * v_ref[...] + (1.0 - ADAM_B2) * jnp.square(gv)
        m_hat = m2 / (1.0 - ADAM_B1 ** ADAM_STEP)
        v_hat = v2 / (1.0 - ADAM_B2 ** ADAM_STEP)
        d_ref[...] = -ADAM_LR * (m_hat / (jnp.sqrt(v_hat) + ADAM_EPS) + ADAM_WD * w_ref[...])
        m2_ref[...] = m2
        v2_ref[...] = v2

    spec = pl.BlockSpec((tr, cols), lambda i: (i, 0))
    shp = SDS((rows, cols), f32)
    return pl.pallas_call(body, name=name, out_shape=(shp, shp, shp), grid=(rows // tr,), in_specs=[spec] * 4,
                          out_specs=(spec, spec, spec))(w, g, m, v)


def _adamw_halves(w, g_mine, g_other, m, v, cvec, name):
    rows, cols = w.shape
    half = rows // 2
    tr = min(256, half)
    per_half = half // tr

    def body(c_ref, w_ref, ga_ref, gb_ref, m_ref, v_ref, g_ref, d_ref, m2_ref, v2_ref):
        in_my_half = (pl.program_id(0) // per_half) == c_ref[0]
        gv = jnp.where(in_my_half, ga_ref[...], gb_ref[...])
        g_ref[...] = gv
        m2 = ADAM_B1 * m_ref[...] + (1.0 - ADAM_B1) * gv
        v2 = ADAM_B2 * v_ref[...] + (1.0 - ADAM_B2) * jnp.square(gv)
        m_hat = m2 / (1.0 - ADAM_B1 ** ADAM_STEP)
        v_hat = v2 / (1.0 - ADAM_B2 ** ADAM_STEP)
        d_ref[...] = -ADAM_LR * (m_hat / (jnp.sqrt(v_hat) + ADAM_EPS) + ADAM_WD * w_ref[...])
        m2_ref[...] = m2
        v2_ref[...] = v2

    full = pl.BlockSpec((tr, cols), lambda i, c: (i, 0))
    part = pl.BlockSpec((tr, cols), lambda i, c: (i % per_half, 0))
    shp = SDS((rows, cols), f32)
    grid_spec = pltpu.PrefetchScalarGridSpec(num_scalar_prefetch=1, grid=(rows // tr,),
                                             in_specs=[full, part, part, full, full], out_specs=(full,) * 4)
    return pl.pallas_call(body, name=name, out_shape=(shp,) * 4, grid_spec=grid_spec)(cvec, w, g_mine, g_other, m, v)


def _rows128(a):
    return a.reshape(-1, 128)


def _pad_lanes(a):
    a2 = a.reshape(-1, a.shape[-1])
    return jnp.pad(a2, ((0, 0), (0, 128 - a2.shape[1])))


def kernel(x, c, ctx, c_ctx, w_ada, b_ada, norm_g, w_in, q_norm_g, k_norm_g, rpb, conv_w, conv_b, w_out, loss_target, m_c_ctx, m_w_ada, m_b_ada, m_norm_g, m_w_in, m_q_norm_g, m_k_norm_g, m_rpb, m_conv_w, m_conv_b, m_w_out, v_c_ctx, v_w_ada, v_b_ada, v_norm_g, v_w_in, v_q_norm_g, v_k_norm_g, v_rpb, v_conv_w, v_conv_b, v_w_out):
    xi, yi, ci = lax.axis_index("x"), lax.axis_index("y"), lax.axis_index("c")
    dev = 4 * xi + 2 * yi + ci
    chip = 2 * xi + yi
    cvec = jnp.reshape(ci, (1,)).astype(i32)
    jvec = jnp.reshape(chip, (1,)).astype(i32)
    w_ada_s = w_ada[0]
    ncol = w_ada_s.shape[1]

    c8 = _all_gather8(c.reshape(8, 128), "gather_c").reshape(8, D)
    cc = jnp.concatenate([c8, c_ctx.reshape(1, D), jnp.zeros((7, D), f32)], axis=0)
    m_shard, sc16 = _adaln_shard(cc, w_ada_s)

    conv_w_pad = jnp.pad(conv_w[0], ((0, 5), (0, 0)))
    m4, cw4 = _chip_gather([m_shard, conv_w_pad], "gather_mod")

    ssem, rsem, w4s, wo4s, token = _halves_start(
        [_cast_to_slot(w_in[0], jvec, "cast_w_in"), _cast_to_slot(w_out[0], jvec, "cast_w_out")], m4,
        "weights_ici_start")
    m_full = jnp.transpose(m4, (1, 0, 2)).reshape(16, 4 * ncol) + token[0:1, 0:1]
    mrow = lax.dynamic_slice(m_full, (dev, 0), (1, 3 * D))
    mrow_c = m_full[8:9]
    conv_w_full = jnp.transpose(cw4[:, 0:3, :], (1, 0, 2)).reshape(3, DC)

    def weights(after):
        w4w, wo4w = _halves_wait(ssem, rsem, [w4s, wo4s], after, "weights_ici_wait")
        w4f, wo4f = _halves_forward([w4w, wo4w], "weights_pair_forward")
        return w4f, wo4f.reshape(D, D)

    exchange = _exchange_copies([1, 1, 1, 1, 4])
    pending = {}

    def mid(g_w_in, g_w_out, b_ada_in):
        srcs = list(g_w_in) + [g_w_out]
        shapes = [SDS((1, D // 2, D), f32)] * 4 + [SDS((4, D // 8, D), f32)]
        out = _split_start(srcs, shapes, 8, exchange, "grad_pair_start")
        pending["ex"] = (out[0], out[1], list(out[2:7]), list(out[7:12]))
        return b_ada_in + out[12][0:1, 0:1]

    r = _local_step(x[0], ctx[0], loss_target[0], mrow, mrow_c, b_ada, norm_g, weights, q_norm_g, k_norm_g,
                    rpb[0], conv_w_full, conv_b, mid)
    dm = jnp.concatenate([r["dshift"], r["dscale"], r["dgate"]], axis=1)
    dmc = jnp.concatenate([r["dshift_c"], r["dscale_c"], jnp.zeros((1, D), f32)], axis=1)
    pack_parts = [_rows128(dm), _rows128(dmc), _rows128(r["dng"]), _rows128(r["dng_c"]), _rows128(r["g_gq"]),
                  _rows128(r["g_gk"]), _rows128(r["g_gk_c"]), r["g_rpb"].reshape(H * N_DR, 128),
                  _rows128(r["g_conv_b"]), _rows128(r["g_conv_w"][0:3]), jnp.pad(r["loss_sum"], ((0, 0), (0, 127)))]
    pack = jnp.concatenate([jnp.pad(p, ((0, -p.shape[0] % 8), (0, 0))) for p in pack_parts], axis=0)
    gathered = _all_gather8(pack, "gather_small")

    ex_ssem, ex_rsem, ex_srcs, ex_lands = pending["ex"]
    ex_g, ex = _split_wait(ex_ssem, ex_rsem, ex_srcs, ex_lands, gathered, exchange, "grad_pair_wait")
    t32, tb = _pair_sum_w_in(ex_g[0:4], ex[0:4], cvec)
    to32, tob = _pair_sum_w_out(ex_g[4], ex[4], cvec)
    sc_out = _split_start([tb, tob], [SDS((3, D // 2, D), bf16), SDS((3, D // 8, D), bf16)], 6, _scatter_copies,
                          "grad_chip_start")
    sc16 = sc16 + sc_out[6][0:1, 0:1]

    red = _small_reduce(gathered)
    g_b_ada = red[RS_B_ADA:RS_B_ADA + 24].reshape(1, 3 * D)
    g_norm_g = red[RS_NORM_G:RS_NORM_G + 8].reshape(1, D)
    g_q = red[RS_GQ:RS_GQ + 1, 0:DH]
    g_k = red[RS_GK:RS_GK + 1, 0:DH]
    g_rpb = jnp.pad(red[RS_RPB:RS_RPB + H * N_DR, N_DC - 1::-1], ((0, 0), (0, 128 - N_DC)))
    g_conv_b = red[RS_CONV_B:RS_CONV_B + 4].reshape(1, DC)
    g_conv_w_full = red[RS_CONV_W:RS_CONV_W + 12].reshape(3, DC)
    assert pack.shape[0] == PK_ROWS
    g_conv_w_s = lax.dynamic_slice(g_conv_w_full, (0, chip * 128), (3, 128))
    dmc_tot = red[RS_DMC:RS_DMC + 24].reshape(1, 3 * D)
    loss = red[RS_LOSS, 0] * (0.5 / D)

    a0, a1 = _PK["dm"]
    dm8 = gathered[:, a0:a1, :].reshape(8, 3 * D)
    dm16 = jnp.concatenate([dm8, dmc_tot, jnp.zeros((7, 3 * D), f32)], axis=0)
    dm16s = lax.dynamic_slice(dm16, (0, chip * ncol), (16, ncol))
    g_w_ada_s, cpart = _w_ada_grad(sc16, dm16s, w_ada_s)
    d_w_ada, nm_w_ada, nv_w_ada = _adamw(w_ada_s, g_w_ada_s, m_w_ada[0], v_w_ada[0], "adamw_w_ada")

    _, (r2, ro2) = _split_wait(sc_out[0], sc_out[1], [sc_out[2], sc_out[3]], [sc_out[4], sc_out[5]], nm_w_ada,
                               _scatter_copies, "grad_chip_wait")
    u_in = _chip_sum(t32, r2, jvec, "chip_sum_w_in")
    u_out = _chip_sum(to32, ro2, jvec, "chip_sum_w_out")

    (cparts4,) = _chip_gather([cpart], "gather_c_ctx_parts", after=u_out)
    g_c_ctx = _c_ctx_grad(cparts4, c_ctx.reshape(1, D))

    def small_pack(a_c_ctx, a_b_ada, a_norm_g, a_q, a_k, a_rpb, a_conv_w, a_conv_b):
        parts = [_rows128(a_c_ctx), _rows128(a_b_ada), _rows128(a_norm_g), _pad_lanes(a_q), _pad_lanes(a_k),
                 a_rpb, _rows128(a_conv_w), _rows128(a_conv_b)]
        return jnp.concatenate([jnp.pad(p, ((0, -p.shape[0] % 8), (0, 0))) for p in parts], axis=0)

    w_pk = small_pack(c_ctx, b_ada, norm_g, q_norm_g, k_norm_g, _pad_lanes(rpb[0]), conv_w[0], conv_b)
    g_pk = small_pack(g_c_ctx, g_b_ada, g_norm_g, g_q, g_k, g_rpb, g_conv_w_s, g_conv_b)
    m_pk = small_pack(m_c_ctx, m_b_ada, m_norm_g, m_q_norm_g, m_k_norm_g, _pad_lanes(m_rpb[0]), m_conv_w[0], m_conv_b)
    v_pk = small_pack(v_c_ctx, v_b_ada, v_norm_g, v_q_norm_g, v_k_norm_g, _pad_lanes(v_rpb[0]), v_conv_w[0], v_conv_b)
    d_pk, nm_pk, nv_pk = _adamw(w_pk, g_pk, m_pk, v_pk, "adamw_small")

    o_in, o_out = _sibling_send([u_in, u_out], "grad_pair_send")
    g_w_in_s, d_w_in, nm_w_in, nv_w_in = _adamw_halves(w_in[0], u_in, o_in, m_w_in[0], v_w_in[0], cvec, "adamw_w_in")
    g_w_out_s, d_w_out, nm_w_out, nv_w_out = _adamw_halves(w_out[0], u_out, o_out, m_w_out[0], v_w_out[0], cvec,
                                                           "adamw_w_out")

    def small_unpack(pk):
        o = 0
        out = []
        for rows, fn in ((8, lambda a: a.reshape(D)), (24, lambda a: a.reshape(1, 3 * D)), (8, lambda a: a.reshape(1, D)),
                         (1, lambda a: a[:, 0:DH]), (1, lambda a: a[:, 0:DH]),
                         (H * N_DR, lambda a: a[:, 0:N_DC].reshape(1, H, N_DR, N_DC)),
                         (3, lambda a: a.reshape(1, 3, 128)), (4, lambda a: a.reshape(1, DC))):
            out.append(fn(pk[o:o + rows]))
            o += rows + (-rows % 8)
        return out

    def ordered(small, big_w_ada, big_w_in, big_w_out):
        s_c_ctx, s_b_ada, s_norm_g, s_q, s_k, s_rpb, s_conv_w, s_conv_b = small
        return [s_c_ctx, big_w_ada[None], s_b_ada, s_norm_g, big_w_in[None], s_q, s_k, s_rpb, s_conv_w, s_conv_b,
                big_w_out[None]]

    grads = ordered(small_unpack(g_pk), g_w_ada_s, g_w_in_s, g_w_out_s)
    deltas = ordered(small_unpack(d_pk), d_w_ada, d_w_in, d_w_out)
    new_m = ordered(small_unpack(nm_pk), nm_w_ada, nm_w_in, nm_w_out)
    new_v = ordered(small_unpack(nv_pk), nv_w_ada, nv_w_in, nv_w_out)
    return (loss, r["grad_x"][None], *grads, *deltas, *new_m, *new_v)
```

```python
import functools

import jax
import jax.numpy as jnp
from jax import lax
from jax.experimental import pallas as pl
from jax.experimental.pallas import tpu as pltpu

f32, bf16, i32 = jnp.float32, jnp.bfloat16, jnp.int32
MESH = pl.DeviceIdType.MESH
HIGHEST = lax.Precision.HIGHEST

D = 1024
S = 2048
L = 256
GW = 64
ROWS = S // GW
H = 8
DH = 64
DA = H * DH
DC = 512
WIN_H, WIN_W = 8, 16
N_DR, N_DC = 2 * WIN_H - 1, 2 * WIN_W - 1
RMS_EPS = 1e-6
ROPE_THETA = 10000.0
QK_SCALE = DH ** -0.5
NEG = -1e30

QB = 128
NQB = S // QB
KR = 9
KB = KR * GW
TILE_GEOM = ((0, 0), (2, 0), (4, 0), (28, 23), (30, 23))
NT = len(TILE_GEOM)

ADAM_LR, ADAM_B1, ADAM_B2, ADAM_EPS, ADAM_WD, ADAM_STEP = 0.001, 0.9, 0.999, 1e-08, 0.01, 10

VMEM_SPEC = pl.BlockSpec(memory_space=pltpu.VMEM)
ANY_SPEC = pl.BlockSpec(memory_space=pl.ANY)
SMEM_SPEC = pl.BlockSpec(memory_space=pltpu.SMEM)
SDS = jax.ShapeDtypeStruct


_pallas_call = pl.pallas_call


def _hbm_call(body, *, out_shape, in_specs=None, out_specs=None, grid_spec=None, **kw):
    n_pre = 0
    if grid_spec is not None:
        ispecs, ospecs, n_pre = grid_spec.in_specs, grid_spec.out_specs, grid_spec.num_scalar_prefetch
        kw["grid_spec"] = grid_spec
    else:
        ispecs, ospecs = in_specs, out_specs
        kw.update(in_specs=in_specs, out_specs=out_specs)

    def blocked(spec):
        return isinstance(spec, pl.BlockSpec) and spec.block_shape is not None

    single = not isinstance(out_shape, (tuple, list))
    shapes = [out_shape] if single else list(out_shape)
    ospec_list = list(ospecs) if isinstance(ospecs, (tuple, list)) else [ospecs]
    shapes = [pltpu.HBM(s.shape, s.dtype) if blocked(sp) else s for s, sp in zip(shapes, ospec_list)]
    call = _pallas_call(body, out_shape=shapes[0] if single else tuple(shapes), **kw)

    def run(*args):
        arrays = [pltpu.with_memory_space_constraint(a, pltpu.HBM) if blocked(sp) else a
                  for a, sp in zip(args[n_pre:], ispecs)]
        return call(*args[:n_pre], *arrays)

    return run


def _cp(vmem_mb=None, **kw):
    if vmem_mb is not None:
        kw["vmem_limit_bytes"] = vmem_mb << 20
    return pltpu.CompilerParams(**kw)


def _silu(z):
    return z * jax.nn.sigmoid(z)


def _dsilu(z):
    sg = jax.nn.sigmoid(z)
    return sg * (1.0 + z * (1.0 - sg))


def _row_start(i):
    return min(max(i - WIN_H // 2, 0), ROWS - WIN_H)


def _my_pos():
    return lax.axis_index("x"), lax.axis_index("y"), lax.axis_index("c")


def _flip(v, bit):
    return 1 - v if bit else v


def _all_gather8(xin, name):
    R, N = xin.shape

    def body(x_ref, o_ref, ssem, rsem, lsem):
        x, y, c = _my_pos()
        me = 4 * x + 2 * y + c
        own = pltpu.make_async_copy(x_ref, o_ref.at[me], lsem)
        own.start()
        sends = []
        for k in range(1, 8):
            tgt = (_flip(x, (k >> 2) & 1), _flip(y, (k >> 1) & 1), _flip(c, k & 1))
            cp = pltpu.make_async_remote_copy(src_ref=x_ref, dst_ref=o_ref.at[me], send_sem=ssem.at[k - 1],
                                              recv_sem=rsem.at[k - 1], device_id=tgt, device_id_type=MESH)
            cp.start()
            sends.append(cp)
        for k in range(1, 8):
            tgt = (_flip(x, (k >> 2) & 1), _flip(y, (k >> 1) & 1), _flip(c, k & 1))
            peer = 4 * tgt[0] + 2 * tgt[1] + tgt[2]
            pltpu.make_async_remote_copy(src_ref=x_ref, dst_ref=o_ref.at[peer], send_sem=ssem.at[k - 1],
                                         recv_sem=rsem.at[k - 1], device_id=tgt, device_id_type=MESH).wait_recv()
        for cp in sends:
            cp.wait_send()
        own.wait()

    return _hbm_call(
        body, name=name, out_shape=SDS((8, R, N), xin.dtype), in_specs=[VMEM_SPEC], out_specs=VMEM_SPEC,
        scratch_shapes=[pltpu.SemaphoreType.DMA((7,)), pltpu.SemaphoreType.DMA((7,)), pltpu.SemaphoreType.DMA],
    )(xin)


def _chip_gather(smalls, name, after=None):
    ns = len(smalls)

    def body(*refs):
        s_in, s_out = refs[:ns], refs[ns + 1:2 * ns + 1]
        ssem, rsem, lsem = refs[2 * ns + 1:]
        x, y, c = _my_pos()
        j = 2 * x + y
        chips = _peer_chips(x, y, c)
        local = [pltpu.make_async_copy(s_in[a], s_out[a].at[j], lsem.at[a]) for a in range(ns)]
        for cp in local:
            cp.start()
        sends = []
        for a in range(ns):
            for k in range(3):
                cp = pltpu.make_async_remote_copy(src_ref=s_in[a], dst_ref=s_out[a].at[j], send_sem=ssem.at[3 * a + k],
                                                  recv_sem=rsem.at[3 * a + k], device_id=chips[k][0], device_id_type=MESH)
                cp.start()
                sends.append(cp)
        for a in range(ns):
            for k in range(3):
                pltpu.make_async_remote_copy(src_ref=s_in[a], dst_ref=s_out[a].at[chips[k][1]], send_sem=ssem.at[3 * a + k],
                                             recv_sem=rsem.at[3 * a + k], device_id=chips[k][0],
                                             device_id_type=MESH).wait_recv()
        for cp in sends:
            cp.wait_send()
        for cp in local:
            cp.wait()

    return _hbm_call(
        body, name=name, out_shape=[SDS((4,) + a.shape, a.dtype) for a in smalls],
        in_specs=[VMEM_SPEC] * ns + [ANY_SPEC], out_specs=[VMEM_SPEC] * ns,
        scratch_shapes=[pltpu.SemaphoreType.DMA((3 * ns,)), pltpu.SemaphoreType.DMA((3 * ns,)),
                        pltpu.SemaphoreType.DMA((ns,))],
    )(*smalls, smalls[0] if after is None else after)


HBM_SPEC = pl.BlockSpec(memory_space=pltpu.HBM)
SEM_SPEC = pl.BlockSpec(memory_space=pltpu.SEMAPHORE)
DATAFLOW = pltpu.SideEffectType.DATAFLOW_SIDE_EFFECTING


def _peer_chips(x, y, c):
    out = []
    for k in range(1, 4):
        px, py = _flip(x, (k >> 1) & 1), _flip(y, k & 1)
        out.append(((px, py, c), 2 * px + py))
    return out


def _half_copies(srcs, dsts, ssem, rsem):
    x, y, c = _my_pos()
    j = 2 * x + y
    pairs = []
    for a in range(len(srcs)):
        half = srcs[a].shape[1] // 2
        mine = pl.ds(pl.multiple_of(c * half, 8), half)
        for k, (dev, pj) in enumerate(_peer_chips(x, y, c)):
            sem = 3 * a + k
            send = pltpu.make_async_remote_copy(src_ref=srcs[a].at[j, mine], dst_ref=dsts[a].at[j, mine],
                                                send_sem=ssem.at[sem], recv_sem=rsem.at[sem], device_id=dev,
                                                device_id_type=MESH)
            arrive = pltpu.make_async_remote_copy(src_ref=srcs[a].at[j, mine], dst_ref=dsts[a].at[pj, mine],
                                                  send_sem=ssem.at[sem], recv_sem=rsem.at[sem], device_id=dev,
                                                  device_id_type=MESH)
            pairs.append((send, arrive))
    return pairs


def _halves_start(bigs, after, name):
    nb = len(bigs)

    def body(*refs):
        b_in = refs[:nb]
        ssem, rsem = refs[nb + 1], refs[nb + 2]
        b_out = refs[nb + 3:2 * nb + 3]
        token = refs[2 * nb + 3]
        for send, _ in _half_copies(b_in, b_out, ssem, rsem):
            send.start()
        token[...] = jnp.zeros_like(token)

    out_shape = (pltpu.SemaphoreType.DMA((3 * nb,)), pltpu.SemaphoreType.DMA((3 * nb,)),
                 *[pltpu.HBM(b.shape, b.dtype) for b in bigs], SDS((8, 128), f32))
    return _hbm_call(
        body, name=name, out_shape=out_shape, in_specs=[HBM_SPEC] * nb + [ANY_SPEC],
        out_specs=(SEM_SPEC, SEM_SPEC, *[HBM_SPEC] * nb, VMEM_SPEC),
        input_output_aliases={a: 2 + a for a in range(nb)}, compiler_params=_cp(has_side_effects=DATAFLOW),
    )(*[pltpu.with_memory_space_constraint(b, pltpu.HBM) for b in bigs], after)


def _halves_wait(ssem, rsem, bigs, after, name):
    nb = len(bigs)

    def body(*refs):
        b_in = refs[:nb]
        ssem_ref, rsem_ref = refs[nb], refs[nb + 1]
        for send, arrive in _half_copies(b_in, b_in, ssem_ref, rsem_ref):
            send.wait_send()
            arrive.wait_recv()

    return _hbm_call(
        body, name=name, out_shape=tuple(pltpu.HBM(b.shape, b.dtype) for b in bigs),
        in_specs=[HBM_SPEC] * nb + [SEM_SPEC, SEM_SPEC, ANY_SPEC], out_specs=tuple([HBM_SPEC] * nb),
        input_output_aliases={a: a for a in range(nb)}, compiler_params=_cp(has_side_effects=DATAFLOW),
    )(*bigs, ssem, rsem, after)


def _halves_forward(bigs, name):
    nb = len(bigs)

    def body(*refs):
        b_in, b_out = refs[:nb], refs[nb:2 * nb]
        ssem, rsem = refs[2 * nb:]
        x, y, c = _my_pos()
        sib = (x, y, 1 - c)
        sends = []
        for a in range(nb):
            half = b_in[a].shape[1] // 2
            mine = pl.ds(pl.multiple_of(c * half, 8), half)
            for k, (_, pj) in enumerate(_peer_chips(x, y, c)):
                cp = pltpu.make_async_remote_copy(src_ref=b_in[a].at[pj, mine], dst_ref=b_out[a].at[pj, mine],
                                                  send_sem=ssem.at[3 * a + k], recv_sem=rsem.at[3 * a + k],
                                                  device_id=sib, device_id_type=MESH)
                cp.start()
                sends.append(cp)
        for a in range(nb):
            half = b_in[a].shape[1] // 2
            other = pl.ds(pl.multiple_of((1 - c) * half, 8), half)
            for k, (_, pj) in enumerate(_peer_chips(x, y, c)):
                pltpu.make_async_remote_copy(src_ref=b_in[a].at[pj, other], dst_ref=b_out[a].at[pj, other],
                                             send_sem=ssem.at[3 * a + k], recv_sem=rsem.at[3 * a + k],
                                             device_id=sib, device_id_type=MESH).wait_recv()
        for cp in sends:
            cp.wait_send()

    return _hbm_call(
        body, name=name, out_shape=[SDS(b.shape, b.dtype) for b in bigs], in_specs=[ANY_SPEC] * nb,
        out_specs=[ANY_SPEC] * nb, input_output_aliases={a: a for a in range(nb)},
        scratch_shapes=[pltpu.SemaphoreType.DMA((3 * nb,)), pltpu.SemaphoreType.DMA((3 * nb,))],
    )(*bigs)


def _cast_to_slot(w, jvec, name):
    rows, cols = w.shape
    tr = 256

    def body(j_ref, w_ref, o_ref):
        o_ref[...] = w_ref[...].astype(bf16)

    grid_spec = pltpu.PrefetchScalarGridSpec(
        num_scalar_prefetch=1, grid=(rows // tr,), in_specs=[pl.BlockSpec((tr, cols), lambda i, j: (i, 0))],
        out_specs=pl.BlockSpec((None, tr, cols), lambda i, j: (j[0], i, 0)))
    return _hbm_call(body, name=name, out_shape=SDS((4, rows, cols), bf16), grid_spec=grid_spec)(jvec, w)


def _exchange_copies(n_blocks):
    def make(srcs, lands, ssem, rsem):
        x, y, c = _my_pos()
        cps = []
        sem = 0
        for a in range(len(srcs)):
            rb = srcs[a].shape[0] // n_blocks[a]
            half = rb // 2
            for jb in range(n_blocks[a]):
                theirs = pl.ds(pl.multiple_of(jb * rb + (1 - c) * half, 8), half)
                cps.append(pltpu.make_async_remote_copy(src_ref=srcs[a].at[theirs], dst_ref=lands[a].at[jb],
                                                        send_sem=ssem.at[sem], recv_sem=rsem.at[sem],
                                                        device_id=(x, y, 1 - c), device_id_type=MESH))
                sem += 1
        return cps
    return make


def _scatter_copies(srcs, lands, ssem, rsem):
    x, y, c = _my_pos()
    cps = []
    for a in range(len(srcs)):
        for k, (dev, pj) in enumerate(_peer_chips(x, y, c)):
            cps.append(pltpu.make_async_remote_copy(src_ref=srcs[a].at[pj], dst_ref=lands[a].at[k],
                                                    send_sem=ssem.at[3 * a + k], recv_sem=rsem.at[3 * a + k],
                                                    device_id=dev, device_id_type=MESH))
    return cps


def _split_start(srcs, land_shapes, n_cp, make, name):
    ns, nl = len(srcs), len(land_shapes)

    def body(*refs):
        s_in = refs[:ns]
        ssem, rsem = refs[ns + nl], refs[ns + nl + 1]
        l_out = refs[2 * ns + nl + 2:2 * ns + 2 * nl + 2]
        token = refs[2 * ns + 2 * nl + 2]
        for cp in make(s_in, l_out, ssem, rsem):
            cp.start()
        token[...] = jnp.zeros_like(token)

    lands = [pltpu.with_memory_space_constraint(lax.empty(sh.shape, sh.dtype), pltpu.HBM) for sh in land_shapes]
    out_shape = (pltpu.SemaphoreType.DMA((n_cp,)), pltpu.SemaphoreType.DMA((n_cp,)),
                 *[pltpu.HBM(b.shape, b.dtype) for b in srcs], *[pltpu.HBM(b.shape, b.dtype) for b in land_shapes],
                 SDS((8, 128), f32))
    return _hbm_call(
        body, name=name, out_shape=out_shape, in_specs=[HBM_SPEC] * (ns + nl),
        out_specs=(SEM_SPEC, SEM_SPEC, *[HBM_SPEC] * (ns + nl), VMEM_SPEC),
        input_output_aliases={i: 2 + i for i in range(ns + nl)}, compiler_params=_cp(has_side_effects=DATAFLOW),
    )(*[pltpu.with_memory_space_constraint(b, pltpu.HBM) for b in srcs], *lands)


def _split_wait(ssem, rsem, srcs, lands, after, make, name):
    ns, nl = len(srcs), len(lands)

    def body(*refs):
        s_in, l_in = refs[:ns], refs[ns:ns + nl]
        ssem_ref, rsem_ref = refs[ns + nl], refs[ns + nl + 1]
        for cp in make(s_in, l_in, ssem_ref, rsem_ref):
            cp.wait_send()
            cp.wait_recv()

    outs = _hbm_call(
        body, name=name, out_shape=tuple(pltpu.HBM(b.shape, b.dtype) for b in (*srcs, *lands)),
        in_specs=[HBM_SPEC] * (ns + nl) + [SEM_SPEC, SEM_SPEC, ANY_SPEC], out_specs=tuple([HBM_SPEC] * (ns + nl)),
        input_output_aliases={i: i for i in range(ns + nl)}, compiler_params=_cp(has_side_effects=DATAFLOW),
    )(*srcs, *lands, ssem, rsem, after)
    return list(outs[:ns]), list(outs[ns:])


def _sibling_send(halves, name):
    n = len(halves)

    def body(*refs):
        ins, outs = refs[:n], refs[n:2 * n]
        ssem, rsem = refs[2 * n:]
        x, y, c = _my_pos()
        cps = []
        for a in range(n):
            cp = pltpu.make_async_remote_copy(src_ref=ins[a], dst_ref=outs[a], send_sem=ssem.at[a],
                                              recv_sem=rsem.at[a], device_id=(x, y, 1 - c), device_id_type=MESH)
            cp.start()
            cps.append(cp)
        for cp in cps:
            cp.wait_recv()
        for cp in cps:
            cp.wait_send()

    out_shape = [SDS(h.shape, h.dtype) for h in halves]
    return _hbm_call(
        body, name=name, out_shape=out_shape, in_specs=[ANY_SPEC] * n, out_specs=[ANY_SPEC] * n,
        scratch_shapes=[pltpu.SemaphoreType.DMA((n,)), pltpu.SemaphoreType.DMA((n,))],
    )(*halves)


def _adaln_shard(cc, w_ada_shard):
    def body(c_ref, w_ref, m_ref, sc_ref):
        sc = _silu(c_ref[...])
        sc_ref[...] = sc
        m_ref[...] = jnp.dot(sc, w_ref[...], precision=HIGHEST, preferred_element_type=f32)

    return _hbm_call(
        body, name="adaln_shard", out_shape=(SDS((16, w_ada_shard.shape[1]), f32), SDS((16, D), f32)),
        in_specs=[VMEM_SPEC, VMEM_SPEC], out_specs=(VMEM_SPEC, VMEM_SPEC), compiler_params=_cp(32),
    )(cc, w_ada_shard)


def _prenorm(xx, norm_g, mrow, b_ada, tm, name):
    n = xx.shape[0]

    def body(x_ref, g_ref, m_ref, b_ref, h_ref):
        x = x_ref[...]
        shift = m_ref[:, 0:D] + b_ref[:, 0:D]
        scale = m_ref[:, D:2 * D] + b_ref[:, D:2 * D]
        r = lax.rsqrt(jnp.mean(x * x, axis=-1, keepdims=True) + RMS_EPS)
        y = (x * r) * g_ref[...]
        h_ref[...] = (y * (1.0 + scale) + shift).astype(bf16)

    row = lambda i: (i, 0)
    fixed = lambda i: (0, 0)
    return _hbm_call(
        body, name=name, out_shape=SDS((n, D), bf16), grid=(n // tm,),
        in_specs=[pl.BlockSpec((tm, D), row), pl.BlockSpec((1, D), fixed), pl.BlockSpec((1, 3 * D), fixed),
                  pl.BlockSpec((1, 3 * D), fixed)],
        out_specs=pl.BlockSpec((tm, D), row),
    )(xx, norm_g, mrow, b_ada)


def _in_proj(h, w4):
    tm = 512

    def body(h_ref, w_ref, p_ref):
        p_ref[...] = jnp.dot(h_ref[...], w_ref[...], preferred_element_type=f32)

    return _hbm_call(
        body, name="in_proj", out_shape=SDS((S, 4 * D), f32), grid=(4, S // tm),
        in_specs=[pl.BlockSpec((tm, D), lambda j, k: (k, 0)), pl.BlockSpec((None, D, D), lambda j, k: (j, 0, 0))],
        out_specs=pl.BlockSpec((tm, D), lambda j, k: (k, j)),
    )(h, w4)


def _ctx_proj(hc, w4):
    def body(h_ref, w0_ref, w1_ref, p_ref):
        hv = h_ref[...]
        p_ref[:, 0:DA] = jnp.dot(hv, w0_ref[:, DA:2 * DA], preferred_element_type=f32)
        p_ref[:, DA:2 * DA] = jnp.dot(hv, w1_ref[:, 0:DA], preferred_element_type=f32)

    return _hbm_call(
        body, name="ctx_proj", out_shape=SDS((L, 2 * DA), f32), grid=(1,),
        in_specs=[pl.BlockSpec((L, D), lambda i: (0, 0)), pl.BlockSpec((None, D, D), lambda i: (0, 0, 0)),
                  pl.BlockSpec((None, D, D), lambda i: (1, 0, 0))],
        out_specs=pl.BlockSpec((L, 2 * DA), lambda i: (0, 0)),
    )(hc, w4, w4)


def _head_ones():
    r = lax.broadcasted_iota(i32, (DA, DA), 0) // DH
    c = lax.broadcasted_iota(i32, (DA, DA), 1) // DH
    return (r == c).astype(bf16)


def _head_sum(v, ones_bd):
    hi = v.astype(bf16)
    lo = (v - hi.astype(f32)).astype(bf16)
    return jnp.dot(hi, ones_bd, preferred_element_type=f32) + jnp.dot(lo, ones_bd, preferred_element_type=f32)


def _swap16(v):
    lane = lax.broadcasted_iota(i32, v.shape, 1)
    return jnp.where((lane & 31) < 16, pltpu.roll(v, DA - 16, 1), pltpu.roll(v, 16, 1))


def _rope_block(ct_ref, rt_ref, tm):
    rows = [jnp.tile(rt_ref[8 * j:8 * j + 8, :], (GW // 8, 1)) for j in range(tm // GW)]
    return jnp.tile(ct_ref[...], (tm // GW, 1)) + jnp.concatenate(rows, axis=0)


def _rope_specs(tm):
    col = pl.BlockSpec((GW, DA), lambda i: (0, 0))
    row = pl.BlockSpec((8 * tm // GW, DA), lambda i: (i, 0))
    return [col, row, col, row]


def _qk_prep(p, gq, gk, rope):
    tm = 256

    def body(qk_ref, v_ref, gq_ref, gk_ref, cc_ref, cr_ref, sc_ref, sr_ref, qr_ref, qp_ref, kr_ref, vh_ref):
        ones_bd = _head_ones()
        cs, sn = _rope_block(cc_ref, cr_ref, tm), _rope_block(sc_ref, sr_ref, tm)
        q = qk_ref[:, 0:DA]
        k = qk_ref[:, DA:2 * DA]
        yq = (q * lax.rsqrt(_head_sum(q * q, ones_bd) * (1.0 / DH) + RMS_EPS)) * gq_ref[...]
        yk = (k * lax.rsqrt(_head_sum(k * k, ones_bd) * (1.0 / DH) + RMS_EPS)) * gk_ref[...]
        qr = (yq * cs + _swap16(yq) * sn) * QK_SCALE
        qp = yq * QK_SCALE
        kr = yk * cs + _swap16(yk) * sn
        vv = v_ref[...]
        for hh in range(H):
            sl = slice(hh * DH, (hh + 1) * DH)
            qr_ref[hh] = qr[:, sl].astype(bf16)
            qp_ref[hh] = qp[:, sl].astype(bf16)
            kr_ref[hh] = kr[:, sl].astype(bf16)
            vh_ref[hh] = vv[:, sl].astype(bf16)

    hm = SDS((H, S, DH), bf16)
    hspec = pl.BlockSpec((H, tm, DH), lambda i: (0, i, 0))
    fixed = lambda i: (0, 0)
    return _hbm_call(
        body, name="qk_prep", out_shape=(hm, hm, hm, hm), grid=(S // tm,),
        in_specs=[pl.BlockSpec((tm, 2 * DA), lambda i: (i, 0)), pl.BlockSpec((tm, DA), lambda i: (i, 2)),
                  pl.BlockSpec((1, DA), fixed), pl.BlockSpec((1, DA), fixed)] + _rope_specs(tm),
        out_specs=(hspec, hspec, hspec, hspec),
    )(p, p, gq, gk, *rope)


def _ctx_prep(pc, gk):
    def body(p_ref, gk_ref, kc_ref, vc_ref):
        ones_bd = _head_ones()
        k = p_ref[:, 0:DA]
        yk = (k * lax.rsqrt(_head_sum(k * k, ones_bd) * (1.0 / DH) + RMS_EPS)) * gk_ref[...]
        vv = p_ref[:, DA:2 * DA]
        for hh in range(H):
            sl = slice(hh * DH, (hh + 1) * DH)
            kc_ref[hh] = yk[:, sl].astype(bf16)
            vc_ref[hh] = vv[:, sl].astype(bf16)

    hm = SDS((H, L, DH), bf16)
    return _hbm_call(
        body, name="ctx_prep", out_shape=(hm, hm), in_specs=[VMEM_SPEC, VMEM_SPEC], out_specs=(VMEM_SPEC, VMEM_SPEC),
    )(pc, gk)


def _tile_pieces():
    out = []
    for (i0, u0) in TILE_GEOM:
        rows = []
        for j in range(2):
            i = i0 + j
            rs = _row_start(i)
            rows.append([(u0 + u - i + WIN_H - 1) if rs <= u0 + u < rs + WIN_H else None for u in range(KR)])
        out.append(rows)
    return out


def _bias_prep(rpb_rev_pad):
    pieces = _tile_pieces()

    def body(r_ref, o_ref):
        rp = r_ref[...]
        xs = jnp.broadcast_to(rp[:, None, :], (N_DR, GW, 128)).reshape(N_DR * GW, 128)
        row = lax.broadcasted_iota(i32, xs.shape, 0)
        lane = lax.broadcasted_iota(i32, xs.shape, 1)
        for b in range(6):
            xs = jnp.where(((row >> b) & 1) == 1, pltpu.roll(xs, 1 << b, 1), xs)
        xs = pltpu.roll(xs, 128 - (WIN_W - 1), 1)
        k = row & (GW - 1)
        c0 = jnp.clip(lane - WIN_W // 2, 0, GW - WIN_W)
        xs = jnp.where((k >= c0) & (k < c0 + WIN_W), xs, NEG)
        neg = jnp.full((GW, GW), NEG, f32)
        for t in range(NT):
            for j in range(2):
                for u in range(KR):
                    dr = pieces[t][j][u]
                    piece = neg if dr is None else xs[dr * GW:(dr + 1) * GW, 0:GW]
                    o_ref[t, u * GW:(u + 1) * GW, j * GW:(j + 1) * GW] = piece

    return _hbm_call(
        body, name="bias_prep", out_shape=SDS((H, NT, KB, QB), f32), grid=(H,),
        in_specs=[pl.BlockSpec((None, N_DR, 128), lambda h: (h, 0, 0))],
        out_specs=pl.BlockSpec((None, NT, KB, QB), lambda h: (h, 0, 0, 0)),
    )(rpb_rev_pad)


def _block_geom(b):
    qs = b * QB
    ks = min(max(2 * b - 4, 0), ROWS - KR) * GW
    t = b if b < 2 else (b - (NQB - NT) if b > NQB - 3 else 2)
    return qs, ks, t


def _tt(a, b):
    return lax.dot_general(a, b, (((1,), (1,)), ((), ())), preferred_element_type=f32)


def _tn(a, b):
    return lax.dot_general(a, b, (((0,), (0,)), ((), ())), preferred_element_type=f32)


def _softmax_t(s_lat, s_ctx):
    m = jnp.maximum(jnp.max(s_lat, axis=0, keepdims=True), jnp.max(s_ctx, axis=0, keepdims=True))
    e_lat = jnp.exp(s_lat - m)
    e_ctx = jnp.exp(s_ctx - m)
    inv = 1.0 / (jnp.sum(e_lat, axis=0, keepdims=True) + jnp.sum(e_ctx, axis=0, keepdims=True))
    return e_lat * inv, e_ctx * inv


def _staged(n_blocks, stages):
    held = [dict() for _ in stages]
    for step in range(n_blocks + len(stages) - 1):
        for s, fn in enumerate(stages):
            b = step - s
            if 0 <= b < n_blocks:
                held[s][b] = fn(b) if s == 0 else fn(b, held[s - 1].pop(b))


def _attn_fwd(qr, qp, kr, vh, kc, vc, btt):
    def body(qr_ref, qp_ref, kr_ref, v_ref, kc_ref, vc_ref, bt_ref, o_ref):
        kcv, vcv = kc_ref[...], vc_ref[...]

        def scores(b):
            qs, ks, t = _block_geom(b)
            return (_tt(kr_ref[ks:ks + KB, :], qr_ref[qs:qs + QB, :]) + bt_ref[t], _tt(kcv, qp_ref[qs:qs + QB, :]))

        def probs(b, sc):
            p_lat, p_ctx = _softmax_t(*sc)
            return p_lat.astype(bf16), p_ctx.astype(bf16)

        def values(b, p):
            qs, ks, _ = _block_geom(b)
            o_ref[qs:qs + QB, :] = _tn(p[0], v_ref[ks:ks + KB, :]) + _tn(p[1], vcv)

        _staged(NQB, (scores, probs, values))

    sq = pl.BlockSpec((None, S, DH), lambda h: (h, 0, 0))
    sc = pl.BlockSpec((None, L, DH), lambda h: (h, 0, 0))
    return _hbm_call(
        body, name="attn_fwd", out_shape=SDS((H, S, DH), f32), grid=(H,),
        in_specs=[sq, sq, sq, sq, sc, sc, pl.BlockSpec((None, NT, KB, QB), lambda h: (h, 0, 0, 0))],
        out_specs=sq, compiler_params=_cp(48),
    )(qr, qp, kr, vh, kc, vc, btt)


def _attn_gate(o, p):
    tm = 256

    def body(o_ref, za_ref, a_ref):
        sz = _silu(za_ref[...])
        for hh in range(H):
            sl = slice(hh * DH, (hh + 1) * DH)
            a_ref[:, sl] = (o_ref[hh] * sz[:, sl]).astype(bf16)

    return _hbm_call(
        body, name="attn_gate", out_shape=SDS((S, DA), bf16), grid=(S // tm,),
        in_specs=[pl.BlockSpec((H, tm, DH), lambda i: (0, i, 0)), pl.BlockSpec((tm, DA), lambda i: (i, 3))],
        out_specs=pl.BlockSpec((tm, DA), lambda i: (i, 0)),
    )(o, p)


def _shift_rows(v, down):
    n = v.shape[0]
    row = lax.broadcasted_iota(i32, v.shape, 0)
    if down:
        return jnp.where(row == 0, 0.0, pltpu.roll(v, 1, 0))
    return jnp.where(row == n - 1, 0.0, pltpu.roll(v, n - 1, 0))


def _conv_specs():
    col = lambda off: pl.BlockSpec((S, 128), lambda i, off=off: (0, off + i))
    return [col(16), col(20), col(24), col(28), pl.BlockSpec((3, 128), lambda i: (0, i)),
            pl.BlockSpec((1, 128), lambda i: (0, i))]


def _conv_fwd(p, conv_w, conv_b):
    def body(u_ref, bg_ref, cg_ref, zc_ref, w_ref, b_ref, o_ref):
        cu = cg_ref[...] * u_ref[...]
        cv = b_ref[...] + _shift_rows(cu, True) * w_ref[0:1, :]
        cv = cv + cu * w_ref[1:2, :]
        cv = cv + _shift_rows(cu, False) * w_ref[2:3, :]
        o_ref[...] = ((bg_ref[...] * cv) * _silu(zc_ref[...])).astype(bf16)

    return _hbm_call(
        body, name="conv_fwd", out_shape=SDS((S, DC), bf16), grid=(DC // 128,),
        in_specs=_conv_specs(), out_specs=pl.BlockSpec((S, 128), lambda i: (0, i)), compiler_params=_cp(40),
    )(p, p, p, p, conv_w, conv_b)


def _out_proj_loss(attn_g, conv_g, w_out, xx, tgt, mrow, b_ada):
    tm = 256

    def body(a_ref, c_ref, w_ref, x_ref, t_ref, m_ref, b_ref, dy_ref, dmix_ref, gwo_ref, dgate_ref, loss_ref):
        k = pl.program_id(0)

        @pl.when(k == 0)
        def _():
            gwo_ref[...] = jnp.zeros_like(gwo_ref)
            dgate_ref[...] = jnp.zeros_like(dgate_ref)
            loss_ref[0, 0] = 0.0

        gate = m_ref[:, 2 * D:3 * D] + b_ref[:, 2 * D:3 * D]
        av, cv = a_ref[...], c_ref[...]
        mo = jnp.dot(av, w_ref[0:DA, :], preferred_element_type=f32)
        mo = mo + jnp.dot(cv, w_ref[DA:DA + DC, :], preferred_element_type=f32)
        y = x_ref[...] + gate * mo
        diff = y - t_ref[...]
        loss_ref[0, 0] += jnp.sum(diff * diff)
        dy = diff * (1.0 / D)
        dy_ref[...] = dy
        dgate_ref[...] += jnp.sum(dy * mo, axis=0, keepdims=True)
        dmo = (dy * gate).astype(bf16)
        dmix_ref[...] = _tt(dmo, w_ref[...])
        gwo_ref[0:DA, :] += _tn(av, dmo)
        gwo_ref[DA:DA + DC, :] += _tn(cv, dmo)

    row = lambda i: (i, 0)
    fixed = lambda i: (0, 0)
    return _hbm_call(
        body, name="out_proj_loss",
        out_shape=(SDS((S, D), f32), SDS((S, D), f32), SDS((D, D), f32), SDS((1, D), f32), SDS((1, 1), f32)),
        grid=(S // tm,),
        in_specs=[pl.BlockSpec((tm, DA), row), pl.BlockSpec((tm, DC), row), pl.BlockSpec((D, D), fixed),
                  pl.BlockSpec((tm, D), row), pl.BlockSpec((tm, D), row), pl.BlockSpec((1, 3 * D), fixed),
                  pl.BlockSpec((1, 3 * D), fixed)],
        out_specs=(pl.BlockSpec((tm, D), row), pl.BlockSpec((tm, D), row), pl.BlockSpec((D, D), fixed),
                   pl.BlockSpec((1, D), fixed), SMEM_SPEC),
        compiler_params=_cp(48, dimension_semantics=("arbitrary",)),
    )(attn_g, conv_g, w_out, xx, tgt, mrow, b_ada)


def _conv_bwd(dmix, p, conv_w, conv_b):
    def body(d_ref, u_ref, bg_ref, cg_ref, zc_ref, w_ref, b_ref, du_ref, dbg_ref, dcg_ref, dzc_ref, gw_ref, gb_ref):
        dconv = d_ref[...]
        u, bg, cg, zc = u_ref[...], bg_ref[...], cg_ref[...], zc_ref[...]
        w0, w1, w2 = w_ref[0:1, :], w_ref[1:2, :], w_ref[2:3, :]
        cu = cg * u
        cu_m, cu_p = _shift_rows(cu, True), _shift_rows(cu, False)
        cv = b_ref[...] + cu_m * w0
        cv = cv + cu * w1
        cv = cv + cu_p * w2
        sz = _silu(zc)
        dbg_ref[...] = ((dconv * sz) * cv).astype(bf16)
        dzc_ref[...] = ((dconv * (bg * cv)) * _dsilu(zc)).astype(bf16)
        dcv = (dconv * sz) * bg
        gb_ref[...] = jnp.sum(dcv, axis=0, keepdims=True)
        gw_ref[0:1, :] = jnp.sum(dcv * cu_m, axis=0, keepdims=True)
        gw_ref[1:2, :] = jnp.sum(dcv * cu, axis=0, keepdims=True)
        gw_ref[2:3, :] = jnp.sum(dcv * cu_p, axis=0, keepdims=True)
        gw_ref[3:8, :] = jnp.zeros((5, 128), f32)
        dcu = _shift_rows(dcv, False) * w0 + dcv * w1 + _shift_rows(dcv, True) * w2
        dcg_ref[...] = (dcu * u).astype(bf16)
        du_ref[...] = (dcu * cg).astype(bf16)

    piece = SDS((S, DC), bf16)
    ospec = pl.BlockSpec((S, 128), lambda i: (0, i))
    return _hbm_call(
        body, name="conv_bwd", out_shape=(piece, piece, piece, piece, SDS((8, DC), f32), SDS((1, DC), f32)),
        grid=(DC // 128,),
        in_specs=[pl.BlockSpec((S, 128), lambda i: (0, 4 + i))] + _conv_specs(),
        out_specs=(ospec, ospec, ospec, ospec, pl.BlockSpec((8, 128), lambda i: (0, i)),
                   pl.BlockSpec((1, 128), lambda i: (0, i))),
        compiler_params=_cp(48),
    )(dmix, p, p, p, p, conv_w, conv_b)


def _gate_bwd(dmix, p, o):
    tm = 256

    def body(d_ref, za_ref, o_ref, do_ref, dza_ref):
        za = za_ref[...]
        dattn = d_ref[...]
        a = dattn * _silu(za)
        bb = dattn * _dsilu(za)
        for hh in range(H):
            sl = slice(hh * DH, (hh + 1) * DH)
            do_ref[hh] = a[:, sl].astype(bf16)
            dza_ref[:, sl] = (bb[:, sl] * o_ref[hh]).astype(bf16)

    hspec = pl.BlockSpec((H, tm, DH), lambda i: (0, i, 0))
    return _hbm_call(
        body, name="gate_bwd", out_shape=(SDS((H, S, DH), bf16), SDS((S, DA), bf16)), grid=(S // tm,),
        in_specs=[pl.BlockSpec((tm, DA), lambda i: (i, 0)), pl.BlockSpec((tm, DA), lambda i: (i, 3)), hspec],
        out_specs=(hspec, pl.BlockSpec((tm, DA), lambda i: (i, 0))),
    )(dmix, p, o)


def _attn_bwd(qr, qp, kr, vh, kc, vc, btt, do):
    def body(qr_ref, qp_ref, kr_ref, v_ref, kc_ref, vc_ref, bt_ref, do_ref,
             dqr_ref, dqp_ref, dkr_ref, dv_ref, dkc_ref, dvc_ref, dbt_ref):
        kcv, vcv = kc_ref[...], vc_ref[...]
        dkr_ref[...] = jnp.zeros_like(dkr_ref)
        dv_ref[...] = jnp.zeros_like(dv_ref)
        dbt_ref[...] = jnp.zeros_like(dbt_ref)
        ctx_acc = {}

        def products(b):
            qs, ks, t = _block_geom(b)
            dob = do_ref[qs:qs + QB, :]
            s_lat = _tt(kr_ref[ks:ks + KB, :], qr_ref[qs:qs + QB, :]) + bt_ref[t]
            s_ctx = _tt(kcv, qp_ref[qs:qs + QB, :])
            return s_lat, s_ctx, _tt(v_ref[ks:ks + KB, :], dob), _tt(vcv, dob)

        def score_grads(b, x):
            s_lat, s_ctx, dp_lat, dp_ctx = x
            p_lat, p_ctx = _softmax_t(s_lat, s_ctx)
            delta = jnp.sum(p_lat * dp_lat, axis=0, keepdims=True) + jnp.sum(p_ctx * dp_ctx, axis=0, keepdims=True)
            ds_lat = p_lat * (dp_lat - delta)
            ds_ctx = p_ctx * (dp_ctx - delta)
            return ds_lat, ds_lat.astype(bf16), ds_ctx.astype(bf16), p_lat.astype(bf16), p_ctx.astype(bf16)

        def operand_grads(b, y):
            qs, ks, t = _block_geom(b)
            ds_lat, dsb_lat, dsb_ctx, pb_lat, pb_ctx = y
            qrb, qpb, dob = qr_ref[qs:qs + QB, :], qp_ref[qs:qs + QB, :], do_ref[qs:qs + QB, :]
            dbt_ref[t] += ds_lat
            dqr_ref[qs:qs + QB, :] = _tn(dsb_lat, kr_ref[ks:ks + KB, :])
            dqp_ref[qs:qs + QB, :] = _tn(dsb_ctx, kcv)
            dkr_ref[ks:ks + KB, :] += jnp.dot(dsb_lat, qrb, preferred_element_type=f32)
            dv_ref[ks:ks + KB, :] += jnp.dot(pb_lat, dob, preferred_element_type=f32)
            dkc = jnp.dot(dsb_ctx, qpb, preferred_element_type=f32)
            dvc = jnp.dot(pb_ctx, dob, preferred_element_type=f32)
            ctx_acc["k"] = dkc if b == 0 else ctx_acc["k"] + dkc
            ctx_acc["v"] = dvc if b == 0 else ctx_acc["v"] + dvc

        _staged(NQB, (products, score_grads, operand_grads))
        dkc_ref[...] = ctx_acc["k"]
        dvc_ref[...] = ctx_acc["v"]

    sq = pl.BlockSpec((None, S, DH), lambda h: (h, 0, 0))
    sc = pl.BlockSpec((None, L, DH), lambda h: (h, 0, 0))
    sb = pl.BlockSpec((None, NT, KB, QB), lambda h: (h, 0, 0, 0))
    big, ctxs = SDS((H, S, DH), f32), SDS((H, L, DH), f32)
    return _hbm_call(
        body, name="attn_bwd", out_shape=(big, big, big, big, ctxs, ctxs, SDS((H, NT, KB, QB), f32)), grid=(H,),
        in_specs=[sq, sq, sq, sq, sc, sc, sb, sq], out_specs=(sq, sq, sq, sq, sc, sc, sb), compiler_params=_cp(56),
    )(qr, qp, kr, vh, kc, vc, btt, do)


def _bias_bwd(dbtt):
    pieces = _tile_pieces()

    def body(d_ref, o_ref, scr):
        scr[...] = jnp.zeros_like(scr)
        acc = [None] * N_DR
        for t in range(NT):
            for j in range(2):
                for u in range(KR):
                    dr = pieces[t][j][u]
                    if dr is None:
                        continue
                    piece = d_ref[t, u * GW:(u + 1) * GW, j * GW:(j + 1) * GW]
                    acc[dr] = piece if acc[dr] is None else acc[dr] + piece
        for dr in range(N_DR):
            scr[dr * GW:(dr + 1) * GW, 0:GW] = acc[dr]
        xs = pltpu.roll(scr[...], WIN_W - 1, 1)
        row = lax.broadcasted_iota(i32, xs.shape, 0)
        for b in range(6):
            xs = jnp.where(((row >> b) & 1) == 1, pltpu.roll(xs, 128 - (1 << b), 1), xs)
        o_ref[...] = jnp.sum(xs.reshape(N_DR, GW, 128), axis=1)

    return _hbm_call(
        body, name="bias_bwd", out_shape=SDS((H, N_DR, 128), f32), grid=(H,),
        in_specs=[pl.BlockSpec((None, NT, KB, QB), lambda h: (h, 0, 0, 0))],
        out_specs=pl.BlockSpec((None, N_DR, 128), lambda h: (h, 0, 0)),
        scratch_shapes=[pltpu.VMEM((N_DR * GW, 128), f32)],
    )(dbtt)


def _merge_heads(ref):
    return jnp.concatenate([ref[hh] for hh in range(H)], axis=1)


def _head_norm_bwd(xraw, gain, dy, ones_bd):
    r = lax.rsqrt(_head_sum(xraw * xraw, ones_bd) * (1.0 / DH) + RMS_EPS)
    xh = xraw * r
    gdy = dy * gain
    dx = r * (gdy - xh * (_head_sum(xh * gdy, ones_bd) * (1.0 / DH)))
    return dx, jnp.sum(dy * xh, axis=0, keepdims=True)


def _qk_bwd(dqr, dqp, dkr, dvh, p, gq, gk, rope):
    tm = 256

    def body(dqr_ref, dqp_ref, dkr_ref, dv_ref, qk_ref, gq_ref, gk_ref, cc_ref, cr_ref, sc_ref, sr_ref,
             dq_ref, dk_ref, dvo_ref, ggq_ref, ggk_ref):
        @pl.when(pl.program_id(0) == 0)
        def _():
            ggq_ref[...] = jnp.zeros_like(ggq_ref)
            ggk_ref[...] = jnp.zeros_like(ggk_ref)

        ones_bd = _head_ones()
        cs, sn = _rope_block(cc_ref, cr_ref, tm), _rope_block(sc_ref, sr_ref, tm)
        a = _merge_heads(dqr_ref)
        dyq = ((a * cs - _swap16(a) * sn) + _merge_heads(dqp_ref)) * QK_SCALE
        bk = _merge_heads(dkr_ref)
        dyk = bk * cs - _swap16(bk) * sn
        dq, gq_part = _head_norm_bwd(qk_ref[:, 0:DA], gq_ref[...], dyq, ones_bd)
        dk, gk_part = _head_norm_bwd(qk_ref[:, DA:2 * DA], gk_ref[...], dyk, ones_bd)
        dq_ref[...] = dq.astype(bf16)
        dk_ref[...] = dk.astype(bf16)
        dvo_ref[...] = _merge_heads(dv_ref).astype(bf16)
        ggq_ref[...] += gq_part
        ggk_ref[...] += gk_part

    hspec = pl.BlockSpec((H, tm, DH), lambda i: (0, i, 0))
    row = pl.BlockSpec((tm, DA), lambda i: (i, 0))
    fixed = pl.BlockSpec((1, DA), lambda i: (0, 0))
    piece = SDS((S, DA), bf16)
    return _hbm_call(
        body, name="qk_bwd", out_shape=(piece, piece, piece, SDS((1, DA), f32), SDS((1, DA), f32)), grid=(S // tm,),
        in_specs=[hspec, hspec, hspec, hspec, pl.BlockSpec((tm, 2 * DA), lambda i: (i, 0)), fixed, fixed]
        + _rope_specs(tm),
        out_specs=(row, row, row, fixed, fixed), compiler_params=_cp(40, dimension_semantics=("arbitrary",)),
    )(dqr, dqp, dkr, dvh, p, gq, gk, *rope)


def _ctx_bwd(dkc, dvc, pc, gk):
    def body(dkc_ref, dvc_ref, p_ref, gk_ref, dk_ref, dv_ref, ggk_ref):
        ones_bd = _head_ones()
        dk, gk_part = _head_norm_bwd(p_ref[:, 0:DA], gk_ref[...], _merge_heads(dkc_ref), ones_bd)
        dk_ref[...] = dk.astype(bf16)
        dv_ref[...] = _merge_heads(dvc_ref).astype(bf16)
        ggk_ref[...] = gk_part

    piece = SDS((L, DA), bf16)
    return _hbm_call(
        body, name="ctx_bwd", out_shape=(piece, piece, SDS((1, DA), f32)), in_specs=[VMEM_SPEC] * 4,
        out_specs=(VMEM_SPEC,) * 3,
    )(dkc, dvc, pc, gk)


def _grad_w_in_block(h, pa, pb, name, hc=None, pc=None, ctx_cols=None):
    tm = 512

    def body(*refs):
        if hc is None:
            h_ref, a_ref, b_ref, g_ref = refs
        else:
            h_ref, a_ref, b_ref, hc_ref, pc_ref, g_ref = refs

        @pl.when(pl.program_id(0) == 0)
        def _():
            g_ref[...] = jnp.zeros_like(g_ref)
            if hc is not None:
                g_ref[:, ctx_cols[0]:ctx_cols[1]] = _tn(hc_ref[...], pc_ref[...])

        hv = h_ref[...]
        g_ref[:, 0:DA] += _tn(hv, a_ref[...])
        g_ref[:, DA:2 * DA] += _tn(hv, b_ref[...])

    row = lambda i: (i, 0)
    fixed = lambda i: (0, 0)
    in_specs = [pl.BlockSpec((tm, D), row), pl.BlockSpec((tm, DA), row), pl.BlockSpec((tm, DA), row)]
    args = [h, pa, pb]
    if hc is not None:
        in_specs += [pl.BlockSpec((L, D), fixed), pl.BlockSpec((L, DA), fixed)]
        args += [hc, pc]
    return _hbm_call(
        body, name=name, out_shape=SDS((D, D), f32), grid=(S // tm,), in_specs=in_specs,
        out_specs=pl.BlockSpec((D, D), fixed), compiler_params=_cp(40, dimension_semantics=("arbitrary",)),
    )(*args)


def _norm_mod_bwd(x, dh, g, scale):
    r = lax.rsqrt(jnp.mean(x * x, axis=-1, keepdims=True) + RMS_EPS)
    xh = x * r
    y = xh * g
    dshift = jnp.sum(dh, axis=0, keepdims=True)
    dscale = jnp.sum(dh * y, axis=0, keepdims=True)
    dyn = dh * (1.0 + scale)
    dg = jnp.sum(dyn * xh, axis=0, keepdims=True)
    gdy = dyn * g
    dx = r * (gdy - xh * jnp.mean(xh * gdy, axis=-1, keepdims=True))
    return dx, dshift, dscale, dg


def _dh_grad_x(pieces, w4, xx, dy, norm_g, mrow, b_ada):
    tm = 256

    def body(*refs):
        p_refs = refs[:8]
        w_ref, x_ref, dy_ref, g_ref, m_ref, b_ref, gx_ref, dsh_ref, dsc_ref, dg_ref = refs[8:]

        @pl.when(pl.program_id(0) == 0)
        def _():
            dsh_ref[...] = jnp.zeros_like(dsh_ref)
            dsc_ref[...] = jnp.zeros_like(dsc_ref)
            dg_ref[...] = jnp.zeros_like(dg_ref)

        dh = None
        for j in range(4):
            for half in range(2):
                term = _tt(p_refs[2 * j + half][...], w_ref[j, :, half * DA:(half + 1) * DA])
                dh = term if dh is None else dh + term
        scale = m_ref[:, D:2 * D] + b_ref[:, D:2 * D]
        dx, dshift, dscale, dg = _norm_mod_bwd(x_ref[...], dh, g_ref[...], scale)
        gx_ref[...] = dy_ref[...] + dx
        dsh_ref[...] += dshift
        dsc_ref[...] += dscale
        dg_ref[...] += dg

    row = lambda i: (i, 0)
    fixed = lambda i: (0, 0)
    vec = SDS((1, D), f32)
    return _hbm_call(
        body, name="dh_grad_x", out_shape=(SDS((S, D), f32), vec, vec, vec), grid=(S // tm,),
        in_specs=[pl.BlockSpec((tm, DA), row)] * 8 + [pl.BlockSpec((4, D, D), lambda i: (0, 0, 0)),
                                                      pl.BlockSpec((tm, D), row), pl.BlockSpec((tm, D), row),
                                                      pl.BlockSpec((1, D), fixed), pl.BlockSpec((1, 3 * D), fixed),
                                                      pl.BlockSpec((1, 3 * D), fixed)],
        out_specs=(pl.BlockSpec((tm, D), row), pl.BlockSpec((1, D), fixed), pl.BlockSpec((1, D), fixed),
                   pl.BlockSpec((1, D), fixed)),
        compiler_params=_cp(56, dimension_semantics=("arbitrary",)),
    )(*pieces, w4, xx, dy, norm_g, mrow, b_ada)


def _dhc_sums(dkc_raw, dvc_m, w4, ctx2, norm_g, mrow_c, b_ada):
    def body(dk_ref, dv_ref, w0_ref, w1_ref, x_ref, g_ref, m_ref, b_ref, dsh_ref, dsc_ref, dg_ref):
        dh = _tt(dk_ref[...], w0_ref[:, DA:2 * DA]) + _tt(dv_ref[...], w1_ref[:, 0:DA])
        scale = m_ref[:, D:2 * D] + b_ref[:, D:2 * D]
        _, dshift, dscale, dg = _norm_mod_bwd(x_ref[...], dh, g_ref[...], scale)
        dsh_ref[...] = dshift
        dsc_ref[...] = dscale
        dg_ref[...] = dg

    fixed = lambda i: (0, 0)
    vec = SDS((1, D), f32)
    vspec = pl.BlockSpec((1, D), fixed)
    return _hbm_call(
        body, name="dhc_sums", out_shape=(vec, vec, vec), grid=(1,),
        in_specs=[pl.BlockSpec((L, DA), fixed), pl.BlockSpec((L, DA), fixed),
                  pl.BlockSpec((None, D, D), lambda i: (0, 0, 0)), pl.BlockSpec((None, D, D), lambda i: (1, 0, 0)),
                  pl.BlockSpec((L, D), fixed), vspec, pl.BlockSpec((1, 3 * D), fixed), pl.BlockSpec((1, 3 * D), fixed)],
        out_specs=(vspec, vspec, vspec), compiler_params=_cp(32),
    )(dkc_raw, dvc_m, w4, w4, ctx2, norm_g, mrow_c, b_ada)


def _rope_tables():
    nf = DH // 4
    inv = ROPE_THETA ** (-jnp.arange(nf, dtype=f32) / nf)
    ang_c = jnp.arange(GW, dtype=i32).astype(f32)[:, None] * inv
    ang_r = jnp.arange(ROWS, dtype=i32).astype(f32)[:, None] * inv
    zc, zr = jnp.zeros((GW, 2 * nf), f32), jnp.zeros((ROWS, 2 * nf), f32)
    ct_cos = jnp.tile(jnp.concatenate([zc, jnp.cos(ang_c), jnp.cos(ang_c)], axis=1), (1, H))
    ct_sin = jnp.tile(jnp.concatenate([zc, -jnp.sin(ang_c), jnp.sin(ang_c)], axis=1), (1, H))
    rt_cos = jnp.tile(jnp.concatenate([jnp.cos(ang_r), jnp.cos(ang_r), zr], axis=1), (1, H))
    rt_sin = jnp.tile(jnp.concatenate([-jnp.sin(ang_r), jnp.sin(ang_r), zr], axis=1), (1, H))
    rep8 = lambda t: jnp.broadcast_to(t[:, None, :], (ROWS, 8, DA)).reshape(ROWS * 8, DA)
    return ct_cos, rep8(rt_cos), ct_sin, rep8(rt_sin)


def _local_step(xx, ctx2, tgt, mrow, mrow_c, b_ada, norm_g, weights, q_norm_g, k_norm_g, rpb2, conv_w_full, conv_b,
                mid=None):
    gq = jnp.tile(q_norm_g, (1, H))
    gk = jnp.tile(k_norm_g, (1, H))
    rope = _rope_tables()
    rpb_pad = jnp.pad(rpb2[:, :, ::-1], ((0, 0), (0, 0), (0, 128 - N_DC)))

    h = _prenorm(xx, norm_g, mrow, b_ada, 256, "prenorm_x")
    hc = _prenorm(ctx2, norm_g, mrow_c, b_ada, L, "prenorm_ctx")
    btb = _bias_prep(rpb_pad)
    w4, w_out_full = weights(btb)
    p = _in_proj(h, w4)
    pc = _ctx_proj(hc, w4)
    qr, qp, kr, vh = _qk_prep(p, gq, gk, rope)
    kc, vc = _ctx_prep(pc, gk)
    o = _attn_fwd(qr, qp, kr, vh, kc, vc, btb)
    attn_g = _attn_gate(o, p)
    conv_g = _conv_fwd(p, conv_w_full, conv_b)
    dy, dmix, g_w_out, dgate, loss_sum = _out_proj_loss(attn_g, conv_g, w_out_full, xx, tgt, mrow, b_ada)

    du, dbg, dcg, dzc, g_conv_w, g_conv_b = _conv_bwd(dmix, p, conv_w_full, conv_b)
    do, dza = _gate_bwd(dmix, p, o)
    dqr, dqp, dkr, dvh, dkc, dvc, dbtb = _attn_bwd(qr, qp, kr, vh, kc, vc, btb, do)
    g_rpb = _bias_bwd(dbtb)
    dq, dk, dv, g_gq, g_gk = _qk_bwd(dqr, dqp, dkr, dvh, p, gq, gk, rope)
    dkc_raw, dvc_m, g_gk_c = _ctx_bwd(dkc, dvc, pc, gk)
    g_w_in = [
        _grad_w_in_block(h, dq, dk, "grad_w_in_0", hc, dkc_raw, (DA, 2 * DA)),
        _grad_w_in_block(h, dv, dza, "grad_w_in_1", hc, dvc_m, (0, DA)),
        _grad_w_in_block(h, du, dbg, "grad_w_in_2"),
        _grad_w_in_block(h, dcg, dzc, "grad_w_in_3"),
    ]
    b_ada_late = b_ada if mid is None else mid(g_w_in, g_w_out, b_ada)
    grad_x, dshift, dscale, dng = _dh_grad_x([dq, dk, dv, dza, du, dbg, dcg, dzc], w4, xx, dy, norm_g, mrow,
                                             b_ada_late)
    dshift_c, dscale_c, dng_c = _dhc_sums(dkc_raw, dvc_m, w4, ctx2, norm_g, mrow_c, b_ada_late)
    return dict(loss_sum=loss_sum, grad_x=grad_x, g_w_in=g_w_in, g_w_out=g_w_out, g_conv_w=g_conv_w,
                g_conv_b=g_conv_b, g_rpb=g_rpb, g_gq=g_gq, g_gk=g_gk, g_gk_c=g_gk_c, dshift=dshift, dscale=dscale,
                dgate=dgate, dng=dng, dshift_c=dshift_c, dscale_c=dscale_c, dng_c=dng_c)


def _pair_sum_w_in(gs, rs, cvec):
    tr = 128

    def body(c_ref, g0, g1, g2, g3, r0, r1, r2, r3, t32_ref, tb_ref):
        for q, (g_ref, r_ref) in enumerate(((g0, r0), (g1, r1), (g2, r2), (g3, r3))):
            t = g_ref[...] + r_ref[...]
            t32_ref[q] = t
            tb_ref[q] = t.astype(bf16)

    half = D // 2
    gs_spec = pl.BlockSpec((tr, D), lambda i, c: (c[0] * (half // tr) + i, 0))
    rs_spec = pl.BlockSpec((None, tr, D), lambda i, c: (0, i, 0))
    ospec = pl.BlockSpec((4, tr, D), lambda i, c: (0, i, 0))
    grid_spec = pltpu.PrefetchScalarGridSpec(num_scalar_prefetch=1, grid=(half // tr,), in_specs=[gs_spec] * 4 + [rs_spec] * 4,
                                             out_specs=(ospec, ospec))
    return _hbm_call(body, name="pair_sum_w_in", out_shape=(SDS((4, half, D), f32), SDS((4, half, D), bf16)),
                          grid_spec=grid_spec, compiler_params=_cp(40))(cvec, *gs, *rs)


def _pair_sum_w_out(g, r, cvec):
    hr = D // 8

    def body(c_ref, g0, g1, g2, g3, r_ref, t32_ref, tb_ref):
        for q, g_ref in enumerate((g0, g1, g2, g3)):
            t = g_ref[...] + r_ref[q]
            t32_ref[q] = t
            tb_ref[q] = t.astype(bf16)

    gspecs = [pl.BlockSpec((hr, D), lambda i, c, q=q: (2 * q + c[0], 0)) for q in range(4)]
    full = pl.BlockSpec((4, hr, D), lambda i, c: (0, 0, 0))
    grid_spec = pltpu.PrefetchScalarGridSpec(num_scalar_prefetch=1, grid=(1,), in_specs=gspecs + [full],
                                             out_specs=(full, full))
    return _hbm_call(body, name="pair_sum_w_out", out_shape=(SDS((4, hr, D), f32), SDS((4, hr, D), bf16)),
                          grid_spec=grid_spec)(cvec, g, g, g, g, r)


def _chip_sum(t32, r2, jvec, name):
    rows = t32.shape[1]
    tr = min(rows, 128)

    def body(j_ref, t_ref, r_ref, u_ref):
        u_ref[...] = ((t_ref[...] + r_ref[0].astype(f32)) + r_ref[1].astype(f32)) + r_ref[2].astype(f32)

    grid_spec = pltpu.PrefetchScalarGridSpec(
        num_scalar_prefetch=1, grid=(rows // tr,),
        in_specs=[pl.BlockSpec((None, tr, D), lambda i, j: (j[0], i, 0)), pl.BlockSpec((3, tr, D), lambda i, j: (0, i, 0))],
        out_specs=pl.BlockSpec((tr, D), lambda i, j: (i, 0)))
    return _hbm_call(body, name=name, out_shape=SDS((rows, D), f32), grid_spec=grid_spec)(jvec, t32, r2)


_PK = {}
_off = 0
for _name, _rows in (("dm", 24), ("dmc", 24), ("dng", 8), ("dng_c", 8), ("gq", 8), ("gk", 8), ("gk_c", 8),
                     ("rpb", H * N_DR), ("conv_b", 8), ("conv_w", 16), ("loss", 8)):
    _PK[_name] = (_off, _off + _rows)
    _off += _rows
PK_ROWS = _off
RS_B_ADA, RS_NORM_G, RS_GQ, RS_GK, RS_RPB, RS_CONV_B, RS_CONV_W, RS_DMC, RS_LOSS, RS_ROWS = (
    0, 24, 32, 40, 48, 168, 176, 192, 216, 224)


def _small_reduce(gathered):
    def body(g_ref, o_ref):
        tot = g_ref[0]
        for b in range(1, 8):
            tot = tot + g_ref[b]

        def rows(name):
            a, z = _PK[name]
            return tot[a:z]

        o_ref[RS_B_ADA:RS_B_ADA + 24] = rows("dm") + rows("dmc")
        o_ref[RS_NORM_G:RS_NORM_G + 8] = rows("dng") + rows("dng_c")
        gq = jnp.broadcast_to(jnp.sum(rows("gq"), axis=0, keepdims=True), (8, 128))
        gk = jnp.broadcast_to(jnp.sum(rows("gk") + rows("gk_c"), axis=0, keepdims=True), (8, 128))
        o_ref[RS_GQ:RS_GQ + 8] = gq + pltpu.roll(gq, DH, 1)
        o_ref[RS_GK:RS_GK + 8] = gk + pltpu.roll(gk, DH, 1)
        o_ref[RS_RPB:RS_RPB + H * N_DR] = rows("rpb")
        o_ref[RS_CONV_B:RS_CONV_B + 8] = rows("conv_b")
        o_ref[RS_CONV_W:RS_CONV_W + 16] = rows("conv_w")
        o_ref[RS_DMC:RS_DMC + 24] = rows("dmc")
        o_ref[RS_LOSS:RS_LOSS + 8] = rows("loss")

    return _hbm_call(body, name="small_reduce", out_shape=SDS((RS_ROWS, 128), f32), in_specs=[VMEM_SPEC],
                          out_specs=VMEM_SPEC)(gathered)


def _w_ada_grad(sc16, dm16s, w_ada_shard):
    def body(sc_ref, dm_ref, w_ref, g_ref, part_ref):
        dm = dm_ref[...]
        g_ref[...] = lax.dot_general(sc_ref[...], dm, (((0,), (0,)), ((), ())), precision=HIGHEST,
                                     preferred_element_type=f32)
        part_ref[...] = lax.dot_general(dm[8:16], w_ref[...], (((1,), (1,)), ((), ())), precision=HIGHEST,
                                        preferred_element_type=f32)

    ncol = w_ada_shard.shape[1]
    return _hbm_call(body, name="w_ada_grad", out_shape=(SDS((D, ncol), f32), SDS((8, D), f32)),
                          in_specs=[VMEM_SPEC] * 3, out_specs=(VMEM_SPEC, VMEM_SPEC), compiler_params=_cp(40),
                          )(sc16, dm16s, w_ada_shard)


def _c_ctx_grad(parts4, c_ctx_row):
    def body(p_ref, c_ref, o_ref):
        tot = ((p_ref[0] + p_ref[1]) + p_ref[2]) + p_ref[3]
        o_ref[...] = tot[0:1] * _dsilu(c_ref[...])

    return _hbm_call(body, name="c_ctx_grad", out_shape=SDS((1, D), f32), in_specs=[VMEM_SPEC, VMEM_SPEC],
                          out_specs=VMEM_SPEC)(parts4, c_ctx_row)


def _adamw(w, g, m, v, name):
    rows, cols = w.shape
    tr = 256 if rows % 256 == 0 else rows

    def body(w_ref, g_ref, m_ref, v_ref, d_ref, m2_ref, v2_ref):
        gv = g_ref[...]
        m2 = ADAM_B1 * m_ref[...] + (1.0 - ADAM_B1) * gv
        v2 = ADAM_B2 * v_ref[...] + (1.0 - ADAM_B2) * jnp.square(gv)
        m_hat = m2 / (1.0 - ADAM_B1 ** ADAM_STEP)
        v_hat = v2 / (1.0 - ADAM_B2 ** ADAM_STEP)
        d_ref[...] = -ADAM_LR * (m_hat / (jnp.sqrt(v_hat) + ADAM_EPS) + ADAM_WD * w_ref[...])
        m2_ref[...] = m2
        v2_ref[...] = v2

    spec = pl.BlockSpec((tr, cols), lambda i: (i, 0))
    shp = SDS((rows, cols), f32)
    return _hbm_call(body, name=name, out_shape=(shp, shp, shp), grid=(rows // tr,), in_specs=[spec] * 4,
                          out_specs=(spec, spec, spec))(w, g, m, v)


def _adamw_halves(w, g_mine, g_other, m, v, cvec, name):
    rows, cols = w.shape
    half = rows // 2
    tr = min(256, half)
    per_half = half // tr

    def body(c_ref, w_ref, ga_ref, gb_ref, m_ref, v_ref, g_ref, d_ref, m2_ref, v2_ref):
        in_my_half = (pl.program_id(0) // per_half) == c_ref[0]
        gv = jnp.where(in_my_half, ga_ref[...], gb_ref[...])
        g_ref[...] = gv
        m2 = ADAM_B1 * m_ref[...] + (1.0 - ADAM_B1) * gv
        v2 = ADAM_B2 * v_ref[...] + (1.0 - ADAM_B2) * jnp.square(gv)
        m_hat = m2 / (1.0 - ADAM_B1 ** ADAM_STEP)
        v_hat = v2 / (1.0 - ADAM_B2 ** ADAM_STEP)
        d_ref[...] = -ADAM_LR * (m_hat / (jnp.sqrt(v_hat) + ADAM_EPS) + ADAM_WD * w_ref[...])
        m2_ref[...] = m2
        v2_ref[...] = v2

    full = pl.BlockSpec((tr, cols), lambda i, c: (i, 0))
    part = pl.BlockSpec((tr, cols), lambda i, c: (i % per_half, 0))
    shp = SDS((rows, cols), f32)
    grid_spec = pltpu.PrefetchScalarGridSpec(num_scalar_prefetch=1, grid=(rows // tr,),
                                             in_specs=[full, part, part, full, full], out_specs=(full,) * 4)
    return _hbm_call(body, name=name, out_shape=(shp,) * 4, grid_spec=grid_spec)(cvec, w, g_mine, g_other, m, v)


def _rows128(a):
    return a.reshape(-1, 128)


def _pad_lanes(a):
    a2 = a.reshape(-1, a.shape[-1])
    return jnp.pad(a2, ((0, 0), (0, 128 - a2.shape[1])))


def kernel(x, c, ctx, c_ctx, w_ada, b_ada, norm_g, w_in, q_norm_g, k_norm_g, rpb, conv_w, conv_b, w_out, loss_target, m_c_ctx, m_w_ada, m_b_ada, m_norm_g, m_w_in, m_q_norm_g, m_k_norm_g, m_rpb, m_conv_w, m_conv_b, m_w_out, v_c_ctx, v_w_ada, v_b_ada, v_norm_g, v_w_in, v_q_norm_g, v_k_norm_g, v_rpb, v_conv_w, v_conv_b, v_w_out):
    xi, yi, ci = lax.axis_index("x"), lax.axis_index("y"), lax.axis_index("c")
    dev = 4 * xi + 2 * yi + ci
    chip = 2 * xi + yi
    cvec = jnp.reshape(ci, (1,)).astype(i32)
    jvec = jnp.reshape(chip, (1,)).astype(i32)
    w_ada_s = w_ada[0]
    ncol = w_ada_s.shape[1]

    c8 = _all_gather8(c.reshape(8, 128), "gather_c").reshape(8, D)
    cc = jnp.concatenate([c8, c_ctx.reshape(1, D), jnp.zeros((7, D), f32)], axis=0)
    m_shard, sc16 = _adaln_shard(cc, w_ada_s)

    conv_w_pad = jnp.pad(conv_w[0], ((0, 5), (0, 0)))
    m4, cw4 = _chip_gather([m_shard, conv_w_pad], "gather_mod")

    ssem, rsem, w4s, wo4s, token = _halves_start(
        [_cast_to_slot(w_in[0], jvec, "cast_w_in"), _cast_to_slot(w_out[0], jvec, "cast_w_out")], m4,
        "weights_ici_start")
    m_full = jnp.transpose(m4, (1, 0, 2)).reshape(16, 4 * ncol) + token[0:1, 0:1]
    mrow = lax.dynamic_slice(m_full, (dev, 0), (1, 3 * D))
    mrow_c = m_full[8:9]
    conv_w_full = jnp.transpose(cw4[:, 0:3, :], (1, 0, 2)).reshape(3, DC)

    def weights(after):
        w4w, wo4w = _halves_wait(ssem, rsem, [w4s, wo4s], after, "weights_ici_wait")
        w4f, wo4f = _halves_forward([w4w, wo4w], "weights_pair_forward")
        return w4f, wo4f.reshape(D, D)

    exchange = _exchange_copies([1, 1, 1, 1, 4])
    pending = {}

    def mid(g_w_in, g_w_out, b_ada_in):
        srcs = list(g_w_in) + [g_w_out]
        shapes = [SDS((1, D // 2, D), f32)] * 4 + [SDS((4, D // 8, D), f32)]
        out = _split_start(srcs, shapes, 8, exchange, "grad_pair_start")
        pending["ex"] = (out[0], out[1], list(out[2:7]), list(out[7:12]))
        return b_ada_in + out[12][0:1, 0:1]

    r = _local_step(x[0], ctx[0], loss_target[0], mrow, mrow_c, b_ada, norm_g, weights, q_norm_g, k_norm_g,
                    rpb[0], conv_w_full, conv_b, mid)
    dm = jnp.concatenate([r["dshift"], r["dscale"], r["dgate"]], axis=1)
    dmc = jnp.concatenate([r["dshift_c"], r["dscale_c"], jnp.zeros((1, D), f32)], axis=1)
    pack_parts = [_rows128(dm), _rows128(dmc), _rows128(r["dng"]), _rows128(r["dng_c"]), _rows128(r["g_gq"]),
                  _rows128(r["g_gk"]), _rows128(r["g_gk_c"]), r["g_rpb"].reshape(H * N_DR, 128),
                  _rows128(r["g_conv_b"]), _rows128(r["g_conv_w"][0:3]), jnp.pad(r["loss_sum"], ((0, 0), (0, 127)))]
    pack = jnp.concatenate([jnp.pad(p, ((0, -p.shape[0] % 8), (0, 0))) for p in pack_parts], axis=0)
    gathered = _all_gather8(pack, "gather_small")

    ex_ssem, ex_rsem, ex_srcs, ex_lands = pending["ex"]
    ex_g, ex = _split_wait(ex_ssem, ex_rsem, ex_srcs, ex_lands, gathered, exchange, "grad_pair_wait")
    t32, tb = _pair_sum_w_in(ex_g[0:4], ex[0:4], cvec)
    to32, tob = _pair_sum_w_out(ex_g[4], ex[4], cvec)
    sc_out = _split_start([tb, tob], [SDS((3, D // 2, D), bf16), SDS((3, D // 8, D), bf16)], 6, _scatter_copies,
                          "grad_chip_start")
    sc16 = sc16 + sc_out[6][0:1, 0:1]

    red = _small_reduce(gathered)
    g_b_ada = red[RS_B_ADA:RS_B_ADA + 24].reshape(1, 3 * D)
    g_norm_g = red[RS_NORM_G:RS_NORM_G + 8].reshape(1, D)
    g_q = red[RS_GQ:RS_GQ + 1, 0:DH]
    g_k = red[RS_GK:RS_GK + 1, 0:DH]
    g_rpb = jnp.pad(red[RS_RPB:RS_RPB + H * N_DR, N_DC - 1::-1], ((0, 0), (0, 128 - N_DC)))
    g_conv_b = red[RS_CONV_B:RS_CONV_B + 4].reshape(1, DC)
    g_conv_w_full = red[RS_CONV_W:RS_CONV_W + 12].reshape(3, DC)
    assert pack.shape[0] == PK_ROWS
    g_conv_w_s = lax.dynamic_slice(g_conv_w_full, (0, chip * 128), (3, 128))
    dmc_tot = red[RS_DMC:RS_DMC + 24].reshape(1, 3 * D)
    loss = red[RS_LOSS, 0] * (0.5 / D)

    a0, a1 = _PK["dm"]
    dm8 = gathered[:, a0:a1, :].reshape(8, 3 * D)
    dm16 = jnp.concatenate([dm8, dmc_tot, jnp.zeros((7, 3 * D), f32)], axis=0)
    dm16s = lax.dynamic_slice(dm16, (0, chip * ncol), (16, ncol))
    g_w_ada_s, cpart = _w_ada_grad(sc16, dm16s, w_ada_s)
    d_w_ada, nm_w_ada, nv_w_ada = _adamw(w_ada_s, g_w_ada_s, m_w_ada[0], v_w_ada[0], "adamw_w_ada")

    _, (r2, ro2) = _split_wait(sc_out[0], sc_out[1], [sc_out[2], sc_out[3]], [sc_out[4], sc_out[5]], nm_w_ada,
                               _scatter_copies, "grad_chip_wait")
    u_in = _chip_sum(t32, r2, jvec, "chip_sum_w_in")
    u_out = _chip_sum(to32, ro2, jvec, "chip_sum_w_out")

    (cparts4,) = _chip_gather([cpart], "gather_c_ctx_parts", after=u_out)
    g_c_ctx = _c_ctx_grad(cparts4, c_ctx.reshape(1, D))

    def small_pack(a_c_ctx, a_b_ada, a_norm_g, a_q, a_k, a_rpb, a_conv_w, a_conv_b):
        parts = [_rows128(a_c_ctx), _rows128(a_b_ada), _rows128(a_norm_g), _pad_lanes(a_q), _pad_lanes(a_k),
                 a_rpb, _rows128(a_conv_w), _rows128(a_conv_b)]
        return jnp.concatenate([jnp.pad(p, ((0, -p.shape[0] % 8), (0, 0))) for p in parts], axis=0)

    w_pk = small_pack(c_ctx, b_ada, norm_g, q_norm_g, k_norm_g, _pad_lanes(rpb[0]), conv_w[0], conv_b)
    g_pk = small_pack(g_c_ctx, g_b_ada, g_norm_g, g_q, g_k, g_rpb, g_conv_w_s, g_conv_b)
    m_pk = small_pack(m_c_ctx, m_b_ada, m_norm_g, m_q_norm_g, m_k_norm_g, _pad_lanes(m_rpb[0]), m_conv_w[0], m_conv_b)
    v_pk = small_pack(v_c_ctx, v_b_ada, v_norm_g, v_q_norm_g, v_k_norm_g, _pad_lanes(v_rpb[0]), v_conv_w[0], v_conv_b)
    d_pk, nm_pk, nv_pk = _adamw(w_pk, g_pk, m_pk, v_pk, "adamw_small")

    o_in, o_out = _sibling_send([u_in, u_out], "grad_pair_send")
    g_w_in_s, d_w_in, nm_w_in, nv_w_in = _adamw_halves(w_in[0], u_in, o_in, m_w_in[0], v_w_in[0], cvec, "adamw_w_in")
    g_w_out_s, d_w_out, nm_w_out, nv_w_out = _adamw_halves(w_out[0], u_out, o_out, m_w_out[0], v_w_out[0], cvec,
                                                           "adamw_w_out")

    def small_unpack(pk):
        o = 0
        out = []
        for rows, fn in ((8, lambda a: a.reshape(D)), (24, lambda a: a.reshape(1, 3 * D)), (8, lambda a: a.reshape(1, D)),
                         (1, lambda a: a[:, 0:DH]), (1, lambda a: a[:, 0:DH]),
                         (H * N_DR, lambda a: a[:, 0:N_DC].reshape(1, H, N_DR, N_DC)),
                         (3, lambda a: a.reshape(1, 3, 128)), (4, lambda a: a.reshape(1, DC))):
            out.append(fn(pk[o:o + rows]))
            o += rows + (-rows % 8)
        return out

    def ordered(small, big_w_ada, big_w_in, big_w_out):
        s_c_ctx, s_b_ada, s_norm_g, s_q, s_k, s_rpb, s_conv_w, s_conv_b = small
        return [s_c_ctx, big_w_ada[None], s_b_ada, s_norm_g, big_w_in[None], s_q, s_k, s_rpb, s_conv_w, s_conv_b,
                big_w_out[None]]

    grads = ordered(small_unpack(g_pk), g_w_ada_s, g_w_in_s, g_w_out_s)
    deltas = ordered(small_unpack(d_pk), d_w_ada, d_w_in, d_w_out)
    new_m = ordered(small_unpack(nm_pk), nm_w_ada, nm_w_in, nm_w_out)
    new_v = ordered(small_unpack(nv_pk), nv_w_ada, nv_w_in, nv_w_out)
    return (loss, r["grad_x"][None], *grads, *deltas, *new_m, *new_v)
```

```python
import functools

import jax
import jax.numpy as jnp
import numpy as np
from jax import lax
from jax.experimental import pallas as pl
from jax.experimental.pallas import tpu as pltpu

f32, bf16, i32 = jnp.float32, jnp.bfloat16, jnp.int32
MESH = pl.DeviceIdType.MESH
HIGHEST = lax.Precision.HIGHEST

D = 1024
S = 2048
L = 256
GW = 64
ROWS = S // GW
H = 8
DH = 64
DA = H * DH
DC = 512
WIN_H, WIN_W = 8, 16
N_DR, N_DC = 2 * WIN_H - 1, 2 * WIN_W - 1
RMS_EPS = 1e-6
ROPE_THETA = 10000.0
QK_SCALE = DH ** -0.5
NEG = -1e30

QB = 128
NQB = S // QB
KR = 9
KB = KR * GW
TILE_GEOM = ((0, 0), (2, 0), (4, 0), (28, 23), (30, 23))
NT = len(TILE_GEOM)

ADAM_LR, ADAM_B1, ADAM_B2, ADAM_EPS, ADAM_WD, ADAM_STEP = 0.001, 0.9, 0.999, 1e-08, 0.01, 10

VMEM_SPEC = pl.BlockSpec(memory_space=pltpu.VMEM)
ANY_SPEC = pl.BlockSpec(memory_space=pl.ANY)
SMEM_SPEC = pl.BlockSpec(memory_space=pltpu.SMEM)
SDS = jax.ShapeDtypeStruct


_pallas_call = pl.pallas_call


def _hbm_call(body, *, out_shape, in_specs=None, out_specs=None, grid_spec=None, **kw):
    n_pre = 0
    if grid_spec is not None:
        ispecs, ospecs, n_pre = grid_spec.in_specs, grid_spec.out_specs, grid_spec.num_scalar_prefetch
        kw["grid_spec"] = grid_spec
    else:
        ispecs, ospecs = in_specs, out_specs
        kw.update(in_specs=in_specs, out_specs=out_specs)

    def blocked(spec):
        return isinstance(spec, pl.BlockSpec) and spec.block_shape is not None

    single = not isinstance(out_shape, (tuple, list))
    shapes = [out_shape] if single else list(out_shape)
    ospec_list = list(ospecs) if isinstance(ospecs, (tuple, list)) else [ospecs]
    shapes = [pltpu.HBM(s.shape, s.dtype) if blocked(sp) else s for s, sp in zip(shapes, ospec_list)]
    call = _pallas_call(body, out_shape=shapes[0] if single else tuple(shapes), **kw)

    def run(*args):
        arrays = [pltpu.with_memory_space_constraint(a, pltpu.HBM) if blocked(sp) else a
                  for a, sp in zip(args[n_pre:], ispecs)]
        return call(*args[:n_pre], *arrays)

    return run


def _cp(vmem_mb=None, **kw):
    if vmem_mb is not None:
        kw["vmem_limit_bytes"] = vmem_mb << 20
    return pltpu.CompilerParams(**kw)


def _silu(z):
    return z * jax.nn.sigmoid(z)


def _dsilu(z):
    sg = jax.nn.sigmoid(z)
    return sg * (1.0 + z * (1.0 - sg))


def _row_start(i):
    return min(max(i - WIN_H // 2, 0), ROWS - WIN_H)


def _my_pos():
    return lax.axis_index("x"), lax.axis_index("y"), lax.axis_index("c")


def _flip(v, bit):
    return 1 - v if bit else v


def _all_gather8(xin, name):
    R, N = xin.shape

    def body(x_ref, o_ref, ssem, rsem, lsem):
        x, y, c = _my_pos()
        me = 4 * x + 2 * y + c
        own = pltpu.make_async_copy(x_ref, o_ref.at[me], lsem)
        own.start()
        sends = []
        for k in range(1, 8):
            tgt = (_flip(x, (k >> 2) & 1), _flip(y, (k >> 1) & 1), _flip(c, k & 1))
            cp = pltpu.make_async_remote_copy(src_ref=x_ref, dst_ref=o_ref.at[me], send_sem=ssem.at[k - 1],
                                              recv_sem=rsem.at[k - 1], device_id=tgt, device_id_type=MESH)
            cp.start()
            sends.append(cp)
        for k in range(1, 8):
            tgt = (_flip(x, (k >> 2) & 1), _flip(y, (k >> 1) & 1), _flip(c, k & 1))
            peer = 4 * tgt[0] + 2 * tgt[1] + tgt[2]
            pltpu.make_async_remote_copy(src_ref=x_ref, dst_ref=o_ref.at[peer], send_sem=ssem.at[k - 1],
                                         recv_sem=rsem.at[k - 1], device_id=tgt, device_id_type=MESH).wait_recv()
        for cp in sends:
            cp.wait_send()
        own.wait()

    return _hbm_call(
        body, name=name, out_shape=SDS((8, R, N), xin.dtype), in_specs=[VMEM_SPEC], out_specs=VMEM_SPEC,
        scratch_shapes=[pltpu.SemaphoreType.DMA((7,)), pltpu.SemaphoreType.DMA((7,)), pltpu.SemaphoreType.DMA],
    )(xin)


def _chip_gather(smalls, name, after=None):
    ns = len(smalls)

    def body(*refs):
        s_in, s_out = refs[:ns], refs[ns + 1:2 * ns + 1]
        ssem, rsem, lsem = refs[2 * ns + 1:]
        x, y, c = _my_pos()
        j = 2 * x + y
        chips = _peer_chips(x, y, c)
        local = [pltpu.make_async_copy(s_in[a], s_out[a].at[j], lsem.at[a]) for a in range(ns)]
        for cp in local:
            cp.start()
        sends = []
        for a in range(ns):
            for k in range(3):
                cp = pltpu.make_async_remote_copy(src_ref=s_in[a], dst_ref=s_out[a].at[j], send_sem=ssem.at[3 * a + k],
                                                  recv_sem=rsem.at[3 * a + k], device_id=chips[k][0], device_id_type=MESH)
                cp.start()
                sends.append(cp)
        for a in range(ns):
            for k in range(3):
                pltpu.make_async_remote_copy(src_ref=s_in[a], dst_ref=s_out[a].at[chips[k][1]], send_sem=ssem.at[3 * a + k],
                                             recv_sem=rsem.at[3 * a + k], device_id=chips[k][0],
                                             device_id_type=MESH).wait_recv()
        for cp in sends:
            cp.wait_send()
        for cp in local:
            cp.wait()

    return _hbm_call(
        body, name=name, out_shape=[SDS((4,) + a.shape, a.dtype) for a in smalls],
        in_specs=[VMEM_SPEC] * ns + [ANY_SPEC], out_specs=[VMEM_SPEC] * ns,
        scratch_shapes=[pltpu.SemaphoreType.DMA((3 * ns,)), pltpu.SemaphoreType.DMA((3 * ns,)),
                        pltpu.SemaphoreType.DMA((ns,))],
    )(*smalls, smalls[0] if after is None else after)


HBM_SPEC = pl.BlockSpec(memory_space=pltpu.HBM)
SEM_SPEC = pl.BlockSpec(memory_space=pltpu.SEMAPHORE)
DATAFLOW = pltpu.SideEffectType.DATAFLOW_SIDE_EFFECTING


def _peer_chips(x, y, c):
    out = []
    for k in range(1, 4):
        px, py = _flip(x, (k >> 1) & 1), _flip(y, k & 1)
        out.append(((px, py, c), 2 * px + py))
    return out


def _half_copies(srcs, dsts, ssem, rsem):
    x, y, c = _my_pos()
    j = 2 * x + y
    pairs = []
    for a in range(len(srcs)):
        half = srcs[a].shape[1] // 2
        mine = pl.ds(pl.multiple_of(c * half, 8), half)
        for k, (dev, pj) in enumerate(_peer_chips(x, y, c)):
            sem = 3 * a + k
            send = pltpu.make_async_remote_copy(src_ref=srcs[a].at[j, mine], dst_ref=dsts[a].at[j, mine],
                                                send_sem=ssem.at[sem], recv_sem=rsem.at[sem], device_id=dev,
                                                device_id_type=MESH)
            arrive = pltpu.make_async_remote_copy(src_ref=srcs[a].at[j, mine], dst_ref=dsts[a].at[pj, mine],
                                                  send_sem=ssem.at[sem], recv_sem=rsem.at[sem], device_id=dev,
                                                  device_id_type=MESH)
            pairs.append((send, arrive))
    return pairs


def _halves_start(bigs, after, name):
    nb = len(bigs)

    def body(*refs):
        b_in = refs[:nb]
        ssem, rsem = refs[nb + 1], refs[nb + 2]
        b_out = refs[nb + 3:2 * nb + 3]
        token = refs[2 * nb + 3]
        for send, _ in _half_copies(b_in, b_out, ssem, rsem):
            send.start()
        token[...] = jnp.zeros_like(token)

    out_shape = (pltpu.SemaphoreType.DMA((3 * nb,)), pltpu.SemaphoreType.DMA((3 * nb,)),
                 *[pltpu.HBM(b.shape, b.dtype) for b in bigs], SDS((8, 128), f32))
    return _hbm_call(
        body, name=name, out_shape=out_shape, in_specs=[HBM_SPEC] * nb + [ANY_SPEC],
        out_specs=(SEM_SPEC, SEM_SPEC, *[HBM_SPEC] * nb, VMEM_SPEC),
        input_output_aliases={a: 2 + a for a in range(nb)}, compiler_params=_cp(has_side_effects=DATAFLOW),
    )(*[pltpu.with_memory_space_constraint(b, pltpu.HBM) for b in bigs], after)


def _halves_wait(ssem, rsem, bigs, after, name):
    nb = len(bigs)

    def body(*refs):
        b_in = refs[:nb]
        ssem_ref, rsem_ref = refs[nb], refs[nb + 1]
        for send, arrive in _half_copies(b_in, b_in, ssem_ref, rsem_ref):
            send.wait_send()
            arrive.wait_recv()

    return _hbm_call(
        body, name=name, out_shape=tuple(pltpu.HBM(b.shape, b.dtype) for b in bigs),
        in_specs=[HBM_SPEC] * nb + [SEM_SPEC, SEM_SPEC, ANY_SPEC], out_specs=tuple([HBM_SPEC] * nb),
        input_output_aliases={a: a for a in range(nb)}, compiler_params=_cp(has_side_effects=DATAFLOW),
    )(*bigs, ssem, rsem, after)


def _halves_forward(bigs, name):
    nb = len(bigs)

    def body(*refs):
        b_in, b_out = refs[:nb], refs[nb:2 * nb]
        ssem, rsem = refs[2 * nb:]
        x, y, c = _my_pos()
        sib = (x, y, 1 - c)
        sends = []
        for a in range(nb):
            half = b_in[a].shape[1] // 2
            mine = pl.ds(pl.multiple_of(c * half, 8), half)
            for k, (_, pj) in enumerate(_peer_chips(x, y, c)):
                cp = pltpu.make_async_remote_copy(src_ref=b_in[a].at[pj, mine], dst_ref=b_out[a].at[pj, mine],
                                                  send_sem=ssem.at[3 * a + k], recv_sem=rsem.at[3 * a + k],
                                                  device_id=sib, device_id_type=MESH)
                cp.start()
                sends.append(cp)
        for a in range(nb):
            half = b_in[a].shape[1] // 2
            other = pl.ds(pl.multiple_of((1 - c) * half, 8), half)
            for k, (_, pj) in enumerate(_peer_chips(x, y, c)):
                pltpu.make_async_remote_copy(src_ref=b_in[a].at[pj, other], dst_ref=b_out[a].at[pj, other],
                                             send_sem=ssem.at[3 * a + k], recv_sem=rsem.at[3 * a + k],
                                             device_id=sib, device_id_type=MESH).wait_recv()
        for cp in sends:
            cp.wait_send()

    return _hbm_call(
        body, name=name, out_shape=[SDS(b.shape, b.dtype) for b in bigs], in_specs=[ANY_SPEC] * nb,
        out_specs=[ANY_SPEC] * nb, input_output_aliases={a: a for a in range(nb)},
        scratch_shapes=[pltpu.SemaphoreType.DMA((3 * nb,)), pltpu.SemaphoreType.DMA((3 * nb,))],
    )(*bigs)


def _cast_to_slot(w, jvec, name):
    rows, cols = w.shape
    tr = 256

    def body(j_ref, w_ref, o_ref):
        o_ref[...] = w_ref[...].astype(bf16)

    grid_spec = pltpu.PrefetchScalarGridSpec(
        num_scalar_prefetch=1, grid=(rows // tr,), in_specs=[pl.BlockSpec((tr, cols), lambda i, j: (i, 0))],
        out_specs=pl.BlockSpec((None, tr, cols), lambda i, j: (j[0], i, 0)))
    return _hbm_call(body, name=name, out_shape=SDS((4, rows, cols), bf16), grid_spec=grid_spec)(jvec, w)


def _exchange_copies(srcs, lands, ssem, rsem):
    x, y, c = _my_pos()
    cps = []
    for a in range(len(srcs)):
        stacked = len(srcs[a].shape) == 3
        rb = srcs[a].shape[1] if stacked else srcs[a].shape[0] // 4
        half = rb // 2
        for jb in range(4):
            if stacked:
                src = srcs[a].at[jb, pl.ds(pl.multiple_of((1 - c) * half, 8), half)]
            else:
                src = srcs[a].at[pl.ds(pl.multiple_of(jb * rb + (1 - c) * half, 8), half)]
            cps.append(pltpu.make_async_remote_copy(src_ref=src, dst_ref=lands[a].at[jb], send_sem=ssem.at[4 * a + jb],
                                                    recv_sem=rsem.at[4 * a + jb], device_id=(x, y, 1 - c),
                                                    device_id_type=MESH))
    return cps


def _scatter_copies(srcs, lands, ssem, rsem):
    x, y, c = _my_pos()
    cps = []
    for a in range(len(srcs)):
        for k, (dev, pj) in enumerate(_peer_chips(x, y, c)):
            cps.append(pltpu.make_async_remote_copy(src_ref=srcs[a].at[pj], dst_ref=lands[a].at[k],
                                                    send_sem=ssem.at[3 * a + k], recv_sem=rsem.at[3 * a + k],
                                                    device_id=dev, device_id_type=MESH))
    return cps


def _split_start(srcs, land_shapes, n_cp, make, name):
    ns, nl = len(srcs), len(land_shapes)

    def body(*refs):
        s_in = refs[:ns]
        ssem, rsem = refs[ns + nl], refs[ns + nl + 1]
        l_out = refs[2 * ns + nl + 2:2 * ns + 2 * nl + 2]
        token = refs[2 * ns + 2 * nl + 2]
        for cp in make(s_in, l_out, ssem, rsem):
            cp.start()
        token[...] = jnp.zeros_like(token)

    lands = [pltpu.with_memory_space_constraint(lax.empty(sh.shape, sh.dtype), pltpu.HBM) for sh in land_shapes]
    out_shape = (pltpu.SemaphoreType.DMA((n_cp,)), pltpu.SemaphoreType.DMA((n_cp,)),
                 *[pltpu.HBM(b.shape, b.dtype) for b in srcs], *[pltpu.HBM(b.shape, b.dtype) for b in land_shapes],
                 SDS((8, 128), f32))
    return _hbm_call(
        body, name=name, out_shape=out_shape, in_specs=[HBM_SPEC] * (ns + nl),
        out_specs=(SEM_SPEC, SEM_SPEC, *[HBM_SPEC] * (ns + nl), VMEM_SPEC),
        input_output_aliases={i: 2 + i for i in range(ns + nl)}, compiler_params=_cp(has_side_effects=DATAFLOW),
    )(*[pltpu.with_memory_space_constraint(b, pltpu.HBM) for b in srcs], *lands)


def _split_wait(ssem, rsem, srcs, lands, after, make, name):
    ns, nl = len(srcs), len(lands)

    def body(*refs):
        s_in, l_in = refs[:ns], refs[ns:ns + nl]
        ssem_ref, rsem_ref = refs[ns + nl], refs[ns + nl + 1]
        for cp in make(s_in, l_in, ssem_ref, rsem_ref):
            cp.wait_send()
            cp.wait_recv()

    outs = _hbm_call(
        body, name=name, out_shape=tuple(pltpu.HBM(b.shape, b.dtype) for b in (*srcs, *lands)),
        in_specs=[HBM_SPEC] * (ns + nl) + [SEM_SPEC, SEM_SPEC, ANY_SPEC], out_specs=tuple([HBM_SPEC] * (ns + nl)),
        input_output_aliases={i: i for i in range(ns + nl)}, compiler_params=_cp(has_side_effects=DATAFLOW),
    )(*srcs, *lands, ssem, rsem, after)
    return list(outs[:ns]), list(outs[ns:])


def _sibling_send(halves, name):
    n = len(halves)

    def body(*refs):
        ins, outs = refs[:n], refs[n:2 * n]
        ssem, rsem = refs[2 * n:]
        x, y, c = _my_pos()
        cps = []
        for a in range(n):
            cp = pltpu.make_async_remote_copy(src_ref=ins[a], dst_ref=outs[a], send_sem=ssem.at[a],
                                              recv_sem=rsem.at[a], device_id=(x, y, 1 - c), device_id_type=MESH)
            cp.start()
            cps.append(cp)
        for cp in cps:
            cp.wait_recv()
        for cp in cps:
            cp.wait_send()

    out_shape = [SDS(h.shape, h.dtype) for h in halves]
    return _hbm_call(
        body, name=name, out_shape=out_shape, in_specs=[ANY_SPEC] * n, out_specs=[ANY_SPEC] * n,
        scratch_shapes=[pltpu.SemaphoreType.DMA((n,)), pltpu.SemaphoreType.DMA((n,))],
    )(*halves)


def _adaln_shard(cc, w_ada_shard):
    def body(c_ref, w_ref, m_ref, sc_ref):
        sc = _silu(c_ref[...])
        sc_ref[...] = sc
        m_ref[...] = jnp.dot(sc, w_ref[...], precision=HIGHEST, preferred_element_type=f32)

    return _hbm_call(
        body, name="adaln_shard", out_shape=(SDS((16, w_ada_shard.shape[1]), f32), SDS((16, D), f32)),
        in_specs=[VMEM_SPEC, VMEM_SPEC], out_specs=(VMEM_SPEC, VMEM_SPEC), compiler_params=_cp(32),
    )(cc, w_ada_shard)


def _prenorm(xx, norm_g, mrow, b_ada, tm, name):
    n = xx.shape[0]

    def body(x_ref, g_ref, m_ref, b_ref, h_ref):
        x = x_ref[...]
        shift = m_ref[:, 0:D] + b_ref[:, 0:D]
        scale = m_ref[:, D:2 * D] + b_ref[:, D:2 * D]
        r = lax.rsqrt(jnp.mean(x * x, axis=-1, keepdims=True) + RMS_EPS)
        y = (x * r) * g_ref[...]
        h_ref[...] = (y * (1.0 + scale) + shift).astype(bf16)

    row = lambda i: (i, 0)
    fixed = lambda i: (0, 0)
    return _hbm_call(
        body, name=name, out_shape=SDS((n, D), bf16), grid=(n // tm,),
        in_specs=[pl.BlockSpec((tm, D), row), pl.BlockSpec((1, D), fixed), pl.BlockSpec((1, 3 * D), fixed),
                  pl.BlockSpec((1, 3 * D), fixed)],
        out_specs=pl.BlockSpec((tm, D), row),
    )(xx, norm_g, mrow, b_ada)


def _in_proj(h, w4):
    tm = 512

    def body(h_ref, w_ref, p_ref):
        p_ref[...] = jnp.dot(h_ref[...], w_ref[...], preferred_element_type=f32)

    return _hbm_call(
        body, name="in_proj", out_shape=SDS((S, 4 * D), f32), grid=(4, S // tm),
        in_specs=[pl.BlockSpec((tm, D), lambda j, k: (k, 0)), pl.BlockSpec((None, D, D), lambda j, k: (j, 0, 0))],
        out_specs=pl.BlockSpec((tm, D), lambda j, k: (k, j)),
    )(h, w4)


def _ctx_proj(hc, w4):
    def body(h_ref, w0_ref, w1_ref, p_ref):
        hv = h_ref[...]
        p_ref[:, 0:DA] = jnp.dot(hv, w0_ref[:, DA:2 * DA], preferred_element_type=f32)
        p_ref[:, DA:2 * DA] = jnp.dot(hv, w1_ref[:, 0:DA], preferred_element_type=f32)

    return _hbm_call(
        body, name="ctx_proj", out_shape=SDS((L, 2 * DA), f32), grid=(1,),
        in_specs=[pl.BlockSpec((L, D), lambda i: (0, 0)), pl.BlockSpec((None, D, D), lambda i: (0, 0, 0)),
                  pl.BlockSpec((None, D, D), lambda i: (1, 0, 0))],
        out_specs=pl.BlockSpec((L, 2 * DA), lambda i: (0, 0)),
    )(hc, w4, w4)


def _head_ones():
    r = lax.broadcasted_iota(i32, (DA, DA), 0) // DH
    c = lax.broadcasted_iota(i32, (DA, DA), 1) // DH
    return (r == c).astype(bf16)


def _head_sum(v, ones_bd):
    hi = v.astype(bf16)
    lo = (v - hi.astype(f32)).astype(bf16)
    return jnp.dot(hi, ones_bd, preferred_element_type=f32) + jnp.dot(lo, ones_bd, preferred_element_type=f32)


def _swap16(v):
    lane = lax.broadcasted_iota(i32, v.shape, 1)
    return jnp.where((lane & 31) < 16, pltpu.roll(v, DA - 16, 1), pltpu.roll(v, 16, 1))


def _rope_block(ct_ref, rt_ref, tm):
    rows = [jnp.tile(rt_ref[8 * j:8 * j + 8, :], (GW // 8, 1)) for j in range(tm // GW)]
    return jnp.tile(ct_ref[...], (tm // GW, 1)) + jnp.concatenate(rows, axis=0)


def _rope_specs(tm):
    col = pl.BlockSpec((GW, DA), lambda i: (0, 0))
    row = pl.BlockSpec((8 * tm // GW, DA), lambda i: (i, 0))
    return [col, row, col, row]


def _qk_prep(p, gq, gk, rope):
    tm = 256

    def body(qk_ref, v_ref, gq_ref, gk_ref, cc_ref, cr_ref, sc_ref, sr_ref, qr_ref, qp_ref, kr_ref, vh_ref):
        ones_bd = _head_ones()
        cs, sn = _rope_block(cc_ref, cr_ref, tm), _rope_block(sc_ref, sr_ref, tm)
        q = qk_ref[:, 0:DA]
        k = qk_ref[:, DA:2 * DA]
        yq = (q * lax.rsqrt(_head_sum(q * q, ones_bd) * (1.0 / DH) + RMS_EPS)) * gq_ref[...]
        yk = (k * lax.rsqrt(_head_sum(k * k, ones_bd) * (1.0 / DH) + RMS_EPS)) * gk_ref[...]
        qr = (yq * cs + _swap16(yq) * sn) * QK_SCALE
        qp = yq * QK_SCALE
        kr = yk * cs + _swap16(yk) * sn
        vv = v_ref[...]
        for hh in range(H):
            sl = slice(hh * DH, (hh + 1) * DH)
            qr_ref[hh] = qr[:, sl].astype(bf16)
            qp_ref[hh] = qp[:, sl].astype(bf16)
            kr_ref[hh] = kr[:, sl].astype(bf16)
            vh_ref[hh] = vv[:, sl].astype(bf16)

    hm = SDS((H, S, DH), bf16)
    hspec = pl.BlockSpec((H, tm, DH), lambda i: (0, i, 0))
    fixed = lambda i: (0, 0)
    return _hbm_call(
        body, name="qk_prep", out_shape=(hm, hm, hm, hm), grid=(S // tm,),
        in_specs=[pl.BlockSpec((tm, 2 * DA), lambda i: (i, 0)), pl.BlockSpec((tm, DA), lambda i: (i, 2)),
                  pl.BlockSpec((1, DA), fixed), pl.BlockSpec((1, DA), fixed)] + _rope_specs(tm),
        out_specs=(hspec, hspec, hspec, hspec),
    )(p, p, gq, gk, *rope)


def _ctx_prep(pc, gk):
    def body(p_ref, gk_ref, kc_ref, vc_ref):
        ones_bd = _head_ones()
        k = p_ref[:, 0:DA]
        yk = (k * lax.rsqrt(_head_sum(k * k, ones_bd) * (1.0 / DH) + RMS_EPS)) * gk_ref[...]
        vv = p_ref[:, DA:2 * DA]
        for hh in range(H):
            sl = slice(hh * DH, (hh + 1) * DH)
            kc_ref[hh] = yk[:, sl].astype(bf16)
            vc_ref[hh] = vv[:, sl].astype(bf16)

    hm = SDS((H, L, DH), bf16)
    return _hbm_call(
        body, name="ctx_prep", out_shape=(hm, hm), in_specs=[VMEM_SPEC, VMEM_SPEC], out_specs=(VMEM_SPEC, VMEM_SPEC),
    )(pc, gk)


def _tile_pieces():
    out = []
    for (i0, u0) in TILE_GEOM:
        rows = []
        for j in range(2):
            i = i0 + j
            rs = _row_start(i)
            rows.append([(u0 + u - i + WIN_H - 1) if rs <= u0 + u < rs + WIN_H else None for u in range(KR)])
        out.append(rows)
    return out


def _bias_prep(rpb_rev_pad):
    pieces = _tile_pieces()

    def body(r_ref, o_ref):
        rp = r_ref[...]
        xs = jnp.broadcast_to(rp[:, None, :], (N_DR, GW, 128)).reshape(N_DR * GW, 128)
        row = lax.broadcasted_iota(i32, xs.shape, 0)
        lane = lax.broadcasted_iota(i32, xs.shape, 1)
        for b in range(6):
            xs = jnp.where(((row >> b) & 1) == 1, pltpu.roll(xs, 1 << b, 1), xs)
        xs = pltpu.roll(xs, 128 - (WIN_W - 1), 1)
        k = row & (GW - 1)
        c0 = jnp.clip(lane - WIN_W // 2, 0, GW - WIN_W)
        xs = jnp.where((k >= c0) & (k < c0 + WIN_W), xs, NEG)
        neg = jnp.full((GW, GW), NEG, f32)
        for t in range(NT):
            for j in range(2):
                for u in range(KR):
                    dr = pieces[t][j][u]
                    piece = neg if dr is None else xs[dr * GW:(dr + 1) * GW, 0:GW]
                    o_ref[t, u * GW:(u + 1) * GW, j * GW:(j + 1) * GW] = piece

    return _hbm_call(
        body, name="bias_prep", out_shape=SDS((H, NT, KB, QB), f32), grid=(H,),
        in_specs=[pl.BlockSpec((None, N_DR, 128), lambda h: (h, 0, 0))],
        out_specs=pl.BlockSpec((None, NT, KB, QB), lambda h: (h, 0, 0, 0)),
    )(rpb_rev_pad)


def _block_geom(b):
    qs = b * QB
    ks = min(max(2 * b - 4, 0), ROWS - KR) * GW
    t = b if b < 2 else (b - (NQB - NT) if b > NQB - 3 else 2)
    return qs, ks, t


def _tt(a, b):
    return lax.dot_general(a, b, (((1,), (1,)), ((), ())), preferred_element_type=f32)


def _tn(a, b):
    return lax.dot_general(a, b, (((0,), (0,)), ((), ())), preferred_element_type=f32)


def _softmax_t(s_lat, s_ctx):
    m = jnp.maximum(jnp.max(s_lat, axis=0, keepdims=True), jnp.max(s_ctx, axis=0, keepdims=True))
    e_lat = jnp.exp(s_lat - m)
    e_ctx = jnp.exp(s_ctx - m)
    inv = 1.0 / (jnp.sum(e_lat, axis=0, keepdims=True) + jnp.sum(e_ctx, axis=0, keepdims=True))
    return e_lat * inv, e_ctx * inv


def _staged(n_blocks, stages):
    held = [dict() for _ in stages]
    for step in range(n_blocks + len(stages) - 1):
        for s, fn in enumerate(stages):
            b = step - s
            if 0 <= b < n_blocks:
                held[s][b] = fn(b) if s == 0 else fn(b, held[s - 1].pop(b))


def _attn_fwd(qr, qp, kr, vh, kc, vc, btt):
    def body(qr_ref, qp_ref, kr_ref, v_ref, kc_ref, vc_ref, bt_ref, o_ref):
        kcv, vcv = kc_ref[...], vc_ref[...]

        def scores(b):
            qs, ks, t = _block_geom(b)
            return (_tt(kr_ref[ks:ks + KB, :], qr_ref[qs:qs + QB, :]) + bt_ref[t], _tt(kcv, qp_ref[qs:qs + QB, :]))

        def probs(b, sc):
            p_lat, p_ctx = _softmax_t(*sc)
            return p_lat.astype(bf16), p_ctx.astype(bf16)

        def values(b, p):
            qs, ks, _ = _block_geom(b)
            o_ref[qs:qs + QB, :] = _tn(p[0], v_ref[ks:ks + KB, :]) + _tn(p[1], vcv)

        _staged(NQB, (scores, probs, values))

    sq = pl.BlockSpec((None, S, DH), lambda h: (h, 0, 0))
    sc = pl.BlockSpec((None, L, DH), lambda h: (h, 0, 0))
    return _hbm_call(
        body, name="attn_fwd", out_shape=SDS((H, S, DH), f32), grid=(H,),
        in_specs=[sq, sq, sq, sq, sc, sc, pl.BlockSpec((None, NT, KB, QB), lambda h: (h, 0, 0, 0))],
        out_specs=sq, compiler_params=_cp(48),
    )(qr, qp, kr, vh, kc, vc, btt)


def _attn_gate(o, p):
    tm = 256

    def body(o_ref, za_ref, a_ref):
        sz = _silu(za_ref[...])
        for hh in range(H):
            sl = slice(hh * DH, (hh + 1) * DH)
            a_ref[:, sl] = (o_ref[hh] * sz[:, sl]).astype(bf16)

    return _hbm_call(
        body, name="attn_gate", out_shape=SDS((S, DA), bf16), grid=(S // tm,),
        in_specs=[pl.BlockSpec((H, tm, DH), lambda i: (0, i, 0)), pl.BlockSpec((tm, DA), lambda i: (i, 3))],
        out_specs=pl.BlockSpec((tm, DA), lambda i: (i, 0)),
    )(o, p)


def _shift_rows(v, down):
    n = v.shape[0]
    row = lax.broadcasted_iota(i32, v.shape, 0)
    if down:
        return jnp.where(row == 0, 0.0, pltpu.roll(v, 1, 0))
    return jnp.where(row == n - 1, 0.0, pltpu.roll(v, n - 1, 0))


def _conv_specs():
    col = lambda off: pl.BlockSpec((S, 128), lambda i, off=off: (0, off + i))
    return [col(16), col(20), col(24), col(28), pl.BlockSpec((3, 128), lambda i: (0, i)),
            pl.BlockSpec((1, 128), lambda i: (0, i))]


def _conv_fwd(p, conv_w, conv_b):
    def body(u_ref, bg_ref, cg_ref, zc_ref, w_ref, b_ref, o_ref):
        cu = cg_ref[...] * u_ref[...]
        cv = b_ref[...] + _shift_rows(cu, True) * w_ref[0:1, :]
        cv = cv + cu * w_ref[1:2, :]
        cv = cv + _shift_rows(cu, False) * w_ref[2:3, :]
        o_ref[...] = ((bg_ref[...] * cv) * _silu(zc_ref[...])).astype(bf16)

    return _hbm_call(
        body, name="conv_fwd", out_shape=SDS((S, DC), bf16), grid=(DC // 128,),
        in_specs=_conv_specs(), out_specs=pl.BlockSpec((S, 128), lambda i: (0, i)), compiler_params=_cp(40),
    )(p, p, p, p, conv_w, conv_b)


def _out_proj_loss(attn_g, conv_g, w_out, xx, tgt, mrow, b_ada):
    tm = 256

    def body(a_ref, c_ref, w_ref, x_ref, t_ref, m_ref, b_ref, dy_ref, dmix_ref, gwo_ref, dgate_ref, loss_ref):
        k = pl.program_id(0)

        @pl.when(k == 0)
        def _():
            gwo_ref[...] = jnp.zeros_like(gwo_ref)
            dgate_ref[...] = jnp.zeros_like(dgate_ref)
            loss_ref[0, 0] = 0.0

        gate = m_ref[:, 2 * D:3 * D] + b_ref[:, 2 * D:3 * D]
        av, cv = a_ref[...], c_ref[...]
        mo = jnp.dot(av, w_ref[0:DA, :], preferred_element_type=f32)
        mo = mo + jnp.dot(cv, w_ref[DA:DA + DC, :], preferred_element_type=f32)
        y = x_ref[...] + gate * mo
        diff = y - t_ref[...]
        loss_ref[0, 0] += jnp.sum(diff * diff)
        dy = diff * (1.0 / D)
        dy_ref[...] = dy
        dgate_ref[...] += jnp.sum(dy * mo, axis=0, keepdims=True)
        dmo = (dy * gate).astype(bf16)
        dmix_ref[...] = _tt(dmo, w_ref[...])
        gwo_ref[0:DA, :] += _tn(av, dmo)
        gwo_ref[DA:DA + DC, :] += _tn(cv, dmo)

    row = lambda i: (i, 0)
    fixed = lambda i: (0, 0)
    return _hbm_call(
        body, name="out_proj_loss",
        out_shape=(SDS((S, D), f32), SDS((S, D), f32), SDS((D, D), f32), SDS((1, D), f32), SDS((1, 1), f32)),
        grid=(S // tm,),
        in_specs=[pl.BlockSpec((tm, DA), row), pl.BlockSpec((tm, DC), row), pl.BlockSpec((D, D), fixed),
                  pl.BlockSpec((tm, D), row), pl.BlockSpec((tm, D), row), pl.BlockSpec((1, 3 * D), fixed),
                  pl.BlockSpec((1, 3 * D), fixed)],
        out_specs=(pl.BlockSpec((tm, D), row), pl.BlockSpec((tm, D), row), pl.BlockSpec((D, D), fixed),
                   pl.BlockSpec((1, D), fixed), SMEM_SPEC),
        compiler_params=_cp(48, dimension_semantics=("arbitrary",)),
    )(attn_g, conv_g, w_out, xx, tgt, mrow, b_ada)


DP_Q, DP_K, DP_V, DP_ZA, DP_U, DP_BG, DP_CG, DP_ZC = range(8)


def _conv_bwd(dmix, p, conv_w, conv_b):
    def body(d_ref, u_ref, bg_ref, cg_ref, zc_ref, w_ref, b_ref, dp_ref, gw_ref, gb_ref):
        du_ref, dbg_ref, dcg_ref, dzc_ref = dp_ref.at[0], dp_ref.at[1], dp_ref.at[2], dp_ref.at[3]
        dconv = d_ref[...]
        u, bg, cg, zc = u_ref[...], bg_ref[...], cg_ref[...], zc_ref[...]
        w0, w1, w2 = w_ref[0:1, :], w_ref[1:2, :], w_ref[2:3, :]
        cu = cg * u
        cu_m, cu_p = _shift_rows(cu, True), _shift_rows(cu, False)
        cv = b_ref[...] + cu_m * w0
        cv = cv + cu * w1
        cv = cv + cu_p * w2
        sz = _silu(zc)
        dbg_ref[...] = ((dconv * sz) * cv).astype(bf16)
        dzc_ref[...] = ((dconv * (bg * cv)) * _dsilu(zc)).astype(bf16)
        dcv = (dconv * sz) * bg
        gb_ref[...] = jnp.sum(dcv, axis=0, keepdims=True)
        gw_ref[0:1, :] = jnp.sum(dcv * cu_m, axis=0, keepdims=True)
        gw_ref[1:2, :] = jnp.sum(dcv * cu, axis=0, keepdims=True)
        gw_ref[2:3, :] = jnp.sum(dcv * cu_p, axis=0, keepdims=True)
        gw_ref[3:8, :] = jnp.zeros((5, 128), f32)
        dcu = _shift_rows(dcv, False) * w0 + dcv * w1 + _shift_rows(dcv, True) * w2
        dcg_ref[...] = (dcu * u).astype(bf16)
        du_ref[...] = (dcu * cg).astype(bf16)

    return _hbm_call(
        body, name="conv_bwd", out_shape=(SDS((8, S, DC), bf16), SDS((8, DC), f32), SDS((1, DC), f32)),
        grid=(DC // 128,),
        in_specs=[pl.BlockSpec((S, 128), lambda i: (0, 4 + i))] + _conv_specs(),
        out_specs=(pl.BlockSpec((4, S, 128), lambda i: (DP_U // 4, 0, i)), pl.BlockSpec((8, 128), lambda i: (0, i)),
                   pl.BlockSpec((1, 128), lambda i: (0, i))),
        compiler_params=_cp(48),
    )(dmix, p, p, p, p, conv_w, conv_b)


def _gate_bwd(dmix, p, o, dp8):
    tm = 256

    def body(d_ref, za_ref, o_ref, dp_in_ref, do_ref, dza_ref):
        za = za_ref[...]
        dattn = d_ref[...]
        a = dattn * _silu(za)
        bb = dattn * _dsilu(za)
        for hh in range(H):
            sl = slice(hh * DH, (hh + 1) * DH)
            do_ref[hh] = a[:, sl].astype(bf16)
            dza_ref[:, sl] = (bb[:, sl] * o_ref[hh]).astype(bf16)

    hspec = pl.BlockSpec((H, tm, DH), lambda i: (0, i, 0))
    return _hbm_call(
        body, name="gate_bwd", out_shape=(SDS((H, S, DH), bf16), SDS((8, S, DA), bf16)), grid=(S // tm,),
        in_specs=[pl.BlockSpec((tm, DA), lambda i: (i, 0)), pl.BlockSpec((tm, DA), lambda i: (i, 3)), hspec, ANY_SPEC],
        out_specs=(hspec, pl.BlockSpec((None, tm, DA), lambda i: (DP_ZA, i, 0))),
        input_output_aliases={3: 1},
    )(dmix, p, o, dp8)


def _attn_bwd(qr, qp, kr, vh, kc, vc, btt, do):
    def body(qr_ref, qp_ref, kr_ref, v_ref, kc_ref, vc_ref, bt_ref, do_ref,
             dqr_ref, dqp_ref, dkr_ref, dv_ref, dkc_ref, dvc_ref, dbt_ref):
        kcv, vcv = kc_ref[...], vc_ref[...]
        dkr_ref[...] = jnp.zeros_like(dkr_ref)
        dv_ref[...] = jnp.zeros_like(dv_ref)
        dbt_ref[...] = jnp.zeros_like(dbt_ref)
        ctx_acc = {}

        def products(b):
            qs, ks, t = _block_geom(b)
            dob = do_ref[qs:qs + QB, :]
            s_lat = _tt(kr_ref[ks:ks + KB, :], qr_ref[qs:qs + QB, :]) + bt_ref[t]
            s_ctx = _tt(kcv, qp_ref[qs:qs + QB, :])
            return s_lat, s_ctx, _tt(v_ref[ks:ks + KB, :], dob), _tt(vcv, dob)

        def score_grads(b, x):
            s_lat, s_ctx, dp_lat, dp_ctx = x
            p_lat, p_ctx = _softmax_t(s_lat, s_ctx)
            delta = jnp.sum(p_lat * dp_lat, axis=0, keepdims=True) + jnp.sum(p_ctx * dp_ctx, axis=0, keepdims=True)
            ds_lat = p_lat * (dp_lat - delta)
            ds_ctx = p_ctx * (dp_ctx - delta)
            return ds_lat, ds_lat.astype(bf16), ds_ctx.astype(bf16), p_lat.astype(bf16), p_ctx.astype(bf16)

        def operand_grads(b, y):
            qs, ks, t = _block_geom(b)
            ds_lat, dsb_lat, dsb_ctx, pb_lat, pb_ctx = y
            qrb, qpb, dob = qr_ref[qs:qs + QB, :], qp_ref[qs:qs + QB, :], do_ref[qs:qs + QB, :]
            dbt_ref[t] += ds_lat
            dqr_ref[qs:qs + QB, :] = _tn(dsb_lat, kr_ref[ks:ks + KB, :])
            dqp_ref[qs:qs + QB, :] = _tn(dsb_ctx, kcv)
            dkr_ref[ks:ks + KB, :] += jnp.dot(dsb_lat, qrb, preferred_element_type=f32)
            dv_ref[ks:ks + KB, :] += jnp.dot(pb_lat, dob, preferred_element_type=f32)
            dkc = jnp.dot(dsb_ctx, qpb, preferred_element_type=f32)
            dvc = jnp.dot(pb_ctx, dob, preferred_element_type=f32)
            ctx_acc["k"] = dkc if b == 0 else ctx_acc["k"] + dkc
            ctx_acc["v"] = dvc if b == 0 else ctx_acc["v"] + dvc

        _staged(NQB, (products, score_grads, operand_grads))
        dkc_ref[...] = ctx_acc["k"]
        dvc_ref[...] = ctx_acc["v"]

    sq = pl.BlockSpec((None, S, DH), lambda h: (h, 0, 0))
    sc = pl.BlockSpec((None, L, DH), lambda h: (h, 0, 0))
    sb = pl.BlockSpec((None, NT, KB, QB), lambda h: (h, 0, 0, 0))
    big, ctxs = SDS((H, S, DH), f32), SDS((H, L, DH), f32)
    return _hbm_call(
        body, name="attn_bwd", out_shape=(big, big, big, big, ctxs, ctxs, SDS((H, NT, KB, QB), f32)), grid=(H,),
        in_specs=[sq, sq, sq, sq, sc, sc, sb, sq], out_specs=(sq, sq, sq, sq, sc, sc, sb), compiler_params=_cp(56),
    )(qr, qp, kr, vh, kc, vc, btt, do)


def _bias_bwd(dbtt):
    pieces = _tile_pieces()

    def body(d_ref, o_ref, scr):
        scr[...] = jnp.zeros_like(scr)
        acc = [None] * N_DR
        for t in range(NT):
            for j in range(2):
                for u in range(KR):
                    dr = pieces[t][j][u]
                    if dr is None:
                        continue
                    piece = d_ref[t, u * GW:(u + 1) * GW, j * GW:(j + 1) * GW]
                    acc[dr] = piece if acc[dr] is None else acc[dr] + piece
        for dr in range(N_DR):
            scr[dr * GW:(dr + 1) * GW, 0:GW] = acc[dr]
        xs = pltpu.roll(scr[...], WIN_W - 1, 1)
        row = lax.broadcasted_iota(i32, xs.shape, 0)
        for b in range(6):
            xs = jnp.where(((row >> b) & 1) == 1, pltpu.roll(xs, 128 - (1 << b), 1), xs)
        rev = jnp.sum(xs.reshape(N_DR, GW, 128), axis=1)
        a = lax.broadcasted_iota(i32, (128, 128), 0)
        b = lax.broadcasted_iota(i32, (128, 128), 1)
        flip = ((a + b == N_DC - 1) & (a < N_DC)).astype(f32)
        o_ref[...] = jnp.dot(rev, flip, precision=HIGHEST, preferred_element_type=f32)

    return _hbm_call(
        body, name="bias_bwd", out_shape=SDS((H, N_DR, 128), f32), grid=(H,),
        in_specs=[pl.BlockSpec((None, NT, KB, QB), lambda h: (h, 0, 0, 0))],
        out_specs=pl.BlockSpec((None, N_DR, 128), lambda h: (h, 0, 0)),
        scratch_shapes=[pltpu.VMEM((N_DR * GW, 128), f32)],
    )(dbtt)


def _merge_heads(ref):
    return jnp.concatenate([ref[hh] for hh in range(H)], axis=1)


def _head_norm_bwd(xraw, gain, dy, ones_bd):
    r = lax.rsqrt(_head_sum(xraw * xraw, ones_bd) * (1.0 / DH) + RMS_EPS)
    xh = xraw * r
    gdy = dy * gain
    dx = r * (gdy - xh * (_head_sum(xh * gdy, ones_bd) * (1.0 / DH)))
    return dx, jnp.sum(dy * xh, axis=0, keepdims=True)


def _qk_bwd(dqr, dqp, dkr, dvh, p, gq, gk, rope, dp8):
    tm = 256

    def body(dqr_ref, dqp_ref, dkr_ref, dv_ref, qk_ref, gq_ref, gk_ref, cc_ref, cr_ref, sc_ref, sr_ref, dp_in_ref,
             dp_ref, ggq_ref, ggk_ref):
        dq_ref, dk_ref, dvo_ref = dp_ref.at[DP_Q], dp_ref.at[DP_K], dp_ref.at[DP_V]

        @pl.when(pl.program_id(0) == 0)
        def _():
            ggq_ref[...] = jnp.zeros_like(ggq_ref)
            ggk_ref[...] = jnp.zeros_like(ggk_ref)

        ones_bd = _head_ones()
        cs, sn = _rope_block(cc_ref, cr_ref, tm), _rope_block(sc_ref, sr_ref, tm)
        a = _merge_heads(dqr_ref)
        dyq = ((a * cs - _swap16(a) * sn) + _merge_heads(dqp_ref)) * QK_SCALE
        bk = _merge_heads(dkr_ref)
        dyk = bk * cs - _swap16(bk) * sn
        dq, gq_part = _head_norm_bwd(qk_ref[:, 0:DA], gq_ref[...], dyq, ones_bd)
        dk, gk_part = _head_norm_bwd(qk_ref[:, DA:2 * DA], gk_ref[...], dyk, ones_bd)
        dq_ref[...] = dq.astype(bf16)
        dk_ref[...] = dk.astype(bf16)
        dvo_ref[...] = _merge_heads(dv_ref).astype(bf16)
        ggq_ref[...] += gq_part
        ggk_ref[...] += gk_part

    hspec = pl.BlockSpec((H, tm, DH), lambda i: (0, i, 0))
    fixed = pl.BlockSpec((1, DA), lambda i: (0, 0))
    return _hbm_call(
        body, name="qk_bwd", out_shape=(SDS((8, S, DA), bf16), SDS((1, DA), f32), SDS((1, DA), f32)), grid=(S // tm,),
        in_specs=[hspec, hspec, hspec, hspec, pl.BlockSpec((tm, 2 * DA), lambda i: (i, 0)), fixed, fixed]
        + _rope_specs(tm) + [ANY_SPEC],
        out_specs=(pl.BlockSpec((3, tm, DA), lambda i: (0, i, 0)), fixed, fixed), input_output_aliases={11: 0},
        compiler_params=_cp(40, dimension_semantics=("arbitrary",)),
    )(dqr, dqp, dkr, dvh, p, gq, gk, *rope, dp8)


def _ctx_bwd(dkc, dvc, pc, gk):
    def body(dkc_ref, dvc_ref, p_ref, gk_ref, dk_ref, dv_ref, ggk_ref):
        ones_bd = _head_ones()
        dk, gk_part = _head_norm_bwd(p_ref[:, 0:DA], gk_ref[...], _merge_heads(dkc_ref), ones_bd)
        dk_ref[...] = dk.astype(bf16)
        dv_ref[...] = _merge_heads(dvc_ref).astype(bf16)
        ggk_ref[...] = gk_part

    piece = SDS((L, DA), bf16)
    return _hbm_call(
        body, name="ctx_bwd", out_shape=(piece, piece, SDS((1, DA), f32)), in_specs=[VMEM_SPEC] * 4,
        out_specs=(VMEM_SPEC,) * 3,
    )(dkc, dvc, pc, gk)


def _grad_w_in(h, dp8, hc, dkc_raw, dvc_m):
    tm = 512

    def body(h_ref, p_ref, hc_ref, dk_ref, dv_ref, g_ref):
        j, k = pl.program_id(0), pl.program_id(1)

        @pl.when(k == 0)
        def _():
            g_ref[...] = jnp.zeros_like(g_ref)

        @pl.when((k == 0) & (j == 0))
        def _():
            g_ref[:, DA:2 * DA] = _tn(hc_ref[...], dk_ref[...])

        @pl.when((k == 0) & (j == 1))
        def _():
            g_ref[:, 0:DA] = _tn(hc_ref[...], dv_ref[...])

        hv = h_ref[...]
        g_ref[:, 0:DA] += _tn(hv, p_ref[0])
        g_ref[:, DA:2 * DA] += _tn(hv, p_ref[1])

    fixed = lambda j, k: (0, 0)
    return _hbm_call(
        body, name="grad_w_in", out_shape=SDS((4, D, D), f32), grid=(4, S // tm),
        in_specs=[pl.BlockSpec((tm, D), lambda j, k: (k, 0)), pl.BlockSpec((2, tm, DA), lambda j, k: (j, k, 0)),
                  pl.BlockSpec((L, D), fixed), pl.BlockSpec((L, DA), fixed), pl.BlockSpec((L, DA), fixed)],
        out_specs=pl.BlockSpec((None, D, D), lambda j, k: (j, 0, 0)),
        compiler_params=_cp(40, dimension_semantics=("arbitrary", "arbitrary")),
    )(h, dp8, hc, dkc_raw, dvc_m)


def _norm_mod_bwd(x, dh, g, scale):
    r = lax.rsqrt(jnp.mean(x * x, axis=-1, keepdims=True) + RMS_EPS)
    xh = x * r
    y = xh * g
    dshift = jnp.sum(dh, axis=0, keepdims=True)
    dscale = jnp.sum(dh * y, axis=0, keepdims=True)
    dyn = dh * (1.0 + scale)
    dg = jnp.sum(dyn * xh, axis=0, keepdims=True)
    gdy = dyn * g
    dx = r * (gdy - xh * jnp.mean(xh * gdy, axis=-1, keepdims=True))
    return dx, dshift, dscale, dg


def _dh_grad_x(dp8, w4, xx, dy, norm_g, mrow, b_ada):
    tm = 256

    def body(p_ref, w_ref, x_ref, dy_ref, g_ref, m_ref, b_ref, gx_ref, dsh_ref, dsc_ref, dg_ref):
        @pl.when(pl.program_id(0) == 0)
        def _():
            dsh_ref[...] = jnp.zeros_like(dsh_ref)
            dsc_ref[...] = jnp.zeros_like(dsc_ref)
            dg_ref[...] = jnp.zeros_like(dg_ref)

        dh = None
        for j in range(4):
            for half in range(2):
                term = _tt(p_ref[2 * j + half], w_ref[j, :, half * DA:(half + 1) * DA])
                dh = term if dh is None else dh + term
        scale = m_ref[:, D:2 * D] + b_ref[:, D:2 * D]
        dx, dshift, dscale, dg = _norm_mod_bwd(x_ref[...], dh, g_ref[...], scale)
        gx_ref[...] = dy_ref[...] + dx
        dsh_ref[...] += dshift
        dsc_ref[...] += dscale
        dg_ref[...] += dg

    row = lambda i: (i, 0)
    fixed = lambda i: (0, 0)
    vec = SDS((1, D), f32)
    return _hbm_call(
        body, name="dh_grad_x", out_shape=(SDS((S, D), f32), vec, vec, vec), grid=(S // tm,),
        in_specs=[pl.BlockSpec((8, tm, DA), lambda i: (0, i, 0)), pl.BlockSpec((4, D, D), lambda i: (0, 0, 0)),
                  pl.BlockSpec((tm, D), row), pl.BlockSpec((tm, D), row), pl.BlockSpec((1, D), fixed),
                  pl.BlockSpec((1, 3 * D), fixed), pl.BlockSpec((1, 3 * D), fixed)],
        out_specs=(pl.BlockSpec((tm, D), row), pl.BlockSpec((1, D), fixed), pl.BlockSpec((1, D), fixed),
                   pl.BlockSpec((1, D), fixed)),
        compiler_params=_cp(56, dimension_semantics=("arbitrary",)),
    )(dp8, w4, xx, dy, norm_g, mrow, b_ada)


def _dhc_sums(dkc_raw, dvc_m, w4, ctx2, norm_g, mrow_c, b_ada):
    def body(dk_ref, dv_ref, w0_ref, w1_ref, x_ref, g_ref, m_ref, b_ref, dsh_ref, dsc_ref, dg_ref):
        dh = _tt(dk_ref[...], w0_ref[:, DA:2 * DA]) + _tt(dv_ref[...], w1_ref[:, 0:DA])
        scale = m_ref[:, D:2 * D] + b_ref[:, D:2 * D]
        _, dshift, dscale, dg = _norm_mod_bwd(x_ref[...], dh, g_ref[...], scale)
        dsh_ref[...] = dshift
        dsc_ref[...] = dscale
        dg_ref[...] = dg

    fixed = lambda i: (0, 0)
    vec = SDS((1, D), f32)
    vspec = pl.BlockSpec((1, D), fixed)
    return _hbm_call(
        body, name="dhc_sums", out_shape=(vec, vec, vec), grid=(1,),
        in_specs=[pl.BlockSpec((L, DA), fixed), pl.BlockSpec((L, DA), fixed),
                  pl.BlockSpec((None, D, D), lambda i: (0, 0, 0)), pl.BlockSpec((None, D, D), lambda i: (1, 0, 0)),
                  pl.BlockSpec((L, D), fixed), vspec, pl.BlockSpec((1, 3 * D), fixed), pl.BlockSpec((1, 3 * D), fixed)],
        out_specs=(vspec, vspec, vspec), compiler_params=_cp(32),
    )(dkc_raw, dvc_m, w4, w4, ctx2, norm_g, mrow_c, b_ada)


def _rope_tables():
    nf = DH // 4
    inv = np.float32(ROPE_THETA) ** (-np.arange(nf, dtype=np.float32) / np.float32(nf))
    ang_c = np.arange(GW, dtype=np.float32)[:, None] * inv
    ang_r = np.arange(ROWS, dtype=np.float32)[:, None] * inv
    zc, zr = np.zeros((GW, 2 * nf), np.float32), np.zeros((ROWS, 2 * nf), np.float32)
    ct_cos = np.tile(np.concatenate([zc, np.cos(ang_c), np.cos(ang_c)], axis=1), (1, H))
    ct_sin = np.tile(np.concatenate([zc, -np.sin(ang_c), np.sin(ang_c)], axis=1), (1, H))
    rt_cos = np.tile(np.concatenate([np.cos(ang_r), np.cos(ang_r), zr], axis=1), (1, H))
    rt_sin = np.tile(np.concatenate([-np.sin(ang_r), np.sin(ang_r), zr], axis=1), (1, H))
    rep8 = lambda t: np.ascontiguousarray(np.broadcast_to(t[:, None, :], (ROWS, 8, DA))).reshape(ROWS * 8, DA)
    return tuple(jnp.asarray(t, f32) for t in (ct_cos, rep8(rt_cos), ct_sin, rep8(rt_sin)))


def _local_step(xx, ctx2, tgt, mrow, mrow_c, b_ada, norm_g, weights, q_norm_g, k_norm_g, rpb2, conv_w_full, conv_b,
                mid=None):
    gq = jnp.tile(q_norm_g, (1, H))
    gk = jnp.tile(k_norm_g, (1, H))
    rope = _rope_tables()
    rpb_pad = jnp.pad(rpb2[:, :, ::-1], ((0, 0), (0, 0), (0, 128 - N_DC)))

    h = _prenorm(xx, norm_g, mrow, b_ada, 256, "prenorm_x")
    hc = _prenorm(ctx2, norm_g, mrow_c, b_ada, L, "prenorm_ctx")
    btb = _bias_prep(rpb_pad)
    w4, w_out_full = weights(btb)
    p = _in_proj(h, w4)
    pc = _ctx_proj(hc, w4)
    qr, qp, kr, vh = _qk_prep(p, gq, gk, rope)
    kc, vc = _ctx_prep(pc, gk)
    o = _attn_fwd(qr, qp, kr, vh, kc, vc, btb)
    attn_g = _attn_gate(o, p)
    conv_g = _conv_fwd(p, conv_w_full, conv_b)
    dy, dmix, g_w_out, dgate, loss_sum = _out_proj_loss(attn_g, conv_g, w_out_full, xx, tgt, mrow, b_ada)

    dp8, g_conv_w, g_conv_b = _conv_bwd(dmix, p, conv_w_full, conv_b)
    do, dp8 = _gate_bwd(dmix, p, o, dp8)
    dqr, dqp, dkr, dvh, dkc, dvc, dbtb = _attn_bwd(qr, qp, kr, vh, kc, vc, btb, do)
    g_rpb = _bias_bwd(dbtb)
    dp8, g_gq, g_gk = _qk_bwd(dqr, dqp, dkr, dvh, p, gq, gk, rope, dp8)
    dkc_raw, dvc_m, g_gk_c = _ctx_bwd(dkc, dvc, pc, gk)
    g_w_in = _grad_w_in(h, dp8, hc, dkc_raw, dvc_m)
    b_ada_late = b_ada if mid is None else mid(g_w_in, g_w_out, b_ada)
    grad_x, dshift, dscale, dng = _dh_grad_x(dp8, w4, xx, dy, norm_g, mrow, b_ada_late)
    dshift_c, dscale_c, dng_c = _dhc_sums(dkc_raw, dvc_m, w4, ctx2, norm_g, mrow_c, b_ada_late)
    return dict(loss_sum=loss_sum, grad_x=grad_x, g_w_in=g_w_in, g_w_out=g_w_out, g_conv_w=g_conv_w,
                g_conv_b=g_conv_b, g_rpb=g_rpb, g_gq=g_gq, g_gk=g_gk, g_gk_c=g_gk_c, dshift=dshift, dscale=dscale,
                dgate=dgate, dng=dng, dshift_c=dshift_c, dscale_c=dscale_c, dng_c=dng_c)


def _pair_sum_w_in(g, r, cvec):
    tr = 128

    def body(c_ref, g_ref, r_ref, t32_ref, tb_ref):
        t = g_ref[...] + r_ref[...]
        t32_ref[...] = t
        tb_ref[...] = t.astype(bf16)

    half = D // 2
    g_spec = pl.BlockSpec((4, tr, D), lambda i, c: (0, c[0] * (half // tr) + i, 0))
    o_spec = pl.BlockSpec((4, tr, D), lambda i, c: (0, i, 0))
    grid_spec = pltpu.PrefetchScalarGridSpec(num_scalar_prefetch=1, grid=(half // tr,), in_specs=[g_spec, o_spec],
                                             out_specs=(o_spec, o_spec))
    return _hbm_call(body, name="pair_sum_w_in", out_shape=(SDS((4, half, D), f32), SDS((4, half, D), bf16)),
                     grid_spec=grid_spec, compiler_params=_cp(40))(cvec, g, r)


def _pair_sum_w_out(g, r, cvec):
    hr = D // 8

    def body(c_ref, g0, g1, g2, g3, r_ref, t32_ref, tb_ref):
        for q, g_ref in enumerate((g0, g1, g2, g3)):
            t = g_ref[...] + r_ref[q]
            t32_ref[q] = t
            tb_ref[q] = t.astype(bf16)

    gspecs = [pl.BlockSpec((hr, D), lambda i, c, q=q: (2 * q + c[0], 0)) for q in range(4)]
    full = pl.BlockSpec((4, hr, D), lambda i, c: (0, 0, 0))
    grid_spec = pltpu.PrefetchScalarGridSpec(num_scalar_prefetch=1, grid=(1,), in_specs=gspecs + [full],
                                             out_specs=(full, full))
    return _hbm_call(body, name="pair_sum_w_out", out_shape=(SDS((4, hr, D), f32), SDS((4, hr, D), bf16)),
                          grid_spec=grid_spec)(cvec, g, g, g, g, r)


def _chip_sum(t32, r2, jvec, name):
    rows = t32.shape[1]
    tr = min(rows, 128)

    def body(j_ref, t_ref, r_ref, u_ref):
        u_ref[...] = ((t_ref[...] + r_ref[0].astype(f32)) + r_ref[1].astype(f32)) + r_ref[2].astype(f32)

    grid_spec = pltpu.PrefetchScalarGridSpec(
        num_scalar_prefetch=1, grid=(rows // tr,),
        in_specs=[pl.BlockSpec((None, tr, D), lambda i, j: (j[0], i, 0)), pl.BlockSpec((3, tr, D), lambda i, j: (0, i, 0))],
        out_specs=pl.BlockSpec((tr, D), lambda i, j: (i, 0)))
    return _hbm_call(body, name=name, out_shape=SDS((rows, D), f32), grid_spec=grid_spec)(jvec, t32, r2)


_PK = {}
_off = 0
for _name, _rows in (("dm", 24), ("dmc", 24), ("dng", 8), ("dng_c", 8), ("gq", 8), ("gk", 8), ("gk_c", 8),
                     ("rpb", H * N_DR), ("conv_b", 8), ("conv_w", 16), ("loss", 8)):
    _PK[_name] = (_off, _off + _rows)
    _off += _rows
PK_ROWS = _off
RS_B_ADA, RS_NORM_G, RS_GQ, RS_GK, RS_RPB, RS_CONV_B, RS_CONV_W, RS_DMC, RS_LOSS, RS_ROWS = (
    0, 24, 32, 40, 48, 168, 176, 192, 216, 224)


def _small_reduce(gathered):
    def body(g_ref, o_ref, dm_ref):
        a0 = _PK["dm"][0]
        dm_ref[...] = jnp.zeros_like(dm_ref)
        for b in range(8):
            for i in range(24):
                dm_ref[b:b + 1, 128 * i:128 * (i + 1)] = g_ref[b, a0 + i:a0 + i + 1, :]
        tot = g_ref[0]
        for b in range(1, 8):
            tot = tot + g_ref[b]

        def rows(name):
            a, z = _PK[name]
            return tot[a:z]

        o_ref[RS_B_ADA:RS_B_ADA + 24] = rows("dm") + rows("dmc")
        o_ref[RS_NORM_G:RS_NORM_G + 8] = rows("dng") + rows("dng_c")
        gq = jnp.broadcast_to(jnp.sum(rows("gq"), axis=0, keepdims=True), (8, 128))
        gk = jnp.broadcast_to(jnp.sum(rows("gk") + rows("gk_c"), axis=0, keepdims=True), (8, 128))
        o_ref[RS_GQ:RS_GQ + 8] = gq + pltpu.roll(gq, DH, 1)
        o_ref[RS_GK:RS_GK + 8] = gk + pltpu.roll(gk, DH, 1)
        o_ref[RS_RPB:RS_RPB + H * N_DR] = rows("rpb")
        o_ref[RS_CONV_B:RS_CONV_B + 8] = rows("conv_b")
        o_ref[RS_CONV_W:RS_CONV_W + 16] = rows("conv_w")
        dmc = rows("dmc")
        o_ref[RS_DMC:RS_DMC + 24] = dmc
        o_ref[RS_LOSS:RS_LOSS + 8] = rows("loss")
        for i in range(24):
            dm_ref[8:9, 128 * i:128 * (i + 1)] = dmc[i:i + 1]

    return _hbm_call(body, name="small_reduce", out_shape=(SDS((RS_ROWS, 128), f32), SDS((16, 3 * D), f32)),
                     in_specs=[VMEM_SPEC], out_specs=(VMEM_SPEC, VMEM_SPEC))(gathered)


def _w_ada_grad(sc16, dm16, w_ada_shard, jvec):
    ncol = w_ada_shard.shape[1]

    def body(j_ref, sc_ref, dm_ref, w_ref, g_ref, part_ref):
        dm = dm_ref[...]
        g_ref[...] = lax.dot_general(sc_ref[...], dm, (((0,), (0,)), ((), ())), precision=HIGHEST,
                                     preferred_element_type=f32)
        part_ref[...] = lax.dot_general(dm[8:16], w_ref[...], (((1,), (1,)), ((), ())), precision=HIGHEST,
                                        preferred_element_type=f32)

    fixed = lambda i, j: (0, 0)
    grid_spec = pltpu.PrefetchScalarGridSpec(
        num_scalar_prefetch=1, grid=(1,),
        in_specs=[pl.BlockSpec((16, D), fixed), pl.BlockSpec((16, ncol), lambda i, j: (0, j[0])),
                  pl.BlockSpec((D, ncol), fixed)],
        out_specs=(pl.BlockSpec((D, ncol), fixed), pl.BlockSpec((8, D), fixed)))
    return _pallas_call(body, name="w_ada_grad", out_shape=(SDS((D, ncol), f32), SDS((8, D), f32)),
                        grid_spec=grid_spec, compiler_params=_cp(40))(jvec, sc16, dm16, w_ada_shard)


def _c_ctx_grad(parts4, c_ctx_row):
    def body(p_ref, c_ref, o_ref):
        tot = ((p_ref[0] + p_ref[1]) + p_ref[2]) + p_ref[3]
        o_ref[...] = tot[0:1] * _dsilu(c_ref[...])

    return _hbm_call(body, name="c_ctx_grad", out_shape=SDS((1, D), f32), in_specs=[VMEM_SPEC, VMEM_SPEC],
                          out_specs=VMEM_SPEC)(parts4, c_ctx_row)


def _adamw(w, g, m, v, name):
    rows, cols = w.shape
    tr = 256 if rows % 256 == 0 else rows

    def body(w_ref, g_ref, m_ref, v_ref, d_ref, m2_ref, v2_ref):
        gv = g_ref[...]
        m2 = ADAM_B1 * m_ref[...] + (1.0 - ADAM_B1) * gv
        v2 = ADAM_B2 * v_ref[...] + (1.0 - ADAM_B2) * jnp.square(gv)
        m_hat = m2 / (1.0 - ADAM_B1 ** ADAM_STEP)
        v_hat = v2 / (1.0 - ADAM_B2 ** ADAM_STEP)
        d_ref[...] = -ADAM_LR * (m_hat / (jnp.sqrt(v_hat) + ADAM_EPS) + ADAM_WD * w_ref[...])
        m2_ref[...] = m2
        v2_ref[...] = v2

    spec = pl.BlockSpec((tr, cols), lambda i: (i, 0))
    shp = SDS((rows, cols), f32)
    return _hbm_call(body, name=name, out_shape=(shp, shp, shp), grid=(rows // tr,), in_specs=[spec] * 4,
                          out_specs=(spec, spec, spec))(w, g, m, v)


def _adamw_halves(w, g_mine, g_other, m, v, cvec, name):
    rows, cols = w.shape
    half = rows // 2
    tr = min(256, half)
    per_half = half // tr

    def body(c_ref, w_ref, ga_ref, gb_ref, m_ref, v_ref, g_ref, d_ref, m2_ref, v2_ref):
        in_my_half = (pl.program_id(0) // per_half) == c_ref[0]
        gv = jnp.where(in_my_half, ga_ref[...], gb_ref[...])
        g_ref[...] = gv
        m2 = ADAM_B1 * m_ref[...] + (1.0 - ADAM_B1) * gv
        v2 = ADAM_B2 * v_ref[...] + (1.0 - ADAM_B2) * jnp.square(gv)
        m_hat = m2 / (1.0 - ADAM_B1 ** ADAM_STEP)
        v_hat = v2 / (1.0 - ADAM_B2 ** ADAM_STEP)
        d_ref[...] = -ADAM_LR * (m_hat / (jnp.sqrt(v_hat) + ADAM_EPS) + ADAM_WD * w_ref[...])
        m2_ref[...] = m2
        v2_ref[...] = v2

    full = pl.BlockSpec((tr, cols), lambda i, c: (i, 0))
    part = pl.BlockSpec((tr, cols), lambda i, c: (i % per_half, 0))
    shp = SDS((rows, cols), f32)
    grid_spec = pltpu.PrefetchScalarGridSpec(num_scalar_prefetch=1, grid=(rows // tr,),
                                             in_specs=[full, part, part, full, full], out_specs=(full,) * 4)
    return _hbm_call(body, name=name, out_shape=(shp,) * 4, grid_spec=grid_spec)(cvec, w, g_mine, g_other, m, v)


def _adam_math(w, g, m, v):
    m2 = ADAM_B1 * m + (1.0 - ADAM_B1) * g
    v2 = ADAM_B2 * v + (1.0 - ADAM_B2) * jnp.square(g)
    m_hat = m2 / (1.0 - ADAM_B1 ** ADAM_STEP)
    v_hat = v2 / (1.0 - ADAM_B2 ** ADAM_STEP)
    return -ADAM_LR * (m_hat / (jnp.sqrt(v_hat) + ADAM_EPS) + ADAM_WD * w), m2, v2


def _adamw_small(red, g_c_ctx, jvec, ws, ms, vs):
    n = len(ws)

    def body(*refs):
        red_ref, gc_ref, j_ref = refs[:3]
        w_refs, m_refs, v_refs = refs[3:3 + n], refs[3 + n:3 + 2 * n], refs[3 + 2 * n:3 + 3 * n]
        outs = refs[3 + 3 * n:]
        g_out, d_out, m_out, v_out = outs[:n], outs[n:2 * n], outs[2 * n:3 * n], outs[3 * n:]
        chip = j_ref[0]
        lanes = lambda i: (slice(None), slice(128 * i, 128 * (i + 1)))
        row = lambda r0, i: (lambda: red_ref[r0 + i:r0 + i + 1, :])
        whole = (slice(None), slice(None))
        chunks = [
            [(whole, lambda: gc_ref[...])],
            [(lanes(i), row(RS_B_ADA, i)) for i in range(3 * D // 128)],
            [(lanes(i), row(RS_NORM_G, i)) for i in range(D // 128)],
            [(whole, lambda: red_ref[RS_GQ:RS_GQ + 1, 0:DH])],
            [(whole, lambda: red_ref[RS_GK:RS_GK + 1, 0:DH])],
            [((0, h), (lambda h=h: red_ref[RS_RPB + N_DR * h:RS_RPB + N_DR * (h + 1), 0:N_DC])) for h in range(H)],
            [((0, slice(r, r + 1), slice(None)), (lambda r=r: red_ref[pl.ds(RS_CONV_W + 4 * r + chip, 1), :]))
             for r in range(3)],
            [(lanes(i), row(RS_CONV_B, i)) for i in range(DC // 128)],
        ]
        for a in range(n):
            for idx, grad in chunks[a]:
                g = grad()
                d, m2, v2 = _adam_math(w_refs[a][idx], g, m_refs[a][idx], v_refs[a][idx])
                g_out[a][idx] = g
                d_out[a][idx] = d
                m_out[a][idx] = m2
                v_out[a][idx] = v2

    shapes = [SDS(w.shape, f32) for w in ws]
    res = _pallas_call(body, name="adamw_small", out_shape=shapes * 4,
                       in_specs=[VMEM_SPEC, VMEM_SPEC, SMEM_SPEC] + [VMEM_SPEC] * (3 * n),
                       out_specs=[VMEM_SPEC] * (4 * n))(red, g_c_ctx, jvec, *ws, *ms, *vs)
    return [list(res[k * n:(k + 1) * n]) for k in range(4)]


def _rows128(a):
    return a.reshape(-1, 128)


def _pad_lanes(a):
    a2 = a.reshape(-1, a.shape[-1])
    return jnp.pad(a2, ((0, 0), (0, 128 - a2.shape[1])))


def kernel(x, c, ctx, c_ctx, w_ada, b_ada, norm_g, w_in, q_norm_g, k_norm_g, rpb, conv_w, conv_b, w_out, loss_target, m_c_ctx, m_w_ada, m_b_ada, m_norm_g, m_w_in, m_q_norm_g, m_k_norm_g, m_rpb, m_conv_w, m_conv_b, m_w_out, v_c_ctx, v_w_ada, v_b_ada, v_norm_g, v_w_in, v_q_norm_g, v_k_norm_g, v_rpb, v_conv_w, v_conv_b, v_w_out):
    xi, yi, ci = lax.axis_index("x"), lax.axis_index("y"), lax.axis_index("c")
    dev = 4 * xi + 2 * yi + ci
    chip = 2 * xi + yi
    cvec = jnp.reshape(ci, (1,)).astype(i32)
    jvec = jnp.reshape(chip, (1,)).astype(i32)
    w_ada_s = w_ada[0]
    ncol = w_ada_s.shape[1]

    c8 = _all_gather8(c.reshape(8, 128), "gather_c").reshape(8, D)
    cc = jnp.concatenate([c8, c_ctx.reshape(1, D), jnp.zeros((7, D), f32)], axis=0)
    m_shard, sc16 = _adaln_shard(cc, w_ada_s)

    conv_w_pad = jnp.pad(conv_w[0], ((0, 5), (0, 0)))
    m4, cw4 = _chip_gather([m_shard, conv_w_pad], "gather_mod")

    ssem, rsem, w4s, wo4s, token = _halves_start(
        [_cast_to_slot(w_in[0], jvec, "cast_w_in"), _cast_to_slot(w_out[0], jvec, "cast_w_out")], m4,
        "weights_ici_start")
    m_full = jnp.transpose(m4, (1, 0, 2)).reshape(16, 4 * ncol) + token[0:1, 0:1]
    mrow = lax.dynamic_slice(m_full, (dev, 0), (1, 3 * D))
    mrow_c = m_full[8:9]
    conv_w_full = jnp.transpose(cw4[:, 0:3, :], (1, 0, 2)).reshape(3, DC)

    def weights(after):
        w4w, wo4w = _halves_wait(ssem, rsem, [w4s, wo4s], after, "weights_ici_wait")
        w4f, wo4f = _halves_forward([w4w, wo4w], "weights_pair_forward")
        return w4f, wo4f.reshape(D, D)

    exchange = _exchange_copies
    pending = {}

    def mid(g_w_in, g_w_out, b_ada_in):
        shapes = [SDS((4, D // 2, D), f32), SDS((4, D // 8, D), f32)]
        out = _split_start([g_w_in, g_w_out], shapes, 8, exchange, "grad_pair_start")
        pending["ex"] = (out[0], out[1], list(out[2:4]), list(out[4:6]))
        return b_ada_in + out[6][0:1, 0:1]

    r = _local_step(x[0], ctx[0], loss_target[0], mrow, mrow_c, b_ada, norm_g, weights, q_norm_g, k_norm_g,
                    rpb[0], conv_w_full, conv_b, mid)
    dm = jnp.concatenate([r["dshift"], r["dscale"], r["dgate"]], axis=1)
    dmc = jnp.concatenate([r["dshift_c"], r["dscale_c"], jnp.zeros((1, D), f32)], axis=1)
    pack_parts = [_rows128(dm), _rows128(dmc), _rows128(r["dng"]), _rows128(r["dng_c"]), _rows128(r["g_gq"]),
                  _rows128(r["g_gk"]), _rows128(r["g_gk_c"]), r["g_rpb"].reshape(H * N_DR, 128),
                  _rows128(r["g_conv_b"]), _rows128(r["g_conv_w"][0:3]), jnp.pad(r["loss_sum"], ((0, 0), (0, 127)))]
    pack = jnp.concatenate([jnp.pad(p, ((0, -p.shape[0] % 8), (0, 0))) for p in pack_parts], axis=0)
    gathered = _all_gather8(pack, "gather_small")

    ex_ssem, ex_rsem, ex_srcs, ex_lands = pending["ex"]
    ex_g, ex = _split_wait(ex_ssem, ex_rsem, ex_srcs, ex_lands, gathered, exchange, "grad_pair_wait")
    t32, tb = _pair_sum_w_in(ex_g[0], ex[0], cvec)
    to32, tob = _pair_sum_w_out(ex_g[1], ex[1], cvec)
    sc_out = _split_start([tb, tob], [SDS((3, D // 2, D), bf16), SDS((3, D // 8, D), bf16)], 6, _scatter_copies,
                          "grad_chip_start")
    sc16 = sc16 + sc_out[6][0:1, 0:1]

    assert pack.shape[0] == PK_ROWS
    red, dm16 = _small_reduce(gathered)
    loss = red[RS_LOSS, 0] * (0.5 / D)

    g_w_ada_s, cpart = _w_ada_grad(sc16, dm16, w_ada_s, jvec)
    d_w_ada, nm_w_ada, nv_w_ada = _adamw(w_ada_s, g_w_ada_s, m_w_ada[0], v_w_ada[0], "adamw_w_ada")

    _, (r2, ro2) = _split_wait(sc_out[0], sc_out[1], [sc_out[2], sc_out[3]], [sc_out[4], sc_out[5]], nm_w_ada,
                               _scatter_copies, "grad_chip_wait")
    u_in = _chip_sum(t32, r2, jvec, "chip_sum_w_in")
    u_out = _chip_sum(to32, ro2, jvec, "chip_sum_w_out")

    (cparts4,) = _chip_gather([cpart], "gather_c_ctx_parts", after=u_out)
    g_c_ctx = _c_ctx_grad(cparts4, c_ctx.reshape(1, D))

    as_row = lambda a: a.reshape(1, D)
    small = _adamw_small(
        red, g_c_ctx, jvec,
        [as_row(c_ctx), b_ada, norm_g, q_norm_g, k_norm_g, rpb, conv_w, conv_b],
        [as_row(m_c_ctx), m_b_ada, m_norm_g, m_q_norm_g, m_k_norm_g, m_rpb, m_conv_w, m_conv_b],
        [as_row(v_c_ctx), v_b_ada, v_norm_g, v_q_norm_g, v_k_norm_g, v_rpb, v_conv_w, v_conv_b])

    o_in, o_out = _sibling_send([u_in, u_out], "grad_pair_send")
    g_w_in_s, d_w_in, nm_w_in, nv_w_in = _adamw_halves(w_in[0], u_in, o_in, m_w_in[0], v_w_in[0], cvec, "adamw_w_in")
    g_w_out_s, d_w_out, nm_w_out, nv_w_out = _adamw_halves(w_out[0], u_out, o_out, m_w_out[0], v_w_out[0], cvec,
                                                           "adamw_w_out")

    def ordered(kind, big_w_ada, big_w_in, big_w_out):
        s_c_ctx, s_b_ada, s_norm_g, s_q, s_k, s_rpb, s_conv_w, s_conv_b = small[kind]
        return [s_c_ctx.reshape(D), big_w_ada[None], s_b_ada, s_norm_g, big_w_in[None], s_q, s_k, s_rpb, s_conv_w,
                s_conv_b, big_w_out[None]]

    grads = ordered(0, g_w_ada_s, g_w_in_s, g_w_out_s)
    deltas = ordered(1, d_w_ada, d_w_in, d_w_out)
    new_m = ordered(2, nm_w_ada, nm_w_in, nm_w_out)
    new_v = ordered(3, nv_w_ada, nv_w_in, nv_w_out)
    return (loss, r["grad_x"][None], *grads, *deltas, *new_m, *new_v)
```

```python
import functools

import jax
import jax.numpy as jnp
import numpy as np
from jax import lax
from jax.experimental import pallas as pl
from jax.experimental.pallas import tpu as pltpu

f32, bf16, i32 = jnp.float32, jnp.bfloat16, jnp.int32
MESH = pl.DeviceIdType.MESH
HIGHEST = lax.Precision.HIGHEST

D = 1024
S = 2048
L = 256
GW = 64
ROWS = S // GW
H = 8
DH = 64
DA = H * DH
DC = 512
WIN_H, WIN_W = 8, 16
N_DR, N_DC = 2 * WIN_H - 1, 2 * WIN_W - 1
RMS_EPS = 1e-6
ROPE_THETA = 10000.0
QK_SCALE = DH ** -0.5
NEG = -1e30

QB = 128
NQB = S // QB
KR = 9
KB = KR * GW
TILE_GEOM = ((0, 0), (2, 0), (4, 0), (28, 23), (30, 23))
NT = len(TILE_GEOM)

ADAM_LR, ADAM_B1, ADAM_B2, ADAM_EPS, ADAM_WD, ADAM_STEP = 0.001, 0.9, 0.999, 1e-08, 0.01, 10

VMEM_SPEC = pl.BlockSpec(memory_space=pltpu.VMEM)
ANY_SPEC = pl.BlockSpec(memory_space=pl.ANY)
SMEM_SPEC = pl.BlockSpec(memory_space=pltpu.SMEM)
SDS = jax.ShapeDtypeStruct


_pallas_call = pl.pallas_call


def _hbm_call(body, *, out_shape, in_specs=None, out_specs=None, grid_spec=None, **kw):
    n_pre = 0
    if grid_spec is not None:
        ispecs, ospecs, n_pre = grid_spec.in_specs, grid_spec.out_specs, grid_spec.num_scalar_prefetch
        kw["grid_spec"] = grid_spec
    else:
        ispecs, ospecs = in_specs, out_specs
        kw.update(in_specs=in_specs, out_specs=out_specs)

    def blocked(spec):
        return isinstance(spec, pl.BlockSpec) and spec.block_shape is not None

    single = not isinstance(out_shape, (tuple, list))
    shapes = [out_shape] if single else list(out_shape)
    ospec_list = list(ospecs) if isinstance(ospecs, (tuple, list)) else [ospecs]
    shapes = [pltpu.HBM(s.shape, s.dtype) if blocked(sp) else s for s, sp in zip(shapes, ospec_list)]
    call = _pallas_call(body, out_shape=shapes[0] if single else tuple(shapes), **kw)

    def run(*args):
        arrays = [pltpu.with_memory_space_constraint(a, pltpu.HBM) if blocked(sp) else a
                  for a, sp in zip(args[n_pre:], ispecs)]
        return call(*args[:n_pre], *arrays)

    return run


def _cp(vmem_mb=None, **kw):
    if vmem_mb is not None:
        kw["vmem_limit_bytes"] = vmem_mb << 20
    return pltpu.CompilerParams(**kw)


def _silu(z):
    return z * jax.nn.sigmoid(z)


def _dsilu(z):
    sg = jax.nn.sigmoid(z)
    return sg * (1.0 + z * (1.0 - sg))


def _row_start(i):
    return min(max(i - WIN_H // 2, 0), ROWS - WIN_H)


def _my_pos():
    return lax.axis_index("x"), lax.axis_index("y"), lax.axis_index("c")


def _flip(v, bit):
    return 1 - v if bit else v


def _all_gather8(xin, name):
    R, N = xin.shape

    def body(x_ref, o_ref, ssem, rsem, lsem):
        x, y, c = _my_pos()
        me = 4 * x + 2 * y + c
        own = pltpu.make_async_copy(x_ref, o_ref.at[me], lsem)
        own.start()
        sends = []
        for k in range(1, 8):
            tgt = (_flip(x, (k >> 2) & 1), _flip(y, (k >> 1) & 1), _flip(c, k & 1))
            cp = pltpu.make_async_remote_copy(src_ref=x_ref, dst_ref=o_ref.at[me], send_sem=ssem.at[k - 1],
                                              recv_sem=rsem.at[k - 1], device_id=tgt, device_id_type=MESH)
            cp.start()
            sends.append(cp)
        for k in range(1, 8):
            tgt = (_flip(x, (k >> 2) & 1), _flip(y, (k >> 1) & 1), _flip(c, k & 1))
            peer = 4 * tgt[0] + 2 * tgt[1] + tgt[2]
            pltpu.make_async_remote_copy(src_ref=x_ref, dst_ref=o_ref.at[peer], send_sem=ssem.at[k - 1],
                                         recv_sem=rsem.at[k - 1], device_id=tgt, device_id_type=MESH).wait_recv()
        for cp in sends:
            cp.wait_send()
        own.wait()

    return _hbm_call(
        body, name=name, out_shape=SDS((8, R, N), xin.dtype), in_specs=[VMEM_SPEC], out_specs=VMEM_SPEC,
        scratch_shapes=[pltpu.SemaphoreType.DMA((7,)), pltpu.SemaphoreType.DMA((7,)), pltpu.SemaphoreType.DMA],
    )(xin)


def _chip_gather(smalls, name, after=None):
    ns = len(smalls)

    def body(*refs):
        s_in, s_out = refs[:ns], refs[ns + 1:2 * ns + 1]
        ssem, rsem, lsem = refs[2 * ns + 1:]
        x, y, c = _my_pos()
        j = 2 * x + y
        chips = _peer_chips(x, y, c)
        local = [pltpu.make_async_copy(s_in[a], s_out[a].at[j], lsem.at[a]) for a in range(ns)]
        for cp in local:
            cp.start()
        sends = []
        for a in range(ns):
            for k in range(3):
                cp = pltpu.make_async_remote_copy(src_ref=s_in[a], dst_ref=s_out[a].at[j], send_sem=ssem.at[3 * a + k],
                                                  recv_sem=rsem.at[3 * a + k], device_id=chips[k][0], device_id_type=MESH)
                cp.start()
                sends.append(cp)
        for a in range(ns):
            for k in range(3):
                pltpu.make_async_remote_copy(src_ref=s_in[a], dst_ref=s_out[a].at[chips[k][1]], send_sem=ssem.at[3 * a + k],
                                             recv_sem=rsem.at[3 * a + k], device_id=chips[k][0],
                                             device_id_type=MESH).wait_recv()
        for cp in sends:
            cp.wait_send()
        for cp in local:
            cp.wait()

    return _hbm_call(
        body, name=name, out_shape=[SDS((4,) + a.shape, a.dtype) for a in smalls],
        in_specs=[VMEM_SPEC] * ns + [ANY_SPEC], out_specs=[VMEM_SPEC] * ns,
        scratch_shapes=[pltpu.SemaphoreType.DMA((3 * ns,)), pltpu.SemaphoreType.DMA((3 * ns,)),
                        pltpu.SemaphoreType.DMA((ns,))],
    )(*smalls, smalls[0] if after is None else after)


HBM_SPEC = pl.BlockSpec(memory_space=pltpu.HBM)
SEM_SPEC = pl.BlockSpec(memory_space=pltpu.SEMAPHORE)
DATAFLOW = pltpu.SideEffectType.DATAFLOW_SIDE_EFFECTING


def _peer_chips(x, y, c):
    out = []
    for k in range(1, 4):
        px, py = _flip(x, (k >> 1) & 1), _flip(y, k & 1)
        out.append(((px, py, c), 2 * px + py))
    return out


def _half_copies(srcs, dsts, ssem, rsem, which):
    x, y, c = _my_pos()
    j = 2 * x + y
    peers = _peer_chips(x, y, c)
    pairs = []
    for pos, group, k in which:
        half = srcs[pos].shape[1] // 2
        mine = pl.ds(pl.multiple_of(c * half, 8), half)
        dev, pj = peers[k]
        sem = 3 * group + k
        send = pltpu.make_async_remote_copy(src_ref=srcs[pos].at[j, mine], dst_ref=dsts[pos].at[j, mine],
                                            send_sem=ssem.at[sem], recv_sem=rsem.at[sem], device_id=dev,
                                            device_id_type=MESH)
        arrive = pltpu.make_async_remote_copy(src_ref=srcs[pos].at[j, mine], dst_ref=dsts[pos].at[pj, mine],
                                              send_sem=ssem.at[sem], recv_sem=rsem.at[sem], device_id=dev,
                                              device_id_type=MESH)
        pairs.append((send, arrive))
    return pairs


def _halves_start(bigs, after, order, name):
    nb = len(bigs)

    def body(*refs):
        b_in = refs[:nb]
        ssem, rsem = refs[nb + 1], refs[nb + 2]
        b_out = refs[nb + 3:2 * nb + 3]
        token = refs[2 * nb + 3]
        for send, _ in _half_copies(b_in, b_out, ssem, rsem, order):
            send.start()
        token[...] = jnp.zeros_like(token)

    out_shape = (pltpu.SemaphoreType.DMA((3 * nb,)), pltpu.SemaphoreType.DMA((3 * nb,)),
                 *[pltpu.HBM(b.shape, b.dtype) for b in bigs], SDS((8, 128), f32))
    return _hbm_call(
        body, name=name, out_shape=out_shape, in_specs=[HBM_SPEC] * nb + [ANY_SPEC],
        out_specs=(SEM_SPEC, SEM_SPEC, *[HBM_SPEC] * nb, VMEM_SPEC),
        input_output_aliases={a: 2 + a for a in range(nb)}, compiler_params=_cp(has_side_effects=DATAFLOW),
    )(*[pltpu.with_memory_space_constraint(b, pltpu.HBM) for b in bigs], after)


def _halves_wait(ssem, rsem, bigs, after, which, name):
    nb = len(bigs)

    def body(*refs):
        b_in = refs[:nb]
        ssem_ref, rsem_ref = refs[nb], refs[nb + 1]
        for send, arrive in _half_copies(b_in, b_in, ssem_ref, rsem_ref, which):
            send.wait_send()
            arrive.wait_recv()

    return _hbm_call(
        body, name=name, out_shape=tuple(pltpu.HBM(b.shape, b.dtype) for b in bigs),
        in_specs=[HBM_SPEC] * nb + [SEM_SPEC, SEM_SPEC, ANY_SPEC], out_specs=tuple([HBM_SPEC] * nb),
        input_output_aliases={a: a for a in range(nb)}, compiler_params=_cp(has_side_effects=DATAFLOW),
    )(*bigs, ssem, rsem, after)


def _halves_forward(bigs, relations, name):
    nb, nr = len(bigs), len(relations)

    def body(*refs):
        b_in, b_out = refs[:nb], refs[nb:2 * nb]
        ssem, rsem = refs[2 * nb:]
        x, y, c = _my_pos()
        sib = (x, y, 1 - c)
        peers = _peer_chips(x, y, c)
        sends = []
        for a in range(nb):
            half = b_in[a].shape[1] // 2
            mine = pl.ds(pl.multiple_of(c * half, 8), half)
            for i, k in enumerate(relations):
                pj = peers[k][1]
                cp = pltpu.make_async_remote_copy(src_ref=b_in[a].at[pj, mine], dst_ref=b_out[a].at[pj, mine],
                                                  send_sem=ssem.at[nr * a + i], recv_sem=rsem.at[nr * a + i],
                                                  device_id=sib, device_id_type=MESH)
                cp.start()
                sends.append(cp)
        for a in range(nb):
            half = b_in[a].shape[1] // 2
            other = pl.ds(pl.multiple_of((1 - c) * half, 8), half)
            for i, k in enumerate(relations):
                pj = peers[k][1]
                pltpu.make_async_remote_copy(src_ref=b_in[a].at[pj, other], dst_ref=b_out[a].at[pj, other],
                                             send_sem=ssem.at[nr * a + i], recv_sem=rsem.at[nr * a + i],
                                             device_id=sib, device_id_type=MESH).wait_recv()
        for cp in sends:
            cp.wait_send()

    return _hbm_call(
        body, name=name, out_shape=[SDS(b.shape, b.dtype) for b in bigs], in_specs=[ANY_SPEC] * nb,
        out_specs=[ANY_SPEC] * nb, input_output_aliases={a: a for a in range(nb)},
        scratch_shapes=[pltpu.SemaphoreType.DMA((nr * nb,)), pltpu.SemaphoreType.DMA((nr * nb,))],
    )(*bigs)


def _cast_to_slot(w, jvec, name):
    rows, cols = w.shape
    tr = 256

    def body(j_ref, w_ref, o_ref):
        o_ref[...] = w_ref[...].astype(bf16)

    grid_spec = pltpu.PrefetchScalarGridSpec(
        num_scalar_prefetch=1, grid=(rows // tr,), in_specs=[pl.BlockSpec((tr, cols), lambda i, j: (i, 0))],
        out_specs=pl.BlockSpec((None, tr, cols), lambda i, j: (j[0], i, 0)))
    return _hbm_call(body, name=name, out_shape=SDS((4, rows, cols), bf16), grid_spec=grid_spec)(jvec, w)


def _exchange_copies(srcs, lands, ssem, rsem):
    x, y, c = _my_pos()
    cps = []
    for a in range(len(srcs)):
        stacked = len(srcs[a].shape) == 3
        rb = srcs[a].shape[1] if stacked else srcs[a].shape[0] // 4
        half = rb // 2
        for jb in range(4):
            if stacked:
                src = srcs[a].at[jb, pl.ds(pl.multiple_of((1 - c) * half, 8), half)]
            else:
                src = srcs[a].at[pl.ds(pl.multiple_of(jb * rb + (1 - c) * half, 8), half)]
            cps.append(pltpu.make_async_remote_copy(src_ref=src, dst_ref=lands[a].at[jb], send_sem=ssem.at[4 * a + jb],
                                                    recv_sem=rsem.at[4 * a + jb], device_id=(x, y, 1 - c),
                                                    device_id_type=MESH))
    return cps


def _scatter_copies(srcs, lands, ssem, rsem):
    x, y, c = _my_pos()
    cps = []
    for a in range(len(srcs)):
        for k, (dev, pj) in enumerate(_peer_chips(x, y, c)):
            cps.append(pltpu.make_async_remote_copy(src_ref=srcs[a].at[pj], dst_ref=lands[a].at[k],
                                                    send_sem=ssem.at[3 * a + k], recv_sem=rsem.at[3 * a + k],
                                                    device_id=dev, device_id_type=MESH))
    return cps


def _split_start(srcs, land_shapes, n_cp, make, name):
    ns, nl = len(srcs), len(land_shapes)

    def body(*refs):
        s_in = refs[:ns]
        ssem, rsem = refs[ns + nl], refs[ns + nl + 1]
        l_out = refs[2 * ns + nl + 2:2 * ns + 2 * nl + 2]
        token = refs[2 * ns + 2 * nl + 2]
        for cp in make(s_in, l_out, ssem, rsem):
            cp.start()
        token[...] = jnp.zeros_like(token)

    lands = [pltpu.with_memory_space_constraint(lax.empty(sh.shape, sh.dtype), pltpu.HBM) for sh in land_shapes]
    out_shape = (pltpu.SemaphoreType.DMA((n_cp,)), pltpu.SemaphoreType.DMA((n_cp,)),
                 *[pltpu.HBM(b.shape, b.dtype) for b in srcs], *[pltpu.HBM(b.shape, b.dtype) for b in land_shapes],
                 SDS((8, 128), f32))
    return _hbm_call(
        body, name=name, out_shape=out_shape, in_specs=[HBM_SPEC] * (ns + nl),
        out_specs=(SEM_SPEC, SEM_SPEC, *[HBM_SPEC] * (ns + nl), VMEM_SPEC),
        input_output_aliases={i: 2 + i for i in range(ns + nl)}, compiler_params=_cp(has_side_effects=DATAFLOW),
    )(*[pltpu.with_memory_space_constraint(b, pltpu.HBM) for b in srcs], *lands)


def _split_wait(ssem, rsem, srcs, lands, after, make, name):
    ns, nl = len(srcs), len(lands)

    def body(*refs):
        s_in, l_in = refs[:ns], refs[ns:ns + nl]
        ssem_ref, rsem_ref = refs[ns + nl], refs[ns + nl + 1]
        for cp in make(s_in, l_in, ssem_ref, rsem_ref):
            cp.wait_send()
            cp.wait_recv()

    outs = _hbm_call(
        body, name=name, out_shape=tuple(pltpu.HBM(b.shape, b.dtype) for b in (*srcs, *lands)),
        in_specs=[HBM_SPEC] * (ns + nl) + [SEM_SPEC, SEM_SPEC, ANY_SPEC], out_specs=tuple([HBM_SPEC] * (ns + nl)),
        input_output_aliases={i: i for i in range(ns + nl)}, compiler_params=_cp(has_side_effects=DATAFLOW),
    )(*srcs, *lands, ssem, rsem, after)
    return list(outs[:ns]), list(outs[ns:])


def _sibling_send(halves, name):
    n = len(halves)

    def body(*refs):
        ins, outs = refs[:n], refs[n:2 * n]
        ssem, rsem = refs[2 * n:]
        x, y, c = _my_pos()
        cps = []
        for a in range(n):
            cp = pltpu.make_async_remote_copy(src_ref=ins[a], dst_ref=outs[a], send_sem=ssem.at[a],
                                              recv_sem=rsem.at[a], device_id=(x, y, 1 - c), device_id_type=MESH)
            cp.start()
            cps.append(cp)
        for cp in cps:
            cp.wait_recv()
        for cp in cps:
            cp.wait_send()

    out_shape = [SDS(h.shape, h.dtype) for h in halves]
    return _hbm_call(
        body, name=name, out_shape=out_shape, in_specs=[ANY_SPEC] * n, out_specs=[ANY_SPEC] * n,
        scratch_shapes=[pltpu.SemaphoreType.DMA((n,)), pltpu.SemaphoreType.DMA((n,))],
    )(*halves)


def _adaln_shard(cc, w_ada_shard):
    def body(c_ref, w_ref, m_ref, sc_ref):
        sc = _silu(c_ref[...])
        sc_ref[...] = sc
        m_ref[...] = jnp.dot(sc, w_ref[...], precision=HIGHEST, preferred_element_type=f32)

    return _hbm_call(
        body, name="adaln_shard", out_shape=(SDS((16, w_ada_shard.shape[1]), f32), SDS((16, D), f32)),
        in_specs=[VMEM_SPEC, VMEM_SPEC], out_specs=(VMEM_SPEC, VMEM_SPEC), compiler_params=_cp(32),
    )(cc, w_ada_shard)


def _prenorm(xx, norm_g, mrow, b_ada, tm, name):
    n = xx.shape[0]

    def body(x_ref, g_ref, m_ref, b_ref, h_ref):
        x = x_ref[...]
        shift = m_ref[:, 0:D] + b_ref[:, 0:D]
        scale = m_ref[:, D:2 * D] + b_ref[:, D:2 * D]
        r = lax.rsqrt(jnp.mean(x * x, axis=-1, keepdims=True) + RMS_EPS)
        y = (x * r) * g_ref[...]
        h_ref[...] = (y * (1.0 + scale) + shift).astype(bf16)

    row = lambda i: (i, 0)
    fixed = lambda i: (0, 0)
    return _hbm_call(
        body, name=name, out_shape=SDS((n, D), bf16), grid=(n // tm,),
        in_specs=[pl.BlockSpec((tm, D), row), pl.BlockSpec((1, D), fixed), pl.BlockSpec((1, 3 * D), fixed),
                  pl.BlockSpec((1, 3 * D), fixed)],
        out_specs=pl.BlockSpec((tm, D), row),
    )(xx, norm_g, mrow, b_ada)


def _in_proj_own(h, w_own, jvec):
    tm = 512

    def body(j_ref, h_ref, w_ref, p_ref):
        p_ref[...] = jnp.dot(h_ref[...], w_ref[...].astype(bf16), preferred_element_type=f32)

    grid_spec = pltpu.PrefetchScalarGridSpec(
        num_scalar_prefetch=1, grid=(S // tm,),
        in_specs=[pl.BlockSpec((tm, D), lambda i, j: (i, 0)), pl.BlockSpec((D, D), lambda i, j: (0, 0))],
        out_specs=pl.BlockSpec((tm, D), lambda i, j: (i, j[0])))
    return _hbm_call(body, name="in_proj_own", out_shape=SDS((S, 4 * D), f32), grid_spec=grid_spec,
                     compiler_params=_cp(40))(jvec, h, w_own)


def _in_proj_block(h, w4, p, bvec, name):
    tm = 512

    def body(b_ref, h_ref, w_ref, p_in_ref, p_ref):
        p_ref[...] = jnp.dot(h_ref[...], w_ref[...], preferred_element_type=f32)

    grid_spec = pltpu.PrefetchScalarGridSpec(
        num_scalar_prefetch=1, grid=(S // tm,),
        in_specs=[pl.BlockSpec((tm, D), lambda i, b: (i, 0)), pl.BlockSpec((None, D, D), lambda i, b: (b[0], 0, 0)),
                  ANY_SPEC],
        out_specs=pl.BlockSpec((tm, D), lambda i, b: (i, b[0])))
    return _hbm_call(body, name=name, out_shape=SDS((S, 4 * D), f32), grid_spec=grid_spec,
                     input_output_aliases={3: 0})(bvec, h, w4, p)


def _ctx_proj(hc, w4):
    def body(h_ref, w0_ref, w1_ref, p_ref):
        hv = h_ref[...]
        p_ref[:, 0:DA] = jnp.dot(hv, w0_ref[:, DA:2 * DA], preferred_element_type=f32)
        p_ref[:, DA:2 * DA] = jnp.dot(hv, w1_ref[:, 0:DA], preferred_element_type=f32)

    return _hbm_call(
        body, name="ctx_proj", out_shape=SDS((L, 2 * DA), f32), grid=(1,),
        in_specs=[pl.BlockSpec((L, D), lambda i: (0, 0)), pl.BlockSpec((None, D, D), lambda i: (0, 0, 0)),
                  pl.BlockSpec((None, D, D), lambda i: (1, 0, 0))],
        out_specs=pl.BlockSpec((L, 2 * DA), lambda i: (0, 0)),
    )(hc, w4, w4)


def _head_ones():
    r = lax.broadcasted_iota(i32, (DA, DA), 0) // DH
    c = lax.broadcasted_iota(i32, (DA, DA), 1) // DH
    return (r == c).astype(bf16)


def _head_sum(v, ones_bd):
    hi = v.astype(bf16)
    lo = (v - hi.astype(f32)).astype(bf16)
    return jnp.dot(hi, ones_bd, preferred_element_type=f32) + jnp.dot(lo, ones_bd, preferred_element_type=f32)


def _swap16(v):
    lane = lax.broadcasted_iota(i32, v.shape, 1)
    return jnp.where((lane & 31) < 16, pltpu.roll(v, DA - 16, 1), pltpu.roll(v, 16, 1))


def _rope_block(ct_ref, rt_ref, tm):
    rows = [jnp.tile(rt_ref[8 * j:8 * j + 8, :], (GW // 8, 1)) for j in range(tm // GW)]
    return jnp.tile(ct_ref[...], (tm // GW, 1)) + jnp.concatenate(rows, axis=0)


def _rope_specs(tm):
    col = pl.BlockSpec((GW, DA), lambda i: (0, 0))
    row = pl.BlockSpec((8 * tm // GW, DA), lambda i: (i, 0))
    return [col, row, col, row]


def _qk_prep(p, gq, gk, rope):
    tm = 256

    def body(qk_ref, v_ref, gq_ref, gk_ref, cc_ref, cr_ref, sc_ref, sr_ref, qr_ref, qp_ref, kr_ref, vh_ref):
        ones_bd = _head_ones()
        cs, sn = _rope_block(cc_ref, cr_ref, tm), _rope_block(sc_ref, sr_ref, tm)
        q = qk_ref[:, 0:DA]
        k = qk_ref[:, DA:2 * DA]
        yq = (q * lax.rsqrt(_head_sum(q * q, ones_bd) * (1.0 / DH) + RMS_EPS)) * gq_ref[...]
        yk = (k * lax.rsqrt(_head_sum(k * k, ones_bd) * (1.0 / DH) + RMS_EPS)) * gk_ref[...]
        qr = (yq * cs + _swap16(yq) * sn) * QK_SCALE
        qp = yq * QK_SCALE
        kr = yk * cs + _swap16(yk) * sn
        vv = v_ref[...]
        for hh in range(H):
            sl = slice(hh * DH, (hh + 1) * DH)
            qr_ref[hh] = qr[:, sl].astype(bf16)
            qp_ref[hh] = qp[:, sl].astype(bf16)
            kr_ref[hh] = kr[:, sl].astype(bf16)
            vh_ref[hh] = vv[:, sl].astype(bf16)

    hm = SDS((H, S, DH), bf16)
    hspec = pl.BlockSpec((H, tm, DH), lambda i: (0, i, 0))
    fixed = lambda i: (0, 0)
    return _hbm_call(
        body, name="qk_prep", out_shape=(hm, hm, hm, hm), grid=(S // tm,),
        in_specs=[pl.BlockSpec((tm, 2 * DA), lambda i: (i, 0)), pl.BlockSpec((tm, DA), lambda i: (i, 2)),
                  pl.BlockSpec((1, DA), fixed), pl.BlockSpec((1, DA), fixed)] + _rope_specs(tm),
        out_specs=(hspec, hspec, hspec, hspec),
    )(p, p, gq, gk, *rope)


def _ctx_prep(pc, gk):
    def body(p_ref, gk_ref, kc_ref, vc_ref):
        ones_bd = _head_ones()
        k = p_ref[:, 0:DA]
        yk = (k * lax.rsqrt(_head_sum(k * k, ones_bd) * (1.0 / DH) + RMS_EPS)) * gk_ref[...]
        vv = p_ref[:, DA:2 * DA]
        for hh in range(H):
            sl = slice(hh * DH, (hh + 1) * DH)
            kc_ref[hh] = yk[:, sl].astype(bf16)
            vc_ref[hh] = vv[:, sl].astype(bf16)

    hm = SDS((H, L, DH), bf16)
    return _hbm_call(
        body, name="ctx_prep", out_shape=(hm, hm), in_specs=[VMEM_SPEC, VMEM_SPEC], out_specs=(VMEM_SPEC, VMEM_SPEC),
    )(pc, gk)


def _tile_pieces():
    out = []
    for (i0, u0) in TILE_GEOM:
        rows = []
        for j in range(2):
            i = i0 + j
            rs = _row_start(i)
            rows.append([(u0 + u - i + WIN_H - 1) if rs <= u0 + u < rs + WIN_H else None for u in range(KR)])
        out.append(rows)
    return out


def _bias_prep(rpb_rev_pad):
    pieces = _tile_pieces()

    def body(r_ref, o_ref):
        rp = r_ref[...]
        xs = jnp.broadcast_to(rp[:, None, :], (N_DR, GW, 128)).reshape(N_DR * GW, 128)
        row = lax.broadcasted_iota(i32, xs.shape, 0)
        lane = lax.broadcasted_iota(i32, xs.shape, 1)
        for b in range(6):
            xs = jnp.where(((row >> b) & 1) == 1, pltpu.roll(xs, 1 << b, 1), xs)
        xs = pltpu.roll(xs, 128 - (WIN_W - 1), 1)
        k = row & (GW - 1)
        c0 = jnp.clip(lane - WIN_W // 2, 0, GW - WIN_W)
        xs = jnp.where((k >= c0) & (k < c0 + WIN_W), xs, NEG)
        neg = jnp.full((GW, GW), NEG, f32)
        for t in range(NT):
            for j in range(2):
                for u in range(KR):
                    dr = pieces[t][j][u]
                    piece = neg if dr is None else xs[dr * GW:(dr + 1) * GW, 0:GW]
                    o_ref[t, u * GW:(u + 1) * GW, j * GW:(j + 1) * GW] = piece

    return _hbm_call(
        body, name="bias_prep", out_shape=SDS((H, NT, KB, QB), f32), grid=(H,),
        in_specs=[pl.BlockSpec((None, N_DR, 128), lambda h: (h, 0, 0))],
        out_specs=pl.BlockSpec((None, NT, KB, QB), lambda h: (h, 0, 0, 0)),
    )(rpb_rev_pad)


def _block_geom(b):
    qs = b * QB
    ks = min(max(2 * b - 4, 0), ROWS - KR) * GW
    t = b if b < 2 else (b - (NQB - NT) if b > NQB - 3 else 2)
    return qs, ks, t


def _tt(a, b):
    return lax.dot_general(a, b, (((1,), (1,)), ((), ())), preferred_element_type=f32)


def _tn(a, b):
    return lax.dot_general(a, b, (((0,), (0,)), ((), ())), preferred_element_type=f32)


def _softmax_t(s_lat, s_ctx):
    m = jnp.maximum(jnp.max(s_lat, axis=0, keepdims=True), jnp.max(s_ctx, axis=0, keepdims=True))
    e_lat = jnp.exp(s_lat - m)
    e_ctx = jnp.exp(s_ctx - m)
    inv = 1.0 / (jnp.sum(e_lat, axis=0, keepdims=True) + jnp.sum(e_ctx, axis=0, keepdims=True))
    return e_lat * inv, e_ctx * inv


def _staged(n_blocks, stages):
    held = [dict() for _ in stages]
    for step in range(n_blocks + len(stages) - 1):
        for s, fn in enumerate(stages):
            b = step - s
            if 0 <= b < n_blocks:
                held[s][b] = fn(b) if s == 0 else fn(b, held[s - 1].pop(b))


def _attn_fwd(qr, qp, kr, vh, kc, vc, btt):
    def body(qr_ref, qp_ref, kr_ref, v_ref, kc_ref, vc_ref, bt_ref, o_ref):
        kcv, vcv = kc_ref[...], vc_ref[...]

        def scores(b):
            qs, ks, t = _block_geom(b)
            return (_tt(kr_ref[ks:ks + KB, :], qr_ref[qs:qs + QB, :]) + bt_ref[t], _tt(kcv, qp_ref[qs:qs + QB, :]))

        def probs(b, sc):
            p_lat, p_ctx = _softmax_t(*sc)
            return p_lat.astype(bf16), p_ctx.astype(bf16)

        def values(b, p):
            qs, ks, _ = _block_geom(b)
            o_ref[qs:qs + QB, :] = _tn(p[0], v_ref[ks:ks + KB, :]) + _tn(p[1], vcv)

        _staged(NQB, (scores, probs, values))

    sq = pl.BlockSpec((None, S, DH), lambda h: (h, 0, 0))
    sc = pl.BlockSpec((None, L, DH), lambda h: (h, 0, 0))
    return _hbm_call(
        body, name="attn_fwd", out_shape=SDS((H, S, DH), f32), grid=(H,),
        in_specs=[sq, sq, sq, sq, sc, sc, pl.BlockSpec((None, NT, KB, QB), lambda h: (h, 0, 0, 0))],
        out_specs=sq, compiler_params=_cp(48),
    )(qr, qp, kr, vh, kc, vc, btt)


def _attn_gate(o, p):
    tm = 256

    def body(o_ref, za_ref, a_ref):
        sz = _silu(za_ref[...])
        for hh in range(H):
            sl = slice(hh * DH, (hh + 1) * DH)
            a_ref[:, sl] = (o_ref[hh] * sz[:, sl]).astype(bf16)

    return _hbm_call(
        body, name="attn_gate", out_shape=SDS((S, DA), bf16), grid=(S // tm,),
        in_specs=[pl.BlockSpec((H, tm, DH), lambda i: (0, i, 0)), pl.BlockSpec((tm, DA), lambda i: (i, 3))],
        out_specs=pl.BlockSpec((tm, DA), lambda i: (i, 0)),
    )(o, p)


def _shift_rows(v, down):
    n = v.shape[0]
    row = lax.broadcasted_iota(i32, v.shape, 0)
    if down:
        return jnp.where(row == 0, 0.0, pltpu.roll(v, 1, 0))
    return jnp.where(row == n - 1, 0.0, pltpu.roll(v, n - 1, 0))


def _conv_specs():
    col = lambda off: pl.BlockSpec((S, 128), lambda i, off=off: (0, off + i))
    return [col(16), col(20), col(24), col(28), pl.BlockSpec((3, 128), lambda i: (0, i)),
            pl.BlockSpec((1, 128), lambda i: (0, i))]


def _conv_fwd(p, conv_w, conv_b):
    def body(u_ref, bg_ref, cg_ref, zc_ref, w_ref, b_ref, o_ref):
        cu = cg_ref[...] * u_ref[...]
        cv = b_ref[...] + _shift_rows(cu, True) * w_ref[0:1, :]
        cv = cv + cu * w_ref[1:2, :]
        cv = cv + _shift_rows(cu, False) * w_ref[2:3, :]
        o_ref[...] = ((bg_ref[...] * cv) * _silu(zc_ref[...])).astype(bf16)

    return _hbm_call(
        body, name="conv_fwd", out_shape=SDS((S, DC), bf16), grid=(DC // 128,),
        in_specs=_conv_specs(), out_specs=pl.BlockSpec((S, 128), lambda i: (0, i)), compiler_params=_cp(40),
    )(p, p, p, p, conv_w, conv_b)


def _out_proj_loss(attn_g, conv_g, w_out, xx, tgt, mrow, b_ada):
    tm = 256

    def body(a_ref, c_ref, w_ref, x_ref, t_ref, m_ref, b_ref, dy_ref, dmix_ref, gwo_ref, dgate_ref, loss_ref):
        k = pl.program_id(0)

        @pl.when(k == 0)
        def _():
            gwo_ref[...] = jnp.zeros_like(gwo_ref)
            dgate_ref[...] = jnp.zeros_like(dgate_ref)
            loss_ref[0, 0] = 0.0

        gate = m_ref[:, 2 * D:3 * D] + b_ref[:, 2 * D:3 * D]
        av, cv = a_ref[...], c_ref[...]
        mo = jnp.dot(av, w_ref[0:DA, :], preferred_element_type=f32)
        mo = mo + jnp.dot(cv, w_ref[DA:DA + DC, :], preferred_element_type=f32)
        y = x_ref[...] + gate * mo
        diff = y - t_ref[...]
        loss_ref[0, 0] += jnp.sum(diff * diff)
        dy = diff * (1.0 / D)
        dy_ref[...] = dy
        dgate_ref[...] += jnp.sum(dy * mo, axis=0, keepdims=True)
        dmo = (dy * gate).astype(bf16)
        dmix_ref[...] = _tt(dmo, w_ref[...])
        gwo_ref[0:DA, :] += _tn(av, dmo)
        gwo_ref[DA:DA + DC, :] += _tn(cv, dmo)

    row = lambda i: (i, 0)
    fixed = lambda i: (0, 0)
    return _hbm_call(
        body, name="out_proj_loss",
        out_shape=(SDS((S, D), f32), SDS((S, D), f32), SDS((D, D), f32), SDS((1, D), f32), SDS((1, 1), f32)),
        grid=(S // tm,),
        in_specs=[pl.BlockSpec((tm, DA), row), pl.BlockSpec((tm, DC), row), pl.BlockSpec((D, D), fixed),
                  pl.BlockSpec((tm, D), row), pl.BlockSpec((tm, D), row), pl.BlockSpec((1, 3 * D), fixed),
                  pl.BlockSpec((1, 3 * D), fixed)],
        out_specs=(pl.BlockSpec((tm, D), row), pl.BlockSpec((tm, D), row), pl.BlockSpec((D, D), fixed),
                   pl.BlockSpec((1, D), fixed), SMEM_SPEC),
        compiler_params=_cp(48, dimension_semantics=("arbitrary",)),
    )(attn_g, conv_g, w_out, xx, tgt, mrow, b_ada)


DP_Q, DP_K, DP_V, DP_ZA, DP_U, DP_BG, DP_CG, DP_ZC = range(8)


def _conv_bwd(dmix, p, conv_w, conv_b):
    def body(d_ref, u_ref, bg_ref, cg_ref, zc_ref, w_ref, b_ref, dp_ref, gw_ref, gb_ref):
        du_ref, dbg_ref, dcg_ref, dzc_ref = dp_ref.at[0], dp_ref.at[1], dp_ref.at[2], dp_ref.at[3]
        dconv = d_ref[...]
        u, bg, cg, zc = u_ref[...], bg_ref[...], cg_ref[...], zc_ref[...]
        w0, w1, w2 = w_ref[0:1, :], w_ref[1:2, :], w_ref[2:3, :]
        cu = cg * u
        cu_m, cu_p = _shift_rows(cu, True), _shift_rows(cu, False)
        cv = b_ref[...] + cu_m * w0
        cv = cv + cu * w1
        cv = cv + cu_p * w2
        sz = _silu(zc)
        dbg_ref[...] = ((dconv * sz) * cv).astype(bf16)
        dzc_ref[...] = ((dconv * (bg * cv)) * _dsilu(zc)).astype(bf16)
        dcv = (dconv * sz) * bg
        gb_ref[...] = jnp.sum(dcv, axis=0, keepdims=True)
        gw_ref[0:1, :] = jnp.sum(dcv * cu_m, axis=0, keepdims=True)
        gw_ref[1:2, :] = jnp.sum(dcv * cu, axis=0, keepdims=True)
        gw_ref[2:3, :] = jnp.sum(dcv * cu_p, axis=0, keepdims=True)
        gw_ref[3:8, :] = jnp.zeros((5, 128), f32)
        dcu = _shift_rows(dcv, False) * w0 + dcv * w1 + _shift_rows(dcv, True) * w2
        dcg_ref[...] = (dcu * u).astype(bf16)
        du_ref[...] = (dcu * cg).astype(bf16)

    return _hbm_call(
        body, name="conv_bwd", out_shape=(SDS((8, S, DC), bf16), SDS((8, DC), f32), SDS((1, DC), f32)),
        grid=(DC // 128,),
        in_specs=[pl.BlockSpec((S, 128), lambda i: (0, 4 + i))] + _conv_specs(),
        out_specs=(pl.BlockSpec((4, S, 128), lambda i: (DP_U // 4, 0, i)), pl.BlockSpec((8, 128), lambda i: (0, i)),
                   pl.BlockSpec((1, 128), lambda i: (0, i))),
        compiler_params=_cp(48),
    )(dmix, p, p, p, p, conv_w, conv_b)


def _gate_bwd(dmix, p, o, dp8):
    tm = 256

    def body(d_ref, za_ref, o_ref, dp_in_ref, do_ref, dza_ref):
        za = za_ref[...]
        dattn = d_ref[...]
        a = dattn * _silu(za)
        bb = dattn * _dsilu(za)
        for hh in range(H):
            sl = slice(hh * DH, (hh + 1) * DH)
            do_ref[hh] = a[:, sl].astype(bf16)
            dza_ref[:, sl] = (bb[:, sl] * o_ref[hh]).astype(bf16)

    hspec = pl.BlockSpec((H, tm, DH), lambda i: (0, i, 0))
    return _hbm_call(
        body, name="gate_bwd", out_shape=(SDS((H, S, DH), bf16), SDS((8, S, DA), bf16)), grid=(S // tm,),
        in_specs=[pl.BlockSpec((tm, DA), lambda i: (i, 0)), pl.BlockSpec((tm, DA), lambda i: (i, 3)), hspec, ANY_SPEC],
        out_specs=(hspec, pl.BlockSpec((None, tm, DA), lambda i: (DP_ZA, i, 0))),
        input_output_aliases={3: 1},
    )(dmix, p, o, dp8)


def _attn_bwd(qr, qp, kr, vh, kc, vc, btt, do):
    def body(qr_ref, qp_ref, kr_ref, v_ref, kc_ref, vc_ref, bt_ref, do_ref,
             dqr_ref, dqp_ref, dkr_ref, dv_ref, dkc_ref, dvc_ref, dbt_ref):
        kcv, vcv = kc_ref[...], vc_ref[...]
        dkr_ref[...] = jnp.zeros_like(dkr_ref)
        dv_ref[...] = jnp.zeros_like(dv_ref)
        dbt_ref[...] = jnp.zeros_like(dbt_ref)
        ctx_acc = {}

        def products(b):
            qs, ks, t = _block_geom(b)
            dob = do_ref[qs:qs + QB, :]
            s_lat = _tt(kr_ref[ks:ks + KB, :], qr_ref[qs:qs + QB, :]) + bt_ref[t]
            s_ctx = _tt(kcv, qp_ref[qs:qs + QB, :])
            return s_lat, s_ctx, _tt(v_ref[ks:ks + KB, :], dob), _tt(vcv, dob)

        def score_grads(b, x):
            s_lat, s_ctx, dp_lat, dp_ctx = x
            p_lat, p_ctx = _softmax_t(s_lat, s_ctx)
            delta = jnp.sum(p_lat * dp_lat, axis=0, keepdims=True) + jnp.sum(p_ctx * dp_ctx, axis=0, keepdims=True)
            ds_lat = p_lat * (dp_lat - delta)
            ds_ctx = p_ctx * (dp_ctx - delta)
            return ds_lat, ds_lat.astype(bf16), ds_ctx.astype(bf16), p_lat.astype(bf16), p_ctx.astype(bf16)

        def operand_grads(b, y):
            qs, ks, t = _block_geom(b)
            ds_lat, dsb_lat, dsb_ctx, pb_lat, pb_ctx = y
            qrb, qpb, dob = qr_ref[qs:qs + QB, :], qp_ref[qs:qs + QB, :], do_ref[qs:qs + QB, :]
            dbt_ref[t] += ds_lat
            dqr_ref[qs:qs + QB, :] = _tn(dsb_lat, kr_ref[ks:ks + KB, :])
            dqp_ref[qs:qs + QB, :] = _tn(dsb_ctx, kcv)
            dkr_ref[ks:ks + KB, :] += jnp.dot(dsb_lat, qrb, preferred_element_type=f32)
            dv_ref[ks:ks + KB, :] += jnp.dot(pb_lat, dob, preferred_element_type=f32)
            dkc = jnp.dot(dsb_ctx, qpb, preferred_element_type=f32)
            dvc = jnp.dot(pb_ctx, dob, preferred_element_type=f32)
            ctx_acc["k"] = dkc if b == 0 else ctx_acc["k"] + dkc
            ctx_acc["v"] = dvc if b == 0 else ctx_acc["v"] + dvc

        _staged(NQB, (products, score_grads, operand_grads))
        dkc_ref[...] = ctx_acc["k"]
        dvc_ref[...] = ctx_acc["v"]

    sq = pl.BlockSpec((None, S, DH), lambda h: (h, 0, 0))
    sc = pl.BlockSpec((None, L, DH), lambda h: (h, 0, 0))
    sb = pl.BlockSpec((None, NT, KB, QB), lambda h: (h, 0, 0, 0))
    big, ctxs = SDS((H, S, DH), f32), SDS((H, L, DH), f32)
    return _hbm_call(
        body, name="attn_bwd", out_shape=(big, big, big, big, ctxs, ctxs, SDS((H, NT, KB, QB), f32)), grid=(H,),
        in_specs=[sq, sq, sq, sq, sc, sc, sb, sq], out_specs=(sq, sq, sq, sq, sc, sc, sb), compiler_params=_cp(56),
    )(qr, qp, kr, vh, kc, vc, btt, do)


def _bias_bwd(dbtt):
    pieces = _tile_pieces()

    def body(d_ref, o_ref, scr):
        scr[...] = jnp.zeros_like(scr)
        acc = [None] * N_DR
        for t in range(NT):
            for j in range(2):
                for u in range(KR):
                    dr = pieces[t][j][u]
                    if dr is None:
                        continue
                    piece = d_ref[t, u * GW:(u + 1) * GW, j * GW:(j + 1) * GW]
                    acc[dr] = piece if acc[dr] is None else acc[dr] + piece
        for dr in range(N_DR):
            scr[dr * GW:(dr + 1) * GW, 0:GW] = acc[dr]
        xs = pltpu.roll(scr[...], WIN_W - 1, 1)
        row = lax.broadcasted_iota(i32, xs.shape, 0)
        for b in range(6):
            xs = jnp.where(((row >> b) & 1) == 1, pltpu.roll(xs, 128 - (1 << b), 1), xs)
        rev = jnp.sum(xs.reshape(N_DR, GW, 128), axis=1)
        a = lax.broadcasted_iota(i32, (128, 128), 0)
        b = lax.broadcasted_iota(i32, (128, 128), 1)
        flip = ((a + b == N_DC - 1) & (a < N_DC)).astype(f32)
        o_ref[...] = jnp.dot(rev, flip, precision=HIGHEST, preferred_element_type=f32)

    return _hbm_call(
        body, name="bias_bwd", out_shape=SDS((H, N_DR, 128), f32), grid=(H,),
        in_specs=[pl.BlockSpec((None, NT, KB, QB), lambda h: (h, 0, 0, 0))],
        out_specs=pl.BlockSpec((None, N_DR, 128), lambda h: (h, 0, 0)),
        scratch_shapes=[pltpu.VMEM((N_DR * GW, 128), f32)],
    )(dbtt)


def _merge_heads(ref):
    return jnp.concatenate([ref[hh] for hh in range(H)], axis=1)


def _head_norm_bwd(xraw, gain, dy, ones_bd):
    r = lax.rsqrt(_head_sum(xraw * xraw, ones_bd) * (1.0 / DH) + RMS_EPS)
    xh = xraw * r
    gdy = dy * gain
    dx = r * (gdy - xh * (_head_sum(xh * gdy, ones_bd) * (1.0 / DH)))
    return dx, jnp.sum(dy * xh, axis=0, keepdims=True)


def _qk_bwd(dqr, dqp, dkr, dvh, p, gq, gk, rope, dp8):
    tm = 256

    def body(dqr_ref, dqp_ref, dkr_ref, dv_ref, qk_ref, gq_ref, gk_ref, cc_ref, cr_ref, sc_ref, sr_ref, dp_in_ref,
             dp_ref, ggq_ref, ggk_ref):
        dq_ref, dk_ref, dvo_ref = dp_ref.at[DP_Q], dp_ref.at[DP_K], dp_ref.at[DP_V]

        @pl.when(pl.program_id(0) == 0)
        def _():
            ggq_ref[...] = jnp.zeros_like(ggq_ref)
            ggk_ref[...] = jnp.zeros_like(ggk_ref)

        ones_bd = _head_ones()
        cs, sn = _rope_block(cc_ref, cr_ref, tm), _rope_block(sc_ref, sr_ref, tm)
        a = _merge_heads(dqr_ref)
        dyq = ((a * cs - _swap16(a) * sn) + _merge_heads(dqp_ref)) * QK_SCALE
        bk = _merge_heads(dkr_ref)
        dyk = bk * cs - _swap16(bk) * sn
        dq, gq_part = _head_norm_bwd(qk_ref[:, 0:DA], gq_ref[...], dyq, ones_bd)
        dk, gk_part = _head_norm_bwd(qk_ref[:, DA:2 * DA], gk_ref[...], dyk, ones_bd)
        dq_ref[...] = dq.astype(bf16)
        dk_ref[...] = dk.astype(bf16)
        dvo_ref[...] = _merge_heads(dv_ref).astype(bf16)
        ggq_ref[...] += gq_part
        ggk_ref[...] += gk_part

    hspec = pl.BlockSpec((H, tm, DH), lambda i: (0, i, 0))
    fixed = pl.BlockSpec((1, DA), lambda i: (0, 0))
    return _hbm_call(
        body, name="qk_bwd", out_shape=(SDS((8, S, DA), bf16), SDS((1, DA), f32), SDS((1, DA), f32)), grid=(S // tm,),
        in_specs=[hspec, hspec, hspec, hspec, pl.BlockSpec((tm, 2 * DA), lambda i: (i, 0)), fixed, fixed]
        + _rope_specs(tm) + [ANY_SPEC],
        out_specs=(pl.BlockSpec((3, tm, DA), lambda i: (0, i, 0)), fixed, fixed), input_output_aliases={11: 0},
        compiler_params=_cp(40, dimension_semantics=("arbitrary",)),
    )(dqr, dqp, dkr, dvh, p, gq, gk, *rope, dp8)


def _ctx_bwd(dkc, dvc, pc, gk):
    def body(dkc_ref, dvc_ref, p_ref, gk_ref, dk_ref, dv_ref, ggk_ref):
        ones_bd = _head_ones()
        dk, gk_part = _head_norm_bwd(p_ref[:, 0:DA], gk_ref[...], _merge_heads(dkc_ref), ones_bd)
        dk_ref[...] = dk.astype(bf16)
        dv_ref[...] = _merge_heads(dvc_ref).astype(bf16)
        ggk_ref[...] = gk_part

    piece = SDS((L, DA), bf16)
    return _hbm_call(
        body, name="ctx_bwd", out_shape=(piece, piece, SDS((1, DA), f32)), in_specs=[VMEM_SPEC] * 4,
        out_specs=(VMEM_SPEC,) * 3,
    )(dkc, dvc, pc, gk)


def _grad_w_in(h, dp8, hc, dkc_raw, dvc_m):
    tm = 512

    def body(h_ref, p_ref, hc_ref, dk_ref, dv_ref, g_ref):
        j, k = pl.program_id(0), pl.program_id(1)

        @pl.when(k == 0)
        def _():
            g_ref[...] = jnp.zeros_like(g_ref)

        @pl.when((k == 0) & (j == 0))
        def _():
            g_ref[:, DA:2 * DA] = _tn(hc_ref[...], dk_ref[...])

        @pl.when((k == 0) & (j == 1))
        def _():
            g_ref[:, 0:DA] = _tn(hc_ref[...], dv_ref[...])

        hv = h_ref[...]
        g_ref[:, 0:DA] += _tn(hv, p_ref[0])
        g_ref[:, DA:2 * DA] += _tn(hv, p_ref[1])

    fixed = lambda j, k: (0, 0)
    return _hbm_call(
        body, name="grad_w_in", out_shape=SDS((4, D, D), f32), grid=(4, S // tm),
        in_specs=[pl.BlockSpec((tm, D), lambda j, k: (k, 0)), pl.BlockSpec((2, tm, DA), lambda j, k: (j, k, 0)),
                  pl.BlockSpec((L, D), fixed), pl.BlockSpec((L, DA), fixed), pl.BlockSpec((L, DA), fixed)],
        out_specs=pl.BlockSpec((None, D, D), lambda j, k: (j, 0, 0)),
        compiler_params=_cp(40, dimension_semantics=("arbitrary", "arbitrary")),
    )(h, dp8, hc, dkc_raw, dvc_m)


def _norm_mod_bwd(x, dh, g, scale):
    r = lax.rsqrt(jnp.mean(x * x, axis=-1, keepdims=True) + RMS_EPS)
    xh = x * r
    y = xh * g
    dshift = jnp.sum(dh, axis=0, keepdims=True)
    dscale = jnp.sum(dh * y, axis=0, keepdims=True)
    dyn = dh * (1.0 + scale)
    dg = jnp.sum(dyn * xh, axis=0, keepdims=True)
    gdy = dyn * g
    dx = r * (gdy - xh * jnp.mean(xh * gdy, axis=-1, keepdims=True))
    return dx, dshift, dscale, dg


def _dh_grad_x(dp8, w4, xx, dy, norm_g, mrow, b_ada):
    tm = 256

    def body(p_ref, w_ref, x_ref, dy_ref, g_ref, m_ref, b_ref, gx_ref, dsh_ref, dsc_ref, dg_ref):
        @pl.when(pl.program_id(0) == 0)
        def _():
            dsh_ref[...] = jnp.zeros_like(dsh_ref)
            dsc_ref[...] = jnp.zeros_like(dsc_ref)
            dg_ref[...] = jnp.zeros_like(dg_ref)

        dh = None
        for j in range(4):
            for half in range(2):
                term = _tt(p_ref[2 * j + half], w_ref[j, :, half * DA:(half + 1) * DA])
                dh = term if dh is None else dh + term
        scale = m_ref[:, D:2 * D] + b_ref[:, D:2 * D]
        dx, dshift, dscale, dg = _norm_mod_bwd(x_ref[...], dh, g_ref[...], scale)
        gx_ref[...] = dy_ref[...] + dx
        dsh_ref[...] += dshift
        dsc_ref[...] += dscale
        dg_ref[...] += dg

    row = lambda i: (i, 0)
    fixed = lambda i: (0, 0)
    vec = SDS((1, D), f32)
    return _hbm_call(
        body, name="dh_grad_x", out_shape=(SDS((S, D), f32), vec, vec, vec), grid=(S // tm,),
        in_specs=[pl.BlockSpec((8, tm, DA), lambda i: (0, i, 0)), pl.BlockSpec((4, D, D), lambda i: (0, 0, 0)),
                  pl.BlockSpec((tm, D), row), pl.BlockSpec((tm, D), row), pl.BlockSpec((1, D), fixed),
                  pl.BlockSpec((1, 3 * D), fixed), pl.BlockSpec((1, 3 * D), fixed)],
        out_specs=(pl.BlockSpec((tm, D), row), pl.BlockSpec((1, D), fixed), pl.BlockSpec((1, D), fixed),
                   pl.BlockSpec((1, D), fixed)),
        compiler_params=_cp(56, dimension_semantics=("arbitrary",)),
    )(dp8, w4, xx, dy, norm_g, mrow, b_ada)


def _dhc_sums(dkc_raw, dvc_m, w4, ctx2, norm_g, mrow_c, b_ada):
    def body(dk_ref, dv_ref, w0_ref, w1_ref, x_ref, g_ref, m_ref, b_ref, dsh_ref, dsc_ref, dg_ref):
        dh = _tt(dk_ref[...], w0_ref[:, DA:2 * DA]) + _tt(dv_ref[...], w1_ref[:, 0:DA])
        scale = m_ref[:, D:2 * D] + b_ref[:, D:2 * D]
        _, dshift, dscale, dg = _norm_mod_bwd(x_ref[...], dh, g_ref[...], scale)
        dsh_ref[...] = dshift
        dsc_ref[...] = dscale
        dg_ref[...] = dg

    fixed = lambda i: (0, 0)
    vec = SDS((1, D), f32)
    vspec = pl.BlockSpec((1, D), fixed)
    return _hbm_call(
        body, name="dhc_sums", out_shape=(vec, vec, vec), grid=(1,),
        in_specs=[pl.BlockSpec((L, DA), fixed), pl.BlockSpec((L, DA), fixed),
                  pl.BlockSpec((None, D, D), lambda i: (0, 0, 0)), pl.BlockSpec((None, D, D), lambda i: (1, 0, 0)),
                  pl.BlockSpec((L, D), fixed), vspec, pl.BlockSpec((1, 3 * D), fixed), pl.BlockSpec((1, 3 * D), fixed)],
        out_specs=(vspec, vspec, vspec), compiler_params=_cp(32),
    )(dkc_raw, dvc_m, w4, w4, ctx2, norm_g, mrow_c, b_ada)


def _rope_tables():
    nf = DH // 4
    inv = np.float32(ROPE_THETA) ** (-np.arange(nf, dtype=np.float32) / np.float32(nf))
    ang_c = np.arange(GW, dtype=np.float32)[:, None] * inv
    ang_r = np.arange(ROWS, dtype=np.float32)[:, None] * inv
    zc, zr = np.zeros((GW, 2 * nf), np.float32), np.zeros((ROWS, 2 * nf), np.float32)
    ct_cos = np.tile(np.concatenate([zc, np.cos(ang_c), np.cos(ang_c)], axis=1), (1, H))
    ct_sin = np.tile(np.concatenate([zc, -np.sin(ang_c), np.sin(ang_c)], axis=1), (1, H))
    rt_cos = np.tile(np.concatenate([np.cos(ang_r), np.cos(ang_r), zr], axis=1), (1, H))
    rt_sin = np.tile(np.concatenate([-np.sin(ang_r), np.sin(ang_r), zr], axis=1), (1, H))
    rep8 = lambda t: np.ascontiguousarray(np.broadcast_to(t[:, None, :], (ROWS, 8, DA))).reshape(ROWS * 8, DA)
    return tuple(jnp.asarray(t, f32) for t in (ct_cos, rep8(rt_cos), ct_sin, rep8(rt_sin)))


def _local_step(xx, ctx2, tgt, mrow, mrow_c, b_ada, norm_g, weights, q_norm_g, k_norm_g, rpb2, conv_w_full, conv_b,
                mid=None):
    gq = jnp.tile(q_norm_g, (1, H))
    gk = jnp.tile(k_norm_g, (1, H))
    rope = _rope_tables()
    rpb_pad = jnp.pad(rpb2[:, :, ::-1], ((0, 0), (0, 0), (0, 128 - N_DC)))

    h = _prenorm(xx, norm_g, mrow, b_ada, 256, "prenorm_x")
    hc = _prenorm(ctx2, norm_g, mrow_c, b_ada, L, "prenorm_ctx")
    btb = _bias_prep(rpb_pad)
    jv = weights["jvec"]
    p = _in_proj_own(h, weights["own"], jv)
    w4 = weights["near"](p)
    p = _in_proj_block(h, w4, p, jv ^ 1, "in_proj_y")
    p = _in_proj_block(h, w4, p, jv ^ 2, "in_proj_x")
    w4 = weights["far"](w4, p)
    p = _in_proj_block(h, w4, p, jv ^ 3, "in_proj_xy")
    pc = _ctx_proj(hc, w4)
    qr, qp, kr, vh = _qk_prep(p, gq, gk, rope)
    kc, vc = _ctx_prep(pc, gk)
    o = _attn_fwd(qr, qp, kr, vh, kc, vc, btb)
    attn_g = _attn_gate(o, p)
    conv_g = _conv_fwd(p, conv_w_full, conv_b)
    w_out_full = weights["out"](conv_g)
    dy, dmix, g_w_out, dgate, loss_sum = _out_proj_loss(attn_g, conv_g, w_out_full, xx, tgt, mrow, b_ada)

    dp8, g_conv_w, g_conv_b = _conv_bwd(dmix, p, conv_w_full, conv_b)
    do, dp8 = _gate_bwd(dmix, p, o, dp8)
    dqr, dqp, dkr, dvh, dkc, dvc, dbtb = _attn_bwd(qr, qp, kr, vh, kc, vc, btb, do)
    g_rpb = _bias_bwd(dbtb)
    dp8, g_gq, g_gk = _qk_bwd(dqr, dqp, dkr, dvh, p, gq, gk, rope, dp8)
    dkc_raw, dvc_m, g_gk_c = _ctx_bwd(dkc, dvc, pc, gk)
    g_w_in = _grad_w_in(h, dp8, hc, dkc_raw, dvc_m)
    b_ada_late = b_ada if mid is None else mid(g_w_in, g_w_out, b_ada)
    grad_x, dshift, dscale, dng = _dh_grad_x(dp8, w4, xx, dy, norm_g, mrow, b_ada_late)
    dshift_c, dscale_c, dng_c = _dhc_sums(dkc_raw, dvc_m, w4, ctx2, norm_g, mrow_c, b_ada_late)
    return dict(loss_sum=loss_sum, grad_x=grad_x, g_w_in=g_w_in, g_w_out=g_w_out, g_conv_w=g_conv_w,
                g_conv_b=g_conv_b, g_rpb=g_rpb, g_gq=g_gq, g_gk=g_gk, g_gk_c=g_gk_c, dshift=dshift, dscale=dscale,
                dgate=dgate, dng=dng, dshift_c=dshift_c, dscale_c=dscale_c, dng_c=dng_c)


def _pair_sum_w_in(g, r, cvec):
    tr = 128

    def body(c_ref, g_ref, r_ref, t32_ref, tb_ref):
        t = g_ref[...] + r_ref[...]
        t32_ref[...] = t
        tb_ref[...] = t.astype(bf16)

    half = D // 2
    g_spec = pl.BlockSpec((4, tr, D), lambda i, c: (0, c[0] * (half // tr) + i, 0))
    o_spec = pl.BlockSpec((4, tr, D), lambda i, c: (0, i, 0))
    grid_spec = pltpu.PrefetchScalarGridSpec(num_scalar_prefetch=1, grid=(half // tr,), in_specs=[g_spec, o_spec],
                                             out_specs=(o_spec, o_spec))
    return _hbm_call(body, name="pair_sum_w_in", out_shape=(SDS((4, half, D), f32), SDS((4, half, D), bf16)),
                     grid_spec=grid_spec, compiler_params=_cp(40))(cvec, g, r)


def _pair_sum_w_out(g, r, cvec):
    hr = D // 8

    def body(c_ref, g0, g1, g2, g3, r_ref, t32_ref, tb_ref):
        for q, g_ref in enumerate((g0, g1, g2, g3)):
            t = g_ref[...] + r_ref[q]
            t32_ref[q] = t
            tb_ref[q] = t.astype(bf16)

    gspecs = [pl.BlockSpec((hr, D), lambda i, c, q=q: (2 * q + c[0], 0)) for q in range(4)]
    full = pl.BlockSpec((4, hr, D), lambda i, c: (0, 0, 0))
    grid_spec = pltpu.PrefetchScalarGridSpec(num_scalar_prefetch=1, grid=(1,), in_specs=gspecs + [full],
                                             out_specs=(full, full))
    return _hbm_call(body, name="pair_sum_w_out", out_shape=(SDS((4, hr, D), f32), SDS((4, hr, D), bf16)),
                          grid_spec=grid_spec)(cvec, g, g, g, g, r)


def _chip_sum(t32, r2, jvec, name):
    rows = t32.shape[1]
    tr = min(rows, 128)

    def body(j_ref, t_ref, r_ref, u_ref):
        u_ref[...] = ((t_ref[...] + r_ref[0].astype(f32)) + r_ref[1].astype(f32)) + r_ref[2].astype(f32)

    grid_spec = pltpu.PrefetchScalarGridSpec(
        num_scalar_prefetch=1, grid=(rows // tr,),
        in_specs=[pl.BlockSpec((None, tr, D), lambda i, j: (j[0], i, 0)), pl.BlockSpec((3, tr, D), lambda i, j: (0, i, 0))],
        out_specs=pl.BlockSpec((tr, D), lambda i, j: (i, 0)))
    return _hbm_call(body, name=name, out_shape=SDS((rows, D), f32), grid_spec=grid_spec)(jvec, t32, r2)


_PK = {}
_off = 0
for _name, _rows in (("dm", 24), ("dmc", 24), ("dng", 8), ("dng_c", 8), ("gq", 8), ("gk", 8), ("gk_c", 8),
                     ("rpb", H * N_DR), ("conv_b", 8), ("conv_w", 16), ("loss", 8)):
    _PK[_name] = (_off, _off + _rows)
    _off += _rows
PK_ROWS = _off
RS_B_ADA, RS_NORM_G, RS_GQ, RS_GK, RS_RPB, RS_CONV_B, RS_CONV_W, RS_DMC, RS_LOSS, RS_ROWS = (
    0, 24, 32, 40, 48, 168, 176, 192, 216, 224)


def _small_reduce(gathered):
    def body(g_ref, o_ref, dm_ref):
        a0 = _PK["dm"][0]
        dm_ref[...] = jnp.zeros_like(dm_ref)
        for b in range(8):
            for i in range(24):
                dm_ref[b:b + 1, 128 * i:128 * (i + 1)] = g_ref[b, a0 + i:a0 + i + 1, :]
        tot = g_ref[0]
        for b in range(1, 8):
            tot = tot + g_ref[b]

        def rows(name):
            a, z = _PK[name]
            return tot[a:z]

        o_ref[RS_B_ADA:RS_B_ADA + 24] = rows("dm") + rows("dmc")
        o_ref[RS_NORM_G:RS_NORM_G + 8] = rows("dng") + rows("dng_c")
        gq = jnp.broadcast_to(jnp.sum(rows("gq"), axis=0, keepdims=True), (8, 128))
        gk = jnp.broadcast_to(jnp.sum(rows("gk") + rows("gk_c"), axis=0, keepdims=True), (8, 128))
        o_ref[RS_GQ:RS_GQ + 8] = gq + pltpu.roll(gq, DH, 1)
        o_ref[RS_GK:RS_GK + 8] = gk + pltpu.roll(gk, DH, 1)
        o_ref[RS_RPB:RS_RPB + H * N_DR] = rows("rpb")
        o_ref[RS_CONV_B:RS_CONV_B + 8] = rows("conv_b")
        o_ref[RS_CONV_W:RS_CONV_W + 16] = rows("conv_w")
        dmc = rows("dmc")
        o_ref[RS_DMC:RS_DMC + 24] = dmc
        o_ref[RS_LOSS:RS_LOSS + 8] = rows("loss")
        for i in range(24):
            dm_ref[8:9, 128 * i:128 * (i + 1)] = dmc[i:i + 1]

    return _hbm_call(body, name="small_reduce", out_shape=(SDS((RS_ROWS, 128), f32), SDS((16, 3 * D), f32)),
                     in_specs=[VMEM_SPEC], out_specs=(VMEM_SPEC, VMEM_SPEC))(gathered)


def _w_ada_grad(sc16, dm16, w_ada_shard, jvec):
    ncol = w_ada_shard.shape[1]

    def body(j_ref, sc_ref, dm_ref, w_ref, g_ref, part_ref):
        dm = dm_ref[...]
        g_ref[...] = lax.dot_general(sc_ref[...], dm, (((0,), (0,)), ((), ())), precision=HIGHEST,
                                     preferred_element_type=f32)
        part_ref[...] = lax.dot_general(dm[8:16], w_ref[...], (((1,), (1,)), ((), ())), precision=HIGHEST,
                                        preferred_element_type=f32)

    fixed = lambda i, j: (0, 0)
    grid_spec = pltpu.PrefetchScalarGridSpec(
        num_scalar_prefetch=1, grid=(1,),
        in_specs=[pl.BlockSpec((16, D), fixed), pl.BlockSpec((16, ncol), lambda i, j: (0, j[0])),
                  pl.BlockSpec((D, ncol), fixed)],
        out_specs=(pl.BlockSpec((D, ncol), fixed), pl.BlockSpec((8, D), fixed)))
    return _pallas_call(body, name="w_ada_grad", out_shape=(SDS((D, ncol), f32), SDS((8, D), f32)),
                        grid_spec=grid_spec, compiler_params=_cp(40))(jvec, sc16, dm16, w_ada_shard)


def _c_ctx_grad(parts4, c_ctx_row):
    def body(p_ref, c_ref, o_ref):
        tot = ((p_ref[0] + p_ref[1]) + p_ref[2]) + p_ref[3]
        o_ref[...] = tot[0:1] * _dsilu(c_ref[...])

    return _hbm_call(body, name="c_ctx_grad", out_shape=SDS((1, D), f32), in_specs=[VMEM_SPEC, VMEM_SPEC],
                          out_specs=VMEM_SPEC)(parts4, c_ctx_row)


def _adamw(w, g, m, v, name):
    rows, cols = w.shape
    tr = 256 if rows % 256 == 0 else rows

    def body(w_ref, g_ref, m_ref, v_ref, d_ref, m2_ref, v2_ref):
        gv = g_ref[...]
        m2 = ADAM_B1 * m_ref[...] + (1.0 - ADAM_B1) * gv
        v2 = ADAM_B2 * v_ref[...] + (1.0 - ADAM_B2) * jnp.square(gv)
        m_hat = m2 / (1.0 - ADAM_B1 ** ADAM_STEP)
        v_hat = v2 / (1.0 - ADAM_B2 ** ADAM_STEP)
        d_ref[...] = -ADAM_LR * (m_hat / (jnp.sqrt(v_hat) + ADAM_EPS) + ADAM_WD * w_ref[...])
        m2_ref[...] = m2
        v2_ref[...] = v2

    spec = pl.BlockSpec((tr, cols), lambda i: (i, 0))
    shp = SDS((rows, cols), f32)
    return _hbm_call(body, name=name, out_shape=(shp, shp, shp), grid=(rows // tr,), in_specs=[spec] * 4,
                          out_specs=(spec, spec, spec))(w, g, m, v)


def _adamw_halves(w, g_mine, g_other, m, v, cvec, name):
    rows, cols = w.shape
    half = rows // 2
    tr = min(256, half)
    per_half = half // tr

    def body(c_ref, w_ref, ga_ref, gb_ref, m_ref, v_ref, g_ref, d_ref, m2_ref, v2_ref):
        in_my_half = (pl.program_id(0) // per_half) == c_ref[0]
        gv = jnp.where(in_my_half, ga_ref[...], gb_ref[...])
        g_ref[...] = gv
        m2 = ADAM_B1 * m_ref[...] + (1.0 - ADAM_B1) * gv
        v2 = ADAM_B2 * v_ref[...] + (1.0 - ADAM_B2) * jnp.square(gv)
        m_hat = m2 / (1.0 - ADAM_B1 ** ADAM_STEP)
        v_hat = v2 / (1.0 - ADAM_B2 ** ADAM_STEP)
        d_ref[...] = -ADAM_LR * (m_hat / (jnp.sqrt(v_hat) + ADAM_EPS) + ADAM_WD * w_ref[...])
        m2_ref[...] = m2
        v2_ref[...] = v2

    full = pl.BlockSpec((tr, cols), lambda i, c: (i, 0))
    part = pl.BlockSpec((tr, cols), lambda i, c: (i % per_half, 0))
    shp = SDS((rows, cols), f32)
    grid_spec = pltpu.PrefetchScalarGridSpec(num_scalar_prefetch=1, grid=(rows // tr,),
                                             in_specs=[full, part, part, full, full], out_specs=(full,) * 4)
    return _hbm_call(body, name=name, out_shape=(shp,) * 4, grid_spec=grid_spec)(cvec, w, g_mine, g_other, m, v)


def _adam_math(w, g, m, v):
    m2 = ADAM_B1 * m + (1.0 - ADAM_B1) * g
    v2 = ADAM_B2 * v + (1.0 - ADAM_B2) * jnp.square(g)
    m_hat = m2 / (1.0 - ADAM_B1 ** ADAM_STEP)
    v_hat = v2 / (1.0 - ADAM_B2 ** ADAM_STEP)
    return -ADAM_LR * (m_hat / (jnp.sqrt(v_hat) + ADAM_EPS) + ADAM_WD * w), m2, v2


def _adamw_small(red, g_c_ctx, jvec, ws, ms, vs):
    n = len(ws)

    def body(*refs):
        red_ref, gc_ref, j_ref = refs[:3]
        w_refs, m_refs, v_refs = refs[3:3 + n], refs[3 + n:3 + 2 * n], refs[3 + 2 * n:3 + 3 * n]
        outs = refs[3 + 3 * n:]
        g_out, d_out, m_out, v_out = outs[:n], outs[n:2 * n], outs[2 * n:3 * n], outs[3 * n:]
        chip = j_ref[0]
        lanes = lambda i: (slice(None), slice(128 * i, 128 * (i + 1)))
        row = lambda r0, i: (lambda: red_ref[r0 + i:r0 + i + 1, :])
        whole = (slice(None), slice(None))
        chunks = [
            [(whole, lambda: gc_ref[...])],
            [(lanes(i), row(RS_B_ADA, i)) for i in range(3 * D // 128)],
            [(lanes(i), row(RS_NORM_G, i)) for i in range(D // 128)],
            [(whole, lambda: red_ref[RS_GQ:RS_GQ + 1, 0:DH])],
            [(whole, lambda: red_ref[RS_GK:RS_GK + 1, 0:DH])],
            [((0, h), (lambda h=h: red_ref[RS_RPB + N_DR * h:RS_RPB + N_DR * (h + 1), 0:N_DC])) for h in range(H)],
            [((0, slice(r, r + 1), slice(None)), (lambda r=r: red_ref[pl.ds(RS_CONV_W + 4 * r + chip, 1), :]))
             for r in range(3)],
            [(lanes(i), row(RS_CONV_B, i)) for i in range(DC // 128)],
        ]
        for a in range(n):
            for idx, grad in chunks[a]:
                g = grad()
                d, m2, v2 = _adam_math(w_refs[a][idx], g, m_refs[a][idx], v_refs[a][idx])
                g_out[a][idx] = g
                d_out[a][idx] = d
                m_out[a][idx] = m2
                v_out[a][idx] = v2

    shapes = [SDS(w.shape, f32) for w in ws]
    res = _pallas_call(body, name="adamw_small", out_shape=shapes * 4,
                       in_specs=[VMEM_SPEC, VMEM_SPEC, SMEM_SPEC] + [VMEM_SPEC] * (3 * n),
                       out_specs=[VMEM_SPEC] * (4 * n))(red, g_c_ctx, jvec, *ws, *ms, *vs)
    return [list(res[k * n:(k + 1) * n]) for k in range(4)]


def _rows128(a):
    return a.reshape(-1, 128)


def _pad_lanes(a):
    a2 = a.reshape(-1, a.shape[-1])
    return jnp.pad(a2, ((0, 0), (0, 128 - a2.shape[1])))


def kernel(x, c, ctx, c_ctx, w_ada, b_ada, norm_g, w_in, q_norm_g, k_norm_g, rpb, conv_w, conv_b, w_out, loss_target, m_c_ctx, m_w_ada, m_b_ada, m_norm_g, m_w_in, m_q_norm_g, m_k_norm_g, m_rpb, m_conv_w, m_conv_b, m_w_out, v_c_ctx, v_w_ada, v_b_ada, v_norm_g, v_w_in, v_q_norm_g, v_k_norm_g, v_rpb, v_conv_w, v_conv_b, v_w_out):
    xi, yi, ci = lax.axis_index("x"), lax.axis_index("y"), lax.axis_index("c")
    dev = 4 * xi + 2 * yi + ci
    chip = 2 * xi + yi
    cvec = jnp.reshape(ci, (1,)).astype(i32)
    jvec = jnp.reshape(chip, (1,)).astype(i32)
    w_ada_s = w_ada[0]
    ncol = w_ada_s.shape[1]

    c8 = _all_gather8(c.reshape(8, 128), "gather_c").reshape(8, D)
    cc = jnp.concatenate([c8, c_ctx.reshape(1, D), jnp.zeros((7, D), f32)], axis=0)
    m_shard, sc16 = _adaln_shard(cc, w_ada_s)

    conv_w_pad = jnp.pad(conv_w[0], ((0, 5), (0, 0)))
    m4, cw4 = _chip_gather([m_shard, conv_w_pad], "gather_mod")

    order = [(0, 0, 0), (0, 0, 1), (0, 0, 2), (1, 1, 0), (1, 1, 1), (1, 1, 2)]
    ssem, rsem, w4s, wo4s, token = _halves_start(
        [_cast_to_slot(w_in[0], jvec, "cast_w_in"), _cast_to_slot(w_out[0], jvec, "cast_w_out")], m4, order,
        "weights_ici_start")
    m_full = jnp.transpose(m4, (1, 0, 2)).reshape(16, 4 * ncol) + token[0:1, 0:1]
    mrow = lax.dynamic_slice(m_full, (dev, 0), (1, 3 * D))
    mrow_c = m_full[8:9]
    conv_w_full = jnp.transpose(cw4[:, 0:3, :], (1, 0, 2)).reshape(3, DC)

    def near(after):
        (w4w,) = _halves_wait(ssem, rsem, [w4s], after, [(0, 0, 0), (0, 0, 1)], "weights_near_wait")
        return _halves_forward([w4w], [0, 1], "weights_near_forward")[0]

    def far(w4, after):
        (w4w,) = _halves_wait(ssem, rsem, [w4], after, [(0, 0, 2)], "weights_far_wait")
        return _halves_forward([w4w], [2], "weights_far_forward")[0]

    def w_out_gathered(after):
        (wow,) = _halves_wait(ssem, rsem, [wo4s], after, [(0, 1, 0), (0, 1, 1), (0, 1, 2)], "weights_out_wait")
        return _halves_forward([wow], [0, 1, 2], "weights_out_forward")[0].reshape(D, D)

    weights = dict(own=w_in[0], jvec=jvec, near=near, far=far, out=w_out_gathered)

    exchange = _exchange_copies
    pending = {}

    def mid(g_w_in, g_w_out, b_ada_in):
        shapes = [SDS((4, D // 2, D), f32), SDS((4, D // 8, D), f32)]
        out = _split_start([g_w_in, g_w_out], shapes, 8, exchange, "grad_pair_start")
        pending["ex"] = (out[0], out[1], list(out[2:4]), list(out[4:6]))
        return b_ada_in + out[6][0:1, 0:1]

    r = _local_step(x[0], ctx[0], loss_target[0], mrow, mrow_c, b_ada, norm_g, weights, q_norm_g, k_norm_g,
                    rpb[0], conv_w_full, conv_b, mid)
    dm = jnp.concatenate([r["dshift"], r["dscale"], r["dgate"]], axis=1)
    dmc = jnp.concatenate([r["dshift_c"], r["dscale_c"], jnp.zeros((1, D), f32)], axis=1)
    pack_parts = [_rows128(dm), _rows128(dmc), _rows128(r["dng"]), _rows128(r["dng_c"]), _rows128(r["g_gq"]),
                  _rows128(r["g_gk"]), _rows128(r["g_gk_c"]), r["g_rpb"].reshape(H * N_DR, 128),
                  _rows128(r["g_conv_b"]), _rows128(r["g_conv_w"][0:3]), jnp.pad(r["loss_sum"], ((0, 0), (0, 127)))]
    pack = jnp.concatenate([jnp.pad(p, ((0, -p.shape[0] % 8), (0, 0))) for p in pack_parts], axis=0)
    gathered = _all_gather8(pack, "gather_small")

    ex_ssem, ex_rsem, ex_srcs, ex_lands = pending["ex"]
    ex_g, ex = _split_wait(ex_ssem, ex_rsem, ex_srcs, ex_lands, gathered, exchange, "grad_pair_wait")
    t32, tb = _pair_sum_w_in(ex_g[0], ex[0], cvec)
    to32, tob = _pair_sum_w_out(ex_g[1], ex[1], cvec)
    sc_out = _split_start([tb, tob], [SDS((3, D // 2, D), bf16), SDS((3, D // 8, D), bf16)], 6, _scatter_copies,
                          "grad_chip_start")
    sc16 = sc16 + sc_out[6][0:1, 0:1]

    assert pack.shape[0] == PK_ROWS
    red, dm16 = _small_reduce(gathered)
    loss = red[RS_LOSS, 0] * (0.5 / D)

    g_w_ada_s, cpart = _w_ada_grad(sc16, dm16, w_ada_s, jvec)
    d_w_ada, nm_w_ada, nv_w_ada = _adamw(w_ada_s, g_w_ada_s, m_w_ada[0], v_w_ada[0], "adamw_w_ada")

    _, (r2, ro2) = _split_wait(sc_out[0], sc_out[1], [sc_out[2], sc_out[3]], [sc_out[4], sc_out[5]], nm_w_ada,
                               _scatter_copies, "grad_chip_wait")
    u_in = _chip_sum(t32, r2, jvec, "chip_sum_w_in")
    u_out = _chip_sum(to32, ro2, jvec, "chip_sum_w_out")

    (cparts4,) = _chip_gather([cpart], "gather_c_ctx_parts", after=u_out)
    g_c_ctx = _c_ctx_grad(cparts4, c_ctx.reshape(1, D))

    as_row = lambda a: a.reshape(1, D)
    small = _adamw_small(
        red, g_c_ctx, jvec,
        [as_row(c_ctx), b_ada, norm_g, q_norm_g, k_norm_g, rpb, conv_w, conv_b],
        [as_row(m_c_ctx), m_b_ada, m_norm_g, m_q_norm_g, m_k_norm_g, m_rpb, m_conv_w, m_conv_b],
        [as_row(v_c_ctx), v_b_ada, v_norm_g, v_q_norm_g, v_k_norm_g, v_rpb, v_conv_w, v_conv_b])

    o_in, o_out = _sibling_send([u_in, u_out], "grad_pair_send")
    g_w_in_s, d_w_in, nm_w_in, nv_w_in = _adamw_halves(w_in[0], u_in, o_in, m_w_in[0], v_w_in[0], cvec, "adamw_w_in")
    g_w_out_s, d_w_out, nm_w_out, nv_w_out = _adamw_halves(w_out[0], u_out, o_out, m_w_out[0], v_w_out[0], cvec,
                                                           "adamw_w_out")

    def ordered(kind, big_w_ada, big_w_in, big_w_out):
        s_c_ctx, s_b_ada, s_norm_g, s_q, s_k, s_rpb, s_conv_w, s_conv_b = small[kind]
        return [s_c_ctx.reshape(D), big_w_ada[None], s_b_ada, s_norm_g, big_w_in[None], s_q, s_k, s_rpb, s_conv_w,
                s_conv_b, big_w_out[None]]

    grads = ordered(0, g_w_ada_s, g_w_in_s, g_w_out_s)
    deltas = ordered(1, d_w_ada, d_w_in, d_w_out)
    new_m = ordered(2, nm_w_ada, nm_w_in, nm_w_out)
    new_v = ordered(3, nv_w_ada, nv_w_in, nv_w_out)
    return (loss, r["grad_x"][None], *grads, *deltas, *new_m, *new_v)
```

```python
import functools

import jax
import jax.numpy as jnp
import numpy as np
from jax import lax
from jax.experimental import pallas as pl
from jax.experimental.pallas import tpu as pltpu

f32, bf16, i32 = jnp.float32, jnp.bfloat16, jnp.int32
MESH = pl.DeviceIdType.MESH
HIGHEST = lax.Precision.HIGHEST

D = 1024
S = 2048
L = 256
GW = 64
ROWS = S // GW
H = 8
DH = 64
DA = H * DH
DC = 512
WIN_H, WIN_W = 8, 16
N_DR, N_DC = 2 * WIN_H - 1, 2 * WIN_W - 1
RMS_EPS = 1e-6
ROPE_THETA = 10000.0
QK_SCALE = DH ** -0.5
NEG = -1e30

QB = 128
NQB = S // QB
KR = 9
KB = KR * GW
TILE_GEOM = ((0, 0), (2, 0), (4, 0), (28, 23), (30, 23))
NT = len(TILE_GEOM)

ADAM_LR, ADAM_B1, ADAM_B2, ADAM_EPS, ADAM_WD, ADAM_STEP = 0.001, 0.9, 0.999, 1e-08, 0.01, 10

VMEM_SPEC = pl.BlockSpec(memory_space=pltpu.VMEM)
ANY_SPEC = pl.BlockSpec(memory_space=pl.ANY)
SMEM_SPEC = pl.BlockSpec(memory_space=pltpu.SMEM)
SDS = jax.ShapeDtypeStruct


_pallas_call = pl.pallas_call


def _hbm_call(body, *, out_shape, in_specs=None, out_specs=None, grid_spec=None, **kw):
    n_pre = 0
    if grid_spec is not None:
        ispecs, ospecs, n_pre = grid_spec.in_specs, grid_spec.out_specs, grid_spec.num_scalar_prefetch
        kw["grid_spec"] = grid_spec
    else:
        ispecs, ospecs = in_specs, out_specs
        kw.update(in_specs=in_specs, out_specs=out_specs)

    def blocked(spec):
        return isinstance(spec, pl.BlockSpec) and spec.block_shape is not None

    single = not isinstance(out_shape, (tuple, list))
    shapes = [out_shape] if single else list(out_shape)
    ospec_list = list(ospecs) if isinstance(ospecs, (tuple, list)) else [ospecs]
    shapes = [pltpu.HBM(s.shape, s.dtype) if blocked(sp) else s for s, sp in zip(shapes, ospec_list)]
    call = _pallas_call(body, out_shape=shapes[0] if single else tuple(shapes), **kw)

    def run(*args):
        arrays = [pltpu.with_memory_space_constraint(a, pltpu.HBM) if blocked(sp) else a
                  for a, sp in zip(args[n_pre:], ispecs)]
        return call(*args[:n_pre], *arrays)

    return run


def _cp(vmem_mb=None, **kw):
    if vmem_mb is not None:
        kw["vmem_limit_bytes"] = vmem_mb << 20
    return pltpu.CompilerParams(**kw)


def _silu(z):
    return z * jax.nn.sigmoid(z)


def _dsilu(z):
    sg = jax.nn.sigmoid(z)
    return sg * (1.0 + z * (1.0 - sg))


def _row_start(i):
    return min(max(i - WIN_H // 2, 0), ROWS - WIN_H)


def _my_pos():
    return lax.axis_index("x"), lax.axis_index("y"), lax.axis_index("c")


def _flip(v, bit):
    return 1 - v if bit else v


def _all_gather8(xin, name):
    R, N = xin.shape

    def body(x_ref, o_ref, ssem, rsem, lsem):
        x, y, c = _my_pos()
        me = 4 * x + 2 * y + c
        own = pltpu.make_async_copy(x_ref, o_ref.at[me], lsem)
        own.start()
        sends = []
        for k in range(1, 8):
            tgt = (_flip(x, (k >> 2) & 1), _flip(y, (k >> 1) & 1), _flip(c, k & 1))
            cp = pltpu.make_async_remote_copy(src_ref=x_ref, dst_ref=o_ref.at[me], send_sem=ssem.at[k - 1],
                                              recv_sem=rsem.at[k - 1], device_id=tgt, device_id_type=MESH)
            cp.start()
            sends.append(cp)
        for k in range(1, 8):
            tgt = (_flip(x, (k >> 2) & 1), _flip(y, (k >> 1) & 1), _flip(c, k & 1))
            peer = 4 * tgt[0] + 2 * tgt[1] + tgt[2]
            pltpu.make_async_remote_copy(src_ref=x_ref, dst_ref=o_ref.at[peer], send_sem=ssem.at[k - 1],
                                         recv_sem=rsem.at[k - 1], device_id=tgt, device_id_type=MESH).wait_recv()
        for cp in sends:
            cp.wait_send()
        own.wait()

    return _hbm_call(
        body, name=name, out_shape=SDS((8, R, N), xin.dtype), in_specs=[VMEM_SPEC], out_specs=VMEM_SPEC,
        scratch_shapes=[pltpu.SemaphoreType.DMA((7,)), pltpu.SemaphoreType.DMA((7,)), pltpu.SemaphoreType.DMA],
    )(xin)


def _chip_gather(smalls, name, after=None):
    ns = len(smalls)

    def body(*refs):
        s_in, s_out = refs[:ns], refs[ns + 1:2 * ns + 1]
        ssem, rsem, lsem = refs[2 * ns + 1:]
        x, y, c = _my_pos()
        j = 2 * x + y
        chips = _peer_chips(x, y, c)
        local = [pltpu.make_async_copy(s_in[a], s_out[a].at[j], lsem.at[a]) for a in range(ns)]
        for cp in local:
            cp.start()
        sends = []
        for a in range(ns):
            for k in range(3):
                cp = pltpu.make_async_remote_copy(src_ref=s_in[a], dst_ref=s_out[a].at[j], send_sem=ssem.at[3 * a + k],
                                                  recv_sem=rsem.at[3 * a + k], device_id=chips[k][0], device_id_type=MESH)
                cp.start()
                sends.append(cp)
        for a in range(ns):
            for k in range(3):
                pltpu.make_async_remote_copy(src_ref=s_in[a], dst_ref=s_out[a].at[chips[k][1]], send_sem=ssem.at[3 * a + k],
                                             recv_sem=rsem.at[3 * a + k], device_id=chips[k][0],
                                             device_id_type=MESH).wait_recv()
        for cp in sends:
            cp.wait_send()
        for cp in local:
            cp.wait()

    return _hbm_call(
        body, name=name, out_shape=[SDS((4,) + a.shape, a.dtype) for a in smalls],
        in_specs=[VMEM_SPEC] * ns + [ANY_SPEC], out_specs=[VMEM_SPEC] * ns,
        scratch_shapes=[pltpu.SemaphoreType.DMA((3 * ns,)), pltpu.SemaphoreType.DMA((3 * ns,)),
                        pltpu.SemaphoreType.DMA((ns,))],
    )(*smalls, smalls[0] if after is None else after)


HBM_SPEC = pl.BlockSpec(memory_space=pltpu.HBM)
SEM_SPEC = pl.BlockSpec(memory_space=pltpu.SEMAPHORE)
DATAFLOW = pltpu.SideEffectType.DATAFLOW_SIDE_EFFECTING


def _peer_chips(x, y, c):
    out = []
    for k in range(1, 4):
        px, py = _flip(x, (k >> 1) & 1), _flip(y, k & 1)
        out.append(((px, py, c), 2 * px + py))
    return out


def _half_copies(srcs, dsts, ssem, rsem, which):
    x, y, c = _my_pos()
    j = 2 * x + y
    peers = _peer_chips(x, y, c)
    pairs = []
    for pos, group, k in which:
        half = srcs[pos].shape[1] // 2
        mine = pl.ds(pl.multiple_of(c * half, 8), half)
        dev, pj = peers[k]
        sem = 3 * group + k
        send = pltpu.make_async_remote_copy(src_ref=srcs[pos].at[j, mine], dst_ref=dsts[pos].at[j, mine],
                                            send_sem=ssem.at[sem], recv_sem=rsem.at[sem], device_id=dev,
                                            device_id_type=MESH)
        arrive = pltpu.make_async_remote_copy(src_ref=srcs[pos].at[j, mine], dst_ref=dsts[pos].at[pj, mine],
                                              send_sem=ssem.at[sem], recv_sem=rsem.at[sem], device_id=dev,
                                              device_id_type=MESH)
        pairs.append((send, arrive))
    return pairs


def _halves_start(bigs, after, order, name):
    nb = len(bigs)

    def body(*refs):
        b_in = refs[:nb]
        ssem, rsem = refs[nb + 1], refs[nb + 2]
        b_out = refs[nb + 3:2 * nb + 3]
        token = refs[2 * nb + 3]
        for send, _ in _half_copies(b_in, b_out, ssem, rsem, order):
            send.start()
        token[...] = jnp.zeros_like(token)

    out_shape = (pltpu.SemaphoreType.DMA((3 * nb,)), pltpu.SemaphoreType.DMA((3 * nb,)),
                 *[pltpu.HBM(b.shape, b.dtype) for b in bigs], SDS((8, 128), f32))
    return _hbm_call(
        body, name=name, out_shape=out_shape, in_specs=[HBM_SPEC] * nb + [ANY_SPEC],
        out_specs=(SEM_SPEC, SEM_SPEC, *[HBM_SPEC] * nb, VMEM_SPEC),
        input_output_aliases={a: 2 + a for a in range(nb)}, compiler_params=_cp(has_side_effects=DATAFLOW),
    )(*[pltpu.with_memory_space_constraint(b, pltpu.HBM) for b in bigs], after)


def _halves_wait(ssem, rsem, bigs, after, which, name):
    nb = len(bigs)

    def body(*refs):
        b_in = refs[:nb]
        ssem_ref, rsem_ref = refs[nb], refs[nb + 1]
        for send, arrive in _half_copies(b_in, b_in, ssem_ref, rsem_ref, which):
            send.wait_send()
            arrive.wait_recv()

    return _hbm_call(
        body, name=name, out_shape=tuple(pltpu.HBM(b.shape, b.dtype) for b in bigs),
        in_specs=[HBM_SPEC] * nb + [SEM_SPEC, SEM_SPEC, ANY_SPEC], out_specs=tuple([HBM_SPEC] * nb),
        input_output_aliases={a: a for a in range(nb)}, compiler_params=_cp(has_side_effects=DATAFLOW),
    )(*bigs, ssem, rsem, after)


def _halves_forward(bigs, relations, name):
    nb, nr = len(bigs), len(relations)

    def body(*refs):
        b_in, b_out = refs[:nb], refs[nb:2 * nb]
        ssem, rsem = refs[2 * nb:]
        x, y, c = _my_pos()
        sib = (x, y, 1 - c)
        peers = _peer_chips(x, y, c)
        sends = []
        for a in range(nb):
            half = b_in[a].shape[1] // 2
            mine = pl.ds(pl.multiple_of(c * half, 8), half)
            for i, k in enumerate(relations):
                pj = peers[k][1]
                cp = pltpu.make_async_remote_copy(src_ref=b_in[a].at[pj, mine], dst_ref=b_out[a].at[pj, mine],
                                                  send_sem=ssem.at[nr * a + i], recv_sem=rsem.at[nr * a + i],
                                                  device_id=sib, device_id_type=MESH)
                cp.start()
                sends.append(cp)
        for a in range(nb):
            half = b_in[a].shape[1] // 2
            other = pl.ds(pl.multiple_of((1 - c) * half, 8), half)
            for i, k in enumerate(relations):
                pj = peers[k][1]
                pltpu.make_async_remote_copy(src_ref=b_in[a].at[pj, other], dst_ref=b_out[a].at[pj, other],
                                             send_sem=ssem.at[nr * a + i], recv_sem=rsem.at[nr * a + i],
                                             device_id=sib, device_id_type=MESH).wait_recv()
        for cp in sends:
            cp.wait_send()

    return _hbm_call(
        body, name=name, out_shape=[SDS(b.shape, b.dtype) for b in bigs], in_specs=[ANY_SPEC] * nb,
        out_specs=[ANY_SPEC] * nb, input_output_aliases={a: a for a in range(nb)},
        scratch_shapes=[pltpu.SemaphoreType.DMA((nr * nb,)), pltpu.SemaphoreType.DMA((nr * nb,))],
    )(*bigs)


def _cast_to_slot(w, jvec, name):
    rows, cols = w.shape
    tr = 256

    def body(j_ref, w_ref, o_ref):
        o_ref[...] = w_ref[...].astype(bf16)

    grid_spec = pltpu.PrefetchScalarGridSpec(
        num_scalar_prefetch=1, grid=(rows // tr,), in_specs=[pl.BlockSpec((tr, cols), lambda i, j: (i, 0))],
        out_specs=pl.BlockSpec((None, tr, cols), lambda i, j: (j[0], i, 0)))
    return _hbm_call(body, name=name, out_shape=SDS((4, rows, cols), bf16), grid_spec=grid_spec)(jvec, w)


def _exchange_copies(srcs, lands, ssem, rsem):
    x, y, c = _my_pos()
    cps = []
    for a in range(len(srcs)):
        stacked = len(srcs[a].shape) == 3
        rb = srcs[a].shape[1] if stacked else srcs[a].shape[0] // 4
        half = rb // 2
        for jb in range(4):
            if stacked:
                src = srcs[a].at[jb, pl.ds(pl.multiple_of((1 - c) * half, 8), half)]
            else:
                src = srcs[a].at[pl.ds(pl.multiple_of(jb * rb + (1 - c) * half, 8), half)]
            cps.append(pltpu.make_async_remote_copy(src_ref=src, dst_ref=lands[a].at[jb], send_sem=ssem.at[4 * a + jb],
                                                    recv_sem=rsem.at[4 * a + jb], device_id=(x, y, 1 - c),
                                                    device_id_type=MESH))
    return cps


def _scatter_copies(srcs, lands, ssem, rsem):
    x, y, c = _my_pos()
    cps = []
    for a in range(len(srcs)):
        for k, (dev, pj) in enumerate(_peer_chips(x, y, c)):
            cps.append(pltpu.make_async_remote_copy(src_ref=srcs[a].at[pj], dst_ref=lands[a].at[k],
                                                    send_sem=ssem.at[3 * a + k], recv_sem=rsem.at[3 * a + k],
                                                    device_id=dev, device_id_type=MESH))
    return cps


def _split_start(srcs, land_shapes, n_cp, make, name):
    ns, nl = len(srcs), len(land_shapes)

    def body(*refs):
        s_in = refs[:ns]
        ssem, rsem = refs[ns + nl], refs[ns + nl + 1]
        l_out = refs[2 * ns + nl + 2:2 * ns + 2 * nl + 2]
        token = refs[2 * ns + 2 * nl + 2]
        for cp in make(s_in, l_out, ssem, rsem):
            cp.start()
        token[...] = jnp.zeros_like(token)

    lands = [pltpu.with_memory_space_constraint(lax.empty(sh.shape, sh.dtype), pltpu.HBM) for sh in land_shapes]
    out_shape = (pltpu.SemaphoreType.DMA((n_cp,)), pltpu.SemaphoreType.DMA((n_cp,)),
                 *[pltpu.HBM(b.shape, b.dtype) for b in srcs], *[pltpu.HBM(b.shape, b.dtype) for b in land_shapes],
                 SDS((8, 128), f32))
    return _hbm_call(
        body, name=name, out_shape=out_shape, in_specs=[HBM_SPEC] * (ns + nl),
        out_specs=(SEM_SPEC, SEM_SPEC, *[HBM_SPEC] * (ns + nl), VMEM_SPEC),
        input_output_aliases={i: 2 + i for i in range(ns + nl)}, compiler_params=_cp(has_side_effects=DATAFLOW),
    )(*[pltpu.with_memory_space_constraint(b, pltpu.HBM) for b in srcs], *lands)


def _split_wait(ssem, rsem, srcs, lands, after, make, name):
    ns, nl = len(srcs), len(lands)

    def body(*refs):
        s_in, l_in = refs[:ns], refs[ns:ns + nl]
        ssem_ref, rsem_ref = refs[ns + nl], refs[ns + nl + 1]
        for cp in make(s_in, l_in, ssem_ref, rsem_ref):
            cp.wait_send()
            cp.wait_recv()

    outs = _hbm_call(
        body, name=name, out_shape=tuple(pltpu.HBM(b.shape, b.dtype) for b in (*srcs, *lands)),
        in_specs=[HBM_SPEC] * (ns + nl) + [SEM_SPEC, SEM_SPEC, ANY_SPEC], out_specs=tuple([HBM_SPEC] * (ns + nl)),
        input_output_aliases={i: i for i in range(ns + nl)}, compiler_params=_cp(has_side_effects=DATAFLOW),
    )(*srcs, *lands, ssem, rsem, after)
    return list(outs[:ns]), list(outs[ns:])


def _sibling_send(halves, name):
    n = len(halves)

    def body(*refs):
        ins, outs = refs[:n], refs[n:2 * n]
        ssem, rsem = refs[2 * n:]
        x, y, c = _my_pos()
        cps = []
        for a in range(n):
            cp = pltpu.make_async_remote_copy(src_ref=ins[a], dst_ref=outs[a], send_sem=ssem.at[a],
                                              recv_sem=rsem.at[a], device_id=(x, y, 1 - c), device_id_type=MESH)
            cp.start()
            cps.append(cp)
        for cp in cps:
            cp.wait_recv()
        for cp in cps:
            cp.wait_send()

    out_shape = [SDS(h.shape, h.dtype) for h in halves]
    return _hbm_call(
        body, name=name, out_shape=out_shape, in_specs=[ANY_SPEC] * n, out_specs=[ANY_SPEC] * n,
        scratch_shapes=[pltpu.SemaphoreType.DMA((n,)), pltpu.SemaphoreType.DMA((n,))],
    )(*halves)


def _adaln_shard(cc, w_ada_shard):
    def body(c_ref, w_ref, m_ref, sc_ref):
        sc = _silu(c_ref[...])
        sc_ref[...] = sc
        m_ref[...] = jnp.dot(sc, w_ref[...], precision=HIGHEST, preferred_element_type=f32)

    return _hbm_call(
        body, name="adaln_shard", out_shape=(SDS((16, w_ada_shard.shape[1]), f32), SDS((16, D), f32)),
        in_specs=[VMEM_SPEC, VMEM_SPEC], out_specs=(VMEM_SPEC, VMEM_SPEC), compiler_params=_cp(32),
    )(cc, w_ada_shard)


def _prenorm(xx, norm_g, mrow, b_ada, tm, name):
    n = xx.shape[0]

    def body(x_ref, g_ref, m_ref, b_ref, h_ref):
        x = x_ref[...]
        shift = m_ref[:, 0:D] + b_ref[:, 0:D]
        scale = m_ref[:, D:2 * D] + b_ref[:, D:2 * D]
        r = lax.rsqrt(jnp.mean(x * x, axis=-1, keepdims=True) + RMS_EPS)
        y = (x * r) * g_ref[...]
        h_ref[...] = (y * (1.0 + scale) + shift).astype(bf16)

    row = lambda i: (i, 0)
    fixed = lambda i: (0, 0)
    return _hbm_call(
        body, name=name, out_shape=SDS((n, D), bf16), grid=(n // tm,),
        in_specs=[pl.BlockSpec((tm, D), row), pl.BlockSpec((1, D), fixed), pl.BlockSpec((1, 3 * D), fixed),
                  pl.BlockSpec((1, 3 * D), fixed)],
        out_specs=pl.BlockSpec((tm, D), row),
    )(xx, norm_g, mrow, b_ada)


def _in_proj_own(h, w_own, jvec):
    tm = 512

    def body(j_ref, h_ref, w_ref, p_ref):
        p_ref[...] = jnp.dot(h_ref[...], w_ref[...].astype(bf16), preferred_element_type=f32)

    grid_spec = pltpu.PrefetchScalarGridSpec(
        num_scalar_prefetch=1, grid=(S // tm,),
        in_specs=[pl.BlockSpec((tm, D), lambda i, j: (i, 0)), pl.BlockSpec((D, D), lambda i, j: (0, 0))],
        out_specs=pl.BlockSpec((tm, D), lambda i, j: (i, j[0])))
    return _hbm_call(body, name="in_proj_own", out_shape=SDS((S, 4 * D), f32), grid_spec=grid_spec,
                     compiler_params=_cp(40))(jvec, h, w_own)


def _in_proj_block(h, w4, p, bvec, name):
    tm = 512

    def body(b_ref, h_ref, w_ref, p_in_ref, p_ref):
        p_ref[...] = jnp.dot(h_ref[...], w_ref[...], preferred_element_type=f32)

    grid_spec = pltpu.PrefetchScalarGridSpec(
        num_scalar_prefetch=1, grid=(S // tm,),
        in_specs=[pl.BlockSpec((tm, D), lambda i, b: (i, 0)), pl.BlockSpec((None, D, D), lambda i, b: (b[0], 0, 0)),
                  ANY_SPEC],
        out_specs=pl.BlockSpec((tm, D), lambda i, b: (i, b[0])))
    return _hbm_call(body, name=name, out_shape=SDS((S, 4 * D), f32), grid_spec=grid_spec,
                     input_output_aliases={3: 0})(bvec, h, w4, p)


def _ctx_proj(hc, w4):
    def body(h_ref, w0_ref, w1_ref, p_ref):
        hv = h_ref[...]
        p_ref[:, 0:DA] = jnp.dot(hv, w0_ref[:, DA:2 * DA], preferred_element_type=f32)
        p_ref[:, DA:2 * DA] = jnp.dot(hv, w1_ref[:, 0:DA], preferred_element_type=f32)

    return _hbm_call(
        body, name="ctx_proj", out_shape=SDS((L, 2 * DA), f32), grid=(1,),
        in_specs=[pl.BlockSpec((L, D), lambda i: (0, 0)), pl.BlockSpec((None, D, D), lambda i: (0, 0, 0)),
                  pl.BlockSpec((None, D, D), lambda i: (1, 0, 0))],
        out_specs=pl.BlockSpec((L, 2 * DA), lambda i: (0, 0)),
    )(hc, w4, w4)


def _head_ones():
    r = lax.broadcasted_iota(i32, (DA, DA), 0) // DH
    c = lax.broadcasted_iota(i32, (DA, DA), 1) // DH
    return (r == c).astype(bf16)


def _head_sum(v, ones_bd):
    hi = v.astype(bf16)
    lo = (v - hi.astype(f32)).astype(bf16)
    return jnp.dot(hi, ones_bd, preferred_element_type=f32) + jnp.dot(lo, ones_bd, preferred_element_type=f32)


def _swap16(v):
    lane = lax.broadcasted_iota(i32, v.shape, 1)
    return jnp.where((lane & 31) < 16, pltpu.roll(v, DA - 16, 1), pltpu.roll(v, 16, 1))


def _rope_block(ct_ref, rt_ref, tm):
    rows = [jnp.tile(rt_ref[8 * j:8 * j + 8, :], (GW // 8, 1)) for j in range(tm // GW)]
    return jnp.tile(ct_ref[...], (tm // GW, 1)) + jnp.concatenate(rows, axis=0)


def _rope_specs(tm):
    col = pl.BlockSpec((GW, DA), lambda i: (0, 0))
    row = pl.BlockSpec((8 * tm // GW, DA), lambda i: (i, 0))
    return [col, row, col, row]


def _qk_prep(p, gq, gk, rope):
    tm = 256

    def body(qk_ref, v_ref, gq_ref, gk_ref, cc_ref, cr_ref, sc_ref, sr_ref, qr_ref, qp_ref, kr_ref, vh_ref):
        ones_bd = _head_ones()
        cs, sn = _rope_block(cc_ref, cr_ref, tm), _rope_block(sc_ref, sr_ref, tm)
        q = qk_ref[:, 0:DA]
        k = qk_ref[:, DA:2 * DA]
        yq = (q * lax.rsqrt(_head_sum(q * q, ones_bd) * (1.0 / DH) + RMS_EPS)) * gq_ref[...]
        yk = (k * lax.rsqrt(_head_sum(k * k, ones_bd) * (1.0 / DH) + RMS_EPS)) * gk_ref[...]
        qr = (yq * cs + _swap16(yq) * sn) * QK_SCALE
        qp = yq * QK_SCALE
        kr = yk * cs + _swap16(yk) * sn
        vv = v_ref[...]
        for hh in range(H):
            sl = slice(hh * DH, (hh + 1) * DH)
            qr_ref[hh] = qr[:, sl].astype(bf16)
            qp_ref[hh] = qp[:, sl].astype(bf16)
            kr_ref[hh] = kr[:, sl].astype(bf16)
            vh_ref[hh] = vv[:, sl].astype(bf16)

    hm = SDS((H, S, DH), bf16)
    hspec = pl.BlockSpec((H, tm, DH), lambda i: (0, i, 0))
    fixed = lambda i: (0, 0)
    return _hbm_call(
        body, name="qk_prep", out_shape=(hm, hm, hm, hm), grid=(S // tm,),
        in_specs=[pl.BlockSpec((tm, 2 * DA), lambda i: (i, 0)), pl.BlockSpec((tm, DA), lambda i: (i, 2)),
                  pl.BlockSpec((1, DA), fixed), pl.BlockSpec((1, DA), fixed)] + _rope_specs(tm),
        out_specs=(hspec, hspec, hspec, hspec),
    )(p, p, gq, gk, *rope)


def _ctx_prep(pc, gk):
    def body(p_ref, gk_ref, kc_ref, vc_ref):
        ones_bd = _head_ones()
        k = p_ref[:, 0:DA]
        yk = (k * lax.rsqrt(_head_sum(k * k, ones_bd) * (1.0 / DH) + RMS_EPS)) * gk_ref[...]
        vv = p_ref[:, DA:2 * DA]
        for hh in range(H):
            sl = slice(hh * DH, (hh + 1) * DH)
            kc_ref[hh] = yk[:, sl].astype(bf16)
            vc_ref[hh] = vv[:, sl].astype(bf16)

    hm = SDS((H, L, DH), bf16)
    return _hbm_call(
        body, name="ctx_prep", out_shape=(hm, hm), in_specs=[VMEM_SPEC, VMEM_SPEC], out_specs=(VMEM_SPEC, VMEM_SPEC),
    )(pc, gk)


def _tile_pieces():
    out = []
    for (i0, u0) in TILE_GEOM:
        rows = []
        for j in range(2):
            i = i0 + j
            rs = _row_start(i)
            rows.append([(u0 + u - i + WIN_H - 1) if rs <= u0 + u < rs + WIN_H else None for u in range(KR)])
        out.append(rows)
    return out


def _bias_prep(rpb_rev_pad):
    pieces = _tile_pieces()

    def body(r_ref, o_ref):
        rp = r_ref[...]
        xs = jnp.broadcast_to(rp[:, None, :], (N_DR, GW, 128)).reshape(N_DR * GW, 128)
        row = lax.broadcasted_iota(i32, xs.shape, 0)
        lane = lax.broadcasted_iota(i32, xs.shape, 1)
        for b in range(6):
            xs = jnp.where(((row >> b) & 1) == 1, pltpu.roll(xs, 1 << b, 1), xs)
        xs = pltpu.roll(xs, 128 - (WIN_W - 1), 1)
        k = row & (GW - 1)
        c0 = jnp.clip(lane - WIN_W // 2, 0, GW - WIN_W)
        xs = jnp.where((k >= c0) & (k < c0 + WIN_W), xs, NEG)
        neg = jnp.full((GW, GW), NEG, f32)
        for t in range(NT):
            for j in range(2):
                for u in range(KR):
                    dr = pieces[t][j][u]
                    piece = neg if dr is None else xs[dr * GW:(dr + 1) * GW, 0:GW]
                    o_ref[t, u * GW:(u + 1) * GW, j * GW:(j + 1) * GW] = piece

    return _hbm_call(
        body, name="bias_prep", out_shape=SDS((H, NT, KB, QB), f32), grid=(H,),
        in_specs=[pl.BlockSpec((None, N_DR, 128), lambda h: (h, 0, 0))],
        out_specs=pl.BlockSpec((None, NT, KB, QB), lambda h: (h, 0, 0, 0)),
    )(rpb_rev_pad)


def _block_geom(b):
    qs = b * QB
    ks = min(max(2 * b - 4, 0), ROWS - KR) * GW
    t = b if b < 2 else (b - (NQB - NT) if b > NQB - 3 else 2)
    return qs, ks, t


def _tt(a, b):
    return lax.dot_general(a, b, (((1,), (1,)), ((), ())), preferred_element_type=f32)


def _tn(a, b):
    return lax.dot_general(a, b, (((0,), (0,)), ((), ())), preferred_element_type=f32)


def _softmax_t(s_lat, s_ctx):
    m = jnp.maximum(jnp.max(s_lat, axis=0, keepdims=True), jnp.max(s_ctx, axis=0, keepdims=True))
    e_lat = jnp.exp(s_lat - m)
    e_ctx = jnp.exp(s_ctx - m)
    inv = 1.0 / (jnp.sum(e_lat, axis=0, keepdims=True) + jnp.sum(e_ctx, axis=0, keepdims=True))
    return e_lat * inv, e_ctx * inv


def _staged(n_blocks, stages):
    held = [dict() for _ in stages]
    for step in range(n_blocks + len(stages) - 1):
        for s, fn in enumerate(stages):
            b = step - s
            if 0 <= b < n_blocks:
                held[s][b] = fn(b) if s == 0 else fn(b, held[s - 1].pop(b))


def _attn_fwd(qr, qp, kr, vh, kc, vc, btt):
    def body(qr_ref, qp_ref, kr_ref, v_ref, kc_ref, vc_ref, bt_ref, o_ref):
        kcv, vcv = kc_ref[...], vc_ref[...]

        def scores(b):
            qs, ks, t = _block_geom(b)
            return (_tt(kr_ref[ks:ks + KB, :], qr_ref[qs:qs + QB, :]) + bt_ref[t], _tt(kcv, qp_ref[qs:qs + QB, :]))

        def probs(b, sc):
            p_lat, p_ctx = _softmax_t(*sc)
            return p_lat.astype(bf16), p_ctx.astype(bf16)

        def values(b, p):
            qs, ks, _ = _block_geom(b)
            o_ref[qs:qs + QB, :] = _tn(p[0], v_ref[ks:ks + KB, :]) + _tn(p[1], vcv)

        _staged(NQB, (scores, probs, values))

    sq = pl.BlockSpec((None, S, DH), lambda h: (h, 0, 0))
    sc = pl.BlockSpec((None, L, DH), lambda h: (h, 0, 0))
    return _hbm_call(
        body, name="attn_fwd", out_shape=SDS((H, S, DH), f32), grid=(H,),
        in_specs=[sq, sq, sq, sq, sc, sc, pl.BlockSpec((None, NT, KB, QB), lambda h: (h, 0, 0, 0))],
        out_specs=sq, compiler_params=_cp(48),
    )(qr, qp, kr, vh, kc, vc, btt)


def _attn_gate(o, p):
    tm = 256

    def body(o_ref, za_ref, a_ref):
        sz = _silu(za_ref[...])
        for hh in range(H):
            sl = slice(hh * DH, (hh + 1) * DH)
            a_ref[:, sl] = (o_ref[hh] * sz[:, sl]).astype(bf16)

    return _hbm_call(
        body, name="attn_gate", out_shape=SDS((S, DA), bf16), grid=(S // tm,),
        in_specs=[pl.BlockSpec((H, tm, DH), lambda i: (0, i, 0)), pl.BlockSpec((tm, DA), lambda i: (i, 3))],
        out_specs=pl.BlockSpec((tm, DA), lambda i: (i, 0)),
    )(o, p)


def _shift_rows(v, down):
    n = v.shape[0]
    row = lax.broadcasted_iota(i32, v.shape, 0)
    if down:
        return jnp.where(row == 0, 0.0, pltpu.roll(v, 1, 0))
    return jnp.where(row == n - 1, 0.0, pltpu.roll(v, n - 1, 0))


def _conv_specs():
    col = lambda off: pl.BlockSpec((S, 128), lambda i, off=off: (0, off + i))
    return [col(16), col(20), col(24), col(28), pl.BlockSpec((3, 128), lambda i: (0, i)),
            pl.BlockSpec((1, 128), lambda i: (0, i))]


def _conv_fwd(p, conv_w, conv_b):
    def body(u_ref, bg_ref, cg_ref, zc_ref, w_ref, b_ref, o_ref):
        cu = cg_ref[...] * u_ref[...]
        cv = b_ref[...] + _shift_rows(cu, True) * w_ref[0:1, :]
        cv = cv + cu * w_ref[1:2, :]
        cv = cv + _shift_rows(cu, False) * w_ref[2:3, :]
        o_ref[...] = ((bg_ref[...] * cv) * _silu(zc_ref[...])).astype(bf16)

    return _hbm_call(
        body, name="conv_fwd", out_shape=SDS((S, DC), bf16), grid=(DC // 128,),
        in_specs=_conv_specs(), out_specs=pl.BlockSpec((S, 128), lambda i: (0, i)), compiler_params=_cp(40),
    )(p, p, p, p, conv_w, conv_b)


def _out_proj_loss(attn_g, conv_g, w_out, xx, tgt, mrow, b_ada):
    tm = 256

    def body(a_ref, c_ref, w_ref, x_ref, t_ref, m_ref, b_ref, dy_ref, dmix_ref, gwo_ref, dgate_ref, loss_ref):
        k = pl.program_id(0)

        @pl.when(k == 0)
        def _():
            gwo_ref[...] = jnp.zeros_like(gwo_ref)
            dgate_ref[...] = jnp.zeros_like(dgate_ref)
            loss_ref[0, 0] = 0.0

        gate = m_ref[:, 2 * D:3 * D] + b_ref[:, 2 * D:3 * D]
        av, cv = a_ref[...], c_ref[...]
        mo = jnp.dot(av, w_ref[0:DA, :], preferred_element_type=f32)
        mo = mo + jnp.dot(cv, w_ref[DA:DA + DC, :], preferred_element_type=f32)
        y = x_ref[...] + gate * mo
        diff = y - t_ref[...]
        loss_ref[0, 0] += jnp.sum(diff * diff)
        dy = diff * (1.0 / D)
        dy_ref[...] = dy
        dgate_ref[...] += jnp.sum(dy * mo, axis=0, keepdims=True)
        dmo = (dy * gate).astype(bf16)
        dmix_ref[...] = _tt(dmo, w_ref[...])
        gwo_ref[0:DA, :] += _tn(av, dmo)
        gwo_ref[DA:DA + DC, :] += _tn(cv, dmo)

    row = lambda i: (i, 0)
    fixed = lambda i: (0, 0)
    return _hbm_call(
        body, name="out_proj_loss",
        out_shape=(SDS((S, D), f32), SDS((S, D), f32), SDS((D, D), f32), SDS((1, D), f32), SDS((1, 1), f32)),
        grid=(S // tm,),
        in_specs=[pl.BlockSpec((tm, DA), row), pl.BlockSpec((tm, DC), row), pl.BlockSpec((D, D), fixed),
                  pl.BlockSpec((tm, D), row), pl.BlockSpec((tm, D), row), pl.BlockSpec((1, 3 * D), fixed),
                  pl.BlockSpec((1, 3 * D), fixed)],
        out_specs=(pl.BlockSpec((tm, D), row), pl.BlockSpec((tm, D), row), pl.BlockSpec((D, D), fixed),
                   pl.BlockSpec((1, D), fixed), SMEM_SPEC),
        compiler_params=_cp(48, dimension_semantics=("arbitrary",)),
    )(attn_g, conv_g, w_out, xx, tgt, mrow, b_ada)


DP_Q, DP_K, DP_V, DP_ZA, DP_U, DP_BG, DP_CG, DP_ZC = range(8)


def _conv_bwd(dmix, p, conv_w, conv_b):
    def body(d_ref, u_ref, bg_ref, cg_ref, zc_ref, w_ref, b_ref, dp_ref, gw_ref, gb_ref):
        du_ref, dbg_ref, dcg_ref, dzc_ref = dp_ref.at[0], dp_ref.at[1], dp_ref.at[2], dp_ref.at[3]
        dconv = d_ref[...]
        u, bg, cg, zc = u_ref[...], bg_ref[...], cg_ref[...], zc_ref[...]
        w0, w1, w2 = w_ref[0:1, :], w_ref[1:2, :], w_ref[2:3, :]
        cu = cg * u
        cu_m, cu_p = _shift_rows(cu, True), _shift_rows(cu, False)
        cv = b_ref[...] + cu_m * w0
        cv = cv + cu * w1
        cv = cv + cu_p * w2
        sz = _silu(zc)
        dbg_ref[...] = ((dconv * sz) * cv).astype(bf16)
        dzc_ref[...] = ((dconv * (bg * cv)) * _dsilu(zc)).astype(bf16)
        dcv = (dconv * sz) * bg
        gb_ref[...] = jnp.sum(dcv, axis=0, keepdims=True)
        gw_ref[0:1, :] = jnp.sum(dcv * cu_m, axis=0, keepdims=True)
        gw_ref[1:2, :] = jnp.sum(dcv * cu, axis=0, keepdims=True)
        gw_ref[2:3, :] = jnp.sum(dcv * cu_p, axis=0, keepdims=True)
        gw_ref[3:8, :] = jnp.zeros((5, 128), f32)
        dcu = _shift_rows(dcv, False) * w0 + dcv * w1 + _shift_rows(dcv, True) * w2
        dcg_ref[...] = (dcu * u).astype(bf16)
        du_ref[...] = (dcu * cg).astype(bf16)

    return _hbm_call(
        body, name="conv_bwd", out_shape=(SDS((8, S, DC), bf16), SDS((8, DC), f32), SDS((1, DC), f32)),
        grid=(DC // 128,),
        in_specs=[pl.BlockSpec((S, 128), lambda i: (0, 4 + i))] + _conv_specs(),
        out_specs=(pl.BlockSpec((4, S, 128), lambda i: (DP_U // 4, 0, i)), pl.BlockSpec((8, 128), lambda i: (0, i)),
                   pl.BlockSpec((1, 128), lambda i: (0, i))),
        compiler_params=_cp(48),
    )(dmix, p, p, p, p, conv_w, conv_b)


def _gate_bwd(dmix, p, o, dp8):
    tm = 256

    def body(d_ref, za_ref, o_ref, dp_in_ref, do_ref, dza_ref):
        za = za_ref[...]
        dattn = d_ref[...]
        a = dattn * _silu(za)
        bb = dattn * _dsilu(za)
        for hh in range(H):
            sl = slice(hh * DH, (hh + 1) * DH)
            do_ref[hh] = a[:, sl].astype(bf16)
            dza_ref[:, sl] = (bb[:, sl] * o_ref[hh]).astype(bf16)

    hspec = pl.BlockSpec((H, tm, DH), lambda i: (0, i, 0))
    return _hbm_call(
        body, name="gate_bwd", out_shape=(SDS((H, S, DH), bf16), SDS((8, S, DA), bf16)), grid=(S // tm,),
        in_specs=[pl.BlockSpec((tm, DA), lambda i: (i, 0)), pl.BlockSpec((tm, DA), lambda i: (i, 3)), hspec, ANY_SPEC],
        out_specs=(hspec, pl.BlockSpec((None, tm, DA), lambda i: (DP_ZA, i, 0))),
        input_output_aliases={3: 1},
    )(dmix, p, o, dp8)


def _attn_bwd(qr, qp, kr, vh, kc, vc, btt, do):
    def body(qr_ref, qp_ref, kr_ref, v_ref, kc_ref, vc_ref, bt_ref, do_ref,
             dqr_ref, dqp_ref, dkr_ref, dv_ref, dkc_ref, dvc_ref, dbt_ref):
        kcv, vcv = kc_ref[...], vc_ref[...]
        dkr_ref[...] = jnp.zeros_like(dkr_ref)
        dv_ref[...] = jnp.zeros_like(dv_ref)
        dbt_ref[...] = jnp.zeros_like(dbt_ref)
        ctx_acc = {}

        def products(b):
            qs, ks, t = _block_geom(b)
            dob = do_ref[qs:qs + QB, :]
            s_lat = _tt(kr_ref[ks:ks + KB, :], qr_ref[qs:qs + QB, :]) + bt_ref[t]
            s_ctx = _tt(kcv, qp_ref[qs:qs + QB, :])
            return s_lat, s_ctx, _tt(v_ref[ks:ks + KB, :], dob), _tt(vcv, dob)

        def score_grads(b, x):
            s_lat, s_ctx, dp_lat, dp_ctx = x
            p_lat, p_ctx = _softmax_t(s_lat, s_ctx)
            delta = jnp.sum(p_lat * dp_lat, axis=0, keepdims=True) + jnp.sum(p_ctx * dp_ctx, axis=0, keepdims=True)
            ds_lat = p_lat * (dp_lat - delta)
            ds_ctx = p_ctx * (dp_ctx - delta)
            return ds_lat, ds_lat.astype(bf16), ds_ctx.astype(bf16), p_lat.astype(bf16), p_ctx.astype(bf16)

        def operand_grads(b, y):
            qs, ks, t = _block_geom(b)
            ds_lat, dsb_lat, dsb_ctx, pb_lat, pb_ctx = y
            qrb, qpb, dob = qr_ref[qs:qs + QB, :], qp_ref[qs:qs + QB, :], do_ref[qs:qs + QB, :]
            dbt_ref[t] += ds_lat
            dqr_ref[qs:qs + QB, :] = _tn(dsb_lat, kr_ref[ks:ks + KB, :])
            dqp_ref[qs:qs + QB, :] = _tn(dsb_ctx, kcv)
            dkr_ref[ks:ks + KB, :] += jnp.dot(dsb_lat, qrb, preferred_element_type=f32)
            dv_ref[ks:ks + KB, :] += jnp.dot(pb_lat, dob, preferred_element_type=f32)
            dkc = jnp.dot(dsb_ctx, qpb, preferred_element_type=f32)
            dvc = jnp.dot(pb_ctx, dob, preferred_element_type=f32)
            ctx_acc["k"] = dkc if b == 0 else ctx_acc["k"] + dkc
            ctx_acc["v"] = dvc if b == 0 else ctx_acc["v"] + dvc

        _staged(NQB, (products, score_grads, operand_grads))
        dkc_ref[...] = ctx_acc["k"]
        dvc_ref[...] = ctx_acc["v"]

    sq = pl.BlockSpec((None, S, DH), lambda h: (h, 0, 0))
    sc = pl.BlockSpec((None, L, DH), lambda h: (h, 0, 0))
    sb = pl.BlockSpec((None, NT, KB, QB), lambda h: (h, 0, 0, 0))
    big, ctxs = SDS((H, S, DH), f32), SDS((H, L, DH), f32)
    return _hbm_call(
        body, name="attn_bwd", out_shape=(big, big, big, big, ctxs, ctxs, SDS((H, NT, KB, QB), f32)), grid=(H,),
        in_specs=[sq, sq, sq, sq, sc, sc, sb, sq], out_specs=(sq, sq, sq, sq, sc, sc, sb), compiler_params=_cp(56),
    )(qr, qp, kr, vh, kc, vc, btt, do)


def _bias_bwd(dbtt):
    pieces = _tile_pieces()

    def body(d_ref, o_ref, scr):
        scr[...] = jnp.zeros_like(scr)
        acc = [None] * N_DR
        for t in range(NT):
            for j in range(2):
                for u in range(KR):
                    dr = pieces[t][j][u]
                    if dr is None:
                        continue
                    piece = d_ref[t, u * GW:(u + 1) * GW, j * GW:(j + 1) * GW]
                    acc[dr] = piece if acc[dr] is None else acc[dr] + piece
        for dr in range(N_DR):
            scr[dr * GW:(dr + 1) * GW, 0:GW] = acc[dr]
        xs = pltpu.roll(scr[...], WIN_W - 1, 1)
        row = lax.broadcasted_iota(i32, xs.shape, 0)
        for b in range(6):
            xs = jnp.where(((row >> b) & 1) == 1, pltpu.roll(xs, 128 - (1 << b), 1), xs)
        rev = jnp.sum(xs.reshape(N_DR, GW, 128), axis=1)
        a = lax.broadcasted_iota(i32, (128, 128), 0)
        b = lax.broadcasted_iota(i32, (128, 128), 1)
        flip = ((a + b == N_DC - 1) & (a < N_DC)).astype(f32)
        o_ref[...] = jnp.dot(rev, flip, precision=HIGHEST, preferred_element_type=f32)

    return _hbm_call(
        body, name="bias_bwd", out_shape=SDS((H, N_DR, 128), f32), grid=(H,),
        in_specs=[pl.BlockSpec((None, NT, KB, QB), lambda h: (h, 0, 0, 0))],
        out_specs=pl.BlockSpec((None, N_DR, 128), lambda h: (h, 0, 0)),
        scratch_shapes=[pltpu.VMEM((N_DR * GW, 128), f32)],
    )(dbtt)


def _merge_heads(ref):
    return jnp.concatenate([ref[hh] for hh in range(H)], axis=1)


def _head_norm_bwd(xraw, gain, dy, ones_bd):
    r = lax.rsqrt(_head_sum(xraw * xraw, ones_bd) * (1.0 / DH) + RMS_EPS)
    xh = xraw * r
    gdy = dy * gain
    dx = r * (gdy - xh * (_head_sum(xh * gdy, ones_bd) * (1.0 / DH)))
    return dx, jnp.sum(dy * xh, axis=0, keepdims=True)


def _qk_bwd(dqr, dqp, dkr, dvh, p, gq, gk, rope, dp8):
    tm = 256

    def body(dqr_ref, dqp_ref, dkr_ref, dv_ref, qk_ref, gq_ref, gk_ref, cc_ref, cr_ref, sc_ref, sr_ref, dp_in_ref,
             dp_ref, ggq_ref, ggk_ref):
        dq_ref, dk_ref, dvo_ref = dp_ref.at[DP_Q], dp_ref.at[DP_K], dp_ref.at[DP_V]

        @pl.when(pl.program_id(0) == 0)
        def _():
            ggq_ref[...] = jnp.zeros_like(ggq_ref)
            ggk_ref[...] = jnp.zeros_like(ggk_ref)

        ones_bd = _head_ones()
        cs, sn = _rope_block(cc_ref, cr_ref, tm), _rope_block(sc_ref, sr_ref, tm)
        a = _merge_heads(dqr_ref)
        dyq = ((a * cs - _swap16(a) * sn) + _merge_heads(dqp_ref)) * QK_SCALE
        bk = _merge_heads(dkr_ref)
        dyk = bk * cs - _swap16(bk) * sn
        dq, gq_part = _head_norm_bwd(qk_ref[:, 0:DA], gq_ref[...], dyq, ones_bd)
        dk, gk_part = _head_norm_bwd(qk_ref[:, DA:2 * DA], gk_ref[...], dyk, ones_bd)
        dq_ref[...] = dq.astype(bf16)
        dk_ref[...] = dk.astype(bf16)
        dvo_ref[...] = _merge_heads(dv_ref).astype(bf16)
        ggq_ref[...] += gq_part
        ggk_ref[...] += gk_part

    hspec = pl.BlockSpec((H, tm, DH), lambda i: (0, i, 0))
    fixed = pl.BlockSpec((1, DA), lambda i: (0, 0))
    return _hbm_call(
        body, name="qk_bwd", out_shape=(SDS((8, S, DA), bf16), SDS((1, DA), f32), SDS((1, DA), f32)), grid=(S // tm,),
        in_specs=[hspec, hspec, hspec, hspec, pl.BlockSpec((tm, 2 * DA), lambda i: (i, 0)), fixed, fixed]
        + _rope_specs(tm) + [ANY_SPEC],
        out_specs=(pl.BlockSpec((3, tm, DA), lambda i: (0, i, 0)), fixed, fixed), input_output_aliases={11: 0},
        compiler_params=_cp(40, dimension_semantics=("arbitrary",)),
    )(dqr, dqp, dkr, dvh, p, gq, gk, *rope, dp8)


def _ctx_bwd(dkc, dvc, pc, gk):
    def body(dkc_ref, dvc_ref, p_ref, gk_ref, dk_ref, dv_ref, ggk_ref):
        ones_bd = _head_ones()
        dk, gk_part = _head_norm_bwd(p_ref[:, 0:DA], gk_ref[...], _merge_heads(dkc_ref), ones_bd)
        dk_ref[...] = dk.astype(bf16)
        dv_ref[...] = _merge_heads(dvc_ref).astype(bf16)
        ggk_ref[...] = gk_part

    piece = SDS((L, DA), bf16)
    return _hbm_call(
        body, name="ctx_bwd", out_shape=(piece, piece, SDS((1, DA), f32)), in_specs=[VMEM_SPEC] * 4,
        out_specs=(VMEM_SPEC,) * 3,
    )(dkc, dvc, pc, gk)


def _grad_w_in(h, dp8, hc, dkc_raw, dvc_m):
    tm = 512

    def body(h_ref, p_ref, hc_ref, dk_ref, dv_ref, g_ref):
        j, k = pl.program_id(0), pl.program_id(1)

        @pl.when(k == 0)
        def _():
            g_ref[...] = jnp.zeros_like(g_ref)

        @pl.when((k == 0) & (j == 0))
        def _():
            g_ref[:, DA:2 * DA] = _tn(hc_ref[...], dk_ref[...])

        @pl.when((k == 0) & (j == 1))
        def _():
            g_ref[:, 0:DA] = _tn(hc_ref[...], dv_ref[...])

        hv = h_ref[...]
        g_ref[:, 0:DA] += _tn(hv, p_ref[0])
        g_ref[:, DA:2 * DA] += _tn(hv, p_ref[1])

    fixed = lambda j, k: (0, 0)
    return _hbm_call(
        body, name="grad_w_in", out_shape=SDS((4, D, D), f32), grid=(4, S // tm),
        in_specs=[pl.BlockSpec((tm, D), lambda j, k: (k, 0)), pl.BlockSpec((2, tm, DA), lambda j, k: (j, k, 0)),
                  pl.BlockSpec((L, D), fixed), pl.BlockSpec((L, DA), fixed), pl.BlockSpec((L, DA), fixed)],
        out_specs=pl.BlockSpec((None, D, D), lambda j, k: (j, 0, 0)),
        compiler_params=_cp(40, dimension_semantics=("arbitrary", "arbitrary")),
    )(h, dp8, hc, dkc_raw, dvc_m)


def _norm_mod_bwd(x, dh, g, scale):
    r = lax.rsqrt(jnp.mean(x * x, axis=-1, keepdims=True) + RMS_EPS)
    xh = x * r
    y = xh * g
    dshift = jnp.sum(dh, axis=0, keepdims=True)
    dscale = jnp.sum(dh * y, axis=0, keepdims=True)
    dyn = dh * (1.0 + scale)
    dg = jnp.sum(dyn * xh, axis=0, keepdims=True)
    gdy = dyn * g
    dx = r * (gdy - xh * jnp.mean(xh * gdy, axis=-1, keepdims=True))
    return dx, dshift, dscale, dg


def _dh_grad_x(dp8, w4, xx, dy, norm_g, mrow, b_ada):
    tm = 256

    def body(p_ref, w_ref, x_ref, dy_ref, g_ref, m_ref, b_ref, gx_ref, dsh_ref, dsc_ref, dg_ref):
        @pl.when(pl.program_id(0) == 0)
        def _():
            dsh_ref[...] = jnp.zeros_like(dsh_ref)
            dsc_ref[...] = jnp.zeros_like(dsc_ref)
            dg_ref[...] = jnp.zeros_like(dg_ref)

        dh = None
        for j in range(4):
            for half in range(2):
                term = _tt(p_ref[2 * j + half], w_ref[j, :, half * DA:(half + 1) * DA])
                dh = term if dh is None else dh + term
        scale = m_ref[:, D:2 * D] + b_ref[:, D:2 * D]
        dx, dshift, dscale, dg = _norm_mod_bwd(x_ref[...], dh, g_ref[...], scale)
        gx_ref[...] = dy_ref[...] + dx
        dsh_ref[...] += dshift
        dsc_ref[...] += dscale
        dg_ref[...] += dg

    row = lambda i: (i, 0)
    fixed = lambda i: (0, 0)
    vec = SDS((1, D), f32)
    return _hbm_call(
        body, name="dh_grad_x", out_shape=(SDS((S, D), f32), vec, vec, vec), grid=(S // tm,),
        in_specs=[pl.BlockSpec((8, tm, DA), lambda i: (0, i, 0)), pl.BlockSpec((4, D, D), lambda i: (0, 0, 0)),
                  pl.BlockSpec((tm, D), row), pl.BlockSpec((tm, D), row), pl.BlockSpec((1, D), fixed),
                  pl.BlockSpec((1, 3 * D), fixed), pl.BlockSpec((1, 3 * D), fixed)],
        out_specs=(pl.BlockSpec((tm, D), row), pl.BlockSpec((1, D), fixed), pl.BlockSpec((1, D), fixed),
                   pl.BlockSpec((1, D), fixed)),
        compiler_params=_cp(56, dimension_semantics=("arbitrary",)),
    )(dp8, w4, xx, dy, norm_g, mrow, b_ada)


def _dhc_sums(dkc_raw, dvc_m, w4, ctx2, norm_g, mrow_c, b_ada):
    def body(dk_ref, dv_ref, w0_ref, w1_ref, x_ref, g_ref, m_ref, b_ref, dsh_ref, dsc_ref, dg_ref):
        dh = _tt(dk_ref[...], w0_ref[:, DA:2 * DA]) + _tt(dv_ref[...], w1_ref[:, 0:DA])
        scale = m_ref[:, D:2 * D] + b_ref[:, D:2 * D]
        _, dshift, dscale, dg = _norm_mod_bwd(x_ref[...], dh, g_ref[...], scale)
        dsh_ref[...] = dshift
        dsc_ref[...] = dscale
        dg_ref[...] = dg

    fixed = lambda i: (0, 0)
    vec = SDS((1, D), f32)
    vspec = pl.BlockSpec((1, D), fixed)
    return _hbm_call(
        body, name="dhc_sums", out_shape=(vec, vec, vec), grid=(1,),
        in_specs=[pl.BlockSpec((L, DA), fixed), pl.BlockSpec((L, DA), fixed),
                  pl.BlockSpec((None, D, D), lambda i: (0, 0, 0)), pl.BlockSpec((None, D, D), lambda i: (1, 0, 0)),
                  pl.BlockSpec((L, D), fixed), vspec, pl.BlockSpec((1, 3 * D), fixed), pl.BlockSpec((1, 3 * D), fixed)],
        out_specs=(vspec, vspec, vspec), compiler_params=_cp(32),
    )(dkc_raw, dvc_m, w4, w4, ctx2, norm_g, mrow_c, b_ada)


def _rope_tables():
    nf = DH // 4
    inv = np.float32(ROPE_THETA) ** (-np.arange(nf, dtype=np.float32) / np.float32(nf))
    ang_c = np.arange(GW, dtype=np.float32)[:, None] * inv
    ang_r = np.arange(ROWS, dtype=np.float32)[:, None] * inv
    zc, zr = np.zeros((GW, 2 * nf), np.float32), np.zeros((ROWS, 2 * nf), np.float32)
    ct_cos = np.tile(np.concatenate([zc, np.cos(ang_c), np.cos(ang_c)], axis=1), (1, H))
    ct_sin = np.tile(np.concatenate([zc, -np.sin(ang_c), np.sin(ang_c)], axis=1), (1, H))
    rt_cos = np.tile(np.concatenate([np.cos(ang_r), np.cos(ang_r), zr], axis=1), (1, H))
    rt_sin = np.tile(np.concatenate([-np.sin(ang_r), np.sin(ang_r), zr], axis=1), (1, H))
    rep8 = lambda t: np.ascontiguousarray(np.broadcast_to(t[:, None, :], (ROWS, 8, DA))).reshape(ROWS * 8, DA)
    return tuple(jnp.asarray(t, f32) for t in (ct_cos, rep8(rt_cos), ct_sin, rep8(rt_sin)))


def _local_step(xx, ctx2, tgt, mrow, mrow_c, b_ada, norm_g, weights, q_norm_g, k_norm_g, rpb2, conv_w_full, conv_b,
                mid=None):
    gq = jnp.tile(q_norm_g, (1, H))
    gk = jnp.tile(k_norm_g, (1, H))
    rope = _rope_tables()
    rpb_pad = jnp.pad(rpb2[:, :, ::-1], ((0, 0), (0, 0), (0, 128 - N_DC)))

    h = _prenorm(xx, norm_g, mrow, b_ada, 256, "prenorm_x")
    hc = _prenorm(ctx2, norm_g, mrow_c, b_ada, L, "prenorm_ctx")
    btb = _bias_prep(rpb_pad)
    jv = weights["jvec"]
    p = _in_proj_own(h, weights["own"], jv)
    w4 = weights["near"](p)
    p = _in_proj_block(h, w4, p, jv ^ 1, "in_proj_y")
    p = _in_proj_block(h, w4, p, jv ^ 2, "in_proj_x")
    w4 = weights["far"](w4, p)
    p = _in_proj_block(h, w4, p, jv ^ 3, "in_proj_xy")
    pc = _ctx_proj(hc, w4)
    qr, qp, kr, vh = _qk_prep(p, gq, gk, rope)
    kc, vc = _ctx_prep(pc, gk)
    o = _attn_fwd(qr, qp, kr, vh, kc, vc, btb)
    attn_g = _attn_gate(o, p)
    conv_g = _conv_fwd(p, conv_w_full, conv_b)
    w_out_full = weights["out"](conv_g)
    dy, dmix, g_w_out, dgate, loss_sum = _out_proj_loss(attn_g, conv_g, w_out_full, xx, tgt, mrow, b_ada)

    dp8, g_conv_w, g_conv_b = _conv_bwd(dmix, p, conv_w_full, conv_b)
    do, dp8 = _gate_bwd(dmix, p, o, dp8)
    dqr, dqp, dkr, dvh, dkc, dvc, dbtb = _attn_bwd(qr, qp, kr, vh, kc, vc, btb, do)
    g_rpb = _bias_bwd(dbtb)
    dp8, g_gq, g_gk = _qk_bwd(dqr, dqp, dkr, dvh, p, gq, gk, rope, dp8)
    dkc_raw, dvc_m, g_gk_c = _ctx_bwd(dkc, dvc, pc, gk)
    g_w_in = _grad_w_in(h, dp8, hc, dkc_raw, dvc_m)
    b_ada_late = b_ada if mid is None else mid(g_w_in, g_w_out, b_ada)
    grad_x, dshift, dscale, dng = _dh_grad_x(dp8, w4, xx, dy, norm_g, mrow, b_ada_late)
    dshift_c, dscale_c, dng_c = _dhc_sums(dkc_raw, dvc_m, w4, ctx2, norm_g, mrow_c, b_ada_late)
    return dict(loss_sum=loss_sum, grad_x=grad_x, g_w_in=g_w_in, g_w_out=g_w_out, g_conv_w=g_conv_w,
                g_conv_b=g_conv_b, g_rpb=g_rpb, g_gq=g_gq, g_gk=g_gk, g_gk_c=g_gk_c, dshift=dshift, dscale=dscale,
                dgate=dgate, dng=dng, dshift_c=dshift_c, dscale_c=dscale_c, dng_c=dng_c)


def _pair_sum_w_in(g, r, cvec):
    tr = 128

    def body(c_ref, g_ref, r_ref, t32_ref, tb_ref):
        t = g_ref[...] + r_ref[...]
        t32_ref[...] = t
        tb_ref[...] = t.astype(bf16)

    half = D // 2
    g_spec = pl.BlockSpec((4, tr, D), lambda i, c: (0, c[0] * (half // tr) + i, 0))
    o_spec = pl.BlockSpec((4, tr, D), lambda i, c: (0, i, 0))
    grid_spec = pltpu.PrefetchScalarGridSpec(num_scalar_prefetch=1, grid=(half // tr,), in_specs=[g_spec, o_spec],
                                             out_specs=(o_spec, o_spec))
    return _hbm_call(body, name="pair_sum_w_in", out_shape=(SDS((4, half, D), f32), SDS((4, half, D), bf16)),
                     grid_spec=grid_spec, compiler_params=_cp(40))(cvec, g, r)


def _pair_sum_w_out(g, r, cvec):
    hr = D // 8

    def body(c_ref, g0, g1, g2, g3, r_ref, t32_ref, tb_ref):
        for q, g_ref in enumerate((g0, g1, g2, g3)):
            t = g_ref[...] + r_ref[q]
            t32_ref[q] = t
            tb_ref[q] = t.astype(bf16)

    gspecs = [pl.BlockSpec((hr, D), lambda i, c, q=q: (2 * q + c[0], 0)) for q in range(4)]
    full = pl.BlockSpec((4, hr, D), lambda i, c: (0, 0, 0))
    grid_spec = pltpu.PrefetchScalarGridSpec(num_scalar_prefetch=1, grid=(1,), in_specs=gspecs + [full],
                                             out_specs=(full, full))
    return _hbm_call(body, name="pair_sum_w_out", out_shape=(SDS((4, hr, D), f32), SDS((4, hr, D), bf16)),
                          grid_spec=grid_spec)(cvec, g, g, g, g, r)


def _chip_sum(t32, r2, jvec, name):
    rows = t32.shape[1]
    tr = min(rows, 128)

    def body(j_ref, t_ref, r_ref, u_ref):
        u_ref[...] = ((t_ref[...] + r_ref[0].astype(f32)) + r_ref[1].astype(f32)) + r_ref[2].astype(f32)

    grid_spec = pltpu.PrefetchScalarGridSpec(
        num_scalar_prefetch=1, grid=(rows // tr,),
        in_specs=[pl.BlockSpec((None, tr, D), lambda i, j: (j[0], i, 0)), pl.BlockSpec((3, tr, D), lambda i, j: (0, i, 0))],
        out_specs=pl.BlockSpec((tr, D), lambda i, j: (i, 0)))
    return _hbm_call(body, name=name, out_shape=SDS((rows, D), f32), grid_spec=grid_spec)(jvec, t32, r2)


_PK = {}
_off = 0
for _name, _rows in (("dm", 24), ("dmc", 24), ("dng", 8), ("dng_c", 8), ("gq", 8), ("gk", 8), ("gk_c", 8),
                     ("rpb", H * N_DR), ("conv_b", 8), ("conv_w", 16), ("loss", 8)):
    _PK[_name] = (_off, _off + _rows)
    _off += _rows
PK_ROWS = _off
RS_B_ADA, RS_NORM_G, RS_GQ, RS_GK, RS_RPB, RS_CONV_B, RS_CONV_W, RS_DMC, RS_LOSS, RS_ROWS = (
    0, 24, 32, 40, 48, 168, 176, 192, 216, 224)


def _small_reduce(gathered):
    def body(g_ref, o_ref, dm_ref):
        a0 = _PK["dm"][0]
        dm_ref[...] = jnp.zeros_like(dm_ref)
        for b in range(8):
            for i in range(24):
                dm_ref[b:b + 1, 128 * i:128 * (i + 1)] = g_ref[b, a0 + i:a0 + i + 1, :]
        tot = g_ref[0]
        for b in range(1, 8):
            tot = tot + g_ref[b]

        def rows(name):
            a, z = _PK[name]
            return tot[a:z]

        o_ref[RS_B_ADA:RS_B_ADA + 24] = rows("dm") + rows("dmc")
        o_ref[RS_NORM_G:RS_NORM_G + 8] = rows("dng") + rows("dng_c")
        gq = jnp.broadcast_to(jnp.sum(rows("gq"), axis=0, keepdims=True), (8, 128))
        gk = jnp.broadcast_to(jnp.sum(rows("gk") + rows("gk_c"), axis=0, keepdims=True), (8, 128))
        o_ref[RS_GQ:RS_GQ + 8] = gq + pltpu.roll(gq, DH, 1)
        o_ref[RS_GK:RS_GK + 8] = gk + pltpu.roll(gk, DH, 1)
        o_ref[RS_RPB:RS_RPB + H * N_DR] = rows("rpb")
        o_ref[RS_CONV_B:RS_CONV_B + 8] = rows("conv_b")
        o_ref[RS_CONV_W:RS_CONV_W + 16] = rows("conv_w")
        dmc = rows("dmc")
        o_ref[RS_DMC:RS_DMC + 24] = dmc
        o_ref[RS_LOSS:RS_LOSS + 8] = rows("loss")
        for i in range(24):
            dm_ref[8:9, 128 * i:128 * (i + 1)] = dmc[i:i + 1]

    return _hbm_call(body, name="small_reduce", out_shape=(SDS((RS_ROWS, 128), f32), SDS((16, 3 * D), f32)),
                     in_specs=[VMEM_SPEC], out_specs=(VMEM_SPEC, VMEM_SPEC))(gathered)


def _w_ada_grad(sc16, dm16, w_ada_shard, jvec):
    ncol = w_ada_shard.shape[1]

    def body(j_ref, sc_ref, dm_ref, w_ref, g_ref, part_ref):
        dm = dm_ref[...]
        g_ref[...] = lax.dot_general(sc_ref[...], dm, (((0,), (0,)), ((), ())), precision=HIGHEST,
                                     preferred_element_type=f32)
        part_ref[...] = lax.dot_general(dm[8:16], w_ref[...], (((1,), (1,)), ((), ())), precision=HIGHEST,
                                        preferred_element_type=f32)

    fixed = lambda i, j: (0, 0)
    grid_spec = pltpu.PrefetchScalarGridSpec(
        num_scalar_prefetch=1, grid=(1,),
        in_specs=[pl.BlockSpec((16, D), fixed), pl.BlockSpec((16, ncol), lambda i, j: (0, j[0])),
                  pl.BlockSpec((D, ncol), fixed)],
        out_specs=(pl.BlockSpec((D, ncol), fixed), pl.BlockSpec((8, D), fixed)))
    return _pallas_call(body, name="w_ada_grad", out_shape=(SDS((D, ncol), f32), SDS((8, D), f32)),
                        grid_spec=grid_spec, compiler_params=_cp(40))(jvec, sc16, dm16, w_ada_shard)


def _c_ctx_grad(parts4, c_ctx_row):
    def body(p_ref, c_ref, o_ref):
        tot = ((p_ref[0] + p_ref[1]) + p_ref[2]) + p_ref[3]
        o_ref[...] = tot[0:1] * _dsilu(c_ref[...])

    return _hbm_call(body, name="c_ctx_grad", out_shape=SDS((1, D), f32), in_specs=[VMEM_SPEC, VMEM_SPEC],
                          out_specs=VMEM_SPEC)(parts4, c_ctx_row)


def _adamw(w, g, m, v, name):
    rows, cols = w.shape
    tr = 256 if rows % 256 == 0 else rows

    def body(w_ref, g_ref, m_ref, v_ref, d_ref, m2_ref, v2_ref):
        gv = g_ref[...]
        m2 = ADAM_B1 * m_ref[...] + (1.0 - ADAM_B1) * gv
        v2 = ADAM_B2 * v_ref[...] + (1.0 - ADAM_B2) * jnp.square(gv)
        m_hat = m2 / (1.0 - ADAM_B1 ** ADAM_STEP)
        v_hat = v2 / (1.0 - ADAM_B2 ** ADAM_STEP)
        d_ref[...] = -ADAM_LR * (m_hat / (jnp.sqrt(v_hat) + ADAM_EPS) + ADAM_WD * w_ref[...])
        m2_ref[...] = m2
        v2_ref[...] = v2

    spec = pl.BlockSpec((tr, cols), lambda i: (i, 0))
    shp = SDS((rows, cols), f32)
    return _hbm_call(body, name=name, out_shape=(shp, shp, shp), grid=(rows // tr,), in_specs=[spec] * 4,
                          out_specs=(spec, spec, spec))(w, g, m, v)


def _adamw_halves(w, g_mine, g_other, m, v, cvec, name):
    rows, cols = w.shape
    half = rows // 2
    tr = min(256, half)
    per_half = half // tr

    def body(c_ref, w_ref, ga_ref, gb_ref, m_ref, v_ref, g_ref, d_ref, m2_ref, v2_ref):
        in_my_half = (pl.program_id(0) // per_half) == c_ref[0]
        gv = jnp.where(in_my_half, ga_ref[...], gb_ref[...])
        g_ref[...] = gv
        m2 = ADAM_B1 * m_ref[...] + (1.0 - ADAM_B1) * gv
        v2 = ADAM_B2 * v_ref[...] + (1.0 - ADAM_B2) * jnp.square(gv)
        m_hat = m2 / (1.0 - ADAM_B1 ** ADAM_STEP)
        v_hat = v2 / (1.0 - ADAM_B2 ** ADAM_STEP)
        d_ref[...] = -ADAM_LR * (m_hat / (jnp.sqrt(v_hat) + ADAM_EPS) + ADAM_WD * w_ref[...])
        m2_ref[...] = m2
        v2_ref[...] = v2

    full = pl.BlockSpec((tr, cols), lambda i, c: (i, 0))
    part = pl.BlockSpec((tr, cols), lambda i, c: (i % per_half, 0))
    shp = SDS((rows, cols), f32)
    grid_spec = pltpu.PrefetchScalarGridSpec(num_scalar_prefetch=1, grid=(rows // tr,),
                                             in_specs=[full, part, part, full, full], out_specs=(full,) * 4)
    return _hbm_call(body, name=name, out_shape=(shp,) * 4, grid_spec=grid_spec)(cvec, w, g_mine, g_other, m, v)


def _adam_math(w, g, m, v):
    m2 = ADAM_B1 * m + (1.0 - ADAM_B1) * g
    v2 = ADAM_B2 * v + (1.0 - ADAM_B2) * jnp.square(g)
    m_hat = m2 / (1.0 - ADAM_B1 ** ADAM_STEP)
    v_hat = v2 / (1.0 - ADAM_B2 ** ADAM_STEP)
    return -ADAM_LR * (m_hat / (jnp.sqrt(v_hat) + ADAM_EPS) + ADAM_WD * w), m2, v2


def _adamw_small(red, g_c_ctx, jvec, ws, ms, vs):
    n = len(ws)

    def body(*refs):
        red_ref, gc_ref, j_ref = refs[:3]
        w_refs, m_refs, v_refs = refs[3:3 + n], refs[3 + n:3 + 2 * n], refs[3 + 2 * n:3 + 3 * n]
        outs = refs[3 + 3 * n:]
        g_out, d_out, m_out, v_out = outs[:n], outs[n:2 * n], outs[2 * n:3 * n], outs[3 * n:]
        chip = j_ref[0]
        lanes = lambda i: (slice(None), slice(128 * i, 128 * (i + 1)))
        row = lambda r0, i: (lambda: red_ref[r0 + i:r0 + i + 1, :])
        whole = (slice(None), slice(None))
        chunks = [
            [(whole, lambda: gc_ref[...])],
            [(lanes(i), row(RS_B_ADA, i)) for i in range(3 * D // 128)],
            [(lanes(i), row(RS_NORM_G, i)) for i in range(D // 128)],
            [(whole, lambda: red_ref[RS_GQ:RS_GQ + 1, 0:DH])],
            [(whole, lambda: red_ref[RS_GK:RS_GK + 1, 0:DH])],
            [((0, h), (lambda h=h: red_ref[RS_RPB + N_DR * h:RS_RPB + N_DR * (h + 1), 0:N_DC])) for h in range(H)],
            [((0, slice(r, r + 1), slice(None)), (lambda r=r: red_ref[pl.ds(RS_CONV_W + 4 * r + chip, 1), :]))
             for r in range(3)],
            [(lanes(i), row(RS_CONV_B, i)) for i in range(DC // 128)],
        ]
        for a in range(n):
            for idx, grad in chunks[a]:
                g = grad()
                d, m2, v2 = _adam_math(w_refs[a][idx], g, m_refs[a][idx], v_refs[a][idx])
                g_out[a][idx] = g
                d_out[a][idx] = d
                m_out[a][idx] = m2
                v_out[a][idx] = v2

    shapes = [SDS(w.shape, f32) for w in ws]
    res = _pallas_call(body, name="adamw_small", out_shape=shapes * 4,
                       in_specs=[VMEM_SPEC, VMEM_SPEC, SMEM_SPEC] + [VMEM_SPEC] * (3 * n),
                       out_specs=[VMEM_SPEC] * (4 * n))(red, g_c_ctx, jvec, *ws, *ms, *vs)
    return [list(res[k * n:(k + 1) * n]) for k in range(4)]


def _rows128(a):
    return a.reshape(-1, 128)


def _pad_lanes(a):
    a2 = a.reshape(-1, a.shape[-1])
    return jnp.pad(a2, ((0, 0), (0, 128 - a2.shape[1])))


def kernel(x, c, ctx, c_ctx, w_ada, b_ada, norm_g, w_in, q_norm_g, k_norm_g, rpb, conv_w, conv_b, w_out, loss_target, m_c_ctx, m_w_ada, m_b_ada, m_norm_g, m_w_in, m_q_norm_g, m_k_norm_g, m_rpb, m_conv_w, m_conv_b, m_w_out, v_c_ctx, v_w_ada, v_b_ada, v_norm_g, v_w_in, v_q_norm_g, v_k_norm_g, v_rpb, v_conv_w, v_conv_b, v_w_out):
    xi, yi, ci = lax.axis_index("x"), lax.axis_index("y"), lax.axis_index("c")
    dev = 4 * xi + 2 * yi + ci
    chip = 2 * xi + yi
    cvec = jnp.reshape(ci, (1,)).astype(i32)
    jvec = jnp.reshape(chip, (1,)).astype(i32)
    w_ada_s = w_ada[0]
    ncol = w_ada_s.shape[1]

    c8 = _all_gather8(c.reshape(8, 128), "gather_c").reshape(8, D)
    cc = jnp.concatenate([c8, c_ctx.reshape(1, D), jnp.zeros((7, D), f32)], axis=0)
    m_shard, sc16 = _adaln_shard(cc, w_ada_s)

    conv_w_pad = jnp.pad(conv_w[0], ((0, 5), (0, 0)))
    m4, cw4 = _chip_gather([m_shard, conv_w_pad], "gather_mod")

    all_k = [(0, 0, 0), (0, 0, 1), (0, 0, 2)]
    wo4c = _cast_to_slot(w_out[0], jvec, "cast_w_out")
    sem_a, rem_a, w4s, token = _halves_start([_cast_to_slot(w_in[0], jvec, "cast_w_in")], m4, all_k[0:2],
                                             "weights_near_start")
    m_full = jnp.transpose(m4, (1, 0, 2)).reshape(16, 4 * ncol) + token[0:1, 0:1]
    mrow = lax.dynamic_slice(m_full, (dev, 0), (1, 3 * D))
    mrow_c = m_full[8:9]
    conv_w_full = jnp.transpose(cw4[:, 0:3, :], (1, 0, 2)).reshape(3, DC)
    waves = {}

    def near(after):
        (w4w,) = _halves_wait(sem_a, rem_a, [w4s], after, all_k[0:2], "weights_near_wait")
        w4f = _halves_forward([w4w], [0, 1], "weights_near_forward")[0]
        sem_b, rem_b, w4b, _ = _halves_start([w4f], after, all_k[2:3], "weights_far_start")
        waves["far"] = (sem_b, rem_b)
        return w4b

    def far(w4, after):
        (w4w,) = _halves_wait(*waves["far"], [w4], after, all_k[2:3], "weights_far_wait")
        w4f = _halves_forward([w4w], [2], "weights_far_forward")[0]
        sem_c, rem_c, wo4s, _ = _halves_start([wo4c], w4f, all_k, "weights_out_start")
        waves["out"] = (sem_c, rem_c, wo4s)
        return w4f

    def w_out_gathered(after):
        sem_c, rem_c, wo4s = waves["out"]
        (wow,) = _halves_wait(sem_c, rem_c, [wo4s], after, all_k, "weights_out_wait")
        return _halves_forward([wow], [0, 1, 2], "weights_out_forward")[0].reshape(D, D)

    weights = dict(own=w_in[0], jvec=jvec, near=near, far=far, out=w_out_gathered)

    exchange = _exchange_copies
    pending = {}

    def mid(g_w_in, g_w_out, b_ada_in):
        shapes = [SDS((4, D // 2, D), f32), SDS((4, D // 8, D), f32)]
        out = _split_start([g_w_in, g_w_out], shapes, 8, exchange, "grad_pair_start")
        pending["ex"] = (out[0], out[1], list(out[2:4]), list(out[4:6]))
        return b_ada_in + out[6][0:1, 0:1]

    r = _local_step(x[0], ctx[0], loss_target[0], mrow, mrow_c, b_ada, norm_g, weights, q_norm_g, k_norm_g,
                    rpb[0], conv_w_full, conv_b, mid)
    dm = jnp.concatenate([r["dshift"], r["dscale"], r["dgate"]], axis=1)
    dmc = jnp.concatenate([r["dshift_c"], r["dscale_c"], jnp.zeros((1, D), f32)], axis=1)
    pack_parts = [_rows128(dm), _rows128(dmc), _rows128(r["dng"]), _rows128(r["dng_c"]), _rows128(r["g_gq"]),
                  _rows128(r["g_gk"]), _rows128(r["g_gk_c"]), r["g_rpb"].reshape(H * N_DR, 128),
                  _rows128(r["g_conv_b"]), _rows128(r["g_conv_w"][0:3]), jnp.pad(r["loss_sum"], ((0, 0), (0, 127)))]
    pack = jnp.concatenate([jnp.pad(p, ((0, -p.shape[0] % 8), (0, 0))) for p in pack_parts], axis=0)
    gathered = _all_gather8(pack, "gather_small")

    ex_ssem, ex_rsem, ex_srcs, ex_lands = pending["ex"]
    ex_g, ex = _split_wait(ex_ssem, ex_rsem, ex_srcs, ex_lands, gathered, exchange, "grad_pair_wait")
    t32, tb = _pair_sum_w_in(ex_g[0], ex[0], cvec)
    to32, tob = _pair_sum_w_out(ex_g[1], ex[1], cvec)
    sc_out = _split_start([tb, tob], [SDS((3, D // 2, D), bf16), SDS((3, D // 8, D), bf16)], 6, _scatter_copies,
                          "grad_chip_start")
    sc16 = sc16 + sc_out[6][0:1, 0:1]

    assert pack.shape[0] == PK_ROWS
    red, dm16 = _small_reduce(gathered)
    loss = red[RS_LOSS, 0] * (0.5 / D)

    g_w_ada_s, cpart = _w_ada_grad(sc16, dm16, w_ada_s, jvec)
    d_w_ada, nm_w_ada, nv_w_ada = _adamw(w_ada_s, g_w_ada_s, m_w_ada[0], v_w_ada[0], "adamw_w_ada")

    _, (r2, ro2) = _split_wait(sc_out[0], sc_out[1], [sc_out[2], sc_out[3]], [sc_out[4], sc_out[5]], nm_w_ada,
                               _scatter_copies, "grad_chip_wait")
    u_in = _chip_sum(t32, r2, jvec, "chip_sum_w_in")
    u_out = _chip_sum(to32, ro2, jvec, "chip_sum_w_out")

    (cparts4,) = _chip_gather([cpart], "gather_c_ctx_parts", after=u_out)
    g_c_ctx = _c_ctx_grad(cparts4, c_ctx.reshape(1, D))

    as_row = lambda a: a.reshape(1, D)
    small = _adamw_small(
        red, g_c_ctx, jvec,
        [as_row(c_ctx), b_ada, norm_g, q_norm_g, k_norm_g, rpb, conv_w, conv_b],
        [as_row(m_c_ctx), m_b_ada, m_norm_g, m_q_norm_g, m_k_norm_g, m_rpb, m_conv_w, m_conv_b],
        [as_row(v_c_ctx), v_b_ada, v_norm_g, v_q_norm_g, v_k_norm_g, v_rpb, v_conv_w, v_conv_b])

    o_in, o_out = _sibling_send([u_in, u_out], "grad_pair_send")
    g_w_in_s, d_w_in, nm_w_in, nv_w_in = _adamw_halves(w_in[0], u_in, o_in, m_w_in[0], v_w_in[0], cvec, "adamw_w_in")
    g_w_out_s, d_w_out, nm_w_out, nv_w_out = _adamw_halves(w_out[0], u_out, o_out, m_w_out[0], v_w_out[0], cvec,
                                                           "adamw_w_out")

    def ordered(kind, big_w_ada, big_w_in, big_w_out):
        s_c_ctx, s_b_ada, s_norm_g, s_q, s_k, s_rpb, s_conv_w, s_conv_b = small[kind]
        return [s_c_ctx.reshape(D), big_w_ada[None], s_b_ada, s_norm_g, big_w_in[None], s_q, s_k, s_rpb, s_conv_w,
                s_conv_b, big_w_out[None]]

    grads = ordered(0, g_w_ada_s, g_w_in_s, g_w_out_s)
    deltas = ordered(1, d_w_ada, d_w_in, d_w_out)
    new_m = ordered(2, nm_w_ada, nm_w_in, nm_w_out)
    new_v = ordered(3, nv_w_ada, nv_w_in, nv_w_out)
    return (loss, r["grad_x"][None], *grads, *deltas, *new_m, *new_v)
```

```python
import functools

import jax
import jax.numpy as jnp
import numpy as np
from jax import lax
from jax.experimental import pallas as pl
from jax.experimental.pallas import tpu as pltpu

f32, bf16, i32 = jnp.float32, jnp.bfloat16, jnp.int32
MESH = pl.DeviceIdType.MESH
HIGHEST = lax.Precision.HIGHEST

D = 1024
S = 2048
L = 256
GW = 64
ROWS = S // GW
H = 8
DH = 64
DA = H * DH
DC = 512
WIN_H, WIN_W = 8, 16
N_DR, N_DC = 2 * WIN_H - 1, 2 * WIN_W - 1
RMS_EPS = 1e-6
ROPE_THETA = 10000.0
QK_SCALE = DH ** -0.5
NEG = -1e30

QB = 128
NQB = S // QB
KR = 9
KB = KR * GW
TILE_GEOM = ((0, 0), (2, 0), (4, 0), (28, 23), (30, 23))
NT = len(TILE_GEOM)

ADAM_LR, ADAM_B1, ADAM_B2, ADAM_EPS, ADAM_WD, ADAM_STEP = 0.001, 0.9, 0.999, 1e-08, 0.01, 10

VMEM_SPEC = pl.BlockSpec(memory_space=pltpu.VMEM)
ANY_SPEC = pl.BlockSpec(memory_space=pl.ANY)
SMEM_SPEC = pl.BlockSpec(memory_space=pltpu.SMEM)
SDS = jax.ShapeDtypeStruct


_pallas_call = pl.pallas_call


def _hbm_call(body, *, out_shape, in_specs=None, out_specs=None, grid_spec=None, **kw):
    n_pre = 0
    if grid_spec is not None:
        ispecs, ospecs, n_pre = grid_spec.in_specs, grid_spec.out_specs, grid_spec.num_scalar_prefetch
        kw["grid_spec"] = grid_spec
    else:
        ispecs, ospecs = in_specs, out_specs
        kw.update(in_specs=in_specs, out_specs=out_specs)

    def blocked(spec):
        return isinstance(spec, pl.BlockSpec) and spec.block_shape is not None

    single = not isinstance(out_shape, (tuple, list))
    shapes = [out_shape] if single else list(out_shape)
    ospec_list = list(ospecs) if isinstance(ospecs, (tuple, list)) else [ospecs]
    shapes = [pltpu.HBM(s.shape, s.dtype) if blocked(sp) else s for s, sp in zip(shapes, ospec_list)]
    call = _pallas_call(body, out_shape=shapes[0] if single else tuple(shapes), **kw)

    def run(*args):
        arrays = [pltpu.with_memory_space_constraint(a, pltpu.HBM) if blocked(sp) else a
                  for a, sp in zip(args[n_pre:], ispecs)]
        return call(*args[:n_pre], *arrays)

    return run


def _cp(vmem_mb=None, **kw):
    if vmem_mb is not None:
        kw["vmem_limit_bytes"] = vmem_mb << 20
    return pltpu.CompilerParams(**kw)


def _silu(z):
    return z * jax.nn.sigmoid(z)


def _dsilu(z):
    sg = jax.nn.sigmoid(z)
    return sg * (1.0 + z * (1.0 - sg))


def _row_start(i):
    return min(max(i - WIN_H // 2, 0), ROWS - WIN_H)


def _my_pos():
    return lax.axis_index("x"), lax.axis_index("y"), lax.axis_index("c")


def _flip(v, bit):
    return 1 - v if bit else v


def _all_gather8(xin, name):
    R, N = xin.shape

    def body(x_ref, o_ref, ssem, rsem, lsem):
        x, y, c = _my_pos()
        me = 4 * x + 2 * y + c
        own = pltpu.make_async_copy(x_ref, o_ref.at[me], lsem)
        own.start()
        sends = []
        for k in range(1, 8):
            tgt = (_flip(x, (k >> 2) & 1), _flip(y, (k >> 1) & 1), _flip(c, k & 1))
            cp = pltpu.make_async_remote_copy(src_ref=x_ref, dst_ref=o_ref.at[me], send_sem=ssem.at[k - 1],
                                              recv_sem=rsem.at[k - 1], device_id=tgt, device_id_type=MESH)
            cp.start()
            sends.append(cp)
        for k in range(1, 8):
            tgt = (_flip(x, (k >> 2) & 1), _flip(y, (k >> 1) & 1), _flip(c, k & 1))
            peer = 4 * tgt[0] + 2 * tgt[1] + tgt[2]
            pltpu.make_async_remote_copy(src_ref=x_ref, dst_ref=o_ref.at[peer], send_sem=ssem.at[k - 1],
                                         recv_sem=rsem.at[k - 1], device_id=tgt, device_id_type=MESH).wait_recv()
        for cp in sends:
            cp.wait_send()
        own.wait()

    return _hbm_call(
        body, name=name, out_shape=SDS((8, R, N), xin.dtype), in_specs=[VMEM_SPEC], out_specs=VMEM_SPEC,
        scratch_shapes=[pltpu.SemaphoreType.DMA((7,)), pltpu.SemaphoreType.DMA((7,)), pltpu.SemaphoreType.DMA],
    )(xin)


def _chip_gather(smalls, name, after=None):
    ns = len(smalls)

    def body(*refs):
        s_in, s_out = refs[:ns], refs[ns + 1:2 * ns + 1]
        ssem, rsem, lsem = refs[2 * ns + 1:]
        x, y, c = _my_pos()
        j = 2 * x + y
        chips = _peer_chips(x, y, c)
        local = [pltpu.make_async_copy(s_in[a], s_out[a].at[j], lsem.at[a]) for a in range(ns)]
        for cp in local:
            cp.start()
        sends = []
        for a in range(ns):
            for k in range(3):
                cp = pltpu.make_async_remote_copy(src_ref=s_in[a], dst_ref=s_out[a].at[j], send_sem=ssem.at[3 * a + k],
                                                  recv_sem=rsem.at[3 * a + k], device_id=chips[k][0], device_id_type=MESH)
                cp.start()
                sends.append(cp)
        for a in range(ns):
            for k in range(3):
                pltpu.make_async_remote_copy(src_ref=s_in[a], dst_ref=s_out[a].at[chips[k][1]], send_sem=ssem.at[3 * a + k],
                                             recv_sem=rsem.at[3 * a + k], device_id=chips[k][0],
                                             device_id_type=MESH).wait_recv()
        for cp in sends:
            cp.wait_send()
        for cp in local:
            cp.wait()

    return _hbm_call(
        body, name=name, out_shape=[SDS((4,) + a.shape, a.dtype) for a in smalls],
        in_specs=[VMEM_SPEC] * ns + [ANY_SPEC], out_specs=[VMEM_SPEC] * ns,
        scratch_shapes=[pltpu.SemaphoreType.DMA((3 * ns,)), pltpu.SemaphoreType.DMA((3 * ns,)),
                        pltpu.SemaphoreType.DMA((ns,))],
    )(*smalls, smalls[0] if after is None else after)


HBM_SPEC = pl.BlockSpec(memory_space=pltpu.HBM)
SEM_SPEC = pl.BlockSpec(memory_space=pltpu.SEMAPHORE)
DATAFLOW = pltpu.SideEffectType.DATAFLOW_SIDE_EFFECTING


def _peer_chips(x, y, c):
    out = []
    for k in range(1, 4):
        px, py = _flip(x, (k >> 1) & 1), _flip(y, k & 1)
        out.append(((px, py, c), 2 * px + py))
    return out


def _half_copies(srcs, dsts, ssem, rsem, which):
    x, y, c = _my_pos()
    j = 2 * x + y
    peers = _peer_chips(x, y, c)
    pairs = []
    for pos, group, k in which:
        half = srcs[pos].shape[1] // 2
        mine = pl.ds(pl.multiple_of(c * half, 8), half)
        dev, pj = peers[k]
        sem = 3 * group + k
        send = pltpu.make_async_remote_copy(src_ref=srcs[pos].at[j, mine], dst_ref=dsts[pos].at[j, mine],
                                            send_sem=ssem.at[sem], recv_sem=rsem.at[sem], device_id=dev,
                                            device_id_type=MESH)
        arrive = pltpu.make_async_remote_copy(src_ref=srcs[pos].at[j, mine], dst_ref=dsts[pos].at[pj, mine],
                                              send_sem=ssem.at[sem], recv_sem=rsem.at[sem], device_id=dev,
                                              device_id_type=MESH)
        pairs.append((send, arrive))
    return pairs


def _halves_start(bigs, after, order, name):
    nb = len(bigs)

    def body(*refs):
        b_in = refs[:nb]
        ssem, rsem = refs[nb + 1], refs[nb + 2]
        b_out = refs[nb + 3:2 * nb + 3]
        token = refs[2 * nb + 3]
        for send, _ in _half_copies(b_in, b_out, ssem, rsem, order):
            send.start()
        token[...] = jnp.zeros_like(token)

    out_shape = (pltpu.SemaphoreType.DMA((3 * nb,)), pltpu.SemaphoreType.DMA((3 * nb,)),
                 *[pltpu.HBM(b.shape, b.dtype) for b in bigs], SDS((8, 128), f32))
    return _hbm_call(
        body, name=name, out_shape=out_shape, in_specs=[HBM_SPEC] * nb + [ANY_SPEC],
        out_specs=(SEM_SPEC, SEM_SPEC, *[HBM_SPEC] * nb, VMEM_SPEC),
        input_output_aliases={a: 2 + a for a in range(nb)}, compiler_params=_cp(has_side_effects=DATAFLOW),
    )(*[pltpu.with_memory_space_constraint(b, pltpu.HBM) for b in bigs], after)


def _halves_wait(ssem, rsem, bigs, after, which, name):
    nb = len(bigs)

    def body(*refs):
        b_in = refs[:nb]
        ssem_ref, rsem_ref = refs[nb], refs[nb + 1]
        for send, arrive in _half_copies(b_in, b_in, ssem_ref, rsem_ref, which):
            send.wait_send()
            arrive.wait_recv()

    return _hbm_call(
        body, name=name, out_shape=tuple(pltpu.HBM(b.shape, b.dtype) for b in bigs),
        in_specs=[HBM_SPEC] * nb + [SEM_SPEC, SEM_SPEC, ANY_SPEC], out_specs=tuple([HBM_SPEC] * nb),
        input_output_aliases={a: a for a in range(nb)}, compiler_params=_cp(has_side_effects=DATAFLOW),
    )(*bigs, ssem, rsem, after)


def _halves_forward(bigs, relations, name):
    nb, nr = len(bigs), len(relations)

    def body(*refs):
        b_in, b_out = refs[:nb], refs[nb:2 * nb]
        ssem, rsem = refs[2 * nb:]
        x, y, c = _my_pos()
        sib = (x, y, 1 - c)
        peers = _peer_chips(x, y, c)
        sends = []
        for a in range(nb):
            half = b_in[a].shape[1] // 2
            mine = pl.ds(pl.multiple_of(c * half, 8), half)
            for i, k in enumerate(relations):
                pj = peers[k][1]
                cp = pltpu.make_async_remote_copy(src_ref=b_in[a].at[pj, mine], dst_ref=b_out[a].at[pj, mine],
                                                  send_sem=ssem.at[nr * a + i], recv_sem=rsem.at[nr * a + i],
                                                  device_id=sib, device_id_type=MESH)
                cp.start()
                sends.append(cp)
        for a in range(nb):
            half = b_in[a].shape[1] // 2
            other = pl.ds(pl.multiple_of((1 - c) * half, 8), half)
            for i, k in enumerate(relations):
                pj = peers[k][1]
                pltpu.make_async_remote_copy(src_ref=b_in[a].at[pj, other], dst_ref=b_out[a].at[pj, other],
                                             send_sem=ssem.at[nr * a + i], recv_sem=rsem.at[nr * a + i],
                                             device_id=sib, device_id_type=MESH).wait_recv()
        for cp in sends:
            cp.wait_send()

    return _hbm_call(
        body, name=name, out_shape=[SDS(b.shape, b.dtype) for b in bigs], in_specs=[ANY_SPEC] * nb,
        out_specs=[ANY_SPEC] * nb, input_output_aliases={a: a for a in range(nb)},
        scratch_shapes=[pltpu.SemaphoreType.DMA((nr * nb,)), pltpu.SemaphoreType.DMA((nr * nb,))],
    )(*bigs)


def _cast_to_slot(w, jvec, name):
    rows, cols = w.shape
    tr = 256

    def body(j_ref, w_ref, o_ref):
        o_ref[...] = w_ref[...].astype(bf16)

    grid_spec = pltpu.PrefetchScalarGridSpec(
        num_scalar_prefetch=1, grid=(rows // tr,), in_specs=[pl.BlockSpec((tr, cols), lambda i, j: (i, 0))],
        out_specs=pl.BlockSpec((None, tr, cols), lambda i, j: (j[0], i, 0)))
    return _hbm_call(body, name=name, out_shape=SDS((4, rows, cols), bf16), grid_spec=grid_spec)(jvec, w)


def _exchange_copies(srcs, lands, ssem, rsem):
    x, y, c = _my_pos()
    cps = []
    for a in range(len(srcs)):
        stacked = len(srcs[a].shape) == 3
        rb = srcs[a].shape[1] if stacked else srcs[a].shape[0] // 4
        half = rb // 2
        for jb in range(4):
            if stacked:
                src = srcs[a].at[jb, pl.ds(pl.multiple_of((1 - c) * half, 8), half)]
            else:
                src = srcs[a].at[pl.ds(pl.multiple_of(jb * rb + (1 - c) * half, 8), half)]
            cps.append(pltpu.make_async_remote_copy(src_ref=src, dst_ref=lands[a].at[jb], send_sem=ssem.at[4 * a + jb],
                                                    recv_sem=rsem.at[4 * a + jb], device_id=(x, y, 1 - c),
                                                    device_id_type=MESH))
    return cps


def _scatter_copies(srcs, lands, ssem, rsem):
    x, y, c = _my_pos()
    cps = []
    for a in range(len(srcs)):
        for k, (dev, pj) in enumerate(_peer_chips(x, y, c)):
            cps.append(pltpu.make_async_remote_copy(src_ref=srcs[a].at[pj], dst_ref=lands[a].at[k],
                                                    send_sem=ssem.at[3 * a + k], recv_sem=rsem.at[3 * a + k],
                                                    device_id=dev, device_id_type=MESH))
    return cps


def _split_start(srcs, land_shapes, n_cp, make, name):
    ns, nl = len(srcs), len(land_shapes)

    def body(*refs):
        s_in = refs[:ns]
        ssem, rsem = refs[ns + nl], refs[ns + nl + 1]
        l_out = refs[2 * ns + nl + 2:2 * ns + 2 * nl + 2]
        token = refs[2 * ns + 2 * nl + 2]
        for cp in make(s_in, l_out, ssem, rsem):
            cp.start()
        token[...] = jnp.zeros_like(token)

    lands = [pltpu.with_memory_space_constraint(lax.empty(sh.shape, sh.dtype), pltpu.HBM) for sh in land_shapes]
    out_shape = (pltpu.SemaphoreType.DMA((n_cp,)), pltpu.SemaphoreType.DMA((n_cp,)),
                 *[pltpu.HBM(b.shape, b.dtype) for b in srcs], *[pltpu.HBM(b.shape, b.dtype) for b in land_shapes],
                 SDS((8, 128), f32))
    return _hbm_call(
        body, name=name, out_shape=out_shape, in_specs=[HBM_SPEC] * (ns + nl),
        out_specs=(SEM_SPEC, SEM_SPEC, *[HBM_SPEC] * (ns + nl), VMEM_SPEC),
        input_output_aliases={i: 2 + i for i in range(ns + nl)}, compiler_params=_cp(has_side_effects=DATAFLOW),
    )(*[pltpu.with_memory_space_constraint(b, pltpu.HBM) for b in srcs], *lands)


def _split_wait(ssem, rsem, srcs, lands, after, make, name):
    ns, nl = len(srcs), len(lands)

    def body(*refs):
        s_in, l_in = refs[:ns], refs[ns:ns + nl]
        ssem_ref, rsem_ref = refs[ns + nl], refs[ns + nl + 1]
        for cp in make(s_in, l_in, ssem_ref, rsem_ref):
            cp.wait_send()
            cp.wait_recv()

    outs = _hbm_call(
        body, name=name, out_shape=tuple(pltpu.HBM(b.shape, b.dtype) for b in (*srcs, *lands)),
        in_specs=[HBM_SPEC] * (ns + nl) + [SEM_SPEC, SEM_SPEC, ANY_SPEC], out_specs=tuple([HBM_SPEC] * (ns + nl)),
        input_output_aliases={i: i for i in range(ns + nl)}, compiler_params=_cp(has_side_effects=DATAFLOW),
    )(*srcs, *lands, ssem, rsem, after)
    return list(outs[:ns]), list(outs[ns:])


def _sibling_send(halves, name):
    n = len(halves)

    def body(*refs):
        ins, outs = refs[:n], refs[n:2 * n]
        ssem, rsem = refs[2 * n:]
        x, y, c = _my_pos()
        cps = []
        for a in range(n):
            cp = pltpu.make_async_remote_copy(src_ref=ins[a], dst_ref=outs[a], send_sem=ssem.at[a],
                                              recv_sem=rsem.at[a], device_id=(x, y, 1 - c), device_id_type=MESH)
            cp.start()
            cps.append(cp)
        for cp in cps:
            cp.wait_recv()
        for cp in cps:
            cp.wait_send()

    out_shape = [SDS(h.shape, h.dtype) for h in halves]
    return _hbm_call(
        body, name=name, out_shape=out_shape, in_specs=[ANY_SPEC] * n, out_specs=[ANY_SPEC] * n,
        scratch_shapes=[pltpu.SemaphoreType.DMA((n,)), pltpu.SemaphoreType.DMA((n,))],
    )(*halves)


def _adaln_shard(cc, w_ada_shard):
    def body(c_ref, w_ref, m_ref, sc_ref):
        sc = _silu(c_ref[...])
        sc_ref[...] = sc
        m_ref[...] = jnp.dot(sc, w_ref[...], precision=HIGHEST, preferred_element_type=f32)

    return _hbm_call(
        body, name="adaln_shard", out_shape=(SDS((16, w_ada_shard.shape[1]), f32), SDS((16, D), f32)),
        in_specs=[VMEM_SPEC, VMEM_SPEC], out_specs=(VMEM_SPEC, VMEM_SPEC), compiler_params=_cp(32),
    )(cc, w_ada_shard)


def _prenorm(xx, norm_g, mrow, b_ada, tm, name):
    n = xx.shape[0]

    def body(x_ref, g_ref, m_ref, b_ref, h_ref):
        x = x_ref[...]
        shift = m_ref[:, 0:D] + b_ref[:, 0:D]
        scale = m_ref[:, D:2 * D] + b_ref[:, D:2 * D]
        r = lax.rsqrt(jnp.mean(x * x, axis=-1, keepdims=True) + RMS_EPS)
        y = (x * r) * g_ref[...]
        h_ref[...] = (y * (1.0 + scale) + shift).astype(bf16)

    row = lambda i: (i, 0)
    fixed = lambda i: (0, 0)
    return _hbm_call(
        body, name=name, out_shape=SDS((n, D), bf16), grid=(n // tm,),
        in_specs=[pl.BlockSpec((tm, D), row), pl.BlockSpec((1, D), fixed), pl.BlockSpec((1, 3 * D), fixed),
                  pl.BlockSpec((1, 3 * D), fixed)],
        out_specs=pl.BlockSpec((tm, D), row),
    )(xx, norm_g, mrow, b_ada)


def _in_proj_own(h, w_own, jvec):
    tm = 512

    def body(j_ref, h_ref, w_ref, p_ref):
        p_ref[...] = jnp.dot(h_ref[...], w_ref[...].astype(bf16), preferred_element_type=f32)

    grid_spec = pltpu.PrefetchScalarGridSpec(
        num_scalar_prefetch=1, grid=(S // tm,),
        in_specs=[pl.BlockSpec((tm, D), lambda i, j: (i, 0)), pl.BlockSpec((D, D), lambda i, j: (0, 0))],
        out_specs=pl.BlockSpec((tm, D), lambda i, j: (i, j[0])))
    return _hbm_call(body, name="in_proj_own", out_shape=SDS((S, 4 * D), f32), grid_spec=grid_spec,
                     compiler_params=_cp(40))(jvec, h, w_own)


def _in_proj_block(h, w4, p, bvec, name):
    tm = 512

    def body(b_ref, h_ref, w_ref, p_in_ref, p_ref):
        p_ref[...] = jnp.dot(h_ref[...], w_ref[...], preferred_element_type=f32)

    grid_spec = pltpu.PrefetchScalarGridSpec(
        num_scalar_prefetch=1, grid=(S // tm,),
        in_specs=[pl.BlockSpec((tm, D), lambda i, b: (i, 0)), pl.BlockSpec((None, D, D), lambda i, b: (b[0], 0, 0)),
                  ANY_SPEC],
        out_specs=pl.BlockSpec((tm, D), lambda i, b: (i, b[0])))
    return _hbm_call(body, name=name, out_shape=SDS((S, 4 * D), f32), grid_spec=grid_spec,
                     input_output_aliases={3: 0})(bvec, h, w4, p)


def _ctx_proj(hc, w4):
    def body(h_ref, w0_ref, w1_ref, p_ref):
        hv = h_ref[...]
        p_ref[:, 0:DA] = jnp.dot(hv, w0_ref[:, DA:2 * DA], preferred_element_type=f32)
        p_ref[:, DA:2 * DA] = jnp.dot(hv, w1_ref[:, 0:DA], preferred_element_type=f32)

    return _hbm_call(
        body, name="ctx_proj", out_shape=SDS((L, 2 * DA), f32), grid=(1,),
        in_specs=[pl.BlockSpec((L, D), lambda i: (0, 0)), pl.BlockSpec((None, D, D), lambda i: (0, 0, 0)),
                  pl.BlockSpec((None, D, D), lambda i: (1, 0, 0))],
        out_specs=pl.BlockSpec((L, 2 * DA), lambda i: (0, 0)),
    )(hc, w4, w4)


def _head_ones():
    r = lax.broadcasted_iota(i32, (DA, DA), 0) // DH
    c = lax.broadcasted_iota(i32, (DA, DA), 1) // DH
    return (r == c).astype(bf16)


def _head_sum(v, ones_bd):
    hi = v.astype(bf16)
    lo = (v - hi.astype(f32)).astype(bf16)
    return jnp.dot(hi, ones_bd, preferred_element_type=f32) + jnp.dot(lo, ones_bd, preferred_element_type=f32)


def _swap16(v):
    lane = lax.broadcasted_iota(i32, v.shape, 1)
    return jnp.where((lane & 31) < 16, pltpu.roll(v, DA - 16, 1), pltpu.roll(v, 16, 1))


def _rope_block(ct_ref, rt_ref, tm):
    rows = [jnp.tile(rt_ref[8 * j:8 * j + 8, :], (GW // 8, 1)) for j in range(tm // GW)]
    return jnp.tile(ct_ref[...], (tm // GW, 1)) + jnp.concatenate(rows, axis=0)


def _rope_specs(tm):
    col = pl.BlockSpec((GW, DA), lambda i: (0, 0))
    row = pl.BlockSpec((8 * tm // GW, DA), lambda i: (i, 0))
    return [col, row, col, row]


def _qk_prep(p, gq, gk, rope):
    tm = 256

    def body(qk_ref, v_ref, gq_ref, gk_ref, cc_ref, cr_ref, sc_ref, sr_ref, qr_ref, qp_ref, kr_ref, vh_ref):
        ones_bd = _head_ones()
        cs, sn = _rope_block(cc_ref, cr_ref, tm), _rope_block(sc_ref, sr_ref, tm)
        q = qk_ref[:, 0:DA]
        k = qk_ref[:, DA:2 * DA]
        yq = (q * lax.rsqrt(_head_sum(q * q, ones_bd) * (1.0 / DH) + RMS_EPS)) * gq_ref[...]
        yk = (k * lax.rsqrt(_head_sum(k * k, ones_bd) * (1.0 / DH) + RMS_EPS)) * gk_ref[...]
        qr = (yq * cs + _swap16(yq) * sn) * QK_SCALE
        qp = yq * QK_SCALE
        kr = yk * cs + _swap16(yk) * sn
        vv = v_ref[...]
        for hh in range(H):
            sl = slice(hh * DH, (hh + 1) * DH)
            qr_ref[hh] = qr[:, sl].astype(bf16)
            qp_ref[hh] = qp[:, sl].astype(bf16)
            kr_ref[hh] = kr[:, sl].astype(bf16)
            vh_ref[hh] = vv[:, sl].astype(bf16)

    hm = SDS((H, S, DH), bf16)
    hspec = pl.BlockSpec((H, tm, DH), lambda i: (0, i, 0))
    fixed = lambda i: (0, 0)
    return _hbm_call(
        body, name="qk_prep", out_shape=(hm, hm, hm, hm), grid=(S // tm,),
        in_specs=[pl.BlockSpec((tm, 2 * DA), lambda i: (i, 0)), pl.BlockSpec((tm, DA), lambda i: (i, 2)),
                  pl.BlockSpec((1, DA), fixed), pl.BlockSpec((1, DA), fixed)] + _rope_specs(tm),
        out_specs=(hspec, hspec, hspec, hspec),
    )(p, p, gq, gk, *rope)


def _ctx_prep(pc, gk):
    def body(p_ref, gk_ref, kc_ref, vc_ref):
        ones_bd = _head_ones()
        k = p_ref[:, 0:DA]
        yk = (k * lax.rsqrt(_head_sum(k * k, ones_bd) * (1.0 / DH) + RMS_EPS)) * gk_ref[...]
        vv = p_ref[:, DA:2 * DA]
        for hh in range(H):
            sl = slice(hh * DH, (hh + 1) * DH)
            kc_ref[hh] = yk[:, sl].astype(bf16)
            vc_ref[hh] = vv[:, sl].astype(bf16)

    hm = SDS((H, L, DH), bf16)
    return _hbm_call(
        body, name="ctx_prep", out_shape=(hm, hm), in_specs=[VMEM_SPEC, VMEM_SPEC], out_specs=(VMEM_SPEC, VMEM_SPEC),
    )(pc, gk)


def _tile_pieces():
    out = []
    for (i0, u0) in TILE_GEOM:
        rows = []
        for j in range(2):
            i = i0 + j
            rs = _row_start(i)
            rows.append([(u0 + u - i + WIN_H - 1) if rs <= u0 + u < rs + WIN_H else None for u in range(KR)])
        out.append(rows)
    return out


def _bias_prep(rpb_rev_pad):
    pieces = _tile_pieces()

    def body(r_ref, o_ref):
        rp = r_ref[...]
        xs = jnp.broadcast_to(rp[:, None, :], (N_DR, GW, 128)).reshape(N_DR * GW, 128)
        row = lax.broadcasted_iota(i32, xs.shape, 0)
        lane = lax.broadcasted_iota(i32, xs.shape, 1)
        for b in range(6):
            xs = jnp.where(((row >> b) & 1) == 1, pltpu.roll(xs, 1 << b, 1), xs)
        xs = pltpu.roll(xs, 128 - (WIN_W - 1), 1)
        k = row & (GW - 1)
        c0 = jnp.clip(lane - WIN_W // 2, 0, GW - WIN_W)
        xs = jnp.where((k >= c0) & (k < c0 + WIN_W), xs, NEG)
        neg = jnp.full((GW, GW), NEG, f32)
        for t in range(NT):
            for j in range(2):
                for u in range(KR):
                    dr = pieces[t][j][u]
                    piece = neg if dr is None else xs[dr * GW:(dr + 1) * GW, 0:GW]
                    o_ref[t, u * GW:(u + 1) * GW, j * GW:(j + 1) * GW] = piece

    return _hbm_call(
        body, name="bias_prep", out_shape=SDS((H, NT, KB, QB), f32), grid=(H,),
        in_specs=[pl.BlockSpec((None, N_DR, 128), lambda h: (h, 0, 0))],
        out_specs=pl.BlockSpec((None, NT, KB, QB), lambda h: (h, 0, 0, 0)),
    )(rpb_rev_pad)


def _block_geom(b):
    qs = b * QB
    ks = min(max(2 * b - 4, 0), ROWS - KR) * GW
    t = b if b < 2 else (b - (NQB - NT) if b > NQB - 3 else 2)
    return qs, ks, t


def _tt(a, b):
    return lax.dot_general(a, b, (((1,), (1,)), ((), ())), preferred_element_type=f32)


def _tn(a, b):
    return lax.dot_general(a, b, (((0,), (0,)), ((), ())), preferred_element_type=f32)


def _softmax_t(s_lat, s_ctx):
    m = jnp.maximum(jnp.max(s_lat, axis=0, keepdims=True), jnp.max(s_ctx, axis=0, keepdims=True))
    e_lat = jnp.exp(s_lat - m)
    e_ctx = jnp.exp(s_ctx - m)
    inv = 1.0 / (jnp.sum(e_lat, axis=0, keepdims=True) + jnp.sum(e_ctx, axis=0, keepdims=True))
    return e_lat * inv, e_ctx * inv


def _staged(n_blocks, stages):
    held = [dict() for _ in stages]
    for step in range(n_blocks + len(stages) - 1):
        for s, fn in enumerate(stages):
            b = step - s
            if 0 <= b < n_blocks:
                held[s][b] = fn(b) if s == 0 else fn(b, held[s - 1].pop(b))


def _attn_fwd(qr, qp, kr, vh, kc, vc, btt):
    def body(qr_ref, qp_ref, kr_ref, v_ref, kc_ref, vc_ref, bt_ref, o_ref):
        kcv, vcv = kc_ref[...], vc_ref[...]

        def scores(b):
            qs, ks, t = _block_geom(b)
            return (_tt(kr_ref[ks:ks + KB, :], qr_ref[qs:qs + QB, :]) + bt_ref[t], _tt(kcv, qp_ref[qs:qs + QB, :]))

        def probs(b, sc):
            p_lat, p_ctx = _softmax_t(*sc)
            return p_lat.astype(bf16), p_ctx.astype(bf16)

        def values(b, p):
            qs, ks, _ = _block_geom(b)
            o_ref[qs:qs + QB, :] = _tn(p[0], v_ref[ks:ks + KB, :]) + _tn(p[1], vcv)

        _staged(NQB, (scores, probs, values))

    sq = pl.BlockSpec((None, S, DH), lambda h: (h, 0, 0))
    sc = pl.BlockSpec((None, L, DH), lambda h: (h, 0, 0))
    return _hbm_call(
        body, name="attn_fwd", out_shape=SDS((H, S, DH), f32), grid=(H,),
        in_specs=[sq, sq, sq, sq, sc, sc, pl.BlockSpec((None, NT, KB, QB), lambda h: (h, 0, 0, 0))],
        out_specs=sq, compiler_params=_cp(48),
    )(qr, qp, kr, vh, kc, vc, btt)


def _shift_rows(v, down):
    n = v.shape[0]
    row = lax.broadcasted_iota(i32, v.shape, 0)
    if down:
        return jnp.where(row == 0, 0.0, pltpu.roll(v, 1, 0))
    return jnp.where(row == n - 1, 0.0, pltpu.roll(v, n - 1, 0))


def _conv_specs():
    col = lambda off: pl.BlockSpec((S, 128), lambda i, off=off: (0, off + i))
    return [col(16), col(20), col(24), col(28), pl.BlockSpec((3, 128), lambda i: (0, i)),
            pl.BlockSpec((1, 128), lambda i: (0, i))]


def _conv_fwd(p, conv_w, conv_b):
    def body(u_ref, bg_ref, cg_ref, zc_ref, w_ref, b_ref, o_ref):
        cu = cg_ref[...] * u_ref[...]
        cv = b_ref[...] + _shift_rows(cu, True) * w_ref[0:1, :]
        cv = cv + cu * w_ref[1:2, :]
        cv = cv + _shift_rows(cu, False) * w_ref[2:3, :]
        o_ref[...] = ((bg_ref[...] * cv) * _silu(zc_ref[...])).astype(bf16)

    return _hbm_call(
        body, name="conv_fwd", out_shape=SDS((S, DC), bf16), grid=(DC // 128,),
        in_specs=_conv_specs(), out_specs=pl.BlockSpec((S, 128), lambda i: (0, i)), compiler_params=_cp(40),
    )(p, p, p, p, conv_w, conv_b)


DP_Q, DP_K, DP_V, DP_ZA, DP_U, DP_BG, DP_CG, DP_ZC = range(8)


def _out_proj_loss(o, p, conv_g, w_out, xx, tgt, mrow, b_ada):
    tm = 256

    def body(o_ref, za_ref, c_ref, w_ref, x_ref, t_ref, m_ref, b_ref,
             dy_ref, dconv_ref, dp_ref, do_ref, gwo_ref, dgate_ref, loss_ref):
        k = pl.program_id(0)

        @pl.when(k == 0)
        def _():
            gwo_ref[...] = jnp.zeros_like(gwo_ref)
            dgate_ref[...] = jnp.zeros_like(dgate_ref)
            loss_ref[0, 0] = 0.0

        gate = m_ref[:, 2 * D:3 * D] + b_ref[:, 2 * D:3 * D]
        za = za_ref[...]
        sz = _silu(za)
        om = _merge_heads(o_ref)
        av, cv = (om * sz).astype(bf16), c_ref[...]
        mo = jnp.dot(av, w_ref[0:DA, :], preferred_element_type=f32)
        mo = mo + jnp.dot(cv, w_ref[DA:DA + DC, :], preferred_element_type=f32)
        y = x_ref[...] + gate * mo
        diff = y - t_ref[...]
        loss_ref[0, 0] += jnp.sum(diff * diff)
        dy = diff * (1.0 / D)
        dy_ref[...] = dy
        dgate_ref[...] += jnp.sum(dy * mo, axis=0, keepdims=True)
        dmo = (dy * gate).astype(bf16)
        dmix = _tt(dmo, w_ref[...])
        dattn = dmix[:, 0:DA]
        dconv_ref[...] = dmix[:, DA:DA + DC]
        a = dattn * sz
        for hh in range(H):
            do_ref[hh] = a[:, hh * DH:(hh + 1) * DH].astype(bf16)
        dp_ref[...] = ((dattn * _dsilu(za)) * om).astype(bf16)
        gwo_ref[0:DA, :] += _tn(av, dmo)
        gwo_ref[DA:DA + DC, :] += _tn(cv, dmo)

    row = lambda i: (i, 0)
    fixed = lambda i: (0, 0)
    hspec = pl.BlockSpec((H, tm, DH), lambda i: (0, i, 0))
    return _hbm_call(
        body, name="out_proj_loss",
        out_shape=(SDS((S, D), f32), SDS((S, DC), f32), SDS((8, S, DA), bf16), SDS((H, S, DH), bf16),
                   SDS((D, D), f32), SDS((1, D), f32), SDS((1, 1), f32)),
        grid=(S // tm,),
        in_specs=[hspec, pl.BlockSpec((tm, DA), lambda i: (i, 3)), pl.BlockSpec((tm, DC), row),
                  pl.BlockSpec((D, D), fixed), pl.BlockSpec((tm, D), row), pl.BlockSpec((tm, D), row),
                  pl.BlockSpec((1, 3 * D), fixed), pl.BlockSpec((1, 3 * D), fixed)],
        out_specs=(pl.BlockSpec((tm, D), row), pl.BlockSpec((tm, DC), row),
                   pl.BlockSpec((None, tm, DA), lambda i: (DP_ZA, i, 0)), hspec, pl.BlockSpec((D, D), fixed),
                   pl.BlockSpec((1, D), fixed), SMEM_SPEC),
        compiler_params=_cp(56, dimension_semantics=("arbitrary",)),
    )(o, p, conv_g, w_out, xx, tgt, mrow, b_ada)


def _conv_bwd(dconv, p, conv_w, conv_b, dp8):
    def body(d_ref, u_ref, bg_ref, cg_ref, zc_ref, w_ref, b_ref, dp_in_ref, dp_ref, gw_ref, gb_ref):
        du_ref, dbg_ref, dcg_ref, dzc_ref = dp_ref.at[0], dp_ref.at[1], dp_ref.at[2], dp_ref.at[3]
        dconv = d_ref[...]
        u, bg, cg, zc = u_ref[...], bg_ref[...], cg_ref[...], zc_ref[...]
        w0, w1, w2 = w_ref[0:1, :], w_ref[1:2, :], w_ref[2:3, :]
        cu = cg * u
        cu_m, cu_p = _shift_rows(cu, True), _shift_rows(cu, False)
        cv = b_ref[...] + cu_m * w0
        cv = cv + cu * w1
        cv = cv + cu_p * w2
        sz = _silu(zc)
        dbg_ref[...] = ((dconv * sz) * cv).astype(bf16)
        dzc_ref[...] = ((dconv * (bg * cv)) * _dsilu(zc)).astype(bf16)
        dcv = (dconv * sz) * bg
        gb_ref[...] = jnp.sum(dcv, axis=0, keepdims=True)
        gw_ref[0:1, :] = jnp.sum(dcv * cu_m, axis=0, keepdims=True)
        gw_ref[1:2, :] = jnp.sum(dcv * cu, axis=0, keepdims=True)
        gw_ref[2:3, :] = jnp.sum(dcv * cu_p, axis=0, keepdims=True)
        gw_ref[3:8, :] = jnp.zeros((5, 128), f32)
        dcu = _shift_rows(dcv, False) * w0 + dcv * w1 + _shift_rows(dcv, True) * w2
        dcg_ref[...] = (dcu * u).astype(bf16)
        du_ref[...] = (dcu * cg).astype(bf16)

    return _hbm_call(
        body, name="conv_bwd", out_shape=(SDS((8, S, DC), bf16), SDS((8, DC), f32), SDS((1, DC), f32)),
        grid=(DC // 128,),
        in_specs=[pl.BlockSpec((S, 128), lambda i: (0, i))] + _conv_specs() + [ANY_SPEC],
        out_specs=(pl.BlockSpec((4, S, 128), lambda i: (DP_U // 4, 0, i)), pl.BlockSpec((8, 128), lambda i: (0, i)),
                   pl.BlockSpec((1, 128), lambda i: (0, i))),
        input_output_aliases={7: 0}, compiler_params=_cp(48),
    )(dconv, p, p, p, p, conv_w, conv_b, dp8)


def _attn_bwd(qr, qp, kr, vh, kc, vc, btt, do):
    def body(qr_ref, qp_ref, kr_ref, v_ref, kc_ref, vc_ref, bt_ref, do_ref,
             dqr_ref, dqp_ref, dkr_ref, dv_ref, dkc_ref, dvc_ref, dbt_ref):
        kcv, vcv = kc_ref[...], vc_ref[...]
        dkr_ref[...] = jnp.zeros_like(dkr_ref)
        dv_ref[...] = jnp.zeros_like(dv_ref)
        dbt_ref[...] = jnp.zeros_like(dbt_ref)
        ctx_acc = {}

        def products(b):
            qs, ks, t = _block_geom(b)
            dob = do_ref[qs:qs + QB, :]
            s_lat = _tt(kr_ref[ks:ks + KB, :], qr_ref[qs:qs + QB, :]) + bt_ref[t]
            s_ctx = _tt(kcv, qp_ref[qs:qs + QB, :])
            return s_lat, s_ctx, _tt(v_ref[ks:ks + KB, :], dob), _tt(vcv, dob)

        def score_grads(b, x):
            s_lat, s_ctx, dp_lat, dp_ctx = x
            p_lat, p_ctx = _softmax_t(s_lat, s_ctx)
            delta = jnp.sum(p_lat * dp_lat, axis=0, keepdims=True) + jnp.sum(p_ctx * dp_ctx, axis=0, keepdims=True)
            ds_lat = p_lat * (dp_lat - delta)
            ds_ctx = p_ctx * (dp_ctx - delta)
            return ds_lat, ds_lat.astype(bf16), ds_ctx.astype(bf16), p_lat.astype(bf16), p_ctx.astype(bf16)

        def operand_grads(b, y):
            qs, ks, t = _block_geom(b)
            ds_lat, dsb_lat, dsb_ctx, pb_lat, pb_ctx = y
            qrb, qpb, dob = qr_ref[qs:qs + QB, :], qp_ref[qs:qs + QB, :], do_ref[qs:qs + QB, :]
            dbt_ref[t] += ds_lat
            dqr_ref[qs:qs + QB, :] = _tn(dsb_lat, kr_ref[ks:ks + KB, :])
            dqp_ref[qs:qs + QB, :] = _tn(dsb_ctx, kcv)
            dkr_ref[ks:ks + KB, :] += jnp.dot(dsb_lat, qrb, preferred_element_type=f32)
            dv_ref[ks:ks + KB, :] += jnp.dot(pb_lat, dob, preferred_element_type=f32)
            dkc = jnp.dot(dsb_ctx, qpb, preferred_element_type=f32)
            dvc = jnp.dot(pb_ctx, dob, preferred_element_type=f32)
            ctx_acc["k"] = dkc if b == 0 else ctx_acc["k"] + dkc
            ctx_acc["v"] = dvc if b == 0 else ctx_acc["v"] + dvc

        _staged(NQB, (products, score_grads, operand_grads))
        dkc_ref[...] = ctx_acc["k"]
        dvc_ref[...] = ctx_acc["v"]

    sq = pl.BlockSpec((None, S, DH), lambda h: (h, 0, 0))
    sc = pl.BlockSpec((None, L, DH), lambda h: (h, 0, 0))
    sb = pl.BlockSpec((None, NT, KB, QB), lambda h: (h, 0, 0, 0))
    big, ctxs = SDS((H, S, DH), f32), SDS((H, L, DH), f32)
    return _hbm_call(
        body, name="attn_bwd", out_shape=(big, big, big, big, ctxs, ctxs, SDS((H, NT, KB, QB), f32)), grid=(H,),
        in_specs=[sq, sq, sq, sq, sc, sc, sb, sq], out_specs=(sq, sq, sq, sq, sc, sc, sb), compiler_params=_cp(56),
    )(qr, qp, kr, vh, kc, vc, btt, do)


def _bias_bwd(dbtt):
    pieces = _tile_pieces()

    def body(d_ref, o_ref, scr):
        scr[...] = jnp.zeros_like(scr)
        acc = [None] * N_DR
        for t in range(NT):
            for j in range(2):
                for u in range(KR):
                    dr = pieces[t][j][u]
                    if dr is None:
                        continue
                    piece = d_ref[t, u * GW:(u + 1) * GW, j * GW:(j + 1) * GW]
                    acc[dr] = piece if acc[dr] is None else acc[dr] + piece
        for dr in range(N_DR):
            scr[dr * GW:(dr + 1) * GW, 0:GW] = acc[dr]
        xs = pltpu.roll(scr[...], WIN_W - 1, 1)
        row = lax.broadcasted_iota(i32, xs.shape, 0)
        for b in range(6):
            xs = jnp.where(((row >> b) & 1) == 1, pltpu.roll(xs, 128 - (1 << b), 1), xs)
        rev = jnp.sum(xs.reshape(N_DR, GW, 128), axis=1)
        a = lax.broadcasted_iota(i32, (128, 128), 0)
        b = lax.broadcasted_iota(i32, (128, 128), 1)
        flip = ((a + b == N_DC - 1) & (a < N_DC)).astype(f32)
        o_ref[...] = jnp.dot(rev, flip, precision=HIGHEST, preferred_element_type=f32)

    return _hbm_call(
        body, name="bias_bwd", out_shape=SDS((H, N_DR, 128), f32), grid=(H,),
        in_specs=[pl.BlockSpec((None, NT, KB, QB), lambda h: (h, 0, 0, 0))],
        out_specs=pl.BlockSpec((None, N_DR, 128), lambda h: (h, 0, 0)),
        scratch_shapes=[pltpu.VMEM((N_DR * GW, 128), f32)],
    )(dbtt)


def _merge_heads(ref):
    return jnp.concatenate([ref[hh] for hh in range(H)], axis=1)


def _head_norm_bwd(xraw, gain, dy, ones_bd):
    r = lax.rsqrt(_head_sum(xraw * xraw, ones_bd) * (1.0 / DH) + RMS_EPS)
    xh = xraw * r
    gdy = dy * gain
    dx = r * (gdy - xh * (_head_sum(xh * gdy, ones_bd) * (1.0 / DH)))
    return dx, jnp.sum(dy * xh, axis=0, keepdims=True)


def _qk_bwd(dqr, dqp, dkr, dvh, p, gq, gk, rope, dp8):
    tm = 256

    def body(dqr_ref, dqp_ref, dkr_ref, dv_ref, qk_ref, gq_ref, gk_ref, cc_ref, cr_ref, sc_ref, sr_ref, dp_in_ref,
             dp_ref, ggq_ref, ggk_ref):
        dq_ref, dk_ref, dvo_ref = dp_ref.at[DP_Q], dp_ref.at[DP_K], dp_ref.at[DP_V]

        @pl.when(pl.program_id(0) == 0)
        def _():
            ggq_ref[...] = jnp.zeros_like(ggq_ref)
            ggk_ref[...] = jnp.zeros_like(ggk_ref)

        ones_bd = _head_ones()
        cs, sn = _rope_block(cc_ref, cr_ref, tm), _rope_block(sc_ref, sr_ref, tm)
        a = _merge_heads(dqr_ref)
        dyq = ((a * cs - _swap16(a) * sn) + _merge_heads(dqp_ref)) * QK_SCALE
        bk = _merge_heads(dkr_ref)
        dyk = bk * cs - _swap16(bk) * sn
        dq, gq_part = _head_norm_bwd(qk_ref[:, 0:DA], gq_ref[...], dyq, ones_bd)
        dk, gk_part = _head_norm_bwd(qk_ref[:, DA:2 * DA], gk_ref[...], dyk, ones_bd)
        dq_ref[...] = dq.astype(bf16)
        dk_ref[...] = dk.astype(bf16)
        dvo_ref[...] = _merge_heads(dv_ref).astype(bf16)
        ggq_ref[...] += gq_part
        ggk_ref[...] += gk_part

    hspec = pl.BlockSpec((H, tm, DH), lambda i: (0, i, 0))
    fixed = pl.BlockSpec((1, DA), lambda i: (0, 0))
    return _hbm_call(
        body, name="qk_bwd", out_shape=(SDS((8, S, DA), bf16), SDS((1, DA), f32), SDS((1, DA), f32)), grid=(S // tm,),
        in_specs=[hspec, hspec, hspec, hspec, pl.BlockSpec((tm, 2 * DA), lambda i: (i, 0)), fixed, fixed]
        + _rope_specs(tm) + [ANY_SPEC],
        out_specs=(pl.BlockSpec((3, tm, DA), lambda i: (0, i, 0)), fixed, fixed), input_output_aliases={11: 0},
        compiler_params=_cp(40, dimension_semantics=("arbitrary",)),
    )(dqr, dqp, dkr, dvh, p, gq, gk, *rope, dp8)


def _ctx_bwd(dkc, dvc, pc, gk):
    def body(dkc_ref, dvc_ref, p_ref, gk_ref, dk_ref, dv_ref, ggk_ref):
        ones_bd = _head_ones()
        dk, gk_part = _head_norm_bwd(p_ref[:, 0:DA], gk_ref[...], _merge_heads(dkc_ref), ones_bd)
        dk_ref[...] = dk.astype(bf16)
        dv_ref[...] = _merge_heads(dvc_ref).astype(bf16)
        ggk_ref[...] = gk_part

    piece = SDS((L, DA), bf16)
    return _hbm_call(
        body, name="ctx_bwd", out_shape=(piece, piece, SDS((1, DA), f32)), in_specs=[VMEM_SPEC] * 4,
        out_specs=(VMEM_SPEC,) * 3,
    )(dkc, dvc, pc, gk)


def _grad_w_in(h, dp8, hc, dkc_raw, dvc_m):
    tm = 512

    def body(h_ref, p_ref, hc_ref, dk_ref, dv_ref, g_ref):
        j, k = pl.program_id(0), pl.program_id(1)

        @pl.when(k == 0)
        def _():
            g_ref[...] = jnp.zeros_like(g_ref)

        @pl.when((k == 0) & (j == 0))
        def _():
            g_ref[:, DA:2 * DA] = _tn(hc_ref[...], dk_ref[...])

        @pl.when((k == 0) & (j == 1))
        def _():
            g_ref[:, 0:DA] = _tn(hc_ref[...], dv_ref[...])

        hv = h_ref[...]
        g_ref[:, 0:DA] += _tn(hv, p_ref[0])
        g_ref[:, DA:2 * DA] += _tn(hv, p_ref[1])

    fixed = lambda j, k: (0, 0)
    return _hbm_call(
        body, name="grad_w_in", out_shape=SDS((4, D, D), f32), grid=(4, S // tm),
        in_specs=[pl.BlockSpec((tm, D), lambda j, k: (k, 0)), pl.BlockSpec((2, tm, DA), lambda j, k: (j, k, 0)),
                  pl.BlockSpec((L, D), fixed), pl.BlockSpec((L, DA), fixed), pl.BlockSpec((L, DA), fixed)],
        out_specs=pl.BlockSpec((None, D, D), lambda j, k: (j, 0, 0)),
        compiler_params=_cp(40, dimension_semantics=("arbitrary", "arbitrary")),
    )(h, dp8, hc, dkc_raw, dvc_m)


def _norm_mod_bwd(x, dh, g, scale):
    r = lax.rsqrt(jnp.mean(x * x, axis=-1, keepdims=True) + RMS_EPS)
    xh = x * r
    y = xh * g
    dshift = jnp.sum(dh, axis=0, keepdims=True)
    dscale = jnp.sum(dh * y, axis=0, keepdims=True)
    dyn = dh * (1.0 + scale)
    dg = jnp.sum(dyn * xh, axis=0, keepdims=True)
    gdy = dyn * g
    dx = r * (gdy - xh * jnp.mean(xh * gdy, axis=-1, keepdims=True))
    return dx, dshift, dscale, dg


def _dh_grad_x(dp8, w4, xx, dy, norm_g, mrow, b_ada):
    tm = 256

    def body(p_ref, w_ref, x_ref, dy_ref, g_ref, m_ref, b_ref, gx_ref, dsh_ref, dsc_ref, dg_ref):
        @pl.when(pl.program_id(0) == 0)
        def _():
            dsh_ref[...] = jnp.zeros_like(dsh_ref)
            dsc_ref[...] = jnp.zeros_like(dsc_ref)
            dg_ref[...] = jnp.zeros_like(dg_ref)

        dh = None
        for j in range(4):
            for half in range(2):
                term = _tt(p_ref[2 * j + half], w_ref[j, :, half * DA:(half + 1) * DA])
                dh = term if dh is None else dh + term
        scale = m_ref[:, D:2 * D] + b_ref[:, D:2 * D]
        dx, dshift, dscale, dg = _norm_mod_bwd(x_ref[...], dh, g_ref[...], scale)
        gx_ref[...] = dy_ref[...] + dx
        dsh_ref[...] += dshift
        dsc_ref[...] += dscale
        dg_ref[...] += dg

    row = lambda i: (i, 0)
    fixed = lambda i: (0, 0)
    vec = SDS((1, D), f32)
    return _hbm_call(
        body, name="dh_grad_x", out_shape=(SDS((S, D), f32), vec, vec, vec), grid=(S // tm,),
        in_specs=[pl.BlockSpec((8, tm, DA), lambda i: (0, i, 0)), pl.BlockSpec((4, D, D), lambda i: (0, 0, 0)),
                  pl.BlockSpec((tm, D), row), pl.BlockSpec((tm, D), row), pl.BlockSpec((1, D), fixed),
                  pl.BlockSpec((1, 3 * D), fixed), pl.BlockSpec((1, 3 * D), fixed)],
        out_specs=(pl.BlockSpec((tm, D), row), pl.BlockSpec((1, D), fixed), pl.BlockSpec((1, D), fixed),
                   pl.BlockSpec((1, D), fixed)),
        compiler_params=_cp(56, dimension_semantics=("arbitrary",)),
    )(dp8, w4, xx, dy, norm_g, mrow, b_ada)


def _dhc_sums(dkc_raw, dvc_m, w4, ctx2, norm_g, mrow_c, b_ada):
    def body(dk_ref, dv_ref, w0_ref, w1_ref, x_ref, g_ref, m_ref, b_ref, dsh_ref, dsc_ref, dg_ref):
        dh = _tt(dk_ref[...], w0_ref[:, DA:2 * DA]) + _tt(dv_ref[...], w1_ref[:, 0:DA])
        scale = m_ref[:, D:2 * D] + b_ref[:, D:2 * D]
        _, dshift, dscale, dg = _norm_mod_bwd(x_ref[...], dh, g_ref[...], scale)
        dsh_ref[...] = dshift
        dsc_ref[...] = dscale
        dg_ref[...] = dg

    fixed = lambda i: (0, 0)
    vec = SDS((1, D), f32)
    vspec = pl.BlockSpec((1, D), fixed)
    return _hbm_call(
        body, name="dhc_sums", out_shape=(vec, vec, vec), grid=(1,),
        in_specs=[pl.BlockSpec((L, DA), fixed), pl.BlockSpec((L, DA), fixed),
                  pl.BlockSpec((None, D, D), lambda i: (0, 0, 0)), pl.BlockSpec((None, D, D), lambda i: (1, 0, 0)),
                  pl.BlockSpec((L, D), fixed), vspec, pl.BlockSpec((1, 3 * D), fixed), pl.BlockSpec((1, 3 * D), fixed)],
        out_specs=(vspec, vspec, vspec), compiler_params=_cp(32),
    )(dkc_raw, dvc_m, w4, w4, ctx2, norm_g, mrow_c, b_ada)


def _rope_tables():
    nf = DH // 4
    inv = np.float32(ROPE_THETA) ** (-np.arange(nf, dtype=np.float32) / np.float32(nf))
    ang_c = np.arange(GW, dtype=np.float32)[:, None] * inv
    ang_r = np.arange(ROWS, dtype=np.float32)[:, None] * inv
    zc, zr = np.zeros((GW, 2 * nf), np.float32), np.zeros((ROWS, 2 * nf), np.float32)
    ct_cos = np.tile(np.concatenate([zc, np.cos(ang_c), np.cos(ang_c)], axis=1), (1, H))
    ct_sin = np.tile(np.concatenate([zc, -np.sin(ang_c), np.sin(ang_c)], axis=1), (1, H))
    rt_cos = np.tile(np.concatenate([np.cos(ang_r), np.cos(ang_r), zr], axis=1), (1, H))
    rt_sin = np.tile(np.concatenate([-np.sin(ang_r), np.sin(ang_r), zr], axis=1), (1, H))
    rep8 = lambda t: np.ascontiguousarray(np.broadcast_to(t[:, None, :], (ROWS, 8, DA))).reshape(ROWS * 8, DA)
    return tuple(jnp.asarray(t, f32) for t in (ct_cos, rep8(rt_cos), ct_sin, rep8(rt_sin)))


def _local_step(xx, ctx2, tgt, mrow, mrow_c, b_ada, norm_g, weights, q_norm_g, k_norm_g, rpb2, conv_w_full, conv_b,
                mid=None):
    gq = jnp.tile(q_norm_g, (1, H))
    gk = jnp.tile(k_norm_g, (1, H))
    rope = _rope_tables()
    rpb_pad = jnp.pad(rpb2[:, :, ::-1], ((0, 0), (0, 0), (0, 128 - N_DC)))

    h = _prenorm(xx, norm_g, mrow, b_ada, 256, "prenorm_x")
    hc = _prenorm(ctx2, norm_g, mrow_c, b_ada, L, "prenorm_ctx")
    btb = _bias_prep(rpb_pad)
    jv = weights["jvec"]
    p = _in_proj_own(h, weights["own"], jv)
    w4 = weights["near"](p)
    p = _in_proj_block(h, w4, p, jv ^ 1, "in_proj_y")
    p = _in_proj_block(h, w4, p, jv ^ 2, "in_proj_x")
    w4 = weights["far"](w4, p)
    p = _in_proj_block(h, w4, p, jv ^ 3, "in_proj_xy")
    pc = _ctx_proj(hc, w4)
    qr, qp, kr, vh = _qk_prep(p, gq, gk, rope)
    kc, vc = _ctx_prep(pc, gk)
    o = _attn_fwd(qr, qp, kr, vh, kc, vc, btb)
    conv_g = _conv_fwd(p, conv_w_full, conv_b)
    w_out_full = weights["out"](o)
    dy, dconv, dp8, do, g_w_out, dgate, loss_sum = _out_proj_loss(o, p, conv_g, w_out_full, xx, tgt, mrow, b_ada)
    dp8, g_conv_w, g_conv_b = _conv_bwd(dconv, p, conv_w_full, conv_b, dp8)
    dqr, dqp, dkr, dvh, dkc, dvc, dbtb = _attn_bwd(qr, qp, kr, vh, kc, vc, btb, do)
    g_rpb = _bias_bwd(dbtb)
    dp8, g_gq, g_gk = _qk_bwd(dqr, dqp, dkr, dvh, p, gq, gk, rope, dp8)
    dkc_raw, dvc_m, g_gk_c = _ctx_bwd(dkc, dvc, pc, gk)
    g_w_in = _grad_w_in(h, dp8, hc, dkc_raw, dvc_m)
    b_ada_late = b_ada if mid is None else mid(g_w_in, g_w_out, b_ada)
    grad_x, dshift, dscale, dng = _dh_grad_x(dp8, w4, xx, dy, norm_g, mrow, b_ada_late)
    dshift_c, dscale_c, dng_c = _dhc_sums(dkc_raw, dvc_m, w4, ctx2, norm_g, mrow_c, b_ada_late)
    return dict(loss_sum=loss_sum, grad_x=grad_x, g_w_in=g_w_in, g_w_out=g_w_out, g_conv_w=g_conv_w,
                g_conv_b=g_conv_b, g_rpb=g_rpb, g_gq=g_gq, g_gk=g_gk, g_gk_c=g_gk_c, dshift=dshift, dscale=dscale,
                dgate=dgate, dng=dng, dshift_c=dshift_c, dscale_c=dscale_c, dng_c=dng_c)


def _pair_sum_w_in(g, r, cvec):
    tr = 128

    def body(c_ref, g_ref, r_ref, t32_ref, tb_ref):
        t = g_ref[...] + r_ref[...]
        t32_ref[...] = t
        tb_ref[...] = t.astype(bf16)

    half = D // 2
    g_spec = pl.BlockSpec((4, tr, D), lambda i, c: (0, c[0] * (half // tr) + i, 0))
    o_spec = pl.BlockSpec((4, tr, D), lambda i, c: (0, i, 0))
    grid_spec = pltpu.PrefetchScalarGridSpec(num_scalar_prefetch=1, grid=(half // tr,), in_specs=[g_spec, o_spec],
                                             out_specs=(o_spec, o_spec))
    return _hbm_call(body, name="pair_sum_w_in", out_shape=(SDS((4, half, D), f32), SDS((4, half, D), bf16)),
                     grid_spec=grid_spec, compiler_params=_cp(40))(cvec, g, r)


def _pair_sum_w_out(g, r, cvec):
    hr = D // 8

    def body(c_ref, g0, g1, g2, g3, r_ref, t32_ref, tb_ref):
        for q, g_ref in enumerate((g0, g1, g2, g3)):
            t = g_ref[...] + r_ref[q]
            t32_ref[q] = t
            tb_ref[q] = t.astype(bf16)

    gspecs = [pl.BlockSpec((hr, D), lambda i, c, q=q: (2 * q + c[0], 0)) for q in range(4)]
    full = pl.BlockSpec((4, hr, D), lambda i, c: (0, 0, 0))
    grid_spec = pltpu.PrefetchScalarGridSpec(num_scalar_prefetch=1, grid=(1,), in_specs=gspecs + [full],
                                             out_specs=(full, full))
    return _hbm_call(body, name="pair_sum_w_out", out_shape=(SDS((4, hr, D), f32), SDS((4, hr, D), bf16)),
                          grid_spec=grid_spec)(cvec, g, g, g, g, r)


def _chip_sum(t32, r2, jvec, name):
    rows = t32.shape[1]
    tr = min(rows, 128)

    def body(j_ref, t_ref, r_ref, u_ref):
        u_ref[...] = ((t_ref[...] + r_ref[0].astype(f32)) + r_ref[1].astype(f32)) + r_ref[2].astype(f32)

    grid_spec = pltpu.PrefetchScalarGridSpec(
        num_scalar_prefetch=1, grid=(rows // tr,),
        in_specs=[pl.BlockSpec((None, tr, D), lambda i, j: (j[0], i, 0)), pl.BlockSpec((3, tr, D), lambda i, j: (0, i, 0))],
        out_specs=pl.BlockSpec((tr, D), lambda i, j: (i, 0)))
    return _hbm_call(body, name=name, out_shape=SDS((rows, D), f32), grid_spec=grid_spec)(jvec, t32, r2)


_PK = {}
_off = 0
for _name, _rows in (("dm", 24), ("dmc", 24), ("dng", 8), ("dng_c", 8), ("gq", 8), ("gk", 8), ("gk_c", 8),
                     ("rpb", H * N_DR), ("conv_b", 8), ("conv_w", 16), ("loss", 8)):
    _PK[_name] = (_off, _off + _rows)
    _off += _rows
PK_ROWS = _off
RS_B_ADA, RS_NORM_G, RS_GQ, RS_GK, RS_RPB, RS_CONV_B, RS_CONV_W, RS_DMC, RS_LOSS, RS_ROWS = (
    0, 24, 32, 40, 48, 168, 176, 192, 216, 224)


def _small_reduce(gathered):
    def body(g_ref, o_ref, dm_ref):
        a0 = _PK["dm"][0]
        dm_ref[...] = jnp.zeros_like(dm_ref)
        for b in range(8):
            for i in range(24):
                dm_ref[b:b + 1, 128 * i:128 * (i + 1)] = g_ref[b, a0 + i:a0 + i + 1, :]
        tot = g_ref[0]
        for b in range(1, 8):
            tot = tot + g_ref[b]

        def rows(name):
            a, z = _PK[name]
            return tot[a:z]

        o_ref[RS_B_ADA:RS_B_ADA + 24] = rows("dm") + rows("dmc")
        o_ref[RS_NORM_G:RS_NORM_G + 8] = rows("dng") + rows("dng_c")
        gq = jnp.broadcast_to(jnp.sum(rows("gq"), axis=0, keepdims=True), (8, 128))
        gk = jnp.broadcast_to(jnp.sum(rows("gk") + rows("gk_c"), axis=0, keepdims=True), (8, 128))
        o_ref[RS_GQ:RS_GQ + 8] = gq + pltpu.roll(gq, DH, 1)
        o_ref[RS_GK:RS_GK + 8] = gk + pltpu.roll(gk, DH, 1)
        o_ref[RS_RPB:RS_RPB + H * N_DR] = rows("rpb")
        o_ref[RS_CONV_B:RS_CONV_B + 8] = rows("conv_b")
        o_ref[RS_CONV_W:RS_CONV_W + 16] = rows("conv_w")
        dmc = rows("dmc")
        o_ref[RS_DMC:RS_DMC + 24] = dmc
        o_ref[RS_LOSS:RS_LOSS + 8] = rows("loss")
        for i in range(24):
            dm_ref[8:9, 128 * i:128 * (i + 1)] = dmc[i:i + 1]

    return _hbm_call(body, name="small_reduce", out_shape=(SDS((RS_ROWS, 128), f32), SDS((16, 3 * D), f32)),
                     in_specs=[VMEM_SPEC], out_specs=(VMEM_SPEC, VMEM_SPEC))(gathered)


def _w_ada_grad(sc16, dm16, w_ada_shard, jvec):
    ncol = w_ada_shard.shape[1]

    def body(j_ref, sc_ref, dm_ref, w_ref, g_ref, part_ref):
        dm = dm_ref[...]
        g_ref[...] = lax.dot_general(sc_ref[...], dm, (((0,), (0,)), ((), ())), precision=HIGHEST,
                                     preferred_element_type=f32)
        part_ref[...] = lax.dot_general(dm[8:16], w_ref[...], (((1,), (1,)), ((), ())), precision=HIGHEST,
                                        preferred_element_type=f32)

    fixed = lambda i, j: (0, 0)
    grid_spec = pltpu.PrefetchScalarGridSpec(
        num_scalar_prefetch=1, grid=(1,),
        in_specs=[pl.BlockSpec((16, D), fixed), pl.BlockSpec((16, ncol), lambda i, j: (0, j[0])),
                  pl.BlockSpec((D, ncol), fixed)],
        out_specs=(pl.BlockSpec((D, ncol), fixed), pl.BlockSpec((8, D), fixed)))
    return _pallas_call(body, name="w_ada_grad", out_shape=(SDS((D, ncol), f32), SDS((8, D), f32)),
                        grid_spec=grid_spec, compiler_params=_cp(40))(jvec, sc16, dm16, w_ada_shard)


def _c_ctx_grad(parts4, c_ctx_row):
    def body(p_ref, c_ref, o_ref):
        tot = ((p_ref[0] + p_ref[1]) + p_ref[2]) + p_ref[3]
        o_ref[...] = tot[0:1] * _dsilu(c_ref[...])

    return _hbm_call(body, name="c_ctx_grad", out_shape=SDS((1, D), f32), in_specs=[VMEM_SPEC, VMEM_SPEC],
                          out_specs=VMEM_SPEC)(parts4, c_ctx_row)


def _adamw(w, g, m, v, name):
    rows, cols = w.shape
    tr = 256 if rows % 256 == 0 else rows

    def body(w_ref, g_ref, m_ref, v_ref, d_ref, m2_ref, v2_ref):
        gv = g_ref[...]
        m2 = ADAM_B1 * m_ref[...] + (1.0 - ADAM_B1) * gv
        v2 = ADAM_B2 * v_ref[...] + (1.0 - ADAM_B2) * jnp.square(gv)
        m_hat = m2 / (1.0 - ADAM_B1 ** ADAM_STEP)
        v_hat = v2 / (1.0 - ADAM_B2 ** ADAM_STEP)
        d_ref[...] = -ADAM_LR * (m_hat / (jnp.sqrt(v_hat) + ADAM_EPS) + ADAM_WD * w_ref[...])
        m2_ref[...] = m2
        v2_ref[...] = v2

    spec = pl.BlockSpec((tr, cols), lambda i: (i, 0))
    shp = SDS((rows, cols), f32)
    return _hbm_call(body, name=name, out_shape=(shp, shp, shp), grid=(rows // tr,), in_specs=[spec] * 4,
                          out_specs=(spec, spec, spec))(w, g, m, v)


def _adamw_halves(w, g_mine, g_other, m, v, cvec, name):
    rows, cols = w.shape
    half = rows // 2
    tr = min(256, half)
    per_half = half // tr

    def body(c_ref, w_ref, ga_ref, gb_ref, m_ref, v_ref, g_ref, d_ref, m2_ref, v2_ref):
        in_my_half = (pl.program_id(0) // per_half) == c_ref[0]
        gv = jnp.where(in_my_half, ga_ref[...], gb_ref[...])
        g_ref[...] = gv
        m2 = ADAM_B1 * m_ref[...] + (1.0 - ADAM_B1) * gv
        v2 = ADAM_B2 * v_ref[...] + (1.0 - ADAM_B2) * jnp.square(gv)
        m_hat = m2 / (1.0 - ADAM_B1 ** ADAM_STEP)
        v_hat = v2 / (1.0 - ADAM_B2 ** ADAM_STEP)
        d_ref[...] = -ADAM_LR * (m_hat / (jnp.sqrt(v_hat) + ADAM_EPS) + ADAM_WD * w_ref[...])
        m2_ref[...] = m2
        v2_ref[...] = v2

    full = pl.BlockSpec((tr, cols), lambda i, c: (i, 0))
    part = pl.BlockSpec((tr, cols), lambda i, c: (i % per_half, 0))
    shp = SDS((rows, cols), f32)
    grid_spec = pltpu.PrefetchScalarGridSpec(num_scalar_prefetch=1, grid=(rows // tr,),
                                             in_specs=[full, part, part, full, full], out_specs=(full,) * 4)
    return _hbm_call(body, name=name, out_shape=(shp,) * 4, grid_spec=grid_spec)(cvec, w, g_mine, g_other, m, v)


def _adam_math(w, g, m, v):
    m2 = ADAM_B1 * m + (1.0 - ADAM_B1) * g
    v2 = ADAM_B2 * v + (1.0 - ADAM_B2) * jnp.square(g)
    m_hat = m2 / (1.0 - ADAM_B1 ** ADAM_STEP)
    v_hat = v2 / (1.0 - ADAM_B2 ** ADAM_STEP)
    return -ADAM_LR * (m_hat / (jnp.sqrt(v_hat) + ADAM_EPS) + ADAM_WD * w), m2, v2


def _adamw_small(red, g_c_ctx, jvec, ws, ms, vs):
    n = len(ws)

    def body(*refs):
        red_ref, gc_ref, j_ref = refs[:3]
        w_refs, m_refs, v_refs = refs[3:3 + n], refs[3 + n:3 + 2 * n], refs[3 + 2 * n:3 + 3 * n]
        outs = refs[3 + 3 * n:]
        g_out, d_out, m_out, v_out = outs[:n], outs[n:2 * n], outs[2 * n:3 * n], outs[3 * n:]
        chip = j_ref[0]
        lanes = lambda i: (slice(None), slice(128 * i, 128 * (i + 1)))
        row = lambda r0, i: (lambda: red_ref[r0 + i:r0 + i + 1, :])
        whole = (slice(None), slice(None))
        chunks = [
            [(whole, lambda: gc_ref[...])],
            [(lanes(i), row(RS_B_ADA, i)) for i in range(3 * D // 128)],
            [(lanes(i), row(RS_NORM_G, i)) for i in range(D // 128)],
            [(whole, lambda: red_ref[RS_GQ:RS_GQ + 1, 0:DH])],
            [(whole, lambda: red_ref[RS_GK:RS_GK + 1, 0:DH])],
            [((0, h), (lambda h=h: red_ref[RS_RPB + N_DR * h:RS_RPB + N_DR * (h + 1), 0:N_DC])) for h in range(H)],
            [((0, slice(r, r + 1), slice(None)), (lambda r=r: red_ref[pl.ds(RS_CONV_W + 4 * r + chip, 1), :]))
             for r in range(3)],
            [(lanes(i), row(RS_CONV_B, i)) for i in range(DC // 128)],
        ]
        for a in range(n):
            for idx, grad in chunks[a]:
                g = grad()
                d, m2, v2 = _adam_math(w_refs[a][idx], g, m_refs[a][idx], v_refs[a][idx])
                g_out[a][idx] = g
                d_out[a][idx] = d
                m_out[a][idx] = m2
                v_out[a][idx] = v2

    shapes = [SDS(w.shape, f32) for w in ws]
    res = _pallas_call(body, name="adamw_small", out_shape=shapes * 4,
                       in_specs=[VMEM_SPEC, VMEM_SPEC, SMEM_SPEC] + [VMEM_SPEC] * (3 * n),
                       out_specs=[VMEM_SPEC] * (4 * n))(red, g_c_ctx, jvec, *ws, *ms, *vs)
    return [list(res[k * n:(k + 1) * n]) for k in range(4)]


def _rows128(a):
    return a.reshape(-1, 128)


def _pad_lanes(a):
    a2 = a.reshape(-1, a.shape[-1])
    return jnp.pad(a2, ((0, 0), (0, 128 - a2.shape[1])))


def kernel(x, c, ctx, c_ctx, w_ada, b_ada, norm_g, w_in, q_norm_g, k_norm_g, rpb, conv_w, conv_b, w_out, loss_target, m_c_ctx, m_w_ada, m_b_ada, m_norm_g, m_w_in, m_q_norm_g, m_k_norm_g, m_rpb, m_conv_w, m_conv_b, m_w_out, v_c_ctx, v_w_ada, v_b_ada, v_norm_g, v_w_in, v_q_norm_g, v_k_norm_g, v_rpb, v_conv_w, v_conv_b, v_w_out):
    xi, yi, ci = lax.axis_index("x"), lax.axis_index("y"), lax.axis_index("c")
    dev = 4 * xi + 2 * yi + ci
    chip = 2 * xi + yi
    cvec = jnp.reshape(ci, (1,)).astype(i32)
    jvec = jnp.reshape(chip, (1,)).astype(i32)
    w_ada_s = w_ada[0]
    ncol = w_ada_s.shape[1]

    c8 = _all_gather8(c.reshape(8, 128), "gather_c").reshape(8, D)
    cc = jnp.concatenate([c8, c_ctx.reshape(1, D), jnp.zeros((7, D), f32)], axis=0)
    m_shard, sc16 = _adaln_shard(cc, w_ada_s)

    conv_w_pad = jnp.pad(conv_w[0], ((0, 5), (0, 0)))
    m4, cw4 = _chip_gather([m_shard, conv_w_pad], "gather_mod")

    all_k = [(0, 0, 0), (0, 0, 1), (0, 0, 2)]
    wo4c = _cast_to_slot(w_out[0], jvec, "cast_w_out")
    sem_a, rem_a, w4s, token = _halves_start([_cast_to_slot(w_in[0], jvec, "cast_w_in")], m4, all_k[0:2],
                                             "weights_near_start")
    m_full = jnp.transpose(m4, (1, 0, 2)).reshape(16, 4 * ncol) + token[0:1, 0:1]
    mrow = lax.dynamic_slice(m_full, (dev, 0), (1, 3 * D))
    mrow_c = m_full[8:9]
    conv_w_full = jnp.transpose(cw4[:, 0:3, :], (1, 0, 2)).reshape(3, DC)
    waves = {}

    def near(after):
        (w4w,) = _halves_wait(sem_a, rem_a, [w4s], after, all_k[0:2], "weights_near_wait")
        w4f = _halves_forward([w4w], [0, 1], "weights_near_forward")[0]
        sem_b, rem_b, w4b, _ = _halves_start([w4f], after, all_k[2:3], "weights_far_start")
        waves["far"] = (sem_b, rem_b)
        return w4b

    def far(w4, after):
        (w4w,) = _halves_wait(*waves["far"], [w4], after, all_k[2:3], "weights_far_wait")
        w4f = _halves_forward([w4w], [2], "weights_far_forward")[0]
        sem_c, rem_c, wo4s, _ = _halves_start([wo4c], w4f, all_k, "weights_out_start")
        waves["out"] = (sem_c, rem_c, wo4s)
        return w4f

    def w_out_gathered(after):
        sem_c, rem_c, wo4s = waves["out"]
        (wow,) = _halves_wait(sem_c, rem_c, [wo4s], after, all_k, "weights_out_wait")
        return _halves_forward([wow], [0, 1, 2], "weights_out_forward")[0].reshape(D, D)

    weights = dict(own=w_in[0], jvec=jvec, near=near, far=far, out=w_out_gathered)

    exchange = _exchange_copies
    pending = {}

    def mid(g_w_in, g_w_out, b_ada_in):
        shapes = [SDS((4, D // 2, D), f32), SDS((4, D // 8, D), f32)]
        out = _split_start([g_w_in, g_w_out], shapes, 8, exchange, "grad_pair_start")
        pending["ex"] = (out[0], out[1], list(out[2:4]), list(out[4:6]))
        return b_ada_in + out[6][0:1, 0:1]

    r = _local_step(x[0], ctx[0], loss_target[0], mrow, mrow_c, b_ada, norm_g, weights, q_norm_g, k_norm_g,
                    rpb[0], conv_w_full, conv_b, mid)
    dm = jnp.concatenate([r["dshift"], r["dscale"], r["dgate"]], axis=1)
    dmc = jnp.concatenate([r["dshift_c"], r["dscale_c"], jnp.zeros((1, D), f32)], axis=1)
    pack_parts = [_rows128(dm), _rows128(dmc), _rows128(r["dng"]), _rows128(r["dng_c"]), _rows128(r["g_gq"]),
                  _rows128(r["g_gk"]), _rows128(r["g_gk_c"]), r["g_rpb"].reshape(H * N_DR, 128),
                  _rows128(r["g_conv_b"]), _rows128(r["g_conv_w"][0:3]), jnp.pad(r["loss_sum"], ((0, 0), (0, 127)))]
    pack = jnp.concatenate([jnp.pad(p, ((0, -p.shape[0] % 8), (0, 0))) for p in pack_parts], axis=0)
    gathered = _all_gather8(pack, "gather_small")

    ex_ssem, ex_rsem, ex_srcs, ex_lands = pending["ex"]
    ex_g, ex = _split_wait(ex_ssem, ex_rsem, ex_srcs, ex_lands, gathered, exchange, "grad_pair_wait")
    t32, tb = _pair_sum_w_in(ex_g[0], ex[0], cvec)
    to32, tob = _pair_sum_w_out(ex_g[1], ex[1], cvec)
    sc_out = _split_start([tb, tob], [SDS((3, D // 2, D), bf16), SDS((3, D // 8, D), bf16)], 6, _scatter_copies,
                          "grad_chip_start")
    sc16 = sc16 + sc_out[6][0:1, 0:1]

    assert pack.shape[0] == PK_ROWS
    red, dm16 = _small_reduce(gathered)
    loss = red[RS_LOSS, 0] * (0.5 / D)

    g_w_ada_s, cpart = _w_ada_grad(sc16, dm16, w_ada_s, jvec)
    d_w_ada, nm_w_ada, nv_w_ada = _adamw(w_ada_s, g_w_ada_s, m_w_ada[0], v_w_ada[0], "adamw_w_ada")

    (cparts4,) = _chip_gather([cpart], "gather_c_ctx_parts", after=nm_w_ada)
    g_c_ctx = _c_ctx_grad(cparts4, c_ctx.reshape(1, D))

    _, (r2, ro2) = _split_wait(sc_out[0], sc_out[1], [sc_out[2], sc_out[3]], [sc_out[4], sc_out[5]], g_c_ctx,
                               _scatter_copies, "grad_chip_wait")
    u_in = _chip_sum(t32, r2, jvec, "chip_sum_w_in")
    u_out = _chip_sum(to32, ro2, jvec, "chip_sum_w_out")

    as_row = lambda a: a.reshape(1, D)
    small = _adamw_small(
        red, g_c_ctx, jvec,
        [as_row(c_ctx), b_ada, norm_g, q_norm_g, k_norm_g, rpb, conv_w, conv_b],
        [as_row(m_c_ctx), m_b_ada, m_norm_g, m_q_norm_g, m_k_norm_g, m_rpb, m_conv_w, m_conv_b],
        [as_row(v_c_ctx), v_b_ada, v_norm_g, v_q_norm_g, v_k_norm_g, v_rpb, v_conv_w, v_conv_b])

    o_in, o_out = _sibling_send([u_in, u_out], "grad_pair_send")
    g_w_in_s, d_w_in, nm_w_in, nv_w_in = _adamw_halves(w_in[0], u_in, o_in, m_w_in[0], v_w_in[0], cvec, "adamw_w_in")
    g_w_out_s, d_w_out, nm_w_out, nv_w_out = _adamw_halves(w_out[0], u_out, o_out, m_w_out[0], v_w_out[0], cvec,
                                                           "adamw_w_out")

    def ordered(kind, big_w_ada, big_w_in, big_w_out):
        s_c_ctx, s_b_ada, s_norm_g, s_q, s_k, s_rpb, s_conv_w, s_conv_b = small[kind]
        return [s_c_ctx.reshape(D), big_w_ada[None], s_b_ada, s_norm_g, big_w_in[None], s_q, s_k, s_rpb, s_conv_w,
                s_conv_b, big_w_out[None]]

    grads = ordered(0, g_w_ada_s, g_w_in_s, g_w_out_s)
    deltas = ordered(1, d_w_ada, d_w_in, d_w_out)
    new_m = ordered(2, nm_w_ada, nm_w_in, nm_w_out)
    new_v = ordered(3, nv_w_ada, nv_w_in, nv_w_out)
    return (loss, r["grad_x"][None], *grads, *deltas, *new_m, *new_v)
```

```python
import functools

import jax
import jax.numpy as jnp
import numpy as np
from jax import lax
from jax.experimental import pallas as pl
from jax.experimental.pallas import tpu as pltpu

f32, bf16, i32 = jnp.float32, jnp.bfloat16, jnp.int32
MESH = pl.DeviceIdType.MESH
HIGHEST = lax.Precision.HIGHEST

D = 1024
S = 2048
L = 256
GW = 64
ROWS = S // GW
H = 8
DH = 64
DA = H * DH
DC = 512
WIN_H, WIN_W = 8, 16
N_DR, N_DC = 2 * WIN_H - 1, 2 * WIN_W - 1
RMS_EPS = 1e-6
ROPE_THETA = 10000.0
QK_SCALE = DH ** -0.5
NEG = -1e30

QB = 128
NQB = S // QB
KR = 9
KB = KR * GW
TILE_GEOM = ((0, 0), (2, 0), (4, 0), (28, 23), (30, 23))
NT = len(TILE_GEOM)

ADAM_LR, ADAM_B1, ADAM_B2, ADAM_EPS, ADAM_WD, ADAM_STEP = 0.001, 0.9, 0.999, 1e-08, 0.01, 10

VMEM_SPEC = pl.BlockSpec(memory_space=pltpu.VMEM)
ANY_SPEC = pl.BlockSpec(memory_space=pl.ANY)
SMEM_SPEC = pl.BlockSpec(memory_space=pltpu.SMEM)
SDS = jax.ShapeDtypeStruct


_pallas_call = pl.pallas_call


def _hbm_call(body, *, out_shape, in_specs=None, out_specs=None, grid_spec=None, **kw):
    n_pre = 0
    if grid_spec is not None:
        ispecs, ospecs, n_pre = grid_spec.in_specs, grid_spec.out_specs, grid_spec.num_scalar_prefetch
        kw["grid_spec"] = grid_spec
    else:
        ispecs, ospecs = in_specs, out_specs
        kw.update(in_specs=in_specs, out_specs=out_specs)

    def blocked(spec):
        return isinstance(spec, pl.BlockSpec) and spec.block_shape is not None

    single = not isinstance(out_shape, (tuple, list))
    shapes = [out_shape] if single else list(out_shape)
    ospec_list = list(ospecs) if isinstance(ospecs, (tuple, list)) else [ospecs]
    shapes = [pltpu.HBM(s.shape, s.dtype) if blocked(sp) else s for s, sp in zip(shapes, ospec_list)]
    call = _pallas_call(body, out_shape=shapes[0] if single else tuple(shapes), **kw)

    def run(*args):
        arrays = [pltpu.with_memory_space_constraint(a, pltpu.HBM) if blocked(sp) else a
                  for a, sp in zip(args[n_pre:], ispecs)]
        return call(*args[:n_pre], *arrays)

    return run


def _cp(vmem_mb=None, **kw):
    if vmem_mb is not None:
        kw["vmem_limit_bytes"] = vmem_mb << 20
    return pltpu.CompilerParams(**kw)


def _silu(z):
    return z * jax.nn.sigmoid(z)


def _dsilu(z):
    sg = jax.nn.sigmoid(z)
    return sg * (1.0 + z * (1.0 - sg))


def _row_start(i):
    return min(max(i - WIN_H // 2, 0), ROWS - WIN_H)


def _my_pos():
    return lax.axis_index("x"), lax.axis_index("y"), lax.axis_index("c")


def _flip(v, bit):
    return 1 - v if bit else v


def _all_gather8(xin, name):
    R, N = xin.shape

    def body(x_ref, o_ref, ssem, rsem, lsem):
        x, y, c = _my_pos()
        me = 4 * x + 2 * y + c
        own = pltpu.make_async_copy(x_ref, o_ref.at[me], lsem)
        own.start()
        sends = []
        for k in range(1, 8):
            tgt = (_flip(x, (k >> 2) & 1), _flip(y, (k >> 1) & 1), _flip(c, k & 1))
            cp = pltpu.make_async_remote_copy(src_ref=x_ref, dst_ref=o_ref.at[me], send_sem=ssem.at[k - 1],
                                              recv_sem=rsem.at[k - 1], device_id=tgt, device_id_type=MESH)
            cp.start()
            sends.append(cp)
        for k in range(1, 8):
            tgt = (_flip(x, (k >> 2) & 1), _flip(y, (k >> 1) & 1), _flip(c, k & 1))
            peer = 4 * tgt[0] + 2 * tgt[1] + tgt[2]
            pltpu.make_async_remote_copy(src_ref=x_ref, dst_ref=o_ref.at[peer], send_sem=ssem.at[k - 1],
                                         recv_sem=rsem.at[k - 1], device_id=tgt, device_id_type=MESH).wait_recv()
        for cp in sends:
            cp.wait_send()
        own.wait()

    return _hbm_call(
        body, name=name, out_shape=SDS((8, R, N), xin.dtype), in_specs=[VMEM_SPEC], out_specs=VMEM_SPEC,
        scratch_shapes=[pltpu.SemaphoreType.DMA((7,)), pltpu.SemaphoreType.DMA((7,)), pltpu.SemaphoreType.DMA],
    )(xin)


def _chip_gather(smalls, name, after=None):
    ns = len(smalls)

    def body(*refs):
        s_in, s_out = refs[:ns], refs[ns + 1:2 * ns + 1]
        ssem, rsem, lsem = refs[2 * ns + 1:]
        x, y, c = _my_pos()
        j = 2 * x + y
        chips = _peer_chips(x, y, c)
        local = [pltpu.make_async_copy(s_in[a], s_out[a].at[j], lsem.at[a]) for a in range(ns)]
        for cp in local:
            cp.start()
        sends = []
        for a in range(ns):
            for k in range(3):
                cp = pltpu.make_async_remote_copy(src_ref=s_in[a], dst_ref=s_out[a].at[j], send_sem=ssem.at[3 * a + k],
                                                  recv_sem=rsem.at[3 * a + k], device_id=chips[k][0], device_id_type=MESH)
                cp.start()
                sends.append(cp)
        for a in range(ns):
            for k in range(3):
                pltpu.make_async_remote_copy(src_ref=s_in[a], dst_ref=s_out[a].at[chips[k][1]], send_sem=ssem.at[3 * a + k],
                                             recv_sem=rsem.at[3 * a + k], device_id=chips[k][0],
                                             device_id_type=MESH).wait_recv()
        for cp in sends:
            cp.wait_send()
        for cp in local:
            cp.wait()

    return _hbm_call(
        body, name=name, out_shape=[SDS((4,) + a.shape, a.dtype) for a in smalls],
        in_specs=[VMEM_SPEC] * ns + [ANY_SPEC], out_specs=[VMEM_SPEC] * ns,
        scratch_shapes=[pltpu.SemaphoreType.DMA((3 * ns,)), pltpu.SemaphoreType.DMA((3 * ns,)),
                        pltpu.SemaphoreType.DMA((ns,))],
    )(*smalls, smalls[0] if after is None else after)


HBM_SPEC = pl.BlockSpec(memory_space=pltpu.HBM)
SEM_SPEC = pl.BlockSpec(memory_space=pltpu.SEMAPHORE)
DATAFLOW = pltpu.SideEffectType.DATAFLOW_SIDE_EFFECTING


def _peer_chips(x, y, c):
    out = []
    for k in range(1, 4):
        px, py = _flip(x, (k >> 1) & 1), _flip(y, k & 1)
        out.append(((px, py, c), 2 * px + py))
    return out


def _half_copies(srcs, dsts, ssem, rsem, which):
    x, y, c = _my_pos()
    j = 2 * x + y
    peers = _peer_chips(x, y, c)
    pairs = []
    for pos, group, k in which:
        half = srcs[pos].shape[1] // 2
        mine = pl.ds(pl.multiple_of(c * half, 8), half)
        dev, pj = peers[k]
        sem = 3 * group + k
        send = pltpu.make_async_remote_copy(src_ref=srcs[pos].at[j, mine], dst_ref=dsts[pos].at[j, mine],
                                            send_sem=ssem.at[sem], recv_sem=rsem.at[sem], device_id=dev,
                                            device_id_type=MESH)
        arrive = pltpu.make_async_remote_copy(src_ref=srcs[pos].at[j, mine], dst_ref=dsts[pos].at[pj, mine],
                                              send_sem=ssem.at[sem], recv_sem=rsem.at[sem], device_id=dev,
                                              device_id_type=MESH)
        pairs.append((send, arrive))
    return pairs


def _halves_start(bigs, after, order, name):
    nb = len(bigs)

    def body(*refs):
        b_in = refs[:nb]
        ssem, rsem = refs[nb + 1], refs[nb + 2]
        b_out = refs[nb + 3:2 * nb + 3]
        token = refs[2 * nb + 3]
        for send, _ in _half_copies(b_in, b_out, ssem, rsem, order):
            send.start()
        token[...] = jnp.zeros_like(token)

    out_shape = (pltpu.SemaphoreType.DMA((3 * nb,)), pltpu.SemaphoreType.DMA((3 * nb,)),
                 *[pltpu.HBM(b.shape, b.dtype) for b in bigs], SDS((8, 128), f32))
    return _hbm_call(
        body, name=name, out_shape=out_shape, in_specs=[HBM_SPEC] * nb + [ANY_SPEC],
        out_specs=(SEM_SPEC, SEM_SPEC, *[HBM_SPEC] * nb, VMEM_SPEC),
        input_output_aliases={a: 2 + a for a in range(nb)}, compiler_params=_cp(has_side_effects=DATAFLOW),
    )(*[pltpu.with_memory_space_constraint(b, pltpu.HBM) for b in bigs], after)


def _halves_wait(ssem, rsem, bigs, after, which, name):
    nb = len(bigs)

    def body(*refs):
        b_in = refs[:nb]
        ssem_ref, rsem_ref = refs[nb], refs[nb + 1]
        for send, arrive in _half_copies(b_in, b_in, ssem_ref, rsem_ref, which):
            send.wait_send()
            arrive.wait_recv()

    return _hbm_call(
        body, name=name, out_shape=tuple(pltpu.HBM(b.shape, b.dtype) for b in bigs),
        in_specs=[HBM_SPEC] * nb + [SEM_SPEC, SEM_SPEC, ANY_SPEC], out_specs=tuple([HBM_SPEC] * nb),
        input_output_aliases={a: a for a in range(nb)}, compiler_params=_cp(has_side_effects=DATAFLOW),
    )(*bigs, ssem, rsem, after)


def _halves_forward(bigs, relations, name):
    nb, nr = len(bigs), len(relations)

    def body(*refs):
        b_in, b_out = refs[:nb], refs[nb:2 * nb]
        ssem, rsem = refs[2 * nb:]
        x, y, c = _my_pos()
        sib = (x, y, 1 - c)
        peers = _peer_chips(x, y, c)
        sends = []
        for a in range(nb):
            half = b_in[a].shape[1] // 2
            mine = pl.ds(pl.multiple_of(c * half, 8), half)
            for i, k in enumerate(relations):
                pj = peers[k][1]
                cp = pltpu.make_async_remote_copy(src_ref=b_in[a].at[pj, mine], dst_ref=b_out[a].at[pj, mine],
                                                  send_sem=ssem.at[nr * a + i], recv_sem=rsem.at[nr * a + i],
                                                  device_id=sib, device_id_type=MESH)
                cp.start()
                sends.append(cp)
        for a in range(nb):
            half = b_in[a].shape[1] // 2
            other = pl.ds(pl.multiple_of((1 - c) * half, 8), half)
            for i, k in enumerate(relations):
                pj = peers[k][1]
                pltpu.make_async_remote_copy(src_ref=b_in[a].at[pj, other], dst_ref=b_out[a].at[pj, other],
                                             send_sem=ssem.at[nr * a + i], recv_sem=rsem.at[nr * a + i],
                                             device_id=sib, device_id_type=MESH).wait_recv()
        for cp in sends:
            cp.wait_send()

    return _hbm_call(
        body, name=name, out_shape=[SDS(b.shape, b.dtype) for b in bigs], in_specs=[ANY_SPEC] * nb,
        out_specs=[ANY_SPEC] * nb, input_output_aliases={a: a for a in range(nb)},
        scratch_shapes=[pltpu.SemaphoreType.DMA((nr * nb,)), pltpu.SemaphoreType.DMA((nr * nb,))],
    )(*bigs)


def _cast_to_slot(w, jvec, name):
    rows, cols = w.shape
    tr = 256

    def body(j_ref, w_ref, o_ref):
        o_ref[...] = w_ref[...].astype(bf16)

    grid_spec = pltpu.PrefetchScalarGridSpec(
        num_scalar_prefetch=1, grid=(rows // tr,), in_specs=[pl.BlockSpec((tr, cols), lambda i, j: (i, 0))],
        out_specs=pl.BlockSpec((None, tr, cols), lambda i, j: (j[0], i, 0)))
    return _hbm_call(body, name=name, out_shape=SDS((4, rows, cols), bf16), grid_spec=grid_spec)(jvec, w)


def _exchange_copies(srcs, lands, ssem, rsem):
    x, y, c = _my_pos()
    cps = []
    for a in range(len(srcs)):
        stacked = len(srcs[a].shape) == 3
        rb = srcs[a].shape[1] if stacked else srcs[a].shape[0] // 4
        half = rb // 2
        for jb in range(4):
            if stacked:
                src = srcs[a].at[jb, pl.ds(pl.multiple_of((1 - c) * half, 8), half)]
            else:
                src = srcs[a].at[pl.ds(pl.multiple_of(jb * rb + (1 - c) * half, 8), half)]
            cps.append(pltpu.make_async_remote_copy(src_ref=src, dst_ref=lands[a].at[jb], send_sem=ssem.at[4 * a + jb],
                                                    recv_sem=rsem.at[4 * a + jb], device_id=(x, y, 1 - c),
                                                    device_id_type=MESH))
    return cps


def _scatter_copies(srcs, lands, ssem, rsem):
    x, y, c = _my_pos()
    cps = []
    for a in range(len(srcs)):
        for k, (dev, pj) in enumerate(_peer_chips(x, y, c)):
            cps.append(pltpu.make_async_remote_copy(src_ref=srcs[a].at[pj], dst_ref=lands[a].at[k],
                                                    send_sem=ssem.at[3 * a + k], recv_sem=rsem.at[3 * a + k],
                                                    device_id=dev, device_id_type=MESH))
    return cps


def _split_start(srcs, land_shapes, n_cp, make, name):
    ns, nl = len(srcs), len(land_shapes)

    def body(*refs):
        s_in = refs[:ns]
        ssem, rsem = refs[ns + nl], refs[ns + nl + 1]
        l_out = refs[2 * ns + nl + 2:2 * ns + 2 * nl + 2]
        token = refs[2 * ns + 2 * nl + 2]
        for cp in make(s_in, l_out, ssem, rsem):
            cp.start()
        token[...] = jnp.zeros_like(token)

    lands = [pltpu.with_memory_space_constraint(lax.empty(sh.shape, sh.dtype), pltpu.HBM) for sh in land_shapes]
    out_shape = (pltpu.SemaphoreType.DMA((n_cp,)), pltpu.SemaphoreType.DMA((n_cp,)),
                 *[pltpu.HBM(b.shape, b.dtype) for b in srcs], *[pltpu.HBM(b.shape, b.dtype) for b in land_shapes],
                 SDS((8, 128), f32))
    return _hbm_call(
        body, name=name, out_shape=out_shape, in_specs=[HBM_SPEC] * (ns + nl),
        out_specs=(SEM_SPEC, SEM_SPEC, *[HBM_SPEC] * (ns + nl), VMEM_SPEC),
        input_output_aliases={i: 2 + i for i in range(ns + nl)}, compiler_params=_cp(has_side_effects=DATAFLOW),
    )(*[pltpu.with_memory_space_constraint(b, pltpu.HBM) for b in srcs], *lands)


def _split_wait(ssem, rsem, srcs, lands, after, make, name):
    ns, nl = len(srcs), len(lands)

    def body(*refs):
        s_in, l_in = refs[:ns], refs[ns:ns + nl]
        ssem_ref, rsem_ref = refs[ns + nl], refs[ns + nl + 1]
        for cp in make(s_in, l_in, ssem_ref, rsem_ref):
            cp.wait_send()
            cp.wait_recv()

    outs = _hbm_call(
        body, name=name, out_shape=tuple(pltpu.HBM(b.shape, b.dtype) for b in (*srcs, *lands)),
        in_specs=[HBM_SPEC] * (ns + nl) + [SEM_SPEC, SEM_SPEC, ANY_SPEC], out_specs=tuple([HBM_SPEC] * (ns + nl)),
        input_output_aliases={i: i for i in range(ns + nl)}, compiler_params=_cp(has_side_effects=DATAFLOW),
    )(*srcs, *lands, ssem, rsem, after)
    return list(outs[:ns]), list(outs[ns:])


def _sibling_send(halves, name):
    n = len(halves)

    def body(*refs):
        ins, outs = refs[:n], refs[n:2 * n]
        ssem, rsem = refs[2 * n:]
        x, y, c = _my_pos()
        cps = []
        for a in range(n):
            cp = pltpu.make_async_remote_copy(src_ref=ins[a], dst_ref=outs[a], send_sem=ssem.at[a],
                                              recv_sem=rsem.at[a], device_id=(x, y, 1 - c), device_id_type=MESH)
            cp.start()
            cps.append(cp)
        for cp in cps:
            cp.wait_recv()
        for cp in cps:
            cp.wait_send()

    out_shape = [SDS(h.shape, h.dtype) for h in halves]
    return _hbm_call(
        body, name=name, out_shape=out_shape, in_specs=[ANY_SPEC] * n, out_specs=[ANY_SPEC] * n,
        scratch_shapes=[pltpu.SemaphoreType.DMA((n,)), pltpu.SemaphoreType.DMA((n,))],
    )(*halves)


def _adaln_shard(cc, w_ada_shard):
    def body(c_ref, w_ref, m_ref, sc_ref):
        sc = _silu(c_ref[...])
        sc_ref[...] = sc
        m_ref[...] = jnp.dot(sc, w_ref[...], precision=HIGHEST, preferred_element_type=f32)

    return _hbm_call(
        body, name="adaln_shard", out_shape=(SDS((16, w_ada_shard.shape[1]), f32), SDS((16, D), f32)),
        in_specs=[VMEM_SPEC, VMEM_SPEC], out_specs=(VMEM_SPEC, VMEM_SPEC), compiler_params=_cp(32),
    )(cc, w_ada_shard)


def _prenorm(xx, norm_g, mrow, b_ada, tm, name):
    n = xx.shape[0]

    def body(x_ref, g_ref, m_ref, b_ref, h_ref):
        x = x_ref[...]
        shift = m_ref[:, 0:D] + b_ref[:, 0:D]
        scale = m_ref[:, D:2 * D] + b_ref[:, D:2 * D]
        r = lax.rsqrt(jnp.mean(x * x, axis=-1, keepdims=True) + RMS_EPS)
        y = (x * r) * g_ref[...]
        h_ref[...] = (y * (1.0 + scale) + shift).astype(bf16)

    row = lambda i: (i, 0)
    fixed = lambda i: (0, 0)
    return _hbm_call(
        body, name=name, out_shape=SDS((n, D), bf16), grid=(n // tm,),
        in_specs=[pl.BlockSpec((tm, D), row), pl.BlockSpec((1, D), fixed), pl.BlockSpec((1, 3 * D), fixed),
                  pl.BlockSpec((1, 3 * D), fixed)],
        out_specs=pl.BlockSpec((tm, D), row),
    )(xx, norm_g, mrow, b_ada)


def _in_proj_own(h, w_own, jvec):
    tm = 512

    def body(j_ref, h_ref, w_ref, p_ref):
        p_ref[...] = jnp.dot(h_ref[...], w_ref[...].astype(bf16), preferred_element_type=f32)

    grid_spec = pltpu.PrefetchScalarGridSpec(
        num_scalar_prefetch=1, grid=(S // tm,),
        in_specs=[pl.BlockSpec((tm, D), lambda i, j: (i, 0)), pl.BlockSpec((D, D), lambda i, j: (0, 0))],
        out_specs=pl.BlockSpec((tm, D), lambda i, j: (i, j[0])))
    return _hbm_call(body, name="in_proj_own", out_shape=SDS((S, 4 * D), f32), grid_spec=grid_spec,
                     compiler_params=_cp(40))(jvec, h, w_own)


def _in_proj_block(h, w4, p, bvec, name, after=None):
    tm = 512

    def body(b_ref, h_ref, w_ref, p_in_ref, after_ref, p_ref):
        p_ref[...] = jnp.dot(h_ref[...], w_ref[...], preferred_element_type=f32)

    grid_spec = pltpu.PrefetchScalarGridSpec(
        num_scalar_prefetch=1, grid=(S // tm,),
        in_specs=[pl.BlockSpec((tm, D), lambda i, b: (i, 0)), pl.BlockSpec((None, D, D), lambda i, b: (b[0], 0, 0)),
                  ANY_SPEC, ANY_SPEC],
        out_specs=pl.BlockSpec((tm, D), lambda i, b: (i, b[0])))
    return _hbm_call(body, name=name, out_shape=SDS((S, 4 * D), f32), grid_spec=grid_spec,
                     input_output_aliases={3: 0})(bvec, h, w4, p, bvec if after is None else after)


def _ctx_proj(hc, w4):
    def body(h_ref, w0_ref, w1_ref, p_ref):
        hv = h_ref[...]
        p_ref[:, 0:DA] = jnp.dot(hv, w0_ref[:, DA:2 * DA], preferred_element_type=f32)
        p_ref[:, DA:2 * DA] = jnp.dot(hv, w1_ref[:, 0:DA], preferred_element_type=f32)

    return _hbm_call(
        body, name="ctx_proj", out_shape=SDS((L, 2 * DA), f32), grid=(1,),
        in_specs=[pl.BlockSpec((L, D), lambda i: (0, 0)), pl.BlockSpec((None, D, D), lambda i: (0, 0, 0)),
                  pl.BlockSpec((None, D, D), lambda i: (1, 0, 0))],
        out_specs=pl.BlockSpec((L, 2 * DA), lambda i: (0, 0)),
    )(hc, w4, w4)


def _head_ones():
    r = lax.broadcasted_iota(i32, (DA, DA), 0) // DH
    c = lax.broadcasted_iota(i32, (DA, DA), 1) // DH
    return (r == c).astype(bf16)


def _head_sum(v, ones_bd):
    hi = v.astype(bf16)
    lo = (v - hi.astype(f32)).astype(bf16)
    return jnp.dot(hi, ones_bd, preferred_element_type=f32) + jnp.dot(lo, ones_bd, preferred_element_type=f32)


def _swap16(v):
    lane = lax.broadcasted_iota(i32, v.shape, 1)
    return jnp.where((lane & 31) < 16, pltpu.roll(v, DA - 16, 1), pltpu.roll(v, 16, 1))


def _rope_block(ct_ref, rt_ref, tm):
    rows = [jnp.tile(rt_ref[8 * j:8 * j + 8, :], (GW // 8, 1)) for j in range(tm // GW)]
    return jnp.tile(ct_ref[...], (tm // GW, 1)) + jnp.concatenate(rows, axis=0)


def _rope_specs(tm):
    col = pl.BlockSpec((GW, DA), lambda i: (0, 0))
    row = pl.BlockSpec((8 * tm // GW, DA), lambda i: (i, 0))
    return [col, row, col, row]


def _qk_prep(p, gq, gk, rope):
    tm = 256

    def body(qk_ref, v_ref, gq_ref, gk_ref, cc_ref, cr_ref, sc_ref, sr_ref, qr_ref, qp_ref, kr_ref, vh_ref):
        ones_bd = _head_ones()
        cs, sn = _rope_block(cc_ref, cr_ref, tm), _rope_block(sc_ref, sr_ref, tm)
        q = qk_ref[:, 0:DA]
        k = qk_ref[:, DA:2 * DA]
        yq = (q * lax.rsqrt(_head_sum(q * q, ones_bd) * (1.0 / DH) + RMS_EPS)) * gq_ref[...]
        yk = (k * lax.rsqrt(_head_sum(k * k, ones_bd) * (1.0 / DH) + RMS_EPS)) * gk_ref[...]
        qr = (yq * cs + _swap16(yq) * sn) * QK_SCALE
        qp = yq * QK_SCALE
        kr = yk * cs + _swap16(yk) * sn
        vv = v_ref[...]
        for hh in range(H):
            sl = slice(hh * DH, (hh + 1) * DH)
            qr_ref[hh] = qr[:, sl].astype(bf16)
            qp_ref[hh] = qp[:, sl].astype(bf16)
            kr_ref[hh] = kr[:, sl].astype(bf16)
            vh_ref[hh] = vv[:, sl].astype(bf16)

    hm = SDS((H, S, DH), bf16)
    hspec = pl.BlockSpec((H, tm, DH), lambda i: (0, i, 0))
    fixed = lambda i: (0, 0)
    return _hbm_call(
        body, name="qk_prep", out_shape=(hm, hm, hm, hm), grid=(S // tm,),
        in_specs=[pl.BlockSpec((tm, 2 * DA), lambda i: (i, 0)), pl.BlockSpec((tm, DA), lambda i: (i, 2)),
                  pl.BlockSpec((1, DA), fixed), pl.BlockSpec((1, DA), fixed)] + _rope_specs(tm),
        out_specs=(hspec, hspec, hspec, hspec),
    )(p, p, gq, gk, *rope)


def _ctx_prep(pc, gk):
    def body(p_ref, gk_ref, kc_ref, vc_ref):
        ones_bd = _head_ones()
        k = p_ref[:, 0:DA]
        yk = (k * lax.rsqrt(_head_sum(k * k, ones_bd) * (1.0 / DH) + RMS_EPS)) * gk_ref[...]
        vv = p_ref[:, DA:2 * DA]
        for hh in range(H):
            sl = slice(hh * DH, (hh + 1) * DH)
            kc_ref[hh] = yk[:, sl].astype(bf16)
            vc_ref[hh] = vv[:, sl].astype(bf16)

    hm = SDS((H, L, DH), bf16)
    return _hbm_call(
        body, name="ctx_prep", out_shape=(hm, hm), in_specs=[VMEM_SPEC, VMEM_SPEC], out_specs=(VMEM_SPEC, VMEM_SPEC),
    )(pc, gk)


def _tile_pieces():
    out = []
    for (i0, u0) in TILE_GEOM:
        rows = []
        for j in range(2):
            i = i0 + j
            rs = _row_start(i)
            rows.append([(u0 + u - i + WIN_H - 1) if rs <= u0 + u < rs + WIN_H else None for u in range(KR)])
        out.append(rows)
    return out


def _bias_prep(rpb_rev_pad):
    pieces = _tile_pieces()

    def body(r_ref, o_ref):
        rp = r_ref[...]
        xs = jnp.broadcast_to(rp[:, None, :], (N_DR, GW, 128)).reshape(N_DR * GW, 128)
        row = lax.broadcasted_iota(i32, xs.shape, 0)
        lane = lax.broadcasted_iota(i32, xs.shape, 1)
        for b in range(6):
            xs = jnp.where(((row >> b) & 1) == 1, pltpu.roll(xs, 1 << b, 1), xs)
        xs = pltpu.roll(xs, 128 - (WIN_W - 1), 1)
        k = row & (GW - 1)
        c0 = jnp.clip(lane - WIN_W // 2, 0, GW - WIN_W)
        xs = jnp.where((k >= c0) & (k < c0 + WIN_W), xs, NEG)
        neg = jnp.full((GW, GW), NEG, f32)
        for t in range(NT):
            for j in range(2):
                for u in range(KR):
                    dr = pieces[t][j][u]
                    piece = neg if dr is None else xs[dr * GW:(dr + 1) * GW, 0:GW]
                    o_ref[t, u * GW:(u + 1) * GW, j * GW:(j + 1) * GW] = piece

    return _hbm_call(
        body, name="bias_prep", out_shape=SDS((H, NT, KB, QB), f32), grid=(H,),
        in_specs=[pl.BlockSpec((None, N_DR, 128), lambda h: (h, 0, 0))],
        out_specs=pl.BlockSpec((None, NT, KB, QB), lambda h: (h, 0, 0, 0)),
    )(rpb_rev_pad)


def _block_geom(b):
    qs = b * QB
    ks = min(max(2 * b - 4, 0), ROWS - KR) * GW
    t = b if b < 2 else (b - (NQB - NT) if b > NQB - 3 else 2)
    return qs, ks, t


def _tt(a, b):
    return lax.dot_general(a, b, (((1,), (1,)), ((), ())), preferred_element_type=f32)


def _tn(a, b):
    return lax.dot_general(a, b, (((0,), (0,)), ((), ())), preferred_element_type=f32)


def _softmax_t(s_lat, s_ctx):
    m = jnp.maximum(jnp.max(s_lat, axis=0, keepdims=True), jnp.max(s_ctx, axis=0, keepdims=True))
    e_lat = jnp.exp(s_lat - m)
    e_ctx = jnp.exp(s_ctx - m)
    inv = 1.0 / (jnp.sum(e_lat, axis=0, keepdims=True) + jnp.sum(e_ctx, axis=0, keepdims=True))
    return e_lat * inv, e_ctx * inv


def _staged(n_blocks, stages):
    held = [dict() for _ in stages]
    for step in range(n_blocks + len(stages) - 1):
        for s, fn in enumerate(stages):
            b = step - s
            if 0 <= b < n_blocks:
                held[s][b] = fn(b) if s == 0 else fn(b, held[s - 1].pop(b))


def _attn_fwd(qr, qp, kr, vh, kc, vc, btt):
    def body(qr_ref, qp_ref, kr_ref, v_ref, kc_ref, vc_ref, bt_ref, o_ref):
        kcv, vcv = kc_ref[...], vc_ref[...]

        def scores(b):
            qs, ks, t = _block_geom(b)
            return (_tt(kr_ref[ks:ks + KB, :], qr_ref[qs:qs + QB, :]) + bt_ref[t], _tt(kcv, qp_ref[qs:qs + QB, :]))

        def probs(b, sc):
            p_lat, p_ctx = _softmax_t(*sc)
            return p_lat.astype(bf16), p_ctx.astype(bf16)

        def values(b, p):
            qs, ks, _ = _block_geom(b)
            o_ref[qs:qs + QB, :] = _tn(p[0], v_ref[ks:ks + KB, :]) + _tn(p[1], vcv)

        _staged(NQB, (scores, probs, values))

    sq = pl.BlockSpec((None, S, DH), lambda h: (h, 0, 0))
    sc = pl.BlockSpec((None, L, DH), lambda h: (h, 0, 0))
    return _hbm_call(
        body, name="attn_fwd", out_shape=SDS((H, S, DH), f32), grid=(H,),
        in_specs=[sq, sq, sq, sq, sc, sc, pl.BlockSpec((None, NT, KB, QB), lambda h: (h, 0, 0, 0))],
        out_specs=sq, compiler_params=_cp(48),
    )(qr, qp, kr, vh, kc, vc, btt)


def _shift_rows(v, down):
    n = v.shape[0]
    row = lax.broadcasted_iota(i32, v.shape, 0)
    if down:
        return jnp.where(row == 0, 0.0, pltpu.roll(v, 1, 0))
    return jnp.where(row == n - 1, 0.0, pltpu.roll(v, n - 1, 0))


def _conv_specs():
    col = lambda off: pl.BlockSpec((S, 128), lambda i, off=off: (0, off + i))
    return [col(16), col(20), col(24), col(28), pl.BlockSpec((3, 128), lambda i: (0, i)),
            pl.BlockSpec((1, 128), lambda i: (0, i))]


def _conv_fwd(p, conv_w, conv_b):
    def body(u_ref, bg_ref, cg_ref, zc_ref, w_ref, b_ref, o_ref):
        cu = cg_ref[...] * u_ref[...]
        cv = b_ref[...] + _shift_rows(cu, True) * w_ref[0:1, :]
        cv = cv + cu * w_ref[1:2, :]
        cv = cv + _shift_rows(cu, False) * w_ref[2:3, :]
        o_ref[...] = ((bg_ref[...] * cv) * _silu(zc_ref[...])).astype(bf16)

    return _hbm_call(
        body, name="conv_fwd", out_shape=SDS((S, DC), bf16), grid=(DC // 128,),
        in_specs=_conv_specs(), out_specs=pl.BlockSpec((S, 128), lambda i: (0, i)), compiler_params=_cp(40),
    )(p, p, p, p, conv_w, conv_b)


DP_Q, DP_K, DP_V, DP_ZA, DP_U, DP_BG, DP_CG, DP_ZC = range(8)


def _out_proj_loss(o, p, conv_g, w_out, xx, tgt, mrow, b_ada):
    tm = 256

    def body(o_ref, za_ref, c_ref, w_ref, x_ref, t_ref, m_ref, b_ref,
             dy_ref, dconv_ref, dp_ref, do_ref, gwo_ref, dgate_ref, loss_ref):
        k = pl.program_id(0)

        @pl.when(k == 0)
        def _():
            gwo_ref[...] = jnp.zeros_like(gwo_ref)
            dgate_ref[...] = jnp.zeros_like(dgate_ref)
            loss_ref[0, 0] = 0.0

        gate = m_ref[:, 2 * D:3 * D] + b_ref[:, 2 * D:3 * D]
        za = za_ref[...]
        sz = _silu(za)
        om = _merge_heads(o_ref)
        av, cv = (om * sz).astype(bf16), c_ref[...]
        mo = jnp.dot(av, w_ref[0:DA, :], preferred_element_type=f32)
        mo = mo + jnp.dot(cv, w_ref[DA:DA + DC, :], preferred_element_type=f32)
        y = x_ref[...] + gate * mo
        diff = y - t_ref[...]
        loss_ref[0, 0] += jnp.sum(diff * diff)
        dy = diff * (1.0 / D)
        dy_ref[...] = dy
        dgate_ref[...] += jnp.sum(dy * mo, axis=0, keepdims=True)
        dmo = (dy * gate).astype(bf16)
        dmix = _tt(dmo, w_ref[...])
        dattn = dmix[:, 0:DA]
        dconv_ref[...] = dmix[:, DA:DA + DC]
        a = dattn * sz
        for hh in range(H):
            do_ref[hh] = a[:, hh * DH:(hh + 1) * DH].astype(bf16)
        dp_ref[...] = ((dattn * _dsilu(za)) * om).astype(bf16)
        gwo_ref[0:DA, :] += _tn(av, dmo)
        gwo_ref[DA:DA + DC, :] += _tn(cv, dmo)

    row = lambda i: (i, 0)
    fixed = lambda i: (0, 0)
    hspec = pl.BlockSpec((H, tm, DH), lambda i: (0, i, 0))
    return _hbm_call(
        body, name="out_proj_loss",
        out_shape=(SDS((S, D), f32), SDS((S, DC), f32), SDS((8, S, DA), bf16), SDS((H, S, DH), bf16),
                   SDS((D, D), f32), SDS((1, D), f32), SDS((1, 1), f32)),
        grid=(S // tm,),
        in_specs=[hspec, pl.BlockSpec((tm, DA), lambda i: (i, 3)), pl.BlockSpec((tm, DC), row),
                  pl.BlockSpec((D, D), fixed), pl.BlockSpec((tm, D), row), pl.BlockSpec((tm, D), row),
                  pl.BlockSpec((1, 3 * D), fixed), pl.BlockSpec((1, 3 * D), fixed)],
        out_specs=(pl.BlockSpec((tm, D), row), pl.BlockSpec((tm, DC), row),
                   pl.BlockSpec((None, tm, DA), lambda i: (DP_ZA, i, 0)), hspec, pl.BlockSpec((D, D), fixed),
                   pl.BlockSpec((1, D), fixed), SMEM_SPEC),
        compiler_params=_cp(56, dimension_semantics=("arbitrary",)),
    )(o, p, conv_g, w_out, xx, tgt, mrow, b_ada)


def _conv_bwd(dconv, p, conv_w, conv_b, dp8, after=None):
    def body(d_ref, u_ref, bg_ref, cg_ref, zc_ref, w_ref, b_ref, dp_in_ref, after_ref, dp_ref, gw_ref, gb_ref):
        du_ref, dbg_ref, dcg_ref, dzc_ref = dp_ref.at[0], dp_ref.at[1], dp_ref.at[2], dp_ref.at[3]
        dconv = d_ref[...]
        u, bg, cg, zc = u_ref[...], bg_ref[...], cg_ref[...], zc_ref[...]
        w0, w1, w2 = w_ref[0:1, :], w_ref[1:2, :], w_ref[2:3, :]
        cu = cg * u
        cu_m, cu_p = _shift_rows(cu, True), _shift_rows(cu, False)
        cv = b_ref[...] + cu_m * w0
        cv = cv + cu * w1
        cv = cv + cu_p * w2
        sz = _silu(zc)
        dbg_ref[...] = ((dconv * sz) * cv).astype(bf16)
        dzc_ref[...] = ((dconv * (bg * cv)) * _dsilu(zc)).astype(bf16)
        dcv = (dconv * sz) * bg
        gb_ref[...] = jnp.sum(dcv, axis=0, keepdims=True)
        gw_ref[0:1, :] = jnp.sum(dcv * cu_m, axis=0, keepdims=True)
        gw_ref[1:2, :] = jnp.sum(dcv * cu, axis=0, keepdims=True)
        gw_ref[2:3, :] = jnp.sum(dcv * cu_p, axis=0, keepdims=True)
        gw_ref[3:8, :] = jnp.zeros((5, 128), f32)
        dcu = _shift_rows(dcv, False) * w0 + dcv * w1 + _shift_rows(dcv, True) * w2
        dcg_ref[...] = (dcu * u).astype(bf16)
        du_ref[...] = (dcu * cg).astype(bf16)

    return _hbm_call(
        body, name="conv_bwd", out_shape=(SDS((8, S, DC), bf16), SDS((8, DC), f32), SDS((1, DC), f32)),
        grid=(DC // 128,),
        in_specs=[pl.BlockSpec((S, 128), lambda i: (0, i))] + _conv_specs() + [ANY_SPEC, ANY_SPEC],
        out_specs=(pl.BlockSpec((4, S, 128), lambda i: (DP_U // 4, 0, i)), pl.BlockSpec((8, 128), lambda i: (0, i)),
                   pl.BlockSpec((1, 128), lambda i: (0, i))),
        input_output_aliases={7: 0}, compiler_params=_cp(48),
    )(dconv, p, p, p, p, conv_w, conv_b, dp8, conv_b if after is None else after)


def _attn_bwd(qr, qp, kr, vh, kc, vc, btt, do, after=None):
    def body(qr_ref, qp_ref, kr_ref, v_ref, kc_ref, vc_ref, bt_ref, do_ref, after_ref,
             dqr_ref, dqp_ref, dkr_ref, dv_ref, dkc_ref, dvc_ref, dbt_ref):
        kcv, vcv = kc_ref[...], vc_ref[...]
        dkr_ref[...] = jnp.zeros_like(dkr_ref)
        dv_ref[...] = jnp.zeros_like(dv_ref)
        dbt_ref[...] = jnp.zeros_like(dbt_ref)
        ctx_acc = {}

        def products(b):
            qs, ks, t = _block_geom(b)
            dob = do_ref[qs:qs + QB, :]
            s_lat = _tt(kr_ref[ks:ks + KB, :], qr_ref[qs:qs + QB, :]) + bt_ref[t]
            s_ctx = _tt(kcv, qp_ref[qs:qs + QB, :])
            return s_lat, s_ctx, _tt(v_ref[ks:ks + KB, :], dob), _tt(vcv, dob)

        def score_grads(b, x):
            s_lat, s_ctx, dp_lat, dp_ctx = x
            p_lat, p_ctx = _softmax_t(s_lat, s_ctx)
            delta = jnp.sum(p_lat * dp_lat, axis=0, keepdims=True) + jnp.sum(p_ctx * dp_ctx, axis=0, keepdims=True)
            ds_lat = p_lat * (dp_lat - delta)
            ds_ctx = p_ctx * (dp_ctx - delta)
            return ds_lat, ds_lat.astype(bf16), ds_ctx.astype(bf16), p_lat.astype(bf16), p_ctx.astype(bf16)

        def operand_grads(b, y):
            qs, ks, t = _block_geom(b)
            ds_lat, dsb_lat, dsb_ctx, pb_lat, pb_ctx = y
            qrb, qpb, dob = qr_ref[qs:qs + QB, :], qp_ref[qs:qs + QB, :], do_ref[qs:qs + QB, :]
            dbt_ref[t] += ds_lat
            dqr_ref[qs:qs + QB, :] = _tn(dsb_lat, kr_ref[ks:ks + KB, :])
            dqp_ref[qs:qs + QB, :] = _tn(dsb_ctx, kcv)
            dkr_ref[ks:ks + KB, :] += jnp.dot(dsb_lat, qrb, preferred_element_type=f32)
            dv_ref[ks:ks + KB, :] += jnp.dot(pb_lat, dob, preferred_element_type=f32)
            dkc = jnp.dot(dsb_ctx, qpb, preferred_element_type=f32)
            dvc = jnp.dot(pb_ctx, dob, preferred_element_type=f32)
            ctx_acc["k"] = dkc if b == 0 else ctx_acc["k"] + dkc
            ctx_acc["v"] = dvc if b == 0 else ctx_acc["v"] + dvc

        _staged(NQB, (products, score_grads, operand_grads))
        dkc_ref[...] = ctx_acc["k"]
        dvc_ref[...] = ctx_acc["v"]

    sq = pl.BlockSpec((None, S, DH), lambda h: (h, 0, 0))
    sc = pl.BlockSpec((None, L, DH), lambda h: (h, 0, 0))
    sb = pl.BlockSpec((None, NT, KB, QB), lambda h: (h, 0, 0, 0))
    big, ctxs = SDS((H, S, DH), f32), SDS((H, L, DH), f32)
    return _hbm_call(
        body, name="attn_bwd", out_shape=(big, big, big, big, ctxs, ctxs, SDS((H, NT, KB, QB), f32)), grid=(H,),
        in_specs=[sq, sq, sq, sq, sc, sc, sb, sq, ANY_SPEC], out_specs=(sq, sq, sq, sq, sc, sc, sb),
        compiler_params=_cp(56),
    )(qr, qp, kr, vh, kc, vc, btt, do, do if after is None else after)


def _bias_bwd(dbtt):
    pieces = _tile_pieces()

    def body(d_ref, o_ref, scr):
        scr[...] = jnp.zeros_like(scr)
        acc = [None] * N_DR
        for t in range(NT):
            for j in range(2):
                for u in range(KR):
                    dr = pieces[t][j][u]
                    if dr is None:
                        continue
                    piece = d_ref[t, u * GW:(u + 1) * GW, j * GW:(j + 1) * GW]
                    acc[dr] = piece if acc[dr] is None else acc[dr] + piece
        for dr in range(N_DR):
            scr[dr * GW:(dr + 1) * GW, 0:GW] = acc[dr]
        xs = pltpu.roll(scr[...], WIN_W - 1, 1)
        row = lax.broadcasted_iota(i32, xs.shape, 0)
        for b in range(6):
            xs = jnp.where(((row >> b) & 1) == 1, pltpu.roll(xs, 128 - (1 << b), 1), xs)
        rev = jnp.sum(xs.reshape(N_DR, GW, 128), axis=1)
        a = lax.broadcasted_iota(i32, (128, 128), 0)
        b = lax.broadcasted_iota(i32, (128, 128), 1)
        flip = ((a + b == N_DC - 1) & (a < N_DC)).astype(f32)
        o_ref[...] = jnp.dot(rev, flip, precision=HIGHEST, preferred_element_type=f32)

    return _hbm_call(
        body, name="bias_bwd", out_shape=SDS((H, N_DR, 128), f32), grid=(H,),
        in_specs=[pl.BlockSpec((None, NT, KB, QB), lambda h: (h, 0, 0, 0))],
        out_specs=pl.BlockSpec((None, N_DR, 128), lambda h: (h, 0, 0)),
        scratch_shapes=[pltpu.VMEM((N_DR * GW, 128), f32)],
    )(dbtt)


def _merge_heads(ref):
    return jnp.concatenate([ref[hh] for hh in range(H)], axis=1)


def _head_norm_bwd(xraw, gain, dy, ones_bd):
    r = lax.rsqrt(_head_sum(xraw * xraw, ones_bd) * (1.0 / DH) + RMS_EPS)
    xh = xraw * r
    gdy = dy * gain
    dx = r * (gdy - xh * (_head_sum(xh * gdy, ones_bd) * (1.0 / DH)))
    return dx, jnp.sum(dy * xh, axis=0, keepdims=True)


def _qk_bwd(dqr, dqp, dkr, dvh, p, gq, gk, rope, dp8):
    tm = 256

    def body(dqr_ref, dqp_ref, dkr_ref, dv_ref, qk_ref, gq_ref, gk_ref, cc_ref, cr_ref, sc_ref, sr_ref, dp_in_ref,
             dp_ref, ggq_ref, ggk_ref):
        dq_ref, dk_ref, dvo_ref = dp_ref.at[DP_Q], dp_ref.at[DP_K], dp_ref.at[DP_V]

        @pl.when(pl.program_id(0) == 0)
        def _():
            ggq_ref[...] = jnp.zeros_like(ggq_ref)
            ggk_ref[...] = jnp.zeros_like(ggk_ref)

        ones_bd = _head_ones()
        cs, sn = _rope_block(cc_ref, cr_ref, tm), _rope_block(sc_ref, sr_ref, tm)
        a = _merge_heads(dqr_ref)
        dyq = ((a * cs - _swap16(a) * sn) + _merge_heads(dqp_ref)) * QK_SCALE
        bk = _merge_heads(dkr_ref)
        dyk = bk * cs - _swap16(bk) * sn
        dq, gq_part = _head_norm_bwd(qk_ref[:, 0:DA], gq_ref[...], dyq, ones_bd)
        dk, gk_part = _head_norm_bwd(qk_ref[:, DA:2 * DA], gk_ref[...], dyk, ones_bd)
        dq_ref[...] = dq.astype(bf16)
        dk_ref[...] = dk.astype(bf16)
        dvo_ref[...] = _merge_heads(dv_ref).astype(bf16)
        ggq_ref[...] += gq_part
        ggk_ref[...] += gk_part

    hspec = pl.BlockSpec((H, tm, DH), lambda i: (0, i, 0))
    fixed = pl.BlockSpec((1, DA), lambda i: (0, 0))
    return _hbm_call(
        body, name="qk_bwd", out_shape=(SDS((8, S, DA), bf16), SDS((1, DA), f32), SDS((1, DA), f32)), grid=(S // tm,),
        in_specs=[hspec, hspec, hspec, hspec, pl.BlockSpec((tm, 2 * DA), lambda i: (i, 0)), fixed, fixed]
        + _rope_specs(tm) + [ANY_SPEC],
        out_specs=(pl.BlockSpec((3, tm, DA), lambda i: (0, i, 0)), fixed, fixed), input_output_aliases={11: 0},
        compiler_params=_cp(40, dimension_semantics=("arbitrary",)),
    )(dqr, dqp, dkr, dvh, p, gq, gk, *rope, dp8)


def _ctx_bwd(dkc, dvc, pc, gk):
    def body(dkc_ref, dvc_ref, p_ref, gk_ref, dk_ref, dv_ref, ggk_ref):
        ones_bd = _head_ones()
        dk, gk_part = _head_norm_bwd(p_ref[:, 0:DA], gk_ref[...], _merge_heads(dkc_ref), ones_bd)
        dk_ref[...] = dk.astype(bf16)
        dv_ref[...] = _merge_heads(dvc_ref).astype(bf16)
        ggk_ref[...] = gk_part

    piece = SDS((L, DA), bf16)
    return _hbm_call(
        body, name="ctx_bwd", out_shape=(piece, piece, SDS((1, DA), f32)), in_specs=[VMEM_SPEC] * 4,
        out_specs=(VMEM_SPEC,) * 3,
    )(dkc, dvc, pc, gk)


def _grad_w_in(h, dp8, hc, dkc_raw, dvc_m):
    tm = 512

    def body(h_ref, p_ref, hc_ref, dk_ref, dv_ref, g_ref):
        j, k = pl.program_id(0), pl.program_id(1)

        @pl.when(k == 0)
        def _():
            g_ref[...] = jnp.zeros_like(g_ref)

        @pl.when((k == 0) & (j == 0))
        def _():
            g_ref[:, DA:2 * DA] = _tn(hc_ref[...], dk_ref[...])

        @pl.when((k == 0) & (j == 1))
        def _():
            g_ref[:, 0:DA] = _tn(hc_ref[...], dv_ref[...])

        hv = h_ref[...]
        g_ref[:, 0:DA] += _tn(hv, p_ref[0])
        g_ref[:, DA:2 * DA] += _tn(hv, p_ref[1])

    fixed = lambda j, k: (0, 0)
    return _hbm_call(
        body, name="grad_w_in", out_shape=SDS((4, D, D), f32), grid=(4, S // tm),
        in_specs=[pl.BlockSpec((tm, D), lambda j, k: (k, 0)), pl.BlockSpec((2, tm, DA), lambda j, k: (j, k, 0)),
                  pl.BlockSpec((L, D), fixed), pl.BlockSpec((L, DA), fixed), pl.BlockSpec((L, DA), fixed)],
        out_specs=pl.BlockSpec((None, D, D), lambda j, k: (j, 0, 0)),
        compiler_params=_cp(40, dimension_semantics=("arbitrary", "arbitrary")),
    )(h, dp8, hc, dkc_raw, dvc_m)


def _norm_mod_bwd(x, dh, g, scale):
    r = lax.rsqrt(jnp.mean(x * x, axis=-1, keepdims=True) + RMS_EPS)
    xh = x * r
    y = xh * g
    dshift = jnp.sum(dh, axis=0, keepdims=True)
    dscale = jnp.sum(dh * y, axis=0, keepdims=True)
    dyn = dh * (1.0 + scale)
    dg = jnp.sum(dyn * xh, axis=0, keepdims=True)
    gdy = dyn * g
    dx = r * (gdy - xh * jnp.mean(xh * gdy, axis=-1, keepdims=True))
    return dx, dshift, dscale, dg


def _dh_grad_x(dp8, w4, xx, dy, norm_g, mrow, b_ada):
    tm = 256

    def body(p_ref, w_ref, x_ref, dy_ref, g_ref, m_ref, b_ref, gx_ref, dsh_ref, dsc_ref, dg_ref):
        @pl.when(pl.program_id(0) == 0)
        def _():
            dsh_ref[...] = jnp.zeros_like(dsh_ref)
            dsc_ref[...] = jnp.zeros_like(dsc_ref)
            dg_ref[...] = jnp.zeros_like(dg_ref)

        dh = None
        for j in range(4):
            for half in range(2):
                term = _tt(p_ref[2 * j + half], w_ref[j, :, half * DA:(half + 1) * DA])
                dh = term if dh is None else dh + term
        scale = m_ref[:, D:2 * D] + b_ref[:, D:2 * D]
        dx, dshift, dscale, dg = _norm_mod_bwd(x_ref[...], dh, g_ref[...], scale)
        gx_ref[...] = dy_ref[...] + dx
        dsh_ref[...] += dshift
        dsc_ref[...] += dscale
        dg_ref[...] += dg

    row = lambda i: (i, 0)
    fixed = lambda i: (0, 0)
    vec = SDS((1, D), f32)
    return _hbm_call(
        body, name="dh_grad_x", out_shape=(SDS((S, D), f32), vec, vec, vec), grid=(S // tm,),
        in_specs=[pl.BlockSpec((8, tm, DA), lambda i: (0, i, 0)), pl.BlockSpec((4, D, D), lambda i: (0, 0, 0)),
                  pl.BlockSpec((tm, D), row), pl.BlockSpec((tm, D), row), pl.BlockSpec((1, D), fixed),
                  pl.BlockSpec((1, 3 * D), fixed), pl.BlockSpec((1, 3 * D), fixed)],
        out_specs=(pl.BlockSpec((tm, D), row), pl.BlockSpec((1, D), fixed), pl.BlockSpec((1, D), fixed),
                   pl.BlockSpec((1, D), fixed)),
        compiler_params=_cp(56, dimension_semantics=("arbitrary",)),
    )(dp8, w4, xx, dy, norm_g, mrow, b_ada)


def _dhc_sums(dkc_raw, dvc_m, w4, ctx2, norm_g, mrow_c, b_ada):
    def body(dk_ref, dv_ref, w0_ref, w1_ref, x_ref, g_ref, m_ref, b_ref, dsh_ref, dsc_ref, dg_ref):
        dh = _tt(dk_ref[...], w0_ref[:, DA:2 * DA]) + _tt(dv_ref[...], w1_ref[:, 0:DA])
        scale = m_ref[:, D:2 * D] + b_ref[:, D:2 * D]
        _, dshift, dscale, dg = _norm_mod_bwd(x_ref[...], dh, g_ref[...], scale)
        dsh_ref[...] = dshift
        dsc_ref[...] = dscale
        dg_ref[...] = dg

    fixed = lambda i: (0, 0)
    vec = SDS((1, D), f32)
    vspec = pl.BlockSpec((1, D), fixed)
    return _hbm_call(
        body, name="dhc_sums", out_shape=(vec, vec, vec), grid=(1,),
        in_specs=[pl.BlockSpec((L, DA), fixed), pl.BlockSpec((L, DA), fixed),
                  pl.BlockSpec((None, D, D), lambda i: (0, 0, 0)), pl.BlockSpec((None, D, D), lambda i: (1, 0, 0)),
                  pl.BlockSpec((L, D), fixed), vspec, pl.BlockSpec((1, 3 * D), fixed), pl.BlockSpec((1, 3 * D), fixed)],
        out_specs=(vspec, vspec, vspec), compiler_params=_cp(32),
    )(dkc_raw, dvc_m, w4, w4, ctx2, norm_g, mrow_c, b_ada)


def _rope_tables():
    nf = DH // 4
    inv = np.float32(ROPE_THETA) ** (-np.arange(nf, dtype=np.float32) / np.float32(nf))
    ang_c = np.arange(GW, dtype=np.float32)[:, None] * inv
    ang_r = np.arange(ROWS, dtype=np.float32)[:, None] * inv
    zc, zr = np.zeros((GW, 2 * nf), np.float32), np.zeros((ROWS, 2 * nf), np.float32)
    ct_cos = np.tile(np.concatenate([zc, np.cos(ang_c), np.cos(ang_c)], axis=1), (1, H))
    ct_sin = np.tile(np.concatenate([zc, -np.sin(ang_c), np.sin(ang_c)], axis=1), (1, H))
    rt_cos = np.tile(np.concatenate([np.cos(ang_r), np.cos(ang_r), zr], axis=1), (1, H))
    rt_sin = np.tile(np.concatenate([-np.sin(ang_r), np.sin(ang_r), zr], axis=1), (1, H))
    rep8 = lambda t: np.ascontiguousarray(np.broadcast_to(t[:, None, :], (ROWS, 8, DA))).reshape(ROWS * 8, DA)
    return tuple(jnp.asarray(t, f32) for t in (ct_cos, rep8(rt_cos), ct_sin, rep8(rt_sin)))


def _local_step(xx, ctx2, tgt, mrow, mrow_c, b_ada, norm_g, weights, q_norm_g, k_norm_g, rpb2, conv_w_full, conv_b,
                hooks=None):
    hooks = hooks or {}
    gq = jnp.tile(q_norm_g, (1, H))
    gk = jnp.tile(k_norm_g, (1, H))
    rope = _rope_tables()
    rpb_pad = jnp.pad(rpb2[:, :, ::-1], ((0, 0), (0, 0), (0, 128 - N_DC)))

    h = _prenorm(xx, norm_g, mrow, b_ada, 256, "prenorm_x")
    hc = _prenorm(ctx2, norm_g, mrow_c, b_ada, L, "prenorm_ctx")
    btb = _bias_prep(rpb_pad)
    jv = weights["jvec"]
    p = _in_proj_own(h, weights["own"], jv)
    w4 = weights["near"](p)
    p = _in_proj_block(h, w4, p, jv ^ 1, "in_proj_y")
    p = _in_proj_block(h, w4, p, jv ^ 2, "in_proj_x")
    w4, started = weights["far"](w4, p)
    p = _in_proj_block(h, w4, p, jv ^ 3, "in_proj_xy", after=started)
    pc = _ctx_proj(hc, w4)
    qr, qp, kr, vh = _qk_prep(p, gq, gk, rope)
    kc, vc = _ctx_prep(pc, gk)
    o = _attn_fwd(qr, qp, kr, vh, kc, vc, btb)
    conv_g = _conv_fwd(p, conv_w_full, conv_b)
    w_out_full = weights["out"](o)
    dy, dconv, dp8, do, g_w_out, dgate, loss_sum = _out_proj_loss(o, p, conv_g, w_out_full, xx, tgt, mrow, b_ada)
    started = hooks["g_w_out"](g_w_out) if "g_w_out" in hooks else None
    dp8, g_conv_w, g_conv_b = _conv_bwd(dconv, p, conv_w_full, conv_b, dp8, after=started)
    started = hooks["after_conv"](dp8) if "after_conv" in hooks else None
    dqr, dqp, dkr, dvh, dkc, dvc, dbtb = _attn_bwd(qr, qp, kr, vh, kc, vc, btb, do, after=started)
    g_rpb = _bias_bwd(dbtb)
    dp8, g_gq, g_gk = _qk_bwd(dqr, dqp, dkr, dvh, p, gq, gk, rope, dp8)
    dkc_raw, dvc_m, g_gk_c = _ctx_bwd(dkc, dvc, pc, gk)
    g_w_in = _grad_w_in(h, dp8, hc, dkc_raw, dvc_m)
    b_ada_late = hooks["g_w_in"](g_w_in, b_ada) if "g_w_in" in hooks else b_ada
    grad_x, dshift, dscale, dng = _dh_grad_x(dp8, w4, xx, dy, norm_g, mrow, b_ada_late)
    dshift_c, dscale_c, dng_c = _dhc_sums(dkc_raw, dvc_m, w4, ctx2, norm_g, mrow_c, b_ada_late)
    return dict(loss_sum=loss_sum, grad_x=grad_x, g_w_in=g_w_in, g_w_out=g_w_out, g_conv_w=g_conv_w,
                g_conv_b=g_conv_b, g_rpb=g_rpb, g_gq=g_gq, g_gk=g_gk, g_gk_c=g_gk_c, dshift=dshift, dscale=dscale,
                dgate=dgate, dng=dng, dshift_c=dshift_c, dscale_c=dscale_c, dng_c=dng_c)


def _pair_sum_w_in(g, r, cvec):
    tr = 128

    def body(c_ref, g_ref, r_ref, t32_ref, tb_ref):
        t = g_ref[...] + r_ref[...]
        t32_ref[...] = t
        tb_ref[...] = t.astype(bf16)

    half = D // 2
    g_spec = pl.BlockSpec((4, tr, D), lambda i, c: (0, c[0] * (half // tr) + i, 0))
    o_spec = pl.BlockSpec((4, tr, D), lambda i, c: (0, i, 0))
    grid_spec = pltpu.PrefetchScalarGridSpec(num_scalar_prefetch=1, grid=(half // tr,), in_specs=[g_spec, o_spec],
                                             out_specs=(o_spec, o_spec))
    return _hbm_call(body, name="pair_sum_w_in", out_shape=(SDS((4, half, D), f32), SDS((4, half, D), bf16)),
                     grid_spec=grid_spec, compiler_params=_cp(40))(cvec, g, r)


def _pair_sum_w_out(g, r, cvec):
    hr = D // 8

    def body(c_ref, g0, g1, g2, g3, r_ref, t32_ref, tb_ref):
        for q, g_ref in enumerate((g0, g1, g2, g3)):
            t = g_ref[...] + r_ref[q]
            t32_ref[q] = t
            tb_ref[q] = t.astype(bf16)

    gspecs = [pl.BlockSpec((hr, D), lambda i, c, q=q: (2 * q + c[0], 0)) for q in range(4)]
    full = pl.BlockSpec((4, hr, D), lambda i, c: (0, 0, 0))
    grid_spec = pltpu.PrefetchScalarGridSpec(num_scalar_prefetch=1, grid=(1,), in_specs=gspecs + [full],
                                             out_specs=(full, full))
    return _hbm_call(body, name="pair_sum_w_out", out_shape=(SDS((4, hr, D), f32), SDS((4, hr, D), bf16)),
                          grid_spec=grid_spec)(cvec, g, g, g, g, r)


def _chip_sum(t32, r2, jvec, name):
    rows = t32.shape[1]
    tr = min(rows, 128)

    def body(j_ref, t_ref, r_ref, u_ref):
        u_ref[...] = ((t_ref[...] + r_ref[0].astype(f32)) + r_ref[1].astype(f32)) + r_ref[2].astype(f32)

    grid_spec = pltpu.PrefetchScalarGridSpec(
        num_scalar_prefetch=1, grid=(rows // tr,),
        in_specs=[pl.BlockSpec((None, tr, D), lambda i, j: (j[0], i, 0)), pl.BlockSpec((3, tr, D), lambda i, j: (0, i, 0))],
        out_specs=pl.BlockSpec((tr, D), lambda i, j: (i, 0)))
    return _hbm_call(body, name=name, out_shape=SDS((rows, D), f32), grid_spec=grid_spec)(jvec, t32, r2)


_PK = {}
_off = 0
for _name, _rows in (("dm", 24), ("dmc", 24), ("dng", 8), ("dng_c", 8), ("gq", 8), ("gk", 8), ("gk_c", 8),
                     ("rpb", H * N_DR), ("conv_b", 8), ("conv_w", 16), ("loss", 8)):
    _PK[_name] = (_off, _off + _rows)
    _off += _rows
PK_ROWS = _off
RS_B_ADA, RS_NORM_G, RS_GQ, RS_GK, RS_RPB, RS_CONV_B, RS_CONV_W, RS_DMC, RS_LOSS, RS_ROWS = (
    0, 24, 32, 40, 48, 168, 176, 192, 216, 224)


def _small_reduce(gathered):
    def body(g_ref, o_ref, dm_ref):
        a0 = _PK["dm"][0]
        dm_ref[...] = jnp.zeros_like(dm_ref)
        for b in range(8):
            for i in range(24):
                dm_ref[b:b + 1, 128 * i:128 * (i + 1)] = g_ref[b, a0 + i:a0 + i + 1, :]
        tot = g_ref[0]
        for b in range(1, 8):
            tot = tot + g_ref[b]

        def rows(name):
            a, z = _PK[name]
            return tot[a:z]

        o_ref[RS_B_ADA:RS_B_ADA + 24] = rows("dm") + rows("dmc")
        o_ref[RS_NORM_G:RS_NORM_G + 8] = rows("dng") + rows("dng_c")
        gq = jnp.broadcast_to(jnp.sum(rows("gq"), axis=0, keepdims=True), (8, 128))
        gk = jnp.broadcast_to(jnp.sum(rows("gk") + rows("gk_c"), axis=0, keepdims=True), (8, 128))
        o_ref[RS_GQ:RS_GQ + 8] = gq + pltpu.roll(gq, DH, 1)
        o_ref[RS_GK:RS_GK + 8] = gk + pltpu.roll(gk, DH, 1)
        o_ref[RS_RPB:RS_RPB + H * N_DR] = rows("rpb")
        o_ref[RS_CONV_B:RS_CONV_B + 8] = rows("conv_b")
        o_ref[RS_CONV_W:RS_CONV_W + 16] = rows("conv_w")
        dmc = rows("dmc")
        o_ref[RS_DMC:RS_DMC + 24] = dmc
        o_ref[RS_LOSS:RS_LOSS + 8] = rows("loss")
        for i in range(24):
            dm_ref[8:9, 128 * i:128 * (i + 1)] = dmc[i:i + 1]

    return _hbm_call(body, name="small_reduce", out_shape=(SDS((RS_ROWS, 128), f32), SDS((16, 3 * D), f32)),
                     in_specs=[VMEM_SPEC], out_specs=(VMEM_SPEC, VMEM_SPEC))(gathered)


def _w_ada_grad(sc16, dm16, w_ada_shard, jvec):
    ncol = w_ada_shard.shape[1]

    def body(j_ref, sc_ref, dm_ref, w_ref, g_ref, part_ref):
        dm = dm_ref[...]
        g_ref[...] = lax.dot_general(sc_ref[...], dm, (((0,), (0,)), ((), ())), precision=HIGHEST,
                                     preferred_element_type=f32)
        part_ref[...] = lax.dot_general(dm[8:16], w_ref[...], (((1,), (1,)), ((), ())), precision=HIGHEST,
                                        preferred_element_type=f32)

    fixed = lambda i, j: (0, 0)
    grid_spec = pltpu.PrefetchScalarGridSpec(
        num_scalar_prefetch=1, grid=(1,),
        in_specs=[pl.BlockSpec((16, D), fixed), pl.BlockSpec((16, ncol), lambda i, j: (0, j[0])),
                  pl.BlockSpec((D, ncol), fixed)],
        out_specs=(pl.BlockSpec((D, ncol), fixed), pl.BlockSpec((8, D), fixed)))
    return _pallas_call(body, name="w_ada_grad", out_shape=(SDS((D, ncol), f32), SDS((8, D), f32)),
                        grid_spec=grid_spec, compiler_params=_cp(40))(jvec, sc16, dm16, w_ada_shard)


def _c_ctx_grad(parts4, c_ctx):
    def body(p_ref, c_ref, o_ref):
        tot = ((p_ref[0] + p_ref[1]) + p_ref[2]) + p_ref[3]
        o_ref[...] = tot[0:1] * _dsilu(c_ref[...].reshape(1, D))

    return _pallas_call(body, name="c_ctx_grad", out_shape=SDS((1, D), f32), in_specs=[VMEM_SPEC, VMEM_SPEC],
                        out_specs=VMEM_SPEC)(parts4, c_ctx)


def _adamw(w, g, m, v, name):
    rows, cols = w.shape
    tr = 256 if rows % 256 == 0 else rows

    def body(w_ref, g_ref, m_ref, v_ref, d_ref, m2_ref, v2_ref):
        gv = g_ref[...]
        m2 = ADAM_B1 * m_ref[...] + (1.0 - ADAM_B1) * gv
        v2 = ADAM_B2 * v_ref[...] + (1.0 - ADAM_B2) * jnp.square(gv)
        m_hat = m2 / (1.0 - ADAM_B1 ** ADAM_STEP)
        v_hat = v2 / (1.0 - ADAM_B2 ** ADAM_STEP)
        d_ref[...] = -ADAM_LR * (m_hat / (jnp.sqrt(v_hat) + ADAM_EPS) + ADAM_WD * w_ref[...])
        m2_ref[...] = m2
        v2_ref[...] = v2

    spec = pl.BlockSpec((tr, cols), lambda i: (i, 0))
    shp = SDS((rows, cols), f32)
    return _hbm_call(body, name=name, out_shape=(shp, shp, shp), grid=(rows // tr,), in_specs=[spec] * 4,
                          out_specs=(spec, spec, spec))(w, g, m, v)


def _adamw_halves(w, g_mine, g_other, m, v, cvec, name):
    rows, cols = w.shape
    half = rows // 2
    tr = min(256, half)
    per_half = half // tr

    def body(c_ref, w_ref, ga_ref, gb_ref, m_ref, v_ref, g_ref, d_ref, m2_ref, v2_ref):
        in_my_half = (pl.program_id(0) // per_half) == c_ref[0]
        gv = jnp.where(in_my_half, ga_ref[...], gb_ref[...])
        g_ref[...] = gv
        m2 = ADAM_B1 * m_ref[...] + (1.0 - ADAM_B1) * gv
        v2 = ADAM_B2 * v_ref[...] + (1.0 - ADAM_B2) * jnp.square(gv)
        m_hat = m2 / (1.0 - ADAM_B1 ** ADAM_STEP)
        v_hat = v2 / (1.0 - ADAM_B2 ** ADAM_STEP)
        d_ref[...] = -ADAM_LR * (m_hat / (jnp.sqrt(v_hat) + ADAM_EPS) + ADAM_WD * w_ref[...])
        m2_ref[...] = m2
        v2_ref[...] = v2

    full = pl.BlockSpec((tr, cols), lambda i, c: (i, 0))
    part = pl.BlockSpec((tr, cols), lambda i, c: (i % per_half, 0))
    shp = SDS((rows, cols), f32)
    grid_spec = pltpu.PrefetchScalarGridSpec(num_scalar_prefetch=1, grid=(rows // tr,),
                                             in_specs=[full, part, part, full, full], out_specs=(full,) * 4)
    return _hbm_call(body, name=name, out_shape=(shp,) * 4, grid_spec=grid_spec)(cvec, w, g_mine, g_other, m, v)


def _adam_math(w, g, m, v):
    m2 = ADAM_B1 * m + (1.0 - ADAM_B1) * g
    v2 = ADAM_B2 * v + (1.0 - ADAM_B2) * jnp.square(g)
    m_hat = m2 / (1.0 - ADAM_B1 ** ADAM_STEP)
    v_hat = v2 / (1.0 - ADAM_B2 ** ADAM_STEP)
    return -ADAM_LR * (m_hat / (jnp.sqrt(v_hat) + ADAM_EPS) + ADAM_WD * w), m2, v2


def _adamw_small(red, g_c_ctx, jvec, ws, ms, vs):
    n = len(ws)

    def body(*refs):
        red_ref, gc_ref, j_ref = refs[:3]
        w_refs, m_refs, v_refs = refs[3:3 + n], refs[3 + n:3 + 2 * n], refs[3 + 2 * n:3 + 3 * n]
        outs = refs[3 + 3 * n:]
        g_out, d_out, m_out, v_out = outs[:n], outs[n:2 * n], outs[2 * n:3 * n], outs[3 * n:]
        chip = j_ref[0]
        lanes = lambda i: (slice(None), slice(128 * i, 128 * (i + 1)))
        row = lambda r0, i: (lambda: red_ref[r0 + i:r0 + i + 1, :])
        whole = (slice(None), slice(None))
        chunks = [
            [((slice(None),), lambda: gc_ref[...].reshape(D))],
            [(lanes(i), row(RS_B_ADA, i)) for i in range(3 * D // 128)],
            [(lanes(i), row(RS_NORM_G, i)) for i in range(D // 128)],
            [(whole, lambda: red_ref[RS_GQ:RS_GQ + 1, 0:DH])],
            [(whole, lambda: red_ref[RS_GK:RS_GK + 1, 0:DH])],
            [((dr,), (lambda dr=dr: red_ref[pl.ds(RS_RPB + dr, H, stride=N_DR), 0:N_DC])) for dr in range(N_DR)],
            [((r,), (lambda r=r: red_ref[pl.ds(RS_CONV_W + 4 * r + chip, 1), :])) for r in range(3)],
            [(lanes(i), row(RS_CONV_B, i)) for i in range(DC // 128)],
        ]
        for a in range(n):
            for idx, grad in chunks[a]:
                g = grad()
                d, m2, v2 = _adam_math(w_refs[a][idx], g, m_refs[a][idx], v_refs[a][idx])
                g_out[a][idx] = g
                d_out[a][idx] = d
                m_out[a][idx] = m2
                v_out[a][idx] = v2

    shapes = [SDS(w.shape, f32) for w in ws]
    res = _pallas_call(body, name="adamw_small", out_shape=shapes * 4,
                       in_specs=[VMEM_SPEC, VMEM_SPEC, SMEM_SPEC] + [VMEM_SPEC] * (3 * n),
                       out_specs=[VMEM_SPEC] * (4 * n))(red, g_c_ctx, jvec, *ws, *ms, *vs)
    return [list(res[k * n:(k + 1) * n]) for k in range(4)]


def _rows128(a):
    return a.reshape(-1, 128)


def _pad_lanes(a):
    a2 = a.reshape(-1, a.shape[-1])
    return jnp.pad(a2, ((0, 0), (0, 128 - a2.shape[1])))


def kernel(x, c, ctx, c_ctx, w_ada, b_ada, norm_g, w_in, q_norm_g, k_norm_g, rpb, conv_w, conv_b, w_out, loss_target, m_c_ctx, m_w_ada, m_b_ada, m_norm_g, m_w_in, m_q_norm_g, m_k_norm_g, m_rpb, m_conv_w, m_conv_b, m_w_out, v_c_ctx, v_w_ada, v_b_ada, v_norm_g, v_w_in, v_q_norm_g, v_k_norm_g, v_rpb, v_conv_w, v_conv_b, v_w_out):
    xi, yi, ci = lax.axis_index("x"), lax.axis_index("y"), lax.axis_index("c")
    dev = 4 * xi + 2 * yi + ci
    chip = 2 * xi + yi
    cvec = jnp.reshape(ci, (1,)).astype(i32)
    jvec = jnp.reshape(chip, (1,)).astype(i32)
    w_ada_s = w_ada[0]
    ncol = w_ada_s.shape[1]

    c8 = _all_gather8(c.reshape(8, 128), "gather_c").reshape(8, D)
    cc = jnp.concatenate([c8, c_ctx.reshape(1, D), jnp.zeros((7, D), f32)], axis=0)
    m_shard, sc16 = _adaln_shard(cc, w_ada_s)

    conv_w_pad = jnp.pad(conv_w[0], ((0, 5), (0, 0)))
    m4, cw4 = _chip_gather([m_shard, conv_w_pad], "gather_mod")

    all_k = [(0, 0, 0), (0, 0, 1), (0, 0, 2)]
    wo4c = _cast_to_slot(w_out[0], jvec, "cast_w_out")
    sem_a, rem_a, w4s, token = _halves_start([_cast_to_slot(w_in[0], jvec, "cast_w_in")], m4, all_k[0:2],
                                             "weights_near_start")
    m_full = jnp.transpose(m4, (1, 0, 2)).reshape(16, 4 * ncol) + token[0:1, 0:1]
    mrow = lax.dynamic_slice(m_full, (dev, 0), (1, 3 * D))
    mrow_c = m_full[8:9]
    conv_w_full = jnp.transpose(cw4[:, 0:3, :], (1, 0, 2)).reshape(3, DC)
    waves = {}

    def near(after):
        (w4w,) = _halves_wait(sem_a, rem_a, [w4s], after, all_k[0:2], "weights_near_wait")
        w4f = _halves_forward([w4w], [0, 1], "weights_near_forward")[0]
        sem_b, rem_b, w4b, _ = _halves_start([w4f], after, all_k[2:3], "weights_far_start")
        waves["far"] = (sem_b, rem_b)
        return w4b

    def far(w4, after):
        (w4w,) = _halves_wait(*waves["far"], [w4], after, all_k[2:3], "weights_far_wait")
        w4f = _halves_forward([w4w], [2], "weights_far_forward")[0]
        sem_c, rem_c, wo4s, started = _halves_start([wo4c], w4f, all_k, "weights_out_start")
        waves["out"] = (sem_c, rem_c, wo4s)
        return w4f, started

    def w_out_gathered(after):
        sem_c, rem_c, wo4s = waves["out"]
        (wow,) = _halves_wait(sem_c, rem_c, [wo4s], after, all_k, "weights_out_wait")
        return _halves_forward([wow], [0, 1, 2], "weights_out_forward")[0].reshape(D, D)

    weights = dict(own=w_in[0], jvec=jvec, near=near, far=far, out=w_out_gathered)

    exchange = _exchange_copies
    pending = {}

    def on_g_w_out(g_w_out):
        out = _split_start([g_w_out], [SDS((4, D // 8, D), f32)], 4, exchange, "grad_out_pair_start")
        pending["ex_out"] = out
        return out[4]

    def after_conv(dp8):
        ssem_o, rsem_o, g_o, land_o, _ = pending["ex_out"]
        (g_o,), (ex_o,) = _split_wait(ssem_o, rsem_o, [g_o], [land_o], dp8, exchange, "grad_out_pair_wait")
        to32, tob = _pair_sum_w_out(g_o, ex_o, cvec)
        out = _split_start([tob], [SDS((3, D // 8, D), bf16)], 3, _scatter_copies, "grad_out_chip_start")
        pending["sc_out"] = (out, to32)
        return out[4]

    def on_g_w_in(g_w_in, b_ada_in):
        out = _split_start([g_w_in], [SDS((4, D // 2, D), f32)], 4, exchange, "grad_pair_start")
        pending["ex"] = (out[0], out[1], [out[2]], [out[3]])
        return b_ada_in + out[4][0:1, 0:1]

    r = _local_step(x[0], ctx[0], loss_target[0], mrow, mrow_c, b_ada, norm_g, weights, q_norm_g, k_norm_g,
                    rpb[0], conv_w_full, conv_b, dict(g_w_out=on_g_w_out, after_conv=after_conv, g_w_in=on_g_w_in))
    dm = jnp.concatenate([r["dshift"], r["dscale"], r["dgate"]], axis=1)
    dmc = jnp.concatenate([r["dshift_c"], r["dscale_c"], jnp.zeros((1, D), f32)], axis=1)
    pack_parts = [_rows128(dm), _rows128(dmc), _rows128(r["dng"]), _rows128(r["dng_c"]), _rows128(r["g_gq"]),
                  _rows128(r["g_gk"]), _rows128(r["g_gk_c"]), r["g_rpb"].reshape(H * N_DR, 128),
                  _rows128(r["g_conv_b"]), _rows128(r["g_conv_w"][0:3]), jnp.pad(r["loss_sum"], ((0, 0), (0, 127)))]
    pack = jnp.concatenate([jnp.pad(p, ((0, -p.shape[0] % 8), (0, 0))) for p in pack_parts], axis=0)
    gathered = _all_gather8(pack, "gather_small")

    ex_ssem, ex_rsem, ex_srcs, ex_lands = pending["ex"]
    ex_g, ex = _split_wait(ex_ssem, ex_rsem, ex_srcs, ex_lands, gathered, exchange, "grad_pair_wait")
    t32, tb = _pair_sum_w_in(ex_g[0], ex[0], cvec)
    sc_out = _split_start([tb], [SDS((3, D // 2, D), bf16)], 3, _scatter_copies, "grad_chip_start")
    sc16 = sc16 + sc_out[4][0:1, 0:1]

    assert pack.shape[0] == PK_ROWS
    red, dm16 = _small_reduce(gathered)
    loss = red[RS_LOSS, 0] * (0.5 / D)

    g_w_ada_s, cpart = _w_ada_grad(sc16, dm16, w_ada_s, jvec)
    d_w_ada, nm_w_ada, nv_w_ada = _adamw(w_ada_s, g_w_ada_s, m_w_ada[0], v_w_ada[0], "adamw_w_ada")

    (cparts4,) = _chip_gather([cpart], "gather_c_ctx_parts", after=nm_w_ada)
    g_c_ctx = _c_ctx_grad(cparts4, c_ctx)

    _, (r2,) = _split_wait(sc_out[0], sc_out[1], [sc_out[2]], [sc_out[3]], g_c_ctx, _scatter_copies,
                           "grad_chip_wait")
    u_in = _chip_sum(t32, r2, jvec, "chip_sum_w_in")
    sc_o, to32 = pending["sc_out"]
    _, (ro2,) = _split_wait(sc_o[0], sc_o[1], [sc_o[2]], [sc_o[3]], u_in, _scatter_copies, "grad_out_chip_wait")
    u_out = _chip_sum(to32, ro2, jvec, "chip_sum_w_out")

    t_rpb = lambda a: jnp.transpose(a, (0, 2, 1, 3)).reshape(N_DR, H, N_DC)
    t_cw = lambda a: jnp.transpose(a, (1, 0, 2))
    small = _adamw_small(
        red, g_c_ctx, jvec,
        [c_ctx, b_ada, norm_g, q_norm_g, k_norm_g, t_rpb(rpb), t_cw(conv_w), conv_b],
        [m_c_ctx, m_b_ada, m_norm_g, m_q_norm_g, m_k_norm_g, t_rpb(m_rpb), t_cw(m_conv_w), m_conv_b],
        [v_c_ctx, v_b_ada, v_norm_g, v_q_norm_g, v_k_norm_g, t_rpb(v_rpb), t_cw(v_conv_w), v_conv_b])
    for kind in small:
        kind[5] = jnp.transpose(kind[5].reshape(1, N_DR, H, N_DC), (0, 2, 1, 3))
        kind[6] = jnp.transpose(kind[6], (1, 0, 2))

    o_in, o_out = _sibling_send([u_in, u_out], "grad_pair_send")
    g_w_in_s, d_w_in, nm_w_in, nv_w_in = _adamw_halves(w_in[0], u_in, o_in, m_w_in[0], v_w_in[0], cvec, "adamw_w_in")
    g_w_out_s, d_w_out, nm_w_out, nv_w_out = _adamw_halves(w_out[0], u_out, o_out, m_w_out[0], v_w_out[0], cvec,
                                                           "adamw_w_out")

    def ordered(kind, big_w_ada, big_w_in, big_w_out):
        s_c_ctx, s_b_ada, s_norm_g, s_q, s_k, s_rpb, s_conv_w, s_conv_b = small[kind]
        return [s_c_ctx, big_w_ada[None], s_b_ada, s_norm_g, big_w_in[None], s_q, s_k, s_rpb, s_conv_w,
                s_conv_b, big_w_out[None]]

    grads = ordered(0, g_w_ada_s, g_w_in_s, g_w_out_s)
    deltas = ordered(1, d_w_ada, d_w_in, d_w_out)
    new_m = ordered(2, nm_w_ada, nm_w_in, nm_w_out)
    new_v = ordered(3, nv_w_ada, nv_w_in, nv_w_out)
    return (loss, r["grad_x"][None], *grads, *deltas, *new_m, *new_v)
```

```python
import functools

import jax
import jax.numpy as jnp
import numpy as np
from jax import lax
from jax.experimental import pallas as pl
from jax.experimental.pallas import tpu as pltpu

f32, bf16, i32 = jnp.float32, jnp.bfloat16, jnp.int32
MESH = pl.DeviceIdType.MESH
HIGHEST = lax.Precision.HIGHEST

D = 1024
S = 2048
L = 256
GW = 64
ROWS = S // GW
H = 8
DH = 64
DA = H * DH
DC = 512
WIN_H, WIN_W = 8, 16
N_DR, N_DC = 2 * WIN_H - 1, 2 * WIN_W - 1
RMS_EPS = 1e-6
ROPE_THETA = 10000.0
QK_SCALE = DH ** -0.5
NEG = -1e30

QB = 128
NQB = S // QB
KR = 9
KB = KR * GW
TILE_GEOM = ((0, 0), (2, 0), (4, 0), (28, 23), (30, 23))
NT = len(TILE_GEOM)

ADAM_LR, ADAM_B1, ADAM_B2, ADAM_EPS, ADAM_WD, ADAM_STEP = 0.001, 0.9, 0.999, 1e-08, 0.01, 10

VMEM_SPEC = pl.BlockSpec(memory_space=pltpu.VMEM)
ANY_SPEC = pl.BlockSpec(memory_space=pl.ANY)
SMEM_SPEC = pl.BlockSpec(memory_space=pltpu.SMEM)
SDS = jax.ShapeDtypeStruct


_pallas_call = pl.pallas_call


def _hbm_call(body, *, out_shape, in_specs=None, out_specs=None, grid_spec=None, **kw):
    n_pre = 0
    if grid_spec is not None:
        ispecs, ospecs, n_pre = grid_spec.in_specs, grid_spec.out_specs, grid_spec.num_scalar_prefetch
        kw["grid_spec"] = grid_spec
    else:
        ispecs, ospecs = in_specs, out_specs
        kw.update(in_specs=in_specs, out_specs=out_specs)

    def blocked(spec):
        return isinstance(spec, pl.BlockSpec) and spec.block_shape is not None

    single = not isinstance(out_shape, (tuple, list))
    shapes = [out_shape] if single else list(out_shape)
    ospec_list = list(ospecs) if isinstance(ospecs, (tuple, list)) else [ospecs]
    shapes = [pltpu.HBM(s.shape, s.dtype) if blocked(sp) else s for s, sp in zip(shapes, ospec_list)]
    call = _pallas_call(body, out_shape=shapes[0] if single else tuple(shapes), **kw)

    def run(*args):
        arrays = [pltpu.with_memory_space_constraint(a, pltpu.HBM) if blocked(sp) else a
                  for a, sp in zip(args[n_pre:], ispecs)]
        return call(*args[:n_pre], *arrays)

    return run


def _cp(vmem_mb=None, **kw):
    if vmem_mb is not None:
        kw["vmem_limit_bytes"] = vmem_mb << 20
    return pltpu.CompilerParams(**kw)


def _silu(z):
    return z * jax.nn.sigmoid(z)


def _dsilu(z):
    sg = jax.nn.sigmoid(z)
    return sg * (1.0 + z * (1.0 - sg))


def _row_start(i):
    return min(max(i - WIN_H // 2, 0), ROWS - WIN_H)


def _my_pos():
    return lax.axis_index("x"), lax.axis_index("y"), lax.axis_index("c")


def _flip(v, bit):
    return 1 - v if bit else v


def _chip_gather(smalls, name, after=None):
    ns = len(smalls)

    def body(*refs):
        s_in, s_out = refs[:ns], refs[ns + 1:2 * ns + 1]
        ssem, rsem, lsem = refs[2 * ns + 1:]
        x, y, c = _my_pos()
        j = 2 * x + y
        chips = _peer_chips(x, y, c)
        local = [pltpu.make_async_copy(s_in[a], s_out[a].at[j], lsem.at[a]) for a in range(ns)]
        for cp in local:
            cp.start()
        sends = []
        for a in range(ns):
            for k in range(3):
                cp = pltpu.make_async_remote_copy(src_ref=s_in[a], dst_ref=s_out[a].at[j], send_sem=ssem.at[3 * a + k],
                                                  recv_sem=rsem.at[3 * a + k], device_id=chips[k][0], device_id_type=MESH)
                cp.start()
                sends.append(cp)
        for a in range(ns):
            for k in range(3):
                pltpu.make_async_remote_copy(src_ref=s_in[a], dst_ref=s_out[a].at[chips[k][1]], send_sem=ssem.at[3 * a + k],
                                             recv_sem=rsem.at[3 * a + k], device_id=chips[k][0],
                                             device_id_type=MESH).wait_recv()
        for cp in sends:
            cp.wait_send()
        for cp in local:
            cp.wait()

    return _hbm_call(
        body, name=name, out_shape=[SDS((4,) + a.shape, a.dtype) for a in smalls],
        in_specs=[VMEM_SPEC] * ns + [ANY_SPEC], out_specs=[VMEM_SPEC] * ns,
        scratch_shapes=[pltpu.SemaphoreType.DMA((3 * ns,)), pltpu.SemaphoreType.DMA((3 * ns,)),
                        pltpu.SemaphoreType.DMA((ns,))],
    )(*smalls, smalls[0] if after is None else after)


HBM_SPEC = pl.BlockSpec(memory_space=pltpu.HBM)
SEM_SPEC = pl.BlockSpec(memory_space=pltpu.SEMAPHORE)
DATAFLOW = pltpu.SideEffectType.DATAFLOW_SIDE_EFFECTING


def _peer_chips(x, y, c):
    out = []
    for k in range(1, 4):
        px, py = _flip(x, (k >> 1) & 1), _flip(y, k & 1)
        out.append(((px, py, c), 2 * px + py))
    return out


def _half_copies(srcs, dsts, ssem, rsem, which):
    x, y, c = _my_pos()
    j = 2 * x + y
    peers = _peer_chips(x, y, c)
    pairs = []
    for pos, group, k in which:
        half = srcs[pos].shape[1] // 2
        mine = pl.ds(pl.multiple_of(c * half, 8), half)
        dev, pj = peers[k]
        sem = 3 * group + k
        send = pltpu.make_async_remote_copy(src_ref=srcs[pos].at[j, mine], dst_ref=dsts[pos].at[j, mine],
                                            send_sem=ssem.at[sem], recv_sem=rsem.at[sem], device_id=dev,
                                            device_id_type=MESH)
        arrive = pltpu.make_async_remote_copy(src_ref=srcs[pos].at[j, mine], dst_ref=dsts[pos].at[pj, mine],
                                              send_sem=ssem.at[sem], recv_sem=rsem.at[sem], device_id=dev,
                                              device_id_type=MESH)
        pairs.append((send, arrive))
    return pairs


def _halves_start(bigs, after, order, name):
    nb = len(bigs)

    def body(*refs):
        b_in = refs[:nb]
        ssem, rsem = refs[nb + 1], refs[nb + 2]
        b_out = refs[nb + 3:2 * nb + 3]
        token = refs[2 * nb + 3]
        for send, _ in _half_copies(b_in, b_out, ssem, rsem, order):
            send.start()
        token[...] = jnp.zeros_like(token)

    out_shape = (pltpu.SemaphoreType.DMA((3 * nb,)), pltpu.SemaphoreType.DMA((3 * nb,)),
                 *[pltpu.HBM(b.shape, b.dtype) for b in bigs], SDS((8, 128), f32))
    return _hbm_call(
        body, name=name, out_shape=out_shape, in_specs=[HBM_SPEC] * nb + [ANY_SPEC],
        out_specs=(SEM_SPEC, SEM_SPEC, *[HBM_SPEC] * nb, VMEM_SPEC),
        input_output_aliases={a: 2 + a for a in range(nb)}, compiler_params=_cp(has_side_effects=DATAFLOW),
    )(*[pltpu.with_memory_space_constraint(b, pltpu.HBM) for b in bigs], after)


def _halves_wait(ssem, rsem, bigs, after, which, name):
    nb = len(bigs)

    def body(*refs):
        b_in = refs[:nb]
        ssem_ref, rsem_ref = refs[nb], refs[nb + 1]
        for send, arrive in _half_copies(b_in, b_in, ssem_ref, rsem_ref, which):
            send.wait_send()
            arrive.wait_recv()

    return _hbm_call(
        body, name=name, out_shape=tuple(pltpu.HBM(b.shape, b.dtype) for b in bigs),
        in_specs=[HBM_SPEC] * nb + [SEM_SPEC, SEM_SPEC, ANY_SPEC], out_specs=tuple([HBM_SPEC] * nb),
        input_output_aliases={a: a for a in range(nb)}, compiler_params=_cp(has_side_effects=DATAFLOW),
    )(*bigs, ssem, rsem, after)


def _halves_forward(bigs, relations, name):
    nb, nr = len(bigs), len(relations)

    def body(*refs):
        b_in, b_out = refs[:nb], refs[nb:2 * nb]
        ssem, rsem = refs[2 * nb:]
        x, y, c = _my_pos()
        sib = (x, y, 1 - c)
        peers = _peer_chips(x, y, c)
        sends = []
        for a in range(nb):
            half = b_in[a].shape[1] // 2
            mine = pl.ds(pl.multiple_of(c * half, 8), half)
            for i, k in enumerate(relations):
                pj = peers[k][1]
                cp = pltpu.make_async_remote_copy(src_ref=b_in[a].at[pj, mine], dst_ref=b_out[a].at[pj, mine],
                                                  send_sem=ssem.at[nr * a + i], recv_sem=rsem.at[nr * a + i],
                                                  device_id=sib, device_id_type=MESH)
                cp.start()
                sends.append(cp)
        for a in range(nb):
            half = b_in[a].shape[1] // 2
            other = pl.ds(pl.multiple_of((1 - c) * half, 8), half)
            for i, k in enumerate(relations):
                pj = peers[k][1]
                pltpu.make_async_remote_copy(src_ref=b_in[a].at[pj, other], dst_ref=b_out[a].at[pj, other],
                                             send_sem=ssem.at[nr * a + i], recv_sem=rsem.at[nr * a + i],
                                             device_id=sib, device_id_type=MESH).wait_recv()
        for cp in sends:
            cp.wait_send()

    return _hbm_call(
        body, name=name, out_shape=[SDS(b.shape, b.dtype) for b in bigs], in_specs=[ANY_SPEC] * nb,
        out_specs=[ANY_SPEC] * nb, input_output_aliases={a: a for a in range(nb)},
        scratch_shapes=[pltpu.SemaphoreType.DMA((nr * nb,)), pltpu.SemaphoreType.DMA((nr * nb,))],
    )(*bigs)


def _cast_to_slot(w, jvec, name):
    rows, cols = w.shape
    tr = 256

    def body(j_ref, w_ref, o_ref):
        o_ref[...] = w_ref[...].astype(bf16)

    grid_spec = pltpu.PrefetchScalarGridSpec(
        num_scalar_prefetch=1, grid=(rows // tr,), in_specs=[pl.BlockSpec((tr, cols), lambda i, j: (i, 0))],
        out_specs=pl.BlockSpec((None, tr, cols), lambda i, j: (j[0], i, 0)))
    return _hbm_call(body, name=name, out_shape=SDS((4, rows, cols), bf16), grid_spec=grid_spec)(jvec, w)


def _exchange_copies(srcs, lands, ssem, rsem):
    x, y, c = _my_pos()
    cps = []
    for a in range(len(srcs)):
        stacked = len(srcs[a].shape) == 3
        rb = srcs[a].shape[1] if stacked else srcs[a].shape[0] // 4
        half = rb // 2
        for jb in range(4):
            if stacked:
                src = srcs[a].at[jb, pl.ds(pl.multiple_of((1 - c) * half, 8), half)]
            else:
                src = srcs[a].at[pl.ds(pl.multiple_of(jb * rb + (1 - c) * half, 8), half)]
            cps.append(pltpu.make_async_remote_copy(src_ref=src, dst_ref=lands[a].at[jb], send_sem=ssem.at[4 * a + jb],
                                                    recv_sem=rsem.at[4 * a + jb], device_id=(x, y, 1 - c),
                                                    device_id_type=MESH))
    return cps


def _scatter_copies(srcs, lands, ssem, rsem):
    x, y, c = _my_pos()
    cps = []
    for a in range(len(srcs)):
        for k, (dev, pj) in enumerate(_peer_chips(x, y, c)):
            cps.append(pltpu.make_async_remote_copy(src_ref=srcs[a].at[pj], dst_ref=lands[a].at[k],
                                                    send_sem=ssem.at[3 * a + k], recv_sem=rsem.at[3 * a + k],
                                                    device_id=dev, device_id_type=MESH))
    return cps


class _SlotCopy:
    def __init__(self, src, dst, mine, theirs, ssem, rsem, dev):
        make = lambda slot: pltpu.make_async_remote_copy(src_ref=src.at[mine], dst_ref=dst.at[slot], send_sem=ssem,
                                                         recv_sem=rsem, device_id=dev, device_id_type=MESH)
        send, arrive = make(mine), make(theirs)
        self.start, self.wait_send, self.wait_recv = send.start, send.wait_send, arrive.wait_recv


def _gather8_copies(srcs, dsts, ssem, rsem):
    x, y, c = _my_pos()
    me = 4 * x + 2 * y + c
    cps = []
    for a in range(len(srcs)):
        for k in range(1, 8):
            tgt = (_flip(x, (k >> 2) & 1), _flip(y, (k >> 1) & 1), _flip(c, k & 1))
            cps.append(_SlotCopy(srcs[a], dsts[a], me, 4 * tgt[0] + 2 * tgt[1] + tgt[2], ssem.at[7 * a + k - 1],
                                 rsem.at[7 * a + k - 1], tgt))
    return cps


def _gather4_copies(srcs, dsts, ssem, rsem):
    x, y, c = _my_pos()
    cps = []
    for a in range(len(srcs)):
        for k, (dev, pj) in enumerate(_peer_chips(x, y, c)):
            cps.append(_SlotCopy(srcs[a], dsts[a], 2 * x + y, pj, ssem.at[3 * a + k], rsem.at[3 * a + k], dev))
    return cps


def _to_slot(a, n, i):
    return lax.dynamic_update_slice(jnp.zeros((n,) + a.shape, a.dtype), a[None], (i,) + (0,) * a.ndim)


def _split_start(srcs, land_shapes, n_cp, make, name):
    ns, nl = len(srcs), len(land_shapes)

    def body(*refs):
        s_in = refs[:ns]
        ssem, rsem = refs[ns + nl], refs[ns + nl + 1]
        l_out = refs[2 * ns + nl + 2:2 * ns + 2 * nl + 2]
        token = refs[2 * ns + 2 * nl + 2]
        s_out = refs[ns + nl + 2:2 * ns + nl + 2]
        for cp in make(s_in, l_out if nl else s_out, ssem, rsem):
            cp.start()
        token[...] = jnp.zeros_like(token)

    lands = [pltpu.with_memory_space_constraint(lax.empty(sh.shape, sh.dtype), pltpu.HBM) for sh in land_shapes]
    out_shape = (pltpu.SemaphoreType.DMA((n_cp,)), pltpu.SemaphoreType.DMA((n_cp,)),
                 *[pltpu.HBM(b.shape, b.dtype) for b in srcs], *[pltpu.HBM(b.shape, b.dtype) for b in land_shapes],
                 SDS((8, 128), f32))
    return _hbm_call(
        body, name=name, out_shape=out_shape, in_specs=[HBM_SPEC] * (ns + nl),
        out_specs=(SEM_SPEC, SEM_SPEC, *[HBM_SPEC] * (ns + nl), VMEM_SPEC),
        input_output_aliases={i: 2 + i for i in range(ns + nl)}, compiler_params=_cp(has_side_effects=DATAFLOW),
    )(*[pltpu.with_memory_space_constraint(b, pltpu.HBM) for b in srcs], *lands)


def _split_wait(ssem, rsem, srcs, lands, after, make, name):
    ns, nl = len(srcs), len(lands)

    def body(*refs):
        s_in, l_in = refs[:ns], refs[ns:ns + nl]
        ssem_ref, rsem_ref = refs[ns + nl], refs[ns + nl + 1]
        for cp in make(s_in, l_in if nl else s_in, ssem_ref, rsem_ref):
            cp.wait_send()
            cp.wait_recv()

    outs = _hbm_call(
        body, name=name, out_shape=tuple(pltpu.HBM(b.shape, b.dtype) for b in (*srcs, *lands)),
        in_specs=[HBM_SPEC] * (ns + nl) + [SEM_SPEC, SEM_SPEC, ANY_SPEC], out_specs=tuple([HBM_SPEC] * (ns + nl)),
        input_output_aliases={i: i for i in range(ns + nl)}, compiler_params=_cp(has_side_effects=DATAFLOW),
    )(*srcs, *lands, ssem, rsem, after)
    return list(outs[:ns]), list(outs[ns:])


def _sibling_send(halves, name):
    n = len(halves)

    def body(*refs):
        ins, outs = refs[:n], refs[n:2 * n]
        ssem, rsem = refs[2 * n:]
        x, y, c = _my_pos()
        cps = []
        for a in range(n):
            cp = pltpu.make_async_remote_copy(src_ref=ins[a], dst_ref=outs[a], send_sem=ssem.at[a],
                                              recv_sem=rsem.at[a], device_id=(x, y, 1 - c), device_id_type=MESH)
            cp.start()
            cps.append(cp)
        for cp in cps:
            cp.wait_recv()
        for cp in cps:
            cp.wait_send()

    out_shape = [SDS(h.shape, h.dtype) for h in halves]
    return _hbm_call(
        body, name=name, out_shape=out_shape, in_specs=[ANY_SPEC] * n, out_specs=[ANY_SPEC] * n,
        scratch_shapes=[pltpu.SemaphoreType.DMA((n,)), pltpu.SemaphoreType.DMA((n,))],
    )(*halves)


def _adaln_shard(cc, w_ada_shard):
    def body(c_ref, w_ref, m_ref, sc_ref):
        sc = _silu(c_ref[...])
        sc_ref[...] = sc
        m_ref[...] = jnp.dot(sc, w_ref[...], precision=HIGHEST, preferred_element_type=f32)

    return _hbm_call(
        body, name="adaln_shard", out_shape=(SDS((16, w_ada_shard.shape[1]), f32), SDS((16, D), f32)),
        in_specs=[VMEM_SPEC, VMEM_SPEC], out_specs=(VMEM_SPEC, VMEM_SPEC), compiler_params=_cp(32),
    )(cc, w_ada_shard)


def _prenorm(xx, norm_g, mrow, b_ada, tm, name):
    n = xx.shape[0]

    def body(x_ref, g_ref, m_ref, b_ref, h_ref):
        x = x_ref[...]
        shift = m_ref[:, 0:D] + b_ref[:, 0:D]
        scale = m_ref[:, D:2 * D] + b_ref[:, D:2 * D]
        r = lax.rsqrt(jnp.mean(x * x, axis=-1, keepdims=True) + RMS_EPS)
        y = (x * r) * g_ref[...]
        h_ref[...] = (y * (1.0 + scale) + shift).astype(bf16)

    row = lambda i: (i, 0)
    fixed = lambda i: (0, 0)
    return _hbm_call(
        body, name=name, out_shape=SDS((n, D), bf16), grid=(n // tm,),
        in_specs=[pl.BlockSpec((tm, D), row), pl.BlockSpec((1, D), fixed), pl.BlockSpec((1, 3 * D), fixed),
                  pl.BlockSpec((1, 3 * D), fixed)],
        out_specs=pl.BlockSpec((tm, D), row),
    )(xx, norm_g, mrow, b_ada)


def _in_proj_own(h, w_own, jvec):
    tm = 512

    def body(j_ref, h_ref, w_ref, p_ref):
        p_ref[...] = jnp.dot(h_ref[...], w_ref[...].astype(bf16), preferred_element_type=f32)

    grid_spec = pltpu.PrefetchScalarGridSpec(
        num_scalar_prefetch=1, grid=(S // tm,),
        in_specs=[pl.BlockSpec((tm, D), lambda i, j: (i, 0)), pl.BlockSpec((D, D), lambda i, j: (0, 0))],
        out_specs=pl.BlockSpec((tm, D), lambda i, j: (i, j[0])))
    return _hbm_call(body, name="in_proj_own", out_shape=SDS((S, 4 * D), f32), grid_spec=grid_spec,
                     compiler_params=_cp(40))(jvec, h, w_own)


def _in_proj_block(h, w4, p, bvec, name, after=None):
    tm = 512

    def body(b_ref, h_ref, w_ref, p_in_ref, after_ref, p_ref):
        p_ref[...] = jnp.dot(h_ref[...], w_ref[...], preferred_element_type=f32)

    grid_spec = pltpu.PrefetchScalarGridSpec(
        num_scalar_prefetch=1, grid=(S // tm,),
        in_specs=[pl.BlockSpec((tm, D), lambda i, b: (i, 0)), pl.BlockSpec((None, D, D), lambda i, b: (b[0], 0, 0)),
                  ANY_SPEC, ANY_SPEC],
        out_specs=pl.BlockSpec((tm, D), lambda i, b: (i, b[0])))
    return _hbm_call(body, name=name, out_shape=SDS((S, 4 * D), f32), grid_spec=grid_spec,
                     input_output_aliases={3: 0})(bvec, h, w4, p, bvec if after is None else after)


def _ctx_proj(hc, w4):
    def body(h_ref, w0_ref, w1_ref, p_ref):
        hv = h_ref[...]
        p_ref[:, 0:DA] = jnp.dot(hv, w0_ref[:, DA:2 * DA], preferred_element_type=f32)
        p_ref[:, DA:2 * DA] = jnp.dot(hv, w1_ref[:, 0:DA], preferred_element_type=f32)

    return _hbm_call(
        body, name="ctx_proj", out_shape=SDS((L, 2 * DA), f32), grid=(1,),
        in_specs=[pl.BlockSpec((L, D), lambda i: (0, 0)), pl.BlockSpec((None, D, D), lambda i: (0, 0, 0)),
                  pl.BlockSpec((None, D, D), lambda i: (1, 0, 0))],
        out_specs=pl.BlockSpec((L, 2 * DA), lambda i: (0, 0)),
    )(hc, w4, w4)


def _head_ones():
    r = lax.broadcasted_iota(i32, (DA, DA), 0) // DH
    c = lax.broadcasted_iota(i32, (DA, DA), 1) // DH
    return (r == c).astype(bf16)


def _head_sum(v, ones_bd):
    hi = v.astype(bf16)
    lo = (v - hi.astype(f32)).astype(bf16)
    return jnp.dot(hi, ones_bd, preferred_element_type=f32) + jnp.dot(lo, ones_bd, preferred_element_type=f32)


def _swap16(v):
    lane = lax.broadcasted_iota(i32, v.shape, 1)
    return jnp.where((lane & 31) < 16, pltpu.roll(v, DA - 16, 1), pltpu.roll(v, 16, 1))


def _rope_block(ct_ref, rt_ref, tm):
    rows = [jnp.tile(rt_ref[8 * j:8 * j + 8, :], (GW // 8, 1)) for j in range(tm // GW)]
    return jnp.tile(ct_ref[...], (tm // GW, 1)) + jnp.concatenate(rows, axis=0)


def _rope_specs(tm):
    col = pl.BlockSpec((GW, DA), lambda i: (0, 0))
    row = pl.BlockSpec((8 * tm // GW, DA), lambda i: (i, 0))
    return [col, row, col, row]


def _qk_prep(p, gq, gk, rope):
    tm = 256

    def body(qk_ref, v_ref, gq_ref, gk_ref, cc_ref, cr_ref, sc_ref, sr_ref, qr_ref, qp_ref, kr_ref, vh_ref):
        ones_bd = _head_ones()
        cs, sn = _rope_block(cc_ref, cr_ref, tm), _rope_block(sc_ref, sr_ref, tm)
        q = qk_ref[:, 0:DA]
        k = qk_ref[:, DA:2 * DA]
        yq = (q * lax.rsqrt(_head_sum(q * q, ones_bd) * (1.0 / DH) + RMS_EPS)) * gq_ref[...]
        yk = (k * lax.rsqrt(_head_sum(k * k, ones_bd) * (1.0 / DH) + RMS_EPS)) * gk_ref[...]
        qr = (yq * cs + _swap16(yq) * sn) * QK_SCALE
        qp = yq * QK_SCALE
        kr = yk * cs + _swap16(yk) * sn
        vv = v_ref[...]
        for hh in range(H):
            sl = slice(hh * DH, (hh + 1) * DH)
            qr_ref[hh] = qr[:, sl].astype(bf16)
            qp_ref[hh] = qp[:, sl].astype(bf16)
            kr_ref[hh] = kr[:, sl].astype(bf16)
            vh_ref[hh] = vv[:, sl].astype(bf16)

    hm = SDS((H, S, DH), bf16)
    hspec = pl.BlockSpec((H, tm, DH), lambda i: (0, i, 0))
    fixed = lambda i: (0, 0)
    return _hbm_call(
        body, name="qk_prep", out_shape=(hm, hm, hm, hm), grid=(S // tm,),
        in_specs=[pl.BlockSpec((tm, 2 * DA), lambda i: (i, 0)), pl.BlockSpec((tm, DA), lambda i: (i, 2)),
                  pl.BlockSpec((1, DA), fixed), pl.BlockSpec((1, DA), fixed)] + _rope_specs(tm),
        out_specs=(hspec, hspec, hspec, hspec),
    )(p, p, gq, gk, *rope)


def _ctx_prep(pc, gk):
    def body(p_ref, gk_ref, kc_ref, vc_ref):
        ones_bd = _head_ones()
        k = p_ref[:, 0:DA]
        yk = (k * lax.rsqrt(_head_sum(k * k, ones_bd) * (1.0 / DH) + RMS_EPS)) * gk_ref[...]
        vv = p_ref[:, DA:2 * DA]
        for hh in range(H):
            sl = slice(hh * DH, (hh + 1) * DH)
            kc_ref[hh] = yk[:, sl].astype(bf16)
            vc_ref[hh] = vv[:, sl].astype(bf16)

    hm = SDS((H, L, DH), bf16)
    return _hbm_call(
        body, name="ctx_prep", out_shape=(hm, hm), in_specs=[VMEM_SPEC, VMEM_SPEC], out_specs=(VMEM_SPEC, VMEM_SPEC),
    )(pc, gk)


def _tile_pieces():
    out = []
    for (i0, u0) in TILE_GEOM:
        rows = []
        for j in range(2):
            i = i0 + j
            rs = _row_start(i)
            rows.append([(u0 + u - i + WIN_H - 1) if rs <= u0 + u < rs + WIN_H else None for u in range(KR)])
        out.append(rows)
    return out


def _bias_prep(rpb_rev_pad):
    pieces = _tile_pieces()

    def body(r_ref, o_ref):
        rp = r_ref[...]
        xs = jnp.broadcast_to(rp[:, None, :], (N_DR, GW, 128)).reshape(N_DR * GW, 128)
        row = lax.broadcasted_iota(i32, xs.shape, 0)
        lane = lax.broadcasted_iota(i32, xs.shape, 1)
        for b in range(6):
            xs = jnp.where(((row >> b) & 1) == 1, pltpu.roll(xs, 1 << b, 1), xs)
        xs = pltpu.roll(xs, 128 - (WIN_W - 1), 1)
        k = row & (GW - 1)
        c0 = jnp.clip(lane - WIN_W // 2, 0, GW - WIN_W)
        xs = jnp.where((k >= c0) & (k < c0 + WIN_W), xs, NEG)
        neg = jnp.full((GW, GW), NEG, f32)
        for t in range(NT):
            for j in range(2):
                for u in range(KR):
                    dr = pieces[t][j][u]
                    piece = neg if dr is None else xs[dr * GW:(dr + 1) * GW, 0:GW]
                    o_ref[t, u * GW:(u + 1) * GW, j * GW:(j + 1) * GW] = piece

    return _hbm_call(
        body, name="bias_prep", out_shape=SDS((H, NT, KB, QB), f32), grid=(H,),
        in_specs=[pl.BlockSpec((None, N_DR, 128), lambda h: (h, 0, 0))],
        out_specs=pl.BlockSpec((None, NT, KB, QB), lambda h: (h, 0, 0, 0)),
    )(rpb_rev_pad)


def _bias_tiles(rpb2, token=None):
    rpb_pad = jnp.pad(rpb2[:, :, ::-1], ((0, 0), (0, 0), (0, 128 - N_DC)))
    return _bias_prep(rpb_pad if token is None else rpb_pad + token)


def _block_geom(b):
    qs = b * QB
    ks = min(max(2 * b - 4, 0), ROWS - KR) * GW
    t = b if b < 2 else (b - (NQB - NT) if b > NQB - 3 else 2)
    return qs, ks, t


def _tt(a, b):
    return lax.dot_general(a, b, (((1,), (1,)), ((), ())), preferred_element_type=f32)


def _tn(a, b):
    return lax.dot_general(a, b, (((0,), (0,)), ((), ())), preferred_element_type=f32)


def _softmax_t(s_lat, s_ctx):
    m = jnp.maximum(jnp.max(s_lat, axis=0, keepdims=True), jnp.max(s_ctx, axis=0, keepdims=True))
    e_lat = jnp.exp(s_lat - m)
    e_ctx = jnp.exp(s_ctx - m)
    inv = 1.0 / (jnp.sum(e_lat, axis=0, keepdims=True) + jnp.sum(e_ctx, axis=0, keepdims=True))
    return e_lat * inv, e_ctx * inv


def _staged(n_blocks, stages):
    held = [dict() for _ in stages]
    for step in range(n_blocks + len(stages) - 1):
        for s, fn in enumerate(stages):
            b = step - s
            if 0 <= b < n_blocks:
                held[s][b] = fn(b) if s == 0 else fn(b, held[s - 1].pop(b))


def _attn_fwd(qr, qp, kr, vh, kc, vc, btt):
    def body(qr_ref, qp_ref, kr_ref, v_ref, kc_ref, vc_ref, bt_ref, o_ref):
        kcv, vcv = kc_ref[...], vc_ref[...]

        def scores(b):
            qs, ks, t = _block_geom(b)
            return (_tt(kr_ref[ks:ks + KB, :], qr_ref[qs:qs + QB, :]) + bt_ref[t], _tt(kcv, qp_ref[qs:qs + QB, :]))

        def probs(b, sc):
            p_lat, p_ctx = _softmax_t(*sc)
            return p_lat.astype(bf16), p_ctx.astype(bf16)

        def values(b, p):
            qs, ks, _ = _block_geom(b)
            o_ref[qs:qs + QB, :] = _tn(p[0], v_ref[ks:ks + KB, :]) + _tn(p[1], vcv)

        _staged(NQB, (scores, probs, values))

    sq = pl.BlockSpec((None, S, DH), lambda h: (h, 0, 0))
    sc = pl.BlockSpec((None, L, DH), lambda h: (h, 0, 0))
    return _hbm_call(
        body, name="attn_fwd", out_shape=SDS((H, S, DH), f32), grid=(H,),
        in_specs=[sq, sq, sq, sq, sc, sc, pl.BlockSpec((None, NT, KB, QB), lambda h: (h, 0, 0, 0))],
        out_specs=sq, compiler_params=_cp(48),
    )(qr, qp, kr, vh, kc, vc, btt)


def _shift_rows(v, down):
    n = v.shape[0]
    row = lax.broadcasted_iota(i32, v.shape, 0)
    if down:
        return jnp.where(row == 0, 0.0, pltpu.roll(v, 1, 0))
    return jnp.where(row == n - 1, 0.0, pltpu.roll(v, n - 1, 0))


def _conv_specs():
    col = lambda off: pl.BlockSpec((S, 128), lambda i, off=off: (0, off + i))
    return [col(16), col(20), col(24), col(28), pl.BlockSpec((3, 128), lambda i: (0, i)),
            pl.BlockSpec((1, 128), lambda i: (0, i))]


def _conv_fwd(p, conv_w, conv_b):
    def body(u_ref, bg_ref, cg_ref, zc_ref, w_ref, b_ref, o_ref):
        cu = cg_ref[...] * u_ref[...]
        cv = b_ref[...] + _shift_rows(cu, True) * w_ref[0:1, :]
        cv = cv + cu * w_ref[1:2, :]
        cv = cv + _shift_rows(cu, False) * w_ref[2:3, :]
        o_ref[...] = ((bg_ref[...] * cv) * _silu(zc_ref[...])).astype(bf16)

    return _hbm_call(
        body, name="conv_fwd", out_shape=SDS((S, DC), bf16), grid=(DC // 128,),
        in_specs=_conv_specs(), out_specs=pl.BlockSpec((S, 128), lambda i: (0, i)), compiler_params=_cp(40),
    )(p, p, p, p, conv_w, conv_b)


DP_Q, DP_K, DP_V, DP_ZA, DP_U, DP_BG, DP_CG, DP_ZC = range(8)


def _out_proj_loss(o, p, conv_g, w_out, xx, tgt, mrow, b_ada):
    tm = 256

    def body(o_ref, za_ref, c_ref, w_ref, x_ref, t_ref, m_ref, b_ref,
             dy_ref, dconv_ref, dp_ref, do_ref, gwo_ref, dgate_ref, loss_ref):
        k = pl.program_id(0)

        @pl.when(k == 0)
        def _():
            gwo_ref[...] = jnp.zeros_like(gwo_ref)
            dgate_ref[...] = jnp.zeros_like(dgate_ref)
            loss_ref[0, 0] = 0.0

        gate = m_ref[:, 2 * D:3 * D] + b_ref[:, 2 * D:3 * D]
        za = za_ref[...]
        sz = _silu(za)
        om = _merge_heads(o_ref)
        av, cv = (om * sz).astype(bf16), c_ref[...]
        mo = jnp.dot(av, w_ref[0:DA, :], preferred_element_type=f32)
        mo = mo + jnp.dot(cv, w_ref[DA:DA + DC, :], preferred_element_type=f32)
        y = x_ref[...] + gate * mo
        diff = y - t_ref[...]
        loss_ref[0, 0] += jnp.sum(diff * diff)
        dy = diff * (1.0 / D)
        dy_ref[...] = dy
        dgate_ref[...] += jnp.sum(dy * mo, axis=0, keepdims=True)
        dmo = (dy * gate).astype(bf16)
        dmix = _tt(dmo, w_ref[...])
        dattn = dmix[:, 0:DA]
        dconv_ref[...] = dmix[:, DA:DA + DC]
        a = dattn * sz
        for hh in range(H):
            do_ref[hh] = a[:, hh * DH:(hh + 1) * DH].astype(bf16)
        dp_ref[...] = ((dattn * _dsilu(za)) * om).astype(bf16)
        gwo_ref[0:DA, :] += _tn(av, dmo)
        gwo_ref[DA:DA + DC, :] += _tn(cv, dmo)

    row = lambda i: (i, 0)
    fixed = lambda i: (0, 0)
    hspec = pl.BlockSpec((H, tm, DH), lambda i: (0, i, 0))
    return _hbm_call(
        body, name="out_proj_loss",
        out_shape=(SDS((S, D), f32), SDS((S, DC), f32), SDS((8, S, DA), bf16), SDS((H, S, DH), bf16),
                   SDS((D, D), f32), SDS((1, D), f32), SDS((1, 1), f32)),
        grid=(S // tm,),
        in_specs=[hspec, pl.BlockSpec((tm, DA), lambda i: (i, 3)), pl.BlockSpec((tm, DC), row),
                  pl.BlockSpec((D, D), fixed), pl.BlockSpec((tm, D), row), pl.BlockSpec((tm, D), row),
                  pl.BlockSpec((1, 3 * D), fixed), pl.BlockSpec((1, 3 * D), fixed)],
        out_specs=(pl.BlockSpec((tm, D), row), pl.BlockSpec((tm, DC), row),
                   pl.BlockSpec((None, tm, DA), lambda i: (DP_ZA, i, 0)), hspec, pl.BlockSpec((D, D), fixed),
                   pl.BlockSpec((1, D), fixed), SMEM_SPEC),
        compiler_params=_cp(56, dimension_semantics=("arbitrary",)),
    )(o, p, conv_g, w_out, xx, tgt, mrow, b_ada)


def _conv_bwd(dconv, p, conv_w, conv_b, dp8, after=None):
    def body(d_ref, u_ref, bg_ref, cg_ref, zc_ref, w_ref, b_ref, dp_in_ref, after_ref, dp_ref, gw_ref, gb_ref):
        du_ref, dbg_ref, dcg_ref, dzc_ref = dp_ref.at[0], dp_ref.at[1], dp_ref.at[2], dp_ref.at[3]
        dconv = d_ref[...]
        u, bg, cg, zc = u_ref[...], bg_ref[...], cg_ref[...], zc_ref[...]
        w0, w1, w2 = w_ref[0:1, :], w_ref[1:2, :], w_ref[2:3, :]
        cu = cg * u
        cu_m, cu_p = _shift_rows(cu, True), _shift_rows(cu, False)
        cv = b_ref[...] + cu_m * w0
        cv = cv + cu * w1
        cv = cv + cu_p * w2
        sz = _silu(zc)
        dbg_ref[...] = ((dconv * sz) * cv).astype(bf16)
        dzc_ref[...] = ((dconv * (bg * cv)) * _dsilu(zc)).astype(bf16)
        dcv = (dconv * sz) * bg
        gb_ref[...] = jnp.sum(dcv, axis=0, keepdims=True)
        gw_ref[0:1, :] = jnp.sum(dcv * cu_m, axis=0, keepdims=True)
        gw_ref[1:2, :] = jnp.sum(dcv * cu, axis=0, keepdims=True)
        gw_ref[2:3, :] = jnp.sum(dcv * cu_p, axis=0, keepdims=True)
        gw_ref[3:8, :] = jnp.zeros((5, 128), f32)
        dcu = _shift_rows(dcv, False) * w0 + dcv * w1 + _shift_rows(dcv, True) * w2
        dcg_ref[...] = (dcu * u).astype(bf16)
        du_ref[...] = (dcu * cg).astype(bf16)

    return _hbm_call(
        body, name="conv_bwd", out_shape=(SDS((8, S, DC), bf16), SDS((8, DC), f32), SDS((1, DC), f32)),
        grid=(DC // 128,),
        in_specs=[pl.BlockSpec((S, 128), lambda i: (0, i))] + _conv_specs() + [ANY_SPEC, ANY_SPEC],
        out_specs=(pl.BlockSpec((4, S, 128), lambda i: (DP_U // 4, 0, i)), pl.BlockSpec((8, 128), lambda i: (0, i)),
                   pl.BlockSpec((1, 128), lambda i: (0, i))),
        input_output_aliases={7: 0}, compiler_params=_cp(48),
    )(dconv, p, p, p, p, conv_w, conv_b, dp8, conv_b if after is None else after)


def _attn_bwd(qr, qp, kr, vh, kc, vc, btt, do, after=None):
    def body(qr_ref, qp_ref, kr_ref, v_ref, kc_ref, vc_ref, bt_ref, do_ref, after_ref,
             dqr_ref, dqp_ref, dkr_ref, dv_ref, dkc_ref, dvc_ref, dbt_ref):
        kcv, vcv = kc_ref[...], vc_ref[...]
        dkr_ref[...] = jnp.zeros_like(dkr_ref)
        dv_ref[...] = jnp.zeros_like(dv_ref)
        dbt_ref[...] = jnp.zeros_like(dbt_ref)
        ctx_acc = {}

        def products(b):
            qs, ks, t = _block_geom(b)
            dob = do_ref[qs:qs + QB, :]
            s_lat = _tt(kr_ref[ks:ks + KB, :], qr_ref[qs:qs + QB, :]) + bt_ref[t]
            s_ctx = _tt(kcv, qp_ref[qs:qs + QB, :])
            return s_lat, s_ctx, _tt(v_ref[ks:ks + KB, :], dob), _tt(vcv, dob)

        def score_grads(b, x):
            s_lat, s_ctx, dp_lat, dp_ctx = x
            p_lat, p_ctx = _softmax_t(s_lat, s_ctx)
            delta = jnp.sum(p_lat * dp_lat, axis=0, keepdims=True) + jnp.sum(p_ctx * dp_ctx, axis=0, keepdims=True)
            ds_lat = p_lat * (dp_lat - delta)
            ds_ctx = p_ctx * (dp_ctx - delta)
            return ds_lat, ds_lat.astype(bf16), ds_ctx.astype(bf16), p_lat.astype(bf16), p_ctx.astype(bf16)

        def operand_grads(b, y):
            qs, ks, t = _block_geom(b)
            ds_lat, dsb_lat, dsb_ctx, pb_lat, pb_ctx = y
            qrb, qpb, dob = qr_ref[qs:qs + QB, :], qp_ref[qs:qs + QB, :], do_ref[qs:qs + QB, :]
            dbt_ref[t] += ds_lat
            dqr_ref[qs:qs + QB, :] = _tn(dsb_lat, kr_ref[ks:ks + KB, :])
            dqp_ref[qs:qs + QB, :] = _tn(dsb_ctx, kcv)
            dkr_ref[ks:ks + KB, :] += jnp.dot(dsb_lat, qrb, preferred_element_type=f32)
            dv_ref[ks:ks + KB, :] += jnp.dot(pb_lat, dob, preferred_element_type=f32)
            dkc = jnp.dot(dsb_ctx, qpb, preferred_element_type=f32)
            dvc = jnp.dot(pb_ctx, dob, preferred_element_type=f32)
            ctx_acc["k"] = dkc if b == 0 else ctx_acc["k"] + dkc
            ctx_acc["v"] = dvc if b == 0 else ctx_acc["v"] + dvc

        _staged(NQB, (products, score_grads, operand_grads))
        dkc_ref[...] = ctx_acc["k"]
        dvc_ref[...] = ctx_acc["v"]

    sq = pl.BlockSpec((None, S, DH), lambda h: (h, 0, 0))
    sc = pl.BlockSpec((None, L, DH), lambda h: (h, 0, 0))
    sb = pl.BlockSpec((None, NT, KB, QB), lambda h: (h, 0, 0, 0))
    big, ctxs = SDS((H, S, DH), f32), SDS((H, L, DH), f32)
    return _hbm_call(
        body, name="attn_bwd", out_shape=(big, big, big, big, ctxs, ctxs, SDS((H, NT, KB, QB), f32)), grid=(H,),
        in_specs=[sq, sq, sq, sq, sc, sc, sb, sq, ANY_SPEC], out_specs=(sq, sq, sq, sq, sc, sc, sb),
        compiler_params=_cp(56),
    )(qr, qp, kr, vh, kc, vc, btt, do, do if after is None else after)


def _bias_bwd(dbtt):
    pieces = _tile_pieces()

    def body(d_ref, o_ref, scr):
        scr[...] = jnp.zeros_like(scr)
        acc = [None] * N_DR
        for t in range(NT):
            for j in range(2):
                for u in range(KR):
                    dr = pieces[t][j][u]
                    if dr is None:
                        continue
                    piece = d_ref[t, u * GW:(u + 1) * GW, j * GW:(j + 1) * GW]
                    acc[dr] = piece if acc[dr] is None else acc[dr] + piece
        for dr in range(N_DR):
            scr[dr * GW:(dr + 1) * GW, 0:GW] = acc[dr]
        xs = pltpu.roll(scr[...], WIN_W - 1, 1)
        row = lax.broadcasted_iota(i32, xs.shape, 0)
        for b in range(6):
            xs = jnp.where(((row >> b) & 1) == 1, pltpu.roll(xs, 128 - (1 << b), 1), xs)
        rev = jnp.sum(xs.reshape(N_DR, GW, 128), axis=1)
        a = lax.broadcasted_iota(i32, (128, 128), 0)
        b = lax.broadcasted_iota(i32, (128, 128), 1)
        flip = ((a + b == N_DC - 1) & (a < N_DC)).astype(f32)
        o_ref[...] = jnp.dot(rev, flip, precision=HIGHEST, preferred_element_type=f32)

    return _hbm_call(
        body, name="bias_bwd", out_shape=SDS((H, N_DR, 128), f32), grid=(H,),
        in_specs=[pl.BlockSpec((None, NT, KB, QB), lambda h: (h, 0, 0, 0))],
        out_specs=pl.BlockSpec((None, N_DR, 128), lambda h: (h, 0, 0)),
        scratch_shapes=[pltpu.VMEM((N_DR * GW, 128), f32)],
    )(dbtt)


def _merge_heads(ref):
    return jnp.concatenate([ref[hh] for hh in range(H)], axis=1)


def _head_norm_bwd(xraw, gain, dy, ones_bd):
    r = lax.rsqrt(_head_sum(xraw * xraw, ones_bd) * (1.0 / DH) + RMS_EPS)
    xh = xraw * r
    gdy = dy * gain
    dx = r * (gdy - xh * (_head_sum(xh * gdy, ones_bd) * (1.0 / DH)))
    return dx, jnp.sum(dy * xh, axis=0, keepdims=True)


def _qk_bwd(dqr, dqp, dkr, dvh, p, gq, gk, rope, dp8):
    tm = 256

    def body(dqr_ref, dqp_ref, dkr_ref, dv_ref, qk_ref, gq_ref, gk_ref, cc_ref, cr_ref, sc_ref, sr_ref, dp_in_ref,
             dp_ref, ggq_ref, ggk_ref):
        dq_ref, dk_ref, dvo_ref = dp_ref.at[DP_Q], dp_ref.at[DP_K], dp_ref.at[DP_V]

        @pl.when(pl.program_id(0) == 0)
        def _():
            ggq_ref[...] = jnp.zeros_like(ggq_ref)
            ggk_ref[...] = jnp.zeros_like(ggk_ref)

        ones_bd = _head_ones()
        cs, sn = _rope_block(cc_ref, cr_ref, tm), _rope_block(sc_ref, sr_ref, tm)
        a = _merge_heads(dqr_ref)
        dyq = ((a * cs - _swap16(a) * sn) + _merge_heads(dqp_ref)) * QK_SCALE
        bk = _merge_heads(dkr_ref)
        dyk = bk * cs - _swap16(bk) * sn
        dq, gq_part = _head_norm_bwd(qk_ref[:, 0:DA], gq_ref[...], dyq, ones_bd)
        dk, gk_part = _head_norm_bwd(qk_ref[:, DA:2 * DA], gk_ref[...], dyk, ones_bd)
        dq_ref[...] = dq.astype(bf16)
        dk_ref[...] = dk.astype(bf16)
        dvo_ref[...] = _merge_heads(dv_ref).astype(bf16)
        ggq_ref[...] += gq_part
        ggk_ref[...] += gk_part

    hspec = pl.BlockSpec((H, tm, DH), lambda i: (0, i, 0))
    fixed = pl.BlockSpec((1, DA), lambda i: (0, 0))
    return _hbm_call(
        body, name="qk_bwd", out_shape=(SDS((8, S, DA), bf16), SDS((1, DA), f32), SDS((1, DA), f32)), grid=(S // tm,),
        in_specs=[hspec, hspec, hspec, hspec, pl.BlockSpec((tm, 2 * DA), lambda i: (i, 0)), fixed, fixed]
        + _rope_specs(tm) + [ANY_SPEC],
        out_specs=(pl.BlockSpec((3, tm, DA), lambda i: (0, i, 0)), fixed, fixed), input_output_aliases={11: 0},
        compiler_params=_cp(40, dimension_semantics=("arbitrary",)),
    )(dqr, dqp, dkr, dvh, p, gq, gk, *rope, dp8)


def _ctx_bwd(dkc, dvc, pc, gk):
    def body(dkc_ref, dvc_ref, p_ref, gk_ref, dk_ref, dv_ref, ggk_ref):
        ones_bd = _head_ones()
        dk, gk_part = _head_norm_bwd(p_ref[:, 0:DA], gk_ref[...], _merge_heads(dkc_ref), ones_bd)
        dk_ref[...] = dk.astype(bf16)
        dv_ref[...] = _merge_heads(dvc_ref).astype(bf16)
        ggk_ref[...] = gk_part

    piece = SDS((L, DA), bf16)
    return _hbm_call(
        body, name="ctx_bwd", out_shape=(piece, piece, SDS((1, DA), f32)), in_specs=[VMEM_SPEC] * 4,
        out_specs=(VMEM_SPEC,) * 3,
    )(dkc, dvc, pc, gk)


def _grad_w_in(h, dp8, hc, dkc_raw, dvc_m):
    tm = 512

    def body(h_ref, p_ref, hc_ref, dk_ref, dv_ref, g_ref):
        j, k = pl.program_id(0), pl.program_id(1)

        @pl.when(k == 0)
        def _():
            g_ref[...] = jnp.zeros_like(g_ref)

        @pl.when((k == 0) & (j == 0))
        def _():
            g_ref[:, DA:2 * DA] = _tn(hc_ref[...], dk_ref[...])

        @pl.when((k == 0) & (j == 1))
        def _():
            g_ref[:, 0:DA] = _tn(hc_ref[...], dv_ref[...])

        hv = h_ref[...]
        g_ref[:, 0:DA] += _tn(hv, p_ref[0])
        g_ref[:, DA:2 * DA] += _tn(hv, p_ref[1])

    fixed = lambda j, k: (0, 0)
    return _hbm_call(
        body, name="grad_w_in", out_shape=SDS((4, D, D), f32), grid=(4, S // tm),
        in_specs=[pl.BlockSpec((tm, D), lambda j, k: (k, 0)), pl.BlockSpec((2, tm, DA), lambda j, k: (j, k, 0)),
                  pl.BlockSpec((L, D), fixed), pl.BlockSpec((L, DA), fixed), pl.BlockSpec((L, DA), fixed)],
        out_specs=pl.BlockSpec((None, D, D), lambda j, k: (j, 0, 0)),
        compiler_params=_cp(40, dimension_semantics=("arbitrary", "arbitrary")),
    )(h, dp8, hc, dkc_raw, dvc_m)


def _norm_mod_bwd(x, dh, g, scale):
    r = lax.rsqrt(jnp.mean(x * x, axis=-1, keepdims=True) + RMS_EPS)
    xh = x * r
    y = xh * g
    dshift = jnp.sum(dh, axis=0, keepdims=True)
    dscale = jnp.sum(dh * y, axis=0, keepdims=True)
    dyn = dh * (1.0 + scale)
    dg = jnp.sum(dyn * xh, axis=0, keepdims=True)
    gdy = dyn * g
    dx = r * (gdy - xh * jnp.mean(xh * gdy, axis=-1, keepdims=True))
    return dx, dshift, dscale, dg


def _dh_grad_x(dp8, w4, xx, dy, norm_g, mrow, b_ada):
    tm = 256

    def body(p_ref, w_ref, x_ref, dy_ref, g_ref, m_ref, b_ref, gx_ref, dsh_ref, dsc_ref, dg_ref):
        @pl.when(pl.program_id(0) == 0)
        def _():
            dsh_ref[...] = jnp.zeros_like(dsh_ref)
            dsc_ref[...] = jnp.zeros_like(dsc_ref)
            dg_ref[...] = jnp.zeros_like(dg_ref)

        dh = None
        for j in range(4):
            for half in range(2):
                term = _tt(p_ref[2 * j + half], w_ref[j, :, half * DA:(half + 1) * DA])
                dh = term if dh is None else dh + term
        scale = m_ref[:, D:2 * D] + b_ref[:, D:2 * D]
        dx, dshift, dscale, dg = _norm_mod_bwd(x_ref[...], dh, g_ref[...], scale)
        gx_ref[...] = dy_ref[...] + dx
        dsh_ref[...] += dshift
        dsc_ref[...] += dscale
        dg_ref[...] += dg

    row = lambda i: (i, 0)
    fixed = lambda i: (0, 0)
    vec = SDS((1, D), f32)
    return _hbm_call(
        body, name="dh_grad_x", out_shape=(SDS((S, D), f32), vec, vec, vec), grid=(S // tm,),
        in_specs=[pl.BlockSpec((8, tm, DA), lambda i: (0, i, 0)), pl.BlockSpec((4, D, D), lambda i: (0, 0, 0)),
                  pl.BlockSpec((tm, D), row), pl.BlockSpec((tm, D), row), pl.BlockSpec((1, D), fixed),
                  pl.BlockSpec((1, 3 * D), fixed), pl.BlockSpec((1, 3 * D), fixed)],
        out_specs=(pl.BlockSpec((tm, D), row), pl.BlockSpec((1, D), fixed), pl.BlockSpec((1, D), fixed),
                   pl.BlockSpec((1, D), fixed)),
        compiler_params=_cp(56, dimension_semantics=("arbitrary",)),
    )(dp8, w4, xx, dy, norm_g, mrow, b_ada)


def _dhc_sums(dkc_raw, dvc_m, w4, ctx2, norm_g, mrow_c, b_ada):
    def body(dk_ref, dv_ref, w0_ref, w1_ref, x_ref, g_ref, m_ref, b_ref, dsh_ref, dsc_ref, dg_ref):
        dh = _tt(dk_ref[...], w0_ref[:, DA:2 * DA]) + _tt(dv_ref[...], w1_ref[:, 0:DA])
        scale = m_ref[:, D:2 * D] + b_ref[:, D:2 * D]
        _, dshift, dscale, dg = _norm_mod_bwd(x_ref[...], dh, g_ref[...], scale)
        dsh_ref[...] = dshift
        dsc_ref[...] = dscale
        dg_ref[...] = dg

    fixed = lambda i: (0, 0)
    vec = SDS((1, D), f32)
    vspec = pl.BlockSpec((1, D), fixed)
    return _hbm_call(
        body, name="dhc_sums", out_shape=(vec, vec, vec), grid=(1,),
        in_specs=[pl.BlockSpec((L, DA), fixed), pl.BlockSpec((L, DA), fixed),
                  pl.BlockSpec((None, D, D), lambda i: (0, 0, 0)), pl.BlockSpec((None, D, D), lambda i: (1, 0, 0)),
                  pl.BlockSpec((L, D), fixed), vspec, pl.BlockSpec((1, 3 * D), fixed), pl.BlockSpec((1, 3 * D), fixed)],
        out_specs=(vspec, vspec, vspec), compiler_params=_cp(32),
    )(dkc_raw, dvc_m, w4, w4, ctx2, norm_g, mrow_c, b_ada)


def _rope_tables():
    nf = DH // 4
    inv = np.float32(ROPE_THETA) ** (-np.arange(nf, dtype=np.float32) / np.float32(nf))
    ang_c = np.arange(GW, dtype=np.float32)[:, None] * inv
    ang_r = np.arange(ROWS, dtype=np.float32)[:, None] * inv
    zc, zr = np.zeros((GW, 2 * nf), np.float32), np.zeros((ROWS, 2 * nf), np.float32)
    ct_cos = np.tile(np.concatenate([zc, np.cos(ang_c), np.cos(ang_c)], axis=1), (1, H))
    ct_sin = np.tile(np.concatenate([zc, -np.sin(ang_c), np.sin(ang_c)], axis=1), (1, H))
    rt_cos = np.tile(np.concatenate([np.cos(ang_r), np.cos(ang_r), zr], axis=1), (1, H))
    rt_sin = np.tile(np.concatenate([-np.sin(ang_r), np.sin(ang_r), zr], axis=1), (1, H))
    rep8 = lambda t: np.ascontiguousarray(np.broadcast_to(t[:, None, :], (ROWS, 8, DA))).reshape(ROWS * 8, DA)
    return tuple(jnp.asarray(t, f32) for t in (ct_cos, rep8(rt_cos), ct_sin, rep8(rt_sin)))


def _local_step(xx, ctx2, tgt, mrow, mrow_c, b_ada, norm_g, weights, q_norm_g, k_norm_g, btb, conv_w_full, conv_b,
                hooks=None):
    hooks = hooks or {}
    gq = jnp.tile(q_norm_g, (1, H))
    gk = jnp.tile(k_norm_g, (1, H))
    rope = _rope_tables()

    h = _prenorm(xx, norm_g, mrow, b_ada, 256, "prenorm_x")
    hc = _prenorm(ctx2, norm_g, mrow_c, b_ada, L, "prenorm_ctx")
    jv = weights["jvec"]
    p = _in_proj_own(h, weights["own"], jv)
    w4 = weights["near"](p)
    p = _in_proj_block(h, w4, p, jv ^ 1, "in_proj_y")
    p = _in_proj_block(h, w4, p, jv ^ 2, "in_proj_x")
    w4, started = weights["far"](w4, p)
    p = _in_proj_block(h, w4, p, jv ^ 3, "in_proj_xy", after=started)
    pc = _ctx_proj(hc, w4)
    qr, qp, kr, vh = _qk_prep(p, gq, gk, rope)
    kc, vc = _ctx_prep(pc, gk)
    o = _attn_fwd(qr, qp, kr, vh, kc, vc, btb)
    conv_g = _conv_fwd(p, conv_w_full, conv_b)
    w_out_full = weights["out"](o)
    dy, dconv, dp8, do, g_w_out, dgate, loss_sum = _out_proj_loss(o, p, conv_g, w_out_full, xx, tgt, mrow, b_ada)
    started = hooks["g_w_out"](g_w_out) if "g_w_out" in hooks else None
    dp8, g_conv_w, g_conv_b = _conv_bwd(dconv, p, conv_w_full, conv_b, dp8, after=started)
    started = hooks["after_conv"](dp8) if "after_conv" in hooks else None
    dqr, dqp, dkr, dvh, dkc, dvc, dbtb = _attn_bwd(qr, qp, kr, vh, kc, vc, btb, do, after=started)
    g_rpb = _bias_bwd(dbtb)
    dp8, g_gq, g_gk = _qk_bwd(dqr, dqp, dkr, dvh, p, gq, gk, rope, dp8)
    dkc_raw, dvc_m, g_gk_c = _ctx_bwd(dkc, dvc, pc, gk)
    g_w_in = _grad_w_in(h, dp8, hc, dkc_raw, dvc_m)
    b_ada_late = hooks["g_w_in"](g_w_in, b_ada) if "g_w_in" in hooks else b_ada
    grad_x, dshift, dscale, dng = _dh_grad_x(dp8, w4, xx, dy, norm_g, mrow, b_ada_late)
    dshift_c, dscale_c, dng_c = _dhc_sums(dkc_raw, dvc_m, w4, ctx2, norm_g, mrow_c, b_ada_late)
    return dict(loss_sum=loss_sum, grad_x=grad_x, g_w_in=g_w_in, g_w_out=g_w_out, g_conv_w=g_conv_w,
                g_conv_b=g_conv_b, g_rpb=g_rpb, g_gq=g_gq, g_gk=g_gk, g_gk_c=g_gk_c, dshift=dshift, dscale=dscale,
                dgate=dgate, dng=dng, dshift_c=dshift_c, dscale_c=dscale_c, dng_c=dng_c)


def _pair_sum_w_in(g, r, cvec):
    tr = 128

    def body(c_ref, g_ref, r_ref, t32_ref, tb_ref):
        t = g_ref[...] + r_ref[...]
        t32_ref[...] = t
        tb_ref[...] = t.astype(bf16)

    half = D // 2
    g_spec = pl.BlockSpec((4, tr, D), lambda i, c: (0, c[0] * (half // tr) + i, 0))
    o_spec = pl.BlockSpec((4, tr, D), lambda i, c: (0, i, 0))
    grid_spec = pltpu.PrefetchScalarGridSpec(num_scalar_prefetch=1, grid=(half // tr,), in_specs=[g_spec, o_spec],
                                             out_specs=(o_spec, o_spec))
    return _hbm_call(body, name="pair_sum_w_in", out_shape=(SDS((4, half, D), f32), SDS((4, half, D), bf16)),
                     grid_spec=grid_spec, compiler_params=_cp(40))(cvec, g, r)


def _pair_sum_w_out(g, r, cvec):
    hr = D // 8

    def body(c_ref, g0, g1, g2, g3, r_ref, t32_ref, tb_ref):
        for q, g_ref in enumerate((g0, g1, g2, g3)):
            t = g_ref[...] + r_ref[q]
            t32_ref[q] = t
            tb_ref[q] = t.astype(bf16)

    gspecs = [pl.BlockSpec((hr, D), lambda i, c, q=q: (2 * q + c[0], 0)) for q in range(4)]
    full = pl.BlockSpec((4, hr, D), lambda i, c: (0, 0, 0))
    grid_spec = pltpu.PrefetchScalarGridSpec(num_scalar_prefetch=1, grid=(1,), in_specs=gspecs + [full],
                                             out_specs=(full, full))
    return _hbm_call(body, name="pair_sum_w_out", out_shape=(SDS((4, hr, D), f32), SDS((4, hr, D), bf16)),
                          grid_spec=grid_spec)(cvec, g, g, g, g, r)


def _chip_sum(t32, r2, jvec, name):
    rows = t32.shape[1]
    tr = min(rows, 128)

    def body(j_ref, t_ref, r_ref, u_ref):
        u_ref[...] = ((t_ref[...] + r_ref[0].astype(f32)) + r_ref[1].astype(f32)) + r_ref[2].astype(f32)

    grid_spec = pltpu.PrefetchScalarGridSpec(
        num_scalar_prefetch=1, grid=(rows // tr,),
        in_specs=[pl.BlockSpec((None, tr, D), lambda i, j: (j[0], i, 0)), pl.BlockSpec((3, tr, D), lambda i, j: (0, i, 0))],
        out_specs=pl.BlockSpec((tr, D), lambda i, j: (i, 0)))
    return _hbm_call(body, name=name, out_shape=SDS((rows, D), f32), grid_spec=grid_spec)(jvec, t32, r2)


_PK = {}
_off = 0
for _name, _rows in (("dm", 24), ("dmc", 24), ("dng", 8), ("dng_c", 8), ("gq", 8), ("gk", 8), ("gk_c", 8),
                     ("rpb", H * N_DR), ("conv_b", 8), ("conv_w", 16), ("loss", 8)):
    _PK[_name] = (_off, _off + _rows)
    _off += _rows
PK_ROWS = _off
RS_B_ADA, RS_NORM_G, RS_GQ, RS_GK, RS_RPB, RS_CONV_B, RS_CONV_W, RS_DMC, RS_LOSS, RS_ROWS = (
    0, 24, 32, 40, 48, 168, 176, 192, 216, 224)


def _small_reduce(gathered):
    def body(g_ref, o_ref, dm_ref):
        a0 = _PK["dm"][0]
        dm_ref[...] = jnp.zeros_like(dm_ref)
        for b in range(8):
            for i in range(24):
                dm_ref[b:b + 1, 128 * i:128 * (i + 1)] = g_ref[b, a0 + i:a0 + i + 1, :]
        tot = g_ref[0]
        for b in range(1, 8):
            tot = tot + g_ref[b]

        def rows(name):
            a, z = _PK[name]
            return tot[a:z]

        o_ref[RS_B_ADA:RS_B_ADA + 24] = rows("dm") + rows("dmc")
        o_ref[RS_NORM_G:RS_NORM_G + 8] = rows("dng") + rows("dng_c")
        gq = jnp.broadcast_to(jnp.sum(rows("gq"), axis=0, keepdims=True), (8, 128))
        gk = jnp.broadcast_to(jnp.sum(rows("gk") + rows("gk_c"), axis=0, keepdims=True), (8, 128))
        o_ref[RS_GQ:RS_GQ + 8] = gq + pltpu.roll(gq, DH, 1)
        o_ref[RS_GK:RS_GK + 8] = gk + pltpu.roll(gk, DH, 1)
        o_ref[RS_RPB:RS_RPB + H * N_DR] = rows("rpb")
        o_ref[RS_CONV_B:RS_CONV_B + 8] = rows("conv_b")
        o_ref[RS_CONV_W:RS_CONV_W + 16] = rows("conv_w")
        dmc = rows("dmc")
        o_ref[RS_DMC:RS_DMC + 24] = dmc
        o_ref[RS_LOSS:RS_LOSS + 8] = rows("loss")
        for i in range(24):
            dm_ref[8:9, 128 * i:128 * (i + 1)] = dmc[i:i + 1]

    return _hbm_call(body, name="small_reduce", out_shape=(SDS((RS_ROWS, 128), f32), SDS((16, 3 * D), f32)),
                     in_specs=[VMEM_SPEC], out_specs=(VMEM_SPEC, VMEM_SPEC))(gathered)


def _w_ada_grad(sc16, dm16, w_ada_shard, jvec):
    ncol = w_ada_shard.shape[1]

    def body(j_ref, sc_ref, dm_ref, w_ref, g_ref, part_ref):
        dm = dm_ref[...]
        g_ref[...] = lax.dot_general(sc_ref[...], dm, (((0,), (0,)), ((), ())), precision=HIGHEST,
                                     preferred_element_type=f32)
        part_ref[...] = lax.dot_general(dm[8:16], w_ref[...], (((1,), (1,)), ((), ())), precision=HIGHEST,
                                        preferred_element_type=f32)

    fixed = lambda i, j: (0, 0)
    grid_spec = pltpu.PrefetchScalarGridSpec(
        num_scalar_prefetch=1, grid=(1,),
        in_specs=[pl.BlockSpec((16, D), fixed), pl.BlockSpec((16, ncol), lambda i, j: (0, j[0])),
                  pl.BlockSpec((D, ncol), fixed)],
        out_specs=(pl.BlockSpec((D, ncol), fixed), pl.BlockSpec((8, D), fixed)))
    return _pallas_call(body, name="w_ada_grad", out_shape=(SDS((D, ncol), f32), SDS((8, D), f32)),
                        grid_spec=grid_spec, compiler_params=_cp(40))(jvec, sc16, dm16, w_ada_shard)


def _c_ctx_grad(parts4, c_ctx):
    def body(p_ref, c_ref, o_ref):
        tot = ((p_ref[0] + p_ref[1]) + p_ref[2]) + p_ref[3]
        o_ref[...] = tot[0:1] * _dsilu(c_ref[...].reshape(1, D))

    return _pallas_call(body, name="c_ctx_grad", out_shape=SDS((1, D), f32), in_specs=[VMEM_SPEC, VMEM_SPEC],
                        out_specs=VMEM_SPEC)(parts4, c_ctx)


def _adamw(w, g, m, v, name):
    rows, cols = w.shape
    tr = 256 if rows % 256 == 0 else rows

    def body(w_ref, g_ref, m_ref, v_ref, d_ref, m2_ref, v2_ref):
        gv = g_ref[...]
        m2 = ADAM_B1 * m_ref[...] + (1.0 - ADAM_B1) * gv
        v2 = ADAM_B2 * v_ref[...] + (1.0 - ADAM_B2) * jnp.square(gv)
        m_hat = m2 / (1.0 - ADAM_B1 ** ADAM_STEP)
        v_hat = v2 / (1.0 - ADAM_B2 ** ADAM_STEP)
        d_ref[...] = -ADAM_LR * (m_hat / (jnp.sqrt(v_hat) + ADAM_EPS) + ADAM_WD * w_ref[...])
        m2_ref[...] = m2
        v2_ref[...] = v2

    spec = pl.BlockSpec((tr, cols), lambda i: (i, 0))
    shp = SDS((rows, cols), f32)
    return _hbm_call(body, name=name, out_shape=(shp, shp, shp), grid=(rows // tr,), in_specs=[spec] * 4,
                          out_specs=(spec, spec, spec))(w, g, m, v)


def _adamw_halves(w, g_mine, g_other, m, v, cvec, name):
    rows, cols = w.shape
    half = rows // 2
    tr = min(256, half)
    per_half = half // tr

    def body(c_ref, w_ref, ga_ref, gb_ref, m_ref, v_ref, g_ref, d_ref, m2_ref, v2_ref):
        in_my_half = (pl.program_id(0) // per_half) == c_ref[0]
        gv = jnp.where(in_my_half, ga_ref[...], gb_ref[...])
        g_ref[...] = gv
        m2 = ADAM_B1 * m_ref[...] + (1.0 - ADAM_B1) * gv
        v2 = ADAM_B2 * v_ref[...] + (1.0 - ADAM_B2) * jnp.square(gv)
        m_hat = m2 / (1.0 - ADAM_B1 ** ADAM_STEP)
        v_hat = v2 / (1.0 - ADAM_B2 ** ADAM_STEP)
        d_ref[...] = -ADAM_LR * (m_hat / (jnp.sqrt(v_hat) + ADAM_EPS) + ADAM_WD * w_ref[...])
        m2_ref[...] = m2
        v2_ref[...] = v2

    full = pl.BlockSpec((tr, cols), lambda i, c: (i, 0))
    part = pl.BlockSpec((tr, cols), lambda i, c: (i % per_half, 0))
    shp = SDS((rows, cols), f32)
    grid_spec = pltpu.PrefetchScalarGridSpec(num_scalar_prefetch=1, grid=(rows // tr,),
                                             in_specs=[full, part, part, full, full], out_specs=(full,) * 4)
    return _hbm_call(body, name=name, out_shape=(shp,) * 4, grid_spec=grid_spec)(cvec, w, g_mine, g_other, m, v)


def _adam_math(w, g, m, v):
    m2 = ADAM_B1 * m + (1.0 - ADAM_B1) * g
    v2 = ADAM_B2 * v + (1.0 - ADAM_B2) * jnp.square(g)
    m_hat = m2 / (1.0 - ADAM_B1 ** ADAM_STEP)
    v_hat = v2 / (1.0 - ADAM_B2 ** ADAM_STEP)
    return -ADAM_LR * (m_hat / (jnp.sqrt(v_hat) + ADAM_EPS) + ADAM_WD * w), m2, v2


def _adamw_small(red, g_c_ctx, jvec, ws, ms, vs):
    n = len(ws)

    def body(*refs):
        red_ref, gc_ref, j_ref = refs[:3]
        w_refs, m_refs, v_refs = refs[3:3 + n], refs[3 + n:3 + 2 * n], refs[3 + 2 * n:3 + 3 * n]
        outs = refs[3 + 3 * n:]
        g_out, d_out, m_out, v_out = outs[:n], outs[n:2 * n], outs[2 * n:3 * n], outs[3 * n:]
        chip = j_ref[0]
        lanes = lambda i: (slice(None), slice(128 * i, 128 * (i + 1)))
        row = lambda r0, i: (lambda: red_ref[r0 + i:r0 + i + 1, :])
        whole = (slice(None), slice(None))
        chunks = [
            [((slice(None),), lambda: gc_ref[...].reshape(D))],
            [(lanes(i), row(RS_B_ADA, i)) for i in range(3 * D // 128)],
            [(lanes(i), row(RS_NORM_G, i)) for i in range(D // 128)],
            [(whole, lambda: red_ref[RS_GQ:RS_GQ + 1, 0:DH])],
            [(whole, lambda: red_ref[RS_GK:RS_GK + 1, 0:DH])],
            [((dr,), (lambda dr=dr: red_ref[pl.ds(RS_RPB + dr, H, stride=N_DR), 0:N_DC])) for dr in range(N_DR)],
            [((r,), (lambda r=r: red_ref[pl.ds(RS_CONV_W + 4 * r + chip, 1), :])) for r in range(3)],
            [(lanes(i), row(RS_CONV_B, i)) for i in range(DC // 128)],
        ]
        for a in range(n):
            for idx, grad in chunks[a]:
                g = grad()
                d, m2, v2 = _adam_math(w_refs[a][idx], g, m_refs[a][idx], v_refs[a][idx])
                g_out[a][idx] = g
                d_out[a][idx] = d
                m_out[a][idx] = m2
                v_out[a][idx] = v2

    shapes = [SDS(w.shape, f32) for w in ws]
    res = _pallas_call(body, name="adamw_small", out_shape=shapes * 4,
                       in_specs=[VMEM_SPEC, VMEM_SPEC, SMEM_SPEC] + [VMEM_SPEC] * (3 * n),
                       out_specs=[VMEM_SPEC] * (4 * n))(red, g_c_ctx, jvec, *ws, *ms, *vs)
    return [list(res[k * n:(k + 1) * n]) for k in range(4)]


def _rows128(a):
    return a.reshape(-1, 128)


def kernel(x, c, ctx, c_ctx, w_ada, b_ada, norm_g, w_in, q_norm_g, k_norm_g, rpb, conv_w, conv_b, w_out, loss_target, m_c_ctx, m_w_ada, m_b_ada, m_norm_g, m_w_in, m_q_norm_g, m_k_norm_g, m_rpb, m_conv_w, m_conv_b, m_w_out, v_c_ctx, v_w_ada, v_b_ada, v_norm_g, v_w_in, v_q_norm_g, v_k_norm_g, v_rpb, v_conv_w, v_conv_b, v_w_out):
    xi, yi, ci = lax.axis_index("x"), lax.axis_index("y"), lax.axis_index("c")
    dev = 4 * xi + 2 * yi + ci
    chip = 2 * xi + yi
    cvec = jnp.reshape(ci, (1,)).astype(i32)
    jvec = jnp.reshape(chip, (1,)).astype(i32)
    w_ada_s = w_ada[0]
    ncol = w_ada_s.shape[1]

    gc = _split_start([_to_slot(c.reshape(8, 128), 8, dev)], [], 7, _gather8_copies, "gather_c_start")
    btb = _bias_tiles(rpb[0], gc[3][0:1, 0:1])
    (c8,), _ = _split_wait(gc[0], gc[1], [gc[2]], [], btb, _gather8_copies, "gather_c_wait")
    cc = jnp.concatenate([c8.reshape(8, D), c_ctx.reshape(1, D), jnp.zeros((7, D), f32)], axis=0)
    m_shard, sc16 = _adaln_shard(cc, w_ada_s)

    conv_w_pad = jnp.pad(conv_w[0], ((0, 5), (0, 0)))
    gm = _split_start([_to_slot(m_shard, 4, chip), _to_slot(conv_w_pad, 4, chip)], [], 6, _gather4_copies,
                      "gather_mod_start")
    w4c = _cast_to_slot(w_in[0], jvec, "cast_w_in")
    (m4, cw4), _ = _split_wait(gm[0], gm[1], [gm[2], gm[3]], [], w4c, _gather4_copies, "gather_mod_wait")

    all_k = [(0, 0, 0), (0, 0, 1), (0, 0, 2)]
    wo4c = _cast_to_slot(w_out[0], jvec, "cast_w_out")
    sem_a, rem_a, w4s, token = _halves_start([w4c], m4, all_k[0:2], "weights_near_start")
    m_full = jnp.transpose(m4, (1, 0, 2)).reshape(16, 4 * ncol) + token[0:1, 0:1]
    mrow = lax.dynamic_slice(m_full, (dev, 0), (1, 3 * D))
    mrow_c = m_full[8:9]
    conv_w_full = jnp.transpose(cw4[:, 0:3, :], (1, 0, 2)).reshape(3, DC)
    waves = {}

    def near(after):
        (w4w,) = _halves_wait(sem_a, rem_a, [w4s], after, all_k[0:2], "weights_near_wait")
        w4f = _halves_forward([w4w], [0, 1], "weights_near_forward")[0]
        sem_b, rem_b, w4b, _ = _halves_start([w4f], after, all_k[2:3], "weights_far_start")
        waves["far"] = (sem_b, rem_b)
        return w4b

    def far(w4, after):
        (w4w,) = _halves_wait(*waves["far"], [w4], after, all_k[2:3], "weights_far_wait")
        w4f = _halves_forward([w4w], [2], "weights_far_forward")[0]
        sem_c, rem_c, wo4s, started = _halves_start([wo4c], w4f, all_k, "weights_out_start")
        waves["out"] = (sem_c, rem_c, wo4s)
        return w4f, started

    def w_out_gathered(after):
        sem_c, rem_c, wo4s = waves["out"]
        (wow,) = _halves_wait(sem_c, rem_c, [wo4s], after, all_k, "weights_out_wait")
        return _halves_forward([wow], [0, 1, 2], "weights_out_forward")[0].reshape(D, D)

    weights = dict(own=w_in[0], jvec=jvec, near=near, far=far, out=w_out_gathered)

    exchange = _exchange_copies
    pending = {}

    def on_g_w_out(g_w_out):
        out = _split_start([g_w_out], [SDS((4, D // 8, D), f32)], 4, exchange, "grad_out_pair_start")
        pending["ex_out"] = out
        return out[4]

    def after_conv(dp8):
        ssem_o, rsem_o, g_o, land_o, _ = pending["ex_out"]
        (g_o,), (ex_o,) = _split_wait(ssem_o, rsem_o, [g_o], [land_o], dp8, exchange, "grad_out_pair_wait")
        to32, tob = _pair_sum_w_out(g_o, ex_o, cvec)
        out = _split_start([tob], [SDS((3, D // 8, D), bf16)], 3, _scatter_copies, "grad_out_chip_start")
        pending["sc_out"] = (out, to32)
        return out[4]

    def on_g_w_in(g_w_in, b_ada_in):
        out = _split_start([g_w_in], [SDS((4, D // 2, D), f32)], 4, exchange, "grad_pair_start")
        pending["ex"] = (out[0], out[1], [out[2]], [out[3]])
        return b_ada_in + out[4][0:1, 0:1]

    r = _local_step(x[0], ctx[0], loss_target[0], mrow, mrow_c, b_ada, norm_g, weights, q_norm_g, k_norm_g,
                    btb, conv_w_full, conv_b, dict(g_w_out=on_g_w_out, after_conv=after_conv, g_w_in=on_g_w_in))
    dm = jnp.concatenate([r["dshift"], r["dscale"], r["dgate"]], axis=1)
    dmc = jnp.concatenate([r["dshift_c"], r["dscale_c"], jnp.zeros((1, D), f32)], axis=1)
    pack_parts = [_rows128(dm), _rows128(dmc), _rows128(r["dng"]), _rows128(r["dng_c"]), _rows128(r["g_gq"]),
                  _rows128(r["g_gk"]), _rows128(r["g_gk_c"]), r["g_rpb"].reshape(H * N_DR, 128),
                  _rows128(r["g_conv_b"]), _rows128(r["g_conv_w"][0:3]), jnp.pad(r["loss_sum"], ((0, 0), (0, 127)))]
    pack = jnp.concatenate([jnp.pad(p, ((0, -p.shape[0] % 8), (0, 0))) for p in pack_parts], axis=0)
    assert pack.shape[0] == PK_ROWS
    gs = _split_start([_to_slot(pack, 8, dev)], [], 7, _gather8_copies, "gather_small_start")

    ex_ssem, ex_rsem, ex_srcs, ex_lands = pending["ex"]
    ex_g, ex = _split_wait(ex_ssem, ex_rsem, ex_srcs, ex_lands, gs[3], exchange, "grad_pair_wait")
    t32, tb = _pair_sum_w_in(ex_g[0], ex[0], cvec)
    sc_out = _split_start([tb], [SDS((3, D // 2, D), bf16)], 3, _scatter_copies, "grad_chip_start")

    (gathered,), _ = _split_wait(gs[0], gs[1], [gs[2]], [], sc_out[4], _gather8_copies, "gather_small_wait")
    red, dm16 = _small_reduce(gathered)
    loss = red[RS_LOSS, 0] * (0.5 / D)

    g_w_ada_s, cpart = _w_ada_grad(sc16, dm16, w_ada_s, jvec)
    d_w_ada, nm_w_ada, nv_w_ada = _adamw(w_ada_s, g_w_ada_s, m_w_ada[0], v_w_ada[0], "adamw_w_ada")

    (cparts4,) = _chip_gather([cpart], "gather_c_ctx_parts", after=nm_w_ada)
    g_c_ctx = _c_ctx_grad(cparts4, c_ctx)

    _, (r2,) = _split_wait(sc_out[0], sc_out[1], [sc_out[2]], [sc_out[3]], g_c_ctx, _scatter_copies,
                           "grad_chip_wait")
    u_in = _chip_sum(t32, r2, jvec, "chip_sum_w_in")
    sc_o, to32 = pending["sc_out"]
    _, (ro2,) = _split_wait(sc_o[0], sc_o[1], [sc_o[2]], [sc_o[3]], u_in, _scatter_copies, "grad_out_chip_wait")
    u_out = _chip_sum(to32, ro2, jvec, "chip_sum_w_out")

    t_rpb = lambda a: jnp.transpose(a, (0, 2, 1, 3)).reshape(N_DR, H, N_DC)
    t_cw = lambda a: jnp.transpose(a, (1, 0, 2))
    small = _adamw_small(
        red, g_c_ctx, jvec,
        [c_ctx, b_ada, norm_g, q_norm_g, k_norm_g, t_rpb(rpb), t_cw(conv_w), conv_b],
        [m_c_ctx, m_b_ada, m_norm_g, m_q_norm_g, m_k_norm_g, t_rpb(m_rpb), t_cw(m_conv_w), m_conv_b],
        [v_c_ctx, v_b_ada, v_norm_g, v_q_norm_g, v_k_norm_g, t_rpb(v_rpb), t_cw(v_conv_w), v_conv_b])
    for kind in small:
        kind[5] = jnp.transpose(kind[5].reshape(1, N_DR, H, N_DC), (0, 2, 1, 3))
        kind[6] = jnp.transpose(kind[6], (1, 0, 2))

    o_in, o_out = _sibling_send([u_in, u_out], "grad_pair_send")
    g_w_in_s, d_w_in, nm_w_in, nv_w_in = _adamw_halves(w_in[0], u_in, o_in, m_w_in[0], v_w_in[0], cvec, "adamw_w_in")
    g_w_out_s, d_w_out, nm_w_out, nv_w_out = _adamw_halves(w_out[0], u_out, o_out, m_w_out[0], v_w_out[0], cvec,
                                                           "adamw_w_out")

    def ordered(kind, big_w_ada, big_w_in, big_w_out):
        s_c_ctx, s_b_ada, s_norm_g, s_q, s_k, s_rpb, s_conv_w, s_conv_b = small[kind]
        return [s_c_ctx, big_w_ada[None], s_b_ada, s_norm_g, big_w_in[None], s_q, s_k, s_rpb, s_conv_w,
                s_conv_b, big_w_out[None]]

    grads = ordered(0, g_w_ada_s, g_w_in_s, g_w_out_s)
    deltas = ordered(1, d_w_ada, d_w_in, d_w_out)
    new_m = ordered(2, nm_w_ada, nm_w_in, nm_w_out)
    new_v = ordered(3, nv_w_ada, nv_w_in, nv_w_out)
    return (loss, r["grad_x"][None], *grads, *deltas, *new_m, *new_v)
```

```python
import functools

import jax
import jax.numpy as jnp
import numpy as np
from jax import lax
from jax.experimental import pallas as pl
from jax.experimental.pallas import tpu as pltpu

f32, bf16, i32 = jnp.float32, jnp.bfloat16, jnp.int32
MESH = pl.DeviceIdType.MESH
HIGHEST = lax.Precision.HIGHEST

D = 1024
S = 2048
L = 256
GW = 64
ROWS = S // GW
H = 8
DH = 64
DA = H * DH
DC = 512
WIN_H, WIN_W = 8, 16
N_DR, N_DC = 2 * WIN_H - 1, 2 * WIN_W - 1
RMS_EPS = 1e-6
ROPE_THETA = 10000.0
QK_SCALE = DH ** -0.5
NEG = -1e30

QB = 128
NQB = S // QB
KR = 9
KB = KR * GW
TILE_GEOM = ((0, 0), (2, 0), (4, 0), (28, 23), (30, 23))
NT = len(TILE_GEOM)

ADAM_LR, ADAM_B1, ADAM_B2, ADAM_EPS, ADAM_WD, ADAM_STEP = 0.001, 0.9, 0.999, 1e-08, 0.01, 10

VMEM_SPEC = pl.BlockSpec(memory_space=pltpu.VMEM)
ANY_SPEC = pl.BlockSpec(memory_space=pl.ANY)
SMEM_SPEC = pl.BlockSpec(memory_space=pltpu.SMEM)
SDS = jax.ShapeDtypeStruct


_pallas_call = pl.pallas_call


def _hbm_call(body, *, out_shape, in_specs=None, out_specs=None, grid_spec=None, **kw):
    n_pre = 0
    if grid_spec is not None:
        ispecs, ospecs, n_pre = grid_spec.in_specs, grid_spec.out_specs, grid_spec.num_scalar_prefetch
        kw["grid_spec"] = grid_spec
    else:
        ispecs, ospecs = in_specs, out_specs
        kw.update(in_specs=in_specs, out_specs=out_specs)

    def blocked(spec):
        return isinstance(spec, pl.BlockSpec) and spec.block_shape is not None

    single = not isinstance(out_shape, (tuple, list))
    shapes = [out_shape] if single else list(out_shape)
    ospec_list = list(ospecs) if isinstance(ospecs, (tuple, list)) else [ospecs]
    shapes = [pltpu.HBM(s.shape, s.dtype) if blocked(sp) else s for s, sp in zip(shapes, ospec_list)]
    call = _pallas_call(body, out_shape=shapes[0] if single else tuple(shapes), **kw)

    def run(*args):
        arrays = [pltpu.with_memory_space_constraint(a, pltpu.HBM) if blocked(sp) else a
                  for a, sp in zip(args[n_pre:], ispecs)]
        return call(*args[:n_pre], *arrays)

    return run


def _cp(vmem_mb=None, **kw):
    if vmem_mb is not None:
        kw["vmem_limit_bytes"] = vmem_mb << 20
    return pltpu.CompilerParams(**kw)


def _silu(z):
    return z * jax.nn.sigmoid(z)


def _dsilu(z):
    sg = jax.nn.sigmoid(z)
    return sg * (1.0 + z * (1.0 - sg))


def _row_start(i):
    return min(max(i - WIN_H // 2, 0), ROWS - WIN_H)


def _my_pos():
    return lax.axis_index("x"), lax.axis_index("y"), lax.axis_index("c")


def _flip(v, bit):
    return 1 - v if bit else v


def _chip_gather(smalls, name, after=None):
    ns = len(smalls)

    def body(*refs):
        s_in, s_out = refs[:ns], refs[ns + 1:2 * ns + 1]
        ssem, rsem, lsem = refs[2 * ns + 1:]
        x, y, c = _my_pos()
        j = 2 * x + y
        chips = _peer_chips(x, y, c)
        local = [pltpu.make_async_copy(s_in[a], s_out[a].at[j], lsem.at[a]) for a in range(ns)]
        for cp in local:
            cp.start()
        sends = []
        for a in range(ns):
            for k in range(3):
                cp = pltpu.make_async_remote_copy(src_ref=s_in[a], dst_ref=s_out[a].at[j], send_sem=ssem.at[3 * a + k],
                                                  recv_sem=rsem.at[3 * a + k], device_id=chips[k][0], device_id_type=MESH)
                cp.start()
                sends.append(cp)
        for a in range(ns):
            for k in range(3):
                pltpu.make_async_remote_copy(src_ref=s_in[a], dst_ref=s_out[a].at[chips[k][1]], send_sem=ssem.at[3 * a + k],
                                             recv_sem=rsem.at[3 * a + k], device_id=chips[k][0],
                                             device_id_type=MESH).wait_recv()
        for cp in sends:
            cp.wait_send()
        for cp in local:
            cp.wait()

    return _hbm_call(
        body, name=name, out_shape=[SDS((4,) + a.shape, a.dtype) for a in smalls],
        in_specs=[VMEM_SPEC] * ns + [ANY_SPEC], out_specs=[VMEM_SPEC] * ns,
        scratch_shapes=[pltpu.SemaphoreType.DMA((3 * ns,)), pltpu.SemaphoreType.DMA((3 * ns,)),
                        pltpu.SemaphoreType.DMA((ns,))],
    )(*smalls, smalls[0] if after is None else after)


HBM_SPEC = pl.BlockSpec(memory_space=pltpu.HBM)
SEM_SPEC = pl.BlockSpec(memory_space=pltpu.SEMAPHORE)
DATAFLOW = pltpu.SideEffectType.DATAFLOW_SIDE_EFFECTING


def _peer_chips(x, y, c):
    out = []
    for k in range(1, 4):
        px, py = _flip(x, (k >> 1) & 1), _flip(y, k & 1)
        out.append(((px, py, c), 2 * px + py))
    return out


def _half_copies(srcs, dsts, ssem, rsem, which):
    x, y, c = _my_pos()
    j = 2 * x + y
    peers = _peer_chips(x, y, c)
    pairs = []
    for pos, group, k in which:
        half = srcs[pos].shape[1] // 2
        mine = pl.ds(pl.multiple_of(c * half, 8), half)
        dev, pj = peers[k]
        sem = 3 * group + k
        send = pltpu.make_async_remote_copy(src_ref=srcs[pos].at[j, mine], dst_ref=dsts[pos].at[j, mine],
                                            send_sem=ssem.at[sem], recv_sem=rsem.at[sem], device_id=dev,
                                            device_id_type=MESH)
        arrive = pltpu.make_async_remote_copy(src_ref=srcs[pos].at[j, mine], dst_ref=dsts[pos].at[pj, mine],
                                              send_sem=ssem.at[sem], recv_sem=rsem.at[sem], device_id=dev,
                                              device_id_type=MESH)
        pairs.append((send, arrive))
    return pairs


def _halves_start(bigs, after, order, name):
    nb = len(bigs)

    def body(*refs):
        b_in = refs[:nb]
        ssem, rsem = refs[nb + 1], refs[nb + 2]
        b_out = refs[nb + 3:2 * nb + 3]
        token = refs[2 * nb + 3]
        for send, _ in _half_copies(b_in, b_out, ssem, rsem, order):
            send.start()
        token[...] = jnp.zeros_like(token)

    out_shape = (pltpu.SemaphoreType.DMA((3 * nb,)), pltpu.SemaphoreType.DMA((3 * nb,)),
                 *[pltpu.HBM(b.shape, b.dtype) for b in bigs], SDS((8, 128), f32))
    return _hbm_call(
        body, name=name, out_shape=out_shape, in_specs=[HBM_SPEC] * nb + [ANY_SPEC],
        out_specs=(SEM_SPEC, SEM_SPEC, *[HBM_SPEC] * nb, VMEM_SPEC),
        input_output_aliases={a: 2 + a for a in range(nb)}, compiler_params=_cp(has_side_effects=DATAFLOW),
    )(*[pltpu.with_memory_space_constraint(b, pltpu.HBM) for b in bigs], after)


def _halves_wait(ssem, rsem, bigs, after, which, name):
    nb = len(bigs)

    def body(*refs):
        b_in = refs[:nb]
        ssem_ref, rsem_ref = refs[nb], refs[nb + 1]
        for send, arrive in _half_copies(b_in, b_in, ssem_ref, rsem_ref, which):
            send.wait_send()
            arrive.wait_recv()

    return _hbm_call(
        body, name=name, out_shape=tuple(pltpu.HBM(b.shape, b.dtype) for b in bigs),
        in_specs=[HBM_SPEC] * nb + [SEM_SPEC, SEM_SPEC, ANY_SPEC], out_specs=tuple([HBM_SPEC] * nb),
        input_output_aliases={a: a for a in range(nb)}, compiler_params=_cp(has_side_effects=DATAFLOW),
    )(*bigs, ssem, rsem, after)


def _halves_forward(bigs, relations, name):
    nb, nr = len(bigs), len(relations)

    def body(*refs):
        b_in, b_out = refs[:nb], refs[nb:2 * nb]
        ssem, rsem = refs[2 * nb:]
        x, y, c = _my_pos()
        sib = (x, y, 1 - c)
        peers = _peer_chips(x, y, c)
        sends = []
        for a in range(nb):
            half = b_in[a].shape[1] // 2
            mine = pl.ds(pl.multiple_of(c * half, 8), half)
            for i, k in enumerate(relations):
                pj = peers[k][1]
                cp = pltpu.make_async_remote_copy(src_ref=b_in[a].at[pj, mine], dst_ref=b_out[a].at[pj, mine],
                                                  send_sem=ssem.at[nr * a + i], recv_sem=rsem.at[nr * a + i],
                                                  device_id=sib, device_id_type=MESH)
                cp.start()
                sends.append(cp)
        for a in range(nb):
            half = b_in[a].shape[1] // 2
            other = pl.ds(pl.multiple_of((1 - c) * half, 8), half)
            for i, k in enumerate(relations):
                pj = peers[k][1]
                pltpu.make_async_remote_copy(src_ref=b_in[a].at[pj, other], dst_ref=b_out[a].at[pj, other],
                                             send_sem=ssem.at[nr * a + i], recv_sem=rsem.at[nr * a + i],
                                             device_id=sib, device_id_type=MESH).wait_recv()
        for cp in sends:
            cp.wait_send()

    return _hbm_call(
        body, name=name, out_shape=[SDS(b.shape, b.dtype) for b in bigs], in_specs=[ANY_SPEC] * nb,
        out_specs=[ANY_SPEC] * nb, input_output_aliases={a: a for a in range(nb)},
        scratch_shapes=[pltpu.SemaphoreType.DMA((nr * nb,)), pltpu.SemaphoreType.DMA((nr * nb,))],
    )(*bigs)


def _cast_to_slot(w, jvec, name):
    rows, cols = w.shape
    tr = 256

    def body(j_ref, w_ref, o_ref):
        o_ref[...] = w_ref[...].astype(bf16)

    grid_spec = pltpu.PrefetchScalarGridSpec(
        num_scalar_prefetch=1, grid=(rows // tr,), in_specs=[pl.BlockSpec((tr, cols), lambda i, j: (i, 0))],
        out_specs=pl.BlockSpec((None, tr, cols), lambda i, j: (j[0], i, 0)))
    return _hbm_call(body, name=name, out_shape=SDS((4, rows, cols), bf16), grid_spec=grid_spec)(jvec, w)


def _exchange_copies(srcs, lands, ssem, rsem):
    x, y, c = _my_pos()
    cps = []
    for a in range(len(srcs)):
        stacked = len(srcs[a].shape) == 3
        rb = srcs[a].shape[1] if stacked else srcs[a].shape[0] // 4
        half = rb // 2
        for jb in range(4):
            if stacked:
                src = srcs[a].at[jb, pl.ds(pl.multiple_of((1 - c) * half, 8), half)]
            else:
                src = srcs[a].at[pl.ds(pl.multiple_of(jb * rb + (1 - c) * half, 8), half)]
            cps.append(pltpu.make_async_remote_copy(src_ref=src, dst_ref=lands[a].at[jb], send_sem=ssem.at[4 * a + jb],
                                                    recv_sem=rsem.at[4 * a + jb], device_id=(x, y, 1 - c),
                                                    device_id_type=MESH))
    return cps


def _scatter_copies(srcs, lands, ssem, rsem):
    x, y, c = _my_pos()
    cps = []
    for a in range(len(srcs)):
        for k, (dev, pj) in enumerate(_peer_chips(x, y, c)):
            cps.append(pltpu.make_async_remote_copy(src_ref=srcs[a].at[pj], dst_ref=lands[a].at[k],
                                                    send_sem=ssem.at[3 * a + k], recv_sem=rsem.at[3 * a + k],
                                                    device_id=dev, device_id_type=MESH))
    return cps


class _SlotCopy:
    def __init__(self, src, dst, mine, theirs, ssem, rsem, dev):
        make = lambda slot: pltpu.make_async_remote_copy(src_ref=src.at[mine], dst_ref=dst.at[slot], send_sem=ssem,
                                                         recv_sem=rsem, device_id=dev, device_id_type=MESH)
        send, arrive = make(mine), make(theirs)
        self.start, self.wait_send, self.wait_recv = send.start, send.wait_send, arrive.wait_recv


def _gather8_copies(srcs, dsts, ssem, rsem):
    x, y, c = _my_pos()
    me = 4 * x + 2 * y + c
    cps = []
    for a in range(len(srcs)):
        for k in range(1, 8):
            tgt = (_flip(x, (k >> 2) & 1), _flip(y, (k >> 1) & 1), _flip(c, k & 1))
            cps.append(_SlotCopy(srcs[a], dsts[a], me, 4 * tgt[0] + 2 * tgt[1] + tgt[2], ssem.at[7 * a + k - 1],
                                 rsem.at[7 * a + k - 1], tgt))
    return cps


def _gather4_copies(srcs, dsts, ssem, rsem):
    x, y, c = _my_pos()
    cps = []
    for a in range(len(srcs)):
        for k, (dev, pj) in enumerate(_peer_chips(x, y, c)):
            cps.append(_SlotCopy(srcs[a], dsts[a], 2 * x + y, pj, ssem.at[3 * a + k], rsem.at[3 * a + k], dev))
    return cps


def _to_slot(a, n, i):
    return lax.dynamic_update_slice(jnp.zeros((n,) + a.shape, a.dtype), a[None], (i,) + (0,) * a.ndim)


def _split_start(srcs, land_shapes, n_cp, make, name):
    ns, nl = len(srcs), len(land_shapes)

    def body(*refs):
        s_in = refs[:ns]
        ssem, rsem = refs[ns + nl], refs[ns + nl + 1]
        l_out = refs[2 * ns + nl + 2:2 * ns + 2 * nl + 2]
        token = refs[2 * ns + 2 * nl + 2]
        s_out = refs[ns + nl + 2:2 * ns + nl + 2]
        for cp in make(s_in, l_out if nl else s_out, ssem, rsem):
            cp.start()
        token[...] = jnp.zeros_like(token)

    lands = [pltpu.with_memory_space_constraint(lax.empty(sh.shape, sh.dtype), pltpu.HBM) for sh in land_shapes]
    out_shape = (pltpu.SemaphoreType.DMA((n_cp,)), pltpu.SemaphoreType.DMA((n_cp,)),
                 *[pltpu.HBM(b.shape, b.dtype) for b in srcs], *[pltpu.HBM(b.shape, b.dtype) for b in land_shapes],
                 SDS((8, 128), f32))
    return _hbm_call(
        body, name=name, out_shape=out_shape, in_specs=[HBM_SPEC] * (ns + nl),
        out_specs=(SEM_SPEC, SEM_SPEC, *[HBM_SPEC] * (ns + nl), VMEM_SPEC),
        input_output_aliases={i: 2 + i for i in range(ns + nl)}, compiler_params=_cp(has_side_effects=DATAFLOW),
    )(*[pltpu.with_memory_space_constraint(b, pltpu.HBM) for b in srcs], *lands)


def _split_wait(ssem, rsem, srcs, lands, after, make, name):
    ns, nl = len(srcs), len(lands)

    def body(*refs):
        s_in, l_in = refs[:ns], refs[ns:ns + nl]
        ssem_ref, rsem_ref = refs[ns + nl], refs[ns + nl + 1]
        for cp in make(s_in, l_in if nl else s_in, ssem_ref, rsem_ref):
            cp.wait_send()
            cp.wait_recv()

    outs = _hbm_call(
        body, name=name, out_shape=tuple(pltpu.HBM(b.shape, b.dtype) for b in (*srcs, *lands)),
        in_specs=[HBM_SPEC] * (ns + nl) + [SEM_SPEC, SEM_SPEC, ANY_SPEC], out_specs=tuple([HBM_SPEC] * (ns + nl)),
        input_output_aliases={i: i for i in range(ns + nl)}, compiler_params=_cp(has_side_effects=DATAFLOW),
    )(*srcs, *lands, ssem, rsem, after)
    return list(outs[:ns]), list(outs[ns:])


def _sibling_send(halves, name):
    n = len(halves)

    def body(*refs):
        ins, outs = refs[:n], refs[n:2 * n]
        ssem, rsem = refs[2 * n:]
        x, y, c = _my_pos()
        cps = []
        for a in range(n):
            cp = pltpu.make_async_remote_copy(src_ref=ins[a], dst_ref=outs[a], send_sem=ssem.at[a],
                                              recv_sem=rsem.at[a], device_id=(x, y, 1 - c), device_id_type=MESH)
            cp.start()
            cps.append(cp)
        for cp in cps:
            cp.wait_recv()
        for cp in cps:
            cp.wait_send()

    out_shape = [SDS(h.shape, h.dtype) for h in halves]
    return _hbm_call(
        body, name=name, out_shape=out_shape, in_specs=[ANY_SPEC] * n, out_specs=[ANY_SPEC] * n,
        scratch_shapes=[pltpu.SemaphoreType.DMA((n,)), pltpu.SemaphoreType.DMA((n,))],
    )(*halves)


def _adaln_shard(cc, w_ada_shard):
    def body(c_ref, w_ref, m_ref, sc_ref):
        sc = _silu(c_ref[...])
        sc_ref[...] = sc
        m_ref[...] = jnp.dot(sc, w_ref[...], precision=HIGHEST, preferred_element_type=f32)

    return _hbm_call(
        body, name="adaln_shard", out_shape=(SDS((16, w_ada_shard.shape[1]), f32), SDS((16, D), f32)),
        in_specs=[VMEM_SPEC, VMEM_SPEC], out_specs=(VMEM_SPEC, VMEM_SPEC), compiler_params=_cp(32),
    )(cc, w_ada_shard)


def _prenorm(xx, norm_g, mrow, b_ada, tm, name):
    n = xx.shape[0]

    def body(x_ref, g_ref, m_ref, b_ref, h_ref):
        x = x_ref[...]
        shift = m_ref[:, 0:D] + b_ref[:, 0:D]
        scale = m_ref[:, D:2 * D] + b_ref[:, D:2 * D]
        r = lax.rsqrt(jnp.mean(x * x, axis=-1, keepdims=True) + RMS_EPS)
        y = (x * r) * g_ref[...]
        h_ref[...] = (y * (1.0 + scale) + shift).astype(bf16)

    row = lambda i: (i, 0)
    fixed = lambda i: (0, 0)
    return _hbm_call(
        body, name=name, out_shape=SDS((n, D), bf16), grid=(n // tm,),
        in_specs=[pl.BlockSpec((tm, D), row), pl.BlockSpec((1, D), fixed), pl.BlockSpec((1, 3 * D), fixed),
                  pl.BlockSpec((1, 3 * D), fixed)],
        out_specs=pl.BlockSpec((tm, D), row),
    )(xx, norm_g, mrow, b_ada)


def _in_proj_own(h, w_own, jvec):
    tm = 512

    def body(j_ref, h_ref, w_ref, p_ref):
        p_ref[...] = jnp.dot(h_ref[...], w_ref[...].astype(bf16), preferred_element_type=f32)

    grid_spec = pltpu.PrefetchScalarGridSpec(
        num_scalar_prefetch=1, grid=(S // tm,),
        in_specs=[pl.BlockSpec((tm, D), lambda i, j: (i, 0)), pl.BlockSpec((D, D), lambda i, j: (0, 0))],
        out_specs=pl.BlockSpec((tm, D), lambda i, j: (i, j[0])))
    return _hbm_call(body, name="in_proj_own", out_shape=SDS((S, 4 * D), f32), grid_spec=grid_spec,
                     compiler_params=_cp(40))(jvec, h, w_own)


def _in_proj_block(h, w4, p, bvec, name, after=None):
    tm = 512

    def body(b_ref, h_ref, w_ref, p_in_ref, after_ref, p_ref):
        p_ref[...] = jnp.dot(h_ref[...], w_ref[...], preferred_element_type=f32)

    grid_spec = pltpu.PrefetchScalarGridSpec(
        num_scalar_prefetch=1, grid=(S // tm,),
        in_specs=[pl.BlockSpec((tm, D), lambda i, b: (i, 0)), pl.BlockSpec((None, D, D), lambda i, b: (b[0], 0, 0)),
                  ANY_SPEC, ANY_SPEC],
        out_specs=pl.BlockSpec((tm, D), lambda i, b: (i, b[0])))
    return _hbm_call(body, name=name, out_shape=SDS((S, 4 * D), f32), grid_spec=grid_spec,
                     input_output_aliases={3: 0})(bvec, h, w4, p, bvec if after is None else after)


def _ctx_proj(hc, w4):
    def body(h_ref, w0_ref, w1_ref, p_ref):
        hv = h_ref[...]
        p_ref[:, 0:DA] = jnp.dot(hv, w0_ref[:, DA:2 * DA], preferred_element_type=f32)
        p_ref[:, DA:2 * DA] = jnp.dot(hv, w1_ref[:, 0:DA], preferred_element_type=f32)

    return _hbm_call(
        body, name="ctx_proj", out_shape=SDS((L, 2 * DA), f32), grid=(1,),
        in_specs=[pl.BlockSpec((L, D), lambda i: (0, 0)), pl.BlockSpec((None, D, D), lambda i: (0, 0, 0)),
                  pl.BlockSpec((None, D, D), lambda i: (1, 0, 0))],
        out_specs=pl.BlockSpec((L, 2 * DA), lambda i: (0, 0)),
    )(hc, w4, w4)


def _head_ones():
    r = lax.broadcasted_iota(i32, (DA, DA), 0) // DH
    c = lax.broadcasted_iota(i32, (DA, DA), 1) // DH
    return (r == c).astype(bf16)


def _head_sum(v, ones_bd):
    hi = v.astype(bf16)
    lo = (v - hi.astype(f32)).astype(bf16)
    return jnp.dot(hi, ones_bd, preferred_element_type=f32) + jnp.dot(lo, ones_bd, preferred_element_type=f32)


def _swap16(v):
    lane = lax.broadcasted_iota(i32, v.shape, 1)
    return jnp.where((lane & 31) < 16, pltpu.roll(v, DA - 16, 1), pltpu.roll(v, 16, 1))


def _rope_block(ct_ref, rt_ref, tm):
    rows = [jnp.tile(rt_ref[8 * j:8 * j + 8, :], (GW // 8, 1)) for j in range(tm // GW)]
    return jnp.tile(ct_ref[...], (tm // GW, 1)) + jnp.concatenate(rows, axis=0)


def _rope_specs(tm):
    col = pl.BlockSpec((GW, DA), lambda i: (0, 0))
    row = pl.BlockSpec((8 * tm // GW, DA), lambda i: (i, 0))
    return [col, row, col, row]


def _qk_prep(p, gq, gk, rope):
    tm = 256

    def body(qk_ref, v_ref, gq_ref, gk_ref, cc_ref, cr_ref, sc_ref, sr_ref, qr_ref, qp_ref, kr_ref, vh_ref):
        ones_bd = _head_ones()
        cs, sn = _rope_block(cc_ref, cr_ref, tm), _rope_block(sc_ref, sr_ref, tm)
        q = qk_ref[:, 0:DA]
        k = qk_ref[:, DA:2 * DA]
        yq = (q * lax.rsqrt(_head_sum(q * q, ones_bd) * (1.0 / DH) + RMS_EPS)) * gq_ref[...]
        yk = (k * lax.rsqrt(_head_sum(k * k, ones_bd) * (1.0 / DH) + RMS_EPS)) * gk_ref[...]
        qr = (yq * cs + _swap16(yq) * sn) * QK_SCALE
        qp = yq * QK_SCALE
        kr = yk * cs + _swap16(yk) * sn
        vv = v_ref[...]
        for hh in range(H):
            sl = slice(hh * DH, (hh + 1) * DH)
            qr_ref[hh] = qr[:, sl].astype(bf16)
            qp_ref[hh] = qp[:, sl].astype(bf16)
            kr_ref[hh] = kr[:, sl].astype(bf16)
            vh_ref[hh] = vv[:, sl].astype(bf16)

    hm = SDS((H, S, DH), bf16)
    hspec = pl.BlockSpec((H, tm, DH), lambda i: (0, i, 0))
    fixed = lambda i: (0, 0)
    return _hbm_call(
        body, name="qk_prep", out_shape=(hm, hm, hm, hm), grid=(S // tm,),
        in_specs=[pl.BlockSpec((tm, 2 * DA), lambda i: (i, 0)), pl.BlockSpec((tm, DA), lambda i: (i, 2)),
                  pl.BlockSpec((1, DA), fixed), pl.BlockSpec((1, DA), fixed)] + _rope_specs(tm),
        out_specs=(hspec, hspec, hspec, hspec),
    )(p, p, gq, gk, *rope)


def _ctx_prep(pc, gk):
    def body(p_ref, gk_ref, kc_ref, vc_ref):
        ones_bd = _head_ones()
        k = p_ref[:, 0:DA]
        yk = (k * lax.rsqrt(_head_sum(k * k, ones_bd) * (1.0 / DH) + RMS_EPS)) * gk_ref[...]
        vv = p_ref[:, DA:2 * DA]
        for hh in range(H):
            sl = slice(hh * DH, (hh + 1) * DH)
            kc_ref[hh] = yk[:, sl].astype(bf16)
            vc_ref[hh] = vv[:, sl].astype(bf16)

    hm = SDS((H, L, DH), bf16)
    return _hbm_call(
        body, name="ctx_prep", out_shape=(hm, hm), in_specs=[VMEM_SPEC, VMEM_SPEC], out_specs=(VMEM_SPEC, VMEM_SPEC),
    )(pc, gk)


def _tile_pieces():
    out = []
    for (i0, u0) in TILE_GEOM:
        rows = []
        for j in range(2):
            i = i0 + j
            rs = _row_start(i)
            rows.append([(u0 + u - i + WIN_H - 1) if rs <= u0 + u < rs + WIN_H else None for u in range(KR)])
        out.append(rows)
    return out


def _bias_prep(rpb_rev_pad):
    pieces = _tile_pieces()

    def body(r_ref, o_ref):
        rp = r_ref[...]
        xs = jnp.broadcast_to(rp[:, None, :], (N_DR, GW, 128)).reshape(N_DR * GW, 128)
        row = lax.broadcasted_iota(i32, xs.shape, 0)
        lane = lax.broadcasted_iota(i32, xs.shape, 1)
        for b in range(6):
            xs = jnp.where(((row >> b) & 1) == 1, pltpu.roll(xs, 1 << b, 1), xs)
        xs = pltpu.roll(xs, 128 - (WIN_W - 1), 1)
        k = row & (GW - 1)
        c0 = jnp.clip(lane - WIN_W // 2, 0, GW - WIN_W)
        xs = jnp.where((k >= c0) & (k < c0 + WIN_W), xs, NEG)
        neg = jnp.full((GW, GW), NEG, f32)
        for t in range(NT):
            for j in range(2):
                for u in range(KR):
                    dr = pieces[t][j][u]
                    piece = neg if dr is None else xs[dr * GW:(dr + 1) * GW, 0:GW]
                    o_ref[t, u * GW:(u + 1) * GW, j * GW:(j + 1) * GW] = piece

    return _hbm_call(
        body, name="bias_prep", out_shape=SDS((H, NT, KB, QB), f32), grid=(H,),
        in_specs=[pl.BlockSpec((None, N_DR, 128), lambda h: (h, 0, 0))],
        out_specs=pl.BlockSpec((None, NT, KB, QB), lambda h: (h, 0, 0, 0)),
    )(rpb_rev_pad)


def _bias_tiles(rpb2, token=None):
    rpb_pad = jnp.pad(rpb2[:, :, ::-1], ((0, 0), (0, 0), (0, 128 - N_DC)))
    return _bias_prep(rpb_pad if token is None else rpb_pad + token)


def _block_geom(b):
    qs = b * QB
    ks = min(max(2 * b - 4, 0), ROWS - KR) * GW
    t = b if b < 2 else (b - (NQB - NT) if b > NQB - 3 else 2)
    return qs, ks, t


def _tt(a, b):
    return lax.dot_general(a, b, (((1,), (1,)), ((), ())), preferred_element_type=f32)


def _tn(a, b):
    return lax.dot_general(a, b, (((0,), (0,)), ((), ())), preferred_element_type=f32)


def _softmax_t(s_lat, s_ctx):
    m = jnp.maximum(jnp.max(s_lat, axis=0, keepdims=True), jnp.max(s_ctx, axis=0, keepdims=True))
    e_lat = jnp.exp(s_lat - m)
    e_ctx = jnp.exp(s_ctx - m)
    inv = 1.0 / (jnp.sum(e_lat, axis=0, keepdims=True) + jnp.sum(e_ctx, axis=0, keepdims=True))
    return e_lat * inv, e_ctx * inv


def _staged(n_blocks, stages):
    held = [dict() for _ in stages]
    for step in range(n_blocks + len(stages) - 1):
        for s, fn in enumerate(stages):
            b = step - s
            if 0 <= b < n_blocks:
                held[s][b] = fn(b) if s == 0 else fn(b, held[s - 1].pop(b))


def _attn_fwd(qr, qp, kr, vh, kc, vc, btt):
    def body(qr_ref, qp_ref, kr_ref, v_ref, kc_ref, vc_ref, bt_ref, o_ref):
        kcv, vcv = kc_ref[...], vc_ref[...]

        def scores(b):
            qs, ks, t = _block_geom(b)
            return (_tt(kr_ref[ks:ks + KB, :], qr_ref[qs:qs + QB, :]) + bt_ref[t], _tt(kcv, qp_ref[qs:qs + QB, :]))

        def probs(b, sc):
            p_lat, p_ctx = _softmax_t(*sc)
            return p_lat.astype(bf16), p_ctx.astype(bf16)

        def values(b, p):
            qs, ks, _ = _block_geom(b)
            o_ref[qs:qs + QB, :] = _tn(p[0], v_ref[ks:ks + KB, :]) + _tn(p[1], vcv)

        _staged(NQB, (scores, probs, values))

    sq = pl.BlockSpec((None, S, DH), lambda h: (h, 0, 0))
    sc = pl.BlockSpec((None, L, DH), lambda h: (h, 0, 0))
    return _hbm_call(
        body, name="attn_fwd", out_shape=SDS((H, S, DH), f32), grid=(H,),
        in_specs=[sq, sq, sq, sq, sc, sc, pl.BlockSpec((None, NT, KB, QB), lambda h: (h, 0, 0, 0))],
        out_specs=sq, compiler_params=_cp(48),
    )(qr, qp, kr, vh, kc, vc, btt)


def _shift_rows(v, down):
    n = v.shape[0]
    row = lax.broadcasted_iota(i32, v.shape, 0)
    if down:
        return jnp.where(row == 0, 0.0, pltpu.roll(v, 1, 0))
    return jnp.where(row == n - 1, 0.0, pltpu.roll(v, n - 1, 0))


def _conv_specs():
    col = lambda off: pl.BlockSpec((S, 128), lambda i, off=off: (0, off + i))
    return [col(16), col(20), col(24), col(28), pl.BlockSpec((3, 128), lambda i: (0, i)),
            pl.BlockSpec((1, 128), lambda i: (0, i))]


def _conv_fwd(p, conv_w, conv_b):
    def body(u_ref, bg_ref, cg_ref, zc_ref, w_ref, b_ref, o_ref):
        cu = cg_ref[...] * u_ref[...]
        cv = b_ref[...] + _shift_rows(cu, True) * w_ref[0:1, :]
        cv = cv + cu * w_ref[1:2, :]
        cv = cv + _shift_rows(cu, False) * w_ref[2:3, :]
        o_ref[...] = ((bg_ref[...] * cv) * _silu(zc_ref[...])).astype(bf16)

    return _hbm_call(
        body, name="conv_fwd", out_shape=SDS((S, DC), bf16), grid=(DC // 128,),
        in_specs=_conv_specs(), out_specs=pl.BlockSpec((S, 128), lambda i: (0, i)), compiler_params=_cp(40),
    )(p, p, p, p, conv_w, conv_b)


DP_Q, DP_K, DP_V, DP_ZA, DP_U, DP_BG, DP_CG, DP_ZC = range(8)


def _out_proj_loss(o, p, conv_g, w_out, xx, tgt, mrow, b_ada):
    tm = 256

    def body(o_ref, za_ref, c_ref, w_ref, x_ref, t_ref, m_ref, b_ref,
             dy_ref, dconv_ref, dp_ref, do_ref, gwo_ref, dgate_ref, loss_ref):
        k = pl.program_id(0)

        @pl.when(k == 0)
        def _():
            gwo_ref[...] = jnp.zeros_like(gwo_ref)
            dgate_ref[...] = jnp.zeros_like(dgate_ref)
            loss_ref[0, 0] = 0.0

        gate = m_ref[:, 2 * D:3 * D] + b_ref[:, 2 * D:3 * D]
        za = za_ref[...]
        sz = _silu(za)
        om = _merge_heads(o_ref)
        av, cv = (om * sz).astype(bf16), c_ref[...]
        mo = jnp.dot(av, w_ref[0:DA, :], preferred_element_type=f32)
        mo = mo + jnp.dot(cv, w_ref[DA:DA + DC, :], preferred_element_type=f32)
        y = x_ref[...] + gate * mo
        diff = y - t_ref[...]
        loss_ref[0, 0] += jnp.sum(diff * diff)
        dy = diff * (1.0 / D)
        dy_ref[...] = dy
        dgate_ref[...] += jnp.sum(dy * mo, axis=0, keepdims=True)
        dmo = (dy * gate).astype(bf16)
        dmix = _tt(dmo, w_ref[...])
        dattn = dmix[:, 0:DA]
        dconv_ref[...] = dmix[:, DA:DA + DC]
        a = dattn * sz
        for hh in range(H):
            do_ref[hh] = a[:, hh * DH:(hh + 1) * DH].astype(bf16)
        dp_ref[...] = ((dattn * _dsilu(za)) * om).astype(bf16)
        gwo_ref[0:DA, :] += _tn(av, dmo)
        gwo_ref[DA:DA + DC, :] += _tn(cv, dmo)

    row = lambda i: (i, 0)
    fixed = lambda i: (0, 0)
    hspec = pl.BlockSpec((H, tm, DH), lambda i: (0, i, 0))
    return _hbm_call(
        body, name="out_proj_loss",
        out_shape=(SDS((S, D), f32), SDS((S, DC), f32), SDS((8, S, DA), bf16), SDS((H, S, DH), bf16),
                   SDS((D, D), f32), SDS((1, D), f32), SDS((1, 1), f32)),
        grid=(S // tm,),
        in_specs=[hspec, pl.BlockSpec((tm, DA), lambda i: (i, 3)), pl.BlockSpec((tm, DC), row),
                  pl.BlockSpec((D, D), fixed), pl.BlockSpec((tm, D), row), pl.BlockSpec((tm, D), row),
                  pl.BlockSpec((1, 3 * D), fixed), pl.BlockSpec((1, 3 * D), fixed)],
        out_specs=(pl.BlockSpec((tm, D), row), pl.BlockSpec((tm, DC), row),
                   pl.BlockSpec((None, tm, DA), lambda i: (DP_ZA, i, 0)), hspec, pl.BlockSpec((D, D), fixed),
                   pl.BlockSpec((1, D), fixed), SMEM_SPEC),
        compiler_params=_cp(56, dimension_semantics=("arbitrary",)),
    )(o, p, conv_g, w_out, xx, tgt, mrow, b_ada)


def _conv_bwd(dconv, p, conv_w, conv_b, dp8, after=None):
    def body(d_ref, u_ref, bg_ref, cg_ref, zc_ref, w_ref, b_ref, dp_in_ref, after_ref, dp_ref, gw_ref, gb_ref):
        du_ref, dbg_ref, dcg_ref, dzc_ref = dp_ref.at[0], dp_ref.at[1], dp_ref.at[2], dp_ref.at[3]
        dconv = d_ref[...]
        u, bg, cg, zc = u_ref[...], bg_ref[...], cg_ref[...], zc_ref[...]
        w0, w1, w2 = w_ref[0:1, :], w_ref[1:2, :], w_ref[2:3, :]
        cu = cg * u
        cu_m, cu_p = _shift_rows(cu, True), _shift_rows(cu, False)
        cv = b_ref[...] + cu_m * w0
        cv = cv + cu * w1
        cv = cv + cu_p * w2
        sz = _silu(zc)
        dbg_ref[...] = ((dconv * sz) * cv).astype(bf16)
        dzc_ref[...] = ((dconv * (bg * cv)) * _dsilu(zc)).astype(bf16)
        dcv = (dconv * sz) * bg
        gb_ref[...] = jnp.sum(dcv, axis=0, keepdims=True)
        gw_ref[0:1, :] = jnp.sum(dcv * cu_m, axis=0, keepdims=True)
        gw_ref[1:2, :] = jnp.sum(dcv * cu, axis=0, keepdims=True)
        gw_ref[2:3, :] = jnp.sum(dcv * cu_p, axis=0, keepdims=True)
        gw_ref[3:8, :] = jnp.zeros((5, 128), f32)
        dcu = _shift_rows(dcv, False) * w0 + dcv * w1 + _shift_rows(dcv, True) * w2
        dcg_ref[...] = (dcu * u).astype(bf16)
        du_ref[...] = (dcu * cg).astype(bf16)

    return _hbm_call(
        body, name="conv_bwd", out_shape=(SDS((8, S, DC), bf16), SDS((8, DC), f32), SDS((1, DC), f32)),
        grid=(DC // 128,),
        in_specs=[pl.BlockSpec((S, 128), lambda i: (0, i))] + _conv_specs() + [ANY_SPEC, ANY_SPEC],
        out_specs=(pl.BlockSpec((4, S, 128), lambda i: (DP_U // 4, 0, i)), pl.BlockSpec((8, 128), lambda i: (0, i)),
                   pl.BlockSpec((1, 128), lambda i: (0, i))),
        input_output_aliases={7: 0}, compiler_params=_cp(48),
    )(dconv, p, p, p, p, conv_w, conv_b, dp8, conv_b if after is None else after)


def _attn_bwd(qr, qp, kr, vh, kc, vc, btt, do, after=None):
    def body(qr_ref, qp_ref, kr_ref, v_ref, kc_ref, vc_ref, bt_ref, do_ref, after_ref,
             dqr_ref, dqp_ref, dkr_ref, dv_ref, dkc_ref, dvc_ref, dbt_ref):
        kcv, vcv = kc_ref[...], vc_ref[...]
        dkr_ref[...] = jnp.zeros_like(dkr_ref)
        dv_ref[...] = jnp.zeros_like(dv_ref)
        dbt_ref[...] = jnp.zeros_like(dbt_ref)
        ctx_acc = {}

        def products(b):
            qs, ks, t = _block_geom(b)
            dob = do_ref[qs:qs + QB, :]
            s_lat = _tt(kr_ref[ks:ks + KB, :], qr_ref[qs:qs + QB, :]) + bt_ref[t]
            s_ctx = _tt(kcv, qp_ref[qs:qs + QB, :])
            return s_lat, s_ctx, _tt(v_ref[ks:ks + KB, :], dob), _tt(vcv, dob)

        def score_grads(b, x):
            s_lat, s_ctx, dp_lat, dp_ctx = x
            p_lat, p_ctx = _softmax_t(s_lat, s_ctx)
            delta = jnp.sum(p_lat * dp_lat, axis=0, keepdims=True) + jnp.sum(p_ctx * dp_ctx, axis=0, keepdims=True)
            ds_lat = p_lat * (dp_lat - delta)
            ds_ctx = p_ctx * (dp_ctx - delta)
            return ds_lat, ds_lat.astype(bf16), ds_ctx.astype(bf16), p_lat.astype(bf16), p_ctx.astype(bf16)

        def operand_grads(b, y):
            qs, ks, t = _block_geom(b)
            ds_lat, dsb_lat, dsb_ctx, pb_lat, pb_ctx = y
            qrb, qpb, dob = qr_ref[qs:qs + QB, :], qp_ref[qs:qs + QB, :], do_ref[qs:qs + QB, :]
            dbt_ref[t] += ds_lat
            dqr_ref[qs:qs + QB, :] = _tn(dsb_lat, kr_ref[ks:ks + KB, :])
            dqp_ref[qs:qs + QB, :] = _tn(dsb_ctx, kcv)
            dkr_ref[ks:ks + KB, :] += jnp.dot(dsb_lat, qrb, preferred_element_type=f32)
            dv_ref[ks:ks + KB, :] += jnp.dot(pb_lat, dob, preferred_element_type=f32)
            dkc = jnp.dot(dsb_ctx, qpb, preferred_element_type=f32)
            dvc = jnp.dot(pb_ctx, dob, preferred_element_type=f32)
            ctx_acc["k"] = dkc if b == 0 else ctx_acc["k"] + dkc
            ctx_acc["v"] = dvc if b == 0 else ctx_acc["v"] + dvc

        _staged(NQB, (products, score_grads, operand_grads))
        dkc_ref[...] = ctx_acc["k"]
        dvc_ref[...] = ctx_acc["v"]

    sq = pl.BlockSpec((None, S, DH), lambda h: (h, 0, 0))
    sc = pl.BlockSpec((None, L, DH), lambda h: (h, 0, 0))
    sb = pl.BlockSpec((None, NT, KB, QB), lambda h: (h, 0, 0, 0))
    big, ctxs = SDS((H, S, DH), f32), SDS((H, L, DH), f32)
    return _hbm_call(
        body, name="attn_bwd", out_shape=(big, big, big, big, ctxs, ctxs, SDS((H, NT, KB, QB), f32)), grid=(H,),
        in_specs=[sq, sq, sq, sq, sc, sc, sb, sq, ANY_SPEC], out_specs=(sq, sq, sq, sq, sc, sc, sb),
        compiler_params=_cp(56),
    )(qr, qp, kr, vh, kc, vc, btt, do, do if after is None else after)


def _bias_bwd(dbtt, after=None):
    pieces = _tile_pieces()

    def body(d_ref, after_ref, o_ref, scr):
        scr[...] = jnp.zeros_like(scr)
        acc = [None] * N_DR
        for t in range(NT):
            for j in range(2):
                for u in range(KR):
                    dr = pieces[t][j][u]
                    if dr is None:
                        continue
                    piece = d_ref[t, u * GW:(u + 1) * GW, j * GW:(j + 1) * GW]
                    acc[dr] = piece if acc[dr] is None else acc[dr] + piece
        for dr in range(N_DR):
            scr[dr * GW:(dr + 1) * GW, 0:GW] = acc[dr]
        xs = pltpu.roll(scr[...], WIN_W - 1, 1)
        row = lax.broadcasted_iota(i32, xs.shape, 0)
        for b in range(6):
            xs = jnp.where(((row >> b) & 1) == 1, pltpu.roll(xs, 128 - (1 << b), 1), xs)
        rev = jnp.sum(xs.reshape(N_DR, GW, 128), axis=1)
        a = lax.broadcasted_iota(i32, (128, 128), 0)
        b = lax.broadcasted_iota(i32, (128, 128), 1)
        flip = ((a + b == N_DC - 1) & (a < N_DC)).astype(f32)
        o_ref[...] = jnp.dot(rev, flip, precision=HIGHEST, preferred_element_type=f32)

    return _hbm_call(
        body, name="bias_bwd", out_shape=SDS((H, N_DR, 128), f32), grid=(H,),
        in_specs=[pl.BlockSpec((None, NT, KB, QB), lambda h: (h, 0, 0, 0)), ANY_SPEC],
        out_specs=pl.BlockSpec((None, N_DR, 128), lambda h: (h, 0, 0)),
        scratch_shapes=[pltpu.VMEM((N_DR * GW, 128), f32)],
    )(dbtt, dbtt if after is None else after)


def _merge_heads(ref):
    return jnp.concatenate([ref[hh] for hh in range(H)], axis=1)


def _head_norm_bwd(xraw, gain, dy, ones_bd):
    r = lax.rsqrt(_head_sum(xraw * xraw, ones_bd) * (1.0 / DH) + RMS_EPS)
    xh = xraw * r
    gdy = dy * gain
    dx = r * (gdy - xh * (_head_sum(xh * gdy, ones_bd) * (1.0 / DH)))
    return dx, jnp.sum(dy * xh, axis=0, keepdims=True)


def _qk_bwd(dqr, dqp, dkr, dvh, p, gq, gk, rope, dp8):
    tm = 256

    def body(dqr_ref, dqp_ref, dkr_ref, dv_ref, qk_ref, gq_ref, gk_ref, cc_ref, cr_ref, sc_ref, sr_ref, dp_in_ref,
             dp_ref, ggq_ref, ggk_ref):
        dq_ref, dk_ref, dvo_ref = dp_ref.at[DP_Q], dp_ref.at[DP_K], dp_ref.at[DP_V]

        @pl.when(pl.program_id(0) == 0)
        def _():
            ggq_ref[...] = jnp.zeros_like(ggq_ref)
            ggk_ref[...] = jnp.zeros_like(ggk_ref)

        ones_bd = _head_ones()
        cs, sn = _rope_block(cc_ref, cr_ref, tm), _rope_block(sc_ref, sr_ref, tm)
        a = _merge_heads(dqr_ref)
        dyq = ((a * cs - _swap16(a) * sn) + _merge_heads(dqp_ref)) * QK_SCALE
        bk = _merge_heads(dkr_ref)
        dyk = bk * cs - _swap16(bk) * sn
        dq, gq_part = _head_norm_bwd(qk_ref[:, 0:DA], gq_ref[...], dyq, ones_bd)
        dk, gk_part = _head_norm_bwd(qk_ref[:, DA:2 * DA], gk_ref[...], dyk, ones_bd)
        dq_ref[...] = dq.astype(bf16)
        dk_ref[...] = dk.astype(bf16)
        dvo_ref[...] = _merge_heads(dv_ref).astype(bf16)
        ggq_ref[...] += gq_part
        ggk_ref[...] += gk_part

    hspec = pl.BlockSpec((H, tm, DH), lambda i: (0, i, 0))
    fixed = pl.BlockSpec((1, DA), lambda i: (0, 0))
    return _hbm_call(
        body, name="qk_bwd", out_shape=(SDS((8, S, DA), bf16), SDS((1, DA), f32), SDS((1, DA), f32)), grid=(S // tm,),
        in_specs=[hspec, hspec, hspec, hspec, pl.BlockSpec((tm, 2 * DA), lambda i: (i, 0)), fixed, fixed]
        + _rope_specs(tm) + [ANY_SPEC],
        out_specs=(pl.BlockSpec((3, tm, DA), lambda i: (0, i, 0)), fixed, fixed), input_output_aliases={11: 0},
        compiler_params=_cp(40, dimension_semantics=("arbitrary",)),
    )(dqr, dqp, dkr, dvh, p, gq, gk, *rope, dp8)


def _ctx_bwd(dkc, dvc, pc, gk):
    def body(dkc_ref, dvc_ref, p_ref, gk_ref, dk_ref, dv_ref, ggk_ref):
        ones_bd = _head_ones()
        dk, gk_part = _head_norm_bwd(p_ref[:, 0:DA], gk_ref[...], _merge_heads(dkc_ref), ones_bd)
        dk_ref[...] = dk.astype(bf16)
        dv_ref[...] = _merge_heads(dvc_ref).astype(bf16)
        ggk_ref[...] = gk_part

    piece = SDS((L, DA), bf16)
    return _hbm_call(
        body, name="ctx_bwd", out_shape=(piece, piece, SDS((1, DA), f32)), in_specs=[VMEM_SPEC] * 4,
        out_specs=(VMEM_SPEC,) * 3,
    )(dkc, dvc, pc, gk)


def _grad_w_in(h, dp8, hc, dkc_raw, dvc_m):
    tm = 512

    def body(h_ref, p_ref, hc_ref, dk_ref, dv_ref, g_ref):
        j, k = pl.program_id(0), pl.program_id(1)

        @pl.when(k == 0)
        def _():
            g_ref[...] = jnp.zeros_like(g_ref)

        @pl.when((k == 0) & (j == 0))
        def _():
            g_ref[:, DA:2 * DA] = _tn(hc_ref[...], dk_ref[...])

        @pl.when((k == 0) & (j == 1))
        def _():
            g_ref[:, 0:DA] = _tn(hc_ref[...], dv_ref[...])

        hv = h_ref[...]
        g_ref[:, 0:DA] += _tn(hv, p_ref[0])
        g_ref[:, DA:2 * DA] += _tn(hv, p_ref[1])

    fixed = lambda j, k: (0, 0)
    return _hbm_call(
        body, name="grad_w_in", out_shape=SDS((4, D, D), f32), grid=(4, S // tm),
        in_specs=[pl.BlockSpec((tm, D), lambda j, k: (k, 0)), pl.BlockSpec((2, tm, DA), lambda j, k: (j, k, 0)),
                  pl.BlockSpec((L, D), fixed), pl.BlockSpec((L, DA), fixed), pl.BlockSpec((L, DA), fixed)],
        out_specs=pl.BlockSpec((None, D, D), lambda j, k: (j, 0, 0)),
        compiler_params=_cp(40, dimension_semantics=("arbitrary", "arbitrary")),
    )(h, dp8, hc, dkc_raw, dvc_m)


def _norm_mod_bwd(x, dh, g, scale):
    r = lax.rsqrt(jnp.mean(x * x, axis=-1, keepdims=True) + RMS_EPS)
    xh = x * r
    y = xh * g
    dshift = jnp.sum(dh, axis=0, keepdims=True)
    dscale = jnp.sum(dh * y, axis=0, keepdims=True)
    dyn = dh * (1.0 + scale)
    dg = jnp.sum(dyn * xh, axis=0, keepdims=True)
    gdy = dyn * g
    dx = r * (gdy - xh * jnp.mean(xh * gdy, axis=-1, keepdims=True))
    return dx, dshift, dscale, dg


def _dh_grad_x(dp8, w4, xx, dy, norm_g, mrow, b_ada):
    tm = 256

    def body(p_ref, w_ref, x_ref, dy_ref, g_ref, m_ref, b_ref, gx_ref, dsh_ref, dsc_ref, dg_ref):
        @pl.when(pl.program_id(0) == 0)
        def _():
            dsh_ref[...] = jnp.zeros_like(dsh_ref)
            dsc_ref[...] = jnp.zeros_like(dsc_ref)
            dg_ref[...] = jnp.zeros_like(dg_ref)

        dh = None
        for j in range(4):
            for half in range(2):
                term = _tt(p_ref[2 * j + half], w_ref[j, :, half * DA:(half + 1) * DA])
                dh = term if dh is None else dh + term
        scale = m_ref[:, D:2 * D] + b_ref[:, D:2 * D]
        dx, dshift, dscale, dg = _norm_mod_bwd(x_ref[...], dh, g_ref[...], scale)
        gx_ref[...] = dy_ref[...] + dx
        dsh_ref[...] += dshift
        dsc_ref[...] += dscale
        dg_ref[...] += dg

    row = lambda i: (i, 0)
    fixed = lambda i: (0, 0)
    vec = SDS((1, D), f32)
    return _hbm_call(
        body, name="dh_grad_x", out_shape=(SDS((S, D), f32), vec, vec, vec), grid=(S // tm,),
        in_specs=[pl.BlockSpec((8, tm, DA), lambda i: (0, i, 0)), pl.BlockSpec((4, D, D), lambda i: (0, 0, 0)),
                  pl.BlockSpec((tm, D), row), pl.BlockSpec((tm, D), row), pl.BlockSpec((1, D), fixed),
                  pl.BlockSpec((1, 3 * D), fixed), pl.BlockSpec((1, 3 * D), fixed)],
        out_specs=(pl.BlockSpec((tm, D), row), pl.BlockSpec((1, D), fixed), pl.BlockSpec((1, D), fixed),
                   pl.BlockSpec((1, D), fixed)),
        compiler_params=_cp(56, dimension_semantics=("arbitrary",)),
    )(dp8, w4, xx, dy, norm_g, mrow, b_ada)


def _dhc_sums(dkc_raw, dvc_m, w4, ctx2, norm_g, mrow_c, b_ada):
    def body(dk_ref, dv_ref, w0_ref, w1_ref, x_ref, g_ref, m_ref, b_ref, dsh_ref, dsc_ref, dg_ref):
        dh = _tt(dk_ref[...], w0_ref[:, DA:2 * DA]) + _tt(dv_ref[...], w1_ref[:, 0:DA])
        scale = m_ref[:, D:2 * D] + b_ref[:, D:2 * D]
        _, dshift, dscale, dg = _norm_mod_bwd(x_ref[...], dh, g_ref[...], scale)
        dsh_ref[...] = dshift
        dsc_ref[...] = dscale
        dg_ref[...] = dg

    fixed = lambda i: (0, 0)
    vec = SDS((1, D), f32)
    vspec = pl.BlockSpec((1, D), fixed)
    return _hbm_call(
        body, name="dhc_sums", out_shape=(vec, vec, vec), grid=(1,),
        in_specs=[pl.BlockSpec((L, DA), fixed), pl.BlockSpec((L, DA), fixed),
                  pl.BlockSpec((None, D, D), lambda i: (0, 0, 0)), pl.BlockSpec((None, D, D), lambda i: (1, 0, 0)),
                  pl.BlockSpec((L, D), fixed), vspec, pl.BlockSpec((1, 3 * D), fixed), pl.BlockSpec((1, 3 * D), fixed)],
        out_specs=(vspec, vspec, vspec), compiler_params=_cp(32),
    )(dkc_raw, dvc_m, w4, w4, ctx2, norm_g, mrow_c, b_ada)


def _rope_tables():
    nf = DH // 4
    inv = np.float32(ROPE_THETA) ** (-np.arange(nf, dtype=np.float32) / np.float32(nf))
    ang_c = np.arange(GW, dtype=np.float32)[:, None] * inv
    ang_r = np.arange(ROWS, dtype=np.float32)[:, None] * inv
    zc, zr = np.zeros((GW, 2 * nf), np.float32), np.zeros((ROWS, 2 * nf), np.float32)
    ct_cos = np.tile(np.concatenate([zc, np.cos(ang_c), np.cos(ang_c)], axis=1), (1, H))
    ct_sin = np.tile(np.concatenate([zc, -np.sin(ang_c), np.sin(ang_c)], axis=1), (1, H))
    rt_cos = np.tile(np.concatenate([np.cos(ang_r), np.cos(ang_r), zr], axis=1), (1, H))
    rt_sin = np.tile(np.concatenate([-np.sin(ang_r), np.sin(ang_r), zr], axis=1), (1, H))
    rep8 = lambda t: np.ascontiguousarray(np.broadcast_to(t[:, None, :], (ROWS, 8, DA))).reshape(ROWS * 8, DA)
    return tuple(jnp.asarray(t, f32) for t in (ct_cos, rep8(rt_cos), ct_sin, rep8(rt_sin)))


def _local_step(xx, ctx2, tgt, mrow, mrow_c, b_ada, norm_g, weights, q_norm_g, k_norm_g, btb, conv_w_full, conv_b,
                hooks=None):
    hooks = hooks or {}
    gq = jnp.tile(q_norm_g, (1, H))
    gk = jnp.tile(k_norm_g, (1, H))
    rope = _rope_tables()

    h = _prenorm(xx, norm_g, mrow, b_ada, 256, "prenorm_x")
    hc = _prenorm(ctx2, norm_g, mrow_c, b_ada, L, "prenorm_ctx")
    jv = weights["jvec"]
    p = _in_proj_own(h, weights["own"], jv)
    w4 = weights["near"](p)
    p = _in_proj_block(h, w4, p, jv ^ 1, "in_proj_y")
    p = _in_proj_block(h, w4, p, jv ^ 2, "in_proj_x")
    w4, started = weights["far"](w4, p)
    p = _in_proj_block(h, w4, p, jv ^ 3, "in_proj_xy", after=started)
    pc = _ctx_proj(hc, w4)
    qr, qp, kr, vh = _qk_prep(p, gq, gk, rope)
    kc, vc = _ctx_prep(pc, gk)
    o = _attn_fwd(qr, qp, kr, vh, kc, vc, btb)
    conv_g = _conv_fwd(p, conv_w_full, conv_b)
    w_out_full = weights["out"](o)
    dy, dconv, dp8, do, g_w_out, dgate, loss_sum = _out_proj_loss(o, p, conv_g, w_out_full, xx, tgt, mrow, b_ada)
    started = hooks["g_w_out"](g_w_out) if "g_w_out" in hooks else None
    dp8, g_conv_w, g_conv_b = _conv_bwd(dconv, p, conv_w_full, conv_b, dp8, after=started)
    started = hooks["after_conv"](dp8) if "after_conv" in hooks else None
    dqr, dqp, dkr, dvh, dkc, dvc, dbtb = _attn_bwd(qr, qp, kr, vh, kc, vc, btb, do, after=started)
    dp8, g_gq, g_gk = _qk_bwd(dqr, dqp, dkr, dvh, p, gq, gk, rope, dp8)
    dkc_raw, dvc_m, g_gk_c = _ctx_bwd(dkc, dvc, pc, gk)
    g_w_in = _grad_w_in(h, dp8, hc, dkc_raw, dvc_m)
    behind = lambda token: b_ada if token is None else b_ada + token[0:1, 0:1]
    started = hooks["g_w_in"](g_w_in) if "g_w_in" in hooks else None
    dshift_c, dscale_c, dng_c = _dhc_sums(dkc_raw, dvc_m, w4, ctx2, norm_g, mrow_c, behind(started))
    g_rpb = _bias_bwd(dbtb, after=dshift_c)
    started = hooks["g_rpb"](g_rpb) if "g_rpb" in hooks else None
    grad_x, dshift, dscale, dng = _dh_grad_x(dp8, w4, xx, dy, norm_g, mrow, behind(started))
    return dict(loss_sum=loss_sum, grad_x=grad_x, g_w_in=g_w_in, g_w_out=g_w_out, g_conv_w=g_conv_w,
                g_conv_b=g_conv_b, g_rpb=g_rpb, g_gq=g_gq, g_gk=g_gk, g_gk_c=g_gk_c, dshift=dshift, dscale=dscale,
                dgate=dgate, dng=dng, dshift_c=dshift_c, dscale_c=dscale_c, dng_c=dng_c)


def _pair_sum_w_in(g, r, cvec):
    tr = 128

    def body(c_ref, g_ref, r_ref, t32_ref, tb_ref):
        t = g_ref[...] + r_ref[...]
        t32_ref[...] = t
        tb_ref[...] = t.astype(bf16)

    half = D // 2
    g_spec = pl.BlockSpec((4, tr, D), lambda i, c: (0, c[0] * (half // tr) + i, 0))
    o_spec = pl.BlockSpec((4, tr, D), lambda i, c: (0, i, 0))
    grid_spec = pltpu.PrefetchScalarGridSpec(num_scalar_prefetch=1, grid=(half // tr,), in_specs=[g_spec, o_spec],
                                             out_specs=(o_spec, o_spec))
    return _hbm_call(body, name="pair_sum_w_in", out_shape=(SDS((4, half, D), f32), SDS((4, half, D), bf16)),
                     grid_spec=grid_spec, compiler_params=_cp(40))(cvec, g, r)


def _pair_sum_w_out(g, r, cvec):
    hr = D // 8

    def body(c_ref, g0, g1, g2, g3, r_ref, t32_ref, tb_ref):
        for q, g_ref in enumerate((g0, g1, g2, g3)):
            t = g_ref[...] + r_ref[q]
            t32_ref[q] = t
            tb_ref[q] = t.astype(bf16)

    gspecs = [pl.BlockSpec((hr, D), lambda i, c, q=q: (2 * q + c[0], 0)) for q in range(4)]
    full = pl.BlockSpec((4, hr, D), lambda i, c: (0, 0, 0))
    grid_spec = pltpu.PrefetchScalarGridSpec(num_scalar_prefetch=1, grid=(1,), in_specs=gspecs + [full],
                                             out_specs=(full, full))
    return _hbm_call(body, name="pair_sum_w_out", out_shape=(SDS((4, hr, D), f32), SDS((4, hr, D), bf16)),
                          grid_spec=grid_spec)(cvec, g, g, g, g, r)


def _chip_sum(t32, r2, jvec, name):
    rows = t32.shape[1]
    tr = min(rows, 128)

    def body(j_ref, t_ref, r_ref, u_ref):
        u_ref[...] = ((t_ref[...] + r_ref[0].astype(f32)) + r_ref[1].astype(f32)) + r_ref[2].astype(f32)

    grid_spec = pltpu.PrefetchScalarGridSpec(
        num_scalar_prefetch=1, grid=(rows // tr,),
        in_specs=[pl.BlockSpec((None, tr, D), lambda i, j: (j[0], i, 0)), pl.BlockSpec((3, tr, D), lambda i, j: (0, i, 0))],
        out_specs=pl.BlockSpec((tr, D), lambda i, j: (i, 0)))
    return _hbm_call(body, name=name, out_shape=SDS((rows, D), f32), grid_spec=grid_spec)(jvec, t32, r2)


_PK = {}
_off = 0
for _name, _rows in (("dm", 24), ("dmc", 24), ("dng", 8), ("dng_c", 8), ("gq", 8), ("gk", 8), ("gk_c", 8),
                     ("rpb", H * N_DR), ("conv_b", 8), ("conv_w", 16), ("loss", 8)):
    _PK[_name] = (_off, _off + _rows)
    _off += _rows
PK_ROWS = _off
RS_B_ADA, RS_NORM_G, RS_GQ, RS_GK, RS_RPB, RS_CONV_B, RS_CONV_W, RS_DMC, RS_LOSS, RS_ROWS = (
    0, 24, 32, 40, 48, 168, 176, 192, 216, 224)


def _small_reduce(gathered):
    def body(g_ref, o_ref, dm_ref):
        a0 = _PK["dm"][0]
        dm_ref[...] = jnp.zeros_like(dm_ref)
        for b in range(8):
            for i in range(24):
                dm_ref[b:b + 1, 128 * i:128 * (i + 1)] = g_ref[b, a0 + i:a0 + i + 1, :]
        tot = g_ref[0]
        for b in range(1, 8):
            tot = tot + g_ref[b]

        def rows(name):
            a, z = _PK[name]
            return tot[a:z]

        o_ref[RS_B_ADA:RS_B_ADA + 24] = rows("dm") + rows("dmc")
        o_ref[RS_NORM_G:RS_NORM_G + 8] = rows("dng") + rows("dng_c")
        gq = jnp.broadcast_to(jnp.sum(rows("gq"), axis=0, keepdims=True), (8, 128))
        gk = jnp.broadcast_to(jnp.sum(rows("gk") + rows("gk_c"), axis=0, keepdims=True), (8, 128))
        o_ref[RS_GQ:RS_GQ + 8] = gq + pltpu.roll(gq, DH, 1)
        o_ref[RS_GK:RS_GK + 8] = gk + pltpu.roll(gk, DH, 1)
        o_ref[RS_RPB:RS_RPB + H * N_DR] = rows("rpb")
        o_ref[RS_CONV_B:RS_CONV_B + 8] = rows("conv_b")
        o_ref[RS_CONV_W:RS_CONV_W + 16] = rows("conv_w")
        dmc = rows("dmc")
        o_ref[RS_DMC:RS_DMC + 24] = dmc
        o_ref[RS_LOSS:RS_LOSS + 8] = rows("loss")
        for i in range(24):
            dm_ref[8:9, 128 * i:128 * (i + 1)] = dmc[i:i + 1]

    return _hbm_call(body, name="small_reduce", out_shape=(SDS((RS_ROWS, 128), f32), SDS((16, 3 * D), f32)),
                     in_specs=[VMEM_SPEC], out_specs=(VMEM_SPEC, VMEM_SPEC))(gathered)


def _w_ada_grad(sc16, dm16, w_ada_shard, jvec):
    ncol = w_ada_shard.shape[1]

    def body(j_ref, sc_ref, dm_ref, w_ref, g_ref, part_ref):
        dm = dm_ref[...]
        g_ref[...] = lax.dot_general(sc_ref[...], dm, (((0,), (0,)), ((), ())), precision=HIGHEST,
                                     preferred_element_type=f32)
        part_ref[...] = lax.dot_general(dm[8:16], w_ref[...], (((1,), (1,)), ((), ())), precision=HIGHEST,
                                        preferred_element_type=f32)

    fixed = lambda i, j: (0, 0)
    grid_spec = pltpu.PrefetchScalarGridSpec(
        num_scalar_prefetch=1, grid=(1,),
        in_specs=[pl.BlockSpec((16, D), fixed), pl.BlockSpec((16, ncol), lambda i, j: (0, j[0])),
                  pl.BlockSpec((D, ncol), fixed)],
        out_specs=(pl.BlockSpec((D, ncol), fixed), pl.BlockSpec((8, D), fixed)))
    return _pallas_call(body, name="w_ada_grad", out_shape=(SDS((D, ncol), f32), SDS((8, D), f32)),
                        grid_spec=grid_spec, compiler_params=_cp(40))(jvec, sc16, dm16, w_ada_shard)


def _c_ctx_grad(parts4, c_ctx):
    def body(p_ref, c_ref, o_ref):
        tot = ((p_ref[0] + p_ref[1]) + p_ref[2]) + p_ref[3]
        o_ref[...] = tot[0:1] * _dsilu(c_ref[...].reshape(1, D))

    return _pallas_call(body, name="c_ctx_grad", out_shape=SDS((1, D), f32), in_specs=[VMEM_SPEC, VMEM_SPEC],
                        out_specs=VMEM_SPEC)(parts4, c_ctx)


def _adamw(w, g, m, v, name):
    rows, cols = w.shape
    tr = 256 if rows % 256 == 0 else rows

    def body(w_ref, g_ref, m_ref, v_ref, d_ref, m2_ref, v2_ref):
        gv = g_ref[...]
        m2 = ADAM_B1 * m_ref[...] + (1.0 - ADAM_B1) * gv
        v2 = ADAM_B2 * v_ref[...] + (1.0 - ADAM_B2) * jnp.square(gv)
        m_hat = m2 / (1.0 - ADAM_B1 ** ADAM_STEP)
        v_hat = v2 / (1.0 - ADAM_B2 ** ADAM_STEP)
        d_ref[...] = -ADAM_LR * (m_hat / (jnp.sqrt(v_hat) + ADAM_EPS) + ADAM_WD * w_ref[...])
        m2_ref[...] = m2
        v2_ref[...] = v2

    spec = pl.BlockSpec((tr, cols), lambda i: (i, 0))
    shp = SDS((rows, cols), f32)
    return _hbm_call(body, name=name, out_shape=(shp, shp, shp), grid=(rows // tr,), in_specs=[spec] * 4,
                          out_specs=(spec, spec, spec))(w, g, m, v)


def _adamw_halves(w, g_mine, g_other, m, v, cvec, name):
    rows, cols = w.shape
    half = rows // 2
    tr = min(256, half)
    per_half = half // tr

    def body(c_ref, w_ref, ga_ref, gb_ref, m_ref, v_ref, g_ref, d_ref, m2_ref, v2_ref):
        in_my_half = (pl.program_id(0) // per_half) == c_ref[0]
        gv = jnp.where(in_my_half, ga_ref[...], gb_ref[...])
        g_ref[...] = gv
        m2 = ADAM_B1 * m_ref[...] + (1.0 - ADAM_B1) * gv
        v2 = ADAM_B2 * v_ref[...] + (1.0 - ADAM_B2) * jnp.square(gv)
        m_hat = m2 / (1.0 - ADAM_B1 ** ADAM_STEP)
        v_hat = v2 / (1.0 - ADAM_B2 ** ADAM_STEP)
        d_ref[...] = -ADAM_LR * (m_hat / (jnp.sqrt(v_hat) + ADAM_EPS) + ADAM_WD * w_ref[...])
        m2_ref[...] = m2
        v2_ref[...] = v2

    full = pl.BlockSpec((tr, cols), lambda i, c: (i, 0))
    part = pl.BlockSpec((tr, cols), lambda i, c: (i % per_half, 0))
    shp = SDS((rows, cols), f32)
    grid_spec = pltpu.PrefetchScalarGridSpec(num_scalar_prefetch=1, grid=(rows // tr,),
                                             in_specs=[full, part, part, full, full], out_specs=(full,) * 4)
    return _hbm_call(body, name=name, out_shape=(shp,) * 4, grid_spec=grid_spec)(cvec, w, g_mine, g_other, m, v)


def _adam_math(w, g, m, v):
    m2 = ADAM_B1 * m + (1.0 - ADAM_B1) * g
    v2 = ADAM_B2 * v + (1.0 - ADAM_B2) * jnp.square(g)
    m_hat = m2 / (1.0 - ADAM_B1 ** ADAM_STEP)
    v_hat = v2 / (1.0 - ADAM_B2 ** ADAM_STEP)
    return -ADAM_LR * (m_hat / (jnp.sqrt(v_hat) + ADAM_EPS) + ADAM_WD * w), m2, v2


def _adamw_small(red, g_c_ctx, jvec, ws, ms, vs):
    n = len(ws)

    def body(*refs):
        red_ref, gc_ref, j_ref = refs[:3]
        w_refs, m_refs, v_refs = refs[3:3 + n], refs[3 + n:3 + 2 * n], refs[3 + 2 * n:3 + 3 * n]
        outs = refs[3 + 3 * n:]
        g_out, d_out, m_out, v_out = outs[:n], outs[n:2 * n], outs[2 * n:3 * n], outs[3 * n:]
        chip = j_ref[0]
        lanes = lambda i: (slice(None), slice(128 * i, 128 * (i + 1)))
        row = lambda r0, i: (lambda: red_ref[r0 + i:r0 + i + 1, :])
        whole = (slice(None), slice(None))
        chunks = [
            [((slice(None),), lambda: gc_ref[...].reshape(D))],
            [(lanes(i), row(RS_B_ADA, i)) for i in range(3 * D // 128)],
            [(lanes(i), row(RS_NORM_G, i)) for i in range(D // 128)],
            [(whole, lambda: red_ref[RS_GQ:RS_GQ + 1, 0:DH])],
            [(whole, lambda: red_ref[RS_GK:RS_GK + 1, 0:DH])],
            [((dr,), (lambda dr=dr: red_ref[pl.ds(RS_RPB + dr, H, stride=N_DR), 0:N_DC])) for dr in range(N_DR)],
            [((r,), (lambda r=r: red_ref[pl.ds(RS_CONV_W + 4 * r + chip, 1), :])) for r in range(3)],
            [(lanes(i), row(RS_CONV_B, i)) for i in range(DC // 128)],
        ]
        for a in range(n):
            for idx, grad in chunks[a]:
                g = grad()
                d, m2, v2 = _adam_math(w_refs[a][idx], g, m_refs[a][idx], v_refs[a][idx])
                g_out[a][idx] = g
                d_out[a][idx] = d
                m_out[a][idx] = m2
                v_out[a][idx] = v2

    shapes = [SDS(w.shape, f32) for w in ws]
    res = _pallas_call(body, name="adamw_small", out_shape=shapes * 4,
                       in_specs=[VMEM_SPEC, VMEM_SPEC, SMEM_SPEC] + [VMEM_SPEC] * (3 * n),
                       out_specs=[VMEM_SPEC] * (4 * n))(red, g_c_ctx, jvec, *ws, *ms, *vs)
    return [list(res[k * n:(k + 1) * n]) for k in range(4)]


def _rows128(a):
    return a.reshape(-1, 128)


def kernel(x, c, ctx, c_ctx, w_ada, b_ada, norm_g, w_in, q_norm_g, k_norm_g, rpb, conv_w, conv_b, w_out, loss_target, m_c_ctx, m_w_ada, m_b_ada, m_norm_g, m_w_in, m_q_norm_g, m_k_norm_g, m_rpb, m_conv_w, m_conv_b, m_w_out, v_c_ctx, v_w_ada, v_b_ada, v_norm_g, v_w_in, v_q_norm_g, v_k_norm_g, v_rpb, v_conv_w, v_conv_b, v_w_out):
    xi, yi, ci = lax.axis_index("x"), lax.axis_index("y"), lax.axis_index("c")
    dev = 4 * xi + 2 * yi + ci
    chip = 2 * xi + yi
    cvec = jnp.reshape(ci, (1,)).astype(i32)
    jvec = jnp.reshape(chip, (1,)).astype(i32)
    w_ada_s = w_ada[0]
    ncol = w_ada_s.shape[1]

    gc = _split_start([_to_slot(c.reshape(8, 128), 8, dev)], [], 7, _gather8_copies, "gather_c_start")
    btb = _bias_tiles(rpb[0], gc[3][0:1, 0:1])
    (c8,), _ = _split_wait(gc[0], gc[1], [gc[2]], [], btb, _gather8_copies, "gather_c_wait")
    cc = jnp.concatenate([c8.reshape(8, D), c_ctx.reshape(1, D), jnp.zeros((7, D), f32)], axis=0)
    m_shard, sc16 = _adaln_shard(cc, w_ada_s)

    conv_w_pad = jnp.pad(conv_w[0], ((0, 5), (0, 0)))
    gm = _split_start([_to_slot(m_shard, 4, chip), _to_slot(conv_w_pad, 4, chip)], [], 6, _gather4_copies,
                      "gather_mod_start")
    w4c = _cast_to_slot(w_in[0], jvec, "cast_w_in")
    (m4, cw4), _ = _split_wait(gm[0], gm[1], [gm[2], gm[3]], [], w4c, _gather4_copies, "gather_mod_wait")

    all_k = [(0, 0, 0), (0, 0, 1), (0, 0, 2)]
    wo4c = _cast_to_slot(w_out[0], jvec, "cast_w_out")
    sem_a, rem_a, w4s, token = _halves_start([w4c], m4, all_k[0:2], "weights_near_start")
    m_full = jnp.transpose(m4, (1, 0, 2)).reshape(16, 4 * ncol) + token[0:1, 0:1]
    mrow = lax.dynamic_slice(m_full, (dev, 0), (1, 3 * D))
    mrow_c = m_full[8:9]
    conv_w_full = jnp.transpose(cw4[:, 0:3, :], (1, 0, 2)).reshape(3, DC)
    waves = {}

    def near(after):
        (w4w,) = _halves_wait(sem_a, rem_a, [w4s], after, all_k[0:2], "weights_near_wait")
        w4f = _halves_forward([w4w], [0, 1], "weights_near_forward")[0]
        sem_b, rem_b, w4b, _ = _halves_start([w4f], after, all_k[2:3], "weights_far_start")
        waves["far"] = (sem_b, rem_b)
        return w4b

    def far(w4, after):
        (w4w,) = _halves_wait(*waves["far"], [w4], after, all_k[2:3], "weights_far_wait")
        w4f = _halves_forward([w4w], [2], "weights_far_forward")[0]
        sem_c, rem_c, wo4s, started = _halves_start([wo4c], w4f, all_k, "weights_out_start")
        waves["out"] = (sem_c, rem_c, wo4s)
        return w4f, started

    def w_out_gathered(after):
        sem_c, rem_c, wo4s = waves["out"]
        (wow,) = _halves_wait(sem_c, rem_c, [wo4s], after, all_k, "weights_out_wait")
        return _halves_forward([wow], [0, 1, 2], "weights_out_forward")[0].reshape(D, D)

    weights = dict(own=w_in[0], jvec=jvec, near=near, far=far, out=w_out_gathered)

    exchange = _exchange_copies
    pending = {}

    def on_g_w_out(g_w_out):
        out = _split_start([g_w_out], [SDS((4, D // 8, D), f32)], 4, exchange, "grad_out_pair_start")
        pending["ex_out"] = out
        return out[4]

    def after_conv(dp8):
        ssem_o, rsem_o, g_o, land_o, _ = pending["ex_out"]
        (g_o,), (ex_o,) = _split_wait(ssem_o, rsem_o, [g_o], [land_o], dp8, exchange, "grad_out_pair_wait")
        to32, tob = _pair_sum_w_out(g_o, ex_o, cvec)
        out = _split_start([tob], [SDS((3, D // 8, D), bf16)], 3, _scatter_copies, "grad_out_chip_start")
        pending["sc_out"] = (out, to32)
        return out[4]

    def on_g_w_in(g_w_in):
        out = _split_start([g_w_in], [SDS((4, D // 2, D), f32)], 4, exchange, "grad_pair_start")
        pending["ex"] = (out[0], out[1], [out[2]], [out[3]])
        return out[4]

    def on_g_rpb(g_rpb):
        ex_ssem, ex_rsem, ex_srcs, ex_lands = pending["ex"]
        ex_g, ex = _split_wait(ex_ssem, ex_rsem, ex_srcs, ex_lands, g_rpb, exchange, "grad_pair_wait")
        t32, tb = _pair_sum_w_in(ex_g[0], ex[0], cvec)
        out = _split_start([tb], [SDS((3, D // 2, D), bf16)], 3, _scatter_copies, "grad_chip_start")
        pending["sc_in"] = (out, t32)
        return out[4]

    r = _local_step(x[0], ctx[0], loss_target[0], mrow, mrow_c, b_ada, norm_g, weights, q_norm_g, k_norm_g,
                    btb, conv_w_full, conv_b,
                    dict(g_w_out=on_g_w_out, after_conv=after_conv, g_w_in=on_g_w_in, g_rpb=on_g_rpb))
    dm = jnp.concatenate([r["dshift"], r["dscale"], r["dgate"]], axis=1)
    dmc = jnp.concatenate([r["dshift_c"], r["dscale_c"], jnp.zeros((1, D), f32)], axis=1)
    pack_parts = [_rows128(dm), _rows128(dmc), _rows128(r["dng"]), _rows128(r["dng_c"]), _rows128(r["g_gq"]),
                  _rows128(r["g_gk"]), _rows128(r["g_gk_c"]), r["g_rpb"].reshape(H * N_DR, 128),
                  _rows128(r["g_conv_b"]), _rows128(r["g_conv_w"][0:3]), jnp.pad(r["loss_sum"], ((0, 0), (0, 127)))]
    pack = jnp.concatenate([jnp.pad(p, ((0, -p.shape[0] % 8), (0, 0))) for p in pack_parts], axis=0)
    assert pack.shape[0] == PK_ROWS
    gs = _split_start([_to_slot(pack, 8, dev)], [], 7, _gather8_copies, "gather_small_start")
    sc_o, to32 = pending["sc_out"]
    _, (ro2,) = _split_wait(sc_o[0], sc_o[1], [sc_o[2]], [sc_o[3]], gs[3], _scatter_copies, "grad_out_chip_wait")
    u_out = _chip_sum(to32, ro2, jvec, "chip_sum_w_out")
    (o_out,) = _sibling_send([u_out], "grad_out_pair_send")
    g_w_out_s, d_w_out, nm_w_out, nv_w_out = _adamw_halves(w_out[0], u_out, o_out, m_w_out[0], v_w_out[0], cvec,
                                                           "adamw_w_out")
    (gathered,), _ = _split_wait(gs[0], gs[1], [gs[2]], [], nm_w_out, _gather8_copies, "gather_small_wait")
    red, dm16 = _small_reduce(gathered)
    loss = red[RS_LOSS, 0] * (0.5 / D)

    g_w_ada_s, cpart = _w_ada_grad(sc16, dm16, w_ada_s, jvec)
    d_w_ada, nm_w_ada, nv_w_ada = _adamw(w_ada_s, g_w_ada_s, m_w_ada[0], v_w_ada[0], "adamw_w_ada")

    (cparts4,) = _chip_gather([cpart], "gather_c_ctx_parts", after=nm_w_ada)
    g_c_ctx = _c_ctx_grad(cparts4, c_ctx)

    sc_in, t32 = pending["sc_in"]
    _, (r2,) = _split_wait(sc_in[0], sc_in[1], [sc_in[2]], [sc_in[3]], g_c_ctx, _scatter_copies, "grad_chip_wait")
    u_in = _chip_sum(t32, r2, jvec, "chip_sum_w_in")

    t_rpb = lambda a: jnp.transpose(a, (0, 2, 1, 3)).reshape(N_DR, H, N_DC)
    t_cw = lambda a: jnp.transpose(a, (1, 0, 2))
    small = _adamw_small(
        red, g_c_ctx, jvec,
        [c_ctx, b_ada, norm_g, q_norm_g, k_norm_g, t_rpb(rpb), t_cw(conv_w), conv_b],
        [m_c_ctx, m_b_ada, m_norm_g, m_q_norm_g, m_k_norm_g, t_rpb(m_rpb), t_cw(m_conv_w), m_conv_b],
        [v_c_ctx, v_b_ada, v_norm_g, v_q_norm_g, v_k_norm_g, t_rpb(v_rpb), t_cw(v_conv_w), v_conv_b])
    for kind in small:
        kind[5] = jnp.transpose(kind[5].reshape(1, N_DR, H, N_DC), (0, 2, 1, 3))
        kind[6] = jnp.transpose(kind[6], (1, 0, 2))

    (o_in,) = _sibling_send([u_in], "grad_pair_send")
    g_w_in_s, d_w_in, nm_w_in, nv_w_in = _adamw_halves(w_in[0], u_in, o_in, m_w_in[0], v_w_in[0], cvec, "adamw_w_in")

    def ordered(kind, big_w_ada, big_w_in, big_w_out):
        s_c_ctx, s_b_ada, s_norm_g, s_q, s_k, s_rpb, s_conv_w, s_conv_b = small[kind]
        return [s_c_ctx, big_w_ada[None], s_b_ada, s_norm_g, big_w_in[None], s_q, s_k, s_rpb, s_conv_w,
                s_conv_b, big_w_out[None]]

    grads = ordered(0, g_w_ada_s, g_w_in_s, g_w_out_s)
    deltas = ordered(1, d_w_ada, d_w_in, d_w_out)
    new_m = ordered(2, nm_w_ada, nm_w_in, nm_w_out)
    new_v = ordered(3, nv_w_ada, nv_w_in, nv_w_out)
    return (loss, r["grad_x"][None], *grads, *deltas, *new_m, *new_v)
```

```python
import functools

import jax
import jax.numpy as jnp
import numpy as np
from jax import lax
from jax.experimental import pallas as pl
from jax.experimental.pallas import tpu as pltpu

f32, bf16, i32 = jnp.float32, jnp.bfloat16, jnp.int32
MESH = pl.DeviceIdType.MESH
HIGHEST = lax.Precision.HIGHEST

D = 1024
S = 2048
L = 256
GW = 64
ROWS = S // GW
H = 8
DH = 64
DA = H * DH
DC = 512
WIN_H, WIN_W = 8, 16
N_DR, N_DC = 2 * WIN_H - 1, 2 * WIN_W - 1
RMS_EPS = 1e-6
ROPE_THETA = 10000.0
QK_SCALE = DH ** -0.5
NEG = -1e30

QB = 128
NQB = S // QB
KR = 9
KB = KR * GW
TILE_GEOM = ((0, 0), (2, 0), (4, 0), (28, 23), (30, 23))
NT = len(TILE_GEOM)

ADAM_LR, ADAM_B1, ADAM_B2, ADAM_EPS, ADAM_WD, ADAM_STEP = 0.001, 0.9, 0.999, 1e-08, 0.01, 10

VMEM_SPEC = pl.BlockSpec(memory_space=pltpu.VMEM)
ANY_SPEC = pl.BlockSpec(memory_space=pl.ANY)
SMEM_SPEC = pl.BlockSpec(memory_space=pltpu.SMEM)
SDS = jax.ShapeDtypeStruct


_pallas_call = pl.pallas_call


def _hbm_call(body, *, out_shape, in_specs=None, out_specs=None, grid_spec=None, **kw):
    n_pre = 0
    if grid_spec is not None:
        ispecs, ospecs, n_pre = grid_spec.in_specs, grid_spec.out_specs, grid_spec.num_scalar_prefetch
        kw["grid_spec"] = grid_spec
    else:
        ispecs, ospecs = in_specs, out_specs
        kw.update(in_specs=in_specs, out_specs=out_specs)

    def blocked(spec):
        return isinstance(spec, pl.BlockSpec) and spec.block_shape is not None

    single = not isinstance(out_shape, (tuple, list))
    shapes = [out_shape] if single else list(out_shape)
    ospec_list = list(ospecs) if isinstance(ospecs, (tuple, list)) else [ospecs]
    shapes = [pltpu.HBM(s.shape, s.dtype) if blocked(sp) else s for s, sp in zip(shapes, ospec_list)]
    call = _pallas_call(body, out_shape=shapes[0] if single else tuple(shapes), **kw)

    def run(*args):
        arrays = [pltpu.with_memory_space_constraint(a, pltpu.HBM) if blocked(sp) else a
                  for a, sp in zip(args[n_pre:], ispecs)]
        return call(*args[:n_pre], *arrays)

    return run


def _cp(vmem_mb=None, **kw):
    if vmem_mb is not None:
        kw["vmem_limit_bytes"] = vmem_mb << 20
    return pltpu.CompilerParams(**kw)


def _silu(z):
    return z * jax.nn.sigmoid(z)


def _dsilu(z):
    sg = jax.nn.sigmoid(z)
    return sg * (1.0 + z * (1.0 - sg))


def _row_start(i):
    return min(max(i - WIN_H // 2, 0), ROWS - WIN_H)


def _my_pos():
    return lax.axis_index("x"), lax.axis_index("y"), lax.axis_index("c")


def _flip(v, bit):
    return 1 - v if bit else v


def _chip_gather(smalls, name, after=None):
    ns = len(smalls)

    def body(*refs):
        s_in, s_out = refs[:ns], refs[ns + 1:2 * ns + 1]
        ssem, rsem, lsem = refs[2 * ns + 1:]
        x, y, c = _my_pos()
        j = 2 * x + y
        chips = _peer_chips(x, y, c)
        local = [pltpu.make_async_copy(s_in[a], s_out[a].at[j], lsem.at[a]) for a in range(ns)]
        for cp in local:
            cp.start()
        sends = []
        for a in range(ns):
            for k in range(3):
                cp = pltpu.make_async_remote_copy(src_ref=s_in[a], dst_ref=s_out[a].at[j], send_sem=ssem.at[3 * a + k],
                                                  recv_sem=rsem.at[3 * a + k], device_id=chips[k][0], device_id_type=MESH)
                cp.start()
                sends.append(cp)
        for a in range(ns):
            for k in range(3):
                pltpu.make_async_remote_copy(src_ref=s_in[a], dst_ref=s_out[a].at[chips[k][1]], send_sem=ssem.at[3 * a + k],
                                             recv_sem=rsem.at[3 * a + k], device_id=chips[k][0],
                                             device_id_type=MESH).wait_recv()
        for cp in sends:
            cp.wait_send()
        for cp in local:
            cp.wait()

    return _hbm_call(
        body, name=name, out_shape=[SDS((4,) + a.shape, a.dtype) for a in smalls],
        in_specs=[VMEM_SPEC] * ns + [ANY_SPEC], out_specs=[VMEM_SPEC] * ns,
        scratch_shapes=[pltpu.SemaphoreType.DMA((3 * ns,)), pltpu.SemaphoreType.DMA((3 * ns,)),
                        pltpu.SemaphoreType.DMA((ns,))],
    )(*smalls, smalls[0] if after is None else after)


HBM_SPEC = pl.BlockSpec(memory_space=pltpu.HBM)
SEM_SPEC = pl.BlockSpec(memory_space=pltpu.SEMAPHORE)
DATAFLOW = pltpu.SideEffectType.DATAFLOW_SIDE_EFFECTING


def _peer_chips(x, y, c):
    out = []
    for k in range(1, 4):
        px, py = _flip(x, (k >> 1) & 1), _flip(y, k & 1)
        out.append(((px, py, c), 2 * px + py))
    return out


def _half_copies(srcs, dsts, ssem, rsem, which):
    x, y, c = _my_pos()
    j = 2 * x + y
    peers = _peer_chips(x, y, c)
    pairs = []
    for pos, group, k in which:
        half = srcs[pos].shape[1] // 2
        mine = pl.ds(pl.multiple_of(c * half, 8), half)
        dev, pj = peers[k]
        sem = 3 * group + k
        send = pltpu.make_async_remote_copy(src_ref=srcs[pos].at[j, mine], dst_ref=dsts[pos].at[j, mine],
                                            send_sem=ssem.at[sem], recv_sem=rsem.at[sem], device_id=dev,
                                            device_id_type=MESH)
        arrive = pltpu.make_async_remote_copy(src_ref=srcs[pos].at[j, mine], dst_ref=dsts[pos].at[pj, mine],
                                              send_sem=ssem.at[sem], recv_sem=rsem.at[sem], device_id=dev,
                                              device_id_type=MESH)
        pairs.append((send, arrive))
    return pairs


def _halves_start(bigs, after, order, name):
    nb = len(bigs)

    def body(*refs):
        b_in = refs[:nb]
        ssem, rsem = refs[nb + 1], refs[nb + 2]
        b_out = refs[nb + 3:2 * nb + 3]
        token = refs[2 * nb + 3]
        for send, _ in _half_copies(b_in, b_out, ssem, rsem, order):
            send.start()
        token[...] = jnp.zeros_like(token)

    out_shape = (pltpu.SemaphoreType.DMA((3 * nb,)), pltpu.SemaphoreType.DMA((3 * nb,)),
                 *[pltpu.HBM(b.shape, b.dtype) for b in bigs], SDS((8, 128), f32))
    return _hbm_call(
        body, name=name, out_shape=out_shape, in_specs=[HBM_SPEC] * nb + [ANY_SPEC],
        out_specs=(SEM_SPEC, SEM_SPEC, *[HBM_SPEC] * nb, VMEM_SPEC),
        input_output_aliases={a: 2 + a for a in range(nb)}, compiler_params=_cp(has_side_effects=DATAFLOW),
    )(*[pltpu.with_memory_space_constraint(b, pltpu.HBM) for b in bigs], after)


def _halves_wait(ssem, rsem, bigs, after, which, name):
    nb = len(bigs)

    def body(*refs):
        b_in = refs[:nb]
        ssem_ref, rsem_ref = refs[nb], refs[nb + 1]
        for send, arrive in _half_copies(b_in, b_in, ssem_ref, rsem_ref, which):
            send.wait_send()
            arrive.wait_recv()

    return _hbm_call(
        body, name=name, out_shape=tuple(pltpu.HBM(b.shape, b.dtype) for b in bigs),
        in_specs=[HBM_SPEC] * nb + [SEM_SPEC, SEM_SPEC, ANY_SPEC], out_specs=tuple([HBM_SPEC] * nb),
        input_output_aliases={a: a for a in range(nb)}, compiler_params=_cp(has_side_effects=DATAFLOW),
    )(*bigs, ssem, rsem, after)


def _halves_forward(bigs, relations, name):
    nb, nr = len(bigs), len(relations)

    def body(*refs):
        b_in, b_out = refs[:nb], refs[nb:2 * nb]
        ssem, rsem = refs[2 * nb:]
        x, y, c = _my_pos()
        sib = (x, y, 1 - c)
        peers = _peer_chips(x, y, c)
        sends = []
        for a in range(nb):
            half = b_in[a].shape[1] // 2
            mine = pl.ds(pl.multiple_of(c * half, 8), half)
            for i, k in enumerate(relations):
                pj = peers[k][1]
                cp = pltpu.make_async_remote_copy(src_ref=b_in[a].at[pj, mine], dst_ref=b_out[a].at[pj, mine],
                                                  send_sem=ssem.at[nr * a + i], recv_sem=rsem.at[nr * a + i],
                                                  device_id=sib, device_id_type=MESH)
                cp.start()
                sends.append(cp)
        for a in range(nb):
            half = b_in[a].shape[1] // 2
            other = pl.ds(pl.multiple_of((1 - c) * half, 8), half)
            for i, k in enumerate(relations):
                pj = peers[k][1]
                pltpu.make_async_remote_copy(src_ref=b_in[a].at[pj, other], dst_ref=b_out[a].at[pj, other],
                                             send_sem=ssem.at[nr * a + i], recv_sem=rsem.at[nr * a + i],
                                             device_id=sib, device_id_type=MESH).wait_recv()
        for cp in sends:
            cp.wait_send()

    return _hbm_call(
        body, name=name, out_shape=[SDS(b.shape, b.dtype) for b in bigs], in_specs=[ANY_SPEC] * nb,
        out_specs=[ANY_SPEC] * nb, input_output_aliases={a: a for a in range(nb)},
        scratch_shapes=[pltpu.SemaphoreType.DMA((nr * nb,)), pltpu.SemaphoreType.DMA((nr * nb,))],
    )(*bigs)


def _cast_to_slot(w, jvec, name):
    rows, cols = w.shape
    tr = 256

    def body(j_ref, w_ref, o_ref):
        o_ref[...] = w_ref[...].astype(bf16)

    grid_spec = pltpu.PrefetchScalarGridSpec(
        num_scalar_prefetch=1, grid=(rows // tr,), in_specs=[pl.BlockSpec((tr, cols), lambda i, j: (i, 0))],
        out_specs=pl.BlockSpec((None, tr, cols), lambda i, j: (j[0], i, 0)))
    return _hbm_call(body, name=name, out_shape=SDS((4, rows, cols), bf16), grid_spec=grid_spec)(jvec, w)


def _exchange_copies(srcs, lands, ssem, rsem):
    x, y, c = _my_pos()
    cps = []
    for a in range(len(srcs)):
        stacked = len(srcs[a].shape) == 3
        rb = srcs[a].shape[1] if stacked else srcs[a].shape[0] // 4
        half = rb // 2
        for jb in range(4):
            if stacked:
                src = srcs[a].at[jb, pl.ds(pl.multiple_of((1 - c) * half, 8), half)]
            else:
                src = srcs[a].at[pl.ds(pl.multiple_of(jb * rb + (1 - c) * half, 8), half)]
            cps.append(pltpu.make_async_remote_copy(src_ref=src, dst_ref=lands[a].at[jb], send_sem=ssem.at[4 * a + jb],
                                                    recv_sem=rsem.at[4 * a + jb], device_id=(x, y, 1 - c),
                                                    device_id_type=MESH))
    return cps


def _scatter_copies(srcs, lands, ssem, rsem):
    x, y, c = _my_pos()
    cps = []
    for a in range(len(srcs)):
        for k, (dev, pj) in enumerate(_peer_chips(x, y, c)):
            cps.append(pltpu.make_async_remote_copy(src_ref=srcs[a].at[pj], dst_ref=lands[a].at[k],
                                                    send_sem=ssem.at[3 * a + k], recv_sem=rsem.at[3 * a + k],
                                                    device_id=dev, device_id_type=MESH))
    return cps


class _SlotCopy:
    def __init__(self, src, dst, mine, theirs, ssem, rsem, dev):
        make = lambda slot: pltpu.make_async_remote_copy(src_ref=src.at[mine], dst_ref=dst.at[slot], send_sem=ssem,
                                                         recv_sem=rsem, device_id=dev, device_id_type=MESH)
        send, arrive = make(mine), make(theirs)
        self.start, self.wait_send, self.wait_recv = send.start, send.wait_send, arrive.wait_recv


def _gather8_copies(srcs, dsts, ssem, rsem):
    x, y, c = _my_pos()
    me = 4 * x + 2 * y + c
    cps = []
    for a in range(len(srcs)):
        for k in range(1, 8):
            tgt = (_flip(x, (k >> 2) & 1), _flip(y, (k >> 1) & 1), _flip(c, k & 1))
            cps.append(_SlotCopy(srcs[a], dsts[a], me, 4 * tgt[0] + 2 * tgt[1] + tgt[2], ssem.at[7 * a + k - 1],
                                 rsem.at[7 * a + k - 1], tgt))
    return cps


def _gather4_copies(srcs, dsts, ssem, rsem):
    x, y, c = _my_pos()
    cps = []
    for a in range(len(srcs)):
        for k, (dev, pj) in enumerate(_peer_chips(x, y, c)):
            cps.append(_SlotCopy(srcs[a], dsts[a], 2 * x + y, pj, ssem.at[3 * a + k], rsem.at[3 * a + k], dev))
    return cps


def _to_slot(a, n, i):
    return lax.dynamic_update_slice(jnp.zeros((n,) + a.shape, a.dtype), a[None], (i,) + (0,) * a.ndim)


def _split_start(srcs, land_shapes, n_cp, make, name):
    ns, nl = len(srcs), len(land_shapes)

    def body(*refs):
        s_in = refs[:ns]
        ssem, rsem = refs[ns + nl], refs[ns + nl + 1]
        l_out = refs[2 * ns + nl + 2:2 * ns + 2 * nl + 2]
        token = refs[2 * ns + 2 * nl + 2]
        s_out = refs[ns + nl + 2:2 * ns + nl + 2]
        for cp in make(s_in, l_out if nl else s_out, ssem, rsem):
            cp.start()
        token[...] = jnp.zeros_like(token)

    lands = [pltpu.with_memory_space_constraint(lax.empty(sh.shape, sh.dtype), pltpu.HBM) for sh in land_shapes]
    out_shape = (pltpu.SemaphoreType.DMA((n_cp,)), pltpu.SemaphoreType.DMA((n_cp,)),
                 *[pltpu.HBM(b.shape, b.dtype) for b in srcs], *[pltpu.HBM(b.shape, b.dtype) for b in land_shapes],
                 SDS((8, 128), f32))
    return _hbm_call(
        body, name=name, out_shape=out_shape, in_specs=[HBM_SPEC] * (ns + nl),
        out_specs=(SEM_SPEC, SEM_SPEC, *[HBM_SPEC] * (ns + nl), VMEM_SPEC),
        input_output_aliases={i: 2 + i for i in range(ns + nl)}, compiler_params=_cp(has_side_effects=DATAFLOW),
    )(*[pltpu.with_memory_space_constraint(b, pltpu.HBM) for b in srcs], *lands)


def _split_wait(ssem, rsem, srcs, lands, after, make, name):
    ns, nl = len(srcs), len(lands)

    def body(*refs):
        s_in, l_in = refs[:ns], refs[ns:ns + nl]
        ssem_ref, rsem_ref = refs[ns + nl], refs[ns + nl + 1]
        for cp in make(s_in, l_in if nl else s_in, ssem_ref, rsem_ref):
            cp.wait_send()
            cp.wait_recv()

    outs = _hbm_call(
        body, name=name, out_shape=tuple(pltpu.HBM(b.shape, b.dtype) for b in (*srcs, *lands)),
        in_specs=[HBM_SPEC] * (ns + nl) + [SEM_SPEC, SEM_SPEC, ANY_SPEC], out_specs=tuple([HBM_SPEC] * (ns + nl)),
        input_output_aliases={i: i for i in range(ns + nl)}, compiler_params=_cp(has_side_effects=DATAFLOW),
    )(*srcs, *lands, ssem, rsem, after)
    return list(outs[:ns]), list(outs[ns:])


def _sibling_send(halves, name):
    n = len(halves)

    def body(*refs):
        ins, outs = refs[:n], refs[n:2 * n]
        ssem, rsem = refs[2 * n:]
        x, y, c = _my_pos()
        cps = []
        for a in range(n):
            cp = pltpu.make_async_remote_copy(src_ref=ins[a], dst_ref=outs[a], send_sem=ssem.at[a],
                                              recv_sem=rsem.at[a], device_id=(x, y, 1 - c), device_id_type=MESH)
            cp.start()
            cps.append(cp)
        for cp in cps:
            cp.wait_recv()
        for cp in cps:
            cp.wait_send()

    out_shape = [SDS(h.shape, h.dtype) for h in halves]
    return _hbm_call(
        body, name=name, out_shape=out_shape, in_specs=[ANY_SPEC] * n, out_specs=[ANY_SPEC] * n,
        scratch_shapes=[pltpu.SemaphoreType.DMA((n,)), pltpu.SemaphoreType.DMA((n,))],
    )(*halves)


def _adaln_shard(cc, w_ada_shard):
    def body(c_ref, w_ref, m_ref, sc_ref):
        sc = _silu(c_ref[...])
        sc_ref[...] = sc
        m_ref[...] = jnp.dot(sc, w_ref[...], precision=HIGHEST, preferred_element_type=f32)

    return _hbm_call(
        body, name="adaln_shard", out_shape=(SDS((16, w_ada_shard.shape[1]), f32), SDS((16, D), f32)),
        in_specs=[VMEM_SPEC, VMEM_SPEC], out_specs=(VMEM_SPEC, VMEM_SPEC), compiler_params=_cp(32),
    )(cc, w_ada_shard)


def _prenorm(xx, norm_g, mrow, b_ada, tm, name):
    n = xx.shape[0]

    def body(x_ref, g_ref, m_ref, b_ref, h_ref, ht_ref):
        x = x_ref[...]
        shift = m_ref[:, 0:D] + b_ref[:, 0:D]
        scale = m_ref[:, D:2 * D] + b_ref[:, D:2 * D]
        r = lax.rsqrt(jnp.mean(x * x, axis=-1, keepdims=True) + RMS_EPS)
        y = (x * r) * g_ref[...]
        h = y * (1.0 + scale) + shift
        h_ref[...] = h.astype(bf16)
        ht_ref[...] = h.T.astype(bf16)

    row = lambda i: (i, 0)
    fixed = lambda i: (0, 0)
    return _hbm_call(
        body, name=name, out_shape=(SDS((n, D), bf16), SDS((D, n), bf16)), grid=(n // tm,),
        in_specs=[pl.BlockSpec((tm, D), row), pl.BlockSpec((1, D), fixed), pl.BlockSpec((1, 3 * D), fixed),
                  pl.BlockSpec((1, 3 * D), fixed)],
        out_specs=(pl.BlockSpec((tm, D), row), pl.BlockSpec((D, tm), lambda i: (0, i))),
    )(xx, norm_g, mrow, b_ada)


def _in_proj_own(h, w_own, jvec):
    tm = 512

    def body(j_ref, h_ref, w_ref, p_ref):
        p_ref[...] = jnp.dot(h_ref[...], w_ref[...].astype(bf16), preferred_element_type=f32)

    grid_spec = pltpu.PrefetchScalarGridSpec(
        num_scalar_prefetch=1, grid=(S // tm,),
        in_specs=[pl.BlockSpec((tm, D), lambda i, j: (i, 0)), pl.BlockSpec((D, D), lambda i, j: (0, 0))],
        out_specs=pl.BlockSpec((tm, D), lambda i, j: (i, j[0])))
    return _hbm_call(body, name="in_proj_own", out_shape=SDS((S, 4 * D), f32), grid_spec=grid_spec,
                     compiler_params=_cp(40))(jvec, h, w_own)


def _in_proj_block(h, w4, p, bvec, name, after=None):
    tm = 512

    def body(b_ref, h_ref, w_ref, p_in_ref, after_ref, p_ref):
        p_ref[...] = jnp.dot(h_ref[...], w_ref[...], preferred_element_type=f32)

    grid_spec = pltpu.PrefetchScalarGridSpec(
        num_scalar_prefetch=1, grid=(S // tm,),
        in_specs=[pl.BlockSpec((tm, D), lambda i, b: (i, 0)), pl.BlockSpec((None, D, D), lambda i, b: (b[0], 0, 0)),
                  ANY_SPEC, ANY_SPEC],
        out_specs=pl.BlockSpec((tm, D), lambda i, b: (i, b[0])))
    return _hbm_call(body, name=name, out_shape=SDS((S, 4 * D), f32), grid_spec=grid_spec,
                     input_output_aliases={3: 0})(bvec, h, w4, p, bvec if after is None else after)


def _ctx_proj(hc, w4):
    def body(h_ref, w0_ref, w1_ref, p_ref):
        hv = h_ref[...]
        p_ref[:, 0:DA] = jnp.dot(hv, w0_ref[:, DA:2 * DA], preferred_element_type=f32)
        p_ref[:, DA:2 * DA] = jnp.dot(hv, w1_ref[:, 0:DA], preferred_element_type=f32)

    return _hbm_call(
        body, name="ctx_proj", out_shape=SDS((L, 2 * DA), f32), grid=(1,),
        in_specs=[pl.BlockSpec((L, D), lambda i: (0, 0)), pl.BlockSpec((None, D, D), lambda i: (0, 0, 0)),
                  pl.BlockSpec((None, D, D), lambda i: (1, 0, 0))],
        out_specs=pl.BlockSpec((L, 2 * DA), lambda i: (0, 0)),
    )(hc, w4, w4)


def _head_ones():
    r = lax.broadcasted_iota(i32, (DA, DA), 0) // DH
    c = lax.broadcasted_iota(i32, (DA, DA), 1) // DH
    return (r == c).astype(bf16)


def _head_sum(v, ones_bd):
    hi = v.astype(bf16)
    lo = (v - hi.astype(f32)).astype(bf16)
    return jnp.dot(hi, ones_bd, preferred_element_type=f32) + jnp.dot(lo, ones_bd, preferred_element_type=f32)


def _swap16(v):
    lane = lax.broadcasted_iota(i32, v.shape, 1)
    return jnp.where((lane & 31) < 16, pltpu.roll(v, DA - 16, 1), pltpu.roll(v, 16, 1))


def _rope_block(ct_ref, rt_ref, tm):
    rows = [jnp.tile(rt_ref[8 * j:8 * j + 8, :], (GW // 8, 1)) for j in range(tm // GW)]
    return jnp.tile(ct_ref[...], (tm // GW, 1)) + jnp.concatenate(rows, axis=0)


def _rope_specs(tm):
    col = pl.BlockSpec((GW, DA), lambda i: (0, 0))
    row = pl.BlockSpec((8 * tm // GW, DA), lambda i: (i, 0))
    return [col, row, col, row]


def _qk_prep(p, gq, gk, rope):
    tm = 256

    def body(qk_ref, v_ref, gq_ref, gk_ref, cc_ref, cr_ref, sc_ref, sr_ref, qr_ref, qp_ref, kr_ref, vh_ref):
        ones_bd = _head_ones()
        cs, sn = _rope_block(cc_ref, cr_ref, tm), _rope_block(sc_ref, sr_ref, tm)
        q = qk_ref[:, 0:DA]
        k = qk_ref[:, DA:2 * DA]
        yq = (q * lax.rsqrt(_head_sum(q * q, ones_bd) * (1.0 / DH) + RMS_EPS)) * gq_ref[...]
        yk = (k * lax.rsqrt(_head_sum(k * k, ones_bd) * (1.0 / DH) + RMS_EPS)) * gk_ref[...]
        qr = (yq * cs + _swap16(yq) * sn) * QK_SCALE
        qp = yq * QK_SCALE
        kr = yk * cs + _swap16(yk) * sn
        vv = v_ref[...]
        for hh in range(H):
            sl = slice(hh * DH, (hh + 1) * DH)
            qr_ref[hh] = qr[:, sl].astype(bf16)
            qp_ref[hh] = qp[:, sl].astype(bf16)
            kr_ref[hh] = kr[:, sl].astype(bf16)
            vh_ref[hh] = vv[:, sl].astype(bf16)

    hm = SDS((H, S, DH), bf16)
    hspec = pl.BlockSpec((H, tm, DH), lambda i: (0, i, 0))
    fixed = lambda i: (0, 0)
    return _hbm_call(
        body, name="qk_prep", out_shape=(hm, hm, hm, hm), grid=(S // tm,),
        in_specs=[pl.BlockSpec((tm, 2 * DA), lambda i: (i, 0)), pl.BlockSpec((tm, DA), lambda i: (i, 2)),
                  pl.BlockSpec((1, DA), fixed), pl.BlockSpec((1, DA), fixed)] + _rope_specs(tm),
        out_specs=(hspec, hspec, hspec, hspec),
    )(p, p, gq, gk, *rope)


def _ctx_prep(pc, gk):
    def body(p_ref, gk_ref, kc_ref, vc_ref):
        ones_bd = _head_ones()
        k = p_ref[:, 0:DA]
        yk = (k * lax.rsqrt(_head_sum(k * k, ones_bd) * (1.0 / DH) + RMS_EPS)) * gk_ref[...]
        vv = p_ref[:, DA:2 * DA]
        for hh in range(H):
            sl = slice(hh * DH, (hh + 1) * DH)
            kc_ref[hh] = yk[:, sl].astype(bf16)
            vc_ref[hh] = vv[:, sl].astype(bf16)

    hm = SDS((H, L, DH), bf16)
    return _hbm_call(
        body, name="ctx_prep", out_shape=(hm, hm), in_specs=[VMEM_SPEC, VMEM_SPEC], out_specs=(VMEM_SPEC, VMEM_SPEC),
    )(pc, gk)


def _tile_pieces():
    out = []
    for (i0, u0) in TILE_GEOM:
        rows = []
        for j in range(2):
            i = i0 + j
            rs = _row_start(i)
            rows.append([(u0 + u - i + WIN_H - 1) if rs <= u0 + u < rs + WIN_H else None for u in range(KR)])
        out.append(rows)
    return out


def _bias_prep(rpb_rev_pad):
    pieces = _tile_pieces()

    def body(r_ref, o_ref):
        rp = r_ref[...]
        xs = jnp.broadcast_to(rp[:, None, :], (N_DR, GW, 128)).reshape(N_DR * GW, 128)
        row = lax.broadcasted_iota(i32, xs.shape, 0)
        lane = lax.broadcasted_iota(i32, xs.shape, 1)
        for b in range(6):
            xs = jnp.where(((row >> b) & 1) == 1, pltpu.roll(xs, 1 << b, 1), xs)
        xs = pltpu.roll(xs, 128 - (WIN_W - 1), 1)
        k = row & (GW - 1)
        c0 = jnp.clip(lane - WIN_W // 2, 0, GW - WIN_W)
        xs = jnp.where((k >= c0) & (k < c0 + WIN_W), xs, NEG)
        neg = jnp.full((GW, GW), NEG, f32)
        for t in range(NT):
            for j in range(2):
                for u in range(KR):
                    dr = pieces[t][j][u]
                    piece = neg if dr is None else xs[dr * GW:(dr + 1) * GW, 0:GW]
                    o_ref[t, u * GW:(u + 1) * GW, j * GW:(j + 1) * GW] = piece

    return _hbm_call(
        body, name="bias_prep", out_shape=SDS((H, NT, KB, QB), f32), grid=(H,),
        in_specs=[pl.BlockSpec((None, N_DR, 128), lambda h: (h, 0, 0))],
        out_specs=pl.BlockSpec((None, NT, KB, QB), lambda h: (h, 0, 0, 0)),
    )(rpb_rev_pad)


def _bias_tiles(rpb2, token=None):
    rpb_pad = jnp.pad(rpb2[:, :, ::-1], ((0, 0), (0, 0), (0, 128 - N_DC)))
    return _bias_prep(rpb_pad if token is None else rpb_pad + token)


def _block_geom(b):
    qs = b * QB
    ks = min(max(2 * b - 4, 0), ROWS - KR) * GW
    t = b if b < 2 else (b - (NQB - NT) if b > NQB - 3 else 2)
    return qs, ks, t


def _tt(a, b):
    return lax.dot_general(a, b, (((1,), (1,)), ((), ())), preferred_element_type=f32)


def _tn(a, b):
    return lax.dot_general(a, b, (((0,), (0,)), ((), ())), preferred_element_type=f32)


def _softmax_t(s_lat, s_ctx):
    m = jnp.maximum(jnp.max(s_lat, axis=0, keepdims=True), jnp.max(s_ctx, axis=0, keepdims=True))
    e_lat = jnp.exp(s_lat - m)
    e_ctx = jnp.exp(s_ctx - m)
    inv = 1.0 / (jnp.sum(e_lat, axis=0, keepdims=True) + jnp.sum(e_ctx, axis=0, keepdims=True))
    return e_lat * inv, e_ctx * inv


def _staged(n_blocks, stages):
    held = [dict() for _ in stages]
    for step in range(n_blocks + len(stages) - 1):
        for s, fn in enumerate(stages):
            b = step - s
            if 0 <= b < n_blocks:
                held[s][b] = fn(b) if s == 0 else fn(b, held[s - 1].pop(b))


def _attn_fwd(qr, qp, kr, vh, kc, vc, btt):
    def body(qr_ref, qp_ref, kr_ref, v_ref, kc_ref, vc_ref, bt_ref, o_ref):
        kcv, vcv = kc_ref[...], vc_ref[...]

        def scores(b):
            qs, ks, t = _block_geom(b)
            return (_tt(kr_ref[ks:ks + KB, :], qr_ref[qs:qs + QB, :]) + bt_ref[t], _tt(kcv, qp_ref[qs:qs + QB, :]))

        def probs(b, sc):
            p_lat, p_ctx = _softmax_t(*sc)
            return p_lat.astype(bf16), p_ctx.astype(bf16)

        def values(b, p):
            qs, ks, _ = _block_geom(b)
            o_ref[qs:qs + QB, :] = _tn(p[0], v_ref[ks:ks + KB, :]) + _tn(p[1], vcv)

        _staged(NQB, (scores, probs, values))

    sq = pl.BlockSpec((None, S, DH), lambda h: (h, 0, 0))
    sc = pl.BlockSpec((None, L, DH), lambda h: (h, 0, 0))
    return _hbm_call(
        body, name="attn_fwd", out_shape=SDS((H, S, DH), f32), grid=(H,),
        in_specs=[sq, sq, sq, sq, sc, sc, pl.BlockSpec((None, NT, KB, QB), lambda h: (h, 0, 0, 0))],
        out_specs=sq, compiler_params=_cp(48),
    )(qr, qp, kr, vh, kc, vc, btt)


def _shift_rows(v, down):
    n = v.shape[0]
    row = lax.broadcasted_iota(i32, v.shape, 0)
    if down:
        return jnp.where(row == 0, 0.0, pltpu.roll(v, 1, 0))
    return jnp.where(row == n - 1, 0.0, pltpu.roll(v, n - 1, 0))


def _conv_specs():
    col = lambda off: pl.BlockSpec((S, 128), lambda i, off=off: (0, off + i))
    return [col(16), col(20), col(24), col(28), pl.BlockSpec((3, 128), lambda i: (0, i)),
            pl.BlockSpec((1, 128), lambda i: (0, i))]


def _conv_fwd(p, conv_w, conv_b):
    def body(u_ref, bg_ref, cg_ref, zc_ref, w_ref, b_ref, o_ref):
        cu = cg_ref[...] * u_ref[...]
        cv = b_ref[...] + _shift_rows(cu, True) * w_ref[0:1, :]
        cv = cv + cu * w_ref[1:2, :]
        cv = cv + _shift_rows(cu, False) * w_ref[2:3, :]
        o_ref[...] = ((bg_ref[...] * cv) * _silu(zc_ref[...])).astype(bf16)

    return _hbm_call(
        body, name="conv_fwd", out_shape=SDS((S, DC), bf16), grid=(DC // 128,),
        in_specs=_conv_specs(), out_specs=pl.BlockSpec((S, 128), lambda i: (0, i)), compiler_params=_cp(40),
    )(p, p, p, p, conv_w, conv_b)


DP_Q, DP_K, DP_V, DP_ZA, DP_U, DP_BG, DP_CG, DP_ZC = range(8)


def _out_proj_loss(o, p, conv_g, w_out, xx, tgt, mrow, b_ada):
    tm = 256

    def body(o_ref, za_ref, c_ref, w_ref, x_ref, t_ref, m_ref, b_ref,
             dy_ref, dconv_ref, dp_ref, do_ref, gwo_ref, dgate_ref, loss_ref):
        k = pl.program_id(0)

        @pl.when(k == 0)
        def _():
            gwo_ref[...] = jnp.zeros_like(gwo_ref)
            dgate_ref[...] = jnp.zeros_like(dgate_ref)
            loss_ref[0, 0] = 0.0

        gate = m_ref[:, 2 * D:3 * D] + b_ref[:, 2 * D:3 * D]
        za = za_ref[...]
        sz = _silu(za)
        om = _merge_heads(o_ref)
        av, cv = (om * sz).astype(bf16), c_ref[...]
        mo = jnp.dot(av, w_ref[0:DA, :], preferred_element_type=f32)
        mo = mo + jnp.dot(cv, w_ref[DA:DA + DC, :], preferred_element_type=f32)
        y = x_ref[...] + gate * mo
        diff = y - t_ref[...]
        loss_ref[0, 0] += jnp.sum(diff * diff)
        dy = diff * (1.0 / D)
        dy_ref[...] = dy
        dgate_ref[...] += jnp.sum(dy * mo, axis=0, keepdims=True)
        dmo = (dy * gate).astype(bf16)
        dmix = _tt(dmo, w_ref[...])
        dattn = dmix[:, 0:DA]
        dconv_ref[...] = dmix[:, DA:DA + DC]
        a = dattn * sz
        for hh in range(H):
            do_ref[hh] = a[:, hh * DH:(hh + 1) * DH].astype(bf16)
        dp_ref[...] = ((dattn * _dsilu(za)) * om).astype(bf16)
        gwo_ref[0:DA, :] += _tn(av, dmo)
        gwo_ref[DA:DA + DC, :] += _tn(cv, dmo)

    row = lambda i: (i, 0)
    fixed = lambda i: (0, 0)
    hspec = pl.BlockSpec((H, tm, DH), lambda i: (0, i, 0))
    return _hbm_call(
        body, name="out_proj_loss",
        out_shape=(SDS((S, D), f32), SDS((S, DC), f32), SDS((8, S, DA), bf16), SDS((H, S, DH), bf16),
                   SDS((D, D), f32), SDS((1, D), f32), SDS((1, 1), f32)),
        grid=(S // tm,),
        in_specs=[hspec, pl.BlockSpec((tm, DA), lambda i: (i, 3)), pl.BlockSpec((tm, DC), row),
                  pl.BlockSpec((D, D), fixed), pl.BlockSpec((tm, D), row), pl.BlockSpec((tm, D), row),
                  pl.BlockSpec((1, 3 * D), fixed), pl.BlockSpec((1, 3 * D), fixed)],
        out_specs=(pl.BlockSpec((tm, D), row), pl.BlockSpec((tm, DC), row),
                   pl.BlockSpec((None, tm, DA), lambda i: (DP_ZA, i, 0)), hspec, pl.BlockSpec((D, D), fixed),
                   pl.BlockSpec((1, D), fixed), SMEM_SPEC),
        compiler_params=_cp(56, dimension_semantics=("arbitrary",)),
    )(o, p, conv_g, w_out, xx, tgt, mrow, b_ada)


def _conv_bwd(dconv, p, conv_w, conv_b, dp8, after=None):
    def body(d_ref, u_ref, bg_ref, cg_ref, zc_ref, w_ref, b_ref, dp_in_ref, after_ref, dp_ref, gw_ref, gb_ref):
        du_ref, dbg_ref, dcg_ref, dzc_ref = dp_ref.at[0], dp_ref.at[1], dp_ref.at[2], dp_ref.at[3]
        dconv = d_ref[...]
        u, bg, cg, zc = u_ref[...], bg_ref[...], cg_ref[...], zc_ref[...]
        w0, w1, w2 = w_ref[0:1, :], w_ref[1:2, :], w_ref[2:3, :]
        cu = cg * u
        cu_m, cu_p = _shift_rows(cu, True), _shift_rows(cu, False)
        cv = b_ref[...] + cu_m * w0
        cv = cv + cu * w1
        cv = cv + cu_p * w2
        sz = _silu(zc)
        dbg_ref[...] = ((dconv * sz) * cv).astype(bf16)
        dzc_ref[...] = ((dconv * (bg * cv)) * _dsilu(zc)).astype(bf16)
        dcv = (dconv * sz) * bg
        gb_ref[...] = jnp.sum(dcv, axis=0, keepdims=True)
        gw_ref[0:1, :] = jnp.sum(dcv * cu_m, axis=0, keepdims=True)
        gw_ref[1:2, :] = jnp.sum(dcv * cu, axis=0, keepdims=True)
        gw_ref[2:3, :] = jnp.sum(dcv * cu_p, axis=0, keepdims=True)
        gw_ref[3:8, :] = jnp.zeros((5, 128), f32)
        dcu = _shift_rows(dcv, False) * w0 + dcv * w1 + _shift_rows(dcv, True) * w2
        dcg_ref[...] = (dcu * u).astype(bf16)
        du_ref[...] = (dcu * cg).astype(bf16)

    return _hbm_call(
        body, name="conv_bwd", out_shape=(SDS((8, S, DC), bf16), SDS((8, DC), f32), SDS((1, DC), f32)),
        grid=(DC // 128,),
        in_specs=[pl.BlockSpec((S, 128), lambda i: (0, i))] + _conv_specs() + [ANY_SPEC, ANY_SPEC],
        out_specs=(pl.BlockSpec((4, S, 128), lambda i: (DP_U // 4, 0, i)), pl.BlockSpec((8, 128), lambda i: (0, i)),
                   pl.BlockSpec((1, 128), lambda i: (0, i))),
        input_output_aliases={7: 0}, compiler_params=_cp(48),
    )(dconv, p, p, p, p, conv_w, conv_b, dp8, conv_b if after is None else after)


def _attn_bwd(qr, qp, kr, vh, kc, vc, btt, do, after=None):
    def body(qr_ref, qp_ref, kr_ref, v_ref, kc_ref, vc_ref, bt_ref, do_ref, after_ref,
             dqr_ref, dqp_ref, dkr_ref, dv_ref, dkc_ref, dvc_ref, dbt_ref):
        kcv, vcv = kc_ref[...], vc_ref[...]
        dkr_ref[...] = jnp.zeros_like(dkr_ref)
        dv_ref[...] = jnp.zeros_like(dv_ref)
        dbt_ref[...] = jnp.zeros_like(dbt_ref)
        ctx_acc = {}

        def products(b):
            qs, ks, t = _block_geom(b)
            dob = do_ref[qs:qs + QB, :]
            s_lat = _tt(kr_ref[ks:ks + KB, :], qr_ref[qs:qs + QB, :]) + bt_ref[t]
            s_ctx = _tt(kcv, qp_ref[qs:qs + QB, :])
            return s_lat, s_ctx, _tt(v_ref[ks:ks + KB, :], dob), _tt(vcv, dob)

        def score_grads(b, x):
            s_lat, s_ctx, dp_lat, dp_ctx = x
            p_lat, p_ctx = _softmax_t(s_lat, s_ctx)
            delta = jnp.sum(p_lat * dp_lat, axis=0, keepdims=True) + jnp.sum(p_ctx * dp_ctx, axis=0, keepdims=True)
            ds_lat = p_lat * (dp_lat - delta)
            ds_ctx = p_ctx * (dp_ctx - delta)
            return ds_lat, ds_lat.astype(bf16), ds_ctx.astype(bf16), p_lat.astype(bf16), p_ctx.astype(bf16)

        def operand_grads(b, y):
            qs, ks, t = _block_geom(b)
            ds_lat, dsb_lat, dsb_ctx, pb_lat, pb_ctx = y
            qrb, qpb, dob = qr_ref[qs:qs + QB, :], qp_ref[qs:qs + QB, :], do_ref[qs:qs + QB, :]
            dbt_ref[t] += ds_lat
            dqr_ref[qs:qs + QB, :] = _tn(dsb_lat, kr_ref[ks:ks + KB, :])
            dqp_ref[qs:qs + QB, :] = _tn(dsb_ctx, kcv)
            dkr_ref[ks:ks + KB, :] += jnp.dot(dsb_lat, qrb, preferred_element_type=f32)
            dv_ref[ks:ks + KB, :] += jnp.dot(pb_lat, dob, preferred_element_type=f32)
            dkc = jnp.dot(dsb_ctx, qpb, preferred_element_type=f32)
            dvc = jnp.dot(pb_ctx, dob, preferred_element_type=f32)
            ctx_acc["k"] = dkc if b == 0 else ctx_acc["k"] + dkc
            ctx_acc["v"] = dvc if b == 0 else ctx_acc["v"] + dvc

        _staged(NQB, (products, score_grads, operand_grads))
        dkc_ref[...] = ctx_acc["k"]
        dvc_ref[...] = ctx_acc["v"]

    sq = pl.BlockSpec((None, S, DH), lambda h: (h, 0, 0))
    sc = pl.BlockSpec((None, L, DH), lambda h: (h, 0, 0))
    sb = pl.BlockSpec((None, NT, KB, QB), lambda h: (h, 0, 0, 0))
    big, ctxs = SDS((H, S, DH), f32), SDS((H, L, DH), f32)
    return _hbm_call(
        body, name="attn_bwd", out_shape=(big, big, big, big, ctxs, ctxs, SDS((H, NT, KB, QB), f32)), grid=(H,),
        in_specs=[sq, sq, sq, sq, sc, sc, sb, sq, ANY_SPEC], out_specs=(sq, sq, sq, sq, sc, sc, sb),
        compiler_params=_cp(56),
    )(qr, qp, kr, vh, kc, vc, btt, do, do if after is None else after)


def _bias_bwd(dbtt, after=None):
    pieces = _tile_pieces()

    def body(d_ref, after_ref, o_ref, scr):
        scr[...] = jnp.zeros_like(scr)
        acc = [None] * N_DR
        for t in range(NT):
            for j in range(2):
                for u in range(KR):
                    dr = pieces[t][j][u]
                    if dr is None:
                        continue
                    piece = d_ref[t, u * GW:(u + 1) * GW, j * GW:(j + 1) * GW]
                    acc[dr] = piece if acc[dr] is None else acc[dr] + piece
        for dr in range(N_DR):
            scr[dr * GW:(dr + 1) * GW, 0:GW] = acc[dr]
        xs = pltpu.roll(scr[...], WIN_W - 1, 1)
        row = lax.broadcasted_iota(i32, xs.shape, 0)
        for b in range(6):
            xs = jnp.where(((row >> b) & 1) == 1, pltpu.roll(xs, 128 - (1 << b), 1), xs)
        rev = jnp.sum(xs.reshape(N_DR, GW, 128), axis=1)
        a = lax.broadcasted_iota(i32, (128, 128), 0)
        b = lax.broadcasted_iota(i32, (128, 128), 1)
        flip = ((a + b == N_DC - 1) & (a < N_DC)).astype(f32)
        o_ref[...] = jnp.dot(rev, flip, precision=HIGHEST, preferred_element_type=f32)

    return _hbm_call(
        body, name="bias_bwd", out_shape=SDS((H, N_DR, 128), f32), grid=(H,),
        in_specs=[pl.BlockSpec((None, NT, KB, QB), lambda h: (h, 0, 0, 0)), ANY_SPEC],
        out_specs=pl.BlockSpec((None, N_DR, 128), lambda h: (h, 0, 0)),
        scratch_shapes=[pltpu.VMEM((N_DR * GW, 128), f32)],
    )(dbtt, dbtt if after is None else after)


def _merge_heads(ref):
    return jnp.concatenate([ref[hh] for hh in range(H)], axis=1)


def _head_norm_bwd(xraw, gain, dy, ones_bd):
    r = lax.rsqrt(_head_sum(xraw * xraw, ones_bd) * (1.0 / DH) + RMS_EPS)
    xh = xraw * r
    gdy = dy * gain
    dx = r * (gdy - xh * (_head_sum(xh * gdy, ones_bd) * (1.0 / DH)))
    return dx, jnp.sum(dy * xh, axis=0, keepdims=True)


def _qk_bwd(dqr, dqp, dkr, dvh, p, gq, gk, rope, dp8):
    tm = 256

    def body(dqr_ref, dqp_ref, dkr_ref, dv_ref, qk_ref, gq_ref, gk_ref, cc_ref, cr_ref, sc_ref, sr_ref, dp_in_ref,
             dp_ref, ggq_ref, ggk_ref):
        dq_ref, dk_ref, dvo_ref = dp_ref.at[DP_Q], dp_ref.at[DP_K], dp_ref.at[DP_V]

        @pl.when(pl.program_id(0) == 0)
        def _():
            ggq_ref[...] = jnp.zeros_like(ggq_ref)
            ggk_ref[...] = jnp.zeros_like(ggk_ref)

        ones_bd = _head_ones()
        cs, sn = _rope_block(cc_ref, cr_ref, tm), _rope_block(sc_ref, sr_ref, tm)
        a = _merge_heads(dqr_ref)
        dyq = ((a * cs - _swap16(a) * sn) + _merge_heads(dqp_ref)) * QK_SCALE
        bk = _merge_heads(dkr_ref)
        dyk = bk * cs - _swap16(bk) * sn
        dq, gq_part = _head_norm_bwd(qk_ref[:, 0:DA], gq_ref[...], dyq, ones_bd)
        dk, gk_part = _head_norm_bwd(qk_ref[:, DA:2 * DA], gk_ref[...], dyk, ones_bd)
        dq_ref[...] = dq.astype(bf16)
        dk_ref[...] = dk.astype(bf16)
        dvo_ref[...] = _merge_heads(dv_ref).astype(bf16)
        ggq_ref[...] += gq_part
        ggk_ref[...] += gk_part

    hspec = pl.BlockSpec((H, tm, DH), lambda i: (0, i, 0))
    fixed = pl.BlockSpec((1, DA), lambda i: (0, 0))
    return _hbm_call(
        body, name="qk_bwd", out_shape=(SDS((8, S, DA), bf16), SDS((1, DA), f32), SDS((1, DA), f32)), grid=(S // tm,),
        in_specs=[hspec, hspec, hspec, hspec, pl.BlockSpec((tm, 2 * DA), lambda i: (i, 0)), fixed, fixed]
        + _rope_specs(tm) + [ANY_SPEC],
        out_specs=(pl.BlockSpec((3, tm, DA), lambda i: (0, i, 0)), fixed, fixed), input_output_aliases={11: 0},
        compiler_params=_cp(40, dimension_semantics=("arbitrary",)),
    )(dqr, dqp, dkr, dvh, p, gq, gk, *rope, dp8)


def _ctx_bwd(dkc, dvc, pc, gk):
    def body(dkc_ref, dvc_ref, p_ref, gk_ref, dk_ref, dv_ref, ggk_ref):
        ones_bd = _head_ones()
        dk, gk_part = _head_norm_bwd(p_ref[:, 0:DA], gk_ref[...], _merge_heads(dkc_ref), ones_bd)
        dk_ref[...] = dk.astype(bf16)
        dv_ref[...] = _merge_heads(dvc_ref).astype(bf16)
        ggk_ref[...] = gk_part

    piece = SDS((L, DA), bf16)
    return _hbm_call(
        body, name="ctx_bwd", out_shape=(piece, piece, SDS((1, DA), f32)), in_specs=[VMEM_SPEC] * 4,
        out_specs=(VMEM_SPEC,) * 3,
    )(dkc, dvc, pc, gk)


def _grad_w_in(ht, dp8, hct, dkc_raw, dvc_m):
    tm = 256

    def body(ht_ref, p_ref, hct_ref, dk_ref, dv_ref, g_ref):
        j = pl.program_id(0)
        hv = ht_ref[...]
        g_ref[:, 0:DA] = jnp.dot(hv, p_ref[0], preferred_element_type=f32)
        g_ref[:, DA:2 * DA] = jnp.dot(hv, p_ref[1], preferred_element_type=f32)

        @pl.when(j == 0)
        def _():
            g_ref[:, DA:2 * DA] += jnp.dot(hct_ref[...], dk_ref[...], preferred_element_type=f32)

        @pl.when(j == 1)
        def _():
            g_ref[:, 0:DA] += jnp.dot(hct_ref[...], dv_ref[...], preferred_element_type=f32)

    fixed = lambda j, m: (0, 0)
    return _hbm_call(
        body, name="grad_w_in", out_shape=SDS((4, D, D), f32), grid=(4, D // tm),
        in_specs=[pl.BlockSpec((tm, S), lambda j, m: (m, 0)), pl.BlockSpec((2, S, DA), lambda j, m: (j, 0, 0)),
                  pl.BlockSpec((tm, L), lambda j, m: (m, 0)), pl.BlockSpec((L, DA), fixed), pl.BlockSpec((L, DA), fixed)],
        out_specs=pl.BlockSpec((None, tm, D), lambda j, m: (j, m, 0)),
        compiler_params=_cp(40, dimension_semantics=("arbitrary", "arbitrary")),
    )(ht, dp8, hct, dkc_raw, dvc_m)


def _norm_mod_bwd(x, dh, g, scale):
    r = lax.rsqrt(jnp.mean(x * x, axis=-1, keepdims=True) + RMS_EPS)
    xh = x * r
    y = xh * g
    dshift = jnp.sum(dh, axis=0, keepdims=True)
    dscale = jnp.sum(dh * y, axis=0, keepdims=True)
    dyn = dh * (1.0 + scale)
    dg = jnp.sum(dyn * xh, axis=0, keepdims=True)
    gdy = dyn * g
    dx = r * (gdy - xh * jnp.mean(xh * gdy, axis=-1, keepdims=True))
    return dx, dshift, dscale, dg


def _dh_grad_x(dp8, w4, xx, dy, norm_g, mrow, b_ada):
    tm = 256

    def body(p_ref, w_ref, x_ref, dy_ref, g_ref, m_ref, b_ref, gx_ref, dsh_ref, dsc_ref, dg_ref):
        @pl.when(pl.program_id(0) == 0)
        def _():
            dsh_ref[...] = jnp.zeros_like(dsh_ref)
            dsc_ref[...] = jnp.zeros_like(dsc_ref)
            dg_ref[...] = jnp.zeros_like(dg_ref)

        dh = None
        for j in range(4):
            for half in range(2):
                term = _tt(p_ref[2 * j + half], w_ref[j, :, half * DA:(half + 1) * DA])
                dh = term if dh is None else dh + term
        scale = m_ref[:, D:2 * D] + b_ref[:, D:2 * D]
        dx, dshift, dscale, dg = _norm_mod_bwd(x_ref[...], dh, g_ref[...], scale)
        gx_ref[...] = dy_ref[...] + dx
        dsh_ref[...] += dshift
        dsc_ref[...] += dscale
        dg_ref[...] += dg

    row = lambda i: (i, 0)
    fixed = lambda i: (0, 0)
    vec = SDS((1, D), f32)
    return _hbm_call(
        body, name="dh_grad_x", out_shape=(SDS((S, D), f32), vec, vec, vec), grid=(S // tm,),
        in_specs=[pl.BlockSpec((8, tm, DA), lambda i: (0, i, 0)), pl.BlockSpec((4, D, D), lambda i: (0, 0, 0)),
                  pl.BlockSpec((tm, D), row), pl.BlockSpec((tm, D), row), pl.BlockSpec((1, D), fixed),
                  pl.BlockSpec((1, 3 * D), fixed), pl.BlockSpec((1, 3 * D), fixed)],
        out_specs=(pl.BlockSpec((tm, D), row), pl.BlockSpec((1, D), fixed), pl.BlockSpec((1, D), fixed),
                   pl.BlockSpec((1, D), fixed)),
        compiler_params=_cp(56, dimension_semantics=("arbitrary",)),
    )(dp8, w4, xx, dy, norm_g, mrow, b_ada)


def _dhc_sums(dkc_raw, dvc_m, w4, ctx2, norm_g, mrow_c, b_ada):
    def body(dk_ref, dv_ref, w0_ref, w1_ref, x_ref, g_ref, m_ref, b_ref, dsh_ref, dsc_ref, dg_ref):
        dh = _tt(dk_ref[...], w0_ref[:, DA:2 * DA]) + _tt(dv_ref[...], w1_ref[:, 0:DA])
        scale = m_ref[:, D:2 * D] + b_ref[:, D:2 * D]
        _, dshift, dscale, dg = _norm_mod_bwd(x_ref[...], dh, g_ref[...], scale)
        dsh_ref[...] = dshift
        dsc_ref[...] = dscale
        dg_ref[...] = dg

    fixed = lambda i: (0, 0)
    vec = SDS((1, D), f32)
    vspec = pl.BlockSpec((1, D), fixed)
    return _hbm_call(
        body, name="dhc_sums", out_shape=(vec, vec, vec), grid=(1,),
        in_specs=[pl.BlockSpec((L, DA), fixed), pl.BlockSpec((L, DA), fixed),
                  pl.BlockSpec((None, D, D), lambda i: (0, 0, 0)), pl.BlockSpec((None, D, D), lambda i: (1, 0, 0)),
                  pl.BlockSpec((L, D), fixed), vspec, pl.BlockSpec((1, 3 * D), fixed), pl.BlockSpec((1, 3 * D), fixed)],
        out_specs=(vspec, vspec, vspec), compiler_params=_cp(32),
    )(dkc_raw, dvc_m, w4, w4, ctx2, norm_g, mrow_c, b_ada)


def _rope_tables():
    nf = DH // 4
    inv = np.float32(ROPE_THETA) ** (-np.arange(nf, dtype=np.float32) / np.float32(nf))
    ang_c = np.arange(GW, dtype=np.float32)[:, None] * inv
    ang_r = np.arange(ROWS, dtype=np.float32)[:, None] * inv
    zc, zr = np.zeros((GW, 2 * nf), np.float32), np.zeros((ROWS, 2 * nf), np.float32)
    ct_cos = np.tile(np.concatenate([zc, np.cos(ang_c), np.cos(ang_c)], axis=1), (1, H))
    ct_sin = np.tile(np.concatenate([zc, -np.sin(ang_c), np.sin(ang_c)], axis=1), (1, H))
    rt_cos = np.tile(np.concatenate([np.cos(ang_r), np.cos(ang_r), zr], axis=1), (1, H))
    rt_sin = np.tile(np.concatenate([-np.sin(ang_r), np.sin(ang_r), zr], axis=1), (1, H))
    rep8 = lambda t: np.ascontiguousarray(np.broadcast_to(t[:, None, :], (ROWS, 8, DA))).reshape(ROWS * 8, DA)
    return tuple(jnp.asarray(t, f32) for t in (ct_cos, rep8(rt_cos), ct_sin, rep8(rt_sin)))


def _local_step(xx, ctx2, tgt, mrow, mrow_c, b_ada, norm_g, weights, q_norm_g, k_norm_g, btb, conv_w_full, conv_b,
                hooks=None):
    hooks = hooks or {}
    gq = jnp.tile(q_norm_g, (1, H))
    gk = jnp.tile(k_norm_g, (1, H))
    rope = _rope_tables()

    h, ht = _prenorm(xx, norm_g, mrow, b_ada, 256, "prenorm_x")
    hc, hct = _prenorm(ctx2, norm_g, mrow_c, b_ada, L, "prenorm_ctx")
    jv = weights["jvec"]
    p = _in_proj_own(h, weights["own"], jv)
    w4 = weights["near"](p)
    p = _in_proj_block(h, w4, p, jv ^ 1, "in_proj_y")
    p = _in_proj_block(h, w4, p, jv ^ 2, "in_proj_x")
    w4, started = weights["far"](w4, p)
    p = _in_proj_block(h, w4, p, jv ^ 3, "in_proj_xy", after=started)
    pc = _ctx_proj(hc, w4)
    qr, qp, kr, vh = _qk_prep(p, gq, gk, rope)
    kc, vc = _ctx_prep(pc, gk)
    o = _attn_fwd(qr, qp, kr, vh, kc, vc, btb)
    conv_g = _conv_fwd(p, conv_w_full, conv_b)
    w_out_full = weights["out"](o)
    dy, dconv, dp8, do, g_w_out, dgate, loss_sum = _out_proj_loss(o, p, conv_g, w_out_full, xx, tgt, mrow, b_ada)
    started = hooks["g_w_out"](g_w_out) if "g_w_out" in hooks else None
    dp8, g_conv_w, g_conv_b = _conv_bwd(dconv, p, conv_w_full, conv_b, dp8, after=started)
    started = hooks["after_conv"](dp8) if "after_conv" in hooks else None
    dqr, dqp, dkr, dvh, dkc, dvc, dbtb = _attn_bwd(qr, qp, kr, vh, kc, vc, btb, do, after=started)
    dp8, g_gq, g_gk = _qk_bwd(dqr, dqp, dkr, dvh, p, gq, gk, rope, dp8)
    dkc_raw, dvc_m, g_gk_c = _ctx_bwd(dkc, dvc, pc, gk)
    g_w_in = _grad_w_in(ht, dp8, hct, dkc_raw, dvc_m)
    behind = lambda token: b_ada if token is None else b_ada + token[0:1, 0:1]
    started = hooks["g_w_in"](g_w_in) if "g_w_in" in hooks else None
    dshift_c, dscale_c, dng_c = _dhc_sums(dkc_raw, dvc_m, w4, ctx2, norm_g, mrow_c, behind(started))
    g_rpb = _bias_bwd(dbtb, after=dshift_c)
    started = hooks["g_rpb"](g_rpb) if "g_rpb" in hooks else None
    grad_x, dshift, dscale, dng = _dh_grad_x(dp8, w4, xx, dy, norm_g, mrow, behind(started))
    return dict(loss_sum=loss_sum, grad_x=grad_x, g_w_in=g_w_in, g_w_out=g_w_out, g_conv_w=g_conv_w,
                g_conv_b=g_conv_b, g_rpb=g_rpb, g_gq=g_gq, g_gk=g_gk, g_gk_c=g_gk_c, dshift=dshift, dscale=dscale,
                dgate=dgate, dng=dng, dshift_c=dshift_c, dscale_c=dscale_c, dng_c=dng_c)


def _pair_sum_w_in(g, r, cvec):
    tr = 128

    def body(c_ref, g_ref, r_ref, t32_ref, tb_ref):
        t = g_ref[...] + r_ref[...]
        t32_ref[...] = t
        tb_ref[...] = t.astype(bf16)

    half = D // 2
    g_spec = pl.BlockSpec((4, tr, D), lambda i, c: (0, c[0] * (half // tr) + i, 0))
    o_spec = pl.BlockSpec((4, tr, D), lambda i, c: (0, i, 0))
    grid_spec = pltpu.PrefetchScalarGridSpec(num_scalar_prefetch=1, grid=(half // tr,), in_specs=[g_spec, o_spec],
                                             out_specs=(o_spec, o_spec))
    return _hbm_call(body, name="pair_sum_w_in", out_shape=(SDS((4, half, D), f32), SDS((4, half, D), bf16)),
                     grid_spec=grid_spec, compiler_params=_cp(40))(cvec, g, r)


def _pair_sum_w_out(g, r, cvec):
    hr = D // 8

    def body(c_ref, g0, g1, g2, g3, r_ref, t32_ref, tb_ref):
        for q, g_ref in enumerate((g0, g1, g2, g3)):
            t = g_ref[...] + r_ref[q]
            t32_ref[q] = t
            tb_ref[q] = t.astype(bf16)

    gspecs = [pl.BlockSpec((hr, D), lambda i, c, q=q: (2 * q + c[0], 0)) for q in range(4)]
    full = pl.BlockSpec((4, hr, D), lambda i, c: (0, 0, 0))
    grid_spec = pltpu.PrefetchScalarGridSpec(num_scalar_prefetch=1, grid=(1,), in_specs=gspecs + [full],
                                             out_specs=(full, full))
    return _hbm_call(body, name="pair_sum_w_out", out_shape=(SDS((4, hr, D), f32), SDS((4, hr, D), bf16)),
                          grid_spec=grid_spec)(cvec, g, g, g, g, r)


def _chip_sum(t32, r2, jvec, name):
    rows = t32.shape[1]
    tr = min(rows, 128)

    def body(j_ref, t_ref, r_ref, u_ref):
        u_ref[...] = ((t_ref[...] + r_ref[0].astype(f32)) + r_ref[1].astype(f32)) + r_ref[2].astype(f32)

    grid_spec = pltpu.PrefetchScalarGridSpec(
        num_scalar_prefetch=1, grid=(rows // tr,),
        in_specs=[pl.BlockSpec((None, tr, D), lambda i, j: (j[0], i, 0)), pl.BlockSpec((3, tr, D), lambda i, j: (0, i, 0))],
        out_specs=pl.BlockSpec((tr, D), lambda i, j: (i, 0)))
    return _hbm_call(body, name=name, out_shape=SDS((rows, D), f32), grid_spec=grid_spec)(jvec, t32, r2)


_PK = {}
_off = 0
for _name, _rows in (("dm", 24), ("dmc", 24), ("dng", 8), ("dng_c", 8), ("gq", 8), ("gk", 8), ("gk_c", 8),
                     ("rpb", H * N_DR), ("conv_b", 8), ("conv_w", 16), ("loss", 8)):
    _PK[_name] = (_off, _off + _rows)
    _off += _rows
PK_ROWS = _off
RS_B_ADA, RS_NORM_G, RS_GQ, RS_GK, RS_RPB, RS_CONV_B, RS_CONV_W, RS_DMC, RS_LOSS, RS_ROWS = (
    0, 24, 32, 40, 48, 168, 176, 192, 216, 224)


def _small_reduce(gathered):
    def body(g_ref, o_ref, dm_ref):
        a0 = _PK["dm"][0]
        dm_ref[...] = jnp.zeros_like(dm_ref)
        for b in range(8):
            for i in range(24):
                dm_ref[b:b + 1, 128 * i:128 * (i + 1)] = g_ref[b, a0 + i:a0 + i + 1, :]
        tot = g_ref[0]
        for b in range(1, 8):
            tot = tot + g_ref[b]

        def rows(name):
            a, z = _PK[name]
            return tot[a:z]

        o_ref[RS_B_ADA:RS_B_ADA + 24] = rows("dm") + rows("dmc")
        o_ref[RS_NORM_G:RS_NORM_G + 8] = rows("dng") + rows("dng_c")
        gq = jnp.broadcast_to(jnp.sum(rows("gq"), axis=0, keepdims=True), (8, 128))
        gk = jnp.broadcast_to(jnp.sum(rows("gk") + rows("gk_c"), axis=0, keepdims=True), (8, 128))
        o_ref[RS_GQ:RS_GQ + 8] = gq + pltpu.roll(gq, DH, 1)
        o_ref[RS_GK:RS_GK + 8] = gk + pltpu.roll(gk, DH, 1)
        o_ref[RS_RPB:RS_RPB + H * N_DR] = rows("rpb")
        o_ref[RS_CONV_B:RS_CONV_B + 8] = rows("conv_b")
        o_ref[RS_CONV_W:RS_CONV_W + 16] = rows("conv_w")
        dmc = rows("dmc")
        o_ref[RS_DMC:RS_DMC + 24] = dmc
        o_ref[RS_LOSS:RS_LOSS + 8] = rows("loss")
        for i in range(24):
            dm_ref[8:9, 128 * i:128 * (i + 1)] = dmc[i:i + 1]

    return _hbm_call(body, name="small_reduce", out_shape=(SDS((RS_ROWS, 128), f32), SDS((16, 3 * D), f32)),
                     in_specs=[VMEM_SPEC], out_specs=(VMEM_SPEC, VMEM_SPEC))(gathered)


def _w_ada_grad(sc16, dm16, w_ada_shard, jvec):
    ncol = w_ada_shard.shape[1]

    def body(j_ref, sc_ref, dm_ref, w_ref, g_ref, part_ref):
        dm = dm_ref[...]
        g_ref[...] = lax.dot_general(sc_ref[...], dm, (((0,), (0,)), ((), ())), precision=HIGHEST,
                                     preferred_element_type=f32)
        part_ref[...] = lax.dot_general(dm[8:16], w_ref[...], (((1,), (1,)), ((), ())), precision=HIGHEST,
                                        preferred_element_type=f32)

    fixed = lambda i, j: (0, 0)
    grid_spec = pltpu.PrefetchScalarGridSpec(
        num_scalar_prefetch=1, grid=(1,),
        in_specs=[pl.BlockSpec((16, D), fixed), pl.BlockSpec((16, ncol), lambda i, j: (0, j[0])),
                  pl.BlockSpec((D, ncol), fixed)],
        out_specs=(pl.BlockSpec((D, ncol), fixed), pl.BlockSpec((8, D), fixed)))
    return _pallas_call(body, name="w_ada_grad", out_shape=(SDS((D, ncol), f32), SDS((8, D), f32)),
                        grid_spec=grid_spec, compiler_params=_cp(40))(jvec, sc16, dm16, w_ada_shard)


def _c_ctx_grad(parts4, c_ctx):
    def body(p_ref, c_ref, o_ref):
        tot = ((p_ref[0] + p_ref[1]) + p_ref[2]) + p_ref[3]
        o_ref[...] = tot[0:1] * _dsilu(c_ref[...].reshape(1, D))

    return _pallas_call(body, name="c_ctx_grad", out_shape=SDS((1, D), f32), in_specs=[VMEM_SPEC, VMEM_SPEC],
                        out_specs=VMEM_SPEC)(parts4, c_ctx)


def _adamw(w, g, m, v, name):
    rows, cols = w.shape
    tr = 256 if rows % 256 == 0 else rows

    def body(w_ref, g_ref, m_ref, v_ref, d_ref, m2_ref, v2_ref):
        gv = g_ref[...]
        m2 = ADAM_B1 * m_ref[...] + (1.0 - ADAM_B1) * gv
        v2 = ADAM_B2 * v_ref[...] + (1.0 - ADAM_B2) * jnp.square(gv)
        m_hat = m2 / (1.0 - ADAM_B1 ** ADAM_STEP)
        v_hat = v2 / (1.0 - ADAM_B2 ** ADAM_STEP)
        d_ref[...] = -ADAM_LR * (m_hat / (jnp.sqrt(v_hat) + ADAM_EPS) + ADAM_WD * w_ref[...])
        m2_ref[...] = m2
        v2_ref[...] = v2

    spec = pl.BlockSpec((tr, cols), lambda i: (i, 0))
    shp = SDS((rows, cols), f32)
    return _hbm_call(body, name=name, out_shape=(shp, shp, shp), grid=(rows // tr,), in_specs=[spec] * 4,
                          out_specs=(spec, spec, spec))(w, g, m, v)


def _adamw_halves(w, g_mine, g_other, m, v, cvec, name):
    rows, cols = w.shape
    half = rows // 2
    tr = min(256, half)
    per_half = half // tr

    def body(c_ref, w_ref, ga_ref, gb_ref, m_ref, v_ref, g_ref, d_ref, m2_ref, v2_ref):
        in_my_half = (pl.program_id(0) // per_half) == c_ref[0]
        gv = jnp.where(in_my_half, ga_ref[...], gb_ref[...])
        g_ref[...] = gv
        m2 = ADAM_B1 * m_ref[...] + (1.0 - ADAM_B1) * gv
        v2 = ADAM_B2 * v_ref[...] + (1.0 - ADAM_B2) * jnp.square(gv)
        m_hat = m2 / (1.0 - ADAM_B1 ** ADAM_STEP)
        v_hat = v2 / (1.0 - ADAM_B2 ** ADAM_STEP)
        d_ref[...] = -ADAM_LR * (m_hat / (jnp.sqrt(v_hat) + ADAM_EPS) + ADAM_WD * w_ref[...])
        m2_ref[...] = m2
        v2_ref[...] = v2

    full = pl.BlockSpec((tr, cols), lambda i, c: (i, 0))
    part = pl.BlockSpec((tr, cols), lambda i, c: (i % per_half, 0))
    shp = SDS((rows, cols), f32)
    grid_spec = pltpu.PrefetchScalarGridSpec(num_scalar_prefetch=1, grid=(rows // tr,),
                                             in_specs=[full, part, part, full, full], out_specs=(full,) * 4)
    return _hbm_call(body, name=name, out_shape=(shp,) * 4, grid_spec=grid_spec)(cvec, w, g_mine, g_other, m, v)


def _adam_math(w, g, m, v):
    m2 = ADAM_B1 * m + (1.0 - ADAM_B1) * g
    v2 = ADAM_B2 * v + (1.0 - ADAM_B2) * jnp.square(g)
    m_hat = m2 / (1.0 - ADAM_B1 ** ADAM_STEP)
    v_hat = v2 / (1.0 - ADAM_B2 ** ADAM_STEP)
    return -ADAM_LR * (m_hat / (jnp.sqrt(v_hat) + ADAM_EPS) + ADAM_WD * w), m2, v2


def _adamw_small(red, g_c_ctx, jvec, ws, ms, vs):
    n = len(ws)

    def body(*refs):
        red_ref, gc_ref, j_ref = refs[:3]
        w_refs, m_refs, v_refs = refs[3:3 + n], refs[3 + n:3 + 2 * n], refs[3 + 2 * n:3 + 3 * n]
        outs = refs[3 + 3 * n:]
        g_out, d_out, m_out, v_out = outs[:n], outs[n:2 * n], outs[2 * n:3 * n], outs[3 * n:]
        chip = j_ref[0]
        lanes = lambda i: (slice(None), slice(128 * i, 128 * (i + 1)))
        row = lambda r0, i: (lambda: red_ref[r0 + i:r0 + i + 1, :])
        whole = (slice(None), slice(None))
        chunks = [
            [((slice(None),), lambda: gc_ref[...].reshape(D))],
            [(lanes(i), row(RS_B_ADA, i)) for i in range(3 * D // 128)],
            [(lanes(i), row(RS_NORM_G, i)) for i in range(D // 128)],
            [(whole, lambda: red_ref[RS_GQ:RS_GQ + 1, 0:DH])],
            [(whole, lambda: red_ref[RS_GK:RS_GK + 1, 0:DH])],
            [((dr,), (lambda dr=dr: red_ref[pl.ds(RS_RPB + dr, H, stride=N_DR), 0:N_DC])) for dr in range(N_DR)],
            [((r,), (lambda r=r: red_ref[pl.ds(RS_CONV_W + 4 * r + chip, 1), :])) for r in range(3)],
            [(lanes(i), row(RS_CONV_B, i)) for i in range(DC // 128)],
        ]
        for a in range(n):
            for idx, grad in chunks[a]:
                g = grad()
                d, m2, v2 = _adam_math(w_refs[a][idx], g, m_refs[a][idx], v_refs[a][idx])
                g_out[a][idx] = g
                d_out[a][idx] = d
                m_out[a][idx] = m2
                v_out[a][idx] = v2

    shapes = [SDS(w.shape, f32) for w in ws]
    res = _pallas_call(body, name="adamw_small", out_shape=shapes * 4,
                       in_specs=[VMEM_SPEC, VMEM_SPEC, SMEM_SPEC] + [VMEM_SPEC] * (3 * n),
                       out_specs=[VMEM_SPEC] * (4 * n))(red, g_c_ctx, jvec, *ws, *ms, *vs)
    return [list(res[k * n:(k + 1) * n]) for k in range(4)]


def _rows128(a):
    return a.reshape(-1, 128)


def kernel(x, c, ctx, c_ctx, w_ada, b_ada, norm_g, w_in, q_norm_g, k_norm_g, rpb, conv_w, conv_b, w_out, loss_target, m_c_ctx, m_w_ada, m_b_ada, m_norm_g, m_w_in, m_q_norm_g, m_k_norm_g, m_rpb, m_conv_w, m_conv_b, m_w_out, v_c_ctx, v_w_ada, v_b_ada, v_norm_g, v_w_in, v_q_norm_g, v_k_norm_g, v_rpb, v_conv_w, v_conv_b, v_w_out):
    xi, yi, ci = lax.axis_index("x"), lax.axis_index("y"), lax.axis_index("c")
    dev = 4 * xi + 2 * yi + ci
    chip = 2 * xi + yi
    cvec = jnp.reshape(ci, (1,)).astype(i32)
    jvec = jnp.reshape(chip, (1,)).astype(i32)
    w_ada_s = w_ada[0]
    ncol = w_ada_s.shape[1]

    gc = _split_start([_to_slot(c.reshape(8, 128), 8, dev)], [], 7, _gather8_copies, "gather_c_start")
    btb = _bias_tiles(rpb[0], gc[3][0:1, 0:1])
    (c8,), _ = _split_wait(gc[0], gc[1], [gc[2]], [], btb, _gather8_copies, "gather_c_wait")
    cc = jnp.concatenate([c8.reshape(8, D), c_ctx.reshape(1, D), jnp.zeros((7, D), f32)], axis=0)
    m_shard, sc16 = _adaln_shard(cc, w_ada_s)

    conv_w_pad = jnp.pad(conv_w[0], ((0, 5), (0, 0)))
    gm = _split_start([_to_slot(m_shard, 4, chip), _to_slot(conv_w_pad, 4, chip)], [], 6, _gather4_copies,
                      "gather_mod_start")
    w4c = _cast_to_slot(w_in[0], jvec, "cast_w_in")
    (m4, cw4), _ = _split_wait(gm[0], gm[1], [gm[2], gm[3]], [], w4c, _gather4_copies, "gather_mod_wait")

    all_k = [(0, 0, 0), (0, 0, 1), (0, 0, 2)]
    wo4c = _cast_to_slot(w_out[0], jvec, "cast_w_out")
    sem_a, rem_a, w4s, token = _halves_start([w4c], m4, all_k[0:2], "weights_near_start")
    m_full = jnp.transpose(m4, (1, 0, 2)).reshape(16, 4 * ncol) + token[0:1, 0:1]
    mrow = lax.dynamic_slice(m_full, (dev, 0), (1, 3 * D))
    mrow_c = m_full[8:9]
    conv_w_full = jnp.transpose(cw4[:, 0:3, :], (1, 0, 2)).reshape(3, DC)
    waves = {}

    def near(after):
        (w4w,) = _halves_wait(sem_a, rem_a, [w4s], after, all_k[0:2], "weights_near_wait")
        w4f = _halves_forward([w4w], [0, 1], "weights_near_forward")[0]
        sem_b, rem_b, w4b, _ = _halves_start([w4f], after, all_k[2:3], "weights_far_start")
        waves["far"] = (sem_b, rem_b)
        return w4b

    def far(w4, after):
        (w4w,) = _halves_wait(*waves["far"], [w4], after, all_k[2:3], "weights_far_wait")
        w4f = _halves_forward([w4w], [2], "weights_far_forward")[0]
        sem_c, rem_c, wo4s, started = _halves_start([wo4c], w4f, all_k, "weights_out_start")
        waves["out"] = (sem_c, rem_c, wo4s)
        return w4f, started

    def w_out_gathered(after):
        sem_c, rem_c, wo4s = waves["out"]
        (wow,) = _halves_wait(sem_c, rem_c, [wo4s], after, all_k, "weights_out_wait")
        return _halves_forward([wow], [0, 1, 2], "weights_out_forward")[0].reshape(D, D)

    weights = dict(own=w_in[0], jvec=jvec, near=near, far=far, out=w_out_gathered)

    exchange = _exchange_copies
    pending = {}

    def on_g_w_out(g_w_out):
        out = _split_start([g_w_out], [SDS((4, D // 8, D), f32)], 4, exchange, "grad_out_pair_start")
        pending["ex_out"] = out
        return out[4]

    def after_conv(dp8):
        ssem_o, rsem_o, g_o, land_o, _ = pending["ex_out"]
        (g_o,), (ex_o,) = _split_wait(ssem_o, rsem_o, [g_o], [land_o], dp8, exchange, "grad_out_pair_wait")
        to32, tob = _pair_sum_w_out(g_o, ex_o, cvec)
        out = _split_start([tob], [SDS((3, D // 8, D), bf16)], 3, _scatter_copies, "grad_out_chip_start")
        pending["sc_out"] = (out, to32)
        return out[4]

    def on_g_w_in(g_w_in):
        out = _split_start([g_w_in], [SDS((4, D // 2, D), f32)], 4, exchange, "grad_pair_start")
        pending["ex"] = (out[0], out[1], [out[2]], [out[3]])
        return out[4]

    def on_g_rpb(g_rpb):
        ex_ssem, ex_rsem, ex_srcs, ex_lands = pending["ex"]
        ex_g, ex = _split_wait(ex_ssem, ex_rsem, ex_srcs, ex_lands, g_rpb, exchange, "grad_pair_wait")
        t32, tb = _pair_sum_w_in(ex_g[0], ex[0], cvec)
        out = _split_start([tb], [SDS((3, D // 2, D), bf16)], 3, _scatter_copies, "grad_chip_start")
        pending["sc_in"] = (out, t32)
        return out[4]

    r = _local_step(x[0], ctx[0], loss_target[0], mrow, mrow_c, b_ada, norm_g, weights, q_norm_g, k_norm_g,
                    btb, conv_w_full, conv_b,
                    dict(g_w_out=on_g_w_out, after_conv=after_conv, g_w_in=on_g_w_in, g_rpb=on_g_rpb))
    dm = jnp.concatenate([r["dshift"], r["dscale"], r["dgate"]], axis=1)
    dmc = jnp.concatenate([r["dshift_c"], r["dscale_c"], jnp.zeros((1, D), f32)], axis=1)
    pack_parts = [_rows128(dm), _rows128(dmc), _rows128(r["dng"]), _rows128(r["dng_c"]), _rows128(r["g_gq"]),
                  _rows128(r["g_gk"]), _rows128(r["g_gk_c"]), r["g_rpb"].reshape(H * N_DR, 128),
                  _rows128(r["g_conv_b"]), _rows128(r["g_conv_w"][0:3]), jnp.pad(r["loss_sum"], ((0, 0), (0, 127)))]
    pack = jnp.concatenate([jnp.pad(p, ((0, -p.shape[0] % 8), (0, 0))) for p in pack_parts], axis=0)
    assert pack.shape[0] == PK_ROWS
    gs = _split_start([_to_slot(pack, 8, dev)], [], 7, _gather8_copies, "gather_small_start")
    sc_o, to32 = pending["sc_out"]
    _, (ro2,) = _split_wait(sc_o[0], sc_o[1], [sc_o[2]], [sc_o[3]], gs[3], _scatter_copies, "grad_out_chip_wait")
    u_out = _chip_sum(to32, ro2, jvec, "chip_sum_w_out")
    (o_out,) = _sibling_send([u_out], "grad_out_pair_send")
    g_w_out_s, d_w_out, nm_w_out, nv_w_out = _adamw_halves(w_out[0], u_out, o_out, m_w_out[0], v_w_out[0], cvec,
                                                           "adamw_w_out")
    (gathered,), _ = _split_wait(gs[0], gs[1], [gs[2]], [], nm_w_out, _gather8_copies, "gather_small_wait")
    red, dm16 = _small_reduce(gathered)
    loss = red[RS_LOSS, 0] * (0.5 / D)

    g_w_ada_s, cpart = _w_ada_grad(sc16, dm16, w_ada_s, jvec)
    d_w_ada, nm_w_ada, nv_w_ada = _adamw(w_ada_s, g_w_ada_s, m_w_ada[0], v_w_ada[0], "adamw_w_ada")

    (cparts4,) = _chip_gather([cpart], "gather_c_ctx_parts", after=nm_w_ada)
    g_c_ctx = _c_ctx_grad(cparts4, c_ctx)

    sc_in, t32 = pending["sc_in"]
    _, (r2,) = _split_wait(sc_in[0], sc_in[1], [sc_in[2]], [sc_in[3]], g_c_ctx, _scatter_copies, "grad_chip_wait")
    u_in = _chip_sum(t32, r2, jvec, "chip_sum_w_in")

    t_rpb = lambda a: jnp.transpose(a, (0, 2, 1, 3)).reshape(N_DR, H, N_DC)
    t_cw = lambda a: jnp.transpose(a, (1, 0, 2))
    small = _adamw_small(
        red, g_c_ctx, jvec,
        [c_ctx, b_ada, norm_g, q_norm_g, k_norm_g, t_rpb(rpb), t_cw(conv_w), conv_b],
        [m_c_ctx, m_b_ada, m_norm_g, m_q_norm_g, m_k_norm_g, t_rpb(m_rpb), t_cw(m_conv_w), m_conv_b],
        [v_c_ctx, v_b_ada, v_norm_g, v_q_norm_g, v_k_norm_g, t_rpb(v_rpb), t_cw(v_conv_w), v_conv_b])
    for kind in small:
        kind[5] = jnp.transpose(kind[5].reshape(1, N_DR, H, N_DC), (0, 2, 1, 3))
        kind[6] = jnp.transpose(kind[6], (1, 0, 2))

    (o_in,) = _sibling_send([u_in], "grad_pair_send")
    g_w_in_s, d_w_in, nm_w_in, nv_w_in = _adamw_halves(w_in[0], u_in, o_in, m_w_in[0], v_w_in[0], cvec, "adamw_w_in")

    def ordered(kind, big_w_ada, big_w_in, big_w_out):
        s_c_ctx, s_b_ada, s_norm_g, s_q, s_k, s_rpb, s_conv_w, s_conv_b = small[kind]
        return [s_c_ctx, big_w_ada[None], s_b_ada, s_norm_g, big_w_in[None], s_q, s_k, s_rpb, s_conv_w,
                s_conv_b, big_w_out[None]]

    grads = ordered(0, g_w_ada_s, g_w_in_s, g_w_out_s)
    deltas = ordered(1, d_w_ada, d_w_in, d_w_out)
    new_m = ordered(2, nm_w_ada, nm_w_in, nm_w_out)
    new_v = ordered(3, nv_w_ada, nv_w_in, nv_w_out)
    return (loss, r["grad_x"][None], *grads, *deltas, *new_m, *new_v)
```

```python
import functools

import jax
import jax.numpy as jnp
import numpy as np
from jax import lax
from jax.experimental import pallas as pl
from jax.experimental.pallas import tpu as pltpu

f32, bf16, i32 = jnp.float32, jnp.bfloat16, jnp.int32
MESH = pl.DeviceIdType.MESH
HIGHEST = lax.Precision.HIGHEST

D = 1024
S = 2048
L = 256
GW = 64
ROWS = S // GW
H = 8
DH = 64
DA = H * DH
DC = 512
WIN_H, WIN_W = 8, 16
N_DR, N_DC = 2 * WIN_H - 1, 2 * WIN_W - 1
RMS_EPS = 1e-6
ROPE_THETA = 10000.0
QK_SCALE = DH ** -0.5
NEG = -1e30

QB = 128
NQB = S // QB
KR = 9
KB = KR * GW
TILE_GEOM = ((0, 0), (2, 0), (4, 0), (28, 23), (30, 23))
NT = len(TILE_GEOM)

ADAM_LR, ADAM_B1, ADAM_B2, ADAM_EPS, ADAM_WD, ADAM_STEP = 0.001, 0.9, 0.999, 1e-08, 0.01, 10

VMEM_SPEC = pl.BlockSpec(memory_space=pltpu.VMEM)
ANY_SPEC = pl.BlockSpec(memory_space=pl.ANY)
SMEM_SPEC = pl.BlockSpec(memory_space=pltpu.SMEM)
SDS = jax.ShapeDtypeStruct


_pallas_call = pl.pallas_call


def _hbm_call(body, *, out_shape, in_specs=None, out_specs=None, grid_spec=None, **kw):
    n_pre = 0
    if grid_spec is not None:
        ispecs, ospecs, n_pre = grid_spec.in_specs, grid_spec.out_specs, grid_spec.num_scalar_prefetch
        kw["grid_spec"] = grid_spec
    else:
        ispecs, ospecs = in_specs, out_specs
        kw.update(in_specs=in_specs, out_specs=out_specs)

    def blocked(spec):
        return isinstance(spec, pl.BlockSpec) and spec.block_shape is not None

    single = not isinstance(out_shape, (tuple, list))
    shapes = [out_shape] if single else list(out_shape)
    ospec_list = list(ospecs) if isinstance(ospecs, (tuple, list)) else [ospecs]
    shapes = [pltpu.HBM(s.shape, s.dtype) if blocked(sp) else s for s, sp in zip(shapes, ospec_list)]
    call = _pallas_call(body, out_shape=shapes[0] if single else tuple(shapes), **kw)

    def run(*args):
        arrays = [pltpu.with_memory_space_constraint(a, pltpu.HBM) if blocked(sp) else a
                  for a, sp in zip(args[n_pre:], ispecs)]
        return call(*args[:n_pre], *arrays)

    return run


def _cp(vmem_mb=None, **kw):
    if vmem_mb is not None:
        kw["vmem_limit_bytes"] = vmem_mb << 20
    return pltpu.CompilerParams(**kw)


def _silu(z):
    return z * jax.nn.sigmoid(z)


def _dsilu(z):
    sg = jax.nn.sigmoid(z)
    return sg * (1.0 + z * (1.0 - sg))


def _row_start(i):
    return min(max(i - WIN_H // 2, 0), ROWS - WIN_H)


def _my_pos():
    return lax.axis_index("x"), lax.axis_index("y"), lax.axis_index("c")


def _flip(v, bit):
    return 1 - v if bit else v


def _chip_gather(smalls, name, after=None):
    ns = len(smalls)

    def body(*refs):
        s_in, s_out = refs[:ns], refs[ns + 1:2 * ns + 1]
        ssem, rsem, lsem = refs[2 * ns + 1:]
        x, y, c = _my_pos()
        j = 2 * x + y
        chips = _peer_chips(x, y, c)
        local = [pltpu.make_async_copy(s_in[a], s_out[a].at[j], lsem.at[a]) for a in range(ns)]
        for cp in local:
            cp.start()
        sends = []
        for a in range(ns):
            for k in range(3):
                cp = pltpu.make_async_remote_copy(src_ref=s_in[a], dst_ref=s_out[a].at[j], send_sem=ssem.at[3 * a + k],
                                                  recv_sem=rsem.at[3 * a + k], device_id=chips[k][0], device_id_type=MESH)
                cp.start()
                sends.append(cp)
        for a in range(ns):
            for k in range(3):
                pltpu.make_async_remote_copy(src_ref=s_in[a], dst_ref=s_out[a].at[chips[k][1]], send_sem=ssem.at[3 * a + k],
                                             recv_sem=rsem.at[3 * a + k], device_id=chips[k][0],
                                             device_id_type=MESH).wait_recv()
        for cp in sends:
            cp.wait_send()
        for cp in local:
            cp.wait()

    return _hbm_call(
        body, name=name, out_shape=[SDS((4,) + a.shape, a.dtype) for a in smalls],
        in_specs=[VMEM_SPEC] * ns + [ANY_SPEC], out_specs=[VMEM_SPEC] * ns,
        scratch_shapes=[pltpu.SemaphoreType.DMA((3 * ns,)), pltpu.SemaphoreType.DMA((3 * ns,)),
                        pltpu.SemaphoreType.DMA((ns,))],
    )(*smalls, smalls[0] if after is None else after)


HBM_SPEC = pl.BlockSpec(memory_space=pltpu.HBM)
SEM_SPEC = pl.BlockSpec(memory_space=pltpu.SEMAPHORE)
DATAFLOW = pltpu.SideEffectType.DATAFLOW_SIDE_EFFECTING


def _peer_chips(x, y, c):
    out = []
    for k in range(1, 4):
        px, py = _flip(x, (k >> 1) & 1), _flip(y, k & 1)
        out.append(((px, py, c), 2 * px + py))
    return out


def _half_copies(srcs, dsts, ssem, rsem, which):
    x, y, c = _my_pos()
    j = 2 * x + y
    peers = _peer_chips(x, y, c)
    pairs = []
    for pos, group, k in which:
        half = srcs[pos].shape[1] // 2
        mine = pl.ds(pl.multiple_of(c * half, 8), half)
        dev, pj = peers[k]
        sem = 3 * group + k
        send = pltpu.make_async_remote_copy(src_ref=srcs[pos].at[j, mine], dst_ref=dsts[pos].at[j, mine],
                                            send_sem=ssem.at[sem], recv_sem=rsem.at[sem], device_id=dev,
                                            device_id_type=MESH)
        arrive = pltpu.make_async_remote_copy(src_ref=srcs[pos].at[j, mine], dst_ref=dsts[pos].at[pj, mine],
                                              send_sem=ssem.at[sem], recv_sem=rsem.at[sem], device_id=dev,
                                              device_id_type=MESH)
        pairs.append((send, arrive))
    return pairs


def _halves_start(bigs, after, order, name):
    nb = len(bigs)

    def body(*refs):
        b_in = refs[:nb]
        ssem, rsem = refs[nb + 1], refs[nb + 2]
        b_out = refs[nb + 3:2 * nb + 3]
        token = refs[2 * nb + 3]
        for send, _ in _half_copies(b_in, b_out, ssem, rsem, order):
            send.start()
        token[...] = jnp.zeros_like(token)

    out_shape = (pltpu.SemaphoreType.DMA((3 * nb,)), pltpu.SemaphoreType.DMA((3 * nb,)),
                 *[pltpu.HBM(b.shape, b.dtype) for b in bigs], SDS((8, 128), f32))
    return _hbm_call(
        body, name=name, out_shape=out_shape, in_specs=[HBM_SPEC] * nb + [ANY_SPEC],
        out_specs=(SEM_SPEC, SEM_SPEC, *[HBM_SPEC] * nb, VMEM_SPEC),
        input_output_aliases={a: 2 + a for a in range(nb)}, compiler_params=_cp(has_side_effects=DATAFLOW),
    )(*[pltpu.with_memory_space_constraint(b, pltpu.HBM) for b in bigs], after)


def _halves_wait(ssem, rsem, bigs, after, which, name):
    nb = len(bigs)

    def body(*refs):
        b_in = refs[:nb]
        ssem_ref, rsem_ref = refs[nb], refs[nb + 1]
        for send, arrive in _half_copies(b_in, b_in, ssem_ref, rsem_ref, which):
            send.wait_send()
            arrive.wait_recv()

    return _hbm_call(
        body, name=name, out_shape=tuple(pltpu.HBM(b.shape, b.dtype) for b in bigs),
        in_specs=[HBM_SPEC] * nb + [SEM_SPEC, SEM_SPEC, ANY_SPEC], out_specs=tuple([HBM_SPEC] * nb),
        input_output_aliases={a: a for a in range(nb)}, compiler_params=_cp(has_side_effects=DATAFLOW),
    )(*bigs, ssem, rsem, after)


def _halves_forward(bigs, relations, name):
    nb, nr = len(bigs), len(relations)

    def body(*refs):
        b_in, b_out = refs[:nb], refs[nb:2 * nb]
        ssem, rsem = refs[2 * nb:]
        x, y, c = _my_pos()
        sib = (x, y, 1 - c)
        peers = _peer_chips(x, y, c)
        sends = []
        for a in range(nb):
            half = b_in[a].shape[1] // 2
            mine = pl.ds(pl.multiple_of(c * half, 8), half)
            for i, k in enumerate(relations):
                pj = peers[k][1]
                cp = pltpu.make_async_remote_copy(src_ref=b_in[a].at[pj, mine], dst_ref=b_out[a].at[pj, mine],
                                                  send_sem=ssem.at[nr * a + i], recv_sem=rsem.at[nr * a + i],
                                                  device_id=sib, device_id_type=MESH)
                cp.start()
                sends.append(cp)
        for a in range(nb):
            half = b_in[a].shape[1] // 2
            other = pl.ds(pl.multiple_of((1 - c) * half, 8), half)
            for i, k in enumerate(relations):
                pj = peers[k][1]
                pltpu.make_async_remote_copy(src_ref=b_in[a].at[pj, other], dst_ref=b_out[a].at[pj, other],
                                             send_sem=ssem.at[nr * a + i], recv_sem=rsem.at[nr * a + i],
                                             device_id=sib, device_id_type=MESH).wait_recv()
        for cp in sends:
            cp.wait_send()

    return _hbm_call(
        body, name=name, out_shape=[SDS(b.shape, b.dtype) for b in bigs], in_specs=[ANY_SPEC] * nb,
        out_specs=[ANY_SPEC] * nb, input_output_aliases={a: a for a in range(nb)},
        scratch_shapes=[pltpu.SemaphoreType.DMA((nr * nb,)), pltpu.SemaphoreType.DMA((nr * nb,))],
    )(*bigs)


def _cast_to_slot(w, jvec, name):
    rows, cols = w.shape
    tr = 256

    def body(j_ref, w_ref, o_ref):
        o_ref[...] = w_ref[...].astype(bf16)

    grid_spec = pltpu.PrefetchScalarGridSpec(
        num_scalar_prefetch=1, grid=(rows // tr,), in_specs=[pl.BlockSpec((tr, cols), lambda i, j: (i, 0))],
        out_specs=pl.BlockSpec((None, tr, cols), lambda i, j: (j[0], i, 0)))
    return _hbm_call(body, name=name, out_shape=SDS((4, rows, cols), bf16), grid_spec=grid_spec)(jvec, w)


def _exchange_copies(srcs, lands, ssem, rsem):
    x, y, c = _my_pos()
    cps = []
    for a in range(len(srcs)):
        stacked = len(srcs[a].shape) == 3
        rb = srcs[a].shape[1] if stacked else srcs[a].shape[0] // 4
        half = rb // 2
        for jb in range(4):
            if stacked:
                src = srcs[a].at[jb, pl.ds(pl.multiple_of((1 - c) * half, 8), half)]
            else:
                src = srcs[a].at[pl.ds(pl.multiple_of(jb * rb + (1 - c) * half, 8), half)]
            cps.append(pltpu.make_async_remote_copy(src_ref=src, dst_ref=lands[a].at[jb], send_sem=ssem.at[4 * a + jb],
                                                    recv_sem=rsem.at[4 * a + jb], device_id=(x, y, 1 - c),
                                                    device_id_type=MESH))
    return cps


def _scatter_copies(srcs, lands, ssem, rsem):
    x, y, c = _my_pos()
    cps = []
    for a in range(len(srcs)):
        for k, (dev, pj) in enumerate(_peer_chips(x, y, c)):
            cps.append(pltpu.make_async_remote_copy(src_ref=srcs[a].at[pj], dst_ref=lands[a].at[k],
                                                    send_sem=ssem.at[3 * a + k], recv_sem=rsem.at[3 * a + k],
                                                    device_id=dev, device_id_type=MESH))
    return cps


class _Copy:
    def __init__(self, src, dst, arrive, ssem, rsem, dev):
        make = lambda to: pltpu.make_async_remote_copy(src_ref=src, dst_ref=to, send_sem=ssem, recv_sem=rsem,
                                                       device_id=dev, device_id_type=MESH)
        send, arrival = make(dst), make(arrive)
        self.start, self.wait_send, self.wait_recv = send.start, send.wait_send, arrival.wait_recv


def _toward(x, y, along_x):
    return x + along_x * (1 - 2 * x), y + (1 - along_x) * (1 - 2 * y)


def _near_copies(srcs, dsts, ssem, rsem):
    x, y, c = _my_pos()
    px, py = _toward(x, y, c)
    j = 2 * x + y
    return [_Copy(srcs[0].at[j], dsts[0].at[j], dsts[0].at[2 * px + py], ssem.at[0], rsem.at[0], (px, py, c))]


def _pass_copy(srcs, dsts, ssem, rsem):
    x, y, c = _my_pos()
    px, py = _toward(x, y, c)
    qx, qy = _toward(x, y, 1 - c)
    got = 2 * px + py
    return [_Copy(srcs[0].at[got], dsts[0].at[got], dsts[0].at[2 * qx + qy], ssem.at[0], rsem.at[0], (x, y, 1 - c))]


def _relay_copy(srcs, dsts, ssem, rsem):
    x, y, c = _my_pos()
    px, py = _toward(x, y, c)
    qx, qy = _toward(x, y, 1 - c)
    half = srcs[0].shape[1] // 2
    mine = pl.ds(pl.multiple_of(c * half, 8), half)
    got, diag = 2 * px + py, 3 - (2 * x + y)
    return [_Copy(srcs[0].at[got, mine], dsts[0].at[got, mine], dsts[0].at[diag, mine], ssem.at[1], rsem.at[1],
                  (qx, qy, c))]


def _pass_relay_copies(srcs, dsts, ssem, rsem):
    return _pass_copy(srcs, dsts, ssem, rsem) + _relay_copy(srcs, dsts, ssem, rsem)


def _gather8_copies(srcs, dsts, ssem, rsem):
    x, y, c = _my_pos()
    me = 4 * x + 2 * y + c
    cps = []
    for a in range(len(srcs)):
        for k in range(1, 8):
            tgt = (_flip(x, (k >> 2) & 1), _flip(y, (k >> 1) & 1), _flip(c, k & 1))
            cps.append(_Copy(srcs[a].at[me], dsts[a].at[me], dsts[a].at[4 * tgt[0] + 2 * tgt[1] + tgt[2]],
                             ssem.at[7 * a + k - 1], rsem.at[7 * a + k - 1], tgt))
    return cps


def _gather4_copies(srcs, dsts, ssem, rsem):
    x, y, c = _my_pos()
    j = 2 * x + y
    cps = []
    for a in range(len(srcs)):
        for k, (dev, pj) in enumerate(_peer_chips(x, y, c)):
            cps.append(_Copy(srcs[a].at[j], dsts[a].at[j], dsts[a].at[pj], ssem.at[3 * a + k], rsem.at[3 * a + k], dev))
    return cps


def _to_slot(a, n, i):
    return lax.dynamic_update_slice(jnp.zeros((n,) + a.shape, a.dtype), a[None], (i,) + (0,) * a.ndim)


def _split_start(srcs, land_shapes, n_cp, make, name, after=None):
    ns, nl = len(srcs), len(land_shapes)
    n_in = ns + nl + (after is not None)

    def body(*refs):
        s_in = refs[:ns]
        ssem, rsem = refs[n_in], refs[n_in + 1]
        s_out = refs[n_in + 2:n_in + 2 + ns]
        l_out = refs[n_in + 2 + ns:n_in + 2 + ns + nl]
        token = refs[n_in + 2 + ns + nl]
        for cp in make(s_in, l_out if nl else s_out, ssem, rsem):
            cp.start()
        token[...] = jnp.zeros_like(token)

    lands = [pltpu.with_memory_space_constraint(lax.empty(sh.shape, sh.dtype), pltpu.HBM) for sh in land_shapes]
    out_shape = (pltpu.SemaphoreType.DMA((n_cp,)), pltpu.SemaphoreType.DMA((n_cp,)),
                 *[pltpu.HBM(b.shape, b.dtype) for b in srcs], *[pltpu.HBM(b.shape, b.dtype) for b in land_shapes],
                 SDS((8, 128), f32))
    return _hbm_call(
        body, name=name, out_shape=out_shape, in_specs=[HBM_SPEC] * (ns + nl) + [ANY_SPEC] * (after is not None),
        out_specs=(SEM_SPEC, SEM_SPEC, *[HBM_SPEC] * (ns + nl), VMEM_SPEC),
        input_output_aliases={i: 2 + i for i in range(ns + nl)}, compiler_params=_cp(has_side_effects=DATAFLOW),
    )(*[pltpu.with_memory_space_constraint(b, pltpu.HBM) for b in srcs], *lands, *([] if after is None else [after]))


def _split_wait(ssem, rsem, srcs, lands, after, make, name):
    ns, nl = len(srcs), len(lands)

    def body(*refs):
        s_in, l_in = refs[:ns], refs[ns:ns + nl]
        ssem_ref, rsem_ref = refs[ns + nl], refs[ns + nl + 1]
        for cp in make(s_in, l_in if nl else s_in, ssem_ref, rsem_ref):
            cp.wait_send()
            cp.wait_recv()

    outs = _hbm_call(
        body, name=name, out_shape=tuple(pltpu.HBM(b.shape, b.dtype) for b in (*srcs, *lands)),
        in_specs=[HBM_SPEC] * (ns + nl) + [SEM_SPEC, SEM_SPEC, ANY_SPEC], out_specs=tuple([HBM_SPEC] * (ns + nl)),
        input_output_aliases={i: i for i in range(ns + nl)}, compiler_params=_cp(has_side_effects=DATAFLOW),
    )(*srcs, *lands, ssem, rsem, after)
    return list(outs[:ns]), list(outs[ns:])


def _sibling_send(halves, name):
    n = len(halves)

    def body(*refs):
        ins, outs = refs[:n], refs[n:2 * n]
        ssem, rsem = refs[2 * n:]
        x, y, c = _my_pos()
        cps = []
        for a in range(n):
            cp = pltpu.make_async_remote_copy(src_ref=ins[a], dst_ref=outs[a], send_sem=ssem.at[a],
                                              recv_sem=rsem.at[a], device_id=(x, y, 1 - c), device_id_type=MESH)
            cp.start()
            cps.append(cp)
        for cp in cps:
            cp.wait_recv()
        for cp in cps:
            cp.wait_send()

    out_shape = [SDS(h.shape, h.dtype) for h in halves]
    return _hbm_call(
        body, name=name, out_shape=out_shape, in_specs=[ANY_SPEC] * n, out_specs=[ANY_SPEC] * n,
        scratch_shapes=[pltpu.SemaphoreType.DMA((n,)), pltpu.SemaphoreType.DMA((n,))],
    )(*halves)


def _adaln_shard(cc, w_ada_shard):
    def body(c_ref, w_ref, m_ref, sc_ref):
        sc = _silu(c_ref[...])
        sc_ref[...] = sc
        m_ref[...] = jnp.dot(sc, w_ref[...], precision=HIGHEST, preferred_element_type=f32)

    return _hbm_call(
        body, name="adaln_shard", out_shape=(SDS((16, w_ada_shard.shape[1]), f32), SDS((16, D), f32)),
        in_specs=[VMEM_SPEC, VMEM_SPEC], out_specs=(VMEM_SPEC, VMEM_SPEC), compiler_params=_cp(32),
    )(cc, w_ada_shard)


def _prenorm(xx, norm_g, mrow, b_ada, tm, name):
    n = xx.shape[0]

    def body(x_ref, g_ref, m_ref, b_ref, h_ref):
        x = x_ref[...]
        shift = m_ref[:, 0:D] + b_ref[:, 0:D]
        scale = m_ref[:, D:2 * D] + b_ref[:, D:2 * D]
        r = lax.rsqrt(jnp.mean(x * x, axis=-1, keepdims=True) + RMS_EPS)
        y = (x * r) * g_ref[...]
        h_ref[...] = (y * (1.0 + scale) + shift).astype(bf16)

    row = lambda i: (i, 0)
    fixed = lambda i: (0, 0)
    return _hbm_call(
        body, name=name, out_shape=SDS((n, D), bf16), grid=(n // tm,),
        in_specs=[pl.BlockSpec((tm, D), row), pl.BlockSpec((1, D), fixed), pl.BlockSpec((1, 3 * D), fixed),
                  pl.BlockSpec((1, 3 * D), fixed)],
        out_specs=pl.BlockSpec((tm, D), row),
    )(xx, norm_g, mrow, b_ada)


def _in_proj_own(h, w_own, jvec):
    tm = 512

    def body(j_ref, h_ref, w_ref, p_ref):
        p_ref[...] = jnp.dot(h_ref[...], w_ref[...].astype(bf16), preferred_element_type=f32)

    grid_spec = pltpu.PrefetchScalarGridSpec(
        num_scalar_prefetch=1, grid=(S // tm,),
        in_specs=[pl.BlockSpec((tm, D), lambda i, j: (i, 0)), pl.BlockSpec((D, D), lambda i, j: (0, 0))],
        out_specs=pl.BlockSpec((tm, D), lambda i, j: (i, j[0])))
    return _hbm_call(body, name="in_proj_own", out_shape=SDS((S, 4 * D), f32), grid_spec=grid_spec,
                     compiler_params=_cp(40))(jvec, h, w_own)


def _in_proj_block(h, w4, p, bvec, name, after=None):
    tm = 512

    def body(b_ref, h_ref, w_ref, p_in_ref, after_ref, p_ref):
        p_ref[...] = jnp.dot(h_ref[...], w_ref[...], preferred_element_type=f32)

    grid_spec = pltpu.PrefetchScalarGridSpec(
        num_scalar_prefetch=1, grid=(S // tm,),
        in_specs=[pl.BlockSpec((tm, D), lambda i, b: (i, 0)), pl.BlockSpec((None, D, D), lambda i, b: (b[0], 0, 0)),
                  ANY_SPEC, ANY_SPEC],
        out_specs=pl.BlockSpec((tm, D), lambda i, b: (i, b[0])))
    return _hbm_call(body, name=name, out_shape=SDS((S, 4 * D), f32), grid_spec=grid_spec,
                     input_output_aliases={3: 0})(bvec, h, w4, p, bvec if after is None else after)


def _ctx_proj(hc, w4):
    def body(h_ref, w0_ref, w1_ref, p_ref):
        hv = h_ref[...]
        p_ref[:, 0:DA] = jnp.dot(hv, w0_ref[:, DA:2 * DA], preferred_element_type=f32)
        p_ref[:, DA:2 * DA] = jnp.dot(hv, w1_ref[:, 0:DA], preferred_element_type=f32)

    return _hbm_call(
        body, name="ctx_proj", out_shape=SDS((L, 2 * DA), f32), grid=(1,),
        in_specs=[pl.BlockSpec((L, D), lambda i: (0, 0)), pl.BlockSpec((None, D, D), lambda i: (0, 0, 0)),
                  pl.BlockSpec((None, D, D), lambda i: (1, 0, 0))],
        out_specs=pl.BlockSpec((L, 2 * DA), lambda i: (0, 0)),
    )(hc, w4, w4)


def _head_ones():
    r = lax.broadcasted_iota(i32, (DA, DA), 0) // DH
    c = lax.broadcasted_iota(i32, (DA, DA), 1) // DH
    return (r == c).astype(bf16)


def _head_sum(v, ones_bd):
    hi = v.astype(bf16)
    lo = (v - hi.astype(f32)).astype(bf16)
    return jnp.dot(hi, ones_bd, preferred_element_type=f32) + jnp.dot(lo, ones_bd, preferred_element_type=f32)


def _swap16(v):
    lane = lax.broadcasted_iota(i32, v.shape, 1)
    return jnp.where((lane & 31) < 16, pltpu.roll(v, DA - 16, 1), pltpu.roll(v, 16, 1))


def _rope_block(ct_ref, rt_ref, tm):
    rows = [jnp.tile(rt_ref[8 * j:8 * j + 8, :], (GW // 8, 1)) for j in range(tm // GW)]
    return jnp.tile(ct_ref[...], (tm // GW, 1)) + jnp.concatenate(rows, axis=0)


def _rope_specs(tm):
    col = pl.BlockSpec((GW, DA), lambda i: (0, 0))
    row = pl.BlockSpec((8 * tm // GW, DA), lambda i: (i, 0))
    return [col, row, col, row]


def _qk_prep(p, gq, gk, rope):
    tm = 256

    def body(qk_ref, v_ref, gq_ref, gk_ref, cc_ref, cr_ref, sc_ref, sr_ref, qr_ref, qp_ref, kr_ref, vh_ref):
        ones_bd = _head_ones()
        cs, sn = _rope_block(cc_ref, cr_ref, tm), _rope_block(sc_ref, sr_ref, tm)
        q = qk_ref[:, 0:DA]
        k = qk_ref[:, DA:2 * DA]
        yq = (q * lax.rsqrt(_head_sum(q * q, ones_bd) * (1.0 / DH) + RMS_EPS)) * gq_ref[...]
        yk = (k * lax.rsqrt(_head_sum(k * k, ones_bd) * (1.0 / DH) + RMS_EPS)) * gk_ref[...]
        qr = (yq * cs + _swap16(yq) * sn) * QK_SCALE
        qp = yq * QK_SCALE
        kr = yk * cs + _swap16(yk) * sn
        vv = v_ref[...]
        for hh in range(H):
            sl = slice(hh * DH, (hh + 1) * DH)
            qr_ref[hh] = qr[:, sl].astype(bf16)
            qp_ref[hh] = qp[:, sl].astype(bf16)
            kr_ref[hh] = kr[:, sl].astype(bf16)
            vh_ref[hh] = vv[:, sl].astype(bf16)

    hm = SDS((H, S, DH), bf16)
    hspec = pl.BlockSpec((H, tm, DH), lambda i: (0, i, 0))
    fixed = lambda i: (0, 0)
    return _hbm_call(
        body, name="qk_prep", out_shape=(hm, hm, hm, hm), grid=(S // tm,),
        in_specs=[pl.BlockSpec((tm, 2 * DA), lambda i: (i, 0)), pl.BlockSpec((tm, DA), lambda i: (i, 2)),
                  pl.BlockSpec((1, DA), fixed), pl.BlockSpec((1, DA), fixed)] + _rope_specs(tm),
        out_specs=(hspec, hspec, hspec, hspec),
    )(p, p, gq, gk, *rope)


def _ctx_prep(pc, gk):
    def body(p_ref, gk_ref, kc_ref, vc_ref):
        ones_bd = _head_ones()
        k = p_ref[:, 0:DA]
        yk = (k * lax.rsqrt(_head_sum(k * k, ones_bd) * (1.0 / DH) + RMS_EPS)) * gk_ref[...]
        vv = p_ref[:, DA:2 * DA]
        for hh in range(H):
            sl = slice(hh * DH, (hh + 1) * DH)
            kc_ref[hh] = yk[:, sl].astype(bf16)
            vc_ref[hh] = vv[:, sl].astype(bf16)

    hm = SDS((H, L, DH), bf16)
    return _hbm_call(
        body, name="ctx_prep", out_shape=(hm, hm), in_specs=[VMEM_SPEC, VMEM_SPEC], out_specs=(VMEM_SPEC, VMEM_SPEC),
    )(pc, gk)


def _tile_pieces():
    out = []
    for (i0, u0) in TILE_GEOM:
        rows = []
        for j in range(2):
            i = i0 + j
            rs = _row_start(i)
            rows.append([(u0 + u - i + WIN_H - 1) if rs <= u0 + u < rs + WIN_H else None for u in range(KR)])
        out.append(rows)
    return out


def _bias_prep(rpb_rev_pad):
    pieces = _tile_pieces()

    def body(r_ref, o_ref):
        rp = r_ref[...]
        xs = jnp.broadcast_to(rp[:, None, :], (N_DR, GW, 128)).reshape(N_DR * GW, 128)
        row = lax.broadcasted_iota(i32, xs.shape, 0)
        lane = lax.broadcasted_iota(i32, xs.shape, 1)
        for b in range(6):
            xs = jnp.where(((row >> b) & 1) == 1, pltpu.roll(xs, 1 << b, 1), xs)
        xs = pltpu.roll(xs, 128 - (WIN_W - 1), 1)
        k = row & (GW - 1)
        c0 = jnp.clip(lane - WIN_W // 2, 0, GW - WIN_W)
        xs = jnp.where((k >= c0) & (k < c0 + WIN_W), xs, NEG)
        neg = jnp.full((GW, GW), NEG, f32)
        for t in range(NT):
            for j in range(2):
                for u in range(KR):
                    dr = pieces[t][j][u]
                    piece = neg if dr is None else xs[dr * GW:(dr + 1) * GW, 0:GW]
                    o_ref[t, u * GW:(u + 1) * GW, j * GW:(j + 1) * GW] = piece

    return _hbm_call(
        body, name="bias_prep", out_shape=SDS((H, NT, KB, QB), f32), grid=(H,),
        in_specs=[pl.BlockSpec((None, N_DR, 128), lambda h: (h, 0, 0))],
        out_specs=pl.BlockSpec((None, NT, KB, QB), lambda h: (h, 0, 0, 0)),
    )(rpb_rev_pad)


def _bias_tiles(rpb2, token=None):
    rpb_pad = jnp.pad(rpb2[:, :, ::-1], ((0, 0), (0, 0), (0, 128 - N_DC)))
    return _bias_prep(rpb_pad if token is None else rpb_pad + token)


def _block_geom(b):
    qs = b * QB
    ks = min(max(2 * b - 4, 0), ROWS - KR) * GW
    t = b if b < 2 else (b - (NQB - NT) if b > NQB - 3 else 2)
    return qs, ks, t


def _tt(a, b):
    return lax.dot_general(a, b, (((1,), (1,)), ((), ())), preferred_element_type=f32)


def _tn(a, b):
    return lax.dot_general(a, b, (((0,), (0,)), ((), ())), preferred_element_type=f32)


def _softmax_t(s_lat, s_ctx):
    m = jnp.maximum(jnp.max(s_lat, axis=0, keepdims=True), jnp.max(s_ctx, axis=0, keepdims=True))
    e_lat = jnp.exp(s_lat - m)
    e_ctx = jnp.exp(s_ctx - m)
    inv = 1.0 / (jnp.sum(e_lat, axis=0, keepdims=True) + jnp.sum(e_ctx, axis=0, keepdims=True))
    return e_lat * inv, e_ctx * inv


def _staged(n_blocks, stages):
    held = [dict() for _ in stages]
    for step in range(n_blocks + len(stages) - 1):
        for s, fn in enumerate(stages):
            b = step - s
            if 0 <= b < n_blocks:
                held[s][b] = fn(b) if s == 0 else fn(b, held[s - 1].pop(b))


def _attn_fwd(qr, qp, kr, vh, kc, vc, btt):
    def body(qr_ref, qp_ref, kr_ref, v_ref, kc_ref, vc_ref, bt_ref, o_ref):
        kcv, vcv = kc_ref[...], vc_ref[...]

        def scores(b):
            qs, ks, t = _block_geom(b)
            return (_tt(kr_ref[ks:ks + KB, :], qr_ref[qs:qs + QB, :]) + bt_ref[t], _tt(kcv, qp_ref[qs:qs + QB, :]))

        def probs(b, sc):
            p_lat, p_ctx = _softmax_t(*sc)
            return p_lat.astype(bf16), p_ctx.astype(bf16)

        def values(b, p):
            qs, ks, _ = _block_geom(b)
            o_ref[qs:qs + QB, :] = _tn(p[0], v_ref[ks:ks + KB, :]) + _tn(p[1], vcv)

        _staged(NQB, (scores, probs, values))

    sq = pl.BlockSpec((None, S, DH), lambda h: (h, 0, 0))
    sc = pl.BlockSpec((None, L, DH), lambda h: (h, 0, 0))
    return _hbm_call(
        body, name="attn_fwd", out_shape=SDS((H, S, DH), f32), grid=(H,),
        in_specs=[sq, sq, sq, sq, sc, sc, pl.BlockSpec((None, NT, KB, QB), lambda h: (h, 0, 0, 0))],
        out_specs=sq, compiler_params=_cp(48),
    )(qr, qp, kr, vh, kc, vc, btt)


def _shift_rows(v, down):
    n = v.shape[0]
    row = lax.broadcasted_iota(i32, v.shape, 0)
    if down:
        return jnp.where(row == 0, 0.0, pltpu.roll(v, 1, 0))
    return jnp.where(row == n - 1, 0.0, pltpu.roll(v, n - 1, 0))


def _conv_specs():
    col = lambda off: pl.BlockSpec((S, 128), lambda i, off=off: (0, off + i))
    return [col(16), col(20), col(24), col(28), pl.BlockSpec((3, 128), lambda i: (0, i)),
            pl.BlockSpec((1, 128), lambda i: (0, i))]


def _conv_fwd(p, conv_w, conv_b):
    def body(u_ref, bg_ref, cg_ref, zc_ref, w_ref, b_ref, o_ref):
        cu = cg_ref[...] * u_ref[...]
        cv = b_ref[...] + _shift_rows(cu, True) * w_ref[0:1, :]
        cv = cv + cu * w_ref[1:2, :]
        cv = cv + _shift_rows(cu, False) * w_ref[2:3, :]
        o_ref[...] = ((bg_ref[...] * cv) * _silu(zc_ref[...])).astype(bf16)

    return _hbm_call(
        body, name="conv_fwd", out_shape=SDS((S, DC), bf16), grid=(DC // 128,),
        in_specs=_conv_specs(), out_specs=pl.BlockSpec((S, 128), lambda i: (0, i)), compiler_params=_cp(40),
    )(p, p, p, p, conv_w, conv_b)


DP_Q, DP_K, DP_V, DP_ZA, DP_U, DP_BG, DP_CG, DP_ZC = range(8)


def _out_proj_loss(o, p, conv_g, w_out, xx, tgt, mrow, b_ada):
    tm = 256

    def body(o_ref, za_ref, c_ref, w_ref, x_ref, t_ref, m_ref, b_ref,
             dy_ref, dconv_ref, dp_ref, do_ref, gwo_ref, dgate_ref, loss_ref):
        k = pl.program_id(0)

        @pl.when(k == 0)
        def _():
            gwo_ref[...] = jnp.zeros_like(gwo_ref)
            dgate_ref[...] = jnp.zeros_like(dgate_ref)
            loss_ref[0, 0] = 0.0

        gate = m_ref[:, 2 * D:3 * D] + b_ref[:, 2 * D:3 * D]
        za = za_ref[...]
        sz = _silu(za)
        om = _merge_heads(o_ref)
        av, cv = (om * sz).astype(bf16), c_ref[...]
        mo = jnp.dot(av, w_ref[0:DA, :], preferred_element_type=f32)
        mo = mo + jnp.dot(cv, w_ref[DA:DA + DC, :], preferred_element_type=f32)
        y = x_ref[...] + gate * mo
        diff = y - t_ref[...]
        loss_ref[0, 0] += jnp.sum(diff * diff)
        dy = diff * (1.0 / D)
        dy_ref[...] = dy
        dgate_ref[...] += jnp.sum(dy * mo, axis=0, keepdims=True)
        dmo = (dy * gate).astype(bf16)
        dmix = _tt(dmo, w_ref[...])
        dattn = dmix[:, 0:DA]
        dconv_ref[...] = dmix[:, DA:DA + DC]
        a = dattn * sz
        for hh in range(H):
            do_ref[hh] = a[:, hh * DH:(hh + 1) * DH].astype(bf16)
        dp_ref[...] = ((dattn * _dsilu(za)) * om).astype(bf16)
        gwo_ref[0:DA, :] += _tn(av, dmo)
        gwo_ref[DA:DA + DC, :] += _tn(cv, dmo)

    row = lambda i: (i, 0)
    fixed = lambda i: (0, 0)
    hspec = pl.BlockSpec((H, tm, DH), lambda i: (0, i, 0))
    return _hbm_call(
        body, name="out_proj_loss",
        out_shape=(SDS((S, D), f32), SDS((S, DC), f32), SDS((8, S, DA), bf16), SDS((H, S, DH), bf16),
                   SDS((D, D), f32), SDS((1, D), f32), SDS((1, 1), f32)),
        grid=(S // tm,),
        in_specs=[hspec, pl.BlockSpec((tm, DA), lambda i: (i, 3)), pl.BlockSpec((tm, DC), row),
                  pl.BlockSpec((D, D), fixed), pl.BlockSpec((tm, D), row), pl.BlockSpec((tm, D), row),
                  pl.BlockSpec((1, 3 * D), fixed), pl.BlockSpec((1, 3 * D), fixed)],
        out_specs=(pl.BlockSpec((tm, D), row), pl.BlockSpec((tm, DC), row),
                   pl.BlockSpec((None, tm, DA), lambda i: (DP_ZA, i, 0)), hspec, pl.BlockSpec((D, D), fixed),
                   pl.BlockSpec((1, D), fixed), SMEM_SPEC),
        compiler_params=_cp(56, dimension_semantics=("arbitrary",)),
    )(o, p, conv_g, w_out, xx, tgt, mrow, b_ada)


def _conv_bwd(dconv, p, conv_w, conv_b, dp8, after=None):
    def body(d_ref, u_ref, bg_ref, cg_ref, zc_ref, w_ref, b_ref, dp_in_ref, after_ref, dp_ref, gw_ref, gb_ref):
        du_ref, dbg_ref, dcg_ref, dzc_ref = dp_ref.at[0], dp_ref.at[1], dp_ref.at[2], dp_ref.at[3]
        dconv = d_ref[...]
        u, bg, cg, zc = u_ref[...], bg_ref[...], cg_ref[...], zc_ref[...]
        w0, w1, w2 = w_ref[0:1, :], w_ref[1:2, :], w_ref[2:3, :]
        cu = cg * u
        cu_m, cu_p = _shift_rows(cu, True), _shift_rows(cu, False)
        cv = b_ref[...] + cu_m * w0
        cv = cv + cu * w1
        cv = cv + cu_p * w2
        sz = _silu(zc)
        dbg_ref[...] = ((dconv * sz) * cv).astype(bf16)
        dzc_ref[...] = ((dconv * (bg * cv)) * _dsilu(zc)).astype(bf16)
        dcv = (dconv * sz) * bg
        gb_ref[...] = jnp.sum(dcv, axis=0, keepdims=True)
        gw_ref[0:1, :] = jnp.sum(dcv * cu_m, axis=0, keepdims=True)
        gw_ref[1:2, :] = jnp.sum(dcv * cu, axis=0, keepdims=True)
        gw_ref[2:3, :] = jnp.sum(dcv * cu_p, axis=0, keepdims=True)
        gw_ref[3:8, :] = jnp.zeros((5, 128), f32)
        dcu = _shift_rows(dcv, False) * w0 + dcv * w1 + _shift_rows(dcv, True) * w2
        dcg_ref[...] = (dcu * u).astype(bf16)
        du_ref[...] = (dcu * cg).astype(bf16)

    return _hbm_call(
        body, name="conv_bwd", out_shape=(SDS((8, S, DC), bf16), SDS((8, DC), f32), SDS((1, DC), f32)),
        grid=(DC // 128,),
        in_specs=[pl.BlockSpec((S, 128), lambda i: (0, i))] + _conv_specs() + [ANY_SPEC, ANY_SPEC],
        out_specs=(pl.BlockSpec((4, S, 128), lambda i: (DP_U // 4, 0, i)), pl.BlockSpec((8, 128), lambda i: (0, i)),
                   pl.BlockSpec((1, 128), lambda i: (0, i))),
        input_output_aliases={7: 0}, compiler_params=_cp(48),
    )(dconv, p, p, p, p, conv_w, conv_b, dp8, conv_b if after is None else after)


def _attn_bwd(qr, qp, kr, vh, kc, vc, btt, do, after=None):
    def body(qr_ref, qp_ref, kr_ref, v_ref, kc_ref, vc_ref, bt_ref, do_ref, after_ref,
             dqr_ref, dqp_ref, dkr_ref, dv_ref, dkc_ref, dvc_ref, dbt_ref):
        kcv, vcv = kc_ref[...], vc_ref[...]
        dkr_ref[...] = jnp.zeros_like(dkr_ref)
        dv_ref[...] = jnp.zeros_like(dv_ref)
        dbt_ref[...] = jnp.zeros_like(dbt_ref)
        ctx_acc = {}

        def products(b):
            qs, ks, t = _block_geom(b)
            dob = do_ref[qs:qs + QB, :]
            s_lat = _tt(kr_ref[ks:ks + KB, :], qr_ref[qs:qs + QB, :]) + bt_ref[t]
            s_ctx = _tt(kcv, qp_ref[qs:qs + QB, :])
            return s_lat, s_ctx, _tt(v_ref[ks:ks + KB, :], dob), _tt(vcv, dob)

        def score_grads(b, x):
            s_lat, s_ctx, dp_lat, dp_ctx = x
            p_lat, p_ctx = _softmax_t(s_lat, s_ctx)
            delta = jnp.sum(p_lat * dp_lat, axis=0, keepdims=True) + jnp.sum(p_ctx * dp_ctx, axis=0, keepdims=True)
            ds_lat = p_lat * (dp_lat - delta)
            ds_ctx = p_ctx * (dp_ctx - delta)
            return ds_lat, ds_lat.astype(bf16), ds_ctx.astype(bf16), p_lat.astype(bf16), p_ctx.astype(bf16)

        def operand_grads(b, y):
            qs, ks, t = _block_geom(b)
            ds_lat, dsb_lat, dsb_ctx, pb_lat, pb_ctx = y
            qrb, qpb, dob = qr_ref[qs:qs + QB, :], qp_ref[qs:qs + QB, :], do_ref[qs:qs + QB, :]
            dbt_ref[t] += ds_lat
            dqr_ref[qs:qs + QB, :] = _tn(dsb_lat, kr_ref[ks:ks + KB, :])
            dqp_ref[qs:qs + QB, :] = _tn(dsb_ctx, kcv)
            dkr_ref[ks:ks + KB, :] += jnp.dot(dsb_lat, qrb, preferred_element_type=f32)
            dv_ref[ks:ks + KB, :] += jnp.dot(pb_lat, dob, preferred_element_type=f32)
            dkc = jnp.dot(dsb_ctx, qpb, preferred_element_type=f32)
            dvc = jnp.dot(pb_ctx, dob, preferred_element_type=f32)
            ctx_acc["k"] = dkc if b == 0 else ctx_acc["k"] + dkc
            ctx_acc["v"] = dvc if b == 0 else ctx_acc["v"] + dvc

        _staged(NQB, (products, score_grads, operand_grads))
        dkc_ref[...] = ctx_acc["k"]
        dvc_ref[...] = ctx_acc["v"]

    sq = pl.BlockSpec((None, S, DH), lambda h: (h, 0, 0))
    sc = pl.BlockSpec((None, L, DH), lambda h: (h, 0, 0))
    sb = pl.BlockSpec((None, NT, KB, QB), lambda h: (h, 0, 0, 0))
    big, ctxs = SDS((H, S, DH), f32), SDS((H, L, DH), f32)
    return _hbm_call(
        body, name="attn_bwd", out_shape=(big, big, big, big, ctxs, ctxs, SDS((H, NT, KB, QB), f32)), grid=(H,),
        in_specs=[sq, sq, sq, sq, sc, sc, sb, sq, ANY_SPEC], out_specs=(sq, sq, sq, sq, sc, sc, sb),
        compiler_params=_cp(56),
    )(qr, qp, kr, vh, kc, vc, btt, do, do if after is None else after)


def _bias_bwd(dbtt, after=None):
    pieces = _tile_pieces()

    def body(d_ref, after_ref, o_ref, scr):
        scr[...] = jnp.zeros_like(scr)
        acc = [None] * N_DR
        for t in range(NT):
            for j in range(2):
                for u in range(KR):
                    dr = pieces[t][j][u]
                    if dr is None:
                        continue
                    piece = d_ref[t, u * GW:(u + 1) * GW, j * GW:(j + 1) * GW]
                    acc[dr] = piece if acc[dr] is None else acc[dr] + piece
        for dr in range(N_DR):
            scr[dr * GW:(dr + 1) * GW, 0:GW] = acc[dr]
        xs = pltpu.roll(scr[...], WIN_W - 1, 1)
        row = lax.broadcasted_iota(i32, xs.shape, 0)
        for b in range(6):
            xs = jnp.where(((row >> b) & 1) == 1, pltpu.roll(xs, 128 - (1 << b), 1), xs)
        rev = jnp.sum(xs.reshape(N_DR, GW, 128), axis=1)
        a = lax.broadcasted_iota(i32, (128, 128), 0)
        b = lax.broadcasted_iota(i32, (128, 128), 1)
        flip = ((a + b == N_DC - 1) & (a < N_DC)).astype(f32)
        o_ref[...] = jnp.dot(rev, flip, precision=HIGHEST, preferred_element_type=f32)

    return _hbm_call(
        body, name="bias_bwd", out_shape=SDS((H, N_DR, 128), f32), grid=(H,),
        in_specs=[pl.BlockSpec((None, NT, KB, QB), lambda h: (h, 0, 0, 0)), ANY_SPEC],
        out_specs=pl.BlockSpec((None, N_DR, 128), lambda h: (h, 0, 0)),
        scratch_shapes=[pltpu.VMEM((N_DR * GW, 128), f32)],
    )(dbtt, dbtt if after is None else after)


def _merge_heads(ref):
    return jnp.concatenate([ref[hh] for hh in range(H)], axis=1)


def _head_norm_bwd(xraw, gain, dy, ones_bd):
    r = lax.rsqrt(_head_sum(xraw * xraw, ones_bd) * (1.0 / DH) + RMS_EPS)
    xh = xraw * r
    gdy = dy * gain
    dx = r * (gdy - xh * (_head_sum(xh * gdy, ones_bd) * (1.0 / DH)))
    return dx, jnp.sum(dy * xh, axis=0, keepdims=True)


def _qk_bwd(dqr, dqp, dkr, dvh, p, gq, gk, rope, dp8):
    tm = 256

    def body(dqr_ref, dqp_ref, dkr_ref, dv_ref, qk_ref, gq_ref, gk_ref, cc_ref, cr_ref, sc_ref, sr_ref, dp_in_ref,
             dp_ref, ggq_ref, ggk_ref):
        dq_ref, dk_ref, dvo_ref = dp_ref.at[DP_Q], dp_ref.at[DP_K], dp_ref.at[DP_V]

        @pl.when(pl.program_id(0) == 0)
        def _():
            ggq_ref[...] = jnp.zeros_like(ggq_ref)
            ggk_ref[...] = jnp.zeros_like(ggk_ref)

        ones_bd = _head_ones()
        cs, sn = _rope_block(cc_ref, cr_ref, tm), _rope_block(sc_ref, sr_ref, tm)
        a = _merge_heads(dqr_ref)
        dyq = ((a * cs - _swap16(a) * sn) + _merge_heads(dqp_ref)) * QK_SCALE
        bk = _merge_heads(dkr_ref)
        dyk = bk * cs - _swap16(bk) * sn
        dq, gq_part = _head_norm_bwd(qk_ref[:, 0:DA], gq_ref[...], dyq, ones_bd)
        dk, gk_part = _head_norm_bwd(qk_ref[:, DA:2 * DA], gk_ref[...], dyk, ones_bd)
        dq_ref[...] = dq.astype(bf16)
        dk_ref[...] = dk.astype(bf16)
        dvo_ref[...] = _merge_heads(dv_ref).astype(bf16)
        ggq_ref[...] += gq_part
        ggk_ref[...] += gk_part

    hspec = pl.BlockSpec((H, tm, DH), lambda i: (0, i, 0))
    fixed = pl.BlockSpec((1, DA), lambda i: (0, 0))
    return _hbm_call(
        body, name="qk_bwd", out_shape=(SDS((8, S, DA), bf16), SDS((1, DA), f32), SDS((1, DA), f32)), grid=(S // tm,),
        in_specs=[hspec, hspec, hspec, hspec, pl.BlockSpec((tm, 2 * DA), lambda i: (i, 0)), fixed, fixed]
        + _rope_specs(tm) + [ANY_SPEC],
        out_specs=(pl.BlockSpec((3, tm, DA), lambda i: (0, i, 0)), fixed, fixed), input_output_aliases={11: 0},
        compiler_params=_cp(40, dimension_semantics=("arbitrary",)),
    )(dqr, dqp, dkr, dvh, p, gq, gk, *rope, dp8)


def _ctx_bwd(dkc, dvc, pc, gk):
    def body(dkc_ref, dvc_ref, p_ref, gk_ref, dk_ref, dv_ref, ggk_ref):
        ones_bd = _head_ones()
        dk, gk_part = _head_norm_bwd(p_ref[:, 0:DA], gk_ref[...], _merge_heads(dkc_ref), ones_bd)
        dk_ref[...] = dk.astype(bf16)
        dv_ref[...] = _merge_heads(dvc_ref).astype(bf16)
        ggk_ref[...] = gk_part

    piece = SDS((L, DA), bf16)
    return _hbm_call(
        body, name="ctx_bwd", out_shape=(piece, piece, SDS((1, DA), f32)), in_specs=[VMEM_SPEC] * 4,
        out_specs=(VMEM_SPEC,) * 3,
    )(dkc, dvc, pc, gk)


def _grad_w_in(h, dp8, hc, dkc_raw, dvc_m):
    tm = 512

    def body(h_ref, p_ref, hc_ref, dk_ref, dv_ref, g_ref):
        j, k = pl.program_id(0), pl.program_id(1)

        @pl.when(k == 0)
        def _():
            g_ref[...] = jnp.zeros_like(g_ref)

        @pl.when((k == 0) & (j == 0))
        def _():
            g_ref[:, DA:2 * DA] = _tn(hc_ref[...], dk_ref[...])

        @pl.when((k == 0) & (j == 1))
        def _():
            g_ref[:, 0:DA] = _tn(hc_ref[...], dv_ref[...])

        hv = h_ref[...]
        g_ref[:, 0:DA] += _tn(hv, p_ref[0])
        g_ref[:, DA:2 * DA] += _tn(hv, p_ref[1])

    fixed = lambda j, k: (0, 0)
    return _hbm_call(
        body, name="grad_w_in", out_shape=SDS((4, D, D), f32), grid=(4, S // tm),
        in_specs=[pl.BlockSpec((tm, D), lambda j, k: (k, 0)), pl.BlockSpec((2, tm, DA), lambda j, k: (j, k, 0)),
                  pl.BlockSpec((L, D), fixed), pl.BlockSpec((L, DA), fixed), pl.BlockSpec((L, DA), fixed)],
        out_specs=pl.BlockSpec((None, D, D), lambda j, k: (j, 0, 0)),
        compiler_params=_cp(40, dimension_semantics=("arbitrary", "arbitrary")),
    )(h, dp8, hc, dkc_raw, dvc_m)


def _norm_mod_bwd(x, dh, g, scale):
    r = lax.rsqrt(jnp.mean(x * x, axis=-1, keepdims=True) + RMS_EPS)
    xh = x * r
    y = xh * g
    dshift = jnp.sum(dh, axis=0, keepdims=True)
    dscale = jnp.sum(dh * y, axis=0, keepdims=True)
    dyn = dh * (1.0 + scale)
    dg = jnp.sum(dyn * xh, axis=0, keepdims=True)
    gdy = dyn * g
    dx = r * (gdy - xh * jnp.mean(xh * gdy, axis=-1, keepdims=True))
    return dx, dshift, dscale, dg


def _dh_grad_x(dp8, w4, xx, dy, norm_g, mrow, b_ada):
    tm = 256

    def body(p_ref, w_ref, x_ref, dy_ref, g_ref, m_ref, b_ref, gx_ref, dsh_ref, dsc_ref, dg_ref):
        @pl.when(pl.program_id(0) == 0)
        def _():
            dsh_ref[...] = jnp.zeros_like(dsh_ref)
            dsc_ref[...] = jnp.zeros_like(dsc_ref)
            dg_ref[...] = jnp.zeros_like(dg_ref)

        dh = None
        for j in range(4):
            for half in range(2):
                term = _tt(p_ref[2 * j + half], w_ref[j, :, half * DA:(half + 1) * DA])
                dh = term if dh is None else dh + term
        scale = m_ref[:, D:2 * D] + b_ref[:, D:2 * D]
        dx, dshift, dscale, dg = _norm_mod_bwd(x_ref[...], dh, g_ref[...], scale)
        gx_ref[...] = dy_ref[...] + dx
        dsh_ref[...] += dshift
        dsc_ref[...] += dscale
        dg_ref[...] += dg

    row = lambda i: (i, 0)
    fixed = lambda i: (0, 0)
    vec = SDS((1, D), f32)
    return _hbm_call(
        body, name="dh_grad_x", out_shape=(SDS((S, D), f32), vec, vec, vec), grid=(S // tm,),
        in_specs=[pl.BlockSpec((8, tm, DA), lambda i: (0, i, 0)), pl.BlockSpec((4, D, D), lambda i: (0, 0, 0)),
                  pl.BlockSpec((tm, D), row), pl.BlockSpec((tm, D), row), pl.BlockSpec((1, D), fixed),
                  pl.BlockSpec((1, 3 * D), fixed), pl.BlockSpec((1, 3 * D), fixed)],
        out_specs=(pl.BlockSpec((tm, D), row), pl.BlockSpec((1, D), fixed), pl.BlockSpec((1, D), fixed),
                   pl.BlockSpec((1, D), fixed)),
        compiler_params=_cp(56, dimension_semantics=("arbitrary",)),
    )(dp8, w4, xx, dy, norm_g, mrow, b_ada)


def _dhc_sums(dkc_raw, dvc_m, w4, ctx2, norm_g, mrow_c, b_ada):
    def body(dk_ref, dv_ref, w0_ref, w1_ref, x_ref, g_ref, m_ref, b_ref, dsh_ref, dsc_ref, dg_ref):
        dh = _tt(dk_ref[...], w0_ref[:, DA:2 * DA]) + _tt(dv_ref[...], w1_ref[:, 0:DA])
        scale = m_ref[:, D:2 * D] + b_ref[:, D:2 * D]
        _, dshift, dscale, dg = _norm_mod_bwd(x_ref[...], dh, g_ref[...], scale)
        dsh_ref[...] = dshift
        dsc_ref[...] = dscale
        dg_ref[...] = dg

    fixed = lambda i: (0, 0)
    vec = SDS((1, D), f32)
    vspec = pl.BlockSpec((1, D), fixed)
    return _hbm_call(
        body, name="dhc_sums", out_shape=(vec, vec, vec), grid=(1,),
        in_specs=[pl.BlockSpec((L, DA), fixed), pl.BlockSpec((L, DA), fixed),
                  pl.BlockSpec((None, D, D), lambda i: (0, 0, 0)), pl.BlockSpec((None, D, D), lambda i: (1, 0, 0)),
                  pl.BlockSpec((L, D), fixed), vspec, pl.BlockSpec((1, 3 * D), fixed), pl.BlockSpec((1, 3 * D), fixed)],
        out_specs=(vspec, vspec, vspec), compiler_params=_cp(32),
    )(dkc_raw, dvc_m, w4, w4, ctx2, norm_g, mrow_c, b_ada)


def _rope_tables():
    nf = DH // 4
    inv = np.float32(ROPE_THETA) ** (-np.arange(nf, dtype=np.float32) / np.float32(nf))
    ang_c = np.arange(GW, dtype=np.float32)[:, None] * inv
    ang_r = np.arange(ROWS, dtype=np.float32)[:, None] * inv
    zc, zr = np.zeros((GW, 2 * nf), np.float32), np.zeros((ROWS, 2 * nf), np.float32)
    ct_cos = np.tile(np.concatenate([zc, np.cos(ang_c), np.cos(ang_c)], axis=1), (1, H))
    ct_sin = np.tile(np.concatenate([zc, -np.sin(ang_c), np.sin(ang_c)], axis=1), (1, H))
    rt_cos = np.tile(np.concatenate([np.cos(ang_r), np.cos(ang_r), zr], axis=1), (1, H))
    rt_sin = np.tile(np.concatenate([-np.sin(ang_r), np.sin(ang_r), zr], axis=1), (1, H))
    rep8 = lambda t: np.ascontiguousarray(np.broadcast_to(t[:, None, :], (ROWS, 8, DA))).reshape(ROWS * 8, DA)
    return tuple(jnp.asarray(t, f32) for t in (ct_cos, rep8(rt_cos), ct_sin, rep8(rt_sin)))


def _local_step(xx, ctx2, tgt, mrow, mrow_c, b_ada, norm_g, weights, q_norm_g, k_norm_g, btb, conv_w_full, conv_b,
                hooks=None):
    hooks = hooks or {}
    gq = jnp.tile(q_norm_g, (1, H))
    gk = jnp.tile(k_norm_g, (1, H))
    rope = _rope_tables()

    h = _prenorm(xx, norm_g, mrow, b_ada, 256, "prenorm_x")
    hc = _prenorm(ctx2, norm_g, mrow_c, b_ada, L, "prenorm_ctx")
    jv = weights["jvec"]
    p = _in_proj_own(h, weights["own"], jv)
    w4, started = weights["near"](p)
    p = _in_proj_block(h, w4, p, weights["first"], "in_proj_near", after=started)
    w4 = weights["near2"](w4, p)
    p = _in_proj_block(h, w4, p, weights["second"], "in_proj_near2")
    w4, started = weights["far"](w4, p)
    p = _in_proj_block(h, w4, p, jv ^ 3, "in_proj_far", after=started)
    pc = _ctx_proj(hc, w4)
    qr, qp, kr, vh = _qk_prep(p, gq, gk, rope)
    kc, vc = _ctx_prep(pc, gk)
    o = _attn_fwd(qr, qp, kr, vh, kc, vc, btb)
    conv_g = _conv_fwd(p, conv_w_full, conv_b)
    w_out_full = weights["out"](o)
    dy, dconv, dp8, do, g_w_out, dgate, loss_sum = _out_proj_loss(o, p, conv_g, w_out_full, xx, tgt, mrow, b_ada)
    started = hooks["g_w_out"](g_w_out) if "g_w_out" in hooks else None
    dp8, g_conv_w, g_conv_b = _conv_bwd(dconv, p, conv_w_full, conv_b, dp8, after=started)
    started = hooks["after_conv"](dp8) if "after_conv" in hooks else None
    dqr, dqp, dkr, dvh, dkc, dvc, dbtb = _attn_bwd(qr, qp, kr, vh, kc, vc, btb, do, after=started)
    dp8, g_gq, g_gk = _qk_bwd(dqr, dqp, dkr, dvh, p, gq, gk, rope, dp8)
    dkc_raw, dvc_m, g_gk_c = _ctx_bwd(dkc, dvc, pc, gk)
    g_w_in = _grad_w_in(h, dp8, hc, dkc_raw, dvc_m)
    behind = lambda token: b_ada if token is None else b_ada + token[0:1, 0:1]
    started = hooks["g_w_in"](g_w_in) if "g_w_in" in hooks else None
    dshift_c, dscale_c, dng_c = _dhc_sums(dkc_raw, dvc_m, w4, ctx2, norm_g, mrow_c, behind(started))
    g_rpb = _bias_bwd(dbtb, after=dshift_c)
    started = hooks["g_rpb"](g_rpb) if "g_rpb" in hooks else None
    grad_x, dshift, dscale, dng = _dh_grad_x(dp8, w4, xx, dy, norm_g, mrow, behind(started))
    return dict(loss_sum=loss_sum, grad_x=grad_x, g_w_in=g_w_in, g_w_out=g_w_out, g_conv_w=g_conv_w,
                g_conv_b=g_conv_b, g_rpb=g_rpb, g_gq=g_gq, g_gk=g_gk, g_gk_c=g_gk_c, dshift=dshift, dscale=dscale,
                dgate=dgate, dng=dng, dshift_c=dshift_c, dscale_c=dscale_c, dng_c=dng_c)


def _pair_sum_w_in(g, r, cvec):
    tr = 128

    def body(c_ref, g_ref, r_ref, t32_ref, tb_ref):
        t = g_ref[...] + r_ref[...]
        t32_ref[...] = t
        tb_ref[...] = t.astype(bf16)

    half = D // 2
    g_spec = pl.BlockSpec((4, tr, D), lambda i, c: (0, c[0] * (half // tr) + i, 0))
    o_spec = pl.BlockSpec((4, tr, D), lambda i, c: (0, i, 0))
    grid_spec = pltpu.PrefetchScalarGridSpec(num_scalar_prefetch=1, grid=(half // tr,), in_specs=[g_spec, o_spec],
                                             out_specs=(o_spec, o_spec))
    return _hbm_call(body, name="pair_sum_w_in", out_shape=(SDS((4, half, D), f32), SDS((4, half, D), bf16)),
                     grid_spec=grid_spec, compiler_params=_cp(40))(cvec, g, r)


def _pair_sum_w_out(g, r, cvec):
    hr = D // 8

    def body(c_ref, g0, g1, g2, g3, r_ref, t32_ref, tb_ref):
        for q, g_ref in enumerate((g0, g1, g2, g3)):
            t = g_ref[...] + r_ref[q]
            t32_ref[q] = t
            tb_ref[q] = t.astype(bf16)

    gspecs = [pl.BlockSpec((hr, D), lambda i, c, q=q: (2 * q + c[0], 0)) for q in range(4)]
    full = pl.BlockSpec((4, hr, D), lambda i, c: (0, 0, 0))
    grid_spec = pltpu.PrefetchScalarGridSpec(num_scalar_prefetch=1, grid=(1,), in_specs=gspecs + [full],
                                             out_specs=(full, full))
    return _hbm_call(body, name="pair_sum_w_out", out_shape=(SDS((4, hr, D), f32), SDS((4, hr, D), bf16)),
                          grid_spec=grid_spec)(cvec, g, g, g, g, r)


def _chip_sum(t32, r2, jvec, name):
    rows = t32.shape[1]
    tr = min(rows, 128)

    def body(j_ref, t_ref, r_ref, u_ref):
        u_ref[...] = ((t_ref[...] + r_ref[0].astype(f32)) + r_ref[1].astype(f32)) + r_ref[2].astype(f32)

    grid_spec = pltpu.PrefetchScalarGridSpec(
        num_scalar_prefetch=1, grid=(rows // tr,),
        in_specs=[pl.BlockSpec((None, tr, D), lambda i, j: (j[0], i, 0)), pl.BlockSpec((3, tr, D), lambda i, j: (0, i, 0))],
        out_specs=pl.BlockSpec((tr, D), lambda i, j: (i, 0)))
    return _hbm_call(body, name=name, out_shape=SDS((rows, D), f32), grid_spec=grid_spec)(jvec, t32, r2)


_PK = {}
_off = 0
for _name, _rows in (("dm", 24), ("dmc", 24), ("dng", 8), ("dng_c", 8), ("gq", 8), ("gk", 8), ("gk_c", 8),
                     ("rpb", H * N_DR), ("conv_b", 8), ("conv_w", 16), ("loss", 8)):
    _PK[_name] = (_off, _off + _rows)
    _off += _rows
PK_ROWS = _off
RS_B_ADA, RS_NORM_G, RS_GQ, RS_GK, RS_RPB, RS_CONV_B, RS_CONV_W, RS_DMC, RS_LOSS, RS_ROWS = (
    0, 24, 32, 40, 48, 168, 176, 192, 216, 224)


def _small_reduce(gathered):
    def body(g_ref, o_ref, dm_ref):
        a0 = _PK["dm"][0]
        dm_ref[...] = jnp.zeros_like(dm_ref)
        for b in range(8):
            for i in range(24):
                dm_ref[b:b + 1, 128 * i:128 * (i + 1)] = g_ref[b, a0 + i:a0 + i + 1, :]
        tot = g_ref[0]
        for b in range(1, 8):
            tot = tot + g_ref[b]

        def rows(name):
            a, z = _PK[name]
            return tot[a:z]

        o_ref[RS_B_ADA:RS_B_ADA + 24] = rows("dm") + rows("dmc")
        o_ref[RS_NORM_G:RS_NORM_G + 8] = rows("dng") + rows("dng_c")
        gq = jnp.broadcast_to(jnp.sum(rows("gq"), axis=0, keepdims=True), (8, 128))
        gk = jnp.broadcast_to(jnp.sum(rows("gk") + rows("gk_c"), axis=0, keepdims=True), (8, 128))
        o_ref[RS_GQ:RS_GQ + 8] = gq + pltpu.roll(gq, DH, 1)
        o_ref[RS_GK:RS_GK + 8] = gk + pltpu.roll(gk, DH, 1)
        o_ref[RS_RPB:RS_RPB + H * N_DR] = rows("rpb")
        o_ref[RS_CONV_B:RS_CONV_B + 8] = rows("conv_b")
        o_ref[RS_CONV_W:RS_CONV_W + 16] = rows("conv_w")
        dmc = rows("dmc")
        o_ref[RS_DMC:RS_DMC + 24] = dmc
        o_ref[RS_LOSS:RS_LOSS + 8] = rows("loss")
        for i in range(24):
            dm_ref[8:9, 128 * i:128 * (i + 1)] = dmc[i:i + 1]

    return _hbm_call(body, name="small_reduce", out_shape=(SDS((RS_ROWS, 128), f32), SDS((16, 3 * D), f32)),
                     in_specs=[VMEM_SPEC], out_specs=(VMEM_SPEC, VMEM_SPEC))(gathered)


def _w_ada_grad(sc16, dm16, w_ada_shard, jvec):
    ncol = w_ada_shard.shape[1]

    def body(j_ref, sc_ref, dm_ref, w_ref, g_ref, part_ref):
        dm = dm_ref[...]
        g_ref[...] = lax.dot_general(sc_ref[...], dm, (((0,), (0,)), ((), ())), precision=HIGHEST,
                                     preferred_element_type=f32)
        part_ref[...] = lax.dot_general(dm[8:16], w_ref[...], (((1,), (1,)), ((), ())), precision=HIGHEST,
                                        preferred_element_type=f32)

    fixed = lambda i, j: (0, 0)
    grid_spec = pltpu.PrefetchScalarGridSpec(
        num_scalar_prefetch=1, grid=(1,),
        in_specs=[pl.BlockSpec((16, D), fixed), pl.BlockSpec((16, ncol), lambda i, j: (0, j[0])),
                  pl.BlockSpec((D, ncol), fixed)],
        out_specs=(pl.BlockSpec((D, ncol), fixed), pl.BlockSpec((8, D), fixed)))
    return _pallas_call(body, name="w_ada_grad", out_shape=(SDS((D, ncol), f32), SDS((8, D), f32)),
                        grid_spec=grid_spec, compiler_params=_cp(40))(jvec, sc16, dm16, w_ada_shard)


def _c_ctx_grad(parts4, c_ctx):
    def body(p_ref, c_ref, o_ref):
        tot = ((p_ref[0] + p_ref[1]) + p_ref[2]) + p_ref[3]
        o_ref[...] = tot[0:1] * _dsilu(c_ref[...].reshape(1, D))

    return _pallas_call(body, name="c_ctx_grad", out_shape=SDS((1, D), f32), in_specs=[VMEM_SPEC, VMEM_SPEC],
                        out_specs=VMEM_SPEC)(parts4, c_ctx)


def _adamw(w, g, m, v, name):
    rows, cols = w.shape
    tr = 256 if rows % 256 == 0 else rows

    def body(w_ref, g_ref, m_ref, v_ref, d_ref, m2_ref, v2_ref):
        gv = g_ref[...]
        m2 = ADAM_B1 * m_ref[...] + (1.0 - ADAM_B1) * gv
        v2 = ADAM_B2 * v_ref[...] + (1.0 - ADAM_B2) * jnp.square(gv)
        m_hat = m2 / (1.0 - ADAM_B1 ** ADAM_STEP)
        v_hat = v2 / (1.0 - ADAM_B2 ** ADAM_STEP)
        d_ref[...] = -ADAM_LR * (m_hat / (jnp.sqrt(v_hat) + ADAM_EPS) + ADAM_WD * w_ref[...])
        m2_ref[...] = m2
        v2_ref[...] = v2

    spec = pl.BlockSpec((tr, cols), lambda i: (i, 0))
    shp = SDS((rows, cols), f32)
    return _hbm_call(body, name=name, out_shape=(shp, shp, shp), grid=(rows // tr,), in_specs=[spec] * 4,
                          out_specs=(spec, spec, spec))(w, g, m, v)


def _adamw_halves(w, g_mine, g_other, m, v, cvec, name):
    rows, cols = w.shape
    half = rows // 2
    tr = min(256, half)
    per_half = half // tr

    def body(c_ref, w_ref, ga_ref, gb_ref, m_ref, v_ref, g_ref, d_ref, m2_ref, v2_ref):
        in_my_half = (pl.program_id(0) // per_half) == c_ref[0]
        gv = jnp.where(in_my_half, ga_ref[...], gb_ref[...])
        g_ref[...] = gv
        m2 = ADAM_B1 * m_ref[...] + (1.0 - ADAM_B1) * gv
        v2 = ADAM_B2 * v_ref[...] + (1.0 - ADAM_B2) * jnp.square(gv)
        m_hat = m2 / (1.0 - ADAM_B1 ** ADAM_STEP)
        v_hat = v2 / (1.0 - ADAM_B2 ** ADAM_STEP)
        d_ref[...] = -ADAM_LR * (m_hat / (jnp.sqrt(v_hat) + ADAM_EPS) + ADAM_WD * w_ref[...])
        m2_ref[...] = m2
        v2_ref[...] = v2

    full = pl.BlockSpec((tr, cols), lambda i, c: (i, 0))
    part = pl.BlockSpec((tr, cols), lambda i, c: (i % per_half, 0))
    shp = SDS((rows, cols), f32)
    grid_spec = pltpu.PrefetchScalarGridSpec(num_scalar_prefetch=1, grid=(rows // tr,),
                                             in_specs=[full, part, part, full, full], out_specs=(full,) * 4)
    return _hbm_call(body, name=name, out_shape=(shp,) * 4, grid_spec=grid_spec)(cvec, w, g_mine, g_other, m, v)


def _adam_math(w, g, m, v):
    m2 = ADAM_B1 * m + (1.0 - ADAM_B1) * g
    v2 = ADAM_B2 * v + (1.0 - ADAM_B2) * jnp.square(g)
    m_hat = m2 / (1.0 - ADAM_B1 ** ADAM_STEP)
    v_hat = v2 / (1.0 - ADAM_B2 ** ADAM_STEP)
    return -ADAM_LR * (m_hat / (jnp.sqrt(v_hat) + ADAM_EPS) + ADAM_WD * w), m2, v2


def _adamw_small(red, g_c_ctx, jvec, ws, ms, vs):
    n = len(ws)

    def body(*refs):
        red_ref, gc_ref, j_ref = refs[:3]
        w_refs, m_refs, v_refs = refs[3:3 + n], refs[3 + n:3 + 2 * n], refs[3 + 2 * n:3 + 3 * n]
        outs = refs[3 + 3 * n:]
        g_out, d_out, m_out, v_out = outs[:n], outs[n:2 * n], outs[2 * n:3 * n], outs[3 * n:]
        chip = j_ref[0]
        lanes = lambda i: (slice(None), slice(128 * i, 128 * (i + 1)))
        row = lambda r0, i: (lambda: red_ref[r0 + i:r0 + i + 1, :])
        whole = (slice(None), slice(None))
        chunks = [
            [((slice(None),), lambda: gc_ref[...].reshape(D))],
            [(lanes(i), row(RS_B_ADA, i)) for i in range(3 * D // 128)],
            [(lanes(i), row(RS_NORM_G, i)) for i in range(D // 128)],
            [(whole, lambda: red_ref[RS_GQ:RS_GQ + 1, 0:DH])],
            [(whole, lambda: red_ref[RS_GK:RS_GK + 1, 0:DH])],
            [((dr,), (lambda dr=dr: red_ref[pl.ds(RS_RPB + dr, H, stride=N_DR), 0:N_DC])) for dr in range(N_DR)],
            [((r,), (lambda r=r: red_ref[pl.ds(RS_CONV_W + 4 * r + chip, 1), :])) for r in range(3)],
            [(lanes(i), row(RS_CONV_B, i)) for i in range(DC // 128)],
        ]
        for a in range(n):
            for idx, grad in chunks[a]:
                g = grad()
                d, m2, v2 = _adam_math(w_refs[a][idx], g, m_refs[a][idx], v_refs[a][idx])
                g_out[a][idx] = g
                d_out[a][idx] = d
                m_out[a][idx] = m2
                v_out[a][idx] = v2

    shapes = [SDS(w.shape, f32) for w in ws]
    res = _pallas_call(body, name="adamw_small", out_shape=shapes * 4,
                       in_specs=[VMEM_SPEC, VMEM_SPEC, SMEM_SPEC] + [VMEM_SPEC] * (3 * n),
                       out_specs=[VMEM_SPEC] * (4 * n))(red, g_c_ctx, jvec, *ws, *ms, *vs)
    return [list(res[k * n:(k + 1) * n]) for k in range(4)]


def _rows128(a):
    return a.reshape(-1, 128)


def kernel(x, c, ctx, c_ctx, w_ada, b_ada, norm_g, w_in, q_norm_g, k_norm_g, rpb, conv_w, conv_b, w_out, loss_target, m_c_ctx, m_w_ada, m_b_ada, m_norm_g, m_w_in, m_q_norm_g, m_k_norm_g, m_rpb, m_conv_w, m_conv_b, m_w_out, v_c_ctx, v_w_ada, v_b_ada, v_norm_g, v_w_in, v_q_norm_g, v_k_norm_g, v_rpb, v_conv_w, v_conv_b, v_w_out):
    xi, yi, ci = lax.axis_index("x"), lax.axis_index("y"), lax.axis_index("c")
    dev = 4 * xi + 2 * yi + ci
    chip = 2 * xi + yi
    cvec = jnp.reshape(ci, (1,)).astype(i32)
    jvec = jnp.reshape(chip, (1,)).astype(i32)
    w_ada_s = w_ada[0]
    ncol = w_ada_s.shape[1]

    gc = _split_start([_to_slot(c.reshape(8, 128), 8, dev)], [], 7, _gather8_copies, "gather_c_start")
    btb = _bias_tiles(rpb[0], gc[3][0:1, 0:1])
    (c8,), _ = _split_wait(gc[0], gc[1], [gc[2]], [], btb, _gather8_copies, "gather_c_wait")
    cc = jnp.concatenate([c8.reshape(8, D), c_ctx.reshape(1, D), jnp.zeros((7, D), f32)], axis=0)
    m_shard, sc16 = _adaln_shard(cc, w_ada_s)

    conv_w_pad = jnp.pad(conv_w[0], ((0, 5), (0, 0)))
    gm = _split_start([_to_slot(m_shard, 4, chip), _to_slot(conv_w_pad, 4, chip)], [], 6, _gather4_copies,
                      "gather_mod_start")
    w4c = _cast_to_slot(w_in[0], jvec, "cast_w_in")
    (m4, cw4), _ = _split_wait(gm[0], gm[1], [gm[2], gm[3]], [], w4c, _gather4_copies, "gather_mod_wait")

    all_k = [(0, 0, 0), (0, 0, 1), (0, 0, 2)]
    wo4c = _cast_to_slot(w_out[0], jvec, "cast_w_out")
    sem_a, rem_a, w4s, token = _split_start([w4c], [], 1, _near_copies, "weights_near_start", after=m4)
    m_full = jnp.transpose(m4, (1, 0, 2)).reshape(16, 4 * ncol) + token[0:1, 0:1]
    mrow = lax.dynamic_slice(m_full, (dev, 0), (1, 3 * D))
    mrow_c = m_full[8:9]
    conv_w_full = jnp.transpose(cw4[:, 0:3, :], (1, 0, 2)).reshape(3, DC)
    waves = {}

    def near(after):
        (w4w,), _ = _split_wait(sem_a, rem_a, [w4s], [], after, _near_copies, "weights_near_wait")
        sem_b, rem_b, w4b, started = _split_start([w4w], [], 2, _pass_relay_copies, "weights_pass_start")
        waves["pass"] = (sem_b, rem_b)
        return w4b, started

    def near2(w4, after):
        (w4w,), _ = _split_wait(*waves["pass"], [w4], [], after, _pass_copy, "weights_pass_wait")
        return w4w

    def far(w4, after):
        (w4w,), _ = _split_wait(*waves["pass"], [w4], [], after, _relay_copy, "weights_far_wait")
        w4f = _halves_forward([w4w], [2], "weights_far_forward")[0]
        sem_c, rem_c, wo4s, started = _halves_start([wo4c], w4f, all_k, "weights_out_start")
        waves["out"] = (sem_c, rem_c, wo4s)
        return w4f, started

    def w_out_gathered(after):
        sem_c, rem_c, wo4s = waves["out"]
        (wow,) = _halves_wait(sem_c, rem_c, [wo4s], after, all_k, "weights_out_wait")
        return _halves_forward([wow], [0, 1, 2], "weights_out_forward")[0].reshape(D, D)

    weights = dict(own=w_in[0], jvec=jvec, first=jvec ^ (1 + cvec), second=jvec ^ (2 - cvec), near=near, near2=near2,
                   far=far, out=w_out_gathered)

    exchange = _exchange_copies
    pending = {}

    def on_g_w_out(g_w_out):
        out = _split_start([g_w_out], [SDS((4, D // 8, D), f32)], 4, exchange, "grad_out_pair_start")
        pending["ex_out"] = out
        return out[4]

    def after_conv(dp8):
        ssem_o, rsem_o, g_o, land_o, _ = pending["ex_out"]
        (g_o,), (ex_o,) = _split_wait(ssem_o, rsem_o, [g_o], [land_o], dp8, exchange, "grad_out_pair_wait")
        to32, tob = _pair_sum_w_out(g_o, ex_o, cvec)
        out = _split_start([tob], [SDS((3, D // 8, D), bf16)], 3, _scatter_copies, "grad_out_chip_start")
        pending["sc_out"] = (out, to32)
        return out[4]

    def on_g_w_in(g_w_in):
        out = _split_start([g_w_in], [SDS((4, D // 2, D), f32)], 4, exchange, "grad_pair_start")
        pending["ex"] = (out[0], out[1], [out[2]], [out[3]])
        return out[4]

    def on_g_rpb(g_rpb):
        ex_ssem, ex_rsem, ex_srcs, ex_lands = pending["ex"]
        ex_g, ex = _split_wait(ex_ssem, ex_rsem, ex_srcs, ex_lands, g_rpb, exchange, "grad_pair_wait")
        t32, tb = _pair_sum_w_in(ex_g[0], ex[0], cvec)
        out = _split_start([tb], [SDS((3, D // 2, D), bf16)], 3, _scatter_copies, "grad_chip_start")
        pending["sc_in"] = (out, t32)
        return out[4]

    r = _local_step(x[0], ctx[0], loss_target[0], mrow, mrow_c, b_ada, norm_g, weights, q_norm_g, k_norm_g,
                    btb, conv_w_full, conv_b,
                    dict(g_w_out=on_g_w_out, after_conv=after_conv, g_w_in=on_g_w_in, g_rpb=on_g_rpb))
    dm = jnp.concatenate([r["dshift"], r["dscale"], r["dgate"]], axis=1)
    dmc = jnp.concatenate([r["dshift_c"], r["dscale_c"], jnp.zeros((1, D), f32)], axis=1)
    pack_parts = [_rows128(dm), _rows128(dmc), _rows128(r["dng"]), _rows128(r["dng_c"]), _rows128(r["g_gq"]),
                  _rows128(r["g_gk"]), _rows128(r["g_gk_c"]), r["g_rpb"].reshape(H * N_DR, 128),
                  _rows128(r["g_conv_b"]), _rows128(r["g_conv_w"][0:3]), jnp.pad(r["loss_sum"], ((0, 0), (0, 127)))]
    pack = jnp.concatenate([jnp.pad(p, ((0, -p.shape[0] % 8), (0, 0))) for p in pack_parts], axis=0)
    assert pack.shape[0] == PK_ROWS
    gs = _split_start([_to_slot(pack, 8, dev)], [], 7, _gather8_copies, "gather_small_start")
    sc_o, to32 = pending["sc_out"]
    _, (ro2,) = _split_wait(sc_o[0], sc_o[1], [sc_o[2]], [sc_o[3]], gs[3], _scatter_copies, "grad_out_chip_wait")
    u_out = _chip_sum(to32, ro2, jvec, "chip_sum_w_out")
    (o_out,) = _sibling_send([u_out], "grad_out_pair_send")
    g_w_out_s, d_w_out, nm_w_out, nv_w_out = _adamw_halves(w_out[0], u_out, o_out, m_w_out[0], v_w_out[0], cvec,
                                                           "adamw_w_out")
    (gathered,), _ = _split_wait(gs[0], gs[1], [gs[2]], [], nm_w_out, _gather8_copies, "gather_small_wait")
    red, dm16 = _small_reduce(gathered)
    loss = red[RS_LOSS, 0] * (0.5 / D)

    g_w_ada_s, cpart = _w_ada_grad(sc16, dm16, w_ada_s, jvec)
    d_w_ada, nm_w_ada, nv_w_ada = _adamw(w_ada_s, g_w_ada_s, m_w_ada[0], v_w_ada[0], "adamw_w_ada")

    (cparts4,) = _chip_gather([cpart], "gather_c_ctx_parts", after=nm_w_ada)
    g_c_ctx = _c_ctx_grad(cparts4, c_ctx)

    sc_in, t32 = pending["sc_in"]
    _, (r2,) = _split_wait(sc_in[0], sc_in[1], [sc_in[2]], [sc_in[3]], g_c_ctx, _scatter_copies, "grad_chip_wait")
    u_in = _chip_sum(t32, r2, jvec, "chip_sum_w_in")

    t_rpb = lambda a: jnp.transpose(a, (0, 2, 1, 3)).reshape(N_DR, H, N_DC)
    t_cw = lambda a: jnp.transpose(a, (1, 0, 2))
    small = _adamw_small(
        red, g_c_ctx, jvec,
        [c_ctx, b_ada, norm_g, q_norm_g, k_norm_g, t_rpb(rpb), t_cw(conv_w), conv_b],
        [m_c_ctx, m_b_ada, m_norm_g, m_q_norm_g, m_k_norm_g, t_rpb(m_rpb), t_cw(m_conv_w), m_conv_b],
        [v_c_ctx, v_b_ada, v_norm_g, v_q_norm_g, v_k_norm_g, t_rpb(v_rpb), t_cw(v_conv_w), v_conv_b])
    for kind in small:
        kind[5] = jnp.transpose(kind[5].reshape(1, N_DR, H, N_DC), (0, 2, 1, 3))
        kind[6] = jnp.transpose(kind[6], (1, 0, 2))

    (o_in,) = _sibling_send([u_in], "grad_pair_send")
    g_w_in_s, d_w_in, nm_w_in, nv_w_in = _adamw_halves(w_in[0], u_in, o_in, m_w_in[0], v_w_in[0], cvec, "adamw_w_in")

    def ordered(kind, big_w_ada, big_w_in, big_w_out):
        s_c_ctx, s_b_ada, s_norm_g, s_q, s_k, s_rpb, s_conv_w, s_conv_b = small[kind]
        return [s_c_ctx, big_w_ada[None], s_b_ada, s_norm_g, big_w_in[None], s_q, s_k, s_rpb, s_conv_w,
                s_conv_b, big_w_out[None]]

    grads = ordered(0, g_w_ada_s, g_w_in_s, g_w_out_s)
    deltas = ordered(1, d_w_ada, d_w_in, d_w_out)
    new_m = ordered(2, nm_w_ada, nm_w_in, nm_w_out)
    new_v = ordered(3, nv_w_ada, nv_w_in, nv_w_out)
    return (loss, r["grad_x"][None], *grads, *deltas, *new_m, *new_v)
```

```python
import functools

import jax
import jax.numpy as jnp
import numpy as np
from jax import lax
from jax.experimental import pallas as pl
from jax.experimental.pallas import tpu as pltpu

f32, bf16, i32 = jnp.float32, jnp.bfloat16, jnp.int32
MESH = pl.DeviceIdType.MESH
HIGHEST = lax.Precision.HIGHEST

D = 1024
S = 2048
L = 256
GW = 64
ROWS = S // GW
H = 8
DH = 64
DA = H * DH
DC = 512
WIN_H, WIN_W = 8, 16
N_DR, N_DC = 2 * WIN_H - 1, 2 * WIN_W - 1
RMS_EPS = 1e-6
ROPE_THETA = 10000.0
QK_SCALE = DH ** -0.5
NEG = -1e30

QB = 128
NQB = S // QB
KR = 9
KB = KR * GW
TILE_GEOM = ((0, 0), (2, 0), (4, 0), (28, 23), (30, 23))
NT = len(TILE_GEOM)

ADAM_LR, ADAM_B1, ADAM_B2, ADAM_EPS, ADAM_WD, ADAM_STEP = 0.001, 0.9, 0.999, 1e-08, 0.01, 10

VMEM_SPEC = pl.BlockSpec(memory_space=pltpu.VMEM)
ANY_SPEC = pl.BlockSpec(memory_space=pl.ANY)
SMEM_SPEC = pl.BlockSpec(memory_space=pltpu.SMEM)
SDS = jax.ShapeDtypeStruct


_pallas_call = pl.pallas_call


def _hbm_call(body, *, out_shape, in_specs=None, out_specs=None, grid_spec=None, **kw):
    n_pre = 0
    if grid_spec is not None:
        ispecs, ospecs, n_pre = grid_spec.in_specs, grid_spec.out_specs, grid_spec.num_scalar_prefetch
        kw["grid_spec"] = grid_spec
    else:
        ispecs, ospecs = in_specs, out_specs
        kw.update(in_specs=in_specs, out_specs=out_specs)

    def blocked(spec):
        return isinstance(spec, pl.BlockSpec) and spec.block_shape is not None

    single = not isinstance(out_shape, (tuple, list))
    shapes = [out_shape] if single else list(out_shape)
    ospec_list = list(ospecs) if isinstance(ospecs, (tuple, list)) else [ospecs]
    shapes = [pltpu.HBM(s.shape, s.dtype) if blocked(sp) else s for s, sp in zip(shapes, ospec_list)]
    call = _pallas_call(body, out_shape=shapes[0] if single else tuple(shapes), **kw)

    def run(*args):
        arrays = [pltpu.with_memory_space_constraint(a, pltpu.HBM) if blocked(sp) else a
                  for a, sp in zip(args[n_pre:], ispecs)]
        return call(*args[:n_pre], *arrays)

    return run


def _cp(vmem_mb=None, **kw):
    if vmem_mb is not None:
        kw["vmem_limit_bytes"] = vmem_mb << 20
    return pltpu.CompilerParams(**kw)


def _silu(z):
    return z * jax.nn.sigmoid(z)


def _dsilu(z):
    sg = jax.nn.sigmoid(z)
    return sg * (1.0 + z * (1.0 - sg))


def _row_start(i):
    return min(max(i - WIN_H // 2, 0), ROWS - WIN_H)


def _my_pos():
    return lax.axis_index("x"), lax.axis_index("y"), lax.axis_index("c")


def _flip(v, bit):
    return 1 - v if bit else v


def _chip_gather(smalls, name, after=None):
    ns = len(smalls)

    def body(*refs):
        s_in, s_out = refs[:ns], refs[ns + 1:2 * ns + 1]
        ssem, rsem, lsem = refs[2 * ns + 1:]
        x, y, c = _my_pos()
        j = 2 * x + y
        chips = _peer_chips(x, y, c)
        local = [pltpu.make_async_copy(s_in[a], s_out[a].at[j], lsem.at[a]) for a in range(ns)]
        for cp in local:
            cp.start()
        sends = []
        for a in range(ns):
            for k in range(3):
                cp = pltpu.make_async_remote_copy(src_ref=s_in[a], dst_ref=s_out[a].at[j], send_sem=ssem.at[3 * a + k],
                                                  recv_sem=rsem.at[3 * a + k], device_id=chips[k][0], device_id_type=MESH)
                cp.start()
                sends.append(cp)
        for a in range(ns):
            for k in range(3):
                pltpu.make_async_remote_copy(src_ref=s_in[a], dst_ref=s_out[a].at[chips[k][1]], send_sem=ssem.at[3 * a + k],
                                             recv_sem=rsem.at[3 * a + k], device_id=chips[k][0],
                                             device_id_type=MESH).wait_recv()
        for cp in sends:
            cp.wait_send()
        for cp in local:
            cp.wait()

    return _hbm_call(
        body, name=name, out_shape=[SDS((4,) + a.shape, a.dtype) for a in smalls],
        in_specs=[VMEM_SPEC] * ns + [ANY_SPEC], out_specs=[VMEM_SPEC] * ns,
        scratch_shapes=[pltpu.SemaphoreType.DMA((3 * ns,)), pltpu.SemaphoreType.DMA((3 * ns,)),
                        pltpu.SemaphoreType.DMA((ns,))],
    )(*smalls, smalls[0] if after is None else after)


HBM_SPEC = pl.BlockSpec(memory_space=pltpu.HBM)
SEM_SPEC = pl.BlockSpec(memory_space=pltpu.SEMAPHORE)
DATAFLOW = pltpu.SideEffectType.DATAFLOW_SIDE_EFFECTING


def _peer_chips(x, y, c):
    out = []
    for k in range(1, 4):
        px, py = _flip(x, (k >> 1) & 1), _flip(y, k & 1)
        out.append(((px, py, c), 2 * px + py))
    return out


def _half_copies(srcs, dsts, ssem, rsem, which):
    x, y, c = _my_pos()
    j = 2 * x + y
    peers = _peer_chips(x, y, c)
    pairs = []
    for pos, group, k in which:
        half = srcs[pos].shape[1] // 2
        mine = pl.ds(pl.multiple_of(c * half, 8), half)
        dev, pj = peers[k]
        sem = 3 * group + k
        send = pltpu.make_async_remote_copy(src_ref=srcs[pos].at[j, mine], dst_ref=dsts[pos].at[j, mine],
                                            send_sem=ssem.at[sem], recv_sem=rsem.at[sem], device_id=dev,
                                            device_id_type=MESH)
        arrive = pltpu.make_async_remote_copy(src_ref=srcs[pos].at[j, mine], dst_ref=dsts[pos].at[pj, mine],
                                              send_sem=ssem.at[sem], recv_sem=rsem.at[sem], device_id=dev,
                                              device_id_type=MESH)
        pairs.append((send, arrive))
    return pairs


def _halves_start(bigs, after, order, name):
    nb = len(bigs)

    def body(*refs):
        b_in = refs[:nb]
        ssem, rsem = refs[nb + 1], refs[nb + 2]
        b_out = refs[nb + 3:2 * nb + 3]
        token = refs[2 * nb + 3]
        for send, _ in _half_copies(b_in, b_out, ssem, rsem, order):
            send.start()
        token[...] = jnp.zeros_like(token)

    out_shape = (pltpu.SemaphoreType.DMA((3 * nb,)), pltpu.SemaphoreType.DMA((3 * nb,)),
                 *[pltpu.HBM(b.shape, b.dtype) for b in bigs], SDS((8, 128), f32))
    return _hbm_call(
        body, name=name, out_shape=out_shape, in_specs=[HBM_SPEC] * nb + [ANY_SPEC],
        out_specs=(SEM_SPEC, SEM_SPEC, *[HBM_SPEC] * nb, VMEM_SPEC),
        input_output_aliases={a: 2 + a for a in range(nb)}, compiler_params=_cp(has_side_effects=DATAFLOW),
    )(*[pltpu.with_memory_space_constraint(b, pltpu.HBM) for b in bigs], after)


def _halves_wait(ssem, rsem, bigs, after, which, name):
    nb = len(bigs)

    def body(*refs):
        b_in = refs[:nb]
        ssem_ref, rsem_ref = refs[nb], refs[nb + 1]
        for send, arrive in _half_copies(b_in, b_in, ssem_ref, rsem_ref, which):
            send.wait_send()
            arrive.wait_recv()

    return _hbm_call(
        body, name=name, out_shape=tuple(pltpu.HBM(b.shape, b.dtype) for b in bigs),
        in_specs=[HBM_SPEC] * nb + [SEM_SPEC, SEM_SPEC, ANY_SPEC], out_specs=tuple([HBM_SPEC] * nb),
        input_output_aliases={a: a for a in range(nb)}, compiler_params=_cp(has_side_effects=DATAFLOW),
    )(*bigs, ssem, rsem, after)


def _halves_forward(bigs, relations, name):
    nb, nr = len(bigs), len(relations)

    def body(*refs):
        b_in, b_out = refs[:nb], refs[nb:2 * nb]
        ssem, rsem = refs[2 * nb:]
        x, y, c = _my_pos()
        sib = (x, y, 1 - c)
        peers = _peer_chips(x, y, c)
        sends = []
        for a in range(nb):
            half = b_in[a].shape[1] // 2
            mine = pl.ds(pl.multiple_of(c * half, 8), half)
            for i, k in enumerate(relations):
                pj = peers[k][1]
                cp = pltpu.make_async_remote_copy(src_ref=b_in[a].at[pj, mine], dst_ref=b_out[a].at[pj, mine],
                                                  send_sem=ssem.at[nr * a + i], recv_sem=rsem.at[nr * a + i],
                                                  device_id=sib, device_id_type=MESH)
                cp.start()
                sends.append(cp)
        for a in range(nb):
            half = b_in[a].shape[1] // 2
            other = pl.ds(pl.multiple_of((1 - c) * half, 8), half)
            for i, k in enumerate(relations):
                pj = peers[k][1]
                pltpu.make_async_remote_copy(src_ref=b_in[a].at[pj, other], dst_ref=b_out[a].at[pj, other],
                                             send_sem=ssem.at[nr * a + i], recv_sem=rsem.at[nr * a + i],
                                             device_id=sib, device_id_type=MESH).wait_recv()
        for cp in sends:
            cp.wait_send()

    return _hbm_call(
        body, name=name, out_shape=[SDS(b.shape, b.dtype) for b in bigs], in_specs=[ANY_SPEC] * nb,
        out_specs=[ANY_SPEC] * nb, input_output_aliases={a: a for a in range(nb)},
        scratch_shapes=[pltpu.SemaphoreType.DMA((nr * nb,)), pltpu.SemaphoreType.DMA((nr * nb,))],
    )(*bigs)


def _cast_to_slot(w, jvec, name):
    rows, cols = w.shape
    tr = 256

    def body(j_ref, w_ref, o_ref):
        o_ref[...] = w_ref[...].astype(bf16)

    grid_spec = pltpu.PrefetchScalarGridSpec(
        num_scalar_prefetch=1, grid=(rows // tr,), in_specs=[pl.BlockSpec((tr, cols), lambda i, j: (i, 0))],
        out_specs=pl.BlockSpec((None, tr, cols), lambda i, j: (j[0], i, 0)))
    return _hbm_call(body, name=name, out_shape=SDS((4, rows, cols), bf16), grid_spec=grid_spec)(jvec, w)


def _exchange_copies(srcs, lands, ssem, rsem):
    x, y, c = _my_pos()
    cps = []
    for a in range(len(srcs)):
        stacked = len(srcs[a].shape) == 3
        rb = srcs[a].shape[1] if stacked else srcs[a].shape[0] // 4
        half = rb // 2
        for jb in range(4):
            if stacked:
                src = srcs[a].at[jb, pl.ds(pl.multiple_of((1 - c) * half, 8), half)]
            else:
                src = srcs[a].at[pl.ds(pl.multiple_of(jb * rb + (1 - c) * half, 8), half)]
            cps.append(pltpu.make_async_remote_copy(src_ref=src, dst_ref=lands[a].at[jb], send_sem=ssem.at[4 * a + jb],
                                                    recv_sem=rsem.at[4 * a + jb], device_id=(x, y, 1 - c),
                                                    device_id_type=MESH))
    return cps


def _scatter_copies(srcs, lands, ssem, rsem):
    x, y, c = _my_pos()
    cps = []
    for a in range(len(srcs)):
        for k, (dev, pj) in enumerate(_peer_chips(x, y, c)):
            cps.append(pltpu.make_async_remote_copy(src_ref=srcs[a].at[pj], dst_ref=lands[a].at[k],
                                                    send_sem=ssem.at[3 * a + k], recv_sem=rsem.at[3 * a + k],
                                                    device_id=dev, device_id_type=MESH))
    return cps


class _Copy:
    def __init__(self, src, dst, arrive, ssem, rsem, dev):
        make = lambda to: pltpu.make_async_remote_copy(src_ref=src, dst_ref=to, send_sem=ssem, recv_sem=rsem,
                                                       device_id=dev, device_id_type=MESH)
        send, arrival = make(dst), make(arrive)
        self.start, self.wait_send, self.wait_recv = send.start, send.wait_send, arrival.wait_recv


def _toward(x, y, along_x):
    return x + along_x * (1 - 2 * x), y + (1 - along_x) * (1 - 2 * y)


def _near_copies(srcs, dsts, ssem, rsem):
    x, y, c = _my_pos()
    px, py = _toward(x, y, c)
    j = 2 * x + y
    return [_Copy(srcs[0].at[j], dsts[0].at[j], dsts[0].at[2 * px + py], ssem.at[0], rsem.at[0], (px, py, c))]


def _pass_copy(srcs, dsts, ssem, rsem):
    x, y, c = _my_pos()
    px, py = _toward(x, y, c)
    qx, qy = _toward(x, y, 1 - c)
    got = 2 * px + py
    return [_Copy(srcs[0].at[got], dsts[0].at[got], dsts[0].at[2 * qx + qy], ssem.at[0], rsem.at[0], (x, y, 1 - c))]


def _relay_copy(srcs, dsts, ssem, rsem):
    x, y, c = _my_pos()
    px, py = _toward(x, y, c)
    qx, qy = _toward(x, y, 1 - c)
    half = srcs[0].shape[1] // 2
    mine = pl.ds(pl.multiple_of(c * half, 8), half)
    got, diag = 2 * px + py, 3 - (2 * x + y)
    return [_Copy(srcs[0].at[got, mine], dsts[0].at[got, mine], dsts[0].at[diag, mine], ssem.at[1], rsem.at[1],
                  (qx, qy, c))]


def _forward_copies(srcs, dsts, ssem, rsem):
    x, y, c = _my_pos()
    half = srcs[0].shape[1] // 2
    mine = pl.ds(pl.multiple_of(c * half, 8), half)
    other = pl.ds(pl.multiple_of((1 - c) * half, 8), half)
    return [_Copy(srcs[0].at[pj, mine], dsts[0].at[pj, mine], dsts[0].at[pj, other], ssem.at[k], rsem.at[k],
                  (x, y, 1 - c)) for k, (_, pj) in enumerate(_peer_chips(x, y, c))]


def _pass_relay_copies(srcs, dsts, ssem, rsem):
    return _pass_copy(srcs, dsts, ssem, rsem) + _relay_copy(srcs, dsts, ssem, rsem)


def _gather8_copies(srcs, dsts, ssem, rsem):
    x, y, c = _my_pos()
    me = 4 * x + 2 * y + c
    cps = []
    for a in range(len(srcs)):
        for k in range(1, 8):
            tgt = (_flip(x, (k >> 2) & 1), _flip(y, (k >> 1) & 1), _flip(c, k & 1))
            cps.append(_Copy(srcs[a].at[me], dsts[a].at[me], dsts[a].at[4 * tgt[0] + 2 * tgt[1] + tgt[2]],
                             ssem.at[7 * a + k - 1], rsem.at[7 * a + k - 1], tgt))
    return cps


def _gather4_copies(srcs, dsts, ssem, rsem):
    x, y, c = _my_pos()
    j = 2 * x + y
    cps = []
    for a in range(len(srcs)):
        for k, (dev, pj) in enumerate(_peer_chips(x, y, c)):
            cps.append(_Copy(srcs[a].at[j], dsts[a].at[j], dsts[a].at[pj], ssem.at[3 * a + k], rsem.at[3 * a + k], dev))
    return cps


def _to_slot(a, n, i):
    return lax.dynamic_update_slice(jnp.zeros((n,) + a.shape, a.dtype), a[None], (i,) + (0,) * a.ndim)


def _split_start(srcs, land_shapes, n_cp, make, name, after=None):
    ns, nl = len(srcs), len(land_shapes)
    n_in = ns + nl + (after is not None)

    def body(*refs):
        s_in = refs[:ns]
        ssem, rsem = refs[n_in], refs[n_in + 1]
        s_out = refs[n_in + 2:n_in + 2 + ns]
        l_out = refs[n_in + 2 + ns:n_in + 2 + ns + nl]
        token = refs[n_in + 2 + ns + nl]
        for cp in make(s_in, l_out if nl else s_out, ssem, rsem):
            cp.start()
        token[...] = jnp.zeros_like(token)

    lands = [pltpu.with_memory_space_constraint(lax.empty(sh.shape, sh.dtype), pltpu.HBM) for sh in land_shapes]
    out_shape = (pltpu.SemaphoreType.DMA((n_cp,)), pltpu.SemaphoreType.DMA((n_cp,)),
                 *[pltpu.HBM(b.shape, b.dtype) for b in srcs], *[pltpu.HBM(b.shape, b.dtype) for b in land_shapes],
                 SDS((8, 128), f32))
    return _hbm_call(
        body, name=name, out_shape=out_shape, in_specs=[HBM_SPEC] * (ns + nl) + [ANY_SPEC] * (after is not None),
        out_specs=(SEM_SPEC, SEM_SPEC, *[HBM_SPEC] * (ns + nl), VMEM_SPEC),
        input_output_aliases={i: 2 + i for i in range(ns + nl)}, compiler_params=_cp(has_side_effects=DATAFLOW),
    )(*[pltpu.with_memory_space_constraint(b, pltpu.HBM) for b in srcs], *lands, *([] if after is None else [after]))


def _split_wait(ssem, rsem, srcs, lands, after, make, name):
    ns, nl = len(srcs), len(lands)

    def body(*refs):
        s_in, l_in = refs[:ns], refs[ns:ns + nl]
        ssem_ref, rsem_ref = refs[ns + nl], refs[ns + nl + 1]
        for cp in make(s_in, l_in if nl else s_in, ssem_ref, rsem_ref):
            cp.wait_send()
            cp.wait_recv()

    outs = _hbm_call(
        body, name=name, out_shape=tuple(pltpu.HBM(b.shape, b.dtype) for b in (*srcs, *lands)),
        in_specs=[HBM_SPEC] * (ns + nl) + [SEM_SPEC, SEM_SPEC, ANY_SPEC], out_specs=tuple([HBM_SPEC] * (ns + nl)),
        input_output_aliases={i: i for i in range(ns + nl)}, compiler_params=_cp(has_side_effects=DATAFLOW),
    )(*srcs, *lands, ssem, rsem, after)
    return list(outs[:ns]), list(outs[ns:])


def _sibling_send(halves, name):
    n = len(halves)

    def body(*refs):
        ins, outs = refs[:n], refs[n:2 * n]
        ssem, rsem = refs[2 * n:]
        x, y, c = _my_pos()
        cps = []
        for a in range(n):
            cp = pltpu.make_async_remote_copy(src_ref=ins[a], dst_ref=outs[a], send_sem=ssem.at[a],
                                              recv_sem=rsem.at[a], device_id=(x, y, 1 - c), device_id_type=MESH)
            cp.start()
            cps.append(cp)
        for cp in cps:
            cp.wait_recv()
        for cp in cps:
            cp.wait_send()

    out_shape = [SDS(h.shape, h.dtype) for h in halves]
    return _hbm_call(
        body, name=name, out_shape=out_shape, in_specs=[ANY_SPEC] * n, out_specs=[ANY_SPEC] * n,
        scratch_shapes=[pltpu.SemaphoreType.DMA((n,)), pltpu.SemaphoreType.DMA((n,))],
    )(*halves)


def _adaln_shard(cc, w_ada_shard):
    def body(c_ref, w_ref, m_ref, sc_ref):
        sc = _silu(c_ref[...])
        sc_ref[...] = sc
        m_ref[...] = jnp.dot(sc, w_ref[...], precision=HIGHEST, preferred_element_type=f32)

    return _hbm_call(
        body, name="adaln_shard", out_shape=(SDS((16, w_ada_shard.shape[1]), f32), SDS((16, D), f32)),
        in_specs=[VMEM_SPEC, VMEM_SPEC], out_specs=(VMEM_SPEC, VMEM_SPEC), compiler_params=_cp(32),
    )(cc, w_ada_shard)


def _prenorm(xx, norm_g, mrow, b_ada, tm, name, after=None):
    n = xx.shape[0]

    def body(x_ref, g_ref, m_ref, b_ref, after_ref, h_ref):
        x = x_ref[...]
        shift = m_ref[:, 0:D] + b_ref[:, 0:D]
        scale = m_ref[:, D:2 * D] + b_ref[:, D:2 * D]
        r = lax.rsqrt(jnp.mean(x * x, axis=-1, keepdims=True) + RMS_EPS)
        y = (x * r) * g_ref[...]
        h_ref[...] = (y * (1.0 + scale) + shift).astype(bf16)

    row = lambda i: (i, 0)
    fixed = lambda i: (0, 0)
    return _hbm_call(
        body, name=name, out_shape=SDS((n, D), bf16), grid=(n // tm,),
        in_specs=[pl.BlockSpec((tm, D), row), pl.BlockSpec((1, D), fixed), pl.BlockSpec((1, 3 * D), fixed),
                  pl.BlockSpec((1, 3 * D), fixed), ANY_SPEC],
        out_specs=pl.BlockSpec((tm, D), row),
    )(xx, norm_g, mrow, b_ada, b_ada if after is None else after)


def _in_proj_own(h, w_own, jvec):
    tm = 512

    def body(j_ref, h_ref, w_ref, p_ref):
        p_ref[...] = jnp.dot(h_ref[...], w_ref[...].astype(bf16), preferred_element_type=f32)

    grid_spec = pltpu.PrefetchScalarGridSpec(
        num_scalar_prefetch=1, grid=(S // tm,),
        in_specs=[pl.BlockSpec((tm, D), lambda i, j: (i, 0)), pl.BlockSpec((D, D), lambda i, j: (0, 0))],
        out_specs=pl.BlockSpec((tm, D), lambda i, j: (i, j[0])))
    return _hbm_call(body, name="in_proj_own", out_shape=SDS((S, 4 * D), f32), grid_spec=grid_spec,
                     compiler_params=_cp(40))(jvec, h, w_own)


def _in_proj_block(h, w4, p, bvec, name, after=None):
    tm = 512

    def body(b_ref, h_ref, w_ref, p_in_ref, after_ref, p_ref):
        p_ref[...] = jnp.dot(h_ref[...], w_ref[...], preferred_element_type=f32)

    grid_spec = pltpu.PrefetchScalarGridSpec(
        num_scalar_prefetch=1, grid=(S // tm,),
        in_specs=[pl.BlockSpec((tm, D), lambda i, b: (i, 0)), pl.BlockSpec((None, D, D), lambda i, b: (b[0], 0, 0)),
                  ANY_SPEC, ANY_SPEC],
        out_specs=pl.BlockSpec((tm, D), lambda i, b: (i, b[0])))
    return _hbm_call(body, name=name, out_shape=SDS((S, 4 * D), f32), grid_spec=grid_spec,
                     input_output_aliases={3: 0})(bvec, h, w4, p, bvec if after is None else after)


def _ctx_proj(hc, w4):
    def body(h_ref, w0_ref, w1_ref, p_ref):
        hv = h_ref[...]
        p_ref[:, 0:DA] = jnp.dot(hv, w0_ref[:, DA:2 * DA], preferred_element_type=f32)
        p_ref[:, DA:2 * DA] = jnp.dot(hv, w1_ref[:, 0:DA], preferred_element_type=f32)

    return _hbm_call(
        body, name="ctx_proj", out_shape=SDS((L, 2 * DA), f32), grid=(1,),
        in_specs=[pl.BlockSpec((L, D), lambda i: (0, 0)), pl.BlockSpec((None, D, D), lambda i: (0, 0, 0)),
                  pl.BlockSpec((None, D, D), lambda i: (1, 0, 0))],
        out_specs=pl.BlockSpec((L, 2 * DA), lambda i: (0, 0)),
    )(hc, w4, w4)


def _head_ones():
    r = lax.broadcasted_iota(i32, (DA, DA), 0) // DH
    c = lax.broadcasted_iota(i32, (DA, DA), 1) // DH
    return (r == c).astype(bf16)


def _head_sum(v, ones_bd):
    hi = v.astype(bf16)
    lo = (v - hi.astype(f32)).astype(bf16)
    return jnp.dot(hi, ones_bd, preferred_element_type=f32) + jnp.dot(lo, ones_bd, preferred_element_type=f32)


def _swap16(v):
    lane = lax.broadcasted_iota(i32, v.shape, 1)
    return jnp.where((lane & 31) < 16, pltpu.roll(v, DA - 16, 1), pltpu.roll(v, 16, 1))


def _rope_block(ct_ref, rt_ref, tm):
    rows = [jnp.tile(rt_ref[8 * j:8 * j + 8, :], (GW // 8, 1)) for j in range(tm // GW)]
    return jnp.tile(ct_ref[...], (tm // GW, 1)) + jnp.concatenate(rows, axis=0)


def _rope_specs(tm):
    col = pl.BlockSpec((GW, DA), lambda i: (0, 0))
    row = pl.BlockSpec((8 * tm // GW, DA), lambda i: (i, 0))
    return [col, row, col, row]


def _qk_prep(p, gq, gk, rope):
    tm = 256

    def body(qk_ref, v_ref, gq_ref, gk_ref, cc_ref, cr_ref, sc_ref, sr_ref, qr_ref, qp_ref, kr_ref, vh_ref):
        ones_bd = _head_ones()
        cs, sn = _rope_block(cc_ref, cr_ref, tm), _rope_block(sc_ref, sr_ref, tm)
        q = qk_ref[:, 0:DA]
        k = qk_ref[:, DA:2 * DA]
        yq = (q * lax.rsqrt(_head_sum(q * q, ones_bd) * (1.0 / DH) + RMS_EPS)) * gq_ref[...]
        yk = (k * lax.rsqrt(_head_sum(k * k, ones_bd) * (1.0 / DH) + RMS_EPS)) * gk_ref[...]
        qr = (yq * cs + _swap16(yq) * sn) * QK_SCALE
        qp = yq * QK_SCALE
        kr = yk * cs + _swap16(yk) * sn
        vv = v_ref[...]
        for hh in range(H):
            sl = slice(hh * DH, (hh + 1) * DH)
            qr_ref[hh] = qr[:, sl].astype(bf16)
            qp_ref[hh] = qp[:, sl].astype(bf16)
            kr_ref[hh] = kr[:, sl].astype(bf16)
            vh_ref[hh] = vv[:, sl].astype(bf16)

    hm = SDS((H, S, DH), bf16)
    hspec = pl.BlockSpec((H, tm, DH), lambda i: (0, i, 0))
    fixed = lambda i: (0, 0)
    return _hbm_call(
        body, name="qk_prep", out_shape=(hm, hm, hm, hm), grid=(S // tm,),
        in_specs=[pl.BlockSpec((tm, 2 * DA), lambda i: (i, 0)), pl.BlockSpec((tm, DA), lambda i: (i, 2)),
                  pl.BlockSpec((1, DA), fixed), pl.BlockSpec((1, DA), fixed)] + _rope_specs(tm),
        out_specs=(hspec, hspec, hspec, hspec),
    )(p, p, gq, gk, *rope)


def _ctx_prep(pc, gk):
    def body(p_ref, gk_ref, kc_ref, vc_ref):
        ones_bd = _head_ones()
        k = p_ref[:, 0:DA]
        yk = (k * lax.rsqrt(_head_sum(k * k, ones_bd) * (1.0 / DH) + RMS_EPS)) * gk_ref[...]
        vv = p_ref[:, DA:2 * DA]
        for hh in range(H):
            sl = slice(hh * DH, (hh + 1) * DH)
            kc_ref[hh] = yk[:, sl].astype(bf16)
            vc_ref[hh] = vv[:, sl].astype(bf16)

    hm = SDS((H, L, DH), bf16)
    return _hbm_call(
        body, name="ctx_prep", out_shape=(hm, hm), in_specs=[VMEM_SPEC, VMEM_SPEC], out_specs=(VMEM_SPEC, VMEM_SPEC),
    )(pc, gk)


def _tile_pieces():
    out = []
    for (i0, u0) in TILE_GEOM:
        rows = []
        for j in range(2):
            i = i0 + j
            rs = _row_start(i)
            rows.append([(u0 + u - i + WIN_H - 1) if rs <= u0 + u < rs + WIN_H else None for u in range(KR)])
        out.append(rows)
    return out


def _bias_prep(rpb_rev_pad, after=None):
    pieces = _tile_pieces()

    def body(r_ref, after_ref, o_ref):
        rp = r_ref[...]
        xs = jnp.broadcast_to(rp[:, None, :], (N_DR, GW, 128)).reshape(N_DR * GW, 128)
        row = lax.broadcasted_iota(i32, xs.shape, 0)
        lane = lax.broadcasted_iota(i32, xs.shape, 1)
        for b in range(6):
            xs = jnp.where(((row >> b) & 1) == 1, pltpu.roll(xs, 1 << b, 1), xs)
        xs = pltpu.roll(xs, 128 - (WIN_W - 1), 1)
        k = row & (GW - 1)
        c0 = jnp.clip(lane - WIN_W // 2, 0, GW - WIN_W)
        xs = jnp.where((k >= c0) & (k < c0 + WIN_W), xs, NEG)
        neg = jnp.full((GW, GW), NEG, f32)
        for t in range(NT):
            for j in range(2):
                for u in range(KR):
                    dr = pieces[t][j][u]
                    piece = neg if dr is None else xs[dr * GW:(dr + 1) * GW, 0:GW]
                    o_ref[t, u * GW:(u + 1) * GW, j * GW:(j + 1) * GW] = piece

    return _hbm_call(
        body, name="bias_prep", out_shape=SDS((H, NT, KB, QB), f32), grid=(H,),
        in_specs=[pl.BlockSpec((None, N_DR, 128), lambda h: (h, 0, 0)), ANY_SPEC],
        out_specs=pl.BlockSpec((None, NT, KB, QB), lambda h: (h, 0, 0, 0)),
    )(rpb_rev_pad, rpb_rev_pad if after is None else after)


def _bias_tiles(rpb2, after=None):
    return _bias_prep(jnp.pad(rpb2[:, :, ::-1], ((0, 0), (0, 0), (0, 128 - N_DC))), after)


def _block_geom(b):
    qs = b * QB
    ks = min(max(2 * b - 4, 0), ROWS - KR) * GW
    t = b if b < 2 else (b - (NQB - NT) if b > NQB - 3 else 2)
    return qs, ks, t


def _tt(a, b):
    return lax.dot_general(a, b, (((1,), (1,)), ((), ())), preferred_element_type=f32)


def _tn(a, b):
    return lax.dot_general(a, b, (((0,), (0,)), ((), ())), preferred_element_type=f32)


def _softmax_t(s_lat, s_ctx):
    m = jnp.maximum(jnp.max(s_lat, axis=0, keepdims=True), jnp.max(s_ctx, axis=0, keepdims=True))
    e_lat = jnp.exp(s_lat - m)
    e_ctx = jnp.exp(s_ctx - m)
    inv = 1.0 / (jnp.sum(e_lat, axis=0, keepdims=True) + jnp.sum(e_ctx, axis=0, keepdims=True))
    return e_lat * inv, e_ctx * inv


def _staged(n_blocks, stages):
    held = [dict() for _ in stages]
    for step in range(n_blocks + len(stages) - 1):
        for s, fn in enumerate(stages):
            b = step - s
            if 0 <= b < n_blocks:
                held[s][b] = fn(b) if s == 0 else fn(b, held[s - 1].pop(b))


def _attn_fwd(qr, qp, kr, vh, kc, vc, btt):
    def body(qr_ref, qp_ref, kr_ref, v_ref, kc_ref, vc_ref, bt_ref, o_ref):
        kcv, vcv = kc_ref[...], vc_ref[...]

        def scores(b):
            qs, ks, t = _block_geom(b)
            return (_tt(kr_ref[ks:ks + KB, :], qr_ref[qs:qs + QB, :]) + bt_ref[t], _tt(kcv, qp_ref[qs:qs + QB, :]))

        def probs(b, sc):
            p_lat, p_ctx = _softmax_t(*sc)
            return p_lat.astype(bf16), p_ctx.astype(bf16)

        def values(b, p):
            qs, ks, _ = _block_geom(b)
            o_ref[qs:qs + QB, :] = _tn(p[0], v_ref[ks:ks + KB, :]) + _tn(p[1], vcv)

        _staged(NQB, (scores, probs, values))

    sq = pl.BlockSpec((None, S, DH), lambda h: (h, 0, 0))
    sc = pl.BlockSpec((None, L, DH), lambda h: (h, 0, 0))
    return _hbm_call(
        body, name="attn_fwd", out_shape=SDS((H, S, DH), f32), grid=(H,),
        in_specs=[sq, sq, sq, sq, sc, sc, pl.BlockSpec((None, NT, KB, QB), lambda h: (h, 0, 0, 0))],
        out_specs=sq, compiler_params=_cp(48),
    )(qr, qp, kr, vh, kc, vc, btt)


def _shift_rows(v, down):
    n = v.shape[0]
    row = lax.broadcasted_iota(i32, v.shape, 0)
    if down:
        return jnp.where(row == 0, 0.0, pltpu.roll(v, 1, 0))
    return jnp.where(row == n - 1, 0.0, pltpu.roll(v, n - 1, 0))


def _conv_specs():
    col = lambda off: pl.BlockSpec((S, 128), lambda i, off=off: (0, off + i))
    return [col(16), col(20), col(24), col(28), pl.BlockSpec((3, 128), lambda i: (0, i)),
            pl.BlockSpec((1, 128), lambda i: (0, i))]


def _conv_fwd(p, conv_w, conv_b, after=None):
    def body(u_ref, bg_ref, cg_ref, zc_ref, w_ref, b_ref, after_ref, o_ref):
        cu = cg_ref[...] * u_ref[...]
        cv = b_ref[...] + _shift_rows(cu, True) * w_ref[0:1, :]
        cv = cv + cu * w_ref[1:2, :]
        cv = cv + _shift_rows(cu, False) * w_ref[2:3, :]
        o_ref[...] = ((bg_ref[...] * cv) * _silu(zc_ref[...])).astype(bf16)

    return _hbm_call(
        body, name="conv_fwd", out_shape=SDS((S, DC), bf16), grid=(DC // 128,),
        in_specs=_conv_specs() + [ANY_SPEC], out_specs=pl.BlockSpec((S, 128), lambda i: (0, i)),
        compiler_params=_cp(40),
    )(p, p, p, p, conv_w, conv_b, conv_b if after is None else after)


DP_Q, DP_K, DP_V, DP_ZA, DP_U, DP_BG, DP_CG, DP_ZC = range(8)


def _out_proj_loss(o, p, conv_g, w_out, xx, tgt, mrow, b_ada):
    tm = 256

    def body(o_ref, za_ref, c_ref, w_ref, x_ref, t_ref, m_ref, b_ref,
             dy_ref, dconv_ref, dp_ref, do_ref, gwo_ref, dgate_ref, loss_ref):
        k = pl.program_id(0)

        @pl.when(k == 0)
        def _():
            gwo_ref[...] = jnp.zeros_like(gwo_ref)
            dgate_ref[...] = jnp.zeros_like(dgate_ref)
            loss_ref[0, 0] = 0.0

        gate = m_ref[:, 2 * D:3 * D] + b_ref[:, 2 * D:3 * D]
        za = za_ref[...]
        sz = _silu(za)
        om = _merge_heads(o_ref)
        av, cv = (om * sz).astype(bf16), c_ref[...]
        mo = jnp.dot(av, w_ref[0:DA, :], preferred_element_type=f32)
        mo = mo + jnp.dot(cv, w_ref[DA:DA + DC, :], preferred_element_type=f32)
        y = x_ref[...] + gate * mo
        diff = y - t_ref[...]
        loss_ref[0, 0] += jnp.sum(diff * diff)
        dy = diff * (1.0 / D)
        dy_ref[...] = dy
        dgate_ref[...] += jnp.sum(dy * mo, axis=0, keepdims=True)
        dmo = (dy * gate).astype(bf16)
        dmix = _tt(dmo, w_ref[...])
        dattn = dmix[:, 0:DA]
        dconv_ref[...] = dmix[:, DA:DA + DC]
        a = dattn * sz
        for hh in range(H):
            do_ref[hh] = a[:, hh * DH:(hh + 1) * DH].astype(bf16)
        dp_ref[...] = ((dattn * _dsilu(za)) * om).astype(bf16)
        gwo_ref[0:DA, :] += _tn(av, dmo)
        gwo_ref[DA:DA + DC, :] += _tn(cv, dmo)

    row = lambda i: (i, 0)
    fixed = lambda i: (0, 0)
    hspec = pl.BlockSpec((H, tm, DH), lambda i: (0, i, 0))
    return _hbm_call(
        body, name="out_proj_loss",
        out_shape=(SDS((S, D), f32), SDS((S, DC), f32), SDS((8, S, DA), bf16), SDS((H, S, DH), bf16),
                   SDS((D, D), f32), SDS((1, D), f32), SDS((1, 1), f32)),
        grid=(S // tm,),
        in_specs=[hspec, pl.BlockSpec((tm, DA), lambda i: (i, 3)), pl.BlockSpec((tm, DC), row),
                  pl.BlockSpec((D, D), fixed), pl.BlockSpec((tm, D), row), pl.BlockSpec((tm, D), row),
                  pl.BlockSpec((1, 3 * D), fixed), pl.BlockSpec((1, 3 * D), fixed)],
        out_specs=(pl.BlockSpec((tm, D), row), pl.BlockSpec((tm, DC), row),
                   pl.BlockSpec((None, tm, DA), lambda i: (DP_ZA, i, 0)), hspec, pl.BlockSpec((D, D), fixed),
                   pl.BlockSpec((1, D), fixed), SMEM_SPEC),
        compiler_params=_cp(56, dimension_semantics=("arbitrary",)),
    )(o, p, conv_g, w_out, xx, tgt, mrow, b_ada)


def _conv_bwd(dconv, p, conv_w, conv_b, dp8, after=None):
    def body(d_ref, u_ref, bg_ref, cg_ref, zc_ref, w_ref, b_ref, dp_in_ref, after_ref, dp_ref, gw_ref, gb_ref):
        du_ref, dbg_ref, dcg_ref, dzc_ref = dp_ref.at[0], dp_ref.at[1], dp_ref.at[2], dp_ref.at[3]
        dconv = d_ref[...]
        u, bg, cg, zc = u_ref[...], bg_ref[...], cg_ref[...], zc_ref[...]
        w0, w1, w2 = w_ref[0:1, :], w_ref[1:2, :], w_ref[2:3, :]
        cu = cg * u
        cu_m, cu_p = _shift_rows(cu, True), _shift_rows(cu, False)
        cv = b_ref[...] + cu_m * w0
        cv = cv + cu * w1
        cv = cv + cu_p * w2
        sz = _silu(zc)
        dbg_ref[...] = ((dconv * sz) * cv).astype(bf16)
        dzc_ref[...] = ((dconv * (bg * cv)) * _dsilu(zc)).astype(bf16)
        dcv = (dconv * sz) * bg
        gb_ref[...] = jnp.sum(dcv, axis=0, keepdims=True)
        gw_ref[0:1, :] = jnp.sum(dcv * cu_m, axis=0, keepdims=True)
        gw_ref[1:2, :] = jnp.sum(dcv * cu, axis=0, keepdims=True)
        gw_ref[2:3, :] = jnp.sum(dcv * cu_p, axis=0, keepdims=True)
        gw_ref[3:8, :] = jnp.zeros((5, 128), f32)
        dcu = _shift_rows(dcv, False) * w0 + dcv * w1 + _shift_rows(dcv, True) * w2
        dcg_ref[...] = (dcu * u).astype(bf16)
        du_ref[...] = (dcu * cg).astype(bf16)

    return _hbm_call(
        body, name="conv_bwd", out_shape=(SDS((8, S, DC), bf16), SDS((8, DC), f32), SDS((1, DC), f32)),
        grid=(DC // 128,),
        in_specs=[pl.BlockSpec((S, 128), lambda i: (0, i))] + _conv_specs() + [ANY_SPEC, ANY_SPEC],
        out_specs=(pl.BlockSpec((4, S, 128), lambda i: (DP_U // 4, 0, i)), pl.BlockSpec((8, 128), lambda i: (0, i)),
                   pl.BlockSpec((1, 128), lambda i: (0, i))),
        input_output_aliases={7: 0}, compiler_params=_cp(48),
    )(dconv, p, p, p, p, conv_w, conv_b, dp8, conv_b if after is None else after)


def _attn_bwd(qr, qp, kr, vh, kc, vc, btt, do, after=None):
    def body(qr_ref, qp_ref, kr_ref, v_ref, kc_ref, vc_ref, bt_ref, do_ref, after_ref,
             dqr_ref, dqp_ref, dkr_ref, dv_ref, dkc_ref, dvc_ref, dbt_ref):
        kcv, vcv = kc_ref[...], vc_ref[...]
        dkr_ref[...] = jnp.zeros_like(dkr_ref)
        dv_ref[...] = jnp.zeros_like(dv_ref)
        dbt_ref[...] = jnp.zeros_like(dbt_ref)
        ctx_acc = {}

        def products(b):
            qs, ks, t = _block_geom(b)
            dob = do_ref[qs:qs + QB, :]
            s_lat = _tt(kr_ref[ks:ks + KB, :], qr_ref[qs:qs + QB, :]) + bt_ref[t]
            s_ctx = _tt(kcv, qp_ref[qs:qs + QB, :])
            return s_lat, s_ctx, _tt(v_ref[ks:ks + KB, :], dob), _tt(vcv, dob)

        def score_grads(b, x):
            s_lat, s_ctx, dp_lat, dp_ctx = x
            p_lat, p_ctx = _softmax_t(s_lat, s_ctx)
            delta = jnp.sum(p_lat * dp_lat, axis=0, keepdims=True) + jnp.sum(p_ctx * dp_ctx, axis=0, keepdims=True)
            ds_lat = p_lat * (dp_lat - delta)
            ds_ctx = p_ctx * (dp_ctx - delta)
            return ds_lat, ds_lat.astype(bf16), ds_ctx.astype(bf16), p_lat.astype(bf16), p_ctx.astype(bf16)

        def operand_grads(b, y):
            qs, ks, t = _block_geom(b)
            ds_lat, dsb_lat, dsb_ctx, pb_lat, pb_ctx = y
            qrb, qpb, dob = qr_ref[qs:qs + QB, :], qp_ref[qs:qs + QB, :], do_ref[qs:qs + QB, :]
            dbt_ref[t] += ds_lat
            dqr_ref[qs:qs + QB, :] = _tn(dsb_lat, kr_ref[ks:ks + KB, :])
            dqp_ref[qs:qs + QB, :] = _tn(dsb_ctx, kcv)
            dkr_ref[ks:ks + KB, :] += jnp.dot(dsb_lat, qrb, preferred_element_type=f32)
            dv_ref[ks:ks + KB, :] += jnp.dot(pb_lat, dob, preferred_element_type=f32)
            dkc = jnp.dot(dsb_ctx, qpb, preferred_element_type=f32)
            dvc = jnp.dot(pb_ctx, dob, preferred_element_type=f32)
            ctx_acc["k"] = dkc if b == 0 else ctx_acc["k"] + dkc
            ctx_acc["v"] = dvc if b == 0 else ctx_acc["v"] + dvc

        _staged(NQB, (products, score_grads, operand_grads))
        dkc_ref[...] = ctx_acc["k"]
        dvc_ref[...] = ctx_acc["v"]

    sq = pl.BlockSpec((None, S, DH), lambda h: (h, 0, 0))
    sc = pl.BlockSpec((None, L, DH), lambda h: (h, 0, 0))
    sb = pl.BlockSpec((None, NT, KB, QB), lambda h: (h, 0, 0, 0))
    big, ctxs = SDS((H, S, DH), f32), SDS((H, L, DH), f32)
    return _hbm_call(
        body, name="attn_bwd", out_shape=(big, big, big, big, ctxs, ctxs, SDS((H, NT, KB, QB), f32)), grid=(H,),
        in_specs=[sq, sq, sq, sq, sc, sc, sb, sq, ANY_SPEC], out_specs=(sq, sq, sq, sq, sc, sc, sb),
        compiler_params=_cp(56),
    )(qr, qp, kr, vh, kc, vc, btt, do, do if after is None else after)


def _bias_bwd(dbtt, after=None):
    pieces = _tile_pieces()

    def body(d_ref, after_ref, o_ref, scr):
        scr[...] = jnp.zeros_like(scr)
        acc = [None] * N_DR
        for t in range(NT):
            for j in range(2):
                for u in range(KR):
                    dr = pieces[t][j][u]
                    if dr is None:
                        continue
                    piece = d_ref[t, u * GW:(u + 1) * GW, j * GW:(j + 1) * GW]
                    acc[dr] = piece if acc[dr] is None else acc[dr] + piece
        for dr in range(N_DR):
            scr[dr * GW:(dr + 1) * GW, 0:GW] = acc[dr]
        xs = pltpu.roll(scr[...], WIN_W - 1, 1)
        row = lax.broadcasted_iota(i32, xs.shape, 0)
        for b in range(6):
            xs = jnp.where(((row >> b) & 1) == 1, pltpu.roll(xs, 128 - (1 << b), 1), xs)
        rev = jnp.sum(xs.reshape(N_DR, GW, 128), axis=1)
        a = lax.broadcasted_iota(i32, (128, 128), 0)
        b = lax.broadcasted_iota(i32, (128, 128), 1)
        flip = ((a + b == N_DC - 1) & (a < N_DC)).astype(f32)
        o_ref[...] = jnp.dot(rev, flip, precision=HIGHEST, preferred_element_type=f32)

    return _hbm_call(
        body, name="bias_bwd", out_shape=SDS((H, N_DR, 128), f32), grid=(H,),
        in_specs=[pl.BlockSpec((None, NT, KB, QB), lambda h: (h, 0, 0, 0)), ANY_SPEC],
        out_specs=pl.BlockSpec((None, N_DR, 128), lambda h: (h, 0, 0)),
        scratch_shapes=[pltpu.VMEM((N_DR * GW, 128), f32)],
    )(dbtt, dbtt if after is None else after)


def _merge_heads(ref):
    return jnp.concatenate([ref[hh] for hh in range(H)], axis=1)


def _head_norm_bwd(xraw, gain, dy, ones_bd):
    r = lax.rsqrt(_head_sum(xraw * xraw, ones_bd) * (1.0 / DH) + RMS_EPS)
    xh = xraw * r
    gdy = dy * gain
    dx = r * (gdy - xh * (_head_sum(xh * gdy, ones_bd) * (1.0 / DH)))
    return dx, jnp.sum(dy * xh, axis=0, keepdims=True)


def _qk_bwd(dqr, dqp, dkr, dvh, p, gq, gk, rope, dp8):
    tm = 256

    def body(dqr_ref, dqp_ref, dkr_ref, dv_ref, qk_ref, gq_ref, gk_ref, cc_ref, cr_ref, sc_ref, sr_ref, dp_in_ref,
             dp_ref, ggq_ref, ggk_ref):
        dq_ref, dk_ref, dvo_ref = dp_ref.at[DP_Q], dp_ref.at[DP_K], dp_ref.at[DP_V]

        @pl.when(pl.program_id(0) == 0)
        def _():
            ggq_ref[...] = jnp.zeros_like(ggq_ref)
            ggk_ref[...] = jnp.zeros_like(ggk_ref)

        ones_bd = _head_ones()
        cs, sn = _rope_block(cc_ref, cr_ref, tm), _rope_block(sc_ref, sr_ref, tm)
        a = _merge_heads(dqr_ref)
        dyq = ((a * cs - _swap16(a) * sn) + _merge_heads(dqp_ref)) * QK_SCALE
        bk = _merge_heads(dkr_ref)
        dyk = bk * cs - _swap16(bk) * sn
        dq, gq_part = _head_norm_bwd(qk_ref[:, 0:DA], gq_ref[...], dyq, ones_bd)
        dk, gk_part = _head_norm_bwd(qk_ref[:, DA:2 * DA], gk_ref[...], dyk, ones_bd)
        dq_ref[...] = dq.astype(bf16)
        dk_ref[...] = dk.astype(bf16)
        dvo_ref[...] = _merge_heads(dv_ref).astype(bf16)
        ggq_ref[...] += gq_part
        ggk_ref[...] += gk_part

    hspec = pl.BlockSpec((H, tm, DH), lambda i: (0, i, 0))
    fixed = pl.BlockSpec((1, DA), lambda i: (0, 0))
    return _hbm_call(
        body, name="qk_bwd", out_shape=(SDS((8, S, DA), bf16), SDS((1, DA), f32), SDS((1, DA), f32)), grid=(S // tm,),
        in_specs=[hspec, hspec, hspec, hspec, pl.BlockSpec((tm, 2 * DA), lambda i: (i, 0)), fixed, fixed]
        + _rope_specs(tm) + [ANY_SPEC],
        out_specs=(pl.BlockSpec((3, tm, DA), lambda i: (0, i, 0)), fixed, fixed), input_output_aliases={11: 0},
        compiler_params=_cp(40, dimension_semantics=("arbitrary",)),
    )(dqr, dqp, dkr, dvh, p, gq, gk, *rope, dp8)


def _ctx_bwd(dkc, dvc, pc, gk):
    def body(dkc_ref, dvc_ref, p_ref, gk_ref, dk_ref, dv_ref, ggk_ref):
        ones_bd = _head_ones()
        dk, gk_part = _head_norm_bwd(p_ref[:, 0:DA], gk_ref[...], _merge_heads(dkc_ref), ones_bd)
        dk_ref[...] = dk.astype(bf16)
        dv_ref[...] = _merge_heads(dvc_ref).astype(bf16)
        ggk_ref[...] = gk_part

    piece = SDS((L, DA), bf16)
    return _hbm_call(
        body, name="ctx_bwd", out_shape=(piece, piece, SDS((1, DA), f32)), in_specs=[VMEM_SPEC] * 4,
        out_specs=(VMEM_SPEC,) * 3,
    )(dkc, dvc, pc, gk)


def _grad_w_in(h, dp8, hc, dkc_raw, dvc_m):
    tm = 512

    def body(h_ref, p_ref, hc_ref, dk_ref, dv_ref, g_ref):
        j, k = pl.program_id(0), pl.program_id(1)

        @pl.when(k == 0)
        def _():
            g_ref[...] = jnp.zeros_like(g_ref)

        @pl.when((k == 0) & (j == 0))
        def _():
            g_ref[:, DA:2 * DA] = _tn(hc_ref[...], dk_ref[...])

        @pl.when((k == 0) & (j == 1))
        def _():
            g_ref[:, 0:DA] = _tn(hc_ref[...], dv_ref[...])

        hv = h_ref[...]
        g_ref[:, 0:DA] += _tn(hv, p_ref[0])
        g_ref[:, DA:2 * DA] += _tn(hv, p_ref[1])

    fixed = lambda j, k: (0, 0)
    return _hbm_call(
        body, name="grad_w_in", out_shape=SDS((4, D, D), f32), grid=(4, S // tm),
        in_specs=[pl.BlockSpec((tm, D), lambda j, k: (k, 0)), pl.BlockSpec((2, tm, DA), lambda j, k: (j, k, 0)),
                  pl.BlockSpec((L, D), fixed), pl.BlockSpec((L, DA), fixed), pl.BlockSpec((L, DA), fixed)],
        out_specs=pl.BlockSpec((None, D, D), lambda j, k: (j, 0, 0)),
        compiler_params=_cp(40, dimension_semantics=("arbitrary", "arbitrary")),
    )(h, dp8, hc, dkc_raw, dvc_m)


def _norm_mod_bwd(x, dh, g, scale):
    r = lax.rsqrt(jnp.mean(x * x, axis=-1, keepdims=True) + RMS_EPS)
    xh = x * r
    y = xh * g
    dshift = jnp.sum(dh, axis=0, keepdims=True)
    dscale = jnp.sum(dh * y, axis=0, keepdims=True)
    dyn = dh * (1.0 + scale)
    dg = jnp.sum(dyn * xh, axis=0, keepdims=True)
    gdy = dyn * g
    dx = r * (gdy - xh * jnp.mean(xh * gdy, axis=-1, keepdims=True))
    return dx, dshift, dscale, dg


def _dh_grad_x(dp8, w4, xx, dy, norm_g, mrow, b_ada, after=None):
    tm = 256

    def body(p_ref, w_ref, x_ref, dy_ref, g_ref, m_ref, b_ref, after_ref, gx_ref, dsh_ref, dsc_ref, dg_ref):
        @pl.when(pl.program_id(0) == 0)
        def _():
            dsh_ref[...] = jnp.zeros_like(dsh_ref)
            dsc_ref[...] = jnp.zeros_like(dsc_ref)
            dg_ref[...] = jnp.zeros_like(dg_ref)

        dh = None
        for j in range(4):
            for half in range(2):
                term = _tt(p_ref[2 * j + half], w_ref[j, :, half * DA:(half + 1) * DA])
                dh = term if dh is None else dh + term
        scale = m_ref[:, D:2 * D] + b_ref[:, D:2 * D]
        dx, dshift, dscale, dg = _norm_mod_bwd(x_ref[...], dh, g_ref[...], scale)
        gx_ref[...] = dy_ref[...] + dx
        dsh_ref[...] += dshift
        dsc_ref[...] += dscale
        dg_ref[...] += dg

    row = lambda i: (i, 0)
    fixed = lambda i: (0, 0)
    vec = SDS((1, D), f32)
    return _hbm_call(
        body, name="dh_grad_x", out_shape=(SDS((S, D), f32), vec, vec, vec), grid=(S // tm,),
        in_specs=[pl.BlockSpec((8, tm, DA), lambda i: (0, i, 0)), pl.BlockSpec((4, D, D), lambda i: (0, 0, 0)),
                  pl.BlockSpec((tm, D), row), pl.BlockSpec((tm, D), row), pl.BlockSpec((1, D), fixed),
                  pl.BlockSpec((1, 3 * D), fixed), pl.BlockSpec((1, 3 * D), fixed), ANY_SPEC],
        out_specs=(pl.BlockSpec((tm, D), row), pl.BlockSpec((1, D), fixed), pl.BlockSpec((1, D), fixed),
                   pl.BlockSpec((1, D), fixed)),
        compiler_params=_cp(56, dimension_semantics=("arbitrary",)),
    )(dp8, w4, xx, dy, norm_g, mrow, b_ada, b_ada if after is None else after)


def _dhc_sums(dkc_raw, dvc_m, w4, ctx2, norm_g, mrow_c, b_ada, after=None):
    def body(dk_ref, dv_ref, w0_ref, w1_ref, x_ref, g_ref, m_ref, b_ref, after_ref, dsh_ref, dsc_ref, dg_ref):
        dh = _tt(dk_ref[...], w0_ref[:, DA:2 * DA]) + _tt(dv_ref[...], w1_ref[:, 0:DA])
        scale = m_ref[:, D:2 * D] + b_ref[:, D:2 * D]
        _, dshift, dscale, dg = _norm_mod_bwd(x_ref[...], dh, g_ref[...], scale)
        dsh_ref[...] = dshift
        dsc_ref[...] = dscale
        dg_ref[...] = dg

    fixed = lambda i: (0, 0)
    vec = SDS((1, D), f32)
    vspec = pl.BlockSpec((1, D), fixed)
    return _hbm_call(
        body, name="dhc_sums", out_shape=(vec, vec, vec), grid=(1,),
        in_specs=[pl.BlockSpec((L, DA), fixed), pl.BlockSpec((L, DA), fixed),
                  pl.BlockSpec((None, D, D), lambda i: (0, 0, 0)), pl.BlockSpec((None, D, D), lambda i: (1, 0, 0)),
                  pl.BlockSpec((L, D), fixed), vspec, pl.BlockSpec((1, 3 * D), fixed), pl.BlockSpec((1, 3 * D), fixed),
                  ANY_SPEC],
        out_specs=(vspec, vspec, vspec), compiler_params=_cp(32),
    )(dkc_raw, dvc_m, w4, w4, ctx2, norm_g, mrow_c, b_ada, b_ada if after is None else after)


def _rope_tables():
    nf = DH // 4
    inv = np.float32(ROPE_THETA) ** (-np.arange(nf, dtype=np.float32) / np.float32(nf))
    ang_c = np.arange(GW, dtype=np.float32)[:, None] * inv
    ang_r = np.arange(ROWS, dtype=np.float32)[:, None] * inv
    zc, zr = np.zeros((GW, 2 * nf), np.float32), np.zeros((ROWS, 2 * nf), np.float32)
    ct_cos = np.tile(np.concatenate([zc, np.cos(ang_c), np.cos(ang_c)], axis=1), (1, H))
    ct_sin = np.tile(np.concatenate([zc, -np.sin(ang_c), np.sin(ang_c)], axis=1), (1, H))
    rt_cos = np.tile(np.concatenate([np.cos(ang_r), np.cos(ang_r), zr], axis=1), (1, H))
    rt_sin = np.tile(np.concatenate([-np.sin(ang_r), np.sin(ang_r), zr], axis=1), (1, H))
    rep8 = lambda t: np.ascontiguousarray(np.broadcast_to(t[:, None, :], (ROWS, 8, DA))).reshape(ROWS * 8, DA)
    return tuple(jnp.asarray(t, f32) for t in (ct_cos, rep8(rt_cos), ct_sin, rep8(rt_sin)))


def _local_step(xx, ctx2, tgt, mrow, mrow_c, b_ada, norm_g, weights, q_norm_g, k_norm_g, btb, conv_w_full, conv_b,
                hooks=None):
    hooks = hooks or {}
    gq = jnp.tile(q_norm_g, (1, H))
    gk = jnp.tile(k_norm_g, (1, H))
    rope = _rope_tables()

    h = _prenorm(xx, norm_g, mrow, b_ada, 256, "prenorm_x", after=weights.get("started"))
    hc = _prenorm(ctx2, norm_g, mrow_c, b_ada, L, "prenorm_ctx")
    jv = weights["jvec"]
    p = _in_proj_own(h, weights["own"], jv)
    w4, started = weights["near"](p)
    p = _in_proj_block(h, w4, p, weights["first"], "in_proj_near", after=started)
    w4 = weights["near2"](w4, p)
    p = _in_proj_block(h, w4, p, weights["second"], "in_proj_near2")
    w4, started = weights["far"](w4, p)
    p = _in_proj_block(h, w4, p, jv ^ 3, "in_proj_far", after=started)
    pc = _ctx_proj(hc, w4)
    qr, qp, kr, vh = _qk_prep(p, gq, gk, rope)
    kc, vc = _ctx_prep(pc, gk)
    o = _attn_fwd(qr, qp, kr, vh, kc, vc, btb)
    started = weights["out_arrived"](o) if "out_arrived" in weights else None
    conv_g = _conv_fwd(p, conv_w_full, conv_b, after=started)
    w_out_full = weights["out"](conv_g)
    dy, dconv, dp8, do, g_w_out, dgate, loss_sum = _out_proj_loss(o, p, conv_g, w_out_full, xx, tgt, mrow, b_ada)
    started = hooks["g_w_out"](g_w_out) if "g_w_out" in hooks else None
    dp8, g_conv_w, g_conv_b = _conv_bwd(dconv, p, conv_w_full, conv_b, dp8, after=started)
    started = hooks["after_conv"](dp8) if "after_conv" in hooks else None
    dqr, dqp, dkr, dvh, dkc, dvc, dbtb = _attn_bwd(qr, qp, kr, vh, kc, vc, btb, do, after=started)
    dp8, g_gq, g_gk = _qk_bwd(dqr, dqp, dkr, dvh, p, gq, gk, rope, dp8)
    dkc_raw, dvc_m, g_gk_c = _ctx_bwd(dkc, dvc, pc, gk)
    g_w_in = _grad_w_in(h, dp8, hc, dkc_raw, dvc_m)
    started = hooks["g_w_in"](g_w_in) if "g_w_in" in hooks else None
    dshift_c, dscale_c, dng_c = _dhc_sums(dkc_raw, dvc_m, w4, ctx2, norm_g, mrow_c, b_ada, after=started)
    g_rpb = _bias_bwd(dbtb, after=dshift_c)
    started = hooks["g_rpb"](g_rpb) if "g_rpb" in hooks else None
    grad_x, dshift, dscale, dng = _dh_grad_x(dp8, w4, xx, dy, norm_g, mrow, b_ada, after=started)
    return dict(loss_sum=loss_sum, grad_x=grad_x, g_w_in=g_w_in, g_w_out=g_w_out, g_conv_w=g_conv_w,
                g_conv_b=g_conv_b, g_rpb=g_rpb, g_gq=g_gq, g_gk=g_gk, g_gk_c=g_gk_c, dshift=dshift, dscale=dscale,
                dgate=dgate, dng=dng, dshift_c=dshift_c, dscale_c=dscale_c, dng_c=dng_c)


def _pair_sum_w_in(g, r, cvec):
    tr = 128

    def body(c_ref, g_ref, r_ref, t32_ref, tb_ref):
        t = g_ref[...] + r_ref[...]
        t32_ref[...] = t
        tb_ref[...] = t.astype(bf16)

    half = D // 2
    g_spec = pl.BlockSpec((4, tr, D), lambda i, c: (0, c[0] * (half // tr) + i, 0))
    o_spec = pl.BlockSpec((4, tr, D), lambda i, c: (0, i, 0))
    grid_spec = pltpu.PrefetchScalarGridSpec(num_scalar_prefetch=1, grid=(half // tr,), in_specs=[g_spec, o_spec],
                                             out_specs=(o_spec, o_spec))
    return _hbm_call(body, name="pair_sum_w_in", out_shape=(SDS((4, half, D), f32), SDS((4, half, D), bf16)),
                     grid_spec=grid_spec, compiler_params=_cp(40))(cvec, g, r)


def _pair_sum_w_out(g, r, cvec):
    hr = D // 8

    def body(c_ref, g0, g1, g2, g3, r_ref, t32_ref, tb_ref):
        for q, g_ref in enumerate((g0, g1, g2, g3)):
            t = g_ref[...] + r_ref[q]
            t32_ref[q] = t
            tb_ref[q] = t.astype(bf16)

    gspecs = [pl.BlockSpec((hr, D), lambda i, c, q=q: (2 * q + c[0], 0)) for q in range(4)]
    full = pl.BlockSpec((4, hr, D), lambda i, c: (0, 0, 0))
    grid_spec = pltpu.PrefetchScalarGridSpec(num_scalar_prefetch=1, grid=(1,), in_specs=gspecs + [full],
                                             out_specs=(full, full))
    return _hbm_call(body, name="pair_sum_w_out", out_shape=(SDS((4, hr, D), f32), SDS((4, hr, D), bf16)),
                          grid_spec=grid_spec)(cvec, g, g, g, g, r)


def _chip_sum(t32, r2, jvec, name):
    rows = t32.shape[1]
    tr = min(rows, 128)

    def body(j_ref, t_ref, r_ref, u_ref):
        u_ref[...] = ((t_ref[...] + r_ref[0].astype(f32)) + r_ref[1].astype(f32)) + r_ref[2].astype(f32)

    grid_spec = pltpu.PrefetchScalarGridSpec(
        num_scalar_prefetch=1, grid=(rows // tr,),
        in_specs=[pl.BlockSpec((None, tr, D), lambda i, j: (j[0], i, 0)), pl.BlockSpec((3, tr, D), lambda i, j: (0, i, 0))],
        out_specs=pl.BlockSpec((tr, D), lambda i, j: (i, 0)))
    return _hbm_call(body, name=name, out_shape=SDS((rows, D), f32), grid_spec=grid_spec)(jvec, t32, r2)


_PK = {}
_off = 0
for _name, _rows in (("dm", 24), ("dmc", 24), ("dng", 8), ("dng_c", 8), ("gq", 8), ("gk", 8), ("gk_c", 8),
                     ("rpb", H * N_DR), ("conv_b", 8), ("conv_w", 16), ("loss", 8)):
    _PK[_name] = (_off, _off + _rows)
    _off += _rows
PK_ROWS = _off
RS_B_ADA, RS_NORM_G, RS_GQ, RS_GK, RS_RPB, RS_CONV_B, RS_CONV_W, RS_DMC, RS_LOSS, RS_ROWS = (
    0, 24, 32, 40, 48, 168, 176, 192, 216, 224)


def _small_reduce(gathered):
    def body(g_ref, o_ref, dm_ref):
        a0 = _PK["dm"][0]
        dm_ref[...] = jnp.zeros_like(dm_ref)
        for b in range(8):
            for i in range(24):
                dm_ref[b:b + 1, 128 * i:128 * (i + 1)] = g_ref[b, a0 + i:a0 + i + 1, :]
        tot = g_ref[0]
        for b in range(1, 8):
            tot = tot + g_ref[b]

        def rows(name):
            a, z = _PK[name]
            return tot[a:z]

        o_ref[RS_B_ADA:RS_B_ADA + 24] = rows("dm") + rows("dmc")
        o_ref[RS_NORM_G:RS_NORM_G + 8] = rows("dng") + rows("dng_c")
        gq = jnp.broadcast_to(jnp.sum(rows("gq"), axis=0, keepdims=True), (8, 128))
        gk = jnp.broadcast_to(jnp.sum(rows("gk") + rows("gk_c"), axis=0, keepdims=True), (8, 128))
        o_ref[RS_GQ:RS_GQ + 8] = gq + pltpu.roll(gq, DH, 1)
        o_ref[RS_GK:RS_GK + 8] = gk + pltpu.roll(gk, DH, 1)
        o_ref[RS_RPB:RS_RPB + H * N_DR] = rows("rpb")
        o_ref[RS_CONV_B:RS_CONV_B + 8] = rows("conv_b")
        o_ref[RS_CONV_W:RS_CONV_W + 16] = rows("conv_w")
        dmc = rows("dmc")
        o_ref[RS_DMC:RS_DMC + 24] = dmc
        o_ref[RS_LOSS:RS_LOSS + 8] = rows("loss")
        for i in range(24):
            dm_ref[8:9, 128 * i:128 * (i + 1)] = dmc[i:i + 1]

    return _hbm_call(body, name="small_reduce", out_shape=(SDS((RS_ROWS, 128), f32), SDS((16, 3 * D), f32)),
                     in_specs=[VMEM_SPEC], out_specs=(VMEM_SPEC, VMEM_SPEC))(gathered)


def _w_ada_grad(sc16, dm16, w_ada_shard, jvec):
    ncol = w_ada_shard.shape[1]

    def body(j_ref, sc_ref, dm_ref, w_ref, g_ref, part_ref):
        dm = dm_ref[...]
        g_ref[...] = lax.dot_general(sc_ref[...], dm, (((0,), (0,)), ((), ())), precision=HIGHEST,
                                     preferred_element_type=f32)
        part_ref[...] = lax.dot_general(dm[8:16], w_ref[...], (((1,), (1,)), ((), ())), precision=HIGHEST,
                                        preferred_element_type=f32)

    fixed = lambda i, j: (0, 0)
    grid_spec = pltpu.PrefetchScalarGridSpec(
        num_scalar_prefetch=1, grid=(1,),
        in_specs=[pl.BlockSpec((16, D), fixed), pl.BlockSpec((16, ncol), lambda i, j: (0, j[0])),
                  pl.BlockSpec((D, ncol), fixed)],
        out_specs=(pl.BlockSpec((D, ncol), fixed), pl.BlockSpec((8, D), fixed)))
    return _pallas_call(body, name="w_ada_grad", out_shape=(SDS((D, ncol), f32), SDS((8, D), f32)),
                        grid_spec=grid_spec, compiler_params=_cp(40))(jvec, sc16, dm16, w_ada_shard)


def _c_ctx_grad(parts4, c_ctx):
    def body(p_ref, c_ref, o_ref):
        tot = ((p_ref[0] + p_ref[1]) + p_ref[2]) + p_ref[3]
        o_ref[...] = tot[0:1] * _dsilu(c_ref[...].reshape(1, D))

    return _pallas_call(body, name="c_ctx_grad", out_shape=SDS((1, D), f32), in_specs=[VMEM_SPEC, VMEM_SPEC],
                        out_specs=VMEM_SPEC)(parts4, c_ctx)


def _adamw(w, g, m, v, name):
    rows, cols = w.shape
    tr = 256 if rows % 256 == 0 else rows

    def body(w_ref, g_ref, m_ref, v_ref, d_ref, m2_ref, v2_ref):
        gv = g_ref[...]
        m2 = ADAM_B1 * m_ref[...] + (1.0 - ADAM_B1) * gv
        v2 = ADAM_B2 * v_ref[...] + (1.0 - ADAM_B2) * jnp.square(gv)
        m_hat = m2 / (1.0 - ADAM_B1 ** ADAM_STEP)
        v_hat = v2 / (1.0 - ADAM_B2 ** ADAM_STEP)
        d_ref[...] = -ADAM_LR * (m_hat / (jnp.sqrt(v_hat) + ADAM_EPS) + ADAM_WD * w_ref[...])
        m2_ref[...] = m2
        v2_ref[...] = v2

    spec = pl.BlockSpec((tr, cols), lambda i: (i, 0))
    shp = SDS((rows, cols), f32)
    return _hbm_call(body, name=name, out_shape=(shp, shp, shp), grid=(rows // tr,), in_specs=[spec] * 4,
                          out_specs=(spec, spec, spec))(w, g, m, v)


def _adamw_halves(w, g_mine, g_other, m, v, cvec, name):
    rows, cols = w.shape
    half = rows // 2
    tr = min(256, half)
    per_half = half // tr

    def body(c_ref, w_ref, ga_ref, gb_ref, m_ref, v_ref, g_ref, d_ref, m2_ref, v2_ref):
        in_my_half = (pl.program_id(0) // per_half) == c_ref[0]
        gv = jnp.where(in_my_half, ga_ref[...], gb_ref[...])
        g_ref[...] = gv
        m2 = ADAM_B1 * m_ref[...] + (1.0 - ADAM_B1) * gv
        v2 = ADAM_B2 * v_ref[...] + (1.0 - ADAM_B2) * jnp.square(gv)
        m_hat = m2 / (1.0 - ADAM_B1 ** ADAM_STEP)
        v_hat = v2 / (1.0 - ADAM_B2 ** ADAM_STEP)
        d_ref[...] = -ADAM_LR * (m_hat / (jnp.sqrt(v_hat) + ADAM_EPS) + ADAM_WD * w_ref[...])
        m2_ref[...] = m2
        v2_ref[...] = v2

    full = pl.BlockSpec((tr, cols), lambda i, c: (i, 0))
    part = pl.BlockSpec((tr, cols), lambda i, c: (i % per_half, 0))
    shp = SDS((rows, cols), f32)
    grid_spec = pltpu.PrefetchScalarGridSpec(num_scalar_prefetch=1, grid=(rows // tr,),
                                             in_specs=[full, part, part, full, full], out_specs=(full,) * 4)
    return _hbm_call(body, name=name, out_shape=(shp,) * 4, grid_spec=grid_spec)(cvec, w, g_mine, g_other, m, v)


def _adam_math(w, g, m, v):
    m2 = ADAM_B1 * m + (1.0 - ADAM_B1) * g
    v2 = ADAM_B2 * v + (1.0 - ADAM_B2) * jnp.square(g)
    m_hat = m2 / (1.0 - ADAM_B1 ** ADAM_STEP)
    v_hat = v2 / (1.0 - ADAM_B2 ** ADAM_STEP)
    return -ADAM_LR * (m_hat / (jnp.sqrt(v_hat) + ADAM_EPS) + ADAM_WD * w), m2, v2


def _adamw_small(red, g_c_ctx, jvec, ws, ms, vs):
    n = len(ws)

    def body(*refs):
        red_ref, gc_ref, j_ref = refs[:3]
        w_refs, m_refs, v_refs = refs[3:3 + n], refs[3 + n:3 + 2 * n], refs[3 + 2 * n:3 + 3 * n]
        outs = refs[3 + 3 * n:]
        g_out, d_out, m_out, v_out = outs[:n], outs[n:2 * n], outs[2 * n:3 * n], outs[3 * n:]
        chip = j_ref[0]
        lanes = lambda i: (slice(None), slice(128 * i, 128 * (i + 1)))
        row = lambda r0, i: (lambda: red_ref[r0 + i:r0 + i + 1, :])
        whole = (slice(None), slice(None))
        chunks = [
            [((slice(None),), lambda: gc_ref[...].reshape(D))],
            [(lanes(i), row(RS_B_ADA, i)) for i in range(3 * D // 128)],
            [(lanes(i), row(RS_NORM_G, i)) for i in range(D // 128)],
            [(whole, lambda: red_ref[RS_GQ:RS_GQ + 1, 0:DH])],
            [(whole, lambda: red_ref[RS_GK:RS_GK + 1, 0:DH])],
            [((dr,), (lambda dr=dr: red_ref[pl.ds(RS_RPB + dr, H, stride=N_DR), 0:N_DC])) for dr in range(N_DR)],
            [((r,), (lambda r=r: red_ref[pl.ds(RS_CONV_W + 4 * r + chip, 1), :])) for r in range(3)],
            [(lanes(i), row(RS_CONV_B, i)) for i in range(DC // 128)],
        ]
        for a in range(n):
            for idx, grad in chunks[a]:
                g = grad()
                d, m2, v2 = _adam_math(w_refs[a][idx], g, m_refs[a][idx], v_refs[a][idx])
                g_out[a][idx] = g
                d_out[a][idx] = d
                m_out[a][idx] = m2
                v_out[a][idx] = v2

    shapes = [SDS(w.shape, f32) for w in ws]
    res = _pallas_call(body, name="adamw_small", out_shape=shapes * 4,
                       in_specs=[VMEM_SPEC, VMEM_SPEC, SMEM_SPEC] + [VMEM_SPEC] * (3 * n),
                       out_specs=[VMEM_SPEC] * (4 * n))(red, g_c_ctx, jvec, *ws, *ms, *vs)
    return [list(res[k * n:(k + 1) * n]) for k in range(4)]


def _rows128(a):
    return a.reshape(-1, 128)


def kernel(x, c, ctx, c_ctx, w_ada, b_ada, norm_g, w_in, q_norm_g, k_norm_g, rpb, conv_w, conv_b, w_out, loss_target, m_c_ctx, m_w_ada, m_b_ada, m_norm_g, m_w_in, m_q_norm_g, m_k_norm_g, m_rpb, m_conv_w, m_conv_b, m_w_out, v_c_ctx, v_w_ada, v_b_ada, v_norm_g, v_w_in, v_q_norm_g, v_k_norm_g, v_rpb, v_conv_w, v_conv_b, v_w_out):
    xi, yi, ci = lax.axis_index("x"), lax.axis_index("y"), lax.axis_index("c")
    dev = 4 * xi + 2 * yi + ci
    chip = 2 * xi + yi
    cvec = jnp.reshape(ci, (1,)).astype(i32)
    jvec = jnp.reshape(chip, (1,)).astype(i32)
    w_ada_s = w_ada[0]
    ncol = w_ada_s.shape[1]

    gc = _split_start([_to_slot(c.reshape(8, 128), 8, dev)], [], 7, _gather8_copies, "gather_c_start")
    btb = _bias_tiles(rpb[0], gc[3])
    (c8,), _ = _split_wait(gc[0], gc[1], [gc[2]], [], btb, _gather8_copies, "gather_c_wait")
    cc = jnp.concatenate([c8.reshape(8, D), c_ctx.reshape(1, D), jnp.zeros((7, D), f32)], axis=0)
    m_shard, sc16 = _adaln_shard(cc, w_ada_s)

    conv_w_pad = jnp.pad(conv_w[0], ((0, 5), (0, 0)))
    gm = _split_start([_to_slot(m_shard, 4, chip), _to_slot(conv_w_pad, 4, chip)], [], 6, _gather4_copies,
                      "gather_mod_start")
    w4c = _cast_to_slot(w_in[0], jvec, "cast_w_in")
    (m4, cw4), _ = _split_wait(gm[0], gm[1], [gm[2], gm[3]], [], w4c, _gather4_copies, "gather_mod_wait")

    all_k = [(0, 0, 0), (0, 0, 1), (0, 0, 2)]
    wo4c = _cast_to_slot(w_out[0], jvec, "cast_w_out")
    sem_a, rem_a, w4s, token = _split_start([w4c], [], 1, _near_copies, "weights_near_start", after=m4)
    m_full = jnp.transpose(m4, (1, 0, 2)).reshape(16, 4 * ncol)
    mrow = lax.dynamic_slice(m_full, (dev, 0), (1, 3 * D))
    mrow_c = m_full[8:9]
    conv_w_full = jnp.transpose(cw4[:, 0:3, :], (1, 0, 2)).reshape(3, DC)
    waves = {}

    def near(after):
        (w4w,), _ = _split_wait(sem_a, rem_a, [w4s], [], after, _near_copies, "weights_near_wait")
        sem_b, rem_b, w4b, started = _split_start([w4w], [], 2, _pass_relay_copies, "weights_pass_start")
        waves["pass"] = (sem_b, rem_b)
        return w4b, started

    def near2(w4, after):
        (w4w,), _ = _split_wait(*waves["pass"], [w4], [], after, _pass_copy, "weights_pass_wait")
        return w4w

    def far(w4, after):
        (w4w,), _ = _split_wait(*waves["pass"], [w4], [], after, _relay_copy, "weights_far_wait")
        w4f = _halves_forward([w4w], [2], "weights_far_forward")[0]
        sem_c, rem_c, wo4s, started = _halves_start([wo4c], w4f, all_k, "weights_out_start")
        waves["out"] = (sem_c, rem_c, wo4s)
        return w4f, started

    def w_out_arrived(after):
        sem_c, rem_c, wo4s = waves["out"]
        (wow,) = _halves_wait(sem_c, rem_c, [wo4s], after, all_k, "weights_out_wait")
        sem_d, rem_d, wof, started = _split_start([wow], [], 3, _forward_copies, "weights_out_forward_start")
        waves["out_forward"] = (sem_d, rem_d, wof)
        return started

    def w_out_gathered(after):
        sem_d, rem_d, wof = waves["out_forward"]
        (wo,), _ = _split_wait(sem_d, rem_d, [wof], [], after, _forward_copies, "weights_out_forward_wait")
        return wo.reshape(D, D)

    weights = dict(own=w_in[0], jvec=jvec, started=token, first=jvec ^ (1 + cvec), second=jvec ^ (2 - cvec), near=near, near2=near2,
                   far=far, out_arrived=w_out_arrived, out=w_out_gathered)

    exchange = _exchange_copies
    pending = {}

    def on_g_w_out(g_w_out):
        out = _split_start([g_w_out], [SDS((4, D // 8, D), f32)], 4, exchange, "grad_out_pair_start")
        pending["ex_out"] = out
        return out[4]

    def after_conv(dp8):
        ssem_o, rsem_o, g_o, land_o, _ = pending["ex_out"]
        (g_o,), (ex_o,) = _split_wait(ssem_o, rsem_o, [g_o], [land_o], dp8, exchange, "grad_out_pair_wait")
        to32, tob = _pair_sum_w_out(g_o, ex_o, cvec)
        out = _split_start([tob], [SDS((3, D // 8, D), bf16)], 3, _scatter_copies, "grad_out_chip_start")
        pending["sc_out"] = (out, to32)
        return out[4]

    def on_g_w_in(g_w_in):
        out = _split_start([g_w_in], [SDS((4, D // 2, D), f32)], 4, exchange, "grad_pair_start")
        pending["ex"] = (out[0], out[1], [out[2]], [out[3]])
        return out[4]

    def on_g_rpb(g_rpb):
        ex_ssem, ex_rsem, ex_srcs, ex_lands = pending["ex"]
        ex_g, ex = _split_wait(ex_ssem, ex_rsem, ex_srcs, ex_lands, g_rpb, exchange, "grad_pair_wait")
        t32, tb = _pair_sum_w_in(ex_g[0], ex[0], cvec)
        out = _split_start([tb], [SDS((3, D // 2, D), bf16)], 3, _scatter_copies, "grad_chip_start")
        pending["sc_in"] = (out, t32)
        return out[4]

    r = _local_step(x[0], ctx[0], loss_target[0], mrow, mrow_c, b_ada, norm_g, weights, q_norm_g, k_norm_g,
                    btb, conv_w_full, conv_b,
                    dict(g_w_out=on_g_w_out, after_conv=after_conv, g_w_in=on_g_w_in, g_rpb=on_g_rpb))
    dm = jnp.concatenate([r["dshift"], r["dscale"], r["dgate"]], axis=1)
    dmc = jnp.concatenate([r["dshift_c"], r["dscale_c"], jnp.zeros((1, D), f32)], axis=1)
    pack_parts = [_rows128(dm), _rows128(dmc), _rows128(r["dng"]), _rows128(r["dng_c"]), _rows128(r["g_gq"]),
                  _rows128(r["g_gk"]), _rows128(r["g_gk_c"]), r["g_rpb"].reshape(H * N_DR, 128),
                  _rows128(r["g_conv_b"]), _rows128(r["g_conv_w"][0:3]), jnp.pad(r["loss_sum"], ((0, 0), (0, 127)))]
    pack = jnp.concatenate([jnp.pad(p, ((0, -p.shape[0] % 8), (0, 0))) for p in pack_parts], axis=0)
    assert pack.shape[0] == PK_ROWS
    gs = _split_start([_to_slot(pack, 8, dev)], [], 7, _gather8_copies, "gather_small_start")
    sc_o, to32 = pending["sc_out"]
    _, (ro2,) = _split_wait(sc_o[0], sc_o[1], [sc_o[2]], [sc_o[3]], gs[3], _scatter_copies, "grad_out_chip_wait")
    u_out = _chip_sum(to32, ro2, jvec, "chip_sum_w_out")
    (o_out,) = _sibling_send([u_out], "grad_out_pair_send")
    g_w_out_s, d_w_out, nm_w_out, nv_w_out = _adamw_halves(w_out[0], u_out, o_out, m_w_out[0], v_w_out[0], cvec,
                                                           "adamw_w_out")
    (gathered,), _ = _split_wait(gs[0], gs[1], [gs[2]], [], nm_w_out, _gather8_copies, "gather_small_wait")
    red, dm16 = _small_reduce(gathered)
    loss = red[RS_LOSS, 0] * (0.5 / D)

    g_w_ada_s, cpart = _w_ada_grad(sc16, dm16, w_ada_s, jvec)
    d_w_ada, nm_w_ada, nv_w_ada = _adamw(w_ada_s, g_w_ada_s, m_w_ada[0], v_w_ada[0], "adamw_w_ada")

    (cparts4,) = _chip_gather([cpart], "gather_c_ctx_parts", after=nm_w_ada)
    g_c_ctx = _c_ctx_grad(cparts4, c_ctx)

    sc_in, t32 = pending["sc_in"]
    _, (r2,) = _split_wait(sc_in[0], sc_in[1], [sc_in[2]], [sc_in[3]], g_c_ctx, _scatter_copies, "grad_chip_wait")
    u_in = _chip_sum(t32, r2, jvec, "chip_sum_w_in")

    t_rpb = lambda a: jnp.transpose(a, (0, 2, 1, 3)).reshape(N_DR, H, N_DC)
    t_cw = lambda a: jnp.transpose(a, (1, 0, 2))
    small = _adamw_small(
        red, g_c_ctx, jvec,
        [c_ctx, b_ada, norm_g, q_norm_g, k_norm_g, t_rpb(rpb), t_cw(conv_w), conv_b],
        [m_c_ctx, m_b_ada, m_norm_g, m_q_norm_g, m_k_norm_g, t_rpb(m_rpb), t_cw(m_conv_w), m_conv_b],
        [v_c_ctx, v_b_ada, v_norm_g, v_q_norm_g, v_k_norm_g, t_rpb(v_rpb), t_cw(v_conv_w), v_conv_b])
    for kind in small:
        kind[5] = jnp.transpose(kind[5].reshape(1, N_DR, H, N_DC), (0, 2, 1, 3))
        kind[6] = jnp.transpose(kind[6], (1, 0, 2))

    (o_in,) = _sibling_send([u_in], "grad_pair_send")
    g_w_in_s, d_w_in, nm_w_in, nv_w_in = _adamw_halves(w_in[0], u_in, o_in, m_w_in[0], v_w_in[0], cvec, "adamw_w_in")

    def ordered(kind, big_w_ada, big_w_in, big_w_out):
        s_c_ctx, s_b_ada, s_norm_g, s_q, s_k, s_rpb, s_conv_w, s_conv_b = small[kind]
        return [s_c_ctx, big_w_ada[None], s_b_ada, s_norm_g, big_w_in[None], s_q, s_k, s_rpb, s_conv_w,
                s_conv_b, big_w_out[None]]

    grads = ordered(0, g_w_ada_s, g_w_in_s, g_w_out_s)
    deltas = ordered(1, d_w_ada, d_w_in, d_w_out)
    new_m = ordered(2, nm_w_ada, nm_w_in, nm_w_out)
    new_v = ordered(3, nv_w_ada, nv_w_in, nv_w_out)
    return (loss, r["grad_x"][None], *grads, *deltas, *new_m, *new_v)
```

```python
import functools

import jax
import jax.numpy as jnp
import numpy as np
from jax import lax
from jax.experimental import pallas as pl
from jax.experimental.pallas import tpu as pltpu

f32, bf16, i32 = jnp.float32, jnp.bfloat16, jnp.int32
MESH = pl.DeviceIdType.MESH
HIGHEST = lax.Precision.HIGHEST

D = 1024
S = 2048
L = 256
GW = 64
ROWS = S // GW
H = 8
DH = 64
DA = H * DH
DC = 512
WIN_H, WIN_W = 8, 16
N_DR, N_DC = 2 * WIN_H - 1, 2 * WIN_W - 1
RMS_EPS = 1e-6
ROPE_THETA = 10000.0
QK_SCALE = DH ** -0.5
NEG = -1e30

QB = 128
NQB = S // QB
KR = 9
KB = KR * GW
TILE_GEOM = ((0, 0), (2, 0), (4, 0), (28, 23), (30, 23))
NT = len(TILE_GEOM)

ADAM_LR, ADAM_B1, ADAM_B2, ADAM_EPS, ADAM_WD, ADAM_STEP = 0.001, 0.9, 0.999, 1e-08, 0.01, 10

VMEM_SPEC = pl.BlockSpec(memory_space=pltpu.VMEM)
ANY_SPEC = pl.BlockSpec(memory_space=pl.ANY)
SMEM_SPEC = pl.BlockSpec(memory_space=pltpu.SMEM)
SDS = jax.ShapeDtypeStruct


_pallas_call = pl.pallas_call


def _hbm_call(body, *, out_shape, in_specs=None, out_specs=None, grid_spec=None, **kw):
    n_pre = 0
    if grid_spec is not None:
        ispecs, ospecs, n_pre = grid_spec.in_specs, grid_spec.out_specs, grid_spec.num_scalar_prefetch
        kw["grid_spec"] = grid_spec
    else:
        ispecs, ospecs = in_specs, out_specs
        kw.update(in_specs=in_specs, out_specs=out_specs)

    def blocked(spec):
        return isinstance(spec, pl.BlockSpec) and spec.block_shape is not None

    single = not isinstance(out_shape, (tuple, list))
    shapes = [out_shape] if single else list(out_shape)
    ospec_list = list(ospecs) if isinstance(ospecs, (tuple, list)) else [ospecs]
    shapes = [pltpu.HBM(s.shape, s.dtype) if blocked(sp) else s for s, sp in zip(shapes, ospec_list)]
    call = _pallas_call(body, out_shape=shapes[0] if single else tuple(shapes), **kw)

    def run(*args):
        arrays = [pltpu.with_memory_space_constraint(a, pltpu.HBM) if blocked(sp) else a
                  for a, sp in zip(args[n_pre:], ispecs)]
        return call(*args[:n_pre], *arrays)

    return run


def _cp(vmem_mb=None, **kw):
    if vmem_mb is not None:
        kw["vmem_limit_bytes"] = vmem_mb << 20
    return pltpu.CompilerParams(**kw)


def _silu(z):
    return z * jax.nn.sigmoid(z)


def _dsilu(z):
    sg = jax.nn.sigmoid(z)
    return sg * (1.0 + z * (1.0 - sg))


def _row_start(i):
    return min(max(i - WIN_H // 2, 0), ROWS - WIN_H)


def _my_pos():
    return lax.axis_index("x"), lax.axis_index("y"), lax.axis_index("c")


def _flip(v, bit):
    return 1 - v if bit else v


def _chip_gather(smalls, name, after=None):
    ns = len(smalls)

    def body(*refs):
        s_in, s_out = refs[:ns], refs[ns + 1:2 * ns + 1]
        ssem, rsem, lsem = refs[2 * ns + 1:]
        x, y, c = _my_pos()
        j = 2 * x + y
        chips = _peer_chips(x, y, c)
        local = [pltpu.make_async_copy(s_in[a], s_out[a].at[j], lsem.at[a]) for a in range(ns)]
        for cp in local:
            cp.start()
        sends = []
        for a in range(ns):
            for k in range(3):
                cp = pltpu.make_async_remote_copy(src_ref=s_in[a], dst_ref=s_out[a].at[j], send_sem=ssem.at[3 * a + k],
                                                  recv_sem=rsem.at[3 * a + k], device_id=chips[k][0], device_id_type=MESH)
                cp.start()
                sends.append(cp)
        for a in range(ns):
            for k in range(3):
                pltpu.make_async_remote_copy(src_ref=s_in[a], dst_ref=s_out[a].at[chips[k][1]], send_sem=ssem.at[3 * a + k],
                                             recv_sem=rsem.at[3 * a + k], device_id=chips[k][0],
                                             device_id_type=MESH).wait_recv()
        for cp in sends:
            cp.wait_send()
        for cp in local:
            cp.wait()

    return _hbm_call(
        body, name=name, out_shape=[SDS((4,) + a.shape, a.dtype) for a in smalls],
        in_specs=[VMEM_SPEC] * ns + [ANY_SPEC], out_specs=[VMEM_SPEC] * ns,
        scratch_shapes=[pltpu.SemaphoreType.DMA((3 * ns,)), pltpu.SemaphoreType.DMA((3 * ns,)),
                        pltpu.SemaphoreType.DMA((ns,))],
    )(*smalls, smalls[0] if after is None else after)


HBM_SPEC = pl.BlockSpec(memory_space=pltpu.HBM)
SEM_SPEC = pl.BlockSpec(memory_space=pltpu.SEMAPHORE)
DATAFLOW = pltpu.SideEffectType.DATAFLOW_SIDE_EFFECTING


def _peer_chips(x, y, c):
    out = []
    for k in range(1, 4):
        px, py = _flip(x, (k >> 1) & 1), _flip(y, k & 1)
        out.append(((px, py, c), 2 * px + py))
    return out


def _half_copies(srcs, dsts, ssem, rsem, which):
    x, y, c = _my_pos()
    j = 2 * x + y
    peers = _peer_chips(x, y, c)
    pairs = []
    for pos, group, k in which:
        half = srcs[pos].shape[1] // 2
        mine = pl.ds(pl.multiple_of(c * half, 8), half)
        dev, pj = peers[k]
        sem = 3 * group + k
        send = pltpu.make_async_remote_copy(src_ref=srcs[pos].at[j, mine], dst_ref=dsts[pos].at[j, mine],
                                            send_sem=ssem.at[sem], recv_sem=rsem.at[sem], device_id=dev,
                                            device_id_type=MESH)
        arrive = pltpu.make_async_remote_copy(src_ref=srcs[pos].at[j, mine], dst_ref=dsts[pos].at[pj, mine],
                                              send_sem=ssem.at[sem], recv_sem=rsem.at[sem], device_id=dev,
                                              device_id_type=MESH)
        pairs.append((send, arrive))
    return pairs


def _halves_wait(ssem, rsem, bigs, after, which, name):
    nb = len(bigs)

    def body(*refs):
        b_in = refs[:nb]
        ssem_ref, rsem_ref = refs[nb], refs[nb + 1]
        for send, arrive in _half_copies(b_in, b_in, ssem_ref, rsem_ref, which):
            send.wait_send()
            arrive.wait_recv()

    return _hbm_call(
        body, name=name, out_shape=tuple(pltpu.HBM(b.shape, b.dtype) for b in bigs),
        in_specs=[HBM_SPEC] * nb + [SEM_SPEC, SEM_SPEC, ANY_SPEC], out_specs=tuple([HBM_SPEC] * nb),
        input_output_aliases={a: a for a in range(nb)}, compiler_params=_cp(has_side_effects=DATAFLOW),
    )(*bigs, ssem, rsem, after)


FORWARD_SEM = 3


def _diag_forward_copy(srcs, dsts, ssem, rsem):
    x, y, c = _my_pos()
    half = srcs[0].shape[1] // 2
    diag = 3 - (2 * x + y)
    mine = pl.ds(pl.multiple_of(c * half, 8), half)
    other = pl.ds(pl.multiple_of((1 - c) * half, 8), half)
    return [_Copy(srcs[0].at[diag, mine], dsts[0].at[diag, mine], dsts[0].at[diag, other], ssem.at[FORWARD_SEM],
                  rsem.at[FORWARD_SEM], (x, y, 1 - c))]


def _forward_then_start(fwd, big, order, name):
    def body(f_in, b_in, f_out, ssem, rsem, b_out, token):
        _diag_forward_copy([f_in], [f_out], ssem, rsem)[0].start()
        for send, _ in _half_copies([b_in], [b_out], ssem, rsem, order):
            send.start()
        token[...] = jnp.zeros_like(token)

    n_sem = FORWARD_SEM + 1
    out_shape = (pltpu.HBM(fwd.shape, fwd.dtype), pltpu.SemaphoreType.DMA((n_sem,)), pltpu.SemaphoreType.DMA((n_sem,)),
                 pltpu.HBM(big.shape, big.dtype), SDS((8, 128), f32))
    return _hbm_call(
        body, name=name, out_shape=out_shape, in_specs=[HBM_SPEC, HBM_SPEC],
        out_specs=(HBM_SPEC, SEM_SPEC, SEM_SPEC, HBM_SPEC, VMEM_SPEC), input_output_aliases={0: 0, 1: 3},
        compiler_params=_cp(has_side_effects=DATAFLOW),
    )(*[pltpu.with_memory_space_constraint(b, pltpu.HBM) for b in (fwd, big)])


def _cast_to_slot(w, jvec, name, after=None):
    rows, cols = w.shape
    tr = 256

    def body(j_ref, w_ref, after_ref, o_ref):
        o_ref[...] = w_ref[...].astype(bf16)

    grid_spec = pltpu.PrefetchScalarGridSpec(
        num_scalar_prefetch=1, grid=(rows // tr,),
        in_specs=[pl.BlockSpec((tr, cols), lambda i, j: (i, 0)), ANY_SPEC],
        out_specs=pl.BlockSpec((None, tr, cols), lambda i, j: (j[0], i, 0)))
    return _hbm_call(body, name=name, out_shape=SDS((4, rows, cols), bf16),
                     grid_spec=grid_spec)(jvec, w, jvec if after is None else after)


def _exchange_copies(srcs, lands, ssem, rsem):
    x, y, c = _my_pos()
    cps = []
    for a in range(len(srcs)):
        stacked = len(srcs[a].shape) == 3
        rb = srcs[a].shape[1] if stacked else srcs[a].shape[0] // 4
        half = rb // 2
        for jb in range(4):
            if stacked:
                src = srcs[a].at[jb, pl.ds(pl.multiple_of((1 - c) * half, 8), half)]
            else:
                src = srcs[a].at[pl.ds(pl.multiple_of(jb * rb + (1 - c) * half, 8), half)]
            cps.append(pltpu.make_async_remote_copy(src_ref=src, dst_ref=lands[a].at[jb], send_sem=ssem.at[4 * a + jb],
                                                    recv_sem=rsem.at[4 * a + jb], device_id=(x, y, 1 - c),
                                                    device_id_type=MESH))
    return cps


def _scatter_copies(srcs, lands, ssem, rsem):
    x, y, c = _my_pos()
    cps = []
    for a in range(len(srcs)):
        for k, (dev, pj) in enumerate(_peer_chips(x, y, c)):
            cps.append(pltpu.make_async_remote_copy(src_ref=srcs[a].at[pj], dst_ref=lands[a].at[k],
                                                    send_sem=ssem.at[3 * a + k], recv_sem=rsem.at[3 * a + k],
                                                    device_id=dev, device_id_type=MESH))
    return cps


class _Copy:
    def __init__(self, src, dst, arrive, ssem, rsem, dev):
        make = lambda to: pltpu.make_async_remote_copy(src_ref=src, dst_ref=to, send_sem=ssem, recv_sem=rsem,
                                                       device_id=dev, device_id_type=MESH)
        send, arrival = make(dst), make(arrive)
        self.start, self.wait_send, self.wait_recv = send.start, send.wait_send, arrival.wait_recv


def _toward(x, y, along_x):
    return x + along_x * (1 - 2 * x), y + (1 - along_x) * (1 - 2 * y)


def _near_copies(srcs, dsts, ssem, rsem):
    x, y, c = _my_pos()
    px, py = _toward(x, y, c)
    j = 2 * x + y
    return [_Copy(srcs[0].at[j], dsts[0].at[j], dsts[0].at[2 * px + py], ssem.at[0], rsem.at[0], (px, py, c))]


def _pass_copy(srcs, dsts, ssem, rsem):
    x, y, c = _my_pos()
    px, py = _toward(x, y, c)
    qx, qy = _toward(x, y, 1 - c)
    got = 2 * px + py
    return [_Copy(srcs[0].at[got], dsts[0].at[got], dsts[0].at[2 * qx + qy], ssem.at[0], rsem.at[0], (x, y, 1 - c))]


def _relay_copy(srcs, dsts, ssem, rsem):
    x, y, c = _my_pos()
    px, py = _toward(x, y, c)
    qx, qy = _toward(x, y, 1 - c)
    half = srcs[0].shape[1] // 2
    mine = pl.ds(pl.multiple_of(c * half, 8), half)
    got, diag = 2 * px + py, 3 - (2 * x + y)
    return [_Copy(srcs[0].at[got, mine], dsts[0].at[got, mine], dsts[0].at[diag, mine], ssem.at[1], rsem.at[1],
                  (qx, qy, c))]


def _forward_copies(srcs, dsts, ssem, rsem):
    x, y, c = _my_pos()
    half = srcs[0].shape[1] // 2
    mine = pl.ds(pl.multiple_of(c * half, 8), half)
    other = pl.ds(pl.multiple_of((1 - c) * half, 8), half)
    return [_Copy(srcs[0].at[pj, mine], dsts[0].at[pj, mine], dsts[0].at[pj, other], ssem.at[k], rsem.at[k],
                  (x, y, 1 - c)) for k, (_, pj) in enumerate(_peer_chips(x, y, c))]


def _pass_relay_copies(srcs, dsts, ssem, rsem):
    return _pass_copy(srcs, dsts, ssem, rsem) + _relay_copy(srcs, dsts, ssem, rsem)


def _gather8_copies(srcs, dsts, ssem, rsem):
    x, y, c = _my_pos()
    me = 4 * x + 2 * y + c
    cps = []
    for a in range(len(srcs)):
        for k in range(1, 8):
            tgt = (_flip(x, (k >> 2) & 1), _flip(y, (k >> 1) & 1), _flip(c, k & 1))
            cps.append(_Copy(srcs[a].at[me], dsts[a].at[me], dsts[a].at[4 * tgt[0] + 2 * tgt[1] + tgt[2]],
                             ssem.at[7 * a + k - 1], rsem.at[7 * a + k - 1], tgt))
    return cps


def _gather4_copies(srcs, dsts, ssem, rsem):
    x, y, c = _my_pos()
    j = 2 * x + y
    cps = []
    for a in range(len(srcs)):
        for k, (dev, pj) in enumerate(_peer_chips(x, y, c)):
            cps.append(_Copy(srcs[a].at[j], dsts[a].at[j], dsts[a].at[pj], ssem.at[3 * a + k], rsem.at[3 * a + k], dev))
    return cps


def _to_slot(a, n, i):
    return lax.dynamic_update_slice(jnp.zeros((n,) + a.shape, a.dtype), a[None], (i,) + (0,) * a.ndim)


def _split_start(srcs, land_shapes, n_cp, make, name, after=None):
    ns, nl = len(srcs), len(land_shapes)
    n_in = ns + nl + (after is not None)

    def body(*refs):
        s_in = refs[:ns]
        ssem, rsem = refs[n_in], refs[n_in + 1]
        s_out = refs[n_in + 2:n_in + 2 + ns]
        l_out = refs[n_in + 2 + ns:n_in + 2 + ns + nl]
        token = refs[n_in + 2 + ns + nl]
        for cp in make(s_in, l_out if nl else s_out, ssem, rsem):
            cp.start()
        token[...] = jnp.zeros_like(token)

    lands = [pltpu.with_memory_space_constraint(lax.empty(sh.shape, sh.dtype), pltpu.HBM) for sh in land_shapes]
    out_shape = (pltpu.SemaphoreType.DMA((n_cp,)), pltpu.SemaphoreType.DMA((n_cp,)),
                 *[pltpu.HBM(b.shape, b.dtype) for b in srcs], *[pltpu.HBM(b.shape, b.dtype) for b in land_shapes],
                 SDS((8, 128), f32))
    return _hbm_call(
        body, name=name, out_shape=out_shape, in_specs=[HBM_SPEC] * (ns + nl) + [ANY_SPEC] * (after is not None),
        out_specs=(SEM_SPEC, SEM_SPEC, *[HBM_SPEC] * (ns + nl), VMEM_SPEC),
        input_output_aliases={i: 2 + i for i in range(ns + nl)}, compiler_params=_cp(has_side_effects=DATAFLOW),
    )(*[pltpu.with_memory_space_constraint(b, pltpu.HBM) for b in srcs], *lands, *([] if after is None else [after]))


def _split_wait(ssem, rsem, srcs, lands, after, make, name):
    ns, nl = len(srcs), len(lands)

    def body(*refs):
        s_in, l_in = refs[:ns], refs[ns:ns + nl]
        ssem_ref, rsem_ref = refs[ns + nl], refs[ns + nl + 1]
        for cp in make(s_in, l_in if nl else s_in, ssem_ref, rsem_ref):
            cp.wait_send()
            cp.wait_recv()

    outs = _hbm_call(
        body, name=name, out_shape=tuple(pltpu.HBM(b.shape, b.dtype) for b in (*srcs, *lands)),
        in_specs=[HBM_SPEC] * (ns + nl) + [SEM_SPEC, SEM_SPEC, ANY_SPEC], out_specs=tuple([HBM_SPEC] * (ns + nl)),
        input_output_aliases={i: i for i in range(ns + nl)}, compiler_params=_cp(has_side_effects=DATAFLOW),
    )(*srcs, *lands, ssem, rsem, after)
    return list(outs[:ns]), list(outs[ns:])


def _sibling_send(halves, name):
    n = len(halves)

    def body(*refs):
        ins, outs = refs[:n], refs[n:2 * n]
        ssem, rsem = refs[2 * n:]
        x, y, c = _my_pos()
        cps = []
        for a in range(n):
            cp = pltpu.make_async_remote_copy(src_ref=ins[a], dst_ref=outs[a], send_sem=ssem.at[a],
                                              recv_sem=rsem.at[a], device_id=(x, y, 1 - c), device_id_type=MESH)
            cp.start()
            cps.append(cp)
        for cp in cps:
            cp.wait_recv()
        for cp in cps:
            cp.wait_send()

    out_shape = [SDS(h.shape, h.dtype) for h in halves]
    return _hbm_call(
        body, name=name, out_shape=out_shape, in_specs=[ANY_SPEC] * n, out_specs=[ANY_SPEC] * n,
        scratch_shapes=[pltpu.SemaphoreType.DMA((n,)), pltpu.SemaphoreType.DMA((n,))],
    )(*halves)


def _adaln_shard(cc, w_ada_shard):
    def body(c_ref, w_ref, m_ref, sc_ref):
        sc = _silu(c_ref[...])
        sc_ref[...] = sc
        m_ref[...] = jnp.dot(sc, w_ref[...], precision=HIGHEST, preferred_element_type=f32)

    return _hbm_call(
        body, name="adaln_shard", out_shape=(SDS((16, w_ada_shard.shape[1]), f32), SDS((16, D), f32)),
        in_specs=[VMEM_SPEC, VMEM_SPEC], out_specs=(VMEM_SPEC, VMEM_SPEC), compiler_params=_cp(32),
    )(cc, w_ada_shard)


def _prenorm(xx, norm_g, mrow, b_ada, tm, name, after=None):
    n = xx.shape[0]

    def body(x_ref, g_ref, m_ref, b_ref, after_ref, h_ref):
        x = x_ref[...]
        shift = m_ref[:, 0:D] + b_ref[:, 0:D]
        scale = m_ref[:, D:2 * D] + b_ref[:, D:2 * D]
        r = lax.rsqrt(jnp.mean(x * x, axis=-1, keepdims=True) + RMS_EPS)
        y = (x * r) * g_ref[...]
        h_ref[...] = (y * (1.0 + scale) + shift).astype(bf16)

    row = lambda i: (i, 0)
    fixed = lambda i: (0, 0)
    return _hbm_call(
        body, name=name, out_shape=SDS((n, D), bf16), grid=(n // tm,),
        in_specs=[pl.BlockSpec((tm, D), row), pl.BlockSpec((1, D), fixed), pl.BlockSpec((1, 3 * D), fixed),
                  pl.BlockSpec((1, 3 * D), fixed), ANY_SPEC],
        out_specs=pl.BlockSpec((tm, D), row),
    )(xx, norm_g, mrow, b_ada, b_ada if after is None else after)


def _in_proj_own(h, w_own, jvec):
    tm = 512

    def body(j_ref, h_ref, w_ref, p_ref):
        p_ref[...] = jnp.dot(h_ref[...], w_ref[...].astype(bf16), preferred_element_type=f32)

    grid_spec = pltpu.PrefetchScalarGridSpec(
        num_scalar_prefetch=1, grid=(S // tm,),
        in_specs=[pl.BlockSpec((tm, D), lambda i, j: (i, 0)), pl.BlockSpec((D, D), lambda i, j: (0, 0))],
        out_specs=pl.BlockSpec((tm, D), lambda i, j: (i, j[0])))
    return _hbm_call(body, name="in_proj_own", out_shape=SDS((S, 4 * D), f32), grid_spec=grid_spec,
                     compiler_params=_cp(40))(jvec, h, w_own)


def _in_proj_block(h, w4, p, bvec, name, after=None):
    tm = 512

    def body(b_ref, h_ref, w_ref, p_in_ref, after_ref, p_ref):
        p_ref[...] = jnp.dot(h_ref[...], w_ref[...], preferred_element_type=f32)

    grid_spec = pltpu.PrefetchScalarGridSpec(
        num_scalar_prefetch=1, grid=(S // tm,),
        in_specs=[pl.BlockSpec((tm, D), lambda i, b: (i, 0)), pl.BlockSpec((None, D, D), lambda i, b: (b[0], 0, 0)),
                  ANY_SPEC, ANY_SPEC],
        out_specs=pl.BlockSpec((tm, D), lambda i, b: (i, b[0])))
    return _hbm_call(body, name=name, out_shape=SDS((S, 4 * D), f32), grid_spec=grid_spec,
                     input_output_aliases={3: 0})(bvec, h, w4, p, bvec if after is None else after)


def _ctx_proj(hc, w4):
    def body(h_ref, w0_ref, w1_ref, p_ref):
        hv = h_ref[...]
        p_ref[:, 0:DA] = jnp.dot(hv, w0_ref[:, DA:2 * DA], preferred_element_type=f32)
        p_ref[:, DA:2 * DA] = jnp.dot(hv, w1_ref[:, 0:DA], preferred_element_type=f32)

    return _hbm_call(
        body, name="ctx_proj", out_shape=SDS((L, 2 * DA), f32), grid=(1,),
        in_specs=[pl.BlockSpec((L, D), lambda i: (0, 0)), pl.BlockSpec((None, D, D), lambda i: (0, 0, 0)),
                  pl.BlockSpec((None, D, D), lambda i: (1, 0, 0))],
        out_specs=pl.BlockSpec((L, 2 * DA), lambda i: (0, 0)),
    )(hc, w4, w4)


def _head_ones():
    r = lax.broadcasted_iota(i32, (DA, DA), 0) // DH
    c = lax.broadcasted_iota(i32, (DA, DA), 1) // DH
    return (r == c).astype(bf16)


def _head_sum(v, ones_bd):
    hi = v.astype(bf16)
    lo = (v - hi.astype(f32)).astype(bf16)
    return jnp.dot(hi, ones_bd, preferred_element_type=f32) + jnp.dot(lo, ones_bd, preferred_element_type=f32)


def _swap16(v):
    lane = lax.broadcasted_iota(i32, v.shape, 1)
    return jnp.where((lane & 31) < 16, pltpu.roll(v, DA - 16, 1), pltpu.roll(v, 16, 1))


def _rope_block(ct_ref, rt_ref, tm):
    rows = [jnp.tile(rt_ref[8 * j:8 * j + 8, :], (GW // 8, 1)) for j in range(tm // GW)]
    return jnp.tile(ct_ref[...], (tm // GW, 1)) + jnp.concatenate(rows, axis=0)


def _rope_specs(tm):
    col = pl.BlockSpec((GW, DA), lambda i: (0, 0))
    row = pl.BlockSpec((8 * tm // GW, DA), lambda i: (i, 0))
    return [col, row, col, row]


def _qk_prep(p, gq, gk, rope):
    tm = 256

    def body(qk_ref, v_ref, gq_ref, gk_ref, cc_ref, cr_ref, sc_ref, sr_ref, qr_ref, qp_ref, kr_ref, vh_ref):
        ones_bd = _head_ones()
        cs, sn = _rope_block(cc_ref, cr_ref, tm), _rope_block(sc_ref, sr_ref, tm)
        q = qk_ref[:, 0:DA]
        k = qk_ref[:, DA:2 * DA]
        yq = (q * lax.rsqrt(_head_sum(q * q, ones_bd) * (1.0 / DH) + RMS_EPS)) * gq_ref[...]
        yk = (k * lax.rsqrt(_head_sum(k * k, ones_bd) * (1.0 / DH) + RMS_EPS)) * gk_ref[...]
        qr = (yq * cs + _swap16(yq) * sn) * QK_SCALE
        qp = yq * QK_SCALE
        kr = yk * cs + _swap16(yk) * sn
        vv = v_ref[...]
        for hh in range(H):
            sl = slice(hh * DH, (hh + 1) * DH)
            qr_ref[hh] = qr[:, sl].astype(bf16)
            qp_ref[hh] = qp[:, sl].astype(bf16)
            kr_ref[hh] = kr[:, sl].astype(bf16)
            vh_ref[hh] = vv[:, sl].astype(bf16)

    hm = SDS((H, S, DH), bf16)
    hspec = pl.BlockSpec((H, tm, DH), lambda i: (0, i, 0))
    fixed = lambda i: (0, 0)
    return _hbm_call(
        body, name="qk_prep", out_shape=(hm, hm, hm, hm), grid=(S // tm,),
        in_specs=[pl.BlockSpec((tm, 2 * DA), lambda i: (i, 0)), pl.BlockSpec((tm, DA), lambda i: (i, 2)),
                  pl.BlockSpec((1, DA), fixed), pl.BlockSpec((1, DA), fixed)] + _rope_specs(tm),
        out_specs=(hspec, hspec, hspec, hspec),
    )(p, p, gq, gk, *rope)


def _ctx_prep(pc, gk):
    def body(p_ref, gk_ref, kc_ref, vc_ref):
        ones_bd = _head_ones()
        k = p_ref[:, 0:DA]
        yk = (k * lax.rsqrt(_head_sum(k * k, ones_bd) * (1.0 / DH) + RMS_EPS)) * gk_ref[...]
        vv = p_ref[:, DA:2 * DA]
        for hh in range(H):
            sl = slice(hh * DH, (hh + 1) * DH)
            kc_ref[hh] = yk[:, sl].astype(bf16)
            vc_ref[hh] = vv[:, sl].astype(bf16)

    hm = SDS((H, L, DH), bf16)
    return _hbm_call(
        body, name="ctx_prep", out_shape=(hm, hm), in_specs=[VMEM_SPEC, VMEM_SPEC], out_specs=(VMEM_SPEC, VMEM_SPEC),
    )(pc, gk)


def _tile_pieces():
    out = []
    for (i0, u0) in TILE_GEOM:
        rows = []
        for j in range(2):
            i = i0 + j
            rs = _row_start(i)
            rows.append([(u0 + u - i + WIN_H - 1) if rs <= u0 + u < rs + WIN_H else None for u in range(KR)])
        out.append(rows)
    return out


def _bias_prep(rpb_rev_pad, after=None):
    pieces = _tile_pieces()

    def body(r_ref, after_ref, o_ref):
        rp = r_ref[...]
        xs = jnp.broadcast_to(rp[:, None, :], (N_DR, GW, 128)).reshape(N_DR * GW, 128)
        row = lax.broadcasted_iota(i32, xs.shape, 0)
        lane = lax.broadcasted_iota(i32, xs.shape, 1)
        for b in range(6):
            xs = jnp.where(((row >> b) & 1) == 1, pltpu.roll(xs, 1 << b, 1), xs)
        xs = pltpu.roll(xs, 128 - (WIN_W - 1), 1)
        k = row & (GW - 1)
        c0 = jnp.clip(lane - WIN_W // 2, 0, GW - WIN_W)
        xs = jnp.where((k >= c0) & (k < c0 + WIN_W), xs, NEG)
        neg = jnp.full((GW, GW), NEG, f32)
        for t in range(NT):
            for j in range(2):
                for u in range(KR):
                    dr = pieces[t][j][u]
                    piece = neg if dr is None else xs[dr * GW:(dr + 1) * GW, 0:GW]
                    o_ref[t, u * GW:(u + 1) * GW, j * GW:(j + 1) * GW] = piece

    return _hbm_call(
        body, name="bias_prep", out_shape=SDS((H, NT, KB, QB), f32), grid=(H,),
        in_specs=[pl.BlockSpec((None, N_DR, 128), lambda h: (h, 0, 0)), ANY_SPEC],
        out_specs=pl.BlockSpec((None, NT, KB, QB), lambda h: (h, 0, 0, 0)),
    )(rpb_rev_pad, rpb_rev_pad if after is None else after)


def _bias_tiles(rpb2, after=None):
    return _bias_prep(jnp.pad(rpb2[:, :, ::-1], ((0, 0), (0, 0), (0, 128 - N_DC))), after)


def _block_geom(b):
    qs = b * QB
    ks = min(max(2 * b - 4, 0), ROWS - KR) * GW
    t = b if b < 2 else (b - (NQB - NT) if b > NQB - 3 else 2)
    return qs, ks, t


def _tt(a, b):
    return lax.dot_general(a, b, (((1,), (1,)), ((), ())), preferred_element_type=f32)


def _tn(a, b):
    return lax.dot_general(a, b, (((0,), (0,)), ((), ())), preferred_element_type=f32)


def _softmax_t(s_lat, s_ctx):
    m = jnp.maximum(jnp.max(s_lat, axis=0, keepdims=True), jnp.max(s_ctx, axis=0, keepdims=True))
    e_lat = jnp.exp(s_lat - m)
    e_ctx = jnp.exp(s_ctx - m)
    inv = 1.0 / (jnp.sum(e_lat, axis=0, keepdims=True) + jnp.sum(e_ctx, axis=0, keepdims=True))
    return e_lat * inv, e_ctx * inv


def _staged(n_blocks, stages):
    held = [dict() for _ in stages]
    for step in range(n_blocks + len(stages) - 1):
        for s, fn in enumerate(stages):
            b = step - s
            if 0 <= b < n_blocks:
                held[s][b] = fn(b) if s == 0 else fn(b, held[s - 1].pop(b))


def _attn_fwd(qr, qp, kr, vh, kc, vc, btt):
    def body(qr_ref, qp_ref, kr_ref, v_ref, kc_ref, vc_ref, bt_ref, o_ref):
        kcv, vcv = kc_ref[...], vc_ref[...]

        def scores(b):
            qs, ks, t = _block_geom(b)
            return (_tt(kr_ref[ks:ks + KB, :], qr_ref[qs:qs + QB, :]) + bt_ref[t], _tt(kcv, qp_ref[qs:qs + QB, :]))

        def probs(b, sc):
            p_lat, p_ctx = _softmax_t(*sc)
            return p_lat.astype(bf16), p_ctx.astype(bf16)

        def values(b, p):
            qs, ks, _ = _block_geom(b)
            o_ref[qs:qs + QB, :] = _tn(p[0], v_ref[ks:ks + KB, :]) + _tn(p[1], vcv)

        _staged(NQB, (scores, probs, values))

    sq = pl.BlockSpec((None, S, DH), lambda h: (h, 0, 0))
    sc = pl.BlockSpec((None, L, DH), lambda h: (h, 0, 0))
    return _hbm_call(
        body, name="attn_fwd", out_shape=SDS((H, S, DH), f32), grid=(H,),
        in_specs=[sq, sq, sq, sq, sc, sc, pl.BlockSpec((None, NT, KB, QB), lambda h: (h, 0, 0, 0))],
        out_specs=sq, compiler_params=_cp(48),
    )(qr, qp, kr, vh, kc, vc, btt)


def _shift_rows(v, down):
    n = v.shape[0]
    row = lax.broadcasted_iota(i32, v.shape, 0)
    if down:
        return jnp.where(row == 0, 0.0, pltpu.roll(v, 1, 0))
    return jnp.where(row == n - 1, 0.0, pltpu.roll(v, n - 1, 0))


def _conv_specs():
    col = lambda off: pl.BlockSpec((S, 128), lambda i, off=off: (0, off + i))
    return [col(16), col(20), col(24), col(28), pl.BlockSpec((3, 128), lambda i: (0, i)),
            pl.BlockSpec((1, 128), lambda i: (0, i))]


def _conv_fwd(p, conv_w, conv_b, after=None):
    def body(u_ref, bg_ref, cg_ref, zc_ref, w_ref, b_ref, after_ref, o_ref):
        cu = cg_ref[...] * u_ref[...]
        cv = b_ref[...] + _shift_rows(cu, True) * w_ref[0:1, :]
        cv = cv + cu * w_ref[1:2, :]
        cv = cv + _shift_rows(cu, False) * w_ref[2:3, :]
        o_ref[...] = ((bg_ref[...] * cv) * _silu(zc_ref[...])).astype(bf16)

    return _hbm_call(
        body, name="conv_fwd", out_shape=SDS((S, DC), bf16), grid=(DC // 128,),
        in_specs=_conv_specs() + [ANY_SPEC], out_specs=pl.BlockSpec((S, 128), lambda i: (0, i)),
        compiler_params=_cp(40),
    )(p, p, p, p, conv_w, conv_b, conv_b if after is None else after)


DP_Q, DP_K, DP_V, DP_ZA, DP_U, DP_BG, DP_CG, DP_ZC = range(8)


def _out_proj_loss(o, p, conv_g, w_out, xx, tgt, mrow, b_ada):
    tm = 256

    def body(o_ref, za_ref, c_ref, w_ref, x_ref, t_ref, m_ref, b_ref,
             dy_ref, dconv_ref, dp_ref, do_ref, gwo_ref, dgate_ref, loss_ref):
        k = pl.program_id(0)

        @pl.when(k == 0)
        def _():
            gwo_ref[...] = jnp.zeros_like(gwo_ref)
            dgate_ref[...] = jnp.zeros_like(dgate_ref)
            loss_ref[0, 0] = 0.0

        gate = m_ref[:, 2 * D:3 * D] + b_ref[:, 2 * D:3 * D]
        za = za_ref[...]
        sz = _silu(za)
        om = _merge_heads(o_ref)
        av, cv = (om * sz).astype(bf16), c_ref[...]
        mo = jnp.dot(av, w_ref[0:DA, :], preferred_element_type=f32)
        mo = mo + jnp.dot(cv, w_ref[DA:DA + DC, :], preferred_element_type=f32)
        y = x_ref[...] + gate * mo
        diff = y - t_ref[...]
        loss_ref[0, 0] += jnp.sum(diff * diff)
        dy = diff * (1.0 / D)
        dy_ref[...] = dy
        dgate_ref[...] += jnp.sum(dy * mo, axis=0, keepdims=True)
        dmo = (dy * gate).astype(bf16)
        dmix = _tt(dmo, w_ref[...])
        dattn = dmix[:, 0:DA]
        dconv_ref[...] = dmix[:, DA:DA + DC]
        a = dattn * sz
        for hh in range(H):
            do_ref[hh] = a[:, hh * DH:(hh + 1) * DH].astype(bf16)
        dp_ref[...] = ((dattn * _dsilu(za)) * om).astype(bf16)
        gwo_ref[0:DA, :] += _tn(av, dmo)
        gwo_ref[DA:DA + DC, :] += _tn(cv, dmo)

    row = lambda i: (i, 0)
    fixed = lambda i: (0, 0)
    hspec = pl.BlockSpec((H, tm, DH), lambda i: (0, i, 0))
    return _hbm_call(
        body, name="out_proj_loss",
        out_shape=(SDS((S, D), f32), SDS((S, DC), f32), SDS((8, S, DA), bf16), SDS((H, S, DH), bf16),
                   SDS((D, D), f32), SDS((1, D), f32), SDS((1, 1), f32)),
        grid=(S // tm,),
        in_specs=[hspec, pl.BlockSpec((tm, DA), lambda i: (i, 3)), pl.BlockSpec((tm, DC), row),
                  pl.BlockSpec((D, D), fixed), pl.BlockSpec((tm, D), row), pl.BlockSpec((tm, D), row),
                  pl.BlockSpec((1, 3 * D), fixed), pl.BlockSpec((1, 3 * D), fixed)],
        out_specs=(pl.BlockSpec((tm, D), row), pl.BlockSpec((tm, DC), row),
                   pl.BlockSpec((None, tm, DA), lambda i: (DP_ZA, i, 0)), hspec, pl.BlockSpec((D, D), fixed),
                   pl.BlockSpec((1, D), fixed), SMEM_SPEC),
        compiler_params=_cp(56, dimension_semantics=("arbitrary",)),
    )(o, p, conv_g, w_out, xx, tgt, mrow, b_ada)


def _conv_bwd(dconv, p, conv_w, conv_b, dp8, after=None):
    def body(d_ref, u_ref, bg_ref, cg_ref, zc_ref, w_ref, b_ref, dp_in_ref, after_ref, dp_ref, gw_ref, gb_ref):
        du_ref, dbg_ref, dcg_ref, dzc_ref = dp_ref.at[0], dp_ref.at[1], dp_ref.at[2], dp_ref.at[3]
        dconv = d_ref[...]
        u, bg, cg, zc = u_ref[...], bg_ref[...], cg_ref[...], zc_ref[...]
        w0, w1, w2 = w_ref[0:1, :], w_ref[1:2, :], w_ref[2:3, :]
        cu = cg * u
        cu_m, cu_p = _shift_rows(cu, True), _shift_rows(cu, False)
        cv = b_ref[...] + cu_m * w0
        cv = cv + cu * w1
        cv = cv + cu_p * w2
        sz = _silu(zc)
        dbg_ref[...] = ((dconv * sz) * cv).astype(bf16)
        dzc_ref[...] = ((dconv * (bg * cv)) * _dsilu(zc)).astype(bf16)
        dcv = (dconv * sz) * bg
        gb_ref[...] = jnp.sum(dcv, axis=0, keepdims=True)
        gw_ref[0:1, :] = jnp.sum(dcv * cu_m, axis=0, keepdims=True)
        gw_ref[1:2, :] = jnp.sum(dcv * cu, axis=0, keepdims=True)
        gw_ref[2:3, :] = jnp.sum(dcv * cu_p, axis=0, keepdims=True)
        gw_ref[3:8, :] = jnp.zeros((5, 128), f32)
        dcu = _shift_rows(dcv, False) * w0 + dcv * w1 + _shift_rows(dcv, True) * w2
        dcg_ref[...] = (dcu * u).astype(bf16)
        du_ref[...] = (dcu * cg).astype(bf16)

    return _hbm_call(
        body, name="conv_bwd", out_shape=(SDS((8, S, DC), bf16), SDS((8, DC), f32), SDS((1, DC), f32)),
        grid=(DC // 128,),
        in_specs=[pl.BlockSpec((S, 128), lambda i: (0, i))] + _conv_specs() + [ANY_SPEC, ANY_SPEC],
        out_specs=(pl.BlockSpec((4, S, 128), lambda i: (DP_U // 4, 0, i)), pl.BlockSpec((8, 128), lambda i: (0, i)),
                   pl.BlockSpec((1, 128), lambda i: (0, i))),
        input_output_aliases={7: 0}, compiler_params=_cp(48),
    )(dconv, p, p, p, p, conv_w, conv_b, dp8, conv_b if after is None else after)


def _attn_bwd(qr, qp, kr, vh, kc, vc, btt, do, after=None):
    def body(qr_ref, qp_ref, kr_ref, v_ref, kc_ref, vc_ref, bt_ref, do_ref, after_ref,
             dqr_ref, dqp_ref, dkr_ref, dv_ref, dkc_ref, dvc_ref, dbt_ref):
        kcv, vcv = kc_ref[...], vc_ref[...]
        dkr_ref[...] = jnp.zeros_like(dkr_ref)
        dv_ref[...] = jnp.zeros_like(dv_ref)
        dbt_ref[...] = jnp.zeros_like(dbt_ref)
        ctx_acc = {}

        def products(b):
            qs, ks, t = _block_geom(b)
            dob = do_ref[qs:qs + QB, :]
            s_lat = _tt(kr_ref[ks:ks + KB, :], qr_ref[qs:qs + QB, :]) + bt_ref[t]
            s_ctx = _tt(kcv, qp_ref[qs:qs + QB, :])
            return s_lat, s_ctx, _tt(v_ref[ks:ks + KB, :], dob), _tt(vcv, dob)

        def score_grads(b, x):
            s_lat, s_ctx, dp_lat, dp_ctx = x
            p_lat, p_ctx = _softmax_t(s_lat, s_ctx)
            delta = jnp.sum(p_lat * dp_lat, axis=0, keepdims=True) + jnp.sum(p_ctx * dp_ctx, axis=0, keepdims=True)
            ds_lat = p_lat * (dp_lat - delta)
            ds_ctx = p_ctx * (dp_ctx - delta)
            return ds_lat, ds_lat.astype(bf16), ds_ctx.astype(bf16), p_lat.astype(bf16), p_ctx.astype(bf16)

        def operand_grads(b, y):
            qs, ks, t = _block_geom(b)
            ds_lat, dsb_lat, dsb_ctx, pb_lat, pb_ctx = y
            qrb, qpb, dob = qr_ref[qs:qs + QB, :], qp_ref[qs:qs + QB, :], do_ref[qs:qs + QB, :]
            dbt_ref[t] += ds_lat
            dqr_ref[qs:qs + QB, :] = _tn(dsb_lat, kr_ref[ks:ks + KB, :])
            dqp_ref[qs:qs + QB, :] = _tn(dsb_ctx, kcv)
            dkr_ref[ks:ks + KB, :] += jnp.dot(dsb_lat, qrb, preferred_element_type=f32)
            dv_ref[ks:ks + KB, :] += jnp.dot(pb_lat, dob, preferred_element_type=f32)
            dkc = jnp.dot(dsb_ctx, qpb, preferred_element_type=f32)
            dvc = jnp.dot(pb_ctx, dob, preferred_element_type=f32)
            ctx_acc["k"] = dkc if b == 0 else ctx_acc["k"] + dkc
            ctx_acc["v"] = dvc if b == 0 else ctx_acc["v"] + dvc

        _staged(NQB, (products, score_grads, operand_grads))
        dkc_ref[...] = ctx_acc["k"]
        dvc_ref[...] = ctx_acc["v"]

    sq = pl.BlockSpec((None, S, DH), lambda h: (h, 0, 0))
    sc = pl.BlockSpec((None, L, DH), lambda h: (h, 0, 0))
    sb = pl.BlockSpec((None, NT, KB, QB), lambda h: (h, 0, 0, 0))
    big, ctxs = SDS((H, S, DH), f32), SDS((H, L, DH), f32)
    return _hbm_call(
        body, name="attn_bwd", out_shape=(big, big, big, big, ctxs, ctxs, SDS((H, NT, KB, QB), f32)), grid=(H,),
        in_specs=[sq, sq, sq, sq, sc, sc, sb, sq, ANY_SPEC], out_specs=(sq, sq, sq, sq, sc, sc, sb),
        compiler_params=_cp(56),
    )(qr, qp, kr, vh, kc, vc, btt, do, do if after is None else after)


def _bias_bwd(dbtt, after=None):
    pieces = _tile_pieces()

    def body(d_ref, after_ref, o_ref, scr):
        scr[...] = jnp.zeros_like(scr)
        acc = [None] * N_DR
        for t in range(NT):
            for j in range(2):
                for u in range(KR):
                    dr = pieces[t][j][u]
                    if dr is None:
                        continue
                    piece = d_ref[t, u * GW:(u + 1) * GW, j * GW:(j + 1) * GW]
                    acc[dr] = piece if acc[dr] is None else acc[dr] + piece
        for dr in range(N_DR):
            scr[dr * GW:(dr + 1) * GW, 0:GW] = acc[dr]
        xs = pltpu.roll(scr[...], WIN_W - 1, 1)
        row = lax.broadcasted_iota(i32, xs.shape, 0)
        for b in range(6):
            xs = jnp.where(((row >> b) & 1) == 1, pltpu.roll(xs, 128 - (1 << b), 1), xs)
        rev = jnp.sum(xs.reshape(N_DR, GW, 128), axis=1)
        a = lax.broadcasted_iota(i32, (128, 128), 0)
        b = lax.broadcasted_iota(i32, (128, 128), 1)
        flip = ((a + b == N_DC - 1) & (a < N_DC)).astype(f32)
        o_ref[...] = jnp.dot(rev, flip, precision=HIGHEST, preferred_element_type=f32)

    return _hbm_call(
        body, name="bias_bwd", out_shape=SDS((H, N_DR, 128), f32), grid=(H,),
        in_specs=[pl.BlockSpec((None, NT, KB, QB), lambda h: (h, 0, 0, 0)), ANY_SPEC],
        out_specs=pl.BlockSpec((None, N_DR, 128), lambda h: (h, 0, 0)),
        scratch_shapes=[pltpu.VMEM((N_DR * GW, 128), f32)],
    )(dbtt, dbtt if after is None else after)


def _merge_heads(ref):
    return jnp.concatenate([ref[hh] for hh in range(H)], axis=1)


def _head_norm_bwd(xraw, gain, dy, ones_bd):
    r = lax.rsqrt(_head_sum(xraw * xraw, ones_bd) * (1.0 / DH) + RMS_EPS)
    xh = xraw * r
    gdy = dy * gain
    dx = r * (gdy - xh * (_head_sum(xh * gdy, ones_bd) * (1.0 / DH)))
    return dx, jnp.sum(dy * xh, axis=0, keepdims=True)


def _qk_bwd(dqr, dqp, dkr, dvh, p, gq, gk, rope, dp8):
    tm = 256

    def body(dqr_ref, dqp_ref, dkr_ref, dv_ref, qk_ref, gq_ref, gk_ref, cc_ref, cr_ref, sc_ref, sr_ref, dp_in_ref,
             dp_ref, ggq_ref, ggk_ref):
        dq_ref, dk_ref, dvo_ref = dp_ref.at[DP_Q], dp_ref.at[DP_K], dp_ref.at[DP_V]

        @pl.when(pl.program_id(0) == 0)
        def _():
            ggq_ref[...] = jnp.zeros_like(ggq_ref)
            ggk_ref[...] = jnp.zeros_like(ggk_ref)

        ones_bd = _head_ones()
        cs, sn = _rope_block(cc_ref, cr_ref, tm), _rope_block(sc_ref, sr_ref, tm)
        a = _merge_heads(dqr_ref)
        dyq = ((a * cs - _swap16(a) * sn) + _merge_heads(dqp_ref)) * QK_SCALE
        bk = _merge_heads(dkr_ref)
        dyk = bk * cs - _swap16(bk) * sn
        dq, gq_part = _head_norm_bwd(qk_ref[:, 0:DA], gq_ref[...], dyq, ones_bd)
        dk, gk_part = _head_norm_bwd(qk_ref[:, DA:2 * DA], gk_ref[...], dyk, ones_bd)
        dq_ref[...] = dq.astype(bf16)
        dk_ref[...] = dk.astype(bf16)
        dvo_ref[...] = _merge_heads(dv_ref).astype(bf16)
        ggq_ref[...] += gq_part
        ggk_ref[...] += gk_part

    hspec = pl.BlockSpec((H, tm, DH), lambda i: (0, i, 0))
    fixed = pl.BlockSpec((1, DA), lambda i: (0, 0))
    return _hbm_call(
        body, name="qk_bwd", out_shape=(SDS((8, S, DA), bf16), SDS((1, DA), f32), SDS((1, DA), f32)), grid=(S // tm,),
        in_specs=[hspec, hspec, hspec, hspec, pl.BlockSpec((tm, 2 * DA), lambda i: (i, 0)), fixed, fixed]
        + _rope_specs(tm) + [ANY_SPEC],
        out_specs=(pl.BlockSpec((3, tm, DA), lambda i: (0, i, 0)), fixed, fixed), input_output_aliases={11: 0},
        compiler_params=_cp(40, dimension_semantics=("arbitrary",)),
    )(dqr, dqp, dkr, dvh, p, gq, gk, *rope, dp8)


def _ctx_bwd(dkc, dvc, pc, gk):
    def body(dkc_ref, dvc_ref, p_ref, gk_ref, dk_ref, dv_ref, ggk_ref):
        ones_bd = _head_ones()
        dk, gk_part = _head_norm_bwd(p_ref[:, 0:DA], gk_ref[...], _merge_heads(dkc_ref), ones_bd)
        dk_ref[...] = dk.astype(bf16)
        dv_ref[...] = _merge_heads(dvc_ref).astype(bf16)
        ggk_ref[...] = gk_part

    piece = SDS((L, DA), bf16)
    return _hbm_call(
        body, name="ctx_bwd", out_shape=(piece, piece, SDS((1, DA), f32)), in_specs=[VMEM_SPEC] * 4,
        out_specs=(VMEM_SPEC,) * 3,
    )(dkc, dvc, pc, gk)


def _grad_w_in(h, dp8, hc, dkc_raw, dvc_m):
    tm = 512

    def body(h_ref, p_ref, hc_ref, dk_ref, dv_ref, g_ref):
        j, k = pl.program_id(0), pl.program_id(1)

        @pl.when(k == 0)
        def _():
            g_ref[...] = jnp.zeros_like(g_ref)

        @pl.when((k == 0) & (j == 0))
        def _():
            g_ref[:, DA:2 * DA] = _tn(hc_ref[...], dk_ref[...])

        @pl.when((k == 0) & (j == 1))
        def _():
            g_ref[:, 0:DA] = _tn(hc_ref[...], dv_ref[...])

        hv = h_ref[...]
        g_ref[:, 0:DA] += _tn(hv, p_ref[0])
        g_ref[:, DA:2 * DA] += _tn(hv, p_ref[1])

    fixed = lambda j, k: (0, 0)
    return _hbm_call(
        body, name="grad_w_in", out_shape=SDS((4, D, D), f32), grid=(4, S // tm),
        in_specs=[pl.BlockSpec((tm, D), lambda j, k: (k, 0)), pl.BlockSpec((2, tm, DA), lambda j, k: (j, k, 0)),
                  pl.BlockSpec((L, D), fixed), pl.BlockSpec((L, DA), fixed), pl.BlockSpec((L, DA), fixed)],
        out_specs=pl.BlockSpec((None, D, D), lambda j, k: (j, 0, 0)),
        compiler_params=_cp(40, dimension_semantics=("arbitrary", "arbitrary")),
    )(h, dp8, hc, dkc_raw, dvc_m)


def _norm_mod_bwd(x, dh, g, scale):
    r = lax.rsqrt(jnp.mean(x * x, axis=-1, keepdims=True) + RMS_EPS)
    xh = x * r
    y = xh * g
    dshift = jnp.sum(dh, axis=0, keepdims=True)
    dscale = jnp.sum(dh * y, axis=0, keepdims=True)
    dyn = dh * (1.0 + scale)
    dg = jnp.sum(dyn * xh, axis=0, keepdims=True)
    gdy = dyn * g
    dx = r * (gdy - xh * jnp.mean(xh * gdy, axis=-1, keepdims=True))
    return dx, dshift, dscale, dg


def _dh_grad_x(dp8, w4, xx, dy, norm_g, mrow, b_ada, after=None):
    tm = 256

    def body(p_ref, w_ref, x_ref, dy_ref, g_ref, m_ref, b_ref, after_ref, gx_ref, dsh_ref, dsc_ref, dg_ref):
        @pl.when(pl.program_id(0) == 0)
        def _():
            dsh_ref[...] = jnp.zeros_like(dsh_ref)
            dsc_ref[...] = jnp.zeros_like(dsc_ref)
            dg_ref[...] = jnp.zeros_like(dg_ref)

        dh = None
        for j in range(4):
            for half in range(2):
                term = _tt(p_ref[2 * j + half], w_ref[j, :, half * DA:(half + 1) * DA])
                dh = term if dh is None else dh + term
        scale = m_ref[:, D:2 * D] + b_ref[:, D:2 * D]
        dx, dshift, dscale, dg = _norm_mod_bwd(x_ref[...], dh, g_ref[...], scale)
        gx_ref[...] = dy_ref[...] + dx
        dsh_ref[...] += dshift
        dsc_ref[...] += dscale
        dg_ref[...] += dg

    row = lambda i: (i, 0)
    fixed = lambda i: (0, 0)
    vec = SDS((1, D), f32)
    return _hbm_call(
        body, name="dh_grad_x", out_shape=(SDS((S, D), f32), vec, vec, vec), grid=(S // tm,),
        in_specs=[pl.BlockSpec((8, tm, DA), lambda i: (0, i, 0)), pl.BlockSpec((4, D, D), lambda i: (0, 0, 0)),
                  pl.BlockSpec((tm, D), row), pl.BlockSpec((tm, D), row), pl.BlockSpec((1, D), fixed),
                  pl.BlockSpec((1, 3 * D), fixed), pl.BlockSpec((1, 3 * D), fixed), ANY_SPEC],
        out_specs=(pl.BlockSpec((tm, D), row), pl.BlockSpec((1, D), fixed), pl.BlockSpec((1, D), fixed),
                   pl.BlockSpec((1, D), fixed)),
        compiler_params=_cp(56, dimension_semantics=("arbitrary",)),
    )(dp8, w4, xx, dy, norm_g, mrow, b_ada, b_ada if after is None else after)


def _dhc_sums(dkc_raw, dvc_m, w4, ctx2, norm_g, mrow_c, b_ada, after=None):
    def body(dk_ref, dv_ref, w0_ref, w1_ref, x_ref, g_ref, m_ref, b_ref, after_ref, dsh_ref, dsc_ref, dg_ref):
        dh = _tt(dk_ref[...], w0_ref[:, DA:2 * DA]) + _tt(dv_ref[...], w1_ref[:, 0:DA])
        scale = m_ref[:, D:2 * D] + b_ref[:, D:2 * D]
        _, dshift, dscale, dg = _norm_mod_bwd(x_ref[...], dh, g_ref[...], scale)
        dsh_ref[...] = dshift
        dsc_ref[...] = dscale
        dg_ref[...] = dg

    fixed = lambda i: (0, 0)
    vec = SDS((1, D), f32)
    vspec = pl.BlockSpec((1, D), fixed)
    return _hbm_call(
        body, name="dhc_sums", out_shape=(vec, vec, vec), grid=(1,),
        in_specs=[pl.BlockSpec((L, DA), fixed), pl.BlockSpec((L, DA), fixed),
                  pl.BlockSpec((None, D, D), lambda i: (0, 0, 0)), pl.BlockSpec((None, D, D), lambda i: (1, 0, 0)),
                  pl.BlockSpec((L, D), fixed), vspec, pl.BlockSpec((1, 3 * D), fixed), pl.BlockSpec((1, 3 * D), fixed),
                  ANY_SPEC],
        out_specs=(vspec, vspec, vspec), compiler_params=_cp(32),
    )(dkc_raw, dvc_m, w4, w4, ctx2, norm_g, mrow_c, b_ada, b_ada if after is None else after)


def _rope_tables():
    nf = DH // 4
    inv = np.float32(ROPE_THETA) ** (-np.arange(nf, dtype=np.float32) / np.float32(nf))
    ang_c = np.arange(GW, dtype=np.float32)[:, None] * inv
    ang_r = np.arange(ROWS, dtype=np.float32)[:, None] * inv
    zc, zr = np.zeros((GW, 2 * nf), np.float32), np.zeros((ROWS, 2 * nf), np.float32)
    ct_cos = np.tile(np.concatenate([zc, np.cos(ang_c), np.cos(ang_c)], axis=1), (1, H))
    ct_sin = np.tile(np.concatenate([zc, -np.sin(ang_c), np.sin(ang_c)], axis=1), (1, H))
    rt_cos = np.tile(np.concatenate([np.cos(ang_r), np.cos(ang_r), zr], axis=1), (1, H))
    rt_sin = np.tile(np.concatenate([-np.sin(ang_r), np.sin(ang_r), zr], axis=1), (1, H))
    rep8 = lambda t: np.ascontiguousarray(np.broadcast_to(t[:, None, :], (ROWS, 8, DA))).reshape(ROWS * 8, DA)
    return tuple(jnp.asarray(t, f32) for t in (ct_cos, rep8(rt_cos), ct_sin, rep8(rt_sin)))


def _local_step(xx, ctx2, tgt, mrow, mrow_c, b_ada, norm_g, weights, q_norm_g, k_norm_g, rpb2, conv_w_full, conv_b,
                hooks=None):
    hooks = hooks or {}
    gq = jnp.tile(q_norm_g, (1, H))
    gk = jnp.tile(k_norm_g, (1, H))
    rope = _rope_tables()

    h = _prenorm(xx, norm_g, mrow, b_ada, 256, "prenorm_x", after=weights.get("started"))
    hc = _prenorm(ctx2, norm_g, mrow_c, b_ada, L, "prenorm_ctx")
    jv = weights["jvec"]
    p = _in_proj_own(h, weights["own"], jv)
    btb = _bias_tiles(rpb2, after=p)
    w4, started = weights["near"](btb)
    p = _in_proj_block(h, w4, p, weights["first"], "in_proj_near", after=started)
    w4 = weights["near2"](w4, p)
    p = _in_proj_block(h, w4, p, weights["second"], "in_proj_near2")
    w4, started = weights["far"](w4, p)
    p = _in_proj_block(h, w4, p, jv ^ 3, "in_proj_far", after=started)
    pc = _ctx_proj(hc, w4)
    qr, qp, kr, vh = _qk_prep(p, gq, gk, rope)
    kc, vc = _ctx_prep(pc, gk)
    o = _attn_fwd(qr, qp, kr, vh, kc, vc, btb)
    started = weights["out_arrived"](o) if "out_arrived" in weights else None
    conv_g = _conv_fwd(p, conv_w_full, conv_b, after=started)
    w_out_full = weights["out"](conv_g)
    dy, dconv, dp8, do, g_w_out, dgate, loss_sum = _out_proj_loss(o, p, conv_g, w_out_full, xx, tgt, mrow, b_ada)
    started = hooks["g_w_out"](g_w_out) if "g_w_out" in hooks else None
    dp8, g_conv_w, g_conv_b = _conv_bwd(dconv, p, conv_w_full, conv_b, dp8, after=started)
    started = hooks["after_conv"](dp8) if "after_conv" in hooks else None
    dqr, dqp, dkr, dvh, dkc, dvc, dbtb = _attn_bwd(qr, qp, kr, vh, kc, vc, btb, do, after=started)
    dp8, g_gq, g_gk = _qk_bwd(dqr, dqp, dkr, dvh, p, gq, gk, rope, dp8)
    dkc_raw, dvc_m, g_gk_c = _ctx_bwd(dkc, dvc, pc, gk)
    g_w_in = _grad_w_in(h, dp8, hc, dkc_raw, dvc_m)
    started = hooks["g_w_in"](g_w_in) if "g_w_in" in hooks else None
    dshift_c, dscale_c, dng_c = _dhc_sums(dkc_raw, dvc_m, w4, ctx2, norm_g, mrow_c, b_ada, after=started)
    g_rpb = _bias_bwd(dbtb, after=dshift_c)
    started = hooks["g_rpb"](g_rpb) if "g_rpb" in hooks else None
    grad_x, dshift, dscale, dng = _dh_grad_x(dp8, w4, xx, dy, norm_g, mrow, b_ada, after=started)
    return dict(loss_sum=loss_sum, grad_x=grad_x, g_w_in=g_w_in, g_w_out=g_w_out, g_conv_w=g_conv_w,
                g_conv_b=g_conv_b, g_rpb=g_rpb, g_gq=g_gq, g_gk=g_gk, g_gk_c=g_gk_c, dshift=dshift, dscale=dscale,
                dgate=dgate, dng=dng, dshift_c=dshift_c, dscale_c=dscale_c, dng_c=dng_c)


def _pair_sum_w_in(g, r, cvec):
    tr = 128

    def body(c_ref, g_ref, r_ref, t32_ref, tb_ref):
        t = g_ref[...] + r_ref[...]
        t32_ref[...] = t
        tb_ref[...] = t.astype(bf16)

    half = D // 2
    g_spec = pl.BlockSpec((4, tr, D), lambda i, c: (0, c[0] * (half // tr) + i, 0))
    o_spec = pl.BlockSpec((4, tr, D), lambda i, c: (0, i, 0))
    grid_spec = pltpu.PrefetchScalarGridSpec(num_scalar_prefetch=1, grid=(half // tr,), in_specs=[g_spec, o_spec],
                                             out_specs=(o_spec, o_spec))
    return _hbm_call(body, name="pair_sum_w_in", out_shape=(SDS((4, half, D), f32), SDS((4, half, D), bf16)),
                     grid_spec=grid_spec, compiler_params=_cp(40))(cvec, g, r)


def _pair_sum_w_out(g, r, cvec):
    hr = D // 8

    def body(c_ref, g0, g1, g2, g3, r_ref, t32_ref, tb_ref):
        for q, g_ref in enumerate((g0, g1, g2, g3)):
            t = g_ref[...] + r_ref[q]
            t32_ref[q] = t
            tb_ref[q] = t.astype(bf16)

    gspecs = [pl.BlockSpec((hr, D), lambda i, c, q=q: (2 * q + c[0], 0)) for q in range(4)]
    full = pl.BlockSpec((4, hr, D), lambda i, c: (0, 0, 0))
    grid_spec = pltpu.PrefetchScalarGridSpec(num_scalar_prefetch=1, grid=(1,), in_specs=gspecs + [full],
                                             out_specs=(full, full))
    return _hbm_call(body, name="pair_sum_w_out", out_shape=(SDS((4, hr, D), f32), SDS((4, hr, D), bf16)),
                          grid_spec=grid_spec)(cvec, g, g, g, g, r)


def _chip_sum(t32, r2, jvec, name):
    rows = t32.shape[1]
    tr = min(rows, 128)

    def body(j_ref, t_ref, r_ref, u_ref):
        u_ref[...] = ((t_ref[...] + r_ref[0].astype(f32)) + r_ref[1].astype(f32)) + r_ref[2].astype(f32)

    grid_spec = pltpu.PrefetchScalarGridSpec(
        num_scalar_prefetch=1, grid=(rows // tr,),
        in_specs=[pl.BlockSpec((None, tr, D), lambda i, j: (j[0], i, 0)), pl.BlockSpec((3, tr, D), lambda i, j: (0, i, 0))],
        out_specs=pl.BlockSpec((tr, D), lambda i, j: (i, 0)))
    return _hbm_call(body, name=name, out_shape=SDS((rows, D), f32), grid_spec=grid_spec)(jvec, t32, r2)


_PK = {}
_off = 0
for _name, _rows in (("dm", 24), ("dmc", 24), ("dng", 8), ("dng_c", 8), ("gq", 8), ("gk", 8), ("gk_c", 8),
                     ("rpb", H * N_DR), ("conv_b", 8), ("conv_w", 16), ("loss", 8)):
    _PK[_name] = (_off, _off + _rows)
    _off += _rows
PK_ROWS = _off
RS_B_ADA, RS_NORM_G, RS_GQ, RS_GK, RS_RPB, RS_CONV_B, RS_CONV_W, RS_DMC, RS_LOSS, RS_ROWS = (
    0, 24, 32, 40, 48, 168, 176, 192, 216, 224)


def _small_reduce(gathered):
    def body(g_ref, o_ref, dm_ref):
        a0 = _PK["dm"][0]
        dm_ref[...] = jnp.zeros_like(dm_ref)
        for b in range(8):
            for i in range(24):
                dm_ref[b:b + 1, 128 * i:128 * (i + 1)] = g_ref[b, a0 + i:a0 + i + 1, :]
        tot = g_ref[0]
        for b in range(1, 8):
            tot = tot + g_ref[b]

        def rows(name):
            a, z = _PK[name]
            return tot[a:z]

        o_ref[RS_B_ADA:RS_B_ADA + 24] = rows("dm") + rows("dmc")
        o_ref[RS_NORM_G:RS_NORM_G + 8] = rows("dng") + rows("dng_c")
        gq = jnp.broadcast_to(jnp.sum(rows("gq"), axis=0, keepdims=True), (8, 128))
        gk = jnp.broadcast_to(jnp.sum(rows("gk") + rows("gk_c"), axis=0, keepdims=True), (8, 128))
        o_ref[RS_GQ:RS_GQ + 8] = gq + pltpu.roll(gq, DH, 1)
        o_ref[RS_GK:RS_GK + 8] = gk + pltpu.roll(gk, DH, 1)
        o_ref[RS_RPB:RS_RPB + H * N_DR] = rows("rpb")
        o_ref[RS_CONV_B:RS_CONV_B + 8] = rows("conv_b")
        o_ref[RS_CONV_W:RS_CONV_W + 16] = rows("conv_w")
        dmc = rows("dmc")
        o_ref[RS_DMC:RS_DMC + 24] = dmc
        o_ref[RS_LOSS:RS_LOSS + 8] = rows("loss")
        for i in range(24):
            dm_ref[8:9, 128 * i:128 * (i + 1)] = dmc[i:i + 1]

    return _hbm_call(body, name="small_reduce", out_shape=(SDS((RS_ROWS, 128), f32), SDS((16, 3 * D), f32)),
                     in_specs=[VMEM_SPEC], out_specs=(VMEM_SPEC, VMEM_SPEC))(gathered)


def _w_ada_grad(sc16, dm16, w_ada_shard, jvec):
    ncol = w_ada_shard.shape[1]

    def body(j_ref, sc_ref, dm_ref, w_ref, g_ref, part_ref):
        dm = dm_ref[...]
        g_ref[...] = lax.dot_general(sc_ref[...], dm, (((0,), (0,)), ((), ())), precision=HIGHEST,
                                     preferred_element_type=f32)
        part_ref[...] = lax.dot_general(dm[8:16], w_ref[...], (((1,), (1,)), ((), ())), precision=HIGHEST,
                                        preferred_element_type=f32)

    fixed = lambda i, j: (0, 0)
    grid_spec = pltpu.PrefetchScalarGridSpec(
        num_scalar_prefetch=1, grid=(1,),
        in_specs=[pl.BlockSpec((16, D), fixed), pl.BlockSpec((16, ncol), lambda i, j: (0, j[0])),
                  pl.BlockSpec((D, ncol), fixed)],
        out_specs=(pl.BlockSpec((D, ncol), fixed), pl.BlockSpec((8, D), fixed)))
    return _pallas_call(body, name="w_ada_grad", out_shape=(SDS((D, ncol), f32), SDS((8, D), f32)),
                        grid_spec=grid_spec, compiler_params=_cp(40))(jvec, sc16, dm16, w_ada_shard)


def _c_ctx_grad(parts4, c_ctx):
    def body(p_ref, c_ref, o_ref):
        tot = ((p_ref[0] + p_ref[1]) + p_ref[2]) + p_ref[3]
        o_ref[...] = tot[0:1] * _dsilu(c_ref[...].reshape(1, D))

    return _pallas_call(body, name="c_ctx_grad", out_shape=SDS((1, D), f32), in_specs=[VMEM_SPEC, VMEM_SPEC],
                        out_specs=VMEM_SPEC)(parts4, c_ctx)


def _adamw(w, g, m, v, name):
    rows, cols = w.shape
    tr = 256 if rows % 256 == 0 else rows

    def body(w_ref, g_ref, m_ref, v_ref, d_ref, m2_ref, v2_ref):
        gv = g_ref[...]
        m2 = ADAM_B1 * m_ref[...] + (1.0 - ADAM_B1) * gv
        v2 = ADAM_B2 * v_ref[...] + (1.0 - ADAM_B2) * jnp.square(gv)
        m_hat = m2 / (1.0 - ADAM_B1 ** ADAM_STEP)
        v_hat = v2 / (1.0 - ADAM_B2 ** ADAM_STEP)
        d_ref[...] = -ADAM_LR * (m_hat / (jnp.sqrt(v_hat) + ADAM_EPS) + ADAM_WD * w_ref[...])
        m2_ref[...] = m2
        v2_ref[...] = v2

    spec = pl.BlockSpec((tr, cols), lambda i: (i, 0))
    shp = SDS((rows, cols), f32)
    return _hbm_call(body, name=name, out_shape=(shp, shp, shp), grid=(rows // tr,), in_specs=[spec] * 4,
                          out_specs=(spec, spec, spec))(w, g, m, v)


def _adamw_halves(w, g_mine, g_other, m, v, cvec, name):
    rows, cols = w.shape
    half = rows // 2
    tr = min(256, half)
    per_half = half // tr

    def body(c_ref, w_ref, ga_ref, gb_ref, m_ref, v_ref, g_ref, d_ref, m2_ref, v2_ref):
        in_my_half = (pl.program_id(0) // per_half) == c_ref[0]
        gv = jnp.where(in_my_half, ga_ref[...], gb_ref[...])
        g_ref[...] = gv
        m2 = ADAM_B1 * m_ref[...] + (1.0 - ADAM_B1) * gv
        v2 = ADAM_B2 * v_ref[...] + (1.0 - ADAM_B2) * jnp.square(gv)
        m_hat = m2 / (1.0 - ADAM_B1 ** ADAM_STEP)
        v_hat = v2 / (1.0 - ADAM_B2 ** ADAM_STEP)
        d_ref[...] = -ADAM_LR * (m_hat / (jnp.sqrt(v_hat) + ADAM_EPS) + ADAM_WD * w_ref[...])
        m2_ref[...] = m2
        v2_ref[...] = v2

    full = pl.BlockSpec((tr, cols), lambda i, c: (i, 0))
    part = pl.BlockSpec((tr, cols), lambda i, c: (i % per_half, 0))
    shp = SDS((rows, cols), f32)
    grid_spec = pltpu.PrefetchScalarGridSpec(num_scalar_prefetch=1, grid=(rows // tr,),
                                             in_specs=[full, part, part, full, full], out_specs=(full,) * 4)
    return _hbm_call(body, name=name, out_shape=(shp,) * 4, grid_spec=grid_spec)(cvec, w, g_mine, g_other, m, v)


def _adam_math(w, g, m, v):
    m2 = ADAM_B1 * m + (1.0 - ADAM_B1) * g
    v2 = ADAM_B2 * v + (1.0 - ADAM_B2) * jnp.square(g)
    m_hat = m2 / (1.0 - ADAM_B1 ** ADAM_STEP)
    v_hat = v2 / (1.0 - ADAM_B2 ** ADAM_STEP)
    return -ADAM_LR * (m_hat / (jnp.sqrt(v_hat) + ADAM_EPS) + ADAM_WD * w), m2, v2


def _adamw_small(red, g_c_ctx, jvec, ws, ms, vs):
    n = len(ws)

    def body(*refs):
        red_ref, gc_ref, j_ref = refs[:3]
        w_refs, m_refs, v_refs = refs[3:3 + n], refs[3 + n:3 + 2 * n], refs[3 + 2 * n:3 + 3 * n]
        outs = refs[3 + 3 * n:]
        g_out, d_out, m_out, v_out = outs[:n], outs[n:2 * n], outs[2 * n:3 * n], outs[3 * n:]
        chip = j_ref[0]
        lanes = lambda i: (slice(None), slice(128 * i, 128 * (i + 1)))
        row = lambda r0, i: (lambda: red_ref[r0 + i:r0 + i + 1, :])
        whole = (slice(None), slice(None))
        chunks = [
            [((slice(None),), lambda: gc_ref[...].reshape(D))],
            [(lanes(i), row(RS_B_ADA, i)) for i in range(3 * D // 128)],
            [(lanes(i), row(RS_NORM_G, i)) for i in range(D // 128)],
            [(whole, lambda: red_ref[RS_GQ:RS_GQ + 1, 0:DH])],
            [(whole, lambda: red_ref[RS_GK:RS_GK + 1, 0:DH])],
            [((dr,), (lambda dr=dr: red_ref[pl.ds(RS_RPB + dr, H, stride=N_DR), 0:N_DC])) for dr in range(N_DR)],
            [((r,), (lambda r=r: red_ref[pl.ds(RS_CONV_W + 4 * r + chip, 1), :])) for r in range(3)],
            [(lanes(i), row(RS_CONV_B, i)) for i in range(DC // 128)],
        ]
        for a in range(n):
            for idx, grad in chunks[a]:
                g = grad()
                d, m2, v2 = _adam_math(w_refs[a][idx], g, m_refs[a][idx], v_refs[a][idx])
                g_out[a][idx] = g
                d_out[a][idx] = d
                m_out[a][idx] = m2
                v_out[a][idx] = v2

    shapes = [SDS(w.shape, f32) for w in ws]
    res = _pallas_call(body, name="adamw_small", out_shape=shapes * 4,
                       in_specs=[VMEM_SPEC, VMEM_SPEC, SMEM_SPEC] + [VMEM_SPEC] * (3 * n),
                       out_specs=[VMEM_SPEC] * (4 * n))(red, g_c_ctx, jvec, *ws, *ms, *vs)
    return [list(res[k * n:(k + 1) * n]) for k in range(4)]


def _rows128(a):
    return a.reshape(-1, 128)


def kernel(x, c, ctx, c_ctx, w_ada, b_ada, norm_g, w_in, q_norm_g, k_norm_g, rpb, conv_w, conv_b, w_out, loss_target, m_c_ctx, m_w_ada, m_b_ada, m_norm_g, m_w_in, m_q_norm_g, m_k_norm_g, m_rpb, m_conv_w, m_conv_b, m_w_out, v_c_ctx, v_w_ada, v_b_ada, v_norm_g, v_w_in, v_q_norm_g, v_k_norm_g, v_rpb, v_conv_w, v_conv_b, v_w_out):
    xi, yi, ci = lax.axis_index("x"), lax.axis_index("y"), lax.axis_index("c")
    dev = 4 * xi + 2 * yi + ci
    chip = 2 * xi + yi
    cvec = jnp.reshape(ci, (1,)).astype(i32)
    jvec = jnp.reshape(chip, (1,)).astype(i32)
    w_ada_s = w_ada[0]
    ncol = w_ada_s.shape[1]

    gc = _split_start([_to_slot(c.reshape(8, 128), 8, dev)], [], 7, _gather8_copies, "gather_c_start")
    wo4c = _cast_to_slot(w_out[0], jvec, "cast_w_out", after=gc[3])
    w4c = _cast_to_slot(w_in[0], jvec, "cast_w_in", after=wo4c)
    (c8,), _ = _split_wait(gc[0], gc[1], [gc[2]], [], w4c, _gather8_copies, "gather_c_wait")
    cc = jnp.concatenate([c8.reshape(8, D), c_ctx.reshape(1, D), jnp.zeros((7, D), f32)], axis=0)
    m_shard, sc16 = _adaln_shard(cc, w_ada_s)

    conv_w_pad = jnp.pad(conv_w[0], ((0, 5), (0, 0)))
    gm = _split_start([_to_slot(m_shard, 4, chip), _to_slot(conv_w_pad, 4, chip)], [], 6, _gather4_copies,
                      "gather_mod_start")

    all_k = [(0, 0, 0), (0, 0, 1), (0, 0, 2)]
    sem_a, rem_a, w4s, token = _split_start([w4c], [], 1, _near_copies, "weights_near_start", after=gm[4])
    (m4, cw4), _ = _split_wait(gm[0], gm[1], [gm[2], gm[3]], [], token, _gather4_copies, "gather_mod_wait")
    m_full = jnp.transpose(m4, (1, 0, 2)).reshape(16, 4 * ncol)
    mrow = lax.dynamic_slice(m_full, (dev, 0), (1, 3 * D))
    mrow_c = m_full[8:9]
    conv_w_full = jnp.transpose(cw4[:, 0:3, :], (1, 0, 2)).reshape(3, DC)
    waves = {}

    def near(after):
        (w4w,), _ = _split_wait(sem_a, rem_a, [w4s], [], after, _near_copies, "weights_near_wait")
        sem_b, rem_b, w4b, started = _split_start([w4w], [], 2, _pass_relay_copies, "weights_pass_start")
        waves["pass"] = (sem_b, rem_b)
        return w4b, started

    def near2(w4, after):
        (w4w,), _ = _split_wait(*waves["pass"], [w4], [], after, _pass_copy, "weights_pass_wait")
        return w4w

    def far(w4, after):
        (w4w,), _ = _split_wait(*waves["pass"], [w4], [], after, _relay_copy, "weights_far_wait")
        w4x, sem_c, rem_c, wo4s, started = _forward_then_start(w4w, wo4c, all_k, "weights_far_forward_out_start")
        (w4f,), _ = _split_wait(sem_c, rem_c, [w4x], [], started, _diag_forward_copy, "weights_far_forward_wait")
        waves["out"] = (sem_c, rem_c, wo4s)
        return w4f, started

    def w_out_arrived(after):
        sem_c, rem_c, wo4s = waves["out"]
        (wow,) = _halves_wait(sem_c, rem_c, [wo4s], after, all_k, "weights_out_wait")
        sem_d, rem_d, wof, started = _split_start([wow], [], 3, _forward_copies, "weights_out_forward_start")
        waves["out_forward"] = (sem_d, rem_d, wof)
        return started

    def w_out_gathered(after):
        sem_d, rem_d, wof = waves["out_forward"]
        (wo,), _ = _split_wait(sem_d, rem_d, [wof], [], after, _forward_copies, "weights_out_forward_wait")
        return wo.reshape(D, D)

    weights = dict(own=w_in[0], jvec=jvec, started=token, first=jvec ^ (1 + cvec), second=jvec ^ (2 - cvec), near=near, near2=near2,
                   far=far, out_arrived=w_out_arrived, out=w_out_gathered)

    exchange = _exchange_copies
    pending = {}

    def on_g_w_out(g_w_out):
        out = _split_start([g_w_out], [SDS((4, D // 8, D), f32)], 4, exchange, "grad_out_pair_start")
        pending["ex_out"] = out
        return out[4]

    def after_conv(dp8):
        ssem_o, rsem_o, g_o, land_o, _ = pending["ex_out"]
        (g_o,), (ex_o,) = _split_wait(ssem_o, rsem_o, [g_o], [land_o], dp8, exchange, "grad_out_pair_wait")
        to32, tob = _pair_sum_w_out(g_o, ex_o, cvec)
        out = _split_start([tob], [SDS((3, D // 8, D), bf16)], 3, _scatter_copies, "grad_out_chip_start")
        pending["sc_out"] = (out, to32)
        return out[4]

    def on_g_w_in(g_w_in):
        out = _split_start([g_w_in], [SDS((4, D // 2, D), f32)], 4, exchange, "grad_pair_start")
        pending["ex"] = (out[0], out[1], [out[2]], [out[3]])
        return out[4]

    def on_g_rpb(g_rpb):
        ex_ssem, ex_rsem, ex_srcs, ex_lands = pending["ex"]
        ex_g, ex = _split_wait(ex_ssem, ex_rsem, ex_srcs, ex_lands, g_rpb, exchange, "grad_pair_wait")
        t32, tb = _pair_sum_w_in(ex_g[0], ex[0], cvec)
        out = _split_start([tb], [SDS((3, D // 2, D), bf16)], 3, _scatter_copies, "grad_chip_start")
        pending["sc_in"] = (out, t32)
        return out[4]

    r = _local_step(x[0], ctx[0], loss_target[0], mrow, mrow_c, b_ada, norm_g, weights, q_norm_g, k_norm_g,
                    rpb[0], conv_w_full, conv_b,
                    dict(g_w_out=on_g_w_out, after_conv=after_conv, g_w_in=on_g_w_in, g_rpb=on_g_rpb))
    dm = jnp.concatenate([r["dshift"], r["dscale"], r["dgate"]], axis=1)
    dmc = jnp.concatenate([r["dshift_c"], r["dscale_c"], jnp.zeros((1, D), f32)], axis=1)
    pack_parts = [_rows128(dm), _rows128(dmc), _rows128(r["dng"]), _rows128(r["dng_c"]), _rows128(r["g_gq"]),
                  _rows128(r["g_gk"]), _rows128(r["g_gk_c"]), r["g_rpb"].reshape(H * N_DR, 128),
                  _rows128(r["g_conv_b"]), _rows128(r["g_conv_w"][0:3]), jnp.pad(r["loss_sum"], ((0, 0), (0, 127)))]
    pack = jnp.concatenate([jnp.pad(p, ((0, -p.shape[0] % 8), (0, 0))) for p in pack_parts], axis=0)
    assert pack.shape[0] == PK_ROWS
    gs = _split_start([_to_slot(pack, 8, dev)], [], 7, _gather8_copies, "gather_small_start")
    sc_o, to32 = pending["sc_out"]
    _, (ro2,) = _split_wait(sc_o[0], sc_o[1], [sc_o[2]], [sc_o[3]], gs[3], _scatter_copies, "grad_out_chip_wait")
    u_out = _chip_sum(to32, ro2, jvec, "chip_sum_w_out")
    (o_out,) = _sibling_send([u_out], "grad_out_pair_send")
    g_w_out_s, d_w_out, nm_w_out, nv_w_out = _adamw_halves(w_out[0], u_out, o_out, m_w_out[0], v_w_out[0], cvec,
                                                           "adamw_w_out")
    (gathered,), _ = _split_wait(gs[0], gs[1], [gs[2]], [], nm_w_out, _gather8_copies, "gather_small_wait")
    red, dm16 = _small_reduce(gathered)
    loss = red[RS_LOSS, 0] * (0.5 / D)

    g_w_ada_s, cpart = _w_ada_grad(sc16, dm16, w_ada_s, jvec)
    d_w_ada, nm_w_ada, nv_w_ada = _adamw(w_ada_s, g_w_ada_s, m_w_ada[0], v_w_ada[0], "adamw_w_ada")

    (cparts4,) = _chip_gather([cpart], "gather_c_ctx_parts", after=nm_w_ada)
    g_c_ctx = _c_ctx_grad(cparts4, c_ctx)

    sc_in, t32 = pending["sc_in"]
    _, (r2,) = _split_wait(sc_in[0], sc_in[1], [sc_in[2]], [sc_in[3]], g_c_ctx, _scatter_copies, "grad_chip_wait")
    u_in = _chip_sum(t32, r2, jvec, "chip_sum_w_in")

    t_rpb = lambda a: jnp.transpose(a, (0, 2, 1, 3)).reshape(N_DR, H, N_DC)
    t_cw = lambda a: jnp.transpose(a, (1, 0, 2))
    small = _adamw_small(
        red, g_c_ctx, jvec,
        [c_ctx, b_ada, norm_g, q_norm_g, k_norm_g, t_rpb(rpb), t_cw(conv_w), conv_b],
        [m_c_ctx, m_b_ada, m_norm_g, m_q_norm_g, m_k_norm_g, t_rpb(m_rpb), t_cw(m_conv_w), m_conv_b],
        [v_c_ctx, v_b_ada, v_norm_g, v_q_norm_g, v_k_norm_g, t_rpb(v_rpb), t_cw(v_conv_w), v_conv_b])
    for kind in small:
        kind[5] = jnp.transpose(kind[5].reshape(1, N_DR, H, N_DC), (0, 2, 1, 3))
        kind[6] = jnp.transpose(kind[6], (1, 0, 2))

    (o_in,) = _sibling_send([u_in], "grad_pair_send")
    g_w_in_s, d_w_in, nm_w_in, nv_w_in = _adamw_halves(w_in[0], u_in, o_in, m_w_in[0], v_w_in[0], cvec, "adamw_w_in")

    def ordered(kind, big_w_ada, big_w_in, big_w_out):
        s_c_ctx, s_b_ada, s_norm_g, s_q, s_k, s_rpb, s_conv_w, s_conv_b = small[kind]
        return [s_c_ctx, big_w_ada[None], s_b_ada, s_norm_g, big_w_in[None], s_q, s_k, s_rpb, s_conv_w,
                s_conv_b, big_w_out[None]]

    grads = ordered(0, g_w_ada_s, g_w_in_s, g_w_out_s)
    deltas = ordered(1, d_w_ada, d_w_in, d_w_out)
    new_m = ordered(2, nm_w_ada, nm_w_in, nm_w_out)
    new_v = ordered(3, nv_w_ada, nv_w_in, nv_w_out)
    return (loss, r["grad_x"][None], *grads, *deltas, *new_m, *new_v)
```

```python
import functools

import jax
import jax.numpy as jnp
import numpy as np
from jax import lax
from jax.experimental import pallas as pl
from jax.experimental.pallas import tpu as pltpu

f32, bf16, i32 = jnp.float32, jnp.bfloat16, jnp.int32
MESH = pl.DeviceIdType.MESH
HIGHEST = lax.Precision.HIGHEST

D = 1024
S = 2048
L = 256
GW = 64
ROWS = S // GW
H = 8
DH = 64
DA = H * DH
DC = 512
WIN_H, WIN_W = 8, 16
N_DR, N_DC = 2 * WIN_H - 1, 2 * WIN_W - 1
RMS_EPS = 1e-6
ROPE_THETA = 10000.0
QK_SCALE = DH ** -0.5
NEG = -1e30

QB = 128
NQB = S // QB
KR = 9
KB = KR * GW
TILE_GEOM = ((0, 0), (2, 0), (4, 0), (28, 23), (30, 23))
NT = len(TILE_GEOM)

ADAM_LR, ADAM_B1, ADAM_B2, ADAM_EPS, ADAM_WD, ADAM_STEP = 0.001, 0.9, 0.999, 1e-08, 0.01, 10

VMEM_SPEC = pl.BlockSpec(memory_space=pltpu.VMEM)
ANY_SPEC = pl.BlockSpec(memory_space=pl.ANY)
SMEM_SPEC = pl.BlockSpec(memory_space=pltpu.SMEM)
SDS = jax.ShapeDtypeStruct


_pallas_call = pl.pallas_call


def _hbm_call(body, *, out_shape, in_specs=None, out_specs=None, grid_spec=None, **kw):
    n_pre = 0
    if grid_spec is not None:
        ispecs, ospecs, n_pre = grid_spec.in_specs, grid_spec.out_specs, grid_spec.num_scalar_prefetch
        kw["grid_spec"] = grid_spec
    else:
        ispecs, ospecs = in_specs, out_specs
        kw.update(in_specs=in_specs, out_specs=out_specs)

    def blocked(spec):
        return isinstance(spec, pl.BlockSpec) and spec.block_shape is not None

    single = not isinstance(out_shape, (tuple, list))
    shapes = [out_shape] if single else list(out_shape)
    ospec_list = list(ospecs) if isinstance(ospecs, (tuple, list)) else [ospecs]
    shapes = [pltpu.HBM(s.shape, s.dtype) if blocked(sp) else s for s, sp in zip(shapes, ospec_list)]
    call = _pallas_call(body, out_shape=shapes[0] if single else tuple(shapes), **kw)

    def run(*args):
        arrays = [pltpu.with_memory_space_constraint(a, pltpu.HBM) if blocked(sp) else a
                  for a, sp in zip(args[n_pre:], ispecs)]
        return call(*args[:n_pre], *arrays)

    return run


def _cp(vmem_mb=None, **kw):
    if vmem_mb is not None:
        kw["vmem_limit_bytes"] = vmem_mb << 20
    return pltpu.CompilerParams(**kw)


def _silu(z):
    return z * jax.nn.sigmoid(z)


def _dsilu(z):
    sg = jax.nn.sigmoid(z)
    return sg * (1.0 + z * (1.0 - sg))


def _row_start(i):
    return min(max(i - WIN_H // 2, 0), ROWS - WIN_H)


def _my_pos():
    return lax.axis_index("x"), lax.axis_index("y"), lax.axis_index("c")


def _flip(v, bit):
    return 1 - v if bit else v


def _chip_gather(smalls, name, after=None):
    ns = len(smalls)

    def body(*refs):
        s_in, s_out = refs[:ns], refs[ns + 1:2 * ns + 1]
        ssem, rsem, lsem = refs[2 * ns + 1:]
        x, y, c = _my_pos()
        j = 2 * x + y
        chips = _peer_chips(x, y, c)
        local = [pltpu.make_async_copy(s_in[a], s_out[a].at[j], lsem.at[a]) for a in range(ns)]
        for cp in local:
            cp.start()
        sends = []
        for a in range(ns):
            for k in range(3):
                cp = pltpu.make_async_remote_copy(src_ref=s_in[a], dst_ref=s_out[a].at[j], send_sem=ssem.at[3 * a + k],
                                                  recv_sem=rsem.at[3 * a + k], device_id=chips[k][0], device_id_type=MESH)
                cp.start()
                sends.append(cp)
        for a in range(ns):
            for k in range(3):
                pltpu.make_async_remote_copy(src_ref=s_in[a], dst_ref=s_out[a].at[chips[k][1]], send_sem=ssem.at[3 * a + k],
                                             recv_sem=rsem.at[3 * a + k], device_id=chips[k][0],
                                             device_id_type=MESH).wait_recv()
        for cp in sends:
            cp.wait_send()
        for cp in local:
            cp.wait()

    return _hbm_call(
        body, name=name, out_shape=[SDS((4,) + a.shape, a.dtype) for a in smalls],
        in_specs=[VMEM_SPEC] * ns + [ANY_SPEC], out_specs=[VMEM_SPEC] * ns,
        scratch_shapes=[pltpu.SemaphoreType.DMA((3 * ns,)), pltpu.SemaphoreType.DMA((3 * ns,)),
                        pltpu.SemaphoreType.DMA((ns,))],
    )(*smalls, smalls[0] if after is None else after)


HBM_SPEC = pl.BlockSpec(memory_space=pltpu.HBM)
SEM_SPEC = pl.BlockSpec(memory_space=pltpu.SEMAPHORE)
DATAFLOW = pltpu.SideEffectType.DATAFLOW_SIDE_EFFECTING


def _peer_chips(x, y, c):
    out = []
    for k in range(1, 4):
        px, py = _flip(x, (k >> 1) & 1), _flip(y, k & 1)
        out.append(((px, py, c), 2 * px + py))
    return out


def _half_copies(srcs, dsts, ssem, rsem, which):
    x, y, c = _my_pos()
    j = 2 * x + y
    peers = _peer_chips(x, y, c)
    pairs = []
    for pos, group, k in which:
        half = srcs[pos].shape[1] // 2
        mine = pl.ds(pl.multiple_of(c * half, 8), half)
        dev, pj = peers[k]
        sem = 3 * group + k
        send = pltpu.make_async_remote_copy(src_ref=srcs[pos].at[j, mine], dst_ref=dsts[pos].at[j, mine],
                                            send_sem=ssem.at[sem], recv_sem=rsem.at[sem], device_id=dev,
                                            device_id_type=MESH)
        arrive = pltpu.make_async_remote_copy(src_ref=srcs[pos].at[j, mine], dst_ref=dsts[pos].at[pj, mine],
                                              send_sem=ssem.at[sem], recv_sem=rsem.at[sem], device_id=dev,
                                              device_id_type=MESH)
        pairs.append((send, arrive))
    return pairs


def _halves_wait(ssem, rsem, bigs, after, which, name):
    nb = len(bigs)

    def body(*refs):
        b_in = refs[:nb]
        ssem_ref, rsem_ref = refs[nb], refs[nb + 1]
        for send, arrive in _half_copies(b_in, b_in, ssem_ref, rsem_ref, which):
            send.wait_send()
            arrive.wait_recv()

    return _hbm_call(
        body, name=name, out_shape=tuple(pltpu.HBM(b.shape, b.dtype) for b in bigs),
        in_specs=[HBM_SPEC] * nb + [SEM_SPEC, SEM_SPEC, ANY_SPEC], out_specs=tuple([HBM_SPEC] * nb),
        input_output_aliases={a: a for a in range(nb)}, compiler_params=_cp(has_side_effects=DATAFLOW),
    )(*bigs, ssem, rsem, after)


FORWARD_SEM = 3


def _diag_forward_copy(srcs, dsts, ssem, rsem):
    x, y, c = _my_pos()
    half = srcs[0].shape[1] // 2
    diag = 3 - (2 * x + y)
    mine = pl.ds(pl.multiple_of(c * half, 8), half)
    other = pl.ds(pl.multiple_of((1 - c) * half, 8), half)
    return [_Copy(srcs[0].at[diag, mine], dsts[0].at[diag, mine], dsts[0].at[diag, other], ssem.at[FORWARD_SEM],
                  rsem.at[FORWARD_SEM], (x, y, 1 - c))]


def _forward_then_start(fwd, big, order, name):
    def body(f_in, b_in, f_out, ssem, rsem, b_out, token):
        _diag_forward_copy([f_in], [f_out], ssem, rsem)[0].start()
        for send, _ in _half_copies([b_in], [b_out], ssem, rsem, order):
            send.start()
        token[...] = jnp.zeros_like(token)

    n_sem = FORWARD_SEM + 1
    out_shape = (pltpu.HBM(fwd.shape, fwd.dtype), pltpu.SemaphoreType.DMA((n_sem,)), pltpu.SemaphoreType.DMA((n_sem,)),
                 pltpu.HBM(big.shape, big.dtype), SDS((8, 128), f32))
    return _hbm_call(
        body, name=name, out_shape=out_shape, in_specs=[HBM_SPEC, HBM_SPEC],
        out_specs=(HBM_SPEC, SEM_SPEC, SEM_SPEC, HBM_SPEC, VMEM_SPEC), input_output_aliases={0: 0, 1: 3},
        compiler_params=_cp(has_side_effects=DATAFLOW),
    )(*[pltpu.with_memory_space_constraint(b, pltpu.HBM) for b in (fwd, big)])


def _cast_to_slot(w, jvec, name, after=None):
    rows, cols = w.shape
    tr = 256

    def body(j_ref, w_ref, after_ref, o_ref):
        o_ref[...] = w_ref[...].astype(bf16)

    grid_spec = pltpu.PrefetchScalarGridSpec(
        num_scalar_prefetch=1, grid=(rows // tr,),
        in_specs=[pl.BlockSpec((tr, cols), lambda i, j: (i, 0)), ANY_SPEC],
        out_specs=pl.BlockSpec((None, tr, cols), lambda i, j: (j[0], i, 0)))
    return _hbm_call(body, name=name, out_shape=SDS((4, rows, cols), bf16),
                     grid_spec=grid_spec)(jvec, w, jvec if after is None else after)


def _exchange_copies(srcs, lands, ssem, rsem):
    x, y, c = _my_pos()
    half = srcs[0].shape[0] // 8
    cps = []
    for jb in range(4):
        src = srcs[0].at[pl.ds(pl.multiple_of((2 * jb + 1 - c) * half, 8), half)]
        cps.append(pltpu.make_async_remote_copy(src_ref=src, dst_ref=lands[0].at[jb], send_sem=ssem.at[jb],
                                                recv_sem=rsem.at[jb], device_id=(x, y, 1 - c), device_id_type=MESH))
    return cps


def _block_exchange_copies(srcs, lands, ssem, rsem):
    x, y, c = _my_pos()
    return [pltpu.make_async_remote_copy(src_ref=srcs[0].at[jb], dst_ref=lands[0].at[jb], send_sem=ssem.at[jb],
                                         recv_sem=rsem.at[jb], device_id=(x, y, 1 - c), device_id_type=MESH)
            for jb in range(4)]


def _scatter_copies(srcs, lands, ssem, rsem):
    x, y, c = _my_pos()
    cps = []
    for a in range(len(srcs)):
        for k, (dev, pj) in enumerate(_peer_chips(x, y, c)):
            cps.append(pltpu.make_async_remote_copy(src_ref=srcs[a].at[pj], dst_ref=lands[a].at[k],
                                                    send_sem=ssem.at[3 * a + k], recv_sem=rsem.at[3 * a + k],
                                                    device_id=dev, device_id_type=MESH))
    return cps


class _Copy:
    def __init__(self, src, dst, arrive, ssem, rsem, dev):
        make = lambda to: pltpu.make_async_remote_copy(src_ref=src, dst_ref=to, send_sem=ssem, recv_sem=rsem,
                                                       device_id=dev, device_id_type=MESH)
        send, arrival = make(dst), make(arrive)
        self.start, self.wait_send, self.wait_recv = send.start, send.wait_send, arrival.wait_recv


def _toward(x, y, along_x):
    return x + along_x * (1 - 2 * x), y + (1 - along_x) * (1 - 2 * y)


def _near_copies(srcs, dsts, ssem, rsem):
    x, y, c = _my_pos()
    px, py = _toward(x, y, c)
    j = 2 * x + y
    return [_Copy(srcs[0].at[j], dsts[0].at[j], dsts[0].at[2 * px + py], ssem.at[0], rsem.at[0], (px, py, c))]


def _pass_copy(srcs, dsts, ssem, rsem):
    x, y, c = _my_pos()
    px, py = _toward(x, y, c)
    qx, qy = _toward(x, y, 1 - c)
    got = 2 * px + py
    return [_Copy(srcs[0].at[got], dsts[0].at[got], dsts[0].at[2 * qx + qy], ssem.at[0], rsem.at[0], (x, y, 1 - c))]


def _relay_copy(srcs, dsts, ssem, rsem):
    x, y, c = _my_pos()
    px, py = _toward(x, y, c)
    qx, qy = _toward(x, y, 1 - c)
    half = srcs[0].shape[1] // 2
    mine = pl.ds(pl.multiple_of(c * half, 8), half)
    got, diag = 2 * px + py, 3 - (2 * x + y)
    return [_Copy(srcs[0].at[got, mine], dsts[0].at[got, mine], dsts[0].at[diag, mine], ssem.at[1], rsem.at[1],
                  (qx, qy, c))]


def _forward_copies(srcs, dsts, ssem, rsem):
    x, y, c = _my_pos()
    half = srcs[0].shape[1] // 2
    mine = pl.ds(pl.multiple_of(c * half, 8), half)
    other = pl.ds(pl.multiple_of((1 - c) * half, 8), half)
    return [_Copy(srcs[0].at[pj, mine], dsts[0].at[pj, mine], dsts[0].at[pj, other], ssem.at[k], rsem.at[k],
                  (x, y, 1 - c)) for k, (_, pj) in enumerate(_peer_chips(x, y, c))]


def _pass_relay_copies(srcs, dsts, ssem, rsem):
    return _pass_copy(srcs, dsts, ssem, rsem) + _relay_copy(srcs, dsts, ssem, rsem)


def _gather8_copies(srcs, dsts, ssem, rsem):
    x, y, c = _my_pos()
    me = 4 * x + 2 * y + c
    cps = []
    for a in range(len(srcs)):
        for k in range(1, 8):
            tgt = (_flip(x, (k >> 2) & 1), _flip(y, (k >> 1) & 1), _flip(c, k & 1))
            cps.append(_Copy(srcs[a].at[me], dsts[a].at[me], dsts[a].at[4 * tgt[0] + 2 * tgt[1] + tgt[2]],
                             ssem.at[7 * a + k - 1], rsem.at[7 * a + k - 1], tgt))
    return cps


def _gather4_copies(srcs, dsts, ssem, rsem):
    x, y, c = _my_pos()
    j = 2 * x + y
    cps = []
    for a in range(len(srcs)):
        for k, (dev, pj) in enumerate(_peer_chips(x, y, c)):
            cps.append(_Copy(srcs[a].at[j], dsts[a].at[j], dsts[a].at[pj], ssem.at[3 * a + k], rsem.at[3 * a + k], dev))
    return cps


def _to_slot(a, n, i):
    return lax.dynamic_update_slice(jnp.zeros((n,) + a.shape, a.dtype), a[None], (i,) + (0,) * a.ndim)


def _split_start(srcs, land_shapes, n_cp, make, name, after=None):
    ns, nl = len(srcs), len(land_shapes)
    n_in = ns + nl + (after is not None)

    def body(*refs):
        s_in = refs[:ns]
        ssem, rsem = refs[n_in], refs[n_in + 1]
        s_out = refs[n_in + 2:n_in + 2 + ns]
        l_out = refs[n_in + 2 + ns:n_in + 2 + ns + nl]
        token = refs[n_in + 2 + ns + nl]
        for cp in make(s_in, l_out if nl else s_out, ssem, rsem):
            cp.start()
        token[...] = jnp.zeros_like(token)

    lands = [pltpu.with_memory_space_constraint(lax.empty(sh.shape, sh.dtype), pltpu.HBM) for sh in land_shapes]
    out_shape = (pltpu.SemaphoreType.DMA((n_cp,)), pltpu.SemaphoreType.DMA((n_cp,)),
                 *[pltpu.HBM(b.shape, b.dtype) for b in srcs], *[pltpu.HBM(b.shape, b.dtype) for b in land_shapes],
                 SDS((8, 128), f32))
    return _hbm_call(
        body, name=name, out_shape=out_shape, in_specs=[HBM_SPEC] * (ns + nl) + [ANY_SPEC] * (after is not None),
        out_specs=(SEM_SPEC, SEM_SPEC, *[HBM_SPEC] * (ns + nl), VMEM_SPEC),
        input_output_aliases={i: 2 + i for i in range(ns + nl)}, compiler_params=_cp(has_side_effects=DATAFLOW),
    )(*[pltpu.with_memory_space_constraint(b, pltpu.HBM) for b in srcs], *lands, *([] if after is None else [after]))


def _split_wait(ssem, rsem, srcs, lands, after, make, name):
    ns, nl = len(srcs), len(lands)

    def body(*refs):
        s_in, l_in = refs[:ns], refs[ns:ns + nl]
        ssem_ref, rsem_ref = refs[ns + nl], refs[ns + nl + 1]
        for cp in make(s_in, l_in if nl else s_in, ssem_ref, rsem_ref):
            cp.wait_send()
            cp.wait_recv()

    outs = _hbm_call(
        body, name=name, out_shape=tuple(pltpu.HBM(b.shape, b.dtype) for b in (*srcs, *lands)),
        in_specs=[HBM_SPEC] * (ns + nl) + [SEM_SPEC, SEM_SPEC, ANY_SPEC], out_specs=tuple([HBM_SPEC] * (ns + nl)),
        input_output_aliases={i: i for i in range(ns + nl)}, compiler_params=_cp(has_side_effects=DATAFLOW),
    )(*srcs, *lands, ssem, rsem, after)
    return list(outs[:ns]), list(outs[ns:])


def _sibling_send(halves, name):
    n = len(halves)

    def body(*refs):
        ins, outs = refs[:n], refs[n:2 * n]
        ssem, rsem = refs[2 * n:]
        x, y, c = _my_pos()
        cps = []
        for a in range(n):
            cp = pltpu.make_async_remote_copy(src_ref=ins[a], dst_ref=outs[a], send_sem=ssem.at[a],
                                              recv_sem=rsem.at[a], device_id=(x, y, 1 - c), device_id_type=MESH)
            cp.start()
            cps.append(cp)
        for cp in cps:
            cp.wait_recv()
        for cp in cps:
            cp.wait_send()

    out_shape = [SDS(h.shape, h.dtype) for h in halves]
    return _hbm_call(
        body, name=name, out_shape=out_shape, in_specs=[ANY_SPEC] * n, out_specs=[ANY_SPEC] * n,
        scratch_shapes=[pltpu.SemaphoreType.DMA((n,)), pltpu.SemaphoreType.DMA((n,))],
    )(*halves)


def _adaln_shard(cc, w_ada_shard):
    def body(c_ref, w_ref, m_ref, sc_ref):
        sc = _silu(c_ref[...])
        sc_ref[...] = sc
        m_ref[...] = jnp.dot(sc, w_ref[...], precision=HIGHEST, preferred_element_type=f32)

    return _hbm_call(
        body, name="adaln_shard", out_shape=(SDS((16, w_ada_shard.shape[1]), f32), SDS((16, D), f32)),
        in_specs=[VMEM_SPEC, VMEM_SPEC], out_specs=(VMEM_SPEC, VMEM_SPEC), compiler_params=_cp(32),
    )(cc, w_ada_shard)


def _prenorm(xx, norm_g, mrow, b_ada, tm, name, after=None):
    n = xx.shape[0]

    def body(x_ref, g_ref, m_ref, b_ref, after_ref, h_ref):
        x = x_ref[...]
        shift = m_ref[:, 0:D] + b_ref[:, 0:D]
        scale = m_ref[:, D:2 * D] + b_ref[:, D:2 * D]
        r = lax.rsqrt(jnp.mean(x * x, axis=-1, keepdims=True) + RMS_EPS)
        y = (x * r) * g_ref[...]
        h_ref[...] = (y * (1.0 + scale) + shift).astype(bf16)

    row = lambda i: (i, 0)
    fixed = lambda i: (0, 0)
    return _hbm_call(
        body, name=name, out_shape=SDS((n, D), bf16), grid=(n // tm,),
        in_specs=[pl.BlockSpec((tm, D), row), pl.BlockSpec((1, D), fixed), pl.BlockSpec((1, 3 * D), fixed),
                  pl.BlockSpec((1, 3 * D), fixed), ANY_SPEC],
        out_specs=pl.BlockSpec((tm, D), row),
    )(xx, norm_g, mrow, b_ada, b_ada if after is None else after)


def _in_proj_own(h, w_own, jvec):
    tm = 512

    def body(j_ref, h_ref, w_ref, p_ref):
        p_ref[...] = jnp.dot(h_ref[...], w_ref[...].astype(bf16), preferred_element_type=f32)

    grid_spec = pltpu.PrefetchScalarGridSpec(
        num_scalar_prefetch=1, grid=(S // tm,),
        in_specs=[pl.BlockSpec((tm, D), lambda i, j: (i, 0)), pl.BlockSpec((D, D), lambda i, j: (0, 0))],
        out_specs=pl.BlockSpec((tm, D), lambda i, j: (i, j[0])))
    return _hbm_call(body, name="in_proj_own", out_shape=SDS((S, 4 * D), f32), grid_spec=grid_spec,
                     compiler_params=_cp(40))(jvec, h, w_own)


def _in_proj_block(h, w4, p, bvec, name, after=None):
    tm = 512

    def body(b_ref, h_ref, w_ref, p_in_ref, after_ref, p_ref):
        p_ref[...] = jnp.dot(h_ref[...], w_ref[...], preferred_element_type=f32)

    grid_spec = pltpu.PrefetchScalarGridSpec(
        num_scalar_prefetch=1, grid=(S // tm,),
        in_specs=[pl.BlockSpec((tm, D), lambda i, b: (i, 0)), pl.BlockSpec((None, D, D), lambda i, b: (b[0], 0, 0)),
                  ANY_SPEC, ANY_SPEC],
        out_specs=pl.BlockSpec((tm, D), lambda i, b: (i, b[0])))
    return _hbm_call(body, name=name, out_shape=SDS((S, 4 * D), f32), grid_spec=grid_spec,
                     input_output_aliases={3: 0})(bvec, h, w4, p, bvec if after is None else after)


def _ctx_proj(hc, w4):
    def body(h_ref, w0_ref, w1_ref, p_ref):
        hv = h_ref[...]
        p_ref[:, 0:DA] = jnp.dot(hv, w0_ref[:, DA:2 * DA], preferred_element_type=f32)
        p_ref[:, DA:2 * DA] = jnp.dot(hv, w1_ref[:, 0:DA], preferred_element_type=f32)

    return _hbm_call(
        body, name="ctx_proj", out_shape=SDS((L, 2 * DA), f32), grid=(1,),
        in_specs=[pl.BlockSpec((L, D), lambda i: (0, 0)), pl.BlockSpec((None, D, D), lambda i: (0, 0, 0)),
                  pl.BlockSpec((None, D, D), lambda i: (1, 0, 0))],
        out_specs=pl.BlockSpec((L, 2 * DA), lambda i: (0, 0)),
    )(hc, w4, w4)


def _head_ones():
    r = lax.broadcasted_iota(i32, (DA, DA), 0) // DH
    c = lax.broadcasted_iota(i32, (DA, DA), 1) // DH
    return (r == c).astype(bf16)


def _head_sum(v, ones_bd):
    hi = v.astype(bf16)
    lo = (v - hi.astype(f32)).astype(bf16)
    return jnp.dot(hi, ones_bd, preferred_element_type=f32) + jnp.dot(lo, ones_bd, preferred_element_type=f32)


def _swap16(v):
    lane = lax.broadcasted_iota(i32, v.shape, 1)
    return jnp.where((lane & 31) < 16, pltpu.roll(v, DA - 16, 1), pltpu.roll(v, 16, 1))


def _rope_block(ct_ref, rt_ref, tm):
    rows = [jnp.tile(rt_ref[8 * j:8 * j + 8, :], (GW // 8, 1)) for j in range(tm // GW)]
    return jnp.tile(ct_ref[...], (tm // GW, 1)) + jnp.concatenate(rows, axis=0)


def _rope_specs(tm):
    col = pl.BlockSpec((GW, DA), lambda i: (0, 0))
    row = pl.BlockSpec((8 * tm // GW, DA), lambda i: (i, 0))
    return [col, row, col, row]


def _qk_prep(p, gq, gk, rope):
    tm = 256

    def body(qk_ref, v_ref, gq_ref, gk_ref, cc_ref, cr_ref, sc_ref, sr_ref, qr_ref, qp_ref, kr_ref, vh_ref):
        ones_bd = _head_ones()
        cs, sn = _rope_block(cc_ref, cr_ref, tm), _rope_block(sc_ref, sr_ref, tm)
        q = qk_ref[:, 0:DA]
        k = qk_ref[:, DA:2 * DA]
        yq = (q * lax.rsqrt(_head_sum(q * q, ones_bd) * (1.0 / DH) + RMS_EPS)) * gq_ref[...]
        yk = (k * lax.rsqrt(_head_sum(k * k, ones_bd) * (1.0 / DH) + RMS_EPS)) * gk_ref[...]
        qr = (yq * cs + _swap16(yq) * sn) * QK_SCALE
        qp = yq * QK_SCALE
        kr = yk * cs + _swap16(yk) * sn
        vv = v_ref[...]
        for hh in range(H):
            sl = slice(hh * DH, (hh + 1) * DH)
            qr_ref[hh] = qr[:, sl].astype(bf16)
            qp_ref[hh] = qp[:, sl].astype(bf16)
            kr_ref[hh] = kr[:, sl].astype(bf16)
            vh_ref[hh] = vv[:, sl].astype(bf16)

    hm = SDS((H, S, DH), bf16)
    hspec = pl.BlockSpec((H, tm, DH), lambda i: (0, i, 0))
    fixed = lambda i: (0, 0)
    return _hbm_call(
        body, name="qk_prep", out_shape=(hm, hm, hm, hm), grid=(S // tm,),
        in_specs=[pl.BlockSpec((tm, 2 * DA), lambda i: (i, 0)), pl.BlockSpec((tm, DA), lambda i: (i, 2)),
                  pl.BlockSpec((1, DA), fixed), pl.BlockSpec((1, DA), fixed)] + _rope_specs(tm),
        out_specs=(hspec, hspec, hspec, hspec),
    )(p, p, gq, gk, *rope)


def _ctx_prep(pc, gk):
    def body(p_ref, gk_ref, kc_ref, vc_ref):
        ones_bd = _head_ones()
        k = p_ref[:, 0:DA]
        yk = (k * lax.rsqrt(_head_sum(k * k, ones_bd) * (1.0 / DH) + RMS_EPS)) * gk_ref[...]
        vv = p_ref[:, DA:2 * DA]
        for hh in range(H):
            sl = slice(hh * DH, (hh + 1) * DH)
            kc_ref[hh] = yk[:, sl].astype(bf16)
            vc_ref[hh] = vv[:, sl].astype(bf16)

    hm = SDS((H, L, DH), bf16)
    return _hbm_call(
        body, name="ctx_prep", out_shape=(hm, hm), in_specs=[VMEM_SPEC, VMEM_SPEC], out_specs=(VMEM_SPEC, VMEM_SPEC),
    )(pc, gk)


def _tile_pieces():
    out = []
    for (i0, u0) in TILE_GEOM:
        rows = []
        for j in range(2):
            i = i0 + j
            rs = _row_start(i)
            rows.append([(u0 + u - i + WIN_H - 1) if rs <= u0 + u < rs + WIN_H else None for u in range(KR)])
        out.append(rows)
    return out


def _bias_prep(rpb_rev_pad, after=None):
    pieces = _tile_pieces()

    def body(r_ref, after_ref, o_ref):
        rp = r_ref[...]
        xs = jnp.broadcast_to(rp[:, None, :], (N_DR, GW, 128)).reshape(N_DR * GW, 128)
        row = lax.broadcasted_iota(i32, xs.shape, 0)
        lane = lax.broadcasted_iota(i32, xs.shape, 1)
        for b in range(6):
            xs = jnp.where(((row >> b) & 1) == 1, pltpu.roll(xs, 1 << b, 1), xs)
        xs = pltpu.roll(xs, 128 - (WIN_W - 1), 1)
        k = row & (GW - 1)
        c0 = jnp.clip(lane - WIN_W // 2, 0, GW - WIN_W)
        xs = jnp.where((k >= c0) & (k < c0 + WIN_W), xs, NEG)
        neg = jnp.full((GW, GW), NEG, f32)
        for t in range(NT):
            for j in range(2):
                for u in range(KR):
                    dr = pieces[t][j][u]
                    piece = neg if dr is None else xs[dr * GW:(dr + 1) * GW, 0:GW]
                    o_ref[t, u * GW:(u + 1) * GW, j * GW:(j + 1) * GW] = piece

    return _hbm_call(
        body, name="bias_prep", out_shape=SDS((H, NT, KB, QB), f32), grid=(H,),
        in_specs=[pl.BlockSpec((None, N_DR, 128), lambda h: (h, 0, 0)), ANY_SPEC],
        out_specs=pl.BlockSpec((None, NT, KB, QB), lambda h: (h, 0, 0, 0)),
    )(rpb_rev_pad, rpb_rev_pad if after is None else after)


def _bias_tiles(rpb2, after=None):
    return _bias_prep(jnp.pad(rpb2[:, :, ::-1], ((0, 0), (0, 0), (0, 128 - N_DC))), after)


def _block_geom(b):
    qs = b * QB
    ks = min(max(2 * b - 4, 0), ROWS - KR) * GW
    t = b if b < 2 else (b - (NQB - NT) if b > NQB - 3 else 2)
    return qs, ks, t


def _tt(a, b):
    return lax.dot_general(a, b, (((1,), (1,)), ((), ())), preferred_element_type=f32)


def _tn(a, b):
    return lax.dot_general(a, b, (((0,), (0,)), ((), ())), preferred_element_type=f32)


def _softmax_t(s_lat, s_ctx):
    m = jnp.maximum(jnp.max(s_lat, axis=0, keepdims=True), jnp.max(s_ctx, axis=0, keepdims=True))
    e_lat = jnp.exp(s_lat - m)
    e_ctx = jnp.exp(s_ctx - m)
    inv = 1.0 / (jnp.sum(e_lat, axis=0, keepdims=True) + jnp.sum(e_ctx, axis=0, keepdims=True))
    return e_lat * inv, e_ctx * inv


def _staged(n_blocks, stages):
    held = [dict() for _ in stages]
    for step in range(n_blocks + len(stages) - 1):
        for s, fn in enumerate(stages):
            b = step - s
            if 0 <= b < n_blocks:
                held[s][b] = fn(b) if s == 0 else fn(b, held[s - 1].pop(b))


def _attn_fwd(qr, qp, kr, vh, kc, vc, btt):
    def body(qr_ref, qp_ref, kr_ref, v_ref, kc_ref, vc_ref, bt_ref, o_ref):
        kcv, vcv = kc_ref[...], vc_ref[...]

        def scores(b):
            qs, ks, t = _block_geom(b)
            return (_tt(kr_ref[ks:ks + KB, :], qr_ref[qs:qs + QB, :]) + bt_ref[t], _tt(kcv, qp_ref[qs:qs + QB, :]))

        def probs(b, sc):
            p_lat, p_ctx = _softmax_t(*sc)
            return p_lat.astype(bf16), p_ctx.astype(bf16)

        def values(b, p):
            qs, ks, _ = _block_geom(b)
            o_ref[qs:qs + QB, :] = _tn(p[0], v_ref[ks:ks + KB, :]) + _tn(p[1], vcv)

        _staged(NQB, (scores, probs, values))

    sq = pl.BlockSpec((None, S, DH), lambda h: (h, 0, 0))
    sc = pl.BlockSpec((None, L, DH), lambda h: (h, 0, 0))
    return _hbm_call(
        body, name="attn_fwd", out_shape=SDS((H, S, DH), f32), grid=(H,),
        in_specs=[sq, sq, sq, sq, sc, sc, pl.BlockSpec((None, NT, KB, QB), lambda h: (h, 0, 0, 0))],
        out_specs=sq, compiler_params=_cp(48),
    )(qr, qp, kr, vh, kc, vc, btt)


def _shift_rows(v, down):
    n = v.shape[0]
    row = lax.broadcasted_iota(i32, v.shape, 0)
    if down:
        return jnp.where(row == 0, 0.0, pltpu.roll(v, 1, 0))
    return jnp.where(row == n - 1, 0.0, pltpu.roll(v, n - 1, 0))


def _conv_specs():
    col = lambda off: pl.BlockSpec((S, 128), lambda i, off=off: (0, off + i))
    return [col(16), col(20), col(24), col(28), pl.BlockSpec((3, 128), lambda i: (0, i)),
            pl.BlockSpec((1, 128), lambda i: (0, i))]


def _conv_fwd(p, conv_w, conv_b, after=None):
    def body(u_ref, bg_ref, cg_ref, zc_ref, w_ref, b_ref, after_ref, o_ref):
        cu = cg_ref[...] * u_ref[...]
        cv = b_ref[...] + _shift_rows(cu, True) * w_ref[0:1, :]
        cv = cv + cu * w_ref[1:2, :]
        cv = cv + _shift_rows(cu, False) * w_ref[2:3, :]
        o_ref[...] = ((bg_ref[...] * cv) * _silu(zc_ref[...])).astype(bf16)

    return _hbm_call(
        body, name="conv_fwd", out_shape=SDS((S, DC), bf16), grid=(DC // 128,),
        in_specs=_conv_specs() + [ANY_SPEC], out_specs=pl.BlockSpec((S, 128), lambda i: (0, i)),
        compiler_params=_cp(40),
    )(p, p, p, p, conv_w, conv_b, conv_b if after is None else after)


DP_Q, DP_K, DP_V, DP_ZA, DP_U, DP_BG, DP_CG, DP_ZC = range(8)


def _out_proj_loss(o, p, conv_g, w_out, xx, tgt, mrow, b_ada):
    tm = 256

    def body(o_ref, za_ref, c_ref, w_ref, x_ref, t_ref, m_ref, b_ref,
             dy_ref, dconv_ref, dp_ref, do_ref, gwo_ref, dgate_ref, loss_ref):
        k = pl.program_id(0)

        @pl.when(k == 0)
        def _():
            gwo_ref[...] = jnp.zeros_like(gwo_ref)
            dgate_ref[...] = jnp.zeros_like(dgate_ref)
            loss_ref[0, 0] = 0.0

        gate = m_ref[:, 2 * D:3 * D] + b_ref[:, 2 * D:3 * D]
        za = za_ref[...]
        sz = _silu(za)
        om = _merge_heads(o_ref)
        av, cv = (om * sz).astype(bf16), c_ref[...]
        mo = jnp.dot(av, w_ref[0:DA, :], preferred_element_type=f32)
        mo = mo + jnp.dot(cv, w_ref[DA:DA + DC, :], preferred_element_type=f32)
        y = x_ref[...] + gate * mo
        diff = y - t_ref[...]
        loss_ref[0, 0] += jnp.sum(diff * diff)
        dy = diff * (1.0 / D)
        dy_ref[...] = dy
        dgate_ref[...] += jnp.sum(dy * mo, axis=0, keepdims=True)
        dmo = (dy * gate).astype(bf16)
        dmix = _tt(dmo, w_ref[...])
        dattn = dmix[:, 0:DA]
        dconv_ref[...] = dmix[:, DA:DA + DC]
        a = dattn * sz
        for hh in range(H):
            do_ref[hh] = a[:, hh * DH:(hh + 1) * DH].astype(bf16)
        dp_ref[...] = ((dattn * _dsilu(za)) * om).astype(bf16)
        gwo_ref[0:DA, :] += _tn(av, dmo)
        gwo_ref[DA:DA + DC, :] += _tn(cv, dmo)

    row = lambda i: (i, 0)
    fixed = lambda i: (0, 0)
    hspec = pl.BlockSpec((H, tm, DH), lambda i: (0, i, 0))
    return _hbm_call(
        body, name="out_proj_loss",
        out_shape=(SDS((S, D), f32), SDS((S, DC), f32), SDS((8, S, DA), bf16), SDS((H, S, DH), bf16),
                   SDS((D, D), f32), SDS((1, D), f32), SDS((1, 1), f32)),
        grid=(S // tm,),
        in_specs=[hspec, pl.BlockSpec((tm, DA), lambda i: (i, 3)), pl.BlockSpec((tm, DC), row),
                  pl.BlockSpec((D, D), fixed), pl.BlockSpec((tm, D), row), pl.BlockSpec((tm, D), row),
                  pl.BlockSpec((1, 3 * D), fixed), pl.BlockSpec((1, 3 * D), fixed)],
        out_specs=(pl.BlockSpec((tm, D), row), pl.BlockSpec((tm, DC), row),
                   pl.BlockSpec((None, tm, DA), lambda i: (DP_ZA, i, 0)), hspec, pl.BlockSpec((D, D), fixed),
                   pl.BlockSpec((1, D), fixed), SMEM_SPEC),
        compiler_params=_cp(56, dimension_semantics=("arbitrary",)),
    )(o, p, conv_g, w_out, xx, tgt, mrow, b_ada)


def _conv_bwd(dconv, p, conv_w, conv_b, dp8, after=None):
    def body(d_ref, u_ref, bg_ref, cg_ref, zc_ref, w_ref, b_ref, dp_in_ref, after_ref, dp_ref, gw_ref, gb_ref):
        du_ref, dbg_ref, dcg_ref, dzc_ref = dp_ref.at[0], dp_ref.at[1], dp_ref.at[2], dp_ref.at[3]
        dconv = d_ref[...]
        u, bg, cg, zc = u_ref[...], bg_ref[...], cg_ref[...], zc_ref[...]
        w0, w1, w2 = w_ref[0:1, :], w_ref[1:2, :], w_ref[2:3, :]
        cu = cg * u
        cu_m, cu_p = _shift_rows(cu, True), _shift_rows(cu, False)
        cv = b_ref[...] + cu_m * w0
        cv = cv + cu * w1
        cv = cv + cu_p * w2
        sz = _silu(zc)
        dbg_ref[...] = ((dconv * sz) * cv).astype(bf16)
        dzc_ref[...] = ((dconv * (bg * cv)) * _dsilu(zc)).astype(bf16)
        dcv = (dconv * sz) * bg
        gb_ref[...] = jnp.sum(dcv, axis=0, keepdims=True)
        gw_ref[0:1, :] = jnp.sum(dcv * cu_m, axis=0, keepdims=True)
        gw_ref[1:2, :] = jnp.sum(dcv * cu, axis=0, keepdims=True)
        gw_ref[2:3, :] = jnp.sum(dcv * cu_p, axis=0, keepdims=True)
        gw_ref[3:8, :] = jnp.zeros((5, 128), f32)
        dcu = _shift_rows(dcv, False) * w0 + dcv * w1 + _shift_rows(dcv, True) * w2
        dcg_ref[...] = (dcu * u).astype(bf16)
        du_ref[...] = (dcu * cg).astype(bf16)

    return _hbm_call(
        body, name="conv_bwd", out_shape=(SDS((8, S, DC), bf16), SDS((8, DC), f32), SDS((1, DC), f32)),
        grid=(DC // 128,),
        in_specs=[pl.BlockSpec((S, 128), lambda i: (0, i))] + _conv_specs() + [ANY_SPEC, ANY_SPEC],
        out_specs=(pl.BlockSpec((4, S, 128), lambda i: (DP_U // 4, 0, i)), pl.BlockSpec((8, 128), lambda i: (0, i)),
                   pl.BlockSpec((1, 128), lambda i: (0, i))),
        input_output_aliases={7: 0}, compiler_params=_cp(48),
    )(dconv, p, p, p, p, conv_w, conv_b, dp8, conv_b if after is None else after)


def _attn_bwd(qr, qp, kr, vh, kc, vc, btt, do, after=None):
    def body(qr_ref, qp_ref, kr_ref, v_ref, kc_ref, vc_ref, bt_ref, do_ref, after_ref,
             dqr_ref, dqp_ref, dkr_ref, dv_ref, dkc_ref, dvc_ref, dbt_ref):
        kcv, vcv = kc_ref[...], vc_ref[...]
        dkr_ref[...] = jnp.zeros_like(dkr_ref)
        dv_ref[...] = jnp.zeros_like(dv_ref)
        dbt_ref[...] = jnp.zeros_like(dbt_ref)
        ctx_acc = {}

        def products(b):
            qs, ks, t = _block_geom(b)
            dob = do_ref[qs:qs + QB, :]
            s_lat = _tt(kr_ref[ks:ks + KB, :], qr_ref[qs:qs + QB, :]) + bt_ref[t]
            s_ctx = _tt(kcv, qp_ref[qs:qs + QB, :])
            return s_lat, s_ctx, _tt(v_ref[ks:ks + KB, :], dob), _tt(vcv, dob)

        def score_grads(b, x):
            s_lat, s_ctx, dp_lat, dp_ctx = x
            p_lat, p_ctx = _softmax_t(s_lat, s_ctx)
            delta = jnp.sum(p_lat * dp_lat, axis=0, keepdims=True) + jnp.sum(p_ctx * dp_ctx, axis=0, keepdims=True)
            ds_lat = p_lat * (dp_lat - delta)
            ds_ctx = p_ctx * (dp_ctx - delta)
            return ds_lat, ds_lat.astype(bf16), ds_ctx.astype(bf16), p_lat.astype(bf16), p_ctx.astype(bf16)

        def operand_grads(b, y):
            qs, ks, t = _block_geom(b)
            ds_lat, dsb_lat, dsb_ctx, pb_lat, pb_ctx = y
            qrb, qpb, dob = qr_ref[qs:qs + QB, :], qp_ref[qs:qs + QB, :], do_ref[qs:qs + QB, :]
            dbt_ref[t] += ds_lat
            dqr_ref[qs:qs + QB, :] = _tn(dsb_lat, kr_ref[ks:ks + KB, :])
            dqp_ref[qs:qs + QB, :] = _tn(dsb_ctx, kcv)
            dkr_ref[ks:ks + KB, :] += jnp.dot(dsb_lat, qrb, preferred_element_type=f32)
            dv_ref[ks:ks + KB, :] += jnp.dot(pb_lat, dob, preferred_element_type=f32)
            dkc = jnp.dot(dsb_ctx, qpb, preferred_element_type=f32)
            dvc = jnp.dot(pb_ctx, dob, preferred_element_type=f32)
            ctx_acc["k"] = dkc if b == 0 else ctx_acc["k"] + dkc
            ctx_acc["v"] = dvc if b == 0 else ctx_acc["v"] + dvc

        _staged(NQB, (products, score_grads, operand_grads))
        dkc_ref[...] = ctx_acc["k"]
        dvc_ref[...] = ctx_acc["v"]

    sq = pl.BlockSpec((None, S, DH), lambda h: (h, 0, 0))
    sc = pl.BlockSpec((None, L, DH), lambda h: (h, 0, 0))
    sb = pl.BlockSpec((None, NT, KB, QB), lambda h: (h, 0, 0, 0))
    big, ctxs = SDS((H, S, DH), f32), SDS((H, L, DH), f32)
    return _hbm_call(
        body, name="attn_bwd", out_shape=(big, big, big, big, ctxs, ctxs, SDS((H, NT, KB, QB), f32)), grid=(H,),
        in_specs=[sq, sq, sq, sq, sc, sc, sb, sq, ANY_SPEC], out_specs=(sq, sq, sq, sq, sc, sc, sb),
        compiler_params=_cp(56),
    )(qr, qp, kr, vh, kc, vc, btt, do, do if after is None else after)


def _bias_bwd(dbtt, after=None):
    pieces = _tile_pieces()

    def body(d_ref, after_ref, o_ref, scr):
        scr[...] = jnp.zeros_like(scr)
        acc = [None] * N_DR
        for t in range(NT):
            for j in range(2):
                for u in range(KR):
                    dr = pieces[t][j][u]
                    if dr is None:
                        continue
                    piece = d_ref[t, u * GW:(u + 1) * GW, j * GW:(j + 1) * GW]
                    acc[dr] = piece if acc[dr] is None else acc[dr] + piece
        for dr in range(N_DR):
            scr[dr * GW:(dr + 1) * GW, 0:GW] = acc[dr]
        xs = pltpu.roll(scr[...], WIN_W - 1, 1)
        row = lax.broadcasted_iota(i32, xs.shape, 0)
        for b in range(6):
            xs = jnp.where(((row >> b) & 1) == 1, pltpu.roll(xs, 128 - (1 << b), 1), xs)
        rev = jnp.sum(xs.reshape(N_DR, GW, 128), axis=1)
        a = lax.broadcasted_iota(i32, (128, 128), 0)
        b = lax.broadcasted_iota(i32, (128, 128), 1)
        flip = ((a + b == N_DC - 1) & (a < N_DC)).astype(f32)
        o_ref[...] = jnp.dot(rev, flip, precision=HIGHEST, preferred_element_type=f32)

    return _hbm_call(
        body, name="bias_bwd", out_shape=SDS((H, N_DR, 128), f32), grid=(H,),
        in_specs=[pl.BlockSpec((None, NT, KB, QB), lambda h: (h, 0, 0, 0)), ANY_SPEC],
        out_specs=pl.BlockSpec((None, N_DR, 128), lambda h: (h, 0, 0)),
        scratch_shapes=[pltpu.VMEM((N_DR * GW, 128), f32)],
    )(dbtt, dbtt if after is None else after)


def _merge_heads(ref):
    return jnp.concatenate([ref[hh] for hh in range(H)], axis=1)


def _head_norm_bwd(xraw, gain, dy, ones_bd):
    r = lax.rsqrt(_head_sum(xraw * xraw, ones_bd) * (1.0 / DH) + RMS_EPS)
    xh = xraw * r
    gdy = dy * gain
    dx = r * (gdy - xh * (_head_sum(xh * gdy, ones_bd) * (1.0 / DH)))
    return dx, jnp.sum(dy * xh, axis=0, keepdims=True)


def _qk_bwd(dqr, dqp, dkr, dvh, p, gq, gk, rope, dp8):
    tm = 256

    def body(dqr_ref, dqp_ref, dkr_ref, dv_ref, qk_ref, gq_ref, gk_ref, cc_ref, cr_ref, sc_ref, sr_ref, dp_in_ref,
             dp_ref, ggq_ref, ggk_ref):
        dq_ref, dk_ref, dvo_ref = dp_ref.at[DP_Q], dp_ref.at[DP_K], dp_ref.at[DP_V]

        @pl.when(pl.program_id(0) == 0)
        def _():
            ggq_ref[...] = jnp.zeros_like(ggq_ref)
            ggk_ref[...] = jnp.zeros_like(ggk_ref)

        ones_bd = _head_ones()
        cs, sn = _rope_block(cc_ref, cr_ref, tm), _rope_block(sc_ref, sr_ref, tm)
        a = _merge_heads(dqr_ref)
        dyq = ((a * cs - _swap16(a) * sn) + _merge_heads(dqp_ref)) * QK_SCALE
        bk = _merge_heads(dkr_ref)
        dyk = bk * cs - _swap16(bk) * sn
        dq, gq_part = _head_norm_bwd(qk_ref[:, 0:DA], gq_ref[...], dyq, ones_bd)
        dk, gk_part = _head_norm_bwd(qk_ref[:, DA:2 * DA], gk_ref[...], dyk, ones_bd)
        dq_ref[...] = dq.astype(bf16)
        dk_ref[...] = dk.astype(bf16)
        dvo_ref[...] = _merge_heads(dv_ref).astype(bf16)
        ggq_ref[...] += gq_part
        ggk_ref[...] += gk_part

    hspec = pl.BlockSpec((H, tm, DH), lambda i: (0, i, 0))
    fixed = pl.BlockSpec((1, DA), lambda i: (0, 0))
    return _hbm_call(
        body, name="qk_bwd", out_shape=(SDS((8, S, DA), bf16), SDS((1, DA), f32), SDS((1, DA), f32)), grid=(S // tm,),
        in_specs=[hspec, hspec, hspec, hspec, pl.BlockSpec((tm, 2 * DA), lambda i: (i, 0)), fixed, fixed]
        + _rope_specs(tm) + [ANY_SPEC],
        out_specs=(pl.BlockSpec((3, tm, DA), lambda i: (0, i, 0)), fixed, fixed), input_output_aliases={11: 0},
        compiler_params=_cp(40, dimension_semantics=("arbitrary",)),
    )(dqr, dqp, dkr, dvh, p, gq, gk, *rope, dp8)


def _ctx_bwd(dkc, dvc, pc, gk):
    def body(dkc_ref, dvc_ref, p_ref, gk_ref, dk_ref, dv_ref, ggk_ref):
        ones_bd = _head_ones()
        dk, gk_part = _head_norm_bwd(p_ref[:, 0:DA], gk_ref[...], _merge_heads(dkc_ref), ones_bd)
        dk_ref[...] = dk.astype(bf16)
        dv_ref[...] = _merge_heads(dvc_ref).astype(bf16)
        ggk_ref[...] = gk_part

    piece = SDS((L, DA), bf16)
    return _hbm_call(
        body, name="ctx_bwd", out_shape=(piece, piece, SDS((1, DA), f32)), in_specs=[VMEM_SPEC] * 4,
        out_specs=(VMEM_SPEC,) * 3,
    )(dkc, dvc, pc, gk)


def _grad_w_in(h, dp8, hc, dkc_raw, dvc_m, half, name, after=None):
    tm = 512

    def body(half_ref, h_ref, p_ref, hc_ref, dk_ref, dv_ref, after_ref, g_ref):
        j, k = pl.program_id(0), pl.program_id(1)

        @pl.when(k == 0)
        def _():
            g_ref[...] = jnp.zeros_like(g_ref)

        @pl.when((k == 0) & (j == 0))
        def _():
            g_ref[:, DA:2 * DA] = _tn(hc_ref[...], dk_ref[...])

        @pl.when((k == 0) & (j == 1))
        def _():
            g_ref[:, 0:DA] = _tn(hc_ref[...], dv_ref[...])

        hv = h_ref[...]
        g_ref[:, 0:DA] += _tn(hv, p_ref[0])
        g_ref[:, DA:2 * DA] += _tn(hv, p_ref[1])

    fixed = lambda j, k, s: (0, 0)
    hd = D // 2
    grid_spec = pltpu.PrefetchScalarGridSpec(
        num_scalar_prefetch=1, grid=(4, S // tm),
        in_specs=[pl.BlockSpec((tm, hd), lambda j, k, s: (k, s[0])), pl.BlockSpec((2, tm, DA), lambda j, k, s: (j, k, 0)),
                  pl.BlockSpec((L, hd), lambda j, k, s: (0, s[0])), pl.BlockSpec((L, DA), fixed),
                  pl.BlockSpec((L, DA), fixed), ANY_SPEC],
        out_specs=pl.BlockSpec((None, hd, D), lambda j, k, s: (j, 0, 0)))
    return _hbm_call(
        body, name=name, out_shape=SDS((4, hd, D), f32), grid_spec=grid_spec,
        compiler_params=_cp(40, dimension_semantics=("arbitrary", "arbitrary")),
    )(half, h, dp8, hc, dkc_raw, dvc_m, half if after is None else after)


def _norm_mod_bwd(x, dh, g, scale):
    r = lax.rsqrt(jnp.mean(x * x, axis=-1, keepdims=True) + RMS_EPS)
    xh = x * r
    y = xh * g
    dshift = jnp.sum(dh, axis=0, keepdims=True)
    dscale = jnp.sum(dh * y, axis=0, keepdims=True)
    dyn = dh * (1.0 + scale)
    dg = jnp.sum(dyn * xh, axis=0, keepdims=True)
    gdy = dyn * g
    dx = r * (gdy - xh * jnp.mean(xh * gdy, axis=-1, keepdims=True))
    return dx, dshift, dscale, dg


def _dh_grad_x(dp8, w4, xx, dy, norm_g, mrow, b_ada, after=None):
    tm = 256

    def body(p_ref, w_ref, x_ref, dy_ref, g_ref, m_ref, b_ref, after_ref, gx_ref, dsh_ref, dsc_ref, dg_ref):
        @pl.when(pl.program_id(0) == 0)
        def _():
            dsh_ref[...] = jnp.zeros_like(dsh_ref)
            dsc_ref[...] = jnp.zeros_like(dsc_ref)
            dg_ref[...] = jnp.zeros_like(dg_ref)

        dh = None
        for j in range(4):
            for half in range(2):
                term = _tt(p_ref[2 * j + half], w_ref[j, :, half * DA:(half + 1) * DA])
                dh = term if dh is None else dh + term
        scale = m_ref[:, D:2 * D] + b_ref[:, D:2 * D]
        dx, dshift, dscale, dg = _norm_mod_bwd(x_ref[...], dh, g_ref[...], scale)
        gx_ref[...] = dy_ref[...] + dx
        dsh_ref[...] += dshift
        dsc_ref[...] += dscale
        dg_ref[...] += dg

    row = lambda i: (i, 0)
    fixed = lambda i: (0, 0)
    vec = SDS((1, D), f32)
    return _hbm_call(
        body, name="dh_grad_x", out_shape=(SDS((S, D), f32), vec, vec, vec), grid=(S // tm,),
        in_specs=[pl.BlockSpec((8, tm, DA), lambda i: (0, i, 0)), pl.BlockSpec((4, D, D), lambda i: (0, 0, 0)),
                  pl.BlockSpec((tm, D), row), pl.BlockSpec((tm, D), row), pl.BlockSpec((1, D), fixed),
                  pl.BlockSpec((1, 3 * D), fixed), pl.BlockSpec((1, 3 * D), fixed), ANY_SPEC],
        out_specs=(pl.BlockSpec((tm, D), row), pl.BlockSpec((1, D), fixed), pl.BlockSpec((1, D), fixed),
                   pl.BlockSpec((1, D), fixed)),
        compiler_params=_cp(56, dimension_semantics=("arbitrary",)),
    )(dp8, w4, xx, dy, norm_g, mrow, b_ada, b_ada if after is None else after)


def _dhc_sums(dkc_raw, dvc_m, w4, ctx2, norm_g, mrow_c, b_ada, after=None):
    def body(dk_ref, dv_ref, w0_ref, w1_ref, x_ref, g_ref, m_ref, b_ref, after_ref, dsh_ref, dsc_ref, dg_ref):
        dh = _tt(dk_ref[...], w0_ref[:, DA:2 * DA]) + _tt(dv_ref[...], w1_ref[:, 0:DA])
        scale = m_ref[:, D:2 * D] + b_ref[:, D:2 * D]
        _, dshift, dscale, dg = _norm_mod_bwd(x_ref[...], dh, g_ref[...], scale)
        dsh_ref[...] = dshift
        dsc_ref[...] = dscale
        dg_ref[...] = dg

    fixed = lambda i: (0, 0)
    vec = SDS((1, D), f32)
    vspec = pl.BlockSpec((1, D), fixed)
    return _hbm_call(
        body, name="dhc_sums", out_shape=(vec, vec, vec), grid=(1,),
        in_specs=[pl.BlockSpec((L, DA), fixed), pl.BlockSpec((L, DA), fixed),
                  pl.BlockSpec((None, D, D), lambda i: (0, 0, 0)), pl.BlockSpec((None, D, D), lambda i: (1, 0, 0)),
                  pl.BlockSpec((L, D), fixed), vspec, pl.BlockSpec((1, 3 * D), fixed), pl.BlockSpec((1, 3 * D), fixed),
                  ANY_SPEC],
        out_specs=(vspec, vspec, vspec), compiler_params=_cp(32),
    )(dkc_raw, dvc_m, w4, w4, ctx2, norm_g, mrow_c, b_ada, b_ada if after is None else after)


def _rope_tables():
    nf = DH // 4
    inv = np.float32(ROPE_THETA) ** (-np.arange(nf, dtype=np.float32) / np.float32(nf))
    ang_c = np.arange(GW, dtype=np.float32)[:, None] * inv
    ang_r = np.arange(ROWS, dtype=np.float32)[:, None] * inv
    zc, zr = np.zeros((GW, 2 * nf), np.float32), np.zeros((ROWS, 2 * nf), np.float32)
    ct_cos = np.tile(np.concatenate([zc, np.cos(ang_c), np.cos(ang_c)], axis=1), (1, H))
    ct_sin = np.tile(np.concatenate([zc, -np.sin(ang_c), np.sin(ang_c)], axis=1), (1, H))
    rt_cos = np.tile(np.concatenate([np.cos(ang_r), np.cos(ang_r), zr], axis=1), (1, H))
    rt_sin = np.tile(np.concatenate([-np.sin(ang_r), np.sin(ang_r), zr], axis=1), (1, H))
    rep8 = lambda t: np.ascontiguousarray(np.broadcast_to(t[:, None, :], (ROWS, 8, DA))).reshape(ROWS * 8, DA)
    return tuple(jnp.asarray(t, f32) for t in (ct_cos, rep8(rt_cos), ct_sin, rep8(rt_sin)))


def _local_step(xx, ctx2, tgt, mrow, mrow_c, b_ada, norm_g, weights, q_norm_g, k_norm_g, rpb2, conv_w_full, conv_b,
                hooks=None):
    hooks = hooks or {}
    gq = jnp.tile(q_norm_g, (1, H))
    gk = jnp.tile(k_norm_g, (1, H))
    rope = _rope_tables()

    h = _prenorm(xx, norm_g, mrow, b_ada, 256, "prenorm_x", after=weights.get("started"))
    hc = _prenorm(ctx2, norm_g, mrow_c, b_ada, L, "prenorm_ctx")
    jv = weights["jvec"]
    p = _in_proj_own(h, weights["own"], jv)
    btb = _bias_tiles(rpb2, after=p)
    w4, started = weights["near"](btb)
    p = _in_proj_block(h, w4, p, weights["first"], "in_proj_near", after=started)
    w4 = weights["near2"](w4, p)
    p = _in_proj_block(h, w4, p, weights["second"], "in_proj_near2")
    w4, started = weights["far"](w4, p)
    p = _in_proj_block(h, w4, p, jv ^ 3, "in_proj_far", after=started)
    pc = _ctx_proj(hc, w4)
    qr, qp, kr, vh = _qk_prep(p, gq, gk, rope)
    kc, vc = _ctx_prep(pc, gk)
    o = _attn_fwd(qr, qp, kr, vh, kc, vc, btb)
    started = weights["out_arrived"](o) if "out_arrived" in weights else None
    conv_g = _conv_fwd(p, conv_w_full, conv_b, after=started)
    w_out_full = weights["out"](conv_g)
    dy, dconv, dp8, do, g_w_out, dgate, loss_sum = _out_proj_loss(o, p, conv_g, w_out_full, xx, tgt, mrow, b_ada)
    started = hooks["g_w_out"](g_w_out) if "g_w_out" in hooks else None
    dp8, g_conv_w, g_conv_b = _conv_bwd(dconv, p, conv_w_full, conv_b, dp8, after=started)
    started = hooks["after_conv"](dp8) if "after_conv" in hooks else None
    dqr, dqp, dkr, dvh, dkc, dvc, dbtb = _attn_bwd(qr, qp, kr, vh, kc, vc, btb, do, after=started)
    dp8, g_gq, g_gk = _qk_bwd(dqr, dqp, dkr, dvh, p, gq, gk, rope, dp8)
    dkc_raw, dvc_m, g_gk_c = _ctx_bwd(dkc, dvc, pc, gk)
    first = hooks.get("first_half", jnp.zeros((1,), i32))
    g_first = _grad_w_in(h, dp8, hc, dkc_raw, dvc_m, first, "grad_w_in_first")
    started = hooks["g_w_in_first"](g_first) if "g_w_in_first" in hooks else None
    g_second = _grad_w_in(h, dp8, hc, dkc_raw, dvc_m, 1 - first, "grad_w_in_second", after=started)
    started = hooks["g_w_in_second"](g_second) if "g_w_in_second" in hooks else None
    dshift_c, dscale_c, dng_c = _dhc_sums(dkc_raw, dvc_m, w4, ctx2, norm_g, mrow_c, b_ada, after=started)
    g_rpb = _bias_bwd(dbtb, after=dshift_c)
    grad_x, dshift, dscale, dng = _dh_grad_x(dp8, w4, xx, dy, norm_g, mrow, b_ada, after=g_rpb)
    return dict(loss_sum=loss_sum, grad_x=grad_x, g_w_in=(g_first, g_second), g_w_out=g_w_out, g_conv_w=g_conv_w,
                g_conv_b=g_conv_b, g_rpb=g_rpb, g_gq=g_gq, g_gk=g_gk, g_gk_c=g_gk_c, dshift=dshift, dscale=dscale,
                dgate=dgate, dng=dng, dshift_c=dshift_c, dscale_c=dscale_c, dng_c=dng_c)


def _pair_sum_w_in(g, r):
    tr = 128

    def body(g_ref, r_ref, t32_ref, tb_ref):
        t = g_ref[...] + r_ref[...]
        t32_ref[...] = t
        tb_ref[...] = t.astype(bf16)

    half = D // 2
    spec = pl.BlockSpec((4, tr, D), lambda i: (0, i, 0))
    return _hbm_call(body, name="pair_sum_w_in", out_shape=(SDS((4, half, D), f32), SDS((4, half, D), bf16)),
                     grid=(half // tr,), in_specs=[spec, spec], out_specs=(spec, spec), compiler_params=_cp(40))(g, r)


def _pair_sum_w_out(g, r, cvec):
    hr = D // 8

    def body(c_ref, g0, g1, g2, g3, r_ref, t32_ref, tb_ref):
        for q, g_ref in enumerate((g0, g1, g2, g3)):
            t = g_ref[...] + r_ref[q]
            t32_ref[q] = t
            tb_ref[q] = t.astype(bf16)

    gspecs = [pl.BlockSpec((hr, D), lambda i, c, q=q: (2 * q + c[0], 0)) for q in range(4)]
    full = pl.BlockSpec((4, hr, D), lambda i, c: (0, 0, 0))
    grid_spec = pltpu.PrefetchScalarGridSpec(num_scalar_prefetch=1, grid=(1,), in_specs=gspecs + [full],
                                             out_specs=(full, full))
    return _hbm_call(body, name="pair_sum_w_out", out_shape=(SDS((4, hr, D), f32), SDS((4, hr, D), bf16)),
                          grid_spec=grid_spec)(cvec, g, g, g, g, r)


def _chip_sum(t32, r2, jvec, name):
    rows = t32.shape[1]
    tr = min(rows, 128)

    def body(j_ref, t_ref, r_ref, u_ref):
        u_ref[...] = ((t_ref[...] + r_ref[0].astype(f32)) + r_ref[1].astype(f32)) + r_ref[2].astype(f32)

    grid_spec = pltpu.PrefetchScalarGridSpec(
        num_scalar_prefetch=1, grid=(rows // tr,),
        in_specs=[pl.BlockSpec((None, tr, D), lambda i, j: (j[0], i, 0)), pl.BlockSpec((3, tr, D), lambda i, j: (0, i, 0))],
        out_specs=pl.BlockSpec((tr, D), lambda i, j: (i, 0)))
    return _hbm_call(body, name=name, out_shape=SDS((rows, D), f32), grid_spec=grid_spec)(jvec, t32, r2)


_PK = {}
_off = 0
for _name, _rows in (("dm", 24), ("dmc", 24), ("dng", 8), ("dng_c", 8), ("gq", 8), ("gk", 8), ("gk_c", 8),
                     ("rpb", H * N_DR), ("conv_b", 8), ("conv_w", 16), ("loss", 8)):
    _PK[_name] = (_off, _off + _rows)
    _off += _rows
PK_ROWS = _off
RS_B_ADA, RS_NORM_G, RS_GQ, RS_GK, RS_RPB, RS_CONV_B, RS_CONV_W, RS_DMC, RS_LOSS, RS_ROWS = (
    0, 24, 32, 40, 48, 168, 176, 192, 216, 224)


def _small_reduce(gathered):
    def body(g_ref, o_ref, dm_ref):
        a0 = _PK["dm"][0]
        dm_ref[...] = jnp.zeros_like(dm_ref)
        for b in range(8):
            for i in range(24):
                dm_ref[b:b + 1, 128 * i:128 * (i + 1)] = g_ref[b, a0 + i:a0 + i + 1, :]
        tot = g_ref[0]
        for b in range(1, 8):
            tot = tot + g_ref[b]

        def rows(name):
            a, z = _PK[name]
            return tot[a:z]

        o_ref[RS_B_ADA:RS_B_ADA + 24] = rows("dm") + rows("dmc")
        o_ref[RS_NORM_G:RS_NORM_G + 8] = rows("dng") + rows("dng_c")
        gq = jnp.broadcast_to(jnp.sum(rows("gq"), axis=0, keepdims=True), (8, 128))
        gk = jnp.broadcast_to(jnp.sum(rows("gk") + rows("gk_c"), axis=0, keepdims=True), (8, 128))
        o_ref[RS_GQ:RS_GQ + 8] = gq + pltpu.roll(gq, DH, 1)
        o_ref[RS_GK:RS_GK + 8] = gk + pltpu.roll(gk, DH, 1)
        o_ref[RS_RPB:RS_RPB + H * N_DR] = rows("rpb")
        o_ref[RS_CONV_B:RS_CONV_B + 8] = rows("conv_b")
        o_ref[RS_CONV_W:RS_CONV_W + 16] = rows("conv_w")
        dmc = rows("dmc")
        o_ref[RS_DMC:RS_DMC + 24] = dmc
        o_ref[RS_LOSS:RS_LOSS + 8] = rows("loss")
        for i in range(24):
            dm_ref[8:9, 128 * i:128 * (i + 1)] = dmc[i:i + 1]

    return _hbm_call(body, name="small_reduce", out_shape=(SDS((RS_ROWS, 128), f32), SDS((16, 3 * D), f32)),
                     in_specs=[VMEM_SPEC], out_specs=(VMEM_SPEC, VMEM_SPEC))(gathered)


def _w_ada_grad(sc16, dm16, w_ada_shard, jvec):
    ncol = w_ada_shard.shape[1]

    def body(j_ref, sc_ref, dm_ref, w_ref, g_ref, part_ref):
        dm = dm_ref[...]
        g_ref[...] = lax.dot_general(sc_ref[...], dm, (((0,), (0,)), ((), ())), precision=HIGHEST,
                                     preferred_element_type=f32)
        part_ref[...] = lax.dot_general(dm[8:16], w_ref[...], (((1,), (1,)), ((), ())), precision=HIGHEST,
                                        preferred_element_type=f32)

    fixed = lambda i, j: (0, 0)
    grid_spec = pltpu.PrefetchScalarGridSpec(
        num_scalar_prefetch=1, grid=(1,),
        in_specs=[pl.BlockSpec((16, D), fixed), pl.BlockSpec((16, ncol), lambda i, j: (0, j[0])),
                  pl.BlockSpec((D, ncol), fixed)],
        out_specs=(pl.BlockSpec((D, ncol), fixed), pl.BlockSpec((8, D), fixed)))
    return _pallas_call(body, name="w_ada_grad", out_shape=(SDS((D, ncol), f32), SDS((8, D), f32)),
                        grid_spec=grid_spec, compiler_params=_cp(40))(jvec, sc16, dm16, w_ada_shard)


def _c_ctx_grad(parts4, c_ctx):
    def body(p_ref, c_ref, o_ref):
        tot = ((p_ref[0] + p_ref[1]) + p_ref[2]) + p_ref[3]
        o_ref[...] = tot[0:1] * _dsilu(c_ref[...].reshape(1, D))

    return _pallas_call(body, name="c_ctx_grad", out_shape=SDS((1, D), f32), in_specs=[VMEM_SPEC, VMEM_SPEC],
                        out_specs=VMEM_SPEC)(parts4, c_ctx)


def _adamw(w, g, m, v, name):
    rows, cols = w.shape
    tr = 256 if rows % 256 == 0 else rows

    def body(w_ref, g_ref, m_ref, v_ref, d_ref, m2_ref, v2_ref):
        gv = g_ref[...]
        m2 = ADAM_B1 * m_ref[...] + (1.0 - ADAM_B1) * gv
        v2 = ADAM_B2 * v_ref[...] + (1.0 - ADAM_B2) * jnp.square(gv)
        m_hat = m2 / (1.0 - ADAM_B1 ** ADAM_STEP)
        v_hat = v2 / (1.0 - ADAM_B2 ** ADAM_STEP)
        d_ref[...] = -ADAM_LR * (m_hat / (jnp.sqrt(v_hat) + ADAM_EPS) + ADAM_WD * w_ref[...])
        m2_ref[...] = m2
        v2_ref[...] = v2

    spec = pl.BlockSpec((tr, cols), lambda i: (i, 0))
    shp = SDS((rows, cols), f32)
    return _hbm_call(body, name=name, out_shape=(shp, shp, shp), grid=(rows // tr,), in_specs=[spec] * 4,
                          out_specs=(spec, spec, spec))(w, g, m, v)


def _adamw_halves(w, g_mine, g_other, m, v, cvec, name):
    rows, cols = w.shape
    half = rows // 2
    tr = min(256, half)
    per_half = half // tr

    def body(c_ref, w_ref, ga_ref, gb_ref, m_ref, v_ref, g_ref, d_ref, m2_ref, v2_ref):
        in_my_half = (pl.program_id(0) // per_half) == c_ref[0]
        gv = jnp.where(in_my_half, ga_ref[...], gb_ref[...])
        g_ref[...] = gv
        m2 = ADAM_B1 * m_ref[...] + (1.0 - ADAM_B1) * gv
        v2 = ADAM_B2 * v_ref[...] + (1.0 - ADAM_B2) * jnp.square(gv)
        m_hat = m2 / (1.0 - ADAM_B1 ** ADAM_STEP)
        v_hat = v2 / (1.0 - ADAM_B2 ** ADAM_STEP)
        d_ref[...] = -ADAM_LR * (m_hat / (jnp.sqrt(v_hat) + ADAM_EPS) + ADAM_WD * w_ref[...])
        m2_ref[...] = m2
        v2_ref[...] = v2

    full = pl.BlockSpec((tr, cols), lambda i, c: (i, 0))
    part = pl.BlockSpec((tr, cols), lambda i, c: (i % per_half, 0))
    shp = SDS((rows, cols), f32)
    grid_spec = pltpu.PrefetchScalarGridSpec(num_scalar_prefetch=1, grid=(rows // tr,),
                                             in_specs=[full, part, part, full, full], out_specs=(full,) * 4)
    return _hbm_call(body, name=name, out_shape=(shp,) * 4, grid_spec=grid_spec)(cvec, w, g_mine, g_other, m, v)


def _adam_math(w, g, m, v):
    m2 = ADAM_B1 * m + (1.0 - ADAM_B1) * g
    v2 = ADAM_B2 * v + (1.0 - ADAM_B2) * jnp.square(g)
    m_hat = m2 / (1.0 - ADAM_B1 ** ADAM_STEP)
    v_hat = v2 / (1.0 - ADAM_B2 ** ADAM_STEP)
    return -ADAM_LR * (m_hat / (jnp.sqrt(v_hat) + ADAM_EPS) + ADAM_WD * w), m2, v2


def _adamw_small(red, g_c_ctx, jvec, ws, ms, vs):
    n = len(ws)

    def body(*refs):
        red_ref, gc_ref, j_ref = refs[:3]
        w_refs, m_refs, v_refs = refs[3:3 + n], refs[3 + n:3 + 2 * n], refs[3 + 2 * n:3 + 3 * n]
        outs = refs[3 + 3 * n:]
        g_out, d_out, m_out, v_out = outs[:n], outs[n:2 * n], outs[2 * n:3 * n], outs[3 * n:]
        chip = j_ref[0]
        lanes = lambda i: (slice(None), slice(128 * i, 128 * (i + 1)))
        row = lambda r0, i: (lambda: red_ref[r0 + i:r0 + i + 1, :])
        whole = (slice(None), slice(None))
        chunks = [
            [((slice(None),), lambda: gc_ref[...].reshape(D))],
            [(lanes(i), row(RS_B_ADA, i)) for i in range(3 * D // 128)],
            [(lanes(i), row(RS_NORM_G, i)) for i in range(D // 128)],
            [(whole, lambda: red_ref[RS_GQ:RS_GQ + 1, 0:DH])],
            [(whole, lambda: red_ref[RS_GK:RS_GK + 1, 0:DH])],
            [((dr,), (lambda dr=dr: red_ref[pl.ds(RS_RPB + dr, H, stride=N_DR), 0:N_DC])) for dr in range(N_DR)],
            [((r,), (lambda r=r: red_ref[pl.ds(RS_CONV_W + 4 * r + chip, 1), :])) for r in range(3)],
            [(lanes(i), row(RS_CONV_B, i)) for i in range(DC // 128)],
        ]
        for a in range(n):
            for idx, grad in chunks[a]:
                g = grad()
                d, m2, v2 = _adam_math(w_refs[a][idx], g, m_refs[a][idx], v_refs[a][idx])
                g_out[a][idx] = g
                d_out[a][idx] = d
                m_out[a][idx] = m2
                v_out[a][idx] = v2

    shapes = [SDS(w.shape, f32) for w in ws]
    res = _pallas_call(body, name="adamw_small", out_shape=shapes * 4,
                       in_specs=[VMEM_SPEC, VMEM_SPEC, SMEM_SPEC] + [VMEM_SPEC] * (3 * n),
                       out_specs=[VMEM_SPEC] * (4 * n))(red, g_c_ctx, jvec, *ws, *ms, *vs)
    return [list(res[k * n:(k + 1) * n]) for k in range(4)]


def _rows128(a):
    return a.reshape(-1, 128)


def kernel(x, c, ctx, c_ctx, w_ada, b_ada, norm_g, w_in, q_norm_g, k_norm_g, rpb, conv_w, conv_b, w_out, loss_target, m_c_ctx, m_w_ada, m_b_ada, m_norm_g, m_w_in, m_q_norm_g, m_k_norm_g, m_rpb, m_conv_w, m_conv_b, m_w_out, v_c_ctx, v_w_ada, v_b_ada, v_norm_g, v_w_in, v_q_norm_g, v_k_norm_g, v_rpb, v_conv_w, v_conv_b, v_w_out):
    xi, yi, ci = lax.axis_index("x"), lax.axis_index("y"), lax.axis_index("c")
    dev = 4 * xi + 2 * yi + ci
    chip = 2 * xi + yi
    cvec = jnp.reshape(ci, (1,)).astype(i32)
    jvec = jnp.reshape(chip, (1,)).astype(i32)
    w_ada_s = w_ada[0]
    ncol = w_ada_s.shape[1]

    gc = _split_start([_to_slot(c.reshape(8, 128), 8, dev)], [], 7, _gather8_copies, "gather_c_start")
    wo4c = _cast_to_slot(w_out[0], jvec, "cast_w_out", after=gc[3])
    w4c = _cast_to_slot(w_in[0], jvec, "cast_w_in", after=wo4c)
    (c8,), _ = _split_wait(gc[0], gc[1], [gc[2]], [], w4c, _gather8_copies, "gather_c_wait")
    cc = jnp.concatenate([c8.reshape(8, D), c_ctx.reshape(1, D), jnp.zeros((7, D), f32)], axis=0)
    m_shard, sc16 = _adaln_shard(cc, w_ada_s)

    conv_w_pad = jnp.pad(conv_w[0], ((0, 5), (0, 0)))
    gm = _split_start([_to_slot(m_shard, 4, chip), _to_slot(conv_w_pad, 4, chip)], [], 6, _gather4_copies,
                      "gather_mod_start")

    all_k = [(0, 0, 0), (0, 0, 1), (0, 0, 2)]
    sem_a, rem_a, w4s, token = _split_start([w4c], [], 1, _near_copies, "weights_near_start", after=gm[4])
    (m4, cw4), _ = _split_wait(gm[0], gm[1], [gm[2], gm[3]], [], token, _gather4_copies, "gather_mod_wait")
    m_full = jnp.transpose(m4, (1, 0, 2)).reshape(16, 4 * ncol)
    mrow = lax.dynamic_slice(m_full, (dev, 0), (1, 3 * D))
    mrow_c = m_full[8:9]
    conv_w_full = jnp.transpose(cw4[:, 0:3, :], (1, 0, 2)).reshape(3, DC)
    waves = {}

    def near(after):
        (w4w,), _ = _split_wait(sem_a, rem_a, [w4s], [], after, _near_copies, "weights_near_wait")
        sem_b, rem_b, w4b, started = _split_start([w4w], [], 2, _pass_relay_copies, "weights_pass_start")
        waves["pass"] = (sem_b, rem_b)
        return w4b, started

    def near2(w4, after):
        (w4w,), _ = _split_wait(*waves["pass"], [w4], [], after, _pass_copy, "weights_pass_wait")
        return w4w

    def far(w4, after):
        (w4w,), _ = _split_wait(*waves["pass"], [w4], [], after, _relay_copy, "weights_far_wait")
        w4x, sem_c, rem_c, wo4s, started = _forward_then_start(w4w, wo4c, all_k, "weights_far_forward_out_start")
        (w4f,), _ = _split_wait(sem_c, rem_c, [w4x], [], started, _diag_forward_copy, "weights_far_forward_wait")
        waves["out"] = (sem_c, rem_c, wo4s)
        return w4f, started

    def w_out_arrived(after):
        sem_c, rem_c, wo4s = waves["out"]
        (wow,) = _halves_wait(sem_c, rem_c, [wo4s], after, all_k, "weights_out_wait")
        sem_d, rem_d, wof, started = _split_start([wow], [], 3, _forward_copies, "weights_out_forward_start")
        waves["out_forward"] = (sem_d, rem_d, wof)
        return started

    def w_out_gathered(after):
        sem_d, rem_d, wof = waves["out_forward"]
        (wo,), _ = _split_wait(sem_d, rem_d, [wof], [], after, _forward_copies, "weights_out_forward_wait")
        return wo.reshape(D, D)

    weights = dict(own=w_in[0], jvec=jvec, started=token, first=jvec ^ (1 + cvec), second=jvec ^ (2 - cvec),
                   near=near, near2=near2, far=far, out_arrived=w_out_arrived, out=w_out_gathered)

    exchange = _exchange_copies
    pending = {}

    def on_g_w_out(g_w_out):
        out = _split_start([g_w_out], [SDS((4, D // 8, D), f32)], 4, exchange, "grad_out_pair_start")
        pending["ex_out"] = out
        return out[4]

    def after_conv(dp8):
        ssem_o, rsem_o, g_o, land_o, _ = pending["ex_out"]
        (g_o,), (ex_o,) = _split_wait(ssem_o, rsem_o, [g_o], [land_o], dp8, exchange, "grad_out_pair_wait")
        to32, tob = _pair_sum_w_out(g_o, ex_o, cvec)
        out = _split_start([tob], [SDS((3, D // 8, D), bf16)], 3, _scatter_copies, "grad_out_chip_start")
        pending["sc_out"] = (out, to32)
        return out[4]

    def on_g_w_in_first(g_first):
        out = _split_start([g_first], [SDS((4, D // 2, D), f32)], 4, _block_exchange_copies, "grad_pair_start")
        pending["ex"] = (out[0], out[1], [out[2]], [out[3]])
        return out[4]

    def on_g_w_in_second(g_second):
        ex_ssem, ex_rsem, ex_srcs, ex_lands = pending["ex"]
        _, ex = _split_wait(ex_ssem, ex_rsem, ex_srcs, ex_lands, g_second, _block_exchange_copies, "grad_pair_wait")
        t32, tb = _pair_sum_w_in(g_second, ex[0])
        out = _split_start([tb], [SDS((3, D // 2, D), bf16)], 3, _scatter_copies, "grad_chip_start")
        pending["sc_in"] = (out, t32)
        return out[4]

    r = _local_step(x[0], ctx[0], loss_target[0], mrow, mrow_c, b_ada, norm_g, weights, q_norm_g, k_norm_g,
                    rpb[0], conv_w_full, conv_b,
                    dict(g_w_out=on_g_w_out, after_conv=after_conv, first_half=1 - cvec,
                         g_w_in_first=on_g_w_in_first, g_w_in_second=on_g_w_in_second))
    dm = jnp.concatenate([r["dshift"], r["dscale"], r["dgate"]], axis=1)
    dmc = jnp.concatenate([r["dshift_c"], r["dscale_c"], jnp.zeros((1, D), f32)], axis=1)
    pack_parts = [_rows128(dm), _rows128(dmc), _rows128(r["dng"]), _rows128(r["dng_c"]), _rows128(r["g_gq"]),
                  _rows128(r["g_gk"]), _rows128(r["g_gk_c"]), r["g_rpb"].reshape(H * N_DR, 128),
                  _rows128(r["g_conv_b"]), _rows128(r["g_conv_w"][0:3]), jnp.pad(r["loss_sum"], ((0, 0), (0, 127)))]
    pack = jnp.concatenate([jnp.pad(p, ((0, -p.shape[0] % 8), (0, 0))) for p in pack_parts], axis=0)
    assert pack.shape[0] == PK_ROWS
    gs = _split_start([_to_slot(pack, 8, dev)], [], 7, _gather8_copies, "gather_small_start")
    sc_o, to32 = pending["sc_out"]
    _, (ro2,) = _split_wait(sc_o[0], sc_o[1], [sc_o[2]], [sc_o[3]], gs[3], _scatter_copies, "grad_out_chip_wait")
    u_out = _chip_sum(to32, ro2, jvec, "chip_sum_w_out")
    (o_out,) = _sibling_send([u_out], "grad_out_pair_send")
    g_w_out_s, d_w_out, nm_w_out, nv_w_out = _adamw_halves(w_out[0], u_out, o_out, m_w_out[0], v_w_out[0], cvec,
                                                           "adamw_w_out")
    (gathered,), _ = _split_wait(gs[0], gs[1], [gs[2]], [], nm_w_out, _gather8_copies, "gather_small_wait")
    red, dm16 = _small_reduce(gathered)
    loss = red[RS_LOSS, 0] * (0.5 / D)

    g_w_ada_s, cpart = _w_ada_grad(sc16, dm16, w_ada_s, jvec)
    d_w_ada, nm_w_ada, nv_w_ada = _adamw(w_ada_s, g_w_ada_s, m_w_ada[0], v_w_ada[0], "adamw_w_ada")

    (cparts4,) = _chip_gather([cpart], "gather_c_ctx_parts", after=nm_w_ada)
    g_c_ctx = _c_ctx_grad(cparts4, c_ctx)

    sc_in, t32 = pending["sc_in"]
    _, (r2,) = _split_wait(sc_in[0], sc_in[1], [sc_in[2]], [sc_in[3]], g_c_ctx, _scatter_copies, "grad_chip_wait")
    u_in = _chip_sum(t32, r2, jvec, "chip_sum_w_in")

    t_rpb = lambda a: jnp.transpose(a, (0, 2, 1, 3)).reshape(N_DR, H, N_DC)
    t_cw = lambda a: jnp.transpose(a, (1, 0, 2))
    small = _adamw_small(
        red, g_c_ctx, jvec,
        [c_ctx, b_ada, norm_g, q_norm_g, k_norm_g, t_rpb(rpb), t_cw(conv_w), conv_b],
        [m_c_ctx, m_b_ada, m_norm_g, m_q_norm_g, m_k_norm_g, t_rpb(m_rpb), t_cw(m_conv_w), m_conv_b],
        [v_c_ctx, v_b_ada, v_norm_g, v_q_norm_g, v_k_norm_g, t_rpb(v_rpb), t_cw(v_conv_w), v_conv_b])
    for kind in small:
        kind[5] = jnp.transpose(kind[5].reshape(1, N_DR, H, N_DC), (0, 2, 1, 3))
        kind[6] = jnp.transpose(kind[6], (1, 0, 2))

    (o_in,) = _sibling_send([u_in], "grad_pair_send")
    g_w_in_s, d_w_in, nm_w_in, nv_w_in = _adamw_halves(w_in[0], u_in, o_in, m_w_in[0], v_w_in[0], cvec, "adamw_w_in")

    def ordered(kind, big_w_ada, big_w_in, big_w_out):
        s_c_ctx, s_b_ada, s_norm_g, s_q, s_k, s_rpb, s_conv_w, s_conv_b = small[kind]
        return [s_c_ctx, big_w_ada[None], s_b_ada, s_norm_g, big_w_in[None], s_q, s_k, s_rpb, s_conv_w,
                s_conv_b, big_w_out[None]]

    grads = ordered(0, g_w_ada_s, g_w_in_s, g_w_out_s)
    deltas = ordered(1, d_w_ada, d_w_in, d_w_out)
    new_m = ordered(2, nm_w_ada, nm_w_in, nm_w_out)
    new_v = ordered(3, nv_w_ada, nv_w_in, nv_w_out)
    return (loss, r["grad_x"][None], *grads, *deltas, *new_m, *new_v)
```

```python
import functools

import jax
import jax.numpy as jnp
import numpy as np
from jax import lax
from jax.experimental import pallas as pl
from jax.experimental.pallas import tpu as pltpu

f32, bf16, i32 = jnp.float32, jnp.bfloat16, jnp.int32
MESH = pl.DeviceIdType.MESH
HIGHEST = lax.Precision.HIGHEST

D = 1024
S = 2048
L = 256
GW = 64
ROWS = S // GW
H = 8
DH = 64
DA = H * DH
DC = 512
WIN_H, WIN_W = 8, 16
N_DR, N_DC = 2 * WIN_H - 1, 2 * WIN_W - 1
RMS_EPS = 1e-6
ROPE_THETA = 10000.0
QK_SCALE = DH ** -0.5
NEG = -1e30

QB = 128
NQB = S // QB
KR = 9
KB = KR * GW
TILE_GEOM = ((0, 0), (2, 0), (4, 0), (28, 23), (30, 23))
NT = len(TILE_GEOM)

ADAM_LR, ADAM_B1, ADAM_B2, ADAM_EPS, ADAM_WD, ADAM_STEP = 0.001, 0.9, 0.999, 1e-08, 0.01, 10

VMEM_SPEC = pl.BlockSpec(memory_space=pltpu.VMEM)
ANY_SPEC = pl.BlockSpec(memory_space=pl.ANY)
SMEM_SPEC = pl.BlockSpec(memory_space=pltpu.SMEM)
SDS = jax.ShapeDtypeStruct


_pallas_call = pl.pallas_call


def _hbm_call(body, *, out_shape, in_specs=None, out_specs=None, grid_spec=None, **kw):
    n_pre = 0
    if grid_spec is not None:
        ispecs, ospecs, n_pre = grid_spec.in_specs, grid_spec.out_specs, grid_spec.num_scalar_prefetch
        kw["grid_spec"] = grid_spec
    else:
        ispecs, ospecs = in_specs, out_specs
        kw.update(in_specs=in_specs, out_specs=out_specs)

    def blocked(spec):
        return isinstance(spec, pl.BlockSpec) and spec.block_shape is not None

    single = not isinstance(out_shape, (tuple, list))
    shapes = [out_shape] if single else list(out_shape)
    ospec_list = list(ospecs) if isinstance(ospecs, (tuple, list)) else [ospecs]
    shapes = [pltpu.HBM(s.shape, s.dtype) if blocked(sp) else s for s, sp in zip(shapes, ospec_list)]
    call = _pallas_call(body, out_shape=shapes[0] if single else tuple(shapes), **kw)

    def run(*args):
        arrays = [pltpu.with_memory_space_constraint(a, pltpu.HBM) if blocked(sp) else a
                  for a, sp in zip(args[n_pre:], ispecs)]
        return call(*args[:n_pre], *arrays)

    return run


def _cp(vmem_mb=None, **kw):
    if vmem_mb is not None:
        kw["vmem_limit_bytes"] = vmem_mb << 20
    return pltpu.CompilerParams(**kw)


def _silu(z):
    return z * jax.nn.sigmoid(z)


def _dsilu(z):
    sg = jax.nn.sigmoid(z)
    return sg * (1.0 + z * (1.0 - sg))


def _row_start(i):
    return min(max(i - WIN_H // 2, 0), ROWS - WIN_H)


def _my_pos():
    return lax.axis_index("x"), lax.axis_index("y"), lax.axis_index("c")


def _flip(v, bit):
    return 1 - v if bit else v


def _chip_gather(smalls, name, after=None):
    ns = len(smalls)

    def body(*refs):
        s_in, s_out = refs[:ns], refs[ns + 1:2 * ns + 1]
        ssem, rsem, lsem = refs[2 * ns + 1:]
        x, y, c = _my_pos()
        j = 2 * x + y
        chips = _peer_chips(x, y, c)
        local = [pltpu.make_async_copy(s_in[a], s_out[a].at[j], lsem.at[a]) for a in range(ns)]
        for cp in local:
            cp.start()
        sends = []
        for a in range(ns):
            for k in range(3):
                cp = pltpu.make_async_remote_copy(src_ref=s_in[a], dst_ref=s_out[a].at[j], send_sem=ssem.at[3 * a + k],
                                                  recv_sem=rsem.at[3 * a + k], device_id=chips[k][0], device_id_type=MESH)
                cp.start()
                sends.append(cp)
        for a in range(ns):
            for k in range(3):
                pltpu.make_async_remote_copy(src_ref=s_in[a], dst_ref=s_out[a].at[chips[k][1]], send_sem=ssem.at[3 * a + k],
                                             recv_sem=rsem.at[3 * a + k], device_id=chips[k][0],
                                             device_id_type=MESH).wait_recv()
        for cp in sends:
            cp.wait_send()
        for cp in local:
            cp.wait()

    return _hbm_call(
        body, name=name, out_shape=[SDS((4,) + a.shape, a.dtype) for a in smalls],
        in_specs=[VMEM_SPEC] * ns + [ANY_SPEC], out_specs=[VMEM_SPEC] * ns,
        scratch_shapes=[pltpu.SemaphoreType.DMA((3 * ns,)), pltpu.SemaphoreType.DMA((3 * ns,)),
                        pltpu.SemaphoreType.DMA((ns,))],
    )(*smalls, smalls[0] if after is None else after)


HBM_SPEC = pl.BlockSpec(memory_space=pltpu.HBM)
SEM_SPEC = pl.BlockSpec(memory_space=pltpu.SEMAPHORE)
DATAFLOW = pltpu.SideEffectType.DATAFLOW_SIDE_EFFECTING


def _peer_chips(x, y, c):
    out = []
    for k in range(1, 4):
        px, py = _flip(x, (k >> 1) & 1), _flip(y, k & 1)
        out.append(((px, py, c), 2 * px + py))
    return out


def _half_copies(srcs, dsts, ssem, rsem, which):
    x, y, c = _my_pos()
    j = 2 * x + y
    peers = _peer_chips(x, y, c)
    pairs = []
    for pos, group, k in which:
        half = srcs[pos].shape[1] // 2
        mine = pl.ds(pl.multiple_of(c * half, 8), half)
        dev, pj = peers[k]
        sem = 3 * group + k
        send = pltpu.make_async_remote_copy(src_ref=srcs[pos].at[j, mine], dst_ref=dsts[pos].at[j, mine],
                                            send_sem=ssem.at[sem], recv_sem=rsem.at[sem], device_id=dev,
                                            device_id_type=MESH)
        arrive = pltpu.make_async_remote_copy(src_ref=srcs[pos].at[j, mine], dst_ref=dsts[pos].at[pj, mine],
                                              send_sem=ssem.at[sem], recv_sem=rsem.at[sem], device_id=dev,
                                              device_id_type=MESH)
        pairs.append((send, arrive))
    return pairs


def _halves_wait(ssem, rsem, bigs, after, which, name):
    nb = len(bigs)

    def body(*refs):
        b_in = refs[:nb]
        ssem_ref, rsem_ref = refs[nb], refs[nb + 1]
        for send, arrive in _half_copies(b_in, b_in, ssem_ref, rsem_ref, which):
            send.wait_send()
            arrive.wait_recv()

    return _hbm_call(
        body, name=name, out_shape=tuple(pltpu.HBM(b.shape, b.dtype) for b in bigs),
        in_specs=[HBM_SPEC] * nb + [SEM_SPEC, SEM_SPEC, ANY_SPEC], out_specs=tuple([HBM_SPEC] * nb),
        input_output_aliases={a: a for a in range(nb)}, compiler_params=_cp(has_side_effects=DATAFLOW),
    )(*bigs, ssem, rsem, after)


FORWARD_SEM = 3


def _diag_forward_copy(srcs, dsts, ssem, rsem):
    x, y, c = _my_pos()
    half = srcs[0].shape[1] // 2
    diag = 3 - (2 * x + y)
    mine = pl.ds(pl.multiple_of(c * half, 8), half)
    other = pl.ds(pl.multiple_of((1 - c) * half, 8), half)
    return [_Copy(srcs[0].at[diag, mine], dsts[0].at[diag, mine], dsts[0].at[diag, other], ssem.at[FORWARD_SEM],
                  rsem.at[FORWARD_SEM], (x, y, 1 - c))]


def _forward_then_start(fwd, big, order, name):
    def body(f_in, b_in, f_out, ssem, rsem, b_out, token):
        _diag_forward_copy([f_in], [f_out], ssem, rsem)[0].start()
        for send, _ in _half_copies([b_in], [b_out], ssem, rsem, order):
            send.start()
        token[...] = jnp.zeros_like(token)

    n_sem = FORWARD_SEM + 1
    out_shape = (pltpu.HBM(fwd.shape, fwd.dtype), pltpu.SemaphoreType.DMA((n_sem,)), pltpu.SemaphoreType.DMA((n_sem,)),
                 pltpu.HBM(big.shape, big.dtype), SDS((8, 128), f32))
    return _hbm_call(
        body, name=name, out_shape=out_shape, in_specs=[HBM_SPEC, HBM_SPEC],
        out_specs=(HBM_SPEC, SEM_SPEC, SEM_SPEC, HBM_SPEC, VMEM_SPEC), input_output_aliases={0: 0, 1: 3},
        compiler_params=_cp(has_side_effects=DATAFLOW),
    )(*[pltpu.with_memory_space_constraint(b, pltpu.HBM) for b in (fwd, big)])


def _cast_to_slot(w, jvec, name, after=None):
    rows, cols = w.shape
    tr = 256

    def body(j_ref, w_ref, after_ref, o_ref):
        o_ref[...] = w_ref[...].astype(bf16)

    grid_spec = pltpu.PrefetchScalarGridSpec(
        num_scalar_prefetch=1, grid=(rows // tr,),
        in_specs=[pl.BlockSpec((tr, cols), lambda i, j: (i, 0)), ANY_SPEC],
        out_specs=pl.BlockSpec((None, tr, cols), lambda i, j: (j[0], i, 0)))
    return _hbm_call(body, name=name, out_shape=SDS((4, rows, cols), bf16),
                     grid_spec=grid_spec)(jvec, w, jvec if after is None else after)


def _exchange_copies(srcs, lands, ssem, rsem):
    x, y, c = _my_pos()
    half = srcs[0].shape[0] // 8
    cps = []
    for jb in range(4):
        src = srcs[0].at[pl.ds(pl.multiple_of((2 * jb + 1 - c) * half, 8), half)]
        cps.append(pltpu.make_async_remote_copy(src_ref=src, dst_ref=lands[0].at[jb], send_sem=ssem.at[jb],
                                                recv_sem=rsem.at[jb], device_id=(x, y, 1 - c), device_id_type=MESH))
    return cps


def _block_exchange_copies(srcs, lands, ssem, rsem):
    x, y, c = _my_pos()
    return [pltpu.make_async_remote_copy(src_ref=srcs[0].at[jb], dst_ref=lands[0].at[jb], send_sem=ssem.at[jb],
                                         recv_sem=rsem.at[jb], device_id=(x, y, 1 - c), device_id_type=MESH)
            for jb in range(4)]


def _scatter_copies(srcs, lands, ssem, rsem):
    x, y, c = _my_pos()
    cps = []
    for a in range(len(srcs)):
        for k, (dev, pj) in enumerate(_peer_chips(x, y, c)):
            cps.append(pltpu.make_async_remote_copy(src_ref=srcs[a].at[pj], dst_ref=lands[a].at[k],
                                                    send_sem=ssem.at[3 * a + k], recv_sem=rsem.at[3 * a + k],
                                                    device_id=dev, device_id_type=MESH))
    return cps


class _Copy:
    def __init__(self, src, dst, arrive, ssem, rsem, dev):
        make = lambda to: pltpu.make_async_remote_copy(src_ref=src, dst_ref=to, send_sem=ssem, recv_sem=rsem,
                                                       device_id=dev, device_id_type=MESH)
        send, arrival = make(dst), make(arrive)
        self.start, self.wait_send, self.wait_recv = send.start, send.wait_send, arrival.wait_recv


def _toward(x, y, along_x):
    return x + along_x * (1 - 2 * x), y + (1 - along_x) * (1 - 2 * y)


def _near_copies(srcs, dsts, ssem, rsem):
    x, y, c = _my_pos()
    px, py = _toward(x, y, c)
    j = 2 * x + y
    return [_Copy(srcs[0].at[j], dsts[0].at[j], dsts[0].at[2 * px + py], ssem.at[0], rsem.at[0], (px, py, c))]


def _pass_copy(srcs, dsts, ssem, rsem):
    x, y, c = _my_pos()
    px, py = _toward(x, y, c)
    qx, qy = _toward(x, y, 1 - c)
    got = 2 * px + py
    return [_Copy(srcs[0].at[got], dsts[0].at[got], dsts[0].at[2 * qx + qy], ssem.at[0], rsem.at[0], (x, y, 1 - c))]


def _relay_copy(srcs, dsts, ssem, rsem):
    x, y, c = _my_pos()
    px, py = _toward(x, y, c)
    qx, qy = _toward(x, y, 1 - c)
    half = srcs[0].shape[1] // 2
    mine = pl.ds(pl.multiple_of(c * half, 8), half)
    got, diag = 2 * px + py, 3 - (2 * x + y)
    return [_Copy(srcs[0].at[got, mine], dsts[0].at[got, mine], dsts[0].at[diag, mine], ssem.at[1], rsem.at[1],
                  (qx, qy, c))]


def _forward_copies(srcs, dsts, ssem, rsem):
    x, y, c = _my_pos()
    half = srcs[0].shape[1] // 2
    mine = pl.ds(pl.multiple_of(c * half, 8), half)
    other = pl.ds(pl.multiple_of((1 - c) * half, 8), half)
    return [_Copy(srcs[0].at[pj, mine], dsts[0].at[pj, mine], dsts[0].at[pj, other], ssem.at[k], rsem.at[k],
                  (x, y, 1 - c)) for k, (_, pj) in enumerate(_peer_chips(x, y, c))]


def _pass_relay_copies(srcs, dsts, ssem, rsem):
    return _pass_copy(srcs, dsts, ssem, rsem) + _relay_copy(srcs, dsts, ssem, rsem)


def _gather8_copies(srcs, dsts, ssem, rsem):
    x, y, c = _my_pos()
    me = 4 * x + 2 * y + c
    cps = []
    for a in range(len(srcs)):
        for k in range(1, 8):
            tgt = (_flip(x, (k >> 2) & 1), _flip(y, (k >> 1) & 1), _flip(c, k & 1))
            cps.append(_Copy(srcs[a].at[me], dsts[a].at[me], dsts[a].at[4 * tgt[0] + 2 * tgt[1] + tgt[2]],
                             ssem.at[7 * a + k - 1], rsem.at[7 * a + k - 1], tgt))
    return cps


def _gather4_copies(srcs, dsts, ssem, rsem):
    x, y, c = _my_pos()
    j = 2 * x + y
    cps = []
    for a in range(len(srcs)):
        for k, (dev, pj) in enumerate(_peer_chips(x, y, c)):
            cps.append(_Copy(srcs[a].at[j], dsts[a].at[j], dsts[a].at[pj], ssem.at[3 * a + k], rsem.at[3 * a + k], dev))
    return cps


def _to_slot(a, n, i):
    return lax.dynamic_update_slice(jnp.zeros((n,) + a.shape, a.dtype), a[None], (i,) + (0,) * a.ndim)


def _split_start(srcs, land_shapes, n_cp, make, name, after=None):
    ns, nl = len(srcs), len(land_shapes)
    n_in = ns + nl + (after is not None)

    def body(*refs):
        s_in = refs[:ns]
        ssem, rsem = refs[n_in], refs[n_in + 1]
        s_out = refs[n_in + 2:n_in + 2 + ns]
        l_out = refs[n_in + 2 + ns:n_in + 2 + ns + nl]
        token = refs[n_in + 2 + ns + nl]
        for cp in make(s_in, l_out if nl else s_out, ssem, rsem):
            cp.start()
        token[...] = jnp.zeros_like(token)

    lands = [pltpu.with_memory_space_constraint(lax.empty(sh.shape, sh.dtype), pltpu.HBM) for sh in land_shapes]
    out_shape = (pltpu.SemaphoreType.DMA((n_cp,)), pltpu.SemaphoreType.DMA((n_cp,)),
                 *[pltpu.HBM(b.shape, b.dtype) for b in srcs], *[pltpu.HBM(b.shape, b.dtype) for b in land_shapes],
                 SDS((8, 128), f32))
    return _hbm_call(
        body, name=name, out_shape=out_shape, in_specs=[HBM_SPEC] * (ns + nl) + [ANY_SPEC] * (after is not None),
        out_specs=(SEM_SPEC, SEM_SPEC, *[HBM_SPEC] * (ns + nl), VMEM_SPEC),
        input_output_aliases={i: 2 + i for i in range(ns + nl)}, compiler_params=_cp(has_side_effects=DATAFLOW),
    )(*[pltpu.with_memory_space_constraint(b, pltpu.HBM) for b in srcs], *lands, *([] if after is None else [after]))


def _split_wait(ssem, rsem, srcs, lands, after, make, name):
    ns, nl = len(srcs), len(lands)

    def body(*refs):
        s_in, l_in = refs[:ns], refs[ns:ns + nl]
        ssem_ref, rsem_ref = refs[ns + nl], refs[ns + nl + 1]
        for cp in make(s_in, l_in if nl else s_in, ssem_ref, rsem_ref):
            cp.wait_send()
            cp.wait_recv()

    outs = _hbm_call(
        body, name=name, out_shape=tuple(pltpu.HBM(b.shape, b.dtype) for b in (*srcs, *lands)),
        in_specs=[HBM_SPEC] * (ns + nl) + [SEM_SPEC, SEM_SPEC, ANY_SPEC], out_specs=tuple([HBM_SPEC] * (ns + nl)),
        input_output_aliases={i: i for i in range(ns + nl)}, compiler_params=_cp(has_side_effects=DATAFLOW),
    )(*srcs, *lands, ssem, rsem, after)
    return list(outs[:ns]), list(outs[ns:])


def _sibling_send(halves, name):
    n = len(halves)

    def body(*refs):
        ins, outs = refs[:n], refs[n:2 * n]
        ssem, rsem = refs[2 * n:]
        x, y, c = _my_pos()
        cps = []
        for a in range(n):
            cp = pltpu.make_async_remote_copy(src_ref=ins[a], dst_ref=outs[a], send_sem=ssem.at[a],
                                              recv_sem=rsem.at[a], device_id=(x, y, 1 - c), device_id_type=MESH)
            cp.start()
            cps.append(cp)
        for cp in cps:
            cp.wait_recv()
        for cp in cps:
            cp.wait_send()

    out_shape = [SDS(h.shape, h.dtype) for h in halves]
    return _hbm_call(
        body, name=name, out_shape=out_shape, in_specs=[ANY_SPEC] * n, out_specs=[ANY_SPEC] * n,
        scratch_shapes=[pltpu.SemaphoreType.DMA((n,)), pltpu.SemaphoreType.DMA((n,))],
    )(*halves)


def _adaln_shard(cc, w_ada_shard):
    def body(c_ref, w_ref, m_ref, sc_ref):
        sc = _silu(c_ref[...])
        sc_ref[...] = sc
        m_ref[...] = jnp.dot(sc, w_ref[...], precision=HIGHEST, preferred_element_type=f32)

    return _hbm_call(
        body, name="adaln_shard", out_shape=(SDS((16, w_ada_shard.shape[1]), f32), SDS((16, D), f32)),
        in_specs=[VMEM_SPEC, VMEM_SPEC], out_specs=(VMEM_SPEC, VMEM_SPEC), compiler_params=_cp(32),
    )(cc, w_ada_shard)


def _prenorm(xx, norm_g, mrow, b_ada, tm, name, after=None):
    n = xx.shape[0]

    def body(x_ref, g_ref, m_ref, b_ref, after_ref, h_ref):
        x = x_ref[...]
        shift = m_ref[:, 0:D] + b_ref[:, 0:D]
        scale = m_ref[:, D:2 * D] + b_ref[:, D:2 * D]
        r = lax.rsqrt(jnp.mean(x * x, axis=-1, keepdims=True) + RMS_EPS)
        y = (x * r) * g_ref[...]
        h_ref[...] = (y * (1.0 + scale) + shift).astype(bf16)

    row = lambda i: (i, 0)
    fixed = lambda i: (0, 0)
    return _hbm_call(
        body, name=name, out_shape=SDS((n, D), bf16), grid=(n // tm,),
        in_specs=[pl.BlockSpec((tm, D), row), pl.BlockSpec((1, D), fixed), pl.BlockSpec((1, 3 * D), fixed),
                  pl.BlockSpec((1, 3 * D), fixed), ANY_SPEC],
        out_specs=pl.BlockSpec((tm, D), row),
    )(xx, norm_g, mrow, b_ada, b_ada if after is None else after)


def _in_proj_own(h, w_own, jvec):
    tm = 512

    def body(j_ref, h_ref, w_ref, p_ref):
        p_ref[...] = jnp.dot(h_ref[...], w_ref[...].astype(bf16), preferred_element_type=f32)

    grid_spec = pltpu.PrefetchScalarGridSpec(
        num_scalar_prefetch=1, grid=(S // tm,),
        in_specs=[pl.BlockSpec((tm, D), lambda i, j: (i, 0)), pl.BlockSpec((D, D), lambda i, j: (0, 0))],
        out_specs=pl.BlockSpec((tm, D), lambda i, j: (i, j[0])))
    return _hbm_call(body, name="in_proj_own", out_shape=SDS((S, 4 * D), f32), grid_spec=grid_spec,
                     compiler_params=_cp(40))(jvec, h, w_own)


def _in_proj_block(h, w4, p, bvec, name, after=None):
    tm = 512

    def body(b_ref, h_ref, w_ref, p_in_ref, after_ref, p_ref):
        p_ref[...] = jnp.dot(h_ref[...], w_ref[...], preferred_element_type=f32)

    grid_spec = pltpu.PrefetchScalarGridSpec(
        num_scalar_prefetch=1, grid=(S // tm,),
        in_specs=[pl.BlockSpec((tm, D), lambda i, b: (i, 0)), pl.BlockSpec((None, D, D), lambda i, b: (b[0], 0, 0)),
                  ANY_SPEC, ANY_SPEC],
        out_specs=pl.BlockSpec((tm, D), lambda i, b: (i, b[0])))
    return _hbm_call(body, name=name, out_shape=SDS((S, 4 * D), f32), grid_spec=grid_spec,
                     input_output_aliases={3: 0})(bvec, h, w4, p, bvec if after is None else after)


def _ctx_proj(hc, w4):
    def body(h_ref, w0_ref, w1_ref, p_ref):
        hv = h_ref[...]
        p_ref[:, 0:DA] = jnp.dot(hv, w0_ref[:, DA:2 * DA], preferred_element_type=f32)
        p_ref[:, DA:2 * DA] = jnp.dot(hv, w1_ref[:, 0:DA], preferred_element_type=f32)

    return _hbm_call(
        body, name="ctx_proj", out_shape=SDS((L, 2 * DA), f32), grid=(1,),
        in_specs=[pl.BlockSpec((L, D), lambda i: (0, 0)), pl.BlockSpec((None, D, D), lambda i: (0, 0, 0)),
                  pl.BlockSpec((None, D, D), lambda i: (1, 0, 0))],
        out_specs=pl.BlockSpec((L, 2 * DA), lambda i: (0, 0)),
    )(hc, w4, w4)


def _head_ones():
    r = lax.broadcasted_iota(i32, (DA, DA), 0) // DH
    c = lax.broadcasted_iota(i32, (DA, DA), 1) // DH
    return (r == c).astype(bf16)


def _head_sum(v, ones_bd):
    hi = v.astype(bf16)
    lo = (v - hi.astype(f32)).astype(bf16)
    return jnp.dot(hi, ones_bd, preferred_element_type=f32) + jnp.dot(lo, ones_bd, preferred_element_type=f32)


def _swap16(v):
    lane = lax.broadcasted_iota(i32, v.shape, 1)
    return jnp.where((lane & 31) < 16, pltpu.roll(v, DA - 16, 1), pltpu.roll(v, 16, 1))


def _rope_block(ct_ref, rt_ref, tm):
    rows = [jnp.tile(rt_ref[8 * j:8 * j + 8, :], (GW // 8, 1)) for j in range(tm // GW)]
    return jnp.tile(ct_ref[...], (tm // GW, 1)) + jnp.concatenate(rows, axis=0)


def _rope_specs(tm):
    col = pl.BlockSpec((GW, DA), lambda i: (0, 0))
    row = pl.BlockSpec((8 * tm // GW, DA), lambda i: (i, 0))
    return [col, row, col, row]


def _qk_prep(p, gq, gk, rope):
    tm = 256

    def body(qk_ref, v_ref, gq_ref, gk_ref, cc_ref, cr_ref, sc_ref, sr_ref, qr_ref, qp_ref, kr_ref, vh_ref):
        ones_bd = _head_ones()
        cs, sn = _rope_block(cc_ref, cr_ref, tm), _rope_block(sc_ref, sr_ref, tm)
        q = qk_ref[:, 0:DA]
        k = qk_ref[:, DA:2 * DA]
        yq = (q * lax.rsqrt(_head_sum(q * q, ones_bd) * (1.0 / DH) + RMS_EPS)) * gq_ref[...]
        yk = (k * lax.rsqrt(_head_sum(k * k, ones_bd) * (1.0 / DH) + RMS_EPS)) * gk_ref[...]
        qr = (yq * cs + _swap16(yq) * sn) * QK_SCALE
        qp = yq * QK_SCALE
        kr = yk * cs + _swap16(yk) * sn
        vv = v_ref[...]
        for hh in range(H):
            sl = slice(hh * DH, (hh + 1) * DH)
            qr_ref[hh] = qr[:, sl].astype(bf16)
            qp_ref[hh] = qp[:, sl].astype(bf16)
            kr_ref[hh] = kr[:, sl].astype(bf16)
            vh_ref[hh] = vv[:, sl].astype(bf16)

    hm = SDS((H, S, DH), bf16)
    hspec = pl.BlockSpec((H, tm, DH), lambda i: (0, i, 0))
    fixed = lambda i: (0, 0)
    return _hbm_call(
        body, name="qk_prep", out_shape=(hm, hm, hm, hm), grid=(S // tm,),
        in_specs=[pl.BlockSpec((tm, 2 * DA), lambda i: (i, 0)), pl.BlockSpec((tm, DA), lambda i: (i, 2)),
                  pl.BlockSpec((1, DA), fixed), pl.BlockSpec((1, DA), fixed)] + _rope_specs(tm),
        out_specs=(hspec, hspec, hspec, hspec),
    )(p, p, gq, gk, *rope)


def _ctx_prep(pc, gk):
    def body(p_ref, gk_ref, kc_ref, vc_ref):
        ones_bd = _head_ones()
        k = p_ref[:, 0:DA]
        yk = (k * lax.rsqrt(_head_sum(k * k, ones_bd) * (1.0 / DH) + RMS_EPS)) * gk_ref[...]
        vv = p_ref[:, DA:2 * DA]
        for hh in range(H):
            sl = slice(hh * DH, (hh + 1) * DH)
            kc_ref[hh] = yk[:, sl].astype(bf16)
            vc_ref[hh] = vv[:, sl].astype(bf16)

    hm = SDS((H, L, DH), bf16)
    return _hbm_call(
        body, name="ctx_prep", out_shape=(hm, hm), in_specs=[VMEM_SPEC, VMEM_SPEC], out_specs=(VMEM_SPEC, VMEM_SPEC),
    )(pc, gk)


def _tile_pieces():
    out = []
    for (i0, u0) in TILE_GEOM:
        rows = []
        for j in range(2):
            i = i0 + j
            rs = _row_start(i)
            rows.append([(u0 + u - i + WIN_H - 1) if rs <= u0 + u < rs + WIN_H else None for u in range(KR)])
        out.append(rows)
    return out


def _bias_prep(rpb_rev_pad, after=None):
    pieces = _tile_pieces()

    def body(r_ref, after_ref, o_ref):
        rp = r_ref[...]
        xs = jnp.broadcast_to(rp[:, None, :], (N_DR, GW, 128)).reshape(N_DR * GW, 128)
        row = lax.broadcasted_iota(i32, xs.shape, 0)
        lane = lax.broadcasted_iota(i32, xs.shape, 1)
        for b in range(6):
            xs = jnp.where(((row >> b) & 1) == 1, pltpu.roll(xs, 1 << b, 1), xs)
        xs = pltpu.roll(xs, 128 - (WIN_W - 1), 1)
        k = row & (GW - 1)
        c0 = jnp.clip(lane - WIN_W // 2, 0, GW - WIN_W)
        xs = jnp.where((k >= c0) & (k < c0 + WIN_W), xs, NEG)
        neg = jnp.full((GW, GW), NEG, f32)
        for t in range(NT):
            for j in range(2):
                for u in range(KR):
                    dr = pieces[t][j][u]
                    piece = neg if dr is None else xs[dr * GW:(dr + 1) * GW, 0:GW]
                    o_ref[t, u * GW:(u + 1) * GW, j * GW:(j + 1) * GW] = piece

    return _hbm_call(
        body, name="bias_prep", out_shape=SDS((H, NT, KB, QB), f32), grid=(H,),
        in_specs=[pl.BlockSpec((None, N_DR, 128), lambda h: (h, 0, 0)), ANY_SPEC],
        out_specs=pl.BlockSpec((None, NT, KB, QB), lambda h: (h, 0, 0, 0)),
    )(rpb_rev_pad, rpb_rev_pad if after is None else after)


def _bias_tiles(rpb2, after=None):
    return _bias_prep(jnp.pad(rpb2[:, :, ::-1], ((0, 0), (0, 0), (0, 128 - N_DC))), after)


def _block_geom(b):
    qs = b * QB
    ks = min(max(2 * b - 4, 0), ROWS - KR) * GW
    t = b if b < 2 else (b - (NQB - NT) if b > NQB - 3 else 2)
    return qs, ks, t


def _tt(a, b):
    return lax.dot_general(a, b, (((1,), (1,)), ((), ())), preferred_element_type=f32)


def _tn(a, b):
    return lax.dot_general(a, b, (((0,), (0,)), ((), ())), preferred_element_type=f32)


def _softmax_t(s_lat, s_ctx):
    m = jnp.maximum(jnp.max(s_lat, axis=0, keepdims=True), jnp.max(s_ctx, axis=0, keepdims=True))
    e_lat = jnp.exp(s_lat - m)
    e_ctx = jnp.exp(s_ctx - m)
    inv = 1.0 / (jnp.sum(e_lat, axis=0, keepdims=True) + jnp.sum(e_ctx, axis=0, keepdims=True))
    return e_lat * inv, e_ctx * inv


def _staged(n_blocks, stages):
    held = [dict() for _ in stages]
    for step in range(n_blocks + len(stages) - 1):
        for s, fn in enumerate(stages):
            b = step - s
            if 0 <= b < n_blocks:
                held[s][b] = fn(b) if s == 0 else fn(b, held[s - 1].pop(b))


def _attn_fwd(qr, qp, kr, vh, kc, vc, btt):
    def body(qr_ref, qp_ref, kr_ref, v_ref, kc_ref, vc_ref, bt_ref, o_ref):
        kcv, vcv = kc_ref[...], vc_ref[...]

        def scores(b):
            qs, ks, t = _block_geom(b)
            return (_tt(kr_ref[ks:ks + KB, :], qr_ref[qs:qs + QB, :]) + bt_ref[t], _tt(kcv, qp_ref[qs:qs + QB, :]))

        def probs(b, sc):
            p_lat, p_ctx = _softmax_t(*sc)
            return p_lat.astype(bf16), p_ctx.astype(bf16)

        def values(b, p):
            qs, ks, _ = _block_geom(b)
            o_ref[qs:qs + QB, :] = _tn(p[0], v_ref[ks:ks + KB, :]) + _tn(p[1], vcv)

        _staged(NQB, (scores, probs, values))

    sq = pl.BlockSpec((None, S, DH), lambda h: (h, 0, 0))
    sc = pl.BlockSpec((None, L, DH), lambda h: (h, 0, 0))
    return _hbm_call(
        body, name="attn_fwd", out_shape=SDS((H, S, DH), f32), grid=(H,),
        in_specs=[sq, sq, sq, sq, sc, sc, pl.BlockSpec((None, NT, KB, QB), lambda h: (h, 0, 0, 0))],
        out_specs=sq, compiler_params=_cp(48),
    )(qr, qp, kr, vh, kc, vc, btt)


def _shift_rows(v, down):
    n = v.shape[0]
    row = lax.broadcasted_iota(i32, v.shape, 0)
    if down:
        return jnp.where(row == 0, 0.0, pltpu.roll(v, 1, 0))
    return jnp.where(row == n - 1, 0.0, pltpu.roll(v, n - 1, 0))


def _conv_specs():
    col = lambda off: pl.BlockSpec((S, 128), lambda i, off=off: (0, off + i))
    return [col(16), col(20), col(24), col(28), pl.BlockSpec((3, 128), lambda i: (0, i)),
            pl.BlockSpec((1, 128), lambda i: (0, i))]


def _conv_fwd(p, conv_w, conv_b, after=None):
    def body(u_ref, bg_ref, cg_ref, zc_ref, w_ref, b_ref, after_ref, o_ref):
        cu = cg_ref[...] * u_ref[...]
        cv = b_ref[...] + _shift_rows(cu, True) * w_ref[0:1, :]
        cv = cv + cu * w_ref[1:2, :]
        cv = cv + _shift_rows(cu, False) * w_ref[2:3, :]
        o_ref[...] = ((bg_ref[...] * cv) * _silu(zc_ref[...])).astype(bf16)

    return _hbm_call(
        body, name="conv_fwd", out_shape=SDS((S, DC), bf16), grid=(DC // 128,),
        in_specs=_conv_specs() + [ANY_SPEC], out_specs=pl.BlockSpec((S, 128), lambda i: (0, i)),
        compiler_params=_cp(40),
    )(p, p, p, p, conv_w, conv_b, conv_b if after is None else after)


DP_Q, DP_K, DP_V, DP_ZA, DP_U, DP_BG, DP_CG, DP_ZC = range(8)


def _out_proj_loss(o, p, conv_g, w_out, xx, tgt, mrow, b_ada):
    tm = 256

    def body(o_ref, za_ref, c_ref, w_ref, x_ref, t_ref, m_ref, b_ref,
             dy_ref, dconv_ref, dp_ref, do_ref, gwo_ref, dgate_ref, loss_ref):
        k = pl.program_id(0)

        @pl.when(k == 0)
        def _():
            gwo_ref[...] = jnp.zeros_like(gwo_ref)
            dgate_ref[...] = jnp.zeros_like(dgate_ref)
            loss_ref[0, 0] = 0.0

        gate = m_ref[:, 2 * D:3 * D] + b_ref[:, 2 * D:3 * D]
        za = za_ref[...]
        sz = _silu(za)
        om = _merge_heads(o_ref)
        av, cv = (om * sz).astype(bf16), c_ref[...]
        mo = jnp.dot(av, w_ref[0:DA, :], preferred_element_type=f32)
        mo = mo + jnp.dot(cv, w_ref[DA:DA + DC, :], preferred_element_type=f32)
        y = x_ref[...] + gate * mo
        diff = y - t_ref[...]
        loss_ref[0, 0] += jnp.sum(diff * diff)
        dy = diff * (1.0 / D)
        dy_ref[...] = dy
        dgate_ref[...] += jnp.sum(dy * mo, axis=0, keepdims=True)
        dmo = (dy * gate).astype(bf16)
        dmix = _tt(dmo, w_ref[...])
        dattn = dmix[:, 0:DA]
        dconv_ref[...] = dmix[:, DA:DA + DC]
        a = dattn * sz
        for hh in range(H):
            do_ref[hh] = a[:, hh * DH:(hh + 1) * DH].astype(bf16)
        dp_ref[...] = ((dattn * _dsilu(za)) * om).astype(bf16)
        gwo_ref[0:DA, :] += _tn(av, dmo)
        gwo_ref[DA:DA + DC, :] += _tn(cv, dmo)

    row = lambda i: (i, 0)
    fixed = lambda i: (0, 0)
    hspec = pl.BlockSpec((H, tm, DH), lambda i: (0, i, 0))
    return _hbm_call(
        body, name="out_proj_loss",
        out_shape=(SDS((S, D), f32), SDS((S, DC), f32), SDS((8, S, DA), bf16), SDS((H, S, DH), bf16),
                   SDS((D, D), f32), SDS((1, D), f32), SDS((1, 1), f32)),
        grid=(S // tm,),
        in_specs=[hspec, pl.BlockSpec((tm, DA), lambda i: (i, 3)), pl.BlockSpec((tm, DC), row),
                  pl.BlockSpec((D, D), fixed), pl.BlockSpec((tm, D), row), pl.BlockSpec((tm, D), row),
                  pl.BlockSpec((1, 3 * D), fixed), pl.BlockSpec((1, 3 * D), fixed)],
        out_specs=(pl.BlockSpec((tm, D), row), pl.BlockSpec((tm, DC), row),
                   pl.BlockSpec((None, tm, DA), lambda i: (DP_ZA, i, 0)), hspec, pl.BlockSpec((D, D), fixed),
                   pl.BlockSpec((1, D), fixed), SMEM_SPEC),
        compiler_params=_cp(56, dimension_semantics=("arbitrary",)),
    )(o, p, conv_g, w_out, xx, tgt, mrow, b_ada)


def _conv_bwd(dconv, p, conv_w, conv_b, dp8, after=None):
    def body(d_ref, u_ref, bg_ref, cg_ref, zc_ref, w_ref, b_ref, dp_in_ref, after_ref, dp_ref, gw_ref, gb_ref):
        du_ref, dbg_ref, dcg_ref, dzc_ref = dp_ref.at[0], dp_ref.at[1], dp_ref.at[2], dp_ref.at[3]
        dconv = d_ref[...]
        u, bg, cg, zc = u_ref[...], bg_ref[...], cg_ref[...], zc_ref[...]
        w0, w1, w2 = w_ref[0:1, :], w_ref[1:2, :], w_ref[2:3, :]
        cu = cg * u
        cu_m, cu_p = _shift_rows(cu, True), _shift_rows(cu, False)
        cv = b_ref[...] + cu_m * w0
        cv = cv + cu * w1
        cv = cv + cu_p * w2
        sz = _silu(zc)
        dbg_ref[...] = ((dconv * sz) * cv).astype(bf16)
        dzc_ref[...] = ((dconv * (bg * cv)) * _dsilu(zc)).astype(bf16)
        dcv = (dconv * sz) * bg
        gb_ref[...] = jnp.sum(dcv, axis=0, keepdims=True)
        gw_ref[0:1, :] = jnp.sum(dcv * cu_m, axis=0, keepdims=True)
        gw_ref[1:2, :] = jnp.sum(dcv * cu, axis=0, keepdims=True)
        gw_ref[2:3, :] = jnp.sum(dcv * cu_p, axis=0, keepdims=True)
        gw_ref[3:8, :] = jnp.zeros((5, 128), f32)
        dcu = _shift_rows(dcv, False) * w0 + dcv * w1 + _shift_rows(dcv, True) * w2
        dcg_ref[...] = (dcu * u).astype(bf16)
        du_ref[...] = (dcu * cg).astype(bf16)

    return _hbm_call(
        body, name="conv_bwd", out_shape=(SDS((8, S, DC), bf16), SDS((8, DC), f32), SDS((1, DC), f32)),
        grid=(DC // 128,),
        in_specs=[pl.BlockSpec((S, 128), lambda i: (0, i))] + _conv_specs() + [ANY_SPEC, ANY_SPEC],
        out_specs=(pl.BlockSpec((4, S, 128), lambda i: (DP_U // 4, 0, i)), pl.BlockSpec((8, 128), lambda i: (0, i)),
                   pl.BlockSpec((1, 128), lambda i: (0, i))),
        input_output_aliases={7: 0}, compiler_params=_cp(48),
    )(dconv, p, p, p, p, conv_w, conv_b, dp8, conv_b if after is None else after)


def _attn_bwd(qr, qp, kr, vh, kc, vc, btt, do, after=None):
    def body(qr_ref, qp_ref, kr_ref, v_ref, kc_ref, vc_ref, bt_ref, do_ref, after_ref,
             dqr_ref, dqp_ref, dkr_ref, dv_ref, dkc_ref, dvc_ref, dbt_ref):
        kcv, vcv = kc_ref[...], vc_ref[...]
        dkr_ref[...] = jnp.zeros_like(dkr_ref)
        dv_ref[...] = jnp.zeros_like(dv_ref)
        dbt_ref[...] = jnp.zeros_like(dbt_ref)
        ctx_acc = {}

        def products(b):
            qs, ks, t = _block_geom(b)
            dob = do_ref[qs:qs + QB, :]
            s_lat = _tt(kr_ref[ks:ks + KB, :], qr_ref[qs:qs + QB, :]) + bt_ref[t]
            s_ctx = _tt(kcv, qp_ref[qs:qs + QB, :])
            return s_lat, s_ctx, _tt(v_ref[ks:ks + KB, :], dob), _tt(vcv, dob)

        def score_grads(b, x):
            s_lat, s_ctx, dp_lat, dp_ctx = x
            p_lat, p_ctx = _softmax_t(s_lat, s_ctx)
            delta = jnp.sum(p_lat * dp_lat, axis=0, keepdims=True) + jnp.sum(p_ctx * dp_ctx, axis=0, keepdims=True)
            ds_lat = p_lat * (dp_lat - delta)
            ds_ctx = p_ctx * (dp_ctx - delta)
            return ds_lat, ds_lat.astype(bf16), ds_ctx.astype(bf16), p_lat.astype(bf16), p_ctx.astype(bf16)

        def operand_grads(b, y):
            qs, ks, t = _block_geom(b)
            ds_lat, dsb_lat, dsb_ctx, pb_lat, pb_ctx = y
            qrb, qpb, dob = qr_ref[qs:qs + QB, :], qp_ref[qs:qs + QB, :], do_ref[qs:qs + QB, :]
            dbt_ref[t] += ds_lat
            dqr_ref[qs:qs + QB, :] = _tn(dsb_lat, kr_ref[ks:ks + KB, :])
            dqp_ref[qs:qs + QB, :] = _tn(dsb_ctx, kcv)
            dkr_ref[ks:ks + KB, :] += jnp.dot(dsb_lat, qrb, preferred_element_type=f32)
            dv_ref[ks:ks + KB, :] += jnp.dot(pb_lat, dob, preferred_element_type=f32)
            dkc = jnp.dot(dsb_ctx, qpb, preferred_element_type=f32)
            dvc = jnp.dot(pb_ctx, dob, preferred_element_type=f32)
            ctx_acc["k"] = dkc if b == 0 else ctx_acc["k"] + dkc
            ctx_acc["v"] = dvc if b == 0 else ctx_acc["v"] + dvc

        _staged(NQB, (products, score_grads, operand_grads))
        dkc_ref[...] = ctx_acc["k"]
        dvc_ref[...] = ctx_acc["v"]

    sq = pl.BlockSpec((None, S, DH), lambda h: (h, 0, 0))
    sc = pl.BlockSpec((None, L, DH), lambda h: (h, 0, 0))
    sb = pl.BlockSpec((None, NT, KB, QB), lambda h: (h, 0, 0, 0))
    big, ctxs = SDS((H, S, DH), f32), SDS((H, L, DH), f32)
    return _hbm_call(
        body, name="attn_bwd", out_shape=(big, big, big, big, ctxs, ctxs, SDS((H, NT, KB, QB), f32)), grid=(H,),
        in_specs=[sq, sq, sq, sq, sc, sc, sb, sq, ANY_SPEC], out_specs=(sq, sq, sq, sq, sc, sc, sb),
        compiler_params=_cp(56),
    )(qr, qp, kr, vh, kc, vc, btt, do, do if after is None else after)


def _bias_bwd(dbtt, after=None):
    pieces = _tile_pieces()

    def body(d_ref, after_ref, o_ref, scr):
        scr[...] = jnp.zeros_like(scr)
        acc = [None] * N_DR
        for t in range(NT):
            for j in range(2):
                for u in range(KR):
                    dr = pieces[t][j][u]
                    if dr is None:
                        continue
                    piece = d_ref[t, u * GW:(u + 1) * GW, j * GW:(j + 1) * GW]
                    acc[dr] = piece if acc[dr] is None else acc[dr] + piece
        for dr in range(N_DR):
            scr[dr * GW:(dr + 1) * GW, 0:GW] = acc[dr]
        xs = pltpu.roll(scr[...], WIN_W - 1, 1)
        row = lax.broadcasted_iota(i32, xs.shape, 0)
        for b in range(6):
            xs = jnp.where(((row >> b) & 1) == 1, pltpu.roll(xs, 128 - (1 << b), 1), xs)
        rev = jnp.sum(xs.reshape(N_DR, GW, 128), axis=1)
        a = lax.broadcasted_iota(i32, (128, 128), 0)
        b = lax.broadcasted_iota(i32, (128, 128), 1)
        flip = ((a + b == N_DC - 1) & (a < N_DC)).astype(f32)
        o_ref[...] = jnp.dot(rev, flip, precision=HIGHEST, preferred_element_type=f32)

    return _hbm_call(
        body, name="bias_bwd", out_shape=SDS((H, N_DR, 128), f32), grid=(H,),
        in_specs=[pl.BlockSpec((None, NT, KB, QB), lambda h: (h, 0, 0, 0)), ANY_SPEC],
        out_specs=pl.BlockSpec((None, N_DR, 128), lambda h: (h, 0, 0)),
        scratch_shapes=[pltpu.VMEM((N_DR * GW, 128), f32)],
    )(dbtt, dbtt if after is None else after)


def _merge_heads(ref):
    return jnp.concatenate([ref[hh] for hh in range(H)], axis=1)


def _head_norm_bwd(xraw, gain, dy, ones_bd):
    r = lax.rsqrt(_head_sum(xraw * xraw, ones_bd) * (1.0 / DH) + RMS_EPS)
    xh = xraw * r
    gdy = dy * gain
    dx = r * (gdy - xh * (_head_sum(xh * gdy, ones_bd) * (1.0 / DH)))
    return dx, jnp.sum(dy * xh, axis=0, keepdims=True)


def _qk_bwd(dqr, dqp, dkr, dvh, p, gq, gk, rope, dp8):
    tm = 256

    def body(dqr_ref, dqp_ref, dkr_ref, dv_ref, qk_ref, gq_ref, gk_ref, cc_ref, cr_ref, sc_ref, sr_ref, dp_in_ref,
             dp_ref, ggq_ref, ggk_ref):
        dq_ref, dk_ref, dvo_ref = dp_ref.at[DP_Q], dp_ref.at[DP_K], dp_ref.at[DP_V]

        @pl.when(pl.program_id(0) == 0)
        def _():
            ggq_ref[...] = jnp.zeros_like(ggq_ref)
            ggk_ref[...] = jnp.zeros_like(ggk_ref)

        ones_bd = _head_ones()
        cs, sn = _rope_block(cc_ref, cr_ref, tm), _rope_block(sc_ref, sr_ref, tm)
        a = _merge_heads(dqr_ref)
        dyq = ((a * cs - _swap16(a) * sn) + _merge_heads(dqp_ref)) * QK_SCALE
        bk = _merge_heads(dkr_ref)
        dyk = bk * cs - _swap16(bk) * sn
        dq, gq_part = _head_norm_bwd(qk_ref[:, 0:DA], gq_ref[...], dyq, ones_bd)
        dk, gk_part = _head_norm_bwd(qk_ref[:, DA:2 * DA], gk_ref[...], dyk, ones_bd)
        dq_ref[...] = dq.astype(bf16)
        dk_ref[...] = dk.astype(bf16)
        dvo_ref[...] = _merge_heads(dv_ref).astype(bf16)
        ggq_ref[...] += gq_part
        ggk_ref[...] += gk_part

    hspec = pl.BlockSpec((H, tm, DH), lambda i: (0, i, 0))
    fixed = pl.BlockSpec((1, DA), lambda i: (0, 0))
    return _hbm_call(
        body, name="qk_bwd", out_shape=(SDS((8, S, DA), bf16), SDS((1, DA), f32), SDS((1, DA), f32)), grid=(S // tm,),
        in_specs=[hspec, hspec, hspec, hspec, pl.BlockSpec((tm, 2 * DA), lambda i: (i, 0)), fixed, fixed]
        + _rope_specs(tm) + [ANY_SPEC],
        out_specs=(pl.BlockSpec((3, tm, DA), lambda i: (0, i, 0)), fixed, fixed), input_output_aliases={11: 0},
        compiler_params=_cp(40, dimension_semantics=("arbitrary",)),
    )(dqr, dqp, dkr, dvh, p, gq, gk, *rope, dp8)


def _ctx_bwd(dkc, dvc, pc, gk):
    def body(dkc_ref, dvc_ref, p_ref, gk_ref, dk_ref, dv_ref, ggk_ref):
        ones_bd = _head_ones()
        dk, gk_part = _head_norm_bwd(p_ref[:, 0:DA], gk_ref[...], _merge_heads(dkc_ref), ones_bd)
        dk_ref[...] = dk.astype(bf16)
        dv_ref[...] = _merge_heads(dvc_ref).astype(bf16)
        ggk_ref[...] = gk_part

    piece = SDS((L, DA), bf16)
    return _hbm_call(
        body, name="ctx_bwd", out_shape=(piece, piece, SDS((1, DA), f32)), in_specs=[VMEM_SPEC] * 4,
        out_specs=(VMEM_SPEC,) * 3,
    )(dkc, dvc, pc, gk)


def _grad_w_in(h, dp8, hc, dkc_raw, dvc_m, half, name, after=None):
    def body(half_ref, h_ref, p_ref, hc_ref, dk_ref, dv_ref, after_ref, g_ref):
        j = pl.program_id(0)
        hv = h_ref[...]
        g_ref[:, 0:DA] = _tn(hv, p_ref[0])
        g_ref[:, DA:2 * DA] = _tn(hv, p_ref[1])

        @pl.when(j == 0)
        def _():
            g_ref[:, DA:2 * DA] += _tn(hc_ref[...], dk_ref[...])

        @pl.when(j == 1)
        def _():
            g_ref[:, 0:DA] += _tn(hc_ref[...], dv_ref[...])

    fixed = lambda j, s: (0, 0)
    hd = D // 2
    grid_spec = pltpu.PrefetchScalarGridSpec(
        num_scalar_prefetch=1, grid=(4,),
        in_specs=[pl.BlockSpec((S, hd), lambda j, s: (0, s[0])), pl.BlockSpec((2, S, DA), lambda j, s: (j, 0, 0)),
                  pl.BlockSpec((L, hd), lambda j, s: (0, s[0])), pl.BlockSpec((L, DA), fixed),
                  pl.BlockSpec((L, DA), fixed), ANY_SPEC],
        out_specs=pl.BlockSpec((None, hd, D), lambda j, s: (j, 0, 0)))
    return _hbm_call(
        body, name=name, out_shape=SDS((4, hd, D), f32), grid_spec=grid_spec, compiler_params=_cp(40),
    )(half, h, dp8, hc, dkc_raw, dvc_m, half if after is None else after)


def _norm_mod_bwd(x, dh, g, scale):
    r = lax.rsqrt(jnp.mean(x * x, axis=-1, keepdims=True) + RMS_EPS)
    xh = x * r
    y = xh * g
    dshift = jnp.sum(dh, axis=0, keepdims=True)
    dscale = jnp.sum(dh * y, axis=0, keepdims=True)
    dyn = dh * (1.0 + scale)
    dg = jnp.sum(dyn * xh, axis=0, keepdims=True)
    gdy = dyn * g
    dx = r * (gdy - xh * jnp.mean(xh * gdy, axis=-1, keepdims=True))
    return dx, dshift, dscale, dg


def _dh_grad_x(dp8, w4, xx, dy, norm_g, mrow, b_ada, after=None):
    tm = 256

    def body(p_ref, w_ref, x_ref, dy_ref, g_ref, m_ref, b_ref, after_ref, gx_ref, dsh_ref, dsc_ref, dg_ref):
        @pl.when(pl.program_id(0) == 0)
        def _():
            dsh_ref[...] = jnp.zeros_like(dsh_ref)
            dsc_ref[...] = jnp.zeros_like(dsc_ref)
            dg_ref[...] = jnp.zeros_like(dg_ref)

        dh = None
        for j in range(4):
            for half in range(2):
                term = _tt(p_ref[2 * j + half], w_ref[j, :, half * DA:(half + 1) * DA])
                dh = term if dh is None else dh + term
        scale = m_ref[:, D:2 * D] + b_ref[:, D:2 * D]
        dx, dshift, dscale, dg = _norm_mod_bwd(x_ref[...], dh, g_ref[...], scale)
        gx_ref[...] = dy_ref[...] + dx
        dsh_ref[...] += dshift
        dsc_ref[...] += dscale
        dg_ref[...] += dg

    row = lambda i: (i, 0)
    fixed = lambda i: (0, 0)
    vec = SDS((1, D), f32)
    return _hbm_call(
        body, name="dh_grad_x", out_shape=(SDS((S, D), f32), vec, vec, vec), grid=(S // tm,),
        in_specs=[pl.BlockSpec((8, tm, DA), lambda i: (0, i, 0)), pl.BlockSpec((4, D, D), lambda i: (0, 0, 0)),
                  pl.BlockSpec((tm, D), row), pl.BlockSpec((tm, D), row), pl.BlockSpec((1, D), fixed),
                  pl.BlockSpec((1, 3 * D), fixed), pl.BlockSpec((1, 3 * D), fixed), ANY_SPEC],
        out_specs=(pl.BlockSpec((tm, D), row), pl.BlockSpec((1, D), fixed), pl.BlockSpec((1, D), fixed),
                   pl.BlockSpec((1, D), fixed)),
        compiler_params=_cp(56, dimension_semantics=("arbitrary",)),
    )(dp8, w4, xx, dy, norm_g, mrow, b_ada, b_ada if after is None else after)


def _dhc_sums(dkc_raw, dvc_m, w4, ctx2, norm_g, mrow_c, b_ada, after=None):
    def body(dk_ref, dv_ref, w0_ref, w1_ref, x_ref, g_ref, m_ref, b_ref, after_ref, dsh_ref, dsc_ref, dg_ref):
        dh = _tt(dk_ref[...], w0_ref[:, DA:2 * DA]) + _tt(dv_ref[...], w1_ref[:, 0:DA])
        scale = m_ref[:, D:2 * D] + b_ref[:, D:2 * D]
        _, dshift, dscale, dg = _norm_mod_bwd(x_ref[...], dh, g_ref[...], scale)
        dsh_ref[...] = dshift
        dsc_ref[...] = dscale
        dg_ref[...] = dg

    fixed = lambda i: (0, 0)
    vec = SDS((1, D), f32)
    vspec = pl.BlockSpec((1, D), fixed)
    return _hbm_call(
        body, name="dhc_sums", out_shape=(vec, vec, vec), grid=(1,),
        in_specs=[pl.BlockSpec((L, DA), fixed), pl.BlockSpec((L, DA), fixed),
                  pl.BlockSpec((None, D, D), lambda i: (0, 0, 0)), pl.BlockSpec((None, D, D), lambda i: (1, 0, 0)),
                  pl.BlockSpec((L, D), fixed), vspec, pl.BlockSpec((1, 3 * D), fixed), pl.BlockSpec((1, 3 * D), fixed),
                  ANY_SPEC],
        out_specs=(vspec, vspec, vspec), compiler_params=_cp(32),
    )(dkc_raw, dvc_m, w4, w4, ctx2, norm_g, mrow_c, b_ada, b_ada if after is None else after)


def _rope_tables():
    nf = DH // 4
    inv = np.float32(ROPE_THETA) ** (-np.arange(nf, dtype=np.float32) / np.float32(nf))
    ang_c = np.arange(GW, dtype=np.float32)[:, None] * inv
    ang_r = np.arange(ROWS, dtype=np.float32)[:, None] * inv
    zc, zr = np.zeros((GW, 2 * nf), np.float32), np.zeros((ROWS, 2 * nf), np.float32)
    ct_cos = np.tile(np.concatenate([zc, np.cos(ang_c), np.cos(ang_c)], axis=1), (1, H))
    ct_sin = np.tile(np.concatenate([zc, -np.sin(ang_c), np.sin(ang_c)], axis=1), (1, H))
    rt_cos = np.tile(np.concatenate([np.cos(ang_r), np.cos(ang_r), zr], axis=1), (1, H))
    rt_sin = np.tile(np.concatenate([-np.sin(ang_r), np.sin(ang_r), zr], axis=1), (1, H))
    rep8 = lambda t: np.ascontiguousarray(np.broadcast_to(t[:, None, :], (ROWS, 8, DA))).reshape(ROWS * 8, DA)
    return tuple(jnp.asarray(t, f32) for t in (ct_cos, rep8(rt_cos), ct_sin, rep8(rt_sin)))


def _local_step(xx, ctx2, tgt, mrow, mrow_c, b_ada, norm_g, weights, q_norm_g, k_norm_g, rpb2, conv_w_full, conv_b,
                hooks=None):
    hooks = hooks or {}
    gq = jnp.tile(q_norm_g, (1, H))
    gk = jnp.tile(k_norm_g, (1, H))
    rope = _rope_tables()

    h = _prenorm(xx, norm_g, mrow, b_ada, 256, "prenorm_x", after=weights.get("started"))
    hc = _prenorm(ctx2, norm_g, mrow_c, b_ada, L, "prenorm_ctx")
    jv = weights["jvec"]
    p = _in_proj_own(h, weights["own"], jv)
    btb = _bias_tiles(rpb2, after=p)
    w4, started = weights["near"](btb)
    p = _in_proj_block(h, w4, p, weights["first"], "in_proj_near", after=started)
    w4 = weights["near2"](w4, p)
    p = _in_proj_block(h, w4, p, weights["second"], "in_proj_near2")
    w4, started = weights["far"](w4, p)
    p = _in_proj_block(h, w4, p, jv ^ 3, "in_proj_far", after=started)
    pc = _ctx_proj(hc, w4)
    qr, qp, kr, vh = _qk_prep(p, gq, gk, rope)
    kc, vc = _ctx_prep(pc, gk)
    o = _attn_fwd(qr, qp, kr, vh, kc, vc, btb)
    started = weights["out_arrived"](o) if "out_arrived" in weights else None
    conv_g = _conv_fwd(p, conv_w_full, conv_b, after=started)
    w_out_full = weights["out"](conv_g)
    dy, dconv, dp8, do, g_w_out, dgate, loss_sum = _out_proj_loss(o, p, conv_g, w_out_full, xx, tgt, mrow, b_ada)
    started = hooks["g_w_out"](g_w_out) if "g_w_out" in hooks else None
    dp8, g_conv_w, g_conv_b = _conv_bwd(dconv, p, conv_w_full, conv_b, dp8, after=started)
    started = hooks["after_conv"](dp8) if "after_conv" in hooks else None
    dqr, dqp, dkr, dvh, dkc, dvc, dbtb = _attn_bwd(qr, qp, kr, vh, kc, vc, btb, do, after=started)
    dp8, g_gq, g_gk = _qk_bwd(dqr, dqp, dkr, dvh, p, gq, gk, rope, dp8)
    dkc_raw, dvc_m, g_gk_c = _ctx_bwd(dkc, dvc, pc, gk)
    first = hooks.get("first_half", jnp.zeros((1,), i32))
    g_first = _grad_w_in(h, dp8, hc, dkc_raw, dvc_m, first, "grad_w_in_first")
    started = hooks["g_w_in_first"](g_first) if "g_w_in_first" in hooks else None
    g_second = _grad_w_in(h, dp8, hc, dkc_raw, dvc_m, 1 - first, "grad_w_in_second", after=started)
    started = hooks["g_w_in_second"](g_second) if "g_w_in_second" in hooks else None
    dshift_c, dscale_c, dng_c = _dhc_sums(dkc_raw, dvc_m, w4, ctx2, norm_g, mrow_c, b_ada, after=started)
    g_rpb = _bias_bwd(dbtb, after=dshift_c)
    grad_x, dshift, dscale, dng = _dh_grad_x(dp8, w4, xx, dy, norm_g, mrow, b_ada, after=g_rpb)
    return dict(loss_sum=loss_sum, grad_x=grad_x, g_w_in=(g_first, g_second), g_w_out=g_w_out, g_conv_w=g_conv_w,
                g_conv_b=g_conv_b, g_rpb=g_rpb, g_gq=g_gq, g_gk=g_gk, g_gk_c=g_gk_c, dshift=dshift, dscale=dscale,
                dgate=dgate, dng=dng, dshift_c=dshift_c, dscale_c=dscale_c, dng_c=dng_c)


def _pair_sum_w_in(g, r):
    tr = 128

    def body(g_ref, r_ref, t32_ref, tb_ref):
        t = g_ref[...] + r_ref[...]
        t32_ref[...] = t
        tb_ref[...] = t.astype(bf16)

    half = D // 2
    spec = pl.BlockSpec((4, tr, D), lambda i: (0, i, 0))
    return _hbm_call(body, name="pair_sum_w_in", out_shape=(SDS((4, half, D), f32), SDS((4, half, D), bf16)),
                     grid=(half // tr,), in_specs=[spec, spec], out_specs=(spec, spec), compiler_params=_cp(40))(g, r)


def _pair_sum_w_out(g, r, cvec):
    hr = D // 8

    def body(c_ref, g0, g1, g2, g3, r_ref, t32_ref, tb_ref):
        for q, g_ref in enumerate((g0, g1, g2, g3)):
            t = g_ref[...] + r_ref[q]
            t32_ref[q] = t
            tb_ref[q] = t.astype(bf16)

    gspecs = [pl.BlockSpec((hr, D), lambda i, c, q=q: (2 * q + c[0], 0)) for q in range(4)]
    full = pl.BlockSpec((4, hr, D), lambda i, c: (0, 0, 0))
    grid_spec = pltpu.PrefetchScalarGridSpec(num_scalar_prefetch=1, grid=(1,), in_specs=gspecs + [full],
                                             out_specs=(full, full))
    return _hbm_call(body, name="pair_sum_w_out", out_shape=(SDS((4, hr, D), f32), SDS((4, hr, D), bf16)),
                          grid_spec=grid_spec)(cvec, g, g, g, g, r)


def _chip_sum(t32, r2, jvec, name):
    rows = t32.shape[1]
    tr = min(rows, 128)

    def body(j_ref, t_ref, r_ref, u_ref):
        u_ref[...] = ((t_ref[...] + r_ref[0].astype(f32)) + r_ref[1].astype(f32)) + r_ref[2].astype(f32)

    grid_spec = pltpu.PrefetchScalarGridSpec(
        num_scalar_prefetch=1, grid=(rows // tr,),
        in_specs=[pl.BlockSpec((None, tr, D), lambda i, j: (j[0], i, 0)), pl.BlockSpec((3, tr, D), lambda i, j: (0, i, 0))],
        out_specs=pl.BlockSpec((tr, D), lambda i, j: (i, 0)))
    return _hbm_call(body, name=name, out_shape=SDS((rows, D), f32), grid_spec=grid_spec)(jvec, t32, r2)


_PK = {}
_off = 0
for _name, _rows in (("dm", 24), ("dmc", 24), ("dng", 8), ("dng_c", 8), ("gq", 8), ("gk", 8), ("gk_c", 8),
                     ("rpb", H * N_DR), ("conv_b", 8), ("conv_w", 16), ("loss", 8)):
    _PK[_name] = (_off, _off + _rows)
    _off += _rows
PK_ROWS = _off
RS_B_ADA, RS_NORM_G, RS_GQ, RS_GK, RS_RPB, RS_CONV_B, RS_CONV_W, RS_DMC, RS_LOSS, RS_ROWS = (
    0, 24, 32, 40, 48, 168, 176, 192, 216, 224)


def _small_reduce(gathered):
    def body(g_ref, o_ref, dm_ref):
        a0 = _PK["dm"][0]
        dm_ref[...] = jnp.zeros_like(dm_ref)
        for b in range(8):
            for i in range(24):
                dm_ref[b:b + 1, 128 * i:128 * (i + 1)] = g_ref[b, a0 + i:a0 + i + 1, :]
        tot = g_ref[0]
        for b in range(1, 8):
            tot = tot + g_ref[b]

        def rows(name):
            a, z = _PK[name]
            return tot[a:z]

        o_ref[RS_B_ADA:RS_B_ADA + 24] = rows("dm") + rows("dmc")
        o_ref[RS_NORM_G:RS_NORM_G + 8] = rows("dng") + rows("dng_c")
        gq = jnp.broadcast_to(jnp.sum(rows("gq"), axis=0, keepdims=True), (8, 128))
        gk = jnp.broadcast_to(jnp.sum(rows("gk") + rows("gk_c"), axis=0, keepdims=True), (8, 128))
        o_ref[RS_GQ:RS_GQ + 8] = gq + pltpu.roll(gq, DH, 1)
        o_ref[RS_GK:RS_GK + 8] = gk + pltpu.roll(gk, DH, 1)
        o_ref[RS_RPB:RS_RPB + H * N_DR] = rows("rpb")
        o_ref[RS_CONV_B:RS_CONV_B + 8] = rows("conv_b")
        o_ref[RS_CONV_W:RS_CONV_W + 16] = rows("conv_w")
        dmc = rows("dmc")
        o_ref[RS_DMC:RS_DMC + 24] = dmc
        o_ref[RS_LOSS:RS_LOSS + 8] = rows("loss")
        for i in range(24):
            dm_ref[8:9, 128 * i:128 * (i + 1)] = dmc[i:i + 1]

    return _hbm_call(body, name="small_reduce", out_shape=(SDS((RS_ROWS, 128), f32), SDS((16, 3 * D), f32)),
                     in_specs=[VMEM_SPEC], out_specs=(VMEM_SPEC, VMEM_SPEC))(gathered)


def _w_ada_grad(sc16, dm16, w_ada_shard, jvec):
    ncol = w_ada_shard.shape[1]

    def body(j_ref, sc_ref, dm_ref, w_ref, g_ref, part_ref):
        dm = dm_ref[...]
        g_ref[...] = lax.dot_general(sc_ref[...], dm, (((0,), (0,)), ((), ())), precision=HIGHEST,
                                     preferred_element_type=f32)
        part_ref[...] = lax.dot_general(dm[8:16], w_ref[...], (((1,), (1,)), ((), ())), precision=HIGHEST,
                                        preferred_element_type=f32)

    fixed = lambda i, j: (0, 0)
    grid_spec = pltpu.PrefetchScalarGridSpec(
        num_scalar_prefetch=1, grid=(1,),
        in_specs=[pl.BlockSpec((16, D), fixed), pl.BlockSpec((16, ncol), lambda i, j: (0, j[0])),
                  pl.BlockSpec((D, ncol), fixed)],
        out_specs=(pl.BlockSpec((D, ncol), fixed), pl.BlockSpec((8, D), fixed)))
    return _pallas_call(body, name="w_ada_grad", out_shape=(SDS((D, ncol), f32), SDS((8, D), f32)),
                        grid_spec=grid_spec, compiler_params=_cp(40))(jvec, sc16, dm16, w_ada_shard)


def _c_ctx_grad(parts4, c_ctx):
    def body(p_ref, c_ref, o_ref):
        tot = ((p_ref[0] + p_ref[1]) + p_ref[2]) + p_ref[3]
        o_ref[...] = tot[0:1] * _dsilu(c_ref[...].reshape(1, D))

    return _pallas_call(body, name="c_ctx_grad", out_shape=SDS((1, D), f32), in_specs=[VMEM_SPEC, VMEM_SPEC],
                        out_specs=VMEM_SPEC)(parts4, c_ctx)


def _adamw(w, g, m, v, name):
    rows, cols = w.shape
    tr = 256 if rows % 256 == 0 else rows

    def body(w_ref, g_ref, m_ref, v_ref, d_ref, m2_ref, v2_ref):
        gv = g_ref[...]
        m2 = ADAM_B1 * m_ref[...] + (1.0 - ADAM_B1) * gv
        v2 = ADAM_B2 * v_ref[...] + (1.0 - ADAM_B2) * jnp.square(gv)
        m_hat = m2 / (1.0 - ADAM_B1 ** ADAM_STEP)
        v_hat = v2 / (1.0 - ADAM_B2 ** ADAM_STEP)
        d_ref[...] = -ADAM_LR * (m_hat / (jnp.sqrt(v_hat) + ADAM_EPS) + ADAM_WD * w_ref[...])
        m2_ref[...] = m2
        v2_ref[...] = v2

    spec = pl.BlockSpec((tr, cols), lambda i: (i, 0))
    shp = SDS((rows, cols), f32)
    return _hbm_call(body, name=name, out_shape=(shp, shp, shp), grid=(rows // tr,), in_specs=[spec] * 4,
                          out_specs=(spec, spec, spec))(w, g, m, v)


def _adamw_halves(w, g_mine, g_other, m, v, cvec, name):
    rows, cols = w.shape
    half = rows // 2
    tr = min(256, half)
    per_half = half // tr

    def body(c_ref, w_ref, ga_ref, gb_ref, m_ref, v_ref, g_ref, d_ref, m2_ref, v2_ref):
        in_my_half = (pl.program_id(0) // per_half) == c_ref[0]
        gv = jnp.where(in_my_half, ga_ref[...], gb_ref[...])
        g_ref[...] = gv
        m2 = ADAM_B1 * m_ref[...] + (1.0 - ADAM_B1) * gv
        v2 = ADAM_B2 * v_ref[...] + (1.0 - ADAM_B2) * jnp.square(gv)
        m_hat = m2 / (1.0 - ADAM_B1 ** ADAM_STEP)
        v_hat = v2 / (1.0 - ADAM_B2 ** ADAM_STEP)
        d_ref[...] = -ADAM_LR * (m_hat / (jnp.sqrt(v_hat) + ADAM_EPS) + ADAM_WD * w_ref[...])
        m2_ref[...] = m2
        v2_ref[...] = v2

    full = pl.BlockSpec((tr, cols), lambda i, c: (i, 0))
    part = pl.BlockSpec((tr, cols), lambda i, c: (i % per_half, 0))
    shp = SDS((rows, cols), f32)
    grid_spec = pltpu.PrefetchScalarGridSpec(num_scalar_prefetch=1, grid=(rows // tr,),
                                             in_specs=[full, part, part, full, full], out_specs=(full,) * 4)
    return _hbm_call(body, name=name, out_shape=(shp,) * 4, grid_spec=grid_spec)(cvec, w, g_mine, g_other, m, v)


def _adam_math(w, g, m, v):
    m2 = ADAM_B1 * m + (1.0 - ADAM_B1) * g
    v2 = ADAM_B2 * v + (1.0 - ADAM_B2) * jnp.square(g)
    m_hat = m2 / (1.0 - ADAM_B1 ** ADAM_STEP)
    v_hat = v2 / (1.0 - ADAM_B2 ** ADAM_STEP)
    return -ADAM_LR * (m_hat / (jnp.sqrt(v_hat) + ADAM_EPS) + ADAM_WD * w), m2, v2


def _adamw_small(red, g_c_ctx, jvec, ws, ms, vs):
    n = len(ws)

    def body(*refs):
        red_ref, gc_ref, j_ref = refs[:3]
        w_refs, m_refs, v_refs = refs[3:3 + n], refs[3 + n:3 + 2 * n], refs[3 + 2 * n:3 + 3 * n]
        outs = refs[3 + 3 * n:]
        g_out, d_out, m_out, v_out = outs[:n], outs[n:2 * n], outs[2 * n:3 * n], outs[3 * n:]
        chip = j_ref[0]
        lanes = lambda i: (slice(None), slice(128 * i, 128 * (i + 1)))
        row = lambda r0, i: (lambda: red_ref[r0 + i:r0 + i + 1, :])
        whole = (slice(None), slice(None))
        chunks = [
            [((slice(None),), lambda: gc_ref[...].reshape(D))],
            [(lanes(i), row(RS_B_ADA, i)) for i in range(3 * D // 128)],
            [(lanes(i), row(RS_NORM_G, i)) for i in range(D // 128)],
            [(whole, lambda: red_ref[RS_GQ:RS_GQ + 1, 0:DH])],
            [(whole, lambda: red_ref[RS_GK:RS_GK + 1, 0:DH])],
            [((dr,), (lambda dr=dr: red_ref[pl.ds(RS_RPB + dr, H, stride=N_DR), 0:N_DC])) for dr in range(N_DR)],
            [((r,), (lambda r=r: red_ref[pl.ds(RS_CONV_W + 4 * r + chip, 1), :])) for r in range(3)],
            [(lanes(i), row(RS_CONV_B, i)) for i in range(DC // 128)],
        ]
        for a in range(n):
            for idx, grad in chunks[a]:
                g = grad()
                d, m2, v2 = _adam_math(w_refs[a][idx], g, m_refs[a][idx], v_refs[a][idx])
                g_out[a][idx] = g
                d_out[a][idx] = d
                m_out[a][idx] = m2
                v_out[a][idx] = v2

    shapes = [SDS(w.shape, f32) for w in ws]
    res = _pallas_call(body, name="adamw_small", out_shape=shapes * 4,
                       in_specs=[VMEM_SPEC, VMEM_SPEC, SMEM_SPEC] + [VMEM_SPEC] * (3 * n),
                       out_specs=[VMEM_SPEC] * (4 * n))(red, g_c_ctx, jvec, *ws, *ms, *vs)
    return [list(res[k * n:(k + 1) * n]) for k in range(4)]


def _rows128(a):
    return a.reshape(-1, 128)


def kernel(x, c, ctx, c_ctx, w_ada, b_ada, norm_g, w_in, q_norm_g, k_norm_g, rpb, conv_w, conv_b, w_out, loss_target, m_c_ctx, m_w_ada, m_b_ada, m_norm_g, m_w_in, m_q_norm_g, m_k_norm_g, m_rpb, m_conv_w, m_conv_b, m_w_out, v_c_ctx, v_w_ada, v_b_ada, v_norm_g, v_w_in, v_q_norm_g, v_k_norm_g, v_rpb, v_conv_w, v_conv_b, v_w_out):
    xi, yi, ci = lax.axis_index("x"), lax.axis_index("y"), lax.axis_index("c")
    dev = 4 * xi + 2 * yi + ci
    chip = 2 * xi + yi
    cvec = jnp.reshape(ci, (1,)).astype(i32)
    jvec = jnp.reshape(chip, (1,)).astype(i32)
    w_ada_s = w_ada[0]
    ncol = w_ada_s.shape[1]

    gc = _split_start([_to_slot(c.reshape(8, 128), 8, dev)], [], 7, _gather8_copies, "gather_c_start")
    wo4c = _cast_to_slot(w_out[0], jvec, "cast_w_out", after=gc[3])
    w4c = _cast_to_slot(w_in[0], jvec, "cast_w_in", after=wo4c)
    (c8,), _ = _split_wait(gc[0], gc[1], [gc[2]], [], w4c, _gather8_copies, "gather_c_wait")
    cc = jnp.concatenate([c8.reshape(8, D), c_ctx.reshape(1, D), jnp.zeros((7, D), f32)], axis=0)
    m_shard, sc16 = _adaln_shard(cc, w_ada_s)

    conv_w_pad = jnp.pad(conv_w[0], ((0, 5), (0, 0)))
    gm = _split_start([_to_slot(m_shard, 4, chip), _to_slot(conv_w_pad, 4, chip)], [], 6, _gather4_copies,
                      "gather_mod_start")

    all_k = [(0, 0, 0), (0, 0, 1), (0, 0, 2)]
    sem_a, rem_a, w4s, token = _split_start([w4c], [], 1, _near_copies, "weights_near_start", after=gm[4])
    (m4, cw4), _ = _split_wait(gm[0], gm[1], [gm[2], gm[3]], [], token, _gather4_copies, "gather_mod_wait")
    m_full = jnp.transpose(m4, (1, 0, 2)).reshape(16, 4 * ncol)
    mrow = lax.dynamic_slice(m_full, (dev, 0), (1, 3 * D))
    mrow_c = m_full[8:9]
    conv_w_full = jnp.transpose(cw4[:, 0:3, :], (1, 0, 2)).reshape(3, DC)
    waves = {}

    def near(after):
        (w4w,), _ = _split_wait(sem_a, rem_a, [w4s], [], after, _near_copies, "weights_near_wait")
        sem_b, rem_b, w4b, started = _split_start([w4w], [], 2, _pass_relay_copies, "weights_pass_start")
        waves["pass"] = (sem_b, rem_b)
        return w4b, started

    def near2(w4, after):
        (w4w,), _ = _split_wait(*waves["pass"], [w4], [], after, _pass_copy, "weights_pass_wait")
        return w4w

    def far(w4, after):
        (w4w,), _ = _split_wait(*waves["pass"], [w4], [], after, _relay_copy, "weights_far_wait")
        w4x, sem_c, rem_c, wo4s, started = _forward_then_start(w4w, wo4c, all_k, "weights_far_forward_out_start")
        (w4f,), _ = _split_wait(sem_c, rem_c, [w4x], [], started, _diag_forward_copy, "weights_far_forward_wait")
        waves["out"] = (sem_c, rem_c, wo4s)
        return w4f, started

    def w_out_arrived(after):
        sem_c, rem_c, wo4s = waves["out"]
        (wow,) = _halves_wait(sem_c, rem_c, [wo4s], after, all_k, "weights_out_wait")
        sem_d, rem_d, wof, started = _split_start([wow], [], 3, _forward_copies, "weights_out_forward_start")
        waves["out_forward"] = (sem_d, rem_d, wof)
        return started

    def w_out_gathered(after):
        sem_d, rem_d, wof = waves["out_forward"]
        (wo,), _ = _split_wait(sem_d, rem_d, [wof], [], after, _forward_copies, "weights_out_forward_wait")
        return wo.reshape(D, D)

    weights = dict(own=w_in[0], jvec=jvec, started=token, first=jvec ^ (1 + cvec), second=jvec ^ (2 - cvec),
                   near=near, near2=near2, far=far, out_arrived=w_out_arrived, out=w_out_gathered)

    exchange = _exchange_copies
    pending = {}

    def on_g_w_out(g_w_out):
        out = _split_start([g_w_out], [SDS((4, D // 8, D), f32)], 4, exchange, "grad_out_pair_start")
        pending["ex_out"] = out
        return out[4]

    def after_conv(dp8):
        ssem_o, rsem_o, g_o, land_o, _ = pending["ex_out"]
        (g_o,), (ex_o,) = _split_wait(ssem_o, rsem_o, [g_o], [land_o], dp8, exchange, "grad_out_pair_wait")
        to32, tob = _pair_sum_w_out(g_o, ex_o, cvec)
        out = _split_start([tob], [SDS((3, D // 8, D), bf16)], 3, _scatter_copies, "grad_out_chip_start")
        pending["sc_out"] = (out, to32)
        return out[4]

    def on_g_w_in_first(g_first):
        out = _split_start([g_first], [SDS((4, D // 2, D), f32)], 4, _block_exchange_copies, "grad_pair_start")
        pending["ex"] = (out[0], out[1], [out[2]], [out[3]])
        return out[4]

    def on_g_w_in_second(g_second):
        ex_ssem, ex_rsem, ex_srcs, ex_lands = pending["ex"]
        _, ex = _split_wait(ex_ssem, ex_rsem, ex_srcs, ex_lands, g_second, _block_exchange_copies, "grad_pair_wait")
        t32, tb = _pair_sum_w_in(g_second, ex[0])
        out = _split_start([tb], [SDS((3, D // 2, D), bf16)], 3, _scatter_copies, "grad_chip_start")
        pending["sc_in"] = (out, t32)
        return out[4]

    r = _local_step(x[0], ctx[0], loss_target[0], mrow, mrow_c, b_ada, norm_g, weights, q_norm_g, k_norm_g,
                    rpb[0], conv_w_full, conv_b,
                    dict(g_w_out=on_g_w_out, after_conv=after_conv, first_half=1 - cvec,
                         g_w_in_first=on_g_w_in_first, g_w_in_second=on_g_w_in_second))
    dm = jnp.concatenate([r["dshift"], r["dscale"], r["dgate"]], axis=1)
    dmc = jnp.concatenate([r["dshift_c"], r["dscale_c"], jnp.zeros((1, D), f32)], axis=1)
    pack_parts = [_rows128(dm), _rows128(dmc), _rows128(r["dng"]), _rows128(r["dng_c"]), _rows128(r["g_gq"]),
                  _rows128(r["g_gk"]), _rows128(r["g_gk_c"]), r["g_rpb"].reshape(H * N_DR, 128),
                  _rows128(r["g_conv_b"]), _rows128(r["g_conv_w"][0:3]), jnp.pad(r["loss_sum"], ((0, 0), (0, 127)))]
    pack = jnp.concatenate([jnp.pad(p, ((0, -p.shape[0] % 8), (0, 0))) for p in pack_parts], axis=0)
    assert pack.shape[0] == PK_ROWS
    gs = _split_start([_to_slot(pack, 8, dev)], [], 7, _gather8_copies, "gather_small_start")
    sc_o, to32 = pending["sc_out"]
    _, (ro2,) = _split_wait(sc_o[0], sc_o[1], [sc_o[2]], [sc_o[3]], gs[3], _scatter_copies, "grad_out_chip_wait")
    u_out = _chip_sum(to32, ro2, jvec, "chip_sum_w_out")
    (o_out,) = _sibling_send([u_out], "grad_out_pair_send")
    g_w_out_s, d_w_out, nm_w_out, nv_w_out = _adamw_halves(w_out[0], u_out, o_out, m_w_out[0], v_w_out[0], cvec,
                                                           "adamw_w_out")
    (gathered,), _ = _split_wait(gs[0], gs[1], [gs[2]], [], nm_w_out, _gather8_copies, "gather_small_wait")
    red, dm16 = _small_reduce(gathered)
    loss = red[RS_LOSS, 0] * (0.5 / D)

    g_w_ada_s, cpart = _w_ada_grad(sc16, dm16, w_ada_s, jvec)
    d_w_ada, nm_w_ada, nv_w_ada = _adamw(w_ada_s, g_w_ada_s, m_w_ada[0], v_w_ada[0], "adamw_w_ada")

    (cparts4,) = _chip_gather([cpart], "gather_c_ctx_parts", after=nm_w_ada)
    g_c_ctx = _c_ctx_grad(cparts4, c_ctx)

    sc_in, t32 = pending["sc_in"]
    _, (r2,) = _split_wait(sc_in[0], sc_in[1], [sc_in[2]], [sc_in[3]], g_c_ctx, _scatter_copies, "grad_chip_wait")
    u_in = _chip_sum(t32, r2, jvec, "chip_sum_w_in")

    t_rpb = lambda a: jnp.transpose(a, (0, 2, 1, 3)).reshape(N_DR, H, N_DC)
    t_cw = lambda a: jnp.transpose(a, (1, 0, 2))
    small = _adamw_small(
        red, g_c_ctx, jvec,
        [c_ctx, b_ada, norm_g, q_norm_g, k_norm_g, t_rpb(rpb), t_cw(conv_w), conv_b],
        [m_c_ctx, m_b_ada, m_norm_g, m_q_norm_g, m_k_norm_g, t_rpb(m_rpb), t_cw(m_conv_w), m_conv_b],
        [v_c_ctx, v_b_ada, v_norm_g, v_q_norm_g, v_k_norm_g, t_rpb(v_rpb), t_cw(v_conv_w), v_conv_b])
    for kind in small:
        kind[5] = jnp.transpose(kind[5].reshape(1, N_DR, H, N_DC), (0, 2, 1, 3))
        kind[6] = jnp.transpose(kind[6], (1, 0, 2))

    (o_in,) = _sibling_send([u_in], "grad_pair_send")
    g_w_in_s, d_w_in, nm_w_in, nv_w_in = _adamw_halves(w_in[0], u_in, o_in, m_w_in[0], v_w_in[0], cvec, "adamw_w_in")

    def ordered(kind, big_w_ada, big_w_in, big_w_out):
        s_c_ctx, s_b_ada, s_norm_g, s_q, s_k, s_rpb, s_conv_w, s_conv_b = small[kind]
        return [s_c_ctx, big_w_ada[None], s_b_ada, s_norm_g, big_w_in[None], s_q, s_k, s_rpb, s_conv_w,
                s_conv_b, big_w_out[None]]

    grads = ordered(0, g_w_ada_s, g_w_in_s, g_w_out_s)
    deltas = ordered(1, d_w_ada, d_w_in, d_w_out)
    new_m = ordered(2, nm_w_ada, nm_w_in, nm_w_out)
    new_v = ordered(3, nv_w_ada, nv_w_in, nv_w_out)
    return (loss, r["grad_x"][None], *grads, *deltas, *new_m, *new_v)
```

```python
import functools

import jax
import jax.numpy as jnp
import numpy as np
from jax import lax
from jax.experimental import pallas as pl
from jax.experimental.pallas import tpu as pltpu

f32, bf16, i32 = jnp.float32, jnp.bfloat16, jnp.int32
MESH = pl.DeviceIdType.MESH
HIGHEST = lax.Precision.HIGHEST

D = 1024
S = 2048
L = 256
GW = 64
ROWS = S // GW
H = 8
DH = 64
DA = H * DH
DC = 512
WIN_H, WIN_W = 8, 16
N_DR, N_DC = 2 * WIN_H - 1, 2 * WIN_W - 1
RMS_EPS = 1e-6
ROPE_THETA = 10000.0
QK_SCALE = DH ** -0.5
NEG = -1e30

QB = 128
NQB = S // QB
KR = 9
KB = KR * GW
TILE_GEOM = ((0, 0), (2, 0), (4, 0), (28, 23), (30, 23))
NT = len(TILE_GEOM)

ADAM_LR, ADAM_B1, ADAM_B2, ADAM_EPS, ADAM_WD, ADAM_STEP = 0.001, 0.9, 0.999, 1e-08, 0.01, 10

VMEM_SPEC = pl.BlockSpec(memory_space=pltpu.VMEM)
ANY_SPEC = pl.BlockSpec(memory_space=pl.ANY)
SMEM_SPEC = pl.BlockSpec(memory_space=pltpu.SMEM)
SDS = jax.ShapeDtypeStruct


_pallas_call = pl.pallas_call


def _hbm_call(body, *, out_shape, in_specs=None, out_specs=None, grid_spec=None, **kw):
    n_pre = 0
    if grid_spec is not None:
        ispecs, ospecs, n_pre = grid_spec.in_specs, grid_spec.out_specs, grid_spec.num_scalar_prefetch
        kw["grid_spec"] = grid_spec
    else:
        ispecs, ospecs = in_specs, out_specs
        kw.update(in_specs=in_specs, out_specs=out_specs)

    def blocked(spec):
        return isinstance(spec, pl.BlockSpec) and spec.block_shape is not None

    single = not isinstance(out_shape, (tuple, list))
    shapes = [out_shape] if single else list(out_shape)
    ospec_list = list(ospecs) if isinstance(ospecs, (tuple, list)) else [ospecs]
    shapes = [pltpu.HBM(s.shape, s.dtype) if blocked(sp) else s for s, sp in zip(shapes, ospec_list)]
    call = _pallas_call(body, out_shape=shapes[0] if single else tuple(shapes), **kw)

    def run(*args):
        arrays = [pltpu.with_memory_space_constraint(a, pltpu.HBM) if blocked(sp) else a
                  for a, sp in zip(args[n_pre:], ispecs)]
        return call(*args[:n_pre], *arrays)

    return run


def _cp(vmem_mb=None, **kw):
    if vmem_mb is not None:
        kw["vmem_limit_bytes"] = vmem_mb << 20
    return pltpu.CompilerParams(**kw)


def _silu(z):
    return z * jax.nn.sigmoid(z)


def _dsilu(z):
    sg = jax.nn.sigmoid(z)
    return sg * (1.0 + z * (1.0 - sg))


def _row_start(i):
    return min(max(i - WIN_H // 2, 0), ROWS - WIN_H)


def _my_pos():
    return lax.axis_index("x"), lax.axis_index("y"), lax.axis_index("c")


def _flip(v, bit):
    return 1 - v if bit else v


def _chip_gather_pair_swap(small, halves, name):
    nh = len(halves)

    def body(*refs):
        s_in, h_in, s_out, h_out = refs[0], refs[1:1 + nh], refs[1 + nh], refs[2 + nh:2 + 2 * nh]
        ssem, rsem, lsem = refs[2 + 2 * nh:]
        x, y, c = _my_pos()
        j = 2 * x + y
        local = pltpu.make_async_copy(s_in, s_out.at[j], lsem)
        local.start()
        cps = [_Copy(s_in, s_out.at[j], s_out.at[pj], ssem.at[k], rsem.at[k], dev)
               for k, (dev, pj) in enumerate(_peer_chips(x, y, c))]
        cps += [_Copy(h_in[a], h_out[a], h_out[a], ssem.at[3 + a], rsem.at[3 + a], (x, y, 1 - c)) for a in range(nh)]
        for cp in cps:
            cp.start()
        for cp in cps:
            cp.wait_recv()
        for cp in cps:
            cp.wait_send()
        local.wait()

    outs = _hbm_call(
        body, name=name, out_shape=[SDS((4,) + small.shape, small.dtype)] + [SDS(h.shape, h.dtype) for h in halves],
        in_specs=[VMEM_SPEC] + [ANY_SPEC] * nh, out_specs=[VMEM_SPEC] + [ANY_SPEC] * nh,
        scratch_shapes=[pltpu.SemaphoreType.DMA((3 + nh,)), pltpu.SemaphoreType.DMA((3 + nh,)), pltpu.SemaphoreType.DMA],
    )(small, *halves)
    return outs[0], list(outs[1:])


HBM_SPEC = pl.BlockSpec(memory_space=pltpu.HBM)
SEM_SPEC = pl.BlockSpec(memory_space=pltpu.SEMAPHORE)
DATAFLOW = pltpu.SideEffectType.DATAFLOW_SIDE_EFFECTING


def _peer_chips(x, y, c):
    out = []
    for k in range(1, 4):
        px, py = _flip(x, (k >> 1) & 1), _flip(y, k & 1)
        out.append(((px, py, c), 2 * px + py))
    return out


def _half_copies(srcs, dsts, ssem, rsem, which):
    x, y, c = _my_pos()
    j = 2 * x + y
    peers = _peer_chips(x, y, c)
    pairs = []
    for pos, group, k in which:
        half = srcs[pos].shape[1] // 2
        mine = pl.ds(pl.multiple_of(c * half, 8), half)
        dev, pj = peers[k]
        sem = 3 * group + k
        send = pltpu.make_async_remote_copy(src_ref=srcs[pos].at[j, mine], dst_ref=dsts[pos].at[j, mine],
                                            send_sem=ssem.at[sem], recv_sem=rsem.at[sem], device_id=dev,
                                            device_id_type=MESH)
        arrive = pltpu.make_async_remote_copy(src_ref=srcs[pos].at[j, mine], dst_ref=dsts[pos].at[pj, mine],
                                              send_sem=ssem.at[sem], recv_sem=rsem.at[sem], device_id=dev,
                                              device_id_type=MESH)
        pairs.append((send, arrive))
    return pairs


def _halves_wait(ssem, rsem, bigs, after, which, name):
    nb = len(bigs)

    def body(*refs):
        b_in = refs[:nb]
        ssem_ref, rsem_ref = refs[nb], refs[nb + 1]
        for send, arrive in _half_copies(b_in, b_in, ssem_ref, rsem_ref, which):
            send.wait_send()
            arrive.wait_recv()

    return _hbm_call(
        body, name=name, out_shape=tuple(pltpu.HBM(b.shape, b.dtype) for b in bigs),
        in_specs=[HBM_SPEC] * nb + [SEM_SPEC, SEM_SPEC, ANY_SPEC], out_specs=tuple([HBM_SPEC] * nb),
        input_output_aliases={a: a for a in range(nb)}, compiler_params=_cp(has_side_effects=DATAFLOW),
    )(*bigs, ssem, rsem, after)


FORWARD_SEM = 3


def _diag_forward_copy(srcs, dsts, ssem, rsem):
    x, y, c = _my_pos()
    half = srcs[0].shape[1] // 2
    diag = 3 - (2 * x + y)
    mine = pl.ds(pl.multiple_of(c * half, 8), half)
    other = pl.ds(pl.multiple_of((1 - c) * half, 8), half)
    return [_Copy(srcs[0].at[diag, mine], dsts[0].at[diag, mine], dsts[0].at[diag, other], ssem.at[FORWARD_SEM],
                  rsem.at[FORWARD_SEM], (x, y, 1 - c))]


def _forward_then_start(fwd, big, order, name):
    def body(f_in, b_in, f_out, ssem, rsem, b_out, token):
        _diag_forward_copy([f_in], [f_out], ssem, rsem)[0].start()
        for send, _ in _half_copies([b_in], [b_out], ssem, rsem, order):
            send.start()
        token[...] = jnp.zeros_like(token)

    n_sem = FORWARD_SEM + 1
    out_shape = (pltpu.HBM(fwd.shape, fwd.dtype), pltpu.SemaphoreType.DMA((n_sem,)), pltpu.SemaphoreType.DMA((n_sem,)),
                 pltpu.HBM(big.shape, big.dtype), SDS((8, 128), f32))
    return _hbm_call(
        body, name=name, out_shape=out_shape, in_specs=[HBM_SPEC, HBM_SPEC],
        out_specs=(HBM_SPEC, SEM_SPEC, SEM_SPEC, HBM_SPEC, VMEM_SPEC), input_output_aliases={0: 0, 1: 3},
        compiler_params=_cp(has_side_effects=DATAFLOW),
    )(*[pltpu.with_memory_space_constraint(b, pltpu.HBM) for b in (fwd, big)])


def _cast_to_slot(w, jvec, name, after=None):
    rows, cols = w.shape
    tr = 256

    def body(j_ref, w_ref, after_ref, o_ref):
        o_ref[...] = w_ref[...].astype(bf16)

    grid_spec = pltpu.PrefetchScalarGridSpec(
        num_scalar_prefetch=1, grid=(rows // tr,),
        in_specs=[pl.BlockSpec((tr, cols), lambda i, j: (i, 0)), ANY_SPEC],
        out_specs=pl.BlockSpec((None, tr, cols), lambda i, j: (j[0], i, 0)))
    return _hbm_call(body, name=name, out_shape=SDS((4, rows, cols), bf16),
                     grid_spec=grid_spec)(jvec, w, jvec if after is None else after)


def _exchange_copies(srcs, lands, ssem, rsem):
    x, y, c = _my_pos()
    half = srcs[0].shape[0] // 8
    cps = []
    for jb in range(4):
        src = srcs[0].at[pl.ds(pl.multiple_of((2 * jb + 1 - c) * half, 8), half)]
        cps.append(pltpu.make_async_remote_copy(src_ref=src, dst_ref=lands[0].at[jb], send_sem=ssem.at[jb],
                                                recv_sem=rsem.at[jb], device_id=(x, y, 1 - c), device_id_type=MESH))
    return cps


def _block_exchange_copies(srcs, lands, ssem, rsem):
    x, y, c = _my_pos()
    return [pltpu.make_async_remote_copy(src_ref=srcs[0].at[jb], dst_ref=lands[0].at[jb], send_sem=ssem.at[jb],
                                         recv_sem=rsem.at[jb], device_id=(x, y, 1 - c), device_id_type=MESH)
            for jb in range(4)]


def _scatter_copies(srcs, lands, ssem, rsem):
    x, y, c = _my_pos()
    cps = []
    for a in range(len(srcs)):
        for k, (dev, pj) in enumerate(_peer_chips(x, y, c)):
            cps.append(pltpu.make_async_remote_copy(src_ref=srcs[a].at[pj], dst_ref=lands[a].at[k],
                                                    send_sem=ssem.at[3 * a + k], recv_sem=rsem.at[3 * a + k],
                                                    device_id=dev, device_id_type=MESH))
    return cps


class _Copy:
    def __init__(self, src, dst, arrive, ssem, rsem, dev):
        make = lambda to: pltpu.make_async_remote_copy(src_ref=src, dst_ref=to, send_sem=ssem, recv_sem=rsem,
                                                       device_id=dev, device_id_type=MESH)
        send, arrival = make(dst), make(arrive)
        self.start, self.wait_send, self.wait_recv = send.start, send.wait_send, arrival.wait_recv


def _toward(x, y, along_x):
    return x + along_x * (1 - 2 * x), y + (1 - along_x) * (1 - 2 * y)


def _near_copies(srcs, dsts, ssem, rsem):
    x, y, c = _my_pos()
    px, py = _toward(x, y, c)
    j = 2 * x + y
    return [_Copy(srcs[0].at[j], dsts[0].at[j], dsts[0].at[2 * px + py], ssem.at[0], rsem.at[0], (px, py, c))]


def _pass_copy(srcs, dsts, ssem, rsem):
    x, y, c = _my_pos()
    px, py = _toward(x, y, c)
    qx, qy = _toward(x, y, 1 - c)
    got = 2 * px + py
    return [_Copy(srcs[0].at[got], dsts[0].at[got], dsts[0].at[2 * qx + qy], ssem.at[0], rsem.at[0], (x, y, 1 - c))]


def _relay_copy(srcs, dsts, ssem, rsem):
    x, y, c = _my_pos()
    px, py = _toward(x, y, c)
    qx, qy = _toward(x, y, 1 - c)
    half = srcs[0].shape[1] // 2
    mine = pl.ds(pl.multiple_of(c * half, 8), half)
    got, diag = 2 * px + py, 3 - (2 * x + y)
    return [_Copy(srcs[0].at[got, mine], dsts[0].at[got, mine], dsts[0].at[diag, mine], ssem.at[1], rsem.at[1],
                  (qx, qy, c))]


def _forward_copies(srcs, dsts, ssem, rsem):
    x, y, c = _my_pos()
    half = srcs[0].shape[1] // 2
    mine = pl.ds(pl.multiple_of(c * half, 8), half)
    other = pl.ds(pl.multiple_of((1 - c) * half, 8), half)
    return [_Copy(srcs[0].at[pj, mine], dsts[0].at[pj, mine], dsts[0].at[pj, other], ssem.at[k], rsem.at[k],
                  (x, y, 1 - c)) for k, (_, pj) in enumerate(_peer_chips(x, y, c))]


def _pass_relay_copies(srcs, dsts, ssem, rsem):
    return _pass_copy(srcs, dsts, ssem, rsem) + _relay_copy(srcs, dsts, ssem, rsem)


def _gather8_copies(srcs, dsts, ssem, rsem):
    x, y, c = _my_pos()
    me = 4 * x + 2 * y + c
    cps = []
    for a in range(len(srcs)):
        for k in range(1, 8):
            tgt = (_flip(x, (k >> 2) & 1), _flip(y, (k >> 1) & 1), _flip(c, k & 1))
            cps.append(_Copy(srcs[a].at[me], dsts[a].at[me], dsts[a].at[4 * tgt[0] + 2 * tgt[1] + tgt[2]],
                             ssem.at[7 * a + k - 1], rsem.at[7 * a + k - 1], tgt))
    return cps


def _gather4_copies(srcs, dsts, ssem, rsem):
    x, y, c = _my_pos()
    j = 2 * x + y
    cps = []
    for a in range(len(srcs)):
        for k, (dev, pj) in enumerate(_peer_chips(x, y, c)):
            cps.append(_Copy(srcs[a].at[j], dsts[a].at[j], dsts[a].at[pj], ssem.at[3 * a + k], rsem.at[3 * a + k], dev))
    return cps


def _to_slot(a, n, i):
    return lax.dynamic_update_slice(jnp.zeros((n,) + a.shape, a.dtype), a[None], (i,) + (0,) * a.ndim)


def _split_start(srcs, land_shapes, n_cp, make, name, after=None):
    ns, nl = len(srcs), len(land_shapes)
    n_in = ns + nl + (after is not None)

    def body(*refs):
        s_in = refs[:ns]
        ssem, rsem = refs[n_in], refs[n_in + 1]
        s_out = refs[n_in + 2:n_in + 2 + ns]
        l_out = refs[n_in + 2 + ns:n_in + 2 + ns + nl]
        token = refs[n_in + 2 + ns + nl]
        for cp in make(s_in, l_out if nl else s_out, ssem, rsem):
            cp.start()
        token[...] = jnp.zeros_like(token)

    lands = [pltpu.with_memory_space_constraint(lax.empty(sh.shape, sh.dtype), pltpu.HBM) for sh in land_shapes]
    out_shape = (pltpu.SemaphoreType.DMA((n_cp,)), pltpu.SemaphoreType.DMA((n_cp,)),
                 *[pltpu.HBM(b.shape, b.dtype) for b in srcs], *[pltpu.HBM(b.shape, b.dtype) for b in land_shapes],
                 SDS((8, 128), f32))
    return _hbm_call(
        body, name=name, out_shape=out_shape, in_specs=[HBM_SPEC] * (ns + nl) + [ANY_SPEC] * (after is not None),
        out_specs=(SEM_SPEC, SEM_SPEC, *[HBM_SPEC] * (ns + nl), VMEM_SPEC),
        input_output_aliases={i: 2 + i for i in range(ns + nl)}, compiler_params=_cp(has_side_effects=DATAFLOW),
    )(*[pltpu.with_memory_space_constraint(b, pltpu.HBM) for b in srcs], *lands, *([] if after is None else [after]))


def _split_wait(ssem, rsem, srcs, lands, after, make, name):
    ns, nl = len(srcs), len(lands)

    def body(*refs):
        s_in, l_in = refs[:ns], refs[ns:ns + nl]
        ssem_ref, rsem_ref = refs[ns + nl], refs[ns + nl + 1]
        for cp in make(s_in, l_in if nl else s_in, ssem_ref, rsem_ref):
            cp.wait_send()
            cp.wait_recv()

    outs = _hbm_call(
        body, name=name, out_shape=tuple(pltpu.HBM(b.shape, b.dtype) for b in (*srcs, *lands)),
        in_specs=[HBM_SPEC] * (ns + nl) + [SEM_SPEC, SEM_SPEC, ANY_SPEC], out_specs=tuple([HBM_SPEC] * (ns + nl)),
        input_output_aliases={i: i for i in range(ns + nl)}, compiler_params=_cp(has_side_effects=DATAFLOW),
    )(*srcs, *lands, ssem, rsem, after)
    return list(outs[:ns]), list(outs[ns:])


def _adaln_shard(cc, w_ada_shard):
    def body(c_ref, w_ref, m_ref, sc_ref):
        sc = _silu(c_ref[...])
        sc_ref[...] = sc
        m_ref[...] = jnp.dot(sc, w_ref[...], precision=HIGHEST, preferred_element_type=f32)

    return _hbm_call(
        body, name="adaln_shard", out_shape=(SDS((16, w_ada_shard.shape[1]), f32), SDS((16, D), f32)),
        in_specs=[VMEM_SPEC, VMEM_SPEC], out_specs=(VMEM_SPEC, VMEM_SPEC), compiler_params=_cp(32),
    )(cc, w_ada_shard)


def _prenorm(xx, norm_g, mrow, b_ada, tm, name, after=None):
    n = xx.shape[0]

    def body(x_ref, g_ref, m_ref, b_ref, after_ref, h_ref):
        x = x_ref[...]
        shift = m_ref[:, 0:D] + b_ref[:, 0:D]
        scale = m_ref[:, D:2 * D] + b_ref[:, D:2 * D]
        r = lax.rsqrt(jnp.mean(x * x, axis=-1, keepdims=True) + RMS_EPS)
        y = (x * r) * g_ref[...]
        h_ref[...] = (y * (1.0 + scale) + shift).astype(bf16)

    row = lambda i: (i, 0)
    fixed = lambda i: (0, 0)
    return _hbm_call(
        body, name=name, out_shape=SDS((n, D), bf16), grid=(n // tm,),
        in_specs=[pl.BlockSpec((tm, D), row), pl.BlockSpec((1, D), fixed), pl.BlockSpec((1, 3 * D), fixed),
                  pl.BlockSpec((1, 3 * D), fixed), ANY_SPEC],
        out_specs=pl.BlockSpec((tm, D), row),
    )(xx, norm_g, mrow, b_ada, b_ada if after is None else after)


def _in_proj_own(h, w_own, jvec):
    tm = 512

    def body(j_ref, h_ref, w_ref, p_ref):
        p_ref[...] = jnp.dot(h_ref[...], w_ref[...].astype(bf16), preferred_element_type=f32)

    grid_spec = pltpu.PrefetchScalarGridSpec(
        num_scalar_prefetch=1, grid=(S // tm,),
        in_specs=[pl.BlockSpec((tm, D), lambda i, j: (i, 0)), pl.BlockSpec((D, D), lambda i, j: (0, 0))],
        out_specs=pl.BlockSpec((tm, D), lambda i, j: (i, j[0])))
    return _hbm_call(body, name="in_proj_own", out_shape=SDS((S, 4 * D), f32), grid_spec=grid_spec,
                     compiler_params=_cp(40))(jvec, h, w_own)


def _in_proj_block(h, w4, p, bvec, name, after=None):
    tm = 512

    def body(b_ref, h_ref, w_ref, p_in_ref, after_ref, p_ref):
        p_ref[...] = jnp.dot(h_ref[...], w_ref[...], preferred_element_type=f32)

    grid_spec = pltpu.PrefetchScalarGridSpec(
        num_scalar_prefetch=1, grid=(S // tm,),
        in_specs=[pl.BlockSpec((tm, D), lambda i, b: (i, 0)), pl.BlockSpec((None, D, D), lambda i, b: (b[0], 0, 0)),
                  ANY_SPEC, ANY_SPEC],
        out_specs=pl.BlockSpec((tm, D), lambda i, b: (i, b[0])))
    return _hbm_call(body, name=name, out_shape=SDS((S, 4 * D), f32), grid_spec=grid_spec,
                     input_output_aliases={3: 0})(bvec, h, w4, p, bvec if after is None else after)


def _ctx_proj(hc, w4):
    def body(h_ref, w0_ref, w1_ref, p_ref):
        hv = h_ref[...]
        p_ref[:, 0:DA] = jnp.dot(hv, w0_ref[:, DA:2 * DA], preferred_element_type=f32)
        p_ref[:, DA:2 * DA] = jnp.dot(hv, w1_ref[:, 0:DA], preferred_element_type=f32)

    return _hbm_call(
        body, name="ctx_proj", out_shape=SDS((L, 2 * DA), f32), grid=(1,),
        in_specs=[pl.BlockSpec((L, D), lambda i: (0, 0)), pl.BlockSpec((None, D, D), lambda i: (0, 0, 0)),
                  pl.BlockSpec((None, D, D), lambda i: (1, 0, 0))],
        out_specs=pl.BlockSpec((L, 2 * DA), lambda i: (0, 0)),
    )(hc, w4, w4)


def _head_ones():
    r = lax.broadcasted_iota(i32, (DA, DA), 0) // DH
    c = lax.broadcasted_iota(i32, (DA, DA), 1) // DH
    return (r == c).astype(bf16)


def _head_sum(v, ones_bd):
    hi = v.astype(bf16)
    lo = (v - hi.astype(f32)).astype(bf16)
    return jnp.dot(hi, ones_bd, preferred_element_type=f32) + jnp.dot(lo, ones_bd, preferred_element_type=f32)


def _swap16(v):
    lane = lax.broadcasted_iota(i32, v.shape, 1)
    return jnp.where((lane & 31) < 16, pltpu.roll(v, DA - 16, 1), pltpu.roll(v, 16, 1))


def _rope_block(ct_ref, rt_ref, tm):
    rows = [jnp.tile(rt_ref[8 * j:8 * j + 8, :], (GW // 8, 1)) for j in range(tm // GW)]
    return jnp.tile(ct_ref[...], (tm // GW, 1)) + jnp.concatenate(rows, axis=0)


def _rope_specs(tm):
    col = pl.BlockSpec((GW, DA), lambda i: (0, 0))
    row = pl.BlockSpec((8 * tm // GW, DA), lambda i: (i, 0))
    return [col, row, col, row]


def _qk_prep(p, gq, gk, rope):
    tm = 256

    def body(qk_ref, v_ref, gq_ref, gk_ref, cc_ref, cr_ref, sc_ref, sr_ref, qr_ref, qp_ref, kr_ref, vh_ref):
        ones_bd = _head_ones()
        cs, sn = _rope_block(cc_ref, cr_ref, tm), _rope_block(sc_ref, sr_ref, tm)
        q = qk_ref[:, 0:DA]
        k = qk_ref[:, DA:2 * DA]
        yq = (q * lax.rsqrt(_head_sum(q * q, ones_bd) * (1.0 / DH) + RMS_EPS)) * gq_ref[...]
        yk = (k * lax.rsqrt(_head_sum(k * k, ones_bd) * (1.0 / DH) + RMS_EPS)) * gk_ref[...]
        qr = (yq * cs + _swap16(yq) * sn) * QK_SCALE
        qp = yq * QK_SCALE
        kr = yk * cs + _swap16(yk) * sn
        vv = v_ref[...]
        for hh in range(H):
            sl = slice(hh * DH, (hh + 1) * DH)
            qr_ref[hh] = qr[:, sl].astype(bf16)
            qp_ref[hh] = qp[:, sl].astype(bf16)
            kr_ref[hh] = kr[:, sl].astype(bf16)
            vh_ref[hh] = vv[:, sl].astype(bf16)

    hm = SDS((H, S, DH), bf16)
    hspec = pl.BlockSpec((H, tm, DH), lambda i: (0, i, 0))
    fixed = lambda i: (0, 0)
    return _hbm_call(
        body, name="qk_prep", out_shape=(hm, hm, hm, hm), grid=(S // tm,),
        in_specs=[pl.BlockSpec((tm, 2 * DA), lambda i: (i, 0)), pl.BlockSpec((tm, DA), lambda i: (i, 2)),
                  pl.BlockSpec((1, DA), fixed), pl.BlockSpec((1, DA), fixed)] + _rope_specs(tm),
        out_specs=(hspec, hspec, hspec, hspec),
    )(p, p, gq, gk, *rope)


def _ctx_prep(pc, gk):
    def body(p_ref, gk_ref, kc_ref, vc_ref):
        ones_bd = _head_ones()
        k = p_ref[:, 0:DA]
        yk = (k * lax.rsqrt(_head_sum(k * k, ones_bd) * (1.0 / DH) + RMS_EPS)) * gk_ref[...]
        vv = p_ref[:, DA:2 * DA]
        for hh in range(H):
            sl = slice(hh * DH, (hh + 1) * DH)
            kc_ref[hh] = yk[:, sl].astype(bf16)
            vc_ref[hh] = vv[:, sl].astype(bf16)

    hm = SDS((H, L, DH), bf16)
    return _hbm_call(
        body, name="ctx_prep", out_shape=(hm, hm), in_specs=[VMEM_SPEC, VMEM_SPEC], out_specs=(VMEM_SPEC, VMEM_SPEC),
    )(pc, gk)


def _tile_pieces():
    out = []
    for (i0, u0) in TILE_GEOM:
        rows = []
        for j in range(2):
            i = i0 + j
            rs = _row_start(i)
            rows.append([(u0 + u - i + WIN_H - 1) if rs <= u0 + u < rs + WIN_H else None for u in range(KR)])
        out.append(rows)
    return out


def _bias_prep(rpb_rev_pad, after=None):
    pieces = _tile_pieces()

    def body(r_ref, after_ref, o_ref):
        rp = r_ref[...]
        xs = jnp.broadcast_to(rp[:, None, :], (N_DR, GW, 128)).reshape(N_DR * GW, 128)
        row = lax.broadcasted_iota(i32, xs.shape, 0)
        lane = lax.broadcasted_iota(i32, xs.shape, 1)
        for b in range(6):
            xs = jnp.where(((row >> b) & 1) == 1, pltpu.roll(xs, 1 << b, 1), xs)
        xs = pltpu.roll(xs, 128 - (WIN_W - 1), 1)
        k = row & (GW - 1)
        c0 = jnp.clip(lane - WIN_W // 2, 0, GW - WIN_W)
        xs = jnp.where((k >= c0) & (k < c0 + WIN_W), xs, NEG)
        neg = jnp.full((GW, GW), NEG, f32)
        for t in range(NT):
            for j in range(2):
                for u in range(KR):
                    dr = pieces[t][j][u]
                    piece = neg if dr is None else xs[dr * GW:(dr + 1) * GW, 0:GW]
                    o_ref[t, u * GW:(u + 1) * GW, j * GW:(j + 1) * GW] = piece

    return _hbm_call(
        body, name="bias_prep", out_shape=SDS((H, NT, KB, QB), f32), grid=(H,),
        in_specs=[pl.BlockSpec((None, N_DR, 128), lambda h: (h, 0, 0)), ANY_SPEC],
        out_specs=pl.BlockSpec((None, NT, KB, QB), lambda h: (h, 0, 0, 0)),
    )(rpb_rev_pad, rpb_rev_pad if after is None else after)


def _bias_tiles(rpb2, after=None):
    return _bias_prep(jnp.pad(rpb2[:, :, ::-1], ((0, 0), (0, 0), (0, 128 - N_DC))), after)


def _block_geom(b):
    qs = b * QB
    ks = min(max(2 * b - 4, 0), ROWS - KR) * GW
    t = b if b < 2 else (b - (NQB - NT) if b > NQB - 3 else 2)
    return qs, ks, t


def _tt(a, b):
    return lax.dot_general(a, b, (((1,), (1,)), ((), ())), preferred_element_type=f32)


def _tn(a, b):
    return lax.dot_general(a, b, (((0,), (0,)), ((), ())), preferred_element_type=f32)


def _softmax_t(s_lat, s_ctx):
    m = jnp.maximum(jnp.max(s_lat, axis=0, keepdims=True), jnp.max(s_ctx, axis=0, keepdims=True))
    e_lat = jnp.exp(s_lat - m)
    e_ctx = jnp.exp(s_ctx - m)
    inv = 1.0 / (jnp.sum(e_lat, axis=0, keepdims=True) + jnp.sum(e_ctx, axis=0, keepdims=True))
    return e_lat * inv, e_ctx * inv


def _staged(n_blocks, stages):
    held = [dict() for _ in stages]
    for step in range(n_blocks + len(stages) - 1):
        for s, fn in enumerate(stages):
            b = step - s
            if 0 <= b < n_blocks:
                held[s][b] = fn(b) if s == 0 else fn(b, held[s - 1].pop(b))


def _attn_fwd(qr, qp, kr, vh, kc, vc, btt):
    def body(qr_ref, qp_ref, kr_ref, v_ref, kc_ref, vc_ref, bt_ref, o_ref):
        kcv, vcv = kc_ref[...], vc_ref[...]

        def scores(b):
            qs, ks, t = _block_geom(b)
            return (_tt(kr_ref[ks:ks + KB, :], qr_ref[qs:qs + QB, :]) + bt_ref[t], _tt(kcv, qp_ref[qs:qs + QB, :]))

        def probs(b, sc):
            p_lat, p_ctx = _softmax_t(*sc)
            return p_lat.astype(bf16), p_ctx.astype(bf16)

        def values(b, p):
            qs, ks, _ = _block_geom(b)
            o_ref[qs:qs + QB, :] = _tn(p[0], v_ref[ks:ks + KB, :]) + _tn(p[1], vcv)

        _staged(NQB, (scores, probs, values))

    sq = pl.BlockSpec((None, S, DH), lambda h: (h, 0, 0))
    sc = pl.BlockSpec((None, L, DH), lambda h: (h, 0, 0))
    return _hbm_call(
        body, name="attn_fwd", out_shape=SDS((H, S, DH), f32), grid=(H,),
        in_specs=[sq, sq, sq, sq, sc, sc, pl.BlockSpec((None, NT, KB, QB), lambda h: (h, 0, 0, 0))],
        out_specs=sq, compiler_params=_cp(48),
    )(qr, qp, kr, vh, kc, vc, btt)


def _shift_rows(v, down):
    n = v.shape[0]
    row = lax.broadcasted_iota(i32, v.shape, 0)
    if down:
        return jnp.where(row == 0, 0.0, pltpu.roll(v, 1, 0))
    return jnp.where(row == n - 1, 0.0, pltpu.roll(v, n - 1, 0))


def _conv_specs():
    col = lambda off: pl.BlockSpec((S, 128), lambda i, off=off: (0, off + i))
    return [col(16), col(20), col(24), col(28), pl.BlockSpec((3, 128), lambda i: (0, i)),
            pl.BlockSpec((1, 128), lambda i: (0, i))]


def _conv_fwd(p, conv_w, conv_b, after=None):
    def body(u_ref, bg_ref, cg_ref, zc_ref, w_ref, b_ref, after_ref, o_ref):
        cu = cg_ref[...] * u_ref[...]
        cv = b_ref[...] + _shift_rows(cu, True) * w_ref[0:1, :]
        cv = cv + cu * w_ref[1:2, :]
        cv = cv + _shift_rows(cu, False) * w_ref[2:3, :]
        o_ref[...] = ((bg_ref[...] * cv) * _silu(zc_ref[...])).astype(bf16)

    return _hbm_call(
        body, name="conv_fwd", out_shape=SDS((S, DC), bf16), grid=(DC // 128,),
        in_specs=_conv_specs() + [ANY_SPEC], out_specs=pl.BlockSpec((S, 128), lambda i: (0, i)),
        compiler_params=_cp(40),
    )(p, p, p, p, conv_w, conv_b, conv_b if after is None else after)


DP_Q, DP_K, DP_V, DP_ZA, DP_U, DP_BG, DP_CG, DP_ZC = range(8)


def _out_proj_loss(o, p, conv_g, w_out, xx, tgt, mrow, b_ada):
    tm = 256

    def body(o_ref, za_ref, c_ref, w_ref, x_ref, t_ref, m_ref, b_ref,
             dy_ref, dconv_ref, dp_ref, do_ref, gwo_ref, dgate_ref, loss_ref):
        k = pl.program_id(0)

        @pl.when(k == 0)
        def _():
            gwo_ref[...] = jnp.zeros_like(gwo_ref)
            dgate_ref[...] = jnp.zeros_like(dgate_ref)
            loss_ref[0, 0] = 0.0

        gate = m_ref[:, 2 * D:3 * D] + b_ref[:, 2 * D:3 * D]
        za = za_ref[...]
        sz = _silu(za)
        om = _merge_heads(o_ref)
        av, cv = (om * sz).astype(bf16), c_ref[...]
        mo = jnp.dot(av, w_ref[0:DA, :], preferred_element_type=f32)
        mo = mo + jnp.dot(cv, w_ref[DA:DA + DC, :], preferred_element_type=f32)
        y = x_ref[...] + gate * mo
        diff = y - t_ref[...]
        loss_ref[0, 0] += jnp.sum(diff * diff)
        dy = diff * (1.0 / D)
        dy_ref[...] = dy
        dgate_ref[...] += jnp.sum(dy * mo, axis=0, keepdims=True)
        dmo = (dy * gate).astype(bf16)
        dmix = _tt(dmo, w_ref[...])
        dattn = dmix[:, 0:DA]
        dconv_ref[...] = dmix[:, DA:DA + DC]
        a = dattn * sz
        for hh in range(H):
            do_ref[hh] = a[:, hh * DH:(hh + 1) * DH].astype(bf16)
        dp_ref[...] = ((dattn * _dsilu(za)) * om).astype(bf16)
        gwo_ref[0:DA, :] += _tn(av, dmo)
        gwo_ref[DA:DA + DC, :] += _tn(cv, dmo)

    row = lambda i: (i, 0)
    fixed = lambda i: (0, 0)
    hspec = pl.BlockSpec((H, tm, DH), lambda i: (0, i, 0))
    return _hbm_call(
        body, name="out_proj_loss",
        out_shape=(SDS((S, D), f32), SDS((S, DC), f32), SDS((8, S, DA), bf16), SDS((H, S, DH), bf16),
                   SDS((D, D), f32), SDS((1, D), f32), SDS((1, 1), f32)),
        grid=(S // tm,),
        in_specs=[hspec, pl.BlockSpec((tm, DA), lambda i: (i, 3)), pl.BlockSpec((tm, DC), row),
                  pl.BlockSpec((D, D), fixed), pl.BlockSpec((tm, D), row), pl.BlockSpec((tm, D), row),
                  pl.BlockSpec((1, 3 * D), fixed), pl.BlockSpec((1, 3 * D), fixed)],
        out_specs=(pl.BlockSpec((tm, D), row), pl.BlockSpec((tm, DC), row),
                   pl.BlockSpec((None, tm, DA), lambda i: (DP_ZA, i, 0)), hspec, pl.BlockSpec((D, D), fixed),
                   pl.BlockSpec((1, D), fixed), SMEM_SPEC),
        compiler_params=_cp(56, dimension_semantics=("arbitrary",)),
    )(o, p, conv_g, w_out, xx, tgt, mrow, b_ada)


def _conv_bwd(dconv, p, conv_w, conv_b, dp8, after=None):
    def body(d_ref, u_ref, bg_ref, cg_ref, zc_ref, w_ref, b_ref, dp_in_ref, after_ref, dp_ref, gw_ref, gb_ref):
        du_ref, dbg_ref, dcg_ref, dzc_ref = dp_ref.at[0], dp_ref.at[1], dp_ref.at[2], dp_ref.at[3]
        dconv = d_ref[...]
        u, bg, cg, zc = u_ref[...], bg_ref[...], cg_ref[...], zc_ref[...]
        w0, w1, w2 = w_ref[0:1, :], w_ref[1:2, :], w_ref[2:3, :]
        cu = cg * u
        cu_m, cu_p = _shift_rows(cu, True), _shift_rows(cu, False)
        cv = b_ref[...] + cu_m * w0
        cv = cv + cu * w1
        cv = cv + cu_p * w2
        sz = _silu(zc)
        dbg_ref[...] = ((dconv * sz) * cv).astype(bf16)
        dzc_ref[...] = ((dconv * (bg * cv)) * _dsilu(zc)).astype(bf16)
        dcv = (dconv * sz) * bg
        gb_ref[...] = jnp.sum(dcv, axis=0, keepdims=True)
        gw_ref[0:1, :] = jnp.sum(dcv * cu_m, axis=0, keepdims=True)
        gw_ref[1:2, :] = jnp.sum(dcv * cu, axis=0, keepdims=True)
        gw_ref[2:3, :] = jnp.sum(dcv * cu_p, axis=0, keepdims=True)
        gw_ref[3:8, :] = jnp.zeros((5, 128), f32)
        dcu = _shift_rows(dcv, False) * w0 + dcv * w1 + _shift_rows(dcv, True) * w2
        dcg_ref[...] = (dcu * u).astype(bf16)
        du_ref[...] = (dcu * cg).astype(bf16)

    return _hbm_call(
        body, name="conv_bwd", out_shape=(SDS((8, S, DC), bf16), SDS((8, DC), f32), SDS((1, DC), f32)),
        grid=(DC // 128,),
        in_specs=[pl.BlockSpec((S, 128), lambda i: (0, i))] + _conv_specs() + [ANY_SPEC, ANY_SPEC],
        out_specs=(pl.BlockSpec((4, S, 128), lambda i: (DP_U // 4, 0, i)), pl.BlockSpec((8, 128), lambda i: (0, i)),
                   pl.BlockSpec((1, 128), lambda i: (0, i))),
        input_output_aliases={7: 0}, compiler_params=_cp(48),
    )(dconv, p, p, p, p, conv_w, conv_b, dp8, conv_b if after is None else after)


def _attn_bwd(qr, qp, kr, vh, kc, vc, btt, do, after=None):
    def body(qr_ref, qp_ref, kr_ref, v_ref, kc_ref, vc_ref, bt_ref, do_ref, after_ref,
             dqr_ref, dqp_ref, dkr_ref, dv_ref, dkc_ref, dvc_ref, dbt_ref):
        kcv, vcv = kc_ref[...], vc_ref[...]
        dkr_ref[...] = jnp.zeros_like(dkr_ref)
        dv_ref[...] = jnp.zeros_like(dv_ref)
        dbt_ref[...] = jnp.zeros_like(dbt_ref)
        ctx_acc = {}

        def products(b):
            qs, ks, t = _block_geom(b)
            dob = do_ref[qs:qs + QB, :]
            s_lat = _tt(kr_ref[ks:ks + KB, :], qr_ref[qs:qs + QB, :]) + bt_ref[t]
            s_ctx = _tt(kcv, qp_ref[qs:qs + QB, :])
            return s_lat, s_ctx, _tt(v_ref[ks:ks + KB, :], dob), _tt(vcv, dob)

        def score_grads(b, x):
            s_lat, s_ctx, dp_lat, dp_ctx = x
            p_lat, p_ctx = _softmax_t(s_lat, s_ctx)
            delta = jnp.sum(p_lat * dp_lat, axis=0, keepdims=True) + jnp.sum(p_ctx * dp_ctx, axis=0, keepdims=True)
            ds_lat = p_lat * (dp_lat - delta)
            ds_ctx = p_ctx * (dp_ctx - delta)
            return ds_lat, ds_lat.astype(bf16), ds_ctx.astype(bf16), p_lat.astype(bf16), p_ctx.astype(bf16)

        def operand_grads(b, y):
            qs, ks, t = _block_geom(b)
            ds_lat, dsb_lat, dsb_ctx, pb_lat, pb_ctx = y
            qrb, qpb, dob = qr_ref[qs:qs + QB, :], qp_ref[qs:qs + QB, :], do_ref[qs:qs + QB, :]
            dbt_ref[t] += ds_lat
            dqr_ref[qs:qs + QB, :] = _tn(dsb_lat, kr_ref[ks:ks + KB, :])
            dqp_ref[qs:qs + QB, :] = _tn(dsb_ctx, kcv)
            dkr_ref[ks:ks + KB, :] += jnp.dot(dsb_lat, qrb, preferred_element_type=f32)
            dv_ref[ks:ks + KB, :] += jnp.dot(pb_lat, dob, preferred_element_type=f32)
            dkc = jnp.dot(dsb_ctx, qpb, preferred_element_type=f32)
            dvc = jnp.dot(pb_ctx, dob, preferred_element_type=f32)
            ctx_acc["k"] = dkc if b == 0 else ctx_acc["k"] + dkc
            ctx_acc["v"] = dvc if b == 0 else ctx_acc["v"] + dvc

        _staged(NQB, (products, score_grads, operand_grads))
        dkc_ref[...] = ctx_acc["k"]
        dvc_ref[...] = ctx_acc["v"]

    sq = pl.BlockSpec((None, S, DH), lambda h: (h, 0, 0))
    sc = pl.BlockSpec((None, L, DH), lambda h: (h, 0, 0))
    sb = pl.BlockSpec((None, NT, KB, QB), lambda h: (h, 0, 0, 0))
    big, ctxs = SDS((H, S, DH), f32), SDS((H, L, DH), f32)
    return _hbm_call(
        body, name="attn_bwd", out_shape=(big, big, big, big, ctxs, ctxs, SDS((H, NT, KB, QB), f32)), grid=(H,),
        in_specs=[sq, sq, sq, sq, sc, sc, sb, sq, ANY_SPEC], out_specs=(sq, sq, sq, sq, sc, sc, sb),
        compiler_params=_cp(56),
    )(qr, qp, kr, vh, kc, vc, btt, do, do if after is None else after)


def _bias_bwd(dbtt, after=None):
    pieces = _tile_pieces()

    def body(d_ref, after_ref, o_ref, scr):
        scr[...] = jnp.zeros_like(scr)
        acc = [None] * N_DR
        for t in range(NT):
            for j in range(2):
                for u in range(KR):
                    dr = pieces[t][j][u]
                    if dr is None:
                        continue
                    piece = d_ref[t, u * GW:(u + 1) * GW, j * GW:(j + 1) * GW]
                    acc[dr] = piece if acc[dr] is None else acc[dr] + piece
        for dr in range(N_DR):
            scr[dr * GW:(dr + 1) * GW, 0:GW] = acc[dr]
        xs = pltpu.roll(scr[...], WIN_W - 1, 1)
        row = lax.broadcasted_iota(i32, xs.shape, 0)
        for b in range(6):
            xs = jnp.where(((row >> b) & 1) == 1, pltpu.roll(xs, 128 - (1 << b), 1), xs)
        rev = jnp.sum(xs.reshape(N_DR, GW, 128), axis=1)
        a = lax.broadcasted_iota(i32, (128, 128), 0)
        b = lax.broadcasted_iota(i32, (128, 128), 1)
        flip = ((a + b == N_DC - 1) & (a < N_DC)).astype(f32)
        o_ref[...] = jnp.dot(rev, flip, precision=HIGHEST, preferred_element_type=f32)

    return _hbm_call(
        body, name="bias_bwd", out_shape=SDS((H, N_DR, 128), f32), grid=(H,),
        in_specs=[pl.BlockSpec((None, NT, KB, QB), lambda h: (h, 0, 0, 0)), ANY_SPEC],
        out_specs=pl.BlockSpec((None, N_DR, 128), lambda h: (h, 0, 0)),
        scratch_shapes=[pltpu.VMEM((N_DR * GW, 128), f32)],
    )(dbtt, dbtt if after is None else after)


def _merge_heads(ref):
    return jnp.concatenate([ref[hh] for hh in range(H)], axis=1)


def _head_norm_bwd(xraw, gain, dy, ones_bd):
    r = lax.rsqrt(_head_sum(xraw * xraw, ones_bd) * (1.0 / DH) + RMS_EPS)
    xh = xraw * r
    gdy = dy * gain
    dx = r * (gdy - xh * (_head_sum(xh * gdy, ones_bd) * (1.0 / DH)))
    return dx, jnp.sum(dy * xh, axis=0, keepdims=True)


def _qk_bwd(dqr, dqp, dkr, dvh, p, gq, gk, rope, dp8):
    tm = 256

    def body(dqr_ref, dqp_ref, dkr_ref, dv_ref, qk_ref, gq_ref, gk_ref, cc_ref, cr_ref, sc_ref, sr_ref, dp_in_ref,
             dp_ref, ggq_ref, ggk_ref):
        dq_ref, dk_ref, dvo_ref = dp_ref.at[DP_Q], dp_ref.at[DP_K], dp_ref.at[DP_V]

        @pl.when(pl.program_id(0) == 0)
        def _():
            ggq_ref[...] = jnp.zeros_like(ggq_ref)
            ggk_ref[...] = jnp.zeros_like(ggk_ref)

        ones_bd = _head_ones()
        cs, sn = _rope_block(cc_ref, cr_ref, tm), _rope_block(sc_ref, sr_ref, tm)
        a = _merge_heads(dqr_ref)
        dyq = ((a * cs - _swap16(a) * sn) + _merge_heads(dqp_ref)) * QK_SCALE
        bk = _merge_heads(dkr_ref)
        dyk = bk * cs - _swap16(bk) * sn
        dq, gq_part = _head_norm_bwd(qk_ref[:, 0:DA], gq_ref[...], dyq, ones_bd)
        dk, gk_part = _head_norm_bwd(qk_ref[:, DA:2 * DA], gk_ref[...], dyk, ones_bd)
        dq_ref[...] = dq.astype(bf16)
        dk_ref[...] = dk.astype(bf16)
        dvo_ref[...] = _merge_heads(dv_ref).astype(bf16)
        ggq_ref[...] += gq_part
        ggk_ref[...] += gk_part

    hspec = pl.BlockSpec((H, tm, DH), lambda i: (0, i, 0))
    fixed = pl.BlockSpec((1, DA), lambda i: (0, 0))
    return _hbm_call(
        body, name="qk_bwd", out_shape=(SDS((8, S, DA), bf16), SDS((1, DA), f32), SDS((1, DA), f32)), grid=(S // tm,),
        in_specs=[hspec, hspec, hspec, hspec, pl.BlockSpec((tm, 2 * DA), lambda i: (i, 0)), fixed, fixed]
        + _rope_specs(tm) + [ANY_SPEC],
        out_specs=(pl.BlockSpec((3, tm, DA), lambda i: (0, i, 0)), fixed, fixed), input_output_aliases={11: 0},
        compiler_params=_cp(40, dimension_semantics=("arbitrary",)),
    )(dqr, dqp, dkr, dvh, p, gq, gk, *rope, dp8)


def _ctx_bwd(dkc, dvc, pc, gk):
    def body(dkc_ref, dvc_ref, p_ref, gk_ref, dk_ref, dv_ref, ggk_ref):
        ones_bd = _head_ones()
        dk, gk_part = _head_norm_bwd(p_ref[:, 0:DA], gk_ref[...], _merge_heads(dkc_ref), ones_bd)
        dk_ref[...] = dk.astype(bf16)
        dv_ref[...] = _merge_heads(dvc_ref).astype(bf16)
        ggk_ref[...] = gk_part

    piece = SDS((L, DA), bf16)
    return _hbm_call(
        body, name="ctx_bwd", out_shape=(piece, piece, SDS((1, DA), f32)), in_specs=[VMEM_SPEC] * 4,
        out_specs=(VMEM_SPEC,) * 3,
    )(dkc, dvc, pc, gk)


def _grad_w_in(h, dp8, hc, dkc_raw, dvc_m, half, name, after=None):
    def body(half_ref, h_ref, p_ref, hc_ref, dk_ref, dv_ref, after_ref, g_ref):
        j = pl.program_id(0)
        hv = h_ref[...]
        g_ref[:, 0:DA] = _tn(hv, p_ref[0])
        g_ref[:, DA:2 * DA] = _tn(hv, p_ref[1])

        @pl.when(j == 0)
        def _():
            g_ref[:, DA:2 * DA] += _tn(hc_ref[...], dk_ref[...])

        @pl.when(j == 1)
        def _():
            g_ref[:, 0:DA] += _tn(hc_ref[...], dv_ref[...])

    fixed = lambda j, s: (0, 0)
    hd = D // 2
    grid_spec = pltpu.PrefetchScalarGridSpec(
        num_scalar_prefetch=1, grid=(4,),
        in_specs=[pl.BlockSpec((S, hd), lambda j, s: (0, s[0])), pl.BlockSpec((2, S, DA), lambda j, s: (j, 0, 0)),
                  pl.BlockSpec((L, hd), lambda j, s: (0, s[0])), pl.BlockSpec((L, DA), fixed),
                  pl.BlockSpec((L, DA), fixed), ANY_SPEC],
        out_specs=pl.BlockSpec((None, hd, D), lambda j, s: (j, 0, 0)))
    return _hbm_call(
        body, name=name, out_shape=SDS((4, hd, D), f32), grid_spec=grid_spec, compiler_params=_cp(40),
    )(half, h, dp8, hc, dkc_raw, dvc_m, half if after is None else after)


def _norm_mod_bwd(x, dh, g, scale):
    r = lax.rsqrt(jnp.mean(x * x, axis=-1, keepdims=True) + RMS_EPS)
    xh = x * r
    y = xh * g
    dshift = jnp.sum(dh, axis=0, keepdims=True)
    dscale = jnp.sum(dh * y, axis=0, keepdims=True)
    dyn = dh * (1.0 + scale)
    dg = jnp.sum(dyn * xh, axis=0, keepdims=True)
    gdy = dyn * g
    dx = r * (gdy - xh * jnp.mean(xh * gdy, axis=-1, keepdims=True))
    return dx, dshift, dscale, dg


def _dh_grad_x(dp8, w4, xx, dy, norm_g, mrow, b_ada, after=None):
    tm = 256

    def body(p_ref, w_ref, x_ref, dy_ref, g_ref, m_ref, b_ref, after_ref, gx_ref, dsh_ref, dsc_ref, dg_ref):
        @pl.when(pl.program_id(0) == 0)
        def _():
            dsh_ref[...] = jnp.zeros_like(dsh_ref)
            dsc_ref[...] = jnp.zeros_like(dsc_ref)
            dg_ref[...] = jnp.zeros_like(dg_ref)

        dh = None
        for j in range(4):
            for half in range(2):
                term = _tt(p_ref[2 * j + half], w_ref[j, :, half * DA:(half + 1) * DA])
                dh = term if dh is None else dh + term
        scale = m_ref[:, D:2 * D] + b_ref[:, D:2 * D]
        dx, dshift, dscale, dg = _norm_mod_bwd(x_ref[...], dh, g_ref[...], scale)
        gx_ref[...] = dy_ref[...] + dx
        dsh_ref[...] += dshift
        dsc_ref[...] += dscale
        dg_ref[...] += dg

    row = lambda i: (i, 0)
    fixed = lambda i: (0, 0)
    vec = SDS((1, D), f32)
    return _hbm_call(
        body, name="dh_grad_x", out_shape=(SDS((S, D), f32), vec, vec, vec), grid=(S // tm,),
        in_specs=[pl.BlockSpec((8, tm, DA), lambda i: (0, i, 0)), pl.BlockSpec((4, D, D), lambda i: (0, 0, 0)),
                  pl.BlockSpec((tm, D), row), pl.BlockSpec((tm, D), row), pl.BlockSpec((1, D), fixed),
                  pl.BlockSpec((1, 3 * D), fixed), pl.BlockSpec((1, 3 * D), fixed), ANY_SPEC],
        out_specs=(pl.BlockSpec((tm, D), row), pl.BlockSpec((1, D), fixed), pl.BlockSpec((1, D), fixed),
                   pl.BlockSpec((1, D), fixed)),
        compiler_params=_cp(56, dimension_semantics=("arbitrary",)),
    )(dp8, w4, xx, dy, norm_g, mrow, b_ada, b_ada if after is None else after)


def _dhc_sums(dkc_raw, dvc_m, w4, ctx2, norm_g, mrow_c, b_ada, after=None):
    def body(dk_ref, dv_ref, w0_ref, w1_ref, x_ref, g_ref, m_ref, b_ref, after_ref, dsh_ref, dsc_ref, dg_ref):
        dh = _tt(dk_ref[...], w0_ref[:, DA:2 * DA]) + _tt(dv_ref[...], w1_ref[:, 0:DA])
        scale = m_ref[:, D:2 * D] + b_ref[:, D:2 * D]
        _, dshift, dscale, dg = _norm_mod_bwd(x_ref[...], dh, g_ref[...], scale)
        dsh_ref[...] = dshift
        dsc_ref[...] = dscale
        dg_ref[...] = dg

    fixed = lambda i: (0, 0)
    vec = SDS((1, D), f32)
    vspec = pl.BlockSpec((1, D), fixed)
    return _hbm_call(
        body, name="dhc_sums", out_shape=(vec, vec, vec), grid=(1,),
        in_specs=[pl.BlockSpec((L, DA), fixed), pl.BlockSpec((L, DA), fixed),
                  pl.BlockSpec((None, D, D), lambda i: (0, 0, 0)), pl.BlockSpec((None, D, D), lambda i: (1, 0, 0)),
                  pl.BlockSpec((L, D), fixed), vspec, pl.BlockSpec((1, 3 * D), fixed), pl.BlockSpec((1, 3 * D), fixed),
                  ANY_SPEC],
        out_specs=(vspec, vspec, vspec), compiler_params=_cp(32),
    )(dkc_raw, dvc_m, w4, w4, ctx2, norm_g, mrow_c, b_ada, b_ada if after is None else after)


def _rope_tables():
    nf = DH // 4
    inv = np.float32(ROPE_THETA) ** (-np.arange(nf, dtype=np.float32) / np.float32(nf))
    ang_c = np.arange(GW, dtype=np.float32)[:, None] * inv
    ang_r = np.arange(ROWS, dtype=np.float32)[:, None] * inv
    zc, zr = np.zeros((GW, 2 * nf), np.float32), np.zeros((ROWS, 2 * nf), np.float32)
    ct_cos = np.tile(np.concatenate([zc, np.cos(ang_c), np.cos(ang_c)], axis=1), (1, H))
    ct_sin = np.tile(np.concatenate([zc, -np.sin(ang_c), np.sin(ang_c)], axis=1), (1, H))
    rt_cos = np.tile(np.concatenate([np.cos(ang_r), np.cos(ang_r), zr], axis=1), (1, H))
    rt_sin = np.tile(np.concatenate([-np.sin(ang_r), np.sin(ang_r), zr], axis=1), (1, H))
    rep8 = lambda t: np.ascontiguousarray(np.broadcast_to(t[:, None, :], (ROWS, 8, DA))).reshape(ROWS * 8, DA)
    return tuple(jnp.asarray(t, f32) for t in (ct_cos, rep8(rt_cos), ct_sin, rep8(rt_sin)))


def _local_step(xx, ctx2, tgt, mrow, mrow_c, b_ada, norm_g, weights, q_norm_g, k_norm_g, rpb2, conv_w_full, conv_b,
                hooks=None):
    hooks = hooks or {}
    gq = jnp.tile(q_norm_g, (1, H))
    gk = jnp.tile(k_norm_g, (1, H))
    rope = _rope_tables()

    h = _prenorm(xx, norm_g, mrow, b_ada, 256, "prenorm_x", after=weights.get("started"))
    hc = _prenorm(ctx2, norm_g, mrow_c, b_ada, L, "prenorm_ctx")
    jv = weights["jvec"]
    p = _in_proj_own(h, weights["own"], jv)
    btb = _bias_tiles(rpb2, after=p)
    w4, started = weights["near"](btb)
    p = _in_proj_block(h, w4, p, weights["first"], "in_proj_near", after=started)
    w4 = weights["near2"](w4, p)
    p = _in_proj_block(h, w4, p, weights["second"], "in_proj_near2")
    w4, started = weights["far"](w4, p)
    p = _in_proj_block(h, w4, p, jv ^ 3, "in_proj_far", after=started)
    pc = _ctx_proj(hc, w4)
    qr, qp, kr, vh = _qk_prep(p, gq, gk, rope)
    kc, vc = _ctx_prep(pc, gk)
    o = _attn_fwd(qr, qp, kr, vh, kc, vc, btb)
    started = weights["out_arrived"](o) if "out_arrived" in weights else None
    conv_g = _conv_fwd(p, conv_w_full, conv_b, after=started)
    w_out_full = weights["out"](conv_g)
    dy, dconv, dp8, do, g_w_out, dgate, loss_sum = _out_proj_loss(o, p, conv_g, w_out_full, xx, tgt, mrow, b_ada)
    started = hooks["g_w_out"](g_w_out) if "g_w_out" in hooks else None
    dp8, g_conv_w, g_conv_b = _conv_bwd(dconv, p, conv_w_full, conv_b, dp8, after=started)
    started = hooks["after_conv"](dp8) if "after_conv" in hooks else None
    dqr, dqp, dkr, dvh, dkc, dvc, dbtb = _attn_bwd(qr, qp, kr, vh, kc, vc, btb, do, after=started)
    dp8, g_gq, g_gk = _qk_bwd(dqr, dqp, dkr, dvh, p, gq, gk, rope, dp8)
    dkc_raw, dvc_m, g_gk_c = _ctx_bwd(dkc, dvc, pc, gk)
    first = hooks.get("first_half", jnp.zeros((1,), i32))
    g_first = _grad_w_in(h, dp8, hc, dkc_raw, dvc_m, first, "grad_w_in_first")
    started = hooks["g_w_in_first"](g_first) if "g_w_in_first" in hooks else None
    g_second = _grad_w_in(h, dp8, hc, dkc_raw, dvc_m, 1 - first, "grad_w_in_second", after=started)
    started = hooks["g_w_in_second"](g_second) if "g_w_in_second" in hooks else None
    dshift_c, dscale_c, dng_c = _dhc_sums(dkc_raw, dvc_m, w4, ctx2, norm_g, mrow_c, b_ada, after=started)
    g_rpb = _bias_bwd(dbtb, after=dshift_c)
    grad_x, dshift, dscale, dng = _dh_grad_x(dp8, w4, xx, dy, norm_g, mrow, b_ada, after=g_rpb)
    return dict(loss_sum=loss_sum, grad_x=grad_x, g_w_in=(g_first, g_second), g_w_out=g_w_out, g_conv_w=g_conv_w,
                g_conv_b=g_conv_b, g_rpb=g_rpb, g_gq=g_gq, g_gk=g_gk, g_gk_c=g_gk_c, dshift=dshift, dscale=dscale,
                dgate=dgate, dng=dng, dshift_c=dshift_c, dscale_c=dscale_c, dng_c=dng_c)


def _pair_sum_w_in(g, r):
    tr = 128

    def body(g_ref, r_ref, t32_ref, tb_ref):
        t = g_ref[...] + r_ref[...]
        t32_ref[...] = t
        tb_ref[...] = t.astype(bf16)

    half = D // 2
    spec = pl.BlockSpec((4, tr, D), lambda i: (0, i, 0))
    return _hbm_call(body, name="pair_sum_w_in", out_shape=(SDS((4, half, D), f32), SDS((4, half, D), bf16)),
                     grid=(half // tr,), in_specs=[spec, spec], out_specs=(spec, spec), compiler_params=_cp(40))(g, r)


def _pair_sum_w_out(g, r, cvec):
    hr = D // 8

    def body(c_ref, g0, g1, g2, g3, r_ref, t32_ref, tb_ref):
        for q, g_ref in enumerate((g0, g1, g2, g3)):
            t = g_ref[...] + r_ref[q]
            t32_ref[q] = t
            tb_ref[q] = t.astype(bf16)

    gspecs = [pl.BlockSpec((hr, D), lambda i, c, q=q: (2 * q + c[0], 0)) for q in range(4)]
    full = pl.BlockSpec((4, hr, D), lambda i, c: (0, 0, 0))
    grid_spec = pltpu.PrefetchScalarGridSpec(num_scalar_prefetch=1, grid=(1,), in_specs=gspecs + [full],
                                             out_specs=(full, full))
    return _hbm_call(body, name="pair_sum_w_out", out_shape=(SDS((4, hr, D), f32), SDS((4, hr, D), bf16)),
                          grid_spec=grid_spec)(cvec, g, g, g, g, r)


def _chip_sum(t32, r2, jvec, name):
    rows = t32.shape[1]
    tr = min(rows, 128)

    def body(j_ref, t_ref, r_ref, u_ref):
        u_ref[...] = ((t_ref[...] + r_ref[0].astype(f32)) + r_ref[1].astype(f32)) + r_ref[2].astype(f32)

    grid_spec = pltpu.PrefetchScalarGridSpec(
        num_scalar_prefetch=1, grid=(rows // tr,),
        in_specs=[pl.BlockSpec((None, tr, D), lambda i, j: (j[0], i, 0)), pl.BlockSpec((3, tr, D), lambda i, j: (0, i, 0))],
        out_specs=pl.BlockSpec((tr, D), lambda i, j: (i, 0)))
    return _hbm_call(body, name=name, out_shape=SDS((rows, D), f32), grid_spec=grid_spec)(jvec, t32, r2)


_PK = {}
_off = 0
for _name, _rows in (("dm", 24), ("dmc", 24), ("dng", 8), ("dng_c", 8), ("gq", 8), ("gk", 8), ("gk_c", 8),
                     ("rpb", H * N_DR), ("conv_b", 8), ("conv_w", 16), ("loss", 8)):
    _PK[_name] = (_off, _off + _rows)
    _off += _rows
PK_ROWS = _off
RS_B_ADA, RS_NORM_G, RS_GQ, RS_GK, RS_RPB, RS_CONV_B, RS_CONV_W, RS_DMC, RS_LOSS, RS_ROWS = (
    0, 24, 32, 40, 48, 168, 176, 192, 216, 224)


def _small_reduce(gathered):
    def body(g_ref, o_ref, dm_ref):
        a0 = _PK["dm"][0]
        dm_ref[...] = jnp.zeros_like(dm_ref)
        for b in range(8):
            for i in range(24):
                dm_ref[b:b + 1, 128 * i:128 * (i + 1)] = g_ref[b, a0 + i:a0 + i + 1, :]
        tot = g_ref[0]
        for b in range(1, 8):
            tot = tot + g_ref[b]

        def rows(name):
            a, z = _PK[name]
            return tot[a:z]

        o_ref[RS_B_ADA:RS_B_ADA + 24] = rows("dm") + rows("dmc")
        o_ref[RS_NORM_G:RS_NORM_G + 8] = rows("dng") + rows("dng_c")
        gq = jnp.broadcast_to(jnp.sum(rows("gq"), axis=0, keepdims=True), (8, 128))
        gk = jnp.broadcast_to(jnp.sum(rows("gk") + rows("gk_c"), axis=0, keepdims=True), (8, 128))
        o_ref[RS_GQ:RS_GQ + 8] = gq + pltpu.roll(gq, DH, 1)
        o_ref[RS_GK:RS_GK + 8] = gk + pltpu.roll(gk, DH, 1)
        o_ref[RS_RPB:RS_RPB + H * N_DR] = rows("rpb")
        o_ref[RS_CONV_B:RS_CONV_B + 8] = rows("conv_b")
        o_ref[RS_CONV_W:RS_CONV_W + 16] = rows("conv_w")
        dmc = rows("dmc")
        o_ref[RS_DMC:RS_DMC + 24] = dmc
        o_ref[RS_LOSS:RS_LOSS + 8] = rows("loss")
        for i in range(24):
            dm_ref[8:9, 128 * i:128 * (i + 1)] = dmc[i:i + 1]

    return _hbm_call(body, name="small_reduce", out_shape=(SDS((RS_ROWS, 128), f32), SDS((16, 3 * D), f32)),
                     in_specs=[VMEM_SPEC], out_specs=(VMEM_SPEC, VMEM_SPEC))(gathered)


def _w_ada_grad(sc16, dm16, w_ada_shard, jvec):
    ncol = w_ada_shard.shape[1]

    def body(j_ref, sc_ref, dm_ref, w_ref, g_ref, part_ref):
        dm = dm_ref[...]
        g_ref[...] = lax.dot_general(sc_ref[...], dm, (((0,), (0,)), ((), ())), precision=HIGHEST,
                                     preferred_element_type=f32)
        part_ref[...] = lax.dot_general(dm[8:16], w_ref[...], (((1,), (1,)), ((), ())), precision=HIGHEST,
                                        preferred_element_type=f32)

    fixed = lambda i, j: (0, 0)
    grid_spec = pltpu.PrefetchScalarGridSpec(
        num_scalar_prefetch=1, grid=(1,),
        in_specs=[pl.BlockSpec((16, D), fixed), pl.BlockSpec((16, ncol), lambda i, j: (0, j[0])),
                  pl.BlockSpec((D, ncol), fixed)],
        out_specs=(pl.BlockSpec((D, ncol), fixed), pl.BlockSpec((8, D), fixed)))
    return _pallas_call(body, name="w_ada_grad", out_shape=(SDS((D, ncol), f32), SDS((8, D), f32)),
                        grid_spec=grid_spec, compiler_params=_cp(40))(jvec, sc16, dm16, w_ada_shard)


def _c_ctx_grad(parts4, c_ctx):
    def body(p_ref, c_ref, o_ref):
        tot = ((p_ref[0] + p_ref[1]) + p_ref[2]) + p_ref[3]
        o_ref[...] = tot[0:1] * _dsilu(c_ref[...].reshape(1, D))

    return _pallas_call(body, name="c_ctx_grad", out_shape=SDS((1, D), f32), in_specs=[VMEM_SPEC, VMEM_SPEC],
                        out_specs=VMEM_SPEC)(parts4, c_ctx)


def _adamw(w, g, m, v, name):
    rows, cols = w.shape
    tr = 256 if rows % 256 == 0 else rows

    def body(w_ref, g_ref, m_ref, v_ref, d_ref, m2_ref, v2_ref):
        gv = g_ref[...]
        m2 = ADAM_B1 * m_ref[...] + (1.0 - ADAM_B1) * gv
        v2 = ADAM_B2 * v_ref[...] + (1.0 - ADAM_B2) * jnp.square(gv)
        m_hat = m2 / (1.0 - ADAM_B1 ** ADAM_STEP)
        v_hat = v2 / (1.0 - ADAM_B2 ** ADAM_STEP)
        d_ref[...] = -ADAM_LR * (m_hat / (jnp.sqrt(v_hat) + ADAM_EPS) + ADAM_WD * w_ref[...])
        m2_ref[...] = m2
        v2_ref[...] = v2

    spec = pl.BlockSpec((tr, cols), lambda i: (i, 0))
    shp = SDS((rows, cols), f32)
    return _hbm_call(body, name=name, out_shape=(shp, shp, shp), grid=(rows // tr,), in_specs=[spec] * 4,
                          out_specs=(spec, spec, spec))(w, g, m, v)


def _adamw_halves(w, g_mine, g_other, m, v, cvec, name):
    rows, cols = w.shape
    half = rows // 2
    tr = min(256, half)
    per_half = half // tr

    def body(c_ref, w_ref, ga_ref, gb_ref, m_ref, v_ref, g_ref, d_ref, m2_ref, v2_ref):
        in_my_half = (pl.program_id(0) // per_half) == c_ref[0]
        gv = jnp.where(in_my_half, ga_ref[...], gb_ref[...])
        g_ref[...] = gv
        m2 = ADAM_B1 * m_ref[...] + (1.0 - ADAM_B1) * gv
        v2 = ADAM_B2 * v_ref[...] + (1.0 - ADAM_B2) * jnp.square(gv)
        m_hat = m2 / (1.0 - ADAM_B1 ** ADAM_STEP)
        v_hat = v2 / (1.0 - ADAM_B2 ** ADAM_STEP)
        d_ref[...] = -ADAM_LR * (m_hat / (jnp.sqrt(v_hat) + ADAM_EPS) + ADAM_WD * w_ref[...])
        m2_ref[...] = m2
        v2_ref[...] = v2

    full = pl.BlockSpec((tr, cols), lambda i, c: (i, 0))
    part = pl.BlockSpec((tr, cols), lambda i, c: (i % per_half, 0))
    shp = SDS((rows, cols), f32)
    grid_spec = pltpu.PrefetchScalarGridSpec(num_scalar_prefetch=1, grid=(rows // tr,),
                                             in_specs=[full, part, part, full, full], out_specs=(full,) * 4)
    return _hbm_call(body, name=name, out_shape=(shp,) * 4, grid_spec=grid_spec)(cvec, w, g_mine, g_other, m, v)


def _adam_math(w, g, m, v):
    m2 = ADAM_B1 * m + (1.0 - ADAM_B1) * g
    v2 = ADAM_B2 * v + (1.0 - ADAM_B2) * jnp.square(g)
    m_hat = m2 / (1.0 - ADAM_B1 ** ADAM_STEP)
    v_hat = v2 / (1.0 - ADAM_B2 ** ADAM_STEP)
    return -ADAM_LR * (m_hat / (jnp.sqrt(v_hat) + ADAM_EPS) + ADAM_WD * w), m2, v2


def _adamw_small(red, g_c_ctx, jvec, ws, ms, vs):
    n = len(ws)

    def body(*refs):
        red_ref, gc_ref, j_ref = refs[:3]
        w_refs, m_refs, v_refs = refs[3:3 + n], refs[3 + n:3 + 2 * n], refs[3 + 2 * n:3 + 3 * n]
        outs = refs[3 + 3 * n:]
        g_out, d_out, m_out, v_out = outs[:n], outs[n:2 * n], outs[2 * n:3 * n], outs[3 * n:]
        chip = j_ref[0]
        lanes = lambda i: (slice(None), slice(128 * i, 128 * (i + 1)))
        row = lambda r0, i: (lambda: red_ref[r0 + i:r0 + i + 1, :])
        whole = (slice(None), slice(None))
        chunks = [
            [((slice(None),), lambda: gc_ref[...].reshape(D))],
            [(lanes(i), row(RS_B_ADA, i)) for i in range(3 * D // 128)],
            [(lanes(i), row(RS_NORM_G, i)) for i in range(D // 128)],
            [(whole, lambda: red_ref[RS_GQ:RS_GQ + 1, 0:DH])],
            [(whole, lambda: red_ref[RS_GK:RS_GK + 1, 0:DH])],
            [((dr,), (lambda dr=dr: red_ref[pl.ds(RS_RPB + dr, H, stride=N_DR), 0:N_DC])) for dr in range(N_DR)],
            [((r,), (lambda r=r: red_ref[pl.ds(RS_CONV_W + 4 * r + chip, 1), :])) for r in range(3)],
            [(lanes(i), row(RS_CONV_B, i)) for i in range(DC // 128)],
        ]
        for a in range(n):
            for idx, grad in chunks[a]:
                g = grad()
                d, m2, v2 = _adam_math(w_refs[a][idx], g, m_refs[a][idx], v_refs[a][idx])
                g_out[a][idx] = g
                d_out[a][idx] = d
                m_out[a][idx] = m2
                v_out[a][idx] = v2

    shapes = [SDS(w.shape, f32) for w in ws]
    res = _pallas_call(body, name="adamw_small", out_shape=shapes * 4,
                       in_specs=[VMEM_SPEC, VMEM_SPEC, SMEM_SPEC] + [VMEM_SPEC] * (3 * n),
                       out_specs=[VMEM_SPEC] * (4 * n))(red, g_c_ctx, jvec, *ws, *ms, *vs)
    return [list(res[k * n:(k + 1) * n]) for k in range(4)]


def _rows128(a):
    return a.reshape(-1, 128)


def kernel(x, c, ctx, c_ctx, w_ada, b_ada, norm_g, w_in, q_norm_g, k_norm_g, rpb, conv_w, conv_b, w_out, loss_target, m_c_ctx, m_w_ada, m_b_ada, m_norm_g, m_w_in, m_q_norm_g, m_k_norm_g, m_rpb, m_conv_w, m_conv_b, m_w_out, v_c_ctx, v_w_ada, v_b_ada, v_norm_g, v_w_in, v_q_norm_g, v_k_norm_g, v_rpb, v_conv_w, v_conv_b, v_w_out):
    xi, yi, ci = lax.axis_index("x"), lax.axis_index("y"), lax.axis_index("c")
    dev = 4 * xi + 2 * yi + ci
    chip = 2 * xi + yi
    cvec = jnp.reshape(ci, (1,)).astype(i32)
    jvec = jnp.reshape(chip, (1,)).astype(i32)
    w_ada_s = w_ada[0]
    ncol = w_ada_s.shape[1]

    gc = _split_start([_to_slot(c.reshape(8, 128), 8, dev)], [], 7, _gather8_copies, "gather_c_start")
    wo4c = _cast_to_slot(w_out[0], jvec, "cast_w_out", after=gc[3])
    w4c = _cast_to_slot(w_in[0], jvec, "cast_w_in", after=wo4c)
    (c8,), _ = _split_wait(gc[0], gc[1], [gc[2]], [], w4c, _gather8_copies, "gather_c_wait")
    cc = jnp.concatenate([c8.reshape(8, D), c_ctx.reshape(1, D), jnp.zeros((7, D), f32)], axis=0)
    m_shard, sc16 = _adaln_shard(cc, w_ada_s)

    conv_w_pad = jnp.pad(conv_w[0], ((0, 5), (0, 0)))
    gm = _split_start([_to_slot(m_shard, 4, chip), _to_slot(conv_w_pad, 4, chip)], [], 6, _gather4_copies,
                      "gather_mod_start")

    all_k = [(0, 0, 0), (0, 0, 1), (0, 0, 2)]
    sem_a, rem_a, w4s, token = _split_start([w4c], [], 1, _near_copies, "weights_near_start", after=gm[4])
    (m4, cw4), _ = _split_wait(gm[0], gm[1], [gm[2], gm[3]], [], token, _gather4_copies, "gather_mod_wait")
    m_full = jnp.transpose(m4, (1, 0, 2)).reshape(16, 4 * ncol)
    mrow = lax.dynamic_slice(m_full, (dev, 0), (1, 3 * D))
    mrow_c = m_full[8:9]
    conv_w_full = jnp.transpose(cw4[:, 0:3, :], (1, 0, 2)).reshape(3, DC)
    waves = {}

    def near(after):
        (w4w,), _ = _split_wait(sem_a, rem_a, [w4s], [], after, _near_copies, "weights_near_wait")
        sem_b, rem_b, w4b, started = _split_start([w4w], [], 2, _pass_relay_copies, "weights_pass_start")
        waves["pass"] = (sem_b, rem_b)
        return w4b, started

    def near2(w4, after):
        (w4w,), _ = _split_wait(*waves["pass"], [w4], [], after, _pass_copy, "weights_pass_wait")
        return w4w

    def far(w4, after):
        (w4w,), _ = _split_wait(*waves["pass"], [w4], [], after, _relay_copy, "weights_far_wait")
        w4x, sem_c, rem_c, wo4s, started = _forward_then_start(w4w, wo4c, all_k, "weights_far_forward_out_start")
        (w4f,), _ = _split_wait(sem_c, rem_c, [w4x], [], started, _diag_forward_copy, "weights_far_forward_wait")
        waves["out"] = (sem_c, rem_c, wo4s)
        return w4f, started

    def w_out_arrived(after):
        sem_c, rem_c, wo4s = waves["out"]
        (wow,) = _halves_wait(sem_c, rem_c, [wo4s], after, all_k, "weights_out_wait")
        sem_d, rem_d, wof, started = _split_start([wow], [], 3, _forward_copies, "weights_out_forward_start")
        waves["out_forward"] = (sem_d, rem_d, wof)
        return started

    def w_out_gathered(after):
        sem_d, rem_d, wof = waves["out_forward"]
        (wo,), _ = _split_wait(sem_d, rem_d, [wof], [], after, _forward_copies, "weights_out_forward_wait")
        return wo.reshape(D, D)

    weights = dict(own=w_in[0], jvec=jvec, started=token, first=jvec ^ (1 + cvec), second=jvec ^ (2 - cvec),
                   near=near, near2=near2, far=far, out_arrived=w_out_arrived, out=w_out_gathered)

    exchange = _exchange_copies
    pending = {}

    def on_g_w_out(g_w_out):
        out = _split_start([g_w_out], [SDS((4, D // 8, D), f32)], 4, exchange, "grad_out_pair_start")
        pending["ex_out"] = out
        return out[4]

    def after_conv(dp8):
        ssem_o, rsem_o, g_o, land_o, _ = pending["ex_out"]
        (g_o,), (ex_o,) = _split_wait(ssem_o, rsem_o, [g_o], [land_o], dp8, exchange, "grad_out_pair_wait")
        to32, tob = _pair_sum_w_out(g_o, ex_o, cvec)
        out = _split_start([tob], [SDS((3, D // 8, D), bf16)], 3, _scatter_copies, "grad_out_chip_start")
        pending["sc_out"] = (out, to32)
        return out[4]

    def on_g_w_in_first(g_first):
        out = _split_start([g_first], [SDS((4, D // 2, D), f32)], 4, _block_exchange_copies, "grad_pair_start")
        pending["ex"] = (out[0], out[1], [out[2]], [out[3]])
        return out[4]

    def on_g_w_in_second(g_second):
        ex_ssem, ex_rsem, ex_srcs, ex_lands = pending["ex"]
        _, ex = _split_wait(ex_ssem, ex_rsem, ex_srcs, ex_lands, g_second, _block_exchange_copies, "grad_pair_wait")
        t32, tb = _pair_sum_w_in(g_second, ex[0])
        out = _split_start([tb], [SDS((3, D // 2, D), bf16)], 3, _scatter_copies, "grad_chip_start")
        pending["sc_in"] = (out, t32)
        return out[4]

    r = _local_step(x[0], ctx[0], loss_target[0], mrow, mrow_c, b_ada, norm_g, weights, q_norm_g, k_norm_g,
                    rpb[0], conv_w_full, conv_b,
                    dict(g_w_out=on_g_w_out, after_conv=after_conv, first_half=1 - cvec,
                         g_w_in_first=on_g_w_in_first, g_w_in_second=on_g_w_in_second))
    dm = jnp.concatenate([r["dshift"], r["dscale"], r["dgate"]], axis=1)
    dmc = jnp.concatenate([r["dshift_c"], r["dscale_c"], jnp.zeros((1, D), f32)], axis=1)
    pack_parts = [_rows128(dm), _rows128(dmc), _rows128(r["dng"]), _rows128(r["dng_c"]), _rows128(r["g_gq"]),
                  _rows128(r["g_gk"]), _rows128(r["g_gk_c"]), r["g_rpb"].reshape(H * N_DR, 128),
                  _rows128(r["g_conv_b"]), _rows128(r["g_conv_w"][0:3]), jnp.pad(r["loss_sum"], ((0, 0), (0, 127)))]
    pack = jnp.concatenate([jnp.pad(p, ((0, -p.shape[0] % 8), (0, 0))) for p in pack_parts], axis=0)
    assert pack.shape[0] == PK_ROWS
    gs = _split_start([_to_slot(pack, 8, dev)], [], 7, _gather8_copies, "gather_small_start")
    sc_in, t32 = pending["sc_in"]
    _, (r2,) = _split_wait(sc_in[0], sc_in[1], [sc_in[2]], [sc_in[3]], gs[3], _scatter_copies, "grad_chip_wait")
    u_in = _chip_sum(t32, r2, jvec, "chip_sum_w_in")
    sc_o, to32 = pending["sc_out"]
    _, (ro2,) = _split_wait(sc_o[0], sc_o[1], [sc_o[2]], [sc_o[3]], u_in, _scatter_copies, "grad_out_chip_wait")
    u_out = _chip_sum(to32, ro2, jvec, "chip_sum_w_out")
    (gathered,), _ = _split_wait(gs[0], gs[1], [gs[2]], [], u_out, _gather8_copies, "gather_small_wait")
    red, dm16 = _small_reduce(gathered)
    loss = red[RS_LOSS, 0] * (0.5 / D)

    g_w_ada_s, cpart = _w_ada_grad(sc16, dm16, w_ada_s, jvec)
    cparts4, (o_in, o_out) = _chip_gather_pair_swap(cpart, [u_in, u_out], "gather_c_ctx_parts_pair_swap")
    g_c_ctx = _c_ctx_grad(cparts4, c_ctx)
    d_w_ada, nm_w_ada, nv_w_ada = _adamw(w_ada_s, g_w_ada_s, m_w_ada[0], v_w_ada[0], "adamw_w_ada")

    t_rpb = lambda a: jnp.transpose(a, (0, 2, 1, 3)).reshape(N_DR, H, N_DC)
    t_cw = lambda a: jnp.transpose(a, (1, 0, 2))
    small = _adamw_small(
        red, g_c_ctx, jvec,
        [c_ctx, b_ada, norm_g, q_norm_g, k_norm_g, t_rpb(rpb), t_cw(conv_w), conv_b],
        [m_c_ctx, m_b_ada, m_norm_g, m_q_norm_g, m_k_norm_g, t_rpb(m_rpb), t_cw(m_conv_w), m_conv_b],
        [v_c_ctx, v_b_ada, v_norm_g, v_q_norm_g, v_k_norm_g, t_rpb(v_rpb), t_cw(v_conv_w), v_conv_b])
    for kind in small:
        kind[5] = jnp.transpose(kind[5].reshape(1, N_DR, H, N_DC), (0, 2, 1, 3))
        kind[6] = jnp.transpose(kind[6], (1, 0, 2))

    g_w_in_s, d_w_in, nm_w_in, nv_w_in = _adamw_halves(w_in[0], u_in, o_in, m_w_in[0], v_w_in[0], cvec, "adamw_w_in")
    g_w_out_s, d_w_out, nm_w_out, nv_w_out = _adamw_halves(w_out[0], u_out, o_out, m_w_out[0], v_w_out[0], cvec,
                                                           "adamw_w_out")

    def ordered(kind, big_w_ada, big_w_in, big_w_out):
        s_c_ctx, s_b_ada, s_norm_g, s_q, s_k, s_rpb, s_conv_w, s_conv_b = small[kind]
        return [s_c_ctx, big_w_ada[None], s_b_ada, s_norm_g, big_w_in[None], s_q, s_k, s_rpb, s_conv_w,
                s_conv_b, big_w_out[None]]

    grads = ordered(0, g_w_ada_s, g_w_in_s, g_w_out_s)
    deltas = ordered(1, d_w_ada, d_w_in, d_w_out)
    new_m = ordered(2, nm_w_ada, nm_w_in, nm_w_out)
    new_v = ordered(3, nv_w_ada, nv_w_in, nv_w_out)
    return (loss, r["grad_x"][None], *grads, *deltas, *new_m, *new_v)
```

```python
import functools

import jax
import jax.numpy as jnp
import numpy as np
from jax import lax
from jax.experimental import pallas as pl
from jax.experimental.pallas import tpu as pltpu

f32, bf16, i32 = jnp.float32, jnp.bfloat16, jnp.int32
MESH = pl.DeviceIdType.MESH
HIGHEST = lax.Precision.HIGHEST

D = 1024
S = 2048
L = 256
GW = 64
ROWS = S // GW
H = 8
DH = 64
DA = H * DH
DC = 512
WIN_H, WIN_W = 8, 16
N_DR, N_DC = 2 * WIN_H - 1, 2 * WIN_W - 1
RMS_EPS = 1e-6
ROPE_THETA = 10000.0
QK_SCALE = DH ** -0.5
NEG = -1e30

QB = 128
NQB = S // QB
KR = 9
KB = KR * GW
TILE_GEOM = ((0, 0), (2, 0), (4, 0), (28, 23), (30, 23))
NT = len(TILE_GEOM)

ADAM_LR, ADAM_B1, ADAM_B2, ADAM_EPS, ADAM_WD, ADAM_STEP = 0.001, 0.9, 0.999, 1e-08, 0.01, 10

VMEM_SPEC = pl.BlockSpec(memory_space=pltpu.VMEM)
ANY_SPEC = pl.BlockSpec(memory_space=pl.ANY)
SMEM_SPEC = pl.BlockSpec(memory_space=pltpu.SMEM)
SDS = jax.ShapeDtypeStruct


_pallas_call = pl.pallas_call


def _hbm_call(body, *, out_shape, in_specs=None, out_specs=None, grid_spec=None, **kw):
    n_pre = 0
    if grid_spec is not None:
        ispecs, ospecs, n_pre = grid_spec.in_specs, grid_spec.out_specs, grid_spec.num_scalar_prefetch
        kw["grid_spec"] = grid_spec
    else:
        ispecs, ospecs = in_specs, out_specs
        kw.update(in_specs=in_specs, out_specs=out_specs)

    def blocked(spec):
        return isinstance(spec, pl.BlockSpec) and spec.block_shape is not None

    single = not isinstance(out_shape, (tuple, list))
    shapes = [out_shape] if single else list(out_shape)
    ospec_list = list(ospecs) if isinstance(ospecs, (tuple, list)) else [ospecs]
    shapes = [pltpu.HBM(s.shape, s.dtype) if blocked(sp) else s for s, sp in zip(shapes, ospec_list)]
    call = _pallas_call(body, out_shape=shapes[0] if single else tuple(shapes), **kw)

    def run(*args):
        arrays = [pltpu.with_memory_space_constraint(a, pltpu.HBM) if blocked(sp) else a
                  for a, sp in zip(args[n_pre:], ispecs)]
        return call(*args[:n_pre], *arrays)

    return run


def _cp(vmem_mb=None, **kw):
    if vmem_mb is not None:
        kw["vmem_limit_bytes"] = vmem_mb << 20
    return pltpu.CompilerParams(**kw)


def _silu(z):
    return z * jax.nn.sigmoid(z)


def _dsilu(z):
    sg = jax.nn.sigmoid(z)
    return sg * (1.0 + z * (1.0 - sg))


def _row_start(i):
    return min(max(i - WIN_H // 2, 0), ROWS - WIN_H)


def _my_pos():
    return lax.axis_index("x"), lax.axis_index("y"), lax.axis_index("c")


def _flip(v, bit):
    return 1 - v if bit else v


def _pair_swap(halves, name, after=None):
    nh = len(halves)

    def body(*refs):
        h_in, h_out = refs[:nh], refs[nh + 1:2 * nh + 1]
        ssem, rsem = refs[2 * nh + 1:]
        x, y, c = _my_pos()
        cps = [_Copy(h_in[a], h_out[a], h_out[a], ssem.at[a], rsem.at[a], (x, y, 1 - c)) for a in range(nh)]
        for cp in cps:
            cp.start()
        for cp in cps:
            cp.wait_recv()
        for cp in cps:
            cp.wait_send()

    return _hbm_call(
        body, name=name, out_shape=[SDS(h.shape, h.dtype) for h in halves], in_specs=[ANY_SPEC] * (nh + 1),
        out_specs=[ANY_SPEC] * nh, scratch_shapes=[pltpu.SemaphoreType.DMA((nh,)), pltpu.SemaphoreType.DMA((nh,))],
    )(*halves, halves[0] if after is None else after)


HBM_SPEC = pl.BlockSpec(memory_space=pltpu.HBM)
SEM_SPEC = pl.BlockSpec(memory_space=pltpu.SEMAPHORE)
DATAFLOW = pltpu.SideEffectType.DATAFLOW_SIDE_EFFECTING


def _peer_chips(x, y, c):
    out = []
    for k in range(1, 4):
        px, py = _flip(x, (k >> 1) & 1), _flip(y, k & 1)
        out.append(((px, py, c), 2 * px + py))
    return out


def _half_copies(srcs, dsts, ssem, rsem, which):
    x, y, c = _my_pos()
    j = 2 * x + y
    peers = _peer_chips(x, y, c)
    pairs = []
    for pos, group, k in which:
        half = srcs[pos].shape[1] // 2
        mine = pl.ds(pl.multiple_of(c * half, 8), half)
        dev, pj = peers[k]
        sem = 3 * group + k
        send = pltpu.make_async_remote_copy(src_ref=srcs[pos].at[j, mine], dst_ref=dsts[pos].at[j, mine],
                                            send_sem=ssem.at[sem], recv_sem=rsem.at[sem], device_id=dev,
                                            device_id_type=MESH)
        arrive = pltpu.make_async_remote_copy(src_ref=srcs[pos].at[j, mine], dst_ref=dsts[pos].at[pj, mine],
                                              send_sem=ssem.at[sem], recv_sem=rsem.at[sem], device_id=dev,
                                              device_id_type=MESH)
        pairs.append((send, arrive))
    return pairs


def _halves_wait(ssem, rsem, bigs, after, which, name):
    nb = len(bigs)

    def body(*refs):
        b_in = refs[:nb]
        ssem_ref, rsem_ref = refs[nb], refs[nb + 1]
        for send, arrive in _half_copies(b_in, b_in, ssem_ref, rsem_ref, which):
            send.wait_send()
            arrive.wait_recv()

    return _hbm_call(
        body, name=name, out_shape=tuple(pltpu.HBM(b.shape, b.dtype) for b in bigs),
        in_specs=[HBM_SPEC] * nb + [SEM_SPEC, SEM_SPEC, ANY_SPEC], out_specs=tuple([HBM_SPEC] * nb),
        input_output_aliases={a: a for a in range(nb)}, compiler_params=_cp(has_side_effects=DATAFLOW),
    )(*bigs, ssem, rsem, after)


FORWARD_SEM = 3


def _diag_forward_copy(srcs, dsts, ssem, rsem):
    x, y, c = _my_pos()
    half = srcs[0].shape[1] // 2
    diag = 3 - (2 * x + y)
    mine = pl.ds(pl.multiple_of(c * half, 8), half)
    other = pl.ds(pl.multiple_of((1 - c) * half, 8), half)
    return [_Copy(srcs[0].at[diag, mine], dsts[0].at[diag, mine], dsts[0].at[diag, other], ssem.at[FORWARD_SEM],
                  rsem.at[FORWARD_SEM], (x, y, 1 - c))]


def _forward_then_start(fwd, big, order, name):
    def body(f_in, b_in, f_out, ssem, rsem, b_out, token):
        _diag_forward_copy([f_in], [f_out], ssem, rsem)[0].start()
        for send, _ in _half_copies([b_in], [b_out], ssem, rsem, order):
            send.start()
        token[...] = jnp.zeros_like(token)

    n_sem = FORWARD_SEM + 1
    out_shape = (pltpu.HBM(fwd.shape, fwd.dtype), pltpu.SemaphoreType.DMA((n_sem,)), pltpu.SemaphoreType.DMA((n_sem,)),
                 pltpu.HBM(big.shape, big.dtype), SDS((8, 128), f32))
    return _hbm_call(
        body, name=name, out_shape=out_shape, in_specs=[HBM_SPEC, HBM_SPEC],
        out_specs=(HBM_SPEC, SEM_SPEC, SEM_SPEC, HBM_SPEC, VMEM_SPEC), input_output_aliases={0: 0, 1: 3},
        compiler_params=_cp(has_side_effects=DATAFLOW),
    )(*[pltpu.with_memory_space_constraint(b, pltpu.HBM) for b in (fwd, big)])


def _cast_to_slot(w, jvec, name, after=None):
    rows, cols = w.shape
    tr = 256

    def body(j_ref, w_ref, after_ref, o_ref):
        o_ref[...] = w_ref[...].astype(bf16)

    grid_spec = pltpu.PrefetchScalarGridSpec(
        num_scalar_prefetch=1, grid=(rows // tr,),
        in_specs=[pl.BlockSpec((tr, cols), lambda i, j: (i, 0)), ANY_SPEC],
        out_specs=pl.BlockSpec((None, tr, cols), lambda i, j: (j[0], i, 0)))
    return _hbm_call(body, name=name, out_shape=SDS((4, rows, cols), bf16),
                     grid_spec=grid_spec)(jvec, w, jvec if after is None else after)


def _exchange_copies(srcs, lands, ssem, rsem):
    x, y, c = _my_pos()
    half = srcs[0].shape[0] // 8
    cps = []
    for jb in range(4):
        src = srcs[0].at[pl.ds(pl.multiple_of((2 * jb + 1 - c) * half, 8), half)]
        cps.append(pltpu.make_async_remote_copy(src_ref=src, dst_ref=lands[0].at[jb], send_sem=ssem.at[jb],
                                                recv_sem=rsem.at[jb], device_id=(x, y, 1 - c), device_id_type=MESH))
    return cps


def _block_exchange_copies(srcs, lands, ssem, rsem):
    x, y, c = _my_pos()
    return [pltpu.make_async_remote_copy(src_ref=srcs[0].at[jb], dst_ref=lands[0].at[jb], send_sem=ssem.at[jb],
                                         recv_sem=rsem.at[jb], device_id=(x, y, 1 - c), device_id_type=MESH)
            for jb in range(4)]


def _scatter_copies(srcs, lands, ssem, rsem):
    x, y, c = _my_pos()
    cps = []
    for a in range(len(srcs)):
        for k, (dev, pj) in enumerate(_peer_chips(x, y, c)):
            cps.append(pltpu.make_async_remote_copy(src_ref=srcs[a].at[pj], dst_ref=lands[a].at[k],
                                                    send_sem=ssem.at[3 * a + k], recv_sem=rsem.at[3 * a + k],
                                                    device_id=dev, device_id_type=MESH))
    return cps


class _Copy:
    def __init__(self, src, dst, arrive, ssem, rsem, dev):
        make = lambda to: pltpu.make_async_remote_copy(src_ref=src, dst_ref=to, send_sem=ssem, recv_sem=rsem,
                                                       device_id=dev, device_id_type=MESH)
        send, arrival = make(dst), make(arrive)
        self.start, self.wait_send, self.wait_recv = send.start, send.wait_send, arrival.wait_recv


def _toward(x, y, along_x):
    return x + along_x * (1 - 2 * x), y + (1 - along_x) * (1 - 2 * y)


def _near_copies(srcs, dsts, ssem, rsem):
    x, y, c = _my_pos()
    px, py = _toward(x, y, c)
    j = 2 * x + y
    return [_Copy(srcs[0].at[j], dsts[0].at[j], dsts[0].at[2 * px + py], ssem.at[0], rsem.at[0], (px, py, c))]


def _pass_copy(srcs, dsts, ssem, rsem):
    x, y, c = _my_pos()
    px, py = _toward(x, y, c)
    qx, qy = _toward(x, y, 1 - c)
    got = 2 * px + py
    return [_Copy(srcs[0].at[got], dsts[0].at[got], dsts[0].at[2 * qx + qy], ssem.at[0], rsem.at[0], (x, y, 1 - c))]


def _relay_copy(srcs, dsts, ssem, rsem):
    x, y, c = _my_pos()
    px, py = _toward(x, y, c)
    qx, qy = _toward(x, y, 1 - c)
    half = srcs[0].shape[1] // 2
    mine = pl.ds(pl.multiple_of(c * half, 8), half)
    got, diag = 2 * px + py, 3 - (2 * x + y)
    return [_Copy(srcs[0].at[got, mine], dsts[0].at[got, mine], dsts[0].at[diag, mine], ssem.at[1], rsem.at[1],
                  (qx, qy, c))]


def _forward_copies(srcs, dsts, ssem, rsem):
    x, y, c = _my_pos()
    half = srcs[0].shape[1] // 2
    mine = pl.ds(pl.multiple_of(c * half, 8), half)
    other = pl.ds(pl.multiple_of((1 - c) * half, 8), half)
    return [_Copy(srcs[0].at[pj, mine], dsts[0].at[pj, mine], dsts[0].at[pj, other], ssem.at[k], rsem.at[k],
                  (x, y, 1 - c)) for k, (_, pj) in enumerate(_peer_chips(x, y, c))]


def _pass_relay_copies(srcs, dsts, ssem, rsem):
    return _pass_copy(srcs, dsts, ssem, rsem) + _relay_copy(srcs, dsts, ssem, rsem)


def _gather8_copies(srcs, dsts, ssem, rsem):
    x, y, c = _my_pos()
    me = 4 * x + 2 * y + c
    cps = []
    for a in range(len(srcs)):
        for k in range(1, 8):
            tgt = (_flip(x, (k >> 2) & 1), _flip(y, (k >> 1) & 1), _flip(c, k & 1))
            cps.append(_Copy(srcs[a].at[me], dsts[a].at[me], dsts[a].at[4 * tgt[0] + 2 * tgt[1] + tgt[2]],
                             ssem.at[7 * a + k - 1], rsem.at[7 * a + k - 1], tgt))
    return cps


def _gather4_copies(srcs, dsts, ssem, rsem):
    x, y, c = _my_pos()
    j = 2 * x + y
    cps = []
    for a in range(len(srcs)):
        for k, (dev, pj) in enumerate(_peer_chips(x, y, c)):
            cps.append(_Copy(srcs[a].at[j], dsts[a].at[j], dsts[a].at[pj], ssem.at[3 * a + k], rsem.at[3 * a + k], dev))
    return cps


def _to_slot(a, n, i):
    return lax.dynamic_update_slice(jnp.zeros((n,) + a.shape, a.dtype), a[None], (i,) + (0,) * a.ndim)


def _split_start(srcs, land_shapes, n_cp, make, name, after=None):
    ns, nl = len(srcs), len(land_shapes)
    n_in = ns + nl + (after is not None)

    def body(*refs):
        s_in = refs[:ns]
        ssem, rsem = refs[n_in], refs[n_in + 1]
        s_out = refs[n_in + 2:n_in + 2 + ns]
        l_out = refs[n_in + 2 + ns:n_in + 2 + ns + nl]
        token = refs[n_in + 2 + ns + nl]
        for cp in make(s_in, l_out if nl else s_out, ssem, rsem):
            cp.start()
        token[...] = jnp.zeros_like(token)

    lands = [pltpu.with_memory_space_constraint(lax.empty(sh.shape, sh.dtype), pltpu.HBM) for sh in land_shapes]
    out_shape = (pltpu.SemaphoreType.DMA((n_cp,)), pltpu.SemaphoreType.DMA((n_cp,)),
                 *[pltpu.HBM(b.shape, b.dtype) for b in srcs], *[pltpu.HBM(b.shape, b.dtype) for b in land_shapes],
                 SDS((8, 128), f32))
    return _hbm_call(
        body, name=name, out_shape=out_shape, in_specs=[HBM_SPEC] * (ns + nl) + [ANY_SPEC] * (after is not None),
        out_specs=(SEM_SPEC, SEM_SPEC, *[HBM_SPEC] * (ns + nl), VMEM_SPEC),
        input_output_aliases={i: 2 + i for i in range(ns + nl)}, compiler_params=_cp(has_side_effects=DATAFLOW),
    )(*[pltpu.with_memory_space_constraint(b, pltpu.HBM) for b in srcs], *lands, *([] if after is None else [after]))


def _split_wait(ssem, rsem, srcs, lands, after, make, name):
    ns, nl = len(srcs), len(lands)

    def body(*refs):
        s_in, l_in = refs[:ns], refs[ns:ns + nl]
        ssem_ref, rsem_ref = refs[ns + nl], refs[ns + nl + 1]
        for cp in make(s_in, l_in if nl else s_in, ssem_ref, rsem_ref):
            cp.wait_send()
            cp.wait_recv()

    outs = _hbm_call(
        body, name=name, out_shape=tuple(pltpu.HBM(b.shape, b.dtype) for b in (*srcs, *lands)),
        in_specs=[HBM_SPEC] * (ns + nl) + [SEM_SPEC, SEM_SPEC, ANY_SPEC], out_specs=tuple([HBM_SPEC] * (ns + nl)),
        input_output_aliases={i: i for i in range(ns + nl)}, compiler_params=_cp(has_side_effects=DATAFLOW),
    )(*srcs, *lands, ssem, rsem, after)
    return list(outs[:ns]), list(outs[ns:])


def _adaln_shard(cc, w_ada_shard):
    def body(c_ref, w_ref, m_ref, sc_ref):
        sc = _silu(c_ref[...])
        sc_ref[...] = sc
        m_ref[...] = jnp.dot(sc, w_ref[...], precision=HIGHEST, preferred_element_type=f32)

    return _hbm_call(
        body, name="adaln_shard", out_shape=(SDS((16, w_ada_shard.shape[1]), f32), SDS((16, D), f32)),
        in_specs=[VMEM_SPEC, VMEM_SPEC], out_specs=(VMEM_SPEC, VMEM_SPEC), compiler_params=_cp(32),
    )(cc, w_ada_shard)


def _prenorm(xx, norm_g, mrow, b_ada, tm, name, after=None):
    n = xx.shape[0]

    def body(x_ref, g_ref, m_ref, b_ref, after_ref, h_ref):
        x = x_ref[...]
        shift = m_ref[:, 0:D] + b_ref[:, 0:D]
        scale = m_ref[:, D:2 * D] + b_ref[:, D:2 * D]
        r = lax.rsqrt(jnp.mean(x * x, axis=-1, keepdims=True) + RMS_EPS)
        y = (x * r) * g_ref[...]
        h_ref[...] = (y * (1.0 + scale) + shift).astype(bf16)

    row = lambda i: (i, 0)
    fixed = lambda i: (0, 0)
    return _hbm_call(
        body, name=name, out_shape=SDS((n, D), bf16), grid=(n // tm,),
        in_specs=[pl.BlockSpec((tm, D), row), pl.BlockSpec((1, D), fixed), pl.BlockSpec((1, 3 * D), fixed),
                  pl.BlockSpec((1, 3 * D), fixed), ANY_SPEC],
        out_specs=pl.BlockSpec((tm, D), row),
    )(xx, norm_g, mrow, b_ada, b_ada if after is None else after)


def _in_proj_own(h, w_own, jvec):
    tm = 512

    def body(j_ref, h_ref, w_ref, p_ref):
        p_ref[...] = jnp.dot(h_ref[...], w_ref[...].astype(bf16), preferred_element_type=f32)

    grid_spec = pltpu.PrefetchScalarGridSpec(
        num_scalar_prefetch=1, grid=(S // tm,),
        in_specs=[pl.BlockSpec((tm, D), lambda i, j: (i, 0)), pl.BlockSpec((D, D), lambda i, j: (0, 0))],
        out_specs=pl.BlockSpec((tm, D), lambda i, j: (i, j[0])))
    return _hbm_call(body, name="in_proj_own", out_shape=SDS((S, 4 * D), f32), grid_spec=grid_spec,
                     compiler_params=_cp(40))(jvec, h, w_own)


def _in_proj_block(h, w4, p, bvec, name, after=None):
    tm = 512

    def body(b_ref, h_ref, w_ref, p_in_ref, after_ref, p_ref):
        p_ref[...] = jnp.dot(h_ref[...], w_ref[...], preferred_element_type=f32)

    grid_spec = pltpu.PrefetchScalarGridSpec(
        num_scalar_prefetch=1, grid=(S // tm,),
        in_specs=[pl.BlockSpec((tm, D), lambda i, b: (i, 0)), pl.BlockSpec((None, D, D), lambda i, b: (b[0], 0, 0)),
                  ANY_SPEC, ANY_SPEC],
        out_specs=pl.BlockSpec((tm, D), lambda i, b: (i, b[0])))
    return _hbm_call(body, name=name, out_shape=SDS((S, 4 * D), f32), grid_spec=grid_spec,
                     input_output_aliases={3: 0})(bvec, h, w4, p, bvec if after is None else after)


def _ctx_proj(hc, w4):
    def body(h_ref, w0_ref, w1_ref, p_ref):
        hv = h_ref[...]
        p_ref[:, 0:DA] = jnp.dot(hv, w0_ref[:, DA:2 * DA], preferred_element_type=f32)
        p_ref[:, DA:2 * DA] = jnp.dot(hv, w1_ref[:, 0:DA], preferred_element_type=f32)

    return _hbm_call(
        body, name="ctx_proj", out_shape=SDS((L, 2 * DA), f32), grid=(1,),
        in_specs=[pl.BlockSpec((L, D), lambda i: (0, 0)), pl.BlockSpec((None, D, D), lambda i: (0, 0, 0)),
                  pl.BlockSpec((None, D, D), lambda i: (1, 0, 0))],
        out_specs=pl.BlockSpec((L, 2 * DA), lambda i: (0, 0)),
    )(hc, w4, w4)


def _head_ones():
    r = lax.broadcasted_iota(i32, (DA, DA), 0) // DH
    c = lax.broadcasted_iota(i32, (DA, DA), 1) // DH
    return (r == c).astype(bf16)


def _head_sum(v, ones_bd):
    hi = v.astype(bf16)
    lo = (v - hi.astype(f32)).astype(bf16)
    return jnp.dot(hi, ones_bd, preferred_element_type=f32) + jnp.dot(lo, ones_bd, preferred_element_type=f32)


def _swap16(v):
    lane = lax.broadcasted_iota(i32, v.shape, 1)
    return jnp.where((lane & 31) < 16, pltpu.roll(v, DA - 16, 1), pltpu.roll(v, 16, 1))


def _rope_block(ct_ref, rt_ref, tm):
    rows = [jnp.tile(rt_ref[8 * j:8 * j + 8, :], (GW // 8, 1)) for j in range(tm // GW)]
    return jnp.tile(ct_ref[...], (tm // GW, 1)) + jnp.concatenate(rows, axis=0)


def _rope_specs(tm):
    col = pl.BlockSpec((GW, DA), lambda i: (0, 0))
    row = pl.BlockSpec((8 * tm // GW, DA), lambda i: (i, 0))
    return [col, row, col, row]


def _qk_prep(p, gq, gk, rope):
    tm = 256

    def body(qk_ref, v_ref, gq_ref, gk_ref, cc_ref, cr_ref, sc_ref, sr_ref, qr_ref, qp_ref, kr_ref, vh_ref):
        ones_bd = _head_ones()
        cs, sn = _rope_block(cc_ref, cr_ref, tm), _rope_block(sc_ref, sr_ref, tm)
        q = qk_ref[:, 0:DA]
        k = qk_ref[:, DA:2 * DA]
        yq = (q * lax.rsqrt(_head_sum(q * q, ones_bd) * (1.0 / DH) + RMS_EPS)) * gq_ref[...]
        yk = (k * lax.rsqrt(_head_sum(k * k, ones_bd) * (1.0 / DH) + RMS_EPS)) * gk_ref[...]
        qr = (yq * cs + _swap16(yq) * sn) * QK_SCALE
        qp = yq * QK_SCALE
        kr = yk * cs + _swap16(yk) * sn
        vv = v_ref[...]
        for hh in range(H):
            sl = slice(hh * DH, (hh + 1) * DH)
            qr_ref[hh] = qr[:, sl].astype(bf16)
            qp_ref[hh] = qp[:, sl].astype(bf16)
            kr_ref[hh] = kr[:, sl].astype(bf16)
            vh_ref[hh] = vv[:, sl].astype(bf16)

    hm = SDS((H, S, DH), bf16)
    hspec = pl.BlockSpec((H, tm, DH), lambda i: (0, i, 0))
    fixed = lambda i: (0, 0)
    return _hbm_call(
        body, name="qk_prep", out_shape=(hm, hm, hm, hm), grid=(S // tm,),
        in_specs=[pl.BlockSpec((tm, 2 * DA), lambda i: (i, 0)), pl.BlockSpec((tm, DA), lambda i: (i, 2)),
                  pl.BlockSpec((1, DA), fixed), pl.BlockSpec((1, DA), fixed)] + _rope_specs(tm),
        out_specs=(hspec, hspec, hspec, hspec),
    )(p, p, gq, gk, *rope)


def _ctx_prep(pc, gk):
    def body(p_ref, gk_ref, kc_ref, vc_ref):
        ones_bd = _head_ones()
        k = p_ref[:, 0:DA]
        yk = (k * lax.rsqrt(_head_sum(k * k, ones_bd) * (1.0 / DH) + RMS_EPS)) * gk_ref[...]
        vv = p_ref[:, DA:2 * DA]
        for hh in range(H):
            sl = slice(hh * DH, (hh + 1) * DH)
            kc_ref[hh] = yk[:, sl].astype(bf16)
            vc_ref[hh] = vv[:, sl].astype(bf16)

    hm = SDS((H, L, DH), bf16)
    return _hbm_call(
        body, name="ctx_prep", out_shape=(hm, hm), in_specs=[VMEM_SPEC, VMEM_SPEC], out_specs=(VMEM_SPEC, VMEM_SPEC),
    )(pc, gk)


def _tile_pieces():
    out = []
    for (i0, u0) in TILE_GEOM:
        rows = []
        for j in range(2):
            i = i0 + j
            rs = _row_start(i)
            rows.append([(u0 + u - i + WIN_H - 1) if rs <= u0 + u < rs + WIN_H else None for u in range(KR)])
        out.append(rows)
    return out


def _bias_prep(rpb_rev_pad, after=None):
    pieces = _tile_pieces()

    def body(r_ref, after_ref, o_ref):
        rp = r_ref[...]
        xs = jnp.broadcast_to(rp[:, None, :], (N_DR, GW, 128)).reshape(N_DR * GW, 128)
        row = lax.broadcasted_iota(i32, xs.shape, 0)
        lane = lax.broadcasted_iota(i32, xs.shape, 1)
        for b in range(6):
            xs = jnp.where(((row >> b) & 1) == 1, pltpu.roll(xs, 1 << b, 1), xs)
        xs = pltpu.roll(xs, 128 - (WIN_W - 1), 1)
        k = row & (GW - 1)
        c0 = jnp.clip(lane - WIN_W // 2, 0, GW - WIN_W)
        xs = jnp.where((k >= c0) & (k < c0 + WIN_W), xs, NEG)
        neg = jnp.full((GW, GW), NEG, f32)
        for t in range(NT):
            for j in range(2):
                for u in range(KR):
                    dr = pieces[t][j][u]
                    piece = neg if dr is None else xs[dr * GW:(dr + 1) * GW, 0:GW]
                    o_ref[t, u * GW:(u + 1) * GW, j * GW:(j + 1) * GW] = piece

    return _hbm_call(
        body, name="bias_prep", out_shape=SDS((H, NT, KB, QB), f32), grid=(H,),
        in_specs=[pl.BlockSpec((None, N_DR, 128), lambda h: (h, 0, 0)), ANY_SPEC],
        out_specs=pl.BlockSpec((None, NT, KB, QB), lambda h: (h, 0, 0, 0)),
    )(rpb_rev_pad, rpb_rev_pad if after is None else after)


def _bias_tiles(rpb2, after=None):
    return _bias_prep(jnp.pad(rpb2[:, :, ::-1], ((0, 0), (0, 0), (0, 128 - N_DC))), after)


def _block_geom(b):
    qs = b * QB
    ks = min(max(2 * b - 4, 0), ROWS - KR) * GW
    t = b if b < 2 else (b - (NQB - NT) if b > NQB - 3 else 2)
    return qs, ks, t


def _tt(a, b):
    return lax.dot_general(a, b, (((1,), (1,)), ((), ())), preferred_element_type=f32)


def _tn(a, b):
    return lax.dot_general(a, b, (((0,), (0,)), ((), ())), preferred_element_type=f32)


def _softmax_t(s_lat, s_ctx):
    m = jnp.maximum(jnp.max(s_lat, axis=0, keepdims=True), jnp.max(s_ctx, axis=0, keepdims=True))
    e_lat = jnp.exp(s_lat - m)
    e_ctx = jnp.exp(s_ctx - m)
    inv = 1.0 / (jnp.sum(e_lat, axis=0, keepdims=True) + jnp.sum(e_ctx, axis=0, keepdims=True))
    return e_lat * inv, e_ctx * inv


def _staged(n_blocks, stages):
    held = [dict() for _ in stages]
    for step in range(n_blocks + len(stages) - 1):
        for s, fn in enumerate(stages):
            b = step - s
            if 0 <= b < n_blocks:
                held[s][b] = fn(b) if s == 0 else fn(b, held[s - 1].pop(b))


def _attn_fwd(qr, qp, kr, vh, kc, vc, btt):
    def body(qr_ref, qp_ref, kr_ref, v_ref, kc_ref, vc_ref, bt_ref, o_ref):
        kcv, vcv = kc_ref[...], vc_ref[...]

        def scores(b):
            qs, ks, t = _block_geom(b)
            return (_tt(kr_ref[ks:ks + KB, :], qr_ref[qs:qs + QB, :]) + bt_ref[t], _tt(kcv, qp_ref[qs:qs + QB, :]))

        def probs(b, sc):
            p_lat, p_ctx = _softmax_t(*sc)
            return p_lat.astype(bf16), p_ctx.astype(bf16)

        def values(b, p):
            qs, ks, _ = _block_geom(b)
            o_ref[qs:qs + QB, :] = _tn(p[0], v_ref[ks:ks + KB, :]) + _tn(p[1], vcv)

        _staged(NQB, (scores, probs, values))

    sq = pl.BlockSpec((None, S, DH), lambda h: (h, 0, 0))
    sc = pl.BlockSpec((None, L, DH), lambda h: (h, 0, 0))
    return _hbm_call(
        body, name="attn_fwd", out_shape=SDS((H, S, DH), f32), grid=(H,),
        in_specs=[sq, sq, sq, sq, sc, sc, pl.BlockSpec((None, NT, KB, QB), lambda h: (h, 0, 0, 0))],
        out_specs=sq, compiler_params=_cp(48),
    )(qr, qp, kr, vh, kc, vc, btt)


def _shift_rows(v, down):
    n = v.shape[0]
    row = lax.broadcasted_iota(i32, v.shape, 0)
    if down:
        return jnp.where(row == 0, 0.0, pltpu.roll(v, 1, 0))
    return jnp.where(row == n - 1, 0.0, pltpu.roll(v, n - 1, 0))


def _conv_specs():
    col = lambda off: pl.BlockSpec((S, 128), lambda i, off=off: (0, off + i))
    return [col(16), col(20), col(24), col(28), pl.BlockSpec((3, 128), lambda i: (0, i)),
            pl.BlockSpec((1, 128), lambda i: (0, i))]


def _conv_fwd(p, conv_w, conv_b, after=None):
    def body(u_ref, bg_ref, cg_ref, zc_ref, w_ref, b_ref, after_ref, o_ref):
        cu = cg_ref[...] * u_ref[...]
        cv = b_ref[...] + _shift_rows(cu, True) * w_ref[0:1, :]
        cv = cv + cu * w_ref[1:2, :]
        cv = cv + _shift_rows(cu, False) * w_ref[2:3, :]
        o_ref[...] = ((bg_ref[...] * cv) * _silu(zc_ref[...])).astype(bf16)

    return _hbm_call(
        body, name="conv_fwd", out_shape=SDS((S, DC), bf16), grid=(DC // 128,),
        in_specs=_conv_specs() + [ANY_SPEC], out_specs=pl.BlockSpec((S, 128), lambda i: (0, i)),
        compiler_params=_cp(40),
    )(p, p, p, p, conv_w, conv_b, conv_b if after is None else after)


DP_Q, DP_K, DP_V, DP_ZA, DP_U, DP_BG, DP_CG, DP_ZC = range(8)


def _out_proj_loss(o, p, conv_g, w_out, xx, tgt, mrow, b_ada):
    tm = 256

    def body(o_ref, za_ref, c_ref, w_ref, x_ref, t_ref, m_ref, b_ref,
             dy_ref, dconv_ref, dp_ref, do_ref, gwo_ref, dgate_ref, loss_ref):
        k = pl.program_id(0)

        @pl.when(k == 0)
        def _():
            gwo_ref[...] = jnp.zeros_like(gwo_ref)
            dgate_ref[...] = jnp.zeros_like(dgate_ref)
            loss_ref[0, 0] = 0.0

        gate = m_ref[:, 2 * D:3 * D] + b_ref[:, 2 * D:3 * D]
        za = za_ref[...]
        sz = _silu(za)
        om = _merge_heads(o_ref)
        av, cv = (om * sz).astype(bf16), c_ref[...]
        mo = jnp.dot(av, w_ref[0:DA, :], preferred_element_type=f32)
        mo = mo + jnp.dot(cv, w_ref[DA:DA + DC, :], preferred_element_type=f32)
        y = x_ref[...] + gate * mo
        diff = y - t_ref[...]
        loss_ref[0, 0] += jnp.sum(diff * diff)
        dy = diff * (1.0 / D)
        dy_ref[...] = dy
        dgate_ref[...] += jnp.sum(dy * mo, axis=0, keepdims=True)
        dmo = (dy * gate).astype(bf16)
        dmix = _tt(dmo, w_ref[...])
        dattn = dmix[:, 0:DA]
        dconv_ref[...] = dmix[:, DA:DA + DC]
        a = dattn * sz
        for hh in range(H):
            do_ref[hh] = a[:, hh * DH:(hh + 1) * DH].astype(bf16)
        dp_ref[...] = ((dattn * _dsilu(za)) * om).astype(bf16)
        gwo_ref[0:DA, :] += _tn(av, dmo)
        gwo_ref[DA:DA + DC, :] += _tn(cv, dmo)

    row = lambda i: (i, 0)
    fixed = lambda i: (0, 0)
    hspec = pl.BlockSpec((H, tm, DH), lambda i: (0, i, 0))
    return _hbm_call(
        body, name="out_proj_loss",
        out_shape=(SDS((S, D), f32), SDS((S, DC), f32), SDS((8, S, DA), bf16), SDS((H, S, DH), bf16),
                   SDS((D, D), f32), SDS((1, D), f32), SDS((1, 1), f32)),
        grid=(S // tm,),
        in_specs=[hspec, pl.BlockSpec((tm, DA), lambda i: (i, 3)), pl.BlockSpec((tm, DC), row),
                  pl.BlockSpec((D, D), fixed), pl.BlockSpec((tm, D), row), pl.BlockSpec((tm, D), row),
                  pl.BlockSpec((1, 3 * D), fixed), pl.BlockSpec((1, 3 * D), fixed)],
        out_specs=(pl.BlockSpec((tm, D), row), pl.BlockSpec((tm, DC), row),
                   pl.BlockSpec((None, tm, DA), lambda i: (DP_ZA, i, 0)), hspec, pl.BlockSpec((D, D), fixed),
                   pl.BlockSpec((1, D), fixed), SMEM_SPEC),
        compiler_params=_cp(56, dimension_semantics=("arbitrary",)),
    )(o, p, conv_g, w_out, xx, tgt, mrow, b_ada)


def _conv_bwd(dconv, p, conv_w, conv_b, dp8, after=None):
    def body(d_ref, u_ref, bg_ref, cg_ref, zc_ref, w_ref, b_ref, dp_in_ref, after_ref, dp_ref, gw_ref, gb_ref):
        du_ref, dbg_ref, dcg_ref, dzc_ref = dp_ref.at[0], dp_ref.at[1], dp_ref.at[2], dp_ref.at[3]
        dconv = d_ref[...]
        u, bg, cg, zc = u_ref[...], bg_ref[...], cg_ref[...], zc_ref[...]
        w0, w1, w2 = w_ref[0:1, :], w_ref[1:2, :], w_ref[2:3, :]
        cu = cg * u
        cu_m, cu_p = _shift_rows(cu, True), _shift_rows(cu, False)
        cv = b_ref[...] + cu_m * w0
        cv = cv + cu * w1
        cv = cv + cu_p * w2
        sz = _silu(zc)
        dbg_ref[...] = ((dconv * sz) * cv).astype(bf16)
        dzc_ref[...] = ((dconv * (bg * cv)) * _dsilu(zc)).astype(bf16)
        dcv = (dconv * sz) * bg
        gb_ref[...] = jnp.sum(dcv, axis=0, keepdims=True)
        gw_ref[0:1, :] = jnp.sum(dcv * cu_m, axis=0, keepdims=True)
        gw_ref[1:2, :] = jnp.sum(dcv * cu, axis=0, keepdims=True)
        gw_ref[2:3, :] = jnp.sum(dcv * cu_p, axis=0, keepdims=True)
        gw_ref[3:8, :] = jnp.zeros((5, 128), f32)
        dcu = _shift_rows(dcv, False) * w0 + dcv * w1 + _shift_rows(dcv, True) * w2
        dcg_ref[...] = (dcu * u).astype(bf16)
        du_ref[...] = (dcu * cg).astype(bf16)

    return _hbm_call(
        body, name="conv_bwd", out_shape=(SDS((8, S, DC), bf16), SDS((8, DC), f32), SDS((1, DC), f32)),
        grid=(DC // 128,),
        in_specs=[pl.BlockSpec((S, 128), lambda i: (0, i))] + _conv_specs() + [ANY_SPEC, ANY_SPEC],
        out_specs=(pl.BlockSpec((4, S, 128), lambda i: (DP_U // 4, 0, i)), pl.BlockSpec((8, 128), lambda i: (0, i)),
                   pl.BlockSpec((1, 128), lambda i: (0, i))),
        input_output_aliases={7: 0}, compiler_params=_cp(48),
    )(dconv, p, p, p, p, conv_w, conv_b, dp8, conv_b if after is None else after)


def _attn_bwd(qr, qp, kr, vh, kc, vc, btt, do, after=None):
    def body(qr_ref, qp_ref, kr_ref, v_ref, kc_ref, vc_ref, bt_ref, do_ref, after_ref,
             dqr_ref, dqp_ref, dkr_ref, dv_ref, dkc_ref, dvc_ref, dbt_ref):
        kcv, vcv = kc_ref[...], vc_ref[...]
        dkr_ref[...] = jnp.zeros_like(dkr_ref)
        dv_ref[...] = jnp.zeros_like(dv_ref)
        dbt_ref[...] = jnp.zeros_like(dbt_ref)
        ctx_acc = {}

        def products(b):
            qs, ks, t = _block_geom(b)
            dob = do_ref[qs:qs + QB, :]
            s_lat = _tt(kr_ref[ks:ks + KB, :], qr_ref[qs:qs + QB, :]) + bt_ref[t]
            s_ctx = _tt(kcv, qp_ref[qs:qs + QB, :])
            return s_lat, s_ctx, _tt(v_ref[ks:ks + KB, :], dob), _tt(vcv, dob)

        def score_grads(b, x):
            s_lat, s_ctx, dp_lat, dp_ctx = x
            p_lat, p_ctx = _softmax_t(s_lat, s_ctx)
            delta = jnp.sum(p_lat * dp_lat, axis=0, keepdims=True) + jnp.sum(p_ctx * dp_ctx, axis=0, keepdims=True)
            ds_lat = p_lat * (dp_lat - delta)
            ds_ctx = p_ctx * (dp_ctx - delta)
            return ds_lat, ds_lat.astype(bf16), ds_ctx.astype(bf16), p_lat.astype(bf16), p_ctx.astype(bf16)

        def operand_grads(b, y):
            qs, ks, t = _block_geom(b)
            ds_lat, dsb_lat, dsb_ctx, pb_lat, pb_ctx = y
            qrb, qpb, dob = qr_ref[qs:qs + QB, :], qp_ref[qs:qs + QB, :], do_ref[qs:qs + QB, :]
            dbt_ref[t] += ds_lat
            dqr_ref[qs:qs + QB, :] = _tn(dsb_lat, kr_ref[ks:ks + KB, :])
            dqp_ref[qs:qs + QB, :] = _tn(dsb_ctx, kcv)
            dkr_ref[ks:ks + KB, :] += jnp.dot(dsb_lat, qrb, preferred_element_type=f32)
            dv_ref[ks:ks + KB, :] += jnp.dot(pb_lat, dob, preferred_element_type=f32)
            dkc = jnp.dot(dsb_ctx, qpb, preferred_element_type=f32)
            dvc = jnp.dot(pb_ctx, dob, preferred_element_type=f32)
            ctx_acc["k"] = dkc if b == 0 else ctx_acc["k"] + dkc
            ctx_acc["v"] = dvc if b == 0 else ctx_acc["v"] + dvc

        _staged(NQB, (products, score_grads, operand_grads))
        dkc_ref[...] = ctx_acc["k"]
        dvc_ref[...] = ctx_acc["v"]

    sq = pl.BlockSpec((None, S, DH), lambda h: (h, 0, 0))
    sc = pl.BlockSpec((None, L, DH), lambda h: (h, 0, 0))
    sb = pl.BlockSpec((None, NT, KB, QB), lambda h: (h, 0, 0, 0))
    big, ctxs = SDS((H, S, DH), f32), SDS((H, L, DH), f32)
    return _hbm_call(
        body, name="attn_bwd", out_shape=(big, big, big, big, ctxs, ctxs, SDS((H, NT, KB, QB), f32)), grid=(H,),
        in_specs=[sq, sq, sq, sq, sc, sc, sb, sq, ANY_SPEC], out_specs=(sq, sq, sq, sq, sc, sc, sb),
        compiler_params=_cp(56),
    )(qr, qp, kr, vh, kc, vc, btt, do, do if after is None else after)


def _bias_bwd(dbtt, after=None):
    pieces = _tile_pieces()

    def body(d_ref, after_ref, o_ref, scr):
        scr[...] = jnp.zeros_like(scr)
        acc = [None] * N_DR
        for t in range(NT):
            for j in range(2):
                for u in range(KR):
                    dr = pieces[t][j][u]
                    if dr is None:
                        continue
                    piece = d_ref[t, u * GW:(u + 1) * GW, j * GW:(j + 1) * GW]
                    acc[dr] = piece if acc[dr] is None else acc[dr] + piece
        for dr in range(N_DR):
            scr[dr * GW:(dr + 1) * GW, 0:GW] = acc[dr]
        xs = pltpu.roll(scr[...], WIN_W - 1, 1)
        row = lax.broadcasted_iota(i32, xs.shape, 0)
        for b in range(6):
            xs = jnp.where(((row >> b) & 1) == 1, pltpu.roll(xs, 128 - (1 << b), 1), xs)
        rev = jnp.sum(xs.reshape(N_DR, GW, 128), axis=1)
        a = lax.broadcasted_iota(i32, (128, 128), 0)
        b = lax.broadcasted_iota(i32, (128, 128), 1)
        flip = ((a + b == N_DC - 1) & (a < N_DC)).astype(f32)
        o_ref[...] = jnp.dot(rev, flip, precision=HIGHEST, preferred_element_type=f32)

    return _hbm_call(
        body, name="bias_bwd", out_shape=SDS((H, N_DR, 128), f32), grid=(H,),
        in_specs=[pl.BlockSpec((None, NT, KB, QB), lambda h: (h, 0, 0, 0)), ANY_SPEC],
        out_specs=pl.BlockSpec((None, N_DR, 128), lambda h: (h, 0, 0)),
        scratch_shapes=[pltpu.VMEM((N_DR * GW, 128), f32)],
    )(dbtt, dbtt if after is None else after)


def _merge_heads(ref):
    return jnp.concatenate([ref[hh] for hh in range(H)], axis=1)


def _head_norm_bwd(xraw, gain, dy, ones_bd):
    r = lax.rsqrt(_head_sum(xraw * xraw, ones_bd) * (1.0 / DH) + RMS_EPS)
    xh = xraw * r
    gdy = dy * gain
    dx = r * (gdy - xh * (_head_sum(xh * gdy, ones_bd) * (1.0 / DH)))
    return dx, jnp.sum(dy * xh, axis=0, keepdims=True)


def _qk_bwd(dqr, dqp, dkr, dvh, p, gq, gk, rope, dp8):
    tm = 256

    def body(dqr_ref, dqp_ref, dkr_ref, dv_ref, qk_ref, gq_ref, gk_ref, cc_ref, cr_ref, sc_ref, sr_ref, dp_in_ref,
             dp_ref, ggq_ref, ggk_ref):
        dq_ref, dk_ref, dvo_ref = dp_ref.at[DP_Q], dp_ref.at[DP_K], dp_ref.at[DP_V]

        @pl.when(pl.program_id(0) == 0)
        def _():
            ggq_ref[...] = jnp.zeros_like(ggq_ref)
            ggk_ref[...] = jnp.zeros_like(ggk_ref)

        ones_bd = _head_ones()
        cs, sn = _rope_block(cc_ref, cr_ref, tm), _rope_block(sc_ref, sr_ref, tm)
        a = _merge_heads(dqr_ref)
        dyq = ((a * cs - _swap16(a) * sn) + _merge_heads(dqp_ref)) * QK_SCALE
        bk = _merge_heads(dkr_ref)
        dyk = bk * cs - _swap16(bk) * sn
        dq, gq_part = _head_norm_bwd(qk_ref[:, 0:DA], gq_ref[...], dyq, ones_bd)
        dk, gk_part = _head_norm_bwd(qk_ref[:, DA:2 * DA], gk_ref[...], dyk, ones_bd)
        dq_ref[...] = dq.astype(bf16)
        dk_ref[...] = dk.astype(bf16)
        dvo_ref[...] = _merge_heads(dv_ref).astype(bf16)
        ggq_ref[...] += gq_part
        ggk_ref[...] += gk_part

    hspec = pl.BlockSpec((H, tm, DH), lambda i: (0, i, 0))
    fixed = pl.BlockSpec((1, DA), lambda i: (0, 0))
    return _hbm_call(
        body, name="qk_bwd", out_shape=(SDS((8, S, DA), bf16), SDS((1, DA), f32), SDS((1, DA), f32)), grid=(S // tm,),
        in_specs=[hspec, hspec, hspec, hspec, pl.BlockSpec((tm, 2 * DA), lambda i: (i, 0)), fixed, fixed]
        + _rope_specs(tm) + [ANY_SPEC],
        out_specs=(pl.BlockSpec((3, tm, DA), lambda i: (0, i, 0)), fixed, fixed), input_output_aliases={11: 0},
        compiler_params=_cp(40, dimension_semantics=("arbitrary",)),
    )(dqr, dqp, dkr, dvh, p, gq, gk, *rope, dp8)


def _ctx_bwd(dkc, dvc, pc, gk):
    def body(dkc_ref, dvc_ref, p_ref, gk_ref, dk_ref, dv_ref, ggk_ref):
        ones_bd = _head_ones()
        dk, gk_part = _head_norm_bwd(p_ref[:, 0:DA], gk_ref[...], _merge_heads(dkc_ref), ones_bd)
        dk_ref[...] = dk.astype(bf16)
        dv_ref[...] = _merge_heads(dvc_ref).astype(bf16)
        ggk_ref[...] = gk_part

    piece = SDS((L, DA), bf16)
    return _hbm_call(
        body, name="ctx_bwd", out_shape=(piece, piece, SDS((1, DA), f32)), in_specs=[VMEM_SPEC] * 4,
        out_specs=(VMEM_SPEC,) * 3,
    )(dkc, dvc, pc, gk)


def _grad_w_in(h, dp8, hc, dkc_raw, dvc_m, half, name, after=None):
    def body(half_ref, h_ref, p_ref, hc_ref, dk_ref, dv_ref, after_ref, g_ref):
        j = pl.program_id(0)
        hv = h_ref[...]
        g_ref[:, 0:DA] = _tn(hv, p_ref[0])
        g_ref[:, DA:2 * DA] = _tn(hv, p_ref[1])

        @pl.when(j == 0)
        def _():
            g_ref[:, DA:2 * DA] += _tn(hc_ref[...], dk_ref[...])

        @pl.when(j == 1)
        def _():
            g_ref[:, 0:DA] += _tn(hc_ref[...], dv_ref[...])

    fixed = lambda j, s: (0, 0)
    hd = D // 2
    grid_spec = pltpu.PrefetchScalarGridSpec(
        num_scalar_prefetch=1, grid=(4,),
        in_specs=[pl.BlockSpec((S, hd), lambda j, s: (0, s[0])), pl.BlockSpec((2, S, DA), lambda j, s: (j, 0, 0)),
                  pl.BlockSpec((L, hd), lambda j, s: (0, s[0])), pl.BlockSpec((L, DA), fixed),
                  pl.BlockSpec((L, DA), fixed), ANY_SPEC],
        out_specs=pl.BlockSpec((None, hd, D), lambda j, s: (j, 0, 0)))
    return _hbm_call(
        body, name=name, out_shape=SDS((4, hd, D), f32), grid_spec=grid_spec, compiler_params=_cp(40),
    )(half, h, dp8, hc, dkc_raw, dvc_m, half if after is None else after)


def _norm_mod_bwd(x, dh, g, scale):
    r = lax.rsqrt(jnp.mean(x * x, axis=-1, keepdims=True) + RMS_EPS)
    xh = x * r
    y = xh * g
    dshift = jnp.sum(dh, axis=0, keepdims=True)
    dscale = jnp.sum(dh * y, axis=0, keepdims=True)
    dyn = dh * (1.0 + scale)
    dg = jnp.sum(dyn * xh, axis=0, keepdims=True)
    gdy = dyn * g
    dx = r * (gdy - xh * jnp.mean(xh * gdy, axis=-1, keepdims=True))
    return dx, dshift, dscale, dg


def _dh_grad_x(dp8, w4, xx, dy, norm_g, mrow, b_ada, after=None):
    tm = 256

    def body(p_ref, w_ref, x_ref, dy_ref, g_ref, m_ref, b_ref, after_ref, gx_ref, dsh_ref, dsc_ref, dg_ref):
        @pl.when(pl.program_id(0) == 0)
        def _():
            dsh_ref[...] = jnp.zeros_like(dsh_ref)
            dsc_ref[...] = jnp.zeros_like(dsc_ref)
            dg_ref[...] = jnp.zeros_like(dg_ref)

        dh = None
        for j in range(4):
            for half in range(2):
                term = _tt(p_ref[2 * j + half], w_ref[j, :, half * DA:(half + 1) * DA])
                dh = term if dh is None else dh + term
        scale = m_ref[:, D:2 * D] + b_ref[:, D:2 * D]
        dx, dshift, dscale, dg = _norm_mod_bwd(x_ref[...], dh, g_ref[...], scale)
        gx_ref[...] = dy_ref[...] + dx
        dsh_ref[...] += dshift
        dsc_ref[...] += dscale
        dg_ref[...] += dg

    row = lambda i: (i, 0)
    fixed = lambda i: (0, 0)
    vec = SDS((1, D), f32)
    return _hbm_call(
        body, name="dh_grad_x", out_shape=(SDS((S, D), f32), vec, vec, vec), grid=(S // tm,),
        in_specs=[pl.BlockSpec((8, tm, DA), lambda i: (0, i, 0)), pl.BlockSpec((4, D, D), lambda i: (0, 0, 0)),
                  pl.BlockSpec((tm, D), row), pl.BlockSpec((tm, D), row), pl.BlockSpec((1, D), fixed),
                  pl.BlockSpec((1, 3 * D), fixed), pl.BlockSpec((1, 3 * D), fixed), ANY_SPEC],
        out_specs=(pl.BlockSpec((tm, D), row), pl.BlockSpec((1, D), fixed), pl.BlockSpec((1, D), fixed),
                   pl.BlockSpec((1, D), fixed)),
        compiler_params=_cp(56, dimension_semantics=("arbitrary",)),
    )(dp8, w4, xx, dy, norm_g, mrow, b_ada, b_ada if after is None else after)


def _dhc_sums(dkc_raw, dvc_m, w4, ctx2, norm_g, mrow_c, b_ada, after=None):
    def body(dk_ref, dv_ref, w0_ref, w1_ref, x_ref, g_ref, m_ref, b_ref, after_ref, dsh_ref, dsc_ref, dg_ref):
        dh = _tt(dk_ref[...], w0_ref[:, DA:2 * DA]) + _tt(dv_ref[...], w1_ref[:, 0:DA])
        scale = m_ref[:, D:2 * D] + b_ref[:, D:2 * D]
        _, dshift, dscale, dg = _norm_mod_bwd(x_ref[...], dh, g_ref[...], scale)
        dsh_ref[...] = dshift
        dsc_ref[...] = dscale
        dg_ref[...] = dg

    fixed = lambda i: (0, 0)
    vec = SDS((1, D), f32)
    vspec = pl.BlockSpec((1, D), fixed)
    return _hbm_call(
        body, name="dhc_sums", out_shape=(vec, vec, vec), grid=(1,),
        in_specs=[pl.BlockSpec((L, DA), fixed), pl.BlockSpec((L, DA), fixed),
                  pl.BlockSpec((None, D, D), lambda i: (0, 0, 0)), pl.BlockSpec((None, D, D), lambda i: (1, 0, 0)),
                  pl.BlockSpec((L, D), fixed), vspec, pl.BlockSpec((1, 3 * D), fixed), pl.BlockSpec((1, 3 * D), fixed),
                  ANY_SPEC],
        out_specs=(vspec, vspec, vspec), compiler_params=_cp(32),
    )(dkc_raw, dvc_m, w4, w4, ctx2, norm_g, mrow_c, b_ada, b_ada if after is None else after)


def _rope_tables():
    nf = DH // 4
    inv = np.float32(ROPE_THETA) ** (-np.arange(nf, dtype=np.float32) / np.float32(nf))
    ang_c = np.arange(GW, dtype=np.float32)[:, None] * inv
    ang_r = np.arange(ROWS, dtype=np.float32)[:, None] * inv
    zc, zr = np.zeros((GW, 2 * nf), np.float32), np.zeros((ROWS, 2 * nf), np.float32)
    ct_cos = np.tile(np.concatenate([zc, np.cos(ang_c), np.cos(ang_c)], axis=1), (1, H))
    ct_sin = np.tile(np.concatenate([zc, -np.sin(ang_c), np.sin(ang_c)], axis=1), (1, H))
    rt_cos = np.tile(np.concatenate([np.cos(ang_r), np.cos(ang_r), zr], axis=1), (1, H))
    rt_sin = np.tile(np.concatenate([-np.sin(ang_r), np.sin(ang_r), zr], axis=1), (1, H))
    rep8 = lambda t: np.ascontiguousarray(np.broadcast_to(t[:, None, :], (ROWS, 8, DA))).reshape(ROWS * 8, DA)
    return tuple(jnp.asarray(t, f32) for t in (ct_cos, rep8(rt_cos), ct_sin, rep8(rt_sin)))


def _local_step(xx, ctx2, tgt, mrow, mrow_c, b_ada, norm_g, weights, q_norm_g, k_norm_g, rpb2, conv_w_full, conv_b,
                hooks=None):
    hooks = hooks or {}
    gq = jnp.tile(q_norm_g, (1, H))
    gk = jnp.tile(k_norm_g, (1, H))
    rope = _rope_tables()

    h = _prenorm(xx, norm_g, mrow, b_ada, 256, "prenorm_x", after=weights.get("started"))
    hc = _prenorm(ctx2, norm_g, mrow_c, b_ada, L, "prenorm_ctx")
    jv = weights["jvec"]
    p = _in_proj_own(h, weights["own"], jv)
    btb = _bias_tiles(rpb2, after=p)
    w4, started = weights["near"](btb)
    p = _in_proj_block(h, w4, p, weights["first"], "in_proj_near", after=started)
    w4 = weights["near2"](w4, p)
    p = _in_proj_block(h, w4, p, weights["second"], "in_proj_near2")
    w4, started = weights["far"](w4, p)
    p = _in_proj_block(h, w4, p, jv ^ 3, "in_proj_far", after=started)
    pc = _ctx_proj(hc, w4)
    qr, qp, kr, vh = _qk_prep(p, gq, gk, rope)
    kc, vc = _ctx_prep(pc, gk)
    o = _attn_fwd(qr, qp, kr, vh, kc, vc, btb)
    started = weights["out_arrived"](o) if "out_arrived" in weights else None
    conv_g = _conv_fwd(p, conv_w_full, conv_b, after=started)
    w_out_full = weights["out"](conv_g)
    dy, dconv, dp8, do, g_w_out, dgate, loss_sum = _out_proj_loss(o, p, conv_g, w_out_full, xx, tgt, mrow, b_ada)
    started = hooks["g_w_out"](g_w_out) if "g_w_out" in hooks else None
    dp8, g_conv_w, g_conv_b = _conv_bwd(dconv, p, conv_w_full, conv_b, dp8, after=started)
    started = hooks["after_conv"](dp8) if "after_conv" in hooks else None
    dqr, dqp, dkr, dvh, dkc, dvc, dbtb = _attn_bwd(qr, qp, kr, vh, kc, vc, btb, do, after=started)
    dp8, g_gq, g_gk = _qk_bwd(dqr, dqp, dkr, dvh, p, gq, gk, rope, dp8)
    dkc_raw, dvc_m, g_gk_c = _ctx_bwd(dkc, dvc, pc, gk)
    first = hooks.get("first_half", jnp.zeros((1,), i32))
    g_first = _grad_w_in(h, dp8, hc, dkc_raw, dvc_m, first, "grad_w_in_first")
    started = hooks["g_w_in_first"](g_first) if "g_w_in_first" in hooks else None
    g_second = _grad_w_in(h, dp8, hc, dkc_raw, dvc_m, 1 - first, "grad_w_in_second", after=started)
    started = hooks["g_w_in_second"](g_second) if "g_w_in_second" in hooks else None
    dshift_c, dscale_c, dng_c = _dhc_sums(dkc_raw, dvc_m, w4, ctx2, norm_g, mrow_c, b_ada, after=started)
    g_rpb = _bias_bwd(dbtb, after=dshift_c)
    grad_x, dshift, dscale, dng = _dh_grad_x(dp8, w4, xx, dy, norm_g, mrow, b_ada, after=g_rpb)
    return dict(loss_sum=loss_sum, grad_x=grad_x, g_w_in=(g_first, g_second), g_w_out=g_w_out, g_conv_w=g_conv_w,
                g_conv_b=g_conv_b, g_rpb=g_rpb, g_gq=g_gq, g_gk=g_gk, g_gk_c=g_gk_c, dshift=dshift, dscale=dscale,
                dgate=dgate, dng=dng, dshift_c=dshift_c, dscale_c=dscale_c, dng_c=dng_c)


def _pair_sum_w_in(g, r):
    tr = 128

    def body(g_ref, r_ref, t32_ref, tb_ref):
        t = g_ref[...] + r_ref[...]
        t32_ref[...] = t
        tb_ref[...] = t.astype(bf16)

    half = D // 2
    spec = pl.BlockSpec((4, tr, D), lambda i: (0, i, 0))
    return _hbm_call(body, name="pair_sum_w_in", out_shape=(SDS((4, half, D), f32), SDS((4, half, D), bf16)),
                     grid=(half // tr,), in_specs=[spec, spec], out_specs=(spec, spec), compiler_params=_cp(40))(g, r)


def _pair_sum_w_out(g, r, cvec):
    hr = D // 8

    def body(c_ref, g0, g1, g2, g3, r_ref, t32_ref, tb_ref):
        for q, g_ref in enumerate((g0, g1, g2, g3)):
            t = g_ref[...] + r_ref[q]
            t32_ref[q] = t
            tb_ref[q] = t.astype(bf16)

    gspecs = [pl.BlockSpec((hr, D), lambda i, c, q=q: (2 * q + c[0], 0)) for q in range(4)]
    full = pl.BlockSpec((4, hr, D), lambda i, c: (0, 0, 0))
    grid_spec = pltpu.PrefetchScalarGridSpec(num_scalar_prefetch=1, grid=(1,), in_specs=gspecs + [full],
                                             out_specs=(full, full))
    return _hbm_call(body, name="pair_sum_w_out", out_shape=(SDS((4, hr, D), f32), SDS((4, hr, D), bf16)),
                          grid_spec=grid_spec)(cvec, g, g, g, g, r)


def _chip_sum(t32, r2, jvec, name):
    rows = t32.shape[1]
    tr = min(rows, 128)

    def body(j_ref, t_ref, r_ref, u_ref):
        u_ref[...] = ((t_ref[...] + r_ref[0].astype(f32)) + r_ref[1].astype(f32)) + r_ref[2].astype(f32)

    grid_spec = pltpu.PrefetchScalarGridSpec(
        num_scalar_prefetch=1, grid=(rows // tr,),
        in_specs=[pl.BlockSpec((None, tr, D), lambda i, j: (j[0], i, 0)), pl.BlockSpec((3, tr, D), lambda i, j: (0, i, 0))],
        out_specs=pl.BlockSpec((tr, D), lambda i, j: (i, 0)))
    return _hbm_call(body, name=name, out_shape=SDS((rows, D), f32), grid_spec=grid_spec)(jvec, t32, r2)


_PK = {}
_off = 0
for _name, _rows in (("dm", 24), ("dmc", 24), ("dng", 8), ("dng_c", 8), ("gq", 8), ("gk", 8), ("gk_c", 8),
                     ("rpb", H * N_DR), ("conv_b", 8), ("conv_w", 16), ("loss", 8)):
    _PK[_name] = (_off, _off + _rows)
    _off += _rows
PK_ROWS = _off
RS_B_ADA, RS_NORM_G, RS_GQ, RS_GK, RS_RPB, RS_CONV_B, RS_CONV_W, RS_DMC, RS_LOSS, RS_ROWS = (
    0, 24, 32, 40, 48, 168, 176, 192, 216, 224)


def _small_reduce(gathered):
    def body(g_ref, o_ref, dm_ref):
        a0 = _PK["dm"][0]
        dm_ref[...] = jnp.zeros_like(dm_ref)
        for b in range(8):
            for i in range(24):
                dm_ref[b:b + 1, 128 * i:128 * (i + 1)] = g_ref[b, a0 + i:a0 + i + 1, :]
        tot = g_ref[0]
        for b in range(1, 8):
            tot = tot + g_ref[b]

        def rows(name):
            a, z = _PK[name]
            return tot[a:z]

        o_ref[RS_B_ADA:RS_B_ADA + 24] = rows("dm") + rows("dmc")
        o_ref[RS_NORM_G:RS_NORM_G + 8] = rows("dng") + rows("dng_c")
        gq = jnp.broadcast_to(jnp.sum(rows("gq"), axis=0, keepdims=True), (8, 128))
        gk = jnp.broadcast_to(jnp.sum(rows("gk") + rows("gk_c"), axis=0, keepdims=True), (8, 128))
        o_ref[RS_GQ:RS_GQ + 8] = gq + pltpu.roll(gq, DH, 1)
        o_ref[RS_GK:RS_GK + 8] = gk + pltpu.roll(gk, DH, 1)
        o_ref[RS_RPB:RS_RPB + H * N_DR] = rows("rpb")
        o_ref[RS_CONV_B:RS_CONV_B + 8] = rows("conv_b")
        o_ref[RS_CONV_W:RS_CONV_W + 16] = rows("conv_w")
        dmc = rows("dmc")
        o_ref[RS_DMC:RS_DMC + 24] = dmc
        o_ref[RS_LOSS:RS_LOSS + 8] = rows("loss")
        for i in range(24):
            dm_ref[8:9, 128 * i:128 * (i + 1)] = dmc[i:i + 1]

    return _hbm_call(body, name="small_reduce", out_shape=(SDS((RS_ROWS, 128), f32), SDS((16, 3 * D), f32)),
                     in_specs=[VMEM_SPEC], out_specs=(VMEM_SPEC, VMEM_SPEC))(gathered)


def _w_ada_grad(sc16, dm16, w_ada_shard, jvec):
    ncol = w_ada_shard.shape[1]

    def body(j_ref, sc_ref, dm_ref, w_ref, g_ref, part_ref):
        dm = dm_ref[...]
        g_ref[...] = lax.dot_general(sc_ref[...], dm, (((0,), (0,)), ((), ())), precision=HIGHEST,
                                     preferred_element_type=f32)
        part_ref[...] = lax.dot_general(dm[8:16], w_ref[...], (((1,), (1,)), ((), ())), precision=HIGHEST,
                                        preferred_element_type=f32)

    fixed = lambda i, j: (0, 0)
    grid_spec = pltpu.PrefetchScalarGridSpec(
        num_scalar_prefetch=1, grid=(1,),
        in_specs=[pl.BlockSpec((16, D), fixed), pl.BlockSpec((16, ncol), lambda i, j: (0, j[0])),
                  pl.BlockSpec((D, ncol), fixed)],
        out_specs=(pl.BlockSpec((D, ncol), fixed), pl.BlockSpec((8, D), fixed)))
    return _pallas_call(body, name="w_ada_grad", out_shape=(SDS((D, ncol), f32), SDS((8, D), f32)),
                        grid_spec=grid_spec, compiler_params=_cp(40))(jvec, sc16, dm16, w_ada_shard)


def _c_ctx_grad(parts4, c_ctx):
    def body(p_ref, c_ref, o_ref):
        tot = ((p_ref[0] + p_ref[1]) + p_ref[2]) + p_ref[3]
        o_ref[...] = tot[0:1] * _dsilu(c_ref[...].reshape(1, D))

    return _pallas_call(body, name="c_ctx_grad", out_shape=SDS((1, D), f32), in_specs=[VMEM_SPEC, VMEM_SPEC],
                        out_specs=VMEM_SPEC)(parts4, c_ctx)


def _adamw(w, g, m, v, name):
    rows, cols = w.shape
    tr = 256 if rows % 256 == 0 else rows

    def body(w_ref, g_ref, m_ref, v_ref, d_ref, m2_ref, v2_ref):
        gv = g_ref[...]
        m2 = ADAM_B1 * m_ref[...] + (1.0 - ADAM_B1) * gv
        v2 = ADAM_B2 * v_ref[...] + (1.0 - ADAM_B2) * jnp.square(gv)
        m_hat = m2 / (1.0 - ADAM_B1 ** ADAM_STEP)
        v_hat = v2 / (1.0 - ADAM_B2 ** ADAM_STEP)
        d_ref[...] = -ADAM_LR * (m_hat / (jnp.sqrt(v_hat) + ADAM_EPS) + ADAM_WD * w_ref[...])
        m2_ref[...] = m2
        v2_ref[...] = v2

    spec = pl.BlockSpec((tr, cols), lambda i: (i, 0))
    shp = SDS((rows, cols), f32)
    return _hbm_call(body, name=name, out_shape=(shp, shp, shp), grid=(rows // tr,), in_specs=[spec] * 4,
                          out_specs=(spec, spec, spec))(w, g, m, v)


def _adamw_halves(w, g_mine, g_other, m, v, cvec, name, after=None):
    rows, cols = w.shape
    half = rows // 2
    tr = min(256, half)
    per_half = half // tr

    def body(c_ref, w_ref, ga_ref, gb_ref, m_ref, v_ref, after_ref, g_ref, d_ref, m2_ref, v2_ref):
        in_my_half = (pl.program_id(0) // per_half) == c_ref[0]
        gv = jnp.where(in_my_half, ga_ref[...], gb_ref[...])
        g_ref[...] = gv
        m2 = ADAM_B1 * m_ref[...] + (1.0 - ADAM_B1) * gv
        v2 = ADAM_B2 * v_ref[...] + (1.0 - ADAM_B2) * jnp.square(gv)
        m_hat = m2 / (1.0 - ADAM_B1 ** ADAM_STEP)
        v_hat = v2 / (1.0 - ADAM_B2 ** ADAM_STEP)
        d_ref[...] = -ADAM_LR * (m_hat / (jnp.sqrt(v_hat) + ADAM_EPS) + ADAM_WD * w_ref[...])
        m2_ref[...] = m2
        v2_ref[...] = v2

    full = pl.BlockSpec((tr, cols), lambda i, c: (i, 0))
    part = pl.BlockSpec((tr, cols), lambda i, c: (i % per_half, 0))
    shp = SDS((rows, cols), f32)
    grid_spec = pltpu.PrefetchScalarGridSpec(num_scalar_prefetch=1, grid=(rows // tr,),
                                             in_specs=[full, part, part, full, full, ANY_SPEC], out_specs=(full,) * 4)
    return _hbm_call(body, name=name, out_shape=(shp,) * 4, grid_spec=grid_spec)(
        cvec, w, g_mine, g_other, m, v, cvec if after is None else after)


def _adam_math(w, g, m, v):
    m2 = ADAM_B1 * m + (1.0 - ADAM_B1) * g
    v2 = ADAM_B2 * v + (1.0 - ADAM_B2) * jnp.square(g)
    m_hat = m2 / (1.0 - ADAM_B1 ** ADAM_STEP)
    v_hat = v2 / (1.0 - ADAM_B2 ** ADAM_STEP)
    return -ADAM_LR * (m_hat / (jnp.sqrt(v_hat) + ADAM_EPS) + ADAM_WD * w), m2, v2


def _adamw_small(red, g_c_ctx, jvec, ws, ms, vs):
    n = len(ws)

    def body(*refs):
        red_ref, gc_ref, j_ref = refs[:3]
        w_refs, m_refs, v_refs = refs[3:3 + n], refs[3 + n:3 + 2 * n], refs[3 + 2 * n:3 + 3 * n]
        outs = refs[3 + 3 * n:]
        g_out, d_out, m_out, v_out = outs[:n], outs[n:2 * n], outs[2 * n:3 * n], outs[3 * n:]
        chip = j_ref[0]
        lanes = lambda i: (slice(None), slice(128 * i, 128 * (i + 1)))
        row = lambda r0, i: (lambda: red_ref[r0 + i:r0 + i + 1, :])
        whole = (slice(None), slice(None))
        chunks = [
            [((slice(None),), lambda: gc_ref[...].reshape(D))],
            [(lanes(i), row(RS_B_ADA, i)) for i in range(3 * D // 128)],
            [(lanes(i), row(RS_NORM_G, i)) for i in range(D // 128)],
            [(whole, lambda: red_ref[RS_GQ:RS_GQ + 1, 0:DH])],
            [(whole, lambda: red_ref[RS_GK:RS_GK + 1, 0:DH])],
            [((dr,), (lambda dr=dr: red_ref[pl.ds(RS_RPB + dr, H, stride=N_DR), 0:N_DC])) for dr in range(N_DR)],
            [((r,), (lambda r=r: red_ref[pl.ds(RS_CONV_W + 4 * r + chip, 1), :])) for r in range(3)],
            [(lanes(i), row(RS_CONV_B, i)) for i in range(DC // 128)],
        ]
        for a in range(n):
            for idx, grad in chunks[a]:
                g = grad()
                d, m2, v2 = _adam_math(w_refs[a][idx], g, m_refs[a][idx], v_refs[a][idx])
                g_out[a][idx] = g
                d_out[a][idx] = d
                m_out[a][idx] = m2
                v_out[a][idx] = v2

    shapes = [SDS(w.shape, f32) for w in ws]
    res = _pallas_call(body, name="adamw_small", out_shape=shapes * 4,
                       in_specs=[VMEM_SPEC, VMEM_SPEC, SMEM_SPEC] + [VMEM_SPEC] * (3 * n),
                       out_specs=[VMEM_SPEC] * (4 * n))(red, g_c_ctx, jvec, *ws, *ms, *vs)
    return [list(res[k * n:(k + 1) * n]) for k in range(4)]


def _rows128(a):
    return a.reshape(-1, 128)


def kernel(x, c, ctx, c_ctx, w_ada, b_ada, norm_g, w_in, q_norm_g, k_norm_g, rpb, conv_w, conv_b, w_out, loss_target, m_c_ctx, m_w_ada, m_b_ada, m_norm_g, m_w_in, m_q_norm_g, m_k_norm_g, m_rpb, m_conv_w, m_conv_b, m_w_out, v_c_ctx, v_w_ada, v_b_ada, v_norm_g, v_w_in, v_q_norm_g, v_k_norm_g, v_rpb, v_conv_w, v_conv_b, v_w_out):
    xi, yi, ci = lax.axis_index("x"), lax.axis_index("y"), lax.axis_index("c")
    dev = 4 * xi + 2 * yi + ci
    chip = 2 * xi + yi
    cvec = jnp.reshape(ci, (1,)).astype(i32)
    jvec = jnp.reshape(chip, (1,)).astype(i32)
    w_ada_s = w_ada[0]
    ncol = w_ada_s.shape[1]

    gc = _split_start([_to_slot(c.reshape(8, 128), 8, dev)], [], 7, _gather8_copies, "gather_c_start")
    wo4c = _cast_to_slot(w_out[0], jvec, "cast_w_out", after=gc[3])
    w4c = _cast_to_slot(w_in[0], jvec, "cast_w_in", after=wo4c)
    (c8,), _ = _split_wait(gc[0], gc[1], [gc[2]], [], w4c, _gather8_copies, "gather_c_wait")
    cc = jnp.concatenate([c8.reshape(8, D), c_ctx.reshape(1, D), jnp.zeros((7, D), f32)], axis=0)
    m_shard, sc16 = _adaln_shard(cc, w_ada_s)

    conv_w_pad = jnp.pad(conv_w[0], ((0, 5), (0, 0)))
    gm = _split_start([_to_slot(m_shard, 4, chip), _to_slot(conv_w_pad, 4, chip)], [], 6, _gather4_copies,
                      "gather_mod_start")

    all_k = [(0, 0, 0), (0, 0, 1), (0, 0, 2)]
    sem_a, rem_a, w4s, token = _split_start([w4c], [], 1, _near_copies, "weights_near_start", after=gm[4])
    (m4, cw4), _ = _split_wait(gm[0], gm[1], [gm[2], gm[3]], [], token, _gather4_copies, "gather_mod_wait")
    m_full = jnp.transpose(m4, (1, 0, 2)).reshape(16, 4 * ncol)
    mrow = lax.dynamic_slice(m_full, (dev, 0), (1, 3 * D))
    mrow_c = m_full[8:9]
    conv_w_full = jnp.transpose(cw4[:, 0:3, :], (1, 0, 2)).reshape(3, DC)
    waves = {}

    def near(after):
        (w4w,), _ = _split_wait(sem_a, rem_a, [w4s], [], after, _near_copies, "weights_near_wait")
        sem_b, rem_b, w4b, started = _split_start([w4w], [], 2, _pass_relay_copies, "weights_pass_start")
        waves["pass"] = (sem_b, rem_b)
        return w4b, started

    def near2(w4, after):
        (w4w,), _ = _split_wait(*waves["pass"], [w4], [], after, _pass_copy, "weights_pass_wait")
        return w4w

    def far(w4, after):
        (w4w,), _ = _split_wait(*waves["pass"], [w4], [], after, _relay_copy, "weights_far_wait")
        w4x, sem_c, rem_c, wo4s, started = _forward_then_start(w4w, wo4c, all_k, "weights_far_forward_out_start")
        (w4f,), _ = _split_wait(sem_c, rem_c, [w4x], [], started, _diag_forward_copy, "weights_far_forward_wait")
        waves["out"] = (sem_c, rem_c, wo4s)
        return w4f, started

    def w_out_arrived(after):
        sem_c, rem_c, wo4s = waves["out"]
        (wow,) = _halves_wait(sem_c, rem_c, [wo4s], after, all_k, "weights_out_wait")
        sem_d, rem_d, wof, started = _split_start([wow], [], 3, _forward_copies, "weights_out_forward_start")
        waves["out_forward"] = (sem_d, rem_d, wof)
        return started

    def w_out_gathered(after):
        sem_d, rem_d, wof = waves["out_forward"]
        (wo,), _ = _split_wait(sem_d, rem_d, [wof], [], after, _forward_copies, "weights_out_forward_wait")
        return wo.reshape(D, D)

    weights = dict(own=w_in[0], jvec=jvec, started=token, first=jvec ^ (1 + cvec), second=jvec ^ (2 - cvec),
                   near=near, near2=near2, far=far, out_arrived=w_out_arrived, out=w_out_gathered)

    exchange = _exchange_copies
    pending = {}

    def on_g_w_out(g_w_out):
        out = _split_start([g_w_out], [SDS((4, D // 8, D), f32)], 4, exchange, "grad_out_pair_start")
        pending["ex_out"] = out
        return out[4]

    def after_conv(dp8):
        ssem_o, rsem_o, g_o, land_o, _ = pending["ex_out"]
        (g_o,), (ex_o,) = _split_wait(ssem_o, rsem_o, [g_o], [land_o], dp8, exchange, "grad_out_pair_wait")
        to32, tob = _pair_sum_w_out(g_o, ex_o, cvec)
        out = _split_start([tob], [SDS((3, D // 8, D), bf16)], 3, _scatter_copies, "grad_out_chip_start")
        pending["sc_out"] = (out, to32)
        return out[4]

    def on_g_w_in_first(g_first):
        out = _split_start([g_first], [SDS((4, D // 2, D), f32)], 4, _block_exchange_copies, "grad_pair_start")
        pending["ex"] = (out[0], out[1], [out[2]], [out[3]])
        return out[4]

    def on_g_w_in_second(g_second):
        ex_ssem, ex_rsem, ex_srcs, ex_lands = pending["ex"]
        _, ex = _split_wait(ex_ssem, ex_rsem, ex_srcs, ex_lands, g_second, _block_exchange_copies, "grad_pair_wait")
        t32, tb = _pair_sum_w_in(g_second, ex[0])
        out = _split_start([tb], [SDS((3, D // 2, D), bf16)], 3, _scatter_copies, "grad_chip_start")
        pending["sc_in"] = (out, t32)
        return out[4]

    r = _local_step(x[0], ctx[0], loss_target[0], mrow, mrow_c, b_ada, norm_g, weights, q_norm_g, k_norm_g,
                    rpb[0], conv_w_full, conv_b,
                    dict(g_w_out=on_g_w_out, after_conv=after_conv, first_half=1 - cvec,
                         g_w_in_first=on_g_w_in_first, g_w_in_second=on_g_w_in_second))
    dm = jnp.concatenate([r["dshift"], r["dscale"], r["dgate"]], axis=1)
    dmc = jnp.concatenate([r["dshift_c"], r["dscale_c"], jnp.zeros((1, D), f32)], axis=1)
    pack_parts = [_rows128(dm), _rows128(dmc), _rows128(r["dng"]), _rows128(r["dng_c"]), _rows128(r["g_gq"]),
                  _rows128(r["g_gk"]), _rows128(r["g_gk_c"]), r["g_rpb"].reshape(H * N_DR, 128),
                  _rows128(r["g_conv_b"]), _rows128(r["g_conv_w"][0:3]), jnp.pad(r["loss_sum"], ((0, 0), (0, 127)))]
    pack = jnp.concatenate([jnp.pad(p, ((0, -p.shape[0] % 8), (0, 0))) for p in pack_parts], axis=0)
    assert pack.shape[0] == PK_ROWS
    gs = _split_start([_to_slot(pack, 8, dev)], [], 7, _gather8_copies, "gather_small_start")
    sc_in, t32 = pending["sc_in"]
    _, (r2,) = _split_wait(sc_in[0], sc_in[1], [sc_in[2]], [sc_in[3]], gs[3], _scatter_copies, "grad_chip_wait")
    u_in = _chip_sum(t32, r2, jvec, "chip_sum_w_in")
    sc_o, to32 = pending["sc_out"]
    _, (ro2,) = _split_wait(sc_o[0], sc_o[1], [sc_o[2]], [sc_o[3]], u_in, _scatter_copies, "grad_out_chip_wait")
    u_out = _chip_sum(to32, ro2, jvec, "chip_sum_w_out")
    o_in, o_out = _pair_swap([u_in, u_out], "grad_pair_swap")
    (gathered,), _ = _split_wait(gs[0], gs[1], [gs[2]], [], o_out, _gather8_copies, "gather_small_wait")
    red, dm16 = _small_reduce(gathered)
    loss = red[RS_LOSS, 0] * (0.5 / D)

    g_w_ada_s, cpart = _w_ada_grad(sc16, dm16, w_ada_s, jvec)
    gcp = _split_start([_to_slot(cpart, 4, chip)], [], 3, _gather4_copies, "gather_c_ctx_parts_start")
    g_w_in_s, d_w_in, nm_w_in, nv_w_in = _adamw_halves(w_in[0], u_in, o_in, m_w_in[0], v_w_in[0], cvec, "adamw_w_in",
                                                       after=gcp[3])
    (cparts4,), _ = _split_wait(gcp[0], gcp[1], [gcp[2]], [], nm_w_in, _gather4_copies, "gather_c_ctx_parts_wait")
    g_c_ctx = _c_ctx_grad(cparts4, c_ctx)
    d_w_ada, nm_w_ada, nv_w_ada = _adamw(w_ada_s, g_w_ada_s, m_w_ada[0], v_w_ada[0], "adamw_w_ada")

    t_rpb = lambda a: jnp.transpose(a, (0, 2, 1, 3)).reshape(N_DR, H, N_DC)
    t_cw = lambda a: jnp.transpose(a, (1, 0, 2))
    small = _adamw_small(
        red, g_c_ctx, jvec,
        [c_ctx, b_ada, norm_g, q_norm_g, k_norm_g, t_rpb(rpb), t_cw(conv_w), conv_b],
        [m_c_ctx, m_b_ada, m_norm_g, m_q_norm_g, m_k_norm_g, t_rpb(m_rpb), t_cw(m_conv_w), m_conv_b],
        [v_c_ctx, v_b_ada, v_norm_g, v_q_norm_g, v_k_norm_g, t_rpb(v_rpb), t_cw(v_conv_w), v_conv_b])
    for kind in small:
        kind[5] = jnp.transpose(kind[5].reshape(1, N_DR, H, N_DC), (0, 2, 1, 3))
        kind[6] = jnp.transpose(kind[6], (1, 0, 2))

    g_w_out_s, d_w_out, nm_w_out, nv_w_out = _adamw_halves(w_out[0], u_out, o_out, m_w_out[0], v_w_out[0], cvec,
                                                           "adamw_w_out")

    def ordered(kind, big_w_ada, big_w_in, big_w_out):
        s_c_ctx, s_b_ada, s_norm_g, s_q, s_k, s_rpb, s_conv_w, s_conv_b = small[kind]
        return [s_c_ctx, big_w_ada[None], s_b_ada, s_norm_g, big_w_in[None], s_q, s_k, s_rpb, s_conv_w,
                s_conv_b, big_w_out[None]]

    grads = ordered(0, g_w_ada_s, g_w_in_s, g_w_out_s)
    deltas = ordered(1, d_w_ada, d_w_in, d_w_out)
    new_m = ordered(2, nm_w_ada, nm_w_in, nm_w_out)
    new_v = ordered(3, nv_w_ada, nv_w_in, nv_w_out)
    return (loss, r["grad_x"][None], *grads, *deltas, *new_m, *new_v)
```

```python
import functools

import jax
import jax.numpy as jnp
import numpy as np
from jax import lax
from jax.experimental import pallas as pl
from jax.experimental.pallas import tpu as pltpu

f32, bf16, i32 = jnp.float32, jnp.bfloat16, jnp.int32
MESH = pl.DeviceIdType.MESH
HIGHEST = lax.Precision.HIGHEST

D = 1024
S = 2048
L = 256
GW = 64
ROWS = S // GW
H = 8
DH = 64
DA = H * DH
DC = 512
WIN_H, WIN_W = 8, 16
N_DR, N_DC = 2 * WIN_H - 1, 2 * WIN_W - 1
RMS_EPS = 1e-6
ROPE_THETA = 10000.0
QK_SCALE = DH ** -0.5
NEG = -1e30

QB = 128
NQB = S // QB
KR = 9
KB = KR * GW
TILE_GEOM = ((0, 0), (2, 0), (4, 0), (28, 23), (30, 23))
NT = len(TILE_GEOM)

ADAM_LR, ADAM_B1, ADAM_B2, ADAM_EPS, ADAM_WD, ADAM_STEP = 0.001, 0.9, 0.999, 1e-08, 0.01, 10

VMEM_SPEC = pl.BlockSpec(memory_space=pltpu.VMEM)
ANY_SPEC = pl.BlockSpec(memory_space=pl.ANY)
SMEM_SPEC = pl.BlockSpec(memory_space=pltpu.SMEM)
SDS = jax.ShapeDtypeStruct


_pallas_call = pl.pallas_call


def _hbm_call(body, *, out_shape, in_specs=None, out_specs=None, grid_spec=None, **kw):
    n_pre = 0
    if grid_spec is not None:
        ispecs, ospecs, n_pre = grid_spec.in_specs, grid_spec.out_specs, grid_spec.num_scalar_prefetch
        kw["grid_spec"] = grid_spec
    else:
        ispecs, ospecs = in_specs, out_specs
        kw.update(in_specs=in_specs, out_specs=out_specs)

    def blocked(spec):
        return isinstance(spec, pl.BlockSpec) and spec.block_shape is not None

    single = not isinstance(out_shape, (tuple, list))
    shapes = [out_shape] if single else list(out_shape)
    ospec_list = list(ospecs) if isinstance(ospecs, (tuple, list)) else [ospecs]
    shapes = [pltpu.HBM(s.shape, s.dtype) if blocked(sp) else s for s, sp in zip(shapes, ospec_list)]
    call = _pallas_call(body, out_shape=shapes[0] if single else tuple(shapes), **kw)

    def run(*args):
        arrays = [pltpu.with_memory_space_constraint(a, pltpu.HBM) if blocked(sp) else a
                  for a, sp in zip(args[n_pre:], ispecs)]
        return call(*args[:n_pre], *arrays)

    return run


def _cp(vmem_mb=None, **kw):
    if vmem_mb is not None:
        kw["vmem_limit_bytes"] = vmem_mb << 20
    return pltpu.CompilerParams(**kw)


def _silu(z):
    return z * jax.nn.sigmoid(z)


def _dsilu(z):
    sg = jax.nn.sigmoid(z)
    return sg * (1.0 + z * (1.0 - sg))


def _row_start(i):
    return min(max(i - WIN_H // 2, 0), ROWS - WIN_H)


def _my_pos():
    return lax.axis_index("x"), lax.axis_index("y"), lax.axis_index("c")


def _flip(v, bit):
    return 1 - v if bit else v


def _pair_swap(halves, name, after=None):
    nh = len(halves)

    def body(*refs):
        h_in, h_out = refs[:nh], refs[nh + 1:2 * nh + 1]
        ssem, rsem = refs[2 * nh + 1:]
        x, y, c = _my_pos()
        cps = [_Copy(h_in[a], h_out[a], h_out[a], ssem.at[a], rsem.at[a], (x, y, 1 - c)) for a in range(nh)]
        for cp in cps:
            cp.start()
        for cp in cps:
            cp.wait_recv()
        for cp in cps:
            cp.wait_send()

    return _hbm_call(
        body, name=name, out_shape=[SDS(h.shape, h.dtype) for h in halves], in_specs=[ANY_SPEC] * (nh + 1),
        out_specs=[ANY_SPEC] * nh, scratch_shapes=[pltpu.SemaphoreType.DMA((nh,)), pltpu.SemaphoreType.DMA((nh,))],
    )(*halves, halves[0] if after is None else after)


HBM_SPEC = pl.BlockSpec(memory_space=pltpu.HBM)
SEM_SPEC = pl.BlockSpec(memory_space=pltpu.SEMAPHORE)
DATAFLOW = pltpu.SideEffectType.DATAFLOW_SIDE_EFFECTING


def _peer_chips(x, y, c):
    out = []
    for k in range(1, 4):
        px, py = _flip(x, (k >> 1) & 1), _flip(y, k & 1)
        out.append(((px, py, c), 2 * px + py))
    return out


def _half_copies(srcs, dsts, ssem, rsem, which):
    x, y, c = _my_pos()
    j = 2 * x + y
    peers = _peer_chips(x, y, c)
    pairs = []
    for pos, group, k in which:
        half = srcs[pos].shape[1] // 2
        mine = pl.ds(pl.multiple_of(c * half, 8), half)
        dev, pj = peers[k]
        sem = 3 * group + k
        send = pltpu.make_async_remote_copy(src_ref=srcs[pos].at[j, mine], dst_ref=dsts[pos].at[j, mine],
                                            send_sem=ssem.at[sem], recv_sem=rsem.at[sem], device_id=dev,
                                            device_id_type=MESH)
        arrive = pltpu.make_async_remote_copy(src_ref=srcs[pos].at[j, mine], dst_ref=dsts[pos].at[pj, mine],
                                              send_sem=ssem.at[sem], recv_sem=rsem.at[sem], device_id=dev,
                                              device_id_type=MESH)
        pairs.append((send, arrive))
    return pairs


def _halves_wait(ssem, rsem, bigs, after, which, name):
    nb = len(bigs)

    def body(*refs):
        b_in = refs[:nb]
        ssem_ref, rsem_ref = refs[nb], refs[nb + 1]
        for send, arrive in _half_copies(b_in, b_in, ssem_ref, rsem_ref, which):
            send.wait_send()
            arrive.wait_recv()

    return _hbm_call(
        body, name=name, out_shape=tuple(pltpu.HBM(b.shape, b.dtype) for b in bigs),
        in_specs=[HBM_SPEC] * nb + [SEM_SPEC, SEM_SPEC, ANY_SPEC], out_specs=tuple([HBM_SPEC] * nb),
        input_output_aliases={a: a for a in range(nb)}, compiler_params=_cp(has_side_effects=DATAFLOW),
    )(*bigs, ssem, rsem, after)


FORWARD_SEM = 3


def _diag_forward_copy(srcs, dsts, ssem, rsem):
    x, y, c = _my_pos()
    half = srcs[0].shape[1] // 2
    diag = 3 - (2 * x + y)
    mine = pl.ds(pl.multiple_of(c * half, 8), half)
    other = pl.ds(pl.multiple_of((1 - c) * half, 8), half)
    return [_Copy(srcs[0].at[diag, mine], dsts[0].at[diag, mine], dsts[0].at[diag, other], ssem.at[FORWARD_SEM],
                  rsem.at[FORWARD_SEM], (x, y, 1 - c))]


def _forward_then_start(fwd, big, order, name):
    def body(f_in, b_in, f_out, ssem, rsem, b_out, token):
        _diag_forward_copy([f_in], [f_out], ssem, rsem)[0].start()
        for send, _ in _half_copies([b_in], [b_out], ssem, rsem, order):
            send.start()
        token[...] = jnp.zeros_like(token)

    n_sem = FORWARD_SEM + 1
    out_shape = (pltpu.HBM(fwd.shape, fwd.dtype), pltpu.SemaphoreType.DMA((n_sem,)), pltpu.SemaphoreType.DMA((n_sem,)),
                 pltpu.HBM(big.shape, big.dtype), SDS((8, 128), f32))
    return _hbm_call(
        body, name=name, out_shape=out_shape, in_specs=[HBM_SPEC, HBM_SPEC],
        out_specs=(HBM_SPEC, SEM_SPEC, SEM_SPEC, HBM_SPEC, VMEM_SPEC), input_output_aliases={0: 0, 1: 3},
        compiler_params=_cp(has_side_effects=DATAFLOW),
    )(*[pltpu.with_memory_space_constraint(b, pltpu.HBM) for b in (fwd, big)])


def _cast_to_slot(w, jvec, name, after=None):
    rows, cols = w.shape
    tr = 256

    def body(j_ref, w_ref, after_ref, o_ref):
        o_ref[...] = w_ref[...].astype(bf16)

    grid_spec = pltpu.PrefetchScalarGridSpec(
        num_scalar_prefetch=1, grid=(rows // tr,),
        in_specs=[pl.BlockSpec((tr, cols), lambda i, j: (i, 0)), ANY_SPEC],
        out_specs=pl.BlockSpec((None, tr, cols), lambda i, j: (j[0], i, 0)))
    return _hbm_call(body, name=name, out_shape=SDS((4, rows, cols), bf16),
                     grid_spec=grid_spec)(jvec, w, jvec if after is None else after)


def _exchange_copies(srcs, lands, ssem, rsem):
    x, y, c = _my_pos()
    half = srcs[0].shape[0] // 8
    cps = []
    for jb in range(4):
        src = srcs[0].at[pl.ds(pl.multiple_of((2 * jb + 1 - c) * half, 8), half)]
        cps.append(pltpu.make_async_remote_copy(src_ref=src, dst_ref=lands[0].at[jb], send_sem=ssem.at[jb],
                                                recv_sem=rsem.at[jb], device_id=(x, y, 1 - c), device_id_type=MESH))
    return cps


def _block_exchange_copies(srcs, lands, ssem, rsem):
    x, y, c = _my_pos()
    return [pltpu.make_async_remote_copy(src_ref=srcs[0].at[jb], dst_ref=lands[0].at[jb], send_sem=ssem.at[jb],
                                         recv_sem=rsem.at[jb], device_id=(x, y, 1 - c), device_id_type=MESH)
            for jb in range(4)]


def _scatter_copies(srcs, lands, ssem, rsem):
    x, y, c = _my_pos()
    cps = []
    for a in range(len(srcs)):
        for k, (dev, pj) in enumerate(_peer_chips(x, y, c)):
            cps.append(pltpu.make_async_remote_copy(src_ref=srcs[a].at[pj], dst_ref=lands[a].at[k],
                                                    send_sem=ssem.at[3 * a + k], recv_sem=rsem.at[3 * a + k],
                                                    device_id=dev, device_id_type=MESH))
    return cps


class _Copy:
    def __init__(self, src, dst, arrive, ssem, rsem, dev):
        make = lambda to: pltpu.make_async_remote_copy(src_ref=src, dst_ref=to, send_sem=ssem, recv_sem=rsem,
                                                       device_id=dev, device_id_type=MESH)
        send, arrival = make(dst), make(arrive)
        self.start, self.wait_send, self.wait_recv = send.start, send.wait_send, arrival.wait_recv


def _toward(x, y, along_x):
    return x + along_x * (1 - 2 * x), y + (1 - along_x) * (1 - 2 * y)


def _near_copies(srcs, dsts, ssem, rsem):
    x, y, c = _my_pos()
    px, py = _toward(x, y, c)
    j = 2 * x + y
    return [_Copy(srcs[0].at[j], dsts[0].at[j], dsts[0].at[2 * px + py], ssem.at[0], rsem.at[0], (px, py, c))]


def _pass_copy(srcs, dsts, ssem, rsem):
    x, y, c = _my_pos()
    px, py = _toward(x, y, c)
    qx, qy = _toward(x, y, 1 - c)
    got = 2 * px + py
    return [_Copy(srcs[0].at[got], dsts[0].at[got], dsts[0].at[2 * qx + qy], ssem.at[0], rsem.at[0], (x, y, 1 - c))]


def _relay_copy(srcs, dsts, ssem, rsem):
    x, y, c = _my_pos()
    px, py = _toward(x, y, c)
    qx, qy = _toward(x, y, 1 - c)
    half = srcs[0].shape[1] // 2
    mine = pl.ds(pl.multiple_of(c * half, 8), half)
    got, diag = 2 * px + py, 3 - (2 * x + y)
    return [_Copy(srcs[0].at[got, mine], dsts[0].at[got, mine], dsts[0].at[diag, mine], ssem.at[1], rsem.at[1],
                  (qx, qy, c))]


def _forward_copies(srcs, dsts, ssem, rsem):
    x, y, c = _my_pos()
    half = srcs[0].shape[1] // 2
    mine = pl.ds(pl.multiple_of(c * half, 8), half)
    other = pl.ds(pl.multiple_of((1 - c) * half, 8), half)
    return [_Copy(srcs[0].at[pj, mine], dsts[0].at[pj, mine], dsts[0].at[pj, other], ssem.at[k], rsem.at[k],
                  (x, y, 1 - c)) for k, (_, pj) in enumerate(_peer_chips(x, y, c))]


def _pass_relay_copies(srcs, dsts, ssem, rsem):
    return _pass_copy(srcs, dsts, ssem, rsem) + _relay_copy(srcs, dsts, ssem, rsem)


def _gather8_copies(srcs, dsts, ssem, rsem):
    x, y, c = _my_pos()
    me = 4 * x + 2 * y + c
    cps = []
    for a in range(len(srcs)):
        for k in range(1, 8):
            tgt = (_flip(x, (k >> 2) & 1), _flip(y, (k >> 1) & 1), _flip(c, k & 1))
            cps.append(_Copy(srcs[a].at[me], dsts[a].at[me], dsts[a].at[4 * tgt[0] + 2 * tgt[1] + tgt[2]],
                             ssem.at[7 * a + k - 1], rsem.at[7 * a + k - 1], tgt))
    return cps


def _gather4_copies(srcs, dsts, ssem, rsem):
    x, y, c = _my_pos()
    j = 2 * x + y
    cps = []
    for a in range(len(srcs)):
        for k, (dev, pj) in enumerate(_peer_chips(x, y, c)):
            cps.append(_Copy(srcs[a].at[j], dsts[a].at[j], dsts[a].at[pj], ssem.at[3 * a + k], rsem.at[3 * a + k], dev))
    return cps


def _to_slot(a, n, i):
    return lax.dynamic_update_slice(jnp.zeros((n,) + a.shape, a.dtype), a[None], (i,) + (0,) * a.ndim)


def _split_start(srcs, land_shapes, n_cp, make, name, after=None):
    ns, nl = len(srcs), len(land_shapes)
    n_in = ns + nl + (after is not None)

    def body(*refs):
        s_in = refs[:ns]
        ssem, rsem = refs[n_in], refs[n_in + 1]
        s_out = refs[n_in + 2:n_in + 2 + ns]
        l_out = refs[n_in + 2 + ns:n_in + 2 + ns + nl]
        token = refs[n_in + 2 + ns + nl]
        for cp in make(s_in, l_out if nl else s_out, ssem, rsem):
            cp.start()
        token[...] = jnp.zeros_like(token)

    lands = [pltpu.with_memory_space_constraint(lax.empty(sh.shape, sh.dtype), pltpu.HBM) for sh in land_shapes]
    out_shape = (pltpu.SemaphoreType.DMA((n_cp,)), pltpu.SemaphoreType.DMA((n_cp,)),
                 *[pltpu.HBM(b.shape, b.dtype) for b in srcs], *[pltpu.HBM(b.shape, b.dtype) for b in land_shapes],
                 SDS((8, 128), f32))
    return _hbm_call(
        body, name=name, out_shape=out_shape, in_specs=[HBM_SPEC] * (ns + nl) + [ANY_SPEC] * (after is not None),
        out_specs=(SEM_SPEC, SEM_SPEC, *[HBM_SPEC] * (ns + nl), VMEM_SPEC),
        input_output_aliases={i: 2 + i for i in range(ns + nl)}, compiler_params=_cp(has_side_effects=DATAFLOW),
    )(*[pltpu.with_memory_space_constraint(b, pltpu.HBM) for b in srcs], *lands, *([] if after is None else [after]))


def _split_wait(ssem, rsem, srcs, lands, after, make, name):
    ns, nl = len(srcs), len(lands)

    def body(*refs):
        s_in, l_in = refs[:ns], refs[ns:ns + nl]
        ssem_ref, rsem_ref = refs[ns + nl], refs[ns + nl + 1]
        for cp in make(s_in, l_in if nl else s_in, ssem_ref, rsem_ref):
            cp.wait_send()
            cp.wait_recv()

    outs = _hbm_call(
        body, name=name, out_shape=tuple(pltpu.HBM(b.shape, b.dtype) for b in (*srcs, *lands)),
        in_specs=[HBM_SPEC] * (ns + nl) + [SEM_SPEC, SEM_SPEC, ANY_SPEC], out_specs=tuple([HBM_SPEC] * (ns + nl)),
        input_output_aliases={i: i for i in range(ns + nl)}, compiler_params=_cp(has_side_effects=DATAFLOW),
    )(*srcs, *lands, ssem, rsem, after)
    return list(outs[:ns]), list(outs[ns:])


def _adaln_shard(cc, w_ada_shard):
    def body(c_ref, w_ref, m_ref, sc_ref):
        sc = _silu(c_ref[...])
        sc_ref[...] = sc
        m_ref[...] = jnp.dot(sc, w_ref[...], precision=HIGHEST, preferred_element_type=f32)

    return _hbm_call(
        body, name="adaln_shard", out_shape=(SDS((16, w_ada_shard.shape[1]), f32), SDS((16, D), f32)),
        in_specs=[VMEM_SPEC, VMEM_SPEC], out_specs=(VMEM_SPEC, VMEM_SPEC), compiler_params=_cp(32),
    )(cc, w_ada_shard)


def _prenorm(xx, norm_g, mrow, b_ada, tm, name, after=None):
    n = xx.shape[0]

    def body(x_ref, g_ref, m_ref, b_ref, after_ref, h_ref):
        x = x_ref[...]
        shift = m_ref[:, 0:D] + b_ref[:, 0:D]
        scale = m_ref[:, D:2 * D] + b_ref[:, D:2 * D]
        r = lax.rsqrt(jnp.mean(x * x, axis=-1, keepdims=True) + RMS_EPS)
        y = (x * r) * g_ref[...]
        h_ref[...] = (y * (1.0 + scale) + shift).astype(bf16)

    row = lambda i: (i, 0)
    fixed = lambda i: (0, 0)
    return _hbm_call(
        body, name=name, out_shape=SDS((n, D), bf16), grid=(n // tm,),
        in_specs=[pl.BlockSpec((tm, D), row), pl.BlockSpec((1, D), fixed), pl.BlockSpec((1, 3 * D), fixed),
                  pl.BlockSpec((1, 3 * D), fixed), ANY_SPEC],
        out_specs=pl.BlockSpec((tm, D), row),
    )(xx, norm_g, mrow, b_ada, b_ada if after is None else after)


def _in_proj_own(h, w_own, jvec):
    tm = 512

    def body(j_ref, h_ref, w_ref, p_ref):
        p_ref[...] = jnp.dot(h_ref[...], w_ref[...].astype(bf16), preferred_element_type=f32)

    grid_spec = pltpu.PrefetchScalarGridSpec(
        num_scalar_prefetch=1, grid=(S // tm,),
        in_specs=[pl.BlockSpec((tm, D), lambda i, j: (i, 0)), pl.BlockSpec((D, D), lambda i, j: (0, 0))],
        out_specs=pl.BlockSpec((tm, D), lambda i, j: (i, j[0])))
    return _hbm_call(body, name="in_proj_own", out_shape=SDS((S, 4 * D), f32), grid_spec=grid_spec,
                     compiler_params=_cp(40))(jvec, h, w_own)


def _in_proj_block(h, w4, p, bvec, name, after=None):
    tm = 512

    def body(b_ref, h_ref, w_ref, p_in_ref, after_ref, p_ref):
        p_ref[...] = jnp.dot(h_ref[...], w_ref[...], preferred_element_type=f32)

    grid_spec = pltpu.PrefetchScalarGridSpec(
        num_scalar_prefetch=1, grid=(S // tm,),
        in_specs=[pl.BlockSpec((tm, D), lambda i, b: (i, 0)), pl.BlockSpec((None, D, D), lambda i, b: (b[0], 0, 0)),
                  ANY_SPEC, ANY_SPEC],
        out_specs=pl.BlockSpec((tm, D), lambda i, b: (i, b[0])))
    return _hbm_call(body, name=name, out_shape=SDS((S, 4 * D), f32), grid_spec=grid_spec,
                     input_output_aliases={3: 0})(bvec, h, w4, p, bvec if after is None else after)


def _ctx_proj(hc, w4):
    def body(h_ref, w0_ref, w1_ref, p_ref):
        hv = h_ref[...]
        p_ref[:, 0:DA] = jnp.dot(hv, w0_ref[:, DA:2 * DA], preferred_element_type=f32)
        p_ref[:, DA:2 * DA] = jnp.dot(hv, w1_ref[:, 0:DA], preferred_element_type=f32)

    return _hbm_call(
        body, name="ctx_proj", out_shape=SDS((L, 2 * DA), f32), grid=(1,),
        in_specs=[pl.BlockSpec((L, D), lambda i: (0, 0)), pl.BlockSpec((None, D, D), lambda i: (0, 0, 0)),
                  pl.BlockSpec((None, D, D), lambda i: (1, 0, 0))],
        out_specs=pl.BlockSpec((L, 2 * DA), lambda i: (0, 0)),
    )(hc, w4, w4)


def _head_ones():
    r = lax.broadcasted_iota(i32, (DA, DA), 0) // DH
    c = lax.broadcasted_iota(i32, (DA, DA), 1) // DH
    return (r == c).astype(bf16)


def _head_sum(v, ones_bd):
    hi = v.astype(bf16)
    lo = (v - hi.astype(f32)).astype(bf16)
    return jnp.dot(hi, ones_bd, preferred_element_type=f32) + jnp.dot(lo, ones_bd, preferred_element_type=f32)


def _swap16(v):
    lane = lax.broadcasted_iota(i32, v.shape, 1)
    return jnp.where((lane & 31) < 16, pltpu.roll(v, DA - 16, 1), pltpu.roll(v, 16, 1))


def _rope_block(ct_ref, rt_ref, tm):
    rows = [jnp.tile(rt_ref[8 * j:8 * j + 8, :], (GW // 8, 1)) for j in range(tm // GW)]
    return jnp.tile(ct_ref[...], (tm // GW, 1)) + jnp.concatenate(rows, axis=0)


def _rope_specs(tm):
    col = pl.BlockSpec((GW, DA), lambda i: (0, 0))
    row = pl.BlockSpec((8 * tm // GW, DA), lambda i: (i, 0))
    return [col, row, col, row]


def _qk_prep(p, gq, gk, rope):
    tm = 256

    def body(qk_ref, v_ref, gq_ref, gk_ref, cc_ref, cr_ref, sc_ref, sr_ref, qr_ref, qp_ref, kr_ref, vh_ref):
        ones_bd = _head_ones()
        cs, sn = _rope_block(cc_ref, cr_ref, tm), _rope_block(sc_ref, sr_ref, tm)
        q = qk_ref[:, 0:DA]
        k = qk_ref[:, DA:2 * DA]
        yq = (q * lax.rsqrt(_head_sum(q * q, ones_bd) * (1.0 / DH) + RMS_EPS)) * gq_ref[...]
        yk = (k * lax.rsqrt(_head_sum(k * k, ones_bd) * (1.0 / DH) + RMS_EPS)) * gk_ref[...]
        qr = (yq * cs + _swap16(yq) * sn) * QK_SCALE
        qp = yq * QK_SCALE
        kr = yk * cs + _swap16(yk) * sn
        vv = v_ref[...]
        for hh in range(H):
            sl = slice(hh * DH, (hh + 1) * DH)
            qr_ref[hh] = qr[:, sl].astype(bf16)
            qp_ref[hh] = qp[:, sl].astype(bf16)
            kr_ref[hh] = kr[:, sl].astype(bf16)
            vh_ref[hh] = vv[:, sl].astype(bf16)

    hm = SDS((H, S, DH), bf16)
    hspec = pl.BlockSpec((H, tm, DH), lambda i: (0, i, 0))
    fixed = lambda i: (0, 0)
    return _hbm_call(
        body, name="qk_prep", out_shape=(hm, hm, hm, hm), grid=(S // tm,),
        in_specs=[pl.BlockSpec((tm, 2 * DA), lambda i: (i, 0)), pl.BlockSpec((tm, DA), lambda i: (i, 2)),
                  pl.BlockSpec((1, DA), fixed), pl.BlockSpec((1, DA), fixed)] + _rope_specs(tm),
        out_specs=(hspec, hspec, hspec, hspec),
    )(p, p, gq, gk, *rope)


def _ctx_prep(pc, gk):
    def body(p_ref, gk_ref, kc_ref, vc_ref):
        ones_bd = _head_ones()
        k = p_ref[:, 0:DA]
        yk = (k * lax.rsqrt(_head_sum(k * k, ones_bd) * (1.0 / DH) + RMS_EPS)) * gk_ref[...]
        vv = p_ref[:, DA:2 * DA]
        for hh in range(H):
            sl = slice(hh * DH, (hh + 1) * DH)
            kc_ref[hh] = yk[:, sl].astype(bf16)
            vc_ref[hh] = vv[:, sl].astype(bf16)

    hm = SDS((H, L, DH), bf16)
    return _hbm_call(
        body, name="ctx_prep", out_shape=(hm, hm), in_specs=[VMEM_SPEC, VMEM_SPEC], out_specs=(VMEM_SPEC, VMEM_SPEC),
    )(pc, gk)


def _tile_pieces():
    out = []
    for (i0, u0) in TILE_GEOM:
        rows = []
        for j in range(2):
            i = i0 + j
            rs = _row_start(i)
            rows.append([(u0 + u - i + WIN_H - 1) if rs <= u0 + u < rs + WIN_H else None for u in range(KR)])
        out.append(rows)
    return out


def _bias_prep(rpb_rev_pad, after=None):
    pieces = _tile_pieces()

    def body(r_ref, after_ref, o_ref):
        rp = r_ref[...]
        xs = jnp.broadcast_to(rp[:, None, :], (N_DR, GW, 128)).reshape(N_DR * GW, 128)
        row = lax.broadcasted_iota(i32, xs.shape, 0)
        lane = lax.broadcasted_iota(i32, xs.shape, 1)
        for b in range(6):
            xs = jnp.where(((row >> b) & 1) == 1, pltpu.roll(xs, 1 << b, 1), xs)
        xs = pltpu.roll(xs, 128 - (WIN_W - 1), 1)
        k = row & (GW - 1)
        c0 = jnp.clip(lane - WIN_W // 2, 0, GW - WIN_W)
        xs = jnp.where((k >= c0) & (k < c0 + WIN_W), xs, NEG)
        neg = jnp.full((GW, GW), NEG, f32)
        for t in range(NT):
            for j in range(2):
                for u in range(KR):
                    dr = pieces[t][j][u]
                    piece = neg if dr is None else xs[dr * GW:(dr + 1) * GW, 0:GW]
                    o_ref[t, u * GW:(u + 1) * GW, j * GW:(j + 1) * GW] = piece

    return _hbm_call(
        body, name="bias_prep", out_shape=SDS((H, NT, KB, QB), f32), grid=(H,),
        in_specs=[pl.BlockSpec((None, N_DR, 128), lambda h: (h, 0, 0)), ANY_SPEC],
        out_specs=pl.BlockSpec((None, NT, KB, QB), lambda h: (h, 0, 0, 0)),
    )(rpb_rev_pad, rpb_rev_pad if after is None else after)


def _bias_tiles(rpb2, after=None):
    return _bias_prep(jnp.pad(rpb2[:, :, ::-1], ((0, 0), (0, 0), (0, 128 - N_DC))), after)


def _block_geom(b):
    qs = b * QB
    ks = min(max(2 * b - 4, 0), ROWS - KR) * GW
    t = b if b < 2 else (b - (NQB - NT) if b > NQB - 3 else 2)
    return qs, ks, t


def _tt(a, b):
    return lax.dot_general(a, b, (((1,), (1,)), ((), ())), preferred_element_type=f32)


def _tn(a, b):
    return lax.dot_general(a, b, (((0,), (0,)), ((), ())), preferred_element_type=f32)


def _softmax_t(s_lat, s_ctx):
    m = jnp.maximum(jnp.max(s_lat, axis=0, keepdims=True), jnp.max(s_ctx, axis=0, keepdims=True))
    e_lat = jnp.exp(s_lat - m)
    e_ctx = jnp.exp(s_ctx - m)
    inv = 1.0 / (jnp.sum(e_lat, axis=0, keepdims=True) + jnp.sum(e_ctx, axis=0, keepdims=True))
    return e_lat * inv, e_ctx * inv


def _staged(n_blocks, stages):
    held = [dict() for _ in stages]
    for step in range(n_blocks + len(stages) - 1):
        for s, fn in enumerate(stages):
            b = step - s
            if 0 <= b < n_blocks:
                held[s][b] = fn(b) if s == 0 else fn(b, held[s - 1].pop(b))


def _attn_fwd(qr, qp, kr, vh, kc, vc, btt):
    def body(qr_ref, qp_ref, kr_ref, v_ref, kc_ref, vc_ref, bt_ref, o_ref):
        kcv, vcv = kc_ref[...], vc_ref[...]

        def scores(b):
            qs, ks, t = _block_geom(b)
            return (_tt(kr_ref[ks:ks + KB, :], qr_ref[qs:qs + QB, :]) + bt_ref[t], _tt(kcv, qp_ref[qs:qs + QB, :]))

        def probs(b, sc):
            p_lat, p_ctx = _softmax_t(*sc)
            return p_lat.astype(bf16), p_ctx.astype(bf16)

        def values(b, p):
            qs, ks, _ = _block_geom(b)
            o_ref[qs:qs + QB, :] = _tn(p[0], v_ref[ks:ks + KB, :]) + _tn(p[1], vcv)

        _staged(NQB, (scores, probs, values))

    sq = pl.BlockSpec((None, S, DH), lambda h: (h, 0, 0))
    sc = pl.BlockSpec((None, L, DH), lambda h: (h, 0, 0))
    return _hbm_call(
        body, name="attn_fwd", out_shape=SDS((H, S, DH), f32), grid=(H,),
        in_specs=[sq, sq, sq, sq, sc, sc, pl.BlockSpec((None, NT, KB, QB), lambda h: (h, 0, 0, 0))],
        out_specs=sq, compiler_params=_cp(48),
    )(qr, qp, kr, vh, kc, vc, btt)


def _shift_rows(v, down):
    n = v.shape[0]
    row = lax.broadcasted_iota(i32, v.shape, 0)
    if down:
        return jnp.where(row == 0, 0.0, pltpu.roll(v, 1, 0))
    return jnp.where(row == n - 1, 0.0, pltpu.roll(v, n - 1, 0))


def _conv_specs():
    col = lambda off: pl.BlockSpec((S, 128), lambda i, off=off: (0, off + i))
    return [col(16), col(20), col(24), col(28), pl.BlockSpec((3, 128), lambda i: (0, i)),
            pl.BlockSpec((1, 128), lambda i: (0, i))]


def _conv_fwd(p, conv_w, conv_b, after=None):
    def body(u_ref, bg_ref, cg_ref, zc_ref, w_ref, b_ref, after_ref, o_ref):
        cu = cg_ref[...] * u_ref[...]
        cv = b_ref[...] + _shift_rows(cu, True) * w_ref[0:1, :]
        cv = cv + cu * w_ref[1:2, :]
        cv = cv + _shift_rows(cu, False) * w_ref[2:3, :]
        o_ref[...] = ((bg_ref[...] * cv) * _silu(zc_ref[...])).astype(bf16)

    return _hbm_call(
        body, name="conv_fwd", out_shape=SDS((S, DC), bf16), grid=(DC // 128,),
        in_specs=_conv_specs() + [ANY_SPEC], out_specs=pl.BlockSpec((S, 128), lambda i: (0, i)),
        compiler_params=_cp(40),
    )(p, p, p, p, conv_w, conv_b, conv_b if after is None else after)


DP_Q, DP_K, DP_V, DP_ZA, DP_U, DP_BG, DP_CG, DP_ZC = range(8)


def _out_proj_loss(o, p, conv_g, w_out, xx, tgt, mrow, b_ada):
    tm = 256

    def body(o_ref, za_ref, c_ref, w_ref, x_ref, t_ref, m_ref, b_ref,
             dy_ref, dconv_ref, dp_ref, do_ref, gwo_ref, dgate_ref, loss_ref):
        k = pl.program_id(0)

        @pl.when(k == 0)
        def _():
            gwo_ref[...] = jnp.zeros_like(gwo_ref)
            dgate_ref[...] = jnp.zeros_like(dgate_ref)
            loss_ref[0, 0] = 0.0

        gate = m_ref[:, 2 * D:3 * D] + b_ref[:, 2 * D:3 * D]
        za = za_ref[...]
        sz = _silu(za)
        om = _merge_heads(o_ref)
        av, cv = (om * sz).astype(bf16), c_ref[...]
        mo = jnp.dot(av, w_ref[0:DA, :], preferred_element_type=f32)
        mo = mo + jnp.dot(cv, w_ref[DA:DA + DC, :], preferred_element_type=f32)
        y = x_ref[...] + gate * mo
        diff = y - t_ref[...]
        loss_ref[0, 0] += jnp.sum(diff * diff)
        dy = diff * (1.0 / D)
        dy_ref[...] = dy
        dgate_ref[...] += jnp.sum(dy * mo, axis=0, keepdims=True)
        dmo = (dy * gate).astype(bf16)
        dmix = _tt(dmo, w_ref[...])
        dattn = dmix[:, 0:DA]
        dconv_ref[...] = dmix[:, DA:DA + DC]
        a = dattn * sz
        for hh in range(H):
            do_ref[hh] = a[:, hh * DH:(hh + 1) * DH].astype(bf16)
        dp_ref[...] = ((dattn * _dsilu(za)) * om).astype(bf16)
        gwo_ref[0:DA, :] += _tn(av, dmo)
        gwo_ref[DA:DA + DC, :] += _tn(cv, dmo)

    row = lambda i: (i, 0)
    fixed = lambda i: (0, 0)
    hspec = pl.BlockSpec((H, tm, DH), lambda i: (0, i, 0))
    return _hbm_call(
        body, name="out_proj_loss",
        out_shape=(SDS((S, D), f32), SDS((S, DC), f32), SDS((8, S, DA), bf16), SDS((H, S, DH), bf16),
                   SDS((D, D), f32), SDS((1, D), f32), SDS((1, 1), f32)),
        grid=(S // tm,),
        in_specs=[hspec, pl.BlockSpec((tm, DA), lambda i: (i, 3)), pl.BlockSpec((tm, DC), row),
                  pl.BlockSpec((D, D), fixed), pl.BlockSpec((tm, D), row), pl.BlockSpec((tm, D), row),
                  pl.BlockSpec((1, 3 * D), fixed), pl.BlockSpec((1, 3 * D), fixed)],
        out_specs=(pl.BlockSpec((tm, D), row), pl.BlockSpec((tm, DC), row),
                   pl.BlockSpec((None, tm, DA), lambda i: (DP_ZA, i, 0)), hspec, pl.BlockSpec((D, D), fixed),
                   pl.BlockSpec((1, D), fixed), SMEM_SPEC),
        compiler_params=_cp(56, dimension_semantics=("arbitrary",)),
    )(o, p, conv_g, w_out, xx, tgt, mrow, b_ada)


def _conv_bwd(dconv, p, conv_w, conv_b, dp8, after=None):
    def body(d_ref, u_ref, bg_ref, cg_ref, zc_ref, w_ref, b_ref, dp_in_ref, after_ref, dp_ref, gw_ref, gb_ref):
        du_ref, dbg_ref, dcg_ref, dzc_ref = dp_ref.at[0], dp_ref.at[1], dp_ref.at[2], dp_ref.at[3]
        dconv = d_ref[...]
        u, bg, cg, zc = u_ref[...], bg_ref[...], cg_ref[...], zc_ref[...]
        w0, w1, w2 = w_ref[0:1, :], w_ref[1:2, :], w_ref[2:3, :]
        cu = cg * u
        cu_m, cu_p = _shift_rows(cu, True), _shift_rows(cu, False)
        cv = b_ref[...] + cu_m * w0
        cv = cv + cu * w1
        cv = cv + cu_p * w2
        sz = _silu(zc)
        dbg_ref[...] = ((dconv * sz) * cv).astype(bf16)
        dzc_ref[...] = ((dconv * (bg * cv)) * _dsilu(zc)).astype(bf16)
        dcv = (dconv * sz) * bg
        gb_ref[...] = jnp.sum(dcv, axis=0, keepdims=True)
        gw_ref[0:1, :] = jnp.sum(dcv * cu_m, axis=0, keepdims=True)
        gw_ref[1:2, :] = jnp.sum(dcv * cu, axis=0, keepdims=True)
        gw_ref[2:3, :] = jnp.sum(dcv * cu_p, axis=0, keepdims=True)
        gw_ref[3:8, :] = jnp.zeros((5, 128), f32)
        dcu = _shift_rows(dcv, False) * w0 + dcv * w1 + _shift_rows(dcv, True) * w2
        dcg_ref[...] = (dcu * u).astype(bf16)
        du_ref[...] = (dcu * cg).astype(bf16)

    return _hbm_call(
        body, name="conv_bwd", out_shape=(SDS((8, S, DC), bf16), SDS((8, DC), f32), SDS((1, DC), f32)),
        grid=(DC // 128,),
        in_specs=[pl.BlockSpec((S, 128), lambda i: (0, i))] + _conv_specs() + [ANY_SPEC, ANY_SPEC],
        out_specs=(pl.BlockSpec((4, S, 128), lambda i: (DP_U // 4, 0, i)), pl.BlockSpec((8, 128), lambda i: (0, i)),
                   pl.BlockSpec((1, 128), lambda i: (0, i))),
        input_output_aliases={7: 0}, compiler_params=_cp(48),
    )(dconv, p, p, p, p, conv_w, conv_b, dp8, conv_b if after is None else after)


def _attn_bwd(qr, qp, kr, vh, kc, vc, btt, do, after=None):
    def body(qr_ref, qp_ref, kr_ref, v_ref, kc_ref, vc_ref, bt_ref, do_ref, after_ref,
             dqr_ref, dqp_ref, dkr_ref, dv_ref, dkc_ref, dvc_ref, dbt_ref):
        kcv, vcv = kc_ref[...], vc_ref[...]
        dkr_ref[...] = jnp.zeros_like(dkr_ref)
        dv_ref[...] = jnp.zeros_like(dv_ref)
        dbt_ref[...] = jnp.zeros_like(dbt_ref)
        ctx_acc = {}

        def products(b):
            qs, ks, t = _block_geom(b)
            dob = do_ref[qs:qs + QB, :]
            s_lat = _tt(kr_ref[ks:ks + KB, :], qr_ref[qs:qs + QB, :]) + bt_ref[t]
            s_ctx = _tt(kcv, qp_ref[qs:qs + QB, :])
            return s_lat, s_ctx, _tt(v_ref[ks:ks + KB, :], dob), _tt(vcv, dob)

        def score_grads(b, x):
            s_lat, s_ctx, dp_lat, dp_ctx = x
            p_lat, p_ctx = _softmax_t(s_lat, s_ctx)
            delta = jnp.sum(p_lat * dp_lat, axis=0, keepdims=True) + jnp.sum(p_ctx * dp_ctx, axis=0, keepdims=True)
            ds_lat = p_lat * (dp_lat - delta)
            ds_ctx = p_ctx * (dp_ctx - delta)
            return ds_lat, ds_lat.astype(bf16), ds_ctx.astype(bf16), p_lat.astype(bf16), p_ctx.astype(bf16)

        def operand_grads(b, y):
            qs, ks, t = _block_geom(b)
            ds_lat, dsb_lat, dsb_ctx, pb_lat, pb_ctx = y
            qrb, qpb, dob = qr_ref[qs:qs + QB, :], qp_ref[qs:qs + QB, :], do_ref[qs:qs + QB, :]
            dbt_ref[t] += ds_lat
            dqr_ref[qs:qs + QB, :] = _tn(dsb_lat, kr_ref[ks:ks + KB, :])
            dqp_ref[qs:qs + QB, :] = _tn(dsb_ctx, kcv)
            dkr_ref[ks:ks + KB, :] += jnp.dot(dsb_lat, qrb, preferred_element_type=f32)
            dv_ref[ks:ks + KB, :] += jnp.dot(pb_lat, dob, preferred_element_type=f32)
            dkc = jnp.dot(dsb_ctx, qpb, preferred_element_type=f32)
            dvc = jnp.dot(pb_ctx, dob, preferred_element_type=f32)
            ctx_acc["k"] = dkc if b == 0 else ctx_acc["k"] + dkc
            ctx_acc["v"] = dvc if b == 0 else ctx_acc["v"] + dvc

        _staged(NQB, (products, score_grads, operand_grads))
        dkc_ref[...] = ctx_acc["k"]
        dvc_ref[...] = ctx_acc["v"]

    sq = pl.BlockSpec((None, S, DH), lambda h: (h, 0, 0))
    sc = pl.BlockSpec((None, L, DH), lambda h: (h, 0, 0))
    sb = pl.BlockSpec((None, NT, KB, QB), lambda h: (h, 0, 0, 0))
    big, ctxs = SDS((H, S, DH), f32), SDS((H, L, DH), f32)
    return _hbm_call(
        body, name="attn_bwd", out_shape=(big, big, big, big, ctxs, ctxs, SDS((H, NT, KB, QB), f32)), grid=(H,),
        in_specs=[sq, sq, sq, sq, sc, sc, sb, sq, ANY_SPEC], out_specs=(sq, sq, sq, sq, sc, sc, sb),
        compiler_params=_cp(56),
    )(qr, qp, kr, vh, kc, vc, btt, do, do if after is None else after)


def _bias_bwd(dbtt, after=None):
    pieces = _tile_pieces()

    def body(d_ref, after_ref, o_ref, scr):
        scr[...] = jnp.zeros_like(scr)
        acc = [None] * N_DR
        for t in range(NT):
            for j in range(2):
                for u in range(KR):
                    dr = pieces[t][j][u]
                    if dr is None:
                        continue
                    piece = d_ref[t, u * GW:(u + 1) * GW, j * GW:(j + 1) * GW]
                    acc[dr] = piece if acc[dr] is None else acc[dr] + piece
        for dr in range(N_DR):
            scr[dr * GW:(dr + 1) * GW, 0:GW] = acc[dr]
        xs = pltpu.roll(scr[...], WIN_W - 1, 1)
        row = lax.broadcasted_iota(i32, xs.shape, 0)
        for b in range(6):
            xs = jnp.where(((row >> b) & 1) == 1, pltpu.roll(xs, 128 - (1 << b), 1), xs)
        rev = jnp.sum(xs.reshape(N_DR, GW, 128), axis=1)
        a = lax.broadcasted_iota(i32, (128, 128), 0)
        b = lax.broadcasted_iota(i32, (128, 128), 1)
        flip = ((a + b == N_DC - 1) & (a < N_DC)).astype(f32)
        o_ref[...] = jnp.dot(rev, flip, precision=HIGHEST, preferred_element_type=f32)

    return _hbm_call(
        body, name="bias_bwd", out_shape=SDS((H, N_DR, 128), f32), grid=(H,),
        in_specs=[pl.BlockSpec((None, NT, KB, QB), lambda h: (h, 0, 0, 0)), ANY_SPEC],
        out_specs=pl.BlockSpec((None, N_DR, 128), lambda h: (h, 0, 0)),
        scratch_shapes=[pltpu.VMEM((N_DR * GW, 128), f32)],
    )(dbtt, dbtt if after is None else after)


def _merge_heads(ref):
    return jnp.concatenate([ref[hh] for hh in range(H)], axis=1)


def _head_norm_bwd(xraw, gain, dy, ones_bd):
    r = lax.rsqrt(_head_sum(xraw * xraw, ones_bd) * (1.0 / DH) + RMS_EPS)
    xh = xraw * r
    gdy = dy * gain
    dx = r * (gdy - xh * (_head_sum(xh * gdy, ones_bd) * (1.0 / DH)))
    return dx, jnp.sum(dy * xh, axis=0, keepdims=True)


def _qk_bwd(dqr, dqp, dkr, dvh, p, gq, gk, rope, dp8):
    tm = 256

    def body(dqr_ref, dqp_ref, dkr_ref, dv_ref, qk_ref, gq_ref, gk_ref, cc_ref, cr_ref, sc_ref, sr_ref, dp_in_ref,
             dp_ref, ggq_ref, ggk_ref):
        dq_ref, dk_ref, dvo_ref = dp_ref.at[DP_Q], dp_ref.at[DP_K], dp_ref.at[DP_V]

        @pl.when(pl.program_id(0) == 0)
        def _():
            ggq_ref[...] = jnp.zeros_like(ggq_ref)
            ggk_ref[...] = jnp.zeros_like(ggk_ref)

        ones_bd = _head_ones()
        cs, sn = _rope_block(cc_ref, cr_ref, tm), _rope_block(sc_ref, sr_ref, tm)
        a = _merge_heads(dqr_ref)
        dyq = ((a * cs - _swap16(a) * sn) + _merge_heads(dqp_ref)) * QK_SCALE
        bk = _merge_heads(dkr_ref)
        dyk = bk * cs - _swap16(bk) * sn
        dq, gq_part = _head_norm_bwd(qk_ref[:, 0:DA], gq_ref[...], dyq, ones_bd)
        dk, gk_part = _head_norm_bwd(qk_ref[:, DA:2 * DA], gk_ref[...], dyk, ones_bd)
        dq_ref[...] = dq.astype(bf16)
        dk_ref[...] = dk.astype(bf16)
        dvo_ref[...] = _merge_heads(dv_ref).astype(bf16)
        ggq_ref[...] += gq_part
        ggk_ref[...] += gk_part

    hspec = pl.BlockSpec((H, tm, DH), lambda i: (0, i, 0))
    fixed = pl.BlockSpec((1, DA), lambda i: (0, 0))
    return _hbm_call(
        body, name="qk_bwd", out_shape=(SDS((8, S, DA), bf16), SDS((1, DA), f32), SDS((1, DA), f32)), grid=(S // tm,),
        in_specs=[hspec, hspec, hspec, hspec, pl.BlockSpec((tm, 2 * DA), lambda i: (i, 0)), fixed, fixed]
        + _rope_specs(tm) + [ANY_SPEC],
        out_specs=(pl.BlockSpec((3, tm, DA), lambda i: (0, i, 0)), fixed, fixed), input_output_aliases={11: 0},
        compiler_params=_cp(40, dimension_semantics=("arbitrary",)),
    )(dqr, dqp, dkr, dvh, p, gq, gk, *rope, dp8)


def _ctx_bwd(dkc, dvc, pc, gk):
    def body(dkc_ref, dvc_ref, p_ref, gk_ref, dk_ref, dv_ref, ggk_ref):
        ones_bd = _head_ones()
        dk, gk_part = _head_norm_bwd(p_ref[:, 0:DA], gk_ref[...], _merge_heads(dkc_ref), ones_bd)
        dk_ref[...] = dk.astype(bf16)
        dv_ref[...] = _merge_heads(dvc_ref).astype(bf16)
        ggk_ref[...] = gk_part

    piece = SDS((L, DA), bf16)
    return _hbm_call(
        body, name="ctx_bwd", out_shape=(piece, piece, SDS((1, DA), f32)), in_specs=[VMEM_SPEC] * 4,
        out_specs=(VMEM_SPEC,) * 3,
    )(dkc, dvc, pc, gk)


def _grad_w_in(h, dp8, hc, dkc_raw, dvc_m, half, name, after=None):
    def body(half_ref, h_ref, p_ref, hc_ref, dk_ref, dv_ref, after_ref, g_ref):
        j = pl.program_id(0)
        hv = h_ref[...]
        g_ref[:, 0:DA] = _tn(hv, p_ref[0])
        g_ref[:, DA:2 * DA] = _tn(hv, p_ref[1])

        @pl.when(j == 0)
        def _():
            g_ref[:, DA:2 * DA] += _tn(hc_ref[...], dk_ref[...])

        @pl.when(j == 1)
        def _():
            g_ref[:, 0:DA] += _tn(hc_ref[...], dv_ref[...])

    fixed = lambda j, s: (0, 0)
    hd = D // 2
    grid_spec = pltpu.PrefetchScalarGridSpec(
        num_scalar_prefetch=1, grid=(4,),
        in_specs=[pl.BlockSpec((S, hd), lambda j, s: (0, s[0])), pl.BlockSpec((2, S, DA), lambda j, s: (j, 0, 0)),
                  pl.BlockSpec((L, hd), lambda j, s: (0, s[0])), pl.BlockSpec((L, DA), fixed),
                  pl.BlockSpec((L, DA), fixed), ANY_SPEC],
        out_specs=pl.BlockSpec((None, hd, D), lambda j, s: (j, 0, 0)))
    return _hbm_call(
        body, name=name, out_shape=SDS((4, hd, D), f32), grid_spec=grid_spec, compiler_params=_cp(40),
    )(half, h, dp8, hc, dkc_raw, dvc_m, half if after is None else after)


def _norm_mod_bwd(x, dh, g, scale):
    r = lax.rsqrt(jnp.mean(x * x, axis=-1, keepdims=True) + RMS_EPS)
    xh = x * r
    y = xh * g
    dshift = jnp.sum(dh, axis=0, keepdims=True)
    dscale = jnp.sum(dh * y, axis=0, keepdims=True)
    dyn = dh * (1.0 + scale)
    dg = jnp.sum(dyn * xh, axis=0, keepdims=True)
    gdy = dyn * g
    dx = r * (gdy - xh * jnp.mean(xh * gdy, axis=-1, keepdims=True))
    return dx, dshift, dscale, dg


def _dh_grad_x(dp8, w4, xx, dy, norm_g, mrow, b_ada, after=None):
    tm = 256

    def body(p_ref, w_ref, x_ref, dy_ref, g_ref, m_ref, b_ref, after_ref, gx_ref, dsh_ref, dsc_ref, dg_ref):
        @pl.when(pl.program_id(0) == 0)
        def _():
            dsh_ref[...] = jnp.zeros_like(dsh_ref)
            dsc_ref[...] = jnp.zeros_like(dsc_ref)
            dg_ref[...] = jnp.zeros_like(dg_ref)

        dh = None
        for j in range(4):
            for half in range(2):
                term = _tt(p_ref[2 * j + half], w_ref[j, :, half * DA:(half + 1) * DA])
                dh = term if dh is None else dh + term
        scale = m_ref[:, D:2 * D] + b_ref[:, D:2 * D]
        dx, dshift, dscale, dg = _norm_mod_bwd(x_ref[...], dh, g_ref[...], scale)
        gx_ref[...] = dy_ref[...] + dx
        dsh_ref[...] += dshift
        dsc_ref[...] += dscale
        dg_ref[...] += dg

    row = lambda i: (i, 0)
    fixed = lambda i: (0, 0)
    vec = SDS((1, D), f32)
    return _hbm_call(
        body, name="dh_grad_x", out_shape=(SDS((S, D), f32), vec, vec, vec), grid=(S // tm,),
        in_specs=[pl.BlockSpec((8, tm, DA), lambda i: (0, i, 0)), pl.BlockSpec((4, D, D), lambda i: (0, 0, 0)),
                  pl.BlockSpec((tm, D), row), pl.BlockSpec((tm, D), row), pl.BlockSpec((1, D), fixed),
                  pl.BlockSpec((1, 3 * D), fixed), pl.BlockSpec((1, 3 * D), fixed), ANY_SPEC],
        out_specs=(pl.BlockSpec((tm, D), row), pl.BlockSpec((1, D), fixed), pl.BlockSpec((1, D), fixed),
                   pl.BlockSpec((1, D), fixed)),
        compiler_params=_cp(56, dimension_semantics=("arbitrary",)),
    )(dp8, w4, xx, dy, norm_g, mrow, b_ada, b_ada if after is None else after)


def _dhc_sums(dkc_raw, dvc_m, w4, ctx2, norm_g, mrow_c, b_ada, after=None):
    def body(dk_ref, dv_ref, w0_ref, w1_ref, x_ref, g_ref, m_ref, b_ref, after_ref, dsh_ref, dsc_ref, dg_ref):
        dh = _tt(dk_ref[...], w0_ref[:, DA:2 * DA]) + _tt(dv_ref[...], w1_ref[:, 0:DA])
        scale = m_ref[:, D:2 * D] + b_ref[:, D:2 * D]
        _, dshift, dscale, dg = _norm_mod_bwd(x_ref[...], dh, g_ref[...], scale)
        dsh_ref[...] = dshift
        dsc_ref[...] = dscale
        dg_ref[...] = dg

    fixed = lambda i: (0, 0)
    vec = SDS((1, D), f32)
    vspec = pl.BlockSpec((1, D), fixed)
    return _hbm_call(
        body, name="dhc_sums", out_shape=(vec, vec, vec), grid=(1,),
        in_specs=[pl.BlockSpec((L, DA), fixed), pl.BlockSpec((L, DA), fixed),
                  pl.BlockSpec((None, D, D), lambda i: (0, 0, 0)), pl.BlockSpec((None, D, D), lambda i: (1, 0, 0)),
                  pl.BlockSpec((L, D), fixed), vspec, pl.BlockSpec((1, 3 * D), fixed), pl.BlockSpec((1, 3 * D), fixed),
                  ANY_SPEC],
        out_specs=(vspec, vspec, vspec), compiler_params=_cp(32),
    )(dkc_raw, dvc_m, w4, w4, ctx2, norm_g, mrow_c, b_ada, b_ada if after is None else after)


def _rope_tables():
    nf = DH // 4
    inv = np.float32(ROPE_THETA) ** (-np.arange(nf, dtype=np.float32) / np.float32(nf))
    ang_c = np.arange(GW, dtype=np.float32)[:, None] * inv
    ang_r = np.arange(ROWS, dtype=np.float32)[:, None] * inv
    zc, zr = np.zeros((GW, 2 * nf), np.float32), np.zeros((ROWS, 2 * nf), np.float32)
    ct_cos = np.tile(np.concatenate([zc, np.cos(ang_c), np.cos(ang_c)], axis=1), (1, H))
    ct_sin = np.tile(np.concatenate([zc, -np.sin(ang_c), np.sin(ang_c)], axis=1), (1, H))
    rt_cos = np.tile(np.concatenate([np.cos(ang_r), np.cos(ang_r), zr], axis=1), (1, H))
    rt_sin = np.tile(np.concatenate([-np.sin(ang_r), np.sin(ang_r), zr], axis=1), (1, H))
    rep8 = lambda t: np.ascontiguousarray(np.broadcast_to(t[:, None, :], (ROWS, 8, DA))).reshape(ROWS * 8, DA)
    return tuple(jnp.asarray(t, f32) for t in (ct_cos, rep8(rt_cos), ct_sin, rep8(rt_sin)))


def _local_step(xx, ctx2, tgt, mrow, mrow_c, b_ada, norm_g, weights, q_norm_g, k_norm_g, rpb2, conv_w_full, conv_b,
                hooks=None):
    hooks = hooks or {}
    gq = jnp.tile(q_norm_g, (1, H))
    gk = jnp.tile(k_norm_g, (1, H))
    rope = _rope_tables()

    h = _prenorm(xx, norm_g, mrow, b_ada, 256, "prenorm_x", after=weights.get("started"))
    hc = _prenorm(ctx2, norm_g, mrow_c, b_ada, L, "prenorm_ctx")
    jv = weights["jvec"]
    p = _in_proj_own(h, weights["own"], jv)
    btb = _bias_tiles(rpb2, after=p)
    w4, started = weights["near"](btb)
    p = _in_proj_block(h, w4, p, weights["first"], "in_proj_near", after=started)
    w4 = weights["near2"](w4, p)
    p = _in_proj_block(h, w4, p, weights["second"], "in_proj_near2")
    w4, started = weights["far"](w4, p)
    p = _in_proj_block(h, w4, p, jv ^ 3, "in_proj_far", after=started)
    pc = _ctx_proj(hc, w4)
    qr, qp, kr, vh = _qk_prep(p, gq, gk, rope)
    kc, vc = _ctx_prep(pc, gk)
    o = _attn_fwd(qr, qp, kr, vh, kc, vc, btb)
    started = weights["out_arrived"](o) if "out_arrived" in weights else None
    conv_g = _conv_fwd(p, conv_w_full, conv_b, after=started)
    w_out_full = weights["out"](conv_g)
    dy, dconv, dp8, do, g_w_out, dgate, loss_sum = _out_proj_loss(o, p, conv_g, w_out_full, xx, tgt, mrow, b_ada)
    started = hooks["g_w_out"](g_w_out) if "g_w_out" in hooks else None
    dp8, g_conv_w, g_conv_b = _conv_bwd(dconv, p, conv_w_full, conv_b, dp8, after=started)
    started = hooks["after_conv"](dp8) if "after_conv" in hooks else None
    dqr, dqp, dkr, dvh, dkc, dvc, dbtb = _attn_bwd(qr, qp, kr, vh, kc, vc, btb, do, after=started)
    dp8, g_gq, g_gk = _qk_bwd(dqr, dqp, dkr, dvh, p, gq, gk, rope, dp8)
    dkc_raw, dvc_m, g_gk_c = _ctx_bwd(dkc, dvc, pc, gk)
    first = hooks.get("first_half", jnp.zeros((1,), i32))
    g_first = _grad_w_in(h, dp8, hc, dkc_raw, dvc_m, first, "grad_w_in_first")
    started = hooks["g_w_in_first"](g_first) if "g_w_in_first" in hooks else None
    g_second = _grad_w_in(h, dp8, hc, dkc_raw, dvc_m, 1 - first, "grad_w_in_second", after=started)
    started = hooks["g_w_in_second"](g_second) if "g_w_in_second" in hooks else None
    dshift_c, dscale_c, dng_c = _dhc_sums(dkc_raw, dvc_m, w4, ctx2, norm_g, mrow_c, b_ada, after=started)
    g_rpb = _bias_bwd(dbtb, after=dshift_c)
    grad_x, dshift, dscale, dng = _dh_grad_x(dp8, w4, xx, dy, norm_g, mrow, b_ada, after=g_rpb)
    return dict(loss_sum=loss_sum, grad_x=grad_x, g_w_in=(g_first, g_second), g_w_out=g_w_out, g_conv_w=g_conv_w,
                g_conv_b=g_conv_b, g_rpb=g_rpb, g_gq=g_gq, g_gk=g_gk, g_gk_c=g_gk_c, dshift=dshift, dscale=dscale,
                dgate=dgate, dng=dng, dshift_c=dshift_c, dscale_c=dscale_c, dng_c=dng_c)


def _pair_sum_w_in(g, r):
    tr = 128

    def body(g_ref, r_ref, t32_ref, tb_ref):
        t = g_ref[...] + r_ref[...]
        t32_ref[...] = t
        tb_ref[...] = t.astype(bf16)

    half = D // 2
    spec = pl.BlockSpec((4, tr, D), lambda i: (0, i, 0))
    return _hbm_call(body, name="pair_sum_w_in", out_shape=(SDS((4, half, D), f32), SDS((4, half, D), bf16)),
                     grid=(half // tr,), in_specs=[spec, spec], out_specs=(spec, spec), compiler_params=_cp(40))(g, r)


def _pair_sum_w_out(g, r, cvec):
    hr = D // 8

    def body(c_ref, g0, g1, g2, g3, r_ref, t32_ref, tb_ref):
        for q, g_ref in enumerate((g0, g1, g2, g3)):
            t = g_ref[...] + r_ref[q]
            t32_ref[q] = t
            tb_ref[q] = t.astype(bf16)

    gspecs = [pl.BlockSpec((hr, D), lambda i, c, q=q: (2 * q + c[0], 0)) for q in range(4)]
    full = pl.BlockSpec((4, hr, D), lambda i, c: (0, 0, 0))
    grid_spec = pltpu.PrefetchScalarGridSpec(num_scalar_prefetch=1, grid=(1,), in_specs=gspecs + [full],
                                             out_specs=(full, full))
    return _hbm_call(body, name="pair_sum_w_out", out_shape=(SDS((4, hr, D), f32), SDS((4, hr, D), bf16)),
                          grid_spec=grid_spec)(cvec, g, g, g, g, r)


def _chip_sum(t32, r2, jvec, name):
    rows = t32.shape[1]
    tr = min(rows, 128)

    def body(j_ref, t_ref, r_ref, u_ref):
        u_ref[...] = ((t_ref[...] + r_ref[0].astype(f32)) + r_ref[1].astype(f32)) + r_ref[2].astype(f32)

    grid_spec = pltpu.PrefetchScalarGridSpec(
        num_scalar_prefetch=1, grid=(rows // tr,),
        in_specs=[pl.BlockSpec((None, tr, D), lambda i, j: (j[0], i, 0)), pl.BlockSpec((3, tr, D), lambda i, j: (0, i, 0))],
        out_specs=pl.BlockSpec((tr, D), lambda i, j: (i, 0)))
    return _hbm_call(body, name=name, out_shape=SDS((rows, D), f32), grid_spec=grid_spec)(jvec, t32, r2)


_PK = {}
_off = 0
for _name, _rows in (("dm", 24), ("dmc", 24), ("dng", 8), ("dng_c", 8), ("gq", 8), ("gk", 8), ("gk_c", 8),
                     ("rpb", H * N_DR), ("conv_b", 8), ("conv_w", 16), ("loss", 8)):
    _PK[_name] = (_off, _off + _rows)
    _off += _rows
PK_ROWS = _off
RS_B_ADA, RS_NORM_G, RS_GQ, RS_GK, RS_RPB, RS_CONV_B, RS_CONV_W, RS_DMC, RS_LOSS, RS_ROWS = (
    0, 24, 32, 40, 48, 168, 176, 192, 216, 224)


def _small_reduce(gathered):
    def body(g_ref, o_ref, dm_ref):
        a0 = _PK["dm"][0]
        dm_ref[...] = jnp.zeros_like(dm_ref)
        for b in range(8):
            for i in range(24):
                dm_ref[b:b + 1, 128 * i:128 * (i + 1)] = g_ref[b, a0 + i:a0 + i + 1, :]
        tot = g_ref[0]
        for b in range(1, 8):
            tot = tot + g_ref[b]

        def rows(name):
            a, z = _PK[name]
            return tot[a:z]

        o_ref[RS_B_ADA:RS_B_ADA + 24] = rows("dm") + rows("dmc")
        o_ref[RS_NORM_G:RS_NORM_G + 8] = rows("dng") + rows("dng_c")
        gq = jnp.broadcast_to(jnp.sum(rows("gq"), axis=0, keepdims=True), (8, 128))
        gk = jnp.broadcast_to(jnp.sum(rows("gk") + rows("gk_c"), axis=0, keepdims=True), (8, 128))
        o_ref[RS_GQ:RS_GQ + 8] = gq + pltpu.roll(gq, DH, 1)
        o_ref[RS_GK:RS_GK + 8] = gk + pltpu.roll(gk, DH, 1)
        o_ref[RS_RPB:RS_RPB + H * N_DR] = rows("rpb")
        o_ref[RS_CONV_B:RS_CONV_B + 8] = rows("conv_b")
        o_ref[RS_CONV_W:RS_CONV_W + 16] = rows("conv_w")
        dmc = rows("dmc")
        o_ref[RS_DMC:RS_DMC + 24] = dmc
        o_ref[RS_LOSS:RS_LOSS + 8] = rows("loss")
        for i in range(24):
            dm_ref[8:9, 128 * i:128 * (i + 1)] = dmc[i:i + 1]

    return _hbm_call(body, name="small_reduce", out_shape=(SDS((RS_ROWS, 128), f32), SDS((16, 3 * D), f32)),
                     in_specs=[VMEM_SPEC], out_specs=(VMEM_SPEC, VMEM_SPEC))(gathered)


def _w_ada_grad(sc16, dm16, w_ada_shard, jvec):
    ncol = w_ada_shard.shape[1]

    def body(j_ref, sc_ref, dm_ref, w_ref, g_ref, part_ref):
        dm = dm_ref[...]
        g_ref[...] = lax.dot_general(sc_ref[...], dm, (((0,), (0,)), ((), ())), precision=HIGHEST,
                                     preferred_element_type=f32)
        part_ref[...] = lax.dot_general(dm[8:16], w_ref[...], (((1,), (1,)), ((), ())), precision=HIGHEST,
                                        preferred_element_type=f32)

    fixed = lambda i, j: (0, 0)
    grid_spec = pltpu.PrefetchScalarGridSpec(
        num_scalar_prefetch=1, grid=(1,),
        in_specs=[pl.BlockSpec((16, D), fixed), pl.BlockSpec((16, ncol), lambda i, j: (0, j[0])),
                  pl.BlockSpec((D, ncol), fixed)],
        out_specs=(pl.BlockSpec((D, ncol), fixed), pl.BlockSpec((8, D), fixed)))
    return _pallas_call(body, name="w_ada_grad", out_shape=(SDS((D, ncol), f32), SDS((8, D), f32)),
                        grid_spec=grid_spec, compiler_params=_cp(40))(jvec, sc16, dm16, w_ada_shard)


def _c_ctx_grad(parts4, c_ctx):
    def body(p_ref, c_ref, o_ref):
        tot = ((p_ref[0] + p_ref[1]) + p_ref[2]) + p_ref[3]
        o_ref[...] = tot[0:1] * _dsilu(c_ref[...].reshape(1, D))

    return _pallas_call(body, name="c_ctx_grad", out_shape=SDS((1, D), f32), in_specs=[VMEM_SPEC, VMEM_SPEC],
                        out_specs=VMEM_SPEC)(parts4, c_ctx)


def _adamw(w, g, m, v, name, after=None):
    rows, cols = w.shape
    tr = 256 if rows % 256 == 0 else rows

    def body(w_ref, g_ref, m_ref, v_ref, after_ref, d_ref, m2_ref, v2_ref):
        gv = g_ref[...]
        m2 = ADAM_B1 * m_ref[...] + (1.0 - ADAM_B1) * gv
        v2 = ADAM_B2 * v_ref[...] + (1.0 - ADAM_B2) * jnp.square(gv)
        m_hat = m2 / (1.0 - ADAM_B1 ** ADAM_STEP)
        v_hat = v2 / (1.0 - ADAM_B2 ** ADAM_STEP)
        d_ref[...] = -ADAM_LR * (m_hat / (jnp.sqrt(v_hat) + ADAM_EPS) + ADAM_WD * w_ref[...])
        m2_ref[...] = m2
        v2_ref[...] = v2

    spec = pl.BlockSpec((tr, cols), lambda i: (i, 0))
    shp = SDS((rows, cols), f32)
    return _hbm_call(body, name=name, out_shape=(shp, shp, shp), grid=(rows // tr,), in_specs=[spec] * 4 + [ANY_SPEC],
                     out_specs=(spec, spec, spec))(w, g, m, v, g if after is None else after)


def _adamw_halves(w, g_mine, g_other, m, v, cvec, name, after=None):
    rows, cols = w.shape
    half = rows // 2
    tr = min(256, half)
    per_half = half // tr

    def body(c_ref, w_ref, ga_ref, gb_ref, m_ref, v_ref, after_ref, g_ref, d_ref, m2_ref, v2_ref):
        in_my_half = (pl.program_id(0) // per_half) == c_ref[0]
        gv = jnp.where(in_my_half, ga_ref[...], gb_ref[...])
        g_ref[...] = gv
        m2 = ADAM_B1 * m_ref[...] + (1.0 - ADAM_B1) * gv
        v2 = ADAM_B2 * v_ref[...] + (1.0 - ADAM_B2) * jnp.square(gv)
        m_hat = m2 / (1.0 - ADAM_B1 ** ADAM_STEP)
        v_hat = v2 / (1.0 - ADAM_B2 ** ADAM_STEP)
        d_ref[...] = -ADAM_LR * (m_hat / (jnp.sqrt(v_hat) + ADAM_EPS) + ADAM_WD * w_ref[...])
        m2_ref[...] = m2
        v2_ref[...] = v2

    full = pl.BlockSpec((tr, cols), lambda i, c: (i, 0))
    part = pl.BlockSpec((tr, cols), lambda i, c: (i % per_half, 0))
    shp = SDS((rows, cols), f32)
    grid_spec = pltpu.PrefetchScalarGridSpec(num_scalar_prefetch=1, grid=(rows // tr,),
                                             in_specs=[full, part, part, full, full, ANY_SPEC], out_specs=(full,) * 4)
    return _hbm_call(body, name=name, out_shape=(shp,) * 4, grid_spec=grid_spec)(
        cvec, w, g_mine, g_other, m, v, cvec if after is None else after)


def _adam_math(w, g, m, v):
    m2 = ADAM_B1 * m + (1.0 - ADAM_B1) * g
    v2 = ADAM_B2 * v + (1.0 - ADAM_B2) * jnp.square(g)
    m_hat = m2 / (1.0 - ADAM_B1 ** ADAM_STEP)
    v_hat = v2 / (1.0 - ADAM_B2 ** ADAM_STEP)
    return -ADAM_LR * (m_hat / (jnp.sqrt(v_hat) + ADAM_EPS) + ADAM_WD * w), m2, v2


def _adamw_small(red, g_c_ctx, jvec, ws, ms, vs):
    n = len(ws)

    def body(*refs):
        red_ref, gc_ref, j_ref = refs[:3]
        w_refs, m_refs, v_refs = refs[3:3 + n], refs[3 + n:3 + 2 * n], refs[3 + 2 * n:3 + 3 * n]
        outs = refs[3 + 3 * n:]
        g_out, d_out, m_out, v_out = outs[:n], outs[n:2 * n], outs[2 * n:3 * n], outs[3 * n:]
        chip = j_ref[0]
        lanes = lambda i: (slice(None), slice(128 * i, 128 * (i + 1)))
        row = lambda r0, i: (lambda: red_ref[r0 + i:r0 + i + 1, :])
        whole = (slice(None), slice(None))
        chunks = [
            [((slice(None),), lambda: gc_ref[...].reshape(D))],
            [(lanes(i), row(RS_B_ADA, i)) for i in range(3 * D // 128)],
            [(lanes(i), row(RS_NORM_G, i)) for i in range(D // 128)],
            [(whole, lambda: red_ref[RS_GQ:RS_GQ + 1, 0:DH])],
            [(whole, lambda: red_ref[RS_GK:RS_GK + 1, 0:DH])],
            [((dr,), (lambda dr=dr: red_ref[pl.ds(RS_RPB + dr, H, stride=N_DR), 0:N_DC])) for dr in range(N_DR)],
            [((r,), (lambda r=r: red_ref[pl.ds(RS_CONV_W + 4 * r + chip, 1), :])) for r in range(3)],
            [(lanes(i), row(RS_CONV_B, i)) for i in range(DC // 128)],
        ]
        for a in range(n):
            for idx, grad in chunks[a]:
                g = grad()
                d, m2, v2 = _adam_math(w_refs[a][idx], g, m_refs[a][idx], v_refs[a][idx])
                g_out[a][idx] = g
                d_out[a][idx] = d
                m_out[a][idx] = m2
                v_out[a][idx] = v2

    shapes = [SDS(w.shape, f32) for w in ws]
    res = _pallas_call(body, name="adamw_small", out_shape=shapes * 4,
                       in_specs=[VMEM_SPEC, VMEM_SPEC, SMEM_SPEC] + [VMEM_SPEC] * (3 * n),
                       out_specs=[VMEM_SPEC] * (4 * n))(red, g_c_ctx, jvec, *ws, *ms, *vs)
    return [list(res[k * n:(k + 1) * n]) for k in range(4)]


def _rows128(a):
    return a.reshape(-1, 128)


def kernel(x, c, ctx, c_ctx, w_ada, b_ada, norm_g, w_in, q_norm_g, k_norm_g, rpb, conv_w, conv_b, w_out, loss_target, m_c_ctx, m_w_ada, m_b_ada, m_norm_g, m_w_in, m_q_norm_g, m_k_norm_g, m_rpb, m_conv_w, m_conv_b, m_w_out, v_c_ctx, v_w_ada, v_b_ada, v_norm_g, v_w_in, v_q_norm_g, v_k_norm_g, v_rpb, v_conv_w, v_conv_b, v_w_out):
    xi, yi, ci = lax.axis_index("x"), lax.axis_index("y"), lax.axis_index("c")
    dev = 4 * xi + 2 * yi + ci
    chip = 2 * xi + yi
    cvec = jnp.reshape(ci, (1,)).astype(i32)
    jvec = jnp.reshape(chip, (1,)).astype(i32)
    w_ada_s = w_ada[0]
    ncol = w_ada_s.shape[1]

    gc = _split_start([_to_slot(c.reshape(8, 128), 8, dev)], [], 7, _gather8_copies, "gather_c_start")
    wo4c = _cast_to_slot(w_out[0], jvec, "cast_w_out", after=gc[3])
    w4c = _cast_to_slot(w_in[0], jvec, "cast_w_in", after=wo4c)
    (c8,), _ = _split_wait(gc[0], gc[1], [gc[2]], [], w4c, _gather8_copies, "gather_c_wait")
    cc = jnp.concatenate([c8.reshape(8, D), c_ctx.reshape(1, D), jnp.zeros((7, D), f32)], axis=0)
    m_shard, sc16 = _adaln_shard(cc, w_ada_s)

    conv_w_pad = jnp.pad(conv_w[0], ((0, 5), (0, 0)))
    gm = _split_start([_to_slot(m_shard, 4, chip), _to_slot(conv_w_pad, 4, chip)], [], 6, _gather4_copies,
                      "gather_mod_start")

    all_k = [(0, 0, 0), (0, 0, 1), (0, 0, 2)]
    sem_a, rem_a, w4s, token = _split_start([w4c], [], 1, _near_copies, "weights_near_start", after=gm[4])
    (m4, cw4), _ = _split_wait(gm[0], gm[1], [gm[2], gm[3]], [], token, _gather4_copies, "gather_mod_wait")
    m_full = jnp.transpose(m4, (1, 0, 2)).reshape(16, 4 * ncol)
    mrow = lax.dynamic_slice(m_full, (dev, 0), (1, 3 * D))
    mrow_c = m_full[8:9]
    conv_w_full = jnp.transpose(cw4[:, 0:3, :], (1, 0, 2)).reshape(3, DC)
    waves = {}

    def near(after):
        (w4w,), _ = _split_wait(sem_a, rem_a, [w4s], [], after, _near_copies, "weights_near_wait")
        sem_b, rem_b, w4b, started = _split_start([w4w], [], 2, _pass_relay_copies, "weights_pass_start")
        waves["pass"] = (sem_b, rem_b)
        return w4b, started

    def near2(w4, after):
        (w4w,), _ = _split_wait(*waves["pass"], [w4], [], after, _pass_copy, "weights_pass_wait")
        return w4w

    def far(w4, after):
        (w4w,), _ = _split_wait(*waves["pass"], [w4], [], after, _relay_copy, "weights_far_wait")
        w4x, sem_c, rem_c, wo4s, started = _forward_then_start(w4w, wo4c, all_k, "weights_far_forward_out_start")
        (w4f,), _ = _split_wait(sem_c, rem_c, [w4x], [], started, _diag_forward_copy, "weights_far_forward_wait")
        waves["out"] = (sem_c, rem_c, wo4s)
        return w4f, started

    def w_out_arrived(after):
        sem_c, rem_c, wo4s = waves["out"]
        (wow,) = _halves_wait(sem_c, rem_c, [wo4s], after, all_k, "weights_out_wait")
        sem_d, rem_d, wof, started = _split_start([wow], [], 3, _forward_copies, "weights_out_forward_start")
        waves["out_forward"] = (sem_d, rem_d, wof)
        return started

    def w_out_gathered(after):
        sem_d, rem_d, wof = waves["out_forward"]
        (wo,), _ = _split_wait(sem_d, rem_d, [wof], [], after, _forward_copies, "weights_out_forward_wait")
        return wo.reshape(D, D)

    weights = dict(own=w_in[0], jvec=jvec, started=token, first=jvec ^ (1 + cvec), second=jvec ^ (2 - cvec),
                   near=near, near2=near2, far=far, out_arrived=w_out_arrived, out=w_out_gathered)

    exchange = _exchange_copies
    pending = {}

    def on_g_w_out(g_w_out):
        out = _split_start([g_w_out], [SDS((4, D // 8, D), f32)], 4, exchange, "grad_out_pair_start")
        pending["ex_out"] = out
        return out[4]

    def after_conv(dp8):
        ssem_o, rsem_o, g_o, land_o, _ = pending["ex_out"]
        (g_o,), (ex_o,) = _split_wait(ssem_o, rsem_o, [g_o], [land_o], dp8, exchange, "grad_out_pair_wait")
        to32, tob = _pair_sum_w_out(g_o, ex_o, cvec)
        out = _split_start([tob], [SDS((3, D // 8, D), bf16)], 3, _scatter_copies, "grad_out_chip_start")
        pending["sc_out"] = (out, to32)
        return out[4]

    def on_g_w_in_first(g_first):
        out = _split_start([g_first], [SDS((4, D // 2, D), f32)], 4, _block_exchange_copies, "grad_pair_start")
        pending["ex"] = (out[0], out[1], [out[2]], [out[3]])
        return out[4]

    def on_g_w_in_second(g_second):
        ex_ssem, ex_rsem, ex_srcs, ex_lands = pending["ex"]
        _, ex = _split_wait(ex_ssem, ex_rsem, ex_srcs, ex_lands, g_second, _block_exchange_copies, "grad_pair_wait")
        t32, tb = _pair_sum_w_in(g_second, ex[0])
        out = _split_start([tb], [SDS((3, D // 2, D), bf16)], 3, _scatter_copies, "grad_chip_start")
        pending["sc_in"] = (out, t32)
        return out[4]

    r = _local_step(x[0], ctx[0], loss_target[0], mrow, mrow_c, b_ada, norm_g, weights, q_norm_g, k_norm_g,
                    rpb[0], conv_w_full, conv_b,
                    dict(g_w_out=on_g_w_out, after_conv=after_conv, first_half=1 - cvec,
                         g_w_in_first=on_g_w_in_first, g_w_in_second=on_g_w_in_second))
    sc_in, t32 = pending["sc_in"]
    _, (r2,) = _split_wait(sc_in[0], sc_in[1], [sc_in[2]], [sc_in[3]], r["dng"], _scatter_copies, "grad_chip_wait")
    u_in = _chip_sum(t32, r2, jvec, "chip_sum_w_in")
    sc_o, to32 = pending["sc_out"]
    _, (ro2,) = _split_wait(sc_o[0], sc_o[1], [sc_o[2]], [sc_o[3]], u_in, _scatter_copies, "grad_out_chip_wait")
    u_out = _chip_sum(to32, ro2, jvec, "chip_sum_w_out")
    o_in, o_out = _pair_swap([u_in, u_out], "grad_pair_swap")

    dm = jnp.concatenate([r["dshift"], r["dscale"], r["dgate"]], axis=1)
    dmc = jnp.concatenate([r["dshift_c"], r["dscale_c"], jnp.zeros((1, D), f32)], axis=1)
    pack_parts = [_rows128(dm), _rows128(dmc), _rows128(r["dng"]), _rows128(r["dng_c"]), _rows128(r["g_gq"]),
                  _rows128(r["g_gk"]), _rows128(r["g_gk_c"]), r["g_rpb"].reshape(H * N_DR, 128),
                  _rows128(r["g_conv_b"]), _rows128(r["g_conv_w"][0:3]), jnp.pad(r["loss_sum"], ((0, 0), (0, 127)))]
    pack = jnp.concatenate([jnp.pad(p, ((0, -p.shape[0] % 8), (0, 0))) for p in pack_parts], axis=0)
    assert pack.shape[0] == PK_ROWS
    gs = _split_start([_to_slot(pack, 8, dev)], [], 7, _gather8_copies, "gather_small_start", after=o_out)
    g_w_in_s, d_w_in, nm_w_in, nv_w_in = _adamw_halves(w_in[0], u_in, o_in, m_w_in[0], v_w_in[0], cvec, "adamw_w_in",
                                                       after=gs[3])
    g_w_out_s, d_w_out, nm_w_out, nv_w_out = _adamw_halves(w_out[0], u_out, o_out, m_w_out[0], v_w_out[0], cvec,
                                                           "adamw_w_out", after=nm_w_in)
    (gathered,), _ = _split_wait(gs[0], gs[1], [gs[2]], [], nm_w_out, _gather8_copies, "gather_small_wait")
    red, dm16 = _small_reduce(gathered)
    loss = red[RS_LOSS, 0] * (0.5 / D)

    g_w_ada_s, cpart = _w_ada_grad(sc16, dm16, w_ada_s, jvec)
    gcp = _split_start([_to_slot(cpart, 4, chip)], [], 3, _gather4_copies, "gather_c_ctx_parts_start")
    d_w_ada, nm_w_ada, nv_w_ada = _adamw(w_ada_s, g_w_ada_s, m_w_ada[0], v_w_ada[0], "adamw_w_ada", after=gcp[3])
    (cparts4,), _ = _split_wait(gcp[0], gcp[1], [gcp[2]], [], nm_w_ada, _gather4_copies, "gather_c_ctx_parts_wait")
    g_c_ctx = _c_ctx_grad(cparts4, c_ctx)

    t_rpb = lambda a: jnp.transpose(a, (0, 2, 1, 3)).reshape(N_DR, H, N_DC)
    t_cw = lambda a: jnp.transpose(a, (1, 0, 2))
    small = _adamw_small(
        red, g_c_ctx, jvec,
        [c_ctx, b_ada, norm_g, q_norm_g, k_norm_g, t_rpb(rpb), t_cw(conv_w), conv_b],
        [m_c_ctx, m_b_ada, m_norm_g, m_q_norm_g, m_k_norm_g, t_rpb(m_rpb), t_cw(m_conv_w), m_conv_b],
        [v_c_ctx, v_b_ada, v_norm_g, v_q_norm_g, v_k_norm_g, t_rpb(v_rpb), t_cw(v_conv_w), v_conv_b])
    for kind in small:
        kind[5] = jnp.transpose(kind[5].reshape(1, N_DR, H, N_DC), (0, 2, 1, 3))
        kind[6] = jnp.transpose(kind[6], (1, 0, 2))

    def ordered(kind, big_w_ada, big_w_in, big_w_out):
        s_c_ctx, s_b_ada, s_norm_g, s_q, s_k, s_rpb, s_conv_w, s_conv_b = small[kind]
        return [s_c_ctx, big_w_ada[None], s_b_ada, s_norm_g, big_w_in[None], s_q, s_k, s_rpb, s_conv_w,
                s_conv_b, big_w_out[None]]

    grads = ordered(0, g_w_ada_s, g_w_in_s, g_w_out_s)
    deltas = ordered(1, d_w_ada, d_w_in, d_w_out)
    new_m = ordered(2, nm_w_ada, nm_w_in, nm_w_out)
    new_v = ordered(3, nv_w_ada, nv_w_in, nv_w_out)
    return (loss, r["grad_x"][None], *grads, *deltas, *new_m, *new_v)
```

```python
import functools

import jax
import jax.numpy as jnp
import numpy as np
from jax import lax
from jax.experimental import pallas as pl
from jax.experimental.pallas import tpu as pltpu

f32, bf16, i32 = jnp.float32, jnp.bfloat16, jnp.int32
MESH = pl.DeviceIdType.MESH
HIGHEST = lax.Precision.HIGHEST

D = 1024
S = 2048
L = 256
GW = 64
ROWS = S // GW
H = 8
DH = 64
DA = H * DH
DC = 512
WIN_H, WIN_W = 8, 16
N_DR, N_DC = 2 * WIN_H - 1, 2 * WIN_W - 1
RMS_EPS = 1e-6
ROPE_THETA = 10000.0
QK_SCALE = DH ** -0.5
NEG = -1e30

QB = 128
NQB = S // QB
KR = 9
KB = KR * GW
TILE_GEOM = ((0, 0), (2, 0), (4, 0), (28, 23), (30, 23))
NT = len(TILE_GEOM)

ADAM_LR, ADAM_B1, ADAM_B2, ADAM_EPS, ADAM_WD, ADAM_STEP = 0.001, 0.9, 0.999, 1e-08, 0.01, 10

VMEM_SPEC = pl.BlockSpec(memory_space=pltpu.VMEM)
ANY_SPEC = pl.BlockSpec(memory_space=pl.ANY)
SMEM_SPEC = pl.BlockSpec(memory_space=pltpu.SMEM)
SDS = jax.ShapeDtypeStruct


_pallas_call = pl.pallas_call


def _hbm_call(body, *, out_shape, in_specs=None, out_specs=None, grid_spec=None, **kw):
    n_pre = 0
    if grid_spec is not None:
        ispecs, ospecs, n_pre = grid_spec.in_specs, grid_spec.out_specs, grid_spec.num_scalar_prefetch
        kw["grid_spec"] = grid_spec
    else:
        ispecs, ospecs = in_specs, out_specs
        kw.update(in_specs=in_specs, out_specs=out_specs)

    def blocked(spec):
        return isinstance(spec, pl.BlockSpec) and spec.block_shape is not None

    single = not isinstance(out_shape, (tuple, list))
    shapes = [out_shape] if single else list(out_shape)
    ospec_list = list(ospecs) if isinstance(ospecs, (tuple, list)) else [ospecs]
    shapes = [pltpu.HBM(s.shape, s.dtype) if blocked(sp) else s for s, sp in zip(shapes, ospec_list)]
    call = _pallas_call(body, out_shape=shapes[0] if single else tuple(shapes), **kw)

    def run(*args):
        arrays = [pltpu.with_memory_space_constraint(a, pltpu.HBM) if blocked(sp) else a
                  for a, sp in zip(args[n_pre:], ispecs)]
        return call(*args[:n_pre], *arrays)

    return run


def _cp(vmem_mb=None, **kw):
    if vmem_mb is not None:
        kw["vmem_limit_bytes"] = vmem_mb << 20
    return pltpu.CompilerParams(**kw)


def _silu(z):
    return z * jax.nn.sigmoid(z)


def _dsilu(z):
    sg = jax.nn.sigmoid(z)
    return sg * (1.0 + z * (1.0 - sg))


def _row_start(i):
    return min(max(i - WIN_H // 2, 0), ROWS - WIN_H)


def _my_pos():
    return lax.axis_index("x"), lax.axis_index("y"), lax.axis_index("c")


def _flip(v, bit):
    return 1 - v if bit else v


def _swap_copies(srcs, lands, ssem, rsem):
    x, y, c = _my_pos()
    return [pltpu.make_async_remote_copy(src_ref=srcs[a], dst_ref=lands[a], send_sem=ssem.at[a], recv_sem=rsem.at[a],
                                         device_id=(x, y, 1 - c), device_id_type=MESH) for a in range(len(srcs))]


HBM_SPEC = pl.BlockSpec(memory_space=pltpu.HBM)
SEM_SPEC = pl.BlockSpec(memory_space=pltpu.SEMAPHORE)
DATAFLOW = pltpu.SideEffectType.DATAFLOW_SIDE_EFFECTING


def _peer_chips(x, y, c):
    out = []
    for k in range(1, 4):
        px, py = _flip(x, (k >> 1) & 1), _flip(y, k & 1)
        out.append(((px, py, c), 2 * px + py))
    return out


def _half_copies(srcs, dsts, ssem, rsem, which):
    x, y, c = _my_pos()
    j = 2 * x + y
    peers = _peer_chips(x, y, c)
    pairs = []
    for pos, group, k in which:
        half = srcs[pos].shape[1] // 2
        mine = pl.ds(pl.multiple_of(c * half, 8), half)
        dev, pj = peers[k]
        sem = 3 * group + k
        send = pltpu.make_async_remote_copy(src_ref=srcs[pos].at[j, mine], dst_ref=dsts[pos].at[j, mine],
                                            send_sem=ssem.at[sem], recv_sem=rsem.at[sem], device_id=dev,
                                            device_id_type=MESH)
        arrive = pltpu.make_async_remote_copy(src_ref=srcs[pos].at[j, mine], dst_ref=dsts[pos].at[pj, mine],
                                              send_sem=ssem.at[sem], recv_sem=rsem.at[sem], device_id=dev,
                                              device_id_type=MESH)
        pairs.append((send, arrive))
    return pairs


def _halves_wait(ssem, rsem, bigs, after, which, name):
    nb = len(bigs)

    def body(*refs):
        b_in = refs[:nb]
        ssem_ref, rsem_ref = refs[nb], refs[nb + 1]
        for send, arrive in _half_copies(b_in, b_in, ssem_ref, rsem_ref, which):
            send.wait_send()
            arrive.wait_recv()

    return _hbm_call(
        body, name=name, out_shape=tuple(pltpu.HBM(b.shape, b.dtype) for b in bigs),
        in_specs=[HBM_SPEC] * nb + [SEM_SPEC, SEM_SPEC, ANY_SPEC], out_specs=tuple([HBM_SPEC] * nb),
        input_output_aliases={a: a for a in range(nb)}, compiler_params=_cp(has_side_effects=DATAFLOW),
    )(*bigs, ssem, rsem, after)


FORWARD_SEM = 3


def _diag_forward_copy(srcs, dsts, ssem, rsem):
    x, y, c = _my_pos()
    half = srcs[0].shape[1] // 2
    diag = 3 - (2 * x + y)
    mine = pl.ds(pl.multiple_of(c * half, 8), half)
    other = pl.ds(pl.multiple_of((1 - c) * half, 8), half)
    return [_Copy(srcs[0].at[diag, mine], dsts[0].at[diag, mine], dsts[0].at[diag, other], ssem.at[FORWARD_SEM],
                  rsem.at[FORWARD_SEM], (x, y, 1 - c))]


def _forward_then_start(fwd, big, order, name):
    def body(f_in, b_in, f_out, ssem, rsem, b_out, token):
        _diag_forward_copy([f_in], [f_out], ssem, rsem)[0].start()
        for send, _ in _half_copies([b_in], [b_out], ssem, rsem, order):
            send.start()
        token[...] = jnp.zeros_like(token)

    n_sem = FORWARD_SEM + 1
    out_shape = (pltpu.HBM(fwd.shape, fwd.dtype), pltpu.SemaphoreType.DMA((n_sem,)), pltpu.SemaphoreType.DMA((n_sem,)),
                 pltpu.HBM(big.shape, big.dtype), SDS((8, 128), f32))
    return _hbm_call(
        body, name=name, out_shape=out_shape, in_specs=[HBM_SPEC, HBM_SPEC],
        out_specs=(HBM_SPEC, SEM_SPEC, SEM_SPEC, HBM_SPEC, VMEM_SPEC), input_output_aliases={0: 0, 1: 3},
        compiler_params=_cp(has_side_effects=DATAFLOW),
    )(*[pltpu.with_memory_space_constraint(b, pltpu.HBM) for b in (fwd, big)])


def _cast_to_slot(w, jvec, name, after=None):
    rows, cols = w.shape
    tr = 256

    def body(j_ref, w_ref, after_ref, o_ref):
        o_ref[...] = w_ref[...].astype(bf16)

    grid_spec = pltpu.PrefetchScalarGridSpec(
        num_scalar_prefetch=1, grid=(rows // tr,),
        in_specs=[pl.BlockSpec((tr, cols), lambda i, j: (i, 0)), ANY_SPEC],
        out_specs=pl.BlockSpec((None, tr, cols), lambda i, j: (j[0], i, 0)))
    return _hbm_call(body, name=name, out_shape=SDS((4, rows, cols), bf16),
                     grid_spec=grid_spec)(jvec, w, jvec if after is None else after)


def _exchange_copies(srcs, lands, ssem, rsem):
    x, y, c = _my_pos()
    half = srcs[0].shape[0] // 8
    cps = []
    for jb in range(4):
        src = srcs[0].at[pl.ds(pl.multiple_of((2 * jb + 1 - c) * half, 8), half)]
        cps.append(pltpu.make_async_remote_copy(src_ref=src, dst_ref=lands[0].at[jb], send_sem=ssem.at[jb],
                                                recv_sem=rsem.at[jb], device_id=(x, y, 1 - c), device_id_type=MESH))
    return cps


def _block_exchange_copies(srcs, lands, ssem, rsem):
    x, y, c = _my_pos()
    return [pltpu.make_async_remote_copy(src_ref=srcs[0].at[jb], dst_ref=lands[0].at[jb], send_sem=ssem.at[jb],
                                         recv_sem=rsem.at[jb], device_id=(x, y, 1 - c), device_id_type=MESH)
            for jb in range(4)]


def _scatter_copies(srcs, lands, ssem, rsem):
    x, y, c = _my_pos()
    cps = []
    for a in range(len(srcs)):
        for k, (dev, pj) in enumerate(_peer_chips(x, y, c)):
            cps.append(pltpu.make_async_remote_copy(src_ref=srcs[a].at[pj], dst_ref=lands[a].at[k],
                                                    send_sem=ssem.at[3 * a + k], recv_sem=rsem.at[3 * a + k],
                                                    device_id=dev, device_id_type=MESH))
    return cps


class _Copy:
    def __init__(self, src, dst, arrive, ssem, rsem, dev):
        make = lambda to: pltpu.make_async_remote_copy(src_ref=src, dst_ref=to, send_sem=ssem, recv_sem=rsem,
                                                       device_id=dev, device_id_type=MESH)
        send, arrival = make(dst), make(arrive)
        self.start, self.wait_send, self.wait_recv = send.start, send.wait_send, arrival.wait_recv


def _toward(x, y, along_x):
    return x + along_x * (1 - 2 * x), y + (1 - along_x) * (1 - 2 * y)


def _near_copies(srcs, dsts, ssem, rsem):
    x, y, c = _my_pos()
    px, py = _toward(x, y, c)
    j = 2 * x + y
    return [_Copy(srcs[0].at[j], dsts[0].at[j], dsts[0].at[2 * px + py], ssem.at[0], rsem.at[0], (px, py, c))]


def _pass_copy(srcs, dsts, ssem, rsem):
    x, y, c = _my_pos()
    px, py = _toward(x, y, c)
    qx, qy = _toward(x, y, 1 - c)
    got = 2 * px + py
    return [_Copy(srcs[0].at[got], dsts[0].at[got], dsts[0].at[2 * qx + qy], ssem.at[0], rsem.at[0], (x, y, 1 - c))]


def _relay_copy(srcs, dsts, ssem, rsem):
    x, y, c = _my_pos()
    px, py = _toward(x, y, c)
    qx, qy = _toward(x, y, 1 - c)
    half = srcs[0].shape[1] // 2
    mine = pl.ds(pl.multiple_of(c * half, 8), half)
    got, diag = 2 * px + py, 3 - (2 * x + y)
    return [_Copy(srcs[0].at[got, mine], dsts[0].at[got, mine], dsts[0].at[diag, mine], ssem.at[1], rsem.at[1],
                  (qx, qy, c))]


def _forward_copies(srcs, dsts, ssem, rsem):
    x, y, c = _my_pos()
    half = srcs[0].shape[1] // 2
    mine = pl.ds(pl.multiple_of(c * half, 8), half)
    other = pl.ds(pl.multiple_of((1 - c) * half, 8), half)
    return [_Copy(srcs[0].at[pj, mine], dsts[0].at[pj, mine], dsts[0].at[pj, other], ssem.at[k], rsem.at[k],
                  (x, y, 1 - c)) for k, (_, pj) in enumerate(_peer_chips(x, y, c))]


def _pass_relay_copies(srcs, dsts, ssem, rsem):
    return _pass_copy(srcs, dsts, ssem, rsem) + _relay_copy(srcs, dsts, ssem, rsem)


def _gather8_copies(srcs, dsts, ssem, rsem):
    x, y, c = _my_pos()
    me = 4 * x + 2 * y + c
    cps = []
    for a in range(len(srcs)):
        for k in range(1, 8):
            tgt = (_flip(x, (k >> 2) & 1), _flip(y, (k >> 1) & 1), _flip(c, k & 1))
            cps.append(_Copy(srcs[a].at[me], dsts[a].at[me], dsts[a].at[4 * tgt[0] + 2 * tgt[1] + tgt[2]],
                             ssem.at[7 * a + k - 1], rsem.at[7 * a + k - 1], tgt))
    return cps


def _gather4_copies(srcs, dsts, ssem, rsem):
    x, y, c = _my_pos()
    j = 2 * x + y
    cps = []
    for a in range(len(srcs)):
        for k, (dev, pj) in enumerate(_peer_chips(x, y, c)):
            cps.append(_Copy(srcs[a].at[j], dsts[a].at[j], dsts[a].at[pj], ssem.at[3 * a + k], rsem.at[3 * a + k], dev))
    return cps


def _to_slot(a, n, i):
    return lax.dynamic_update_slice(jnp.zeros((n,) + a.shape, a.dtype), a[None], (i,) + (0,) * a.ndim)


def _split_start(srcs, land_shapes, n_cp, make, name, after=None):
    ns, nl = len(srcs), len(land_shapes)
    n_in = ns + nl + (after is not None)

    def body(*refs):
        s_in = refs[:ns]
        ssem, rsem = refs[n_in], refs[n_in + 1]
        s_out = refs[n_in + 2:n_in + 2 + ns]
        l_out = refs[n_in + 2 + ns:n_in + 2 + ns + nl]
        token = refs[n_in + 2 + ns + nl]
        for cp in make(s_in, l_out if nl else s_out, ssem, rsem):
            cp.start()
        token[...] = jnp.zeros_like(token)

    lands = [pltpu.with_memory_space_constraint(lax.empty(sh.shape, sh.dtype), pltpu.HBM) for sh in land_shapes]
    out_shape = (pltpu.SemaphoreType.DMA((n_cp,)), pltpu.SemaphoreType.DMA((n_cp,)),
                 *[pltpu.HBM(b.shape, b.dtype) for b in srcs], *[pltpu.HBM(b.shape, b.dtype) for b in land_shapes],
                 SDS((8, 128), f32))
    return _hbm_call(
        body, name=name, out_shape=out_shape, in_specs=[HBM_SPEC] * (ns + nl) + [ANY_SPEC] * (after is not None),
        out_specs=(SEM_SPEC, SEM_SPEC, *[HBM_SPEC] * (ns + nl), VMEM_SPEC),
        input_output_aliases={i: 2 + i for i in range(ns + nl)}, compiler_params=_cp(has_side_effects=DATAFLOW),
    )(*[pltpu.with_memory_space_constraint(b, pltpu.HBM) for b in srcs], *lands, *([] if after is None else [after]))


def _split_wait(ssem, rsem, srcs, lands, after, make, name):
    ns, nl = len(srcs), len(lands)

    def body(*refs):
        s_in, l_in = refs[:ns], refs[ns:ns + nl]
        ssem_ref, rsem_ref = refs[ns + nl], refs[ns + nl + 1]
        for cp in make(s_in, l_in if nl else s_in, ssem_ref, rsem_ref):
            cp.wait_send()
            cp.wait_recv()

    outs = _hbm_call(
        body, name=name, out_shape=tuple(pltpu.HBM(b.shape, b.dtype) for b in (*srcs, *lands)),
        in_specs=[HBM_SPEC] * (ns + nl) + [SEM_SPEC, SEM_SPEC, ANY_SPEC], out_specs=tuple([HBM_SPEC] * (ns + nl)),
        input_output_aliases={i: i for i in range(ns + nl)}, compiler_params=_cp(has_side_effects=DATAFLOW),
    )(*srcs, *lands, ssem, rsem, after)
    return list(outs[:ns]), list(outs[ns:])


def _adaln_shard(cc, w_ada_shard):
    def body(c_ref, w_ref, m_ref, sc_ref):
        sc = _silu(c_ref[...])
        sc_ref[...] = sc
        m_ref[...] = jnp.dot(sc, w_ref[...], precision=HIGHEST, preferred_element_type=f32)

    return _hbm_call(
        body, name="adaln_shard", out_shape=(SDS((16, w_ada_shard.shape[1]), f32), SDS((16, D), f32)),
        in_specs=[VMEM_SPEC, VMEM_SPEC], out_specs=(VMEM_SPEC, VMEM_SPEC), compiler_params=_cp(32),
    )(cc, w_ada_shard)


def _prenorm(xx, norm_g, mrow, b_ada, tm, name, after=None):
    n = xx.shape[0]

    def body(x_ref, g_ref, m_ref, b_ref, after_ref, h_ref):
        x = x_ref[...]
        shift = m_ref[:, 0:D] + b_ref[:, 0:D]
        scale = m_ref[:, D:2 * D] + b_ref[:, D:2 * D]
        r = lax.rsqrt(jnp.mean(x * x, axis=-1, keepdims=True) + RMS_EPS)
        y = (x * r) * g_ref[...]
        h_ref[...] = (y * (1.0 + scale) + shift).astype(bf16)

    row = lambda i: (i, 0)
    fixed = lambda i: (0, 0)
    return _hbm_call(
        body, name=name, out_shape=SDS((n, D), bf16), grid=(n // tm,),
        in_specs=[pl.BlockSpec((tm, D), row), pl.BlockSpec((1, D), fixed), pl.BlockSpec((1, 3 * D), fixed),
                  pl.BlockSpec((1, 3 * D), fixed), ANY_SPEC],
        out_specs=pl.BlockSpec((tm, D), row),
    )(xx, norm_g, mrow, b_ada, b_ada if after is None else after)


def _in_proj_own(h, w_own, jvec):
    tm = 512

    def body(j_ref, h_ref, w_ref, p_ref):
        p_ref[...] = jnp.dot(h_ref[...], w_ref[...].astype(bf16), preferred_element_type=f32)

    grid_spec = pltpu.PrefetchScalarGridSpec(
        num_scalar_prefetch=1, grid=(S // tm,),
        in_specs=[pl.BlockSpec((tm, D), lambda i, j: (i, 0)), pl.BlockSpec((D, D), lambda i, j: (0, 0))],
        out_specs=pl.BlockSpec((tm, D), lambda i, j: (i, j[0])))
    return _hbm_call(body, name="in_proj_own", out_shape=SDS((S, 4 * D), f32), grid_spec=grid_spec,
                     compiler_params=_cp(40))(jvec, h, w_own)


def _in_proj_block(h, w4, p, bvec, name, after=None):
    tm = 512

    def body(b_ref, h_ref, w_ref, p_in_ref, after_ref, p_ref):
        p_ref[...] = jnp.dot(h_ref[...], w_ref[...], preferred_element_type=f32)

    grid_spec = pltpu.PrefetchScalarGridSpec(
        num_scalar_prefetch=1, grid=(S // tm,),
        in_specs=[pl.BlockSpec((tm, D), lambda i, b: (i, 0)), pl.BlockSpec((None, D, D), lambda i, b: (b[0], 0, 0)),
                  ANY_SPEC, ANY_SPEC],
        out_specs=pl.BlockSpec((tm, D), lambda i, b: (i, b[0])))
    return _hbm_call(body, name=name, out_shape=SDS((S, 4 * D), f32), grid_spec=grid_spec,
                     input_output_aliases={3: 0})(bvec, h, w4, p, bvec if after is None else after)


def _ctx_proj(hc, w4):
    def body(h_ref, w0_ref, w1_ref, p_ref):
        hv = h_ref[...]
        p_ref[:, 0:DA] = jnp.dot(hv, w0_ref[:, DA:2 * DA], preferred_element_type=f32)
        p_ref[:, DA:2 * DA] = jnp.dot(hv, w1_ref[:, 0:DA], preferred_element_type=f32)

    return _hbm_call(
        body, name="ctx_proj", out_shape=SDS((L, 2 * DA), f32), grid=(1,),
        in_specs=[pl.BlockSpec((L, D), lambda i: (0, 0)), pl.BlockSpec((None, D, D), lambda i: (0, 0, 0)),
                  pl.BlockSpec((None, D, D), lambda i: (1, 0, 0))],
        out_specs=pl.BlockSpec((L, 2 * DA), lambda i: (0, 0)),
    )(hc, w4, w4)


def _head_ones():
    r = lax.broadcasted_iota(i32, (DA, DA), 0) // DH
    c = lax.broadcasted_iota(i32, (DA, DA), 1) // DH
    return (r == c).astype(bf16)


def _head_sum(v, ones_bd):
    hi = v.astype(bf16)
    lo = (v - hi.astype(f32)).astype(bf16)
    return jnp.dot(hi, ones_bd, preferred_element_type=f32) + jnp.dot(lo, ones_bd, preferred_element_type=f32)


def _swap16(v):
    lane = lax.broadcasted_iota(i32, v.shape, 1)
    return jnp.where((lane & 31) < 16, pltpu.roll(v, DA - 16, 1), pltpu.roll(v, 16, 1))


def _rope_block(ct_ref, rt_ref, tm):
    rows = [jnp.tile(rt_ref[8 * j:8 * j + 8, :], (GW // 8, 1)) for j in range(tm // GW)]
    return jnp.tile(ct_ref[...], (tm // GW, 1)) + jnp.concatenate(rows, axis=0)


def _rope_specs(tm):
    col = pl.BlockSpec((GW, DA), lambda i: (0, 0))
    row = pl.BlockSpec((8 * tm // GW, DA), lambda i: (i, 0))
    return [col, row, col, row]


def _qk_prep(p, gq, gk, rope):
    tm = 256

    def body(qk_ref, v_ref, gq_ref, gk_ref, cc_ref, cr_ref, sc_ref, sr_ref, qr_ref, qp_ref, kr_ref, vh_ref):
        ones_bd = _head_ones()
        cs, sn = _rope_block(cc_ref, cr_ref, tm), _rope_block(sc_ref, sr_ref, tm)
        q = qk_ref[:, 0:DA]
        k = qk_ref[:, DA:2 * DA]
        yq = (q * lax.rsqrt(_head_sum(q * q, ones_bd) * (1.0 / DH) + RMS_EPS)) * gq_ref[...]
        yk = (k * lax.rsqrt(_head_sum(k * k, ones_bd) * (1.0 / DH) + RMS_EPS)) * gk_ref[...]
        qr = (yq * cs + _swap16(yq) * sn) * QK_SCALE
        qp = yq * QK_SCALE
        kr = yk * cs + _swap16(yk) * sn
        vv = v_ref[...]
        for hh in range(H):
            sl = slice(hh * DH, (hh + 1) * DH)
            qr_ref[hh] = qr[:, sl].astype(bf16)
            qp_ref[hh] = qp[:, sl].astype(bf16)
            kr_ref[hh] = kr[:, sl].astype(bf16)
            vh_ref[hh] = vv[:, sl].astype(bf16)

    hm = SDS((H, S, DH), bf16)
    hspec = pl.BlockSpec((H, tm, DH), lambda i: (0, i, 0))
    fixed = lambda i: (0, 0)
    return _hbm_call(
        body, name="qk_prep", out_shape=(hm, hm, hm, hm), grid=(S // tm,),
        in_specs=[pl.BlockSpec((tm, 2 * DA), lambda i: (i, 0)), pl.BlockSpec((tm, DA), lambda i: (i, 2)),
                  pl.BlockSpec((1, DA), fixed), pl.BlockSpec((1, DA), fixed)] + _rope_specs(tm),
        out_specs=(hspec, hspec, hspec, hspec),
    )(p, p, gq, gk, *rope)


def _ctx_prep(pc, gk):
    def body(p_ref, gk_ref, kc_ref, vc_ref):
        ones_bd = _head_ones()
        k = p_ref[:, 0:DA]
        yk = (k * lax.rsqrt(_head_sum(k * k, ones_bd) * (1.0 / DH) + RMS_EPS)) * gk_ref[...]
        vv = p_ref[:, DA:2 * DA]
        for hh in range(H):
            sl = slice(hh * DH, (hh + 1) * DH)
            kc_ref[hh] = yk[:, sl].astype(bf16)
            vc_ref[hh] = vv[:, sl].astype(bf16)

    hm = SDS((H, L, DH), bf16)
    return _hbm_call(
        body, name="ctx_prep", out_shape=(hm, hm), in_specs=[VMEM_SPEC, VMEM_SPEC], out_specs=(VMEM_SPEC, VMEM_SPEC),
    )(pc, gk)


def _tile_pieces():
    out = []
    for (i0, u0) in TILE_GEOM:
        rows = []
        for j in range(2):
            i = i0 + j
            rs = _row_start(i)
            rows.append([(u0 + u - i + WIN_H - 1) if rs <= u0 + u < rs + WIN_H else None for u in range(KR)])
        out.append(rows)
    return out


def _bias_prep(rpb_rev_pad, after=None):
    pieces = _tile_pieces()

    def body(r_ref, after_ref, o_ref):
        rp = r_ref[...]
        xs = jnp.broadcast_to(rp[:, None, :], (N_DR, GW, 128)).reshape(N_DR * GW, 128)
        row = lax.broadcasted_iota(i32, xs.shape, 0)
        lane = lax.broadcasted_iota(i32, xs.shape, 1)
        for b in range(6):
            xs = jnp.where(((row >> b) & 1) == 1, pltpu.roll(xs, 1 << b, 1), xs)
        xs = pltpu.roll(xs, 128 - (WIN_W - 1), 1)
        k = row & (GW - 1)
        c0 = jnp.clip(lane - WIN_W // 2, 0, GW - WIN_W)
        xs = jnp.where((k >= c0) & (k < c0 + WIN_W), xs, NEG)
        neg = jnp.full((GW, GW), NEG, f32)
        for t in range(NT):
            for j in range(2):
                for u in range(KR):
                    dr = pieces[t][j][u]
                    piece = neg if dr is None else xs[dr * GW:(dr + 1) * GW, 0:GW]
                    o_ref[t, u * GW:(u + 1) * GW, j * GW:(j + 1) * GW] = piece

    return _hbm_call(
        body, name="bias_prep", out_shape=SDS((H, NT, KB, QB), f32), grid=(H,),
        in_specs=[pl.BlockSpec((None, N_DR, 128), lambda h: (h, 0, 0)), ANY_SPEC],
        out_specs=pl.BlockSpec((None, NT, KB, QB), lambda h: (h, 0, 0, 0)),
    )(rpb_rev_pad, rpb_rev_pad if after is None else after)


def _bias_tiles(rpb2, after=None):
    return _bias_prep(jnp.pad(rpb2[:, :, ::-1], ((0, 0), (0, 0), (0, 128 - N_DC))), after)


def _block_geom(b):
    qs = b * QB
    ks = min(max(2 * b - 4, 0), ROWS - KR) * GW
    t = b if b < 2 else (b - (NQB - NT) if b > NQB - 3 else 2)
    return qs, ks, t


def _tt(a, b):
    return lax.dot_general(a, b, (((1,), (1,)), ((), ())), preferred_element_type=f32)


def _tn(a, b):
    return lax.dot_general(a, b, (((0,), (0,)), ((), ())), preferred_element_type=f32)


def _softmax_t(s_lat, s_ctx):
    m = jnp.maximum(jnp.max(s_lat, axis=0, keepdims=True), jnp.max(s_ctx, axis=0, keepdims=True))
    e_lat = jnp.exp(s_lat - m)
    e_ctx = jnp.exp(s_ctx - m)
    inv = 1.0 / (jnp.sum(e_lat, axis=0, keepdims=True) + jnp.sum(e_ctx, axis=0, keepdims=True))
    return e_lat * inv, e_ctx * inv


def _staged(n_blocks, stages):
    held = [dict() for _ in stages]
    for step in range(n_blocks + len(stages) - 1):
        for s, fn in enumerate(stages):
            b = step - s
            if 0 <= b < n_blocks:
                held[s][b] = fn(b) if s == 0 else fn(b, held[s - 1].pop(b))


def _attn_fwd(qr, qp, kr, vh, kc, vc, btt):
    def body(qr_ref, qp_ref, kr_ref, v_ref, kc_ref, vc_ref, bt_ref, o_ref):
        kcv, vcv = kc_ref[...], vc_ref[...]

        def scores(b):
            qs, ks, t = _block_geom(b)
            return (_tt(kr_ref[ks:ks + KB, :], qr_ref[qs:qs + QB, :]) + bt_ref[t], _tt(kcv, qp_ref[qs:qs + QB, :]))

        def probs(b, sc):
            p_lat, p_ctx = _softmax_t(*sc)
            return p_lat.astype(bf16), p_ctx.astype(bf16)

        def values(b, p):
            qs, ks, _ = _block_geom(b)
            o_ref[qs:qs + QB, :] = _tn(p[0], v_ref[ks:ks + KB, :]) + _tn(p[1], vcv)

        _staged(NQB, (scores, probs, values))

    sq = pl.BlockSpec((None, S, DH), lambda h: (h, 0, 0))
    sc = pl.BlockSpec((None, L, DH), lambda h: (h, 0, 0))
    return _hbm_call(
        body, name="attn_fwd", out_shape=SDS((H, S, DH), f32), grid=(H,),
        in_specs=[sq, sq, sq, sq, sc, sc, pl.BlockSpec((None, NT, KB, QB), lambda h: (h, 0, 0, 0))],
        out_specs=sq, compiler_params=_cp(48),
    )(qr, qp, kr, vh, kc, vc, btt)


def _shift_rows(v, down):
    n = v.shape[0]
    row = lax.broadcasted_iota(i32, v.shape, 0)
    if down:
        return jnp.where(row == 0, 0.0, pltpu.roll(v, 1, 0))
    return jnp.where(row == n - 1, 0.0, pltpu.roll(v, n - 1, 0))


def _conv_specs():
    col = lambda off: pl.BlockSpec((S, 128), lambda i, off=off: (0, off + i))
    return [col(16), col(20), col(24), col(28), pl.BlockSpec((3, 128), lambda i: (0, i)),
            pl.BlockSpec((1, 128), lambda i: (0, i))]


def _conv_fwd(p, conv_w, conv_b, after=None):
    def body(u_ref, bg_ref, cg_ref, zc_ref, w_ref, b_ref, after_ref, o_ref):
        cu = cg_ref[...] * u_ref[...]
        cv = b_ref[...] + _shift_rows(cu, True) * w_ref[0:1, :]
        cv = cv + cu * w_ref[1:2, :]
        cv = cv + _shift_rows(cu, False) * w_ref[2:3, :]
        o_ref[...] = ((bg_ref[...] * cv) * _silu(zc_ref[...])).astype(bf16)

    return _hbm_call(
        body, name="conv_fwd", out_shape=SDS((S, DC), bf16), grid=(DC // 128,),
        in_specs=_conv_specs() + [ANY_SPEC], out_specs=pl.BlockSpec((S, 128), lambda i: (0, i)),
        compiler_params=_cp(40),
    )(p, p, p, p, conv_w, conv_b, conv_b if after is None else after)


DP_Q, DP_K, DP_V, DP_ZA, DP_U, DP_BG, DP_CG, DP_ZC = range(8)


def _out_proj_loss(o, p, conv_g, w_out, xx, tgt, mrow, b_ada):
    tm = 256

    def body(o_ref, za_ref, c_ref, w_ref, x_ref, t_ref, m_ref, b_ref,
             dy_ref, dconv_ref, dp_ref, do_ref, gwo_ref, dgate_ref, loss_ref):
        k = pl.program_id(0)

        @pl.when(k == 0)
        def _():
            gwo_ref[...] = jnp.zeros_like(gwo_ref)
            dgate_ref[...] = jnp.zeros_like(dgate_ref)
            loss_ref[0, 0] = 0.0

        gate = m_ref[:, 2 * D:3 * D] + b_ref[:, 2 * D:3 * D]
        za = za_ref[...]
        sz = _silu(za)
        om = _merge_heads(o_ref)
        av, cv = (om * sz).astype(bf16), c_ref[...]
        mo = jnp.dot(av, w_ref[0:DA, :], preferred_element_type=f32)
        mo = mo + jnp.dot(cv, w_ref[DA:DA + DC, :], preferred_element_type=f32)
        y = x_ref[...] + gate * mo
        diff = y - t_ref[...]
        loss_ref[0, 0] += jnp.sum(diff * diff)
        dy = diff * (1.0 / D)
        dy_ref[...] = dy
        dgate_ref[...] += jnp.sum(dy * mo, axis=0, keepdims=True)
        dmo = (dy * gate).astype(bf16)
        dmix = _tt(dmo, w_ref[...])
        dattn = dmix[:, 0:DA]
        dconv_ref[...] = dmix[:, DA:DA + DC]
        a = dattn * sz
        for hh in range(H):
            do_ref[hh] = a[:, hh * DH:(hh + 1) * DH].astype(bf16)
        dp_ref[...] = ((dattn * _dsilu(za)) * om).astype(bf16)
        gwo_ref[0:DA, :] += _tn(av, dmo)
        gwo_ref[DA:DA + DC, :] += _tn(cv, dmo)

    row = lambda i: (i, 0)
    fixed = lambda i: (0, 0)
    hspec = pl.BlockSpec((H, tm, DH), lambda i: (0, i, 0))
    return _hbm_call(
        body, name="out_proj_loss",
        out_shape=(SDS((S, D), f32), SDS((S, DC), f32), SDS((8, S, DA), bf16), SDS((H, S, DH), bf16),
                   SDS((D, D), f32), SDS((1, D), f32), SDS((1, 1), f32)),
        grid=(S // tm,),
        in_specs=[hspec, pl.BlockSpec((tm, DA), lambda i: (i, 3)), pl.BlockSpec((tm, DC), row),
                  pl.BlockSpec((D, D), fixed), pl.BlockSpec((tm, D), row), pl.BlockSpec((tm, D), row),
                  pl.BlockSpec((1, 3 * D), fixed), pl.BlockSpec((1, 3 * D), fixed)],
        out_specs=(pl.BlockSpec((tm, D), row), pl.BlockSpec((tm, DC), row),
                   pl.BlockSpec((None, tm, DA), lambda i: (DP_ZA, i, 0)), hspec, pl.BlockSpec((D, D), fixed),
                   pl.BlockSpec((1, D), fixed), SMEM_SPEC),
        compiler_params=_cp(56, dimension_semantics=("arbitrary",)),
    )(o, p, conv_g, w_out, xx, tgt, mrow, b_ada)


def _conv_bwd(dconv, p, conv_w, conv_b, dp8, after=None):
    def body(d_ref, u_ref, bg_ref, cg_ref, zc_ref, w_ref, b_ref, dp_in_ref, after_ref, dp_ref, gw_ref, gb_ref):
        du_ref, dbg_ref, dcg_ref, dzc_ref = dp_ref.at[0], dp_ref.at[1], dp_ref.at[2], dp_ref.at[3]
        dconv = d_ref[...]
        u, bg, cg, zc = u_ref[...], bg_ref[...], cg_ref[...], zc_ref[...]
        w0, w1, w2 = w_ref[0:1, :], w_ref[1:2, :], w_ref[2:3, :]
        cu = cg * u
        cu_m, cu_p = _shift_rows(cu, True), _shift_rows(cu, False)
        cv = b_ref[...] + cu_m * w0
        cv = cv + cu * w1
        cv = cv + cu_p * w2
        sz = _silu(zc)
        dbg_ref[...] = ((dconv * sz) * cv).astype(bf16)
        dzc_ref[...] = ((dconv * (bg * cv)) * _dsilu(zc)).astype(bf16)
        dcv = (dconv * sz) * bg
        gb_ref[...] = jnp.sum(dcv, axis=0, keepdims=True)
        gw_ref[0:1, :] = jnp.sum(dcv * cu_m, axis=0, keepdims=True)
        gw_ref[1:2, :] = jnp.sum(dcv * cu, axis=0, keepdims=True)
        gw_ref[2:3, :] = jnp.sum(dcv * cu_p, axis=0, keepdims=True)
        gw_ref[3:8, :] = jnp.zeros((5, 128), f32)
        dcu = _shift_rows(dcv, False) * w0 + dcv * w1 + _shift_rows(dcv, True) * w2
        dcg_ref[...] = (dcu * u).astype(bf16)
        du_ref[...] = (dcu * cg).astype(bf16)

    return _hbm_call(
        body, name="conv_bwd", out_shape=(SDS((8, S, DC), bf16), SDS((8, DC), f32), SDS((1, DC), f32)),
        grid=(DC // 128,),
        in_specs=[pl.BlockSpec((S, 128), lambda i: (0, i))] + _conv_specs() + [ANY_SPEC, ANY_SPEC],
        out_specs=(pl.BlockSpec((4, S, 128), lambda i: (DP_U // 4, 0, i)), pl.BlockSpec((8, 128), lambda i: (0, i)),
                   pl.BlockSpec((1, 128), lambda i: (0, i))),
        input_output_aliases={7: 0}, compiler_params=_cp(48),
    )(dconv, p, p, p, p, conv_w, conv_b, dp8, conv_b if after is None else after)


def _attn_bwd(qr, qp, kr, vh, kc, vc, btt, do, after=None):
    def body(qr_ref, qp_ref, kr_ref, v_ref, kc_ref, vc_ref, bt_ref, do_ref, after_ref,
             dqr_ref, dqp_ref, dkr_ref, dv_ref, dkc_ref, dvc_ref, dbt_ref):
        kcv, vcv = kc_ref[...], vc_ref[...]
        dkr_ref[...] = jnp.zeros_like(dkr_ref)
        dv_ref[...] = jnp.zeros_like(dv_ref)
        dbt_ref[...] = jnp.zeros_like(dbt_ref)
        ctx_acc = {}

        def products(b):
            qs, ks, t = _block_geom(b)
            dob = do_ref[qs:qs + QB, :]
            s_lat = _tt(kr_ref[ks:ks + KB, :], qr_ref[qs:qs + QB, :]) + bt_ref[t]
            s_ctx = _tt(kcv, qp_ref[qs:qs + QB, :])
            return s_lat, s_ctx, _tt(v_ref[ks:ks + KB, :], dob), _tt(vcv, dob)

        def score_grads(b, x):
            s_lat, s_ctx, dp_lat, dp_ctx = x
            p_lat, p_ctx = _softmax_t(s_lat, s_ctx)
            delta = jnp.sum(p_lat * dp_lat, axis=0, keepdims=True) + jnp.sum(p_ctx * dp_ctx, axis=0, keepdims=True)
            ds_lat = p_lat * (dp_lat - delta)
            ds_ctx = p_ctx * (dp_ctx - delta)
            return ds_lat, ds_lat.astype(bf16), ds_ctx.astype(bf16), p_lat.astype(bf16), p_ctx.astype(bf16)

        def operand_grads(b, y):
            qs, ks, t = _block_geom(b)
            ds_lat, dsb_lat, dsb_ctx, pb_lat, pb_ctx = y
            qrb, qpb, dob = qr_ref[qs:qs + QB, :], qp_ref[qs:qs + QB, :], do_ref[qs:qs + QB, :]
            dbt_ref[t] += ds_lat
            dqr_ref[qs:qs + QB, :] = _tn(dsb_lat, kr_ref[ks:ks + KB, :])
            dqp_ref[qs:qs + QB, :] = _tn(dsb_ctx, kcv)
            dkr_ref[ks:ks + KB, :] += jnp.dot(dsb_lat, qrb, preferred_element_type=f32)
            dv_ref[ks:ks + KB, :] += jnp.dot(pb_lat, dob, preferred_element_type=f32)
            dkc = jnp.dot(dsb_ctx, qpb, preferred_element_type=f32)
            dvc = jnp.dot(pb_ctx, dob, preferred_element_type=f32)
            ctx_acc["k"] = dkc if b == 0 else ctx_acc["k"] + dkc
            ctx_acc["v"] = dvc if b == 0 else ctx_acc["v"] + dvc

        _staged(NQB, (products, score_grads, operand_grads))
        dkc_ref[...] = ctx_acc["k"]
        dvc_ref[...] = ctx_acc["v"]

    sq = pl.BlockSpec((None, S, DH), lambda h: (h, 0, 0))
    sc = pl.BlockSpec((None, L, DH), lambda h: (h, 0, 0))
    sb = pl.BlockSpec((None, NT, KB, QB), lambda h: (h, 0, 0, 0))
    big, ctxs = SDS((H, S, DH), f32), SDS((H, L, DH), f32)
    return _hbm_call(
        body, name="attn_bwd", out_shape=(big, big, big, big, ctxs, ctxs, SDS((H, NT, KB, QB), f32)), grid=(H,),
        in_specs=[sq, sq, sq, sq, sc, sc, sb, sq, ANY_SPEC], out_specs=(sq, sq, sq, sq, sc, sc, sb),
        compiler_params=_cp(56),
    )(qr, qp, kr, vh, kc, vc, btt, do, do if after is None else after)


def _bias_bwd(dbtt, after=None):
    pieces = _tile_pieces()

    def body(d_ref, after_ref, o_ref, scr):
        scr[...] = jnp.zeros_like(scr)
        acc = [None] * N_DR
        for t in range(NT):
            for j in range(2):
                for u in range(KR):
                    dr = pieces[t][j][u]
                    if dr is None:
                        continue
                    piece = d_ref[t, u * GW:(u + 1) * GW, j * GW:(j + 1) * GW]
                    acc[dr] = piece if acc[dr] is None else acc[dr] + piece
        for dr in range(N_DR):
            scr[dr * GW:(dr + 1) * GW, 0:GW] = acc[dr]
        xs = pltpu.roll(scr[...], WIN_W - 1, 1)
        row = lax.broadcasted_iota(i32, xs.shape, 0)
        for b in range(6):
            xs = jnp.where(((row >> b) & 1) == 1, pltpu.roll(xs, 128 - (1 << b), 1), xs)
        rev = jnp.sum(xs.reshape(N_DR, GW, 128), axis=1)
        a = lax.broadcasted_iota(i32, (128, 128), 0)
        b = lax.broadcasted_iota(i32, (128, 128), 1)
        flip = ((a + b == N_DC - 1) & (a < N_DC)).astype(f32)
        o_ref[...] = jnp.dot(rev, flip, precision=HIGHEST, preferred_element_type=f32)

    return _hbm_call(
        body, name="bias_bwd", out_shape=SDS((H, N_DR, 128), f32), grid=(H,),
        in_specs=[pl.BlockSpec((None, NT, KB, QB), lambda h: (h, 0, 0, 0)), ANY_SPEC],
        out_specs=pl.BlockSpec((None, N_DR, 128), lambda h: (h, 0, 0)),
        scratch_shapes=[pltpu.VMEM((N_DR * GW, 128), f32)],
    )(dbtt, dbtt if after is None else after)


def _merge_heads(ref):
    return jnp.concatenate([ref[hh] for hh in range(H)], axis=1)


def _head_norm_bwd(xraw, gain, dy, ones_bd):
    r = lax.rsqrt(_head_sum(xraw * xraw, ones_bd) * (1.0 / DH) + RMS_EPS)
    xh = xraw * r
    gdy = dy * gain
    dx = r * (gdy - xh * (_head_sum(xh * gdy, ones_bd) * (1.0 / DH)))
    return dx, jnp.sum(dy * xh, axis=0, keepdims=True)


def _qk_bwd(dqr, dqp, dkr, dvh, p, gq, gk, rope, dp8):
    tm = 256

    def body(dqr_ref, dqp_ref, dkr_ref, dv_ref, qk_ref, gq_ref, gk_ref, cc_ref, cr_ref, sc_ref, sr_ref, dp_in_ref,
             dp_ref, ggq_ref, ggk_ref):
        dq_ref, dk_ref, dvo_ref = dp_ref.at[DP_Q], dp_ref.at[DP_K], dp_ref.at[DP_V]

        @pl.when(pl.program_id(0) == 0)
        def _():
            ggq_ref[...] = jnp.zeros_like(ggq_ref)
            ggk_ref[...] = jnp.zeros_like(ggk_ref)

        ones_bd = _head_ones()
        cs, sn = _rope_block(cc_ref, cr_ref, tm), _rope_block(sc_ref, sr_ref, tm)
        a = _merge_heads(dqr_ref)
        dyq = ((a * cs - _swap16(a) * sn) + _merge_heads(dqp_ref)) * QK_SCALE
        bk = _merge_heads(dkr_ref)
        dyk = bk * cs - _swap16(bk) * sn
        dq, gq_part = _head_norm_bwd(qk_ref[:, 0:DA], gq_ref[...], dyq, ones_bd)
        dk, gk_part = _head_norm_bwd(qk_ref[:, DA:2 * DA], gk_ref[...], dyk, ones_bd)
        dq_ref[...] = dq.astype(bf16)
        dk_ref[...] = dk.astype(bf16)
        dvo_ref[...] = _merge_heads(dv_ref).astype(bf16)
        ggq_ref[...] += gq_part
        ggk_ref[...] += gk_part

    hspec = pl.BlockSpec((H, tm, DH), lambda i: (0, i, 0))
    fixed = pl.BlockSpec((1, DA), lambda i: (0, 0))
    return _hbm_call(
        body, name="qk_bwd", out_shape=(SDS((8, S, DA), bf16), SDS((1, DA), f32), SDS((1, DA), f32)), grid=(S // tm,),
        in_specs=[hspec, hspec, hspec, hspec, pl.BlockSpec((tm, 2 * DA), lambda i: (i, 0)), fixed, fixed]
        + _rope_specs(tm) + [ANY_SPEC],
        out_specs=(pl.BlockSpec((3, tm, DA), lambda i: (0, i, 0)), fixed, fixed), input_output_aliases={11: 0},
        compiler_params=_cp(40, dimension_semantics=("arbitrary",)),
    )(dqr, dqp, dkr, dvh, p, gq, gk, *rope, dp8)


def _ctx_bwd(dkc, dvc, pc, gk):
    def body(dkc_ref, dvc_ref, p_ref, gk_ref, dk_ref, dv_ref, ggk_ref):
        ones_bd = _head_ones()
        dk, gk_part = _head_norm_bwd(p_ref[:, 0:DA], gk_ref[...], _merge_heads(dkc_ref), ones_bd)
        dk_ref[...] = dk.astype(bf16)
        dv_ref[...] = _merge_heads(dvc_ref).astype(bf16)
        ggk_ref[...] = gk_part

    piece = SDS((L, DA), bf16)
    return _hbm_call(
        body, name="ctx_bwd", out_shape=(piece, piece, SDS((1, DA), f32)), in_specs=[VMEM_SPEC] * 4,
        out_specs=(VMEM_SPEC,) * 3,
    )(dkc, dvc, pc, gk)


def _grad_w_in(h, dp8, hc, dkc_raw, dvc_m, half, name, after=None):
    def body(half_ref, h_ref, p_ref, hc_ref, dk_ref, dv_ref, after_ref, g_ref):
        j = pl.program_id(0)
        hv = h_ref[...]
        g_ref[:, 0:DA] = _tn(hv, p_ref[0])
        g_ref[:, DA:2 * DA] = _tn(hv, p_ref[1])

        @pl.when(j == 0)
        def _():
            g_ref[:, DA:2 * DA] += _tn(hc_ref[...], dk_ref[...])

        @pl.when(j == 1)
        def _():
            g_ref[:, 0:DA] += _tn(hc_ref[...], dv_ref[...])

    fixed = lambda j, s: (0, 0)
    hd = D // 2
    grid_spec = pltpu.PrefetchScalarGridSpec(
        num_scalar_prefetch=1, grid=(4,),
        in_specs=[pl.BlockSpec((S, hd), lambda j, s: (0, s[0])), pl.BlockSpec((2, S, DA), lambda j, s: (j, 0, 0)),
                  pl.BlockSpec((L, hd), lambda j, s: (0, s[0])), pl.BlockSpec((L, DA), fixed),
                  pl.BlockSpec((L, DA), fixed), ANY_SPEC],
        out_specs=pl.BlockSpec((None, hd, D), lambda j, s: (j, 0, 0)))
    return _hbm_call(
        body, name=name, out_shape=SDS((4, hd, D), f32), grid_spec=grid_spec, compiler_params=_cp(40),
    )(half, h, dp8, hc, dkc_raw, dvc_m, half if after is None else after)


def _norm_mod_bwd(x, dh, g, scale):
    r = lax.rsqrt(jnp.mean(x * x, axis=-1, keepdims=True) + RMS_EPS)
    xh = x * r
    y = xh * g
    dshift = jnp.sum(dh, axis=0, keepdims=True)
    dscale = jnp.sum(dh * y, axis=0, keepdims=True)
    dyn = dh * (1.0 + scale)
    dg = jnp.sum(dyn * xh, axis=0, keepdims=True)
    gdy = dyn * g
    dx = r * (gdy - xh * jnp.mean(xh * gdy, axis=-1, keepdims=True))
    return dx, dshift, dscale, dg


def _dh_grad_x(dp8, w4, xx, dy, norm_g, mrow, b_ada, after=None):
    tm = 256

    def body(p_ref, w_ref, x_ref, dy_ref, g_ref, m_ref, b_ref, after_ref, gx_ref, dsh_ref, dsc_ref, dg_ref):
        @pl.when(pl.program_id(0) == 0)
        def _():
            dsh_ref[...] = jnp.zeros_like(dsh_ref)
            dsc_ref[...] = jnp.zeros_like(dsc_ref)
            dg_ref[...] = jnp.zeros_like(dg_ref)

        dh = None
        for j in range(4):
            for half in range(2):
                term = _tt(p_ref[2 * j + half], w_ref[j, :, half * DA:(half + 1) * DA])
                dh = term if dh is None else dh + term
        scale = m_ref[:, D:2 * D] + b_ref[:, D:2 * D]
        dx, dshift, dscale, dg = _norm_mod_bwd(x_ref[...], dh, g_ref[...], scale)
        gx_ref[...] = dy_ref[...] + dx
        dsh_ref[...] += dshift
        dsc_ref[...] += dscale
        dg_ref[...] += dg

    row = lambda i: (i, 0)
    fixed = lambda i: (0, 0)
    vec = SDS((1, D), f32)
    return _hbm_call(
        body, name="dh_grad_x", out_shape=(SDS((S, D), f32), vec, vec, vec), grid=(S // tm,),
        in_specs=[pl.BlockSpec((8, tm, DA), lambda i: (0, i, 0)), pl.BlockSpec((4, D, D), lambda i: (0, 0, 0)),
                  pl.BlockSpec((tm, D), row), pl.BlockSpec((tm, D), row), pl.BlockSpec((1, D), fixed),
                  pl.BlockSpec((1, 3 * D), fixed), pl.BlockSpec((1, 3 * D), fixed), ANY_SPEC],
        out_specs=(pl.BlockSpec((tm, D), row), pl.BlockSpec((1, D), fixed), pl.BlockSpec((1, D), fixed),
                   pl.BlockSpec((1, D), fixed)),
        compiler_params=_cp(56, dimension_semantics=("arbitrary",)),
    )(dp8, w4, xx, dy, norm_g, mrow, b_ada, b_ada if after is None else after)


def _dhc_sums(dkc_raw, dvc_m, w4, ctx2, norm_g, mrow_c, b_ada, after=None):
    def body(dk_ref, dv_ref, w0_ref, w1_ref, x_ref, g_ref, m_ref, b_ref, after_ref, dsh_ref, dsc_ref, dg_ref):
        dh = _tt(dk_ref[...], w0_ref[:, DA:2 * DA]) + _tt(dv_ref[...], w1_ref[:, 0:DA])
        scale = m_ref[:, D:2 * D] + b_ref[:, D:2 * D]
        _, dshift, dscale, dg = _norm_mod_bwd(x_ref[...], dh, g_ref[...], scale)
        dsh_ref[...] = dshift
        dsc_ref[...] = dscale
        dg_ref[...] = dg

    fixed = lambda i: (0, 0)
    vec = SDS((1, D), f32)
    vspec = pl.BlockSpec((1, D), fixed)
    return _hbm_call(
        body, name="dhc_sums", out_shape=(vec, vec, vec), grid=(1,),
        in_specs=[pl.BlockSpec((L, DA), fixed), pl.BlockSpec((L, DA), fixed),
                  pl.BlockSpec((None, D, D), lambda i: (0, 0, 0)), pl.BlockSpec((None, D, D), lambda i: (1, 0, 0)),
                  pl.BlockSpec((L, D), fixed), vspec, pl.BlockSpec((1, 3 * D), fixed), pl.BlockSpec((1, 3 * D), fixed),
                  ANY_SPEC],
        out_specs=(vspec, vspec, vspec), compiler_params=_cp(32),
    )(dkc_raw, dvc_m, w4, w4, ctx2, norm_g, mrow_c, b_ada, b_ada if after is None else after)


def _rope_tables():
    nf = DH // 4
    inv = np.float32(ROPE_THETA) ** (-np.arange(nf, dtype=np.float32) / np.float32(nf))
    ang_c = np.arange(GW, dtype=np.float32)[:, None] * inv
    ang_r = np.arange(ROWS, dtype=np.float32)[:, None] * inv
    zc, zr = np.zeros((GW, 2 * nf), np.float32), np.zeros((ROWS, 2 * nf), np.float32)
    ct_cos = np.tile(np.concatenate([zc, np.cos(ang_c), np.cos(ang_c)], axis=1), (1, H))
    ct_sin = np.tile(np.concatenate([zc, -np.sin(ang_c), np.sin(ang_c)], axis=1), (1, H))
    rt_cos = np.tile(np.concatenate([np.cos(ang_r), np.cos(ang_r), zr], axis=1), (1, H))
    rt_sin = np.tile(np.concatenate([-np.sin(ang_r), np.sin(ang_r), zr], axis=1), (1, H))
    rep8 = lambda t: np.ascontiguousarray(np.broadcast_to(t[:, None, :], (ROWS, 8, DA))).reshape(ROWS * 8, DA)
    return tuple(jnp.asarray(t, f32) for t in (ct_cos, rep8(rt_cos), ct_sin, rep8(rt_sin)))


def _local_step(xx, ctx2, tgt, mrow, mrow_c, b_ada, norm_g, weights, q_norm_g, k_norm_g, rpb2, conv_w_full, conv_b,
                hooks=None):
    hooks = hooks or {}
    gq = jnp.tile(q_norm_g, (1, H))
    gk = jnp.tile(k_norm_g, (1, H))
    rope = _rope_tables()

    h = _prenorm(xx, norm_g, mrow, b_ada, 256, "prenorm_x", after=weights.get("started"))
    hc = _prenorm(ctx2, norm_g, mrow_c, b_ada, L, "prenorm_ctx")
    jv = weights["jvec"]
    p = _in_proj_own(h, weights["own"], jv)
    btb = _bias_tiles(rpb2, after=p)
    w4, started = weights["near"](btb)
    p = _in_proj_block(h, w4, p, weights["first"], "in_proj_near", after=started)
    w4 = weights["near2"](w4, p)
    p = _in_proj_block(h, w4, p, weights["second"], "in_proj_near2")
    w4, started = weights["far"](w4, p)
    p = _in_proj_block(h, w4, p, jv ^ 3, "in_proj_far", after=started)
    pc = _ctx_proj(hc, w4)
    qr, qp, kr, vh = _qk_prep(p, gq, gk, rope)
    kc, vc = _ctx_prep(pc, gk)
    o = _attn_fwd(qr, qp, kr, vh, kc, vc, btb)
    started = weights["out_arrived"](o) if "out_arrived" in weights else None
    conv_g = _conv_fwd(p, conv_w_full, conv_b, after=started)
    w_out_full = weights["out"](conv_g)
    dy, dconv, dp8, do, g_w_out, dgate, loss_sum = _out_proj_loss(o, p, conv_g, w_out_full, xx, tgt, mrow, b_ada)
    started = hooks["g_w_out"](g_w_out) if "g_w_out" in hooks else None
    dp8, g_conv_w, g_conv_b = _conv_bwd(dconv, p, conv_w_full, conv_b, dp8, after=started)
    started = hooks["after_conv"](dp8) if "after_conv" in hooks else None
    dqr, dqp, dkr, dvh, dkc, dvc, dbtb = _attn_bwd(qr, qp, kr, vh, kc, vc, btb, do, after=started)
    dp8, g_gq, g_gk = _qk_bwd(dqr, dqp, dkr, dvh, p, gq, gk, rope, dp8)
    dkc_raw, dvc_m, g_gk_c = _ctx_bwd(dkc, dvc, pc, gk)
    first = hooks.get("first_half", jnp.zeros((1,), i32))
    g_first = _grad_w_in(h, dp8, hc, dkc_raw, dvc_m, first, "grad_w_in_first")
    started = hooks["g_w_in_first"](g_first) if "g_w_in_first" in hooks else None
    g_second = _grad_w_in(h, dp8, hc, dkc_raw, dvc_m, 1 - first, "grad_w_in_second", after=started)
    started = hooks["g_w_in_second"](g_second) if "g_w_in_second" in hooks else None
    dshift_c, dscale_c, dng_c = _dhc_sums(dkc_raw, dvc_m, w4, ctx2, norm_g, mrow_c, b_ada, after=started)
    g_rpb = _bias_bwd(dbtb, after=dshift_c)
    grad_x, dshift, dscale, dng = _dh_grad_x(dp8, w4, xx, dy, norm_g, mrow, b_ada, after=g_rpb)
    return dict(loss_sum=loss_sum, grad_x=grad_x, g_w_in=(g_first, g_second), g_w_out=g_w_out, g_conv_w=g_conv_w,
                g_conv_b=g_conv_b, g_rpb=g_rpb, g_gq=g_gq, g_gk=g_gk, g_gk_c=g_gk_c, dshift=dshift, dscale=dscale,
                dgate=dgate, dng=dng, dshift_c=dshift_c, dscale_c=dscale_c, dng_c=dng_c)


def _pair_sum_w_in(g, r):
    tr = 128

    def body(g_ref, r_ref, t32_ref, tb_ref):
        t = g_ref[...] + r_ref[...]
        t32_ref[...] = t
        tb_ref[...] = t.astype(bf16)

    half = D // 2
    spec = pl.BlockSpec((4, tr, D), lambda i: (0, i, 0))
    return _hbm_call(body, name="pair_sum_w_in", out_shape=(SDS((4, half, D), f32), SDS((4, half, D), bf16)),
                     grid=(half // tr,), in_specs=[spec, spec], out_specs=(spec, spec), compiler_params=_cp(40))(g, r)


def _pair_sum_w_out(g, r, cvec):
    hr = D // 8

    def body(c_ref, g0, g1, g2, g3, r_ref, t32_ref, tb_ref):
        for q, g_ref in enumerate((g0, g1, g2, g3)):
            t = g_ref[...] + r_ref[q]
            t32_ref[q] = t
            tb_ref[q] = t.astype(bf16)

    gspecs = [pl.BlockSpec((hr, D), lambda i, c, q=q: (2 * q + c[0], 0)) for q in range(4)]
    full = pl.BlockSpec((4, hr, D), lambda i, c: (0, 0, 0))
    grid_spec = pltpu.PrefetchScalarGridSpec(num_scalar_prefetch=1, grid=(1,), in_specs=gspecs + [full],
                                             out_specs=(full, full))
    return _hbm_call(body, name="pair_sum_w_out", out_shape=(SDS((4, hr, D), f32), SDS((4, hr, D), bf16)),
                          grid_spec=grid_spec)(cvec, g, g, g, g, r)


def _chip_sum(t32, r2, jvec, name):
    rows = t32.shape[1]
    tr = min(rows, 128)

    def body(j_ref, t_ref, r_ref, u_ref):
        u_ref[...] = ((t_ref[...] + r_ref[0].astype(f32)) + r_ref[1].astype(f32)) + r_ref[2].astype(f32)

    grid_spec = pltpu.PrefetchScalarGridSpec(
        num_scalar_prefetch=1, grid=(rows // tr,),
        in_specs=[pl.BlockSpec((None, tr, D), lambda i, j: (j[0], i, 0)),
                  pl.BlockSpec((3, tr, D), lambda i, j: (0, i, 0))],
        out_specs=pl.BlockSpec((tr, D), lambda i, j: (i, 0)))
    return _hbm_call(body, name=name, out_shape=SDS((rows, D), f32), grid_spec=grid_spec)(jvec, t32, r2)


_PK = {}
_off = 0
for _name, _rows in (("dm", 24), ("dmc", 24), ("dng", 8), ("dng_c", 8), ("gq", 8), ("gk", 8), ("gk_c", 8),
                     ("rpb", H * N_DR), ("conv_b", 8), ("conv_w", 16), ("loss", 8)):
    _PK[_name] = (_off, _off + _rows)
    _off += _rows
PK_ROWS = _off
RS_B_ADA, RS_NORM_G, RS_GQ, RS_GK, RS_RPB, RS_CONV_B, RS_CONV_W, RS_DMC, RS_LOSS, RS_ROWS = (
    0, 24, 32, 40, 48, 168, 176, 192, 216, 224)


def _small_reduce(gathered):
    def body(g_ref, o_ref, dm_ref):
        a0 = _PK["dm"][0]
        dm_ref[...] = jnp.zeros_like(dm_ref)
        for b in range(8):
            for i in range(24):
                dm_ref[b:b + 1, 128 * i:128 * (i + 1)] = g_ref[b, a0 + i:a0 + i + 1, :]
        tot = g_ref[0]
        for b in range(1, 8):
            tot = tot + g_ref[b]

        def rows(name):
            a, z = _PK[name]
            return tot[a:z]

        o_ref[RS_B_ADA:RS_B_ADA + 24] = rows("dm") + rows("dmc")
        o_ref[RS_NORM_G:RS_NORM_G + 8] = rows("dng") + rows("dng_c")
        gq = jnp.broadcast_to(jnp.sum(rows("gq"), axis=0, keepdims=True), (8, 128))
        gk = jnp.broadcast_to(jnp.sum(rows("gk") + rows("gk_c"), axis=0, keepdims=True), (8, 128))
        o_ref[RS_GQ:RS_GQ + 8] = gq + pltpu.roll(gq, DH, 1)
        o_ref[RS_GK:RS_GK + 8] = gk + pltpu.roll(gk, DH, 1)
        o_ref[RS_RPB:RS_RPB + H * N_DR] = rows("rpb")
        o_ref[RS_CONV_B:RS_CONV_B + 8] = rows("conv_b")
        o_ref[RS_CONV_W:RS_CONV_W + 16] = rows("conv_w")
        dmc = rows("dmc")
        o_ref[RS_DMC:RS_DMC + 24] = dmc
        o_ref[RS_LOSS:RS_LOSS + 8] = rows("loss")
        for i in range(24):
            dm_ref[8:9, 128 * i:128 * (i + 1)] = dmc[i:i + 1]

    return _hbm_call(body, name="small_reduce", out_shape=(SDS((RS_ROWS, 128), f32), SDS((16, 3 * D), f32)),
                     in_specs=[VMEM_SPEC], out_specs=(VMEM_SPEC, VMEM_SPEC))(gathered)


def _w_ada_grad(sc16, dm16, w_ada_shard, jvec):
    ncol = w_ada_shard.shape[1]

    def body(j_ref, sc_ref, dm_ref, w_ref, g_ref, part_ref):
        dm = dm_ref[...]
        g_ref[...] = lax.dot_general(sc_ref[...], dm, (((0,), (0,)), ((), ())), precision=HIGHEST,
                                     preferred_element_type=f32)
        part_ref[...] = lax.dot_general(dm[8:16], w_ref[...], (((1,), (1,)), ((), ())), precision=HIGHEST,
                                        preferred_element_type=f32)

    fixed = lambda i, j: (0, 0)
    grid_spec = pltpu.PrefetchScalarGridSpec(
        num_scalar_prefetch=1, grid=(1,),
        in_specs=[pl.BlockSpec((16, D), fixed), pl.BlockSpec((16, ncol), lambda i, j: (0, j[0])),
                  pl.BlockSpec((D, ncol), fixed)],
        out_specs=(pl.BlockSpec((D, ncol), fixed), pl.BlockSpec((8, D), fixed)))
    return _pallas_call(body, name="w_ada_grad", out_shape=(SDS((D, ncol), f32), SDS((8, D), f32)),
                        grid_spec=grid_spec, compiler_params=_cp(40))(jvec, sc16, dm16, w_ada_shard)


def _c_ctx_grad(parts4, c_ctx):
    def body(p_ref, c_ref, o_ref):
        tot = ((p_ref[0] + p_ref[1]) + p_ref[2]) + p_ref[3]
        o_ref[...] = tot[0:1] * _dsilu(c_ref[...].reshape(1, D))

    return _pallas_call(body, name="c_ctx_grad", out_shape=SDS((1, D), f32), in_specs=[VMEM_SPEC, VMEM_SPEC],
                        out_specs=VMEM_SPEC)(parts4, c_ctx)


def _adamw(w, g, m, v, name, after=None):
    rows, cols = w.shape
    tr = 256 if rows % 256 == 0 else rows

    def body(w_ref, g_ref, m_ref, v_ref, after_ref, d_ref, m2_ref, v2_ref):
        gv = g_ref[...]
        m2 = ADAM_B1 * m_ref[...] + (1.0 - ADAM_B1) * gv
        v2 = ADAM_B2 * v_ref[...] + (1.0 - ADAM_B2) * jnp.square(gv)
        m_hat = m2 / (1.0 - ADAM_B1 ** ADAM_STEP)
        v_hat = v2 / (1.0 - ADAM_B2 ** ADAM_STEP)
        d_ref[...] = -ADAM_LR * (m_hat / (jnp.sqrt(v_hat) + ADAM_EPS) + ADAM_WD * w_ref[...])
        m2_ref[...] = m2
        v2_ref[...] = v2

    spec = pl.BlockSpec((tr, cols), lambda i: (i, 0))
    shp = SDS((rows, cols), f32)
    return _hbm_call(body, name=name, out_shape=(shp, shp, shp), grid=(rows // tr,), in_specs=[spec] * 4 + [ANY_SPEC],
                     out_specs=(spec, spec, spec))(w, g, m, v, g if after is None else after)


def _adamw_halves(w, g_mine, g_other, m, v, cvec, name, after=None):
    rows, cols = w.shape
    half = rows // 2
    tr = min(256, half)
    per_half = half // tr

    def body(c_ref, w_ref, ga_ref, gb_ref, m_ref, v_ref, after_ref, g_ref, d_ref, m2_ref, v2_ref):
        in_my_half = (pl.program_id(0) // per_half) == c_ref[0]
        gv = jnp.where(in_my_half, ga_ref[...], gb_ref[...])
        g_ref[...] = gv
        m2 = ADAM_B1 * m_ref[...] + (1.0 - ADAM_B1) * gv
        v2 = ADAM_B2 * v_ref[...] + (1.0 - ADAM_B2) * jnp.square(gv)
        m_hat = m2 / (1.0 - ADAM_B1 ** ADAM_STEP)
        v_hat = v2 / (1.0 - ADAM_B2 ** ADAM_STEP)
        d_ref[...] = -ADAM_LR * (m_hat / (jnp.sqrt(v_hat) + ADAM_EPS) + ADAM_WD * w_ref[...])
        m2_ref[...] = m2
        v2_ref[...] = v2

    full = pl.BlockSpec((tr, cols), lambda i, c: (i, 0))
    part = pl.BlockSpec((tr, cols), lambda i, c: (i % per_half, 0))
    shp = SDS((rows, cols), f32)
    grid_spec = pltpu.PrefetchScalarGridSpec(num_scalar_prefetch=1, grid=(rows // tr,),
                                             in_specs=[full, part, part, full, full, ANY_SPEC], out_specs=(full,) * 4)
    return _hbm_call(body, name=name, out_shape=(shp,) * 4, grid_spec=grid_spec)(
        cvec, w, g_mine, g_other, m, v, cvec if after is None else after)


def _adam_math(w, g, m, v):
    m2 = ADAM_B1 * m + (1.0 - ADAM_B1) * g
    v2 = ADAM_B2 * v + (1.0 - ADAM_B2) * jnp.square(g)
    m_hat = m2 / (1.0 - ADAM_B1 ** ADAM_STEP)
    v_hat = v2 / (1.0 - ADAM_B2 ** ADAM_STEP)
    return -ADAM_LR * (m_hat / (jnp.sqrt(v_hat) + ADAM_EPS) + ADAM_WD * w), m2, v2


def _adamw_small(red, g_c_ctx, jvec, ws, ms, vs):
    n = len(ws)

    def body(*refs):
        red_ref, gc_ref, j_ref = refs[:3]
        w_refs, m_refs, v_refs = refs[3:3 + n], refs[3 + n:3 + 2 * n], refs[3 + 2 * n:3 + 3 * n]
        outs = refs[3 + 3 * n:]
        g_out, d_out, m_out, v_out = outs[:n], outs[n:2 * n], outs[2 * n:3 * n], outs[3 * n:]
        chip = j_ref[0]
        lanes = lambda i: (slice(None), slice(128 * i, 128 * (i + 1)))
        row = lambda r0, i: (lambda: red_ref[r0 + i:r0 + i + 1, :])
        whole = (slice(None), slice(None))
        chunks = [
            [((slice(None),), lambda: gc_ref[...].reshape(D))],
            [(lanes(i), row(RS_B_ADA, i)) for i in range(3 * D // 128)],
            [(lanes(i), row(RS_NORM_G, i)) for i in range(D // 128)],
            [(whole, lambda: red_ref[RS_GQ:RS_GQ + 1, 0:DH])],
            [(whole, lambda: red_ref[RS_GK:RS_GK + 1, 0:DH])],
            [((dr,), (lambda dr=dr: red_ref[pl.ds(RS_RPB + dr, H, stride=N_DR), 0:N_DC])) for dr in range(N_DR)],
            [((r,), (lambda r=r: red_ref[pl.ds(RS_CONV_W + 4 * r + chip, 1), :])) for r in range(3)],
            [(lanes(i), row(RS_CONV_B, i)) for i in range(DC // 128)],
        ]
        for a in range(n):
            for idx, grad in chunks[a]:
                g = grad()
                d, m2, v2 = _adam_math(w_refs[a][idx], g, m_refs[a][idx], v_refs[a][idx])
                g_out[a][idx] = g
                d_out[a][idx] = d
                m_out[a][idx] = m2
                v_out[a][idx] = v2

    shapes = [SDS(w.shape, f32) for w in ws]
    res = _pallas_call(body, name="adamw_small", out_shape=shapes * 4,
                       in_specs=[VMEM_SPEC, VMEM_SPEC, SMEM_SPEC] + [VMEM_SPEC] * (3 * n),
                       out_specs=[VMEM_SPEC] * (4 * n))(red, g_c_ctx, jvec, *ws, *ms, *vs)
    return [list(res[k * n:(k + 1) * n]) for k in range(4)]


def _rows128(a):
    return a.reshape(-1, 128)


def kernel(x, c, ctx, c_ctx, w_ada, b_ada, norm_g, w_in, q_norm_g, k_norm_g, rpb, conv_w, conv_b, w_out, loss_target, m_c_ctx, m_w_ada, m_b_ada, m_norm_g, m_w_in, m_q_norm_g, m_k_norm_g, m_rpb, m_conv_w, m_conv_b, m_w_out, v_c_ctx, v_w_ada, v_b_ada, v_norm_g, v_w_in, v_q_norm_g, v_k_norm_g, v_rpb, v_conv_w, v_conv_b, v_w_out):
    xi, yi, ci = lax.axis_index("x"), lax.axis_index("y"), lax.axis_index("c")
    dev = 4 * xi + 2 * yi + ci
    chip = 2 * xi + yi
    cvec = jnp.reshape(ci, (1,)).astype(i32)
    jvec = jnp.reshape(chip, (1,)).astype(i32)
    w_ada_s = w_ada[0]
    ncol = w_ada_s.shape[1]

    gc = _split_start([_to_slot(c.reshape(8, 128), 8, dev)], [], 7, _gather8_copies, "gather_c_start")
    wo4c = _cast_to_slot(w_out[0], jvec, "cast_w_out", after=gc[3])
    w4c = _cast_to_slot(w_in[0], jvec, "cast_w_in", after=wo4c)
    (c8,), _ = _split_wait(gc[0], gc[1], [gc[2]], [], w4c, _gather8_copies, "gather_c_wait")
    cc = jnp.concatenate([c8.reshape(8, D), c_ctx.reshape(1, D), jnp.zeros((7, D), f32)], axis=0)
    m_shard, sc16 = _adaln_shard(cc, w_ada_s)

    conv_w_pad = jnp.pad(conv_w[0], ((0, 5), (0, 0)))
    gm = _split_start([_to_slot(m_shard, 4, chip), _to_slot(conv_w_pad, 4, chip)], [], 6, _gather4_copies,
                      "gather_mod_start")

    all_k = [(0, 0, 0), (0, 0, 1), (0, 0, 2)]
    sem_a, rem_a, w4s, token = _split_start([w4c], [], 1, _near_copies, "weights_near_start", after=gm[4])
    (m4, cw4), _ = _split_wait(gm[0], gm[1], [gm[2], gm[3]], [], token, _gather4_copies, "gather_mod_wait")
    m_full = jnp.transpose(m4, (1, 0, 2)).reshape(16, 4 * ncol)
    mrow = lax.dynamic_slice(m_full, (dev, 0), (1, 3 * D))
    mrow_c = m_full[8:9]
    conv_w_full = jnp.transpose(cw4[:, 0:3, :], (1, 0, 2)).reshape(3, DC)
    waves = {}

    def near(after):
        (w4w,), _ = _split_wait(sem_a, rem_a, [w4s], [], after, _near_copies, "weights_near_wait")
        sem_b, rem_b, w4b, started = _split_start([w4w], [], 2, _pass_relay_copies, "weights_pass_start")
        waves["pass"] = (sem_b, rem_b)
        return w4b, started

    def near2(w4, after):
        (w4w,), _ = _split_wait(*waves["pass"], [w4], [], after, _pass_copy, "weights_pass_wait")
        return w4w

    def far(w4, after):
        (w4w,), _ = _split_wait(*waves["pass"], [w4], [], after, _relay_copy, "weights_far_wait")
        w4x, sem_c, rem_c, wo4s, started = _forward_then_start(w4w, wo4c, all_k, "weights_far_forward_out_start")
        (w4f,), _ = _split_wait(sem_c, rem_c, [w4x], [], started, _diag_forward_copy, "weights_far_forward_wait")
        waves["out"] = (sem_c, rem_c, wo4s)
        return w4f, started

    def w_out_arrived(after):
        sem_c, rem_c, wo4s = waves["out"]
        (wow,) = _halves_wait(sem_c, rem_c, [wo4s], after, all_k, "weights_out_wait")
        sem_d, rem_d, wof, started = _split_start([wow], [], 3, _forward_copies, "weights_out_forward_start")
        waves["out_forward"] = (sem_d, rem_d, wof)
        return started

    def w_out_gathered(after):
        sem_d, rem_d, wof = waves["out_forward"]
        (wo,), _ = _split_wait(sem_d, rem_d, [wof], [], after, _forward_copies, "weights_out_forward_wait")
        return wo.reshape(D, D)

    weights = dict(own=w_in[0], jvec=jvec, started=token, first=jvec ^ (1 + cvec), second=jvec ^ (2 - cvec),
                   near=near, near2=near2, far=far, out_arrived=w_out_arrived, out=w_out_gathered)

    exchange = _exchange_copies
    pending = {}

    def on_g_w_out(g_w_out):
        out = _split_start([g_w_out], [SDS((4, D // 8, D), f32)], 4, exchange, "grad_out_pair_start")
        pending["ex_out"] = out
        return out[4]

    def after_conv(dp8):
        ssem_o, rsem_o, g_o, land_o, _ = pending["ex_out"]
        (g_o,), (ex_o,) = _split_wait(ssem_o, rsem_o, [g_o], [land_o], dp8, exchange, "grad_out_pair_wait")
        to32, tob = _pair_sum_w_out(g_o, ex_o, cvec)
        out = _split_start([tob], [SDS((3, D // 8, D), bf16)], 3, _scatter_copies, "grad_out_chip_start")
        pending["sc_out"] = (out, to32)
        return out[4]

    def on_g_w_in_first(g_first):
        out = _split_start([g_first], [SDS((4, D // 2, D), f32)], 4, _block_exchange_copies, "grad_pair_start")
        pending["ex"] = (out[0], out[1], [out[2]], [out[3]])
        return out[4]

    def on_g_w_in_second(g_second):
        ex_ssem, ex_rsem, ex_srcs, ex_lands = pending["ex"]
        _, ex = _split_wait(ex_ssem, ex_rsem, ex_srcs, ex_lands, g_second, _block_exchange_copies, "grad_pair_wait")
        t32, tb = _pair_sum_w_in(g_second, ex[0])
        out = _split_start([tb], [SDS((3, D // 2, D), bf16)], 3, _scatter_copies, "grad_chip_start")
        pending["sc_in"] = (out, t32)
        return out[4]

    r = _local_step(x[0], ctx[0], loss_target[0], mrow, mrow_c, b_ada, norm_g, weights, q_norm_g, k_norm_g,
                    rpb[0], conv_w_full, conv_b,
                    dict(g_w_out=on_g_w_out, after_conv=after_conv, first_half=1 - cvec,
                         g_w_in_first=on_g_w_in_first, g_w_in_second=on_g_w_in_second))
    sc_in, t32 = pending["sc_in"]
    _, (r2,) = _split_wait(sc_in[0], sc_in[1], [sc_in[2]], [sc_in[3]], r["dng"], _scatter_copies, "grad_chip_wait")
    u_in = _chip_sum(t32, r2, jvec, "chip_sum_w_in")
    sc_o, to32 = pending["sc_out"]
    _, (ro2,) = _split_wait(sc_o[0], sc_o[1], [sc_o[2]], [sc_o[3]], u_in, _scatter_copies, "grad_out_chip_wait")
    u_out = _chip_sum(to32, ro2, jvec, "chip_sum_w_out")
    swap = _split_start([u_in, u_out], [SDS(u_in.shape, f32), SDS(u_out.shape, f32)], 2, _swap_copies,
                        "grad_pair_swap_start")

    dm = jnp.concatenate([r["dshift"], r["dscale"], r["dgate"]], axis=1)
    dmc = jnp.concatenate([r["dshift_c"], r["dscale_c"], jnp.zeros((1, D), f32)], axis=1)
    pack_parts = [_rows128(dm), _rows128(dmc), _rows128(r["dng"]), _rows128(r["dng_c"]), _rows128(r["g_gq"]),
                  _rows128(r["g_gk"]), _rows128(r["g_gk_c"]), r["g_rpb"].reshape(H * N_DR, 128),
                  _rows128(r["g_conv_b"]), _rows128(r["g_conv_w"][0:3]), jnp.pad(r["loss_sum"], ((0, 0), (0, 127)))]
    pack = jnp.concatenate([jnp.pad(p, ((0, -p.shape[0] % 8), (0, 0))) for p in pack_parts], axis=0)
    assert pack.shape[0] == PK_ROWS
    gs = _split_start([_to_slot(pack, 8, dev)], [], 7, _gather8_copies, "gather_small_start", after=swap[6])
    (u_in, u_out), (o_in, o_out) = _split_wait(swap[0], swap[1], swap[2:4], swap[4:6], gs[3], _swap_copies,
                                               "grad_pair_swap_wait")
    g_w_in_s, d_w_in, nm_w_in, nv_w_in = _adamw_halves(w_in[0], u_in, o_in, m_w_in[0], v_w_in[0], cvec, "adamw_w_in",
                                                       after=gs[3])
    g_w_out_s, d_w_out, nm_w_out, nv_w_out = _adamw_halves(w_out[0], u_out, o_out, m_w_out[0], v_w_out[0], cvec,
                                                           "adamw_w_out", after=nm_w_in)
    (gathered,), _ = _split_wait(gs[0], gs[1], [gs[2]], [], nm_w_out, _gather8_copies, "gather_small_wait")
    red, dm16 = _small_reduce(gathered)
    loss = red[RS_LOSS, 0] * (0.5 / D)

    g_w_ada_s, cpart = _w_ada_grad(sc16, dm16, w_ada_s, jvec)
    gcp = _split_start([_to_slot(cpart, 4, chip)], [], 3, _gather4_copies, "gather_c_ctx_parts_start")
    d_w_ada, nm_w_ada, nv_w_ada = _adamw(w_ada_s, g_w_ada_s, m_w_ada[0], v_w_ada[0], "adamw_w_ada", after=gcp[3])
    (cparts4,), _ = _split_wait(gcp[0], gcp[1], [gcp[2]], [], nm_w_ada, _gather4_copies, "gather_c_ctx_parts_wait")
    g_c_ctx = _c_ctx_grad(cparts4, c_ctx)

    t_rpb = lambda a: jnp.transpose(a, (0, 2, 1, 3)).reshape(N_DR, H, N_DC)
    t_cw = lambda a: jnp.transpose(a, (1, 0, 2))
    small = _adamw_small(
        red, g_c_ctx, jvec,
        [c_ctx, b_ada, norm_g, q_norm_g, k_norm_g, t_rpb(rpb), t_cw(conv_w), conv_b],
        [m_c_ctx, m_b_ada, m_norm_g, m_q_norm_g, m_k_norm_g, t_rpb(m_rpb), t_cw(m_conv_w), m_conv_b],
        [v_c_ctx, v_b_ada, v_norm_g, v_q_norm_g, v_k_norm_g, t_rpb(v_rpb), t_cw(v_conv_w), v_conv_b])
    for kind in small:
        kind[5] = jnp.transpose(kind[5].reshape(1, N_DR, H, N_DC), (0, 2, 1, 3))
        kind[6] = jnp.transpose(kind[6], (1, 0, 2))

    def ordered(kind, big_w_ada, big_w_in, big_w_out):
        s_c_ctx, s_b_ada, s_norm_g, s_q, s_k, s_rpb, s_conv_w, s_conv_b = small[kind]
        return [s_c_ctx, big_w_ada[None], s_b_ada, s_norm_g, big_w_in[None], s_q, s_k, s_rpb, s_conv_w,
                s_conv_b, big_w_out[None]]

    grads = ordered(0, g_w_ada_s, g_w_in_s, g_w_out_s)
    deltas = ordered(1, d_w_ada, d_w_in, d_w_out)
    new_m = ordered(2, nm_w_ada, nm_w_in, nm_w_out)
    new_v = ordered(3, nv_w_ada, nv_w_in, nv_w_out)
    return (loss, r["grad_x"][None], *grads, *deltas, *new_m, *new_v)
```

```python
import functools

import jax
import jax.numpy as jnp
import numpy as np
from jax import lax
from jax.experimental import pallas as pl
from jax.experimental.pallas import tpu as pltpu

f32, bf16, i32 = jnp.float32, jnp.bfloat16, jnp.int32
MESH = pl.DeviceIdType.MESH
HIGHEST = lax.Precision.HIGHEST

D = 1024
S = 2048
L = 256
GW = 64
ROWS = S // GW
H = 8
DH = 64
DA = H * DH
DC = 512
WIN_H, WIN_W = 8, 16
N_DR, N_DC = 2 * WIN_H - 1, 2 * WIN_W - 1
RMS_EPS = 1e-6
ROPE_THETA = 10000.0
QK_SCALE = DH ** -0.5
NEG = -1e30

QB = 128
NQB = S // QB
KR = 9
KB = KR * GW
TILE_GEOM = ((0, 0), (2, 0), (4, 0), (28, 23), (30, 23))
NT = len(TILE_GEOM)

ADAM_LR, ADAM_B1, ADAM_B2, ADAM_EPS, ADAM_WD, ADAM_STEP = 0.001, 0.9, 0.999, 1e-08, 0.01, 10

VMEM_SPEC = pl.BlockSpec(memory_space=pltpu.VMEM)
ANY_SPEC = pl.BlockSpec(memory_space=pl.ANY)
SMEM_SPEC = pl.BlockSpec(memory_space=pltpu.SMEM)
SDS = jax.ShapeDtypeStruct


_pallas_call = pl.pallas_call


def _hbm_call(body, *, out_shape, in_specs=None, out_specs=None, grid_spec=None, **kw):
    n_pre = 0
    if grid_spec is not None:
        ispecs, ospecs, n_pre = grid_spec.in_specs, grid_spec.out_specs, grid_spec.num_scalar_prefetch
        kw["grid_spec"] = grid_spec
    else:
        ispecs, ospecs = in_specs, out_specs
        kw.update(in_specs=in_specs, out_specs=out_specs)

    def blocked(spec):
        return isinstance(spec, pl.BlockSpec) and spec.block_shape is not None

    single = not isinstance(out_shape, (tuple, list))
    shapes = [out_shape] if single else list(out_shape)
    ospec_list = list(ospecs) if isinstance(ospecs, (tuple, list)) else [ospecs]
    shapes = [pltpu.HBM(s.shape, s.dtype) if blocked(sp) else s for s, sp in zip(shapes, ospec_list)]
    call = _pallas_call(body, out_shape=shapes[0] if single else tuple(shapes), **kw)

    def run(*args):
        arrays = [pltpu.with_memory_space_constraint(a, pltpu.HBM) if blocked(sp) else a
                  for a, sp in zip(args[n_pre:], ispecs)]
        return call(*args[:n_pre], *arrays)

    return run


def _cp(vmem_mb=None, **kw):
    if vmem_mb is not None:
        kw["vmem_limit_bytes"] = vmem_mb << 20
    return pltpu.CompilerParams(**kw)


def _silu(z):
    return z * jax.nn.sigmoid(z)


def _dsilu(z):
    sg = jax.nn.sigmoid(z)
    return sg * (1.0 + z * (1.0 - sg))


def _row_start(i):
    return min(max(i - WIN_H // 2, 0), ROWS - WIN_H)


def _my_pos():
    return lax.axis_index("x"), lax.axis_index("y"), lax.axis_index("c")


def _flip(v, bit):
    return 1 - v if bit else v


def _swap_copies(srcs, lands, ssem, rsem):
    x, y, c = _my_pos()
    return [pltpu.make_async_remote_copy(src_ref=srcs[a], dst_ref=lands[a], send_sem=ssem.at[a], recv_sem=rsem.at[a],
                                         device_id=(x, y, 1 - c), device_id_type=MESH) for a in range(len(srcs))]


HBM_SPEC = pl.BlockSpec(memory_space=pltpu.HBM)
SEM_SPEC = pl.BlockSpec(memory_space=pltpu.SEMAPHORE)
DATAFLOW = pltpu.SideEffectType.DATAFLOW_SIDE_EFFECTING


def _peer_chips(x, y, c):
    out = []
    for k in range(1, 4):
        px, py = _flip(x, (k >> 1) & 1), _flip(y, k & 1)
        out.append(((px, py, c), 2 * px + py))
    return out


def _half_copies(srcs, dsts, ssem, rsem, which):
    x, y, c = _my_pos()
    j = 2 * x + y
    peers = _peer_chips(x, y, c)
    pairs = []
    for pos, group, k in which:
        half = srcs[pos].shape[1] // 2
        mine = pl.ds(pl.multiple_of(c * half, 8), half)
        dev, pj = peers[k]
        sem = 3 * group + k
        send = pltpu.make_async_remote_copy(src_ref=srcs[pos].at[j, mine], dst_ref=dsts[pos].at[j, mine],
                                            send_sem=ssem.at[sem], recv_sem=rsem.at[sem], device_id=dev,
                                            device_id_type=MESH)
        arrive = pltpu.make_async_remote_copy(src_ref=srcs[pos].at[j, mine], dst_ref=dsts[pos].at[pj, mine],
                                              send_sem=ssem.at[sem], recv_sem=rsem.at[sem], device_id=dev,
                                              device_id_type=MESH)
        pairs.append((send, arrive))
    return pairs


def _halves_wait(ssem, rsem, bigs, after, which, name):
    nb = len(bigs)

    def body(*refs):
        b_in = refs[:nb]
        ssem_ref, rsem_ref = refs[nb], refs[nb + 1]
        for send, arrive in _half_copies(b_in, b_in, ssem_ref, rsem_ref, which):
            send.wait_send()
            arrive.wait_recv()

    return _hbm_call(
        body, name=name, out_shape=tuple(pltpu.HBM(b.shape, b.dtype) for b in bigs),
        in_specs=[HBM_SPEC] * nb + [SEM_SPEC, SEM_SPEC, ANY_SPEC], out_specs=tuple([HBM_SPEC] * nb),
        input_output_aliases={a: a for a in range(nb)}, compiler_params=_cp(has_side_effects=DATAFLOW),
    )(*bigs, ssem, rsem, after)


FORWARD_SEM = 3


def _diag_forward_copy(srcs, dsts, ssem, rsem):
    x, y, c = _my_pos()
    half = srcs[0].shape[1] // 2
    diag = 3 - (2 * x + y)
    mine = pl.ds(pl.multiple_of(c * half, 8), half)
    other = pl.ds(pl.multiple_of((1 - c) * half, 8), half)
    return [_Copy(srcs[0].at[diag, mine], dsts[0].at[diag, mine], dsts[0].at[diag, other], ssem.at[FORWARD_SEM],
                  rsem.at[FORWARD_SEM], (x, y, 1 - c))]


def _forward_then_start(fwd, big, order, name):
    def body(f_in, b_in, f_out, ssem, rsem, b_out, token):
        _diag_forward_copy([f_in], [f_out], ssem, rsem)[0].start()
        for send, _ in _half_copies([b_in], [b_out], ssem, rsem, order):
            send.start()
        token[...] = jnp.zeros_like(token)

    n_sem = FORWARD_SEM + 1
    out_shape = (pltpu.HBM(fwd.shape, fwd.dtype), pltpu.SemaphoreType.DMA((n_sem,)), pltpu.SemaphoreType.DMA((n_sem,)),
                 pltpu.HBM(big.shape, big.dtype), SDS((8, 128), f32))
    return _hbm_call(
        body, name=name, out_shape=out_shape, in_specs=[HBM_SPEC, HBM_SPEC],
        out_specs=(HBM_SPEC, SEM_SPEC, SEM_SPEC, HBM_SPEC, VMEM_SPEC), input_output_aliases={0: 0, 1: 3},
        compiler_params=_cp(has_side_effects=DATAFLOW),
    )(*[pltpu.with_memory_space_constraint(b, pltpu.HBM) for b in (fwd, big)])


def _cast_to_slot(w, jvec, name, after=None):
    rows, cols = w.shape
    tr = 256

    def body(j_ref, w_ref, after_ref, o_ref):
        o_ref[...] = w_ref[...].astype(bf16)

    grid_spec = pltpu.PrefetchScalarGridSpec(
        num_scalar_prefetch=1, grid=(rows // tr,),
        in_specs=[pl.BlockSpec((tr, cols), lambda i, j: (i, 0)), ANY_SPEC],
        out_specs=pl.BlockSpec((None, tr, cols), lambda i, j: (j[0], i, 0)))
    return _hbm_call(body, name=name, out_shape=SDS((4, rows, cols), bf16),
                     grid_spec=grid_spec)(jvec, w, jvec if after is None else after)


def _exchange_copies(srcs, lands, ssem, rsem):
    x, y, c = _my_pos()
    half = srcs[0].shape[0] // 8
    cps = []
    for jb in range(4):
        src = srcs[0].at[pl.ds(pl.multiple_of((2 * jb + 1 - c) * half, 8), half)]
        cps.append(pltpu.make_async_remote_copy(src_ref=src, dst_ref=lands[0].at[jb], send_sem=ssem.at[jb],
                                                recv_sem=rsem.at[jb], device_id=(x, y, 1 - c), device_id_type=MESH))
    return cps


def _block_exchange_copies(srcs, lands, ssem, rsem):
    x, y, c = _my_pos()
    return [pltpu.make_async_remote_copy(src_ref=srcs[0].at[jb], dst_ref=lands[0].at[jb], send_sem=ssem.at[jb],
                                         recv_sem=rsem.at[jb], device_id=(x, y, 1 - c), device_id_type=MESH)
            for jb in range(4)]


def _scatter_copies(srcs, lands, ssem, rsem):
    x, y, c = _my_pos()
    cps = []
    for a in range(len(srcs)):
        for k, (dev, pj) in enumerate(_peer_chips(x, y, c)):
            cps.append(pltpu.make_async_remote_copy(src_ref=srcs[a].at[pj], dst_ref=lands[a].at[k],
                                                    send_sem=ssem.at[3 * a + k], recv_sem=rsem.at[3 * a + k],
                                                    device_id=dev, device_id_type=MESH))
    return cps


class _Copy:
    def __init__(self, src, dst, arrive, ssem, rsem, dev):
        make = lambda to: pltpu.make_async_remote_copy(src_ref=src, dst_ref=to, send_sem=ssem, recv_sem=rsem,
                                                       device_id=dev, device_id_type=MESH)
        send, arrival = make(dst), make(arrive)
        self.start, self.wait_send, self.wait_recv = send.start, send.wait_send, arrival.wait_recv


def _toward(x, y, along_x):
    return x + along_x * (1 - 2 * x), y + (1 - along_x) * (1 - 2 * y)


def _near_copies(srcs, dsts, ssem, rsem):
    x, y, c = _my_pos()
    px, py = _toward(x, y, c)
    j = 2 * x + y
    return [_Copy(srcs[0].at[j], dsts[0].at[j], dsts[0].at[2 * px + py], ssem.at[0], rsem.at[0], (px, py, c))]


def _pass_copy(srcs, dsts, ssem, rsem):
    x, y, c = _my_pos()
    px, py = _toward(x, y, c)
    qx, qy = _toward(x, y, 1 - c)
    got = 2 * px + py
    return [_Copy(srcs[0].at[got], dsts[0].at[got], dsts[0].at[2 * qx + qy], ssem.at[0], rsem.at[0], (x, y, 1 - c))]


def _relay_copy(srcs, dsts, ssem, rsem):
    x, y, c = _my_pos()
    px, py = _toward(x, y, c)
    qx, qy = _toward(x, y, 1 - c)
    half = srcs[0].shape[1] // 2
    mine = pl.ds(pl.multiple_of(c * half, 8), half)
    got, diag = 2 * px + py, 3 - (2 * x + y)
    return [_Copy(srcs[0].at[got, mine], dsts[0].at[got, mine], dsts[0].at[diag, mine], ssem.at[1], rsem.at[1],
                  (qx, qy, c))]


def _forward_copies(srcs, dsts, ssem, rsem):
    x, y, c = _my_pos()
    half = srcs[0].shape[1] // 2
    mine = pl.ds(pl.multiple_of(c * half, 8), half)
    other = pl.ds(pl.multiple_of((1 - c) * half, 8), half)
    return [_Copy(srcs[0].at[pj, mine], dsts[0].at[pj, mine], dsts[0].at[pj, other], ssem.at[k], rsem.at[k],
                  (x, y, 1 - c)) for k, (_, pj) in enumerate(_peer_chips(x, y, c))]


def _pass_relay_copies(srcs, dsts, ssem, rsem):
    return _pass_copy(srcs, dsts, ssem, rsem) + _relay_copy(srcs, dsts, ssem, rsem)


def _gather8_copies(srcs, dsts, ssem, rsem):
    x, y, c = _my_pos()
    me = 4 * x + 2 * y + c
    cps = []
    for a in range(len(srcs)):
        for k in range(1, 8):
            tgt = (_flip(x, (k >> 2) & 1), _flip(y, (k >> 1) & 1), _flip(c, k & 1))
            cps.append(_Copy(srcs[a].at[me], dsts[a].at[me], dsts[a].at[4 * tgt[0] + 2 * tgt[1] + tgt[2]],
                             ssem.at[7 * a + k - 1], rsem.at[7 * a + k - 1], tgt))
    return cps


def _gather4_copies(srcs, dsts, ssem, rsem):
    x, y, c = _my_pos()
    j = 2 * x + y
    cps = []
    for a in range(len(srcs)):
        for k, (dev, pj) in enumerate(_peer_chips(x, y, c)):
            cps.append(_Copy(srcs[a].at[j], dsts[a].at[j], dsts[a].at[pj], ssem.at[3 * a + k], rsem.at[3 * a + k], dev))
    return cps


def _to_slot(a, n, i):
    return lax.dynamic_update_slice(jnp.zeros((n,) + a.shape, a.dtype), a[None], (i,) + (0,) * a.ndim)


def _split_start(srcs, land_shapes, n_cp, make, name, after=None):
    ns, nl = len(srcs), len(land_shapes)
    n_in = ns + nl + (after is not None)

    def body(*refs):
        s_in = refs[:ns]
        ssem, rsem = refs[n_in], refs[n_in + 1]
        s_out = refs[n_in + 2:n_in + 2 + ns]
        l_out = refs[n_in + 2 + ns:n_in + 2 + ns + nl]
        token = refs[n_in + 2 + ns + nl]
        for cp in make(s_in, l_out if nl else s_out, ssem, rsem):
            cp.start()
        token[...] = jnp.zeros_like(token)

    lands = [pltpu.with_memory_space_constraint(lax.empty(sh.shape, sh.dtype), pltpu.HBM) for sh in land_shapes]
    out_shape = (pltpu.SemaphoreType.DMA((n_cp,)), pltpu.SemaphoreType.DMA((n_cp,)),
                 *[pltpu.HBM(b.shape, b.dtype) for b in srcs], *[pltpu.HBM(b.shape, b.dtype) for b in land_shapes],
                 SDS((8, 128), f32))
    return _hbm_call(
        body, name=name, out_shape=out_shape, in_specs=[HBM_SPEC] * (ns + nl) + [ANY_SPEC] * (after is not None),
        out_specs=(SEM_SPEC, SEM_SPEC, *[HBM_SPEC] * (ns + nl), VMEM_SPEC),
        input_output_aliases={i: 2 + i for i in range(ns + nl)}, compiler_params=_cp(has_side_effects=DATAFLOW),
    )(*[pltpu.with_memory_space_constraint(b, pltpu.HBM) for b in srcs], *lands, *([] if after is None else [after]))


def _split_wait(ssem, rsem, srcs, lands, after, make, name):
    ns, nl = len(srcs), len(lands)

    def body(*refs):
        s_in, l_in = refs[:ns], refs[ns:ns + nl]
        ssem_ref, rsem_ref = refs[ns + nl], refs[ns + nl + 1]
        for cp in make(s_in, l_in if nl else s_in, ssem_ref, rsem_ref):
            cp.wait_send()
            cp.wait_recv()

    outs = _hbm_call(
        body, name=name, out_shape=tuple(pltpu.HBM(b.shape, b.dtype) for b in (*srcs, *lands)),
        in_specs=[HBM_SPEC] * (ns + nl) + [SEM_SPEC, SEM_SPEC, ANY_SPEC], out_specs=tuple([HBM_SPEC] * (ns + nl)),
        input_output_aliases={i: i for i in range(ns + nl)}, compiler_params=_cp(has_side_effects=DATAFLOW),
    )(*srcs, *lands, ssem, rsem, after)
    return list(outs[:ns]), list(outs[ns:])


def _adaln_shard(cc, w_ada_shard):
    def body(c_ref, w_ref, m_ref, sc_ref):
        sc = _silu(c_ref[...])
        sc_ref[...] = sc
        m_ref[...] = jnp.dot(sc, w_ref[...], precision=HIGHEST, preferred_element_type=f32)

    return _hbm_call(
        body, name="adaln_shard", out_shape=(SDS((16, w_ada_shard.shape[1]), f32), SDS((16, D), f32)),
        in_specs=[VMEM_SPEC, VMEM_SPEC], out_specs=(VMEM_SPEC, VMEM_SPEC), compiler_params=_cp(32),
    )(cc, w_ada_shard)


def _prenorm(xx, norm_g, mrow, b_ada, tm, name, after=None):
    n = xx.shape[0]

    def body(x_ref, g_ref, m_ref, b_ref, after_ref, h_ref):
        x = x_ref[...]
        shift = m_ref[:, 0:D] + b_ref[:, 0:D]
        scale = m_ref[:, D:2 * D] + b_ref[:, D:2 * D]
        r = lax.rsqrt(jnp.mean(x * x, axis=-1, keepdims=True) + RMS_EPS)
        y = (x * r) * g_ref[...]
        h_ref[...] = (y * (1.0 + scale) + shift).astype(bf16)

    row = lambda i: (i, 0)
    fixed = lambda i: (0, 0)
    return _hbm_call(
        body, name=name, out_shape=SDS((n, D), bf16), grid=(n // tm,),
        in_specs=[pl.BlockSpec((tm, D), row), pl.BlockSpec((1, D), fixed), pl.BlockSpec((1, 3 * D), fixed),
                  pl.BlockSpec((1, 3 * D), fixed), ANY_SPEC],
        out_specs=pl.BlockSpec((tm, D), row),
    )(xx, norm_g, mrow, b_ada, b_ada if after is None else after)


def _in_proj_own(h, w_own, jvec):
    tm = 512

    def body(j_ref, h_ref, w_ref, p_ref):
        p_ref[...] = jnp.dot(h_ref[...], w_ref[...].astype(bf16), preferred_element_type=f32)

    grid_spec = pltpu.PrefetchScalarGridSpec(
        num_scalar_prefetch=1, grid=(S // tm,),
        in_specs=[pl.BlockSpec((tm, D), lambda i, j: (i, 0)), pl.BlockSpec((D, D), lambda i, j: (0, 0))],
        out_specs=pl.BlockSpec((tm, D), lambda i, j: (i, j[0])))
    return _hbm_call(body, name="in_proj_own", out_shape=SDS((S, 4 * D), f32), grid_spec=grid_spec,
                     compiler_params=_cp(40))(jvec, h, w_own)


def _in_proj_block(h, w4, p, bvec, name, after=None):
    tm = 512

    def body(b_ref, h_ref, w_ref, p_in_ref, after_ref, p_ref):
        p_ref[...] = jnp.dot(h_ref[...], w_ref[...], preferred_element_type=f32)

    grid_spec = pltpu.PrefetchScalarGridSpec(
        num_scalar_prefetch=1, grid=(S // tm,),
        in_specs=[pl.BlockSpec((tm, D), lambda i, b: (i, 0)), pl.BlockSpec((None, D, D), lambda i, b: (b[0], 0, 0)),
                  ANY_SPEC, ANY_SPEC],
        out_specs=pl.BlockSpec((tm, D), lambda i, b: (i, b[0])))
    return _hbm_call(body, name=name, out_shape=SDS((S, 4 * D), f32), grid_spec=grid_spec,
                     input_output_aliases={3: 0})(bvec, h, w4, p, bvec if after is None else after)


def _ctx_proj(hc, w4):
    def body(h_ref, w0_ref, w1_ref, p_ref):
        hv = h_ref[...]
        p_ref[:, 0:DA] = jnp.dot(hv, w0_ref[:, DA:2 * DA], preferred_element_type=f32)
        p_ref[:, DA:2 * DA] = jnp.dot(hv, w1_ref[:, 0:DA], preferred_element_type=f32)

    return _hbm_call(
        body, name="ctx_proj", out_shape=SDS((L, 2 * DA), f32), grid=(1,),
        in_specs=[pl.BlockSpec((L, D), lambda i: (0, 0)), pl.BlockSpec((None, D, D), lambda i: (0, 0, 0)),
                  pl.BlockSpec((None, D, D), lambda i: (1, 0, 0))],
        out_specs=pl.BlockSpec((L, 2 * DA), lambda i: (0, 0)),
    )(hc, w4, w4)


def _head_ones():
    r = lax.broadcasted_iota(i32, (DA, DA), 0) // DH
    c = lax.broadcasted_iota(i32, (DA, DA), 1) // DH
    return (r == c).astype(bf16)


def _head_sum(v, ones_bd):
    hi = v.astype(bf16)
    lo = (v - hi.astype(f32)).astype(bf16)
    return jnp.dot(hi, ones_bd, preferred_element_type=f32) + jnp.dot(lo, ones_bd, preferred_element_type=f32)


def _swap16(v):
    lane = lax.broadcasted_iota(i32, v.shape, 1)
    return jnp.where((lane & 31) < 16, pltpu.roll(v, DA - 16, 1), pltpu.roll(v, 16, 1))


def _rope_block(ct_ref, rt_ref, tm):
    rows = [jnp.tile(rt_ref[8 * j:8 * j + 8, :], (GW // 8, 1)) for j in range(tm // GW)]
    return jnp.tile(ct_ref[...], (tm // GW, 1)) + jnp.concatenate(rows, axis=0)


def _rope_specs(tm):
    col = pl.BlockSpec((GW, DA), lambda i: (0, 0))
    row = pl.BlockSpec((8 * tm // GW, DA), lambda i: (i, 0))
    return [col, row, col, row]


def _qk_prep(p, gq, gk, rope):
    tm = 256

    def body(qk_ref, v_ref, gq_ref, gk_ref, cc_ref, cr_ref, sc_ref, sr_ref, qr_ref, qp_ref, kr_ref, vh_ref):
        ones_bd = _head_ones()
        cs, sn = _rope_block(cc_ref, cr_ref, tm), _rope_block(sc_ref, sr_ref, tm)
        q = qk_ref[:, 0:DA]
        k = qk_ref[:, DA:2 * DA]
        yq = (q * lax.rsqrt(_head_sum(q * q, ones_bd) * (1.0 / DH) + RMS_EPS)) * gq_ref[...]
        yk = (k * lax.rsqrt(_head_sum(k * k, ones_bd) * (1.0 / DH) + RMS_EPS)) * gk_ref[...]
        qr = (yq * cs + _swap16(yq) * sn) * QK_SCALE
        qp = yq * QK_SCALE
        kr = yk * cs + _swap16(yk) * sn
        vv = v_ref[...]
        for hh in range(H):
            sl = slice(hh * DH, (hh + 1) * DH)
            qr_ref[hh] = qr[:, sl].astype(bf16)
            qp_ref[hh] = qp[:, sl].astype(bf16)
            kr_ref[hh] = kr[:, sl].astype(bf16)
            vh_ref[hh] = vv[:, sl].astype(bf16)

    hm = SDS((H, S, DH), bf16)
    hspec = pl.BlockSpec((H, tm, DH), lambda i: (0, i, 0))
    fixed = lambda i: (0, 0)
    return _hbm_call(
        body, name="qk_prep", out_shape=(hm, hm, hm, hm), grid=(S // tm,),
        in_specs=[pl.BlockSpec((tm, 2 * DA), lambda i: (i, 0)), pl.BlockSpec((tm, DA), lambda i: (i, 2)),
                  pl.BlockSpec((1, DA), fixed), pl.BlockSpec((1, DA), fixed)] + _rope_specs(tm),
        out_specs=(hspec, hspec, hspec, hspec),
    )(p, p, gq, gk, *rope)


def _ctx_prep(pc, gk):
    def body(p_ref, gk_ref, kc_ref, vc_ref):
        ones_bd = _head_ones()
        k = p_ref[:, 0:DA]
        yk = (k * lax.rsqrt(_head_sum(k * k, ones_bd) * (1.0 / DH) + RMS_EPS)) * gk_ref[...]
        vv = p_ref[:, DA:2 * DA]
        for hh in range(H):
            sl = slice(hh * DH, (hh + 1) * DH)
            kc_ref[hh] = yk[:, sl].astype(bf16)
            vc_ref[hh] = vv[:, sl].astype(bf16)

    hm = SDS((H, L, DH), bf16)
    return _hbm_call(
        body, name="ctx_prep", out_shape=(hm, hm), in_specs=[VMEM_SPEC, VMEM_SPEC], out_specs=(VMEM_SPEC, VMEM_SPEC),
    )(pc, gk)


def _tile_pieces():
    out = []
    for (i0, u0) in TILE_GEOM:
        rows = []
        for j in range(2):
            i = i0 + j
            rs = _row_start(i)
            rows.append([(u0 + u - i + WIN_H - 1) if rs <= u0 + u < rs + WIN_H else None for u in range(KR)])
        out.append(rows)
    return out


def _bias_prep(rpb_rev_pad, after=None):
    pieces = _tile_pieces()

    def body(r_ref, after_ref, o_ref):
        rp = r_ref[...]
        xs = jnp.concatenate([pltpu.roll(jnp.broadcast_to(rp[dr:dr + 1, :], (GW, 128)), 128 - (WIN_W - 1), 1,
                                         stride=1, stride_axis=0) for dr in range(N_DR)], axis=0)
        row = lax.broadcasted_iota(i32, xs.shape, 0)
        lane = lax.broadcasted_iota(i32, xs.shape, 1)
        k = row & (GW - 1)
        c0 = jnp.clip(lane - WIN_W // 2, 0, GW - WIN_W)
        xs = jnp.where((k >= c0) & (k < c0 + WIN_W), xs, NEG)
        neg = jnp.full((GW, GW), NEG, f32)
        for t in range(NT):
            for j in range(2):
                for u in range(KR):
                    dr = pieces[t][j][u]
                    piece = neg if dr is None else xs[dr * GW:(dr + 1) * GW, 0:GW]
                    o_ref[t, u * GW:(u + 1) * GW, j * GW:(j + 1) * GW] = piece

    return _hbm_call(
        body, name="bias_prep", out_shape=SDS((H, NT, KB, QB), f32), grid=(H,),
        in_specs=[pl.BlockSpec((None, N_DR, 128), lambda h: (h, 0, 0)), ANY_SPEC],
        out_specs=pl.BlockSpec((None, NT, KB, QB), lambda h: (h, 0, 0, 0)),
    )(rpb_rev_pad, rpb_rev_pad if after is None else after)


def _bias_tiles(rpb2, after=None):
    return _bias_prep(jnp.pad(rpb2[:, :, ::-1], ((0, 0), (0, 0), (0, 128 - N_DC))), after)


def _block_geom(b):
    qs = b * QB
    ks = min(max(2 * b - 4, 0), ROWS - KR) * GW
    t = b if b < 2 else (b - (NQB - NT) if b > NQB - 3 else 2)
    return qs, ks, t


def _tt(a, b):
    return lax.dot_general(a, b, (((1,), (1,)), ((), ())), preferred_element_type=f32)


def _tn(a, b):
    return lax.dot_general(a, b, (((0,), (0,)), ((), ())), preferred_element_type=f32)


def _softmax_t(s_lat, s_ctx):
    m = jnp.maximum(jnp.max(s_lat, axis=0, keepdims=True), jnp.max(s_ctx, axis=0, keepdims=True))
    e_lat = jnp.exp(s_lat - m)
    e_ctx = jnp.exp(s_ctx - m)
    inv = 1.0 / (jnp.sum(e_lat, axis=0, keepdims=True) + jnp.sum(e_ctx, axis=0, keepdims=True))
    return e_lat * inv, e_ctx * inv


def _staged(n_blocks, stages):
    held = [dict() for _ in stages]
    for step in range(n_blocks + len(stages) - 1):
        for s, fn in enumerate(stages):
            b = step - s
            if 0 <= b < n_blocks:
                held[s][b] = fn(b) if s == 0 else fn(b, held[s - 1].pop(b))


def _attn_fwd(qr, qp, kr, vh, kc, vc, btt):
    def body(qr_ref, qp_ref, kr_ref, v_ref, kc_ref, vc_ref, bt_ref, o_ref):
        kcv, vcv = kc_ref[...], vc_ref[...]

        def scores(b):
            qs, ks, t = _block_geom(b)
            return (_tt(kr_ref[ks:ks + KB, :], qr_ref[qs:qs + QB, :]) + bt_ref[t], _tt(kcv, qp_ref[qs:qs + QB, :]))

        def probs(b, sc):
            p_lat, p_ctx = _softmax_t(*sc)
            return p_lat.astype(bf16), p_ctx.astype(bf16)

        def values(b, p):
            qs, ks, _ = _block_geom(b)
            o_ref[qs:qs + QB, :] = _tn(p[0], v_ref[ks:ks + KB, :]) + _tn(p[1], vcv)

        _staged(NQB, (scores, probs, values))

    sq = pl.BlockSpec((None, S, DH), lambda h: (h, 0, 0))
    sc = pl.BlockSpec((None, L, DH), lambda h: (h, 0, 0))
    return _hbm_call(
        body, name="attn_fwd", out_shape=SDS((H, S, DH), f32), grid=(H,),
        in_specs=[sq, sq, sq, sq, sc, sc, pl.BlockSpec((None, NT, KB, QB), lambda h: (h, 0, 0, 0))],
        out_specs=sq, compiler_params=_cp(48),
    )(qr, qp, kr, vh, kc, vc, btt)


def _shift_rows(v, down):
    n = v.shape[0]
    row = lax.broadcasted_iota(i32, v.shape, 0)
    if down:
        return jnp.where(row == 0, 0.0, pltpu.roll(v, 1, 0))
    return jnp.where(row == n - 1, 0.0, pltpu.roll(v, n - 1, 0))


def _conv_specs():
    col = lambda off: pl.BlockSpec((S, 128), lambda i, off=off: (0, off + i))
    return [col(16), col(20), col(24), col(28), pl.BlockSpec((3, 128), lambda i: (0, i)),
            pl.BlockSpec((1, 128), lambda i: (0, i))]


def _conv_fwd(p, conv_w, conv_b, after=None):
    def body(u_ref, bg_ref, cg_ref, zc_ref, w_ref, b_ref, after_ref, o_ref):
        cu = cg_ref[...] * u_ref[...]
        cv = b_ref[...] + _shift_rows(cu, True) * w_ref[0:1, :]
        cv = cv + cu * w_ref[1:2, :]
        cv = cv + _shift_rows(cu, False) * w_ref[2:3, :]
        o_ref[...] = ((bg_ref[...] * cv) * _silu(zc_ref[...])).astype(bf16)

    return _hbm_call(
        body, name="conv_fwd", out_shape=SDS((S, DC), bf16), grid=(DC // 128,),
        in_specs=_conv_specs() + [ANY_SPEC], out_specs=pl.BlockSpec((S, 128), lambda i: (0, i)),
        compiler_params=_cp(40),
    )(p, p, p, p, conv_w, conv_b, conv_b if after is None else after)


DP_Q, DP_K, DP_V, DP_ZA, DP_U, DP_BG, DP_CG, DP_ZC = range(8)


def _out_proj_loss(o, p, conv_g, w_out, xx, tgt, mrow, b_ada):
    tm = 256

    def body(o_ref, za_ref, c_ref, w_ref, x_ref, t_ref, m_ref, b_ref,
             dy_ref, dconv_ref, dp_ref, do_ref, gwo_ref, dgate_ref, loss_ref):
        k = pl.program_id(0)

        @pl.when(k == 0)
        def _():
            gwo_ref[...] = jnp.zeros_like(gwo_ref)
            dgate_ref[...] = jnp.zeros_like(dgate_ref)
            loss_ref[0, 0] = 0.0

        gate = m_ref[:, 2 * D:3 * D] + b_ref[:, 2 * D:3 * D]
        za = za_ref[...]
        sz = _silu(za)
        om = _merge_heads(o_ref)
        av, cv = (om * sz).astype(bf16), c_ref[...]
        mo = jnp.dot(av, w_ref[0:DA, :], preferred_element_type=f32)
        mo = mo + jnp.dot(cv, w_ref[DA:DA + DC, :], preferred_element_type=f32)
        y = x_ref[...] + gate * mo
        diff = y - t_ref[...]
        loss_ref[0, 0] += jnp.sum(diff * diff)
        dy = diff * (1.0 / D)
        dy_ref[...] = dy
        dgate_ref[...] += jnp.sum(dy * mo, axis=0, keepdims=True)
        dmo = (dy * gate).astype(bf16)
        dmix = _tt(dmo, w_ref[...])
        dattn = dmix[:, 0:DA]
        dconv_ref[...] = dmix[:, DA:DA + DC]
        a = dattn * sz
        for hh in range(H):
            do_ref[hh] = a[:, hh * DH:(hh + 1) * DH].astype(bf16)
        dp_ref[...] = ((dattn * _dsilu(za)) * om).astype(bf16)
        gwo_ref[0:DA, :] += _tn(av, dmo)
        gwo_ref[DA:DA + DC, :] += _tn(cv, dmo)

    row = lambda i: (i, 0)
    fixed = lambda i: (0, 0)
    hspec = pl.BlockSpec((H, tm, DH), lambda i: (0, i, 0))
    return _hbm_call(
        body, name="out_proj_loss",
        out_shape=(SDS((S, D), f32), SDS((S, DC), f32), SDS((8, S, DA), bf16), SDS((H, S, DH), bf16),
                   SDS((D, D), f32), SDS((1, D), f32), SDS((1, 1), f32)),
        grid=(S // tm,),
        in_specs=[hspec, pl.BlockSpec((tm, DA), lambda i: (i, 3)), pl.BlockSpec((tm, DC), row),
                  pl.BlockSpec((D, D), fixed), pl.BlockSpec((tm, D), row), pl.BlockSpec((tm, D), row),
                  pl.BlockSpec((1, 3 * D), fixed), pl.BlockSpec((1, 3 * D), fixed)],
        out_specs=(pl.BlockSpec((tm, D), row), pl.BlockSpec((tm, DC), row),
                   pl.BlockSpec((None, tm, DA), lambda i: (DP_ZA, i, 0)), hspec, pl.BlockSpec((D, D), fixed),
                   pl.BlockSpec((1, D), fixed), SMEM_SPEC),
        compiler_params=_cp(56, dimension_semantics=("arbitrary",)),
    )(o, p, conv_g, w_out, xx, tgt, mrow, b_ada)


def _conv_bwd(dconv, p, conv_w, conv_b, dp8, after=None):
    def body(d_ref, u_ref, bg_ref, cg_ref, zc_ref, w_ref, b_ref, dp_in_ref, after_ref, dp_ref, gw_ref, gb_ref):
        du_ref, dbg_ref, dcg_ref, dzc_ref = dp_ref.at[0], dp_ref.at[1], dp_ref.at[2], dp_ref.at[3]
        dconv = d_ref[...]
        u, bg, cg, zc = u_ref[...], bg_ref[...], cg_ref[...], zc_ref[...]
        w0, w1, w2 = w_ref[0:1, :], w_ref[1:2, :], w_ref[2:3, :]
        cu = cg * u
        cu_m, cu_p = _shift_rows(cu, True), _shift_rows(cu, False)
        cv = b_ref[...] + cu_m * w0
        cv = cv + cu * w1
        cv = cv + cu_p * w2
        sz = _silu(zc)
        dbg_ref[...] = ((dconv * sz) * cv).astype(bf16)
        dzc_ref[...] = ((dconv * (bg * cv)) * _dsilu(zc)).astype(bf16)
        dcv = (dconv * sz) * bg
        gb_ref[...] = jnp.sum(dcv, axis=0, keepdims=True)
        gw_ref[0:1, :] = jnp.sum(dcv * cu_m, axis=0, keepdims=True)
        gw_ref[1:2, :] = jnp.sum(dcv * cu, axis=0, keepdims=True)
        gw_ref[2:3, :] = jnp.sum(dcv * cu_p, axis=0, keepdims=True)
        gw_ref[3:8, :] = jnp.zeros((5, 128), f32)
        dcu = _shift_rows(dcv, False) * w0 + dcv * w1 + _shift_rows(dcv, True) * w2
        dcg_ref[...] = (dcu * u).astype(bf16)
        du_ref[...] = (dcu * cg).astype(bf16)

    return _hbm_call(
        body, name="conv_bwd", out_shape=(SDS((8, S, DC), bf16), SDS((8, DC), f32), SDS((1, DC), f32)),
        grid=(DC // 128,),
        in_specs=[pl.BlockSpec((S, 128), lambda i: (0, i))] + _conv_specs() + [ANY_SPEC, ANY_SPEC],
        out_specs=(pl.BlockSpec((4, S, 128), lambda i: (DP_U // 4, 0, i)), pl.BlockSpec((8, 128), lambda i: (0, i)),
                   pl.BlockSpec((1, 128), lambda i: (0, i))),
        input_output_aliases={7: 0}, compiler_params=_cp(48),
    )(dconv, p, p, p, p, conv_w, conv_b, dp8, conv_b if after is None else after)


def _attn_bwd(qr, qp, kr, vh, kc, vc, btt, do, after=None):
    def body(qr_ref, qp_ref, kr_ref, v_ref, kc_ref, vc_ref, bt_ref, do_ref, after_ref,
             dqr_ref, dqp_ref, dkr_ref, dv_ref, dkc_ref, dvc_ref, dbt_ref):
        kcv, vcv = kc_ref[...], vc_ref[...]
        dkr_ref[...] = jnp.zeros_like(dkr_ref)
        dv_ref[...] = jnp.zeros_like(dv_ref)
        dbt_ref[...] = jnp.zeros_like(dbt_ref)
        ctx_acc = {}

        def products(b):
            qs, ks, t = _block_geom(b)
            dob = do_ref[qs:qs + QB, :]
            s_lat = _tt(kr_ref[ks:ks + KB, :], qr_ref[qs:qs + QB, :]) + bt_ref[t]
            s_ctx = _tt(kcv, qp_ref[qs:qs + QB, :])
            return s_lat, s_ctx, _tt(v_ref[ks:ks + KB, :], dob), _tt(vcv, dob)

        def score_grads(b, x):
            s_lat, s_ctx, dp_lat, dp_ctx = x
            p_lat, p_ctx = _softmax_t(s_lat, s_ctx)
            delta = jnp.sum(p_lat * dp_lat, axis=0, keepdims=True) + jnp.sum(p_ctx * dp_ctx, axis=0, keepdims=True)
            ds_lat = p_lat * (dp_lat - delta)
            ds_ctx = p_ctx * (dp_ctx - delta)
            return ds_lat, ds_lat.astype(bf16), ds_ctx.astype(bf16), p_lat.astype(bf16), p_ctx.astype(bf16)

        def operand_grads(b, y):
            qs, ks, t = _block_geom(b)
            ds_lat, dsb_lat, dsb_ctx, pb_lat, pb_ctx = y
            qrb, qpb, dob = qr_ref[qs:qs + QB, :], qp_ref[qs:qs + QB, :], do_ref[qs:qs + QB, :]
            dbt_ref[t] += ds_lat
            dqr_ref[qs:qs + QB, :] = _tn(dsb_lat, kr_ref[ks:ks + KB, :])
            dqp_ref[qs:qs + QB, :] = _tn(dsb_ctx, kcv)
            dkr_ref[ks:ks + KB, :] += jnp.dot(dsb_lat, qrb, preferred_element_type=f32)
            dv_ref[ks:ks + KB, :] += jnp.dot(pb_lat, dob, preferred_element_type=f32)
            dkc = jnp.dot(dsb_ctx, qpb, preferred_element_type=f32)
            dvc = jnp.dot(pb_ctx, dob, preferred_element_type=f32)
            ctx_acc["k"] = dkc if b == 0 else ctx_acc["k"] + dkc
            ctx_acc["v"] = dvc if b == 0 else ctx_acc["v"] + dvc

        _staged(NQB, (products, score_grads, operand_grads))
        dkc_ref[...] = ctx_acc["k"]
        dvc_ref[...] = ctx_acc["v"]

    sq = pl.BlockSpec((None, S, DH), lambda h: (h, 0, 0))
    sc = pl.BlockSpec((None, L, DH), lambda h: (h, 0, 0))
    sb = pl.BlockSpec((None, NT, KB, QB), lambda h: (h, 0, 0, 0))
    big, ctxs = SDS((H, S, DH), f32), SDS((H, L, DH), f32)
    return _hbm_call(
        body, name="attn_bwd", out_shape=(big, big, big, big, ctxs, ctxs, SDS((H, NT, KB, QB), f32)), grid=(H,),
        in_specs=[sq, sq, sq, sq, sc, sc, sb, sq, ANY_SPEC], out_specs=(sq, sq, sq, sq, sc, sc, sb),
        compiler_params=_cp(56),
    )(qr, qp, kr, vh, kc, vc, btt, do, do if after is None else after)


def _bias_bwd(dbtt, after=None):
    pieces = _tile_pieces()

    def body(d_ref, after_ref, o_ref, scr):
        scr[...] = jnp.zeros_like(scr)
        acc = [None] * N_DR
        for t in range(NT):
            for j in range(2):
                for u in range(KR):
                    dr = pieces[t][j][u]
                    if dr is None:
                        continue
                    piece = d_ref[t, u * GW:(u + 1) * GW, j * GW:(j + 1) * GW]
                    acc[dr] = piece if acc[dr] is None else acc[dr] + piece
        for dr in range(N_DR):
            scr[dr * GW:(dr + 1) * GW, 0:GW] = acc[dr]
        xs = pltpu.roll(scr[...], WIN_W - 1, 1)
        row = lax.broadcasted_iota(i32, xs.shape, 0)
        for b in range(6):
            xs = jnp.where(((row >> b) & 1) == 1, pltpu.roll(xs, 128 - (1 << b), 1), xs)
        rev = jnp.sum(xs.reshape(N_DR, GW, 128), axis=1)
        a = lax.broadcasted_iota(i32, (128, 128), 0)
        b = lax.broadcasted_iota(i32, (128, 128), 1)
        flip = ((a + b == N_DC - 1) & (a < N_DC)).astype(f32)
        o_ref[...] = jnp.dot(rev, flip, precision=HIGHEST, preferred_element_type=f32)

    return _hbm_call(
        body, name="bias_bwd", out_shape=SDS((H, N_DR, 128), f32), grid=(H,),
        in_specs=[pl.BlockSpec((None, NT, KB, QB), lambda h: (h, 0, 0, 0)), ANY_SPEC],
        out_specs=pl.BlockSpec((None, N_DR, 128), lambda h: (h, 0, 0)),
        scratch_shapes=[pltpu.VMEM((N_DR * GW, 128), f32)],
    )(dbtt, dbtt if after is None else after)


def _merge_heads(ref):
    return jnp.concatenate([ref[hh] for hh in range(H)], axis=1)


def _head_norm_bwd(xraw, gain, dy, ones_bd):
    r = lax.rsqrt(_head_sum(xraw * xraw, ones_bd) * (1.0 / DH) + RMS_EPS)
    xh = xraw * r
    gdy = dy * gain
    dx = r * (gdy - xh * (_head_sum(xh * gdy, ones_bd) * (1.0 / DH)))
    return dx, jnp.sum(dy * xh, axis=0, keepdims=True)


def _qk_bwd(dqr, dqp, dkr, dvh, p, gq, gk, rope, dp8):
    tm = 256

    def body(dqr_ref, dqp_ref, dkr_ref, dv_ref, qk_ref, gq_ref, gk_ref, cc_ref, cr_ref, sc_ref, sr_ref, dp_in_ref,
             dp_ref, ggq_ref, ggk_ref):
        dq_ref, dk_ref, dvo_ref = dp_ref.at[DP_Q], dp_ref.at[DP_K], dp_ref.at[DP_V]

        @pl.when(pl.program_id(0) == 0)
        def _():
            ggq_ref[...] = jnp.zeros_like(ggq_ref)
            ggk_ref[...] = jnp.zeros_like(ggk_ref)

        ones_bd = _head_ones()
        cs, sn = _rope_block(cc_ref, cr_ref, tm), _rope_block(sc_ref, sr_ref, tm)
        a = _merge_heads(dqr_ref)
        dyq = ((a * cs - _swap16(a) * sn) + _merge_heads(dqp_ref)) * QK_SCALE
        bk = _merge_heads(dkr_ref)
        dyk = bk * cs - _swap16(bk) * sn
        dq, gq_part = _head_norm_bwd(qk_ref[:, 0:DA], gq_ref[...], dyq, ones_bd)
        dk, gk_part = _head_norm_bwd(qk_ref[:, DA:2 * DA], gk_ref[...], dyk, ones_bd)
        dq_ref[...] = dq.astype(bf16)
        dk_ref[...] = dk.astype(bf16)
        dvo_ref[...] = _merge_heads(dv_ref).astype(bf16)
        ggq_ref[...] += gq_part
        ggk_ref[...] += gk_part

    hspec = pl.BlockSpec((H, tm, DH), lambda i: (0, i, 0))
    fixed = pl.BlockSpec((1, DA), lambda i: (0, 0))
    return _hbm_call(
        body, name="qk_bwd", out_shape=(SDS((8, S, DA), bf16), SDS((1, DA), f32), SDS((1, DA), f32)), grid=(S // tm,),
        in_specs=[hspec, hspec, hspec, hspec, pl.BlockSpec((tm, 2 * DA), lambda i: (i, 0)), fixed, fixed]
        + _rope_specs(tm) + [ANY_SPEC],
        out_specs=(pl.BlockSpec((3, tm, DA), lambda i: (0, i, 0)), fixed, fixed), input_output_aliases={11: 0},
        compiler_params=_cp(40, dimension_semantics=("arbitrary",)),
    )(dqr, dqp, dkr, dvh, p, gq, gk, *rope, dp8)


def _ctx_bwd(dkc, dvc, pc, gk):
    def body(dkc_ref, dvc_ref, p_ref, gk_ref, dk_ref, dv_ref, ggk_ref):
        ones_bd = _head_ones()
        dk, gk_part = _head_norm_bwd(p_ref[:, 0:DA], gk_ref[...], _merge_heads(dkc_ref), ones_bd)
        dk_ref[...] = dk.astype(bf16)
        dv_ref[...] = _merge_heads(dvc_ref).astype(bf16)
        ggk_ref[...] = gk_part

    piece = SDS((L, DA), bf16)
    return _hbm_call(
        body, name="ctx_bwd", out_shape=(piece, piece, SDS((1, DA), f32)), in_specs=[VMEM_SPEC] * 4,
        out_specs=(VMEM_SPEC,) * 3,
    )(dkc, dvc, pc, gk)


def _grad_w_in(h, dp8, hc, dkc_raw, dvc_m, half, name, after=None):
    def body(half_ref, h_ref, p_ref, hc_ref, dk_ref, dv_ref, after_ref, g_ref):
        j = pl.program_id(0)
        hv = h_ref[...]
        g_ref[:, 0:DA] = _tn(hv, p_ref[0])
        g_ref[:, DA:2 * DA] = _tn(hv, p_ref[1])

        @pl.when(j == 0)
        def _():
            g_ref[:, DA:2 * DA] += _tn(hc_ref[...], dk_ref[...])

        @pl.when(j == 1)
        def _():
            g_ref[:, 0:DA] += _tn(hc_ref[...], dv_ref[...])

    fixed = lambda j, s: (0, 0)
    hd = D // 2
    grid_spec = pltpu.PrefetchScalarGridSpec(
        num_scalar_prefetch=1, grid=(4,),
        in_specs=[pl.BlockSpec((S, hd), lambda j, s: (0, s[0])), pl.BlockSpec((2, S, DA), lambda j, s: (j, 0, 0)),
                  pl.BlockSpec((L, hd), lambda j, s: (0, s[0])), pl.BlockSpec((L, DA), fixed),
                  pl.BlockSpec((L, DA), fixed), ANY_SPEC],
        out_specs=pl.BlockSpec((None, hd, D), lambda j, s: (j, 0, 0)))
    return _hbm_call(
        body, name=name, out_shape=SDS((4, hd, D), f32), grid_spec=grid_spec, compiler_params=_cp(40),
    )(half, h, dp8, hc, dkc_raw, dvc_m, half if after is None else after)


def _norm_mod_bwd(x, dh, g, scale):
    r = lax.rsqrt(jnp.mean(x * x, axis=-1, keepdims=True) + RMS_EPS)
    xh = x * r
    y = xh * g
    dshift = jnp.sum(dh, axis=0, keepdims=True)
    dscale = jnp.sum(dh * y, axis=0, keepdims=True)
    dyn = dh * (1.0 + scale)
    dg = jnp.sum(dyn * xh, axis=0, keepdims=True)
    gdy = dyn * g
    dx = r * (gdy - xh * jnp.mean(xh * gdy, axis=-1, keepdims=True))
    return dx, dshift, dscale, dg


def _dh_grad_x(dp8, w4, xx, dy, norm_g, mrow, b_ada, after=None):
    tm = 256

    def body(p_ref, w_ref, x_ref, dy_ref, g_ref, m_ref, b_ref, after_ref, gx_ref, dsh_ref, dsc_ref, dg_ref):
        @pl.when(pl.program_id(0) == 0)
        def _():
            dsh_ref[...] = jnp.zeros_like(dsh_ref)
            dsc_ref[...] = jnp.zeros_like(dsc_ref)
            dg_ref[...] = jnp.zeros_like(dg_ref)

        dh = None
        for j in range(4):
            for half in range(2):
                term = _tt(p_ref[2 * j + half], w_ref[j, :, half * DA:(half + 1) * DA])
                dh = term if dh is None else dh + term
        scale = m_ref[:, D:2 * D] + b_ref[:, D:2 * D]
        dx, dshift, dscale, dg = _norm_mod_bwd(x_ref[...], dh, g_ref[...], scale)
        gx_ref[...] = dy_ref[...] + dx
        dsh_ref[...] += dshift
        dsc_ref[...] += dscale
        dg_ref[...] += dg

    row = lambda i: (i, 0)
    fixed = lambda i: (0, 0)
    vec = SDS((1, D), f32)
    return _hbm_call(
        body, name="dh_grad_x", out_shape=(SDS((S, D), f32), vec, vec, vec), grid=(S // tm,),
        in_specs=[pl.BlockSpec((8, tm, DA), lambda i: (0, i, 0)), pl.BlockSpec((4, D, D), lambda i: (0, 0, 0)),
                  pl.BlockSpec((tm, D), row), pl.BlockSpec((tm, D), row), pl.BlockSpec((1, D), fixed),
                  pl.BlockSpec((1, 3 * D), fixed), pl.BlockSpec((1, 3 * D), fixed), ANY_SPEC],
        out_specs=(pl.BlockSpec((tm, D), row), pl.BlockSpec((1, D), fixed), pl.BlockSpec((1, D), fixed),
                   pl.BlockSpec((1, D), fixed)),
        compiler_params=_cp(56, dimension_semantics=("arbitrary",)),
    )(dp8, w4, xx, dy, norm_g, mrow, b_ada, b_ada if after is None else after)


def _dhc_sums(dkc_raw, dvc_m, w4, ctx2, norm_g, mrow_c, b_ada, after=None):
    def body(dk_ref, dv_ref, w0_ref, w1_ref, x_ref, g_ref, m_ref, b_ref, after_ref, dsh_ref, dsc_ref, dg_ref):
        dh = _tt(dk_ref[...], w0_ref[:, DA:2 * DA]) + _tt(dv_ref[...], w1_ref[:, 0:DA])
        scale = m_ref[:, D:2 * D] + b_ref[:, D:2 * D]
        _, dshift, dscale, dg = _norm_mod_bwd(x_ref[...], dh, g_ref[...], scale)
        dsh_ref[...] = dshift
        dsc_ref[...] = dscale
        dg_ref[...] = dg

    fixed = lambda i: (0, 0)
    vec = SDS((1, D), f32)
    vspec = pl.BlockSpec((1, D), fixed)
    return _hbm_call(
        body, name="dhc_sums", out_shape=(vec, vec, vec), grid=(1,),
        in_specs=[pl.BlockSpec((L, DA), fixed), pl.BlockSpec((L, DA), fixed),
                  pl.BlockSpec((None, D, D), lambda i: (0, 0, 0)), pl.BlockSpec((None, D, D), lambda i: (1, 0, 0)),
                  pl.BlockSpec((L, D), fixed), vspec, pl.BlockSpec((1, 3 * D), fixed), pl.BlockSpec((1, 3 * D), fixed),
                  ANY_SPEC],
        out_specs=(vspec, vspec, vspec), compiler_params=_cp(32),
    )(dkc_raw, dvc_m, w4, w4, ctx2, norm_g, mrow_c, b_ada, b_ada if after is None else after)


def _rope_tables():
    nf = DH // 4
    inv = np.float32(ROPE_THETA) ** (-np.arange(nf, dtype=np.float32) / np.float32(nf))
    ang_c = np.arange(GW, dtype=np.float32)[:, None] * inv
    ang_r = np.arange(ROWS, dtype=np.float32)[:, None] * inv
    zc, zr = np.zeros((GW, 2 * nf), np.float32), np.zeros((ROWS, 2 * nf), np.float32)
    ct_cos = np.tile(np.concatenate([zc, np.cos(ang_c), np.cos(ang_c)], axis=1), (1, H))
    ct_sin = np.tile(np.concatenate([zc, -np.sin(ang_c), np.sin(ang_c)], axis=1), (1, H))
    rt_cos = np.tile(np.concatenate([np.cos(ang_r), np.cos(ang_r), zr], axis=1), (1, H))
    rt_sin = np.tile(np.concatenate([-np.sin(ang_r), np.sin(ang_r), zr], axis=1), (1, H))
    rep8 = lambda t: np.ascontiguousarray(np.broadcast_to(t[:, None, :], (ROWS, 8, DA))).reshape(ROWS * 8, DA)
    return tuple(jnp.asarray(t, f32) for t in (ct_cos, rep8(rt_cos), ct_sin, rep8(rt_sin)))


def _local_step(xx, ctx2, tgt, mrow, mrow_c, b_ada, norm_g, weights, q_norm_g, k_norm_g, rpb2, conv_w_full, conv_b,
                hooks=None):
    hooks = hooks or {}
    gq = jnp.tile(q_norm_g, (1, H))
    gk = jnp.tile(k_norm_g, (1, H))
    rope = _rope_tables()

    h = _prenorm(xx, norm_g, mrow, b_ada, 256, "prenorm_x", after=weights.get("started"))
    hc = _prenorm(ctx2, norm_g, mrow_c, b_ada, L, "prenorm_ctx")
    jv = weights["jvec"]
    p = _in_proj_own(h, weights["own"], jv)
    btb = _bias_tiles(rpb2, after=p)
    w4, started = weights["near"](btb)
    p = _in_proj_block(h, w4, p, weights["first"], "in_proj_near", after=started)
    w4 = weights["near2"](w4, p)
    p = _in_proj_block(h, w4, p, weights["second"], "in_proj_near2")
    w4, started = weights["far"](w4, p)
    p = _in_proj_block(h, w4, p, jv ^ 3, "in_proj_far", after=started)
    pc = _ctx_proj(hc, w4)
    qr, qp, kr, vh = _qk_prep(p, gq, gk, rope)
    kc, vc = _ctx_prep(pc, gk)
    o = _attn_fwd(qr, qp, kr, vh, kc, vc, btb)
    started = weights["out_arrived"](o) if "out_arrived" in weights else None
    conv_g = _conv_fwd(p, conv_w_full, conv_b, after=started)
    w_out_full = weights["out"](conv_g)
    dy, dconv, dp8, do, g_w_out, dgate, loss_sum = _out_proj_loss(o, p, conv_g, w_out_full, xx, tgt, mrow, b_ada)
    started = hooks["g_w_out"](g_w_out) if "g_w_out" in hooks else None
    dp8, g_conv_w, g_conv_b = _conv_bwd(dconv, p, conv_w_full, conv_b, dp8, after=started)
    started = hooks["after_conv"](dp8) if "after_conv" in hooks else None
    dqr, dqp, dkr, dvh, dkc, dvc, dbtb = _attn_bwd(qr, qp, kr, vh, kc, vc, btb, do, after=started)
    dp8, g_gq, g_gk = _qk_bwd(dqr, dqp, dkr, dvh, p, gq, gk, rope, dp8)
    dkc_raw, dvc_m, g_gk_c = _ctx_bwd(dkc, dvc, pc, gk)
    first = hooks.get("first_half", jnp.zeros((1,), i32))
    g_first = _grad_w_in(h, dp8, hc, dkc_raw, dvc_m, first, "grad_w_in_first")
    started = hooks["g_w_in_first"](g_first) if "g_w_in_first" in hooks else None
    g_second = _grad_w_in(h, dp8, hc, dkc_raw, dvc_m, 1 - first, "grad_w_in_second", after=started)
    started = hooks["g_w_in_second"](g_second) if "g_w_in_second" in hooks else None
    dshift_c, dscale_c, dng_c = _dhc_sums(dkc_raw, dvc_m, w4, ctx2, norm_g, mrow_c, b_ada, after=started)
    g_rpb = _bias_bwd(dbtb, after=dshift_c)
    grad_x, dshift, dscale, dng = _dh_grad_x(dp8, w4, xx, dy, norm_g, mrow, b_ada, after=g_rpb)
    return dict(loss_sum=loss_sum, grad_x=grad_x, g_w_in=(g_first, g_second), g_w_out=g_w_out, g_conv_w=g_conv_w,
                g_conv_b=g_conv_b, g_rpb=g_rpb, g_gq=g_gq, g_gk=g_gk, g_gk_c=g_gk_c, dshift=dshift, dscale=dscale,
                dgate=dgate, dng=dng, dshift_c=dshift_c, dscale_c=dscale_c, dng_c=dng_c)


def _pair_sum_w_in(g, r):
    tr = 128

    def body(g_ref, r_ref, t32_ref, tb_ref):
        t = g_ref[...] + r_ref[...]
        t32_ref[...] = t
        tb_ref[...] = t.astype(bf16)

    half = D // 2
    spec = pl.BlockSpec((4, tr, D), lambda i: (0, i, 0))
    return _hbm_call(body, name="pair_sum_w_in", out_shape=(SDS((4, half, D), f32), SDS((4, half, D), bf16)),
                     grid=(half // tr,), in_specs=[spec, spec], out_specs=(spec, spec), compiler_params=_cp(40))(g, r)


def _pair_sum_w_out(g, r, cvec):
    hr = D // 8

    def body(c_ref, g0, g1, g2, g3, r_ref, t32_ref, tb_ref):
        for q, g_ref in enumerate((g0, g1, g2, g3)):
            t = g_ref[...] + r_ref[q]
            t32_ref[q] = t
            tb_ref[q] = t.astype(bf16)

    gspecs = [pl.BlockSpec((hr, D), lambda i, c, q=q: (2 * q + c[0], 0)) for q in range(4)]
    full = pl.BlockSpec((4, hr, D), lambda i, c: (0, 0, 0))
    grid_spec = pltpu.PrefetchScalarGridSpec(num_scalar_prefetch=1, grid=(1,), in_specs=gspecs + [full],
                                             out_specs=(full, full))
    return _hbm_call(body, name="pair_sum_w_out", out_shape=(SDS((4, hr, D), f32), SDS((4, hr, D), bf16)),
                          grid_spec=grid_spec)(cvec, g, g, g, g, r)


def _chip_sum(t32, r2, jvec, name):
    rows = t32.shape[1]
    tr = min(rows, 128)

    def body(j_ref, t_ref, r_ref, u_ref):
        u_ref[...] = ((t_ref[...] + r_ref[0].astype(f32)) + r_ref[1].astype(f32)) + r_ref[2].astype(f32)

    grid_spec = pltpu.PrefetchScalarGridSpec(
        num_scalar_prefetch=1, grid=(rows // tr,),
        in_specs=[pl.BlockSpec((None, tr, D), lambda i, j: (j[0], i, 0)),
                  pl.BlockSpec((3, tr, D), lambda i, j: (0, i, 0))],
        out_specs=pl.BlockSpec((tr, D), lambda i, j: (i, 0)))
    return _hbm_call(body, name=name, out_shape=SDS((rows, D), f32), grid_spec=grid_spec)(jvec, t32, r2)


_PK = {}
_off = 0
for _name, _rows in (("dm", 24), ("dmc", 24), ("dng", 8), ("dng_c", 8), ("gq", 8), ("gk", 8), ("gk_c", 8),
                     ("rpb", H * N_DR), ("conv_b", 8), ("conv_w", 16), ("loss", 8)):
    _PK[_name] = (_off, _off + _rows)
    _off += _rows
PK_ROWS = _off
RS_B_ADA, RS_NORM_G, RS_GQ, RS_GK, RS_RPB, RS_CONV_B, RS_CONV_W, RS_DMC, RS_LOSS, RS_ROWS = (
    0, 24, 32, 40, 48, 168, 176, 192, 216, 224)


def _small_reduce(gathered):
    def body(g_ref, o_ref, dm_ref):
        a0 = _PK["dm"][0]
        dm_ref[...] = jnp.zeros_like(dm_ref)
        for b in range(8):
            for i in range(24):
                dm_ref[b:b + 1, 128 * i:128 * (i + 1)] = g_ref[b, a0 + i:a0 + i + 1, :]
        tot = g_ref[0]
        for b in range(1, 8):
            tot = tot + g_ref[b]

        def rows(name):
            a, z = _PK[name]
            return tot[a:z]

        o_ref[RS_B_ADA:RS_B_ADA + 24] = rows("dm") + rows("dmc")
        o_ref[RS_NORM_G:RS_NORM_G + 8] = rows("dng") + rows("dng_c")
        gq = jnp.broadcast_to(jnp.sum(rows("gq"), axis=0, keepdims=True), (8, 128))
        gk = jnp.broadcast_to(jnp.sum(rows("gk") + rows("gk_c"), axis=0, keepdims=True), (8, 128))
        o_ref[RS_GQ:RS_GQ + 8] = gq + pltpu.roll(gq, DH, 1)
        o_ref[RS_GK:RS_GK + 8] = gk + pltpu.roll(gk, DH, 1)
        o_ref[RS_RPB:RS_RPB + H * N_DR] = rows("rpb")
        o_ref[RS_CONV_B:RS_CONV_B + 8] = rows("conv_b")
        o_ref[RS_CONV_W:RS_CONV_W + 16] = rows("conv_w")
        dmc = rows("dmc")
        o_ref[RS_DMC:RS_DMC + 24] = dmc
        o_ref[RS_LOSS:RS_LOSS + 8] = rows("loss")
        for i in range(24):
            dm_ref[8:9, 128 * i:128 * (i + 1)] = dmc[i:i + 1]

    return _hbm_call(body, name="small_reduce", out_shape=(SDS((RS_ROWS, 128), f32), SDS((16, 3 * D), f32)),
                     in_specs=[VMEM_SPEC], out_specs=(VMEM_SPEC, VMEM_SPEC))(gathered)


def _w_ada_grad(sc16, dm16, w_ada_shard, jvec):
    ncol = w_ada_shard.shape[1]

    def body(j_ref, sc_ref, dm_ref, w_ref, g_ref, part_ref):
        dm = dm_ref[...]
        g_ref[...] = lax.dot_general(sc_ref[...], dm, (((0,), (0,)), ((), ())), precision=HIGHEST,
                                     preferred_element_type=f32)
        part_ref[...] = lax.dot_general(dm[8:16], w_ref[...], (((1,), (1,)), ((), ())), precision=HIGHEST,
                                        preferred_element_type=f32)

    fixed = lambda i, j: (0, 0)
    grid_spec = pltpu.PrefetchScalarGridSpec(
        num_scalar_prefetch=1, grid=(1,),
        in_specs=[pl.BlockSpec((16, D), fixed), pl.BlockSpec((16, ncol), lambda i, j: (0, j[0])),
                  pl.BlockSpec((D, ncol), fixed)],
        out_specs=(pl.BlockSpec((D, ncol), fixed), pl.BlockSpec((8, D), fixed)))
    return _pallas_call(body, name="w_ada_grad", out_shape=(SDS((D, ncol), f32), SDS((8, D), f32)),
                        grid_spec=grid_spec, compiler_params=_cp(40))(jvec, sc16, dm16, w_ada_shard)


def _c_ctx_grad(parts4, c_ctx):
    def body(p_ref, c_ref, o_ref):
        tot = ((p_ref[0] + p_ref[1]) + p_ref[2]) + p_ref[3]
        o_ref[...] = tot[0:1] * _dsilu(c_ref[...].reshape(1, D))

    return _pallas_call(body, name="c_ctx_grad", out_shape=SDS((1, D), f32), in_specs=[VMEM_SPEC, VMEM_SPEC],
                        out_specs=VMEM_SPEC)(parts4, c_ctx)


def _adamw(w, g, m, v, name, after=None):
    rows, cols = w.shape
    tr = 256 if rows % 256 == 0 else rows

    def body(w_ref, g_ref, m_ref, v_ref, after_ref, d_ref, m2_ref, v2_ref):
        gv = g_ref[...]
        m2 = ADAM_B1 * m_ref[...] + (1.0 - ADAM_B1) * gv
        v2 = ADAM_B2 * v_ref[...] + (1.0 - ADAM_B2) * jnp.square(gv)
        m_hat = m2 / (1.0 - ADAM_B1 ** ADAM_STEP)
        v_hat = v2 / (1.0 - ADAM_B2 ** ADAM_STEP)
        d_ref[...] = -ADAM_LR * (m_hat / (jnp.sqrt(v_hat) + ADAM_EPS) + ADAM_WD * w_ref[...])
        m2_ref[...] = m2
        v2_ref[...] = v2

    spec = pl.BlockSpec((tr, cols), lambda i: (i, 0))
    shp = SDS((rows, cols), f32)
    return _hbm_call(body, name=name, out_shape=(shp, shp, shp), grid=(rows // tr,), in_specs=[spec] * 4 + [ANY_SPEC],
                     out_specs=(spec, spec, spec))(w, g, m, v, g if after is None else after)


def _adamw_halves(w, g_mine, g_other, m, v, cvec, name, after=None):
    rows, cols = w.shape
    half = rows // 2
    tr = min(256, half)
    per_half = half // tr

    def body(c_ref, w_ref, ga_ref, gb_ref, m_ref, v_ref, after_ref, g_ref, d_ref, m2_ref, v2_ref):
        in_my_half = (pl.program_id(0) // per_half) == c_ref[0]
        gv = jnp.where(in_my_half, ga_ref[...], gb_ref[...])
        g_ref[...] = gv
        m2 = ADAM_B1 * m_ref[...] + (1.0 - ADAM_B1) * gv
        v2 = ADAM_B2 * v_ref[...] + (1.0 - ADAM_B2) * jnp.square(gv)
        m_hat = m2 / (1.0 - ADAM_B1 ** ADAM_STEP)
        v_hat = v2 / (1.0 - ADAM_B2 ** ADAM_STEP)
        d_ref[...] = -ADAM_LR * (m_hat / (jnp.sqrt(v_hat) + ADAM_EPS) + ADAM_WD * w_ref[...])
        m2_ref[...] = m2
        v2_ref[...] = v2

    full = pl.BlockSpec((tr, cols), lambda i, c: (i, 0))
    part = pl.BlockSpec((tr, cols), lambda i, c: (i % per_half, 0))
    shp = SDS((rows, cols), f32)
    grid_spec = pltpu.PrefetchScalarGridSpec(num_scalar_prefetch=1, grid=(rows // tr,),
                                             in_specs=[full, part, part, full, full, ANY_SPEC], out_specs=(full,) * 4)
    return _hbm_call(body, name=name, out_shape=(shp,) * 4, grid_spec=grid_spec)(
        cvec, w, g_mine, g_other, m, v, cvec if after is None else after)


def _adam_math(w, g, m, v):
    m2 = ADAM_B1 * m + (1.0 - ADAM_B1) * g
    v2 = ADAM_B2 * v + (1.0 - ADAM_B2) * jnp.square(g)
    m_hat = m2 / (1.0 - ADAM_B1 ** ADAM_STEP)
    v_hat = v2 / (1.0 - ADAM_B2 ** ADAM_STEP)
    return -ADAM_LR * (m_hat / (jnp.sqrt(v_hat) + ADAM_EPS) + ADAM_WD * w), m2, v2


def _adamw_small(red, g_c_ctx, jvec, ws, ms, vs):
    n = len(ws)

    def body(*refs):
        red_ref, gc_ref, j_ref = refs[:3]
        w_refs, m_refs, v_refs = refs[3:3 + n], refs[3 + n:3 + 2 * n], refs[3 + 2 * n:3 + 3 * n]
        outs = refs[3 + 3 * n:]
        g_out, d_out, m_out, v_out = outs[:n], outs[n:2 * n], outs[2 * n:3 * n], outs[3 * n:]
        chip = j_ref[0]
        lanes = lambda i: (slice(None), slice(128 * i, 128 * (i + 1)))
        row = lambda r0, i: (lambda: red_ref[r0 + i:r0 + i + 1, :])
        whole = (slice(None), slice(None))
        chunks = [
            [((slice(None),), lambda: gc_ref[...].reshape(D))],
            [(lanes(i), row(RS_B_ADA, i)) for i in range(3 * D // 128)],
            [(lanes(i), row(RS_NORM_G, i)) for i in range(D // 128)],
            [(whole, lambda: red_ref[RS_GQ:RS_GQ + 1, 0:DH])],
            [(whole, lambda: red_ref[RS_GK:RS_GK + 1, 0:DH])],
            [((dr,), (lambda dr=dr: red_ref[pl.ds(RS_RPB + dr, H, stride=N_DR), 0:N_DC])) for dr in range(N_DR)],
            [((r,), (lambda r=r: red_ref[pl.ds(RS_CONV_W + 4 * r + chip, 1), :])) for r in range(3)],
            [(lanes(i), row(RS_CONV_B, i)) for i in range(DC // 128)],
        ]
        for a in range(n):
            for idx, grad in chunks[a]:
                g = grad()
                d, m2, v2 = _adam_math(w_refs[a][idx], g, m_refs[a][idx], v_refs[a][idx])
                g_out[a][idx] = g
                d_out[a][idx] = d
                m_out[a][idx] = m2
                v_out[a][idx] = v2

    shapes = [SDS(w.shape, f32) for w in ws]
    res = _pallas_call(body, name="adamw_small", out_shape=shapes * 4,
                       in_specs=[VMEM_SPEC, VMEM_SPEC, SMEM_SPEC] + [VMEM_SPEC] * (3 * n),
                       out_specs=[VMEM_SPEC] * (4 * n))(red, g_c_ctx, jvec, *ws, *ms, *vs)
    return [list(res[k * n:(k + 1) * n]) for k in range(4)]


def _rows128(a):
    return a.reshape(-1, 128)


def kernel(x, c, ctx, c_ctx, w_ada, b_ada, norm_g, w_in, q_norm_g, k_norm_g, rpb, conv_w, conv_b, w_out, loss_target, m_c_ctx, m_w_ada, m_b_ada, m_norm_g, m_w_in, m_q_norm_g, m_k_norm_g, m_rpb, m_conv_w, m_conv_b, m_w_out, v_c_ctx, v_w_ada, v_b_ada, v_norm_g, v_w_in, v_q_norm_g, v_k_norm_g, v_rpb, v_conv_w, v_conv_b, v_w_out):
    xi, yi, ci = lax.axis_index("x"), lax.axis_index("y"), lax.axis_index("c")
    dev = 4 * xi + 2 * yi + ci
    chip = 2 * xi + yi
    cvec = jnp.reshape(ci, (1,)).astype(i32)
    jvec = jnp.reshape(chip, (1,)).astype(i32)
    w_ada_s = w_ada[0]
    ncol = w_ada_s.shape[1]

    gc = _split_start([_to_slot(c.reshape(8, 128), 8, dev)], [], 7, _gather8_copies, "gather_c_start")
    wo4c = _cast_to_slot(w_out[0], jvec, "cast_w_out", after=gc[3])
    w4c = _cast_to_slot(w_in[0], jvec, "cast_w_in", after=wo4c)
    (c8,), _ = _split_wait(gc[0], gc[1], [gc[2]], [], w4c, _gather8_copies, "gather_c_wait")
    cc = jnp.concatenate([c8.reshape(8, D), c_ctx.reshape(1, D), jnp.zeros((7, D), f32)], axis=0)
    m_shard, sc16 = _adaln_shard(cc, w_ada_s)

    conv_w_pad = jnp.pad(conv_w[0], ((0, 5), (0, 0)))
    gm = _split_start([_to_slot(m_shard, 4, chip), _to_slot(conv_w_pad, 4, chip)], [], 6, _gather4_copies,
                      "gather_mod_start")

    all_k = [(0, 0, 0), (0, 0, 1), (0, 0, 2)]
    sem_a, rem_a, w4s, token = _split_start([w4c], [], 1, _near_copies, "weights_near_start", after=gm[4])
    (m4, cw4), _ = _split_wait(gm[0], gm[1], [gm[2], gm[3]], [], token, _gather4_copies, "gather_mod_wait")
    m_full = jnp.transpose(m4, (1, 0, 2)).reshape(16, 4 * ncol)
    mrow = lax.dynamic_slice(m_full, (dev, 0), (1, 3 * D))
    mrow_c = m_full[8:9]
    conv_w_full = jnp.transpose(cw4[:, 0:3, :], (1, 0, 2)).reshape(3, DC)
    waves = {}

    def near(after):
        (w4w,), _ = _split_wait(sem_a, rem_a, [w4s], [], after, _near_copies, "weights_near_wait")
        sem_b, rem_b, w4b, started = _split_start([w4w], [], 2, _pass_relay_copies, "weights_pass_start")
        waves["pass"] = (sem_b, rem_b)
        return w4b, started

    def near2(w4, after):
        (w4w,), _ = _split_wait(*waves["pass"], [w4], [], after, _pass_copy, "weights_pass_wait")
        return w4w

    def far(w4, after):
        (w4w,), _ = _split_wait(*waves["pass"], [w4], [], after, _relay_copy, "weights_far_wait")
        w4x, sem_c, rem_c, wo4s, started = _forward_then_start(w4w, wo4c, all_k, "weights_far_forward_out_start")
        (w4f,), _ = _split_wait(sem_c, rem_c, [w4x], [], started, _diag_forward_copy, "weights_far_forward_wait")
        waves["out"] = (sem_c, rem_c, wo4s)
        return w4f, started

    def w_out_arrived(after):
        sem_c, rem_c, wo4s = waves["out"]
        (wow,) = _halves_wait(sem_c, rem_c, [wo4s], after, all_k, "weights_out_wait")
        sem_d, rem_d, wof, started = _split_start([wow], [], 3, _forward_copies, "weights_out_forward_start")
        waves["out_forward"] = (sem_d, rem_d, wof)
        return started

    def w_out_gathered(after):
        sem_d, rem_d, wof = waves["out_forward"]
        (wo,), _ = _split_wait(sem_d, rem_d, [wof], [], after, _forward_copies, "weights_out_forward_wait")
        return wo.reshape(D, D)

    weights = dict(own=w_in[0], jvec=jvec, started=token, first=jvec ^ (1 + cvec), second=jvec ^ (2 - cvec),
                   near=near, near2=near2, far=far, out_arrived=w_out_arrived, out=w_out_gathered)

    exchange = _exchange_copies
    pending = {}

    def on_g_w_out(g_w_out):
        out = _split_start([g_w_out], [SDS((4, D // 8, D), f32)], 4, exchange, "grad_out_pair_start")
        pending["ex_out"] = out
        return out[4]

    def after_conv(dp8):
        ssem_o, rsem_o, g_o, land_o, _ = pending["ex_out"]
        (g_o,), (ex_o,) = _split_wait(ssem_o, rsem_o, [g_o], [land_o], dp8, exchange, "grad_out_pair_wait")
        to32, tob = _pair_sum_w_out(g_o, ex_o, cvec)
        out = _split_start([tob], [SDS((3, D // 8, D), bf16)], 3, _scatter_copies, "grad_out_chip_start")
        pending["sc_out"] = (out, to32)
        return out[4]

    def on_g_w_in_first(g_first):
        out = _split_start([g_first], [SDS((4, D // 2, D), f32)], 4, _block_exchange_copies, "grad_pair_start")
        pending["ex"] = (out[0], out[1], [out[2]], [out[3]])
        return out[4]

    def on_g_w_in_second(g_second):
        ex_ssem, ex_rsem, ex_srcs, ex_lands = pending["ex"]
        _, ex = _split_wait(ex_ssem, ex_rsem, ex_srcs, ex_lands, g_second, _block_exchange_copies, "grad_pair_wait")
        t32, tb = _pair_sum_w_in(g_second, ex[0])
        out = _split_start([tb], [SDS((3, D // 2, D), bf16)], 3, _scatter_copies, "grad_chip_start")
        pending["sc_in"] = (out, t32)
        return out[4]

    r = _local_step(x[0], ctx[0], loss_target[0], mrow, mrow_c, b_ada, norm_g, weights, q_norm_g, k_norm_g,
                    rpb[0], conv_w_full, conv_b,
                    dict(g_w_out=on_g_w_out, after_conv=after_conv, first_half=1 - cvec,
                         g_w_in_first=on_g_w_in_first, g_w_in_second=on_g_w_in_second))
    sc_in, t32 = pending["sc_in"]
    _, (r2,) = _split_wait(sc_in[0], sc_in[1], [sc_in[2]], [sc_in[3]], r["dng"], _scatter_copies, "grad_chip_wait")
    u_in = _chip_sum(t32, r2, jvec, "chip_sum_w_in")
    sc_o, to32 = pending["sc_out"]
    _, (ro2,) = _split_wait(sc_o[0], sc_o[1], [sc_o[2]], [sc_o[3]], u_in, _scatter_copies, "grad_out_chip_wait")
    u_out = _chip_sum(to32, ro2, jvec, "chip_sum_w_out")
    swap = _split_start([u_in, u_out], [SDS(u_in.shape, f32), SDS(u_out.shape, f32)], 2, _swap_copies,
                        "grad_pair_swap_start")

    dm = jnp.concatenate([r["dshift"], r["dscale"], r["dgate"]], axis=1)
    dmc = jnp.concatenate([r["dshift_c"], r["dscale_c"], jnp.zeros((1, D), f32)], axis=1)
    pack_parts = [_rows128(dm), _rows128(dmc), _rows128(r["dng"]), _rows128(r["dng_c"]), _rows128(r["g_gq"]),
                  _rows128(r["g_gk"]), _rows128(r["g_gk_c"]), r["g_rpb"].reshape(H * N_DR, 128),
                  _rows128(r["g_conv_b"]), _rows128(r["g_conv_w"][0:3]), jnp.pad(r["loss_sum"], ((0, 0), (0, 127)))]
    pack = jnp.concatenate([jnp.pad(p, ((0, -p.shape[0] % 8), (0, 0))) for p in pack_parts], axis=0)
    assert pack.shape[0] == PK_ROWS
    gs = _split_start([_to_slot(pack, 8, dev)], [], 7, _gather8_copies, "gather_small_start", after=swap[6])
    (u_in, u_out), (o_in, o_out) = _split_wait(swap[0], swap[1], swap[2:4], swap[4:6], gs[3], _swap_copies,
                                               "grad_pair_swap_wait")
    g_w_in_s, d_w_in, nm_w_in, nv_w_in = _adamw_halves(w_in[0], u_in, o_in, m_w_in[0], v_w_in[0], cvec, "adamw_w_in",
                                                       after=gs[3])
    g_w_out_s, d_w_out, nm_w_out, nv_w_out = _adamw_halves(w_out[0], u_out, o_out, m_w_out[0], v_w_out[0], cvec,
                                                           "adamw_w_out", after=nm_w_in)
    (gathered,), _ = _split_wait(gs[0], gs[1], [gs[2]], [], nm_w_out, _gather8_copies, "gather_small_wait")
    red, dm16 = _small_reduce(gathered)
    loss = red[RS_LOSS, 0] * (0.5 / D)

    g_w_ada_s, cpart = _w_ada_grad(sc16, dm16, w_ada_s, jvec)
    gcp = _split_start([_to_slot(cpart, 4, chip)], [], 3, _gather4_copies, "gather_c_ctx_parts_start")
    d_w_ada, nm_w_ada, nv_w_ada = _adamw(w_ada_s, g_w_ada_s, m_w_ada[0], v_w_ada[0], "adamw_w_ada", after=gcp[3])
    (cparts4,), _ = _split_wait(gcp[0], gcp[1], [gcp[2]], [], nm_w_ada, _gather4_copies, "gather_c_ctx_parts_wait")
    g_c_ctx = _c_ctx_grad(cparts4, c_ctx)

    t_rpb = lambda a: jnp.transpose(a, (0, 2, 1, 3)).reshape(N_DR, H, N_DC)
    t_cw = lambda a: jnp.transpose(a, (1, 0, 2))
    small = _adamw_small(
        red, g_c_ctx, jvec,
        [c_ctx, b_ada, norm_g, q_norm_g, k_norm_g, t_rpb(rpb), t_cw(conv_w), conv_b],
        [m_c_ctx, m_b_ada, m_norm_g, m_q_norm_g, m_k_norm_g, t_rpb(m_rpb), t_cw(m_conv_w), m_conv_b],
        [v_c_ctx, v_b_ada, v_norm_g, v_q_norm_g, v_k_norm_g, t_rpb(v_rpb), t_cw(v_conv_w), v_conv_b])
    for kind in small:
        kind[5] = jnp.transpose(kind[5].reshape(1, N_DR, H, N_DC), (0, 2, 1, 3))
        kind[6] = jnp.transpose(kind[6], (1, 0, 2))

    def ordered(kind, big_w_ada, big_w_in, big_w_out):
        s_c_ctx, s_b_ada, s_norm_g, s_q, s_k, s_rpb, s_conv_w, s_conv_b = small[kind]
        return [s_c_ctx, big_w_ada[None], s_b_ada, s_norm_g, big_w_in[None], s_q, s_k, s_rpb, s_conv_w,
                s_conv_b, big_w_out[None]]

    grads = ordered(0, g_w_ada_s, g_w_in_s, g_w_out_s)
    deltas = ordered(1, d_w_ada, d_w_in, d_w_out)
    new_m = ordered(2, nm_w_ada, nm_w_in, nm_w_out)
    new_v = ordered(3, nv_w_ada, nv_w_in, nv_w_out)
    return (loss, r["grad_x"][None], *grads, *deltas, *new_m, *new_v)
```

```python
import functools

import jax
import jax.numpy as jnp
import numpy as np
from jax import lax
from jax.experimental import pallas as pl
from jax.experimental.pallas import tpu as pltpu

f32, bf16, i32 = jnp.float32, jnp.bfloat16, jnp.int32
MESH = pl.DeviceIdType.MESH
HIGHEST = lax.Precision.HIGHEST

D = 1024
S = 2048
L = 256
GW = 64
ROWS = S // GW
H = 8
DH = 64
DA = H * DH
DC = 512
WIN_H, WIN_W = 8, 16
N_DR, N_DC = 2 * WIN_H - 1, 2 * WIN_W - 1
RMS_EPS = 1e-6
ROPE_THETA = 10000.0
QK_SCALE = DH ** -0.5
NEG = -1e30

QB = 128
NQB = S // QB
KR = 9
KB = KR * GW
TILE_GEOM = ((0, 0), (2, 0), (4, 0), (28, 23), (30, 23))
NT = len(TILE_GEOM)

ADAM_LR, ADAM_B1, ADAM_B2, ADAM_EPS, ADAM_WD, ADAM_STEP = 0.001, 0.9, 0.999, 1e-08, 0.01, 10

VMEM_SPEC = pl.BlockSpec(memory_space=pltpu.VMEM)
ANY_SPEC = pl.BlockSpec(memory_space=pl.ANY)
SMEM_SPEC = pl.BlockSpec(memory_space=pltpu.SMEM)
SDS = jax.ShapeDtypeStruct


_pallas_call = pl.pallas_call


def _hbm_call(body, *, out_shape, in_specs=None, out_specs=None, grid_spec=None, **kw):
    n_pre = 0
    if grid_spec is not None:
        ispecs, ospecs, n_pre = grid_spec.in_specs, grid_spec.out_specs, grid_spec.num_scalar_prefetch
        kw["grid_spec"] = grid_spec
    else:
        ispecs, ospecs = in_specs, out_specs
        kw.update(in_specs=in_specs, out_specs=out_specs)

    def blocked(spec):
        return isinstance(spec, pl.BlockSpec) and spec.block_shape is not None

    single = not isinstance(out_shape, (tuple, list))
    shapes = [out_shape] if single else list(out_shape)
    ospec_list = list(ospecs) if isinstance(ospecs, (tuple, list)) else [ospecs]
    shapes = [pltpu.HBM(s.shape, s.dtype) if blocked(sp) else s for s, sp in zip(shapes, ospec_list)]
    call = _pallas_call(body, out_shape=shapes[0] if single else tuple(shapes), **kw)

    def run(*args):
        arrays = [pltpu.with_memory_space_constraint(a, pltpu.HBM) if blocked(sp) else a
                  for a, sp in zip(args[n_pre:], ispecs)]
        return call(*args[:n_pre], *arrays)

    return run


def _cp(vmem_mb=None, **kw):
    if vmem_mb is not None:
        kw["vmem_limit_bytes"] = vmem_mb << 20
    return pltpu.CompilerParams(**kw)


def _silu(z):
    return z * jax.nn.sigmoid(z)


def _dsilu(z):
    sg = jax.nn.sigmoid(z)
    return sg * (1.0 + z * (1.0 - sg))


def _row_start(i):
    return min(max(i - WIN_H // 2, 0), ROWS - WIN_H)


def _my_pos():
    return lax.axis_index("x"), lax.axis_index("y"), lax.axis_index("c")


def _flip(v, bit):
    return 1 - v if bit else v


def _swap_copies(srcs, lands, ssem, rsem):
    x, y, c = _my_pos()
    return [pltpu.make_async_remote_copy(src_ref=srcs[a], dst_ref=lands[a], send_sem=ssem.at[a], recv_sem=rsem.at[a],
                                         device_id=(x, y, 1 - c), device_id_type=MESH) for a in range(len(srcs))]


HBM_SPEC = pl.BlockSpec(memory_space=pltpu.HBM)
SEM_SPEC = pl.BlockSpec(memory_space=pltpu.SEMAPHORE)
DATAFLOW = pltpu.SideEffectType.DATAFLOW_SIDE_EFFECTING


def _peer_chips(x, y, c):
    out = []
    for k in range(1, 4):
        px, py = _flip(x, (k >> 1) & 1), _flip(y, k & 1)
        out.append(((px, py, c), 2 * px + py))
    return out


def _half_copies(srcs, dsts, ssem, rsem, which):
    x, y, c = _my_pos()
    j = 2 * x + y
    peers = _peer_chips(x, y, c)
    pairs = []
    for pos, group, k in which:
        half = srcs[pos].shape[1] // 2
        mine = pl.ds(pl.multiple_of(c * half, 8), half)
        dev, pj = peers[k]
        sem = 3 * group + k
        send = pltpu.make_async_remote_copy(src_ref=srcs[pos].at[j, mine], dst_ref=dsts[pos].at[j, mine],
                                            send_sem=ssem.at[sem], recv_sem=rsem.at[sem], device_id=dev,
                                            device_id_type=MESH)
        arrive = pltpu.make_async_remote_copy(src_ref=srcs[pos].at[j, mine], dst_ref=dsts[pos].at[pj, mine],
                                              send_sem=ssem.at[sem], recv_sem=rsem.at[sem], device_id=dev,
                                              device_id_type=MESH)
        pairs.append((send, arrive))
    return pairs


def _halves_wait(ssem, rsem, bigs, after, which, name):
    nb = len(bigs)

    def body(*refs):
        b_in = refs[:nb]
        ssem_ref, rsem_ref = refs[nb], refs[nb + 1]
        for send, arrive in _half_copies(b_in, b_in, ssem_ref, rsem_ref, which):
            send.wait_send()
            arrive.wait_recv()

    return _hbm_call(
        body, name=name, out_shape=tuple(pltpu.HBM(b.shape, b.dtype) for b in bigs),
        in_specs=[HBM_SPEC] * nb + [SEM_SPEC, SEM_SPEC, ANY_SPEC], out_specs=tuple([HBM_SPEC] * nb),
        input_output_aliases={a: a for a in range(nb)}, compiler_params=_cp(has_side_effects=DATAFLOW),
    )(*bigs, ssem, rsem, after)


FORWARD_SEM = 3


def _diag_forward_copy(srcs, dsts, ssem, rsem):
    x, y, c = _my_pos()
    half = srcs[0].shape[1] // 2
    diag = 3 - (2 * x + y)
    mine = pl.ds(pl.multiple_of(c * half, 8), half)
    other = pl.ds(pl.multiple_of((1 - c) * half, 8), half)
    return [_Copy(srcs[0].at[diag, mine], dsts[0].at[diag, mine], dsts[0].at[diag, other], ssem.at[FORWARD_SEM],
                  rsem.at[FORWARD_SEM], (x, y, 1 - c))]


def _forward_then_start(fwd, big, order, name):
    def body(f_in, b_in, f_out, ssem, rsem, b_out, token):
        _diag_forward_copy([f_in], [f_out], ssem, rsem)[0].start()
        for send, _ in _half_copies([b_in], [b_out], ssem, rsem, order):
            send.start()
        token[...] = jnp.zeros_like(token)

    n_sem = FORWARD_SEM + 1
    out_shape = (pltpu.HBM(fwd.shape, fwd.dtype), pltpu.SemaphoreType.DMA((n_sem,)), pltpu.SemaphoreType.DMA((n_sem,)),
                 pltpu.HBM(big.shape, big.dtype), SDS((8, 128), f32))
    return _hbm_call(
        body, name=name, out_shape=out_shape, in_specs=[HBM_SPEC, HBM_SPEC],
        out_specs=(HBM_SPEC, SEM_SPEC, SEM_SPEC, HBM_SPEC, VMEM_SPEC), input_output_aliases={0: 0, 1: 3},
        compiler_params=_cp(has_side_effects=DATAFLOW),
    )(*[pltpu.with_memory_space_constraint(b, pltpu.HBM) for b in (fwd, big)])


def _cast_to_slot(w, jvec, name, after=None):
    rows, cols = w.shape
    tr = 256

    def body(j_ref, w_ref, after_ref, o_ref):
        o_ref[...] = w_ref[...].astype(bf16)

    grid_spec = pltpu.PrefetchScalarGridSpec(
        num_scalar_prefetch=1, grid=(rows // tr,),
        in_specs=[pl.BlockSpec((tr, cols), lambda i, j: (i, 0)), ANY_SPEC],
        out_specs=pl.BlockSpec((None, tr, cols), lambda i, j: (j[0], i, 0)))
    return _hbm_call(body, name=name, out_shape=SDS((4, rows, cols), bf16),
                     grid_spec=grid_spec)(jvec, w, jvec if after is None else after)


def _exchange_copies(srcs, lands, ssem, rsem):
    x, y, c = _my_pos()
    half = srcs[0].shape[0] // 8
    cps = []
    for jb in range(4):
        src = srcs[0].at[pl.ds(pl.multiple_of((2 * jb + 1 - c) * half, 8), half)]
        cps.append(pltpu.make_async_remote_copy(src_ref=src, dst_ref=lands[0].at[jb], send_sem=ssem.at[jb],
                                                recv_sem=rsem.at[jb], device_id=(x, y, 1 - c), device_id_type=MESH))
    return cps


def _block_exchange_copies(srcs, lands, ssem, rsem):
    x, y, c = _my_pos()
    return [pltpu.make_async_remote_copy(src_ref=srcs[0].at[jb], dst_ref=lands[0].at[jb], send_sem=ssem.at[jb],
                                         recv_sem=rsem.at[jb], device_id=(x, y, 1 - c), device_id_type=MESH)
            for jb in range(4)]


def _scatter_copies(srcs, lands, ssem, rsem):
    x, y, c = _my_pos()
    cps = []
    for a in range(len(srcs)):
        for k, (dev, pj) in enumerate(_peer_chips(x, y, c)):
            cps.append(pltpu.make_async_remote_copy(src_ref=srcs[a].at[pj], dst_ref=lands[a].at[k],
                                                    send_sem=ssem.at[3 * a + k], recv_sem=rsem.at[3 * a + k],
                                                    device_id=dev, device_id_type=MESH))
    return cps


class _Copy:
    def __init__(self, src, dst, arrive, ssem, rsem, dev):
        make = lambda to: pltpu.make_async_remote_copy(src_ref=src, dst_ref=to, send_sem=ssem, recv_sem=rsem,
                                                       device_id=dev, device_id_type=MESH)
        send, arrival = make(dst), make(arrive)
        self.start, self.wait_send, self.wait_recv = send.start, send.wait_send, arrival.wait_recv


def _toward(x, y, along_x):
    return x + along_x * (1 - 2 * x), y + (1 - along_x) * (1 - 2 * y)


def _near_copies(srcs, dsts, ssem, rsem):
    x, y, c = _my_pos()
    px, py = _toward(x, y, c)
    j = 2 * x + y
    return [_Copy(srcs[0].at[j], dsts[0].at[j], dsts[0].at[2 * px + py], ssem.at[0], rsem.at[0], (px, py, c))]


def _pass_copy(srcs, dsts, ssem, rsem):
    x, y, c = _my_pos()
    px, py = _toward(x, y, c)
    qx, qy = _toward(x, y, 1 - c)
    got = 2 * px + py
    return [_Copy(srcs[0].at[got], dsts[0].at[got], dsts[0].at[2 * qx + qy], ssem.at[0], rsem.at[0], (x, y, 1 - c))]


def _relay_copy(srcs, dsts, ssem, rsem):
    x, y, c = _my_pos()
    px, py = _toward(x, y, c)
    qx, qy = _toward(x, y, 1 - c)
    half = srcs[0].shape[1] // 2
    mine = pl.ds(pl.multiple_of(c * half, 8), half)
    got, diag = 2 * px + py, 3 - (2 * x + y)
    return [_Copy(srcs[0].at[got, mine], dsts[0].at[got, mine], dsts[0].at[diag, mine], ssem.at[1], rsem.at[1],
                  (qx, qy, c))]


def _forward_copies(srcs, dsts, ssem, rsem):
    x, y, c = _my_pos()
    half = srcs[0].shape[1] // 2
    mine = pl.ds(pl.multiple_of(c * half, 8), half)
    other = pl.ds(pl.multiple_of((1 - c) * half, 8), half)
    return [_Copy(srcs[0].at[pj, mine], dsts[0].at[pj, mine], dsts[0].at[pj, other], ssem.at[k], rsem.at[k],
                  (x, y, 1 - c)) for k, (_, pj) in enumerate(_peer_chips(x, y, c))]


def _pass_relay_copies(srcs, dsts, ssem, rsem):
    return _pass_copy(srcs, dsts, ssem, rsem) + _relay_copy(srcs, dsts, ssem, rsem)


def _gather8_copies(srcs, dsts, ssem, rsem):
    x, y, c = _my_pos()
    me = 4 * x + 2 * y + c
    cps = []
    for a in range(len(srcs)):
        for k in range(1, 8):
            tgt = (_flip(x, (k >> 2) & 1), _flip(y, (k >> 1) & 1), _flip(c, k & 1))
            cps.append(_Copy(srcs[a].at[me], dsts[a].at[me], dsts[a].at[4 * tgt[0] + 2 * tgt[1] + tgt[2]],
                             ssem.at[7 * a + k - 1], rsem.at[7 * a + k - 1], tgt))
    return cps


def _gather4_copies(srcs, dsts, ssem, rsem):
    x, y, c = _my_pos()
    j = 2 * x + y
    cps = []
    for a in range(len(srcs)):
        for k, (dev, pj) in enumerate(_peer_chips(x, y, c)):
            cps.append(_Copy(srcs[a].at[j], dsts[a].at[j], dsts[a].at[pj], ssem.at[3 * a + k], rsem.at[3 * a + k], dev))
    return cps


def _to_slot(a, n, i):
    return lax.dynamic_update_slice(jnp.zeros((n,) + a.shape, a.dtype), a[None], (i,) + (0,) * a.ndim)


def _split_start(srcs, land_shapes, n_cp, make, name, after=None):
    ns, nl = len(srcs), len(land_shapes)
    n_in = ns + nl + (after is not None)

    def body(*refs):
        s_in = refs[:ns]
        ssem, rsem = refs[n_in], refs[n_in + 1]
        s_out = refs[n_in + 2:n_in + 2 + ns]
        l_out = refs[n_in + 2 + ns:n_in + 2 + ns + nl]
        token = refs[n_in + 2 + ns + nl]
        for cp in make(s_in, l_out if nl else s_out, ssem, rsem):
            cp.start()
        token[...] = jnp.zeros_like(token)

    lands = [pltpu.with_memory_space_constraint(lax.empty(sh.shape, sh.dtype), pltpu.HBM) for sh in land_shapes]
    out_shape = (pltpu.SemaphoreType.DMA((n_cp,)), pltpu.SemaphoreType.DMA((n_cp,)),
                 *[pltpu.HBM(b.shape, b.dtype) for b in srcs], *[pltpu.HBM(b.shape, b.dtype) for b in land_shapes],
                 SDS((8, 128), f32))
    return _hbm_call(
        body, name=name, out_shape=out_shape, in_specs=[HBM_SPEC] * (ns + nl) + [ANY_SPEC] * (after is not None),
        out_specs=(SEM_SPEC, SEM_SPEC, *[HBM_SPEC] * (ns + nl), VMEM_SPEC),
        input_output_aliases={i: 2 + i for i in range(ns + nl)}, compiler_params=_cp(has_side_effects=DATAFLOW),
    )(*[pltpu.with_memory_space_constraint(b, pltpu.HBM) for b in srcs], *lands, *([] if after is None else [after]))


def _split_wait(ssem, rsem, srcs, lands, after, make, name):
    ns, nl = len(srcs), len(lands)

    def body(*refs):
        s_in, l_in = refs[:ns], refs[ns:ns + nl]
        ssem_ref, rsem_ref = refs[ns + nl], refs[ns + nl + 1]
        for cp in make(s_in, l_in if nl else s_in, ssem_ref, rsem_ref):
            cp.wait_send()
            cp.wait_recv()

    outs = _hbm_call(
        body, name=name, out_shape=tuple(pltpu.HBM(b.shape, b.dtype) for b in (*srcs, *lands)),
        in_specs=[HBM_SPEC] * (ns + nl) + [SEM_SPEC, SEM_SPEC, ANY_SPEC], out_specs=tuple([HBM_SPEC] * (ns + nl)),
        input_output_aliases={i: i for i in range(ns + nl)}, compiler_params=_cp(has_side_effects=DATAFLOW),
    )(*srcs, *lands, ssem, rsem, after)
    return list(outs[:ns]), list(outs[ns:])


def _adaln_shard(cc, w_ada_shard):
    def body(c_ref, w_ref, m_ref, sc_ref):
        sc = _silu(c_ref[...])
        sc_ref[...] = sc
        m_ref[...] = jnp.dot(sc, w_ref[...], precision=HIGHEST, preferred_element_type=f32)

    return _hbm_call(
        body, name="adaln_shard", out_shape=(SDS((16, w_ada_shard.shape[1]), f32), SDS((16, D), f32)),
        in_specs=[VMEM_SPEC, VMEM_SPEC], out_specs=(VMEM_SPEC, VMEM_SPEC), compiler_params=_cp(32),
    )(cc, w_ada_shard)


def _prenorm(xx, norm_g, mrow, b_ada, tm, name, after=None):
    n = xx.shape[0]

    def body(x_ref, g_ref, m_ref, b_ref, after_ref, h_ref):
        x = x_ref[...]
        shift = m_ref[:, 0:D] + b_ref[:, 0:D]
        scale = m_ref[:, D:2 * D] + b_ref[:, D:2 * D]
        r = lax.rsqrt(jnp.mean(x * x, axis=-1, keepdims=True) + RMS_EPS)
        y = (x * r) * g_ref[...]
        h_ref[...] = (y * (1.0 + scale) + shift).astype(bf16)

    row = lambda i: (i, 0)
    fixed = lambda i: (0, 0)
    return _hbm_call(
        body, name=name, out_shape=SDS((n, D), bf16), grid=(n // tm,),
        in_specs=[pl.BlockSpec((tm, D), row), pl.BlockSpec((1, D), fixed), pl.BlockSpec((1, 3 * D), fixed),
                  pl.BlockSpec((1, 3 * D), fixed), ANY_SPEC],
        out_specs=pl.BlockSpec((tm, D), row),
    )(xx, norm_g, mrow, b_ada, b_ada if after is None else after)


def _in_proj_own(h, w_own, jvec):
    tm = 512

    def body(j_ref, h_ref, w_ref, p_ref):
        p_ref[...] = jnp.dot(h_ref[...], w_ref[...].astype(bf16), preferred_element_type=f32)

    grid_spec = pltpu.PrefetchScalarGridSpec(
        num_scalar_prefetch=1, grid=(S // tm,),
        in_specs=[pl.BlockSpec((tm, D), lambda i, j: (i, 0)), pl.BlockSpec((D, D), lambda i, j: (0, 0))],
        out_specs=pl.BlockSpec((tm, D), lambda i, j: (i, j[0])))
    return _hbm_call(body, name="in_proj_own", out_shape=SDS((S, 4 * D), f32), grid_spec=grid_spec,
                     compiler_params=_cp(40))(jvec, h, w_own)


def _in_proj_block(h, w4, p, bvec, name, after=None):
    tm = 512

    def body(b_ref, h_ref, w_ref, p_in_ref, after_ref, p_ref):
        p_ref[...] = jnp.dot(h_ref[...], w_ref[...], preferred_element_type=f32)

    grid_spec = pltpu.PrefetchScalarGridSpec(
        num_scalar_prefetch=1, grid=(S // tm,),
        in_specs=[pl.BlockSpec((tm, D), lambda i, b: (i, 0)), pl.BlockSpec((None, D, D), lambda i, b: (b[0], 0, 0)),
                  ANY_SPEC, ANY_SPEC],
        out_specs=pl.BlockSpec((tm, D), lambda i, b: (i, b[0])))
    return _hbm_call(body, name=name, out_shape=SDS((S, 4 * D), f32), grid_spec=grid_spec,
                     input_output_aliases={3: 0})(bvec, h, w4, p, bvec if after is None else after)


def _ctx_proj(hc, w4):
    def body(h_ref, w0_ref, w1_ref, p_ref):
        hv = h_ref[...]
        p_ref[:, 0:DA] = jnp.dot(hv, w0_ref[:, DA:2 * DA], preferred_element_type=f32)
        p_ref[:, DA:2 * DA] = jnp.dot(hv, w1_ref[:, 0:DA], preferred_element_type=f32)

    return _hbm_call(
        body, name="ctx_proj", out_shape=SDS((L, 2 * DA), f32), grid=(1,),
        in_specs=[pl.BlockSpec((L, D), lambda i: (0, 0)), pl.BlockSpec((None, D, D), lambda i: (0, 0, 0)),
                  pl.BlockSpec((None, D, D), lambda i: (1, 0, 0))],
        out_specs=pl.BlockSpec((L, 2 * DA), lambda i: (0, 0)),
    )(hc, w4, w4)


def _head_ones():
    r = lax.broadcasted_iota(i32, (DA, DA), 0) // DH
    c = lax.broadcasted_iota(i32, (DA, DA), 1) // DH
    return (r == c).astype(bf16)


def _head_sum(v, ones_bd):
    hi = v.astype(bf16)
    lo = (v - hi.astype(f32)).astype(bf16)
    return jnp.dot(hi, ones_bd, preferred_element_type=f32) + jnp.dot(lo, ones_bd, preferred_element_type=f32)


def _swap16(v):
    lane = lax.broadcasted_iota(i32, v.shape, 1)
    return jnp.where((lane & 31) < 16, pltpu.roll(v, DA - 16, 1), pltpu.roll(v, 16, 1))


def _rope_block(ct_ref, rt_ref, tm):
    rows = [jnp.tile(rt_ref[8 * j:8 * j + 8, :], (GW // 8, 1)) for j in range(tm // GW)]
    return jnp.tile(ct_ref[...], (tm // GW, 1)) + jnp.concatenate(rows, axis=0)


def _rope_specs(tm):
    col = pl.BlockSpec((GW, DA), lambda i: (0, 0))
    row = pl.BlockSpec((8 * tm // GW, DA), lambda i: (i, 0))
    return [col, row, col, row]


def _qk_prep(p, gq, gk, rope):
    tm = 256

    def body(qk_ref, v_ref, gq_ref, gk_ref, cc_ref, cr_ref, sc_ref, sr_ref, qr_ref, qp_ref, kr_ref, vh_ref):
        ones_bd = _head_ones()
        cs, sn = _rope_block(cc_ref, cr_ref, tm), _rope_block(sc_ref, sr_ref, tm)
        q = qk_ref[:, 0:DA]
        k = qk_ref[:, DA:2 * DA]
        yq = (q * lax.rsqrt(_head_sum(q * q, ones_bd) * (1.0 / DH) + RMS_EPS)) * gq_ref[...]
        yk = (k * lax.rsqrt(_head_sum(k * k, ones_bd) * (1.0 / DH) + RMS_EPS)) * gk_ref[...]
        qr = (yq * cs + _swap16(yq) * sn) * QK_SCALE
        qp = yq * QK_SCALE
        kr = yk * cs + _swap16(yk) * sn
        vv = v_ref[...]
        for hh in range(H):
            sl = slice(hh * DH, (hh + 1) * DH)
            qr_ref[hh] = qr[:, sl].astype(bf16)
            qp_ref[hh] = qp[:, sl].astype(bf16)
            kr_ref[hh] = kr[:, sl].astype(bf16)
            vh_ref[hh] = vv[:, sl].astype(bf16)

    hm = SDS((H, S, DH), bf16)
    hspec = pl.BlockSpec((H, tm, DH), lambda i: (0, i, 0))
    fixed = lambda i: (0, 0)
    return _hbm_call(
        body, name="qk_prep", out_shape=(hm, hm, hm, hm), grid=(S // tm,),
        in_specs=[pl.BlockSpec((tm, 2 * DA), lambda i: (i, 0)), pl.BlockSpec((tm, DA), lambda i: (i, 2)),
                  pl.BlockSpec((1, DA), fixed), pl.BlockSpec((1, DA), fixed)] + _rope_specs(tm),
        out_specs=(hspec, hspec, hspec, hspec),
    )(p, p, gq, gk, *rope)


def _ctx_prep(pc, gk):
    def body(p_ref, gk_ref, kc_ref, vc_ref):
        ones_bd = _head_ones()
        k = p_ref[:, 0:DA]
        yk = (k * lax.rsqrt(_head_sum(k * k, ones_bd) * (1.0 / DH) + RMS_EPS)) * gk_ref[...]
        vv = p_ref[:, DA:2 * DA]
        for hh in range(H):
            sl = slice(hh * DH, (hh + 1) * DH)
            kc_ref[hh] = yk[:, sl].astype(bf16)
            vc_ref[hh] = vv[:, sl].astype(bf16)

    hm = SDS((H, L, DH), bf16)
    return _hbm_call(
        body, name="ctx_prep", out_shape=(hm, hm), in_specs=[VMEM_SPEC, VMEM_SPEC], out_specs=(VMEM_SPEC, VMEM_SPEC),
    )(pc, gk)


def _tile_pieces():
    out = []
    for (i0, u0) in TILE_GEOM:
        rows = []
        for j in range(2):
            i = i0 + j
            rs = _row_start(i)
            rows.append([(u0 + u - i + WIN_H - 1) if rs <= u0 + u < rs + WIN_H else None for u in range(KR)])
        out.append(rows)
    return out


def _bias_prep(rpb_rev_pad, after=None):
    pieces = _tile_pieces()

    def body(r_ref, after_ref, o_ref):
        rp = r_ref[...]
        xs = jnp.concatenate([pltpu.roll(jnp.broadcast_to(rp[dr:dr + 1, :], (GW, 128)), 128 - (WIN_W - 1), 1,
                                         stride=1, stride_axis=0) for dr in range(N_DR)], axis=0)
        row = lax.broadcasted_iota(i32, xs.shape, 0)
        lane = lax.broadcasted_iota(i32, xs.shape, 1)
        k = row & (GW - 1)
        c0 = jnp.clip(lane - WIN_W // 2, 0, GW - WIN_W)
        xs = jnp.where((k >= c0) & (k < c0 + WIN_W), xs, NEG)
        neg = jnp.full((GW, GW), NEG, f32)
        for t in range(NT):
            for j in range(2):
                for u in range(KR):
                    dr = pieces[t][j][u]
                    piece = neg if dr is None else xs[dr * GW:(dr + 1) * GW, 0:GW]
                    o_ref[t, u * GW:(u + 1) * GW, j * GW:(j + 1) * GW] = piece

    return _hbm_call(
        body, name="bias_prep", out_shape=SDS((H, NT, KB, QB), f32), grid=(H,),
        in_specs=[pl.BlockSpec((None, N_DR, 128), lambda h: (h, 0, 0)), ANY_SPEC],
        out_specs=pl.BlockSpec((None, NT, KB, QB), lambda h: (h, 0, 0, 0)),
    )(rpb_rev_pad, rpb_rev_pad if after is None else after)


def _bias_tiles(rpb2, after=None):
    return _bias_prep(jnp.pad(rpb2[:, :, ::-1], ((0, 0), (0, 0), (0, 128 - N_DC))), after)


def _block_geom(b):
    qs = b * QB
    ks = min(max(2 * b - 4, 0), ROWS - KR) * GW
    t = b if b < 2 else (b - (NQB - NT) if b > NQB - 3 else 2)
    return qs, ks, t


def _tt(a, b):
    return lax.dot_general(a, b, (((1,), (1,)), ((), ())), preferred_element_type=f32)


def _tn(a, b):
    return lax.dot_general(a, b, (((0,), (0,)), ((), ())), preferred_element_type=f32)


def _softmax_t(s_lat, s_ctx):
    m = jnp.maximum(jnp.max(s_lat, axis=0, keepdims=True), jnp.max(s_ctx, axis=0, keepdims=True))
    e_lat = jnp.exp(s_lat - m)
    e_ctx = jnp.exp(s_ctx - m)
    inv = 1.0 / (jnp.sum(e_lat, axis=0, keepdims=True) + jnp.sum(e_ctx, axis=0, keepdims=True))
    return e_lat * inv, e_ctx * inv


def _staged(n_blocks, stages):
    held = [dict() for _ in stages]
    for step in range(n_blocks + len(stages) - 1):
        for s, fn in enumerate(stages):
            b = step - s
            if 0 <= b < n_blocks:
                held[s][b] = fn(b) if s == 0 else fn(b, held[s - 1].pop(b))


def _attn_fwd(qr, qp, kr, vh, kc, vc, btt):
    def body(qr_ref, qp_ref, kr_ref, v_ref, kc_ref, vc_ref, bt_ref, o_ref):
        kcv, vcv = kc_ref[...], vc_ref[...]

        def scores(b):
            qs, ks, t = _block_geom(b)
            return (_tt(kr_ref[ks:ks + KB, :], qr_ref[qs:qs + QB, :]) + bt_ref[t], _tt(kcv, qp_ref[qs:qs + QB, :]))

        def probs(b, sc):
            p_lat, p_ctx = _softmax_t(*sc)
            return p_lat.astype(bf16), p_ctx.astype(bf16)

        def values(b, p):
            qs, ks, _ = _block_geom(b)
            o_ref[qs:qs + QB, :] = _tn(p[0], v_ref[ks:ks + KB, :]) + _tn(p[1], vcv)

        _staged(NQB, (scores, probs, values))

    sq = pl.BlockSpec((None, S, DH), lambda h: (h, 0, 0))
    sc = pl.BlockSpec((None, L, DH), lambda h: (h, 0, 0))
    return _hbm_call(
        body, name="attn_fwd", out_shape=SDS((H, S, DH), f32), grid=(H,),
        in_specs=[sq, sq, sq, sq, sc, sc, pl.BlockSpec((None, NT, KB, QB), lambda h: (h, 0, 0, 0))],
        out_specs=sq, compiler_params=_cp(48),
    )(qr, qp, kr, vh, kc, vc, btt)


def _shift_rows(v, down):
    n = v.shape[0]
    row = lax.broadcasted_iota(i32, v.shape, 0)
    if down:
        return jnp.where(row == 0, 0.0, pltpu.roll(v, 1, 0))
    return jnp.where(row == n - 1, 0.0, pltpu.roll(v, n - 1, 0))


def _conv_specs():
    col = lambda off: pl.BlockSpec((S, 128), lambda i, off=off: (0, off + i))
    return [col(16), col(20), col(24), col(28), pl.BlockSpec((3, 128), lambda i: (0, i)),
            pl.BlockSpec((1, 128), lambda i: (0, i))]


def _conv_fwd(p, conv_w, conv_b, after=None):
    def body(u_ref, bg_ref, cg_ref, zc_ref, w_ref, b_ref, after_ref, o_ref):
        cu = cg_ref[...] * u_ref[...]
        cv = b_ref[...] + _shift_rows(cu, True) * w_ref[0:1, :]
        cv = cv + cu * w_ref[1:2, :]
        cv = cv + _shift_rows(cu, False) * w_ref[2:3, :]
        o_ref[...] = ((bg_ref[...] * cv) * _silu(zc_ref[...])).astype(bf16)

    return _hbm_call(
        body, name="conv_fwd", out_shape=SDS((S, DC), bf16), grid=(DC // 128,),
        in_specs=_conv_specs() + [ANY_SPEC], out_specs=pl.BlockSpec((S, 128), lambda i: (0, i)),
        compiler_params=_cp(40),
    )(p, p, p, p, conv_w, conv_b, conv_b if after is None else after)


DP_Q, DP_K, DP_V, DP_ZA, DP_U, DP_BG, DP_CG, DP_ZC = range(8)


def _out_proj_loss(o, p, conv_g, w_out, xx, tgt, mrow, b_ada):
    tm = 256

    def body(o_ref, za_ref, c_ref, w_ref, x_ref, t_ref, m_ref, b_ref,
             dy_ref, dconv_ref, dp_ref, do_ref, gwo_ref, dgate_ref, loss_ref):
        k = pl.program_id(0)

        @pl.when(k == 0)
        def _():
            gwo_ref[...] = jnp.zeros_like(gwo_ref)
            dgate_ref[...] = jnp.zeros_like(dgate_ref)
            loss_ref[0, 0] = 0.0

        gate = m_ref[:, 2 * D:3 * D] + b_ref[:, 2 * D:3 * D]
        za = za_ref[...]
        sz = _silu(za)
        om = _merge_heads(o_ref)
        av, cv = (om * sz).astype(bf16), c_ref[...]
        mo = jnp.dot(av, w_ref[0:DA, :], preferred_element_type=f32)
        mo = mo + jnp.dot(cv, w_ref[DA:DA + DC, :], preferred_element_type=f32)
        y = x_ref[...] + gate * mo
        diff = y - t_ref[...]
        loss_ref[0, 0] += jnp.sum(diff * diff)
        dy = diff * (1.0 / D)
        dy_ref[...] = dy
        dgate_ref[...] += jnp.sum(dy * mo, axis=0, keepdims=True)
        dmo = (dy * gate).astype(bf16)
        dmix = _tt(dmo, w_ref[...])
        dattn = dmix[:, 0:DA]
        dconv_ref[...] = dmix[:, DA:DA + DC]
        a = dattn * sz
        for hh in range(H):
            do_ref[hh] = a[:, hh * DH:(hh + 1) * DH].astype(bf16)
        dp_ref[...] = ((dattn * _dsilu(za)) * om).astype(bf16)
        gwo_ref[0:DA, :] += _tn(av, dmo)
        gwo_ref[DA:DA + DC, :] += _tn(cv, dmo)

    row = lambda i: (i, 0)
    fixed = lambda i: (0, 0)
    hspec = pl.BlockSpec((H, tm, DH), lambda i: (0, i, 0))
    return _hbm_call(
        body, name="out_proj_loss",
        out_shape=(SDS((S, D), f32), SDS((S, DC), f32), SDS((8, S, DA), bf16), SDS((H, S, DH), bf16),
                   SDS((D, D), f32), SDS((1, D), f32), SDS((1, 1), f32)),
        grid=(S // tm,),
        in_specs=[hspec, pl.BlockSpec((tm, DA), lambda i: (i, 3)), pl.BlockSpec((tm, DC), row),
                  pl.BlockSpec((D, D), fixed), pl.BlockSpec((tm, D), row), pl.BlockSpec((tm, D), row),
                  pl.BlockSpec((1, 3 * D), fixed), pl.BlockSpec((1, 3 * D), fixed)],
        out_specs=(pl.BlockSpec((tm, D), row), pl.BlockSpec((tm, DC), row),
                   pl.BlockSpec((None, tm, DA), lambda i: (DP_ZA, i, 0)), hspec, pl.BlockSpec((D, D), fixed),
                   pl.BlockSpec((1, D), fixed), SMEM_SPEC),
        compiler_params=_cp(56, dimension_semantics=("arbitrary",)),
    )(o, p, conv_g, w_out, xx, tgt, mrow, b_ada)


def _conv_bwd(dconv, p, conv_w, conv_b, dp8, after=None):
    def body(d_ref, u_ref, bg_ref, cg_ref, zc_ref, w_ref, b_ref, dp_in_ref, after_ref, dp_ref, gw_ref, gb_ref):
        du_ref, dbg_ref, dcg_ref, dzc_ref = dp_ref.at[0], dp_ref.at[1], dp_ref.at[2], dp_ref.at[3]
        dconv = d_ref[...]
        u, bg, cg, zc = u_ref[...], bg_ref[...], cg_ref[...], zc_ref[...]
        w0, w1, w2 = w_ref[0:1, :], w_ref[1:2, :], w_ref[2:3, :]
        cu = cg * u
        cu_m, cu_p = _shift_rows(cu, True), _shift_rows(cu, False)
        cv = b_ref[...] + cu_m * w0
        cv = cv + cu * w1
        cv = cv + cu_p * w2
        sz = _silu(zc)
        dbg_ref[...] = ((dconv * sz) * cv).astype(bf16)
        dzc_ref[...] = ((dconv * (bg * cv)) * _dsilu(zc)).astype(bf16)
        dcv = (dconv * sz) * bg
        gb_ref[...] = jnp.sum(dcv, axis=0, keepdims=True)
        gw_ref[0:1, :] = jnp.sum(dcv * cu_m, axis=0, keepdims=True)
        gw_ref[1:2, :] = jnp.sum(dcv * cu, axis=0, keepdims=True)
        gw_ref[2:3, :] = jnp.sum(dcv * cu_p, axis=0, keepdims=True)
        gw_ref[3:8, :] = jnp.zeros((5, 128), f32)
        dcu = _shift_rows(dcv, False) * w0 + dcv * w1 + _shift_rows(dcv, True) * w2
        dcg_ref[...] = (dcu * u).astype(bf16)
        du_ref[...] = (dcu * cg).astype(bf16)

    return _hbm_call(
        body, name="conv_bwd", out_shape=(SDS((8, S, DC), bf16), SDS((8, DC), f32), SDS((1, DC), f32)),
        grid=(DC // 128,),
        in_specs=[pl.BlockSpec((S, 128), lambda i: (0, i))] + _conv_specs() + [ANY_SPEC, ANY_SPEC],
        out_specs=(pl.BlockSpec((4, S, 128), lambda i: (DP_U // 4, 0, i)), pl.BlockSpec((8, 128), lambda i: (0, i)),
                   pl.BlockSpec((1, 128), lambda i: (0, i))),
        input_output_aliases={7: 0}, compiler_params=_cp(48),
    )(dconv, p, p, p, p, conv_w, conv_b, dp8, conv_b if after is None else after)


def _attn_bwd(qr, qp, kr, vh, kc, vc, btt, do, after=None):
    def body(qr_ref, qp_ref, kr_ref, v_ref, kc_ref, vc_ref, bt_ref, do_ref, after_ref,
             dqr_ref, dqp_ref, dkr_ref, dv_ref, dkc_ref, dvc_ref, dbt_ref):
        kcv, vcv = kc_ref[...], vc_ref[...]
        dkr_ref[...] = jnp.zeros_like(dkr_ref)
        dv_ref[...] = jnp.zeros_like(dv_ref)
        dbt_ref[...] = jnp.zeros_like(dbt_ref)
        ctx_acc = {}

        def products(b):
            qs, ks, t = _block_geom(b)
            dob = do_ref[qs:qs + QB, :]
            s_lat = _tt(kr_ref[ks:ks + KB, :], qr_ref[qs:qs + QB, :]) + bt_ref[t]
            s_ctx = _tt(kcv, qp_ref[qs:qs + QB, :])
            return s_lat, s_ctx, _tt(v_ref[ks:ks + KB, :], dob), _tt(vcv, dob)

        def score_grads(b, x):
            s_lat, s_ctx, dp_lat, dp_ctx = x
            p_lat, p_ctx = _softmax_t(s_lat, s_ctx)
            delta = jnp.sum(p_lat * dp_lat, axis=0, keepdims=True) + jnp.sum(p_ctx * dp_ctx, axis=0, keepdims=True)
            ds_lat = p_lat * (dp_lat - delta)
            ds_ctx = p_ctx * (dp_ctx - delta)
            return ds_lat, ds_lat.astype(bf16), ds_ctx.astype(bf16), p_lat.astype(bf16), p_ctx.astype(bf16)

        def operand_grads(b, y):
            qs, ks, t = _block_geom(b)
            ds_lat, dsb_lat, dsb_ctx, pb_lat, pb_ctx = y
            qrb, qpb, dob = qr_ref[qs:qs + QB, :], qp_ref[qs:qs + QB, :], do_ref[qs:qs + QB, :]
            dbt_ref[t] += ds_lat
            dqr_ref[qs:qs + QB, :] = _tn(dsb_lat, kr_ref[ks:ks + KB, :])
            dqp_ref[qs:qs + QB, :] = _tn(dsb_ctx, kcv)
            dkr_ref[ks:ks + KB, :] += jnp.dot(dsb_lat, qrb, preferred_element_type=f32)
            dv_ref[ks:ks + KB, :] += jnp.dot(pb_lat, dob, preferred_element_type=f32)
            dkc = jnp.dot(dsb_ctx, qpb, preferred_element_type=f32)
            dvc = jnp.dot(pb_ctx, dob, preferred_element_type=f32)
            ctx_acc["k"] = dkc if b == 0 else ctx_acc["k"] + dkc
            ctx_acc["v"] = dvc if b == 0 else ctx_acc["v"] + dvc

        _staged(NQB, (products, score_grads, operand_grads))
        dkc_ref[...] = ctx_acc["k"]
        dvc_ref[...] = ctx_acc["v"]

    sq = pl.BlockSpec((None, S, DH), lambda h: (h, 0, 0))
    sc = pl.BlockSpec((None, L, DH), lambda h: (h, 0, 0))
    sb = pl.BlockSpec((None, NT, KB, QB), lambda h: (h, 0, 0, 0))
    big, ctxs = SDS((H, S, DH), f32), SDS((H, L, DH), f32)
    return _hbm_call(
        body, name="attn_bwd", out_shape=(big, big, big, big, ctxs, ctxs, SDS((H, NT, KB, QB), f32)), grid=(H,),
        in_specs=[sq, sq, sq, sq, sc, sc, sb, sq, ANY_SPEC], out_specs=(sq, sq, sq, sq, sc, sc, sb),
        compiler_params=_cp(56),
    )(qr, qp, kr, vh, kc, vc, btt, do, do if after is None else after)


def _bias_bwd(dbtt, after=None):
    pieces = _tile_pieces()

    def body(d_ref, after_ref, o_ref, scr):
        scr[...] = jnp.zeros_like(scr)
        acc = [None] * N_DR
        for t in range(NT):
            for j in range(2):
                for u in range(KR):
                    dr = pieces[t][j][u]
                    if dr is None:
                        continue
                    piece = d_ref[t, u * GW:(u + 1) * GW, j * GW:(j + 1) * GW]
                    acc[dr] = piece if acc[dr] is None else acc[dr] + piece
        a = lax.broadcasted_iota(i32, (GW, GW), 0)
        b = lax.broadcasted_iota(i32, (GW, GW), 1)
        flip = (a + b == GW - 1).astype(f32)
        for dr in range(N_DR):
            scr[dr * GW:(dr + 1) * GW, 0:GW] = jnp.dot(acc[dr], flip, precision=HIGHEST, preferred_element_type=f32)
        xs = jnp.concatenate([pltpu.roll(scr[dr * GW:(dr + 1) * GW, :], 128 + (WIN_W - 1) - (GW - 1), 1,
                                         stride=1, stride_axis=0) for dr in range(N_DR)], axis=0)
        tot = jnp.sum(xs.reshape(N_DR, GW, 128), axis=1)
        lane = lax.broadcasted_iota(i32, tot.shape, 1)
        o_ref[...] = jnp.where(lane < N_DC, tot, 0.0)

    return _hbm_call(
        body, name="bias_bwd", out_shape=SDS((H, N_DR, 128), f32), grid=(H,),
        in_specs=[pl.BlockSpec((None, NT, KB, QB), lambda h: (h, 0, 0, 0)), ANY_SPEC],
        out_specs=pl.BlockSpec((None, N_DR, 128), lambda h: (h, 0, 0)),
        scratch_shapes=[pltpu.VMEM((N_DR * GW, 128), f32)],
    )(dbtt, dbtt if after is None else after)


def _merge_heads(ref):
    return jnp.concatenate([ref[hh] for hh in range(H)], axis=1)


def _head_norm_bwd(xraw, gain, dy, ones_bd):
    r = lax.rsqrt(_head_sum(xraw * xraw, ones_bd) * (1.0 / DH) + RMS_EPS)
    xh = xraw * r
    gdy = dy * gain
    dx = r * (gdy - xh * (_head_sum(xh * gdy, ones_bd) * (1.0 / DH)))
    return dx, jnp.sum(dy * xh, axis=0, keepdims=True)


def _qk_bwd(dqr, dqp, dkr, dvh, p, gq, gk, rope, dp8):
    tm = 256

    def body(dqr_ref, dqp_ref, dkr_ref, dv_ref, qk_ref, gq_ref, gk_ref, cc_ref, cr_ref, sc_ref, sr_ref, dp_in_ref,
             dp_ref, ggq_ref, ggk_ref):
        dq_ref, dk_ref, dvo_ref = dp_ref.at[DP_Q], dp_ref.at[DP_K], dp_ref.at[DP_V]

        @pl.when(pl.program_id(0) == 0)
        def _():
            ggq_ref[...] = jnp.zeros_like(ggq_ref)
            ggk_ref[...] = jnp.zeros_like(ggk_ref)

        ones_bd = _head_ones()
        cs, sn = _rope_block(cc_ref, cr_ref, tm), _rope_block(sc_ref, sr_ref, tm)
        a = _merge_heads(dqr_ref)
        dyq = ((a * cs - _swap16(a) * sn) + _merge_heads(dqp_ref)) * QK_SCALE
        bk = _merge_heads(dkr_ref)
        dyk = bk * cs - _swap16(bk) * sn
        dq, gq_part = _head_norm_bwd(qk_ref[:, 0:DA], gq_ref[...], dyq, ones_bd)
        dk, gk_part = _head_norm_bwd(qk_ref[:, DA:2 * DA], gk_ref[...], dyk, ones_bd)
        dq_ref[...] = dq.astype(bf16)
        dk_ref[...] = dk.astype(bf16)
        dvo_ref[...] = _merge_heads(dv_ref).astype(bf16)
        ggq_ref[...] += gq_part
        ggk_ref[...] += gk_part

    hspec = pl.BlockSpec((H, tm, DH), lambda i: (0, i, 0))
    fixed = pl.BlockSpec((1, DA), lambda i: (0, 0))
    return _hbm_call(
        body, name="qk_bwd", out_shape=(SDS((8, S, DA), bf16), SDS((1, DA), f32), SDS((1, DA), f32)), grid=(S // tm,),
        in_specs=[hspec, hspec, hspec, hspec, pl.BlockSpec((tm, 2 * DA), lambda i: (i, 0)), fixed, fixed]
        + _rope_specs(tm) + [ANY_SPEC],
        out_specs=(pl.BlockSpec((3, tm, DA), lambda i: (0, i, 0)), fixed, fixed), input_output_aliases={11: 0},
        compiler_params=_cp(40, dimension_semantics=("arbitrary",)),
    )(dqr, dqp, dkr, dvh, p, gq, gk, *rope, dp8)


def _ctx_bwd(dkc, dvc, pc, gk):
    def body(dkc_ref, dvc_ref, p_ref, gk_ref, dk_ref, dv_ref, ggk_ref):
        ones_bd = _head_ones()
        dk, gk_part = _head_norm_bwd(p_ref[:, 0:DA], gk_ref[...], _merge_heads(dkc_ref), ones_bd)
        dk_ref[...] = dk.astype(bf16)
        dv_ref[...] = _merge_heads(dvc_ref).astype(bf16)
        ggk_ref[...] = gk_part

    piece = SDS((L, DA), bf16)
    return _hbm_call(
        body, name="ctx_bwd", out_shape=(piece, piece, SDS((1, DA), f32)), in_specs=[VMEM_SPEC] * 4,
        out_specs=(VMEM_SPEC,) * 3,
    )(dkc, dvc, pc, gk)


def _grad_w_in(h, dp8, hc, dkc_raw, dvc_m, half, name, after=None):
    def body(half_ref, h_ref, p_ref, hc_ref, dk_ref, dv_ref, after_ref, g_ref):
        j = pl.program_id(0)
        hv = h_ref[...]
        g_ref[:, 0:DA] = _tn(hv, p_ref[0])
        g_ref[:, DA:2 * DA] = _tn(hv, p_ref[1])

        @pl.when(j == 0)
        def _():
            g_ref[:, DA:2 * DA] += _tn(hc_ref[...], dk_ref[...])

        @pl.when(j == 1)
        def _():
            g_ref[:, 0:DA] += _tn(hc_ref[...], dv_ref[...])

    fixed = lambda j, s: (0, 0)
    hd = D // 2
    grid_spec = pltpu.PrefetchScalarGridSpec(
        num_scalar_prefetch=1, grid=(4,),
        in_specs=[pl.BlockSpec((S, hd), lambda j, s: (0, s[0])), pl.BlockSpec((2, S, DA), lambda j, s: (j, 0, 0)),
                  pl.BlockSpec((L, hd), lambda j, s: (0, s[0])), pl.BlockSpec((L, DA), fixed),
                  pl.BlockSpec((L, DA), fixed), ANY_SPEC],
        out_specs=pl.BlockSpec((None, hd, D), lambda j, s: (j, 0, 0)))
    return _hbm_call(
        body, name=name, out_shape=SDS((4, hd, D), f32), grid_spec=grid_spec, compiler_params=_cp(40),
    )(half, h, dp8, hc, dkc_raw, dvc_m, half if after is None else after)


def _norm_mod_bwd(x, dh, g, scale):
    r = lax.rsqrt(jnp.mean(x * x, axis=-1, keepdims=True) + RMS_EPS)
    xh = x * r
    y = xh * g
    dshift = jnp.sum(dh, axis=0, keepdims=True)
    dscale = jnp.sum(dh * y, axis=0, keepdims=True)
    dyn = dh * (1.0 + scale)
    dg = jnp.sum(dyn * xh, axis=0, keepdims=True)
    gdy = dyn * g
    dx = r * (gdy - xh * jnp.mean(xh * gdy, axis=-1, keepdims=True))
    return dx, dshift, dscale, dg


def _dh_grad_x(dp8, w4, xx, dy, norm_g, mrow, b_ada, after=None):
    tm = 256

    def body(p_ref, w_ref, x_ref, dy_ref, g_ref, m_ref, b_ref, after_ref, gx_ref, dsh_ref, dsc_ref, dg_ref):
        @pl.when(pl.program_id(0) == 0)
        def _():
            dsh_ref[...] = jnp.zeros_like(dsh_ref)
            dsc_ref[...] = jnp.zeros_like(dsc_ref)
            dg_ref[...] = jnp.zeros_like(dg_ref)

        dh = None
        for j in range(4):
            for half in range(2):
                term = _tt(p_ref[2 * j + half], w_ref[j, :, half * DA:(half + 1) * DA])
                dh = term if dh is None else dh + term
        scale = m_ref[:, D:2 * D] + b_ref[:, D:2 * D]
        dx, dshift, dscale, dg = _norm_mod_bwd(x_ref[...], dh, g_ref[...], scale)
        gx_ref[...] = dy_ref[...] + dx
        dsh_ref[...] += dshift
        dsc_ref[...] += dscale
        dg_ref[...] += dg

    row = lambda i: (i, 0)
    fixed = lambda i: (0, 0)
    vec = SDS((1, D), f32)
    return _hbm_call(
        body, name="dh_grad_x", out_shape=(SDS((S, D), f32), vec, vec, vec), grid=(S // tm,),
        in_specs=[pl.BlockSpec((8, tm, DA), lambda i: (0, i, 0)), pl.BlockSpec((4, D, D), lambda i: (0, 0, 0)),
                  pl.BlockSpec((tm, D), row), pl.BlockSpec((tm, D), row), pl.BlockSpec((1, D), fixed),
                  pl.BlockSpec((1, 3 * D), fixed), pl.BlockSpec((1, 3 * D), fixed), ANY_SPEC],
        out_specs=(pl.BlockSpec((tm, D), row), pl.BlockSpec((1, D), fixed), pl.BlockSpec((1, D), fixed),
                   pl.BlockSpec((1, D), fixed)),
        compiler_params=_cp(56, dimension_semantics=("arbitrary",)),
    )(dp8, w4, xx, dy, norm_g, mrow, b_ada, b_ada if after is None else after)


def _dhc_sums(dkc_raw, dvc_m, w4, ctx2, norm_g, mrow_c, b_ada, after=None):
    def body(dk_ref, dv_ref, w0_ref, w1_ref, x_ref, g_ref, m_ref, b_ref, after_ref, dsh_ref, dsc_ref, dg_ref):
        dh = _tt(dk_ref[...], w0_ref[:, DA:2 * DA]) + _tt(dv_ref[...], w1_ref[:, 0:DA])
        scale = m_ref[:, D:2 * D] + b_ref[:, D:2 * D]
        _, dshift, dscale, dg = _norm_mod_bwd(x_ref[...], dh, g_ref[...], scale)
        dsh_ref[...] = dshift
        dsc_ref[...] = dscale
        dg_ref[...] = dg

    fixed = lambda i: (0, 0)
    vec = SDS((1, D), f32)
    vspec = pl.BlockSpec((1, D), fixed)
    return _hbm_call(
        body, name="dhc_sums", out_shape=(vec, vec, vec), grid=(1,),
        in_specs=[pl.BlockSpec((L, DA), fixed), pl.BlockSpec((L, DA), fixed),
                  pl.BlockSpec((None, D, D), lambda i: (0, 0, 0)), pl.BlockSpec((None, D, D), lambda i: (1, 0, 0)),
                  pl.BlockSpec((L, D), fixed), vspec, pl.BlockSpec((1, 3 * D), fixed), pl.BlockSpec((1, 3 * D), fixed),
                  ANY_SPEC],
        out_specs=(vspec, vspec, vspec), compiler_params=_cp(32),
    )(dkc_raw, dvc_m, w4, w4, ctx2, norm_g, mrow_c, b_ada, b_ada if after is None else after)


def _rope_tables():
    nf = DH // 4
    inv = np.float32(ROPE_THETA) ** (-np.arange(nf, dtype=np.float32) / np.float32(nf))
    ang_c = np.arange(GW, dtype=np.float32)[:, None] * inv
    ang_r = np.arange(ROWS, dtype=np.float32)[:, None] * inv
    zc, zr = np.zeros((GW, 2 * nf), np.float32), np.zeros((ROWS, 2 * nf), np.float32)
    ct_cos = np.tile(np.concatenate([zc, np.cos(ang_c), np.cos(ang_c)], axis=1), (1, H))
    ct_sin = np.tile(np.concatenate([zc, -np.sin(ang_c), np.sin(ang_c)], axis=1), (1, H))
    rt_cos = np.tile(np.concatenate([np.cos(ang_r), np.cos(ang_r), zr], axis=1), (1, H))
    rt_sin = np.tile(np.concatenate([-np.sin(ang_r), np.sin(ang_r), zr], axis=1), (1, H))
    rep8 = lambda t: np.ascontiguousarray(np.broadcast_to(t[:, None, :], (ROWS, 8, DA))).reshape(ROWS * 8, DA)
    return tuple(jnp.asarray(t, f32) for t in (ct_cos, rep8(rt_cos), ct_sin, rep8(rt_sin)))


def _local_step(xx, ctx2, tgt, mrow, mrow_c, b_ada, norm_g, weights, q_norm_g, k_norm_g, rpb2, conv_w_full, conv_b,
                hooks=None):
    hooks = hooks or {}
    gq = jnp.tile(q_norm_g, (1, H))
    gk = jnp.tile(k_norm_g, (1, H))
    rope = _rope_tables()

    h = _prenorm(xx, norm_g, mrow, b_ada, 256, "prenorm_x", after=weights.get("started"))
    hc = _prenorm(ctx2, norm_g, mrow_c, b_ada, L, "prenorm_ctx")
    jv = weights["jvec"]
    p = _in_proj_own(h, weights["own"], jv)
    btb = _bias_tiles(rpb2, after=p)
    w4, started = weights["near"](btb)
    p = _in_proj_block(h, w4, p, weights["first"], "in_proj_near", after=started)
    w4 = weights["near2"](w4, p)
    p = _in_proj_block(h, w4, p, weights["second"], "in_proj_near2")
    w4, started = weights["far"](w4, p)
    p = _in_proj_block(h, w4, p, jv ^ 3, "in_proj_far", after=started)
    pc = _ctx_proj(hc, w4)
    qr, qp, kr, vh = _qk_prep(p, gq, gk, rope)
    kc, vc = _ctx_prep(pc, gk)
    o = _attn_fwd(qr, qp, kr, vh, kc, vc, btb)
    started = weights["out_arrived"](o) if "out_arrived" in weights else None
    conv_g = _conv_fwd(p, conv_w_full, conv_b, after=started)
    w_out_full = weights["out"](conv_g)
    dy, dconv, dp8, do, g_w_out, dgate, loss_sum = _out_proj_loss(o, p, conv_g, w_out_full, xx, tgt, mrow, b_ada)
    started = hooks["g_w_out"](g_w_out) if "g_w_out" in hooks else None
    dp8, g_conv_w, g_conv_b = _conv_bwd(dconv, p, conv_w_full, conv_b, dp8, after=started)
    started = hooks["after_conv"](dp8) if "after_conv" in hooks else None
    dqr, dqp, dkr, dvh, dkc, dvc, dbtb = _attn_bwd(qr, qp, kr, vh, kc, vc, btb, do, after=started)
    dp8, g_gq, g_gk = _qk_bwd(dqr, dqp, dkr, dvh, p, gq, gk, rope, dp8)
    dkc_raw, dvc_m, g_gk_c = _ctx_bwd(dkc, dvc, pc, gk)
    first = hooks.get("first_half", jnp.zeros((1,), i32))
    g_first = _grad_w_in(h, dp8, hc, dkc_raw, dvc_m, first, "grad_w_in_first")
    started = hooks["g_w_in_first"](g_first) if "g_w_in_first" in hooks else None
    g_second = _grad_w_in(h, dp8, hc, dkc_raw, dvc_m, 1 - first, "grad_w_in_second", after=started)
    started = hooks["g_w_in_second"](g_second) if "g_w_in_second" in hooks else None
    dshift_c, dscale_c, dng_c = _dhc_sums(dkc_raw, dvc_m, w4, ctx2, norm_g, mrow_c, b_ada, after=started)
    g_rpb = _bias_bwd(dbtb, after=dshift_c)
    grad_x, dshift, dscale, dng = _dh_grad_x(dp8, w4, xx, dy, norm_g, mrow, b_ada, after=g_rpb)
    return dict(loss_sum=loss_sum, grad_x=grad_x, g_w_in=(g_first, g_second), g_w_out=g_w_out, g_conv_w=g_conv_w,
                g_conv_b=g_conv_b, g_rpb=g_rpb, g_gq=g_gq, g_gk=g_gk, g_gk_c=g_gk_c, dshift=dshift, dscale=dscale,
                dgate=dgate, dng=dng, dshift_c=dshift_c, dscale_c=dscale_c, dng_c=dng_c)


def _pair_sum_w_in(g, r):
    tr = 128

    def body(g_ref, r_ref, t32_ref, tb_ref):
        t = g_ref[...] + r_ref[...]
        t32_ref[...] = t
        tb_ref[...] = t.astype(bf16)

    half = D // 2
    spec = pl.BlockSpec((4, tr, D), lambda i: (0, i, 0))
    return _hbm_call(body, name="pair_sum_w_in", out_shape=(SDS((4, half, D), f32), SDS((4, half, D), bf16)),
                     grid=(half // tr,), in_specs=[spec, spec], out_specs=(spec, spec), compiler_params=_cp(40))(g, r)


def _pair_sum_w_out(g, r, cvec):
    hr = D // 8

    def body(c_ref, g0, g1, g2, g3, r_ref, t32_ref, tb_ref):
        for q, g_ref in enumerate((g0, g1, g2, g3)):
            t = g_ref[...] + r_ref[q]
            t32_ref[q] = t
            tb_ref[q] = t.astype(bf16)

    gspecs = [pl.BlockSpec((hr, D), lambda i, c, q=q: (2 * q + c[0], 0)) for q in range(4)]
    full = pl.BlockSpec((4, hr, D), lambda i, c: (0, 0, 0))
    grid_spec = pltpu.PrefetchScalarGridSpec(num_scalar_prefetch=1, grid=(1,), in_specs=gspecs + [full],
                                             out_specs=(full, full))
    return _hbm_call(body, name="pair_sum_w_out", out_shape=(SDS((4, hr, D), f32), SDS((4, hr, D), bf16)),
                          grid_spec=grid_spec)(cvec, g, g, g, g, r)


def _chip_sum(t32, r2, jvec, name):
    rows = t32.shape[1]
    tr = min(rows, 128)

    def body(j_ref, t_ref, r_ref, u_ref):
        u_ref[...] = ((t_ref[...] + r_ref[0].astype(f32)) + r_ref[1].astype(f32)) + r_ref[2].astype(f32)

    grid_spec = pltpu.PrefetchScalarGridSpec(
        num_scalar_prefetch=1, grid=(rows // tr,),
        in_specs=[pl.BlockSpec((None, tr, D), lambda i, j: (j[0], i, 0)),
                  pl.BlockSpec((3, tr, D), lambda i, j: (0, i, 0))],
        out_specs=pl.BlockSpec((tr, D), lambda i, j: (i, 0)))
    return _hbm_call(body, name=name, out_shape=SDS((rows, D), f32), grid_spec=grid_spec)(jvec, t32, r2)


_PK = {}
_off = 0
for _name, _rows in (("dm", 24), ("dmc", 24), ("dng", 8), ("dng_c", 8), ("gq", 8), ("gk", 8), ("gk_c", 8),
                     ("rpb", H * N_DR), ("conv_b", 8), ("conv_w", 16), ("loss", 8)):
    _PK[_name] = (_off, _off + _rows)
    _off += _rows
PK_ROWS = _off
RS_B_ADA, RS_NORM_G, RS_GQ, RS_GK, RS_RPB, RS_CONV_B, RS_CONV_W, RS_DMC, RS_LOSS, RS_ROWS = (
    0, 24, 32, 40, 48, 168, 176, 192, 216, 224)


def _small_reduce(gathered):
    def body(g_ref, o_ref, dm_ref):
        a0 = _PK["dm"][0]
        dm_ref[...] = jnp.zeros_like(dm_ref)
        for b in range(8):
            for i in range(24):
                dm_ref[b:b + 1, 128 * i:128 * (i + 1)] = g_ref[b, a0 + i:a0 + i + 1, :]
        tot = g_ref[0]
        for b in range(1, 8):
            tot = tot + g_ref[b]

        def rows(name):
            a, z = _PK[name]
            return tot[a:z]

        o_ref[RS_B_ADA:RS_B_ADA + 24] = rows("dm") + rows("dmc")
        o_ref[RS_NORM_G:RS_NORM_G + 8] = rows("dng") + rows("dng_c")
        gq = jnp.broadcast_to(jnp.sum(rows("gq"), axis=0, keepdims=True), (8, 128))
        gk = jnp.broadcast_to(jnp.sum(rows("gk") + rows("gk_c"), axis=0, keepdims=True), (8, 128))
        o_ref[RS_GQ:RS_GQ + 8] = gq + pltpu.roll(gq, DH, 1)
        o_ref[RS_GK:RS_GK + 8] = gk + pltpu.roll(gk, DH, 1)
        o_ref[RS_RPB:RS_RPB + H * N_DR] = rows("rpb")
        o_ref[RS_CONV_B:RS_CONV_B + 8] = rows("conv_b")
        o_ref[RS_CONV_W:RS_CONV_W + 16] = rows("conv_w")
        dmc = rows("dmc")
        o_ref[RS_DMC:RS_DMC + 24] = dmc
        o_ref[RS_LOSS:RS_LOSS + 8] = rows("loss")
        for i in range(24):
            dm_ref[8:9, 128 * i:128 * (i + 1)] = dmc[i:i + 1]

    return _hbm_call(body, name="small_reduce", out_shape=(SDS((RS_ROWS, 128), f32), SDS((16, 3 * D), f32)),
                     in_specs=[VMEM_SPEC], out_specs=(VMEM_SPEC, VMEM_SPEC))(gathered)


def _w_ada_grad(sc16, dm16, w_ada_shard, jvec):
    ncol = w_ada_shard.shape[1]

    def body(j_ref, sc_ref, dm_ref, w_ref, g_ref, part_ref):
        dm = dm_ref[...]
        g_ref[...] = lax.dot_general(sc_ref[...], dm, (((0,), (0,)), ((), ())), precision=HIGHEST,
                                     preferred_element_type=f32)
        part_ref[...] = lax.dot_general(dm[8:16], w_ref[...], (((1,), (1,)), ((), ())), precision=HIGHEST,
                                        preferred_element_type=f32)

    fixed = lambda i, j: (0, 0)
    grid_spec = pltpu.PrefetchScalarGridSpec(
        num_scalar_prefetch=1, grid=(1,),
        in_specs=[pl.BlockSpec((16, D), fixed), pl.BlockSpec((16, ncol), lambda i, j: (0, j[0])),
                  pl.BlockSpec((D, ncol), fixed)],
        out_specs=(pl.BlockSpec((D, ncol), fixed), pl.BlockSpec((8, D), fixed)))
    return _pallas_call(body, name="w_ada_grad", out_shape=(SDS((D, ncol), f32), SDS((8, D), f32)),
                        grid_spec=grid_spec, compiler_params=_cp(40))(jvec, sc16, dm16, w_ada_shard)


def _c_ctx_grad(parts4, c_ctx):
    def body(p_ref, c_ref, o_ref):
        tot = ((p_ref[0] + p_ref[1]) + p_ref[2]) + p_ref[3]
        o_ref[...] = tot[0:1] * _dsilu(c_ref[...].reshape(1, D))

    return _pallas_call(body, name="c_ctx_grad", out_shape=SDS((1, D), f32), in_specs=[VMEM_SPEC, VMEM_SPEC],
                        out_specs=VMEM_SPEC)(parts4, c_ctx)


def _adamw(w, g, m, v, name, after=None):
    rows, cols = w.shape
    tr = 256 if rows % 256 == 0 else rows

    def body(w_ref, g_ref, m_ref, v_ref, after_ref, d_ref, m2_ref, v2_ref):
        gv = g_ref[...]
        m2 = ADAM_B1 * m_ref[...] + (1.0 - ADAM_B1) * gv
        v2 = ADAM_B2 * v_ref[...] + (1.0 - ADAM_B2) * jnp.square(gv)
        m_hat = m2 / (1.0 - ADAM_B1 ** ADAM_STEP)
        v_hat = v2 / (1.0 - ADAM_B2 ** ADAM_STEP)
        d_ref[...] = -ADAM_LR * (m_hat / (jnp.sqrt(v_hat) + ADAM_EPS) + ADAM_WD * w_ref[...])
        m2_ref[...] = m2
        v2_ref[...] = v2

    spec = pl.BlockSpec((tr, cols), lambda i: (i, 0))
    shp = SDS((rows, cols), f32)
    return _hbm_call(body, name=name, out_shape=(shp, shp, shp), grid=(rows // tr,), in_specs=[spec] * 4 + [ANY_SPEC],
                     out_specs=(spec, spec, spec))(w, g, m, v, g if after is None else after)


def _adamw_halves(w, g_mine, g_other, m, v, cvec, name, after=None):
    rows, cols = w.shape
    half = rows // 2
    tr = min(256, half)
    per_half = half // tr

    def body(c_ref, w_ref, ga_ref, gb_ref, m_ref, v_ref, after_ref, g_ref, d_ref, m2_ref, v2_ref):
        in_my_half = (pl.program_id(0) // per_half) == c_ref[0]
        gv = jnp.where(in_my_half, ga_ref[...], gb_ref[...])
        g_ref[...] = gv
        m2 = ADAM_B1 * m_ref[...] + (1.0 - ADAM_B1) * gv
        v2 = ADAM_B2 * v_ref[...] + (1.0 - ADAM_B2) * jnp.square(gv)
        m_hat = m2 / (1.0 - ADAM_B1 ** ADAM_STEP)
        v_hat = v2 / (1.0 - ADAM_B2 ** ADAM_STEP)
        d_ref[...] = -ADAM_LR * (m_hat / (jnp.sqrt(v_hat) + ADAM_EPS) + ADAM_WD * w_ref[...])
        m2_ref[...] = m2
        v2_ref[...] = v2

    full = pl.BlockSpec((tr, cols), lambda i, c: (i, 0))
    part = pl.BlockSpec((tr, cols), lambda i, c: (i % per_half, 0))
    shp = SDS((rows, cols), f32)
    grid_spec = pltpu.PrefetchScalarGridSpec(num_scalar_prefetch=1, grid=(rows // tr,),
                                             in_specs=[full, part, part, full, full, ANY_SPEC], out_specs=(full,) * 4)
    return _hbm_call(body, name=name, out_shape=(shp,) * 4, grid_spec=grid_spec)(
        cvec, w, g_mine, g_other, m, v, cvec if after is None else after)


def _adam_math(w, g, m, v):
    m2 = ADAM_B1 * m + (1.0 - ADAM_B1) * g
    v2 = ADAM_B2 * v + (1.0 - ADAM_B2) * jnp.square(g)
    m_hat = m2 / (1.0 - ADAM_B1 ** ADAM_STEP)
    v_hat = v2 / (1.0 - ADAM_B2 ** ADAM_STEP)
    return -ADAM_LR * (m_hat / (jnp.sqrt(v_hat) + ADAM_EPS) + ADAM_WD * w), m2, v2


def _adamw_small(red, g_c_ctx, jvec, ws, ms, vs):
    n = len(ws)

    def body(*refs):
        red_ref, gc_ref, j_ref = refs[:3]
        w_refs, m_refs, v_refs = refs[3:3 + n], refs[3 + n:3 + 2 * n], refs[3 + 2 * n:3 + 3 * n]
        outs = refs[3 + 3 * n:]
        g_out, d_out, m_out, v_out = outs[:n], outs[n:2 * n], outs[2 * n:3 * n], outs[3 * n:]
        chip = j_ref[0]
        lanes = lambda i: (slice(None), slice(128 * i, 128 * (i + 1)))
        row = lambda r0, i: (lambda: red_ref[r0 + i:r0 + i + 1, :])
        whole = (slice(None), slice(None))
        chunks = [
            [((slice(None),), lambda: gc_ref[...].reshape(D))],
            [(lanes(i), row(RS_B_ADA, i)) for i in range(3 * D // 128)],
            [(lanes(i), row(RS_NORM_G, i)) for i in range(D // 128)],
            [(whole, lambda: red_ref[RS_GQ:RS_GQ + 1, 0:DH])],
            [(whole, lambda: red_ref[RS_GK:RS_GK + 1, 0:DH])],
            [((dr,), (lambda dr=dr: red_ref[pl.ds(RS_RPB + dr, H, stride=N_DR), 0:N_DC])) for dr in range(N_DR)],
            [((r,), (lambda r=r: red_ref[pl.ds(RS_CONV_W + 4 * r + chip, 1), :])) for r in range(3)],
            [(lanes(i), row(RS_CONV_B, i)) for i in range(DC // 128)],
        ]
        for a in range(n):
            for idx, grad in chunks[a]:
                g = grad()
                d, m2, v2 = _adam_math(w_refs[a][idx], g, m_refs[a][idx], v_refs[a][idx])
                g_out[a][idx] = g
                d_out[a][idx] = d
                m_out[a][idx] = m2
                v_out[a][idx] = v2

    shapes = [SDS(w.shape, f32) for w in ws]
    res = _pallas_call(body, name="adamw_small", out_shape=shapes * 4,
                       in_specs=[VMEM_SPEC, VMEM_SPEC, SMEM_SPEC] + [VMEM_SPEC] * (3 * n),
                       out_specs=[VMEM_SPEC] * (4 * n))(red, g_c_ctx, jvec, *ws, *ms, *vs)
    return [list(res[k * n:(k + 1) * n]) for k in range(4)]


def _rows128(a):
    return a.reshape(-1, 128)


def kernel(x, c, ctx, c_ctx, w_ada, b_ada, norm_g, w_in, q_norm_g, k_norm_g, rpb, conv_w, conv_b, w_out, loss_target, m_c_ctx, m_w_ada, m_b_ada, m_norm_g, m_w_in, m_q_norm_g, m_k_norm_g, m_rpb, m_conv_w, m_conv_b, m_w_out, v_c_ctx, v_w_ada, v_b_ada, v_norm_g, v_w_in, v_q_norm_g, v_k_norm_g, v_rpb, v_conv_w, v_conv_b, v_w_out):
    xi, yi, ci = lax.axis_index("x"), lax.axis_index("y"), lax.axis_index("c")
    dev = 4 * xi + 2 * yi + ci
    chip = 2 * xi + yi
    cvec = jnp.reshape(ci, (1,)).astype(i32)
    jvec = jnp.reshape(chip, (1,)).astype(i32)
    w_ada_s = w_ada[0]
    ncol = w_ada_s.shape[1]

    gc = _split_start([_to_slot(c.reshape(8, 128), 8, dev)], [], 7, _gather8_copies, "gather_c_start")
    wo4c = _cast_to_slot(w_out[0], jvec, "cast_w_out", after=gc[3])
    w4c = _cast_to_slot(w_in[0], jvec, "cast_w_in", after=wo4c)
    (c8,), _ = _split_wait(gc[0], gc[1], [gc[2]], [], w4c, _gather8_copies, "gather_c_wait")
    cc = jnp.concatenate([c8.reshape(8, D), c_ctx.reshape(1, D), jnp.zeros((7, D), f32)], axis=0)
    m_shard, sc16 = _adaln_shard(cc, w_ada_s)

    conv_w_pad = jnp.pad(conv_w[0], ((0, 5), (0, 0)))
    gm = _split_start([_to_slot(m_shard, 4, chip), _to_slot(conv_w_pad, 4, chip)], [], 6, _gather4_copies,
                      "gather_mod_start")

    all_k = [(0, 0, 0), (0, 0, 1), (0, 0, 2)]
    sem_a, rem_a, w4s, token = _split_start([w4c], [], 1, _near_copies, "weights_near_start", after=gm[4])
    (m4, cw4), _ = _split_wait(gm[0], gm[1], [gm[2], gm[3]], [], token, _gather4_copies, "gather_mod_wait")
    m_full = jnp.transpose(m4, (1, 0, 2)).reshape(16, 4 * ncol)
    mrow = lax.dynamic_slice(m_full, (dev, 0), (1, 3 * D))
    mrow_c = m_full[8:9]
    conv_w_full = jnp.transpose(cw4[:, 0:3, :], (1, 0, 2)).reshape(3, DC)
    waves = {}

    def near(after):
        (w4w,), _ = _split_wait(sem_a, rem_a, [w4s], [], after, _near_copies, "weights_near_wait")
        sem_b, rem_b, w4b, started = _split_start([w4w], [], 2, _pass_relay_copies, "weights_pass_start")
        waves["pass"] = (sem_b, rem_b)
        return w4b, started

    def near2(w4, after):
        (w4w,), _ = _split_wait(*waves["pass"], [w4], [], after, _pass_copy, "weights_pass_wait")
        return w4w

    def far(w4, after):
        (w4w,), _ = _split_wait(*waves["pass"], [w4], [], after, _relay_copy, "weights_far_wait")
        w4x, sem_c, rem_c, wo4s, started = _forward_then_start(w4w, wo4c, all_k, "weights_far_forward_out_start")
        (w4f,), _ = _split_wait(sem_c, rem_c, [w4x], [], started, _diag_forward_copy, "weights_far_forward_wait")
        waves["out"] = (sem_c, rem_c, wo4s)
        return w4f, started

    def w_out_arrived(after):
        sem_c, rem_c, wo4s = waves["out"]
        (wow,) = _halves_wait(sem_c, rem_c, [wo4s], after, all_k, "weights_out_wait")
        sem_d, rem_d, wof, started = _split_start([wow], [], 3, _forward_copies, "weights_out_forward_start")
        waves["out_forward"] = (sem_d, rem_d, wof)
        return started

    def w_out_gathered(after):
        sem_d, rem_d, wof = waves["out_forward"]
        (wo,), _ = _split_wait(sem_d, rem_d, [wof], [], after, _forward_copies, "weights_out_forward_wait")
        return wo.reshape(D, D)

    weights = dict(own=w_in[0], jvec=jvec, started=token, first=jvec ^ (1 + cvec), second=jvec ^ (2 - cvec),
                   near=near, near2=near2, far=far, out_arrived=w_out_arrived, out=w_out_gathered)

    exchange = _exchange_copies
    pending = {}

    def on_g_w_out(g_w_out):
        out = _split_start([g_w_out], [SDS((4, D // 8, D), f32)], 4, exchange, "grad_out_pair_start")
        pending["ex_out"] = out
        return out[4]

    def after_conv(dp8):
        ssem_o, rsem_o, g_o, land_o, _ = pending["ex_out"]
        (g_o,), (ex_o,) = _split_wait(ssem_o, rsem_o, [g_o], [land_o], dp8, exchange, "grad_out_pair_wait")
        to32, tob = _pair_sum_w_out(g_o, ex_o, cvec)
        out = _split_start([tob], [SDS((3, D // 8, D), bf16)], 3, _scatter_copies, "grad_out_chip_start")
        pending["sc_out"] = (out, to32)
        return out[4]

    def on_g_w_in_first(g_first):
        out = _split_start([g_first], [SDS((4, D // 2, D), f32)], 4, _block_exchange_copies, "grad_pair_start")
        pending["ex"] = (out[0], out[1], [out[2]], [out[3]])
        return out[4]

    def on_g_w_in_second(g_second):
        ex_ssem, ex_rsem, ex_srcs, ex_lands = pending["ex"]
        _, ex = _split_wait(ex_ssem, ex_rsem, ex_srcs, ex_lands, g_second, _block_exchange_copies, "grad_pair_wait")
        t32, tb = _pair_sum_w_in(g_second, ex[0])
        out = _split_start([tb], [SDS((3, D // 2, D), bf16)], 3, _scatter_copies, "grad_chip_start")
        pending["sc_in"] = (out, t32)
        return out[4]

    r = _local_step(x[0], ctx[0], loss_target[0], mrow, mrow_c, b_ada, norm_g, weights, q_norm_g, k_norm_g,
                    rpb[0], conv_w_full, conv_b,
                    dict(g_w_out=on_g_w_out, after_conv=after_conv, first_half=1 - cvec,
                         g_w_in_first=on_g_w_in_first, g_w_in_second=on_g_w_in_second))
    sc_in, t32 = pending["sc_in"]
    _, (r2,) = _split_wait(sc_in[0], sc_in[1], [sc_in[2]], [sc_in[3]], r["dng"], _scatter_copies, "grad_chip_wait")
    u_in = _chip_sum(t32, r2, jvec, "chip_sum_w_in")
    sc_o, to32 = pending["sc_out"]
    _, (ro2,) = _split_wait(sc_o[0], sc_o[1], [sc_o[2]], [sc_o[3]], u_in, _scatter_copies, "grad_out_chip_wait")
    u_out = _chip_sum(to32, ro2, jvec, "chip_sum_w_out")
    swap = _split_start([u_in, u_out], [SDS(u_in.shape, f32), SDS(u_out.shape, f32)], 2, _swap_copies,
                        "grad_pair_swap_start")

    dm = jnp.concatenate([r["dshift"], r["dscale"], r["dgate"]], axis=1)
    dmc = jnp.concatenate([r["dshift_c"], r["dscale_c"], jnp.zeros((1, D), f32)], axis=1)
    pack_parts = [_rows128(dm), _rows128(dmc), _rows128(r["dng"]), _rows128(r["dng_c"]), _rows128(r["g_gq"]),
                  _rows128(r["g_gk"]), _rows128(r["g_gk_c"]), r["g_rpb"].reshape(H * N_DR, 128),
                  _rows128(r["g_conv_b"]), _rows128(r["g_conv_w"][0:3]), jnp.pad(r["loss_sum"], ((0, 0), (0, 127)))]
    pack = jnp.concatenate([jnp.pad(p, ((0, -p.shape[0] % 8), (0, 0))) for p in pack_parts], axis=0)
    assert pack.shape[0] == PK_ROWS
    gs = _split_start([_to_slot(pack, 8, dev)], [], 7, _gather8_copies, "gather_small_start", after=swap[6])
    (u_in, u_out), (o_in, o_out) = _split_wait(swap[0], swap[1], swap[2:4], swap[4:6], gs[3], _swap_copies,
                                               "grad_pair_swap_wait")
    g_w_in_s, d_w_in, nm_w_in, nv_w_in = _adamw_halves(w_in[0], u_in, o_in, m_w_in[0], v_w_in[0], cvec, "adamw_w_in",
                                                       after=gs[3])
    g_w_out_s, d_w_out, nm_w_out, nv_w_out = _adamw_halves(w_out[0], u_out, o_out, m_w_out[0], v_w_out[0], cvec,
                                                           "adamw_w_out", after=nm_w_in)
    (gathered,), _ = _split_wait(gs[0], gs[1], [gs[2]], [], nm_w_out, _gather8_copies, "gather_small_wait")
    red, dm16 = _small_reduce(gathered)
    loss = red[RS_LOSS, 0] * (0.5 / D)

    g_w_ada_s, cpart = _w_ada_grad(sc16, dm16, w_ada_s, jvec)
    gcp = _split_start([_to_slot(cpart, 4, chip)], [], 3, _gather4_copies, "gather_c_ctx_parts_start")
    d_w_ada, nm_w_ada, nv_w_ada = _adamw(w_ada_s, g_w_ada_s, m_w_ada[0], v_w_ada[0], "adamw_w_ada", after=gcp[3])
    (cparts4,), _ = _split_wait(gcp[0], gcp[1], [gcp[2]], [], nm_w_ada, _gather4_copies, "gather_c_ctx_parts_wait")
    g_c_ctx = _c_ctx_grad(cparts4, c_ctx)

    t_rpb = lambda a: jnp.transpose(a, (0, 2, 1, 3)).reshape(N_DR, H, N_DC)
    t_cw = lambda a: jnp.transpose(a, (1, 0, 2))
    small = _adamw_small(
        red, g_c_ctx, jvec,
        [c_ctx, b_ada, norm_g, q_norm_g, k_norm_g, t_rpb(rpb), t_cw(conv_w), conv_b],
        [m_c_ctx, m_b_ada, m_norm_g, m_q_norm_g, m_k_norm_g, t_rpb(m_rpb), t_cw(m_conv_w), m_conv_b],
        [v_c_ctx, v_b_ada, v_norm_g, v_q_norm_g, v_k_norm_g, t_rpb(v_rpb), t_cw(v_conv_w), v_conv_b])
    for kind in small:
        kind[5] = jnp.transpose(kind[5].reshape(1, N_DR, H, N_DC), (0, 2, 1, 3))
        kind[6] = jnp.transpose(kind[6], (1, 0, 2))

    def ordered(kind, big_w_ada, big_w_in, big_w_out):
        s_c_ctx, s_b_ada, s_norm_g, s_q, s_k, s_rpb, s_conv_w, s_conv_b = small[kind]
        return [s_c_ctx, big_w_ada[None], s_b_ada, s_norm_g, big_w_in[None], s_q, s_k, s_rpb, s_conv_w,
                s_conv_b, big_w_out[None]]

    grads = ordered(0, g_w_ada_s, g_w_in_s, g_w_out_s)
    deltas = ordered(1, d_w_ada, d_w_in, d_w_out)
    new_m = ordered(2, nm_w_ada, nm_w_in, nm_w_out)
    new_v = ordered(3, nv_w_ada, nv_w_in, nv_w_out)
    return (loss, r["grad_x"][None], *grads, *deltas, *new_m, *new_v)
```

```python
import functools

import jax
import jax.numpy as jnp
import numpy as np
from jax import lax
from jax.experimental import pallas as pl
from jax.experimental.pallas import tpu as pltpu

f32, bf16, i32 = jnp.float32, jnp.bfloat16, jnp.int32
MESH = pl.DeviceIdType.MESH
HIGHEST = lax.Precision.HIGHEST

D = 1024
S = 2048
L = 256
GW = 64
ROWS = S // GW
H = 8
DH = 64
DA = H * DH
DC = 512
WIN_H, WIN_W = 8, 16
N_DR, N_DC = 2 * WIN_H - 1, 2 * WIN_W - 1
RMS_EPS = 1e-6
ROPE_THETA = 10000.0
QK_SCALE = DH ** -0.5
NEG = -1e30

QB = 128
NQB = S // QB
KR = 9
KB = KR * GW
TILE_GEOM = ((0, 0), (2, 0), (4, 0), (28, 23), (30, 23))
NT = len(TILE_GEOM)

ADAM_LR, ADAM_B1, ADAM_B2, ADAM_EPS, ADAM_WD, ADAM_STEP = 0.001, 0.9, 0.999, 1e-08, 0.01, 10

VMEM_SPEC = pl.BlockSpec(memory_space=pltpu.VMEM)
ANY_SPEC = pl.BlockSpec(memory_space=pl.ANY)
SMEM_SPEC = pl.BlockSpec(memory_space=pltpu.SMEM)
SDS = jax.ShapeDtypeStruct


_pallas_call = pl.pallas_call


def _hbm_call(body, *, out_shape, in_specs=None, out_specs=None, grid_spec=None, **kw):
    n_pre = 0
    if grid_spec is not None:
        ispecs, ospecs, n_pre = grid_spec.in_specs, grid_spec.out_specs, grid_spec.num_scalar_prefetch
        kw["grid_spec"] = grid_spec
    else:
        ispecs, ospecs = in_specs, out_specs
        kw.update(in_specs=in_specs, out_specs=out_specs)

    def blocked(spec):
        return isinstance(spec, pl.BlockSpec) and spec.block_shape is not None

    single = not isinstance(out_shape, (tuple, list))
    shapes = [out_shape] if single else list(out_shape)
    ospec_list = list(ospecs) if isinstance(ospecs, (tuple, list)) else [ospecs]
    shapes = [pltpu.HBM(s.shape, s.dtype) if blocked(sp) else s for s, sp in zip(shapes, ospec_list)]
    call = _pallas_call(body, out_shape=shapes[0] if single else tuple(shapes), **kw)

    def run(*args):
        arrays = [pltpu.with_memory_space_constraint(a, pltpu.HBM) if blocked(sp) else a
                  for a, sp in zip(args[n_pre:], ispecs)]
        return call(*args[:n_pre], *arrays)

    return run


def _cp(vmem_mb=None, **kw):
    if vmem_mb is not None:
        kw["vmem_limit_bytes"] = vmem_mb << 20
    return pltpu.CompilerParams(**kw)


def _silu(z):
    return z * jax.nn.sigmoid(z)


def _dsilu(z):
    sg = jax.nn.sigmoid(z)
    return sg * (1.0 + z * (1.0 - sg))


def _row_start(i):
    return min(max(i - WIN_H // 2, 0), ROWS - WIN_H)


def _my_pos():
    return lax.axis_index("x"), lax.axis_index("y"), lax.axis_index("c")


def _flip(v, bit):
    return 1 - v if bit else v


def _swap_copies(srcs, lands, ssem, rsem):
    x, y, c = _my_pos()
    return [pltpu.make_async_remote_copy(src_ref=srcs[a], dst_ref=lands[a], send_sem=ssem.at[a], recv_sem=rsem.at[a],
                                         device_id=(x, y, 1 - c), device_id_type=MESH) for a in range(len(srcs))]


HBM_SPEC = pl.BlockSpec(memory_space=pltpu.HBM)
SEM_SPEC = pl.BlockSpec(memory_space=pltpu.SEMAPHORE)
DATAFLOW = pltpu.SideEffectType.DATAFLOW_SIDE_EFFECTING


def _peer_chips(x, y, c):
    out = []
    for k in range(1, 4):
        px, py = _flip(x, (k >> 1) & 1), _flip(y, k & 1)
        out.append(((px, py, c), 2 * px + py))
    return out


def _half_copies(srcs, dsts, ssem, rsem, which):
    x, y, c = _my_pos()
    j = 2 * x + y
    peers = _peer_chips(x, y, c)
    pairs = []
    for pos, group, k in which:
        half = srcs[pos].shape[1] // 2
        mine = pl.ds(pl.multiple_of(c * half, 8), half)
        dev, pj = peers[k]
        sem = 3 * group + k
        send = pltpu.make_async_remote_copy(src_ref=srcs[pos].at[j, mine], dst_ref=dsts[pos].at[j, mine],
                                            send_sem=ssem.at[sem], recv_sem=rsem.at[sem], device_id=dev,
                                            device_id_type=MESH)
        arrive = pltpu.make_async_remote_copy(src_ref=srcs[pos].at[j, mine], dst_ref=dsts[pos].at[pj, mine],
                                              send_sem=ssem.at[sem], recv_sem=rsem.at[sem], device_id=dev,
                                              device_id_type=MESH)
        pairs.append((send, arrive))
    return pairs


def _halves_wait(ssem, rsem, bigs, after, which, name):
    nb = len(bigs)

    def body(*refs):
        b_in = refs[:nb]
        ssem_ref, rsem_ref = refs[nb], refs[nb + 1]
        for send, arrive in _half_copies(b_in, b_in, ssem_ref, rsem_ref, which):
            send.wait_send()
            arrive.wait_recv()

    return _hbm_call(
        body, name=name, out_shape=tuple(pltpu.HBM(b.shape, b.dtype) for b in bigs),
        in_specs=[HBM_SPEC] * nb + [SEM_SPEC, SEM_SPEC, ANY_SPEC], out_specs=tuple([HBM_SPEC] * nb),
        input_output_aliases={a: a for a in range(nb)}, compiler_params=_cp(has_side_effects=DATAFLOW),
    )(*bigs, ssem, rsem, after)


FORWARD_SEM = 3


def _diag_forward_copy(srcs, dsts, ssem, rsem):
    x, y, c = _my_pos()
    half = srcs[0].shape[1] // 2
    diag = 3 - (2 * x + y)
    mine = pl.ds(pl.multiple_of(c * half, 8), half)
    other = pl.ds(pl.multiple_of((1 - c) * half, 8), half)
    return [_Copy(srcs[0].at[diag, mine], dsts[0].at[diag, mine], dsts[0].at[diag, other], ssem.at[FORWARD_SEM],
                  rsem.at[FORWARD_SEM], (x, y, 1 - c))]


def _forward_then_start(fwd, big, order, name):
    def body(f_in, b_in, f_out, ssem, rsem, b_out, token):
        _diag_forward_copy([f_in], [f_out], ssem, rsem)[0].start()
        for send, _ in _half_copies([b_in], [b_out], ssem, rsem, order):
            send.start()
        token[...] = jnp.zeros_like(token)

    n_sem = FORWARD_SEM + 1
    out_shape = (pltpu.HBM(fwd.shape, fwd.dtype), pltpu.SemaphoreType.DMA((n_sem,)), pltpu.SemaphoreType.DMA((n_sem,)),
                 pltpu.HBM(big.shape, big.dtype), SDS((8, 128), f32))
    return _hbm_call(
        body, name=name, out_shape=out_shape, in_specs=[HBM_SPEC, HBM_SPEC],
        out_specs=(HBM_SPEC, SEM_SPEC, SEM_SPEC, HBM_SPEC, VMEM_SPEC), input_output_aliases={0: 0, 1: 3},
        compiler_params=_cp(has_side_effects=DATAFLOW),
    )(*[pltpu.with_memory_space_constraint(b, pltpu.HBM) for b in (fwd, big)])


def _cast_to_slot(w, jvec, name, after=None):
    rows, cols = w.shape
    tr = 256

    def body(j_ref, w_ref, after_ref, o_ref):
        o_ref[...] = w_ref[...].astype(bf16)

    grid_spec = pltpu.PrefetchScalarGridSpec(
        num_scalar_prefetch=1, grid=(rows // tr,),
        in_specs=[pl.BlockSpec((tr, cols), lambda i, j: (i, 0)), ANY_SPEC],
        out_specs=pl.BlockSpec((None, tr, cols), lambda i, j: (j[0], i, 0)))
    return _hbm_call(body, name=name, out_shape=SDS((4, rows, cols), bf16),
                     grid_spec=grid_spec)(jvec, w, jvec if after is None else after)


def _exchange_copies(srcs, lands, ssem, rsem):
    x, y, c = _my_pos()
    half = srcs[0].shape[0] // 8
    cps = []
    for jb in range(4):
        src = srcs[0].at[pl.ds(pl.multiple_of((2 * jb + 1 - c) * half, 8), half)]
        cps.append(pltpu.make_async_remote_copy(src_ref=src, dst_ref=lands[0].at[jb], send_sem=ssem.at[jb],
                                                recv_sem=rsem.at[jb], device_id=(x, y, 1 - c), device_id_type=MESH))
    return cps


def _block_exchange_copies(srcs, lands, ssem, rsem):
    x, y, c = _my_pos()
    return [pltpu.make_async_remote_copy(src_ref=srcs[0].at[jb], dst_ref=lands[0].at[jb], send_sem=ssem.at[jb],
                                         recv_sem=rsem.at[jb], device_id=(x, y, 1 - c), device_id_type=MESH)
            for jb in range(4)]


def _scatter_copies(srcs, lands, ssem, rsem):
    x, y, c = _my_pos()
    cps = []
    for a in range(len(srcs)):
        for k, (dev, pj) in enumerate(_peer_chips(x, y, c)):
            cps.append(pltpu.make_async_remote_copy(src_ref=srcs[a].at[pj], dst_ref=lands[a].at[k],
                                                    send_sem=ssem.at[3 * a + k], recv_sem=rsem.at[3 * a + k],
                                                    device_id=dev, device_id_type=MESH))
    return cps


class _Copy:
    def __init__(self, src, dst, arrive, ssem, rsem, dev):
        make = lambda to: pltpu.make_async_remote_copy(src_ref=src, dst_ref=to, send_sem=ssem, recv_sem=rsem,
                                                       device_id=dev, device_id_type=MESH)
        send, arrival = make(dst), make(arrive)
        self.start, self.wait_send, self.wait_recv = send.start, send.wait_send, arrival.wait_recv


def _toward(x, y, along_x):
    return x + along_x * (1 - 2 * x), y + (1 - along_x) * (1 - 2 * y)


def _near_copies(srcs, dsts, ssem, rsem):
    x, y, c = _my_pos()
    px, py = _toward(x, y, c)
    j = 2 * x + y
    return [_Copy(srcs[0].at[j], dsts[0].at[j], dsts[0].at[2 * px + py], ssem.at[0], rsem.at[0], (px, py, c))]


def _pass_copy(srcs, dsts, ssem, rsem):
    x, y, c = _my_pos()
    px, py = _toward(x, y, c)
    qx, qy = _toward(x, y, 1 - c)
    got = 2 * px + py
    return [_Copy(srcs[0].at[got], dsts[0].at[got], dsts[0].at[2 * qx + qy], ssem.at[0], rsem.at[0], (x, y, 1 - c))]


def _relay_copy(srcs, dsts, ssem, rsem):
    x, y, c = _my_pos()
    px, py = _toward(x, y, c)
    qx, qy = _toward(x, y, 1 - c)
    half = srcs[0].shape[1] // 2
    mine = pl.ds(pl.multiple_of(c * half, 8), half)
    got, diag = 2 * px + py, 3 - (2 * x + y)
    return [_Copy(srcs[0].at[got, mine], dsts[0].at[got, mine], dsts[0].at[diag, mine], ssem.at[1], rsem.at[1],
                  (qx, qy, c))]


def _forward_copies(srcs, dsts, ssem, rsem):
    x, y, c = _my_pos()
    half = srcs[0].shape[1] // 2
    mine = pl.ds(pl.multiple_of(c * half, 8), half)
    other = pl.ds(pl.multiple_of((1 - c) * half, 8), half)
    return [_Copy(srcs[0].at[pj, mine], dsts[0].at[pj, mine], dsts[0].at[pj, other], ssem.at[k], rsem.at[k],
                  (x, y, 1 - c)) for k, (_, pj) in enumerate(_peer_chips(x, y, c))]


def _pass_relay_copies(srcs, dsts, ssem, rsem):
    return _pass_copy(srcs, dsts, ssem, rsem) + _relay_copy(srcs, dsts, ssem, rsem)


def _gather8_copies(srcs, dsts, ssem, rsem):
    x, y, c = _my_pos()
    me = 4 * x + 2 * y + c
    cps = []
    for a in range(len(srcs)):
        for k in range(1, 8):
            tgt = (_flip(x, (k >> 2) & 1), _flip(y, (k >> 1) & 1), _flip(c, k & 1))
            cps.append(_Copy(srcs[a].at[me], dsts[a].at[me], dsts[a].at[4 * tgt[0] + 2 * tgt[1] + tgt[2]],
                             ssem.at[7 * a + k - 1], rsem.at[7 * a + k - 1], tgt))
    return cps


def _gather4_copies(srcs, dsts, ssem, rsem):
    x, y, c = _my_pos()
    j = 2 * x + y
    cps = []
    for a in range(len(srcs)):
        for k, (dev, pj) in enumerate(_peer_chips(x, y, c)):
            cps.append(_Copy(srcs[a].at[j], dsts[a].at[j], dsts[a].at[pj], ssem.at[3 * a + k], rsem.at[3 * a + k], dev))
    return cps


def _to_slot(a, n, i):
    return lax.dynamic_update_slice(jnp.zeros((n,) + a.shape, a.dtype), a[None], (i,) + (0,) * a.ndim)


def _split_start(srcs, land_shapes, n_cp, make, name, after=None):
    ns, nl = len(srcs), len(land_shapes)
    n_in = ns + nl + (after is not None)

    def body(*refs):
        s_in = refs[:ns]
        ssem, rsem = refs[n_in], refs[n_in + 1]
        s_out = refs[n_in + 2:n_in + 2 + ns]
        l_out = refs[n_in + 2 + ns:n_in + 2 + ns + nl]
        token = refs[n_in + 2 + ns + nl]
        for cp in make(s_in, l_out if nl else s_out, ssem, rsem):
            cp.start()
        token[...] = jnp.zeros_like(token)

    lands = [pltpu.with_memory_space_constraint(lax.empty(sh.shape, sh.dtype), pltpu.HBM) for sh in land_shapes]
    out_shape = (pltpu.SemaphoreType.DMA((n_cp,)), pltpu.SemaphoreType.DMA((n_cp,)),
                 *[pltpu.HBM(b.shape, b.dtype) for b in srcs], *[pltpu.HBM(b.shape, b.dtype) for b in land_shapes],
                 SDS((8, 128), f32))
    return _hbm_call(
        body, name=name, out_shape=out_shape, in_specs=[HBM_SPEC] * (ns + nl) + [ANY_SPEC] * (after is not None),
        out_specs=(SEM_SPEC, SEM_SPEC, *[HBM_SPEC] * (ns + nl), VMEM_SPEC),
        input_output_aliases={i: 2 + i for i in range(ns + nl)}, compiler_params=_cp(has_side_effects=DATAFLOW),
    )(*[pltpu.with_memory_space_constraint(b, pltpu.HBM) for b in srcs], *lands, *([] if after is None else [after]))


def _split_wait(ssem, rsem, srcs, lands, after, make, name):
    ns, nl = len(srcs), len(lands)

    def body(*refs):
        s_in, l_in = refs[:ns], refs[ns:ns + nl]
        ssem_ref, rsem_ref = refs[ns + nl], refs[ns + nl + 1]
        for cp in make(s_in, l_in if nl else s_in, ssem_ref, rsem_ref):
            cp.wait_send()
            cp.wait_recv()

    outs = _hbm_call(
        body, name=name, out_shape=tuple(pltpu.HBM(b.shape, b.dtype) for b in (*srcs, *lands)),
        in_specs=[HBM_SPEC] * (ns + nl) + [SEM_SPEC, SEM_SPEC, ANY_SPEC], out_specs=tuple([HBM_SPEC] * (ns + nl)),
        input_output_aliases={i: i for i in range(ns + nl)}, compiler_params=_cp(has_side_effects=DATAFLOW),
    )(*srcs, *lands, ssem, rsem, after)
    return list(outs[:ns]), list(outs[ns:])


def _adaln_shard(cc, w_ada_shard):
    def body(c_ref, w_ref, m_ref, sc_ref):
        sc = _silu(c_ref[...])
        sc_ref[...] = sc
        m_ref[...] = jnp.dot(sc, w_ref[...], precision=HIGHEST, preferred_element_type=f32)

    return _hbm_call(
        body, name="adaln_shard", out_shape=(SDS((16, w_ada_shard.shape[1]), f32), SDS((16, D), f32)),
        in_specs=[VMEM_SPEC, VMEM_SPEC], out_specs=(VMEM_SPEC, VMEM_SPEC), compiler_params=_cp(32),
    )(cc, w_ada_shard)


def _prenorm(xx, norm_g, mrow, b_ada, tm, name, after=None):
    n = xx.shape[0]

    def body(x_ref, g_ref, m_ref, b_ref, after_ref, h_ref):
        x = x_ref[...]
        shift = m_ref[:, 0:D] + b_ref[:, 0:D]
        scale = m_ref[:, D:2 * D] + b_ref[:, D:2 * D]
        r = lax.rsqrt(jnp.mean(x * x, axis=-1, keepdims=True) + RMS_EPS)
        y = (x * r) * g_ref[...]
        h_ref[...] = (y * (1.0 + scale) + shift).astype(bf16)

    row = lambda i: (i, 0)
    fixed = lambda i: (0, 0)
    return _hbm_call(
        body, name=name, out_shape=SDS((n, D), bf16), grid=(n // tm,),
        in_specs=[pl.BlockSpec((tm, D), row), pl.BlockSpec((1, D), fixed), pl.BlockSpec((1, 3 * D), fixed),
                  pl.BlockSpec((1, 3 * D), fixed), ANY_SPEC],
        out_specs=pl.BlockSpec((tm, D), row),
    )(xx, norm_g, mrow, b_ada, b_ada if after is None else after)


def _in_proj_own(h, w_own, jvec):
    tm = 512

    def body(j_ref, h_ref, w_ref, p_ref):
        p_ref[...] = jnp.dot(h_ref[...], w_ref[...].astype(bf16), preferred_element_type=f32)

    grid_spec = pltpu.PrefetchScalarGridSpec(
        num_scalar_prefetch=1, grid=(S // tm,),
        in_specs=[pl.BlockSpec((tm, D), lambda i, j: (i, 0)), pl.BlockSpec((D, D), lambda i, j: (0, 0))],
        out_specs=pl.BlockSpec((tm, D), lambda i, j: (i, j[0])))
    return _hbm_call(body, name="in_proj_own", out_shape=SDS((S, 4 * D), f32), grid_spec=grid_spec,
                     compiler_params=_cp(40))(jvec, h, w_own)


def _in_proj_block(h, w4, p, bvec, name, after=None):
    tm = 512

    def body(b_ref, h_ref, w_ref, p_in_ref, after_ref, p_ref):
        p_ref[...] = jnp.dot(h_ref[...], w_ref[...], preferred_element_type=f32)

    grid_spec = pltpu.PrefetchScalarGridSpec(
        num_scalar_prefetch=1, grid=(S // tm,),
        in_specs=[pl.BlockSpec((tm, D), lambda i, b: (i, 0)), pl.BlockSpec((None, D, D), lambda i, b: (b[0], 0, 0)),
                  ANY_SPEC, ANY_SPEC],
        out_specs=pl.BlockSpec((tm, D), lambda i, b: (i, b[0])))
    return _hbm_call(body, name=name, out_shape=SDS((S, 4 * D), f32), grid_spec=grid_spec,
                     input_output_aliases={3: 0})(bvec, h, w4, p, bvec if after is None else after)


def _ctx_proj(hc, w4):
    def body(h_ref, w0_ref, w1_ref, p_ref):
        hv = h_ref[...]
        p_ref[:, 0:DA] = jnp.dot(hv, w0_ref[:, DA:2 * DA], preferred_element_type=f32)
        p_ref[:, DA:2 * DA] = jnp.dot(hv, w1_ref[:, 0:DA], preferred_element_type=f32)

    return _hbm_call(
        body, name="ctx_proj", out_shape=SDS((L, 2 * DA), f32), grid=(1,),
        in_specs=[pl.BlockSpec((L, D), lambda i: (0, 0)), pl.BlockSpec((None, D, D), lambda i: (0, 0, 0)),
                  pl.BlockSpec((None, D, D), lambda i: (1, 0, 0))],
        out_specs=pl.BlockSpec((L, 2 * DA), lambda i: (0, 0)),
    )(hc, w4, w4)


def _head_ones():
    r = lax.broadcasted_iota(i32, (DA, DA), 0) // DH
    c = lax.broadcasted_iota(i32, (DA, DA), 1) // DH
    return (r == c).astype(bf16)


def _head_sum(v, ones_bd):
    hi = v.astype(bf16)
    lo = (v - hi.astype(f32)).astype(bf16)
    return jnp.dot(hi, ones_bd, preferred_element_type=f32) + jnp.dot(lo, ones_bd, preferred_element_type=f32)


def _swap16(v):
    lane = lax.broadcasted_iota(i32, v.shape, 1)
    return jnp.where((lane & 31) < 16, pltpu.roll(v, DA - 16, 1), pltpu.roll(v, 16, 1))


def _rope_block(ct_ref, rt_ref, tm):
    rows = [jnp.tile(rt_ref[8 * j:8 * j + 8, :], (GW // 8, 1)) for j in range(tm // GW)]
    return jnp.tile(ct_ref[...], (tm // GW, 1)) + jnp.concatenate(rows, axis=0)


def _rope_specs(tm):
    col = pl.BlockSpec((GW, DA), lambda i: (0, 0))
    row = pl.BlockSpec((8 * tm // GW, DA), lambda i: (i, 0))
    return [col, row, col, row]


def _qk_prep(p, gq, gk, rope):
    tm = 256

    def body(qk_ref, v_ref, gq_ref, gk_ref, cc_ref, cr_ref, sc_ref, sr_ref, qr_ref, qp_ref, kr_ref, vh_ref):
        ones_bd = _head_ones()
        cs, sn = _rope_block(cc_ref, cr_ref, tm), _rope_block(sc_ref, sr_ref, tm)
        q = qk_ref[:, 0:DA]
        k = qk_ref[:, DA:2 * DA]
        yq = (q * lax.rsqrt(_head_sum(q * q, ones_bd) * (1.0 / DH) + RMS_EPS)) * gq_ref[...]
        yk = (k * lax.rsqrt(_head_sum(k * k, ones_bd) * (1.0 / DH) + RMS_EPS)) * gk_ref[...]
        qr = (yq * cs + _swap16(yq) * sn) * QK_SCALE
        qp = yq * QK_SCALE
        kr = yk * cs + _swap16(yk) * sn
        vv = v_ref[...]
        for hh in range(H):
            sl = slice(hh * DH, (hh + 1) * DH)
            qr_ref[hh] = qr[:, sl].astype(bf16)
            qp_ref[hh] = qp[:, sl].astype(bf16)
            kr_ref[hh] = kr[:, sl].astype(bf16)
            vh_ref[hh] = vv[:, sl].astype(bf16)

    hm = SDS((H, S, DH), bf16)
    hspec = pl.BlockSpec((H, tm, DH), lambda i: (0, i, 0))
    fixed = lambda i: (0, 0)
    return _hbm_call(
        body, name="qk_prep", out_shape=(hm, hm, hm, hm), grid=(S // tm,),
        in_specs=[pl.BlockSpec((tm, 2 * DA), lambda i: (i, 0)), pl.BlockSpec((tm, DA), lambda i: (i, 2)),
                  pl.BlockSpec((1, DA), fixed), pl.BlockSpec((1, DA), fixed)] + _rope_specs(tm),
        out_specs=(hspec, hspec, hspec, hspec),
    )(p, p, gq, gk, *rope)


def _ctx_prep(pc, gk):
    def body(p_ref, gk_ref, kc_ref, vc_ref):
        ones_bd = _head_ones()
        k = p_ref[:, 0:DA]
        yk = (k * lax.rsqrt(_head_sum(k * k, ones_bd) * (1.0 / DH) + RMS_EPS)) * gk_ref[...]
        vv = p_ref[:, DA:2 * DA]
        for hh in range(H):
            sl = slice(hh * DH, (hh + 1) * DH)
            kc_ref[hh] = yk[:, sl].astype(bf16)
            vc_ref[hh] = vv[:, sl].astype(bf16)

    hm = SDS((H, L, DH), bf16)
    return _hbm_call(
        body, name="ctx_prep", out_shape=(hm, hm), in_specs=[VMEM_SPEC, VMEM_SPEC], out_specs=(VMEM_SPEC, VMEM_SPEC),
    )(pc, gk)


def _tile_pieces():
    out = []
    for (i0, u0) in TILE_GEOM:
        rows = []
        for j in range(2):
            i = i0 + j
            rs = _row_start(i)
            rows.append([(u0 + u - i + WIN_H - 1) if rs <= u0 + u < rs + WIN_H else None for u in range(KR)])
        out.append(rows)
    return out


def _bias_prep(rpb_rev_pad, after=None):
    pieces = _tile_pieces()

    def body(r_ref, after_ref, o_ref):
        rp = r_ref[...]
        xs = jnp.concatenate([pltpu.roll(jnp.broadcast_to(rp[dr:dr + 1, :], (GW, 128)), 128 - (WIN_W - 1), 1,
                                         stride=1, stride_axis=0) for dr in range(N_DR)], axis=0)
        row = lax.broadcasted_iota(i32, xs.shape, 0)
        lane = lax.broadcasted_iota(i32, xs.shape, 1)
        k = row & (GW - 1)
        c0 = jnp.clip(lane - WIN_W // 2, 0, GW - WIN_W)
        xs = jnp.where((k >= c0) & (k < c0 + WIN_W), xs, NEG)
        neg = jnp.full((GW, GW), NEG, f32)
        for t in range(NT):
            for j in range(2):
                for u in range(KR):
                    dr = pieces[t][j][u]
                    piece = neg if dr is None else xs[dr * GW:(dr + 1) * GW, 0:GW]
                    o_ref[t, u * GW:(u + 1) * GW, j * GW:(j + 1) * GW] = piece

    return _hbm_call(
        body, name="bias_prep", out_shape=SDS((H, NT, KB, QB), f32), grid=(H,),
        in_specs=[pl.BlockSpec((None, N_DR, 128), lambda h: (h, 0, 0)), ANY_SPEC],
        out_specs=pl.BlockSpec((None, NT, KB, QB), lambda h: (h, 0, 0, 0)),
    )(rpb_rev_pad, rpb_rev_pad if after is None else after)


def _bias_tiles(rpb2, after=None):
    return _bias_prep(jnp.pad(rpb2[:, :, ::-1], ((0, 0), (0, 0), (0, 128 - N_DC))), after)


def _block_geom(b):
    qs = b * QB
    ks = min(max(2 * b - 4, 0), ROWS - KR) * GW
    t = b if b < 2 else (b - (NQB - NT) if b > NQB - 3 else 2)
    return qs, ks, t


def _tt(a, b):
    return lax.dot_general(a, b, (((1,), (1,)), ((), ())), preferred_element_type=f32)


def _tn(a, b):
    return lax.dot_general(a, b, (((0,), (0,)), ((), ())), preferred_element_type=f32)


def _softmax_t(s_lat, s_ctx):
    m = jnp.maximum(jnp.max(s_lat, axis=0, keepdims=True), jnp.max(s_ctx, axis=0, keepdims=True))
    e_lat = jnp.exp(s_lat - m)
    e_ctx = jnp.exp(s_ctx - m)
    inv = 1.0 / (jnp.sum(e_lat, axis=0, keepdims=True) + jnp.sum(e_ctx, axis=0, keepdims=True))
    return e_lat * inv, e_ctx * inv


def _staged(n_blocks, stages):
    held = [dict() for _ in stages]
    for step in range(n_blocks + len(stages) - 1):
        for s, fn in enumerate(stages):
            b = step - s
            if 0 <= b < n_blocks:
                held[s][b] = fn(b) if s == 0 else fn(b, held[s - 1].pop(b))


def _attn_fwd(qr, qp, kr, vh, kc, vc, btt):
    def body(qr_ref, qp_ref, kr_ref, v_ref, kc_ref, vc_ref, bt_ref, o_ref):
        kcv, vcv = kc_ref[...], vc_ref[...]

        def scores(b):
            qs, ks, t = _block_geom(b)
            return (_tt(kr_ref[ks:ks + KB, :], qr_ref[qs:qs + QB, :]) + bt_ref[t], _tt(kcv, qp_ref[qs:qs + QB, :]))

        def probs(b, sc):
            p_lat, p_ctx = _softmax_t(*sc)
            return p_lat.astype(bf16), p_ctx.astype(bf16)

        def values(b, p):
            qs, ks, _ = _block_geom(b)
            o_ref[qs:qs + QB, :] = _tn(p[0], v_ref[ks:ks + KB, :]) + _tn(p[1], vcv)

        _staged(NQB, (scores, probs, values))

    sq = pl.BlockSpec((None, S, DH), lambda h: (h, 0, 0))
    sc = pl.BlockSpec((None, L, DH), lambda h: (h, 0, 0))
    return _hbm_call(
        body, name="attn_fwd", out_shape=SDS((H, S, DH), f32), grid=(H,),
        in_specs=[sq, sq, sq, sq, sc, sc, pl.BlockSpec((None, NT, KB, QB), lambda h: (h, 0, 0, 0))],
        out_specs=sq, compiler_params=_cp(48),
    )(qr, qp, kr, vh, kc, vc, btt)


def _shift_rows(v, down):
    n = v.shape[0]
    row = lax.broadcasted_iota(i32, v.shape, 0)
    if down:
        return jnp.where(row == 0, 0.0, pltpu.roll(v, 1, 0))
    return jnp.where(row == n - 1, 0.0, pltpu.roll(v, n - 1, 0))


def _conv_specs():
    col = lambda off: pl.BlockSpec((S, 128), lambda i, off=off: (0, off + i))
    return [col(16), col(20), col(24), col(28), pl.BlockSpec((3, 128), lambda i: (0, i)),
            pl.BlockSpec((1, 128), lambda i: (0, i))]


def _conv_fwd(p, conv_w, conv_b, after=None):
    def body(u_ref, bg_ref, cg_ref, zc_ref, w_ref, b_ref, after_ref, o_ref):
        cu = cg_ref[...] * u_ref[...]
        cv = b_ref[...] + _shift_rows(cu, True) * w_ref[0:1, :]
        cv = cv + cu * w_ref[1:2, :]
        cv = cv + _shift_rows(cu, False) * w_ref[2:3, :]
        o_ref[...] = ((bg_ref[...] * cv) * _silu(zc_ref[...])).astype(bf16)

    return _hbm_call(
        body, name="conv_fwd", out_shape=SDS((S, DC), bf16), grid=(DC // 128,),
        in_specs=_conv_specs() + [ANY_SPEC], out_specs=pl.BlockSpec((S, 128), lambda i: (0, i)),
        compiler_params=_cp(40),
    )(p, p, p, p, conv_w, conv_b, conv_b if after is None else after)


DP_Q, DP_K, DP_V, DP_ZA, DP_U, DP_BG, DP_CG, DP_ZC = range(8)


def _out_proj_loss(o, p, conv_g, w_out, xx, tgt, mrow, b_ada):
    tm = 256

    def body(o_ref, za_ref, c_ref, w_ref, x_ref, t_ref, m_ref, b_ref,
             dy_ref, dconv_ref, dp_ref, do_ref, gwo_ref, dgate_ref, loss_ref):
        k = pl.program_id(0)

        @pl.when(k == 0)
        def _():
            gwo_ref[...] = jnp.zeros_like(gwo_ref)
            dgate_ref[...] = jnp.zeros_like(dgate_ref)
            loss_ref[0, 0] = 0.0

        gate = m_ref[:, 2 * D:3 * D] + b_ref[:, 2 * D:3 * D]
        za = za_ref[...]
        sz = _silu(za)
        om = _merge_heads(o_ref)
        av, cv = (om * sz).astype(bf16), c_ref[...]
        mo = jnp.dot(av, w_ref[0:DA, :], preferred_element_type=f32)
        mo = mo + jnp.dot(cv, w_ref[DA:DA + DC, :], preferred_element_type=f32)
        y = x_ref[...] + gate * mo
        diff = y - t_ref[...]
        loss_ref[0, 0] += jnp.sum(diff * diff)
        dy = diff * (1.0 / D)
        dy_ref[...] = dy
        dgate_ref[...] += jnp.sum(dy * mo, axis=0, keepdims=True)
        dmo = (dy * gate).astype(bf16)
        dmix = _tt(dmo, w_ref[...])
        dattn = dmix[:, 0:DA]
        dconv_ref[...] = dmix[:, DA:DA + DC]
        a = dattn * sz
        for hh in range(H):
            do_ref[hh] = a[:, hh * DH:(hh + 1) * DH].astype(bf16)
        dp_ref[...] = ((dattn * _dsilu(za)) * om).astype(bf16)
        gwo_ref[0:DA, :] += _tn(av, dmo)
        gwo_ref[DA:DA + DC, :] += _tn(cv, dmo)

    row = lambda i: (i, 0)
    fixed = lambda i: (0, 0)
    hspec = pl.BlockSpec((H, tm, DH), lambda i: (0, i, 0))
    return _hbm_call(
        body, name="out_proj_loss",
        out_shape=(SDS((S, D), f32), SDS((S, DC), f32), SDS((8, S, DA), bf16), SDS((H, S, DH), bf16),
                   SDS((D, D), f32), SDS((1, D), f32), SDS((1, 1), f32)),
        grid=(S // tm,),
        in_specs=[hspec, pl.BlockSpec((tm, DA), lambda i: (i, 3)), pl.BlockSpec((tm, DC), row),
                  pl.BlockSpec((D, D), fixed), pl.BlockSpec((tm, D), row), pl.BlockSpec((tm, D), row),
                  pl.BlockSpec((1, 3 * D), fixed), pl.BlockSpec((1, 3 * D), fixed)],
        out_specs=(pl.BlockSpec((tm, D), row), pl.BlockSpec((tm, DC), row),
                   pl.BlockSpec((None, tm, DA), lambda i: (DP_ZA, i, 0)), hspec, pl.BlockSpec((D, D), fixed),
                   pl.BlockSpec((1, D), fixed), SMEM_SPEC),
        compiler_params=_cp(56, dimension_semantics=("arbitrary",)),
    )(o, p, conv_g, w_out, xx, tgt, mrow, b_ada)


def _conv_bwd(dconv, p, conv_w, conv_b, dp8, after=None):
    def body(d_ref, u_ref, bg_ref, cg_ref, zc_ref, w_ref, b_ref, dp_in_ref, after_ref, dp_ref, gw_ref, gb_ref):
        du_ref, dbg_ref, dcg_ref, dzc_ref = dp_ref.at[0], dp_ref.at[1], dp_ref.at[2], dp_ref.at[3]
        dconv = d_ref[...]
        u, bg, cg, zc = u_ref[...], bg_ref[...], cg_ref[...], zc_ref[...]
        w0, w1, w2 = w_ref[0:1, :], w_ref[1:2, :], w_ref[2:3, :]
        cu = cg * u
        cu_m, cu_p = _shift_rows(cu, True), _shift_rows(cu, False)
        cv = b_ref[...] + cu_m * w0
        cv = cv + cu * w1
        cv = cv + cu_p * w2
        sz = _silu(zc)
        dbg_ref[...] = ((dconv * sz) * cv).astype(bf16)
        dzc_ref[...] = ((dconv * (bg * cv)) * _dsilu(zc)).astype(bf16)
        dcv = (dconv * sz) * bg
        gb_ref[...] = jnp.sum(dcv, axis=0, keepdims=True)
        gw_ref[0:1, :] = jnp.sum(dcv * cu_m, axis=0, keepdims=True)
        gw_ref[1:2, :] = jnp.sum(dcv * cu, axis=0, keepdims=True)
        gw_ref[2:3, :] = jnp.sum(dcv * cu_p, axis=0, keepdims=True)
        gw_ref[3:8, :] = jnp.zeros((5, 128), f32)
        dcu = _shift_rows(dcv, False) * w0 + dcv * w1 + _shift_rows(dcv, True) * w2
        dcg_ref[...] = (dcu * u).astype(bf16)
        du_ref[...] = (dcu * cg).astype(bf16)

    return _hbm_call(
        body, name="conv_bwd", out_shape=(SDS((8, S, DC), bf16), SDS((8, DC), f32), SDS((1, DC), f32)),
        grid=(DC // 128,),
        in_specs=[pl.BlockSpec((S, 128), lambda i: (0, i))] + _conv_specs() + [ANY_SPEC, ANY_SPEC],
        out_specs=(pl.BlockSpec((4, S, 128), lambda i: (DP_U // 4, 0, i)), pl.BlockSpec((8, 128), lambda i: (0, i)),
                   pl.BlockSpec((1, 128), lambda i: (0, i))),
        input_output_aliases={7: 0}, compiler_params=_cp(48),
    )(dconv, p, p, p, p, conv_w, conv_b, dp8, conv_b if after is None else after)


def _attn_bwd(qr, qp, kr, vh, kc, vc, btt, do, after=None):
    def body(qr_ref, qp_ref, kr_ref, v_ref, kc_ref, vc_ref, bt_ref, do_ref, after_ref,
             dqr_ref, dqp_ref, dkr_ref, dv_ref, dkc_ref, dvc_ref, dbt_ref):
        kcv, vcv = kc_ref[...], vc_ref[...]
        dkr_ref[...] = jnp.zeros_like(dkr_ref)
        dv_ref[...] = jnp.zeros_like(dv_ref)
        dbt_ref[...] = jnp.zeros_like(dbt_ref)
        ctx_acc = {}

        def products(b):
            qs, ks, t = _block_geom(b)
            dob = do_ref[qs:qs + QB, :]
            s_lat = _tt(kr_ref[ks:ks + KB, :], qr_ref[qs:qs + QB, :]) + bt_ref[t]
            s_ctx = _tt(kcv, qp_ref[qs:qs + QB, :])
            return s_lat, s_ctx, _tt(v_ref[ks:ks + KB, :], dob), _tt(vcv, dob)

        def score_grads(b, x):
            s_lat, s_ctx, dp_lat, dp_ctx = x
            p_lat, p_ctx = _softmax_t(s_lat, s_ctx)
            delta = jnp.sum(p_lat * dp_lat, axis=0, keepdims=True) + jnp.sum(p_ctx * dp_ctx, axis=0, keepdims=True)
            ds_lat = p_lat * (dp_lat - delta)
            ds_ctx = p_ctx * (dp_ctx - delta)
            return ds_lat, ds_lat.astype(bf16), ds_ctx.astype(bf16), p_lat.astype(bf16), p_ctx.astype(bf16)

        def operand_grads(b, y):
            qs, ks, t = _block_geom(b)
            ds_lat, dsb_lat, dsb_ctx, pb_lat, pb_ctx = y
            qrb, qpb, dob = qr_ref[qs:qs + QB, :], qp_ref[qs:qs + QB, :], do_ref[qs:qs + QB, :]
            dbt_ref[t] += ds_lat
            dqr_ref[qs:qs + QB, :] = _tn(dsb_lat, kr_ref[ks:ks + KB, :])
            dqp_ref[qs:qs + QB, :] = _tn(dsb_ctx, kcv)
            dkr_ref[ks:ks + KB, :] += jnp.dot(dsb_lat, qrb, preferred_element_type=f32)
            dv_ref[ks:ks + KB, :] += jnp.dot(pb_lat, dob, preferred_element_type=f32)
            dkc = jnp.dot(dsb_ctx, qpb, preferred_element_type=f32)
            dvc = jnp.dot(pb_ctx, dob, preferred_element_type=f32)
            ctx_acc["k"] = dkc if b == 0 else ctx_acc["k"] + dkc
            ctx_acc["v"] = dvc if b == 0 else ctx_acc["v"] + dvc

        _staged(NQB, (products, score_grads, operand_grads))
        dkc_ref[...] = ctx_acc["k"]
        dvc_ref[...] = ctx_acc["v"]

    sq = pl.BlockSpec((None, S, DH), lambda h: (h, 0, 0))
    sc = pl.BlockSpec((None, L, DH), lambda h: (h, 0, 0))
    sb = pl.BlockSpec((None, NT, KB, QB), lambda h: (h, 0, 0, 0))
    big, ctxs = SDS((H, S, DH), f32), SDS((H, L, DH), f32)
    return _hbm_call(
        body, name="attn_bwd", out_shape=(big, big, big, big, ctxs, ctxs, SDS((H, NT, KB, QB), f32)), grid=(H,),
        in_specs=[sq, sq, sq, sq, sc, sc, sb, sq, ANY_SPEC], out_specs=(sq, sq, sq, sq, sc, sc, sb),
        compiler_params=_cp(56),
    )(qr, qp, kr, vh, kc, vc, btt, do, do if after is None else after)


def _bias_bwd(dbtt, after=None):
    pieces = _tile_pieces()

    def body(d_ref, after_ref, o_ref, scr):
        scr[...] = jnp.zeros_like(scr)
        acc = [None] * N_DR
        for t in range(NT):
            for j in range(2):
                for u in range(KR):
                    dr = pieces[t][j][u]
                    if dr is None:
                        continue
                    piece = d_ref[t, u * GW:(u + 1) * GW, j * GW:(j + 1) * GW]
                    acc[dr] = piece if acc[dr] is None else acc[dr] + piece
        a = lax.broadcasted_iota(i32, (GW, GW), 0)
        b = lax.broadcasted_iota(i32, (GW, GW), 1)
        flip = (a + b == GW - 1).astype(f32)
        for dr in range(N_DR):
            scr[dr * GW:(dr + 1) * GW, 0:GW] = jnp.dot(acc[dr], flip, precision=HIGHEST, preferred_element_type=f32)
        xs = jnp.concatenate([pltpu.roll(scr[dr * GW:(dr + 1) * GW, :], 128 + (WIN_W - 1) - (GW - 1), 1,
                                         stride=1, stride_axis=0) for dr in range(N_DR)], axis=0)
        tot = jnp.sum(xs.reshape(N_DR, GW, 128), axis=1)
        lane = lax.broadcasted_iota(i32, tot.shape, 1)
        o_ref[...] = jnp.where(lane < N_DC, tot, 0.0)

    return _hbm_call(
        body, name="bias_bwd", out_shape=SDS((H, N_DR, 128), f32), grid=(H,),
        in_specs=[pl.BlockSpec((None, NT, KB, QB), lambda h: (h, 0, 0, 0)), ANY_SPEC],
        out_specs=pl.BlockSpec((None, N_DR, 128), lambda h: (h, 0, 0)),
        scratch_shapes=[pltpu.VMEM((N_DR * GW, 128), f32)],
    )(dbtt, dbtt if after is None else after)


def _merge_heads(ref):
    return jnp.concatenate([ref[hh] for hh in range(H)], axis=1)


def _head_norm_bwd(xraw, gain, dy, ones_bd):
    r = lax.rsqrt(_head_sum(xraw * xraw, ones_bd) * (1.0 / DH) + RMS_EPS)
    xh = xraw * r
    gdy = dy * gain
    dx = r * (gdy - xh * (_head_sum(xh * gdy, ones_bd) * (1.0 / DH)))
    return dx, jnp.sum(dy * xh, axis=0, keepdims=True)


def _qk_bwd(dqr, dqp, dkr, dvh, p, gq, gk, rope, dp8):
    tm = 256

    def body(dqr_ref, dqp_ref, dkr_ref, dv_ref, qk_ref, gq_ref, gk_ref, cc_ref, cr_ref, sc_ref, sr_ref, dp_in_ref,
             dp_ref, ggq_ref, ggk_ref):
        dq_ref, dk_ref, dvo_ref = dp_ref.at[DP_Q], dp_ref.at[DP_K], dp_ref.at[DP_V]

        @pl.when(pl.program_id(0) == 0)
        def _():
            ggq_ref[...] = jnp.zeros_like(ggq_ref)
            ggk_ref[...] = jnp.zeros_like(ggk_ref)

        ones_bd = _head_ones()
        cs, sn = _rope_block(cc_ref, cr_ref, tm), _rope_block(sc_ref, sr_ref, tm)
        a = _merge_heads(dqr_ref)
        dyq = ((a * cs - _swap16(a) * sn) + _merge_heads(dqp_ref)) * QK_SCALE
        bk = _merge_heads(dkr_ref)
        dyk = bk * cs - _swap16(bk) * sn
        dq, gq_part = _head_norm_bwd(qk_ref[:, 0:DA], gq_ref[...], dyq, ones_bd)
        dk, gk_part = _head_norm_bwd(qk_ref[:, DA:2 * DA], gk_ref[...], dyk, ones_bd)
        dq_ref[...] = dq.astype(bf16)
        dk_ref[...] = dk.astype(bf16)
        dvo_ref[...] = _merge_heads(dv_ref).astype(bf16)
        ggq_ref[...] += gq_part
        ggk_ref[...] += gk_part

    hspec = pl.BlockSpec((H, tm, DH), lambda i: (0, i, 0))
    fixed = pl.BlockSpec((1, DA), lambda i: (0, 0))
    return _hbm_call(
        body, name="qk_bwd", out_shape=(SDS((8, S, DA), bf16), SDS((1, DA), f32), SDS((1, DA), f32)), grid=(S // tm,),
        in_specs=[hspec, hspec, hspec, hspec, pl.BlockSpec((tm, 2 * DA), lambda i: (i, 0)), fixed, fixed]
        + _rope_specs(tm) + [ANY_SPEC],
        out_specs=(pl.BlockSpec((3, tm, DA), lambda i: (0, i, 0)), fixed, fixed), input_output_aliases={11: 0},
        compiler_params=_cp(40, dimension_semantics=("arbitrary",)),
    )(dqr, dqp, dkr, dvh, p, gq, gk, *rope, dp8)


def _ctx_bwd(dkc, dvc, pc, gk):
    def body(dkc_ref, dvc_ref, p_ref, gk_ref, dk_ref, dv_ref, ggk_ref):
        ones_bd = _head_ones()
        dk, gk_part = _head_norm_bwd(p_ref[:, 0:DA], gk_ref[...], _merge_heads(dkc_ref), ones_bd)
        dk_ref[...] = dk.astype(bf16)
        dv_ref[...] = _merge_heads(dvc_ref).astype(bf16)
        ggk_ref[...] = gk_part

    piece = SDS((L, DA), bf16)
    return _hbm_call(
        body, name="ctx_bwd", out_shape=(piece, piece, SDS((1, DA), f32)), in_specs=[VMEM_SPEC] * 4,
        out_specs=(VMEM_SPEC,) * 3,
    )(dkc, dvc, pc, gk)


def _grad_w_in(h, dp8, hc, dkc_raw, dvc_m, half, name, after=None):
    def body(half_ref, h_ref, p_ref, hc_ref, dk_ref, dv_ref, after_ref, g_ref):
        j = pl.program_id(0)
        hv = h_ref[...]
        g_ref[:, 0:DA] = _tn(hv, p_ref[0])
        g_ref[:, DA:2 * DA] = _tn(hv, p_ref[1])

        @pl.when(j == 0)
        def _():
            g_ref[:, DA:2 * DA] += _tn(hc_ref[...], dk_ref[...])

        @pl.when(j == 1)
        def _():
            g_ref[:, 0:DA] += _tn(hc_ref[...], dv_ref[...])

    fixed = lambda j, s: (0, 0)
    hd = D // 2
    grid_spec = pltpu.PrefetchScalarGridSpec(
        num_scalar_prefetch=1, grid=(4,),
        in_specs=[pl.BlockSpec((S, hd), lambda j, s: (0, s[0])), pl.BlockSpec((2, S, DA), lambda j, s: (j, 0, 0)),
                  pl.BlockSpec((L, hd), lambda j, s: (0, s[0])), pl.BlockSpec((L, DA), fixed),
                  pl.BlockSpec((L, DA), fixed), ANY_SPEC],
        out_specs=pl.BlockSpec((None, hd, D), lambda j, s: (j, 0, 0)))
    return _hbm_call(
        body, name=name, out_shape=SDS((4, hd, D), f32), grid_spec=grid_spec, compiler_params=_cp(40),
    )(half, h, dp8, hc, dkc_raw, dvc_m, half if after is None else after)


def _grad_w_in_pair_sum(h, dp8, hc, dkc_raw, dvc_m, half, sent, landed, ssem, rsem):
    def body(half_ref, h_ref, p_ref, hc_ref, dk_ref, dv_ref, sent_ref, land_ref, ssem_ref, rsem_ref,
             t32_ref, tb_ref, buf, lsem):
        j = pl.program_id(0)
        hv = h_ref[...]
        t32_ref[:, 0:DA] = _tn(hv, p_ref[0])
        t32_ref[:, DA:2 * DA] = _tn(hv, p_ref[1])

        @pl.when(j == 0)
        def _():
            t32_ref[:, DA:2 * DA] += _tn(hc_ref[...], dk_ref[...])

        @pl.when(j == 1)
        def _():
            t32_ref[:, 0:DA] += _tn(hc_ref[...], dv_ref[...])

        x, y, c = _my_pos()
        arrival = pltpu.make_async_remote_copy(src_ref=sent_ref.at[j], dst_ref=land_ref.at[j], send_sem=ssem_ref.at[j],
                                               recv_sem=rsem_ref.at[j], device_id=(x, y, 1 - c), device_id_type=MESH)
        arrival.wait_recv()
        arrival.wait_send()
        load = pltpu.make_async_copy(land_ref.at[j], buf, lsem)
        load.start()
        load.wait()
        t = t32_ref[...] + buf[...]
        t32_ref[...] = t
        tb_ref[...] = t.astype(bf16)

    fixed = lambda j, s: (0, 0)
    hd = D // 2
    out_spec = pl.BlockSpec((None, hd, D), lambda j, s: (j, 0, 0))
    grid_spec = pltpu.PrefetchScalarGridSpec(
        num_scalar_prefetch=1, grid=(4,),
        in_specs=[pl.BlockSpec((S, hd), lambda j, s: (0, s[0])), pl.BlockSpec((2, S, DA), lambda j, s: (j, 0, 0)),
                  pl.BlockSpec((L, hd), lambda j, s: (0, s[0])), pl.BlockSpec((L, DA), fixed),
                  pl.BlockSpec((L, DA), fixed), HBM_SPEC, HBM_SPEC, SEM_SPEC, SEM_SPEC],
        out_specs=(out_spec, out_spec), scratch_shapes=[pltpu.VMEM((hd, D), f32), pltpu.SemaphoreType.DMA])
    return _hbm_call(
        body, name="grad_w_in_pair_sum", out_shape=(SDS((4, hd, D), f32), SDS((4, hd, D), bf16)), grid_spec=grid_spec,
        compiler_params=_cp(40, has_side_effects=DATAFLOW),
    )(half, h, dp8, hc, dkc_raw, dvc_m, sent, landed, ssem, rsem)


def _norm_mod_bwd(x, dh, g, scale):
    r = lax.rsqrt(jnp.mean(x * x, axis=-1, keepdims=True) + RMS_EPS)
    xh = x * r
    y = xh * g
    dshift = jnp.sum(dh, axis=0, keepdims=True)
    dscale = jnp.sum(dh * y, axis=0, keepdims=True)
    dyn = dh * (1.0 + scale)
    dg = jnp.sum(dyn * xh, axis=0, keepdims=True)
    gdy = dyn * g
    dx = r * (gdy - xh * jnp.mean(xh * gdy, axis=-1, keepdims=True))
    return dx, dshift, dscale, dg


def _dh_grad_x(dp8, w4, xx, dy, norm_g, mrow, b_ada, after=None):
    tm = 256

    def body(p_ref, w_ref, x_ref, dy_ref, g_ref, m_ref, b_ref, after_ref, gx_ref, dsh_ref, dsc_ref, dg_ref):
        @pl.when(pl.program_id(0) == 0)
        def _():
            dsh_ref[...] = jnp.zeros_like(dsh_ref)
            dsc_ref[...] = jnp.zeros_like(dsc_ref)
            dg_ref[...] = jnp.zeros_like(dg_ref)

        dh = None
        for j in range(4):
            for half in range(2):
                term = _tt(p_ref[2 * j + half], w_ref[j, :, half * DA:(half + 1) * DA])
                dh = term if dh is None else dh + term
        scale = m_ref[:, D:2 * D] + b_ref[:, D:2 * D]
        dx, dshift, dscale, dg = _norm_mod_bwd(x_ref[...], dh, g_ref[...], scale)
        gx_ref[...] = dy_ref[...] + dx
        dsh_ref[...] += dshift
        dsc_ref[...] += dscale
        dg_ref[...] += dg

    row = lambda i: (i, 0)
    fixed = lambda i: (0, 0)
    vec = SDS((1, D), f32)
    return _hbm_call(
        body, name="dh_grad_x", out_shape=(SDS((S, D), f32), vec, vec, vec), grid=(S // tm,),
        in_specs=[pl.BlockSpec((8, tm, DA), lambda i: (0, i, 0)), pl.BlockSpec((4, D, D), lambda i: (0, 0, 0)),
                  pl.BlockSpec((tm, D), row), pl.BlockSpec((tm, D), row), pl.BlockSpec((1, D), fixed),
                  pl.BlockSpec((1, 3 * D), fixed), pl.BlockSpec((1, 3 * D), fixed), ANY_SPEC],
        out_specs=(pl.BlockSpec((tm, D), row), pl.BlockSpec((1, D), fixed), pl.BlockSpec((1, D), fixed),
                   pl.BlockSpec((1, D), fixed)),
        compiler_params=_cp(56, dimension_semantics=("arbitrary",)),
    )(dp8, w4, xx, dy, norm_g, mrow, b_ada, b_ada if after is None else after)


def _dhc_sums(dkc_raw, dvc_m, w4, ctx2, norm_g, mrow_c, b_ada, after=None):
    def body(dk_ref, dv_ref, w0_ref, w1_ref, x_ref, g_ref, m_ref, b_ref, after_ref, dsh_ref, dsc_ref, dg_ref):
        dh = _tt(dk_ref[...], w0_ref[:, DA:2 * DA]) + _tt(dv_ref[...], w1_ref[:, 0:DA])
        scale = m_ref[:, D:2 * D] + b_ref[:, D:2 * D]
        _, dshift, dscale, dg = _norm_mod_bwd(x_ref[...], dh, g_ref[...], scale)
        dsh_ref[...] = dshift
        dsc_ref[...] = dscale
        dg_ref[...] = dg

    fixed = lambda i: (0, 0)
    vec = SDS((1, D), f32)
    vspec = pl.BlockSpec((1, D), fixed)
    return _hbm_call(
        body, name="dhc_sums", out_shape=(vec, vec, vec), grid=(1,),
        in_specs=[pl.BlockSpec((L, DA), fixed), pl.BlockSpec((L, DA), fixed),
                  pl.BlockSpec((None, D, D), lambda i: (0, 0, 0)), pl.BlockSpec((None, D, D), lambda i: (1, 0, 0)),
                  pl.BlockSpec((L, D), fixed), vspec, pl.BlockSpec((1, 3 * D), fixed), pl.BlockSpec((1, 3 * D), fixed),
                  ANY_SPEC],
        out_specs=(vspec, vspec, vspec), compiler_params=_cp(32),
    )(dkc_raw, dvc_m, w4, w4, ctx2, norm_g, mrow_c, b_ada, b_ada if after is None else after)


def _rope_tables():
    nf = DH // 4
    inv = np.float32(ROPE_THETA) ** (-np.arange(nf, dtype=np.float32) / np.float32(nf))
    ang_c = np.arange(GW, dtype=np.float32)[:, None] * inv
    ang_r = np.arange(ROWS, dtype=np.float32)[:, None] * inv
    zc, zr = np.zeros((GW, 2 * nf), np.float32), np.zeros((ROWS, 2 * nf), np.float32)
    ct_cos = np.tile(np.concatenate([zc, np.cos(ang_c), np.cos(ang_c)], axis=1), (1, H))
    ct_sin = np.tile(np.concatenate([zc, -np.sin(ang_c), np.sin(ang_c)], axis=1), (1, H))
    rt_cos = np.tile(np.concatenate([np.cos(ang_r), np.cos(ang_r), zr], axis=1), (1, H))
    rt_sin = np.tile(np.concatenate([-np.sin(ang_r), np.sin(ang_r), zr], axis=1), (1, H))
    rep8 = lambda t: np.ascontiguousarray(np.broadcast_to(t[:, None, :], (ROWS, 8, DA))).reshape(ROWS * 8, DA)
    return tuple(jnp.asarray(t, f32) for t in (ct_cos, rep8(rt_cos), ct_sin, rep8(rt_sin)))


def _local_step(xx, ctx2, tgt, mrow, mrow_c, b_ada, norm_g, weights, q_norm_g, k_norm_g, rpb2, conv_w_full, conv_b,
                hooks=None):
    hooks = hooks or {}
    gq = jnp.tile(q_norm_g, (1, H))
    gk = jnp.tile(k_norm_g, (1, H))
    rope = _rope_tables()

    h = _prenorm(xx, norm_g, mrow, b_ada, 256, "prenorm_x", after=weights.get("started"))
    hc = _prenorm(ctx2, norm_g, mrow_c, b_ada, L, "prenorm_ctx")
    jv = weights["jvec"]
    p = _in_proj_own(h, weights["own"], jv)
    btb = _bias_tiles(rpb2, after=p)
    w4, started = weights["near"](btb)
    p = _in_proj_block(h, w4, p, weights["first"], "in_proj_near", after=started)
    w4 = weights["near2"](w4, p)
    p = _in_proj_block(h, w4, p, weights["second"], "in_proj_near2")
    w4, started = weights["far"](w4, p)
    p = _in_proj_block(h, w4, p, jv ^ 3, "in_proj_far", after=started)
    pc = _ctx_proj(hc, w4)
    qr, qp, kr, vh = _qk_prep(p, gq, gk, rope)
    kc, vc = _ctx_prep(pc, gk)
    o = _attn_fwd(qr, qp, kr, vh, kc, vc, btb)
    started = weights["out_arrived"](o) if "out_arrived" in weights else None
    conv_g = _conv_fwd(p, conv_w_full, conv_b, after=started)
    w_out_full = weights["out"](conv_g)
    dy, dconv, dp8, do, g_w_out, dgate, loss_sum = _out_proj_loss(o, p, conv_g, w_out_full, xx, tgt, mrow, b_ada)
    started = hooks["g_w_out"](g_w_out) if "g_w_out" in hooks else None
    dp8, g_conv_w, g_conv_b = _conv_bwd(dconv, p, conv_w_full, conv_b, dp8, after=started)
    started = hooks["after_conv"](dp8) if "after_conv" in hooks else None
    dqr, dqp, dkr, dvh, dkc, dvc, dbtb = _attn_bwd(qr, qp, kr, vh, kc, vc, btb, do, after=started)
    dp8, g_gq, g_gk = _qk_bwd(dqr, dqp, dkr, dvh, p, gq, gk, rope, dp8)
    dkc_raw, dvc_m, g_gk_c = _ctx_bwd(dkc, dvc, pc, gk)
    first = hooks.get("first_half", jnp.zeros((1,), i32))
    g_first = _grad_w_in(h, dp8, hc, dkc_raw, dvc_m, first, "grad_w_in_first")
    started = hooks["g_w_in_first"](g_first) if "g_w_in_first" in hooks else None
    if "w_in_pair_sum" in hooks:
        g_second = None
        started = hooks["w_in_pair_sum"](functools.partial(_grad_w_in_pair_sum, h, dp8, hc, dkc_raw, dvc_m, 1 - first))
    else:
        g_second = _grad_w_in(h, dp8, hc, dkc_raw, dvc_m, 1 - first, "grad_w_in_second", after=started)
    dshift_c, dscale_c, dng_c = _dhc_sums(dkc_raw, dvc_m, w4, ctx2, norm_g, mrow_c, b_ada, after=started)
    g_rpb = _bias_bwd(dbtb, after=dshift_c)
    grad_x, dshift, dscale, dng = _dh_grad_x(dp8, w4, xx, dy, norm_g, mrow, b_ada, after=g_rpb)
    return dict(loss_sum=loss_sum, grad_x=grad_x, g_w_in=(g_first, g_second), g_w_out=g_w_out, g_conv_w=g_conv_w,
                g_conv_b=g_conv_b, g_rpb=g_rpb, g_gq=g_gq, g_gk=g_gk, g_gk_c=g_gk_c, dshift=dshift, dscale=dscale,
                dgate=dgate, dng=dng, dshift_c=dshift_c, dscale_c=dscale_c, dng_c=dng_c)


def _pair_sum_w_in(g, r):
    tr = 128

    def body(g_ref, r_ref, t32_ref, tb_ref):
        t = g_ref[...] + r_ref[...]
        t32_ref[...] = t
        tb_ref[...] = t.astype(bf16)

    half = D // 2
    spec = pl.BlockSpec((4, tr, D), lambda i: (0, i, 0))
    return _hbm_call(body, name="pair_sum_w_in", out_shape=(SDS((4, half, D), f32), SDS((4, half, D), bf16)),
                     grid=(half // tr,), in_specs=[spec, spec], out_specs=(spec, spec), compiler_params=_cp(40))(g, r)


def _pair_sum_w_out(g, r, cvec):
    hr = D // 8

    def body(c_ref, g0, g1, g2, g3, r_ref, t32_ref, tb_ref):
        for q, g_ref in enumerate((g0, g1, g2, g3)):
            t = g_ref[...] + r_ref[q]
            t32_ref[q] = t
            tb_ref[q] = t.astype(bf16)

    gspecs = [pl.BlockSpec((hr, D), lambda i, c, q=q: (2 * q + c[0], 0)) for q in range(4)]
    full = pl.BlockSpec((4, hr, D), lambda i, c: (0, 0, 0))
    grid_spec = pltpu.PrefetchScalarGridSpec(num_scalar_prefetch=1, grid=(1,), in_specs=gspecs + [full],
                                             out_specs=(full, full))
    return _hbm_call(body, name="pair_sum_w_out", out_shape=(SDS((4, hr, D), f32), SDS((4, hr, D), bf16)),
                          grid_spec=grid_spec)(cvec, g, g, g, g, r)


def _chip_sum(t32, r2, jvec, name):
    rows = t32.shape[1]
    tr = min(rows, 128)

    def body(j_ref, t_ref, r_ref, u_ref):
        u_ref[...] = ((t_ref[...] + r_ref[0].astype(f32)) + r_ref[1].astype(f32)) + r_ref[2].astype(f32)

    grid_spec = pltpu.PrefetchScalarGridSpec(
        num_scalar_prefetch=1, grid=(rows // tr,),
        in_specs=[pl.BlockSpec((None, tr, D), lambda i, j: (j[0], i, 0)),
                  pl.BlockSpec((3, tr, D), lambda i, j: (0, i, 0))],
        out_specs=pl.BlockSpec((tr, D), lambda i, j: (i, 0)))
    return _hbm_call(body, name=name, out_shape=SDS((rows, D), f32), grid_spec=grid_spec)(jvec, t32, r2)


_PK = {}
_off = 0
for _name, _rows in (("dm", 24), ("dmc", 24), ("dng", 8), ("dng_c", 8), ("gq", 8), ("gk", 8), ("gk_c", 8),
                     ("rpb", H * N_DR), ("conv_b", 8), ("conv_w", 16), ("loss", 8)):
    _PK[_name] = (_off, _off + _rows)
    _off += _rows
PK_ROWS = _off
RS_B_ADA, RS_NORM_G, RS_GQ, RS_GK, RS_RPB, RS_CONV_B, RS_CONV_W, RS_DMC, RS_LOSS, RS_ROWS = (
    0, 24, 32, 40, 48, 168, 176, 192, 216, 224)


def _small_reduce(gathered):
    def body(g_ref, o_ref, dm_ref):
        a0 = _PK["dm"][0]
        dm_ref[...] = jnp.zeros_like(dm_ref)
        for b in range(8):
            for i in range(24):
                dm_ref[b:b + 1, 128 * i:128 * (i + 1)] = g_ref[b, a0 + i:a0 + i + 1, :]
        tot = g_ref[0]
        for b in range(1, 8):
            tot = tot + g_ref[b]

        def rows(name):
            a, z = _PK[name]
            return tot[a:z]

        o_ref[RS_B_ADA:RS_B_ADA + 24] = rows("dm") + rows("dmc")
        o_ref[RS_NORM_G:RS_NORM_G + 8] = rows("dng") + rows("dng_c")
        gq = jnp.broadcast_to(jnp.sum(rows("gq"), axis=0, keepdims=True), (8, 128))
        gk = jnp.broadcast_to(jnp.sum(rows("gk") + rows("gk_c"), axis=0, keepdims=True), (8, 128))
        o_ref[RS_GQ:RS_GQ + 8] = gq + pltpu.roll(gq, DH, 1)
        o_ref[RS_GK:RS_GK + 8] = gk + pltpu.roll(gk, DH, 1)
        o_ref[RS_RPB:RS_RPB + H * N_DR] = rows("rpb")
        o_ref[RS_CONV_B:RS_CONV_B + 8] = rows("conv_b")
        o_ref[RS_CONV_W:RS_CONV_W + 16] = rows("conv_w")
        dmc = rows("dmc")
        o_ref[RS_DMC:RS_DMC + 24] = dmc
        o_ref[RS_LOSS:RS_LOSS + 8] = rows("loss")
        for i in range(24):
            dm_ref[8:9, 128 * i:128 * (i + 1)] = dmc[i:i + 1]

    return _hbm_call(body, name="small_reduce", out_shape=(SDS((RS_ROWS, 128), f32), SDS((16, 3 * D), f32)),
                     in_specs=[VMEM_SPEC], out_specs=(VMEM_SPEC, VMEM_SPEC))(gathered)


def _w_ada_grad(sc16, dm16, w_ada_shard, jvec):
    ncol = w_ada_shard.shape[1]

    def body(j_ref, sc_ref, dm_ref, w_ref, g_ref, part_ref):
        dm = dm_ref[...]
        g_ref[...] = lax.dot_general(sc_ref[...], dm, (((0,), (0,)), ((), ())), precision=HIGHEST,
                                     preferred_element_type=f32)
        part_ref[...] = lax.dot_general(dm[8:16], w_ref[...], (((1,), (1,)), ((), ())), precision=HIGHEST,
                                        preferred_element_type=f32)

    fixed = lambda i, j: (0, 0)
    grid_spec = pltpu.PrefetchScalarGridSpec(
        num_scalar_prefetch=1, grid=(1,),
        in_specs=[pl.BlockSpec((16, D), fixed), pl.BlockSpec((16, ncol), lambda i, j: (0, j[0])),
                  pl.BlockSpec((D, ncol), fixed)],
        out_specs=(pl.BlockSpec((D, ncol), fixed), pl.BlockSpec((8, D), fixed)))
    return _pallas_call(body, name="w_ada_grad", out_shape=(SDS((D, ncol), f32), SDS((8, D), f32)),
                        grid_spec=grid_spec, compiler_params=_cp(40))(jvec, sc16, dm16, w_ada_shard)


def _c_ctx_grad(parts4, c_ctx):
    def body(p_ref, c_ref, o_ref):
        tot = ((p_ref[0] + p_ref[1]) + p_ref[2]) + p_ref[3]
        o_ref[...] = tot[0:1] * _dsilu(c_ref[...].reshape(1, D))

    return _pallas_call(body, name="c_ctx_grad", out_shape=SDS((1, D), f32), in_specs=[VMEM_SPEC, VMEM_SPEC],
                        out_specs=VMEM_SPEC)(parts4, c_ctx)


def _adamw(w, g, m, v, name, after=None):
    rows, cols = w.shape
    tr = 256 if rows % 256 == 0 else rows

    def body(w_ref, g_ref, m_ref, v_ref, after_ref, d_ref, m2_ref, v2_ref):
        gv = g_ref[...]
        m2 = ADAM_B1 * m_ref[...] + (1.0 - ADAM_B1) * gv
        v2 = ADAM_B2 * v_ref[...] + (1.0 - ADAM_B2) * jnp.square(gv)
        m_hat = m2 / (1.0 - ADAM_B1 ** ADAM_STEP)
        v_hat = v2 / (1.0 - ADAM_B2 ** ADAM_STEP)
        d_ref[...] = -ADAM_LR * (m_hat / (jnp.sqrt(v_hat) + ADAM_EPS) + ADAM_WD * w_ref[...])
        m2_ref[...] = m2
        v2_ref[...] = v2

    spec = pl.BlockSpec((tr, cols), lambda i: (i, 0))
    shp = SDS((rows, cols), f32)
    return _hbm_call(body, name=name, out_shape=(shp, shp, shp), grid=(rows // tr,), in_specs=[spec] * 4 + [ANY_SPEC],
                     out_specs=(spec, spec, spec))(w, g, m, v, g if after is None else after)


def _adamw_halves(w, g_mine, g_other, m, v, cvec, name, after=None):
    rows, cols = w.shape
    half = rows // 2
    tr = min(256, half)
    per_half = half // tr

    def body(c_ref, w_ref, ga_ref, gb_ref, m_ref, v_ref, after_ref, g_ref, d_ref, m2_ref, v2_ref):
        in_my_half = (pl.program_id(0) // per_half) == c_ref[0]
        gv = jnp.where(in_my_half, ga_ref[...], gb_ref[...])
        g_ref[...] = gv
        m2 = ADAM_B1 * m_ref[...] + (1.0 - ADAM_B1) * gv
        v2 = ADAM_B2 * v_ref[...] + (1.0 - ADAM_B2) * jnp.square(gv)
        m_hat = m2 / (1.0 - ADAM_B1 ** ADAM_STEP)
        v_hat = v2 / (1.0 - ADAM_B2 ** ADAM_STEP)
        d_ref[...] = -ADAM_LR * (m_hat / (jnp.sqrt(v_hat) + ADAM_EPS) + ADAM_WD * w_ref[...])
        m2_ref[...] = m2
        v2_ref[...] = v2

    full = pl.BlockSpec((tr, cols), lambda i, c: (i, 0))
    part = pl.BlockSpec((tr, cols), lambda i, c: (i % per_half, 0))
    shp = SDS((rows, cols), f32)
    grid_spec = pltpu.PrefetchScalarGridSpec(num_scalar_prefetch=1, grid=(rows // tr,),
                                             in_specs=[full, part, part, full, full, ANY_SPEC], out_specs=(full,) * 4)
    return _hbm_call(body, name=name, out_shape=(shp,) * 4, grid_spec=grid_spec)(
        cvec, w, g_mine, g_other, m, v, cvec if after is None else after)


def _adam_math(w, g, m, v):
    m2 = ADAM_B1 * m + (1.0 - ADAM_B1) * g
    v2 = ADAM_B2 * v + (1.0 - ADAM_B2) * jnp.square(g)
    m_hat = m2 / (1.0 - ADAM_B1 ** ADAM_STEP)
    v_hat = v2 / (1.0 - ADAM_B2 ** ADAM_STEP)
    return -ADAM_LR * (m_hat / (jnp.sqrt(v_hat) + ADAM_EPS) + ADAM_WD * w), m2, v2


def _adamw_small(red, g_c_ctx, jvec, ws, ms, vs):
    n = len(ws)

    def body(*refs):
        red_ref, gc_ref, j_ref = refs[:3]
        w_refs, m_refs, v_refs = refs[3:3 + n], refs[3 + n:3 + 2 * n], refs[3 + 2 * n:3 + 3 * n]
        outs = refs[3 + 3 * n:]
        g_out, d_out, m_out, v_out = outs[:n], outs[n:2 * n], outs[2 * n:3 * n], outs[3 * n:]
        chip = j_ref[0]
        lanes = lambda i: (slice(None), slice(128 * i, 128 * (i + 1)))
        row = lambda r0, i: (lambda: red_ref[r0 + i:r0 + i + 1, :])
        whole = (slice(None), slice(None))
        chunks = [
            [((slice(None),), lambda: gc_ref[...].reshape(D))],
            [(lanes(i), row(RS_B_ADA, i)) for i in range(3 * D // 128)],
            [(lanes(i), row(RS_NORM_G, i)) for i in range(D // 128)],
            [(whole, lambda: red_ref[RS_GQ:RS_GQ + 1, 0:DH])],
            [(whole, lambda: red_ref[RS_GK:RS_GK + 1, 0:DH])],
            [((dr,), (lambda dr=dr: red_ref[pl.ds(RS_RPB + dr, H, stride=N_DR), 0:N_DC])) for dr in range(N_DR)],
            [((r,), (lambda r=r: red_ref[pl.ds(RS_CONV_W + 4 * r + chip, 1), :])) for r in range(3)],
            [(lanes(i), row(RS_CONV_B, i)) for i in range(DC // 128)],
        ]
        for a in range(n):
            for idx, grad in chunks[a]:
                g = grad()
                d, m2, v2 = _adam_math(w_refs[a][idx], g, m_refs[a][idx], v_refs[a][idx])
                g_out[a][idx] = g
                d_out[a][idx] = d
                m_out[a][idx] = m2
                v_out[a][idx] = v2

    shapes = [SDS(w.shape, f32) for w in ws]
    res = _pallas_call(body, name="adamw_small", out_shape=shapes * 4,
                       in_specs=[VMEM_SPEC, VMEM_SPEC, SMEM_SPEC] + [VMEM_SPEC] * (3 * n),
                       out_specs=[VMEM_SPEC] * (4 * n))(red, g_c_ctx, jvec, *ws, *ms, *vs)
    return [list(res[k * n:(k + 1) * n]) for k in range(4)]


def _rows128(a):
    return a.reshape(-1, 128)


def kernel(x, c, ctx, c_ctx, w_ada, b_ada, norm_g, w_in, q_norm_g, k_norm_g, rpb, conv_w, conv_b, w_out, loss_target, m_c_ctx, m_w_ada, m_b_ada, m_norm_g, m_w_in, m_q_norm_g, m_k_norm_g, m_rpb, m_conv_w, m_conv_b, m_w_out, v_c_ctx, v_w_ada, v_b_ada, v_norm_g, v_w_in, v_q_norm_g, v_k_norm_g, v_rpb, v_conv_w, v_conv_b, v_w_out):
    xi, yi, ci = lax.axis_index("x"), lax.axis_index("y"), lax.axis_index("c")
    dev = 4 * xi + 2 * yi + ci
    chip = 2 * xi + yi
    cvec = jnp.reshape(ci, (1,)).astype(i32)
    jvec = jnp.reshape(chip, (1,)).astype(i32)
    w_ada_s = w_ada[0]
    ncol = w_ada_s.shape[1]

    gc = _split_start([_to_slot(c.reshape(8, 128), 8, dev)], [], 7, _gather8_copies, "gather_c_start")
    wo4c = _cast_to_slot(w_out[0], jvec, "cast_w_out", after=gc[3])
    w4c = _cast_to_slot(w_in[0], jvec, "cast_w_in", after=wo4c)
    (c8,), _ = _split_wait(gc[0], gc[1], [gc[2]], [], w4c, _gather8_copies, "gather_c_wait")
    cc = jnp.concatenate([c8.reshape(8, D), c_ctx.reshape(1, D), jnp.zeros((7, D), f32)], axis=0)
    m_shard, sc16 = _adaln_shard(cc, w_ada_s)

    conv_w_pad = jnp.pad(conv_w[0], ((0, 5), (0, 0)))
    gm = _split_start([_to_slot(m_shard, 4, chip), _to_slot(conv_w_pad, 4, chip)], [], 6, _gather4_copies,
                      "gather_mod_start")

    all_k = [(0, 0, 0), (0, 0, 1), (0, 0, 2)]
    sem_a, rem_a, w4s, token = _split_start([w4c], [], 1, _near_copies, "weights_near_start", after=gm[4])
    (m4, cw4), _ = _split_wait(gm[0], gm[1], [gm[2], gm[3]], [], token, _gather4_copies, "gather_mod_wait")
    m_full = jnp.transpose(m4, (1, 0, 2)).reshape(16, 4 * ncol)
    mrow = lax.dynamic_slice(m_full, (dev, 0), (1, 3 * D))
    mrow_c = m_full[8:9]
    conv_w_full = jnp.transpose(cw4[:, 0:3, :], (1, 0, 2)).reshape(3, DC)
    waves = {}

    def near(after):
        (w4w,), _ = _split_wait(sem_a, rem_a, [w4s], [], after, _near_copies, "weights_near_wait")
        sem_b, rem_b, w4b, started = _split_start([w4w], [], 2, _pass_relay_copies, "weights_pass_start")
        waves["pass"] = (sem_b, rem_b)
        return w4b, started

    def near2(w4, after):
        (w4w,), _ = _split_wait(*waves["pass"], [w4], [], after, _pass_copy, "weights_pass_wait")
        return w4w

    def far(w4, after):
        (w4w,), _ = _split_wait(*waves["pass"], [w4], [], after, _relay_copy, "weights_far_wait")
        w4x, sem_c, rem_c, wo4s, started = _forward_then_start(w4w, wo4c, all_k, "weights_far_forward_out_start")
        (w4f,), _ = _split_wait(sem_c, rem_c, [w4x], [], started, _diag_forward_copy, "weights_far_forward_wait")
        waves["out"] = (sem_c, rem_c, wo4s)
        return w4f, started

    def w_out_arrived(after):
        sem_c, rem_c, wo4s = waves["out"]
        (wow,) = _halves_wait(sem_c, rem_c, [wo4s], after, all_k, "weights_out_wait")
        sem_d, rem_d, wof, started = _split_start([wow], [], 3, _forward_copies, "weights_out_forward_start")
        waves["out_forward"] = (sem_d, rem_d, wof)
        return started

    def w_out_gathered(after):
        sem_d, rem_d, wof = waves["out_forward"]
        (wo,), _ = _split_wait(sem_d, rem_d, [wof], [], after, _forward_copies, "weights_out_forward_wait")
        return wo.reshape(D, D)

    weights = dict(own=w_in[0], jvec=jvec, started=token, first=jvec ^ (1 + cvec), second=jvec ^ (2 - cvec),
                   near=near, near2=near2, far=far, out_arrived=w_out_arrived, out=w_out_gathered)

    exchange = _exchange_copies
    pending = {}

    def on_g_w_out(g_w_out):
        out = _split_start([g_w_out], [SDS((4, D // 8, D), f32)], 4, exchange, "grad_out_pair_start")
        pending["ex_out"] = out
        return out[4]

    def after_conv(dp8):
        ssem_o, rsem_o, g_o, land_o, _ = pending["ex_out"]
        (g_o,), (ex_o,) = _split_wait(ssem_o, rsem_o, [g_o], [land_o], dp8, exchange, "grad_out_pair_wait")
        to32, tob = _pair_sum_w_out(g_o, ex_o, cvec)
        out = _split_start([tob], [SDS((3, D // 8, D), bf16)], 3, _scatter_copies, "grad_out_chip_start")
        pending["sc_out"] = (out, to32)
        return out[4]

    def on_g_w_in_first(g_first):
        out = _split_start([g_first], [SDS((4, D // 2, D), f32)], 4, _block_exchange_copies, "grad_pair_start")
        pending["ex"] = (out[0], out[1], [out[2]], [out[3]])
        return out[4]

    def on_w_in_pair_sum(pair_sum):
        ex_ssem, ex_rsem, ex_srcs, ex_lands = pending["ex"]
        t32, tb = pair_sum(ex_srcs[0], ex_lands[0], ex_ssem, ex_rsem)
        out = _split_start([tb], [SDS((3, D // 2, D), bf16)], 3, _scatter_copies, "grad_chip_start")
        pending["sc_in"] = (out, t32)
        return out[4]

    r = _local_step(x[0], ctx[0], loss_target[0], mrow, mrow_c, b_ada, norm_g, weights, q_norm_g, k_norm_g,
                    rpb[0], conv_w_full, conv_b,
                    dict(g_w_out=on_g_w_out, after_conv=after_conv, first_half=1 - cvec,
                         g_w_in_first=on_g_w_in_first, w_in_pair_sum=on_w_in_pair_sum))
    sc_in, t32 = pending["sc_in"]
    _, (r2,) = _split_wait(sc_in[0], sc_in[1], [sc_in[2]], [sc_in[3]], r["dng"], _scatter_copies, "grad_chip_wait")
    u_in = _chip_sum(t32, r2, jvec, "chip_sum_w_in")
    sc_o, to32 = pending["sc_out"]
    _, (ro2,) = _split_wait(sc_o[0], sc_o[1], [sc_o[2]], [sc_o[3]], u_in, _scatter_copies, "grad_out_chip_wait")
    u_out = _chip_sum(to32, ro2, jvec, "chip_sum_w_out")
    swap = _split_start([u_in, u_out], [SDS(u_in.shape, f32), SDS(u_out.shape, f32)], 2, _swap_copies,
                        "grad_pair_swap_start")

    dm = jnp.concatenate([r["dshift"], r["dscale"], r["dgate"]], axis=1)
    dmc = jnp.concatenate([r["dshift_c"], r["dscale_c"], jnp.zeros((1, D), f32)], axis=1)
    pack_parts = [_rows128(dm), _rows128(dmc), _rows128(r["dng"]), _rows128(r["dng_c"]), _rows128(r["g_gq"]),
                  _rows128(r["g_gk"]), _rows128(r["g_gk_c"]), r["g_rpb"].reshape(H * N_DR, 128),
                  _rows128(r["g_conv_b"]), _rows128(r["g_conv_w"][0:3]), jnp.pad(r["loss_sum"], ((0, 0), (0, 127)))]
    pack = jnp.concatenate([jnp.pad(p, ((0, -p.shape[0] % 8), (0, 0))) for p in pack_parts], axis=0)
    assert pack.shape[0] == PK_ROWS
    gs = _split_start([_to_slot(pack, 8, dev)], [], 7, _gather8_copies, "gather_small_start", after=swap[6])
    (u_in, u_out), (o_in, o_out) = _split_wait(swap[0], swap[1], swap[2:4], swap[4:6], gs[3], _swap_copies,
                                               "grad_pair_swap_wait")
    g_w_in_s, d_w_in, nm_w_in, nv_w_in = _adamw_halves(w_in[0], u_in, o_in, m_w_in[0], v_w_in[0], cvec, "adamw_w_in",
                                                       after=gs[3])
    g_w_out_s, d_w_out, nm_w_out, nv_w_out = _adamw_halves(w_out[0], u_out, o_out, m_w_out[0], v_w_out[0], cvec,
                                                           "adamw_w_out", after=nm_w_in)
    (gathered,), _ = _split_wait(gs[0], gs[1], [gs[2]], [], nm_w_out, _gather8_copies, "gather_small_wait")
    red, dm16 = _small_reduce(gathered)
    loss = red[RS_LOSS, 0] * (0.5 / D)

    g_w_ada_s, cpart = _w_ada_grad(sc16, dm16, w_ada_s, jvec)
    gcp = _split_start([_to_slot(cpart, 4, chip)], [], 3, _gather4_copies, "gather_c_ctx_parts_start")
    d_w_ada, nm_w_ada, nv_w_ada = _adamw(w_ada_s, g_w_ada_s, m_w_ada[0], v_w_ada[0], "adamw_w_ada", after=gcp[3])
    (cparts4,), _ = _split_wait(gcp[0], gcp[1], [gcp[2]], [], nm_w_ada, _gather4_copies, "gather_c_ctx_parts_wait")
    g_c_ctx = _c_ctx_grad(cparts4, c_ctx)

    t_rpb = lambda a: jnp.transpose(a, (0, 2, 1, 3)).reshape(N_DR, H, N_DC)
    t_cw = lambda a: jnp.transpose(a, (1, 0, 2))
    small = _adamw_small(
        red, g_c_ctx, jvec,
        [c_ctx, b_ada, norm_g, q_norm_g, k_norm_g, t_rpb(rpb), t_cw(conv_w), conv_b],
        [m_c_ctx, m_b_ada, m_norm_g, m_q_norm_g, m_k_norm_g, t_rpb(m_rpb), t_cw(m_conv_w), m_conv_b],
        [v_c_ctx, v_b_ada, v_norm_g, v_q_norm_g, v_k_norm_g, t_rpb(v_rpb), t_cw(v_conv_w), v_conv_b])
    for kind in small:
        kind[5] = jnp.transpose(kind[5].reshape(1, N_DR, H, N_DC), (0, 2, 1, 3))
        kind[6] = jnp.transpose(kind[6], (1, 0, 2))

    def ordered(kind, big_w_ada, big_w_in, big_w_out):
        s_c_ctx, s_b_ada, s_norm_g, s_q, s_k, s_rpb, s_conv_w, s_conv_b = small[kind]
        return [s_c_ctx, big_w_ada[None], s_b_ada, s_norm_g, big_w_in[None], s_q, s_k, s_rpb, s_conv_w,
                s_conv_b, big_w_out[None]]

    grads = ordered(0, g_w_ada_s, g_w_in_s, g_w_out_s)
    deltas = ordered(1, d_w_ada, d_w_in, d_w_out)
    new_m = ordered(2, nm_w_ada, nm_w_in, nm_w_out)
    new_v = ordered(3, nv_w_ada, nv_w_in, nv_w_out)
    return (loss, r["grad_x"][None], *grads, *deltas, *new_m, *new_v)
```

```python
import functools

import jax
import jax.numpy as jnp
import numpy as np
from jax import lax
from jax.experimental import pallas as pl
from jax.experimental.pallas import tpu as pltpu

f32, bf16, i32 = jnp.float32, jnp.bfloat16, jnp.int32
MESH = pl.DeviceIdType.MESH
HIGHEST = lax.Precision.HIGHEST

D = 1024
S = 2048
L = 256
GW = 64
ROWS = S // GW
H = 8
DH = 64
DA = H * DH
DC = 512
WIN_H, WIN_W = 8, 16
N_DR, N_DC = 2 * WIN_H - 1, 2 * WIN_W - 1
RMS_EPS = 1e-6
ROPE_THETA = 10000.0
QK_SCALE = DH ** -0.5
NEG = -1e30

QB = 128
NQB = S // QB
KR = 9
KB = KR * GW
TILE_GEOM = ((0, 0), (2, 0), (4, 0), (28, 23), (30, 23))
NT = len(TILE_GEOM)

ADAM_LR, ADAM_B1, ADAM_B2, ADAM_EPS, ADAM_WD, ADAM_STEP = 0.001, 0.9, 0.999, 1e-08, 0.01, 10

VMEM_SPEC = pl.BlockSpec(memory_space=pltpu.VMEM)
ANY_SPEC = pl.BlockSpec(memory_space=pl.ANY)
SMEM_SPEC = pl.BlockSpec(memory_space=pltpu.SMEM)
SDS = jax.ShapeDtypeStruct


_pallas_call = pl.pallas_call


def _hbm_call(body, *, out_shape, in_specs=None, out_specs=None, grid_spec=None, **kw):
    n_pre = 0
    if grid_spec is not None:
        ispecs, ospecs, n_pre = grid_spec.in_specs, grid_spec.out_specs, grid_spec.num_scalar_prefetch
        kw["grid_spec"] = grid_spec
    else:
        ispecs, ospecs = in_specs, out_specs
        kw.update(in_specs=in_specs, out_specs=out_specs)

    def blocked(spec):
        return isinstance(spec, pl.BlockSpec) and spec.block_shape is not None

    single = not isinstance(out_shape, (tuple, list))
    shapes = [out_shape] if single else list(out_shape)
    ospec_list = list(ospecs) if isinstance(ospecs, (tuple, list)) else [ospecs]
    shapes = [pltpu.HBM(s.shape, s.dtype) if blocked(sp) else s for s, sp in zip(shapes, ospec_list)]
    call = _pallas_call(body, out_shape=shapes[0] if single else tuple(shapes), **kw)

    def run(*args):
        arrays = [pltpu.with_memory_space_constraint(a, pltpu.HBM) if blocked(sp) else a
                  for a, sp in zip(args[n_pre:], ispecs)]
        return call(*args[:n_pre], *arrays)

    return run


def _cp(vmem_mb=None, **kw):
    if vmem_mb is not None:
        kw["vmem_limit_bytes"] = vmem_mb << 20
    return pltpu.CompilerParams(**kw)


def _silu(z):
    return z * jax.nn.sigmoid(z)


def _dsilu(z):
    sg = jax.nn.sigmoid(z)
    return sg * (1.0 + z * (1.0 - sg))


def _row_start(i):
    return min(max(i - WIN_H // 2, 0), ROWS - WIN_H)


def _my_pos():
    return lax.axis_index("x"), lax.axis_index("y"), lax.axis_index("c")


def _flip(v, bit):
    return 1 - v if bit else v


def _swap_copies(srcs, lands, ssem, rsem):
    x, y, c = _my_pos()
    return [pltpu.make_async_remote_copy(src_ref=srcs[a], dst_ref=lands[a], send_sem=ssem.at[a], recv_sem=rsem.at[a],
                                         device_id=(x, y, 1 - c), device_id_type=MESH) for a in range(len(srcs))]


HBM_SPEC = pl.BlockSpec(memory_space=pltpu.HBM)
SEM_SPEC = pl.BlockSpec(memory_space=pltpu.SEMAPHORE)
DATAFLOW = pltpu.SideEffectType.DATAFLOW_SIDE_EFFECTING


def _peer_chips(x, y, c):
    out = []
    for k in range(1, 4):
        px, py = _flip(x, (k >> 1) & 1), _flip(y, k & 1)
        out.append(((px, py, c), 2 * px + py))
    return out


def _half_copies(srcs, dsts, ssem, rsem, which):
    x, y, c = _my_pos()
    j = 2 * x + y
    peers = _peer_chips(x, y, c)
    pairs = []
    for pos, group, k in which:
        half = srcs[pos].shape[1] // 2
        mine = pl.ds(pl.multiple_of(c * half, 8), half)
        dev, pj = peers[k]
        sem = 3 * group + k
        send = pltpu.make_async_remote_copy(src_ref=srcs[pos].at[j, mine], dst_ref=dsts[pos].at[j, mine],
                                            send_sem=ssem.at[sem], recv_sem=rsem.at[sem], device_id=dev,
                                            device_id_type=MESH)
        arrive = pltpu.make_async_remote_copy(src_ref=srcs[pos].at[j, mine], dst_ref=dsts[pos].at[pj, mine],
                                              send_sem=ssem.at[sem], recv_sem=rsem.at[sem], device_id=dev,
                                              device_id_type=MESH)
        pairs.append((send, arrive))
    return pairs


def _halves_wait(ssem, rsem, bigs, after, which, name):
    nb = len(bigs)

    def body(*refs):
        b_in = refs[:nb]
        ssem_ref, rsem_ref = refs[nb], refs[nb + 1]
        for send, arrive in _half_copies(b_in, b_in, ssem_ref, rsem_ref, which):
            send.wait_send()
            arrive.wait_recv()

    return _hbm_call(
        body, name=name, out_shape=tuple(pltpu.HBM(b.shape, b.dtype) for b in bigs),
        in_specs=[HBM_SPEC] * nb + [SEM_SPEC, SEM_SPEC, ANY_SPEC], out_specs=tuple([HBM_SPEC] * nb),
        input_output_aliases={a: a for a in range(nb)}, compiler_params=_cp(has_side_effects=DATAFLOW),
    )(*bigs, ssem, rsem, after)


FORWARD_SEM = 3


def _diag_forward_copy(srcs, dsts, ssem, rsem):
    x, y, c = _my_pos()
    half = srcs[0].shape[1] // 2
    diag = 3 - (2 * x + y)
    mine = pl.ds(pl.multiple_of(c * half, 8), half)
    other = pl.ds(pl.multiple_of((1 - c) * half, 8), half)
    return [_Copy(srcs[0].at[diag, mine], dsts[0].at[diag, mine], dsts[0].at[diag, other], ssem.at[FORWARD_SEM],
                  rsem.at[FORWARD_SEM], (x, y, 1 - c))]


def _forward_then_start(fwd, big, order, name):
    def body(f_in, b_in, f_out, ssem, rsem, b_out, token):
        _diag_forward_copy([f_in], [f_out], ssem, rsem)[0].start()
        for send, _ in _half_copies([b_in], [b_out], ssem, rsem, order):
            send.start()
        token[...] = jnp.zeros_like(token)

    n_sem = FORWARD_SEM + 1
    out_shape = (pltpu.HBM(fwd.shape, fwd.dtype), pltpu.SemaphoreType.DMA((n_sem,)), pltpu.SemaphoreType.DMA((n_sem,)),
                 pltpu.HBM(big.shape, big.dtype), SDS((8, 128), f32))
    return _hbm_call(
        body, name=name, out_shape=out_shape, in_specs=[HBM_SPEC, HBM_SPEC],
        out_specs=(HBM_SPEC, SEM_SPEC, SEM_SPEC, HBM_SPEC, VMEM_SPEC), input_output_aliases={0: 0, 1: 3},
        compiler_params=_cp(has_side_effects=DATAFLOW),
    )(*[pltpu.with_memory_space_constraint(b, pltpu.HBM) for b in (fwd, big)])


def _cast_to_slot(w, jvec, name, after=None):
    rows, cols = w.shape
    tr = 256

    def body(j_ref, w_ref, after_ref, o_ref):
        o_ref[...] = w_ref[...].astype(bf16)

    grid_spec = pltpu.PrefetchScalarGridSpec(
        num_scalar_prefetch=1, grid=(rows // tr,),
        in_specs=[pl.BlockSpec((tr, cols), lambda i, j: (i, 0)), ANY_SPEC],
        out_specs=pl.BlockSpec((None, tr, cols), lambda i, j: (j[0], i, 0)))
    return _hbm_call(body, name=name, out_shape=SDS((4, rows, cols), bf16),
                     grid_spec=grid_spec)(jvec, w, jvec if after is None else after)


def _exchange_copies(srcs, lands, ssem, rsem):
    x, y, c = _my_pos()
    half = srcs[0].shape[0] // 8
    cps = []
    for jb in range(4):
        src = srcs[0].at[pl.ds(pl.multiple_of((2 * jb + 1 - c) * half, 8), half)]
        cps.append(pltpu.make_async_remote_copy(src_ref=src, dst_ref=lands[0].at[jb], send_sem=ssem.at[jb],
                                                recv_sem=rsem.at[jb], device_id=(x, y, 1 - c), device_id_type=MESH))
    return cps


def _block_exchange_copies(srcs, lands, ssem, rsem):
    x, y, c = _my_pos()
    return [pltpu.make_async_remote_copy(src_ref=srcs[0].at[jb], dst_ref=lands[0].at[jb], send_sem=ssem.at[jb],
                                         recv_sem=rsem.at[jb], device_id=(x, y, 1 - c), device_id_type=MESH)
            for jb in range(4)]


def _scatter_copies(srcs, lands, ssem, rsem):
    x, y, c = _my_pos()
    cps = []
    for a in range(len(srcs)):
        for k, (dev, pj) in enumerate(_peer_chips(x, y, c)):
            cps.append(pltpu.make_async_remote_copy(src_ref=srcs[a].at[pj], dst_ref=lands[a].at[k],
                                                    send_sem=ssem.at[3 * a + k], recv_sem=rsem.at[3 * a + k],
                                                    device_id=dev, device_id_type=MESH))
    return cps


class _Copy:
    def __init__(self, src, dst, arrive, ssem, rsem, dev):
        make = lambda to: pltpu.make_async_remote_copy(src_ref=src, dst_ref=to, send_sem=ssem, recv_sem=rsem,
                                                       device_id=dev, device_id_type=MESH)
        send, arrival = make(dst), make(arrive)
        self.start, self.wait_send, self.wait_recv = send.start, send.wait_send, arrival.wait_recv


def _toward(x, y, along_x):
    return x + along_x * (1 - 2 * x), y + (1 - along_x) * (1 - 2 * y)


def _near_copies(srcs, dsts, ssem, rsem):
    x, y, c = _my_pos()
    px, py = _toward(x, y, c)
    j = 2 * x + y
    return [_Copy(srcs[0].at[j], dsts[0].at[j], dsts[0].at[2 * px + py], ssem.at[0], rsem.at[0], (px, py, c))]


def _pass_copy(srcs, dsts, ssem, rsem):
    x, y, c = _my_pos()
    px, py = _toward(x, y, c)
    qx, qy = _toward(x, y, 1 - c)
    got = 2 * px + py
    return [_Copy(srcs[0].at[got], dsts[0].at[got], dsts[0].at[2 * qx + qy], ssem.at[0], rsem.at[0], (x, y, 1 - c))]


def _relay_copy(srcs, dsts, ssem, rsem):
    x, y, c = _my_pos()
    px, py = _toward(x, y, c)
    qx, qy = _toward(x, y, 1 - c)
    half = srcs[0].shape[1] // 2
    mine = pl.ds(pl.multiple_of(c * half, 8), half)
    got, diag = 2 * px + py, 3 - (2 * x + y)
    return [_Copy(srcs[0].at[got, mine], dsts[0].at[got, mine], dsts[0].at[diag, mine], ssem.at[1], rsem.at[1],
                  (qx, qy, c))]


def _forward_copies(srcs, dsts, ssem, rsem):
    x, y, c = _my_pos()
    half = srcs[0].shape[1] // 2
    mine = pl.ds(pl.multiple_of(c * half, 8), half)
    other = pl.ds(pl.multiple_of((1 - c) * half, 8), half)
    return [_Copy(srcs[0].at[pj, mine], dsts[0].at[pj, mine], dsts[0].at[pj, other], ssem.at[k], rsem.at[k],
                  (x, y, 1 - c)) for k, (_, pj) in enumerate(_peer_chips(x, y, c))]


def _pass_relay_copies(srcs, dsts, ssem, rsem):
    return _pass_copy(srcs, dsts, ssem, rsem) + _relay_copy(srcs, dsts, ssem, rsem)


def _gather8_copies(srcs, dsts, ssem, rsem):
    x, y, c = _my_pos()
    me = 4 * x + 2 * y + c
    cps = []
    for a in range(len(srcs)):
        for k in range(1, 8):
            tgt = (_flip(x, (k >> 2) & 1), _flip(y, (k >> 1) & 1), _flip(c, k & 1))
            cps.append(_Copy(srcs[a].at[me], dsts[a].at[me], dsts[a].at[4 * tgt[0] + 2 * tgt[1] + tgt[2]],
                             ssem.at[7 * a + k - 1], rsem.at[7 * a + k - 1], tgt))
    return cps


def _gather4_copies(srcs, dsts, ssem, rsem):
    x, y, c = _my_pos()
    j = 2 * x + y
    cps = []
    for a in range(len(srcs)):
        for k, (dev, pj) in enumerate(_peer_chips(x, y, c)):
            cps.append(_Copy(srcs[a].at[j], dsts[a].at[j], dsts[a].at[pj], ssem.at[3 * a + k], rsem.at[3 * a + k], dev))
    return cps


def _to_slot(a, n, i):
    return lax.dynamic_update_slice(jnp.zeros((n,) + a.shape, a.dtype), a[None], (i,) + (0,) * a.ndim)


def _split_start(srcs, land_shapes, n_cp, make, name, after=None):
    ns, nl = len(srcs), len(land_shapes)
    n_in = ns + nl + (after is not None)

    def body(*refs):
        s_in = refs[:ns]
        ssem, rsem = refs[n_in], refs[n_in + 1]
        s_out = refs[n_in + 2:n_in + 2 + ns]
        l_out = refs[n_in + 2 + ns:n_in + 2 + ns + nl]
        token = refs[n_in + 2 + ns + nl]
        for cp in make(s_in, l_out if nl else s_out, ssem, rsem):
            cp.start()
        token[...] = jnp.zeros_like(token)

    lands = [pltpu.with_memory_space_constraint(lax.empty(sh.shape, sh.dtype), pltpu.HBM) for sh in land_shapes]
    out_shape = (pltpu.SemaphoreType.DMA((n_cp,)), pltpu.SemaphoreType.DMA((n_cp,)),
                 *[pltpu.HBM(b.shape, b.dtype) for b in srcs], *[pltpu.HBM(b.shape, b.dtype) for b in land_shapes],
                 SDS((8, 128), f32))
    return _hbm_call(
        body, name=name, out_shape=out_shape, in_specs=[HBM_SPEC] * (ns + nl) + [ANY_SPEC] * (after is not None),
        out_specs=(SEM_SPEC, SEM_SPEC, *[HBM_SPEC] * (ns + nl), VMEM_SPEC),
        input_output_aliases={i: 2 + i for i in range(ns + nl)}, compiler_params=_cp(has_side_effects=DATAFLOW),
    )(*[pltpu.with_memory_space_constraint(b, pltpu.HBM) for b in srcs], *lands, *([] if after is None else [after]))


def _split_wait(ssem, rsem, srcs, lands, after, make, name):
    ns, nl = len(srcs), len(lands)

    def body(*refs):
        s_in, l_in = refs[:ns], refs[ns:ns + nl]
        ssem_ref, rsem_ref = refs[ns + nl], refs[ns + nl + 1]
        for cp in make(s_in, l_in if nl else s_in, ssem_ref, rsem_ref):
            cp.wait_send()
            cp.wait_recv()

    outs = _hbm_call(
        body, name=name, out_shape=tuple(pltpu.HBM(b.shape, b.dtype) for b in (*srcs, *lands)),
        in_specs=[HBM_SPEC] * (ns + nl) + [SEM_SPEC, SEM_SPEC, ANY_SPEC], out_specs=tuple([HBM_SPEC] * (ns + nl)),
        input_output_aliases={i: i for i in range(ns + nl)}, compiler_params=_cp(has_side_effects=DATAFLOW),
    )(*srcs, *lands, ssem, rsem, after)
    return list(outs[:ns]), list(outs[ns:])


def _adaln_shard(cc, w_ada_shard):
    def body(c_ref, w_ref, m_ref, sc_ref):
        sc = _silu(c_ref[...])
        sc_ref[...] = sc
        m_ref[...] = jnp.dot(sc, w_ref[...], precision=HIGHEST, preferred_element_type=f32)

    return _hbm_call(
        body, name="adaln_shard", out_shape=(SDS((16, w_ada_shard.shape[1]), f32), SDS((16, D), f32)),
        in_specs=[VMEM_SPEC, VMEM_SPEC], out_specs=(VMEM_SPEC, VMEM_SPEC), compiler_params=_cp(32),
    )(cc, w_ada_shard)


def _prenorm(xx, norm_g, mrow, b_ada, tm, name, after=None):
    n = xx.shape[0]

    def body(x_ref, g_ref, m_ref, b_ref, after_ref, h_ref):
        x = x_ref[...]
        shift = m_ref[:, 0:D] + b_ref[:, 0:D]
        scale = m_ref[:, D:2 * D] + b_ref[:, D:2 * D]
        r = lax.rsqrt(jnp.mean(x * x, axis=-1, keepdims=True) + RMS_EPS)
        y = (x * r) * g_ref[...]
        h_ref[...] = (y * (1.0 + scale) + shift).astype(bf16)

    row = lambda i: (i, 0)
    fixed = lambda i: (0, 0)
    return _hbm_call(
        body, name=name, out_shape=SDS((n, D), bf16), grid=(n // tm,),
        in_specs=[pl.BlockSpec((tm, D), row), pl.BlockSpec((1, D), fixed), pl.BlockSpec((1, 3 * D), fixed),
                  pl.BlockSpec((1, 3 * D), fixed), ANY_SPEC],
        out_specs=pl.BlockSpec((tm, D), row),
    )(xx, norm_g, mrow, b_ada, b_ada if after is None else after)


def _in_proj_own(h, w_own, jvec):
    tm = 512

    def body(j_ref, h_ref, w_ref, p_ref):
        p_ref[...] = jnp.dot(h_ref[...], w_ref[...].astype(bf16), preferred_element_type=f32)

    grid_spec = pltpu.PrefetchScalarGridSpec(
        num_scalar_prefetch=1, grid=(S // tm,),
        in_specs=[pl.BlockSpec((tm, D), lambda i, j: (i, 0)), pl.BlockSpec((D, D), lambda i, j: (0, 0))],
        out_specs=pl.BlockSpec((tm, D), lambda i, j: (i, j[0])))
    return _hbm_call(body, name="in_proj_own", out_shape=SDS((S, 4 * D), f32), grid_spec=grid_spec,
                     compiler_params=_cp(40))(jvec, h, w_own)


def _in_proj_block(h, w4, p, bvec, name, after=None):
    tm = 512

    def body(b_ref, h_ref, w_ref, p_in_ref, after_ref, p_ref):
        p_ref[...] = jnp.dot(h_ref[...], w_ref[...], preferred_element_type=f32)

    grid_spec = pltpu.PrefetchScalarGridSpec(
        num_scalar_prefetch=1, grid=(S // tm,),
        in_specs=[pl.BlockSpec((tm, D), lambda i, b: (i, 0)), pl.BlockSpec((None, D, D), lambda i, b: (b[0], 0, 0)),
                  ANY_SPEC, ANY_SPEC],
        out_specs=pl.BlockSpec((tm, D), lambda i, b: (i, b[0])))
    return _hbm_call(body, name=name, out_shape=SDS((S, 4 * D), f32), grid_spec=grid_spec,
                     input_output_aliases={3: 0})(bvec, h, w4, p, bvec if after is None else after)


def _ctx_proj(hc, w4):
    def body(h_ref, w0_ref, w1_ref, p_ref):
        hv = h_ref[...]
        p_ref[:, 0:DA] = jnp.dot(hv, w0_ref[:, DA:2 * DA], preferred_element_type=f32)
        p_ref[:, DA:2 * DA] = jnp.dot(hv, w1_ref[:, 0:DA], preferred_element_type=f32)

    return _hbm_call(
        body, name="ctx_proj", out_shape=SDS((L, 2 * DA), f32), grid=(1,),
        in_specs=[pl.BlockSpec((L, D), lambda i: (0, 0)), pl.BlockSpec((None, D, D), lambda i: (0, 0, 0)),
                  pl.BlockSpec((None, D, D), lambda i: (1, 0, 0))],
        out_specs=pl.BlockSpec((L, 2 * DA), lambda i: (0, 0)),
    )(hc, w4, w4)


def _head_ones():
    r = lax.broadcasted_iota(i32, (DA, DA), 0) // DH
    c = lax.broadcasted_iota(i32, (DA, DA), 1) // DH
    return (r == c).astype(bf16)


def _head_sum(v, ones_bd):
    hi = v.astype(bf16)
    lo = (v - hi.astype(f32)).astype(bf16)
    return jnp.dot(hi, ones_bd, preferred_element_type=f32) + jnp.dot(lo, ones_bd, preferred_element_type=f32)


def _swap16(v):
    lane = lax.broadcasted_iota(i32, v.shape, 1)
    return jnp.where((lane & 31) < 16, pltpu.roll(v, DA - 16, 1), pltpu.roll(v, 16, 1))


def _rope_block(ct_ref, rt_ref, tm):
    rows = [jnp.tile(rt_ref[8 * j:8 * j + 8, :], (GW // 8, 1)) for j in range(tm // GW)]
    return jnp.tile(ct_ref[...], (tm // GW, 1)) + jnp.concatenate(rows, axis=0)


def _rope_specs(tm):
    col = pl.BlockSpec((GW, DA), lambda i: (0, 0))
    row = pl.BlockSpec((8 * tm // GW, DA), lambda i: (i, 0))
    return [col, row, col, row]


def _qk_prep(p, gq, gk, rope):
    tm = 256

    def body(qk_ref, v_ref, gq_ref, gk_ref, cc_ref, cr_ref, sc_ref, sr_ref, qr_ref, qp_ref, kr_ref, vh_ref):
        ones_bd = _head_ones()
        cs, sn = _rope_block(cc_ref, cr_ref, tm), _rope_block(sc_ref, sr_ref, tm)
        q = qk_ref[:, 0:DA]
        k = qk_ref[:, DA:2 * DA]
        yq = (q * lax.rsqrt(_head_sum(q * q, ones_bd) * (1.0 / DH) + RMS_EPS)) * gq_ref[...]
        yk = (k * lax.rsqrt(_head_sum(k * k, ones_bd) * (1.0 / DH) + RMS_EPS)) * gk_ref[...]
        qr = (yq * cs + _swap16(yq) * sn) * QK_SCALE
        qp = yq * QK_SCALE
        kr = yk * cs + _swap16(yk) * sn
        vv = v_ref[...]
        for hh in range(H):
            sl = slice(hh * DH, (hh + 1) * DH)
            qr_ref[hh] = qr[:, sl].astype(bf16)
            qp_ref[hh] = qp[:, sl].astype(bf16)
            kr_ref[hh] = kr[:, sl].astype(bf16)
            vh_ref[hh] = vv[:, sl].astype(bf16)

    hm = SDS((H, S, DH), bf16)
    hspec = pl.BlockSpec((H, tm, DH), lambda i: (0, i, 0))
    fixed = lambda i: (0, 0)
    return _hbm_call(
        body, name="qk_prep", out_shape=(hm, hm, hm, hm), grid=(S // tm,),
        in_specs=[pl.BlockSpec((tm, 2 * DA), lambda i: (i, 0)), pl.BlockSpec((tm, DA), lambda i: (i, 2)),
                  pl.BlockSpec((1, DA), fixed), pl.BlockSpec((1, DA), fixed)] + _rope_specs(tm),
        out_specs=(hspec, hspec, hspec, hspec),
    )(p, p, gq, gk, *rope)


def _ctx_prep(pc, gk):
    def body(p_ref, gk_ref, kc_ref, vc_ref):
        ones_bd = _head_ones()
        k = p_ref[:, 0:DA]
        yk = (k * lax.rsqrt(_head_sum(k * k, ones_bd) * (1.0 / DH) + RMS_EPS)) * gk_ref[...]
        vv = p_ref[:, DA:2 * DA]
        for hh in range(H):
            sl = slice(hh * DH, (hh + 1) * DH)
            kc_ref[hh] = yk[:, sl].astype(bf16)
            vc_ref[hh] = vv[:, sl].astype(bf16)

    hm = SDS((H, L, DH), bf16)
    return _hbm_call(
        body, name="ctx_prep", out_shape=(hm, hm), in_specs=[VMEM_SPEC, VMEM_SPEC], out_specs=(VMEM_SPEC, VMEM_SPEC),
    )(pc, gk)


def _tile_pieces():
    out = []
    for (i0, u0) in TILE_GEOM:
        rows = []
        for j in range(2):
            i = i0 + j
            rs = _row_start(i)
            rows.append([(u0 + u - i + WIN_H - 1) if rs <= u0 + u < rs + WIN_H else None for u in range(KR)])
        out.append(rows)
    return out


def _bias_prep(rpb_rev_pad, after=None):
    pieces = _tile_pieces()

    def body(r_ref, after_ref, o_ref):
        rp = r_ref[...]
        xs = jnp.concatenate([pltpu.roll(jnp.broadcast_to(rp[dr:dr + 1, :], (GW, 128)), 128 - (WIN_W - 1), 1,
                                         stride=1, stride_axis=0) for dr in range(N_DR)], axis=0)
        row = lax.broadcasted_iota(i32, xs.shape, 0)
        lane = lax.broadcasted_iota(i32, xs.shape, 1)
        k = row & (GW - 1)
        c0 = jnp.clip(lane - WIN_W // 2, 0, GW - WIN_W)
        xs = jnp.where((k >= c0) & (k < c0 + WIN_W), xs, NEG)
        neg = jnp.full((GW, GW), NEG, f32)
        for t in range(NT):
            for j in range(2):
                for u in range(KR):
                    dr = pieces[t][j][u]
                    piece = neg if dr is None else xs[dr * GW:(dr + 1) * GW, 0:GW]
                    o_ref[t, u * GW:(u + 1) * GW, j * GW:(j + 1) * GW] = piece

    return _hbm_call(
        body, name="bias_prep", out_shape=SDS((H, NT, KB, QB), f32), grid=(H,),
        in_specs=[pl.BlockSpec((None, N_DR, 128), lambda h: (h, 0, 0)), ANY_SPEC],
        out_specs=pl.BlockSpec((None, NT, KB, QB), lambda h: (h, 0, 0, 0)),
    )(rpb_rev_pad, rpb_rev_pad if after is None else after)


def _bias_tiles(rpb2, after=None):
    return _bias_prep(jnp.pad(rpb2[:, :, ::-1], ((0, 0), (0, 0), (0, 128 - N_DC))), after)


def _block_geom(b):
    qs = b * QB
    ks = min(max(2 * b - 4, 0), ROWS - KR) * GW
    t = b if b < 2 else (b - (NQB - NT) if b > NQB - 3 else 2)
    return qs, ks, t


def _tt(a, b):
    return lax.dot_general(a, b, (((1,), (1,)), ((), ())), preferred_element_type=f32)


def _tn(a, b):
    return lax.dot_general(a, b, (((0,), (0,)), ((), ())), preferred_element_type=f32)


def _softmax_t(s_lat, s_ctx):
    m = jnp.maximum(jnp.max(s_lat, axis=0, keepdims=True), jnp.max(s_ctx, axis=0, keepdims=True))
    e_lat = jnp.exp(s_lat - m)
    e_ctx = jnp.exp(s_ctx - m)
    inv = 1.0 / (jnp.sum(e_lat, axis=0, keepdims=True) + jnp.sum(e_ctx, axis=0, keepdims=True))
    return e_lat * inv, e_ctx * inv


def _staged(n_blocks, stages):
    held = [dict() for _ in stages]
    for step in range(n_blocks + len(stages) - 1):
        for s, fn in enumerate(stages):
            b = step - s
            if 0 <= b < n_blocks:
                held[s][b] = fn(b) if s == 0 else fn(b, held[s - 1].pop(b))


def _attn_fwd(qr, qp, kr, vh, kc, vc, btt):
    def body(qr_ref, qp_ref, kr_ref, v_ref, kc_ref, vc_ref, bt_ref, o_ref):
        kcv, vcv = kc_ref[...], vc_ref[...]

        def scores(b):
            qs, ks, t = _block_geom(b)
            return (_tt(kr_ref[ks:ks + KB, :], qr_ref[qs:qs + QB, :]) + bt_ref[t], _tt(kcv, qp_ref[qs:qs + QB, :]))

        def probs(b, sc):
            p_lat, p_ctx = _softmax_t(*sc)
            return p_lat.astype(bf16), p_ctx.astype(bf16)

        def values(b, p):
            qs, ks, _ = _block_geom(b)
            o_ref[qs:qs + QB, :] = _tn(p[0], v_ref[ks:ks + KB, :]) + _tn(p[1], vcv)

        _staged(NQB, (scores, probs, values))

    sq = pl.BlockSpec((None, S, DH), lambda h: (h, 0, 0))
    sc = pl.BlockSpec((None, L, DH), lambda h: (h, 0, 0))
    return _hbm_call(
        body, name="attn_fwd", out_shape=SDS((H, S, DH), f32), grid=(H,),
        in_specs=[sq, sq, sq, sq, sc, sc, pl.BlockSpec((None, NT, KB, QB), lambda h: (h, 0, 0, 0))],
        out_specs=sq, compiler_params=_cp(48),
    )(qr, qp, kr, vh, kc, vc, btt)


def _shift_rows(v, down):
    n = v.shape[0]
    row = lax.broadcasted_iota(i32, v.shape, 0)
    if down:
        return jnp.where(row == 0, 0.0, pltpu.roll(v, 1, 0))
    return jnp.where(row == n - 1, 0.0, pltpu.roll(v, n - 1, 0))


def _conv_specs():
    col = lambda off: pl.BlockSpec((S, 128), lambda i, off=off: (0, off + i))
    return [col(16), col(20), col(24), col(28), pl.BlockSpec((3, 128), lambda i: (0, i)),
            pl.BlockSpec((1, 128), lambda i: (0, i))]


def _conv_fwd(p, conv_w, conv_b, after=None):
    def body(u_ref, bg_ref, cg_ref, zc_ref, w_ref, b_ref, after_ref, o_ref):
        cu = cg_ref[...] * u_ref[...]
        cv = b_ref[...] + _shift_rows(cu, True) * w_ref[0:1, :]
        cv = cv + cu * w_ref[1:2, :]
        cv = cv + _shift_rows(cu, False) * w_ref[2:3, :]
        o_ref[...] = ((bg_ref[...] * cv) * _silu(zc_ref[...])).astype(bf16)

    return _hbm_call(
        body, name="conv_fwd", out_shape=SDS((S, DC), bf16), grid=(DC // 128,),
        in_specs=_conv_specs() + [ANY_SPEC], out_specs=pl.BlockSpec((S, 128), lambda i: (0, i)),
        compiler_params=_cp(40),
    )(p, p, p, p, conv_w, conv_b, conv_b if after is None else after)


DP_Q, DP_K, DP_V, DP_ZA, DP_U, DP_BG, DP_CG, DP_ZC = range(8)


def _out_proj_loss(o, p, conv_g, w_out, xx, tgt, mrow, b_ada):
    tm = 256

    def body(o_ref, za_ref, c_ref, w_ref, x_ref, t_ref, m_ref, b_ref,
             dy_ref, dconv_ref, dp_ref, do_ref, gwo_ref, dgate_ref, loss_ref):
        k = pl.program_id(0)

        @pl.when(k == 0)
        def _():
            gwo_ref[...] = jnp.zeros_like(gwo_ref)
            dgate_ref[...] = jnp.zeros_like(dgate_ref)
            loss_ref[0, 0] = 0.0

        gate = m_ref[:, 2 * D:3 * D] + b_ref[:, 2 * D:3 * D]
        za = za_ref[...]
        sz = _silu(za)
        om = _merge_heads(o_ref)
        av, cv = (om * sz).astype(bf16), c_ref[...]
        mo = jnp.dot(av, w_ref[0:DA, :], preferred_element_type=f32)
        mo = mo + jnp.dot(cv, w_ref[DA:DA + DC, :], preferred_element_type=f32)
        y = x_ref[...] + gate * mo
        diff = y - t_ref[...]
        loss_ref[0, 0] += jnp.sum(diff * diff)
        dy = diff * (1.0 / D)
        dy_ref[...] = dy
        dgate_ref[...] += jnp.sum(dy * mo, axis=0, keepdims=True)
        dmo = (dy * gate).astype(bf16)
        dmix = _tt(dmo, w_ref[...])
        dattn = dmix[:, 0:DA]
        dconv_ref[...] = dmix[:, DA:DA + DC]
        a = dattn * sz
        for hh in range(H):
            do_ref[hh] = a[:, hh * DH:(hh + 1) * DH].astype(bf16)
        dp_ref[...] = ((dattn * _dsilu(za)) * om).astype(bf16)
        gwo_ref[0:DA, :] += _tn(av, dmo)
        gwo_ref[DA:DA + DC, :] += _tn(cv, dmo)

    row = lambda i: (i, 0)
    fixed = lambda i: (0, 0)
    hspec = pl.BlockSpec((H, tm, DH), lambda i: (0, i, 0))
    return _hbm_call(
        body, name="out_proj_loss",
        out_shape=(SDS((S, D), f32), SDS((S, DC), f32), SDS((8, S, DA), bf16), SDS((H, S, DH), bf16),
                   SDS((D, D), f32), SDS((1, D), f32), SDS((1, 1), f32)),
        grid=(S // tm,),
        in_specs=[hspec, pl.BlockSpec((tm, DA), lambda i: (i, 3)), pl.BlockSpec((tm, DC), row),
                  pl.BlockSpec((D, D), fixed), pl.BlockSpec((tm, D), row), pl.BlockSpec((tm, D), row),
                  pl.BlockSpec((1, 3 * D), fixed), pl.BlockSpec((1, 3 * D), fixed)],
        out_specs=(pl.BlockSpec((tm, D), row), pl.BlockSpec((tm, DC), row),
                   pl.BlockSpec((None, tm, DA), lambda i: (DP_ZA, i, 0)), hspec, pl.BlockSpec((D, D), fixed),
                   pl.BlockSpec((1, D), fixed), SMEM_SPEC),
        compiler_params=_cp(56, dimension_semantics=("arbitrary",)),
    )(o, p, conv_g, w_out, xx, tgt, mrow, b_ada)


def _conv_bwd(dconv, p, conv_w, conv_b, dp8, after=None):
    def body(d_ref, u_ref, bg_ref, cg_ref, zc_ref, w_ref, b_ref, dp_in_ref, after_ref, dp_ref, gw_ref, gb_ref):
        du_ref, dbg_ref, dcg_ref, dzc_ref = dp_ref.at[0], dp_ref.at[1], dp_ref.at[2], dp_ref.at[3]
        dconv = d_ref[...]
        u, bg, cg, zc = u_ref[...], bg_ref[...], cg_ref[...], zc_ref[...]
        w0, w1, w2 = w_ref[0:1, :], w_ref[1:2, :], w_ref[2:3, :]
        cu = cg * u
        cu_m, cu_p = _shift_rows(cu, True), _shift_rows(cu, False)
        cv = b_ref[...] + cu_m * w0
        cv = cv + cu * w1
        cv = cv + cu_p * w2
        sz = _silu(zc)
        dbg_ref[...] = ((dconv * sz) * cv).astype(bf16)
        dzc_ref[...] = ((dconv * (bg * cv)) * _dsilu(zc)).astype(bf16)
        dcv = (dconv * sz) * bg
        gb_ref[...] = jnp.sum(dcv, axis=0, keepdims=True)
        gw_ref[0:1, :] = jnp.sum(dcv * cu_m, axis=0, keepdims=True)
        gw_ref[1:2, :] = jnp.sum(dcv * cu, axis=0, keepdims=True)
        gw_ref[2:3, :] = jnp.sum(dcv * cu_p, axis=0, keepdims=True)
        gw_ref[3:8, :] = jnp.zeros((5, 128), f32)
        dcu = _shift_rows(dcv, False) * w0 + dcv * w1 + _shift_rows(dcv, True) * w2
        dcg_ref[...] = (dcu * u).astype(bf16)
        du_ref[...] = (dcu * cg).astype(bf16)

    return _hbm_call(
        body, name="conv_bwd", out_shape=(SDS((8, S, DC), bf16), SDS((8, DC), f32), SDS((1, DC), f32)),
        grid=(DC // 128,),
        in_specs=[pl.BlockSpec((S, 128), lambda i: (0, i))] + _conv_specs() + [ANY_SPEC, ANY_SPEC],
        out_specs=(pl.BlockSpec((4, S, 128), lambda i: (DP_U // 4, 0, i)), pl.BlockSpec((8, 128), lambda i: (0, i)),
                   pl.BlockSpec((1, 128), lambda i: (0, i))),
        input_output_aliases={7: 0}, compiler_params=_cp(48),
    )(dconv, p, p, p, p, conv_w, conv_b, dp8, conv_b if after is None else after)


def _attn_bwd(qr, qp, kr, vh, kc, vc, btt, do, after=None):
    def body(qr_ref, qp_ref, kr_ref, v_ref, kc_ref, vc_ref, bt_ref, do_ref, after_ref,
             dqr_ref, dqp_ref, dkr_ref, dv_ref, dkc_ref, dvc_ref, dbt_ref):
        kcv, vcv = kc_ref[...], vc_ref[...]
        dkr_ref[...] = jnp.zeros_like(dkr_ref)
        dv_ref[...] = jnp.zeros_like(dv_ref)
        dbt_ref[...] = jnp.zeros_like(dbt_ref)
        ctx_acc = {}

        def products(b):
            qs, ks, t = _block_geom(b)
            dob = do_ref[qs:qs + QB, :]
            s_lat = _tt(kr_ref[ks:ks + KB, :], qr_ref[qs:qs + QB, :]) + bt_ref[t]
            s_ctx = _tt(kcv, qp_ref[qs:qs + QB, :])
            return s_lat, s_ctx, _tt(v_ref[ks:ks + KB, :], dob), _tt(vcv, dob)

        def score_grads(b, x):
            s_lat, s_ctx, dp_lat, dp_ctx = x
            p_lat, p_ctx = _softmax_t(s_lat, s_ctx)
            delta = jnp.sum(p_lat * dp_lat, axis=0, keepdims=True) + jnp.sum(p_ctx * dp_ctx, axis=0, keepdims=True)
            ds_lat = p_lat * (dp_lat - delta)
            ds_ctx = p_ctx * (dp_ctx - delta)
            return ds_lat, ds_lat.astype(bf16), ds_ctx.astype(bf16), p_lat.astype(bf16), p_ctx.astype(bf16)

        def operand_grads(b, y):
            qs, ks, t = _block_geom(b)
            ds_lat, dsb_lat, dsb_ctx, pb_lat, pb_ctx = y
            qrb, qpb, dob = qr_ref[qs:qs + QB, :], qp_ref[qs:qs + QB, :], do_ref[qs:qs + QB, :]
            dbt_ref[t] += ds_lat
            dqr_ref[qs:qs + QB, :] = _tn(dsb_lat, kr_ref[ks:ks + KB, :])
            dqp_ref[qs:qs + QB, :] = _tn(dsb_ctx, kcv)
            dkr_ref[ks:ks + KB, :] += jnp.dot(dsb_lat, qrb, preferred_element_type=f32)
            dv_ref[ks:ks + KB, :] += jnp.dot(pb_lat, dob, preferred_element_type=f32)
            dkc = jnp.dot(dsb_ctx, qpb, preferred_element_type=f32)
            dvc = jnp.dot(pb_ctx, dob, preferred_element_type=f32)
            ctx_acc["k"] = dkc if b == 0 else ctx_acc["k"] + dkc
            ctx_acc["v"] = dvc if b == 0 else ctx_acc["v"] + dvc

        _staged(NQB, (products, score_grads, operand_grads))
        dkc_ref[...] = ctx_acc["k"]
        dvc_ref[...] = ctx_acc["v"]

    sq = pl.BlockSpec((None, S, DH), lambda h: (h, 0, 0))
    sc = pl.BlockSpec((None, L, DH), lambda h: (h, 0, 0))
    sb = pl.BlockSpec((None, NT, KB, QB), lambda h: (h, 0, 0, 0))
    big, ctxs = SDS((H, S, DH), f32), SDS((H, L, DH), f32)
    return _hbm_call(
        body, name="attn_bwd", out_shape=(big, big, big, big, ctxs, ctxs, SDS((H, NT, KB, QB), f32)), grid=(H,),
        in_specs=[sq, sq, sq, sq, sc, sc, sb, sq, ANY_SPEC], out_specs=(sq, sq, sq, sq, sc, sc, sb),
        compiler_params=_cp(56),
    )(qr, qp, kr, vh, kc, vc, btt, do, do if after is None else after)


def _bias_bwd(dbtt, after=None):
    pieces = _tile_pieces()

    def body(d_ref, after_ref, o_ref, scr):
        scr[...] = jnp.zeros_like(scr)
        acc = [None] * N_DR
        for t in range(NT):
            for j in range(2):
                for u in range(KR):
                    dr = pieces[t][j][u]
                    if dr is None:
                        continue
                    piece = d_ref[t, u * GW:(u + 1) * GW, j * GW:(j + 1) * GW]
                    acc[dr] = piece if acc[dr] is None else acc[dr] + piece
        a = lax.broadcasted_iota(i32, (GW, GW), 0)
        b = lax.broadcasted_iota(i32, (GW, GW), 1)
        flip = (a + b == GW - 1).astype(f32)
        for dr in range(N_DR):
            scr[dr * GW:(dr + 1) * GW, 0:GW] = jnp.dot(acc[dr], flip, precision=HIGHEST, preferred_element_type=f32)
        xs = jnp.concatenate([pltpu.roll(scr[dr * GW:(dr + 1) * GW, :], 128 + (WIN_W - 1) - (GW - 1), 1,
                                         stride=1, stride_axis=0) for dr in range(N_DR)], axis=0)
        tot = jnp.sum(xs.reshape(N_DR, GW, 128), axis=1)
        lane = lax.broadcasted_iota(i32, tot.shape, 1)
        o_ref[...] = jnp.where(lane < N_DC, tot, 0.0)

    return _hbm_call(
        body, name="bias_bwd", out_shape=SDS((H, N_DR, 128), f32), grid=(H,),
        in_specs=[pl.BlockSpec((None, NT, KB, QB), lambda h: (h, 0, 0, 0)), ANY_SPEC],
        out_specs=pl.BlockSpec((None, N_DR, 128), lambda h: (h, 0, 0)),
        scratch_shapes=[pltpu.VMEM((N_DR * GW, 128), f32)],
    )(dbtt, dbtt if after is None else after)


def _merge_heads(ref):
    return jnp.concatenate([ref[hh] for hh in range(H)], axis=1)


def _head_norm_bwd(xraw, gain, dy, ones_bd):
    r = lax.rsqrt(_head_sum(xraw * xraw, ones_bd) * (1.0 / DH) + RMS_EPS)
    xh = xraw * r
    gdy = dy * gain
    dx = r * (gdy - xh * (_head_sum(xh * gdy, ones_bd) * (1.0 / DH)))
    return dx, jnp.sum(dy * xh, axis=0, keepdims=True)


def _qk_bwd(dqr, dqp, dkr, dvh, p, gq, gk, rope, dp8):
    tm = 256

    def body(dqr_ref, dqp_ref, dkr_ref, dv_ref, qk_ref, gq_ref, gk_ref, cc_ref, cr_ref, sc_ref, sr_ref, dp_in_ref,
             dp_ref, ggq_ref, ggk_ref):
        dq_ref, dk_ref, dvo_ref = dp_ref.at[DP_Q], dp_ref.at[DP_K], dp_ref.at[DP_V]

        @pl.when(pl.program_id(0) == 0)
        def _():
            ggq_ref[...] = jnp.zeros_like(ggq_ref)
            ggk_ref[...] = jnp.zeros_like(ggk_ref)

        ones_bd = _head_ones()
        cs, sn = _rope_block(cc_ref, cr_ref, tm), _rope_block(sc_ref, sr_ref, tm)
        a = _merge_heads(dqr_ref)
        dyq = ((a * cs - _swap16(a) * sn) + _merge_heads(dqp_ref)) * QK_SCALE
        bk = _merge_heads(dkr_ref)
        dyk = bk * cs - _swap16(bk) * sn
        dq, gq_part = _head_norm_bwd(qk_ref[:, 0:DA], gq_ref[...], dyq, ones_bd)
        dk, gk_part = _head_norm_bwd(qk_ref[:, DA:2 * DA], gk_ref[...], dyk, ones_bd)
        dq_ref[...] = dq.astype(bf16)
        dk_ref[...] = dk.astype(bf16)
        dvo_ref[...] = _merge_heads(dv_ref).astype(bf16)
        ggq_ref[...] += gq_part
        ggk_ref[...] += gk_part

    hspec = pl.BlockSpec((H, tm, DH), lambda i: (0, i, 0))
    fixed = pl.BlockSpec((1, DA), lambda i: (0, 0))
    return _hbm_call(
        body, name="qk_bwd", out_shape=(SDS((8, S, DA), bf16), SDS((1, DA), f32), SDS((1, DA), f32)), grid=(S // tm,),
        in_specs=[hspec, hspec, hspec, hspec, pl.BlockSpec((tm, 2 * DA), lambda i: (i, 0)), fixed, fixed]
        + _rope_specs(tm) + [ANY_SPEC],
        out_specs=(pl.BlockSpec((3, tm, DA), lambda i: (0, i, 0)), fixed, fixed), input_output_aliases={11: 0},
        compiler_params=_cp(40, dimension_semantics=("arbitrary",)),
    )(dqr, dqp, dkr, dvh, p, gq, gk, *rope, dp8)


def _ctx_bwd(dkc, dvc, pc, gk):
    def body(dkc_ref, dvc_ref, p_ref, gk_ref, dk_ref, dv_ref, ggk_ref):
        ones_bd = _head_ones()
        dk, gk_part = _head_norm_bwd(p_ref[:, 0:DA], gk_ref[...], _merge_heads(dkc_ref), ones_bd)
        dk_ref[...] = dk.astype(bf16)
        dv_ref[...] = _merge_heads(dvc_ref).astype(bf16)
        ggk_ref[...] = gk_part

    piece = SDS((L, DA), bf16)
    return _hbm_call(
        body, name="ctx_bwd", out_shape=(piece, piece, SDS((1, DA), f32)), in_specs=[VMEM_SPEC] * 4,
        out_specs=(VMEM_SPEC,) * 3,
    )(dkc, dvc, pc, gk)


def _grad_w_in(h, dp8, hc, dkc_raw, dvc_m, half, name, after=None):
    def body(half_ref, h_ref, p_ref, hc_ref, dk_ref, dv_ref, after_ref, g_ref):
        j = pl.program_id(0)
        hv = h_ref[...]
        g_ref[:, 0:DA] = _tn(hv, p_ref[0])
        g_ref[:, DA:2 * DA] = _tn(hv, p_ref[1])

        @pl.when(j == 0)
        def _():
            g_ref[:, DA:2 * DA] += _tn(hc_ref[...], dk_ref[...])

        @pl.when(j == 1)
        def _():
            g_ref[:, 0:DA] += _tn(hc_ref[...], dv_ref[...])

    fixed = lambda j, s: (0, 0)
    hd = D // 2
    grid_spec = pltpu.PrefetchScalarGridSpec(
        num_scalar_prefetch=1, grid=(4,),
        in_specs=[pl.BlockSpec((S, hd), lambda j, s: (0, s[0])), pl.BlockSpec((2, S, DA), lambda j, s: (j, 0, 0)),
                  pl.BlockSpec((L, hd), lambda j, s: (0, s[0])), pl.BlockSpec((L, DA), fixed),
                  pl.BlockSpec((L, DA), fixed), ANY_SPEC],
        out_specs=pl.BlockSpec((None, hd, D), lambda j, s: (j, 0, 0)))
    return _hbm_call(
        body, name=name, out_shape=SDS((4, hd, D), f32), grid_spec=grid_spec, compiler_params=_cp(40),
    )(half, h, dp8, hc, dkc_raw, dvc_m, half if after is None else after)


def _grad_w_in_pair_sum(h, dp8, hc, dkc_raw, dvc_m, half, sent, landed, ssem, rsem):
    def body(half_ref, h_ref, p_ref, hc_ref, dk_ref, dv_ref, sent_ref, land_ref, ssem_ref, rsem_ref,
             t32_ref, tb_ref, buf, lsem):
        j = pl.program_id(0)
        hv = h_ref[...]
        t32_ref[:, 0:DA] = _tn(hv, p_ref[0])
        x, y, c = _my_pos()
        arrival = pltpu.make_async_remote_copy(src_ref=sent_ref.at[j], dst_ref=land_ref.at[j], send_sem=ssem_ref.at[j],
                                               recv_sem=rsem_ref.at[j], device_id=(x, y, 1 - c), device_id_type=MESH)
        arrival.wait_recv()
        arrival.wait_send()
        load = pltpu.make_async_copy(land_ref.at[j], buf, lsem)
        load.start()
        t32_ref[:, DA:2 * DA] = _tn(hv, p_ref[1])

        @pl.when(j == 0)
        def _():
            t32_ref[:, DA:2 * DA] += _tn(hc_ref[...], dk_ref[...])

        @pl.when(j == 1)
        def _():
            t32_ref[:, 0:DA] += _tn(hc_ref[...], dv_ref[...])

        load.wait()
        t = t32_ref[...] + buf[...]
        t32_ref[...] = t
        tb_ref[...] = t.astype(bf16)

    fixed = lambda j, s: (0, 0)
    hd = D // 2
    out_spec = pl.BlockSpec((None, hd, D), lambda j, s: (j, 0, 0))
    grid_spec = pltpu.PrefetchScalarGridSpec(
        num_scalar_prefetch=1, grid=(4,),
        in_specs=[pl.BlockSpec((S, hd), lambda j, s: (0, s[0])), pl.BlockSpec((2, S, DA), lambda j, s: (j, 0, 0)),
                  pl.BlockSpec((L, hd), lambda j, s: (0, s[0])), pl.BlockSpec((L, DA), fixed),
                  pl.BlockSpec((L, DA), fixed), HBM_SPEC, HBM_SPEC, SEM_SPEC, SEM_SPEC],
        out_specs=(out_spec, out_spec), scratch_shapes=[pltpu.VMEM((hd, D), f32), pltpu.SemaphoreType.DMA])
    return _hbm_call(
        body, name="grad_w_in_pair_sum", out_shape=(SDS((4, hd, D), f32), SDS((4, hd, D), bf16)), grid_spec=grid_spec,
        compiler_params=_cp(40, has_side_effects=DATAFLOW),
    )(half, h, dp8, hc, dkc_raw, dvc_m, sent, landed, ssem, rsem)


def _norm_mod_bwd(x, dh, g, scale):
    r = lax.rsqrt(jnp.mean(x * x, axis=-1, keepdims=True) + RMS_EPS)
    xh = x * r
    y = xh * g
    dshift = jnp.sum(dh, axis=0, keepdims=True)
    dscale = jnp.sum(dh * y, axis=0, keepdims=True)
    dyn = dh * (1.0 + scale)
    dg = jnp.sum(dyn * xh, axis=0, keepdims=True)
    gdy = dyn * g
    dx = r * (gdy - xh * jnp.mean(xh * gdy, axis=-1, keepdims=True))
    return dx, dshift, dscale, dg


def _dh_grad_x(dp8, w4, xx, dy, norm_g, mrow, b_ada, after=None):
    tm = 256

    def body(p_ref, w_ref, x_ref, dy_ref, g_ref, m_ref, b_ref, after_ref, gx_ref, dsh_ref, dsc_ref, dg_ref):
        @pl.when(pl.program_id(0) == 0)
        def _():
            dsh_ref[...] = jnp.zeros_like(dsh_ref)
            dsc_ref[...] = jnp.zeros_like(dsc_ref)
            dg_ref[...] = jnp.zeros_like(dg_ref)

        dh = None
        for j in range(4):
            for half in range(2):
                term = _tt(p_ref[2 * j + half], w_ref[j, :, half * DA:(half + 1) * DA])
                dh = term if dh is None else dh + term
        scale = m_ref[:, D:2 * D] + b_ref[:, D:2 * D]
        dx, dshift, dscale, dg = _norm_mod_bwd(x_ref[...], dh, g_ref[...], scale)
        gx_ref[...] = dy_ref[...] + dx
        dsh_ref[...] += dshift
        dsc_ref[...] += dscale
        dg_ref[...] += dg

    row = lambda i: (i, 0)
    fixed = lambda i: (0, 0)
    vec = SDS((1, D), f32)
    return _hbm_call(
        body, name="dh_grad_x", out_shape=(SDS((S, D), f32), vec, vec, vec), grid=(S // tm,),
        in_specs=[pl.BlockSpec((8, tm, DA), lambda i: (0, i, 0)), pl.BlockSpec((4, D, D), lambda i: (0, 0, 0)),
                  pl.BlockSpec((tm, D), row), pl.BlockSpec((tm, D), row), pl.BlockSpec((1, D), fixed),
                  pl.BlockSpec((1, 3 * D), fixed), pl.BlockSpec((1, 3 * D), fixed), ANY_SPEC],
        out_specs=(pl.BlockSpec((tm, D), row), pl.BlockSpec((1, D), fixed), pl.BlockSpec((1, D), fixed),
                   pl.BlockSpec((1, D), fixed)),
        compiler_params=_cp(56, dimension_semantics=("arbitrary",)),
    )(dp8, w4, xx, dy, norm_g, mrow, b_ada, b_ada if after is None else after)


def _dhc_sums(dkc_raw, dvc_m, w4, ctx2, norm_g, mrow_c, b_ada, after=None):
    def body(dk_ref, dv_ref, w0_ref, w1_ref, x_ref, g_ref, m_ref, b_ref, after_ref, dsh_ref, dsc_ref, dg_ref):
        dh = _tt(dk_ref[...], w0_ref[:, DA:2 * DA]) + _tt(dv_ref[...], w1_ref[:, 0:DA])
        scale = m_ref[:, D:2 * D] + b_ref[:, D:2 * D]
        _, dshift, dscale, dg = _norm_mod_bwd(x_ref[...], dh, g_ref[...], scale)
        dsh_ref[...] = dshift
        dsc_ref[...] = dscale
        dg_ref[...] = dg

    fixed = lambda i: (0, 0)
    vec = SDS((1, D), f32)
    vspec = pl.BlockSpec((1, D), fixed)
    return _hbm_call(
        body, name="dhc_sums", out_shape=(vec, vec, vec), grid=(1,),
        in_specs=[pl.BlockSpec((L, DA), fixed), pl.BlockSpec((L, DA), fixed),
                  pl.BlockSpec((None, D, D), lambda i: (0, 0, 0)), pl.BlockSpec((None, D, D), lambda i: (1, 0, 0)),
                  pl.BlockSpec((L, D), fixed), vspec, pl.BlockSpec((1, 3 * D), fixed), pl.BlockSpec((1, 3 * D), fixed),
                  ANY_SPEC],
        out_specs=(vspec, vspec, vspec), compiler_params=_cp(32),
    )(dkc_raw, dvc_m, w4, w4, ctx2, norm_g, mrow_c, b_ada, b_ada if after is None else after)


def _rope_tables():
    nf = DH // 4
    inv = np.float32(ROPE_THETA) ** (-np.arange(nf, dtype=np.float32) / np.float32(nf))
    ang_c = np.arange(GW, dtype=np.float32)[:, None] * inv
    ang_r = np.arange(ROWS, dtype=np.float32)[:, None] * inv
    zc, zr = np.zeros((GW, 2 * nf), np.float32), np.zeros((ROWS, 2 * nf), np.float32)
    ct_cos = np.tile(np.concatenate([zc, np.cos(ang_c), np.cos(ang_c)], axis=1), (1, H))
    ct_sin = np.tile(np.concatenate([zc, -np.sin(ang_c), np.sin(ang_c)], axis=1), (1, H))
    rt_cos = np.tile(np.concatenate([np.cos(ang_r), np.cos(ang_r), zr], axis=1), (1, H))
    rt_sin = np.tile(np.concatenate([-np.sin(ang_r), np.sin(ang_r), zr], axis=1), (1, H))
    rep8 = lambda t: np.ascontiguousarray(np.broadcast_to(t[:, None, :], (ROWS, 8, DA))).reshape(ROWS * 8, DA)
    return tuple(jnp.asarray(t, f32) for t in (ct_cos, rep8(rt_cos), ct_sin, rep8(rt_sin)))


def _local_step(xx, ctx2, tgt, mrow, mrow_c, b_ada, norm_g, weights, q_norm_g, k_norm_g, rpb2, conv_w_full, conv_b,
                hooks=None):
    hooks = hooks or {}
    gq = jnp.tile(q_norm_g, (1, H))
    gk = jnp.tile(k_norm_g, (1, H))
    rope = _rope_tables()

    h = _prenorm(xx, norm_g, mrow, b_ada, 256, "prenorm_x", after=weights.get("started"))
    hc = _prenorm(ctx2, norm_g, mrow_c, b_ada, L, "prenorm_ctx")
    jv = weights["jvec"]
    p = _in_proj_own(h, weights["own"], jv)
    btb = _bias_tiles(rpb2, after=p)
    w4, started = weights["near"](btb)
    p = _in_proj_block(h, w4, p, weights["first"], "in_proj_near", after=started)
    w4 = weights["near2"](w4, p)
    p = _in_proj_block(h, w4, p, weights["second"], "in_proj_near2")
    w4, started = weights["far"](w4, p)
    p = _in_proj_block(h, w4, p, jv ^ 3, "in_proj_far", after=started)
    pc = _ctx_proj(hc, w4)
    qr, qp, kr, vh = _qk_prep(p, gq, gk, rope)
    kc, vc = _ctx_prep(pc, gk)
    o = _attn_fwd(qr, qp, kr, vh, kc, vc, btb)
    started = weights["out_arrived"](o) if "out_arrived" in weights else None
    conv_g = _conv_fwd(p, conv_w_full, conv_b, after=started)
    w_out_full = weights["out"](conv_g)
    dy, dconv, dp8, do, g_w_out, dgate, loss_sum = _out_proj_loss(o, p, conv_g, w_out_full, xx, tgt, mrow, b_ada)
    started = hooks["g_w_out"](g_w_out) if "g_w_out" in hooks else None
    dp8, g_conv_w, g_conv_b = _conv_bwd(dconv, p, conv_w_full, conv_b, dp8, after=started)
    started = hooks["after_conv"](dp8) if "after_conv" in hooks else None
    dqr, dqp, dkr, dvh, dkc, dvc, dbtb = _attn_bwd(qr, qp, kr, vh, kc, vc, btb, do, after=started)
    dp8, g_gq, g_gk = _qk_bwd(dqr, dqp, dkr, dvh, p, gq, gk, rope, dp8)
    dkc_raw, dvc_m, g_gk_c = _ctx_bwd(dkc, dvc, pc, gk)
    first = hooks.get("first_half", jnp.zeros((1,), i32))
    g_first = _grad_w_in(h, dp8, hc, dkc_raw, dvc_m, first, "grad_w_in_first")
    started = hooks["g_w_in_first"](g_first) if "g_w_in_first" in hooks else None
    if "w_in_pair_sum" in hooks:
        g_second = None
        started = hooks["w_in_pair_sum"](functools.partial(_grad_w_in_pair_sum, h, dp8, hc, dkc_raw, dvc_m, 1 - first))
    else:
        g_second = _grad_w_in(h, dp8, hc, dkc_raw, dvc_m, 1 - first, "grad_w_in_second", after=started)
    dshift_c, dscale_c, dng_c = _dhc_sums(dkc_raw, dvc_m, w4, ctx2, norm_g, mrow_c, b_ada, after=started)
    g_rpb = _bias_bwd(dbtb, after=dshift_c)
    grad_x, dshift, dscale, dng = _dh_grad_x(dp8, w4, xx, dy, norm_g, mrow, b_ada, after=g_rpb)
    return dict(loss_sum=loss_sum, grad_x=grad_x, g_w_in=(g_first, g_second), g_w_out=g_w_out, g_conv_w=g_conv_w,
                g_conv_b=g_conv_b, g_rpb=g_rpb, g_gq=g_gq, g_gk=g_gk, g_gk_c=g_gk_c, dshift=dshift, dscale=dscale,
                dgate=dgate, dng=dng, dshift_c=dshift_c, dscale_c=dscale_c, dng_c=dng_c)


def _pair_sum_w_out(g, r, cvec):
    hr = D // 8

    def body(c_ref, g0, g1, g2, g3, r_ref, t32_ref, tb_ref):
        for q, g_ref in enumerate((g0, g1, g2, g3)):
            t = g_ref[...] + r_ref[q]
            t32_ref[q] = t
            tb_ref[q] = t.astype(bf16)

    gspecs = [pl.BlockSpec((hr, D), lambda i, c, q=q: (2 * q + c[0], 0)) for q in range(4)]
    full = pl.BlockSpec((4, hr, D), lambda i, c: (0, 0, 0))
    grid_spec = pltpu.PrefetchScalarGridSpec(num_scalar_prefetch=1, grid=(1,), in_specs=gspecs + [full],
                                             out_specs=(full, full))
    return _hbm_call(body, name="pair_sum_w_out", out_shape=(SDS((4, hr, D), f32), SDS((4, hr, D), bf16)),
                          grid_spec=grid_spec)(cvec, g, g, g, g, r)


def _chip_sum(t32, r2, jvec, name):
    rows = t32.shape[1]
    tr = min(rows, 128)

    def body(j_ref, t_ref, r_ref, u_ref):
        u_ref[...] = ((t_ref[...] + r_ref[0].astype(f32)) + r_ref[1].astype(f32)) + r_ref[2].astype(f32)

    grid_spec = pltpu.PrefetchScalarGridSpec(
        num_scalar_prefetch=1, grid=(rows // tr,),
        in_specs=[pl.BlockSpec((None, tr, D), lambda i, j: (j[0], i, 0)),
                  pl.BlockSpec((3, tr, D), lambda i, j: (0, i, 0))],
        out_specs=pl.BlockSpec((tr, D), lambda i, j: (i, 0)))
    return _hbm_call(body, name=name, out_shape=SDS((rows, D), f32), grid_spec=grid_spec)(jvec, t32, r2)


_PK = {}
_off = 0
for _name, _rows in (("dm", 24), ("dmc", 24), ("dng", 8), ("dng_c", 8), ("gq", 8), ("gk", 8), ("gk_c", 8),
                     ("rpb", H * N_DR), ("conv_b", 8), ("conv_w", 16), ("loss", 8)):
    _PK[_name] = (_off, _off + _rows)
    _off += _rows
PK_ROWS = _off
RS_B_ADA, RS_NORM_G, RS_GQ, RS_GK, RS_RPB, RS_CONV_B, RS_CONV_W, RS_DMC, RS_LOSS, RS_ROWS = (
    0, 24, 32, 40, 48, 168, 176, 192, 216, 224)


def _small_reduce(gathered):
    def body(g_ref, o_ref, dm_ref):
        a0 = _PK["dm"][0]
        dm_ref[...] = jnp.zeros_like(dm_ref)
        for b in range(8):
            for i in range(24):
                dm_ref[b:b + 1, 128 * i:128 * (i + 1)] = g_ref[b, a0 + i:a0 + i + 1, :]
        tot = g_ref[0]
        for b in range(1, 8):
            tot = tot + g_ref[b]

        def rows(name):
            a, z = _PK[name]
            return tot[a:z]

        o_ref[RS_B_ADA:RS_B_ADA + 24] = rows("dm") + rows("dmc")
        o_ref[RS_NORM_G:RS_NORM_G + 8] = rows("dng") + rows("dng_c")
        gq = jnp.broadcast_to(jnp.sum(rows("gq"), axis=0, keepdims=True), (8, 128))
        gk = jnp.broadcast_to(jnp.sum(rows("gk") + rows("gk_c"), axis=0, keepdims=True), (8, 128))
        o_ref[RS_GQ:RS_GQ + 8] = gq + pltpu.roll(gq, DH, 1)
        o_ref[RS_GK:RS_GK + 8] = gk + pltpu.roll(gk, DH, 1)
        o_ref[RS_RPB:RS_RPB + H * N_DR] = rows("rpb")
        o_ref[RS_CONV_B:RS_CONV_B + 8] = rows("conv_b")
        o_ref[RS_CONV_W:RS_CONV_W + 16] = rows("conv_w")
        dmc = rows("dmc")
        o_ref[RS_DMC:RS_DMC + 24] = dmc
        o_ref[RS_LOSS:RS_LOSS + 8] = rows("loss")
        for i in range(24):
            dm_ref[8:9, 128 * i:128 * (i + 1)] = dmc[i:i + 1]

    return _hbm_call(body, name="small_reduce", out_shape=(SDS((RS_ROWS, 128), f32), SDS((16, 3 * D), f32)),
                     in_specs=[VMEM_SPEC], out_specs=(VMEM_SPEC, VMEM_SPEC))(gathered)


def _w_ada_grad(sc16, dm16, w_ada_shard, jvec):
    ncol = w_ada_shard.shape[1]

    def body(j_ref, sc_ref, dm_ref, w_ref, g_ref, part_ref):
        dm = dm_ref[...]
        g_ref[...] = lax.dot_general(sc_ref[...], dm, (((0,), (0,)), ((), ())), precision=HIGHEST,
                                     preferred_element_type=f32)
        part_ref[...] = lax.dot_general(dm[8:16], w_ref[...], (((1,), (1,)), ((), ())), precision=HIGHEST,
                                        preferred_element_type=f32)

    fixed = lambda i, j: (0, 0)
    grid_spec = pltpu.PrefetchScalarGridSpec(
        num_scalar_prefetch=1, grid=(1,),
        in_specs=[pl.BlockSpec((16, D), fixed), pl.BlockSpec((16, ncol), lambda i, j: (0, j[0])),
                  pl.BlockSpec((D, ncol), fixed)],
        out_specs=(pl.BlockSpec((D, ncol), fixed), pl.BlockSpec((8, D), fixed)))
    return _pallas_call(body, name="w_ada_grad", out_shape=(SDS((D, ncol), f32), SDS((8, D), f32)),
                        grid_spec=grid_spec, compiler_params=_cp(40))(jvec, sc16, dm16, w_ada_shard)


def _c_ctx_grad(parts4, c_ctx):
    def body(p_ref, c_ref, o_ref):
        tot = ((p_ref[0] + p_ref[1]) + p_ref[2]) + p_ref[3]
        o_ref[...] = tot[0:1] * _dsilu(c_ref[...].reshape(1, D))

    return _pallas_call(body, name="c_ctx_grad", out_shape=SDS((1, D), f32), in_specs=[VMEM_SPEC, VMEM_SPEC],
                        out_specs=VMEM_SPEC)(parts4, c_ctx)


def _adamw(w, g, m, v, name, after=None):
    rows, cols = w.shape
    tr = 256 if rows % 256 == 0 else rows

    def body(w_ref, g_ref, m_ref, v_ref, after_ref, d_ref, m2_ref, v2_ref):
        gv = g_ref[...]
        m2 = ADAM_B1 * m_ref[...] + (1.0 - ADAM_B1) * gv
        v2 = ADAM_B2 * v_ref[...] + (1.0 - ADAM_B2) * jnp.square(gv)
        m_hat = m2 / (1.0 - ADAM_B1 ** ADAM_STEP)
        v_hat = v2 / (1.0 - ADAM_B2 ** ADAM_STEP)
        d_ref[...] = -ADAM_LR * (m_hat / (jnp.sqrt(v_hat) + ADAM_EPS) + ADAM_WD * w_ref[...])
        m2_ref[...] = m2
        v2_ref[...] = v2

    spec = pl.BlockSpec((tr, cols), lambda i: (i, 0))
    shp = SDS((rows, cols), f32)
    return _hbm_call(body, name=name, out_shape=(shp, shp, shp), grid=(rows // tr,), in_specs=[spec] * 4 + [ANY_SPEC],
                     out_specs=(spec, spec, spec))(w, g, m, v, g if after is None else after)


def _adamw_halves(w, g_mine, g_other, m, v, cvec, name, after=None):
    rows, cols = w.shape
    half = rows // 2
    tr = min(256, half)
    per_half = half // tr

    def body(c_ref, w_ref, ga_ref, gb_ref, m_ref, v_ref, after_ref, g_ref, d_ref, m2_ref, v2_ref):
        in_my_half = (pl.program_id(0) // per_half) == c_ref[0]
        gv = jnp.where(in_my_half, ga_ref[...], gb_ref[...])
        g_ref[...] = gv
        m2 = ADAM_B1 * m_ref[...] + (1.0 - ADAM_B1) * gv
        v2 = ADAM_B2 * v_ref[...] + (1.0 - ADAM_B2) * jnp.square(gv)
        m_hat = m2 / (1.0 - ADAM_B1 ** ADAM_STEP)
        v_hat = v2 / (1.0 - ADAM_B2 ** ADAM_STEP)
        d_ref[...] = -ADAM_LR * (m_hat / (jnp.sqrt(v_hat) + ADAM_EPS) + ADAM_WD * w_ref[...])
        m2_ref[...] = m2
        v2_ref[...] = v2

    full = pl.BlockSpec((tr, cols), lambda i, c: (i, 0))
    part = pl.BlockSpec((tr, cols), lambda i, c: (i % per_half, 0))
    shp = SDS((rows, cols), f32)
    grid_spec = pltpu.PrefetchScalarGridSpec(num_scalar_prefetch=1, grid=(rows // tr,),
                                             in_specs=[full, part, part, full, full, ANY_SPEC], out_specs=(full,) * 4)
    return _hbm_call(body, name=name, out_shape=(shp,) * 4, grid_spec=grid_spec)(
        cvec, w, g_mine, g_other, m, v, cvec if after is None else after)


def _adam_math(w, g, m, v):
    m2 = ADAM_B1 * m + (1.0 - ADAM_B1) * g
    v2 = ADAM_B2 * v + (1.0 - ADAM_B2) * jnp.square(g)
    m_hat = m2 / (1.0 - ADAM_B1 ** ADAM_STEP)
    v_hat = v2 / (1.0 - ADAM_B2 ** ADAM_STEP)
    return -ADAM_LR * (m_hat / (jnp.sqrt(v_hat) + ADAM_EPS) + ADAM_WD * w), m2, v2


def _adamw_small(red, g_c_ctx, jvec, ws, ms, vs):
    n = len(ws)

    def body(*refs):
        red_ref, gc_ref, j_ref = refs[:3]
        w_refs, m_refs, v_refs = refs[3:3 + n], refs[3 + n:3 + 2 * n], refs[3 + 2 * n:3 + 3 * n]
        outs = refs[3 + 3 * n:]
        g_out, d_out, m_out, v_out = outs[:n], outs[n:2 * n], outs[2 * n:3 * n], outs[3 * n:]
        chip = j_ref[0]
        lanes = lambda i: (slice(None), slice(128 * i, 128 * (i + 1)))
        row = lambda r0, i: (lambda: red_ref[r0 + i:r0 + i + 1, :])
        whole = (slice(None), slice(None))
        chunks = [
            [((slice(None),), lambda: gc_ref[...].reshape(D))],
            [(lanes(i), row(RS_B_ADA, i)) for i in range(3 * D // 128)],
            [(lanes(i), row(RS_NORM_G, i)) for i in range(D // 128)],
            [(whole, lambda: red_ref[RS_GQ:RS_GQ + 1, 0:DH])],
            [(whole, lambda: red_ref[RS_GK:RS_GK + 1, 0:DH])],
            [((dr,), (lambda dr=dr: red_ref[pl.ds(RS_RPB + dr, H, stride=N_DR), 0:N_DC])) for dr in range(N_DR)],
            [((r,), (lambda r=r: red_ref[pl.ds(RS_CONV_W + 4 * r + chip, 1), :])) for r in range(3)],
            [(lanes(i), row(RS_CONV_B, i)) for i in range(DC // 128)],
        ]
        for a in range(n):
            for idx, grad in chunks[a]:
                g = grad()
                d, m2, v2 = _adam_math(w_refs[a][idx], g, m_refs[a][idx], v_refs[a][idx])
                g_out[a][idx] = g
                d_out[a][idx] = d
                m_out[a][idx] = m2
                v_out[a][idx] = v2

    shapes = [SDS(w.shape, f32) for w in ws]
    res = _pallas_call(body, name="adamw_small", out_shape=shapes * 4,
                       in_specs=[VMEM_SPEC, VMEM_SPEC, SMEM_SPEC] + [VMEM_SPEC] * (3 * n),
                       out_specs=[VMEM_SPEC] * (4 * n))(red, g_c_ctx, jvec, *ws, *ms, *vs)
    return [list(res[k * n:(k + 1) * n]) for k in range(4)]


def _rows128(a):
    return a.reshape(-1, 128)


def kernel(x, c, ctx, c_ctx, w_ada, b_ada, norm_g, w_in, q_norm_g, k_norm_g, rpb, conv_w, conv_b, w_out, loss_target, m_c_ctx, m_w_ada, m_b_ada, m_norm_g, m_w_in, m_q_norm_g, m_k_norm_g, m_rpb, m_conv_w, m_conv_b, m_w_out, v_c_ctx, v_w_ada, v_b_ada, v_norm_g, v_w_in, v_q_norm_g, v_k_norm_g, v_rpb, v_conv_w, v_conv_b, v_w_out):
    xi, yi, ci = lax.axis_index("x"), lax.axis_index("y"), lax.axis_index("c")
    dev = 4 * xi + 2 * yi + ci
    chip = 2 * xi + yi
    cvec = jnp.reshape(ci, (1,)).astype(i32)
    jvec = jnp.reshape(chip, (1,)).astype(i32)
    w_ada_s = w_ada[0]
    ncol = w_ada_s.shape[1]

    gc = _split_start([_to_slot(c.reshape(8, 128), 8, dev)], [], 7, _gather8_copies, "gather_c_start")
    wo4c = _cast_to_slot(w_out[0], jvec, "cast_w_out", after=gc[3])
    w4c = _cast_to_slot(w_in[0], jvec, "cast_w_in", after=wo4c)
    (c8,), _ = _split_wait(gc[0], gc[1], [gc[2]], [], w4c, _gather8_copies, "gather_c_wait")
    cc = jnp.concatenate([c8.reshape(8, D), c_ctx.reshape(1, D), jnp.zeros((7, D), f32)], axis=0)
    m_shard, sc16 = _adaln_shard(cc, w_ada_s)

    conv_w_pad = jnp.pad(conv_w[0], ((0, 5), (0, 0)))
    gm = _split_start([_to_slot(m_shard, 4, chip), _to_slot(conv_w_pad, 4, chip)], [], 6, _gather4_copies,
                      "gather_mod_start")

    all_k = [(0, 0, 0), (0, 0, 1), (0, 0, 2)]
    sem_a, rem_a, w4s, token = _split_start([w4c], [], 1, _near_copies, "weights_near_start", after=gm[4])
    (m4, cw4), _ = _split_wait(gm[0], gm[1], [gm[2], gm[3]], [], token, _gather4_copies, "gather_mod_wait")
    m_full = jnp.transpose(m4, (1, 0, 2)).reshape(16, 4 * ncol)
    mrow = lax.dynamic_slice(m_full, (dev, 0), (1, 3 * D))
    mrow_c = m_full[8:9]
    conv_w_full = jnp.transpose(cw4[:, 0:3, :], (1, 0, 2)).reshape(3, DC)
    waves = {}

    def near(after):
        (w4w,), _ = _split_wait(sem_a, rem_a, [w4s], [], after, _near_copies, "weights_near_wait")
        sem_b, rem_b, w4b, started = _split_start([w4w], [], 2, _pass_relay_copies, "weights_pass_start")
        waves["pass"] = (sem_b, rem_b)
        return w4b, started

    def near2(w4, after):
        (w4w,), _ = _split_wait(*waves["pass"], [w4], [], after, _pass_copy, "weights_pass_wait")
        return w4w

    def far(w4, after):
        (w4w,), _ = _split_wait(*waves["pass"], [w4], [], after, _relay_copy, "weights_far_wait")
        w4x, sem_c, rem_c, wo4s, started = _forward_then_start(w4w, wo4c, all_k, "weights_far_forward_out_start")
        (w4f,), _ = _split_wait(sem_c, rem_c, [w4x], [], started, _diag_forward_copy, "weights_far_forward_wait")
        waves["out"] = (sem_c, rem_c, wo4s)
        return w4f, started

    def w_out_arrived(after):
        sem_c, rem_c, wo4s = waves["out"]
        (wow,) = _halves_wait(sem_c, rem_c, [wo4s], after, all_k, "weights_out_wait")
        sem_d, rem_d, wof, started = _split_start([wow], [], 3, _forward_copies, "weights_out_forward_start")
        waves["out_forward"] = (sem_d, rem_d, wof)
        return started

    def w_out_gathered(after):
        sem_d, rem_d, wof = waves["out_forward"]
        (wo,), _ = _split_wait(sem_d, rem_d, [wof], [], after, _forward_copies, "weights_out_forward_wait")
        return wo.reshape(D, D)

    weights = dict(own=w_in[0], jvec=jvec, started=token, first=jvec ^ (1 + cvec), second=jvec ^ (2 - cvec),
                   near=near, near2=near2, far=far, out_arrived=w_out_arrived, out=w_out_gathered)

    exchange = _exchange_copies
    pending = {}

    def on_g_w_out(g_w_out):
        out = _split_start([g_w_out], [SDS((4, D // 8, D), f32)], 4, exchange, "grad_out_pair_start")
        pending["ex_out"] = out
        return out[4]

    def after_conv(dp8):
        ssem_o, rsem_o, g_o, land_o, _ = pending["ex_out"]
        (g_o,), (ex_o,) = _split_wait(ssem_o, rsem_o, [g_o], [land_o], dp8, exchange, "grad_out_pair_wait")
        to32, tob = _pair_sum_w_out(g_o, ex_o, cvec)
        out = _split_start([tob], [SDS((3, D // 8, D), bf16)], 3, _scatter_copies, "grad_out_chip_start")
        pending["sc_out"] = (out, to32)
        return out[4]

    def on_g_w_in_first(g_first):
        out = _split_start([g_first], [SDS((4, D // 2, D), f32)], 4, _block_exchange_copies, "grad_pair_start")
        pending["ex"] = (out[0], out[1], [out[2]], [out[3]])
        return out[4]

    def on_w_in_pair_sum(pair_sum):
        ex_ssem, ex_rsem, ex_srcs, ex_lands = pending["ex"]
        t32, tb = pair_sum(ex_srcs[0], ex_lands[0], ex_ssem, ex_rsem)
        out = _split_start([tb], [SDS((3, D // 2, D), bf16)], 3, _scatter_copies, "grad_chip_start")
        pending["sc_in"] = (out, t32)
        return out[4]

    r = _local_step(x[0], ctx[0], loss_target[0], mrow, mrow_c, b_ada, norm_g, weights, q_norm_g, k_norm_g,
                    rpb[0], conv_w_full, conv_b,
                    dict(g_w_out=on_g_w_out, after_conv=after_conv, first_half=1 - cvec,
                         g_w_in_first=on_g_w_in_first, w_in_pair_sum=on_w_in_pair_sum))
    sc_in, t32 = pending["sc_in"]
    _, (r2,) = _split_wait(sc_in[0], sc_in[1], [sc_in[2]], [sc_in[3]], r["dng"], _scatter_copies, "grad_chip_wait")
    u_in = _chip_sum(t32, r2, jvec, "chip_sum_w_in")
    sc_o, to32 = pending["sc_out"]
    _, (ro2,) = _split_wait(sc_o[0], sc_o[1], [sc_o[2]], [sc_o[3]], u_in, _scatter_copies, "grad_out_chip_wait")
    u_out = _chip_sum(to32, ro2, jvec, "chip_sum_w_out")
    swap = _split_start([u_in, u_out], [SDS(u_in.shape, f32), SDS(u_out.shape, f32)], 2, _swap_copies,
                        "grad_pair_swap_start")

    dm = jnp.concatenate([r["dshift"], r["dscale"], r["dgate"]], axis=1)
    dmc = jnp.concatenate([r["dshift_c"], r["dscale_c"], jnp.zeros((1, D), f32)], axis=1)
    pack_parts = [_rows128(dm), _rows128(dmc), _rows128(r["dng"]), _rows128(r["dng_c"]), _rows128(r["g_gq"]),
                  _rows128(r["g_gk"]), _rows128(r["g_gk_c"]), r["g_rpb"].reshape(H * N_DR, 128),
                  _rows128(r["g_conv_b"]), _rows128(r["g_conv_w"][0:3]), jnp.pad(r["loss_sum"], ((0, 0), (0, 127)))]
    pack = jnp.concatenate([jnp.pad(p, ((0, -p.shape[0] % 8), (0, 0))) for p in pack_parts], axis=0)
    assert pack.shape[0] == PK_ROWS
    gs = _split_start([_to_slot(pack, 8, dev)], [], 7, _gather8_copies, "gather_small_start", after=swap[6])
    (u_in, u_out), (o_in, o_out) = _split_wait(swap[0], swap[1], swap[2:4], swap[4:6], gs[3], _swap_copies,
                                               "grad_pair_swap_wait")
    g_w_in_s, d_w_in, nm_w_in, nv_w_in = _adamw_halves(w_in[0], u_in, o_in, m_w_in[0], v_w_in[0], cvec, "adamw_w_in",
                                                       after=gs[3])
    g_w_out_s, d_w_out, nm_w_out, nv_w_out = _adamw_halves(w_out[0], u_out, o_out, m_w_out[0], v_w_out[0], cvec,
                                                           "adamw_w_out", after=nm_w_in)
    (gathered,), _ = _split_wait(gs[0], gs[1], [gs[2]], [], nm_w_out, _gather8_copies, "gather_small_wait")
    red, dm16 = _small_reduce(gathered)
    loss = red[RS_LOSS, 0] * (0.5 / D)

    g_w_ada_s, cpart = _w_ada_grad(sc16, dm16, w_ada_s, jvec)
    gcp = _split_start([_to_slot(cpart, 4, chip)], [], 3, _gather4_copies, "gather_c_ctx_parts_start")
    d_w_ada, nm_w_ada, nv_w_ada = _adamw(w_ada_s, g_w_ada_s, m_w_ada[0], v_w_ada[0], "adamw_w_ada", after=gcp[3])
    (cparts4,), _ = _split_wait(gcp[0], gcp[1], [gcp[2]], [], nm_w_ada, _gather4_copies, "gather_c_ctx_parts_wait")
    g_c_ctx = _c_ctx_grad(cparts4, c_ctx)

    t_rpb = lambda a: jnp.transpose(a, (0, 2, 1, 3)).reshape(N_DR, H, N_DC)
    t_cw = lambda a: jnp.transpose(a, (1, 0, 2))
    small = _adamw_small(
        red, g_c_ctx, jvec,
        [c_ctx, b_ada, norm_g, q_norm_g, k_norm_g, t_rpb(rpb), t_cw(conv_w), conv_b],
        [m_c_ctx, m_b_ada, m_norm_g, m_q_norm_g, m_k_norm_g, t_rpb(m_rpb), t_cw(m_conv_w), m_conv_b],
        [v_c_ctx, v_b_ada, v_norm_g, v_q_norm_g, v_k_norm_g, t_rpb(v_rpb), t_cw(v_conv_w), v_conv_b])
    for kind in small:
        kind[5] = jnp.transpose(kind[5].reshape(1, N_DR, H, N_DC), (0, 2, 1, 3))
        kind[6] = jnp.transpose(kind[6], (1, 0, 2))

    def ordered(kind, big_w_ada, big_w_in, big_w_out):
        s_c_ctx, s_b_ada, s_norm_g, s_q, s_k, s_rpb, s_conv_w, s_conv_b = small[kind]
        return [s_c_ctx, big_w_ada[None], s_b_ada, s_norm_g, big_w_in[None], s_q, s_k, s_rpb, s_conv_w,
                s_conv_b, big_w_out[None]]

    grads = ordered(0, g_w_ada_s, g_w_in_s, g_w_out_s)
    deltas = ordered(1, d_w_ada, d_w_in, d_w_out)
    new_m = ordered(2, nm_w_ada, nm_w_in, nm_w_out)
    new_v = ordered(3, nv_w_ada, nv_w_in, nv_w_out)
    return (loss, r["grad_x"][None], *grads, *deltas, *new_m, *new_v)
```

```python
import functools

import jax
import jax.numpy as jnp
import numpy as np
from jax import lax
from jax.experimental import pallas as pl
from jax.experimental.pallas import tpu as pltpu

f32, bf16, i32 = jnp.float32, jnp.bfloat16, jnp.int32
MESH = pl.DeviceIdType.MESH
HIGHEST = lax.Precision.HIGHEST

D = 1024
S = 2048
L = 256
GW = 64
ROWS = S // GW
H = 8
DH = 64
DA = H * DH
DC = 512
WIN_H, WIN_W = 8, 16
N_DR, N_DC = 2 * WIN_H - 1, 2 * WIN_W - 1
RMS_EPS = 1e-6
ROPE_THETA = 10000.0
QK_SCALE = DH ** -0.5
NEG = -1e30

QB = 128
NQB = S // QB
KR = 9
KB = KR * GW
TILE_GEOM = ((0, 0), (2, 0), (4, 0), (28, 23), (30, 23))
NT = len(TILE_GEOM)

ADAM_LR, ADAM_B1, ADAM_B2, ADAM_EPS, ADAM_WD, ADAM_STEP = 0.001, 0.9, 0.999, 1e-08, 0.01, 10

VMEM_SPEC = pl.BlockSpec(memory_space=pltpu.VMEM)
ANY_SPEC = pl.BlockSpec(memory_space=pl.ANY)
SMEM_SPEC = pl.BlockSpec(memory_space=pltpu.SMEM)
SDS = jax.ShapeDtypeStruct


_pallas_call = pl.pallas_call


def _hbm_call(body, *, out_shape, in_specs=None, out_specs=None, grid_spec=None, **kw):
    n_pre = 0
    if grid_spec is not None:
        ispecs, ospecs, n_pre = grid_spec.in_specs, grid_spec.out_specs, grid_spec.num_scalar_prefetch
        kw["grid_spec"] = grid_spec
    else:
        ispecs, ospecs = in_specs, out_specs
        kw.update(in_specs=in_specs, out_specs=out_specs)

    def blocked(spec):
        return isinstance(spec, pl.BlockSpec) and spec.block_shape is not None

    single = not isinstance(out_shape, (tuple, list))
    shapes = [out_shape] if single else list(out_shape)
    ospec_list = list(ospecs) if isinstance(ospecs, (tuple, list)) else [ospecs]
    shapes = [pltpu.HBM(s.shape, s.dtype) if blocked(sp) else s for s, sp in zip(shapes, ospec_list)]
    call = _pallas_call(body, out_shape=shapes[0] if single else tuple(shapes), **kw)

    def run(*args):
        arrays = [pltpu.with_memory_space_constraint(a, pltpu.HBM) if blocked(sp) else a
                  for a, sp in zip(args[n_pre:], ispecs)]
        return call(*args[:n_pre], *arrays)

    return run


def _cp(vmem_mb=None, **kw):
    if vmem_mb is not None:
        kw["vmem_limit_bytes"] = vmem_mb << 20
    return pltpu.CompilerParams(**kw)


def _silu(z):
    return z * jax.nn.sigmoid(z)


def _dsilu(z):
    sg = jax.nn.sigmoid(z)
    return sg * (1.0 + z * (1.0 - sg))


def _row_start(i):
    return min(max(i - WIN_H // 2, 0), ROWS - WIN_H)


def _my_pos():
    return lax.axis_index("x"), lax.axis_index("y"), lax.axis_index("c")


def _flip(v, bit):
    return 1 - v if bit else v


def _swap_copies(srcs, lands, ssem, rsem):
    x, y, c = _my_pos()
    return [pltpu.make_async_remote_copy(src_ref=srcs[a], dst_ref=lands[a], send_sem=ssem.at[a], recv_sem=rsem.at[a],
                                         device_id=(x, y, 1 - c), device_id_type=MESH) for a in range(len(srcs))]


HBM_SPEC = pl.BlockSpec(memory_space=pltpu.HBM)
SEM_SPEC = pl.BlockSpec(memory_space=pltpu.SEMAPHORE)
DATAFLOW = pltpu.SideEffectType.DATAFLOW_SIDE_EFFECTING


def _peer_chips(x, y, c):
    out = []
    for k in range(1, 4):
        px, py = _flip(x, (k >> 1) & 1), _flip(y, k & 1)
        out.append(((px, py, c), 2 * px + py))
    return out


def _half_copies(srcs, dsts, ssem, rsem, which):
    x, y, c = _my_pos()
    j = 2 * x + y
    peers = _peer_chips(x, y, c)
    pairs = []
    for pos, group, k in which:
        half = srcs[pos].shape[1] // 2
        mine = pl.ds(pl.multiple_of(c * half, 8), half)
        dev, pj = peers[k]
        sem = 3 * group + k
        send = pltpu.make_async_remote_copy(src_ref=srcs[pos].at[j, mine], dst_ref=dsts[pos].at[j, mine],
                                            send_sem=ssem.at[sem], recv_sem=rsem.at[sem], device_id=dev,
                                            device_id_type=MESH)
        arrive = pltpu.make_async_remote_copy(src_ref=srcs[pos].at[j, mine], dst_ref=dsts[pos].at[pj, mine],
                                              send_sem=ssem.at[sem], recv_sem=rsem.at[sem], device_id=dev,
                                              device_id_type=MESH)
        pairs.append((send, arrive))
    return pairs


def _halves_wait(ssem, rsem, bigs, after, which, name):
    nb = len(bigs)

    def body(*refs):
        b_in = refs[:nb]
        ssem_ref, rsem_ref = refs[nb], refs[nb + 1]
        for send, arrive in _half_copies(b_in, b_in, ssem_ref, rsem_ref, which):
            send.wait_send()
            arrive.wait_recv()

    return _hbm_call(
        body, name=name, out_shape=tuple(pltpu.HBM(b.shape, b.dtype) for b in bigs),
        in_specs=[HBM_SPEC] * nb + [SEM_SPEC, SEM_SPEC, ANY_SPEC], out_specs=tuple([HBM_SPEC] * nb),
        input_output_aliases={a: a for a in range(nb)}, compiler_params=_cp(has_side_effects=DATAFLOW),
    )(*bigs, ssem, rsem, after)


FORWARD_SEM = 3


def _diag_forward_copy(srcs, dsts, ssem, rsem):
    x, y, c = _my_pos()
    half = srcs[0].shape[1] // 2
    diag = 3 - (2 * x + y)
    mine = pl.ds(pl.multiple_of(c * half, 8), half)
    other = pl.ds(pl.multiple_of((1 - c) * half, 8), half)
    return [_Copy(srcs[0].at[diag, mine], dsts[0].at[diag, mine], dsts[0].at[diag, other], ssem.at[FORWARD_SEM],
                  rsem.at[FORWARD_SEM], (x, y, 1 - c))]


def _forward_then_start(fwd, big, order, name):
    def body(f_in, b_in, f_out, ssem, rsem, b_out, token):
        _diag_forward_copy([f_in], [f_out], ssem, rsem)[0].start()
        for send, _ in _half_copies([b_in], [b_out], ssem, rsem, order):
            send.start()
        token[...] = jnp.zeros_like(token)

    n_sem = FORWARD_SEM + 1
    out_shape = (pltpu.HBM(fwd.shape, fwd.dtype), pltpu.SemaphoreType.DMA((n_sem,)), pltpu.SemaphoreType.DMA((n_sem,)),
                 pltpu.HBM(big.shape, big.dtype), SDS((8, 128), f32))
    return _hbm_call(
        body, name=name, out_shape=out_shape, in_specs=[HBM_SPEC, HBM_SPEC],
        out_specs=(HBM_SPEC, SEM_SPEC, SEM_SPEC, HBM_SPEC, VMEM_SPEC), input_output_aliases={0: 0, 1: 3},
        compiler_params=_cp(has_side_effects=DATAFLOW),
    )(*[pltpu.with_memory_space_constraint(b, pltpu.HBM) for b in (fwd, big)])


def _cast_to_slot(w, jvec, name, after=None):
    rows, cols = w.shape
    tr = 256

    def body(j_ref, w_ref, after_ref, o_ref):
        o_ref[...] = w_ref[...].astype(bf16)

    grid_spec = pltpu.PrefetchScalarGridSpec(
        num_scalar_prefetch=1, grid=(rows // tr,),
        in_specs=[pl.BlockSpec((tr, cols), lambda i, j: (i, 0)), ANY_SPEC],
        out_specs=pl.BlockSpec((None, tr, cols), lambda i, j: (j[0], i, 0)))
    return _hbm_call(body, name=name, out_shape=SDS((4, rows, cols), bf16),
                     grid_spec=grid_spec)(jvec, w, jvec if after is None else after)


def _exchange_copies(srcs, lands, ssem, rsem):
    x, y, c = _my_pos()
    half = srcs[0].shape[0] // 8
    cps = []
    for jb in range(4):
        src = srcs[0].at[pl.ds(pl.multiple_of((2 * jb + 1 - c) * half, 8), half)]
        cps.append(pltpu.make_async_remote_copy(src_ref=src, dst_ref=lands[0].at[jb], send_sem=ssem.at[jb],
                                                recv_sem=rsem.at[jb], device_id=(x, y, 1 - c), device_id_type=MESH))
    return cps


def _block_exchange_copies(srcs, lands, ssem, rsem):
    x, y, c = _my_pos()
    return [pltpu.make_async_remote_copy(src_ref=srcs[0].at[jb], dst_ref=lands[0].at[jb], send_sem=ssem.at[jb],
                                         recv_sem=rsem.at[jb], device_id=(x, y, 1 - c), device_id_type=MESH)
            for jb in range(4)]


def _scatter_copies(srcs, lands, ssem, rsem):
    x, y, c = _my_pos()
    cps = []
    for a in range(len(srcs)):
        for k, (dev, pj) in enumerate(_peer_chips(x, y, c)):
            cps.append(pltpu.make_async_remote_copy(src_ref=srcs[a].at[pj], dst_ref=lands[a].at[k],
                                                    send_sem=ssem.at[3 * a + k], recv_sem=rsem.at[3 * a + k],
                                                    device_id=dev, device_id_type=MESH))
    return cps


class _Copy:
    def __init__(self, src, dst, arrive, ssem, rsem, dev):
        make = lambda to: pltpu.make_async_remote_copy(src_ref=src, dst_ref=to, send_sem=ssem, recv_sem=rsem,
                                                       device_id=dev, device_id_type=MESH)
        send, arrival = make(dst), make(arrive)
        self.start, self.wait_send, self.wait_recv = send.start, send.wait_send, arrival.wait_recv


def _toward(x, y, along_x):
    return x + along_x * (1 - 2 * x), y + (1 - along_x) * (1 - 2 * y)


NEAR_SEM = 6


def _near_copies(srcs, dsts, ssem, rsem):
    x, y, c = _my_pos()
    px, py = _toward(x, y, c)
    j = 2 * x + y
    return [_Copy(srcs[-1].at[j], dsts[-1].at[j], dsts[-1].at[2 * px + py], ssem.at[NEAR_SEM], rsem.at[NEAR_SEM],
                  (px, py, c))]


def _mod_and_near_copies(srcs, dsts, ssem, rsem):
    return _gather4_copies(srcs[0:2], dsts[0:2], ssem, rsem) + _near_copies(srcs, dsts, ssem, rsem)


def _pass_copy(srcs, dsts, ssem, rsem):
    x, y, c = _my_pos()
    px, py = _toward(x, y, c)
    qx, qy = _toward(x, y, 1 - c)
    got = 2 * px + py
    return [_Copy(srcs[0].at[got], dsts[0].at[got], dsts[0].at[2 * qx + qy], ssem.at[0], rsem.at[0], (x, y, 1 - c))]


def _relay_copy(srcs, dsts, ssem, rsem):
    x, y, c = _my_pos()
    px, py = _toward(x, y, c)
    qx, qy = _toward(x, y, 1 - c)
    half = srcs[0].shape[1] // 2
    mine = pl.ds(pl.multiple_of(c * half, 8), half)
    got, diag = 2 * px + py, 3 - (2 * x + y)
    return [_Copy(srcs[0].at[got, mine], dsts[0].at[got, mine], dsts[0].at[diag, mine], ssem.at[1], rsem.at[1],
                  (qx, qy, c))]


def _forward_copies(srcs, dsts, ssem, rsem):
    x, y, c = _my_pos()
    half = srcs[0].shape[1] // 2
    mine = pl.ds(pl.multiple_of(c * half, 8), half)
    other = pl.ds(pl.multiple_of((1 - c) * half, 8), half)
    return [_Copy(srcs[0].at[pj, mine], dsts[0].at[pj, mine], dsts[0].at[pj, other], ssem.at[k], rsem.at[k],
                  (x, y, 1 - c)) for k, (_, pj) in enumerate(_peer_chips(x, y, c))]


def _pass_relay_copies(srcs, dsts, ssem, rsem):
    return _pass_copy(srcs, dsts, ssem, rsem) + _relay_copy(srcs, dsts, ssem, rsem)


def _gather8_copies(srcs, dsts, ssem, rsem):
    x, y, c = _my_pos()
    me = 4 * x + 2 * y + c
    cps = []
    for a in range(len(srcs)):
        for k in range(1, 8):
            tgt = (_flip(x, (k >> 2) & 1), _flip(y, (k >> 1) & 1), _flip(c, k & 1))
            cps.append(_Copy(srcs[a].at[me], dsts[a].at[me], dsts[a].at[4 * tgt[0] + 2 * tgt[1] + tgt[2]],
                             ssem.at[7 * a + k - 1], rsem.at[7 * a + k - 1], tgt))
    return cps


def _gather4_copies(srcs, dsts, ssem, rsem):
    x, y, c = _my_pos()
    j = 2 * x + y
    cps = []
    for a in range(len(srcs)):
        for k, (dev, pj) in enumerate(_peer_chips(x, y, c)):
            cps.append(_Copy(srcs[a].at[j], dsts[a].at[j], dsts[a].at[pj], ssem.at[3 * a + k], rsem.at[3 * a + k], dev))
    return cps


def _to_slot(a, n, i):
    return lax.dynamic_update_slice(jnp.zeros((n,) + a.shape, a.dtype), a[None], (i,) + (0,) * a.ndim)


def _split_start(srcs, land_shapes, n_cp, make, name, after=None):
    ns, nl = len(srcs), len(land_shapes)
    n_in = ns + nl + (after is not None)

    def body(*refs):
        s_in = refs[:ns]
        ssem, rsem = refs[n_in], refs[n_in + 1]
        s_out = refs[n_in + 2:n_in + 2 + ns]
        l_out = refs[n_in + 2 + ns:n_in + 2 + ns + nl]
        token = refs[n_in + 2 + ns + nl]
        for cp in make(s_in, l_out if nl else s_out, ssem, rsem):
            cp.start()
        token[...] = jnp.zeros_like(token)

    lands = [pltpu.with_memory_space_constraint(lax.empty(sh.shape, sh.dtype), pltpu.HBM) for sh in land_shapes]
    out_shape = (pltpu.SemaphoreType.DMA((n_cp,)), pltpu.SemaphoreType.DMA((n_cp,)),
                 *[pltpu.HBM(b.shape, b.dtype) for b in srcs], *[pltpu.HBM(b.shape, b.dtype) for b in land_shapes],
                 SDS((8, 128), f32))
    return _hbm_call(
        body, name=name, out_shape=out_shape, in_specs=[HBM_SPEC] * (ns + nl) + [ANY_SPEC] * (after is not None),
        out_specs=(SEM_SPEC, SEM_SPEC, *[HBM_SPEC] * (ns + nl), VMEM_SPEC),
        input_output_aliases={i: 2 + i for i in range(ns + nl)}, compiler_params=_cp(has_side_effects=DATAFLOW),
    )(*[pltpu.with_memory_space_constraint(b, pltpu.HBM) for b in srcs], *lands, *([] if after is None else [after]))


def _split_wait(ssem, rsem, srcs, lands, after, make, name):
    ns, nl = len(srcs), len(lands)

    def body(*refs):
        s_in, l_in = refs[:ns], refs[ns:ns + nl]
        ssem_ref, rsem_ref = refs[ns + nl], refs[ns + nl + 1]
        for cp in make(s_in, l_in if nl else s_in, ssem_ref, rsem_ref):
            cp.wait_send()
            cp.wait_recv()

    outs = _hbm_call(
        body, name=name, out_shape=tuple(pltpu.HBM(b.shape, b.dtype) for b in (*srcs, *lands)),
        in_specs=[HBM_SPEC] * (ns + nl) + [SEM_SPEC, SEM_SPEC, ANY_SPEC], out_specs=tuple([HBM_SPEC] * (ns + nl)),
        input_output_aliases={i: i for i in range(ns + nl)}, compiler_params=_cp(has_side_effects=DATAFLOW),
    )(*srcs, *lands, ssem, rsem, after)
    return list(outs[:ns]), list(outs[ns:])


def _adaln_shard(cc, w_ada_shard):
    def body(c_ref, w_ref, m_ref, sc_ref):
        sc = _silu(c_ref[...])
        sc_ref[...] = sc
        m_ref[...] = jnp.dot(sc, w_ref[...], precision=HIGHEST, preferred_element_type=f32)

    return _hbm_call(
        body, name="adaln_shard", out_shape=(SDS((16, w_ada_shard.shape[1]), f32), SDS((16, D), f32)),
        in_specs=[VMEM_SPEC, VMEM_SPEC], out_specs=(VMEM_SPEC, VMEM_SPEC), compiler_params=_cp(32),
    )(cc, w_ada_shard)


def _prenorm(xx, norm_g, mrow, b_ada, tm, name, after=None):
    n = xx.shape[0]

    def body(x_ref, g_ref, m_ref, b_ref, after_ref, h_ref):
        x = x_ref[...]
        shift = m_ref[:, 0:D] + b_ref[:, 0:D]
        scale = m_ref[:, D:2 * D] + b_ref[:, D:2 * D]
        r = lax.rsqrt(jnp.mean(x * x, axis=-1, keepdims=True) + RMS_EPS)
        y = (x * r) * g_ref[...]
        h_ref[...] = (y * (1.0 + scale) + shift).astype(bf16)

    row = lambda i: (i, 0)
    fixed = lambda i: (0, 0)
    return _hbm_call(
        body, name=name, out_shape=SDS((n, D), bf16), grid=(n // tm,),
        in_specs=[pl.BlockSpec((tm, D), row), pl.BlockSpec((1, D), fixed), pl.BlockSpec((1, 3 * D), fixed),
                  pl.BlockSpec((1, 3 * D), fixed), ANY_SPEC],
        out_specs=pl.BlockSpec((tm, D), row),
    )(xx, norm_g, mrow, b_ada, b_ada if after is None else after)


def _in_proj_own(h, w_own, jvec):
    tm = 512

    def body(j_ref, h_ref, w_ref, p_ref):
        p_ref[...] = jnp.dot(h_ref[...], w_ref[...].astype(bf16), preferred_element_type=f32)

    grid_spec = pltpu.PrefetchScalarGridSpec(
        num_scalar_prefetch=1, grid=(S // tm,),
        in_specs=[pl.BlockSpec((tm, D), lambda i, j: (i, 0)), pl.BlockSpec((D, D), lambda i, j: (0, 0))],
        out_specs=pl.BlockSpec((tm, D), lambda i, j: (i, j[0])))
    return _hbm_call(body, name="in_proj_own", out_shape=SDS((S, 4 * D), f32), grid_spec=grid_spec,
                     compiler_params=_cp(40))(jvec, h, w_own)


def _in_proj_block(h, w4, p, bvec, name, after=None):
    tm = 512

    def body(b_ref, h_ref, w_ref, p_in_ref, after_ref, p_ref):
        p_ref[...] = jnp.dot(h_ref[...], w_ref[...], preferred_element_type=f32)

    grid_spec = pltpu.PrefetchScalarGridSpec(
        num_scalar_prefetch=1, grid=(S // tm,),
        in_specs=[pl.BlockSpec((tm, D), lambda i, b: (i, 0)), pl.BlockSpec((None, D, D), lambda i, b: (b[0], 0, 0)),
                  ANY_SPEC, ANY_SPEC],
        out_specs=pl.BlockSpec((tm, D), lambda i, b: (i, b[0])))
    return _hbm_call(body, name=name, out_shape=SDS((S, 4 * D), f32), grid_spec=grid_spec,
                     input_output_aliases={3: 0})(bvec, h, w4, p, bvec if after is None else after)


def _ctx_proj(hc, w4):
    def body(h_ref, w0_ref, w1_ref, p_ref):
        hv = h_ref[...]
        p_ref[:, 0:DA] = jnp.dot(hv, w0_ref[:, DA:2 * DA], preferred_element_type=f32)
        p_ref[:, DA:2 * DA] = jnp.dot(hv, w1_ref[:, 0:DA], preferred_element_type=f32)

    return _hbm_call(
        body, name="ctx_proj", out_shape=SDS((L, 2 * DA), f32), grid=(1,),
        in_specs=[pl.BlockSpec((L, D), lambda i: (0, 0)), pl.BlockSpec((None, D, D), lambda i: (0, 0, 0)),
                  pl.BlockSpec((None, D, D), lambda i: (1, 0, 0))],
        out_specs=pl.BlockSpec((L, 2 * DA), lambda i: (0, 0)),
    )(hc, w4, w4)


def _head_ones():
    r = lax.broadcasted_iota(i32, (DA, DA), 0) // DH
    c = lax.broadcasted_iota(i32, (DA, DA), 1) // DH
    return (r == c).astype(bf16)


def _head_sum(v, ones_bd):
    hi = v.astype(bf16)
    lo = (v - hi.astype(f32)).astype(bf16)
    return jnp.dot(hi, ones_bd, preferred_element_type=f32) + jnp.dot(lo, ones_bd, preferred_element_type=f32)


def _swap16(v):
    lane = lax.broadcasted_iota(i32, v.shape, 1)
    return jnp.where((lane & 31) < 16, pltpu.roll(v, DA - 16, 1), pltpu.roll(v, 16, 1))


def _rope_block(ct_ref, rt_ref, tm):
    rows = [jnp.tile(rt_ref[8 * j:8 * j + 8, :], (GW // 8, 1)) for j in range(tm // GW)]
    return jnp.tile(ct_ref[...], (tm // GW, 1)) + jnp.concatenate(rows, axis=0)


def _rope_specs(tm):
    col = pl.BlockSpec((GW, DA), lambda i: (0, 0))
    row = pl.BlockSpec((8 * tm // GW, DA), lambda i: (i, 0))
    return [col, row, col, row]


def _qk_prep(p, gq, gk, rope):
    tm = 256

    def body(qk_ref, v_ref, gq_ref, gk_ref, cc_ref, cr_ref, sc_ref, sr_ref, qr_ref, qp_ref, kr_ref, vh_ref):
        ones_bd = _head_ones()
        cs, sn = _rope_block(cc_ref, cr_ref, tm), _rope_block(sc_ref, sr_ref, tm)
        q = qk_ref[:, 0:DA]
        k = qk_ref[:, DA:2 * DA]
        yq = (q * lax.rsqrt(_head_sum(q * q, ones_bd) * (1.0 / DH) + RMS_EPS)) * gq_ref[...]
        yk = (k * lax.rsqrt(_head_sum(k * k, ones_bd) * (1.0 / DH) + RMS_EPS)) * gk_ref[...]
        qr = (yq * cs + _swap16(yq) * sn) * QK_SCALE
        qp = yq * QK_SCALE
        kr = yk * cs + _swap16(yk) * sn
        vv = v_ref[...]
        for hh in range(H):
            sl = slice(hh * DH, (hh + 1) * DH)
            qr_ref[hh] = qr[:, sl].astype(bf16)
            qp_ref[hh] = qp[:, sl].astype(bf16)
            kr_ref[hh] = kr[:, sl].astype(bf16)
            vh_ref[hh] = vv[:, sl].astype(bf16)

    hm = SDS((H, S, DH), bf16)
    hspec = pl.BlockSpec((H, tm, DH), lambda i: (0, i, 0))
    fixed = lambda i: (0, 0)
    return _hbm_call(
        body, name="qk_prep", out_shape=(hm, hm, hm, hm), grid=(S // tm,),
        in_specs=[pl.BlockSpec((tm, 2 * DA), lambda i: (i, 0)), pl.BlockSpec((tm, DA), lambda i: (i, 2)),
                  pl.BlockSpec((1, DA), fixed), pl.BlockSpec((1, DA), fixed)] + _rope_specs(tm),
        out_specs=(hspec, hspec, hspec, hspec),
    )(p, p, gq, gk, *rope)


def _ctx_prep(pc, gk):
    def body(p_ref, gk_ref, kc_ref, vc_ref):
        ones_bd = _head_ones()
        k = p_ref[:, 0:DA]
        yk = (k * lax.rsqrt(_head_sum(k * k, ones_bd) * (1.0 / DH) + RMS_EPS)) * gk_ref[...]
        vv = p_ref[:, DA:2 * DA]
        for hh in range(H):
            sl = slice(hh * DH, (hh + 1) * DH)
            kc_ref[hh] = yk[:, sl].astype(bf16)
            vc_ref[hh] = vv[:, sl].astype(bf16)

    hm = SDS((H, L, DH), bf16)
    return _hbm_call(
        body, name="ctx_prep", out_shape=(hm, hm), in_specs=[VMEM_SPEC, VMEM_SPEC], out_specs=(VMEM_SPEC, VMEM_SPEC),
    )(pc, gk)


def _tile_pieces():
    out = []
    for (i0, u0) in TILE_GEOM:
        rows = []
        for j in range(2):
            i = i0 + j
            rs = _row_start(i)
            rows.append([(u0 + u - i + WIN_H - 1) if rs <= u0 + u < rs + WIN_H else None for u in range(KR)])
        out.append(rows)
    return out


def _bias_prep(rpb_rev_pad, after=None):
    pieces = _tile_pieces()

    def body(r_ref, after_ref, o_ref):
        rp = r_ref[...]
        xs = jnp.concatenate([pltpu.roll(jnp.broadcast_to(rp[dr:dr + 1, :], (GW, 128)), 128 - (WIN_W - 1), 1,
                                         stride=1, stride_axis=0) for dr in range(N_DR)], axis=0)
        row = lax.broadcasted_iota(i32, xs.shape, 0)
        lane = lax.broadcasted_iota(i32, xs.shape, 1)
        k = row & (GW - 1)
        c0 = jnp.clip(lane - WIN_W // 2, 0, GW - WIN_W)
        xs = jnp.where((k >= c0) & (k < c0 + WIN_W), xs, NEG)
        neg = jnp.full((GW, GW), NEG, f32)
        for t in range(NT):
            for j in range(2):
                for u in range(KR):
                    dr = pieces[t][j][u]
                    piece = neg if dr is None else xs[dr * GW:(dr + 1) * GW, 0:GW]
                    o_ref[t, u * GW:(u + 1) * GW, j * GW:(j + 1) * GW] = piece

    return _hbm_call(
        body, name="bias_prep", out_shape=SDS((H, NT, KB, QB), f32), grid=(H,),
        in_specs=[pl.BlockSpec((None, N_DR, 128), lambda h: (h, 0, 0)), ANY_SPEC],
        out_specs=pl.BlockSpec((None, NT, KB, QB), lambda h: (h, 0, 0, 0)),
    )(rpb_rev_pad, rpb_rev_pad if after is None else after)


def _bias_tiles(rpb2, after=None):
    return _bias_prep(jnp.pad(rpb2[:, :, ::-1], ((0, 0), (0, 0), (0, 128 - N_DC))), after)


def _block_geom(b):
    qs = b * QB
    ks = min(max(2 * b - 4, 0), ROWS - KR) * GW
    t = b if b < 2 else (b - (NQB - NT) if b > NQB - 3 else 2)
    return qs, ks, t


def _tt(a, b):
    return lax.dot_general(a, b, (((1,), (1,)), ((), ())), preferred_element_type=f32)


def _tn(a, b):
    return lax.dot_general(a, b, (((0,), (0,)), ((), ())), preferred_element_type=f32)


def _softmax_t(s_lat, s_ctx):
    m = jnp.maximum(jnp.max(s_lat, axis=0, keepdims=True), jnp.max(s_ctx, axis=0, keepdims=True))
    e_lat = jnp.exp(s_lat - m)
    e_ctx = jnp.exp(s_ctx - m)
    inv = 1.0 / (jnp.sum(e_lat, axis=0, keepdims=True) + jnp.sum(e_ctx, axis=0, keepdims=True))
    return e_lat * inv, e_ctx * inv


def _staged(n_blocks, stages):
    held = [dict() for _ in stages]
    for step in range(n_blocks + len(stages) - 1):
        for s, fn in enumerate(stages):
            b = step - s
            if 0 <= b < n_blocks:
                held[s][b] = fn(b) if s == 0 else fn(b, held[s - 1].pop(b))


def _attn_fwd(qr, qp, kr, vh, kc, vc, btt):
    def body(qr_ref, qp_ref, kr_ref, v_ref, kc_ref, vc_ref, bt_ref, o_ref):
        kcv, vcv = kc_ref[...], vc_ref[...]

        def scores(b):
            qs, ks, t = _block_geom(b)
            return (_tt(kr_ref[ks:ks + KB, :], qr_ref[qs:qs + QB, :]) + bt_ref[t], _tt(kcv, qp_ref[qs:qs + QB, :]))

        def probs(b, sc):
            p_lat, p_ctx = _softmax_t(*sc)
            return p_lat.astype(bf16), p_ctx.astype(bf16)

        def values(b, p):
            qs, ks, _ = _block_geom(b)
            o_ref[qs:qs + QB, :] = _tn(p[0], v_ref[ks:ks + KB, :]) + _tn(p[1], vcv)

        _staged(NQB, (scores, probs, values))

    sq = pl.BlockSpec((None, S, DH), lambda h: (h, 0, 0))
    sc = pl.BlockSpec((None, L, DH), lambda h: (h, 0, 0))
    return _hbm_call(
        body, name="attn_fwd", out_shape=SDS((H, S, DH), f32), grid=(H,),
        in_specs=[sq, sq, sq, sq, sc, sc, pl.BlockSpec((None, NT, KB, QB), lambda h: (h, 0, 0, 0))],
        out_specs=sq, compiler_params=_cp(48),
    )(qr, qp, kr, vh, kc, vc, btt)


def _shift_rows(v, down):
    n = v.shape[0]
    row = lax.broadcasted_iota(i32, v.shape, 0)
    if down:
        return jnp.where(row == 0, 0.0, pltpu.roll(v, 1, 0))
    return jnp.where(row == n - 1, 0.0, pltpu.roll(v, n - 1, 0))


def _conv_specs():
    col = lambda off: pl.BlockSpec((S, 128), lambda i, off=off: (0, off + i))
    return [col(16), col(20), col(24), col(28), pl.BlockSpec((3, 128), lambda i: (0, i)),
            pl.BlockSpec((1, 128), lambda i: (0, i))]


def _conv_fwd(p, conv_w, conv_b, after=None):
    def body(u_ref, bg_ref, cg_ref, zc_ref, w_ref, b_ref, after_ref, o_ref):
        cu = cg_ref[...] * u_ref[...]
        cv = b_ref[...] + _shift_rows(cu, True) * w_ref[0:1, :]
        cv = cv + cu * w_ref[1:2, :]
        cv = cv + _shift_rows(cu, False) * w_ref[2:3, :]
        o_ref[...] = ((bg_ref[...] * cv) * _silu(zc_ref[...])).astype(bf16)

    return _hbm_call(
        body, name="conv_fwd", out_shape=SDS((S, DC), bf16), grid=(DC // 128,),
        in_specs=_conv_specs() + [ANY_SPEC], out_specs=pl.BlockSpec((S, 128), lambda i: (0, i)),
        compiler_params=_cp(40),
    )(p, p, p, p, conv_w, conv_b, conv_b if after is None else after)


DP_Q, DP_K, DP_V, DP_ZA, DP_U, DP_BG, DP_CG, DP_ZC = range(8)


def _out_proj_loss(o, p, conv_g, w_out, xx, tgt, mrow, b_ada):
    tm = 256

    def body(o_ref, za_ref, c_ref, w_ref, x_ref, t_ref, m_ref, b_ref,
             dy_ref, dconv_ref, dp_ref, do_ref, gwo_ref, dgate_ref, loss_ref):
        k = pl.program_id(0)

        @pl.when(k == 0)
        def _():
            gwo_ref[...] = jnp.zeros_like(gwo_ref)
            dgate_ref[...] = jnp.zeros_like(dgate_ref)
            loss_ref[0, 0] = 0.0

        gate = m_ref[:, 2 * D:3 * D] + b_ref[:, 2 * D:3 * D]
        za = za_ref[...]
        sz = _silu(za)
        om = _merge_heads(o_ref)
        av, cv = (om * sz).astype(bf16), c_ref[...]
        mo = jnp.dot(av, w_ref[0:DA, :], preferred_element_type=f32)
        mo = mo + jnp.dot(cv, w_ref[DA:DA + DC, :], preferred_element_type=f32)
        y = x_ref[...] + gate * mo
        diff = y - t_ref[...]
        loss_ref[0, 0] += jnp.sum(diff * diff)
        dy = diff * (1.0 / D)
        dy_ref[...] = dy
        dgate_ref[...] += jnp.sum(dy * mo, axis=0, keepdims=True)
        dmo = (dy * gate).astype(bf16)
        dmix = _tt(dmo, w_ref[...])
        dattn = dmix[:, 0:DA]
        dconv_ref[...] = dmix[:, DA:DA + DC]
        a = dattn * sz
        for hh in range(H):
            do_ref[hh] = a[:, hh * DH:(hh + 1) * DH].astype(bf16)
        dp_ref[...] = ((dattn * _dsilu(za)) * om).astype(bf16)
        gwo_ref[0:DA, :] += _tn(av, dmo)
        gwo_ref[DA:DA + DC, :] += _tn(cv, dmo)

    row = lambda i: (i, 0)
    fixed = lambda i: (0, 0)
    hspec = pl.BlockSpec((H, tm, DH), lambda i: (0, i, 0))
    return _hbm_call(
        body, name="out_proj_loss",
        out_shape=(SDS((S, D), f32), SDS((S, DC), f32), SDS((8, S, DA), bf16), SDS((H, S, DH), bf16),
                   SDS((D, D), f32), SDS((1, D), f32), SDS((1, 1), f32)),
        grid=(S // tm,),
        in_specs=[hspec, pl.BlockSpec((tm, DA), lambda i: (i, 3)), pl.BlockSpec((tm, DC), row),
                  pl.BlockSpec((D, D), fixed), pl.BlockSpec((tm, D), row), pl.BlockSpec((tm, D), row),
                  pl.BlockSpec((1, 3 * D), fixed), pl.BlockSpec((1, 3 * D), fixed)],
        out_specs=(pl.BlockSpec((tm, D), row), pl.BlockSpec((tm, DC), row),
                   pl.BlockSpec((None, tm, DA), lambda i: (DP_ZA, i, 0)), hspec, pl.BlockSpec((D, D), fixed),
                   pl.BlockSpec((1, D), fixed), SMEM_SPEC),
        compiler_params=_cp(56, dimension_semantics=("arbitrary",)),
    )(o, p, conv_g, w_out, xx, tgt, mrow, b_ada)


def _conv_bwd(dconv, p, conv_w, conv_b, dp8, after=None):
    def body(d_ref, u_ref, bg_ref, cg_ref, zc_ref, w_ref, b_ref, dp_in_ref, after_ref, dp_ref, gw_ref, gb_ref):
        du_ref, dbg_ref, dcg_ref, dzc_ref = dp_ref.at[0], dp_ref.at[1], dp_ref.at[2], dp_ref.at[3]
        dconv = d_ref[...]
        u, bg, cg, zc = u_ref[...], bg_ref[...], cg_ref[...], zc_ref[...]
        w0, w1, w2 = w_ref[0:1, :], w_ref[1:2, :], w_ref[2:3, :]
        cu = cg * u
        cu_m, cu_p = _shift_rows(cu, True), _shift_rows(cu, False)
        cv = b_ref[...] + cu_m * w0
        cv = cv + cu * w1
        cv = cv + cu_p * w2
        sz = _silu(zc)
        dbg_ref[...] = ((dconv * sz) * cv).astype(bf16)
        dzc_ref[...] = ((dconv * (bg * cv)) * _dsilu(zc)).astype(bf16)
        dcv = (dconv * sz) * bg
        gb_ref[...] = jnp.sum(dcv, axis=0, keepdims=True)
        gw_ref[0:1, :] = jnp.sum(dcv * cu_m, axis=0, keepdims=True)
        gw_ref[1:2, :] = jnp.sum(dcv * cu, axis=0, keepdims=True)
        gw_ref[2:3, :] = jnp.sum(dcv * cu_p, axis=0, keepdims=True)
        gw_ref[3:8, :] = jnp.zeros((5, 128), f32)
        dcu = _shift_rows(dcv, False) * w0 + dcv * w1 + _shift_rows(dcv, True) * w2
        dcg_ref[...] = (dcu * u).astype(bf16)
        du_ref[...] = (dcu * cg).astype(bf16)

    return _hbm_call(
        body, name="conv_bwd", out_shape=(SDS((8, S, DC), bf16), SDS((8, DC), f32), SDS((1, DC), f32)),
        grid=(DC // 128,),
        in_specs=[pl.BlockSpec((S, 128), lambda i: (0, i))] + _conv_specs() + [ANY_SPEC, ANY_SPEC],
        out_specs=(pl.BlockSpec((4, S, 128), lambda i: (DP_U // 4, 0, i)), pl.BlockSpec((8, 128), lambda i: (0, i)),
                   pl.BlockSpec((1, 128), lambda i: (0, i))),
        input_output_aliases={7: 0}, compiler_params=_cp(48),
    )(dconv, p, p, p, p, conv_w, conv_b, dp8, conv_b if after is None else after)


def _attn_bwd(qr, qp, kr, vh, kc, vc, btt, do, after=None):
    def body(qr_ref, qp_ref, kr_ref, v_ref, kc_ref, vc_ref, bt_ref, do_ref, after_ref,
             dqr_ref, dqp_ref, dkr_ref, dv_ref, dkc_ref, dvc_ref, dbt_ref):
        kcv, vcv = kc_ref[...], vc_ref[...]
        dkr_ref[...] = jnp.zeros_like(dkr_ref)
        dv_ref[...] = jnp.zeros_like(dv_ref)
        dbt_ref[...] = jnp.zeros_like(dbt_ref)
        ctx_acc = {}

        def products(b):
            qs, ks, t = _block_geom(b)
            dob = do_ref[qs:qs + QB, :]
            s_lat = _tt(kr_ref[ks:ks + KB, :], qr_ref[qs:qs + QB, :]) + bt_ref[t]
            s_ctx = _tt(kcv, qp_ref[qs:qs + QB, :])
            return s_lat, s_ctx, _tt(v_ref[ks:ks + KB, :], dob), _tt(vcv, dob)

        def score_grads(b, x):
            s_lat, s_ctx, dp_lat, dp_ctx = x
            p_lat, p_ctx = _softmax_t(s_lat, s_ctx)
            delta = jnp.sum(p_lat * dp_lat, axis=0, keepdims=True) + jnp.sum(p_ctx * dp_ctx, axis=0, keepdims=True)
            ds_lat = p_lat * (dp_lat - delta)
            ds_ctx = p_ctx * (dp_ctx - delta)
            return ds_lat, ds_lat.astype(bf16), ds_ctx.astype(bf16), p_lat.astype(bf16), p_ctx.astype(bf16)

        def operand_grads(b, y):
            qs, ks, t = _block_geom(b)
            ds_lat, dsb_lat, dsb_ctx, pb_lat, pb_ctx = y
            qrb, qpb, dob = qr_ref[qs:qs + QB, :], qp_ref[qs:qs + QB, :], do_ref[qs:qs + QB, :]
            dbt_ref[t] += ds_lat
            dqr_ref[qs:qs + QB, :] = _tn(dsb_lat, kr_ref[ks:ks + KB, :])
            dqp_ref[qs:qs + QB, :] = _tn(dsb_ctx, kcv)
            dkr_ref[ks:ks + KB, :] += jnp.dot(dsb_lat, qrb, preferred_element_type=f32)
            dv_ref[ks:ks + KB, :] += jnp.dot(pb_lat, dob, preferred_element_type=f32)
            dkc = jnp.dot(dsb_ctx, qpb, preferred_element_type=f32)
            dvc = jnp.dot(pb_ctx, dob, preferred_element_type=f32)
            ctx_acc["k"] = dkc if b == 0 else ctx_acc["k"] + dkc
            ctx_acc["v"] = dvc if b == 0 else ctx_acc["v"] + dvc

        _staged(NQB, (products, score_grads, operand_grads))
        dkc_ref[...] = ctx_acc["k"]
        dvc_ref[...] = ctx_acc["v"]

    sq = pl.BlockSpec((None, S, DH), lambda h: (h, 0, 0))
    sc = pl.BlockSpec((None, L, DH), lambda h: (h, 0, 0))
    sb = pl.BlockSpec((None, NT, KB, QB), lambda h: (h, 0, 0, 0))
    big, ctxs = SDS((H, S, DH), f32), SDS((H, L, DH), f32)
    return _hbm_call(
        body, name="attn_bwd", out_shape=(big, big, big, big, ctxs, ctxs, SDS((H, NT, KB, QB), f32)), grid=(H,),
        in_specs=[sq, sq, sq, sq, sc, sc, sb, sq, ANY_SPEC], out_specs=(sq, sq, sq, sq, sc, sc, sb),
        compiler_params=_cp(56),
    )(qr, qp, kr, vh, kc, vc, btt, do, do if after is None else after)


def _bias_bwd(dbtt, after=None):
    pieces = _tile_pieces()

    def body(d_ref, after_ref, o_ref, scr):
        scr[...] = jnp.zeros_like(scr)
        acc = [None] * N_DR
        for t in range(NT):
            for j in range(2):
                for u in range(KR):
                    dr = pieces[t][j][u]
                    if dr is None:
                        continue
                    piece = d_ref[t, u * GW:(u + 1) * GW, j * GW:(j + 1) * GW]
                    acc[dr] = piece if acc[dr] is None else acc[dr] + piece
        a = lax.broadcasted_iota(i32, (GW, GW), 0)
        b = lax.broadcasted_iota(i32, (GW, GW), 1)
        flip = (a + b == GW - 1).astype(f32)
        for dr in range(N_DR):
            scr[dr * GW:(dr + 1) * GW, 0:GW] = jnp.dot(acc[dr], flip, precision=HIGHEST, preferred_element_type=f32)
        xs = jnp.concatenate([pltpu.roll(scr[dr * GW:(dr + 1) * GW, :], 128 + (WIN_W - 1) - (GW - 1), 1,
                                         stride=1, stride_axis=0) for dr in range(N_DR)], axis=0)
        tot = jnp.sum(xs.reshape(N_DR, GW, 128), axis=1)
        lane = lax.broadcasted_iota(i32, tot.shape, 1)
        o_ref[...] = jnp.where(lane < N_DC, tot, 0.0)

    return _hbm_call(
        body, name="bias_bwd", out_shape=SDS((H, N_DR, 128), f32), grid=(H,),
        in_specs=[pl.BlockSpec((None, NT, KB, QB), lambda h: (h, 0, 0, 0)), ANY_SPEC],
        out_specs=pl.BlockSpec((None, N_DR, 128), lambda h: (h, 0, 0)),
        scratch_shapes=[pltpu.VMEM((N_DR * GW, 128), f32)],
    )(dbtt, dbtt if after is None else after)


def _merge_heads(ref):
    return jnp.concatenate([ref[hh] for hh in range(H)], axis=1)


def _head_norm_bwd(xraw, gain, dy, ones_bd):
    r = lax.rsqrt(_head_sum(xraw * xraw, ones_bd) * (1.0 / DH) + RMS_EPS)
    xh = xraw * r
    gdy = dy * gain
    dx = r * (gdy - xh * (_head_sum(xh * gdy, ones_bd) * (1.0 / DH)))
    return dx, jnp.sum(dy * xh, axis=0, keepdims=True)


def _qk_bwd(dqr, dqp, dkr, dvh, p, gq, gk, rope, dp8):
    tm = 256

    def body(dqr_ref, dqp_ref, dkr_ref, dv_ref, qk_ref, gq_ref, gk_ref, cc_ref, cr_ref, sc_ref, sr_ref, dp_in_ref,
             dp_ref, ggq_ref, ggk_ref):
        dq_ref, dk_ref, dvo_ref = dp_ref.at[DP_Q], dp_ref.at[DP_K], dp_ref.at[DP_V]

        @pl.when(pl.program_id(0) == 0)
        def _():
            ggq_ref[...] = jnp.zeros_like(ggq_ref)
            ggk_ref[...] = jnp.zeros_like(ggk_ref)

        ones_bd = _head_ones()
        cs, sn = _rope_block(cc_ref, cr_ref, tm), _rope_block(sc_ref, sr_ref, tm)
        a = _merge_heads(dqr_ref)
        dyq = ((a * cs - _swap16(a) * sn) + _merge_heads(dqp_ref)) * QK_SCALE
        bk = _merge_heads(dkr_ref)
        dyk = bk * cs - _swap16(bk) * sn
        dq, gq_part = _head_norm_bwd(qk_ref[:, 0:DA], gq_ref[...], dyq, ones_bd)
        dk, gk_part = _head_norm_bwd(qk_ref[:, DA:2 * DA], gk_ref[...], dyk, ones_bd)
        dq_ref[...] = dq.astype(bf16)
        dk_ref[...] = dk.astype(bf16)
        dvo_ref[...] = _merge_heads(dv_ref).astype(bf16)
        ggq_ref[...] += gq_part
        ggk_ref[...] += gk_part

    hspec = pl.BlockSpec((H, tm, DH), lambda i: (0, i, 0))
    fixed = pl.BlockSpec((1, DA), lambda i: (0, 0))
    return _hbm_call(
        body, name="qk_bwd", out_shape=(SDS((8, S, DA), bf16), SDS((1, DA), f32), SDS((1, DA), f32)), grid=(S // tm,),
        in_specs=[hspec, hspec, hspec, hspec, pl.BlockSpec((tm, 2 * DA), lambda i: (i, 0)), fixed, fixed]
        + _rope_specs(tm) + [ANY_SPEC],
        out_specs=(pl.BlockSpec((3, tm, DA), lambda i: (0, i, 0)), fixed, fixed), input_output_aliases={11: 0},
        compiler_params=_cp(40, dimension_semantics=("arbitrary",)),
    )(dqr, dqp, dkr, dvh, p, gq, gk, *rope, dp8)


def _ctx_bwd(dkc, dvc, pc, gk):
    def body(dkc_ref, dvc_ref, p_ref, gk_ref, dk_ref, dv_ref, ggk_ref):
        ones_bd = _head_ones()
        dk, gk_part = _head_norm_bwd(p_ref[:, 0:DA], gk_ref[...], _merge_heads(dkc_ref), ones_bd)
        dk_ref[...] = dk.astype(bf16)
        dv_ref[...] = _merge_heads(dvc_ref).astype(bf16)
        ggk_ref[...] = gk_part

    piece = SDS((L, DA), bf16)
    return _hbm_call(
        body, name="ctx_bwd", out_shape=(piece, piece, SDS((1, DA), f32)), in_specs=[VMEM_SPEC] * 4,
        out_specs=(VMEM_SPEC,) * 3,
    )(dkc, dvc, pc, gk)


def _grad_w_in(h, dp8, hc, dkc_raw, dvc_m, half, name, after=None):
    def body(half_ref, h_ref, p_ref, hc_ref, dk_ref, dv_ref, after_ref, g_ref):
        j = pl.program_id(0)
        hv = h_ref[...]
        g_ref[:, 0:DA] = _tn(hv, p_ref[0])
        g_ref[:, DA:2 * DA] = _tn(hv, p_ref[1])

        @pl.when(j == 0)
        def _():
            g_ref[:, DA:2 * DA] += _tn(hc_ref[...], dk_ref[...])

        @pl.when(j == 1)
        def _():
            g_ref[:, 0:DA] += _tn(hc_ref[...], dv_ref[...])

    fixed = lambda j, s: (0, 0)
    hd = D // 2
    grid_spec = pltpu.PrefetchScalarGridSpec(
        num_scalar_prefetch=1, grid=(4,),
        in_specs=[pl.BlockSpec((S, hd), lambda j, s: (0, s[0])), pl.BlockSpec((2, S, DA), lambda j, s: (j, 0, 0)),
                  pl.BlockSpec((L, hd), lambda j, s: (0, s[0])), pl.BlockSpec((L, DA), fixed),
                  pl.BlockSpec((L, DA), fixed), ANY_SPEC],
        out_specs=pl.BlockSpec((None, hd, D), lambda j, s: (j, 0, 0)))
    return _hbm_call(
        body, name=name, out_shape=SDS((4, hd, D), f32), grid_spec=grid_spec, compiler_params=_cp(40),
    )(half, h, dp8, hc, dkc_raw, dvc_m, half if after is None else after)


def _grad_w_in_pair_sum(h, dp8, hc, dkc_raw, dvc_m, half, sent, landed, ssem, rsem):
    def body(half_ref, h_ref, p_ref, hc_ref, dk_ref, dv_ref, sent_ref, land_ref, ssem_ref, rsem_ref,
             t32_ref, tb_ref, buf, lsem):
        j = pl.program_id(0)
        hv = h_ref[...]
        t32_ref[:, 0:DA] = _tn(hv, p_ref[0])
        x, y, c = _my_pos()
        arrival = pltpu.make_async_remote_copy(src_ref=sent_ref.at[j], dst_ref=land_ref.at[j], send_sem=ssem_ref.at[j],
                                               recv_sem=rsem_ref.at[j], device_id=(x, y, 1 - c), device_id_type=MESH)
        arrival.wait_recv()
        arrival.wait_send()
        load = pltpu.make_async_copy(land_ref.at[j], buf, lsem)
        load.start()
        t32_ref[:, DA:2 * DA] = _tn(hv, p_ref[1])

        @pl.when(j == 0)
        def _():
            t32_ref[:, DA:2 * DA] += _tn(hc_ref[...], dk_ref[...])

        @pl.when(j == 1)
        def _():
            t32_ref[:, 0:DA] += _tn(hc_ref[...], dv_ref[...])

        load.wait()
        t = t32_ref[...] + buf[...]
        t32_ref[...] = t
        tb_ref[...] = t.astype(bf16)

    fixed = lambda j, s: (0, 0)
    hd = D // 2
    out_spec = pl.BlockSpec((None, hd, D), lambda j, s: (j, 0, 0))
    grid_spec = pltpu.PrefetchScalarGridSpec(
        num_scalar_prefetch=1, grid=(4,),
        in_specs=[pl.BlockSpec((S, hd), lambda j, s: (0, s[0])), pl.BlockSpec((2, S, DA), lambda j, s: (j, 0, 0)),
                  pl.BlockSpec((L, hd), lambda j, s: (0, s[0])), pl.BlockSpec((L, DA), fixed),
                  pl.BlockSpec((L, DA), fixed), HBM_SPEC, HBM_SPEC, SEM_SPEC, SEM_SPEC],
        out_specs=(out_spec, out_spec), scratch_shapes=[pltpu.VMEM((hd, D), f32), pltpu.SemaphoreType.DMA])
    return _hbm_call(
        body, name="grad_w_in_pair_sum", out_shape=(SDS((4, hd, D), f32), SDS((4, hd, D), bf16)), grid_spec=grid_spec,
        compiler_params=_cp(40, has_side_effects=DATAFLOW),
    )(half, h, dp8, hc, dkc_raw, dvc_m, sent, landed, ssem, rsem)


def _norm_mod_bwd(x, dh, g, scale):
    r = lax.rsqrt(jnp.mean(x * x, axis=-1, keepdims=True) + RMS_EPS)
    xh = x * r
    y = xh * g
    dshift = jnp.sum(dh, axis=0, keepdims=True)
    dscale = jnp.sum(dh * y, axis=0, keepdims=True)
    dyn = dh * (1.0 + scale)
    dg = jnp.sum(dyn * xh, axis=0, keepdims=True)
    gdy = dyn * g
    dx = r * (gdy - xh * jnp.mean(xh * gdy, axis=-1, keepdims=True))
    return dx, dshift, dscale, dg


def _dh_grad_x(dp8, w4, xx, dy, norm_g, mrow, b_ada, after=None):
    tm = 256

    def body(p_ref, w_ref, x_ref, dy_ref, g_ref, m_ref, b_ref, after_ref, gx_ref, dsh_ref, dsc_ref, dg_ref):
        @pl.when(pl.program_id(0) == 0)
        def _():
            dsh_ref[...] = jnp.zeros_like(dsh_ref)
            dsc_ref[...] = jnp.zeros_like(dsc_ref)
            dg_ref[...] = jnp.zeros_like(dg_ref)

        dh = None
        for j in range(4):
            for half in range(2):
                term = _tt(p_ref[2 * j + half], w_ref[j, :, half * DA:(half + 1) * DA])
                dh = term if dh is None else dh + term
        scale = m_ref[:, D:2 * D] + b_ref[:, D:2 * D]
        dx, dshift, dscale, dg = _norm_mod_bwd(x_ref[...], dh, g_ref[...], scale)
        gx_ref[...] = dy_ref[...] + dx
        dsh_ref[...] += dshift
        dsc_ref[...] += dscale
        dg_ref[...] += dg

    row = lambda i: (i, 0)
    fixed = lambda i: (0, 0)
    vec = SDS((1, D), f32)
    return _hbm_call(
        body, name="dh_grad_x", out_shape=(SDS((S, D), f32), vec, vec, vec), grid=(S // tm,),
        in_specs=[pl.BlockSpec((8, tm, DA), lambda i: (0, i, 0)), pl.BlockSpec((4, D, D), lambda i: (0, 0, 0)),
                  pl.BlockSpec((tm, D), row), pl.BlockSpec((tm, D), row), pl.BlockSpec((1, D), fixed),
                  pl.BlockSpec((1, 3 * D), fixed), pl.BlockSpec((1, 3 * D), fixed), ANY_SPEC],
        out_specs=(pl.BlockSpec((tm, D), row), pl.BlockSpec((1, D), fixed), pl.BlockSpec((1, D), fixed),
                   pl.BlockSpec((1, D), fixed)),
        compiler_params=_cp(56, dimension_semantics=("arbitrary",)),
    )(dp8, w4, xx, dy, norm_g, mrow, b_ada, b_ada if after is None else after)


def _dhc_sums(dkc_raw, dvc_m, w4, ctx2, norm_g, mrow_c, b_ada, after=None):
    def body(dk_ref, dv_ref, w0_ref, w1_ref, x_ref, g_ref, m_ref, b_ref, after_ref, dsh_ref, dsc_ref, dg_ref):
        dh = _tt(dk_ref[...], w0_ref[:, DA:2 * DA]) + _tt(dv_ref[...], w1_ref[:, 0:DA])
        scale = m_ref[:, D:2 * D] + b_ref[:, D:2 * D]
        _, dshift, dscale, dg = _norm_mod_bwd(x_ref[...], dh, g_ref[...], scale)
        dsh_ref[...] = dshift
        dsc_ref[...] = dscale
        dg_ref[...] = dg

    fixed = lambda i: (0, 0)
    vec = SDS((1, D), f32)
    vspec = pl.BlockSpec((1, D), fixed)
    return _hbm_call(
        body, name="dhc_sums", out_shape=(vec, vec, vec), grid=(1,),
        in_specs=[pl.BlockSpec((L, DA), fixed), pl.BlockSpec((L, DA), fixed),
                  pl.BlockSpec((None, D, D), lambda i: (0, 0, 0)), pl.BlockSpec((None, D, D), lambda i: (1, 0, 0)),
                  pl.BlockSpec((L, D), fixed), vspec, pl.BlockSpec((1, 3 * D), fixed), pl.BlockSpec((1, 3 * D), fixed),
                  ANY_SPEC],
        out_specs=(vspec, vspec, vspec), compiler_params=_cp(32),
    )(dkc_raw, dvc_m, w4, w4, ctx2, norm_g, mrow_c, b_ada, b_ada if after is None else after)


def _rope_tables():
    nf = DH // 4
    inv = np.float32(ROPE_THETA) ** (-np.arange(nf, dtype=np.float32) / np.float32(nf))
    ang_c = np.arange(GW, dtype=np.float32)[:, None] * inv
    ang_r = np.arange(ROWS, dtype=np.float32)[:, None] * inv
    zc, zr = np.zeros((GW, 2 * nf), np.float32), np.zeros((ROWS, 2 * nf), np.float32)
    ct_cos = np.tile(np.concatenate([zc, np.cos(ang_c), np.cos(ang_c)], axis=1), (1, H))
    ct_sin = np.tile(np.concatenate([zc, -np.sin(ang_c), np.sin(ang_c)], axis=1), (1, H))
    rt_cos = np.tile(np.concatenate([np.cos(ang_r), np.cos(ang_r), zr], axis=1), (1, H))
    rt_sin = np.tile(np.concatenate([-np.sin(ang_r), np.sin(ang_r), zr], axis=1), (1, H))
    rep8 = lambda t: np.ascontiguousarray(np.broadcast_to(t[:, None, :], (ROWS, 8, DA))).reshape(ROWS * 8, DA)
    return tuple(jnp.asarray(t, f32) for t in (ct_cos, rep8(rt_cos), ct_sin, rep8(rt_sin)))


def _local_step(xx, ctx2, tgt, mrow, mrow_c, b_ada, norm_g, weights, q_norm_g, k_norm_g, rpb2, conv_w_full, conv_b,
                hooks=None):
    hooks = hooks or {}
    gq = jnp.tile(q_norm_g, (1, H))
    gk = jnp.tile(k_norm_g, (1, H))
    rope = _rope_tables()

    h = _prenorm(xx, norm_g, mrow, b_ada, 256, "prenorm_x", after=weights.get("started"))
    hc = _prenorm(ctx2, norm_g, mrow_c, b_ada, L, "prenorm_ctx")
    jv = weights["jvec"]
    p = _in_proj_own(h, weights["own"], jv)
    btb = _bias_tiles(rpb2, after=p)
    w4, started = weights["near"](btb)
    p = _in_proj_block(h, w4, p, weights["first"], "in_proj_near", after=started)
    w4 = weights["near2"](w4, p)
    p = _in_proj_block(h, w4, p, weights["second"], "in_proj_near2")
    w4, started = weights["far"](w4, p)
    p = _in_proj_block(h, w4, p, jv ^ 3, "in_proj_far", after=started)
    pc = _ctx_proj(hc, w4)
    qr, qp, kr, vh = _qk_prep(p, gq, gk, rope)
    kc, vc = _ctx_prep(pc, gk)
    o = _attn_fwd(qr, qp, kr, vh, kc, vc, btb)
    started = weights["out_arrived"](o) if "out_arrived" in weights else None
    conv_g = _conv_fwd(p, conv_w_full, conv_b, after=started)
    w_out_full = weights["out"](conv_g)
    dy, dconv, dp8, do, g_w_out, dgate, loss_sum = _out_proj_loss(o, p, conv_g, w_out_full, xx, tgt, mrow, b_ada)
    started = hooks["g_w_out"](g_w_out) if "g_w_out" in hooks else None
    dp8, g_conv_w, g_conv_b = _conv_bwd(dconv, p, conv_w_full, conv_b, dp8, after=started)
    started = hooks["after_conv"](dp8) if "after_conv" in hooks else None
    dqr, dqp, dkr, dvh, dkc, dvc, dbtb = _attn_bwd(qr, qp, kr, vh, kc, vc, btb, do, after=started)
    dp8, g_gq, g_gk = _qk_bwd(dqr, dqp, dkr, dvh, p, gq, gk, rope, dp8)
    dkc_raw, dvc_m, g_gk_c = _ctx_bwd(dkc, dvc, pc, gk)
    first = hooks.get("first_half", jnp.zeros((1,), i32))
    g_first = _grad_w_in(h, dp8, hc, dkc_raw, dvc_m, first, "grad_w_in_first")
    started = hooks["g_w_in_first"](g_first) if "g_w_in_first" in hooks else None
    if "w_in_pair_sum" in hooks:
        g_second = None
        started = hooks["w_in_pair_sum"](functools.partial(_grad_w_in_pair_sum, h, dp8, hc, dkc_raw, dvc_m, 1 - first))
    else:
        g_second = _grad_w_in(h, dp8, hc, dkc_raw, dvc_m, 1 - first, "grad_w_in_second", after=started)
    dshift_c, dscale_c, dng_c = _dhc_sums(dkc_raw, dvc_m, w4, ctx2, norm_g, mrow_c, b_ada, after=started)
    g_rpb = _bias_bwd(dbtb, after=dshift_c)
    grad_x, dshift, dscale, dng = _dh_grad_x(dp8, w4, xx, dy, norm_g, mrow, b_ada, after=g_rpb)
    return dict(loss_sum=loss_sum, grad_x=grad_x, g_w_in=(g_first, g_second), g_w_out=g_w_out, g_conv_w=g_conv_w,
                g_conv_b=g_conv_b, g_rpb=g_rpb, g_gq=g_gq, g_gk=g_gk, g_gk_c=g_gk_c, dshift=dshift, dscale=dscale,
                dgate=dgate, dng=dng, dshift_c=dshift_c, dscale_c=dscale_c, dng_c=dng_c)


def _pair_sum_w_out(g, r, cvec):
    hr = D // 8

    def body(c_ref, g0, g1, g2, g3, r_ref, t32_ref, tb_ref):
        for q, g_ref in enumerate((g0, g1, g2, g3)):
            t = g_ref[...] + r_ref[q]
            t32_ref[q] = t
            tb_ref[q] = t.astype(bf16)

    gspecs = [pl.BlockSpec((hr, D), lambda i, c, q=q: (2 * q + c[0], 0)) for q in range(4)]
    full = pl.BlockSpec((4, hr, D), lambda i, c: (0, 0, 0))
    grid_spec = pltpu.PrefetchScalarGridSpec(num_scalar_prefetch=1, grid=(1,), in_specs=gspecs + [full],
                                             out_specs=(full, full))
    return _hbm_call(body, name="pair_sum_w_out", out_shape=(SDS((4, hr, D), f32), SDS((4, hr, D), bf16)),
                          grid_spec=grid_spec)(cvec, g, g, g, g, r)


def _chip_sum(t32, r2, jvec, name):
    rows = t32.shape[1]
    tr = min(rows, 128)

    def body(j_ref, t_ref, r_ref, u_ref):
        u_ref[...] = ((t_ref[...] + r_ref[0].astype(f32)) + r_ref[1].astype(f32)) + r_ref[2].astype(f32)

    grid_spec = pltpu.PrefetchScalarGridSpec(
        num_scalar_prefetch=1, grid=(rows // tr,),
        in_specs=[pl.BlockSpec((None, tr, D), lambda i, j: (j[0], i, 0)),
                  pl.BlockSpec((3, tr, D), lambda i, j: (0, i, 0))],
        out_specs=pl.BlockSpec((tr, D), lambda i, j: (i, 0)))
    return _hbm_call(body, name=name, out_shape=SDS((rows, D), f32), grid_spec=grid_spec)(jvec, t32, r2)


_PK = {}
_off = 0
for _name, _rows in (("dm", 24), ("dmc", 24), ("dng", 8), ("dng_c", 8), ("gq", 8), ("gk", 8), ("gk_c", 8),
                     ("rpb", H * N_DR), ("conv_b", 8), ("conv_w", 16), ("loss", 8)):
    _PK[_name] = (_off, _off + _rows)
    _off += _rows
PK_ROWS = _off
RS_B_ADA, RS_NORM_G, RS_GQ, RS_GK, RS_RPB, RS_CONV_B, RS_CONV_W, RS_DMC, RS_LOSS, RS_ROWS = (
    0, 24, 32, 40, 48, 168, 176, 192, 216, 224)


def _small_reduce(gathered):
    def body(g_ref, o_ref, dm_ref):
        a0 = _PK["dm"][0]
        dm_ref[...] = jnp.zeros_like(dm_ref)
        for b in range(8):
            for i in range(24):
                dm_ref[b:b + 1, 128 * i:128 * (i + 1)] = g_ref[b, a0 + i:a0 + i + 1, :]
        tot = g_ref[0]
        for b in range(1, 8):
            tot = tot + g_ref[b]

        def rows(name):
            a, z = _PK[name]
            return tot[a:z]

        o_ref[RS_B_ADA:RS_B_ADA + 24] = rows("dm") + rows("dmc")
        o_ref[RS_NORM_G:RS_NORM_G + 8] = rows("dng") + rows("dng_c")
        gq = jnp.broadcast_to(jnp.sum(rows("gq"), axis=0, keepdims=True), (8, 128))
        gk = jnp.broadcast_to(jnp.sum(rows("gk") + rows("gk_c"), axis=0, keepdims=True), (8, 128))
        o_ref[RS_GQ:RS_GQ + 8] = gq + pltpu.roll(gq, DH, 1)
        o_ref[RS_GK:RS_GK + 8] = gk + pltpu.roll(gk, DH, 1)
        o_ref[RS_RPB:RS_RPB + H * N_DR] = rows("rpb")
        o_ref[RS_CONV_B:RS_CONV_B + 8] = rows("conv_b")
        o_ref[RS_CONV_W:RS_CONV_W + 16] = rows("conv_w")
        dmc = rows("dmc")
        o_ref[RS_DMC:RS_DMC + 24] = dmc
        o_ref[RS_LOSS:RS_LOSS + 8] = rows("loss")
        for i in range(24):
            dm_ref[8:9, 128 * i:128 * (i + 1)] = dmc[i:i + 1]

    return _hbm_call(body, name="small_reduce", out_shape=(SDS((RS_ROWS, 128), f32), SDS((16, 3 * D), f32)),
                     in_specs=[VMEM_SPEC], out_specs=(VMEM_SPEC, VMEM_SPEC))(gathered)


def _w_ada_grad(sc16, dm16, w_ada_shard, jvec):
    ncol = w_ada_shard.shape[1]

    def body(j_ref, sc_ref, dm_ref, w_ref, g_ref, part_ref):
        dm = dm_ref[...]
        g_ref[...] = lax.dot_general(sc_ref[...], dm, (((0,), (0,)), ((), ())), precision=HIGHEST,
                                     preferred_element_type=f32)
        part_ref[...] = lax.dot_general(dm[8:16], w_ref[...], (((1,), (1,)), ((), ())), precision=HIGHEST,
                                        preferred_element_type=f32)

    fixed = lambda i, j: (0, 0)
    grid_spec = pltpu.PrefetchScalarGridSpec(
        num_scalar_prefetch=1, grid=(1,),
        in_specs=[pl.BlockSpec((16, D), fixed), pl.BlockSpec((16, ncol), lambda i, j: (0, j[0])),
                  pl.BlockSpec((D, ncol), fixed)],
        out_specs=(pl.BlockSpec((D, ncol), fixed), pl.BlockSpec((8, D), fixed)))
    return _pallas_call(body, name="w_ada_grad", out_shape=(SDS((D, ncol), f32), SDS((8, D), f32)),
                        grid_spec=grid_spec, compiler_params=_cp(40))(jvec, sc16, dm16, w_ada_shard)


def _c_ctx_grad(parts4, c_ctx):
    def body(p_ref, c_ref, o_ref):
        tot = ((p_ref[0] + p_ref[1]) + p_ref[2]) + p_ref[3]
        o_ref[...] = tot[0:1] * _dsilu(c_ref[...].reshape(1, D))

    return _pallas_call(body, name="c_ctx_grad", out_shape=SDS((1, D), f32), in_specs=[VMEM_SPEC, VMEM_SPEC],
                        out_specs=VMEM_SPEC)(parts4, c_ctx)


def _adamw(w, g, m, v, name, after=None):
    rows, cols = w.shape
    tr = 256 if rows % 256 == 0 else rows

    def body(w_ref, g_ref, m_ref, v_ref, after_ref, d_ref, m2_ref, v2_ref):
        gv = g_ref[...]
        m2 = ADAM_B1 * m_ref[...] + (1.0 - ADAM_B1) * gv
        v2 = ADAM_B2 * v_ref[...] + (1.0 - ADAM_B2) * jnp.square(gv)
        m_hat = m2 / (1.0 - ADAM_B1 ** ADAM_STEP)
        v_hat = v2 / (1.0 - ADAM_B2 ** ADAM_STEP)
        d_ref[...] = -ADAM_LR * (m_hat / (jnp.sqrt(v_hat) + ADAM_EPS) + ADAM_WD * w_ref[...])
        m2_ref[...] = m2
        v2_ref[...] = v2

    spec = pl.BlockSpec((tr, cols), lambda i: (i, 0))
    shp = SDS((rows, cols), f32)
    return _hbm_call(body, name=name, out_shape=(shp, shp, shp), grid=(rows // tr,), in_specs=[spec] * 4 + [ANY_SPEC],
                     out_specs=(spec, spec, spec))(w, g, m, v, g if after is None else after)


def _adamw_halves(w, g_mine, g_other, m, v, cvec, name, after=None):
    rows, cols = w.shape
    half = rows // 2
    tr = min(256, half)
    per_half = half // tr

    def body(c_ref, w_ref, ga_ref, gb_ref, m_ref, v_ref, after_ref, g_ref, d_ref, m2_ref, v2_ref):
        in_my_half = (pl.program_id(0) // per_half) == c_ref[0]
        gv = jnp.where(in_my_half, ga_ref[...], gb_ref[...])
        g_ref[...] = gv
        m2 = ADAM_B1 * m_ref[...] + (1.0 - ADAM_B1) * gv
        v2 = ADAM_B2 * v_ref[...] + (1.0 - ADAM_B2) * jnp.square(gv)
        m_hat = m2 / (1.0 - ADAM_B1 ** ADAM_STEP)
        v_hat = v2 / (1.0 - ADAM_B2 ** ADAM_STEP)
        d_ref[...] = -ADAM_LR * (m_hat / (jnp.sqrt(v_hat) + ADAM_EPS) + ADAM_WD * w_ref[...])
        m2_ref[...] = m2
        v2_ref[...] = v2

    full = pl.BlockSpec((tr, cols), lambda i, c: (i, 0))
    part = pl.BlockSpec((tr, cols), lambda i, c: (i % per_half, 0))
    shp = SDS((rows, cols), f32)
    grid_spec = pltpu.PrefetchScalarGridSpec(num_scalar_prefetch=1, grid=(rows // tr,),
                                             in_specs=[full, part, part, full, full, ANY_SPEC], out_specs=(full,) * 4)
    return _hbm_call(body, name=name, out_shape=(shp,) * 4, grid_spec=grid_spec)(
        cvec, w, g_mine, g_other, m, v, cvec if after is None else after)


def _adam_math(w, g, m, v):
    m2 = ADAM_B1 * m + (1.0 - ADAM_B1) * g
    v2 = ADAM_B2 * v + (1.0 - ADAM_B2) * jnp.square(g)
    m_hat = m2 / (1.0 - ADAM_B1 ** ADAM_STEP)
    v_hat = v2 / (1.0 - ADAM_B2 ** ADAM_STEP)
    return -ADAM_LR * (m_hat / (jnp.sqrt(v_hat) + ADAM_EPS) + ADAM_WD * w), m2, v2


def _adamw_small(red, g_c_ctx, jvec, ws, ms, vs):
    n = len(ws)

    def body(*refs):
        red_ref, gc_ref, j_ref = refs[:3]
        w_refs, m_refs, v_refs = refs[3:3 + n], refs[3 + n:3 + 2 * n], refs[3 + 2 * n:3 + 3 * n]
        outs = refs[3 + 3 * n:]
        g_out, d_out, m_out, v_out = outs[:n], outs[n:2 * n], outs[2 * n:3 * n], outs[3 * n:]
        chip = j_ref[0]
        lanes = lambda i: (slice(None), slice(128 * i, 128 * (i + 1)))
        row = lambda r0, i: (lambda: red_ref[r0 + i:r0 + i + 1, :])
        whole = (slice(None), slice(None))
        chunks = [
            [((slice(None),), lambda: gc_ref[...].reshape(D))],
            [(lanes(i), row(RS_B_ADA, i)) for i in range(3 * D // 128)],
            [(lanes(i), row(RS_NORM_G, i)) for i in range(D // 128)],
            [(whole, lambda: red_ref[RS_GQ:RS_GQ + 1, 0:DH])],
            [(whole, lambda: red_ref[RS_GK:RS_GK + 1, 0:DH])],
            [((dr,), (lambda dr=dr: red_ref[pl.ds(RS_RPB + dr, H, stride=N_DR), 0:N_DC])) for dr in range(N_DR)],
            [((r,), (lambda r=r: red_ref[pl.ds(RS_CONV_W + 4 * r + chip, 1), :])) for r in range(3)],
            [(lanes(i), row(RS_CONV_B, i)) for i in range(DC // 128)],
        ]
        for a in range(n):
            for idx, grad in chunks[a]:
                g = grad()
                d, m2, v2 = _adam_math(w_refs[a][idx], g, m_refs[a][idx], v_refs[a][idx])
                g_out[a][idx] = g
                d_out[a][idx] = d
                m_out[a][idx] = m2
                v_out[a][idx] = v2

    shapes = [SDS(w.shape, f32) for w in ws]
    res = _pallas_call(body, name="adamw_small", out_shape=shapes * 4,
                       in_specs=[VMEM_SPEC, VMEM_SPEC, SMEM_SPEC] + [VMEM_SPEC] * (3 * n),
                       out_specs=[VMEM_SPEC] * (4 * n))(red, g_c_ctx, jvec, *ws, *ms, *vs)
    return [list(res[k * n:(k + 1) * n]) for k in range(4)]


def _rows128(a):
    return a.reshape(-1, 128)


def kernel(x, c, ctx, c_ctx, w_ada, b_ada, norm_g, w_in, q_norm_g, k_norm_g, rpb, conv_w, conv_b, w_out, loss_target, m_c_ctx, m_w_ada, m_b_ada, m_norm_g, m_w_in, m_q_norm_g, m_k_norm_g, m_rpb, m_conv_w, m_conv_b, m_w_out, v_c_ctx, v_w_ada, v_b_ada, v_norm_g, v_w_in, v_q_norm_g, v_k_norm_g, v_rpb, v_conv_w, v_conv_b, v_w_out):
    xi, yi, ci = lax.axis_index("x"), lax.axis_index("y"), lax.axis_index("c")
    dev = 4 * xi + 2 * yi + ci
    chip = 2 * xi + yi
    cvec = jnp.reshape(ci, (1,)).astype(i32)
    jvec = jnp.reshape(chip, (1,)).astype(i32)
    w_ada_s = w_ada[0]
    ncol = w_ada_s.shape[1]

    gc = _split_start([_to_slot(c.reshape(8, 128), 8, dev)], [], 7, _gather8_copies, "gather_c_start")
    wo4c = _cast_to_slot(w_out[0], jvec, "cast_w_out", after=gc[3])
    w4c = _cast_to_slot(w_in[0], jvec, "cast_w_in", after=wo4c)
    (c8,), _ = _split_wait(gc[0], gc[1], [gc[2]], [], w4c, _gather8_copies, "gather_c_wait")
    cc = jnp.concatenate([c8.reshape(8, D), c_ctx.reshape(1, D), jnp.zeros((7, D), f32)], axis=0)
    m_shard, sc16 = _adaln_shard(cc, w_ada_s)

    conv_w_pad = jnp.pad(conv_w[0], ((0, 5), (0, 0)))
    all_k = [(0, 0, 0), (0, 0, 1), (0, 0, 2)]
    sem_a, rem_a, m4s, cw4s, w4s, token = _split_start(
        [_to_slot(m_shard, 4, chip), _to_slot(conv_w_pad, 4, chip), w4c], [], NEAR_SEM + 1, _mod_and_near_copies,
        "gather_mod_weights_near_start")
    (m4, cw4), _ = _split_wait(sem_a, rem_a, [m4s, cw4s], [], token, _gather4_copies, "gather_mod_wait")
    m_full = jnp.transpose(m4, (1, 0, 2)).reshape(16, 4 * ncol)
    mrow = lax.dynamic_slice(m_full, (dev, 0), (1, 3 * D))
    mrow_c = m_full[8:9]
    conv_w_full = jnp.transpose(cw4[:, 0:3, :], (1, 0, 2)).reshape(3, DC)
    waves = {}

    def near(after):
        (w4w,), _ = _split_wait(sem_a, rem_a, [w4s], [], after, _near_copies, "weights_near_wait")
        sem_b, rem_b, w4b, started = _split_start([w4w], [], 2, _pass_relay_copies, "weights_pass_start")
        waves["pass"] = (sem_b, rem_b)
        return w4b, started

    def near2(w4, after):
        (w4w,), _ = _split_wait(*waves["pass"], [w4], [], after, _pass_copy, "weights_pass_wait")
        return w4w

    def far(w4, after):
        (w4w,), _ = _split_wait(*waves["pass"], [w4], [], after, _relay_copy, "weights_far_wait")
        w4x, sem_c, rem_c, wo4s, started = _forward_then_start(w4w, wo4c, all_k, "weights_far_forward_out_start")
        (w4f,), _ = _split_wait(sem_c, rem_c, [w4x], [], started, _diag_forward_copy, "weights_far_forward_wait")
        waves["out"] = (sem_c, rem_c, wo4s)
        return w4f, started

    def w_out_arrived(after):
        sem_c, rem_c, wo4s = waves["out"]
        (wow,) = _halves_wait(sem_c, rem_c, [wo4s], after, all_k, "weights_out_wait")
        sem_d, rem_d, wof, started = _split_start([wow], [], 3, _forward_copies, "weights_out_forward_start")
        waves["out_forward"] = (sem_d, rem_d, wof)
        return started

    def w_out_gathered(after):
        sem_d, rem_d, wof = waves["out_forward"]
        (wo,), _ = _split_wait(sem_d, rem_d, [wof], [], after, _forward_copies, "weights_out_forward_wait")
        return wo.reshape(D, D)

    weights = dict(own=w_in[0], jvec=jvec, started=token, first=jvec ^ (1 + cvec), second=jvec ^ (2 - cvec),
                   near=near, near2=near2, far=far, out_arrived=w_out_arrived, out=w_out_gathered)

    exchange = _exchange_copies
    pending = {}

    def on_g_w_out(g_w_out):
        out = _split_start([g_w_out], [SDS((4, D // 8, D), f32)], 4, exchange, "grad_out_pair_start")
        pending["ex_out"] = out
        return out[4]

    def after_conv(dp8):
        ssem_o, rsem_o, g_o, land_o, _ = pending["ex_out"]
        (g_o,), (ex_o,) = _split_wait(ssem_o, rsem_o, [g_o], [land_o], dp8, exchange, "grad_out_pair_wait")
        to32, tob = _pair_sum_w_out(g_o, ex_o, cvec)
        out = _split_start([tob], [SDS((3, D // 8, D), bf16)], 3, _scatter_copies, "grad_out_chip_start")
        pending["sc_out"] = (out, to32)
        return out[4]

    def on_g_w_in_first(g_first):
        out = _split_start([g_first], [SDS((4, D // 2, D), f32)], 4, _block_exchange_copies, "grad_pair_start")
        pending["ex"] = (out[0], out[1], [out[2]], [out[3]])
        return out[4]

    def on_w_in_pair_sum(pair_sum):
        ex_ssem, ex_rsem, ex_srcs, ex_lands = pending["ex"]
        t32, tb = pair_sum(ex_srcs[0], ex_lands[0], ex_ssem, ex_rsem)
        out = _split_start([tb], [SDS((3, D // 2, D), bf16)], 3, _scatter_copies, "grad_chip_start")
        pending["sc_in"] = (out, t32)
        return out[4]

    r = _local_step(x[0], ctx[0], loss_target[0], mrow, mrow_c, b_ada, norm_g, weights, q_norm_g, k_norm_g,
                    rpb[0], conv_w_full, conv_b,
                    dict(g_w_out=on_g_w_out, after_conv=after_conv, first_half=1 - cvec,
                         g_w_in_first=on_g_w_in_first, w_in_pair_sum=on_w_in_pair_sum))
    sc_in, t32 = pending["sc_in"]
    _, (r2,) = _split_wait(sc_in[0], sc_in[1], [sc_in[2]], [sc_in[3]], r["dng"], _scatter_copies, "grad_chip_wait")
    u_in = _chip_sum(t32, r2, jvec, "chip_sum_w_in")
    sc_o, to32 = pending["sc_out"]
    _, (ro2,) = _split_wait(sc_o[0], sc_o[1], [sc_o[2]], [sc_o[3]], u_in, _scatter_copies, "grad_out_chip_wait")
    u_out = _chip_sum(to32, ro2, jvec, "chip_sum_w_out")
    swap = _split_start([u_in, u_out], [SDS(u_in.shape, f32), SDS(u_out.shape, f32)], 2, _swap_copies,
                        "grad_pair_swap_start")

    dm = jnp.concatenate([r["dshift"], r["dscale"], r["dgate"]], axis=1)
    dmc = jnp.concatenate([r["dshift_c"], r["dscale_c"], jnp.zeros((1, D), f32)], axis=1)
    pack_parts = [_rows128(dm), _rows128(dmc), _rows128(r["dng"]), _rows128(r["dng_c"]), _rows128(r["g_gq"]),
                  _rows128(r["g_gk"]), _rows128(r["g_gk_c"]), r["g_rpb"].reshape(H * N_DR, 128),
                  _rows128(r["g_conv_b"]), _rows128(r["g_conv_w"][0:3]), jnp.pad(r["loss_sum"], ((0, 0), (0, 127)))]
    pack = jnp.concatenate([jnp.pad(p, ((0, -p.shape[0] % 8), (0, 0))) for p in pack_parts], axis=0)
    assert pack.shape[0] == PK_ROWS
    gs = _split_start([_to_slot(pack, 8, dev)], [], 7, _gather8_copies, "gather_small_start", after=swap[6])
    (u_in, u_out), (o_in, o_out) = _split_wait(swap[0], swap[1], swap[2:4], swap[4:6], gs[3], _swap_copies,
                                               "grad_pair_swap_wait")
    g_w_in_s, d_w_in, nm_w_in, nv_w_in = _adamw_halves(w_in[0], u_in, o_in, m_w_in[0], v_w_in[0], cvec, "adamw_w_in",
                                                       after=gs[3])
    g_w_out_s, d_w_out, nm_w_out, nv_w_out = _adamw_halves(w_out[0], u_out, o_out, m_w_out[0], v_w_out[0], cvec,
                                                           "adamw_w_out", after=nm_w_in)
    (gathered,), _ = _split_wait(gs[0], gs[1], [gs[2]], [], nm_w_out, _gather8_copies, "gather_small_wait")
    red, dm16 = _small_reduce(gathered)
    loss = red[RS_LOSS, 0] * (0.5 / D)

    g_w_ada_s, cpart = _w_ada_grad(sc16, dm16, w_ada_s, jvec)
    gcp = _split_start([_to_slot(cpart, 4, chip)], [], 3, _gather4_copies, "gather_c_ctx_parts_start")
    d_w_ada, nm_w_ada, nv_w_ada = _adamw(w_ada_s, g_w_ada_s, m_w_ada[0], v_w_ada[0], "adamw_w_ada", after=gcp[3])
    (cparts4,), _ = _split_wait(gcp[0], gcp[1], [gcp[2]], [], nm_w_ada, _gather4_copies, "gather_c_ctx_parts_wait")
    g_c_ctx = _c_ctx_grad(cparts4, c_ctx)

    t_rpb = lambda a: jnp.transpose(a, (0, 2, 1, 3)).reshape(N_DR, H, N_DC)
    t_cw = lambda a: jnp.transpose(a, (1, 0, 2))
    small = _adamw_small(
        red, g_c_ctx, jvec,
        [c_ctx, b_ada, norm_g, q_norm_g, k_norm_g, t_rpb(rpb), t_cw(conv_w), conv_b],
        [m_c_ctx, m_b_ada, m_norm_g, m_q_norm_g, m_k_norm_g, t_rpb(m_rpb), t_cw(m_conv_w), m_conv_b],
        [v_c_ctx, v_b_ada, v_norm_g, v_q_norm_g, v_k_norm_g, t_rpb(v_rpb), t_cw(v_conv_w), v_conv_b])
    for kind in small:
        kind[5] = jnp.transpose(kind[5].reshape(1, N_DR, H, N_DC), (0, 2, 1, 3))
        kind[6] = jnp.transpose(kind[6], (1, 0, 2))

    def ordered(kind, big_w_ada, big_w_in, big_w_out):
        s_c_ctx, s_b_ada, s_norm_g, s_q, s_k, s_rpb, s_conv_w, s_conv_b = small[kind]
        return [s_c_ctx, big_w_ada[None], s_b_ada, s_norm_g, big_w_in[None], s_q, s_k, s_rpb, s_conv_w,
                s_conv_b, big_w_out[None]]

    grads = ordered(0, g_w_ada_s, g_w_in_s, g_w_out_s)
    deltas = ordered(1, d_w_ada, d_w_in, d_w_out)
    new_m = ordered(2, nm_w_ada, nm_w_in, nm_w_out)
    new_v = ordered(3, nv_w_ada, nv_w_in, nv_w_out)
    return (loss, r["grad_x"][None], *grads, *deltas, *new_m, *new_v)
```

```python
import functools

import jax
import jax.numpy as jnp
import numpy as np
from jax import lax
from jax.experimental import pallas as pl
from jax.experimental.pallas import tpu as pltpu

f32, bf16, i32 = jnp.float32, jnp.bfloat16, jnp.int32
MESH = pl.DeviceIdType.MESH
HIGHEST = lax.Precision.HIGHEST

D = 1024
S = 2048
L = 256
GW = 64
ROWS = S // GW
H = 8
DH = 64
DA = H * DH
DC = 512
WIN_H, WIN_W = 8, 16
N_DR, N_DC = 2 * WIN_H - 1, 2 * WIN_W - 1
RMS_EPS = 1e-6
ROPE_THETA = 10000.0
QK_SCALE = DH ** -0.5
NEG = -1e30

QB = 128
NQB = S // QB
KR = 9
KB = KR * GW
TILE_GEOM = ((0, 0), (2, 0), (4, 0), (28, 23), (30, 23))
NT = len(TILE_GEOM)

ADAM_LR, ADAM_B1, ADAM_B2, ADAM_EPS, ADAM_WD, ADAM_STEP = 0.001, 0.9, 0.999, 1e-08, 0.01, 10

VMEM_SPEC = pl.BlockSpec(memory_space=pltpu.VMEM)
ANY_SPEC = pl.BlockSpec(memory_space=pl.ANY)
SMEM_SPEC = pl.BlockSpec(memory_space=pltpu.SMEM)
SDS = jax.ShapeDtypeStruct


_pallas_call = pl.pallas_call


def _hbm_call(body, *, out_shape, in_specs=None, out_specs=None, grid_spec=None, **kw):
    n_pre = 0
    if grid_spec is not None:
        ispecs, ospecs, n_pre = grid_spec.in_specs, grid_spec.out_specs, grid_spec.num_scalar_prefetch
        kw["grid_spec"] = grid_spec
    else:
        ispecs, ospecs = in_specs, out_specs
        kw.update(in_specs=in_specs, out_specs=out_specs)

    def blocked(spec):
        return isinstance(spec, pl.BlockSpec) and spec.block_shape is not None

    single = not isinstance(out_shape, (tuple, list))
    shapes = [out_shape] if single else list(out_shape)
    ospec_list = list(ospecs) if isinstance(ospecs, (tuple, list)) else [ospecs]
    shapes = [pltpu.HBM(s.shape, s.dtype) if blocked(sp) else s for s, sp in zip(shapes, ospec_list)]
    call = _pallas_call(body, out_shape=shapes[0] if single else tuple(shapes), **kw)

    def run(*args):
        arrays = [pltpu.with_memory_space_constraint(a, pltpu.HBM) if blocked(sp) else a
                  for a, sp in zip(args[n_pre:], ispecs)]
        return call(*args[:n_pre], *arrays)

    return run


def _cp(vmem_mb=None, **kw):
    if vmem_mb is not None:
        kw["vmem_limit_bytes"] = vmem_mb << 20
    return pltpu.CompilerParams(**kw)


def _silu(z):
    return z * jax.nn.sigmoid(z)


def _dsilu(z):
    sg = jax.nn.sigmoid(z)
    return sg * (1.0 + z * (1.0 - sg))


def _row_start(i):
    return min(max(i - WIN_H // 2, 0), ROWS - WIN_H)


def _my_pos():
    return lax.axis_index("x"), lax.axis_index("y"), lax.axis_index("c")


def _flip(v, bit):
    return 1 - v if bit else v


def _swap_copies(srcs, lands, ssem, rsem):
    x, y, c = _my_pos()
    return [pltpu.make_async_remote_copy(src_ref=srcs[a], dst_ref=lands[a], send_sem=ssem.at[a], recv_sem=rsem.at[a],
                                         device_id=(x, y, 1 - c), device_id_type=MESH) for a in range(len(srcs))]


HBM_SPEC = pl.BlockSpec(memory_space=pltpu.HBM)
SEM_SPEC = pl.BlockSpec(memory_space=pltpu.SEMAPHORE)
DATAFLOW = pltpu.SideEffectType.DATAFLOW_SIDE_EFFECTING


def _peer_chips(x, y, c):
    out = []
    for k in range(1, 4):
        px, py = _flip(x, (k >> 1) & 1), _flip(y, k & 1)
        out.append(((px, py, c), 2 * px + py))
    return out


def _half_copies(srcs, dsts, ssem, rsem, which):
    x, y, c = _my_pos()
    j = 2 * x + y
    peers = _peer_chips(x, y, c)
    pairs = []
    for pos, group, k in which:
        half = srcs[pos].shape[1] // 2
        mine = pl.ds(pl.multiple_of(c * half, 8), half)
        dev, pj = peers[k]
        sem = 3 * group + k
        send = pltpu.make_async_remote_copy(src_ref=srcs[pos].at[j, mine], dst_ref=dsts[pos].at[j, mine],
                                            send_sem=ssem.at[sem], recv_sem=rsem.at[sem], device_id=dev,
                                            device_id_type=MESH)
        arrive = pltpu.make_async_remote_copy(src_ref=srcs[pos].at[j, mine], dst_ref=dsts[pos].at[pj, mine],
                                              send_sem=ssem.at[sem], recv_sem=rsem.at[sem], device_id=dev,
                                              device_id_type=MESH)
        pairs.append((send, arrive))
    return pairs


def _halves_wait(ssem, rsem, bigs, after, which, name):
    nb = len(bigs)

    def body(*refs):
        b_in = refs[:nb]
        ssem_ref, rsem_ref = refs[nb], refs[nb + 1]
        for send, arrive in _half_copies(b_in, b_in, ssem_ref, rsem_ref, which):
            send.wait_send()
            arrive.wait_recv()

    return _hbm_call(
        body, name=name, out_shape=tuple(pltpu.HBM(b.shape, b.dtype) for b in bigs),
        in_specs=[HBM_SPEC] * nb + [SEM_SPEC, SEM_SPEC, ANY_SPEC], out_specs=tuple([HBM_SPEC] * nb),
        input_output_aliases={a: a for a in range(nb)}, compiler_params=_cp(has_side_effects=DATAFLOW),
    )(*bigs, ssem, rsem, after)


FORWARD_SEM = 3


def _diag_forward_copy(srcs, dsts, ssem, rsem):
    x, y, c = _my_pos()
    half = srcs[0].shape[1] // 2
    diag = 3 - (2 * x + y)
    mine = pl.ds(pl.multiple_of(c * half, 8), half)
    other = pl.ds(pl.multiple_of((1 - c) * half, 8), half)
    return [_Copy(srcs[0].at[diag, mine], dsts[0].at[diag, mine], dsts[0].at[diag, other], ssem.at[FORWARD_SEM],
                  rsem.at[FORWARD_SEM], (x, y, 1 - c))]


def _forward_then_start(fwd, big, order, name):
    def body(f_in, b_in, f_out, ssem, rsem, b_out, token):
        _diag_forward_copy([f_in], [f_out], ssem, rsem)[0].start()
        for send, _ in _half_copies([b_in], [b_out], ssem, rsem, order):
            send.start()
        token[...] = jnp.zeros_like(token)

    n_sem = FORWARD_SEM + 1
    out_shape = (pltpu.HBM(fwd.shape, fwd.dtype), pltpu.SemaphoreType.DMA((n_sem,)), pltpu.SemaphoreType.DMA((n_sem,)),
                 pltpu.HBM(big.shape, big.dtype), SDS((8, 128), f32))
    return _hbm_call(
        body, name=name, out_shape=out_shape, in_specs=[HBM_SPEC, HBM_SPEC],
        out_specs=(HBM_SPEC, SEM_SPEC, SEM_SPEC, HBM_SPEC, VMEM_SPEC), input_output_aliases={0: 0, 1: 3},
        compiler_params=_cp(has_side_effects=DATAFLOW),
    )(*[pltpu.with_memory_space_constraint(b, pltpu.HBM) for b in (fwd, big)])


def _cast_to_slot(w, jvec, name, after=None):
    rows, cols = w.shape
    tr = 256

    def body(j_ref, w_ref, after_ref, o_ref):
        o_ref[...] = w_ref[...].astype(bf16)

    grid_spec = pltpu.PrefetchScalarGridSpec(
        num_scalar_prefetch=1, grid=(rows // tr,),
        in_specs=[pl.BlockSpec((tr, cols), lambda i, j: (i, 0)), ANY_SPEC],
        out_specs=pl.BlockSpec((None, tr, cols), lambda i, j: (j[0], i, 0)))
    return _hbm_call(body, name=name, out_shape=SDS((4, rows, cols), bf16),
                     grid_spec=grid_spec)(jvec, w, jvec if after is None else after)


def _exchange_copies(srcs, lands, ssem, rsem):
    x, y, c = _my_pos()
    half = srcs[0].shape[0] // 8
    cps = []
    for jb in range(4):
        src = srcs[0].at[pl.ds(pl.multiple_of((2 * jb + 1 - c) * half, 8), half)]
        cps.append(pltpu.make_async_remote_copy(src_ref=src, dst_ref=lands[0].at[jb], send_sem=ssem.at[jb],
                                                recv_sem=rsem.at[jb], device_id=(x, y, 1 - c), device_id_type=MESH))
    return cps


def _block_exchange_copies(srcs, lands, ssem, rsem):
    x, y, c = _my_pos()
    return [pltpu.make_async_remote_copy(src_ref=srcs[0].at[jb], dst_ref=lands[0].at[jb], send_sem=ssem.at[jb],
                                         recv_sem=rsem.at[jb], device_id=(x, y, 1 - c), device_id_type=MESH)
            for jb in range(4)]


def _scatter_copies(srcs, lands, ssem, rsem):
    x, y, c = _my_pos()
    cps = []
    for a in range(len(srcs)):
        for k, (dev, pj) in enumerate(_peer_chips(x, y, c)):
            cps.append(pltpu.make_async_remote_copy(src_ref=srcs[a].at[pj], dst_ref=lands[a].at[k],
                                                    send_sem=ssem.at[3 * a + k], recv_sem=rsem.at[3 * a + k],
                                                    device_id=dev, device_id_type=MESH))
    return cps


class _Copy:
    def __init__(self, src, dst, arrive, ssem, rsem, dev):
        make = lambda to: pltpu.make_async_remote_copy(src_ref=src, dst_ref=to, send_sem=ssem, recv_sem=rsem,
                                                       device_id=dev, device_id_type=MESH)
        send, arrival = make(dst), make(arrive)
        self.start, self.wait_send, self.wait_recv = send.start, send.wait_send, arrival.wait_recv


def _toward(x, y, along_x):
    return x + along_x * (1 - 2 * x), y + (1 - along_x) * (1 - 2 * y)


def _near_copies(srcs, dsts, ssem, rsem):
    x, y, c = _my_pos()
    px, py = _toward(x, y, c)
    j = 2 * x + y
    return [_Copy(srcs[0].at[j], dsts[0].at[j], dsts[0].at[2 * px + py], ssem.at[0], rsem.at[0], (px, py, c))]


def _pass_copy(srcs, dsts, ssem, rsem):
    x, y, c = _my_pos()
    px, py = _toward(x, y, c)
    qx, qy = _toward(x, y, 1 - c)
    got = 2 * px + py
    return [_Copy(srcs[0].at[got], dsts[0].at[got], dsts[0].at[2 * qx + qy], ssem.at[0], rsem.at[0], (x, y, 1 - c))]


def _relay_copy(srcs, dsts, ssem, rsem):
    x, y, c = _my_pos()
    px, py = _toward(x, y, c)
    qx, qy = _toward(x, y, 1 - c)
    half = srcs[0].shape[1] // 2
    mine = pl.ds(pl.multiple_of(c * half, 8), half)
    got, diag = 2 * px + py, 3 - (2 * x + y)
    return [_Copy(srcs[0].at[got, mine], dsts[0].at[got, mine], dsts[0].at[diag, mine], ssem.at[1], rsem.at[1],
                  (qx, qy, c))]


def _forward_copies(srcs, dsts, ssem, rsem):
    x, y, c = _my_pos()
    half = srcs[0].shape[1] // 2
    mine = pl.ds(pl.multiple_of(c * half, 8), half)
    other = pl.ds(pl.multiple_of((1 - c) * half, 8), half)
    return [_Copy(srcs[0].at[pj, mine], dsts[0].at[pj, mine], dsts[0].at[pj, other], ssem.at[k], rsem.at[k],
                  (x, y, 1 - c)) for k, (_, pj) in enumerate(_peer_chips(x, y, c))]


def _pass_relay_copies(srcs, dsts, ssem, rsem):
    return _pass_copy(srcs, dsts, ssem, rsem) + _relay_copy(srcs, dsts, ssem, rsem)


def _gather8_copies(srcs, dsts, ssem, rsem):
    x, y, c = _my_pos()
    me = 4 * x + 2 * y + c
    cps = []
    for a in range(len(srcs)):
        for k in range(1, 8):
            tgt = (_flip(x, (k >> 2) & 1), _flip(y, (k >> 1) & 1), _flip(c, k & 1))
            cps.append(_Copy(srcs[a].at[me], dsts[a].at[me], dsts[a].at[4 * tgt[0] + 2 * tgt[1] + tgt[2]],
                             ssem.at[7 * a + k - 1], rsem.at[7 * a + k - 1], tgt))
    return cps


def _gather4_copies(srcs, dsts, ssem, rsem):
    x, y, c = _my_pos()
    j = 2 * x + y
    cps = []
    for a in range(len(srcs)):
        for k, (dev, pj) in enumerate(_peer_chips(x, y, c)):
            cps.append(_Copy(srcs[a].at[j], dsts[a].at[j], dsts[a].at[pj], ssem.at[3 * a + k], rsem.at[3 * a + k], dev))
    return cps


def _to_slot(a, n, i):
    return lax.dynamic_update_slice(jnp.zeros((n,) + a.shape, a.dtype), a[None], (i,) + (0,) * a.ndim)


def _split_start(srcs, land_shapes, n_cp, make, name, after=None):
    ns, nl = len(srcs), len(land_shapes)
    n_in = ns + nl + (after is not None)

    def body(*refs):
        s_in = refs[:ns]
        ssem, rsem = refs[n_in], refs[n_in + 1]
        s_out = refs[n_in + 2:n_in + 2 + ns]
        l_out = refs[n_in + 2 + ns:n_in + 2 + ns + nl]
        token = refs[n_in + 2 + ns + nl]
        for cp in make(s_in, l_out if nl else s_out, ssem, rsem):
            cp.start()
        token[...] = jnp.zeros_like(token)

    lands = [pltpu.with_memory_space_constraint(lax.empty(sh.shape, sh.dtype), pltpu.HBM) for sh in land_shapes]
    out_shape = (pltpu.SemaphoreType.DMA((n_cp,)), pltpu.SemaphoreType.DMA((n_cp,)),
                 *[pltpu.HBM(b.shape, b.dtype) for b in srcs], *[pltpu.HBM(b.shape, b.dtype) for b in land_shapes],
                 SDS((8, 128), f32))
    return _hbm_call(
        body, name=name, out_shape=out_shape, in_specs=[HBM_SPEC] * (ns + nl) + [ANY_SPEC] * (after is not None),
        out_specs=(SEM_SPEC, SEM_SPEC, *[HBM_SPEC] * (ns + nl), VMEM_SPEC),
        input_output_aliases={i: 2 + i for i in range(ns + nl)}, compiler_params=_cp(has_side_effects=DATAFLOW),
    )(*[pltpu.with_memory_space_constraint(b, pltpu.HBM) for b in srcs], *lands, *([] if after is None else [after]))


def _split_wait(ssem, rsem, srcs, lands, after, make, name):
    ns, nl = len(srcs), len(lands)

    def body(*refs):
        s_in, l_in = refs[:ns], refs[ns:ns + nl]
        ssem_ref, rsem_ref = refs[ns + nl], refs[ns + nl + 1]
        for cp in make(s_in, l_in if nl else s_in, ssem_ref, rsem_ref):
            cp.wait_send()
            cp.wait_recv()

    outs = _hbm_call(
        body, name=name, out_shape=tuple(pltpu.HBM(b.shape, b.dtype) for b in (*srcs, *lands)),
        in_specs=[HBM_SPEC] * (ns + nl) + [SEM_SPEC, SEM_SPEC, ANY_SPEC], out_specs=tuple([HBM_SPEC] * (ns + nl)),
        input_output_aliases={i: i for i in range(ns + nl)}, compiler_params=_cp(has_side_effects=DATAFLOW),
    )(*srcs, *lands, ssem, rsem, after)
    return list(outs[:ns]), list(outs[ns:])


def _adaln_shard(cc, w_ada_shard):
    def body(c_ref, w_ref, m_ref, sc_ref):
        sc = _silu(c_ref[...])
        sc_ref[...] = sc
        m_ref[...] = jnp.dot(sc, w_ref[...], precision=HIGHEST, preferred_element_type=f32)

    return _hbm_call(
        body, name="adaln_shard", out_shape=(SDS((16, w_ada_shard.shape[1]), f32), SDS((16, D), f32)),
        in_specs=[VMEM_SPEC, VMEM_SPEC], out_specs=(VMEM_SPEC, VMEM_SPEC), compiler_params=_cp(32),
    )(cc, w_ada_shard)


def _prenorm(xx, norm_g, mrow, b_ada, tm, name, after=None):
    n = xx.shape[0]

    def body(x_ref, g_ref, m_ref, b_ref, after_ref, h_ref):
        x = x_ref[...]
        shift = m_ref[:, 0:D] + b_ref[:, 0:D]
        scale = m_ref[:, D:2 * D] + b_ref[:, D:2 * D]
        r = lax.rsqrt(jnp.mean(x * x, axis=-1, keepdims=True) + RMS_EPS)
        y = (x * r) * g_ref[...]
        h_ref[...] = (y * (1.0 + scale) + shift).astype(bf16)

    row = lambda i: (i, 0)
    fixed = lambda i: (0, 0)
    return _hbm_call(
        body, name=name, out_shape=SDS((n, D), bf16), grid=(n // tm,),
        in_specs=[pl.BlockSpec((tm, D), row), pl.BlockSpec((1, D), fixed), pl.BlockSpec((1, 3 * D), fixed),
                  pl.BlockSpec((1, 3 * D), fixed), ANY_SPEC],
        out_specs=pl.BlockSpec((tm, D), row),
    )(xx, norm_g, mrow, b_ada, b_ada if after is None else after)


def _in_proj_own(h, w_own, jvec):
    tm = 512

    def body(j_ref, h_ref, w_ref, p_ref):
        p_ref[...] = jnp.dot(h_ref[...], w_ref[...].astype(bf16), preferred_element_type=f32)

    grid_spec = pltpu.PrefetchScalarGridSpec(
        num_scalar_prefetch=1, grid=(S // tm,),
        in_specs=[pl.BlockSpec((tm, D), lambda i, j: (i, 0)), pl.BlockSpec((D, D), lambda i, j: (0, 0))],
        out_specs=pl.BlockSpec((tm, D), lambda i, j: (i, j[0])))
    return _hbm_call(body, name="in_proj_own", out_shape=SDS((S, 4 * D), f32), grid_spec=grid_spec,
                     compiler_params=_cp(40))(jvec, h, w_own)


def _in_proj_block(h, w4, p, bvec, name, after=None):
    tm = 512

    def body(b_ref, h_ref, w_ref, p_in_ref, after_ref, p_ref):
        p_ref[...] = jnp.dot(h_ref[...], w_ref[...], preferred_element_type=f32)

    grid_spec = pltpu.PrefetchScalarGridSpec(
        num_scalar_prefetch=1, grid=(S // tm,),
        in_specs=[pl.BlockSpec((tm, D), lambda i, b: (i, 0)), pl.BlockSpec((None, D, D), lambda i, b: (b[0], 0, 0)),
                  ANY_SPEC, ANY_SPEC],
        out_specs=pl.BlockSpec((tm, D), lambda i, b: (i, b[0])))
    return _hbm_call(body, name=name, out_shape=SDS((S, 4 * D), f32), grid_spec=grid_spec,
                     input_output_aliases={3: 0})(bvec, h, w4, p, bvec if after is None else after)


def _ctx_proj(hc, w4):
    def body(h_ref, w0_ref, w1_ref, p_ref):
        hv = h_ref[...]
        p_ref[:, 0:DA] = jnp.dot(hv, w0_ref[:, DA:2 * DA], preferred_element_type=f32)
        p_ref[:, DA:2 * DA] = jnp.dot(hv, w1_ref[:, 0:DA], preferred_element_type=f32)

    return _hbm_call(
        body, name="ctx_proj", out_shape=SDS((L, 2 * DA), f32), grid=(1,),
        in_specs=[pl.BlockSpec((L, D), lambda i: (0, 0)), pl.BlockSpec((None, D, D), lambda i: (0, 0, 0)),
                  pl.BlockSpec((None, D, D), lambda i: (1, 0, 0))],
        out_specs=pl.BlockSpec((L, 2 * DA), lambda i: (0, 0)),
    )(hc, w4, w4)


def _head_ones():
    r = lax.broadcasted_iota(i32, (DA, DA), 0) // DH
    c = lax.broadcasted_iota(i32, (DA, DA), 1) // DH
    return (r == c).astype(bf16)


def _head_sum(v, ones_bd):
    hi = v.astype(bf16)
    lo = (v - hi.astype(f32)).astype(bf16)
    return jnp.dot(hi, ones_bd, preferred_element_type=f32) + jnp.dot(lo, ones_bd, preferred_element_type=f32)


def _swap16(v):
    lane = lax.broadcasted_iota(i32, v.shape, 1)
    return jnp.where((lane & 31) < 16, pltpu.roll(v, DA - 16, 1), pltpu.roll(v, 16, 1))


def _rope_block(ct_ref, rt_ref, tm):
    rows = [jnp.tile(rt_ref[8 * j:8 * j + 8, :], (GW // 8, 1)) for j in range(tm // GW)]
    return jnp.tile(ct_ref[...], (tm // GW, 1)) + jnp.concatenate(rows, axis=0)


def _rope_specs(tm):
    col = pl.BlockSpec((GW, DA), lambda i: (0, 0))
    row = pl.BlockSpec((8 * tm // GW, DA), lambda i: (i, 0))
    return [col, row, col, row]


def _qk_prep(p, gq, gk, rope):
    tm = 512

    def body(qk_ref, v_ref, gq_ref, gk_ref, cc_ref, cr_ref, sc_ref, sr_ref, qr_ref, qp_ref, kr_ref, vh_ref):
        ones_bd = _head_ones()
        cs, sn = _rope_block(cc_ref, cr_ref, tm), _rope_block(sc_ref, sr_ref, tm)
        q = qk_ref[:, 0:DA]
        k = qk_ref[:, DA:2 * DA]
        yq = (q * lax.rsqrt(_head_sum(q * q, ones_bd) * (1.0 / DH) + RMS_EPS)) * gq_ref[...]
        yk = (k * lax.rsqrt(_head_sum(k * k, ones_bd) * (1.0 / DH) + RMS_EPS)) * gk_ref[...]
        qr = (yq * cs + _swap16(yq) * sn) * QK_SCALE
        qp = yq * QK_SCALE
        kr = yk * cs + _swap16(yk) * sn
        vv = v_ref[...]
        for hh in range(H):
            sl = slice(hh * DH, (hh + 1) * DH)
            qr_ref[hh] = qr[:, sl].astype(bf16)
            qp_ref[hh] = qp[:, sl].astype(bf16)
            kr_ref[hh] = kr[:, sl].astype(bf16)
            vh_ref[hh] = vv[:, sl].astype(bf16)

    hm = SDS((H, S, DH), bf16)
    hspec = pl.BlockSpec((H, tm, DH), lambda i: (0, i, 0))
    fixed = lambda i: (0, 0)
    return _hbm_call(
        body, name="qk_prep", out_shape=(hm, hm, hm, hm), grid=(S // tm,),
        in_specs=[pl.BlockSpec((tm, 2 * DA), lambda i: (i, 0)), pl.BlockSpec((tm, DA), lambda i: (i, 2)),
                  pl.BlockSpec((1, DA), fixed), pl.BlockSpec((1, DA), fixed)] + _rope_specs(tm),
        out_specs=(hspec, hspec, hspec, hspec),
    )(p, p, gq, gk, *rope)


def _ctx_prep(pc, gk):
    def body(p_ref, gk_ref, kc_ref, vc_ref):
        ones_bd = _head_ones()
        k = p_ref[:, 0:DA]
        yk = (k * lax.rsqrt(_head_sum(k * k, ones_bd) * (1.0 / DH) + RMS_EPS)) * gk_ref[...]
        vv = p_ref[:, DA:2 * DA]
        for hh in range(H):
            sl = slice(hh * DH, (hh + 1) * DH)
            kc_ref[hh] = yk[:, sl].astype(bf16)
            vc_ref[hh] = vv[:, sl].astype(bf16)

    hm = SDS((H, L, DH), bf16)
    return _hbm_call(
        body, name="ctx_prep", out_shape=(hm, hm), in_specs=[VMEM_SPEC, VMEM_SPEC], out_specs=(VMEM_SPEC, VMEM_SPEC),
    )(pc, gk)


def _tile_pieces():
    out = []
    for (i0, u0) in TILE_GEOM:
        rows = []
        for j in range(2):
            i = i0 + j
            rs = _row_start(i)
            rows.append([(u0 + u - i + WIN_H - 1) if rs <= u0 + u < rs + WIN_H else None for u in range(KR)])
        out.append(rows)
    return out


def _bias_prep(rpb_rev_pad, after=None):
    pieces = _tile_pieces()

    def body(r_ref, after_ref, o_ref):
        rp = r_ref[...]
        xs = jnp.concatenate([pltpu.roll(jnp.broadcast_to(rp[dr:dr + 1, :], (GW, 128)), 128 - (WIN_W - 1), 1,
                                         stride=1, stride_axis=0) for dr in range(N_DR)], axis=0)
        row = lax.broadcasted_iota(i32, xs.shape, 0)
        lane = lax.broadcasted_iota(i32, xs.shape, 1)
        k = row & (GW - 1)
        c0 = jnp.clip(lane - WIN_W // 2, 0, GW - WIN_W)
        xs = jnp.where((k >= c0) & (k < c0 + WIN_W), xs, NEG)
        neg = jnp.full((GW, GW), NEG, f32)
        for t in range(NT):
            for j in range(2):
                for u in range(KR):
                    dr = pieces[t][j][u]
                    piece = neg if dr is None else xs[dr * GW:(dr + 1) * GW, 0:GW]
                    o_ref[t, u * GW:(u + 1) * GW, j * GW:(j + 1) * GW] = piece

    return _hbm_call(
        body, name="bias_prep", out_shape=SDS((H, NT, KB, QB), f32), grid=(H,),
        in_specs=[pl.BlockSpec((None, N_DR, 128), lambda h: (h, 0, 0)), ANY_SPEC],
        out_specs=pl.BlockSpec((None, NT, KB, QB), lambda h: (h, 0, 0, 0)),
    )(rpb_rev_pad, rpb_rev_pad if after is None else after)


def _bias_tiles(rpb2, after=None):
    return _bias_prep(jnp.pad(rpb2[:, :, ::-1], ((0, 0), (0, 0), (0, 128 - N_DC))), after)


def _block_geom(b):
    qs = b * QB
    ks = min(max(2 * b - 4, 0), ROWS - KR) * GW
    t = b if b < 2 else (b - (NQB - NT) if b > NQB - 3 else 2)
    return qs, ks, t


def _tt(a, b):
    return lax.dot_general(a, b, (((1,), (1,)), ((), ())), preferred_element_type=f32)


def _tn(a, b):
    return lax.dot_general(a, b, (((0,), (0,)), ((), ())), preferred_element_type=f32)


def _softmax_t(s_lat, s_ctx):
    m = jnp.maximum(jnp.max(s_lat, axis=0, keepdims=True), jnp.max(s_ctx, axis=0, keepdims=True))
    e_lat = jnp.exp(s_lat - m)
    e_ctx = jnp.exp(s_ctx - m)
    inv = 1.0 / (jnp.sum(e_lat, axis=0, keepdims=True) + jnp.sum(e_ctx, axis=0, keepdims=True))
    return e_lat * inv, e_ctx * inv


def _staged(n_blocks, stages):
    held = [dict() for _ in stages]
    for step in range(n_blocks + len(stages) - 1):
        for s, fn in enumerate(stages):
            b = step - s
            if 0 <= b < n_blocks:
                held[s][b] = fn(b) if s == 0 else fn(b, held[s - 1].pop(b))


def _attn_fwd(qr, qp, kr, vh, kc, vc, btt):
    def body(qr_ref, qp_ref, kr_ref, v_ref, kc_ref, vc_ref, bt_ref, o_ref):
        kcv, vcv = kc_ref[...], vc_ref[...]

        def scores(b):
            qs, ks, t = _block_geom(b)
            return (_tt(kr_ref[ks:ks + KB, :], qr_ref[qs:qs + QB, :]) + bt_ref[t], _tt(kcv, qp_ref[qs:qs + QB, :]))

        def probs(b, sc):
            p_lat, p_ctx = _softmax_t(*sc)
            return p_lat.astype(bf16), p_ctx.astype(bf16)

        def values(b, p):
            qs, ks, _ = _block_geom(b)
            o_ref[qs:qs + QB, :] = _tn(p[0], v_ref[ks:ks + KB, :]) + _tn(p[1], vcv)

        _staged(NQB, (scores, probs, values))

    sq = pl.BlockSpec((None, S, DH), lambda h: (h, 0, 0))
    sc = pl.BlockSpec((None, L, DH), lambda h: (h, 0, 0))
    return _hbm_call(
        body, name="attn_fwd", out_shape=SDS((H, S, DH), f32), grid=(H,),
        in_specs=[sq, sq, sq, sq, sc, sc, pl.BlockSpec((None, NT, KB, QB), lambda h: (h, 0, 0, 0))],
        out_specs=sq, compiler_params=_cp(48),
    )(qr, qp, kr, vh, kc, vc, btt)


def _shift_rows(v, down):
    n = v.shape[0]
    row = lax.broadcasted_iota(i32, v.shape, 0)
    if down:
        return jnp.where(row == 0, 0.0, pltpu.roll(v, 1, 0))
    return jnp.where(row == n - 1, 0.0, pltpu.roll(v, n - 1, 0))


def _conv_specs():
    col = lambda off: pl.BlockSpec((S, 128), lambda i, off=off: (0, off + i))
    return [col(16), col(20), col(24), col(28), pl.BlockSpec((3, 128), lambda i: (0, i)),
            pl.BlockSpec((1, 128), lambda i: (0, i))]


def _conv_fwd(p, conv_w, conv_b, after=None):
    def body(u_ref, bg_ref, cg_ref, zc_ref, w_ref, b_ref, after_ref, o_ref):
        cu = cg_ref[...] * u_ref[...]
        cv = b_ref[...] + _shift_rows(cu, True) * w_ref[0:1, :]
        cv = cv + cu * w_ref[1:2, :]
        cv = cv + _shift_rows(cu, False) * w_ref[2:3, :]
        o_ref[...] = ((bg_ref[...] * cv) * _silu(zc_ref[...])).astype(bf16)

    return _hbm_call(
        body, name="conv_fwd", out_shape=SDS((S, DC), bf16), grid=(DC // 128,),
        in_specs=_conv_specs() + [ANY_SPEC], out_specs=pl.BlockSpec((S, 128), lambda i: (0, i)),
        compiler_params=_cp(40),
    )(p, p, p, p, conv_w, conv_b, conv_b if after is None else after)


DP_Q, DP_K, DP_V, DP_ZA, DP_U, DP_BG, DP_CG, DP_ZC = range(8)


def _out_proj_loss(o, p, conv_g, w_out, xx, tgt, mrow, b_ada):
    tm = 512

    def body(o_ref, za_ref, c_ref, w_ref, x_ref, t_ref, m_ref, b_ref,
             dy_ref, dconv_ref, dp_ref, do_ref, gwo_ref, dgate_ref, loss_ref):
        k = pl.program_id(0)

        @pl.when(k == 0)
        def _():
            gwo_ref[...] = jnp.zeros_like(gwo_ref)
            dgate_ref[...] = jnp.zeros_like(dgate_ref)
            loss_ref[0, 0] = 0.0

        gate = m_ref[:, 2 * D:3 * D] + b_ref[:, 2 * D:3 * D]
        za = za_ref[...]
        sz = _silu(za)
        om = _merge_heads(o_ref)
        av, cv = (om * sz).astype(bf16), c_ref[...]
        mo = jnp.dot(av, w_ref[0:DA, :], preferred_element_type=f32)
        mo = mo + jnp.dot(cv, w_ref[DA:DA + DC, :], preferred_element_type=f32)
        y = x_ref[...] + gate * mo
        diff = y - t_ref[...]
        loss_ref[0, 0] += jnp.sum(diff * diff)
        dy = diff * (1.0 / D)
        dy_ref[...] = dy
        dgate_ref[...] += jnp.sum(dy * mo, axis=0, keepdims=True)
        dmo = (dy * gate).astype(bf16)
        dmix = _tt(dmo, w_ref[...])
        dattn = dmix[:, 0:DA]
        dconv_ref[...] = dmix[:, DA:DA + DC]
        a = dattn * sz
        for hh in range(H):
            do_ref[hh] = a[:, hh * DH:(hh + 1) * DH].astype(bf16)
        dp_ref[...] = ((dattn * _dsilu(za)) * om).astype(bf16)
        gwo_ref[0:DA, :] += _tn(av, dmo)
        gwo_ref[DA:DA + DC, :] += _tn(cv, dmo)

    row = lambda i: (i, 0)
    fixed = lambda i: (0, 0)
    hspec = pl.BlockSpec((H, tm, DH), lambda i: (0, i, 0))
    return _hbm_call(
        body, name="out_proj_loss",
        out_shape=(SDS((S, D), f32), SDS((S, DC), f32), SDS((8, S, DA), bf16), SDS((H, S, DH), bf16),
                   SDS((D, D), f32), SDS((1, D), f32), SDS((1, 1), f32)),
        grid=(S // tm,),
        in_specs=[hspec, pl.BlockSpec((tm, DA), lambda i: (i, 3)), pl.BlockSpec((tm, DC), row),
                  pl.BlockSpec((D, D), fixed), pl.BlockSpec((tm, D), row), pl.BlockSpec((tm, D), row),
                  pl.BlockSpec((1, 3 * D), fixed), pl.BlockSpec((1, 3 * D), fixed)],
        out_specs=(pl.BlockSpec((tm, D), row), pl.BlockSpec((tm, DC), row),
                   pl.BlockSpec((None, tm, DA), lambda i: (DP_ZA, i, 0)), hspec, pl.BlockSpec((D, D), fixed),
                   pl.BlockSpec((1, D), fixed), SMEM_SPEC),
        compiler_params=_cp(56, dimension_semantics=("arbitrary",)),
    )(o, p, conv_g, w_out, xx, tgt, mrow, b_ada)


def _conv_bwd(dconv, p, conv_w, conv_b, dp8, after=None):
    def body(d_ref, u_ref, bg_ref, cg_ref, zc_ref, w_ref, b_ref, dp_in_ref, after_ref, dp_ref, gw_ref, gb_ref):
        du_ref, dbg_ref, dcg_ref, dzc_ref = dp_ref.at[0], dp_ref.at[1], dp_ref.at[2], dp_ref.at[3]
        dconv = d_ref[...]
        u, bg, cg, zc = u_ref[...], bg_ref[...], cg_ref[...], zc_ref[...]
        w0, w1, w2 = w_ref[0:1, :], w_ref[1:2, :], w_ref[2:3, :]
        cu = cg * u
        cu_m, cu_p = _shift_rows(cu, True), _shift_rows(cu, False)
        cv = b_ref[...] + cu_m * w0
        cv = cv + cu * w1
        cv = cv + cu_p * w2
        sz = _silu(zc)
        dbg_ref[...] = ((dconv * sz) * cv).astype(bf16)
        dzc_ref[...] = ((dconv * (bg * cv)) * _dsilu(zc)).astype(bf16)
        dcv = (dconv * sz) * bg
        gb_ref[...] = jnp.sum(dcv, axis=0, keepdims=True)
        gw_ref[0:1, :] = jnp.sum(dcv * cu_m, axis=0, keepdims=True)
        gw_ref[1:2, :] = jnp.sum(dcv * cu, axis=0, keepdims=True)
        gw_ref[2:3, :] = jnp.sum(dcv * cu_p, axis=0, keepdims=True)
        gw_ref[3:8, :] = jnp.zeros((5, 128), f32)
        dcu = _shift_rows(dcv, False) * w0 + dcv * w1 + _shift_rows(dcv, True) * w2
        dcg_ref[...] = (dcu * u).astype(bf16)
        du_ref[...] = (dcu * cg).astype(bf16)

    return _hbm_call(
        body, name="conv_bwd", out_shape=(SDS((8, S, DC), bf16), SDS((8, DC), f32), SDS((1, DC), f32)),
        grid=(DC // 128,),
        in_specs=[pl.BlockSpec((S, 128), lambda i: (0, i))] + _conv_specs() + [ANY_SPEC, ANY_SPEC],
        out_specs=(pl.BlockSpec((4, S, 128), lambda i: (DP_U // 4, 0, i)), pl.BlockSpec((8, 128), lambda i: (0, i)),
                   pl.BlockSpec((1, 128), lambda i: (0, i))),
        input_output_aliases={7: 0}, compiler_params=_cp(48),
    )(dconv, p, p, p, p, conv_w, conv_b, dp8, conv_b if after is None else after)


def _attn_bwd(qr, qp, kr, vh, kc, vc, btt, do, after=None):
    def body(qr_ref, qp_ref, kr_ref, v_ref, kc_ref, vc_ref, bt_ref, do_ref, after_ref,
             dqr_ref, dqp_ref, dkr_ref, dv_ref, dkc_ref, dvc_ref, dbt_ref):
        kcv, vcv = kc_ref[...], vc_ref[...]
        dkr_ref[...] = jnp.zeros_like(dkr_ref)
        dv_ref[...] = jnp.zeros_like(dv_ref)
        dbt_ref[...] = jnp.zeros_like(dbt_ref)
        ctx_acc = {}

        def products(b):
            qs, ks, t = _block_geom(b)
            dob = do_ref[qs:qs + QB, :]
            s_lat = _tt(kr_ref[ks:ks + KB, :], qr_ref[qs:qs + QB, :]) + bt_ref[t]
            s_ctx = _tt(kcv, qp_ref[qs:qs + QB, :])
            return s_lat, s_ctx, _tt(v_ref[ks:ks + KB, :], dob), _tt(vcv, dob)

        def score_grads(b, x):
            s_lat, s_ctx, dp_lat, dp_ctx = x
            p_lat, p_ctx = _softmax_t(s_lat, s_ctx)
            delta = jnp.sum(p_lat * dp_lat, axis=0, keepdims=True) + jnp.sum(p_ctx * dp_ctx, axis=0, keepdims=True)
            ds_lat = p_lat * (dp_lat - delta)
            ds_ctx = p_ctx * (dp_ctx - delta)
            return ds_lat, ds_lat.astype(bf16), ds_ctx.astype(bf16), p_lat.astype(bf16), p_ctx.astype(bf16)

        def operand_grads(b, y):
            qs, ks, t = _block_geom(b)
            ds_lat, dsb_lat, dsb_ctx, pb_lat, pb_ctx = y
            qrb, qpb, dob = qr_ref[qs:qs + QB, :], qp_ref[qs:qs + QB, :], do_ref[qs:qs + QB, :]
            dbt_ref[t] += ds_lat
            dqr_ref[qs:qs + QB, :] = _tn(dsb_lat, kr_ref[ks:ks + KB, :])
            dqp_ref[qs:qs + QB, :] = _tn(dsb_ctx, kcv)
            dkr_ref[ks:ks + KB, :] += jnp.dot(dsb_lat, qrb, preferred_element_type=f32)
            dv_ref[ks:ks + KB, :] += jnp.dot(pb_lat, dob, preferred_element_type=f32)
            dkc = jnp.dot(dsb_ctx, qpb, preferred_element_type=f32)
            dvc = jnp.dot(pb_ctx, dob, preferred_element_type=f32)
            ctx_acc["k"] = dkc if b == 0 else ctx_acc["k"] + dkc
            ctx_acc["v"] = dvc if b == 0 else ctx_acc["v"] + dvc

        _staged(NQB, (products, score_grads, operand_grads))
        dkc_ref[...] = ctx_acc["k"]
        dvc_ref[...] = ctx_acc["v"]

    sq = pl.BlockSpec((None, S, DH), lambda h: (h, 0, 0))
    sc = pl.BlockSpec((None, L, DH), lambda h: (h, 0, 0))
    sb = pl.BlockSpec((None, NT, KB, QB), lambda h: (h, 0, 0, 0))
    big, ctxs = SDS((H, S, DH), f32), SDS((H, L, DH), f32)
    return _hbm_call(
        body, name="attn_bwd", out_shape=(big, big, big, big, ctxs, ctxs, SDS((H, NT, KB, QB), f32)), grid=(H,),
        in_specs=[sq, sq, sq, sq, sc, sc, sb, sq, ANY_SPEC], out_specs=(sq, sq, sq, sq, sc, sc, sb),
        compiler_params=_cp(56),
    )(qr, qp, kr, vh, kc, vc, btt, do, do if after is None else after)


def _bias_bwd(dbtt, after=None):
    pieces = _tile_pieces()

    def body(d_ref, after_ref, o_ref, scr):
        scr[...] = jnp.zeros_like(scr)
        acc = [None] * N_DR
        for t in range(NT):
            for j in range(2):
                for u in range(KR):
                    dr = pieces[t][j][u]
                    if dr is None:
                        continue
                    piece = d_ref[t, u * GW:(u + 1) * GW, j * GW:(j + 1) * GW]
                    acc[dr] = piece if acc[dr] is None else acc[dr] + piece
        a = lax.broadcasted_iota(i32, (GW, GW), 0)
        b = lax.broadcasted_iota(i32, (GW, GW), 1)
        flip = (a + b == GW - 1).astype(f32)
        for dr in range(N_DR):
            scr[dr * GW:(dr + 1) * GW, 0:GW] = jnp.dot(acc[dr], flip, precision=HIGHEST, preferred_element_type=f32)
        xs = jnp.concatenate([pltpu.roll(scr[dr * GW:(dr + 1) * GW, :], 128 + (WIN_W - 1) - (GW - 1), 1,
                                         stride=1, stride_axis=0) for dr in range(N_DR)], axis=0)
        tot = jnp.sum(xs.reshape(N_DR, GW, 128), axis=1)
        lane = lax.broadcasted_iota(i32, tot.shape, 1)
        o_ref[...] = jnp.where(lane < N_DC, tot, 0.0)

    return _hbm_call(
        body, name="bias_bwd", out_shape=SDS((H, N_DR, 128), f32), grid=(H,),
        in_specs=[pl.BlockSpec((None, NT, KB, QB), lambda h: (h, 0, 0, 0)), ANY_SPEC],
        out_specs=pl.BlockSpec((None, N_DR, 128), lambda h: (h, 0, 0)),
        scratch_shapes=[pltpu.VMEM((N_DR * GW, 128), f32)],
    )(dbtt, dbtt if after is None else after)


def _merge_heads(ref):
    return jnp.concatenate([ref[hh] for hh in range(H)], axis=1)


def _head_norm_bwd(xraw, gain, dy, ones_bd):
    r = lax.rsqrt(_head_sum(xraw * xraw, ones_bd) * (1.0 / DH) + RMS_EPS)
    xh = xraw * r
    gdy = dy * gain
    dx = r * (gdy - xh * (_head_sum(xh * gdy, ones_bd) * (1.0 / DH)))
    return dx, jnp.sum(dy * xh, axis=0, keepdims=True)


def _qk_bwd(dqr, dqp, dkr, dvh, p, gq, gk, rope, dp8):
    tm = 512

    def body(dqr_ref, dqp_ref, dkr_ref, dv_ref, qk_ref, gq_ref, gk_ref, cc_ref, cr_ref, sc_ref, sr_ref, dp_in_ref,
             dp_ref, ggq_ref, ggk_ref):
        dq_ref, dk_ref, dvo_ref = dp_ref.at[DP_Q], dp_ref.at[DP_K], dp_ref.at[DP_V]

        @pl.when(pl.program_id(0) == 0)
        def _():
            ggq_ref[...] = jnp.zeros_like(ggq_ref)
            ggk_ref[...] = jnp.zeros_like(ggk_ref)

        ones_bd = _head_ones()
        cs, sn = _rope_block(cc_ref, cr_ref, tm), _rope_block(sc_ref, sr_ref, tm)
        a = _merge_heads(dqr_ref)
        dyq = ((a * cs - _swap16(a) * sn) + _merge_heads(dqp_ref)) * QK_SCALE
        bk = _merge_heads(dkr_ref)
        dyk = bk * cs - _swap16(bk) * sn
        dq, gq_part = _head_norm_bwd(qk_ref[:, 0:DA], gq_ref[...], dyq, ones_bd)
        dk, gk_part = _head_norm_bwd(qk_ref[:, DA:2 * DA], gk_ref[...], dyk, ones_bd)
        dq_ref[...] = dq.astype(bf16)
        dk_ref[...] = dk.astype(bf16)
        dvo_ref[...] = _merge_heads(dv_ref).astype(bf16)
        ggq_ref[...] += gq_part
        ggk_ref[...] += gk_part

    hspec = pl.BlockSpec((H, tm, DH), lambda i: (0, i, 0))
    fixed = pl.BlockSpec((1, DA), lambda i: (0, 0))
    return _hbm_call(
        body, name="qk_bwd", out_shape=(SDS((8, S, DA), bf16), SDS((1, DA), f32), SDS((1, DA), f32)), grid=(S // tm,),
        in_specs=[hspec, hspec, hspec, hspec, pl.BlockSpec((tm, 2 * DA), lambda i: (i, 0)), fixed, fixed]
        + _rope_specs(tm) + [ANY_SPEC],
        out_specs=(pl.BlockSpec((3, tm, DA), lambda i: (0, i, 0)), fixed, fixed), input_output_aliases={11: 0},
        compiler_params=_cp(40, dimension_semantics=("arbitrary",)),
    )(dqr, dqp, dkr, dvh, p, gq, gk, *rope, dp8)


def _ctx_bwd(dkc, dvc, pc, gk):
    def body(dkc_ref, dvc_ref, p_ref, gk_ref, dk_ref, dv_ref, ggk_ref):
        ones_bd = _head_ones()
        dk, gk_part = _head_norm_bwd(p_ref[:, 0:DA], gk_ref[...], _merge_heads(dkc_ref), ones_bd)
        dk_ref[...] = dk.astype(bf16)
        dv_ref[...] = _merge_heads(dvc_ref).astype(bf16)
        ggk_ref[...] = gk_part

    piece = SDS((L, DA), bf16)
    return _hbm_call(
        body, name="ctx_bwd", out_shape=(piece, piece, SDS((1, DA), f32)), in_specs=[VMEM_SPEC] * 4,
        out_specs=(VMEM_SPEC,) * 3,
    )(dkc, dvc, pc, gk)


def _grad_w_in(h, dp8, hc, dkc_raw, dvc_m, half, name, after=None):
    def body(half_ref, h_ref, p_ref, hc_ref, dk_ref, dv_ref, after_ref, g_ref):
        j = pl.program_id(0)
        hv = h_ref[...]
        g_ref[:, 0:DA] = _tn(hv, p_ref[0])
        g_ref[:, DA:2 * DA] = _tn(hv, p_ref[1])

        @pl.when(j == 0)
        def _():
            g_ref[:, DA:2 * DA] += _tn(hc_ref[...], dk_ref[...])

        @pl.when(j == 1)
        def _():
            g_ref[:, 0:DA] += _tn(hc_ref[...], dv_ref[...])

    fixed = lambda j, s: (0, 0)
    hd = D // 2
    grid_spec = pltpu.PrefetchScalarGridSpec(
        num_scalar_prefetch=1, grid=(4,),
        in_specs=[pl.BlockSpec((S, hd), lambda j, s: (0, s[0])), pl.BlockSpec((2, S, DA), lambda j, s: (j, 0, 0)),
                  pl.BlockSpec((L, hd), lambda j, s: (0, s[0])), pl.BlockSpec((L, DA), fixed),
                  pl.BlockSpec((L, DA), fixed), ANY_SPEC],
        out_specs=pl.BlockSpec((None, hd, D), lambda j, s: (j, 0, 0)))
    return _hbm_call(
        body, name=name, out_shape=SDS((4, hd, D), f32), grid_spec=grid_spec, compiler_params=_cp(40),
    )(half, h, dp8, hc, dkc_raw, dvc_m, half if after is None else after)


def _grad_w_in_pair_sum(h, dp8, hc, dkc_raw, dvc_m, half, sent, landed, ssem, rsem):
    def body(half_ref, h_ref, p_ref, hc_ref, dk_ref, dv_ref, sent_ref, land_ref, ssem_ref, rsem_ref,
             t32_ref, tb_ref, buf, lsem):
        j = pl.program_id(0)
        hv = h_ref[...]
        t32_ref[:, 0:DA] = _tn(hv, p_ref[0])
        x, y, c = _my_pos()
        arrival = pltpu.make_async_remote_copy(src_ref=sent_ref.at[j], dst_ref=land_ref.at[j], send_sem=ssem_ref.at[j],
                                               recv_sem=rsem_ref.at[j], device_id=(x, y, 1 - c), device_id_type=MESH)
        arrival.wait_recv()
        arrival.wait_send()
        load = pltpu.make_async_copy(land_ref.at[j], buf, lsem)
        load.start()
        t32_ref[:, DA:2 * DA] = _tn(hv, p_ref[1])

        @pl.when(j == 0)
        def _():
            t32_ref[:, DA:2 * DA] += _tn(hc_ref[...], dk_ref[...])

        @pl.when(j == 1)
        def _():
            t32_ref[:, 0:DA] += _tn(hc_ref[...], dv_ref[...])

        load.wait()
        t = t32_ref[...] + buf[...]
        t32_ref[...] = t
        tb_ref[...] = t.astype(bf16)

    fixed = lambda j, s: (0, 0)
    hd = D // 2
    out_spec = pl.BlockSpec((None, hd, D), lambda j, s: (j, 0, 0))
    grid_spec = pltpu.PrefetchScalarGridSpec(
        num_scalar_prefetch=1, grid=(4,),
        in_specs=[pl.BlockSpec((S, hd), lambda j, s: (0, s[0])), pl.BlockSpec((2, S, DA), lambda j, s: (j, 0, 0)),
                  pl.BlockSpec((L, hd), lambda j, s: (0, s[0])), pl.BlockSpec((L, DA), fixed),
                  pl.BlockSpec((L, DA), fixed), HBM_SPEC, HBM_SPEC, SEM_SPEC, SEM_SPEC],
        out_specs=(out_spec, out_spec), scratch_shapes=[pltpu.VMEM((hd, D), f32), pltpu.SemaphoreType.DMA])
    return _hbm_call(
        body, name="grad_w_in_pair_sum", out_shape=(SDS((4, hd, D), f32), SDS((4, hd, D), bf16)), grid_spec=grid_spec,
        compiler_params=_cp(40, has_side_effects=DATAFLOW),
    )(half, h, dp8, hc, dkc_raw, dvc_m, sent, landed, ssem, rsem)


def _norm_mod_bwd(x, dh, g, scale):
    r = lax.rsqrt(jnp.mean(x * x, axis=-1, keepdims=True) + RMS_EPS)
    xh = x * r
    y = xh * g
    dshift = jnp.sum(dh, axis=0, keepdims=True)
    dscale = jnp.sum(dh * y, axis=0, keepdims=True)
    dyn = dh * (1.0 + scale)
    dg = jnp.sum(dyn * xh, axis=0, keepdims=True)
    gdy = dyn * g
    dx = r * (gdy - xh * jnp.mean(xh * gdy, axis=-1, keepdims=True))
    return dx, dshift, dscale, dg


def _dh_grad_x(dp8, w4, xx, dy, norm_g, mrow, b_ada, after=None):
    tm = 512

    def body(p_ref, w_ref, x_ref, dy_ref, g_ref, m_ref, b_ref, after_ref, gx_ref, dsh_ref, dsc_ref, dg_ref):
        @pl.when(pl.program_id(0) == 0)
        def _():
            dsh_ref[...] = jnp.zeros_like(dsh_ref)
            dsc_ref[...] = jnp.zeros_like(dsc_ref)
            dg_ref[...] = jnp.zeros_like(dg_ref)

        dh = None
        for j in range(4):
            for half in range(2):
                term = _tt(p_ref[2 * j + half], w_ref[j, :, half * DA:(half + 1) * DA])
                dh = term if dh is None else dh + term
        scale = m_ref[:, D:2 * D] + b_ref[:, D:2 * D]
        dx, dshift, dscale, dg = _norm_mod_bwd(x_ref[...], dh, g_ref[...], scale)
        gx_ref[...] = dy_ref[...] + dx
        dsh_ref[...] += dshift
        dsc_ref[...] += dscale
        dg_ref[...] += dg

    row = lambda i: (i, 0)
    fixed = lambda i: (0, 0)
    vec = SDS((1, D), f32)
    return _hbm_call(
        body, name="dh_grad_x", out_shape=(SDS((S, D), f32), vec, vec, vec), grid=(S // tm,),
        in_specs=[pl.BlockSpec((8, tm, DA), lambda i: (0, i, 0)), pl.BlockSpec((4, D, D), lambda i: (0, 0, 0)),
                  pl.BlockSpec((tm, D), row), pl.BlockSpec((tm, D), row), pl.BlockSpec((1, D), fixed),
                  pl.BlockSpec((1, 3 * D), fixed), pl.BlockSpec((1, 3 * D), fixed), ANY_SPEC],
        out_specs=(pl.BlockSpec((tm, D), row), pl.BlockSpec((1, D), fixed), pl.BlockSpec((1, D), fixed),
                   pl.BlockSpec((1, D), fixed)),
        compiler_params=_cp(56, dimension_semantics=("arbitrary",)),
    )(dp8, w4, xx, dy, norm_g, mrow, b_ada, b_ada if after is None else after)


def _dhc_sums(dkc_raw, dvc_m, w4, ctx2, norm_g, mrow_c, b_ada, after=None):
    def body(dk_ref, dv_ref, w0_ref, w1_ref, x_ref, g_ref, m_ref, b_ref, after_ref, dsh_ref, dsc_ref, dg_ref):
        dh = _tt(dk_ref[...], w0_ref[:, DA:2 * DA]) + _tt(dv_ref[...], w1_ref[:, 0:DA])
        scale = m_ref[:, D:2 * D] + b_ref[:, D:2 * D]
        _, dshift, dscale, dg = _norm_mod_bwd(x_ref[...], dh, g_ref[...], scale)
        dsh_ref[...] = dshift
        dsc_ref[...] = dscale
        dg_ref[...] = dg

    fixed = lambda i: (0, 0)
    vec = SDS((1, D), f32)
    vspec = pl.BlockSpec((1, D), fixed)
    return _hbm_call(
        body, name="dhc_sums", out_shape=(vec, vec, vec), grid=(1,),
        in_specs=[pl.BlockSpec((L, DA), fixed), pl.BlockSpec((L, DA), fixed),
                  pl.BlockSpec((None, D, D), lambda i: (0, 0, 0)), pl.BlockSpec((None, D, D), lambda i: (1, 0, 0)),
                  pl.BlockSpec((L, D), fixed), vspec, pl.BlockSpec((1, 3 * D), fixed), pl.BlockSpec((1, 3 * D), fixed),
                  ANY_SPEC],
        out_specs=(vspec, vspec, vspec), compiler_params=_cp(32),
    )(dkc_raw, dvc_m, w4, w4, ctx2, norm_g, mrow_c, b_ada, b_ada if after is None else after)


def _rope_tables():
    nf = DH // 4
    inv = np.float32(ROPE_THETA) ** (-np.arange(nf, dtype=np.float32) / np.float32(nf))
    ang_c = np.arange(GW, dtype=np.float32)[:, None] * inv
    ang_r = np.arange(ROWS, dtype=np.float32)[:, None] * inv
    zc, zr = np.zeros((GW, 2 * nf), np.float32), np.zeros((ROWS, 2 * nf), np.float32)
    ct_cos = np.tile(np.concatenate([zc, np.cos(ang_c), np.cos(ang_c)], axis=1), (1, H))
    ct_sin = np.tile(np.concatenate([zc, -np.sin(ang_c), np.sin(ang_c)], axis=1), (1, H))
    rt_cos = np.tile(np.concatenate([np.cos(ang_r), np.cos(ang_r), zr], axis=1), (1, H))
    rt_sin = np.tile(np.concatenate([-np.sin(ang_r), np.sin(ang_r), zr], axis=1), (1, H))
    rep8 = lambda t: np.ascontiguousarray(np.broadcast_to(t[:, None, :], (ROWS, 8, DA))).reshape(ROWS * 8, DA)
    return tuple(jnp.asarray(t, f32) for t in (ct_cos, rep8(rt_cos), ct_sin, rep8(rt_sin)))


def _local_step(xx, ctx2, tgt, mrow, mrow_c, b_ada, norm_g, weights, q_norm_g, k_norm_g, rpb2, conv_w_full, conv_b,
                hooks=None):
    hooks = hooks or {}
    gq = jnp.tile(q_norm_g, (1, H))
    gk = jnp.tile(k_norm_g, (1, H))
    rope = _rope_tables()

    h = _prenorm(xx, norm_g, mrow, b_ada, 256, "prenorm_x", after=weights.get("started"))
    hc = _prenorm(ctx2, norm_g, mrow_c, b_ada, L, "prenorm_ctx")
    jv = weights["jvec"]
    p = _in_proj_own(h, weights["own"], jv)
    btb = _bias_tiles(rpb2, after=p)
    w4, started = weights["near"](btb)
    p = _in_proj_block(h, w4, p, weights["first"], "in_proj_near", after=started)
    w4 = weights["near2"](w4, p)
    p = _in_proj_block(h, w4, p, weights["second"], "in_proj_near2")
    w4, started = weights["far"](w4, p)
    p = _in_proj_block(h, w4, p, jv ^ 3, "in_proj_far", after=started)
    pc = _ctx_proj(hc, w4)
    qr, qp, kr, vh = _qk_prep(p, gq, gk, rope)
    kc, vc = _ctx_prep(pc, gk)
    o = _attn_fwd(qr, qp, kr, vh, kc, vc, btb)
    started = weights["out_arrived"](o) if "out_arrived" in weights else None
    conv_g = _conv_fwd(p, conv_w_full, conv_b, after=started)
    w_out_full = weights["out"](conv_g)
    dy, dconv, dp8, do, g_w_out, dgate, loss_sum = _out_proj_loss(o, p, conv_g, w_out_full, xx, tgt, mrow, b_ada)
    started = hooks["g_w_out"](g_w_out) if "g_w_out" in hooks else None
    dp8, g_conv_w, g_conv_b = _conv_bwd(dconv, p, conv_w_full, conv_b, dp8, after=started)
    started = hooks["after_conv"](dp8) if "after_conv" in hooks else None
    dqr, dqp, dkr, dvh, dkc, dvc, dbtb = _attn_bwd(qr, qp, kr, vh, kc, vc, btb, do, after=started)
    dp8, g_gq, g_gk = _qk_bwd(dqr, dqp, dkr, dvh, p, gq, gk, rope, dp8)
    dkc_raw, dvc_m, g_gk_c = _ctx_bwd(dkc, dvc, pc, gk)
    first = hooks.get("first_half", jnp.zeros((1,), i32))
    g_first = _grad_w_in(h, dp8, hc, dkc_raw, dvc_m, first, "grad_w_in_first")
    started = hooks["g_w_in_first"](g_first) if "g_w_in_first" in hooks else None
    if "w_in_pair_sum" in hooks:
        g_second = None
        started = hooks["w_in_pair_sum"](functools.partial(_grad_w_in_pair_sum, h, dp8, hc, dkc_raw, dvc_m, 1 - first))
    else:
        g_second = _grad_w_in(h, dp8, hc, dkc_raw, dvc_m, 1 - first, "grad_w_in_second", after=started)
    dshift_c, dscale_c, dng_c = _dhc_sums(dkc_raw, dvc_m, w4, ctx2, norm_g, mrow_c, b_ada, after=started)
    g_rpb = _bias_bwd(dbtb, after=dshift_c)
    grad_x, dshift, dscale, dng = _dh_grad_x(dp8, w4, xx, dy, norm_g, mrow, b_ada, after=g_rpb)
    return dict(loss_sum=loss_sum, grad_x=grad_x, g_w_in=(g_first, g_second), g_w_out=g_w_out, g_conv_w=g_conv_w,
                g_conv_b=g_conv_b, g_rpb=g_rpb, g_gq=g_gq, g_gk=g_gk, g_gk_c=g_gk_c, dshift=dshift, dscale=dscale,
                dgate=dgate, dng=dng, dshift_c=dshift_c, dscale_c=dscale_c, dng_c=dng_c)


def _pair_sum_w_out(g, r, cvec):
    hr = D // 8

    def body(c_ref, g0, g1, g2, g3, r_ref, t32_ref, tb_ref):
        for q, g_ref in enumerate((g0, g1, g2, g3)):
            t = g_ref[...] + r_ref[q]
            t32_ref[q] = t
            tb_ref[q] = t.astype(bf16)

    gspecs = [pl.BlockSpec((hr, D), lambda i, c, q=q: (2 * q + c[0], 0)) for q in range(4)]
    full = pl.BlockSpec((4, hr, D), lambda i, c: (0, 0, 0))
    grid_spec = pltpu.PrefetchScalarGridSpec(num_scalar_prefetch=1, grid=(1,), in_specs=gspecs + [full],
                                             out_specs=(full, full))
    return _hbm_call(body, name="pair_sum_w_out", out_shape=(SDS((4, hr, D), f32), SDS((4, hr, D), bf16)),
                          grid_spec=grid_spec)(cvec, g, g, g, g, r)


def _chip_sum(t32, r2, jvec, name):
    rows = t32.shape[1]
    tr = min(rows, 128)

    def body(j_ref, t_ref, r_ref, u_ref):
        u_ref[...] = ((t_ref[...] + r_ref[0].astype(f32)) + r_ref[1].astype(f32)) + r_ref[2].astype(f32)

    grid_spec = pltpu.PrefetchScalarGridSpec(
        num_scalar_prefetch=1, grid=(rows // tr,),
        in_specs=[pl.BlockSpec((None, tr, D), lambda i, j: (j[0], i, 0)),
                  pl.BlockSpec((3, tr, D), lambda i, j: (0, i, 0))],
        out_specs=pl.BlockSpec((tr, D), lambda i, j: (i, 0)))
    return _hbm_call(body, name=name, out_shape=SDS((rows, D), f32), grid_spec=grid_spec)(jvec, t32, r2)


_PK = {}
_off = 0
for _name, _rows in (("dm", 24), ("dmc", 24), ("dng", 8), ("dng_c", 8), ("gq", 8), ("gk", 8), ("gk_c", 8),
                     ("rpb", H * N_DR), ("conv_b", 8), ("conv_w", 16), ("loss", 8)):
    _PK[_name] = (_off, _off + _rows)
    _off += _rows
PK_ROWS = _off
RS_B_ADA, RS_NORM_G, RS_GQ, RS_GK, RS_RPB, RS_CONV_B, RS_CONV_W, RS_DMC, RS_LOSS, RS_ROWS = (
    0, 24, 32, 40, 48, 168, 176, 192, 216, 224)


def _small_reduce(gathered):
    def body(g_ref, o_ref, dm_ref):
        a0 = _PK["dm"][0]
        dm_ref[...] = jnp.zeros_like(dm_ref)
        for b in range(8):
            for i in range(24):
                dm_ref[b:b + 1, 128 * i:128 * (i + 1)] = g_ref[b, a0 + i:a0 + i + 1, :]
        tot = g_ref[0]
        for b in range(1, 8):
            tot = tot + g_ref[b]

        def rows(name):
            a, z = _PK[name]
            return tot[a:z]

        o_ref[RS_B_ADA:RS_B_ADA + 24] = rows("dm") + rows("dmc")
        o_ref[RS_NORM_G:RS_NORM_G + 8] = rows("dng") + rows("dng_c")
        gq = jnp.broadcast_to(jnp.sum(rows("gq"), axis=0, keepdims=True), (8, 128))
        gk = jnp.broadcast_to(jnp.sum(rows("gk") + rows("gk_c"), axis=0, keepdims=True), (8, 128))
        o_ref[RS_GQ:RS_GQ + 8] = gq + pltpu.roll(gq, DH, 1)
        o_ref[RS_GK:RS_GK + 8] = gk + pltpu.roll(gk, DH, 1)
        o_ref[RS_RPB:RS_RPB + H * N_DR] = rows("rpb")
        o_ref[RS_CONV_B:RS_CONV_B + 8] = rows("conv_b")
        o_ref[RS_CONV_W:RS_CONV_W + 16] = rows("conv_w")
        dmc = rows("dmc")
        o_ref[RS_DMC:RS_DMC + 24] = dmc
        o_ref[RS_LOSS:RS_LOSS + 8] = rows("loss")
        for i in range(24):
            dm_ref[8:9, 128 * i:128 * (i + 1)] = dmc[i:i + 1]

    return _hbm_call(body, name="small_reduce", out_shape=(SDS((RS_ROWS, 128), f32), SDS((16, 3 * D), f32)),
                     in_specs=[VMEM_SPEC], out_specs=(VMEM_SPEC, VMEM_SPEC))(gathered)


def _w_ada_grad(sc16, dm16, w_ada_shard, jvec):
    ncol = w_ada_shard.shape[1]

    def body(j_ref, sc_ref, dm_ref, w_ref, g_ref, part_ref):
        dm = dm_ref[...]
        g_ref[...] = lax.dot_general(sc_ref[...], dm, (((0,), (0,)), ((), ())), precision=HIGHEST,
                                     preferred_element_type=f32)
        part_ref[...] = lax.dot_general(dm[8:16], w_ref[...], (((1,), (1,)), ((), ())), precision=HIGHEST,
                                        preferred_element_type=f32)

    fixed = lambda i, j: (0, 0)
    grid_spec = pltpu.PrefetchScalarGridSpec(
        num_scalar_prefetch=1, grid=(1,),
        in_specs=[pl.BlockSpec((16, D), fixed), pl.BlockSpec((16, ncol), lambda i, j: (0, j[0])),
                  pl.BlockSpec((D, ncol), fixed)],
        out_specs=(pl.BlockSpec((D, ncol), fixed), pl.BlockSpec((8, D), fixed)))
    return _pallas_call(body, name="w_ada_grad", out_shape=(SDS((D, ncol), f32), SDS((8, D), f32)),
                        grid_spec=grid_spec, compiler_params=_cp(40))(jvec, sc16, dm16, w_ada_shard)


def _c_ctx_grad(parts4, c_ctx):
    def body(p_ref, c_ref, o_ref):
        tot = ((p_ref[0] + p_ref[1]) + p_ref[2]) + p_ref[3]
        o_ref[...] = tot[0:1] * _dsilu(c_ref[...].reshape(1, D))

    return _pallas_call(body, name="c_ctx_grad", out_shape=SDS((1, D), f32), in_specs=[VMEM_SPEC, VMEM_SPEC],
                        out_specs=VMEM_SPEC)(parts4, c_ctx)


def _adamw(w, g, m, v, name, after=None):
    rows, cols = w.shape
    tr = 256 if rows % 256 == 0 else rows

    def body(w_ref, g_ref, m_ref, v_ref, after_ref, d_ref, m2_ref, v2_ref):
        gv = g_ref[...]
        m2 = ADAM_B1 * m_ref[...] + (1.0 - ADAM_B1) * gv
        v2 = ADAM_B2 * v_ref[...] + (1.0 - ADAM_B2) * jnp.square(gv)
        m_hat = m2 / (1.0 - ADAM_B1 ** ADAM_STEP)
        v_hat = v2 / (1.0 - ADAM_B2 ** ADAM_STEP)
        d_ref[...] = -ADAM_LR * (m_hat / (jnp.sqrt(v_hat) + ADAM_EPS) + ADAM_WD * w_ref[...])
        m2_ref[...] = m2
        v2_ref[...] = v2

    spec = pl.BlockSpec((tr, cols), lambda i: (i, 0))
    shp = SDS((rows, cols), f32)
    return _hbm_call(body, name=name, out_shape=(shp, shp, shp), grid=(rows // tr,), in_specs=[spec] * 4 + [ANY_SPEC],
                     out_specs=(spec, spec, spec))(w, g, m, v, g if after is None else after)


def _adamw_halves(w, g_mine, g_other, m, v, cvec, name, after=None):
    rows, cols = w.shape
    half = rows // 2
    tr = min(256, half)
    per_half = half // tr

    def body(c_ref, w_ref, ga_ref, gb_ref, m_ref, v_ref, after_ref, g_ref, d_ref, m2_ref, v2_ref):
        in_my_half = (pl.program_id(0) // per_half) == c_ref[0]
        gv = jnp.where(in_my_half, ga_ref[...], gb_ref[...])
        g_ref[...] = gv
        m2 = ADAM_B1 * m_ref[...] + (1.0 - ADAM_B1) * gv
        v2 = ADAM_B2 * v_ref[...] + (1.0 - ADAM_B2) * jnp.square(gv)
        m_hat = m2 / (1.0 - ADAM_B1 ** ADAM_STEP)
        v_hat = v2 / (1.0 - ADAM_B2 ** ADAM_STEP)
        d_ref[...] = -ADAM_LR * (m_hat / (jnp.sqrt(v_hat) + ADAM_EPS) + ADAM_WD * w_ref[...])
        m2_ref[...] = m2
        v2_ref[...] = v2

    full = pl.BlockSpec((tr, cols), lambda i, c: (i, 0))
    part = pl.BlockSpec((tr, cols), lambda i, c: (i % per_half, 0))
    shp = SDS((rows, cols), f32)
    grid_spec = pltpu.PrefetchScalarGridSpec(num_scalar_prefetch=1, grid=(rows // tr,),
                                             in_specs=[full, part, part, full, full, ANY_SPEC], out_specs=(full,) * 4)
    return _hbm_call(body, name=name, out_shape=(shp,) * 4, grid_spec=grid_spec)(
        cvec, w, g_mine, g_other, m, v, cvec if after is None else after)


def _adam_math(w, g, m, v):
    m2 = ADAM_B1 * m + (1.0 - ADAM_B1) * g
    v2 = ADAM_B2 * v + (1.0 - ADAM_B2) * jnp.square(g)
    m_hat = m2 / (1.0 - ADAM_B1 ** ADAM_STEP)
    v_hat = v2 / (1.0 - ADAM_B2 ** ADAM_STEP)
    return -ADAM_LR * (m_hat / (jnp.sqrt(v_hat) + ADAM_EPS) + ADAM_WD * w), m2, v2


def _adamw_small(red, g_c_ctx, jvec, ws, ms, vs):
    n = len(ws)

    def body(*refs):
        red_ref, gc_ref, j_ref = refs[:3]
        w_refs, m_refs, v_refs = refs[3:3 + n], refs[3 + n:3 + 2 * n], refs[3 + 2 * n:3 + 3 * n]
        outs = refs[3 + 3 * n:]
        g_out, d_out, m_out, v_out = outs[:n], outs[n:2 * n], outs[2 * n:3 * n], outs[3 * n:]
        chip = j_ref[0]
        lanes = lambda i: (slice(None), slice(128 * i, 128 * (i + 1)))
        row = lambda r0, i: (lambda: red_ref[r0 + i:r0 + i + 1, :])
        whole = (slice(None), slice(None))
        chunks = [
            [((slice(None),), lambda: gc_ref[...].reshape(D))],
            [(lanes(i), row(RS_B_ADA, i)) for i in range(3 * D // 128)],
            [(lanes(i), row(RS_NORM_G, i)) for i in range(D // 128)],
            [(whole, lambda: red_ref[RS_GQ:RS_GQ + 1, 0:DH])],
            [(whole, lambda: red_ref[RS_GK:RS_GK + 1, 0:DH])],
            [((dr,), (lambda dr=dr: red_ref[pl.ds(RS_RPB + dr, H, stride=N_DR), 0:N_DC])) for dr in range(N_DR)],
            [((r,), (lambda r=r: red_ref[pl.ds(RS_CONV_W + 4 * r + chip, 1), :])) for r in range(3)],
            [(lanes(i), row(RS_CONV_B, i)) for i in range(DC // 128)],
        ]
        for a in range(n):
            for idx, grad in chunks[a]:
                g = grad()
                d, m2, v2 = _adam_math(w_refs[a][idx], g, m_refs[a][idx], v_refs[a][idx])
                g_out[a][idx] = g
                d_out[a][idx] = d
                m_out[a][idx] = m2
                v_out[a][idx] = v2

    shapes = [SDS(w.shape, f32) for w in ws]
    res = _pallas_call(body, name="adamw_small", out_shape=shapes * 4,
                       in_specs=[VMEM_SPEC, VMEM_SPEC, SMEM_SPEC] + [VMEM_SPEC] * (3 * n),
                       out_specs=[VMEM_SPEC] * (4 * n))(red, g_c_ctx, jvec, *ws, *ms, *vs)
    return [list(res[k * n:(k + 1) * n]) for k in range(4)]


def _rows128(a):
    return a.reshape(-1, 128)


def kernel(x, c, ctx, c_ctx, w_ada, b_ada, norm_g, w_in, q_norm_g, k_norm_g, rpb, conv_w, conv_b, w_out, loss_target, m_c_ctx, m_w_ada, m_b_ada, m_norm_g, m_w_in, m_q_norm_g, m_k_norm_g, m_rpb, m_conv_w, m_conv_b, m_w_out, v_c_ctx, v_w_ada, v_b_ada, v_norm_g, v_w_in, v_q_norm_g, v_k_norm_g, v_rpb, v_conv_w, v_conv_b, v_w_out):
    xi, yi, ci = lax.axis_index("x"), lax.axis_index("y"), lax.axis_index("c")
    dev = 4 * xi + 2 * yi + ci
    chip = 2 * xi + yi
    cvec = jnp.reshape(ci, (1,)).astype(i32)
    jvec = jnp.reshape(chip, (1,)).astype(i32)
    w_ada_s = w_ada[0]
    ncol = w_ada_s.shape[1]

    gc = _split_start([_to_slot(c.reshape(8, 128), 8, dev)], [], 7, _gather8_copies, "gather_c_start")
    wo4c = _cast_to_slot(w_out[0], jvec, "cast_w_out", after=gc[3])
    w4c = _cast_to_slot(w_in[0], jvec, "cast_w_in", after=wo4c)
    (c8,), _ = _split_wait(gc[0], gc[1], [gc[2]], [], w4c, _gather8_copies, "gather_c_wait")
    cc = jnp.concatenate([c8.reshape(8, D), c_ctx.reshape(1, D), jnp.zeros((7, D), f32)], axis=0)
    m_shard, sc16 = _adaln_shard(cc, w_ada_s)

    conv_w_pad = jnp.pad(conv_w[0], ((0, 5), (0, 0)))
    gm = _split_start([_to_slot(m_shard, 4, chip), _to_slot(conv_w_pad, 4, chip)], [], 6, _gather4_copies,
                      "gather_mod_start")

    all_k = [(0, 0, 0), (0, 0, 1), (0, 0, 2)]
    sem_a, rem_a, w4s, token = _split_start([w4c], [], 1, _near_copies, "weights_near_start", after=gm[4])
    (m4, cw4), _ = _split_wait(gm[0], gm[1], [gm[2], gm[3]], [], token, _gather4_copies, "gather_mod_wait")
    m_full = jnp.transpose(m4, (1, 0, 2)).reshape(16, 4 * ncol)
    mrow = lax.dynamic_slice(m_full, (dev, 0), (1, 3 * D))
    mrow_c = m_full[8:9]
    conv_w_full = jnp.transpose(cw4[:, 0:3, :], (1, 0, 2)).reshape(3, DC)
    waves = {}

    def near(after):
        (w4w,), _ = _split_wait(sem_a, rem_a, [w4s], [], after, _near_copies, "weights_near_wait")
        sem_b, rem_b, w4b, started = _split_start([w4w], [], 2, _pass_relay_copies, "weights_pass_start")
        waves["pass"] = (sem_b, rem_b)
        return w4b, started

    def near2(w4, after):
        (w4w,), _ = _split_wait(*waves["pass"], [w4], [], after, _pass_copy, "weights_pass_wait")
        return w4w

    def far(w4, after):
        (w4w,), _ = _split_wait(*waves["pass"], [w4], [], after, _relay_copy, "weights_far_wait")
        w4x, sem_c, rem_c, wo4s, started = _forward_then_start(w4w, wo4c, all_k, "weights_far_forward_out_start")
        (w4f,), _ = _split_wait(sem_c, rem_c, [w4x], [], started, _diag_forward_copy, "weights_far_forward_wait")
        waves["out"] = (sem_c, rem_c, wo4s)
        return w4f, started

    def w_out_arrived(after):
        sem_c, rem_c, wo4s = waves["out"]
        (wow,) = _halves_wait(sem_c, rem_c, [wo4s], after, all_k, "weights_out_wait")
        sem_d, rem_d, wof, started = _split_start([wow], [], 3, _forward_copies, "weights_out_forward_start")
        waves["out_forward"] = (sem_d, rem_d, wof)
        return started

    def w_out_gathered(after):
        sem_d, rem_d, wof = waves["out_forward"]
        (wo,), _ = _split_wait(sem_d, rem_d, [wof], [], after, _forward_copies, "weights_out_forward_wait")
        return wo.reshape(D, D)

    weights = dict(own=w_in[0], jvec=jvec, started=token, first=jvec ^ (1 + cvec), second=jvec ^ (2 - cvec),
                   near=near, near2=near2, far=far, out_arrived=w_out_arrived, out=w_out_gathered)

    exchange = _exchange_copies
    pending = {}

    def on_g_w_out(g_w_out):
        out = _split_start([g_w_out], [SDS((4, D // 8, D), f32)], 4, exchange, "grad_out_pair_start")
        pending["ex_out"] = out
        return out[4]

    def after_conv(dp8):
        ssem_o, rsem_o, g_o, land_o, _ = pending["ex_out"]
        (g_o,), (ex_o,) = _split_wait(ssem_o, rsem_o, [g_o], [land_o], dp8, exchange, "grad_out_pair_wait")
        to32, tob = _pair_sum_w_out(g_o, ex_o, cvec)
        out = _split_start([tob], [SDS((3, D // 8, D), bf16)], 3, _scatter_copies, "grad_out_chip_start")
        pending["sc_out"] = (out, to32)
        return out[4]

    def on_g_w_in_first(g_first):
        out = _split_start([g_first], [SDS((4, D // 2, D), f32)], 4, _block_exchange_copies, "grad_pair_start")
        pending["ex"] = (out[0], out[1], [out[2]], [out[3]])
        return out[4]

    def on_w_in_pair_sum(pair_sum):
        ex_ssem, ex_rsem, ex_srcs, ex_lands = pending["ex"]
        t32, tb = pair_sum(ex_srcs[0], ex_lands[0], ex_ssem, ex_rsem)
        out = _split_start([tb], [SDS((3, D // 2, D), bf16)], 3, _scatter_copies, "grad_chip_start")
        pending["sc_in"] = (out, t32)
        return out[4]

    r = _local_step(x[0], ctx[0], loss_target[0], mrow, mrow_c, b_ada, norm_g, weights, q_norm_g, k_norm_g,
                    rpb[0], conv_w_full, conv_b,
                    dict(g_w_out=on_g_w_out, after_conv=after_conv, first_half=1 - cvec,
                         g_w_in_first=on_g_w_in_first, w_in_pair_sum=on_w_in_pair_sum))
    sc_in, t32 = pending["sc_in"]
    _, (r2,) = _split_wait(sc_in[0], sc_in[1], [sc_in[2]], [sc_in[3]], r["dng"], _scatter_copies, "grad_chip_wait")
    u_in = _chip_sum(t32, r2, jvec, "chip_sum_w_in")
    sc_o, to32 = pending["sc_out"]
    _, (ro2,) = _split_wait(sc_o[0], sc_o[1], [sc_o[2]], [sc_o[3]], u_in, _scatter_copies, "grad_out_chip_wait")
    u_out = _chip_sum(to32, ro2, jvec, "chip_sum_w_out")
    swap = _split_start([u_in, u_out], [SDS(u_in.shape, f32), SDS(u_out.shape, f32)], 2, _swap_copies,
                        "grad_pair_swap_start")

    dm = jnp.concatenate([r["dshift"], r["dscale"], r["dgate"]], axis=1)
    dmc = jnp.concatenate([r["dshift_c"], r["dscale_c"], jnp.zeros((1, D), f32)], axis=1)
    pack_parts = [_rows128(dm), _rows128(dmc), _rows128(r["dng"]), _rows128(r["dng_c"]), _rows128(r["g_gq"]),
                  _rows128(r["g_gk"]), _rows128(r["g_gk_c"]), r["g_rpb"].reshape(H * N_DR, 128),
                  _rows128(r["g_conv_b"]), _rows128(r["g_conv_w"][0:3]), jnp.pad(r["loss_sum"], ((0, 0), (0, 127)))]
    pack = jnp.concatenate([jnp.pad(p, ((0, -p.shape[0] % 8), (0, 0))) for p in pack_parts], axis=0)
    assert pack.shape[0] == PK_ROWS
    gs = _split_start([_to_slot(pack, 8, dev)], [], 7, _gather8_copies, "gather_small_start", after=swap[6])
    (u_in, u_out), (o_in, o_out) = _split_wait(swap[0], swap[1], swap[2:4], swap[4:6], gs[3], _swap_copies,
                                               "grad_pair_swap_wait")
    g_w_in_s, d_w_in, nm_w_in, nv_w_in = _adamw_halves(w_in[0], u_in, o_in, m_w_in[0], v_w_in[0], cvec, "adamw_w_in",
                                                       after=gs[3])
    g_w_out_s, d_w_out, nm_w_out, nv_w_out = _adamw_halves(w_out[0], u_out, o_out, m_w_out[0], v_w_out[0], cvec,
                                                           "adamw_w_out", after=nm_w_in)
    (gathered,), _ = _split_wait(gs[0], gs[1], [gs[2]], [], nm_w_out, _gather8_copies, "gather_small_wait")
    red, dm16 = _small_reduce(gathered)
    loss = red[RS_LOSS, 0] * (0.5 / D)

    g_w_ada_s, cpart = _w_ada_grad(sc16, dm16, w_ada_s, jvec)
    gcp = _split_start([_to_slot(cpart, 4, chip)], [], 3, _gather4_copies, "gather_c_ctx_parts_start")
    d_w_ada, nm_w_ada, nv_w_ada = _adamw(w_ada_s, g_w_ada_s, m_w_ada[0], v_w_ada[0], "adamw_w_ada", after=gcp[3])
    (cparts4,), _ = _split_wait(gcp[0], gcp[1], [gcp[2]], [], nm_w_ada, _gather4_copies, "gather_c_ctx_parts_wait")
    g_c_ctx = _c_ctx_grad(cparts4, c_ctx)

    t_rpb = lambda a: jnp.transpose(a, (0, 2, 1, 3)).reshape(N_DR, H, N_DC)
    t_cw = lambda a: jnp.transpose(a, (1, 0, 2))
    small = _adamw_small(
        red, g_c_ctx, jvec,
        [c_ctx, b_ada, norm_g, q_norm_g, k_norm_g, t_rpb(rpb), t_cw(conv_w), conv_b],
        [m_c_ctx, m_b_ada, m_norm_g, m_q_norm_g, m_k_norm_g, t_rpb(m_rpb), t_cw(m_conv_w), m_conv_b],
        [v_c_ctx, v_b_ada, v_norm_g, v_q_norm_g, v_k_norm_g, t_rpb(v_rpb), t_cw(v_conv_w), v_conv_b])
    for kind in small:
        kind[5] = jnp.transpose(kind[5].reshape(1, N_DR, H, N_DC), (0, 2, 1, 3))
        kind[6] = jnp.transpose(kind[6], (1, 0, 2))

    def ordered(kind, big_w_ada, big_w_in, big_w_out):
        s_c_ctx, s_b_ada, s_norm_g, s_q, s_k, s_rpb, s_conv_w, s_conv_b = small[kind]
        return [s_c_ctx, big_w_ada[None], s_b_ada, s_norm_g, big_w_in[None], s_q, s_k, s_rpb, s_conv_w,
                s_conv_b, big_w_out[None]]

    grads = ordered(0, g_w_ada_s, g_w_in_s, g_w_out_s)
    deltas = ordered(1, d_w_ada, d_w_in, d_w_out)
    new_m = ordered(2, nm_w_ada, nm_w_in, nm_w_out)
    new_v = ordered(3, nv_w_ada, nv_w_in, nv_w_out)
    return (loss, r["grad_x"][None], *grads, *deltas, *new_m, *new_v)
```

```python
import functools

import jax
import jax.numpy as jnp
import numpy as np
from jax import lax
from jax.experimental import pallas as pl
from jax.experimental.pallas import tpu as pltpu

f32, bf16, i32 = jnp.float32, jnp.bfloat16, jnp.int32
MESH = pl.DeviceIdType.MESH
HIGHEST = lax.Precision.HIGHEST

D = 1024
S = 2048
L = 256
GW = 64
ROWS = S // GW
H = 8
DH = 64
DA = H * DH
DC = 512
WIN_H, WIN_W = 8, 16
N_DR, N_DC = 2 * WIN_H - 1, 2 * WIN_W - 1
RMS_EPS = 1e-6
ROPE_THETA = 10000.0
QK_SCALE = DH ** -0.5
NEG = -1e30

QB = 128
NQB = S // QB
KR = 9
KB = KR * GW
TILE_GEOM = ((0, 0), (2, 0), (4, 0), (28, 23), (30, 23))
NT = len(TILE_GEOM)

ADAM_LR, ADAM_B1, ADAM_B2, ADAM_EPS, ADAM_WD, ADAM_STEP = 0.001, 0.9, 0.999, 1e-08, 0.01, 10

VMEM_SPEC = pl.BlockSpec(memory_space=pltpu.VMEM)
ANY_SPEC = pl.BlockSpec(memory_space=pl.ANY)
SMEM_SPEC = pl.BlockSpec(memory_space=pltpu.SMEM)
SDS = jax.ShapeDtypeStruct


_pallas_call = pl.pallas_call


def _hbm_call(body, *, out_shape, in_specs=None, out_specs=None, grid_spec=None, **kw):
    n_pre = 0
    if grid_spec is not None:
        ispecs, ospecs, n_pre = grid_spec.in_specs, grid_spec.out_specs, grid_spec.num_scalar_prefetch
        kw["grid_spec"] = grid_spec
    else:
        ispecs, ospecs = in_specs, out_specs
        kw.update(in_specs=in_specs, out_specs=out_specs)

    def blocked(spec):
        return isinstance(spec, pl.BlockSpec) and spec.block_shape is not None

    single = not isinstance(out_shape, (tuple, list))
    shapes = [out_shape] if single else list(out_shape)
    ospec_list = list(ospecs) if isinstance(ospecs, (tuple, list)) else [ospecs]
    shapes = [pltpu.HBM(s.shape, s.dtype) if blocked(sp) else s for s, sp in zip(shapes, ospec_list)]
    call = _pallas_call(body, out_shape=shapes[0] if single else tuple(shapes), **kw)

    def run(*args):
        arrays = [pltpu.with_memory_space_constraint(a, pltpu.HBM) if blocked(sp) else a
                  for a, sp in zip(args[n_pre:], ispecs)]
        return call(*args[:n_pre], *arrays)

    return run


def _cp(vmem_mb=None, **kw):
    if vmem_mb is not None:
        kw["vmem_limit_bytes"] = vmem_mb << 20
    return pltpu.CompilerParams(**kw)


def _silu(z):
    return z * jax.nn.sigmoid(z)


def _dsilu(z):
    sg = jax.nn.sigmoid(z)
    return sg * (1.0 + z * (1.0 - sg))


def _row_start(i):
    return min(max(i - WIN_H // 2, 0), ROWS - WIN_H)


def _my_pos():
    return lax.axis_index("x"), lax.axis_index("y"), lax.axis_index("c")


def _flip(v, bit):
    return 1 - v if bit else v


def _swap_copies(srcs, lands, ssem, rsem):
    x, y, c = _my_pos()
    return [pltpu.make_async_remote_copy(src_ref=srcs[a], dst_ref=lands[a], send_sem=ssem.at[a], recv_sem=rsem.at[a],
                                         device_id=(x, y, 1 - c), device_id_type=MESH) for a in range(len(srcs))]


HBM_SPEC = pl.BlockSpec(memory_space=pltpu.HBM)
SEM_SPEC = pl.BlockSpec(memory_space=pltpu.SEMAPHORE)
DATAFLOW = pltpu.SideEffectType.DATAFLOW_SIDE_EFFECTING


def _peer_chips(x, y, c):
    out = []
    for k in range(1, 4):
        px, py = _flip(x, (k >> 1) & 1), _flip(y, k & 1)
        out.append(((px, py, c), 2 * px + py))
    return out


def _half_copies(srcs, dsts, ssem, rsem, which):
    x, y, c = _my_pos()
    j = 2 * x + y
    peers = _peer_chips(x, y, c)
    pairs = []
    for pos, group, k in which:
        half = srcs[pos].shape[1] // 2
        mine = pl.ds(pl.multiple_of(c * half, 8), half)
        dev, pj = peers[k]
        sem = 3 * group + k
        send = pltpu.make_async_remote_copy(src_ref=srcs[pos].at[j, mine], dst_ref=dsts[pos].at[j, mine],
                                            send_sem=ssem.at[sem], recv_sem=rsem.at[sem], device_id=dev,
                                            device_id_type=MESH)
        arrive = pltpu.make_async_remote_copy(src_ref=srcs[pos].at[j, mine], dst_ref=dsts[pos].at[pj, mine],
                                              send_sem=ssem.at[sem], recv_sem=rsem.at[sem], device_id=dev,
                                              device_id_type=MESH)
        pairs.append((send, arrive))
    return pairs


def _halves_wait(ssem, rsem, bigs, after, which, name):
    nb = len(bigs)

    def body(*refs):
        b_in = refs[:nb]
        ssem_ref, rsem_ref = refs[nb], refs[nb + 1]
        for send, arrive in _half_copies(b_in, b_in, ssem_ref, rsem_ref, which):
            send.wait_send()
            arrive.wait_recv()

    return _hbm_call(
        body, name=name, out_shape=tuple(pltpu.HBM(b.shape, b.dtype) for b in bigs),
        in_specs=[HBM_SPEC] * nb + [SEM_SPEC, SEM_SPEC, ANY_SPEC], out_specs=tuple([HBM_SPEC] * nb),
        input_output_aliases={a: a for a in range(nb)}, compiler_params=_cp(has_side_effects=DATAFLOW),
    )(*bigs, ssem, rsem, after)


FORWARD_SEM = 3


def _diag_forward_copy(srcs, dsts, ssem, rsem):
    x, y, c = _my_pos()
    half = srcs[0].shape[1] // 2
    diag = 3 - (2 * x + y)
    mine = pl.ds(pl.multiple_of(c * half, 8), half)
    other = pl.ds(pl.multiple_of((1 - c) * half, 8), half)
    return [_Copy(srcs[0].at[diag, mine], dsts[0].at[diag, mine], dsts[0].at[diag, other], ssem.at[FORWARD_SEM],
                  rsem.at[FORWARD_SEM], (x, y, 1 - c))]


def _forward_then_start(fwd, big, order, name):
    def body(f_in, b_in, f_out, ssem, rsem, b_out, token):
        _diag_forward_copy([f_in], [f_out], ssem, rsem)[0].start()
        for send, _ in _half_copies([b_in], [b_out], ssem, rsem, order):
            send.start()
        token[...] = jnp.zeros_like(token)

    n_sem = FORWARD_SEM + 1
    out_shape = (pltpu.HBM(fwd.shape, fwd.dtype), pltpu.SemaphoreType.DMA((n_sem,)), pltpu.SemaphoreType.DMA((n_sem,)),
                 pltpu.HBM(big.shape, big.dtype), SDS((8, 128), f32))
    return _hbm_call(
        body, name=name, out_shape=out_shape, in_specs=[HBM_SPEC, HBM_SPEC],
        out_specs=(HBM_SPEC, SEM_SPEC, SEM_SPEC, HBM_SPEC, VMEM_SPEC), input_output_aliases={0: 0, 1: 3},
        compiler_params=_cp(has_side_effects=DATAFLOW),
    )(*[pltpu.with_memory_space_constraint(b, pltpu.HBM) for b in (fwd, big)])


def _cast_to_slot(w, jvec, name, after=None):
    rows, cols = w.shape
    tr = 256

    def body(j_ref, w_ref, after_ref, o_ref):
        o_ref[...] = w_ref[...].astype(bf16)

    grid_spec = pltpu.PrefetchScalarGridSpec(
        num_scalar_prefetch=1, grid=(rows // tr,),
        in_specs=[pl.BlockSpec((tr, cols), lambda i, j: (i, 0)), ANY_SPEC],
        out_specs=pl.BlockSpec((None, tr, cols), lambda i, j: (j[0], i, 0)))
    return _hbm_call(body, name=name, out_shape=SDS((4, rows, cols), bf16),
                     grid_spec=grid_spec)(jvec, w, jvec if after is None else after)


def _exchange_copies(srcs, lands, ssem, rsem):
    x, y, c = _my_pos()
    half = srcs[0].shape[0] // 8
    cps = []
    for jb in range(4):
        src = srcs[0].at[pl.ds(pl.multiple_of((2 * jb + 1 - c) * half, 8), half)]
        cps.append(pltpu.make_async_remote_copy(src_ref=src, dst_ref=lands[0].at[jb], send_sem=ssem.at[jb],
                                                recv_sem=rsem.at[jb], device_id=(x, y, 1 - c), device_id_type=MESH))
    return cps


def _block_exchange_copies(srcs, lands, ssem, rsem):
    x, y, c = _my_pos()
    return [pltpu.make_async_remote_copy(src_ref=srcs[0].at[jb], dst_ref=lands[0].at[jb], send_sem=ssem.at[jb],
                                         recv_sem=rsem.at[jb], device_id=(x, y, 1 - c), device_id_type=MESH)
            for jb in range(4)]


def _scatter_copies(srcs, lands, ssem, rsem):
    x, y, c = _my_pos()
    cps = []
    for a in range(len(srcs)):
        for k, (dev, pj) in enumerate(_peer_chips(x, y, c)):
            cps.append(pltpu.make_async_remote_copy(src_ref=srcs[a].at[pj], dst_ref=lands[a].at[k],
                                                    send_sem=ssem.at[3 * a + k], recv_sem=rsem.at[3 * a + k],
                                                    device_id=dev, device_id_type=MESH))
    return cps


class _Copy:
    def __init__(self, src, dst, arrive, ssem, rsem, dev):
        make = lambda to: pltpu.make_async_remote_copy(src_ref=src, dst_ref=to, send_sem=ssem, recv_sem=rsem,
                                                       device_id=dev, device_id_type=MESH)
        send, arrival = make(dst), make(arrive)
        self.start, self.wait_send, self.wait_recv = send.start, send.wait_send, arrival.wait_recv


def _toward(x, y, along_x):
    return x + along_x * (1 - 2 * x), y + (1 - along_x) * (1 - 2 * y)


def _near_copies(srcs, dsts, ssem, rsem):
    x, y, c = _my_pos()
    px, py = _toward(x, y, c)
    j = 2 * x + y
    return [_Copy(srcs[0].at[j], dsts[0].at[j], dsts[0].at[2 * px + py], ssem.at[0], rsem.at[0], (px, py, c))]


def _pass_copy(srcs, dsts, ssem, rsem):
    x, y, c = _my_pos()
    px, py = _toward(x, y, c)
    qx, qy = _toward(x, y, 1 - c)
    got = 2 * px + py
    return [_Copy(srcs[0].at[got], dsts[0].at[got], dsts[0].at[2 * qx + qy], ssem.at[0], rsem.at[0], (x, y, 1 - c))]


def _relay_copy(srcs, dsts, ssem, rsem):
    x, y, c = _my_pos()
    px, py = _toward(x, y, c)
    qx, qy = _toward(x, y, 1 - c)
    half = srcs[0].shape[1] // 2
    mine = pl.ds(pl.multiple_of(c * half, 8), half)
    got, diag = 2 * px + py, 3 - (2 * x + y)
    return [_Copy(srcs[0].at[got, mine], dsts[0].at[got, mine], dsts[0].at[diag, mine], ssem.at[1], rsem.at[1],
                  (qx, qy, c))]


def _forward_copies(srcs, dsts, ssem, rsem):
    x, y, c = _my_pos()
    half = srcs[0].shape[1] // 2
    mine = pl.ds(pl.multiple_of(c * half, 8), half)
    other = pl.ds(pl.multiple_of((1 - c) * half, 8), half)
    return [_Copy(srcs[0].at[pj, mine], dsts[0].at[pj, mine], dsts[0].at[pj, other], ssem.at[k], rsem.at[k],
                  (x, y, 1 - c)) for k, (_, pj) in enumerate(_peer_chips(x, y, c))]


def _pass_relay_copies(srcs, dsts, ssem, rsem):
    return _pass_copy(srcs, dsts, ssem, rsem) + _relay_copy(srcs, dsts, ssem, rsem)


def _gather8_copies(srcs, dsts, ssem, rsem):
    x, y, c = _my_pos()
    me = 4 * x + 2 * y + c
    cps = []
    for a in range(len(srcs)):
        for k in range(1, 8):
            tgt = (_flip(x, (k >> 2) & 1), _flip(y, (k >> 1) & 1), _flip(c, k & 1))
            cps.append(_Copy(srcs[a].at[me], dsts[a].at[me], dsts[a].at[4 * tgt[0] + 2 * tgt[1] + tgt[2]],
                             ssem.at[7 * a + k - 1], rsem.at[7 * a + k - 1], tgt))
    return cps


def _gather4_copies(srcs, dsts, ssem, rsem):
    x, y, c = _my_pos()
    j = 2 * x + y
    cps = []
    for a in range(len(srcs)):
        for k, (dev, pj) in enumerate(_peer_chips(x, y, c)):
            cps.append(_Copy(srcs[a].at[j], dsts[a].at[j], dsts[a].at[pj], ssem.at[3 * a + k], rsem.at[3 * a + k], dev))
    return cps


def _to_slot(a, n, i):
    return lax.dynamic_update_slice(jnp.zeros((n,) + a.shape, a.dtype), a[None], (i,) + (0,) * a.ndim)


def _split_start(srcs, land_shapes, n_cp, make, name, after=None):
    ns, nl = len(srcs), len(land_shapes)
    n_in = ns + nl + (after is not None)

    def body(*refs):
        s_in = refs[:ns]
        ssem, rsem = refs[n_in], refs[n_in + 1]
        s_out = refs[n_in + 2:n_in + 2 + ns]
        l_out = refs[n_in + 2 + ns:n_in + 2 + ns + nl]
        token = refs[n_in + 2 + ns + nl]
        for cp in make(s_in, l_out if nl else s_out, ssem, rsem):
            cp.start()
        token[...] = jnp.zeros_like(token)

    lands = [pltpu.with_memory_space_constraint(lax.empty(sh.shape, sh.dtype), pltpu.HBM) for sh in land_shapes]
    out_shape = (pltpu.SemaphoreType.DMA((n_cp,)), pltpu.SemaphoreType.DMA((n_cp,)),
                 *[pltpu.HBM(b.shape, b.dtype) for b in srcs], *[pltpu.HBM(b.shape, b.dtype) for b in land_shapes],
                 SDS((8, 128), f32))
    return _hbm_call(
        body, name=name, out_shape=out_shape, in_specs=[HBM_SPEC] * (ns + nl) + [ANY_SPEC] * (after is not None),
        out_specs=(SEM_SPEC, SEM_SPEC, *[HBM_SPEC] * (ns + nl), VMEM_SPEC),
        input_output_aliases={i: 2 + i for i in range(ns + nl)}, compiler_params=_cp(has_side_effects=DATAFLOW),
    )(*[pltpu.with_memory_space_constraint(b, pltpu.HBM) for b in srcs], *lands, *([] if after is None else [after]))


def _split_wait(ssem, rsem, srcs, lands, after, make, name):
    ns, nl = len(srcs), len(lands)

    def body(*refs):
        s_in, l_in = refs[:ns], refs[ns:ns + nl]
        ssem_ref, rsem_ref = refs[ns + nl], refs[ns + nl + 1]
        for cp in make(s_in, l_in if nl else s_in, ssem_ref, rsem_ref):
            cp.wait_send()
            cp.wait_recv()

    outs = _hbm_call(
        body, name=name, out_shape=tuple(pltpu.HBM(b.shape, b.dtype) for b in (*srcs, *lands)),
        in_specs=[HBM_SPEC] * (ns + nl) + [SEM_SPEC, SEM_SPEC, ANY_SPEC], out_specs=tuple([HBM_SPEC] * (ns + nl)),
        input_output_aliases={i: i for i in range(ns + nl)}, compiler_params=_cp(has_side_effects=DATAFLOW),
    )(*srcs, *lands, ssem, rsem, after)
    return list(outs[:ns]), list(outs[ns:])


def _adaln_shard(cc, w_ada_shard):
    def body(c_ref, w_ref, m_ref, sc_ref):
        sc = _silu(c_ref[...])
        sc_ref[...] = sc
        m_ref[...] = jnp.dot(sc, w_ref[...], precision=HIGHEST, preferred_element_type=f32)

    return _hbm_call(
        body, name="adaln_shard", out_shape=(SDS((16, w_ada_shard.shape[1]), f32), SDS((16, D), f32)),
        in_specs=[VMEM_SPEC, VMEM_SPEC], out_specs=(VMEM_SPEC, VMEM_SPEC), compiler_params=_cp(32),
    )(cc, w_ada_shard)


def _prenorm(xx, norm_g, mrow, b_ada, tm, name, after=None):
    n = xx.shape[0]

    def body(x_ref, g_ref, m_ref, b_ref, after_ref, h_ref):
        x = x_ref[...]
        shift = m_ref[:, 0:D] + b_ref[:, 0:D]
        scale = m_ref[:, D:2 * D] + b_ref[:, D:2 * D]
        r = lax.rsqrt(jnp.mean(x * x, axis=-1, keepdims=True) + RMS_EPS)
        y = (x * r) * g_ref[...]
        h_ref[...] = (y * (1.0 + scale) + shift).astype(bf16)

    row = lambda i: (i, 0)
    fixed = lambda i: (0, 0)
    return _hbm_call(
        body, name=name, out_shape=SDS((n, D), bf16), grid=(n // tm,),
        in_specs=[pl.BlockSpec((tm, D), row), pl.BlockSpec((1, D), fixed), pl.BlockSpec((1, 3 * D), fixed),
                  pl.BlockSpec((1, 3 * D), fixed), ANY_SPEC],
        out_specs=pl.BlockSpec((tm, D), row),
    )(xx, norm_g, mrow, b_ada, b_ada if after is None else after)


def _in_proj_own(h, w_own, jvec):
    tm = 512

    def body(j_ref, h_ref, w_ref, p_ref):
        p_ref[...] = jnp.dot(h_ref[...], w_ref[...].astype(bf16), preferred_element_type=f32)

    grid_spec = pltpu.PrefetchScalarGridSpec(
        num_scalar_prefetch=1, grid=(S // tm,),
        in_specs=[pl.BlockSpec((tm, D), lambda i, j: (i, 0)), pl.BlockSpec((D, D), lambda i, j: (0, 0))],
        out_specs=pl.BlockSpec((tm, D), lambda i, j: (i, j[0])))
    return _hbm_call(body, name="in_proj_own", out_shape=SDS((S, 4 * D), f32), grid_spec=grid_spec,
                     compiler_params=_cp(40))(jvec, h, w_own)


def _in_proj_block(h, w4, p, bvec, name, after=None):
    tm = 512

    def body(b_ref, h_ref, w_ref, p_in_ref, after_ref, p_ref):
        p_ref[...] = jnp.dot(h_ref[...], w_ref[...], preferred_element_type=f32)

    grid_spec = pltpu.PrefetchScalarGridSpec(
        num_scalar_prefetch=1, grid=(S // tm,),
        in_specs=[pl.BlockSpec((tm, D), lambda i, b: (i, 0)), pl.BlockSpec((None, D, D), lambda i, b: (b[0], 0, 0)),
                  ANY_SPEC, ANY_SPEC],
        out_specs=pl.BlockSpec((tm, D), lambda i, b: (i, b[0])))
    return _hbm_call(body, name=name, out_shape=SDS((S, 4 * D), f32), grid_spec=grid_spec,
                     input_output_aliases={3: 0})(bvec, h, w4, p, bvec if after is None else after)


def _ctx_proj(hc, w4):
    def body(h_ref, w0_ref, w1_ref, p_ref):
        hv = h_ref[...]
        p_ref[:, 0:DA] = jnp.dot(hv, w0_ref[:, DA:2 * DA], preferred_element_type=f32)
        p_ref[:, DA:2 * DA] = jnp.dot(hv, w1_ref[:, 0:DA], preferred_element_type=f32)

    return _hbm_call(
        body, name="ctx_proj", out_shape=SDS((L, 2 * DA), f32), grid=(1,),
        in_specs=[pl.BlockSpec((L, D), lambda i: (0, 0)), pl.BlockSpec((None, D, D), lambda i: (0, 0, 0)),
                  pl.BlockSpec((None, D, D), lambda i: (1, 0, 0))],
        out_specs=pl.BlockSpec((L, 2 * DA), lambda i: (0, 0)),
    )(hc, w4, w4)


def _head_ones():
    r = lax.broadcasted_iota(i32, (DA, DA), 0) // DH
    c = lax.broadcasted_iota(i32, (DA, DA), 1) // DH
    return (r == c).astype(bf16)


def _head_sum(v, ones_bd):
    hi = v.astype(bf16)
    lo = (v - hi.astype(f32)).astype(bf16)
    return jnp.dot(hi, ones_bd, preferred_element_type=f32) + jnp.dot(lo, ones_bd, preferred_element_type=f32)


def _swap16(v):
    lane = lax.broadcasted_iota(i32, v.shape, 1)
    return jnp.where((lane & 31) < 16, pltpu.roll(v, DA - 16, 1), pltpu.roll(v, 16, 1))


def _rope_block(ct_ref, rt_ref, tm):
    rows = [jnp.tile(rt_ref[8 * j:8 * j + 8, :], (GW // 8, 1)) for j in range(tm // GW)]
    return jnp.tile(ct_ref[...], (tm // GW, 1)) + jnp.concatenate(rows, axis=0)


def _rope_specs(tm):
    col = pl.BlockSpec((GW, DA), lambda i: (0, 0))
    row = pl.BlockSpec((8 * tm // GW, DA), lambda i: (i, 0))
    return [col, row, col, row]


def _qk_prep(p, gq, gk, rope):
    tm = 512

    def body(qk_ref, v_ref, gq_ref, gk_ref, cc_ref, cr_ref, sc_ref, sr_ref, qr_ref, qp_ref, kr_ref, vh_ref):
        ones_bd = _head_ones()
        cs, sn = _rope_block(cc_ref, cr_ref, tm), _rope_block(sc_ref, sr_ref, tm)
        q = qk_ref[:, 0:DA]
        k = qk_ref[:, DA:2 * DA]
        yq = (q * lax.rsqrt(_head_sum(q * q, ones_bd) * (1.0 / DH) + RMS_EPS)) * gq_ref[...]
        yk = (k * lax.rsqrt(_head_sum(k * k, ones_bd) * (1.0 / DH) + RMS_EPS)) * gk_ref[...]
        qr = (yq * cs + _swap16(yq) * sn) * QK_SCALE
        qp = yq * QK_SCALE
        kr = yk * cs + _swap16(yk) * sn
        vv = v_ref[...]
        for hh in range(H):
            sl = slice(hh * DH, (hh + 1) * DH)
            qr_ref[hh] = qr[:, sl].astype(bf16)
            qp_ref[hh] = qp[:, sl].astype(bf16)
            kr_ref[hh] = kr[:, sl].astype(bf16)
            vh_ref[hh] = vv[:, sl].astype(bf16)

    hm = SDS((H, S, DH), bf16)
    hspec = pl.BlockSpec((H, tm, DH), lambda i: (0, i, 0))
    fixed = lambda i: (0, 0)
    return _hbm_call(
        body, name="qk_prep", out_shape=(hm, hm, hm, hm), grid=(S // tm,),
        in_specs=[pl.BlockSpec((tm, 2 * DA), lambda i: (i, 0)), pl.BlockSpec((tm, DA), lambda i: (i, 2)),
                  pl.BlockSpec((1, DA), fixed), pl.BlockSpec((1, DA), fixed)] + _rope_specs(tm),
        out_specs=(hspec, hspec, hspec, hspec),
    )(p, p, gq, gk, *rope)


def _ctx_prep(pc, gk):
    def body(p_ref, gk_ref, kc_ref, vc_ref):
        ones_bd = _head_ones()
        k = p_ref[:, 0:DA]
        yk = (k * lax.rsqrt(_head_sum(k * k, ones_bd) * (1.0 / DH) + RMS_EPS)) * gk_ref[...]
        vv = p_ref[:, DA:2 * DA]
        for hh in range(H):
            sl = slice(hh * DH, (hh + 1) * DH)
            kc_ref[hh] = yk[:, sl].astype(bf16)
            vc_ref[hh] = vv[:, sl].astype(bf16)

    hm = SDS((H, L, DH), bf16)
    return _hbm_call(
        body, name="ctx_prep", out_shape=(hm, hm), in_specs=[VMEM_SPEC, VMEM_SPEC], out_specs=(VMEM_SPEC, VMEM_SPEC),
    )(pc, gk)


def _tile_pieces():
    out = []
    for (i0, u0) in TILE_GEOM:
        rows = []
        for j in range(2):
            i = i0 + j
            rs = _row_start(i)
            rows.append([(u0 + u - i + WIN_H - 1) if rs <= u0 + u < rs + WIN_H else None for u in range(KR)])
        out.append(rows)
    return out


def _bias_prep(rpb_rev_pad, after=None):
    pieces = _tile_pieces()

    def body(r_ref, after_ref, o_ref):
        rp = r_ref[...]
        xs = jnp.concatenate([pltpu.roll(jnp.broadcast_to(rp[dr:dr + 1, :], (GW, 128)), 128 - (WIN_W - 1), 1,
                                         stride=1, stride_axis=0) for dr in range(N_DR)], axis=0)
        row = lax.broadcasted_iota(i32, xs.shape, 0)
        lane = lax.broadcasted_iota(i32, xs.shape, 1)
        k = row & (GW - 1)
        c0 = jnp.clip(lane - WIN_W // 2, 0, GW - WIN_W)
        xs = jnp.where((k >= c0) & (k < c0 + WIN_W), xs, NEG)
        neg = jnp.full((GW, GW), NEG, f32)
        for t in range(NT):
            for j in range(2):
                for u in range(KR):
                    dr = pieces[t][j][u]
                    piece = neg if dr is None else xs[dr * GW:(dr + 1) * GW, 0:GW]
                    o_ref[t, u * GW:(u + 1) * GW, j * GW:(j + 1) * GW] = piece

    return _hbm_call(
        body, name="bias_prep", out_shape=SDS((H, NT, KB, QB), f32), grid=(H,),
        in_specs=[pl.BlockSpec((None, N_DR, 128), lambda h: (h, 0, 0)), ANY_SPEC],
        out_specs=pl.BlockSpec((None, NT, KB, QB), lambda h: (h, 0, 0, 0)),
    )(rpb_rev_pad, rpb_rev_pad if after is None else after)


def _bias_tiles(rpb2, after=None):
    return _bias_prep(jnp.pad(rpb2[:, :, ::-1], ((0, 0), (0, 0), (0, 128 - N_DC))), after)


def _block_geom(b):
    qs = b * QB
    ks = min(max(2 * b - 4, 0), ROWS - KR) * GW
    t = b if b < 2 else (b - (NQB - NT) if b > NQB - 3 else 2)
    return qs, ks, t


def _tt(a, b):
    return lax.dot_general(a, b, (((1,), (1,)), ((), ())), preferred_element_type=f32)


def _tn(a, b):
    return lax.dot_general(a, b, (((0,), (0,)), ((), ())), preferred_element_type=f32)


def _softmax_t(s_lat, s_ctx):
    m = jnp.maximum(jnp.max(s_lat, axis=0, keepdims=True), jnp.max(s_ctx, axis=0, keepdims=True))
    e_lat = jnp.exp(s_lat - m)
    e_ctx = jnp.exp(s_ctx - m)
    inv = 1.0 / (jnp.sum(e_lat, axis=0, keepdims=True) + jnp.sum(e_ctx, axis=0, keepdims=True))
    return e_lat * inv, e_ctx * inv


def _staged(n_blocks, stages):
    held = [dict() for _ in stages]
    for step in range(n_blocks + len(stages) - 1):
        for s, fn in enumerate(stages):
            b = step - s
            if 0 <= b < n_blocks:
                held[s][b] = fn(b) if s == 0 else fn(b, held[s - 1].pop(b))


def _attn_fwd(qr, qp, kr, vh, kc, vc, btt):
    def body(qr_ref, qp_ref, kr_ref, v_ref, kc_ref, vc_ref, bt_ref, o_ref):
        kcv, vcv = kc_ref[...], vc_ref[...]

        def scores(b):
            qs, ks, t = _block_geom(b)
            return (_tt(kr_ref[ks:ks + KB, :], qr_ref[qs:qs + QB, :]) + bt_ref[t], _tt(kcv, qp_ref[qs:qs + QB, :]))

        def probs(b, sc):
            p_lat, p_ctx = _softmax_t(*sc)
            return p_lat.astype(bf16), p_ctx.astype(bf16)

        def values(b, p):
            qs, ks, _ = _block_geom(b)
            o_ref[qs:qs + QB, :] = _tn(p[0], v_ref[ks:ks + KB, :]) + _tn(p[1], vcv)

        _staged(NQB, (scores, probs, values))

    sq = pl.BlockSpec((None, S, DH), lambda h: (h, 0, 0))
    sc = pl.BlockSpec((None, L, DH), lambda h: (h, 0, 0))
    return _hbm_call(
        body, name="attn_fwd", out_shape=SDS((H, S, DH), f32), grid=(H,),
        in_specs=[sq, sq, sq, sq, sc, sc, pl.BlockSpec((None, NT, KB, QB), lambda h: (h, 0, 0, 0))],
        out_specs=sq, compiler_params=_cp(48),
    )(qr, qp, kr, vh, kc, vc, btt)


def _shift_rows(v, down):
    n = v.shape[0]
    row = lax.broadcasted_iota(i32, v.shape, 0)
    if down:
        return jnp.where(row == 0, 0.0, pltpu.roll(v, 1, 0))
    return jnp.where(row == n - 1, 0.0, pltpu.roll(v, n - 1, 0))


def _conv_specs():
    col = lambda off: pl.BlockSpec((S, 128), lambda i, off=off: (0, off + i))
    return [col(16), col(20), col(24), col(28), pl.BlockSpec((3, 128), lambda i: (0, i)),
            pl.BlockSpec((1, 128), lambda i: (0, i))]


def _conv_fwd(p, conv_w, conv_b, after=None):
    def body(u_ref, bg_ref, cg_ref, zc_ref, w_ref, b_ref, after_ref, o_ref):
        cu = cg_ref[...] * u_ref[...]
        cv = b_ref[...] + _shift_rows(cu, True) * w_ref[0:1, :]
        cv = cv + cu * w_ref[1:2, :]
        cv = cv + _shift_rows(cu, False) * w_ref[2:3, :]
        o_ref[...] = ((bg_ref[...] * cv) * _silu(zc_ref[...])).astype(bf16)

    return _hbm_call(
        body, name="conv_fwd", out_shape=SDS((S, DC), bf16), grid=(DC // 128,),
        in_specs=_conv_specs() + [ANY_SPEC], out_specs=pl.BlockSpec((S, 128), lambda i: (0, i)),
        compiler_params=_cp(40),
    )(p, p, p, p, conv_w, conv_b, conv_b if after is None else after)


DP_Q, DP_K, DP_V, DP_ZA, DP_U, DP_BG, DP_CG, DP_ZC = range(8)


def _out_proj_loss(o, p, conv_g, w_out, xx, tgt, mrow, b_ada):
    tm = 512

    def body(o_ref, za_ref, c_ref, w_ref, x_ref, t_ref, m_ref, b_ref,
             dy_ref, dconv_ref, dp_ref, do_ref, gwo_ref, dgate_ref, loss_ref):
        k = pl.program_id(0)

        @pl.when(k == 0)
        def _():
            gwo_ref[...] = jnp.zeros_like(gwo_ref)
            dgate_ref[...] = jnp.zeros_like(dgate_ref)
            loss_ref[0, 0] = 0.0

        gate = m_ref[:, 2 * D:3 * D] + b_ref[:, 2 * D:3 * D]
        za = za_ref[...]
        sz = _silu(za)
        om = _merge_heads(o_ref)
        av, cv = (om * sz).astype(bf16), c_ref[...]
        mo = jnp.dot(av, w_ref[0:DA, :], preferred_element_type=f32)
        mo = mo + jnp.dot(cv, w_ref[DA:DA + DC, :], preferred_element_type=f32)
        y = x_ref[...] + gate * mo
        diff = y - t_ref[...]
        loss_ref[0, 0] += jnp.sum(diff * diff)
        dy = diff * (1.0 / D)
        dy_ref[...] = dy
        dgate_ref[...] += jnp.sum(dy * mo, axis=0, keepdims=True)
        dmo = (dy * gate).astype(bf16)
        dmix = _tt(dmo, w_ref[...])
        dattn = dmix[:, 0:DA]
        dconv_ref[...] = dmix[:, DA:DA + DC]
        a = dattn * sz
        for hh in range(H):
            do_ref[hh] = a[:, hh * DH:(hh + 1) * DH].astype(bf16)
        dp_ref[...] = ((dattn * _dsilu(za)) * om).astype(bf16)
        gwo_ref[0:DA, :] += _tn(av, dmo)
        gwo_ref[DA:DA + DC, :] += _tn(cv, dmo)

    row = lambda i: (i, 0)
    fixed = lambda i: (0, 0)
    hspec = pl.BlockSpec((H, tm, DH), lambda i: (0, i, 0))
    return _hbm_call(
        body, name="out_proj_loss",
        out_shape=(SDS((S, D), f32), SDS((S, DC), f32), SDS((8, S, DA), bf16), SDS((H, S, DH), bf16),
                   SDS((D, D), f32), SDS((1, D), f32), SDS((1, 1), f32)),
        grid=(S // tm,),
        in_specs=[hspec, pl.BlockSpec((tm, DA), lambda i: (i, 3)), pl.BlockSpec((tm, DC), row),
                  pl.BlockSpec((D, D), fixed), pl.BlockSpec((tm, D), row), pl.BlockSpec((tm, D), row),
                  pl.BlockSpec((1, 3 * D), fixed), pl.BlockSpec((1, 3 * D), fixed)],
        out_specs=(pl.BlockSpec((tm, D), row), pl.BlockSpec((tm, DC), row),
                   pl.BlockSpec((None, tm, DA), lambda i: (DP_ZA, i, 0)), hspec, pl.BlockSpec((D, D), fixed),
                   pl.BlockSpec((1, D), fixed), SMEM_SPEC),
        compiler_params=_cp(56, dimension_semantics=("arbitrary",)),
    )(o, p, conv_g, w_out, xx, tgt, mrow, b_ada)


def _conv_bwd(dconv, p, conv_w, conv_b, dp8, after=None):
    def body(d_ref, u_ref, bg_ref, cg_ref, zc_ref, w_ref, b_ref, dp_in_ref, after_ref, dp_ref, gw_ref, gb_ref):
        du_ref, dbg_ref, dcg_ref, dzc_ref = dp_ref.at[0], dp_ref.at[1], dp_ref.at[2], dp_ref.at[3]
        dconv = d_ref[...]
        u, bg, cg, zc = u_ref[...], bg_ref[...], cg_ref[...], zc_ref[...]
        w0, w1, w2 = w_ref[0:1, :], w_ref[1:2, :], w_ref[2:3, :]
        cu = cg * u
        cu_m, cu_p = _shift_rows(cu, True), _shift_rows(cu, False)
        cv = b_ref[...] + cu_m * w0
        cv = cv + cu * w1
        cv = cv + cu_p * w2
        sz = _silu(zc)
        dbg_ref[...] = ((dconv * sz) * cv).astype(bf16)
        dzc_ref[...] = ((dconv * (bg * cv)) * _dsilu(zc)).astype(bf16)
        dcv = (dconv * sz) * bg
        gb_ref[...] = jnp.sum(dcv, axis=0, keepdims=True)
        gw_ref[0:1, :] = jnp.sum(dcv * cu_m, axis=0, keepdims=True)
        gw_ref[1:2, :] = jnp.sum(dcv * cu, axis=0, keepdims=True)
        gw_ref[2:3, :] = jnp.sum(dcv * cu_p, axis=0, keepdims=True)
        gw_ref[3:8, :] = jnp.zeros((5, 128), f32)
        dcu = _shift_rows(dcv, False) * w0 + dcv * w1 + _shift_rows(dcv, True) * w2
        dcg_ref[...] = (dcu * u).astype(bf16)
        du_ref[...] = (dcu * cg).astype(bf16)

    return _hbm_call(
        body, name="conv_bwd", out_shape=(SDS((8, S, DC), bf16), SDS((8, DC), f32), SDS((1, DC), f32)),
        grid=(DC // 128,),
        in_specs=[pl.BlockSpec((S, 128), lambda i: (0, i))] + _conv_specs() + [ANY_SPEC, ANY_SPEC],
        out_specs=(pl.BlockSpec((4, S, 128), lambda i: (DP_U // 4, 0, i)), pl.BlockSpec((8, 128), lambda i: (0, i)),
                   pl.BlockSpec((1, 128), lambda i: (0, i))),
        input_output_aliases={7: 0}, compiler_params=_cp(48),
    )(dconv, p, p, p, p, conv_w, conv_b, dp8, conv_b if after is None else after)


def _attn_bwd(qr, qp, kr, vh, kc, vc, btt, do, after=None):
    def body(qr_ref, qp_ref, kr_ref, v_ref, kc_ref, vc_ref, bt_ref, do_ref, after_ref,
             dqr_ref, dqp_ref, dkr_ref, dv_ref, dkc_ref, dvc_ref, dbt_ref):
        kcv, vcv = kc_ref[...], vc_ref[...]
        dkr_ref[...] = jnp.zeros_like(dkr_ref)
        dv_ref[...] = jnp.zeros_like(dv_ref)
        dbt_ref[...] = jnp.zeros_like(dbt_ref)
        ctx_acc = {}

        def products(b):
            qs, ks, t = _block_geom(b)
            dob = do_ref[qs:qs + QB, :]
            s_lat = _tt(kr_ref[ks:ks + KB, :], qr_ref[qs:qs + QB, :]) + bt_ref[t]
            s_ctx = _tt(kcv, qp_ref[qs:qs + QB, :])
            return s_lat, s_ctx, _tt(v_ref[ks:ks + KB, :], dob), _tt(vcv, dob)

        def score_grads(b, x):
            s_lat, s_ctx, dp_lat, dp_ctx = x
            p_lat, p_ctx = _softmax_t(s_lat, s_ctx)
            delta = jnp.sum(p_lat * dp_lat, axis=0, keepdims=True) + jnp.sum(p_ctx * dp_ctx, axis=0, keepdims=True)
            ds_lat = p_lat * (dp_lat - delta)
            ds_ctx = p_ctx * (dp_ctx - delta)
            return ds_lat, ds_lat.astype(bf16), ds_ctx.astype(bf16), p_lat.astype(bf16), p_ctx.astype(bf16)

        def operand_grads(b, y):
            qs, ks, t = _block_geom(b)
            ds_lat, dsb_lat, dsb_ctx, pb_lat, pb_ctx = y
            qrb, qpb, dob = qr_ref[qs:qs + QB, :], qp_ref[qs:qs + QB, :], do_ref[qs:qs + QB, :]
            dbt_ref[t] += ds_lat
            dqr_ref[qs:qs + QB, :] = _tn(dsb_lat, kr_ref[ks:ks + KB, :])
            dqp_ref[qs:qs + QB, :] = _tn(dsb_ctx, kcv)
            dkr_ref[ks:ks + KB, :] += jnp.dot(dsb_lat, qrb, preferred_element_type=f32)
            dv_ref[ks:ks + KB, :] += jnp.dot(pb_lat, dob, preferred_element_type=f32)
            dkc = jnp.dot(dsb_ctx, qpb, preferred_element_type=f32)
            dvc = jnp.dot(pb_ctx, dob, preferred_element_type=f32)
            ctx_acc["k"] = dkc if b == 0 else ctx_acc["k"] + dkc
            ctx_acc["v"] = dvc if b == 0 else ctx_acc["v"] + dvc

        _staged(NQB, (products, score_grads, operand_grads))
        dkc_ref[...] = ctx_acc["k"]
        dvc_ref[...] = ctx_acc["v"]

    sq = pl.BlockSpec((None, S, DH), lambda h: (h, 0, 0))
    sc = pl.BlockSpec((None, L, DH), lambda h: (h, 0, 0))
    sb = pl.BlockSpec((None, NT, KB, QB), lambda h: (h, 0, 0, 0))
    big, ctxs = SDS((H, S, DH), f32), SDS((H, L, DH), f32)
    return _hbm_call(
        body, name="attn_bwd", out_shape=(big, big, big, big, ctxs, ctxs, SDS((H, NT, KB, QB), f32)), grid=(H,),
        in_specs=[sq, sq, sq, sq, sc, sc, sb, sq, ANY_SPEC], out_specs=(sq, sq, sq, sq, sc, sc, sb),
        compiler_params=_cp(56),
    )(qr, qp, kr, vh, kc, vc, btt, do, do if after is None else after)


def _bias_bwd(dbtt, after=None):
    pieces = _tile_pieces()

    def body(d_ref, after_ref, o_ref, scr):
        scr[...] = jnp.zeros_like(scr)
        acc = [None] * N_DR
        for t in range(NT):
            for j in range(2):
                for u in range(KR):
                    dr = pieces[t][j][u]
                    if dr is None:
                        continue
                    piece = d_ref[t, u * GW:(u + 1) * GW, j * GW:(j + 1) * GW]
                    acc[dr] = piece if acc[dr] is None else acc[dr] + piece
        a = lax.broadcasted_iota(i32, (GW, GW), 0)
        b = lax.broadcasted_iota(i32, (GW, GW), 1)
        flip = (a + b == GW - 1).astype(f32)
        for dr in range(N_DR):
            scr[dr * GW:(dr + 1) * GW, 0:GW] = jnp.dot(acc[dr], flip, precision=HIGHEST, preferred_element_type=f32)
        xs = jnp.concatenate([pltpu.roll(scr[dr * GW:(dr + 1) * GW, :], 128 + (WIN_W - 1) - (GW - 1), 1,
                                         stride=1, stride_axis=0) for dr in range(N_DR)], axis=0)
        tot = jnp.sum(xs.reshape(N_DR, GW, 128), axis=1)
        lane = lax.broadcasted_iota(i32, tot.shape, 1)
        o_ref[...] = jnp.where(lane < N_DC, tot, 0.0)

    return _hbm_call(
        body, name="bias_bwd", out_shape=SDS((H, N_DR, 128), f32), grid=(H,),
        in_specs=[pl.BlockSpec((None, NT, KB, QB), lambda h: (h, 0, 0, 0)), ANY_SPEC],
        out_specs=pl.BlockSpec((None, N_DR, 128), lambda h: (h, 0, 0)),
        scratch_shapes=[pltpu.VMEM((N_DR * GW, 128), f32)],
    )(dbtt, dbtt if after is None else after)


def _merge_heads(ref):
    return jnp.concatenate([ref[hh] for hh in range(H)], axis=1)


def _head_norm_bwd(xraw, gain, dy, ones_bd):
    r = lax.rsqrt(_head_sum(xraw * xraw, ones_bd) * (1.0 / DH) + RMS_EPS)
    xh = xraw * r
    gdy = dy * gain
    dx = r * (gdy - xh * (_head_sum(xh * gdy, ones_bd) * (1.0 / DH)))
    return dx, jnp.sum(dy * xh, axis=0, keepdims=True)


def _qk_bwd(dqr, dqp, dkr, dvh, p, gq, gk, rope, dp8):
    tm = 512

    def body(dqr_ref, dqp_ref, dkr_ref, dv_ref, qk_ref, gq_ref, gk_ref, cc_ref, cr_ref, sc_ref, sr_ref, dp_in_ref,
             dp_ref, ggq_ref, ggk_ref):
        dq_ref, dk_ref, dvo_ref = dp_ref.at[DP_Q], dp_ref.at[DP_K], dp_ref.at[DP_V]

        @pl.when(pl.program_id(0) == 0)
        def _():
            ggq_ref[...] = jnp.zeros_like(ggq_ref)
            ggk_ref[...] = jnp.zeros_like(ggk_ref)

        ones_bd = _head_ones()
        cs, sn = _rope_block(cc_ref, cr_ref, tm), _rope_block(sc_ref, sr_ref, tm)
        a = _merge_heads(dqr_ref)
        dyq = ((a * cs - _swap16(a) * sn) + _merge_heads(dqp_ref)) * QK_SCALE
        bk = _merge_heads(dkr_ref)
        dyk = bk * cs - _swap16(bk) * sn
        dq, gq_part = _head_norm_bwd(qk_ref[:, 0:DA], gq_ref[...], dyq, ones_bd)
        dk, gk_part = _head_norm_bwd(qk_ref[:, DA:2 * DA], gk_ref[...], dyk, ones_bd)
        dq_ref[...] = dq.astype(bf16)
        dk_ref[...] = dk.astype(bf16)
        dvo_ref[...] = _merge_heads(dv_ref).astype(bf16)
        ggq_ref[...] += gq_part
        ggk_ref[...] += gk_part

    hspec = pl.BlockSpec((H, tm, DH), lambda i: (0, i, 0))
    fixed = pl.BlockSpec((1, DA), lambda i: (0, 0))
    return _hbm_call(
        body, name="qk_bwd", out_shape=(SDS((8, S, DA), bf16), SDS((1, DA), f32), SDS((1, DA), f32)), grid=(S // tm,),
        in_specs=[hspec, hspec, hspec, hspec, pl.BlockSpec((tm, 2 * DA), lambda i: (i, 0)), fixed, fixed]
        + _rope_specs(tm) + [ANY_SPEC],
        out_specs=(pl.BlockSpec((3, tm, DA), lambda i: (0, i, 0)), fixed, fixed), input_output_aliases={11: 0},
        compiler_params=_cp(40, dimension_semantics=("arbitrary",)),
    )(dqr, dqp, dkr, dvh, p, gq, gk, *rope, dp8)


def _ctx_bwd(dkc, dvc, pc, gk):
    def body(dkc_ref, dvc_ref, p_ref, gk_ref, dk_ref, dv_ref, ggk_ref):
        ones_bd = _head_ones()
        dk, gk_part = _head_norm_bwd(p_ref[:, 0:DA], gk_ref[...], _merge_heads(dkc_ref), ones_bd)
        dk_ref[...] = dk.astype(bf16)
        dv_ref[...] = _merge_heads(dvc_ref).astype(bf16)
        ggk_ref[...] = gk_part

    piece = SDS((L, DA), bf16)
    return _hbm_call(
        body, name="ctx_bwd", out_shape=(piece, piece, SDS((1, DA), f32)), in_specs=[VMEM_SPEC] * 4,
        out_specs=(VMEM_SPEC,) * 3,
    )(dkc, dvc, pc, gk)


def _grad_w_in(h, dp8, hc, dkc_raw, dvc_m, half, name, after=None):
    def body(half_ref, h_ref, p_ref, hc_ref, dk_ref, dv_ref, after_ref, g_ref):
        j = pl.program_id(0)
        hv = h_ref[...]
        g_ref[:, 0:DA] = _tn(hv, p_ref[0])
        g_ref[:, DA:2 * DA] = _tn(hv, p_ref[1])

        @pl.when(j == 0)
        def _():
            g_ref[:, DA:2 * DA] += _tn(hc_ref[...], dk_ref[...])

        @pl.when(j == 1)
        def _():
            g_ref[:, 0:DA] += _tn(hc_ref[...], dv_ref[...])

    fixed = lambda j, s: (0, 0)
    hd = D // 2
    grid_spec = pltpu.PrefetchScalarGridSpec(
        num_scalar_prefetch=1, grid=(4,),
        in_specs=[pl.BlockSpec((S, hd), lambda j, s: (0, s[0])), pl.BlockSpec((2, S, DA), lambda j, s: (j, 0, 0)),
                  pl.BlockSpec((L, hd), lambda j, s: (0, s[0])), pl.BlockSpec((L, DA), fixed),
                  pl.BlockSpec((L, DA), fixed), ANY_SPEC],
        out_specs=pl.BlockSpec((None, hd, D), lambda j, s: (j, 0, 0)))
    return _hbm_call(
        body, name=name, out_shape=SDS((4, hd, D), f32), grid_spec=grid_spec, compiler_params=_cp(40),
    )(half, h, dp8, hc, dkc_raw, dvc_m, half if after is None else after)


def _grad_w_in_pair_sum(h, dp8, hc, dkc_raw, dvc_m, half, sent, landed, ssem, rsem):
    def body(half_ref, h_ref, p_ref, hc_ref, dk_ref, dv_ref, sent_ref, land_ref, ssem_ref, rsem_ref,
             t32_ref, tb_ref, buf, lsem):
        j = pl.program_id(0)
        hv = h_ref[...]
        t32_ref[:, 0:DA] = _tn(hv, p_ref[0])
        x, y, c = _my_pos()
        arrival = pltpu.make_async_remote_copy(src_ref=sent_ref.at[j], dst_ref=land_ref.at[j], send_sem=ssem_ref.at[j],
                                               recv_sem=rsem_ref.at[j], device_id=(x, y, 1 - c), device_id_type=MESH)
        arrival.wait_recv()
        arrival.wait_send()
        load = pltpu.make_async_copy(land_ref.at[j], buf, lsem)
        load.start()
        t32_ref[:, DA:2 * DA] = _tn(hv, p_ref[1])

        @pl.when(j == 0)
        def _():
            t32_ref[:, DA:2 * DA] += _tn(hc_ref[...], dk_ref[...])

        @pl.when(j == 1)
        def _():
            t32_ref[:, 0:DA] += _tn(hc_ref[...], dv_ref[...])

        load.wait()
        t = t32_ref[...] + buf[...]
        t32_ref[...] = t
        tb_ref[...] = t.astype(bf16)

    fixed = lambda j, s: (0, 0)
    hd = D // 2
    out_spec = pl.BlockSpec((None, hd, D), lambda j, s: (j, 0, 0))
    grid_spec = pltpu.PrefetchScalarGridSpec(
        num_scalar_prefetch=1, grid=(4,),
        in_specs=[pl.BlockSpec((S, hd), lambda j, s: (0, s[0])), pl.BlockSpec((2, S, DA), lambda j, s: (j, 0, 0)),
                  pl.BlockSpec((L, hd), lambda j, s: (0, s[0])), pl.BlockSpec((L, DA), fixed),
                  pl.BlockSpec((L, DA), fixed), HBM_SPEC, HBM_SPEC, SEM_SPEC, SEM_SPEC],
        out_specs=(out_spec, out_spec), scratch_shapes=[pltpu.VMEM((hd, D), f32), pltpu.SemaphoreType.DMA])
    return _hbm_call(
        body, name="grad_w_in_pair_sum", out_shape=(SDS((4, hd, D), f32), SDS((4, hd, D), bf16)), grid_spec=grid_spec,
        compiler_params=_cp(40, has_side_effects=DATAFLOW),
    )(half, h, dp8, hc, dkc_raw, dvc_m, sent, landed, ssem, rsem)


def _norm_mod_bwd(x, dh, g, scale):
    r = lax.rsqrt(jnp.mean(x * x, axis=-1, keepdims=True) + RMS_EPS)
    xh = x * r
    y = xh * g
    dshift = jnp.sum(dh, axis=0, keepdims=True)
    dscale = jnp.sum(dh * y, axis=0, keepdims=True)
    dyn = dh * (1.0 + scale)
    dg = jnp.sum(dyn * xh, axis=0, keepdims=True)
    gdy = dyn * g
    dx = r * (gdy - xh * jnp.mean(xh * gdy, axis=-1, keepdims=True))
    return dx, dshift, dscale, dg


def _dh_grad_x(dp8, w4, xx, dy, norm_g, mrow, b_ada, after=None):
    tm = 512

    def body(p_ref, w_ref, x_ref, dy_ref, g_ref, m_ref, b_ref, after_ref, gx_ref, dsh_ref, dsc_ref, dg_ref):
        @pl.when(pl.program_id(0) == 0)
        def _():
            dsh_ref[...] = jnp.zeros_like(dsh_ref)
            dsc_ref[...] = jnp.zeros_like(dsc_ref)
            dg_ref[...] = jnp.zeros_like(dg_ref)

        dh = None
        for j in range(4):
            for half in range(2):
                term = _tt(p_ref[2 * j + half], w_ref[j, :, half * DA:(half + 1) * DA])
                dh = term if dh is None else dh + term
        scale = m_ref[:, D:2 * D] + b_ref[:, D:2 * D]
        dx, dshift, dscale, dg = _norm_mod_bwd(x_ref[...], dh, g_ref[...], scale)
        gx_ref[...] = dy_ref[...] + dx
        dsh_ref[...] += dshift
        dsc_ref[...] += dscale
        dg_ref[...] += dg

    row = lambda i: (i, 0)
    fixed = lambda i: (0, 0)
    vec = SDS((1, D), f32)
    return _hbm_call(
        body, name="dh_grad_x", out_shape=(SDS((S, D), f32), vec, vec, vec), grid=(S // tm,),
        in_specs=[pl.BlockSpec((8, tm, DA), lambda i: (0, i, 0)), pl.BlockSpec((4, D, D), lambda i: (0, 0, 0)),
                  pl.BlockSpec((tm, D), row), pl.BlockSpec((tm, D), row), pl.BlockSpec((1, D), fixed),
                  pl.BlockSpec((1, 3 * D), fixed), pl.BlockSpec((1, 3 * D), fixed), ANY_SPEC],
        out_specs=(pl.BlockSpec((tm, D), row), pl.BlockSpec((1, D), fixed), pl.BlockSpec((1, D), fixed),
                   pl.BlockSpec((1, D), fixed)),
        compiler_params=_cp(56, dimension_semantics=("arbitrary",)),
    )(dp8, w4, xx, dy, norm_g, mrow, b_ada, b_ada if after is None else after)


def _dhc_sums(dkc_raw, dvc_m, w4, ctx2, norm_g, mrow_c, b_ada, after=None):
    def body(dk_ref, dv_ref, w0_ref, w1_ref, x_ref, g_ref, m_ref, b_ref, after_ref, dsh_ref, dsc_ref, dg_ref):
        dh = _tt(dk_ref[...], w0_ref[:, DA:2 * DA]) + _tt(dv_ref[...], w1_ref[:, 0:DA])
        scale = m_ref[:, D:2 * D] + b_ref[:, D:2 * D]
        _, dshift, dscale, dg = _norm_mod_bwd(x_ref[...], dh, g_ref[...], scale)
        dsh_ref[...] = dshift
        dsc_ref[...] = dscale
        dg_ref[...] = dg

    fixed = lambda i: (0, 0)
    vec = SDS((1, D), f32)
    vspec = pl.BlockSpec((1, D), fixed)
    return _hbm_call(
        body, name="dhc_sums", out_shape=(vec, vec, vec), grid=(1,),
        in_specs=[pl.BlockSpec((L, DA), fixed), pl.BlockSpec((L, DA), fixed),
                  pl.BlockSpec((None, D, D), lambda i: (0, 0, 0)), pl.BlockSpec((None, D, D), lambda i: (1, 0, 0)),
                  pl.BlockSpec((L, D), fixed), vspec, pl.BlockSpec((1, 3 * D), fixed), pl.BlockSpec((1, 3 * D), fixed),
                  ANY_SPEC],
        out_specs=(vspec, vspec, vspec), compiler_params=_cp(32),
    )(dkc_raw, dvc_m, w4, w4, ctx2, norm_g, mrow_c, b_ada, b_ada if after is None else after)


def _rope_tables():
    nf = DH // 4
    inv = np.float32(ROPE_THETA) ** (-np.arange(nf, dtype=np.float32) / np.float32(nf))
    ang_c = np.arange(GW, dtype=np.float32)[:, None] * inv
    ang_r = np.arange(ROWS, dtype=np.float32)[:, None] * inv
    zc, zr = np.zeros((GW, 2 * nf), np.float32), np.zeros((ROWS, 2 * nf), np.float32)
    ct_cos = np.tile(np.concatenate([zc, np.cos(ang_c), np.cos(ang_c)], axis=1), (1, H))
    ct_sin = np.tile(np.concatenate([zc, -np.sin(ang_c), np.sin(ang_c)], axis=1), (1, H))
    rt_cos = np.tile(np.concatenate([np.cos(ang_r), np.cos(ang_r), zr], axis=1), (1, H))
    rt_sin = np.tile(np.concatenate([-np.sin(ang_r), np.sin(ang_r), zr], axis=1), (1, H))
    rep8 = lambda t: np.ascontiguousarray(np.broadcast_to(t[:, None, :], (ROWS, 8, DA))).reshape(ROWS * 8, DA)
    return tuple(jnp.asarray(t, f32) for t in (ct_cos, rep8(rt_cos), ct_sin, rep8(rt_sin)))


def _local_step(xx, ctx2, tgt, mrow, mrow_c, b_ada, norm_g, weights, q_norm_g, k_norm_g, rpb2, conv_w_full, conv_b,
                hooks=None):
    hooks = hooks or {}
    gq = jnp.tile(q_norm_g, (1, H))
    gk = jnp.tile(k_norm_g, (1, H))
    rope = _rope_tables()

    h = _prenorm(xx, norm_g, mrow, b_ada, 256, "prenorm_x", after=weights.get("started"))
    jv = weights["jvec"]
    p = _in_proj_own(h, weights["own"], jv)
    btb = _bias_tiles(rpb2, after=p)
    w4, started = weights["near"](btb)
    p = _in_proj_block(h, w4, p, weights["first"], "in_proj_near", after=started)
    w4 = weights["near2"](w4, p)
    p = _in_proj_block(h, w4, p, weights["second"], "in_proj_near2")
    ctx_norm = {}

    def filler(token):
        ctx_norm["hc"] = _prenorm(ctx2, norm_g, mrow_c, b_ada, L, "prenorm_ctx", after=token)
        return ctx_norm["hc"]

    w4, started = weights["far"](w4, p, filler)
    hc = ctx_norm["hc"]
    p = _in_proj_block(h, w4, p, jv ^ 3, "in_proj_far", after=started)
    pc = _ctx_proj(hc, w4)
    qr, qp, kr, vh = _qk_prep(p, gq, gk, rope)
    kc, vc = _ctx_prep(pc, gk)
    o = _attn_fwd(qr, qp, kr, vh, kc, vc, btb)
    started = weights["out_arrived"](o) if "out_arrived" in weights else None
    conv_g = _conv_fwd(p, conv_w_full, conv_b, after=started)
    w_out_full = weights["out"](conv_g)
    dy, dconv, dp8, do, g_w_out, dgate, loss_sum = _out_proj_loss(o, p, conv_g, w_out_full, xx, tgt, mrow, b_ada)
    started = hooks["g_w_out"](g_w_out) if "g_w_out" in hooks else None
    dp8, g_conv_w, g_conv_b = _conv_bwd(dconv, p, conv_w_full, conv_b, dp8, after=started)
    started = hooks["after_conv"](dp8) if "after_conv" in hooks else None
    dqr, dqp, dkr, dvh, dkc, dvc, dbtb = _attn_bwd(qr, qp, kr, vh, kc, vc, btb, do, after=started)
    dp8, g_gq, g_gk = _qk_bwd(dqr, dqp, dkr, dvh, p, gq, gk, rope, dp8)
    dkc_raw, dvc_m, g_gk_c = _ctx_bwd(dkc, dvc, pc, gk)
    first = hooks.get("first_half", jnp.zeros((1,), i32))
    g_first = _grad_w_in(h, dp8, hc, dkc_raw, dvc_m, first, "grad_w_in_first")
    started = hooks["g_w_in_first"](g_first) if "g_w_in_first" in hooks else None
    if "w_in_pair_sum" in hooks:
        g_second = None
        started = hooks["w_in_pair_sum"](functools.partial(_grad_w_in_pair_sum, h, dp8, hc, dkc_raw, dvc_m, 1 - first))
    else:
        g_second = _grad_w_in(h, dp8, hc, dkc_raw, dvc_m, 1 - first, "grad_w_in_second", after=started)
    dshift_c, dscale_c, dng_c = _dhc_sums(dkc_raw, dvc_m, w4, ctx2, norm_g, mrow_c, b_ada, after=started)
    g_rpb = _bias_bwd(dbtb, after=dshift_c)
    grad_x, dshift, dscale, dng = _dh_grad_x(dp8, w4, xx, dy, norm_g, mrow, b_ada, after=g_rpb)
    return dict(loss_sum=loss_sum, grad_x=grad_x, g_w_in=(g_first, g_second), g_w_out=g_w_out, g_conv_w=g_conv_w,
                g_conv_b=g_conv_b, g_rpb=g_rpb, g_gq=g_gq, g_gk=g_gk, g_gk_c=g_gk_c, dshift=dshift, dscale=dscale,
                dgate=dgate, dng=dng, dshift_c=dshift_c, dscale_c=dscale_c, dng_c=dng_c)


def _pair_sum_w_out(g, r, cvec):
    hr = D // 8

    def body(c_ref, g0, g1, g2, g3, r_ref, t32_ref, tb_ref):
        for q, g_ref in enumerate((g0, g1, g2, g3)):
            t = g_ref[...] + r_ref[q]
            t32_ref[q] = t
            tb_ref[q] = t.astype(bf16)

    gspecs = [pl.BlockSpec((hr, D), lambda i, c, q=q: (2 * q + c[0], 0)) for q in range(4)]
    full = pl.BlockSpec((4, hr, D), lambda i, c: (0, 0, 0))
    grid_spec = pltpu.PrefetchScalarGridSpec(num_scalar_prefetch=1, grid=(1,), in_specs=gspecs + [full],
                                             out_specs=(full, full))
    return _hbm_call(body, name="pair_sum_w_out", out_shape=(SDS((4, hr, D), f32), SDS((4, hr, D), bf16)),
                          grid_spec=grid_spec)(cvec, g, g, g, g, r)


def _chip_sum(t32, r2, jvec, name):
    rows = t32.shape[1]
    tr = min(rows, 128)

    def body(j_ref, t_ref, r_ref, u_ref):
        u_ref[...] = ((t_ref[...] + r_ref[0].astype(f32)) + r_ref[1].astype(f32)) + r_ref[2].astype(f32)

    grid_spec = pltpu.PrefetchScalarGridSpec(
        num_scalar_prefetch=1, grid=(rows // tr,),
        in_specs=[pl.BlockSpec((None, tr, D), lambda i, j: (j[0], i, 0)),
                  pl.BlockSpec((3, tr, D), lambda i, j: (0, i, 0))],
        out_specs=pl.BlockSpec((tr, D), lambda i, j: (i, 0)))
    return _hbm_call(body, name=name, out_shape=SDS((rows, D), f32), grid_spec=grid_spec)(jvec, t32, r2)


_PK = {}
_off = 0
for _name, _rows in (("dm", 24), ("dmc", 24), ("dng", 8), ("dng_c", 8), ("gq", 8), ("gk", 8), ("gk_c", 8),
                     ("rpb", H * N_DR), ("conv_b", 8), ("conv_w", 16), ("loss", 8)):
    _PK[_name] = (_off, _off + _rows)
    _off += _rows
PK_ROWS = _off
RS_B_ADA, RS_NORM_G, RS_GQ, RS_GK, RS_RPB, RS_CONV_B, RS_CONV_W, RS_DMC, RS_LOSS, RS_ROWS = (
    0, 24, 32, 40, 48, 168, 176, 192, 216, 224)


def _small_reduce(gathered):
    def body(g_ref, o_ref, dm_ref):
        a0 = _PK["dm"][0]
        dm_ref[...] = jnp.zeros_like(dm_ref)
        for b in range(8):
            for i in range(24):
                dm_ref[b:b + 1, 128 * i:128 * (i + 1)] = g_ref[b, a0 + i:a0 + i + 1, :]
        tot = g_ref[0]
        for b in range(1, 8):
            tot = tot + g_ref[b]

        def rows(name):
            a, z = _PK[name]
            return tot[a:z]

        o_ref[RS_B_ADA:RS_B_ADA + 24] = rows("dm") + rows("dmc")
        o_ref[RS_NORM_G:RS_NORM_G + 8] = rows("dng") + rows("dng_c")
        gq = jnp.broadcast_to(jnp.sum(rows("gq"), axis=0, keepdims=True), (8, 128))
        gk = jnp.broadcast_to(jnp.sum(rows("gk") + rows("gk_c"), axis=0, keepdims=True), (8, 128))
        o_ref[RS_GQ:RS_GQ + 8] = gq + pltpu.roll(gq, DH, 1)
        o_ref[RS_GK:RS_GK + 8] = gk + pltpu.roll(gk, DH, 1)
        o_ref[RS_RPB:RS_RPB + H * N_DR] = rows("rpb")
        o_ref[RS_CONV_B:RS_CONV_B + 8] = rows("conv_b")
        o_ref[RS_CONV_W:RS_CONV_W + 16] = rows("conv_w")
        dmc = rows("dmc")
        o_ref[RS_DMC:RS_DMC + 24] = dmc
        o_ref[RS_LOSS:RS_LOSS + 8] = rows("loss")
        for i in range(24):
            dm_ref[8:9, 128 * i:128 * (i + 1)] = dmc[i:i + 1]

    return _hbm_call(body, name="small_reduce", out_shape=(SDS((RS_ROWS, 128), f32), SDS((16, 3 * D), f32)),
                     in_specs=[VMEM_SPEC], out_specs=(VMEM_SPEC, VMEM_SPEC))(gathered)


def _w_ada_grad(sc16, dm16, w_ada_shard, jvec):
    ncol = w_ada_shard.shape[1]

    def body(j_ref, sc_ref, dm_ref, w_ref, g_ref, part_ref):
        dm = dm_ref[...]
        g_ref[...] = lax.dot_general(sc_ref[...], dm, (((0,), (0,)), ((), ())), precision=HIGHEST,
                                     preferred_element_type=f32)
        part_ref[...] = lax.dot_general(dm[8:16], w_ref[...], (((1,), (1,)), ((), ())), precision=HIGHEST,
                                        preferred_element_type=f32)

    fixed = lambda i, j: (0, 0)
    grid_spec = pltpu.PrefetchScalarGridSpec(
        num_scalar_prefetch=1, grid=(1,),
        in_specs=[pl.BlockSpec((16, D), fixed), pl.BlockSpec((16, ncol), lambda i, j: (0, j[0])),
                  pl.BlockSpec((D, ncol), fixed)],
        out_specs=(pl.BlockSpec((D, ncol), fixed), pl.BlockSpec((8, D), fixed)))
    return _pallas_call(body, name="w_ada_grad", out_shape=(SDS((D, ncol), f32), SDS((8, D), f32)),
                        grid_spec=grid_spec, compiler_params=_cp(40))(jvec, sc16, dm16, w_ada_shard)


def _c_ctx_grad(parts4, c_ctx):
    def body(p_ref, c_ref, o_ref):
        tot = ((p_ref[0] + p_ref[1]) + p_ref[2]) + p_ref[3]
        o_ref[...] = tot[0:1] * _dsilu(c_ref[...].reshape(1, D))

    return _pallas_call(body, name="c_ctx_grad", out_shape=SDS((1, D), f32), in_specs=[VMEM_SPEC, VMEM_SPEC],
                        out_specs=VMEM_SPEC)(parts4, c_ctx)


def _adamw(w, g, m, v, name, after=None):
    rows, cols = w.shape
    tr = 256 if rows % 256 == 0 else rows

    def body(w_ref, g_ref, m_ref, v_ref, after_ref, d_ref, m2_ref, v2_ref):
        gv = g_ref[...]
        m2 = ADAM_B1 * m_ref[...] + (1.0 - ADAM_B1) * gv
        v2 = ADAM_B2 * v_ref[...] + (1.0 - ADAM_B2) * jnp.square(gv)
        m_hat = m2 / (1.0 - ADAM_B1 ** ADAM_STEP)
        v_hat = v2 / (1.0 - ADAM_B2 ** ADAM_STEP)
        d_ref[...] = -ADAM_LR * (m_hat / (jnp.sqrt(v_hat) + ADAM_EPS) + ADAM_WD * w_ref[...])
        m2_ref[...] = m2
        v2_ref[...] = v2

    spec = pl.BlockSpec((tr, cols), lambda i: (i, 0))
    shp = SDS((rows, cols), f32)
    return _hbm_call(body, name=name, out_shape=(shp, shp, shp), grid=(rows // tr,), in_specs=[spec] * 4 + [ANY_SPEC],
                     out_specs=(spec, spec, spec))(w, g, m, v, g if after is None else after)


def _adamw_halves(w, g_mine, g_other, m, v, cvec, name, after=None):
    rows, cols = w.shape
    half = rows // 2
    tr = min(256, half)
    per_half = half // tr

    def body(c_ref, w_ref, ga_ref, gb_ref, m_ref, v_ref, after_ref, g_ref, d_ref, m2_ref, v2_ref):
        in_my_half = (pl.program_id(0) // per_half) == c_ref[0]
        gv = jnp.where(in_my_half, ga_ref[...], gb_ref[...])
        g_ref[...] = gv
        m2 = ADAM_B1 * m_ref[...] + (1.0 - ADAM_B1) * gv
        v2 = ADAM_B2 * v_ref[...] + (1.0 - ADAM_B2) * jnp.square(gv)
        m_hat = m2 / (1.0 - ADAM_B1 ** ADAM_STEP)
        v_hat = v2 / (1.0 - ADAM_B2 ** ADAM_STEP)
        d_ref[...] = -ADAM_LR * (m_hat / (jnp.sqrt(v_hat) + ADAM_EPS) + ADAM_WD * w_ref[...])
        m2_ref[...] = m2
        v2_ref[...] = v2

    full = pl.BlockSpec((tr, cols), lambda i, c: (i, 0))
    part = pl.BlockSpec((tr, cols), lambda i, c: (i % per_half, 0))
    shp = SDS((rows, cols), f32)
    grid_spec = pltpu.PrefetchScalarGridSpec(num_scalar_prefetch=1, grid=(rows // tr,),
                                             in_specs=[full, part, part, full, full, ANY_SPEC], out_specs=(full,) * 4)
    return _hbm_call(body, name=name, out_shape=(shp,) * 4, grid_spec=grid_spec)(
        cvec, w, g_mine, g_other, m, v, cvec if after is None else after)


def _adam_math(w, g, m, v):
    m2 = ADAM_B1 * m + (1.0 - ADAM_B1) * g
    v2 = ADAM_B2 * v + (1.0 - ADAM_B2) * jnp.square(g)
    m_hat = m2 / (1.0 - ADAM_B1 ** ADAM_STEP)
    v_hat = v2 / (1.0 - ADAM_B2 ** ADAM_STEP)
    return -ADAM_LR * (m_hat / (jnp.sqrt(v_hat) + ADAM_EPS) + ADAM_WD * w), m2, v2


def _adamw_small(red, g_c_ctx, jvec, ws, ms, vs):
    n = len(ws)

    def body(*refs):
        red_ref, gc_ref, j_ref = refs[:3]
        w_refs, m_refs, v_refs = refs[3:3 + n], refs[3 + n:3 + 2 * n], refs[3 + 2 * n:3 + 3 * n]
        outs = refs[3 + 3 * n:]
        g_out, d_out, m_out, v_out = outs[:n], outs[n:2 * n], outs[2 * n:3 * n], outs[3 * n:]
        chip = j_ref[0]
        lanes = lambda i: (slice(None), slice(128 * i, 128 * (i + 1)))
        row = lambda r0, i: (lambda: red_ref[r0 + i:r0 + i + 1, :])
        whole = (slice(None), slice(None))
        chunks = [
            [((slice(None),), lambda: gc_ref[...].reshape(D))],
            [(lanes(i), row(RS_B_ADA, i)) for i in range(3 * D // 128)],
            [(lanes(i), row(RS_NORM_G, i)) for i in range(D // 128)],
            [(whole, lambda: red_ref[RS_GQ:RS_GQ + 1, 0:DH])],
            [(whole, lambda: red_ref[RS_GK:RS_GK + 1, 0:DH])],
            [((dr,), (lambda dr=dr: red_ref[pl.ds(RS_RPB + dr, H, stride=N_DR), 0:N_DC])) for dr in range(N_DR)],
            [((r,), (lambda r=r: red_ref[pl.ds(RS_CONV_W + 4 * r + chip, 1), :])) for r in range(3)],
            [(lanes(i), row(RS_CONV_B, i)) for i in range(DC // 128)],
        ]
        for a in range(n):
            for idx, grad in chunks[a]:
                g = grad()
                d, m2, v2 = _adam_math(w_refs[a][idx], g, m_refs[a][idx], v_refs[a][idx])
                g_out[a][idx] = g
                d_out[a][idx] = d
                m_out[a][idx] = m2
                v_out[a][idx] = v2

    shapes = [SDS(w.shape, f32) for w in ws]
    res = _pallas_call(body, name="adamw_small", out_shape=shapes * 4,
                       in_specs=[VMEM_SPEC, VMEM_SPEC, SMEM_SPEC] + [VMEM_SPEC] * (3 * n),
                       out_specs=[VMEM_SPEC] * (4 * n))(red, g_c_ctx, jvec, *ws, *ms, *vs)
    return [list(res[k * n:(k + 1) * n]) for k in range(4)]


def _rows128(a):
    return a.reshape(-1, 128)


def kernel(x, c, ctx, c_ctx, w_ada, b_ada, norm_g, w_in, q_norm_g, k_norm_g, rpb, conv_w, conv_b, w_out, loss_target, m_c_ctx, m_w_ada, m_b_ada, m_norm_g, m_w_in, m_q_norm_g, m_k_norm_g, m_rpb, m_conv_w, m_conv_b, m_w_out, v_c_ctx, v_w_ada, v_b_ada, v_norm_g, v_w_in, v_q_norm_g, v_k_norm_g, v_rpb, v_conv_w, v_conv_b, v_w_out):
    xi, yi, ci = lax.axis_index("x"), lax.axis_index("y"), lax.axis_index("c")
    dev = 4 * xi + 2 * yi + ci
    chip = 2 * xi + yi
    cvec = jnp.reshape(ci, (1,)).astype(i32)
    jvec = jnp.reshape(chip, (1,)).astype(i32)
    w_ada_s = w_ada[0]
    ncol = w_ada_s.shape[1]

    gc = _split_start([_to_slot(c.reshape(8, 128), 8, dev)], [], 7, _gather8_copies, "gather_c_start")
    wo4c = _cast_to_slot(w_out[0], jvec, "cast_w_out", after=gc[3])
    w4c = _cast_to_slot(w_in[0], jvec, "cast_w_in", after=wo4c)
    (c8,), _ = _split_wait(gc[0], gc[1], [gc[2]], [], w4c, _gather8_copies, "gather_c_wait")
    cc = jnp.concatenate([c8.reshape(8, D), c_ctx.reshape(1, D), jnp.zeros((7, D), f32)], axis=0)
    m_shard, sc16 = _adaln_shard(cc, w_ada_s)

    conv_w_pad = jnp.pad(conv_w[0], ((0, 5), (0, 0)))
    gm = _split_start([_to_slot(m_shard, 4, chip), _to_slot(conv_w_pad, 4, chip)], [], 6, _gather4_copies,
                      "gather_mod_start")

    all_k = [(0, 0, 0), (0, 0, 1), (0, 0, 2)]
    sem_a, rem_a, w4s, token = _split_start([w4c], [], 1, _near_copies, "weights_near_start", after=gm[4])
    (m4, cw4), _ = _split_wait(gm[0], gm[1], [gm[2], gm[3]], [], token, _gather4_copies, "gather_mod_wait")
    m_full = jnp.transpose(m4, (1, 0, 2)).reshape(16, 4 * ncol)
    mrow = lax.dynamic_slice(m_full, (dev, 0), (1, 3 * D))
    mrow_c = m_full[8:9]
    conv_w_full = jnp.transpose(cw4[:, 0:3, :], (1, 0, 2)).reshape(3, DC)
    waves = {}

    def near(after):
        (w4w,), _ = _split_wait(sem_a, rem_a, [w4s], [], after, _near_copies, "weights_near_wait")
        sem_b, rem_b, w4b, started = _split_start([w4w], [], 2, _pass_relay_copies, "weights_pass_start")
        waves["pass"] = (sem_b, rem_b)
        return w4b, started

    def near2(w4, after):
        (w4w,), _ = _split_wait(*waves["pass"], [w4], [], after, _pass_copy, "weights_pass_wait")
        return w4w

    def far(w4, after, filler):
        (w4w,), _ = _split_wait(*waves["pass"], [w4], [], after, _relay_copy, "weights_far_wait")
        w4x, sem_c, rem_c, wo4s, started = _forward_then_start(w4w, wo4c, all_k, "weights_far_forward_out_start")
        (w4f,), _ = _split_wait(sem_c, rem_c, [w4x], [], filler(started), _diag_forward_copy,
                                "weights_far_forward_wait")
        waves["out"] = (sem_c, rem_c, wo4s)
        return w4f, started

    def w_out_arrived(after):
        sem_c, rem_c, wo4s = waves["out"]
        (wow,) = _halves_wait(sem_c, rem_c, [wo4s], after, all_k, "weights_out_wait")
        sem_d, rem_d, wof, started = _split_start([wow], [], 3, _forward_copies, "weights_out_forward_start")
        waves["out_forward"] = (sem_d, rem_d, wof)
        return started

    def w_out_gathered(after):
        sem_d, rem_d, wof = waves["out_forward"]
        (wo,), _ = _split_wait(sem_d, rem_d, [wof], [], after, _forward_copies, "weights_out_forward_wait")
        return wo.reshape(D, D)

    weights = dict(own=w_in[0], jvec=jvec, started=token, first=jvec ^ (1 + cvec), second=jvec ^ (2 - cvec),
                   near=near, near2=near2, far=far, out_arrived=w_out_arrived, out=w_out_gathered)

    exchange = _exchange_copies
    pending = {}

    def on_g_w_out(g_w_out):
        out = _split_start([g_w_out], [SDS((4, D // 8, D), f32)], 4, exchange, "grad_out_pair_start")
        pending["ex_out"] = out
        return out[4]

    def after_conv(dp8):
        ssem_o, rsem_o, g_o, land_o, _ = pending["ex_out"]
        (g_o,), (ex_o,) = _split_wait(ssem_o, rsem_o, [g_o], [land_o], dp8, exchange, "grad_out_pair_wait")
        to32, tob = _pair_sum_w_out(g_o, ex_o, cvec)
        out = _split_start([tob], [SDS((3, D // 8, D), bf16)], 3, _scatter_copies, "grad_out_chip_start")
        pending["sc_out"] = (out, to32)
        return out[4]

    def on_g_w_in_first(g_first):
        out = _split_start([g_first], [SDS((4, D // 2, D), f32)], 4, _block_exchange_copies, "grad_pair_start")
        pending["ex"] = (out[0], out[1], [out[2]], [out[3]])
        return out[4]

    def on_w_in_pair_sum(pair_sum):
        ex_ssem, ex_rsem, ex_srcs, ex_lands = pending["ex"]
        t32, tb = pair_sum(ex_srcs[0], ex_lands[0], ex_ssem, ex_rsem)
        out = _split_start([tb], [SDS((3, D // 2, D), bf16)], 3, _scatter_copies, "grad_chip_start")
        pending["sc_in"] = (out, t32)
        return out[4]

    r = _local_step(x[0], ctx[0], loss_target[0], mrow, mrow_c, b_ada, norm_g, weights, q_norm_g, k_norm_g,
                    rpb[0], conv_w_full, conv_b,
                    dict(g_w_out=on_g_w_out, after_conv=after_conv, first_half=1 - cvec,
                         g_w_in_first=on_g_w_in_first, w_in_pair_sum=on_w_in_pair_sum))
    sc_in, t32 = pending["sc_in"]
    _, (r2,) = _split_wait(sc_in[0], sc_in[1], [sc_in[2]], [sc_in[3]], r["dng"], _scatter_copies, "grad_chip_wait")
    u_in = _chip_sum(t32, r2, jvec, "chip_sum_w_in")
    sc_o, to32 = pending["sc_out"]
    _, (ro2,) = _split_wait(sc_o[0], sc_o[1], [sc_o[2]], [sc_o[3]], u_in, _scatter_copies, "grad_out_chip_wait")
    u_out = _chip_sum(to32, ro2, jvec, "chip_sum_w_out")
    swap = _split_start([u_in, u_out], [SDS(u_in.shape, f32), SDS(u_out.shape, f32)], 2, _swap_copies,
                        "grad_pair_swap_start")

    dm = jnp.concatenate([r["dshift"], r["dscale"], r["dgate"]], axis=1)
    dmc = jnp.concatenate([r["dshift_c"], r["dscale_c"], jnp.zeros((1, D), f32)], axis=1)
    pack_parts = [_rows128(dm), _rows128(dmc), _rows128(r["dng"]), _rows128(r["dng_c"]), _rows128(r["g_gq"]),
                  _rows128(r["g_gk"]), _rows128(r["g_gk_c"]), r["g_rpb"].reshape(H * N_DR, 128),
                  _rows128(r["g_conv_b"]), _rows128(r["g_conv_w"][0:3]), jnp.pad(r["loss_sum"], ((0, 0), (0, 127)))]
    pack = jnp.concatenate([jnp.pad(p, ((0, -p.shape[0] % 8), (0, 0))) for p in pack_parts], axis=0)
    assert pack.shape[0] == PK_ROWS
    gs = _split_start([_to_slot(pack, 8, dev)], [], 7, _gather8_copies, "gather_small_start", after=swap[6])
    (u_in, u_out), (o_in, o_out) = _split_wait(swap[0], swap[1], swap[2:4], swap[4:6], gs[3], _swap_copies,
                                               "grad_pair_swap_wait")
    g_w_in_s, d_w_in, nm_w_in, nv_w_in = _adamw_halves(w_in[0], u_in, o_in, m_w_in[0], v_w_in[0], cvec, "adamw_w_in",
                                                       after=gs[3])
    g_w_out_s, d_w_out, nm_w_out, nv_w_out = _adamw_halves(w_out[0], u_out, o_out, m_w_out[0], v_w_out[0], cvec,
                                                           "adamw_w_out", after=nm_w_in)
    (gathered,), _ = _split_wait(gs[0], gs[1], [gs[2]], [], nm_w_out, _gather8_copies, "gather_small_wait")
    red, dm16 = _small_reduce(gathered)
    loss = red[RS_LOSS, 0] * (0.5 / D)

    g_w_ada_s, cpart = _w_ada_grad(sc16, dm16, w_ada_s, jvec)
    gcp = _split_start([_to_slot(cpart, 4, chip)], [], 3, _gather4_copies, "gather_c_ctx_parts_start")
    d_w_ada, nm_w_ada, nv_w_ada = _adamw(w_ada_s, g_w_ada_s, m_w_ada[0], v_w_ada[0], "adamw_w_ada", after=gcp[3])
    (cparts4,), _ = _split_wait(gcp[0], gcp[1], [gcp[2]], [], nm_w_ada, _gather4_copies, "gather_c_ctx_parts_wait")
    g_c_ctx = _c_ctx_grad(cparts4, c_ctx)

    t_rpb = lambda a: jnp.transpose(a, (0, 2, 1, 3)).reshape(N_DR, H, N_DC)
    t_cw = lambda a: jnp.transpose(a, (1, 0, 2))
    small = _adamw_small(
        red, g_c_ctx, jvec,
        [c_ctx, b_ada, norm_g, q_norm_g, k_norm_g, t_rpb(rpb), t_cw(conv_w), conv_b],
        [m_c_ctx, m_b_ada, m_norm_g, m_q_norm_g, m_k_norm_g, t_rpb(m_rpb), t_cw(m_conv_w), m_conv_b],
        [v_c_ctx, v_b_ada, v_norm_g, v_q_norm_g, v_k_norm_g, t_rpb(v_rpb), t_cw(v_conv_w), v_conv_b])
    for kind in small:
        kind[5] = jnp.transpose(kind[5].reshape(1, N_DR, H, N_DC), (0, 2, 1, 3))
        kind[6] = jnp.transpose(kind[6], (1, 0, 2))

    def ordered(kind, big_w_ada, big_w_in, big_w_out):
        s_c_ctx, s_b_ada, s_norm_g, s_q, s_k, s_rpb, s_conv_w, s_conv_b = small[kind]
        return [s_c_ctx, big_w_ada[None], s_b_ada, s_norm_g, big_w_in[None], s_q, s_k, s_rpb, s_conv_w,
                s_conv_b, big_w_out[None]]

    grads = ordered(0, g_w_ada_s, g_w_in_s, g_w_out_s)
    deltas = ordered(1, d_w_ada, d_w_in, d_w_out)
    new_m = ordered(2, nm_w_ada, nm_w_in, nm_w_out)
    new_v = ordered(3, nv_w_ada, nv_w_in, nv_w_out)
    return (loss, r["grad_x"][None], *grads, *deltas, *new_m, *new_v)
```

```python
import functools

import jax
import jax.numpy as jnp
import numpy as np
from jax import lax
from jax.experimental import pallas as pl
from jax.experimental.pallas import tpu as pltpu

f32, bf16, i32 = jnp.float32, jnp.bfloat16, jnp.int32
MESH = pl.DeviceIdType.MESH
HIGHEST = lax.Precision.HIGHEST

D = 1024
S = 2048
L = 256
GW = 64
ROWS = S // GW
H = 8
DH = 64
DA = H * DH
DC = 512
WIN_H, WIN_W = 8, 16
N_DR, N_DC = 2 * WIN_H - 1, 2 * WIN_W - 1
RMS_EPS = 1e-6
ROPE_THETA = 10000.0
QK_SCALE = DH ** -0.5
NEG = -1e30

QB = 128
NQB = S // QB
KR = 9
KB = KR * GW
TILE_GEOM = ((0, 0), (2, 0), (4, 0), (28, 23), (30, 23))
NT = len(TILE_GEOM)

ADAM_LR, ADAM_B1, ADAM_B2, ADAM_EPS, ADAM_WD, ADAM_STEP = 0.001, 0.9, 0.999, 1e-08, 0.01, 10

VMEM_SPEC = pl.BlockSpec(memory_space=pltpu.VMEM)
ANY_SPEC = pl.BlockSpec(memory_space=pl.ANY)
SMEM_SPEC = pl.BlockSpec(memory_space=pltpu.SMEM)
SDS = jax.ShapeDtypeStruct


_pallas_call = pl.pallas_call


def _hbm_call(body, *, out_shape, in_specs=None, out_specs=None, grid_spec=None, **kw):
    n_pre = 0
    if grid_spec is not None:
        ispecs, ospecs, n_pre = grid_spec.in_specs, grid_spec.out_specs, grid_spec.num_scalar_prefetch
        kw["grid_spec"] = grid_spec
    else:
        ispecs, ospecs = in_specs, out_specs
        kw.update(in_specs=in_specs, out_specs=out_specs)

    def blocked(spec):
        return isinstance(spec, pl.BlockSpec) and spec.block_shape is not None

    single = not isinstance(out_shape, (tuple, list))
    shapes = [out_shape] if single else list(out_shape)
    ospec_list = list(ospecs) if isinstance(ospecs, (tuple, list)) else [ospecs]
    shapes = [pltpu.HBM(s.shape, s.dtype) if blocked(sp) else s for s, sp in zip(shapes, ospec_list)]
    call = _pallas_call(body, out_shape=shapes[0] if single else tuple(shapes), **kw)

    def run(*args):
        arrays = [pltpu.with_memory_space_constraint(a, pltpu.HBM) if blocked(sp) else a
                  for a, sp in zip(args[n_pre:], ispecs)]
        return call(*args[:n_pre], *arrays)

    return run


def _cp(vmem_mb=None, **kw):
    if vmem_mb is not None:
        kw["vmem_limit_bytes"] = vmem_mb << 20
    return pltpu.CompilerParams(**kw)


def _silu(z):
    return z * jax.nn.sigmoid(z)


def _dsilu(z):
    sg = jax.nn.sigmoid(z)
    return sg * (1.0 + z * (1.0 - sg))


def _row_start(i):
    return min(max(i - WIN_H // 2, 0), ROWS - WIN_H)


def _my_pos():
    return lax.axis_index("x"), lax.axis_index("y"), lax.axis_index("c")


def _flip(v, bit):
    return 1 - v if bit else v


def _swap_copies(srcs, lands, ssem, rsem):
    x, y, c = _my_pos()
    return [pltpu.make_async_remote_copy(src_ref=srcs[a], dst_ref=lands[a], send_sem=ssem.at[a], recv_sem=rsem.at[a],
                                         device_id=(x, y, 1 - c), device_id_type=MESH) for a in range(len(srcs))]


HBM_SPEC = pl.BlockSpec(memory_space=pltpu.HBM)
SEM_SPEC = pl.BlockSpec(memory_space=pltpu.SEMAPHORE)
DATAFLOW = pltpu.SideEffectType.DATAFLOW_SIDE_EFFECTING


def _peer_chips(x, y, c):
    out = []
    for k in range(1, 4):
        px, py = _flip(x, (k >> 1) & 1), _flip(y, k & 1)
        out.append(((px, py, c), 2 * px + py))
    return out


def _half_copies(srcs, dsts, ssem, rsem, which):
    x, y, c = _my_pos()
    j = 2 * x + y
    peers = _peer_chips(x, y, c)
    pairs = []
    for pos, group, k in which:
        half = srcs[pos].shape[1] // 2
        mine = pl.ds(pl.multiple_of(c * half, 8), half)
        dev, pj = peers[k]
        sem = 3 * group + k
        send = pltpu.make_async_remote_copy(src_ref=srcs[pos].at[j, mine], dst_ref=dsts[pos].at[j, mine],
                                            send_sem=ssem.at[sem], recv_sem=rsem.at[sem], device_id=dev,
                                            device_id_type=MESH)
        arrive = pltpu.make_async_remote_copy(src_ref=srcs[pos].at[j, mine], dst_ref=dsts[pos].at[pj, mine],
                                              send_sem=ssem.at[sem], recv_sem=rsem.at[sem], device_id=dev,
                                              device_id_type=MESH)
        pairs.append((send, arrive))
    return pairs


def _halves_wait(ssem, rsem, bigs, after, which, name):
    nb = len(bigs)

    def body(*refs):
        b_in = refs[:nb]
        ssem_ref, rsem_ref = refs[nb], refs[nb + 1]
        for send, arrive in _half_copies(b_in, b_in, ssem_ref, rsem_ref, which):
            send.wait_send()
            arrive.wait_recv()

    return _hbm_call(
        body, name=name, out_shape=tuple(pltpu.HBM(b.shape, b.dtype) for b in bigs),
        in_specs=[HBM_SPEC] * nb + [SEM_SPEC, SEM_SPEC, ANY_SPEC], out_specs=tuple([HBM_SPEC] * nb),
        input_output_aliases={a: a for a in range(nb)}, compiler_params=_cp(has_side_effects=DATAFLOW),
    )(*bigs, ssem, rsem, after)


FORWARD_SEM = 3


def _diag_forward_copy(srcs, dsts, ssem, rsem):
    x, y, c = _my_pos()
    half = srcs[0].shape[1] // 2
    diag = 3 - (2 * x + y)
    mine = pl.ds(pl.multiple_of(c * half, 8), half)
    other = pl.ds(pl.multiple_of((1 - c) * half, 8), half)
    return [_Copy(srcs[0].at[diag, mine], dsts[0].at[diag, mine], dsts[0].at[diag, other], ssem.at[FORWARD_SEM],
                  rsem.at[FORWARD_SEM], (x, y, 1 - c))]


def _forward_then_start(fwd, big, order, name):
    def body(f_in, b_in, f_out, ssem, rsem, b_out, token):
        _diag_forward_copy([f_in], [f_out], ssem, rsem)[0].start()
        for send, _ in _half_copies([b_in], [b_out], ssem, rsem, order):
            send.start()
        token[...] = jnp.zeros_like(token)

    n_sem = FORWARD_SEM + 1
    out_shape = (pltpu.HBM(fwd.shape, fwd.dtype), pltpu.SemaphoreType.DMA((n_sem,)), pltpu.SemaphoreType.DMA((n_sem,)),
                 pltpu.HBM(big.shape, big.dtype), SDS((8, 128), f32))
    return _hbm_call(
        body, name=name, out_shape=out_shape, in_specs=[HBM_SPEC, HBM_SPEC],
        out_specs=(HBM_SPEC, SEM_SPEC, SEM_SPEC, HBM_SPEC, VMEM_SPEC), input_output_aliases={0: 0, 1: 3},
        compiler_params=_cp(has_side_effects=DATAFLOW),
    )(*[pltpu.with_memory_space_constraint(b, pltpu.HBM) for b in (fwd, big)])


def _cast_to_slot(w, jvec, name, after=None):
    rows, cols = w.shape
    tr = 256

    def body(j_ref, w_ref, after_ref, o_ref):
        o_ref[...] = w_ref[...].astype(bf16)

    grid_spec = pltpu.PrefetchScalarGridSpec(
        num_scalar_prefetch=1, grid=(rows // tr,),
        in_specs=[pl.BlockSpec((tr, cols), lambda i, j: (i, 0)), ANY_SPEC],
        out_specs=pl.BlockSpec((None, tr, cols), lambda i, j: (j[0], i, 0)))
    return _hbm_call(body, name=name, out_shape=SDS((4, rows, cols), bf16),
                     grid_spec=grid_spec)(jvec, w, jvec if after is None else after)


def _exchange_copies(srcs, lands, ssem, rsem):
    x, y, c = _my_pos()
    half = srcs[0].shape[0] // 8
    cps = []
    for jb in range(4):
        src = srcs[0].at[pl.ds(pl.multiple_of((2 * jb + 1 - c) * half, 8), half)]
        cps.append(pltpu.make_async_remote_copy(src_ref=src, dst_ref=lands[0].at[jb], send_sem=ssem.at[jb],
                                                recv_sem=rsem.at[jb], device_id=(x, y, 1 - c), device_id_type=MESH))
    return cps


def _block_exchange_copies(srcs, lands, ssem, rsem):
    x, y, c = _my_pos()
    return [pltpu.make_async_remote_copy(src_ref=srcs[0].at[jb], dst_ref=lands[0].at[jb], send_sem=ssem.at[jb],
                                         recv_sem=rsem.at[jb], device_id=(x, y, 1 - c), device_id_type=MESH)
            for jb in range(4)]


def _scatter_copies(srcs, lands, ssem, rsem):
    x, y, c = _my_pos()
    cps = []
    for a in range(len(srcs)):
        for k, (dev, pj) in enumerate(_peer_chips(x, y, c)):
            cps.append(pltpu.make_async_remote_copy(src_ref=srcs[a].at[pj], dst_ref=lands[a].at[k],
                                                    send_sem=ssem.at[3 * a + k], recv_sem=rsem.at[3 * a + k],
                                                    device_id=dev, device_id_type=MESH))
    return cps


class _Copy:
    def __init__(self, src, dst, arrive, ssem, rsem, dev):
        make = lambda to: pltpu.make_async_remote_copy(src_ref=src, dst_ref=to, send_sem=ssem, recv_sem=rsem,
                                                       device_id=dev, device_id_type=MESH)
        send, arrival = make(dst), make(arrive)
        self.start, self.wait_send, self.wait_recv = send.start, send.wait_send, arrival.wait_recv


def _toward(x, y, along_x):
    return x + along_x * (1 - 2 * x), y + (1 - along_x) * (1 - 2 * y)


def _near_copies(srcs, dsts, ssem, rsem):
    x, y, c = _my_pos()
    px, py = _toward(x, y, c)
    j = 2 * x + y
    return [_Copy(srcs[0].at[j], dsts[0].at[j], dsts[0].at[2 * px + py], ssem.at[0], rsem.at[0], (px, py, c))]


def _pass_copy(srcs, dsts, ssem, rsem):
    x, y, c = _my_pos()
    px, py = _toward(x, y, c)
    qx, qy = _toward(x, y, 1 - c)
    got = 2 * px + py
    return [_Copy(srcs[0].at[got], dsts[0].at[got], dsts[0].at[2 * qx + qy], ssem.at[0], rsem.at[0], (x, y, 1 - c))]


def _relay_copy(srcs, dsts, ssem, rsem):
    x, y, c = _my_pos()
    px, py = _toward(x, y, c)
    qx, qy = _toward(x, y, 1 - c)
    half = srcs[0].shape[1] // 2
    mine = pl.ds(pl.multiple_of(c * half, 8), half)
    got, diag = 2 * px + py, 3 - (2 * x + y)
    return [_Copy(srcs[0].at[got, mine], dsts[0].at[got, mine], dsts[0].at[diag, mine], ssem.at[1], rsem.at[1],
                  (qx, qy, c))]


def _forward_copies(srcs, dsts, ssem, rsem):
    x, y, c = _my_pos()
    half = srcs[0].shape[1] // 2
    mine = pl.ds(pl.multiple_of(c * half, 8), half)
    other = pl.ds(pl.multiple_of((1 - c) * half, 8), half)
    return [_Copy(srcs[0].at[pj, mine], dsts[0].at[pj, mine], dsts[0].at[pj, other], ssem.at[k], rsem.at[k],
                  (x, y, 1 - c)) for k, (_, pj) in enumerate(_peer_chips(x, y, c))]


def _pass_relay_copies(srcs, dsts, ssem, rsem):
    return _pass_copy(srcs, dsts, ssem, rsem) + _relay_copy(srcs, dsts, ssem, rsem)


def _gather8_copies(srcs, dsts, ssem, rsem):
    x, y, c = _my_pos()
    me = 4 * x + 2 * y + c
    cps = []
    for a in range(len(srcs)):
        for k in range(1, 8):
            tgt = (_flip(x, (k >> 2) & 1), _flip(y, (k >> 1) & 1), _flip(c, k & 1))
            cps.append(_Copy(srcs[a].at[me], dsts[a].at[me], dsts[a].at[4 * tgt[0] + 2 * tgt[1] + tgt[2]],
                             ssem.at[7 * a + k - 1], rsem.at[7 * a + k - 1], tgt))
    return cps


def _gather4_copies(srcs, dsts, ssem, rsem):
    x, y, c = _my_pos()
    j = 2 * x + y
    cps = []
    for a in range(len(srcs)):
        for k, (dev, pj) in enumerate(_peer_chips(x, y, c)):
            cps.append(_Copy(srcs[a].at[j], dsts[a].at[j], dsts[a].at[pj], ssem.at[3 * a + k], rsem.at[3 * a + k], dev))
    return cps


def _to_slot(a, n, i):
    return lax.dynamic_update_slice(jnp.zeros((n,) + a.shape, a.dtype), a[None], (i,) + (0,) * a.ndim)


def _split_start(srcs, land_shapes, n_cp, make, name, after=None):
    ns, nl = len(srcs), len(land_shapes)
    n_in = ns + nl + (after is not None)

    def body(*refs):
        s_in = refs[:ns]
        ssem, rsem = refs[n_in], refs[n_in + 1]
        s_out = refs[n_in + 2:n_in + 2 + ns]
        l_out = refs[n_in + 2 + ns:n_in + 2 + ns + nl]
        token = refs[n_in + 2 + ns + nl]
        for cp in make(s_in, l_out if nl else s_out, ssem, rsem):
            cp.start()
        token[...] = jnp.zeros_like(token)

    lands = [pltpu.with_memory_space_constraint(lax.empty(sh.shape, sh.dtype), pltpu.HBM) for sh in land_shapes]
    out_shape = (pltpu.SemaphoreType.DMA((n_cp,)), pltpu.SemaphoreType.DMA((n_cp,)),
                 *[pltpu.HBM(b.shape, b.dtype) for b in srcs], *[pltpu.HBM(b.shape, b.dtype) for b in land_shapes],
                 SDS((8, 128), f32))
    return _hbm_call(
        body, name=name, out_shape=out_shape, in_specs=[HBM_SPEC] * (ns + nl) + [ANY_SPEC] * (after is not None),
        out_specs=(SEM_SPEC, SEM_SPEC, *[HBM_SPEC] * (ns + nl), VMEM_SPEC),
        input_output_aliases={i: 2 + i for i in range(ns + nl)}, compiler_params=_cp(has_side_effects=DATAFLOW),
    )(*[pltpu.with_memory_space_constraint(b, pltpu.HBM) for b in srcs], *lands, *([] if after is None else [after]))


def _split_wait(ssem, rsem, srcs, lands, after, make, name):
    ns, nl = len(srcs), len(lands)

    def body(*refs):
        s_in, l_in = refs[:ns], refs[ns:ns + nl]
        ssem_ref, rsem_ref = refs[ns + nl], refs[ns + nl + 1]
        for cp in make(s_in, l_in if nl else s_in, ssem_ref, rsem_ref):
            cp.wait_send()
            cp.wait_recv()

    outs = _hbm_call(
        body, name=name, out_shape=tuple(pltpu.HBM(b.shape, b.dtype) for b in (*srcs, *lands)),
        in_specs=[HBM_SPEC] * (ns + nl) + [SEM_SPEC, SEM_SPEC, ANY_SPEC], out_specs=tuple([HBM_SPEC] * (ns + nl)),
        input_output_aliases={i: i for i in range(ns + nl)}, compiler_params=_cp(has_side_effects=DATAFLOW),
    )(*srcs, *lands, ssem, rsem, after)
    return list(outs[:ns]), list(outs[ns:])


def _adaln_shard(cc, w_ada_shard):
    def body(c_ref, w_ref, m_ref, sc_ref):
        sc = _silu(c_ref[...])
        sc_ref[...] = sc
        m_ref[...] = jnp.dot(sc, w_ref[...], precision=HIGHEST, preferred_element_type=f32)

    return _hbm_call(
        body, name="adaln_shard", out_shape=(SDS((16, w_ada_shard.shape[1]), f32), SDS((16, D), f32)),
        in_specs=[VMEM_SPEC, VMEM_SPEC], out_specs=(VMEM_SPEC, VMEM_SPEC), compiler_params=_cp(32),
    )(cc, w_ada_shard)


def _prenorm(xx, norm_g, mrow, b_ada, tm, name, after=None):
    n = xx.shape[0]

    def body(x_ref, g_ref, m_ref, b_ref, after_ref, h_ref):
        x = x_ref[...]
        shift = m_ref[:, 0:D] + b_ref[:, 0:D]
        scale = m_ref[:, D:2 * D] + b_ref[:, D:2 * D]
        r = lax.rsqrt(jnp.mean(x * x, axis=-1, keepdims=True) + RMS_EPS)
        y = (x * r) * g_ref[...]
        h_ref[...] = (y * (1.0 + scale) + shift).astype(bf16)

    row = lambda i: (i, 0)
    fixed = lambda i: (0, 0)
    return _hbm_call(
        body, name=name, out_shape=SDS((n, D), bf16), grid=(n // tm,),
        in_specs=[pl.BlockSpec((tm, D), row), pl.BlockSpec((1, D), fixed), pl.BlockSpec((1, 3 * D), fixed),
                  pl.BlockSpec((1, 3 * D), fixed), ANY_SPEC],
        out_specs=pl.BlockSpec((tm, D), row),
    )(xx, norm_g, mrow, b_ada, b_ada if after is None else after)


def _in_proj_own(h, w_own, jvec):
    tm = 512

    def body(j_ref, h_ref, w_ref, p_ref):
        p_ref[...] = jnp.dot(h_ref[...], w_ref[...].astype(bf16), preferred_element_type=f32)

    grid_spec = pltpu.PrefetchScalarGridSpec(
        num_scalar_prefetch=1, grid=(S // tm,),
        in_specs=[pl.BlockSpec((tm, D), lambda i, j: (i, 0)), pl.BlockSpec((D, D), lambda i, j: (0, 0))],
        out_specs=pl.BlockSpec((tm, D), lambda i, j: (i, j[0])))
    return _hbm_call(body, name="in_proj_own", out_shape=SDS((S, 4 * D), f32), grid_spec=grid_spec,
                     compiler_params=_cp(40))(jvec, h, w_own)


def _in_proj_block(h, w4, p, bvec, name, after=None):
    tm = 512

    def body(b_ref, h_ref, w_ref, p_in_ref, after_ref, p_ref):
        p_ref[...] = jnp.dot(h_ref[...], w_ref[...], preferred_element_type=f32)

    grid_spec = pltpu.PrefetchScalarGridSpec(
        num_scalar_prefetch=1, grid=(S // tm,),
        in_specs=[pl.BlockSpec((tm, D), lambda i, b: (i, 0)), pl.BlockSpec((None, D, D), lambda i, b: (b[0], 0, 0)),
                  ANY_SPEC, ANY_SPEC],
        out_specs=pl.BlockSpec((tm, D), lambda i, b: (i, b[0])))
    return _hbm_call(body, name=name, out_shape=SDS((S, 4 * D), f32), grid_spec=grid_spec,
                     input_output_aliases={3: 0})(bvec, h, w4, p, bvec if after is None else after)


def _ctx_proj(hc, w4):
    def body(h_ref, w0_ref, w1_ref, p_ref):
        hv = h_ref[...]
        p_ref[:, 0:DA] = jnp.dot(hv, w0_ref[:, DA:2 * DA], preferred_element_type=f32)
        p_ref[:, DA:2 * DA] = jnp.dot(hv, w1_ref[:, 0:DA], preferred_element_type=f32)

    return _hbm_call(
        body, name="ctx_proj", out_shape=SDS((L, 2 * DA), f32), grid=(1,),
        in_specs=[pl.BlockSpec((L, D), lambda i: (0, 0)), pl.BlockSpec((None, D, D), lambda i: (0, 0, 0)),
                  pl.BlockSpec((None, D, D), lambda i: (1, 0, 0))],
        out_specs=pl.BlockSpec((L, 2 * DA), lambda i: (0, 0)),
    )(hc, w4, w4)


def _head_ones():
    r = lax.broadcasted_iota(i32, (DA, DA), 0) // DH
    c = lax.broadcasted_iota(i32, (DA, DA), 1) // DH
    return (r == c).astype(bf16)


def _head_sum(v, ones_bd):
    hi = v.astype(bf16)
    lo = (v - hi.astype(f32)).astype(bf16)
    return jnp.dot(hi, ones_bd, preferred_element_type=f32) + jnp.dot(lo, ones_bd, preferred_element_type=f32)


def _swap16(v):
    lane = lax.broadcasted_iota(i32, v.shape, 1)
    return jnp.where((lane & 31) < 16, pltpu.roll(v, DA - 16, 1), pltpu.roll(v, 16, 1))


def _rope_block(ct_ref, rt_ref, tm):
    rows = [jnp.tile(rt_ref[8 * j:8 * j + 8, :], (GW // 8, 1)) for j in range(tm // GW)]
    return jnp.tile(ct_ref[...], (tm // GW, 1)) + jnp.concatenate(rows, axis=0)


def _rope_specs(tm):
    col = pl.BlockSpec((GW, DA), lambda i: (0, 0))
    row = pl.BlockSpec((8 * tm // GW, DA), lambda i: (i, 0))
    return [col, row, col, row]


def _qk_prep(p, gq, gk, rope):
    tm = 512

    def body(qk_ref, v_ref, gq_ref, gk_ref, cc_ref, cr_ref, sc_ref, sr_ref, qr_ref, qp_ref, kr_ref, vh_ref):
        ones_bd = _head_ones()
        cs, sn = _rope_block(cc_ref, cr_ref, tm), _rope_block(sc_ref, sr_ref, tm)
        q = qk_ref[:, 0:DA]
        k = qk_ref[:, DA:2 * DA]
        yq = (q * lax.rsqrt(_head_sum(q * q, ones_bd) * (1.0 / DH) + RMS_EPS)) * gq_ref[...]
        yk = (k * lax.rsqrt(_head_sum(k * k, ones_bd) * (1.0 / DH) + RMS_EPS)) * gk_ref[...]
        qr = (yq * cs + _swap16(yq) * sn) * QK_SCALE
        qp = yq * QK_SCALE
        kr = yk * cs + _swap16(yk) * sn
        vv = v_ref[...]
        for hh in range(H):
            sl = slice(hh * DH, (hh + 1) * DH)
            qr_ref[hh] = qr[:, sl].astype(bf16)
            qp_ref[hh] = qp[:, sl].astype(bf16)
            kr_ref[hh] = kr[:, sl].astype(bf16)
            vh_ref[hh] = vv[:, sl].astype(bf16)

    hm = SDS((H, S, DH), bf16)
    hspec = pl.BlockSpec((H, tm, DH), lambda i: (0, i, 0))
    fixed = lambda i: (0, 0)
    return _hbm_call(
        body, name="qk_prep", out_shape=(hm, hm, hm, hm), grid=(S // tm,),
        in_specs=[pl.BlockSpec((tm, 2 * DA), lambda i: (i, 0)), pl.BlockSpec((tm, DA), lambda i: (i, 2)),
                  pl.BlockSpec((1, DA), fixed), pl.BlockSpec((1, DA), fixed)] + _rope_specs(tm),
        out_specs=(hspec, hspec, hspec, hspec),
    )(p, p, gq, gk, *rope)


def _ctx_prep(pc, gk):
    def body(p_ref, gk_ref, kc_ref, vc_ref):
        ones_bd = _head_ones()
        k = p_ref[:, 0:DA]
        yk = (k * lax.rsqrt(_head_sum(k * k, ones_bd) * (1.0 / DH) + RMS_EPS)) * gk_ref[...]
        vv = p_ref[:, DA:2 * DA]
        for hh in range(H):
            sl = slice(hh * DH, (hh + 1) * DH)
            kc_ref[hh] = yk[:, sl].astype(bf16)
            vc_ref[hh] = vv[:, sl].astype(bf16)

    hm = SDS((H, L, DH), bf16)
    return _hbm_call(
        body, name="ctx_prep", out_shape=(hm, hm), in_specs=[VMEM_SPEC, VMEM_SPEC], out_specs=(VMEM_SPEC, VMEM_SPEC),
    )(pc, gk)


def _tile_pieces():
    out = []
    for (i0, u0) in TILE_GEOM:
        rows = []
        for j in range(2):
            i = i0 + j
            rs = _row_start(i)
            rows.append([(u0 + u - i + WIN_H - 1) if rs <= u0 + u < rs + WIN_H else None for u in range(KR)])
        out.append(rows)
    return out


def _bias_prep(rpb_rev_pad, after=None):
    pieces = _tile_pieces()

    def body(r_ref, after_ref, o_ref):
        rp = r_ref[...]
        xs = jnp.concatenate([pltpu.roll(jnp.broadcast_to(rp[dr:dr + 1, :], (GW, 128)), 128 - (WIN_W - 1), 1,
                                         stride=1, stride_axis=0) for dr in range(N_DR)], axis=0)
        row = lax.broadcasted_iota(i32, xs.shape, 0)
        lane = lax.broadcasted_iota(i32, xs.shape, 1)
        k = row & (GW - 1)
        c0 = jnp.clip(lane - WIN_W // 2, 0, GW - WIN_W)
        xs = jnp.where((k >= c0) & (k < c0 + WIN_W), xs, NEG)
        neg = jnp.full((GW, GW), NEG, f32)
        for t in range(NT):
            for j in range(2):
                for u in range(KR):
                    dr = pieces[t][j][u]
                    piece = neg if dr is None else xs[dr * GW:(dr + 1) * GW, 0:GW]
                    o_ref[t, u * GW:(u + 1) * GW, j * GW:(j + 1) * GW] = piece

    return _hbm_call(
        body, name="bias_prep", out_shape=SDS((H, NT, KB, QB), f32), grid=(H,),
        in_specs=[pl.BlockSpec((None, N_DR, 128), lambda h: (h, 0, 0)), ANY_SPEC],
        out_specs=pl.BlockSpec((None, NT, KB, QB), lambda h: (h, 0, 0, 0)),
    )(rpb_rev_pad, rpb_rev_pad if after is None else after)


def _bias_tiles(rpb2, after=None):
    return _bias_prep(jnp.pad(rpb2[:, :, ::-1], ((0, 0), (0, 0), (0, 128 - N_DC))), after)


def _block_geom(b):
    qs = b * QB
    ks = min(max(2 * b - 4, 0), ROWS - KR) * GW
    t = b if b < 2 else (b - (NQB - NT) if b > NQB - 3 else 2)
    return qs, ks, t


def _tt(a, b):
    return lax.dot_general(a, b, (((1,), (1,)), ((), ())), preferred_element_type=f32)


def _tn(a, b):
    return lax.dot_general(a, b, (((0,), (0,)), ((), ())), preferred_element_type=f32)


def _softmax_t(s_lat, s_ctx):
    m = jnp.maximum(jnp.max(s_lat, axis=0, keepdims=True), jnp.max(s_ctx, axis=0, keepdims=True))
    e_lat = jnp.exp(s_lat - m)
    e_ctx = jnp.exp(s_ctx - m)
    inv = 1.0 / (jnp.sum(e_lat, axis=0, keepdims=True) + jnp.sum(e_ctx, axis=0, keepdims=True))
    return e_lat * inv, e_ctx * inv


def _staged(n_blocks, stages):
    held = [dict() for _ in stages]
    for step in range(n_blocks + len(stages) - 1):
        for s, fn in enumerate(stages):
            b = step - s
            if 0 <= b < n_blocks:
                held[s][b] = fn(b) if s == 0 else fn(b, held[s - 1].pop(b))


def _attn_fwd(qr, qp, kr, vh, kc, vc, btt):
    def body(qr_ref, qp_ref, kr_ref, v_ref, kc_ref, vc_ref, bt_ref, o_ref):
        kcv, vcv = kc_ref[...], vc_ref[...]

        def scores(b):
            qs, ks, t = _block_geom(b)
            return (_tt(kr_ref[ks:ks + KB, :], qr_ref[qs:qs + QB, :]) + bt_ref[t], _tt(kcv, qp_ref[qs:qs + QB, :]))

        def probs(b, sc):
            p_lat, p_ctx = _softmax_t(*sc)
            return p_lat.astype(bf16), p_ctx.astype(bf16)

        def values(b, p):
            qs, ks, _ = _block_geom(b)
            o_ref[qs:qs + QB, :] = _tn(p[0], v_ref[ks:ks + KB, :]) + _tn(p[1], vcv)

        _staged(NQB, (scores, probs, values))

    sq = pl.BlockSpec((None, S, DH), lambda h: (h, 0, 0))
    sc = pl.BlockSpec((None, L, DH), lambda h: (h, 0, 0))
    return _hbm_call(
        body, name="attn_fwd", out_shape=SDS((H, S, DH), f32), grid=(H,),
        in_specs=[sq, sq, sq, sq, sc, sc, pl.BlockSpec((None, NT, KB, QB), lambda h: (h, 0, 0, 0))],
        out_specs=sq, compiler_params=_cp(48),
    )(qr, qp, kr, vh, kc, vc, btt)


def _shift_rows(v, down):
    n = v.shape[0]
    row = lax.broadcasted_iota(i32, v.shape, 0)
    if down:
        return jnp.where(row == 0, 0.0, pltpu.roll(v, 1, 0))
    return jnp.where(row == n - 1, 0.0, pltpu.roll(v, n - 1, 0))


def _conv_specs():
    col = lambda off: pl.BlockSpec((S, 128), lambda i, off=off: (0, off + i))
    return [col(16), col(20), col(24), col(28), pl.BlockSpec((3, 128), lambda i: (0, i)),
            pl.BlockSpec((1, 128), lambda i: (0, i))]


def _conv_fwd(p, conv_w, conv_b, after=None):
    def body(u_ref, bg_ref, cg_ref, zc_ref, w_ref, b_ref, after_ref, o_ref):
        cu = cg_ref[...] * u_ref[...]
        cv = b_ref[...] + _shift_rows(cu, True) * w_ref[0:1, :]
        cv = cv + cu * w_ref[1:2, :]
        cv = cv + _shift_rows(cu, False) * w_ref[2:3, :]
        o_ref[...] = ((bg_ref[...] * cv) * _silu(zc_ref[...])).astype(bf16)

    return _hbm_call(
        body, name="conv_fwd", out_shape=SDS((S, DC), bf16), grid=(DC // 128,),
        in_specs=_conv_specs() + [ANY_SPEC], out_specs=pl.BlockSpec((S, 128), lambda i: (0, i)),
        compiler_params=_cp(40),
    )(p, p, p, p, conv_w, conv_b, conv_b if after is None else after)


DP_Q, DP_K, DP_V, DP_ZA, DP_U, DP_BG, DP_CG, DP_ZC = range(8)


def _out_proj_loss(o, p, conv_g, w_out, xx, tgt, mrow, b_ada):
    tm = 512

    def body(o_ref, za_ref, c_ref, w_ref, x_ref, t_ref, m_ref, b_ref,
             dy_ref, dconv_ref, dp_ref, do_ref, gwo_ref, dgate_ref, loss_ref):
        k = pl.program_id(0)

        @pl.when(k == 0)
        def _():
            gwo_ref[...] = jnp.zeros_like(gwo_ref)
            dgate_ref[...] = jnp.zeros_like(dgate_ref)
            loss_ref[0, 0] = 0.0

        gate = m_ref[:, 2 * D:3 * D] + b_ref[:, 2 * D:3 * D]
        za = za_ref[...]
        sz = _silu(za)
        om = _merge_heads(o_ref)
        av, cv = (om * sz).astype(bf16), c_ref[...]
        mo = jnp.dot(av, w_ref[0:DA, :], preferred_element_type=f32)
        mo = mo + jnp.dot(cv, w_ref[DA:DA + DC, :], preferred_element_type=f32)
        y = x_ref[...] + gate * mo
        diff = y - t_ref[...]
        loss_ref[0, 0] += jnp.sum(diff * diff)
        dy = diff * (1.0 / D)
        dy_ref[...] = dy
        dgate_ref[...] += jnp.sum(dy * mo, axis=0, keepdims=True)
        dmo = (dy * gate).astype(bf16)
        dmix = _tt(dmo, w_ref[...])
        dattn = dmix[:, 0:DA]
        dconv_ref[...] = dmix[:, DA:DA + DC]
        a = dattn * sz
        for hh in range(H):
            do_ref[hh] = a[:, hh * DH:(hh + 1) * DH].astype(bf16)
        dp_ref[...] = ((dattn * _dsilu(za)) * om).astype(bf16)
        gwo_ref[0:DA, :] += _tn(av, dmo)
        gwo_ref[DA:DA + DC, :] += _tn(cv, dmo)

    row = lambda i: (i, 0)
    fixed = lambda i: (0, 0)
    hspec = pl.BlockSpec((H, tm, DH), lambda i: (0, i, 0))
    return _hbm_call(
        body, name="out_proj_loss",
        out_shape=(SDS((S, D), f32), SDS((S, DC), f32), SDS((8, S, DA), bf16), SDS((H, S, DH), bf16),
                   SDS((D, D), f32), SDS((1, D), f32), SDS((1, 1), f32)),
        grid=(S // tm,),
        in_specs=[hspec, pl.BlockSpec((tm, DA), lambda i: (i, 3)), pl.BlockSpec((tm, DC), row),
                  pl.BlockSpec((D, D), fixed), pl.BlockSpec((tm, D), row), pl.BlockSpec((tm, D), row),
                  pl.BlockSpec((1, 3 * D), fixed), pl.BlockSpec((1, 3 * D), fixed)],
        out_specs=(pl.BlockSpec((tm, D), row), pl.BlockSpec((tm, DC), row),
                   pl.BlockSpec((None, tm, DA), lambda i: (DP_ZA, i, 0)), hspec, pl.BlockSpec((D, D), fixed),
                   pl.BlockSpec((1, D), fixed), SMEM_SPEC),
        compiler_params=_cp(56, dimension_semantics=("arbitrary",)),
    )(o, p, conv_g, w_out, xx, tgt, mrow, b_ada)


def _conv_bwd(dconv, p, conv_w, conv_b, dp8, after=None):
    def body(d_ref, u_ref, bg_ref, cg_ref, zc_ref, w_ref, b_ref, dp_in_ref, after_ref, dp_ref, gw_ref, gb_ref):
        du_ref, dbg_ref, dcg_ref, dzc_ref = dp_ref.at[0], dp_ref.at[1], dp_ref.at[2], dp_ref.at[3]
        dconv = d_ref[...]
        u, bg, cg, zc = u_ref[...], bg_ref[...], cg_ref[...], zc_ref[...]
        w0, w1, w2 = w_ref[0:1, :], w_ref[1:2, :], w_ref[2:3, :]
        cu = cg * u
        cu_m, cu_p = _shift_rows(cu, True), _shift_rows(cu, False)
        cv = b_ref[...] + cu_m * w0
        cv = cv + cu * w1
        cv = cv + cu_p * w2
        sz = _silu(zc)
        dbg_ref[...] = ((dconv * sz) * cv).astype(bf16)
        dzc_ref[...] = ((dconv * (bg * cv)) * _dsilu(zc)).astype(bf16)
        dcv = (dconv * sz) * bg
        gb_ref[...] = jnp.sum(dcv, axis=0, keepdims=True)
        gw_ref[0:1, :] = jnp.sum(dcv * cu_m, axis=0, keepdims=True)
        gw_ref[1:2, :] = jnp.sum(dcv * cu, axis=0, keepdims=True)
        gw_ref[2:3, :] = jnp.sum(dcv * cu_p, axis=0, keepdims=True)
        gw_ref[3:8, :] = jnp.zeros((5, 128), f32)
        dcu = _shift_rows(dcv, False) * w0 + dcv * w1 + _shift_rows(dcv, True) * w2
        dcg_ref[...] = (dcu * u).astype(bf16)
        du_ref[...] = (dcu * cg).astype(bf16)

    return _hbm_call(
        body, name="conv_bwd", out_shape=(SDS((8, S, DC), bf16), SDS((8, DC), f32), SDS((1, DC), f32)),
        grid=(DC // 128,),
        in_specs=[pl.BlockSpec((S, 128), lambda i: (0, i))] + _conv_specs() + [ANY_SPEC, ANY_SPEC],
        out_specs=(pl.BlockSpec((4, S, 128), lambda i: (DP_U // 4, 0, i)), pl.BlockSpec((8, 128), lambda i: (0, i)),
                   pl.BlockSpec((1, 128), lambda i: (0, i))),
        input_output_aliases={7: 0}, compiler_params=_cp(48),
    )(dconv, p, p, p, p, conv_w, conv_b, dp8, conv_b if after is None else after)


def _attn_bwd(qr, qp, kr, vh, kc, vc, btt, do, after=None):
    def body(qr_ref, qp_ref, kr_ref, v_ref, kc_ref, vc_ref, bt_ref, do_ref, after_ref,
             dqr_ref, dqp_ref, dkr_ref, dv_ref, dkc_ref, dvc_ref, dbt_ref):
        kcv, vcv = kc_ref[...], vc_ref[...]
        dkr_ref[...] = jnp.zeros_like(dkr_ref)
        dv_ref[...] = jnp.zeros_like(dv_ref)
        dbt_ref[...] = jnp.zeros_like(dbt_ref)
        ctx_acc = {}

        def products(b):
            qs, ks, t = _block_geom(b)
            dob = do_ref[qs:qs + QB, :]
            s_lat = _tt(kr_ref[ks:ks + KB, :], qr_ref[qs:qs + QB, :]) + bt_ref[t]
            s_ctx = _tt(kcv, qp_ref[qs:qs + QB, :])
            return s_lat, s_ctx, _tt(v_ref[ks:ks + KB, :], dob), _tt(vcv, dob)

        def score_grads(b, x):
            s_lat, s_ctx, dp_lat, dp_ctx = x
            p_lat, p_ctx = _softmax_t(s_lat, s_ctx)
            delta = jnp.sum(p_lat * dp_lat, axis=0, keepdims=True) + jnp.sum(p_ctx * dp_ctx, axis=0, keepdims=True)
            ds_lat = p_lat * (dp_lat - delta)
            ds_ctx = p_ctx * (dp_ctx - delta)
            return ds_lat, ds_lat.astype(bf16), ds_ctx.astype(bf16), p_lat.astype(bf16), p_ctx.astype(bf16)

        def operand_grads(b, y):
            qs, ks, t = _block_geom(b)
            ds_lat, dsb_lat, dsb_ctx, pb_lat, pb_ctx = y
            qrb, qpb, dob = qr_ref[qs:qs + QB, :], qp_ref[qs:qs + QB, :], do_ref[qs:qs + QB, :]
            dbt_ref[t] += ds_lat
            dqr_ref[qs:qs + QB, :] = _tn(dsb_lat, kr_ref[ks:ks + KB, :])
            dqp_ref[qs:qs + QB, :] = _tn(dsb_ctx, kcv)
            dkr_ref[ks:ks + KB, :] += jnp.dot(dsb_lat, qrb, preferred_element_type=f32)
            dv_ref[ks:ks + KB, :] += jnp.dot(pb_lat, dob, preferred_element_type=f32)
            dkc = jnp.dot(dsb_ctx, qpb, preferred_element_type=f32)
            dvc = jnp.dot(pb_ctx, dob, preferred_element_type=f32)
            ctx_acc["k"] = dkc if b == 0 else ctx_acc["k"] + dkc
            ctx_acc["v"] = dvc if b == 0 else ctx_acc["v"] + dvc

        _staged(NQB, (products, score_grads, operand_grads))
        dkc_ref[...] = ctx_acc["k"]
        dvc_ref[...] = ctx_acc["v"]

    sq = pl.BlockSpec((None, S, DH), lambda h: (h, 0, 0))
    sc = pl.BlockSpec((None, L, DH), lambda h: (h, 0, 0))
    sb = pl.BlockSpec((None, NT, KB, QB), lambda h: (h, 0, 0, 0))
    big, ctxs = SDS((H, S, DH), f32), SDS((H, L, DH), f32)
    return _hbm_call(
        body, name="attn_bwd", out_shape=(big, big, big, big, ctxs, ctxs, SDS((H, NT, KB, QB), f32)), grid=(H,),
        in_specs=[sq, sq, sq, sq, sc, sc, sb, sq, ANY_SPEC], out_specs=(sq, sq, sq, sq, sc, sc, sb),
        compiler_params=_cp(56),
    )(qr, qp, kr, vh, kc, vc, btt, do, do if after is None else after)


def _bias_bwd(dbtt, after=None):
    pieces = _tile_pieces()

    def body(d_ref, after_ref, o_ref, scr):
        scr[...] = jnp.zeros_like(scr)
        acc = [None] * N_DR
        for t in range(NT):
            for j in range(2):
                for u in range(KR):
                    dr = pieces[t][j][u]
                    if dr is None:
                        continue
                    piece = d_ref[t, u * GW:(u + 1) * GW, j * GW:(j + 1) * GW]
                    acc[dr] = piece if acc[dr] is None else acc[dr] + piece
        a = lax.broadcasted_iota(i32, (GW, GW), 0)
        b = lax.broadcasted_iota(i32, (GW, GW), 1)
        flip = (a + b == GW - 1).astype(f32)
        for dr in range(N_DR):
            scr[dr * GW:(dr + 1) * GW, 0:GW] = jnp.dot(acc[dr], flip, precision=HIGHEST, preferred_element_type=f32)
        xs = jnp.concatenate([pltpu.roll(scr[dr * GW:(dr + 1) * GW, :], 128 + (WIN_W - 1) - (GW - 1), 1,
                                         stride=1, stride_axis=0) for dr in range(N_DR)], axis=0)
        tot = jnp.sum(xs.reshape(N_DR, GW, 128), axis=1)
        lane = lax.broadcasted_iota(i32, tot.shape, 1)
        o_ref[...] = jnp.where(lane < N_DC, tot, 0.0)

    return _hbm_call(
        body, name="bias_bwd", out_shape=SDS((H, N_DR, 128), f32), grid=(H,),
        in_specs=[pl.BlockSpec((None, NT, KB, QB), lambda h: (h, 0, 0, 0)), ANY_SPEC],
        out_specs=pl.BlockSpec((None, N_DR, 128), lambda h: (h, 0, 0)),
        scratch_shapes=[pltpu.VMEM((N_DR * GW, 128), f32)],
    )(dbtt, dbtt if after is None else after)


def _merge_heads(ref):
    return jnp.concatenate([ref[hh] for hh in range(H)], axis=1)


def _head_norm_bwd(xraw, gain, dy, ones_bd):
    r = lax.rsqrt(_head_sum(xraw * xraw, ones_bd) * (1.0 / DH) + RMS_EPS)
    xh = xraw * r
    gdy = dy * gain
    dx = r * (gdy - xh * (_head_sum(xh * gdy, ones_bd) * (1.0 / DH)))
    return dx, jnp.sum(dy * xh, axis=0, keepdims=True)


def _qk_bwd(dqr, dqp, dkr, dvh, p, gq, gk, rope, dp8):
    tm = 512

    def body(dqr_ref, dqp_ref, dkr_ref, dv_ref, qk_ref, gq_ref, gk_ref, cc_ref, cr_ref, sc_ref, sr_ref, dp_in_ref,
             dp_ref, ggq_ref, ggk_ref):
        dq_ref, dk_ref, dvo_ref = dp_ref.at[DP_Q], dp_ref.at[DP_K], dp_ref.at[DP_V]

        @pl.when(pl.program_id(0) == 0)
        def _():
            ggq_ref[...] = jnp.zeros_like(ggq_ref)
            ggk_ref[...] = jnp.zeros_like(ggk_ref)

        ones_bd = _head_ones()
        cs, sn = _rope_block(cc_ref, cr_ref, tm), _rope_block(sc_ref, sr_ref, tm)
        a = _merge_heads(dqr_ref)
        dyq = ((a * cs - _swap16(a) * sn) + _merge_heads(dqp_ref)) * QK_SCALE
        bk = _merge_heads(dkr_ref)
        dyk = bk * cs - _swap16(bk) * sn
        dq, gq_part = _head_norm_bwd(qk_ref[:, 0:DA], gq_ref[...], dyq, ones_bd)
        dk, gk_part = _head_norm_bwd(qk_ref[:, DA:2 * DA], gk_ref[...], dyk, ones_bd)
        dq_ref[...] = dq.astype(bf16)
        dk_ref[...] = dk.astype(bf16)
        dvo_ref[...] = _merge_heads(dv_ref).astype(bf16)
        ggq_ref[...] += gq_part
        ggk_ref[...] += gk_part

    hspec = pl.BlockSpec((H, tm, DH), lambda i: (0, i, 0))
    fixed = pl.BlockSpec((1, DA), lambda i: (0, 0))
    return _hbm_call(
        body, name="qk_bwd", out_shape=(SDS((8, S, DA), bf16), SDS((1, DA), f32), SDS((1, DA), f32)), grid=(S // tm,),
        in_specs=[hspec, hspec, hspec, hspec, pl.BlockSpec((tm, 2 * DA), lambda i: (i, 0)), fixed, fixed]
        + _rope_specs(tm) + [ANY_SPEC],
        out_specs=(pl.BlockSpec((3, tm, DA), lambda i: (0, i, 0)), fixed, fixed), input_output_aliases={11: 0},
        compiler_params=_cp(40, dimension_semantics=("arbitrary",)),
    )(dqr, dqp, dkr, dvh, p, gq, gk, *rope, dp8)


def _ctx_bwd(dkc, dvc, pc, gk):
    def body(dkc_ref, dvc_ref, p_ref, gk_ref, dk_ref, dv_ref, ggk_ref):
        ones_bd = _head_ones()
        dk, gk_part = _head_norm_bwd(p_ref[:, 0:DA], gk_ref[...], _merge_heads(dkc_ref), ones_bd)
        dk_ref[...] = dk.astype(bf16)
        dv_ref[...] = _merge_heads(dvc_ref).astype(bf16)
        ggk_ref[...] = gk_part

    piece = SDS((L, DA), bf16)
    return _hbm_call(
        body, name="ctx_bwd", out_shape=(piece, piece, SDS((1, DA), f32)), in_specs=[VMEM_SPEC] * 4,
        out_specs=(VMEM_SPEC,) * 3,
    )(dkc, dvc, pc, gk)


def _grad_w_in(h, dp8, hc, dkc_raw, dvc_m, half, name, after=None):
    def body(half_ref, h_ref, p_ref, hc_ref, dk_ref, dv_ref, after_ref, g_ref):
        j = pl.program_id(0)
        hv = h_ref[...]
        g_ref[:, 0:DA] = _tn(hv, p_ref[0])
        g_ref[:, DA:2 * DA] = _tn(hv, p_ref[1])

        @pl.when(j == 0)
        def _():
            g_ref[:, DA:2 * DA] += _tn(hc_ref[...], dk_ref[...])

        @pl.when(j == 1)
        def _():
            g_ref[:, 0:DA] += _tn(hc_ref[...], dv_ref[...])

    fixed = lambda j, s: (0, 0)
    hd = D // 2
    grid_spec = pltpu.PrefetchScalarGridSpec(
        num_scalar_prefetch=1, grid=(4,),
        in_specs=[pl.BlockSpec((S, hd), lambda j, s: (0, s[0])), pl.BlockSpec((2, S, DA), lambda j, s: (j, 0, 0)),
                  pl.BlockSpec((L, hd), lambda j, s: (0, s[0])), pl.BlockSpec((L, DA), fixed),
                  pl.BlockSpec((L, DA), fixed), ANY_SPEC],
        out_specs=pl.BlockSpec((None, hd, D), lambda j, s: (j, 0, 0)))
    return _hbm_call(
        body, name=name, out_shape=SDS((4, hd, D), f32), grid_spec=grid_spec, compiler_params=_cp(40),
    )(half, h, dp8, hc, dkc_raw, dvc_m, half if after is None else after)


def _grad_w_in_pair_sum(h, dp8, hc, dkc_raw, dvc_m, half, sent, landed, ssem, rsem):
    def body(half_ref, h_ref, p_ref, hc_ref, dk_ref, dv_ref, sent_ref, land_ref, ssem_ref, rsem_ref,
             t32_ref, tb_ref, buf, lsem):
        j = pl.program_id(0)
        hv = h_ref[...]
        t32_ref[:, 0:DA] = _tn(hv, p_ref[0])
        x, y, c = _my_pos()
        arrival = pltpu.make_async_remote_copy(src_ref=sent_ref.at[j], dst_ref=land_ref.at[j], send_sem=ssem_ref.at[j],
                                               recv_sem=rsem_ref.at[j], device_id=(x, y, 1 - c), device_id_type=MESH)
        arrival.wait_recv()
        arrival.wait_send()
        load = pltpu.make_async_copy(land_ref.at[j], buf, lsem)
        load.start()
        t32_ref[:, DA:2 * DA] = _tn(hv, p_ref[1])

        @pl.when(j == 0)
        def _():
            t32_ref[:, DA:2 * DA] += _tn(hc_ref[...], dk_ref[...])

        @pl.when(j == 1)
        def _():
            t32_ref[:, 0:DA] += _tn(hc_ref[...], dv_ref[...])

        load.wait()
        t = t32_ref[...] + buf[...]
        t32_ref[...] = t
        tb_ref[...] = t.astype(bf16)

    fixed = lambda j, s: (0, 0)
    hd = D // 2
    out_spec = pl.BlockSpec((None, hd, D), lambda j, s: (j, 0, 0))
    grid_spec = pltpu.PrefetchScalarGridSpec(
        num_scalar_prefetch=1, grid=(4,),
        in_specs=[pl.BlockSpec((S, hd), lambda j, s: (0, s[0])), pl.BlockSpec((2, S, DA), lambda j, s: (j, 0, 0)),
                  pl.BlockSpec((L, hd), lambda j, s: (0, s[0])), pl.BlockSpec((L, DA), fixed),
                  pl.BlockSpec((L, DA), fixed), HBM_SPEC, HBM_SPEC, SEM_SPEC, SEM_SPEC],
        out_specs=(out_spec, out_spec), scratch_shapes=[pltpu.VMEM((hd, D), f32), pltpu.SemaphoreType.DMA])
    return _hbm_call(
        body, name="grad_w_in_pair_sum", out_shape=(SDS((4, hd, D), f32), SDS((4, hd, D), bf16)), grid_spec=grid_spec,
        compiler_params=_cp(40, has_side_effects=DATAFLOW),
    )(half, h, dp8, hc, dkc_raw, dvc_m, sent, landed, ssem, rsem)


def _norm_mod_bwd(x, dh, g, scale):
    r = lax.rsqrt(jnp.mean(x * x, axis=-1, keepdims=True) + RMS_EPS)
    xh = x * r
    y = xh * g
    dshift = jnp.sum(dh, axis=0, keepdims=True)
    dscale = jnp.sum(dh * y, axis=0, keepdims=True)
    dyn = dh * (1.0 + scale)
    dg = jnp.sum(dyn * xh, axis=0, keepdims=True)
    gdy = dyn * g
    dx = r * (gdy - xh * jnp.mean(xh * gdy, axis=-1, keepdims=True))
    return dx, dshift, dscale, dg


def _dh_grad_x(dp8, w4, xx, dy, norm_g, mrow, b_ada, after=None):
    tm = 512

    def body(p_ref, w_ref, x_ref, dy_ref, g_ref, m_ref, b_ref, after_ref, gx_ref, dsh_ref, dsc_ref, dg_ref):
        @pl.when(pl.program_id(0) == 0)
        def _():
            dsh_ref[...] = jnp.zeros_like(dsh_ref)
            dsc_ref[...] = jnp.zeros_like(dsc_ref)
            dg_ref[...] = jnp.zeros_like(dg_ref)

        dh = None
        for j in range(4):
            for half in range(2):
                term = _tt(p_ref[2 * j + half], w_ref[j, :, half * DA:(half + 1) * DA])
                dh = term if dh is None else dh + term
        scale = m_ref[:, D:2 * D] + b_ref[:, D:2 * D]
        dx, dshift, dscale, dg = _norm_mod_bwd(x_ref[...], dh, g_ref[...], scale)
        gx_ref[...] = dy_ref[...] + dx
        dsh_ref[...] += dshift
        dsc_ref[...] += dscale
        dg_ref[...] += dg

    row = lambda i: (i, 0)
    fixed = lambda i: (0, 0)
    vec = SDS((1, D), f32)
    return _hbm_call(
        body, name="dh_grad_x", out_shape=(SDS((S, D), f32), vec, vec, vec), grid=(S // tm,),
        in_specs=[pl.BlockSpec((8, tm, DA), lambda i: (0, i, 0)), pl.BlockSpec((4, D, D), lambda i: (0, 0, 0)),
                  pl.BlockSpec((tm, D), row), pl.BlockSpec((tm, D), row), pl.BlockSpec((1, D), fixed),
                  pl.BlockSpec((1, 3 * D), fixed), pl.BlockSpec((1, 3 * D), fixed), ANY_SPEC],
        out_specs=(pl.BlockSpec((tm, D), row), pl.BlockSpec((1, D), fixed), pl.BlockSpec((1, D), fixed),
                   pl.BlockSpec((1, D), fixed)),
        compiler_params=_cp(56, dimension_semantics=("arbitrary",)),
    )(dp8, w4, xx, dy, norm_g, mrow, b_ada, b_ada if after is None else after)


def _dhc_sums(dkc_raw, dvc_m, w4, ctx2, norm_g, mrow_c, b_ada, after=None):
    def body(dk_ref, dv_ref, w0_ref, w1_ref, x_ref, g_ref, m_ref, b_ref, after_ref, dsh_ref, dsc_ref, dg_ref):
        dh = _tt(dk_ref[...], w0_ref[:, DA:2 * DA]) + _tt(dv_ref[...], w1_ref[:, 0:DA])
        scale = m_ref[:, D:2 * D] + b_ref[:, D:2 * D]
        _, dshift, dscale, dg = _norm_mod_bwd(x_ref[...], dh, g_ref[...], scale)
        dsh_ref[...] = dshift
        dsc_ref[...] = dscale
        dg_ref[...] = dg

    fixed = lambda i: (0, 0)
    vec = SDS((1, D), f32)
    vspec = pl.BlockSpec((1, D), fixed)
    return _hbm_call(
        body, name="dhc_sums", out_shape=(vec, vec, vec), grid=(1,),
        in_specs=[pl.BlockSpec((L, DA), fixed), pl.BlockSpec((L, DA), fixed),
                  pl.BlockSpec((None, D, D), lambda i: (0, 0, 0)), pl.BlockSpec((None, D, D), lambda i: (1, 0, 0)),
                  pl.BlockSpec((L, D), fixed), vspec, pl.BlockSpec((1, 3 * D), fixed), pl.BlockSpec((1, 3 * D), fixed),
                  ANY_SPEC],
        out_specs=(vspec, vspec, vspec), compiler_params=_cp(32),
    )(dkc_raw, dvc_m, w4, w4, ctx2, norm_g, mrow_c, b_ada, b_ada if after is None else after)


def _rope_tables():
    nf = DH // 4
    inv = np.float32(ROPE_THETA) ** (-np.arange(nf, dtype=np.float32) / np.float32(nf))
    ang_c = np.arange(GW, dtype=np.float32)[:, None] * inv
    ang_r = np.arange(ROWS, dtype=np.float32)[:, None] * inv
    zc, zr = np.zeros((GW, 2 * nf), np.float32), np.zeros((ROWS, 2 * nf), np.float32)
    ct_cos = np.tile(np.concatenate([zc, np.cos(ang_c), np.cos(ang_c)], axis=1), (1, H))
    ct_sin = np.tile(np.concatenate([zc, -np.sin(ang_c), np.sin(ang_c)], axis=1), (1, H))
    rt_cos = np.tile(np.concatenate([np.cos(ang_r), np.cos(ang_r), zr], axis=1), (1, H))
    rt_sin = np.tile(np.concatenate([-np.sin(ang_r), np.sin(ang_r), zr], axis=1), (1, H))
    rep8 = lambda t: np.ascontiguousarray(np.broadcast_to(t[:, None, :], (ROWS, 8, DA))).reshape(ROWS * 8, DA)
    return tuple(jnp.asarray(t, f32) for t in (ct_cos, rep8(rt_cos), ct_sin, rep8(rt_sin)))


def _local_step(xx, ctx2, tgt, mrow, mrow_c, b_ada, norm_g, weights, q_norm_g, k_norm_g, rpb2, conv_w_full, conv_b,
                hooks=None):
    hooks = hooks or {}
    gq = jnp.tile(q_norm_g, (1, H))
    gk = jnp.tile(k_norm_g, (1, H))
    rope = _rope_tables()

    h = _prenorm(xx, norm_g, mrow, b_ada, 256, "prenorm_x", after=weights.get("started"))
    jv = weights["jvec"]
    p = _in_proj_own(h, weights["own"], jv)
    btb = _bias_tiles(rpb2, after=p)
    w4, started = weights["near"](btb)
    p = _in_proj_block(h, w4, p, weights["first"], "in_proj_near", after=started)
    w4 = weights["near2"](w4, p)
    p = _in_proj_block(h, w4, p, weights["second"], "in_proj_near2")
    ctx_norm = {}

    def filler(token):
        ctx_norm["hc"] = _prenorm(ctx2, norm_g, mrow_c, b_ada, L, "prenorm_ctx", after=token)
        return ctx_norm["hc"]

    w4, started = weights["far"](w4, p, filler)
    hc = ctx_norm["hc"]
    p = _in_proj_block(h, w4, p, jv ^ 3, "in_proj_far", after=started)
    pc = _ctx_proj(hc, w4)
    qr, qp, kr, vh = _qk_prep(p, gq, gk, rope)
    kc, vc = _ctx_prep(pc, gk)
    o = _attn_fwd(qr, qp, kr, vh, kc, vc, btb)
    started = weights["out_arrived"](o) if "out_arrived" in weights else None
    conv_g = _conv_fwd(p, conv_w_full, conv_b, after=started)
    w_out_full = weights["out"](conv_g)
    dy, dconv, dp8, do, g_w_out, dgate, loss_sum = _out_proj_loss(o, p, conv_g, w_out_full, xx, tgt, mrow, b_ada)
    started = hooks["g_w_out"](g_w_out) if "g_w_out" in hooks else None
    dp8, g_conv_w, g_conv_b = _conv_bwd(dconv, p, conv_w_full, conv_b, dp8, after=started)
    started = hooks["after_conv"](dp8) if "after_conv" in hooks else None
    dqr, dqp, dkr, dvh, dkc, dvc, dbtb = _attn_bwd(qr, qp, kr, vh, kc, vc, btb, do, after=started)
    dp8, g_gq, g_gk = _qk_bwd(dqr, dqp, dkr, dvh, p, gq, gk, rope, dp8)
    dkc_raw, dvc_m, g_gk_c = _ctx_bwd(dkc, dvc, pc, gk)
    first = hooks.get("first_half", jnp.zeros((1,), i32))
    g_first = _grad_w_in(h, dp8, hc, dkc_raw, dvc_m, first, "grad_w_in_first")
    started = hooks["g_w_in_first"](g_first) if "g_w_in_first" in hooks else None
    if "w_in_pair_sum" in hooks:
        g_second = None
        started = hooks["w_in_pair_sum"](functools.partial(_grad_w_in_pair_sum, h, dp8, hc, dkc_raw, dvc_m, 1 - first))
    else:
        g_second = _grad_w_in(h, dp8, hc, dkc_raw, dvc_m, 1 - first, "grad_w_in_second", after=started)
    dshift_c, dscale_c, dng_c = _dhc_sums(dkc_raw, dvc_m, w4, ctx2, norm_g, mrow_c, b_ada, after=started)
    g_rpb = _bias_bwd(dbtb, after=dshift_c)
    grad_x, dshift, dscale, dng = _dh_grad_x(dp8, w4, xx, dy, norm_g, mrow, b_ada, after=g_rpb)
    return dict(loss_sum=loss_sum, grad_x=grad_x, g_w_in=(g_first, g_second), g_w_out=g_w_out, g_conv_w=g_conv_w,
                g_conv_b=g_conv_b, g_rpb=g_rpb, g_gq=g_gq, g_gk=g_gk, g_gk_c=g_gk_c, dshift=dshift, dscale=dscale,
                dgate=dgate, dng=dng, dshift_c=dshift_c, dscale_c=dscale_c, dng_c=dng_c)


def _pair_sum_w_out(g, r, cvec):
    hr = D // 8

    def body(c_ref, g0, g1, g2, g3, r_ref, t32_ref, tb_ref):
        for q, g_ref in enumerate((g0, g1, g2, g3)):
            t = g_ref[...] + r_ref[q]
            t32_ref[q] = t
            tb_ref[q] = t.astype(bf16)

    gspecs = [pl.BlockSpec((hr, D), lambda i, c, q=q: (2 * q + c[0], 0)) for q in range(4)]
    full = pl.BlockSpec((4, hr, D), lambda i, c: (0, 0, 0))
    grid_spec = pltpu.PrefetchScalarGridSpec(num_scalar_prefetch=1, grid=(1,), in_specs=gspecs + [full],
                                             out_specs=(full, full))
    return _hbm_call(body, name="pair_sum_w_out", out_shape=(SDS((4, hr, D), f32), SDS((4, hr, D), bf16)),
                          grid_spec=grid_spec)(cvec, g, g, g, g, r)


def _chip_sum(t32, r2, jvec, name):
    rows = t32.shape[1]
    tr = min(rows, 128)

    def body(j_ref, t_ref, r_ref, u_ref):
        u_ref[...] = ((t_ref[...] + r_ref[0].astype(f32)) + r_ref[1].astype(f32)) + r_ref[2].astype(f32)

    grid_spec = pltpu.PrefetchScalarGridSpec(
        num_scalar_prefetch=1, grid=(rows // tr,),
        in_specs=[pl.BlockSpec((None, tr, D), lambda i, j: (j[0], i, 0)),
                  pl.BlockSpec((3, tr, D), lambda i, j: (0, i, 0))],
        out_specs=pl.BlockSpec((tr, D), lambda i, j: (i, 0)))
    return _hbm_call(body, name=name, out_shape=SDS((rows, D), f32), grid_spec=grid_spec)(jvec, t32, r2)


_PK = {}
_off = 0
for _name, _rows in (("dm", 24), ("dmc", 24), ("dng", 8), ("dng_c", 8), ("gq", 8), ("gk", 8), ("gk_c", 8),
                     ("rpb", H * N_DR), ("conv_b", 8), ("conv_w", 16), ("loss", 8)):
    _PK[_name] = (_off, _off + _rows)
    _off += _rows
PK_ROWS = _off
RS_B_ADA, RS_NORM_G, RS_GQ, RS_GK, RS_RPB, RS_CONV_B, RS_CONV_W, RS_DMC, RS_LOSS, RS_ROWS = (
    0, 24, 32, 40, 48, 168, 176, 192, 216, 224)


def _small_reduce(gathered):
    def body(g_ref, o_ref, dm_ref):
        a0 = _PK["dm"][0]
        dm_ref[...] = jnp.zeros_like(dm_ref)
        for b in range(8):
            for i in range(24):
                dm_ref[b:b + 1, 128 * i:128 * (i + 1)] = g_ref[b, a0 + i:a0 + i + 1, :]
        tot = g_ref[0]
        for b in range(1, 8):
            tot = tot + g_ref[b]

        def rows(name):
            a, z = _PK[name]
            return tot[a:z]

        o_ref[RS_B_ADA:RS_B_ADA + 24] = rows("dm") + rows("dmc")
        o_ref[RS_NORM_G:RS_NORM_G + 8] = rows("dng") + rows("dng_c")
        gq = jnp.broadcast_to(jnp.sum(rows("gq"), axis=0, keepdims=True), (8, 128))
        gk = jnp.broadcast_to(jnp.sum(rows("gk") + rows("gk_c"), axis=0, keepdims=True), (8, 128))
        o_ref[RS_GQ:RS_GQ + 8] = gq + pltpu.roll(gq, DH, 1)
        o_ref[RS_GK:RS_GK + 8] = gk + pltpu.roll(gk, DH, 1)
        o_ref[RS_RPB:RS_RPB + H * N_DR] = rows("rpb")
        o_ref[RS_CONV_B:RS_CONV_B + 8] = rows("conv_b")
        o_ref[RS_CONV_W:RS_CONV_W + 16] = rows("conv_w")
        dmc = rows("dmc")
        o_ref[RS_DMC:RS_DMC + 24] = dmc
        o_ref[RS_LOSS:RS_LOSS + 8] = rows("loss")
        for i in range(24):
            dm_ref[8:9, 128 * i:128 * (i + 1)] = dmc[i:i + 1]

    return _hbm_call(body, name="small_reduce", out_shape=(SDS((RS_ROWS, 128), f32), SDS((16, 3 * D), f32)),
                     in_specs=[VMEM_SPEC], out_specs=(VMEM_SPEC, VMEM_SPEC))(gathered)


def _w_ada_grad(sc16, dm16, w_ada_shard, jvec):
    ncol = w_ada_shard.shape[1]

    def body(j_ref, sc_ref, dm_ref, w_ref, g_ref, part_ref):
        dm = dm_ref[...]
        g_ref[...] = lax.dot_general(sc_ref[...], dm, (((0,), (0,)), ((), ())), precision=HIGHEST,
                                     preferred_element_type=f32)
        part_ref[...] = lax.dot_general(dm[8:16], w_ref[...], (((1,), (1,)), ((), ())), precision=HIGHEST,
                                        preferred_element_type=f32)

    fixed = lambda i, j: (0, 0)
    grid_spec = pltpu.PrefetchScalarGridSpec(
        num_scalar_prefetch=1, grid=(1,),
        in_specs=[pl.BlockSpec((16, D), fixed), pl.BlockSpec((16, ncol), lambda i, j: (0, j[0])),
                  pl.BlockSpec((D, ncol), fixed)],
        out_specs=(pl.BlockSpec((D, ncol), fixed), pl.BlockSpec((8, D), fixed)))
    return _pallas_call(body, name="w_ada_grad", out_shape=(SDS((D, ncol), f32), SDS((8, D), f32)),
                        grid_spec=grid_spec, compiler_params=_cp(40))(jvec, sc16, dm16, w_ada_shard)


def _c_ctx_grad(parts4, c_ctx):
    def body(p_ref, c_ref, o_ref):
        tot = ((p_ref[0] + p_ref[1]) + p_ref[2]) + p_ref[3]
        o_ref[...] = tot[0:1] * _dsilu(c_ref[...].reshape(1, D))

    return _pallas_call(body, name="c_ctx_grad", out_shape=SDS((1, D), f32), in_specs=[VMEM_SPEC, VMEM_SPEC],
                        out_specs=VMEM_SPEC)(parts4, c_ctx)


def _adamw(w, g, m, v, name, after=None):
    rows, cols = w.shape
    tr = 256 if rows % 256 == 0 else rows

    def body(w_ref, g_ref, m_ref, v_ref, after_ref, d_ref, m2_ref, v2_ref):
        gv = g_ref[...]
        m2 = ADAM_B1 * m_ref[...] + (1.0 - ADAM_B1) * gv
        v2 = ADAM_B2 * v_ref[...] + (1.0 - ADAM_B2) * jnp.square(gv)
        m_hat = m2 / (1.0 - ADAM_B1 ** ADAM_STEP)
        v_hat = v2 / (1.0 - ADAM_B2 ** ADAM_STEP)
        d_ref[...] = -ADAM_LR * (m_hat / (jnp.sqrt(v_hat) + ADAM_EPS) + ADAM_WD * w_ref[...])
        m2_ref[...] = m2
        v2_ref[...] = v2

    spec = pl.BlockSpec((tr, cols), lambda i: (i, 0))
    shp = SDS((rows, cols), f32)
    return _hbm_call(body, name=name, out_shape=(shp, shp, shp), grid=(rows // tr,), in_specs=[spec] * 4 + [ANY_SPEC],
                     out_specs=(spec, spec, spec))(w, g, m, v, g if after is None else after)


def _adamw_halves(w, g_mine, g_other, m, v, cvec, name, after=None):
    rows, cols = w.shape
    half = rows // 2
    chunk = min(128, half // 2)
    per_half = half // chunk

    def body(c_ref, w_ref, ga_ref, gb_ref, m_ref, v_ref, after_ref, g_ref, d_ref, m2_ref, v2_ref,
             bw, bg, bm, bv, bd, sem_in, sem_out):
        c = c_ref[0]
        loads, stores = [], []
        for p in range(2):
            for kk in range(per_half):
                r = pl.ds(pl.multiple_of((c if p == 0 else 1 - c) * half + kk * chunk, chunk), chunk)
                g_src = (ga_ref if p == 0 else gb_ref).at[kk * chunk:(kk + 1) * chunk]
                cps = [pltpu.make_async_copy(src, dst.at[r], sem_in.at[a, p, kk])
                       for a, (src, dst) in enumerate(((w_ref.at[r], bw), (g_src, bg), (m_ref.at[r], bm),
                                                       (v_ref.at[r], bv)))]
                for cp in cps:
                    cp.start()
                loads.append((r, p, kk, cps))
        for r, p, kk, cps in loads:
            for cp in cps:
                cp.wait()
            bd[r], bm[r], bv[r] = _adam_math(bw[r], bg[r], bm[r], bv[r])
            for a, (src, dst) in enumerate(((bg, g_ref), (bd, d_ref), (bm, m2_ref), (bv, v2_ref))):
                cp = pltpu.make_async_copy(src.at[r], dst.at[r], sem_out.at[a, p, kk])
                cp.start()
                stores.append(cp)
        for cp in stores:
            cp.wait()

    shp = pltpu.HBM((rows, cols), f32)
    buf = pltpu.VMEM((rows, cols), f32)
    args = [pltpu.with_memory_space_constraint(a, pltpu.HBM) for a in (w, g_mine, g_other, m, v)]
    return _pallas_call(
        body, name=name, out_shape=(shp,) * 4, in_specs=[SMEM_SPEC] + [HBM_SPEC] * 5 + [ANY_SPEC],
        out_specs=(HBM_SPEC,) * 4,
        scratch_shapes=[buf] * 5 + [pltpu.SemaphoreType.DMA((4, 2, per_half)), pltpu.SemaphoreType.DMA((4, 2, per_half))],
        compiler_params=_cp(5 * rows * cols * 4 // (1 << 20) + 8),
    )(cvec, *args, cvec if after is None else after)


def _adam_math(w, g, m, v):
    m2 = ADAM_B1 * m + (1.0 - ADAM_B1) * g
    v2 = ADAM_B2 * v + (1.0 - ADAM_B2) * jnp.square(g)
    m_hat = m2 / (1.0 - ADAM_B1 ** ADAM_STEP)
    v_hat = v2 / (1.0 - ADAM_B2 ** ADAM_STEP)
    return -ADAM_LR * (m_hat / (jnp.sqrt(v_hat) + ADAM_EPS) + ADAM_WD * w), m2, v2


def _adamw_small(red, g_c_ctx, jvec, ws, ms, vs):
    n = len(ws)

    def body(*refs):
        red_ref, gc_ref, j_ref = refs[:3]
        w_refs, m_refs, v_refs = refs[3:3 + n], refs[3 + n:3 + 2 * n], refs[3 + 2 * n:3 + 3 * n]
        outs = refs[3 + 3 * n:]
        g_out, d_out, m_out, v_out = outs[:n], outs[n:2 * n], outs[2 * n:3 * n], outs[3 * n:]
        chip = j_ref[0]
        lanes = lambda i: (slice(None), slice(128 * i, 128 * (i + 1)))
        row = lambda r0, i: (lambda: red_ref[r0 + i:r0 + i + 1, :])
        whole = (slice(None), slice(None))
        chunks = [
            [((slice(None),), lambda: gc_ref[...].reshape(D))],
            [(lanes(i), row(RS_B_ADA, i)) for i in range(3 * D // 128)],
            [(lanes(i), row(RS_NORM_G, i)) for i in range(D // 128)],
            [(whole, lambda: red_ref[RS_GQ:RS_GQ + 1, 0:DH])],
            [(whole, lambda: red_ref[RS_GK:RS_GK + 1, 0:DH])],
            [((dr,), (lambda dr=dr: red_ref[pl.ds(RS_RPB + dr, H, stride=N_DR), 0:N_DC])) for dr in range(N_DR)],
            [((r,), (lambda r=r: red_ref[pl.ds(RS_CONV_W + 4 * r + chip, 1), :])) for r in range(3)],
            [(lanes(i), row(RS_CONV_B, i)) for i in range(DC // 128)],
        ]
        for a in range(n):
            for idx, grad in chunks[a]:
                g = grad()
                d, m2, v2 = _adam_math(w_refs[a][idx], g, m_refs[a][idx], v_refs[a][idx])
                g_out[a][idx] = g
                d_out[a][idx] = d
                m_out[a][idx] = m2
                v_out[a][idx] = v2

    shapes = [SDS(w.shape, f32) for w in ws]
    res = _pallas_call(body, name="adamw_small", out_shape=shapes * 4,
                       in_specs=[VMEM_SPEC, VMEM_SPEC, SMEM_SPEC] + [VMEM_SPEC] * (3 * n),
                       out_specs=[VMEM_SPEC] * (4 * n))(red, g_c_ctx, jvec, *ws, *ms, *vs)
    return [list(res[k * n:(k + 1) * n]) for k in range(4)]


def _rows128(a):
    return a.reshape(-1, 128)


def kernel(x, c, ctx, c_ctx, w_ada, b_ada, norm_g, w_in, q_norm_g, k_norm_g, rpb, conv_w, conv_b, w_out, loss_target, m_c_ctx, m_w_ada, m_b_ada, m_norm_g, m_w_in, m_q_norm_g, m_k_norm_g, m_rpb, m_conv_w, m_conv_b, m_w_out, v_c_ctx, v_w_ada, v_b_ada, v_norm_g, v_w_in, v_q_norm_g, v_k_norm_g, v_rpb, v_conv_w, v_conv_b, v_w_out):
    xi, yi, ci = lax.axis_index("x"), lax.axis_index("y"), lax.axis_index("c")
    dev = 4 * xi + 2 * yi + ci
    chip = 2 * xi + yi
    cvec = jnp.reshape(ci, (1,)).astype(i32)
    jvec = jnp.reshape(chip, (1,)).astype(i32)
    w_ada_s = w_ada[0]
    ncol = w_ada_s.shape[1]

    gc = _split_start([_to_slot(c.reshape(8, 128), 8, dev)], [], 7, _gather8_copies, "gather_c_start")
    wo4c = _cast_to_slot(w_out[0], jvec, "cast_w_out", after=gc[3])
    w4c = _cast_to_slot(w_in[0], jvec, "cast_w_in", after=wo4c)
    (c8,), _ = _split_wait(gc[0], gc[1], [gc[2]], [], w4c, _gather8_copies, "gather_c_wait")
    cc = jnp.concatenate([c8.reshape(8, D), c_ctx.reshape(1, D), jnp.zeros((7, D), f32)], axis=0)
    m_shard, sc16 = _adaln_shard(cc, w_ada_s)

    conv_w_pad = jnp.pad(conv_w[0], ((0, 5), (0, 0)))
    gm = _split_start([_to_slot(m_shard, 4, chip), _to_slot(conv_w_pad, 4, chip)], [], 6, _gather4_copies,
                      "gather_mod_start")

    all_k = [(0, 0, 0), (0, 0, 1), (0, 0, 2)]
    sem_a, rem_a, w4s, token = _split_start([w4c], [], 1, _near_copies, "weights_near_start", after=gm[4])
    (m4, cw4), _ = _split_wait(gm[0], gm[1], [gm[2], gm[3]], [], token, _gather4_copies, "gather_mod_wait")
    m_full = jnp.transpose(m4, (1, 0, 2)).reshape(16, 4 * ncol)
    mrow = lax.dynamic_slice(m_full, (dev, 0), (1, 3 * D))
    mrow_c = m_full[8:9]
    conv_w_full = jnp.transpose(cw4[:, 0:3, :], (1, 0, 2)).reshape(3, DC)
    waves = {}

    def near(after):
        (w4w,), _ = _split_wait(sem_a, rem_a, [w4s], [], after, _near_copies, "weights_near_wait")
        sem_b, rem_b, w4b, started = _split_start([w4w], [], 2, _pass_relay_copies, "weights_pass_start")
        waves["pass"] = (sem_b, rem_b)
        return w4b, started

    def near2(w4, after):
        (w4w,), _ = _split_wait(*waves["pass"], [w4], [], after, _pass_copy, "weights_pass_wait")
        return w4w

    def far(w4, after, filler):
        (w4w,), _ = _split_wait(*waves["pass"], [w4], [], after, _relay_copy, "weights_far_wait")
        w4x, sem_c, rem_c, wo4s, started = _forward_then_start(w4w, wo4c, all_k, "weights_far_forward_out_start")
        (w4f,), _ = _split_wait(sem_c, rem_c, [w4x], [], filler(started), _diag_forward_copy,
                                "weights_far_forward_wait")
        waves["out"] = (sem_c, rem_c, wo4s)
        return w4f, started

    def w_out_arrived(after):
        sem_c, rem_c, wo4s = waves["out"]
        (wow,) = _halves_wait(sem_c, rem_c, [wo4s], after, all_k, "weights_out_wait")
        sem_d, rem_d, wof, started = _split_start([wow], [], 3, _forward_copies, "weights_out_forward_start")
        waves["out_forward"] = (sem_d, rem_d, wof)
        return started

    def w_out_gathered(after):
        sem_d, rem_d, wof = waves["out_forward"]
        (wo,), _ = _split_wait(sem_d, rem_d, [wof], [], after, _forward_copies, "weights_out_forward_wait")
        return wo.reshape(D, D)

    weights = dict(own=w_in[0], jvec=jvec, started=token, first=jvec ^ (1 + cvec), second=jvec ^ (2 - cvec),
                   near=near, near2=near2, far=far, out_arrived=w_out_arrived, out=w_out_gathered)

    exchange = _exchange_copies
    pending = {}

    def on_g_w_out(g_w_out):
        out = _split_start([g_w_out], [SDS((4, D // 8, D), f32)], 4, exchange, "grad_out_pair_start")
        pending["ex_out"] = out
        return out[4]

    def after_conv(dp8):
        ssem_o, rsem_o, g_o, land_o, _ = pending["ex_out"]
        (g_o,), (ex_o,) = _split_wait(ssem_o, rsem_o, [g_o], [land_o], dp8, exchange, "grad_out_pair_wait")
        to32, tob = _pair_sum_w_out(g_o, ex_o, cvec)
        out = _split_start([tob], [SDS((3, D // 8, D), bf16)], 3, _scatter_copies, "grad_out_chip_start")
        pending["sc_out"] = (out, to32)
        return out[4]

    def on_g_w_in_first(g_first):
        out = _split_start([g_first], [SDS((4, D // 2, D), f32)], 4, _block_exchange_copies, "grad_pair_start")
        pending["ex"] = (out[0], out[1], [out[2]], [out[3]])
        return out[4]

    def on_w_in_pair_sum(pair_sum):
        ex_ssem, ex_rsem, ex_srcs, ex_lands = pending["ex"]
        t32, tb = pair_sum(ex_srcs[0], ex_lands[0], ex_ssem, ex_rsem)
        out = _split_start([tb], [SDS((3, D // 2, D), bf16)], 3, _scatter_copies, "grad_chip_start")
        pending["sc_in"] = (out, t32)
        return out[4]

    r = _local_step(x[0], ctx[0], loss_target[0], mrow, mrow_c, b_ada, norm_g, weights, q_norm_g, k_norm_g,
                    rpb[0], conv_w_full, conv_b,
                    dict(g_w_out=on_g_w_out, after_conv=after_conv, first_half=1 - cvec,
                         g_w_in_first=on_g_w_in_first, w_in_pair_sum=on_w_in_pair_sum))
    sc_in, t32 = pending["sc_in"]
    _, (r2,) = _split_wait(sc_in[0], sc_in[1], [sc_in[2]], [sc_in[3]], r["dng"], _scatter_copies, "grad_chip_wait")
    u_in = _chip_sum(t32, r2, jvec, "chip_sum_w_in")
    sc_o, to32 = pending["sc_out"]
    _, (ro2,) = _split_wait(sc_o[0], sc_o[1], [sc_o[2]], [sc_o[3]], u_in, _scatter_copies, "grad_out_chip_wait")
    u_out = _chip_sum(to32, ro2, jvec, "chip_sum_w_out")
    swap = _split_start([u_in, u_out], [SDS(u_in.shape, f32), SDS(u_out.shape, f32)], 2, _swap_copies,
                        "grad_pair_swap_start")

    dm = jnp.concatenate([r["dshift"], r["dscale"], r["dgate"]], axis=1)
    dmc = jnp.concatenate([r["dshift_c"], r["dscale_c"], jnp.zeros((1, D), f32)], axis=1)
    pack_parts = [_rows128(dm), _rows128(dmc), _rows128(r["dng"]), _rows128(r["dng_c"]), _rows128(r["g_gq"]),
                  _rows128(r["g_gk"]), _rows128(r["g_gk_c"]), r["g_rpb"].reshape(H * N_DR, 128),
                  _rows128(r["g_conv_b"]), _rows128(r["g_conv_w"][0:3]), jnp.pad(r["loss_sum"], ((0, 0), (0, 127)))]
    pack = jnp.concatenate([jnp.pad(p, ((0, -p.shape[0] % 8), (0, 0))) for p in pack_parts], axis=0)
    assert pack.shape[0] == PK_ROWS
    gs = _split_start([_to_slot(pack, 8, dev)], [], 7, _gather8_copies, "gather_small_start", after=swap[6])
    (u_in, u_out), (o_in, o_out) = _split_wait(swap[0], swap[1], swap[2:4], swap[4:6], gs[3], _swap_copies,
                                               "grad_pair_swap_wait")
    g_w_in_s, d_w_in, nm_w_in, nv_w_in = _adamw_halves(w_in[0], u_in, o_in, m_w_in[0], v_w_in[0], cvec, "adamw_w_in",
                                                       after=gs[3])
    g_w_out_s, d_w_out, nm_w_out, nv_w_out = _adamw_halves(w_out[0], u_out, o_out, m_w_out[0], v_w_out[0], cvec,
                                                           "adamw_w_out", after=nm_w_in)
    (gathered,), _ = _split_wait(gs[0], gs[1], [gs[2]], [], nm_w_out, _gather8_copies, "gather_small_wait")
    red, dm16 = _small_reduce(gathered)
    loss = red[RS_LOSS, 0] * (0.5 / D)

    g_w_ada_s, cpart = _w_ada_grad(sc16, dm16, w_ada_s, jvec)
    gcp = _split_start([_to_slot(cpart, 4, chip)], [], 3, _gather4_copies, "gather_c_ctx_parts_start")
    d_w_ada, nm_w_ada, nv_w_ada = _adamw(w_ada_s, g_w_ada_s, m_w_ada[0], v_w_ada[0], "adamw_w_ada", after=gcp[3])
    (cparts4,), _ = _split_wait(gcp[0], gcp[1], [gcp[2]], [], nm_w_ada, _gather4_copies, "gather_c_ctx_parts_wait")
    g_c_ctx = _c_ctx_grad(cparts4, c_ctx)

    t_rpb = lambda a: jnp.transpose(a, (0, 2, 1, 3)).reshape(N_DR, H, N_DC)
    t_cw = lambda a: jnp.transpose(a, (1, 0, 2))
    small = _adamw_small(
        red, g_c_ctx, jvec,
        [c_ctx, b_ada, norm_g, q_norm_g, k_norm_g, t_rpb(rpb), t_cw(conv_w), conv_b],
        [m_c_ctx, m_b_ada, m_norm_g, m_q_norm_g, m_k_norm_g, t_rpb(m_rpb), t_cw(m_conv_w), m_conv_b],
        [v_c_ctx, v_b_ada, v_norm_g, v_q_norm_g, v_k_norm_g, t_rpb(v_rpb), t_cw(v_conv_w), v_conv_b])
    for kind in small:
        kind[5] = jnp.transpose(kind[5].reshape(1, N_DR, H, N_DC), (0, 2, 1, 3))
        kind[6] = jnp.transpose(kind[6], (1, 0, 2))

    def ordered(kind, big_w_ada, big_w_in, big_w_out):
        s_c_ctx, s_b_ada, s_norm_g, s_q, s_k, s_rpb, s_conv_w, s_conv_b = small[kind]
        return [s_c_ctx, big_w_ada[None], s_b_ada, s_norm_g, big_w_in[None], s_q, s_k, s_rpb, s_conv_w,
                s_conv_b, big_w_out[None]]

    grads = ordered(0, g_w_ada_s, g_w_in_s, g_w_out_s)
    deltas = ordered(1, d_w_ada, d_w_in, d_w_out)
    new_m = ordered(2, nm_w_ada, nm_w_in, nm_w_out)
    new_v = ordered(3, nv_w_ada, nv_w_in, nv_w_out)
    return (loss, r["grad_x"][None], *grads, *deltas, *new_m, *new_v)
```

```python
import functools

import jax
import jax.numpy as jnp
import numpy as np
from jax import lax
from jax.experimental import pallas as pl
from jax.experimental.pallas import tpu as pltpu

f32, bf16, i32 = jnp.float32, jnp.bfloat16, jnp.int32
MESH = pl.DeviceIdType.MESH
HIGHEST = lax.Precision.HIGHEST

D = 1024
S = 2048
L = 256
GW = 64
ROWS = S // GW
H = 8
DH = 64
DA = H * DH
DC = 512
WIN_H, WIN_W = 8, 16
N_DR, N_DC = 2 * WIN_H - 1, 2 * WIN_W - 1
RMS_EPS = 1e-6
ROPE_THETA = 10000.0
QK_SCALE = DH ** -0.5
NEG = -1e30

QB = 128
NQB = S // QB
KR = 9
KB = KR * GW
TILE_GEOM = ((0, 0), (2, 0), (4, 0), (28, 23), (30, 23))
NT = len(TILE_GEOM)

ADAM_LR, ADAM_B1, ADAM_B2, ADAM_EPS, ADAM_WD, ADAM_STEP = 0.001, 0.9, 0.999, 1e-08, 0.01, 10

VMEM_SPEC = pl.BlockSpec(memory_space=pltpu.VMEM)
ANY_SPEC = pl.BlockSpec(memory_space=pl.ANY)
SMEM_SPEC = pl.BlockSpec(memory_space=pltpu.SMEM)
SDS = jax.ShapeDtypeStruct


_pallas_call = pl.pallas_call


def _hbm_call(body, *, out_shape, in_specs=None, out_specs=None, grid_spec=None, **kw):
    n_pre = 0
    if grid_spec is not None:
        ispecs, ospecs, n_pre = grid_spec.in_specs, grid_spec.out_specs, grid_spec.num_scalar_prefetch
        kw["grid_spec"] = grid_spec
    else:
        ispecs, ospecs = in_specs, out_specs
        kw.update(in_specs=in_specs, out_specs=out_specs)

    def blocked(spec):
        return isinstance(spec, pl.BlockSpec) and spec.block_shape is not None

    single = not isinstance(out_shape, (tuple, list))
    shapes = [out_shape] if single else list(out_shape)
    ospec_list = list(ospecs) if isinstance(ospecs, (tuple, list)) else [ospecs]
    shapes = [pltpu.HBM(s.shape, s.dtype) if blocked(sp) else s for s, sp in zip(shapes, ospec_list)]
    call = _pallas_call(body, out_shape=shapes[0] if single else tuple(shapes), **kw)

    def run(*args):
        arrays = [pltpu.with_memory_space_constraint(a, pltpu.HBM) if blocked(sp) else a
                  for a, sp in zip(args[n_pre:], ispecs)]
        return call(*args[:n_pre], *arrays)

    return run


def _cp(vmem_mb=None, **kw):
    if vmem_mb is not None:
        kw["vmem_limit_bytes"] = vmem_mb << 20
    return pltpu.CompilerParams(**kw)


def _silu(z):
    return z * jax.nn.sigmoid(z)


def _dsilu(z):
    sg = jax.nn.sigmoid(z)
    return sg * (1.0 + z * (1.0 - sg))


def _row_start(i):
    return min(max(i - WIN_H // 2, 0), ROWS - WIN_H)


def _my_pos():
    return lax.axis_index("x"), lax.axis_index("y"), lax.axis_index("c")


def _flip(v, bit):
    return 1 - v if bit else v


def _swap_copies(srcs, lands, ssem, rsem):
    x, y, c = _my_pos()
    return [pltpu.make_async_remote_copy(src_ref=srcs[a], dst_ref=lands[a], send_sem=ssem.at[a], recv_sem=rsem.at[a],
                                         device_id=(x, y, 1 - c), device_id_type=MESH) for a in range(len(srcs))]


HBM_SPEC = pl.BlockSpec(memory_space=pltpu.HBM)
SEM_SPEC = pl.BlockSpec(memory_space=pltpu.SEMAPHORE)
DATAFLOW = pltpu.SideEffectType.DATAFLOW_SIDE_EFFECTING


def _peer_chips(x, y, c):
    out = []
    for k in range(1, 4):
        px, py = _flip(x, (k >> 1) & 1), _flip(y, k & 1)
        out.append(((px, py, c), 2 * px + py))
    return out


def _half_copies(srcs, dsts, ssem, rsem, which):
    x, y, c = _my_pos()
    j = 2 * x + y
    peers = _peer_chips(x, y, c)
    pairs = []
    for pos, group, k in which:
        half = srcs[pos].shape[1] // 2
        mine = pl.ds(pl.multiple_of(c * half, 8), half)
        dev, pj = peers[k]
        sem = 3 * group + k
        send = pltpu.make_async_remote_copy(src_ref=srcs[pos].at[j, mine], dst_ref=dsts[pos].at[j, mine],
                                            send_sem=ssem.at[sem], recv_sem=rsem.at[sem], device_id=dev,
                                            device_id_type=MESH)
        arrive = pltpu.make_async_remote_copy(src_ref=srcs[pos].at[j, mine], dst_ref=dsts[pos].at[pj, mine],
                                              send_sem=ssem.at[sem], recv_sem=rsem.at[sem], device_id=dev,
                                              device_id_type=MESH)
        pairs.append((send, arrive))
    return pairs


def _halves_wait(ssem, rsem, bigs, after, which, name):
    nb = len(bigs)

    def body(*refs):
        b_in = refs[:nb]
        ssem_ref, rsem_ref = refs[nb], refs[nb + 1]
        for send, arrive in _half_copies(b_in, b_in, ssem_ref, rsem_ref, which):
            send.wait_send()
            arrive.wait_recv()

    return _hbm_call(
        body, name=name, out_shape=tuple(pltpu.HBM(b.shape, b.dtype) for b in bigs),
        in_specs=[HBM_SPEC] * nb + [SEM_SPEC, SEM_SPEC, ANY_SPEC], out_specs=tuple([HBM_SPEC] * nb),
        input_output_aliases={a: a for a in range(nb)}, compiler_params=_cp(has_side_effects=DATAFLOW),
    )(*bigs, ssem, rsem, after)


FORWARD_SEM = 3


def _diag_forward_copy(srcs, dsts, ssem, rsem):
    x, y, c = _my_pos()
    half = srcs[0].shape[1] // 2
    diag = 3 - (2 * x + y)
    mine = pl.ds(pl.multiple_of(c * half, 8), half)
    other = pl.ds(pl.multiple_of((1 - c) * half, 8), half)
    return [_Copy(srcs[0].at[diag, mine], dsts[0].at[diag, mine], dsts[0].at[diag, other], ssem.at[FORWARD_SEM],
                  rsem.at[FORWARD_SEM], (x, y, 1 - c))]


def _forward_then_start(fwd, big, order, name):
    def body(f_in, b_in, f_out, ssem, rsem, b_out, token):
        _diag_forward_copy([f_in], [f_out], ssem, rsem)[0].start()
        for send, _ in _half_copies([b_in], [b_out], ssem, rsem, order):
            send.start()
        token[...] = jnp.zeros_like(token)

    n_sem = FORWARD_SEM + 1
    out_shape = (pltpu.HBM(fwd.shape, fwd.dtype), pltpu.SemaphoreType.DMA((n_sem,)), pltpu.SemaphoreType.DMA((n_sem,)),
                 pltpu.HBM(big.shape, big.dtype), SDS((8, 128), f32))
    return _hbm_call(
        body, name=name, out_shape=out_shape, in_specs=[HBM_SPEC, HBM_SPEC],
        out_specs=(HBM_SPEC, SEM_SPEC, SEM_SPEC, HBM_SPEC, VMEM_SPEC), input_output_aliases={0: 0, 1: 3},
        compiler_params=_cp(has_side_effects=DATAFLOW),
    )(*[pltpu.with_memory_space_constraint(b, pltpu.HBM) for b in (fwd, big)])


def _cast_to_slot(w, jvec, name, after=None):
    rows, cols = w.shape
    tr = 256

    def body(j_ref, w_ref, after_ref, o_ref):
        o_ref[...] = w_ref[...].astype(bf16)

    grid_spec = pltpu.PrefetchScalarGridSpec(
        num_scalar_prefetch=1, grid=(rows // tr,),
        in_specs=[pl.BlockSpec((tr, cols), lambda i, j: (i, 0)), ANY_SPEC],
        out_specs=pl.BlockSpec((None, tr, cols), lambda i, j: (j[0], i, 0)))
    return _hbm_call(body, name=name, out_shape=SDS((4, rows, cols), bf16),
                     grid_spec=grid_spec)(jvec, w, jvec if after is None else after)


def _exchange_copies(srcs, lands, ssem, rsem):
    x, y, c = _my_pos()
    half = srcs[0].shape[0] // 8
    cps = []
    for jb in range(4):
        src = srcs[0].at[pl.ds(pl.multiple_of((2 * jb + 1 - c) * half, 8), half)]
        cps.append(pltpu.make_async_remote_copy(src_ref=src, dst_ref=lands[0].at[jb], send_sem=ssem.at[jb],
                                                recv_sem=rsem.at[jb], device_id=(x, y, 1 - c), device_id_type=MESH))
    return cps


def _block_exchange_copies(srcs, lands, ssem, rsem):
    x, y, c = _my_pos()
    return [pltpu.make_async_remote_copy(src_ref=srcs[0].at[jb], dst_ref=lands[0].at[jb], send_sem=ssem.at[jb],
                                         recv_sem=rsem.at[jb], device_id=(x, y, 1 - c), device_id_type=MESH)
            for jb in range(4)]


def _scatter_copies(srcs, lands, ssem, rsem):
    x, y, c = _my_pos()
    cps = []
    for a in range(len(srcs)):
        for k, (dev, pj) in enumerate(_peer_chips(x, y, c)):
            cps.append(pltpu.make_async_remote_copy(src_ref=srcs[a].at[pj], dst_ref=lands[a].at[k],
                                                    send_sem=ssem.at[3 * a + k], recv_sem=rsem.at[3 * a + k],
                                                    device_id=dev, device_id_type=MESH))
    return cps


class _Copy:
    def __init__(self, src, dst, arrive, ssem, rsem, dev):
        make = lambda to: pltpu.make_async_remote_copy(src_ref=src, dst_ref=to, send_sem=ssem, recv_sem=rsem,
                                                       device_id=dev, device_id_type=MESH)
        send, arrival = make(dst), make(arrive)
        self.start, self.wait_send, self.wait_recv = send.start, send.wait_send, arrival.wait_recv


def _toward(x, y, along_x):
    return x + along_x * (1 - 2 * x), y + (1 - along_x) * (1 - 2 * y)


def _near_copies(srcs, dsts, ssem, rsem):
    x, y, c = _my_pos()
    px, py = _toward(x, y, c)
    j = 2 * x + y
    return [_Copy(srcs[0].at[j], dsts[0].at[j], dsts[0].at[2 * px + py], ssem.at[0], rsem.at[0], (px, py, c))]


def _pass_copy(srcs, dsts, ssem, rsem):
    x, y, c = _my_pos()
    px, py = _toward(x, y, c)
    qx, qy = _toward(x, y, 1 - c)
    got = 2 * px + py
    return [_Copy(srcs[0].at[got], dsts[0].at[got], dsts[0].at[2 * qx + qy], ssem.at[0], rsem.at[0], (x, y, 1 - c))]


def _relay_copy(srcs, dsts, ssem, rsem):
    x, y, c = _my_pos()
    px, py = _toward(x, y, c)
    qx, qy = _toward(x, y, 1 - c)
    half = srcs[0].shape[1] // 2
    mine = pl.ds(pl.multiple_of(c * half, 8), half)
    got, diag = 2 * px + py, 3 - (2 * x + y)
    return [_Copy(srcs[0].at[got, mine], dsts[0].at[got, mine], dsts[0].at[diag, mine], ssem.at[1], rsem.at[1],
                  (qx, qy, c))]


def _forward_copies(srcs, dsts, ssem, rsem):
    x, y, c = _my_pos()
    half = srcs[0].shape[1] // 2
    mine = pl.ds(pl.multiple_of(c * half, 8), half)
    other = pl.ds(pl.multiple_of((1 - c) * half, 8), half)
    return [_Copy(srcs[0].at[pj, mine], dsts[0].at[pj, mine], dsts[0].at[pj, other], ssem.at[k], rsem.at[k],
                  (x, y, 1 - c)) for k, (_, pj) in enumerate(_peer_chips(x, y, c))]


def _pass_relay_copies(srcs, dsts, ssem, rsem):
    return _pass_copy(srcs, dsts, ssem, rsem) + _relay_copy(srcs, dsts, ssem, rsem)


def _gather8_copies(srcs, dsts, ssem, rsem):
    x, y, c = _my_pos()
    me = 4 * x + 2 * y + c
    cps = []
    for a in range(len(srcs)):
        for k in range(1, 8):
            tgt = (_flip(x, (k >> 2) & 1), _flip(y, (k >> 1) & 1), _flip(c, k & 1))
            cps.append(_Copy(srcs[a].at[me], dsts[a].at[me], dsts[a].at[4 * tgt[0] + 2 * tgt[1] + tgt[2]],
                             ssem.at[7 * a + k - 1], rsem.at[7 * a + k - 1], tgt))
    return cps


def _gather4_copies(srcs, dsts, ssem, rsem):
    x, y, c = _my_pos()
    j = 2 * x + y
    cps = []
    for a in range(len(srcs)):
        for k, (dev, pj) in enumerate(_peer_chips(x, y, c)):
            cps.append(_Copy(srcs[a].at[j], dsts[a].at[j], dsts[a].at[pj], ssem.at[3 * a + k], rsem.at[3 * a + k], dev))
    return cps


def _to_slot(a, n, i):
    return lax.dynamic_update_slice(jnp.zeros((n,) + a.shape, a.dtype), a[None], (i,) + (0,) * a.ndim)


def _split_start(srcs, land_shapes, n_cp, make, name, after=None):
    ns, nl = len(srcs), len(land_shapes)
    n_in = ns + nl + (after is not None)

    def body(*refs):
        s_in = refs[:ns]
        ssem, rsem = refs[n_in], refs[n_in + 1]
        s_out = refs[n_in + 2:n_in + 2 + ns]
        l_out = refs[n_in + 2 + ns:n_in + 2 + ns + nl]
        token = refs[n_in + 2 + ns + nl]
        for cp in make(s_in, l_out if nl else s_out, ssem, rsem):
            cp.start()
        token[...] = jnp.zeros_like(token)

    lands = [pltpu.with_memory_space_constraint(lax.empty(sh.shape, sh.dtype), pltpu.HBM) for sh in land_shapes]
    out_shape = (pltpu.SemaphoreType.DMA((n_cp,)), pltpu.SemaphoreType.DMA((n_cp,)),
                 *[pltpu.HBM(b.shape, b.dtype) for b in srcs], *[pltpu.HBM(b.shape, b.dtype) for b in land_shapes],
                 SDS((8, 128), f32))
    return _hbm_call(
        body, name=name, out_shape=out_shape, in_specs=[HBM_SPEC] * (ns + nl) + [ANY_SPEC] * (after is not None),
        out_specs=(SEM_SPEC, SEM_SPEC, *[HBM_SPEC] * (ns + nl), VMEM_SPEC),
        input_output_aliases={i: 2 + i for i in range(ns + nl)}, compiler_params=_cp(has_side_effects=DATAFLOW),
    )(*[pltpu.with_memory_space_constraint(b, pltpu.HBM) for b in srcs], *lands, *([] if after is None else [after]))


def _split_wait(ssem, rsem, srcs, lands, after, make, name):
    ns, nl = len(srcs), len(lands)

    def body(*refs):
        s_in, l_in = refs[:ns], refs[ns:ns + nl]
        ssem_ref, rsem_ref = refs[ns + nl], refs[ns + nl + 1]
        for cp in make(s_in, l_in if nl else s_in, ssem_ref, rsem_ref):
            cp.wait_send()
            cp.wait_recv()

    outs = _hbm_call(
        body, name=name, out_shape=tuple(pltpu.HBM(b.shape, b.dtype) for b in (*srcs, *lands)),
        in_specs=[HBM_SPEC] * (ns + nl) + [SEM_SPEC, SEM_SPEC, ANY_SPEC], out_specs=tuple([HBM_SPEC] * (ns + nl)),
        input_output_aliases={i: i for i in range(ns + nl)}, compiler_params=_cp(has_side_effects=DATAFLOW),
    )(*srcs, *lands, ssem, rsem, after)
    return list(outs[:ns]), list(outs[ns:])


def _adaln_shard(cc, w_ada_shard):
    def body(c_ref, w_ref, m_ref, sc_ref):
        sc = _silu(c_ref[...])
        sc_ref[...] = sc
        m_ref[...] = jnp.dot(sc, w_ref[...], precision=HIGHEST, preferred_element_type=f32)

    return _hbm_call(
        body, name="adaln_shard", out_shape=(SDS((16, w_ada_shard.shape[1]), f32), SDS((16, D), f32)),
        in_specs=[VMEM_SPEC, VMEM_SPEC], out_specs=(VMEM_SPEC, VMEM_SPEC), compiler_params=_cp(32),
    )(cc, w_ada_shard)


def _prenorm(xx, norm_g, mrow, b_ada, tm, name, after=None):
    n = xx.shape[0]

    def body(x_ref, g_ref, m_ref, b_ref, after_ref, h_ref):
        x = x_ref[...]
        shift = m_ref[:, 0:D] + b_ref[:, 0:D]
        scale = m_ref[:, D:2 * D] + b_ref[:, D:2 * D]
        r = lax.rsqrt(jnp.mean(x * x, axis=-1, keepdims=True) + RMS_EPS)
        y = (x * r) * g_ref[...]
        h_ref[...] = (y * (1.0 + scale) + shift).astype(bf16)

    row = lambda i: (i, 0)
    fixed = lambda i: (0, 0)
    return _hbm_call(
        body, name=name, out_shape=SDS((n, D), bf16), grid=(n // tm,),
        in_specs=[pl.BlockSpec((tm, D), row), pl.BlockSpec((1, D), fixed), pl.BlockSpec((1, 3 * D), fixed),
                  pl.BlockSpec((1, 3 * D), fixed), ANY_SPEC],
        out_specs=pl.BlockSpec((tm, D), row),
    )(xx, norm_g, mrow, b_ada, b_ada if after is None else after)


def _in_proj_own(h, w_own, jvec):
    tm = 512

    def body(j_ref, h_ref, w_ref, p_ref):
        p_ref[...] = jnp.dot(h_ref[...], w_ref[...].astype(bf16), preferred_element_type=f32)

    grid_spec = pltpu.PrefetchScalarGridSpec(
        num_scalar_prefetch=1, grid=(S // tm,),
        in_specs=[pl.BlockSpec((tm, D), lambda i, j: (i, 0)), pl.BlockSpec((D, D), lambda i, j: (0, 0))],
        out_specs=pl.BlockSpec((tm, D), lambda i, j: (i, j[0])))
    return _hbm_call(body, name="in_proj_own", out_shape=SDS((S, 4 * D), f32), grid_spec=grid_spec,
                     compiler_params=_cp(40))(jvec, h, w_own)


def _in_proj_block(h, w4, p, bvec, name, after=None):
    tm = 512

    def body(b_ref, h_ref, w_ref, p_in_ref, after_ref, p_ref):
        p_ref[...] = jnp.dot(h_ref[...], w_ref[...], preferred_element_type=f32)

    grid_spec = pltpu.PrefetchScalarGridSpec(
        num_scalar_prefetch=1, grid=(S // tm,),
        in_specs=[pl.BlockSpec((tm, D), lambda i, b: (i, 0)), pl.BlockSpec((None, D, D), lambda i, b: (b[0], 0, 0)),
                  ANY_SPEC, ANY_SPEC],
        out_specs=pl.BlockSpec((tm, D), lambda i, b: (i, b[0])))
    return _hbm_call(body, name=name, out_shape=SDS((S, 4 * D), f32), grid_spec=grid_spec,
                     input_output_aliases={3: 0})(bvec, h, w4, p, bvec if after is None else after)


def _ctx_proj(hc, w4):
    def body(h_ref, w0_ref, w1_ref, p_ref):
        hv = h_ref[...]
        p_ref[:, 0:DA] = jnp.dot(hv, w0_ref[:, DA:2 * DA], preferred_element_type=f32)
        p_ref[:, DA:2 * DA] = jnp.dot(hv, w1_ref[:, 0:DA], preferred_element_type=f32)

    return _hbm_call(
        body, name="ctx_proj", out_shape=SDS((L, 2 * DA), f32), grid=(1,),
        in_specs=[pl.BlockSpec((L, D), lambda i: (0, 0)), pl.BlockSpec((None, D, D), lambda i: (0, 0, 0)),
                  pl.BlockSpec((None, D, D), lambda i: (1, 0, 0))],
        out_specs=pl.BlockSpec((L, 2 * DA), lambda i: (0, 0)),
    )(hc, w4, w4)


def _head_ones():
    r = lax.broadcasted_iota(i32, (DA, DA), 0) // DH
    c = lax.broadcasted_iota(i32, (DA, DA), 1) // DH
    return (r == c).astype(bf16)


def _head_sum(v, ones_bd):
    hi = v.astype(bf16)
    lo = (v - hi.astype(f32)).astype(bf16)
    return jnp.dot(hi, ones_bd, preferred_element_type=f32) + jnp.dot(lo, ones_bd, preferred_element_type=f32)


def _swap16(v):
    lane = lax.broadcasted_iota(i32, v.shape, 1)
    return jnp.where((lane & 31) < 16, pltpu.roll(v, DA - 16, 1), pltpu.roll(v, 16, 1))


def _rope_block(ct_ref, rt_ref, tm):
    rows = [jnp.tile(rt_ref[8 * j:8 * j + 8, :], (GW // 8, 1)) for j in range(tm // GW)]
    return jnp.tile(ct_ref[...], (tm // GW, 1)) + jnp.concatenate(rows, axis=0)


def _rope_specs(tm):
    col = pl.BlockSpec((GW, DA), lambda i: (0, 0))
    row = pl.BlockSpec((8 * tm // GW, DA), lambda i: (i, 0))
    return [col, row, col, row]


def _qk_prep(p, gq, gk, rope):
    tm = 512

    def body(qk_ref, v_ref, gq_ref, gk_ref, cc_ref, cr_ref, sc_ref, sr_ref, qr_ref, qp_ref, kr_ref, vh_ref):
        ones_bd = _head_ones()
        cs, sn = _rope_block(cc_ref, cr_ref, tm), _rope_block(sc_ref, sr_ref, tm)
        q = qk_ref[:, 0:DA]
        k = qk_ref[:, DA:2 * DA]
        yq = (q * lax.rsqrt(_head_sum(q * q, ones_bd) * (1.0 / DH) + RMS_EPS)) * gq_ref[...]
        yk = (k * lax.rsqrt(_head_sum(k * k, ones_bd) * (1.0 / DH) + RMS_EPS)) * gk_ref[...]
        qr = (yq * cs + _swap16(yq) * sn) * QK_SCALE
        qp = yq * QK_SCALE
        kr = yk * cs + _swap16(yk) * sn
        vv = v_ref[...]
        for hh in range(H):
            sl = slice(hh * DH, (hh + 1) * DH)
            qr_ref[hh] = qr[:, sl].astype(bf16)
            qp_ref[hh] = qp[:, sl].astype(bf16)
            kr_ref[hh] = kr[:, sl].astype(bf16)
            vh_ref[hh] = vv[:, sl].astype(bf16)

    hm = SDS((H, S, DH), bf16)
    hspec = pl.BlockSpec((H, tm, DH), lambda i: (0, i, 0))
    fixed = lambda i: (0, 0)
    return _hbm_call(
        body, name="qk_prep", out_shape=(hm, hm, hm, hm), grid=(S // tm,),
        in_specs=[pl.BlockSpec((tm, 2 * DA), lambda i: (i, 0)), pl.BlockSpec((tm, DA), lambda i: (i, 2)),
                  pl.BlockSpec((1, DA), fixed), pl.BlockSpec((1, DA), fixed)] + _rope_specs(tm),
        out_specs=(hspec, hspec, hspec, hspec),
    )(p, p, gq, gk, *rope)


def _ctx_prep(pc, gk):
    def body(p_ref, gk_ref, kc_ref, vc_ref):
        ones_bd = _head_ones()
        k = p_ref[:, 0:DA]
        yk = (k * lax.rsqrt(_head_sum(k * k, ones_bd) * (1.0 / DH) + RMS_EPS)) * gk_ref[...]
        vv = p_ref[:, DA:2 * DA]
        for hh in range(H):
            sl = slice(hh * DH, (hh + 1) * DH)
            kc_ref[hh] = yk[:, sl].astype(bf16)
            vc_ref[hh] = vv[:, sl].astype(bf16)

    hm = SDS((H, L, DH), bf16)
    return _hbm_call(
        body, name="ctx_prep", out_shape=(hm, hm), in_specs=[VMEM_SPEC, VMEM_SPEC], out_specs=(VMEM_SPEC, VMEM_SPEC),
    )(pc, gk)


def _tile_pieces():
    out = []
    for (i0, u0) in TILE_GEOM:
        rows = []
        for j in range(2):
            i = i0 + j
            rs = _row_start(i)
            rows.append([(u0 + u - i + WIN_H - 1) if rs <= u0 + u < rs + WIN_H else None for u in range(KR)])
        out.append(rows)
    return out


def _bias_prep(rpb_rev_pad, after=None):
    pieces = _tile_pieces()

    def body(r_ref, after_ref, o_ref):
        rp = r_ref[...]
        xs = jnp.concatenate([pltpu.roll(jnp.broadcast_to(rp[dr:dr + 1, :], (GW, 128)), 128 - (WIN_W - 1), 1,
                                         stride=1, stride_axis=0) for dr in range(N_DR)], axis=0)
        row = lax.broadcasted_iota(i32, xs.shape, 0)
        lane = lax.broadcasted_iota(i32, xs.shape, 1)
        k = row & (GW - 1)
        c0 = jnp.clip(lane - WIN_W // 2, 0, GW - WIN_W)
        xs = jnp.where((k >= c0) & (k < c0 + WIN_W), xs, NEG)
        neg = jnp.full((GW, GW), NEG, f32)
        for t in range(NT):
            for j in range(2):
                for u in range(KR):
                    dr = pieces[t][j][u]
                    piece = neg if dr is None else xs[dr * GW:(dr + 1) * GW, 0:GW]
                    o_ref[t, u * GW:(u + 1) * GW, j * GW:(j + 1) * GW] = piece

    return _hbm_call(
        body, name="bias_prep", out_shape=SDS((H, NT, KB, QB), f32), grid=(H,),
        in_specs=[pl.BlockSpec((None, N_DR, 128), lambda h: (h, 0, 0)), ANY_SPEC],
        out_specs=pl.BlockSpec((None, NT, KB, QB), lambda h: (h, 0, 0, 0)),
    )(rpb_rev_pad, rpb_rev_pad if after is None else after)


def _bias_tiles(rpb2, after=None):
    return _bias_prep(jnp.pad(rpb2[:, :, ::-1], ((0, 0), (0, 0), (0, 128 - N_DC))), after)


def _block_geom(b):
    qs = b * QB
    ks = min(max(2 * b - 4, 0), ROWS - KR) * GW
    t = b if b < 2 else (b - (NQB - NT) if b > NQB - 3 else 2)
    return qs, ks, t


def _tt(a, b):
    return lax.dot_general(a, b, (((1,), (1,)), ((), ())), preferred_element_type=f32)


def _tn(a, b):
    return lax.dot_general(a, b, (((0,), (0,)), ((), ())), preferred_element_type=f32)


def _softmax_t(s_lat, s_ctx):
    m = jnp.maximum(jnp.max(s_lat, axis=0, keepdims=True), jnp.max(s_ctx, axis=0, keepdims=True))
    e_lat = jnp.exp(s_lat - m)
    e_ctx = jnp.exp(s_ctx - m)
    inv = 1.0 / (jnp.sum(e_lat, axis=0, keepdims=True) + jnp.sum(e_ctx, axis=0, keepdims=True))
    return e_lat * inv, e_ctx * inv


def _staged(n_blocks, stages):
    held = [dict() for _ in stages]
    for step in range(n_blocks + len(stages) - 1):
        for s, fn in enumerate(stages):
            b = step - s
            if 0 <= b < n_blocks:
                held[s][b] = fn(b) if s == 0 else fn(b, held[s - 1].pop(b))


def _attn_fwd(qr, qp, kr, vh, kc, vc, btt):
    def body(qr_ref, qp_ref, kr_ref, v_ref, kc_ref, vc_ref, bt_ref, o_ref):
        kcv, vcv = kc_ref[...], vc_ref[...]

        def scores(b):
            qs, ks, t = _block_geom(b)
            return (_tt(kr_ref[ks:ks + KB, :], qr_ref[qs:qs + QB, :]) + bt_ref[t], _tt(kcv, qp_ref[qs:qs + QB, :]))

        def probs(b, sc):
            p_lat, p_ctx = _softmax_t(*sc)
            return p_lat.astype(bf16), p_ctx.astype(bf16)

        def values(b, p):
            qs, ks, _ = _block_geom(b)
            o_ref[qs:qs + QB, :] = _tn(p[0], v_ref[ks:ks + KB, :]) + _tn(p[1], vcv)

        _staged(NQB, (scores, probs, values))

    sq = pl.BlockSpec((None, S, DH), lambda h: (h, 0, 0))
    sc = pl.BlockSpec((None, L, DH), lambda h: (h, 0, 0))
    return _hbm_call(
        body, name="attn_fwd", out_shape=SDS((H, S, DH), f32), grid=(H,),
        in_specs=[sq, sq, sq, sq, sc, sc, pl.BlockSpec((None, NT, KB, QB), lambda h: (h, 0, 0, 0))],
        out_specs=sq, compiler_params=_cp(48),
    )(qr, qp, kr, vh, kc, vc, btt)


def _shift_rows(v, down):
    n = v.shape[0]
    row = lax.broadcasted_iota(i32, v.shape, 0)
    if down:
        return jnp.where(row == 0, 0.0, pltpu.roll(v, 1, 0))
    return jnp.where(row == n - 1, 0.0, pltpu.roll(v, n - 1, 0))


def _conv_specs():
    col = lambda off: pl.BlockSpec((S, 128), lambda i, off=off: (0, off + i))
    return [col(16), col(20), col(24), col(28), pl.BlockSpec((3, 128), lambda i: (0, i)),
            pl.BlockSpec((1, 128), lambda i: (0, i))]


def _conv_fwd(p, conv_w, conv_b, after=None):
    def body(u_ref, bg_ref, cg_ref, zc_ref, w_ref, b_ref, after_ref, o_ref):
        cu = cg_ref[...] * u_ref[...]
        cv = b_ref[...] + _shift_rows(cu, True) * w_ref[0:1, :]
        cv = cv + cu * w_ref[1:2, :]
        cv = cv + _shift_rows(cu, False) * w_ref[2:3, :]
        o_ref[...] = ((bg_ref[...] * cv) * _silu(zc_ref[...])).astype(bf16)

    return _hbm_call(
        body, name="conv_fwd", out_shape=SDS((S, DC), bf16), grid=(DC // 128,),
        in_specs=_conv_specs() + [ANY_SPEC], out_specs=pl.BlockSpec((S, 128), lambda i: (0, i)),
        compiler_params=_cp(40),
    )(p, p, p, p, conv_w, conv_b, conv_b if after is None else after)


DP_Q, DP_K, DP_V, DP_ZA, DP_U, DP_BG, DP_CG, DP_ZC = range(8)


def _out_proj_loss(o, p, conv_g, w_out, xx, tgt, mrow, b_ada):
    tm = 512

    def body(o_ref, za_ref, c_ref, w_ref, x_ref, t_ref, m_ref, b_ref,
             dy_ref, dconv_ref, dp_ref, do_ref, gwo_ref, dgate_ref, loss_ref):
        k = pl.program_id(0)

        @pl.when(k == 0)
        def _():
            gwo_ref[...] = jnp.zeros_like(gwo_ref)
            dgate_ref[...] = jnp.zeros_like(dgate_ref)
            loss_ref[0, 0] = 0.0

        gate = m_ref[:, 2 * D:3 * D] + b_ref[:, 2 * D:3 * D]
        za = za_ref[...]
        sz = _silu(za)
        om = _merge_heads(o_ref)
        av, cv = (om * sz).astype(bf16), c_ref[...]
        mo = jnp.dot(av, w_ref[0:DA, :], preferred_element_type=f32)
        mo = mo + jnp.dot(cv, w_ref[DA:DA + DC, :], preferred_element_type=f32)
        y = x_ref[...] + gate * mo
        diff = y - t_ref[...]
        loss_ref[0, 0] += jnp.sum(diff * diff)
        dy = diff * (1.0 / D)
        dy_ref[...] = dy
        dgate_ref[...] += jnp.sum(dy * mo, axis=0, keepdims=True)
        dmo = (dy * gate).astype(bf16)
        dmix = _tt(dmo, w_ref[...])
        dattn = dmix[:, 0:DA]
        dconv_ref[...] = dmix[:, DA:DA + DC]
        a = dattn * sz
        for hh in range(H):
            do_ref[hh] = a[:, hh * DH:(hh + 1) * DH].astype(bf16)
        dp_ref[...] = ((dattn * _dsilu(za)) * om).astype(bf16)
        gwo_ref[0:DA, :] += _tn(av, dmo)
        gwo_ref[DA:DA + DC, :] += _tn(cv, dmo)

    row = lambda i: (i, 0)
    fixed = lambda i: (0, 0)
    hspec = pl.BlockSpec((H, tm, DH), lambda i: (0, i, 0))
    return _hbm_call(
        body, name="out_proj_loss",
        out_shape=(SDS((S, D), f32), SDS((S, DC), f32), SDS((8, S, DA), bf16), SDS((H, S, DH), bf16),
                   SDS((D, D), f32), SDS((1, D), f32), SDS((1, 1), f32)),
        grid=(S // tm,),
        in_specs=[hspec, pl.BlockSpec((tm, DA), lambda i: (i, 3)), pl.BlockSpec((tm, DC), row),
                  pl.BlockSpec((D, D), fixed), pl.BlockSpec((tm, D), row), pl.BlockSpec((tm, D), row),
                  pl.BlockSpec((1, 3 * D), fixed), pl.BlockSpec((1, 3 * D), fixed)],
        out_specs=(pl.BlockSpec((tm, D), row), pl.BlockSpec((tm, DC), row),
                   pl.BlockSpec((None, tm, DA), lambda i: (DP_ZA, i, 0)), hspec, pl.BlockSpec((D, D), fixed),
                   pl.BlockSpec((1, D), fixed), SMEM_SPEC),
        compiler_params=_cp(56, dimension_semantics=("arbitrary",)),
    )(o, p, conv_g, w_out, xx, tgt, mrow, b_ada)


def _conv_bwd(dconv, p, conv_w, conv_b, dp8, after=None):
    def body(d_ref, u_ref, bg_ref, cg_ref, zc_ref, w_ref, b_ref, dp_in_ref, after_ref, dp_ref, gw_ref, gb_ref):
        du_ref, dbg_ref, dcg_ref, dzc_ref = dp_ref.at[0], dp_ref.at[1], dp_ref.at[2], dp_ref.at[3]
        dconv = d_ref[...]
        u, bg, cg, zc = u_ref[...], bg_ref[...], cg_ref[...], zc_ref[...]
        w0, w1, w2 = w_ref[0:1, :], w_ref[1:2, :], w_ref[2:3, :]
        cu = cg * u
        cu_m, cu_p = _shift_rows(cu, True), _shift_rows(cu, False)
        cv = b_ref[...] + cu_m * w0
        cv = cv + cu * w1
        cv = cv + cu_p * w2
        sz = _silu(zc)
        dbg_ref[...] = ((dconv * sz) * cv).astype(bf16)
        dzc_ref[...] = ((dconv * (bg * cv)) * _dsilu(zc)).astype(bf16)
        dcv = (dconv * sz) * bg
        gb_ref[...] = jnp.sum(dcv, axis=0, keepdims=True)
        gw_ref[0:1, :] = jnp.sum(dcv * cu_m, axis=0, keepdims=True)
        gw_ref[1:2, :] = jnp.sum(dcv * cu, axis=0, keepdims=True)
        gw_ref[2:3, :] = jnp.sum(dcv * cu_p, axis=0, keepdims=True)
        gw_ref[3:8, :] = jnp.zeros((5, 128), f32)
        dcu = _shift_rows(dcv, False) * w0 + dcv * w1 + _shift_rows(dcv, True) * w2
        dcg_ref[...] = (dcu * u).astype(bf16)
        du_ref[...] = (dcu * cg).astype(bf16)

    return _hbm_call(
        body, name="conv_bwd", out_shape=(SDS((8, S, DC), bf16), SDS((8, DC), f32), SDS((1, DC), f32)),
        grid=(DC // 128,),
        in_specs=[pl.BlockSpec((S, 128), lambda i: (0, i))] + _conv_specs() + [ANY_SPEC, ANY_SPEC],
        out_specs=(pl.BlockSpec((4, S, 128), lambda i: (DP_U // 4, 0, i)), pl.BlockSpec((8, 128), lambda i: (0, i)),
                   pl.BlockSpec((1, 128), lambda i: (0, i))),
        input_output_aliases={7: 0}, compiler_params=_cp(48),
    )(dconv, p, p, p, p, conv_w, conv_b, dp8, conv_b if after is None else after)


def _attn_bwd(qr, qp, kr, vh, kc, vc, btt, do, after=None):
    def body(qr_ref, qp_ref, kr_ref, v_ref, kc_ref, vc_ref, bt_ref, do_ref, after_ref,
             dqr_ref, dqp_ref, dkr_ref, dv_ref, dkc_ref, dvc_ref, dbt_ref):
        kcv, vcv = kc_ref[...], vc_ref[...]
        dkr_ref[...] = jnp.zeros_like(dkr_ref)
        dv_ref[...] = jnp.zeros_like(dv_ref)
        dbt_ref[...] = jnp.zeros_like(dbt_ref)
        ctx_acc = {}

        def products(b):
            qs, ks, t = _block_geom(b)
            dob = do_ref[qs:qs + QB, :]
            s_lat = _tt(kr_ref[ks:ks + KB, :], qr_ref[qs:qs + QB, :]) + bt_ref[t]
            s_ctx = _tt(kcv, qp_ref[qs:qs + QB, :])
            return s_lat, s_ctx, _tt(v_ref[ks:ks + KB, :], dob), _tt(vcv, dob)

        def score_grads(b, x):
            s_lat, s_ctx, dp_lat, dp_ctx = x
            p_lat, p_ctx = _softmax_t(s_lat, s_ctx)
            delta = jnp.sum(p_lat * dp_lat, axis=0, keepdims=True) + jnp.sum(p_ctx * dp_ctx, axis=0, keepdims=True)
            ds_lat = p_lat * (dp_lat - delta)
            ds_ctx = p_ctx * (dp_ctx - delta)
            return ds_lat, ds_lat.astype(bf16), ds_ctx.astype(bf16), p_lat.astype(bf16), p_ctx.astype(bf16)

        def operand_grads(b, y):
            qs, ks, t = _block_geom(b)
            ds_lat, dsb_lat, dsb_ctx, pb_lat, pb_ctx = y
            qrb, qpb, dob = qr_ref[qs:qs + QB, :], qp_ref[qs:qs + QB, :], do_ref[qs:qs + QB, :]
            dbt_ref[t] += ds_lat
            dqr_ref[qs:qs + QB, :] = _tn(dsb_lat, kr_ref[ks:ks + KB, :])
            dqp_ref[qs:qs + QB, :] = _tn(dsb_ctx, kcv)
            dkr_ref[ks:ks + KB, :] += jnp.dot(dsb_lat, qrb, preferred_element_type=f32)
            dv_ref[ks:ks + KB, :] += jnp.dot(pb_lat, dob, preferred_element_type=f32)
            dkc = jnp.dot(dsb_ctx, qpb, preferred_element_type=f32)
            dvc = jnp.dot(pb_ctx, dob, preferred_element_type=f32)
            ctx_acc["k"] = dkc if b == 0 else ctx_acc["k"] + dkc
            ctx_acc["v"] = dvc if b == 0 else ctx_acc["v"] + dvc

        _staged(NQB, (products, score_grads, operand_grads))
        dkc_ref[...] = ctx_acc["k"]
        dvc_ref[...] = ctx_acc["v"]

    sq = pl.BlockSpec((None, S, DH), lambda h: (h, 0, 0))
    sc = pl.BlockSpec((None, L, DH), lambda h: (h, 0, 0))
    sb = pl.BlockSpec((None, NT, KB, QB), lambda h: (h, 0, 0, 0))
    big, ctxs = SDS((H, S, DH), f32), SDS((H, L, DH), f32)
    return _hbm_call(
        body, name="attn_bwd", out_shape=(big, big, big, big, ctxs, ctxs, SDS((H, NT, KB, QB), f32)), grid=(H,),
        in_specs=[sq, sq, sq, sq, sc, sc, sb, sq, ANY_SPEC], out_specs=(sq, sq, sq, sq, sc, sc, sb),
        compiler_params=_cp(56),
    )(qr, qp, kr, vh, kc, vc, btt, do, do if after is None else after)


def _bias_bwd(dbtt, after=None):
    pieces = _tile_pieces()

    def body(d_ref, after_ref, o_ref, scr):
        scr[...] = jnp.zeros_like(scr)
        acc = [None] * N_DR
        for t in range(NT):
            for j in range(2):
                for u in range(KR):
                    dr = pieces[t][j][u]
                    if dr is None:
                        continue
                    piece = d_ref[t, u * GW:(u + 1) * GW, j * GW:(j + 1) * GW]
                    acc[dr] = piece if acc[dr] is None else acc[dr] + piece
        a = lax.broadcasted_iota(i32, (GW, GW), 0)
        b = lax.broadcasted_iota(i32, (GW, GW), 1)
        flip = (a + b == GW - 1).astype(f32)
        for dr in range(N_DR):
            scr[dr * GW:(dr + 1) * GW, 0:GW] = jnp.dot(acc[dr], flip, precision=HIGHEST, preferred_element_type=f32)
        xs = jnp.concatenate([pltpu.roll(scr[dr * GW:(dr + 1) * GW, :], 128 + (WIN_W - 1) - (GW - 1), 1,
                                         stride=1, stride_axis=0) for dr in range(N_DR)], axis=0)
        tot = jnp.sum(xs.reshape(N_DR, GW, 128), axis=1)
        lane = lax.broadcasted_iota(i32, tot.shape, 1)
        o_ref[...] = jnp.where(lane < N_DC, tot, 0.0)

    return _hbm_call(
        body, name="bias_bwd", out_shape=SDS((H, N_DR, 128), f32), grid=(H,),
        in_specs=[pl.BlockSpec((None, NT, KB, QB), lambda h: (h, 0, 0, 0)), ANY_SPEC],
        out_specs=pl.BlockSpec((None, N_DR, 128), lambda h: (h, 0, 0)),
        scratch_shapes=[pltpu.VMEM((N_DR * GW, 128), f32)],
    )(dbtt, dbtt if after is None else after)


def _merge_heads(ref):
    return jnp.concatenate([ref[hh] for hh in range(H)], axis=1)


def _head_norm_bwd(xraw, gain, dy, ones_bd):
    r = lax.rsqrt(_head_sum(xraw * xraw, ones_bd) * (1.0 / DH) + RMS_EPS)
    xh = xraw * r
    gdy = dy * gain
    dx = r * (gdy - xh * (_head_sum(xh * gdy, ones_bd) * (1.0 / DH)))
    return dx, jnp.sum(dy * xh, axis=0, keepdims=True)


def _qk_bwd(dqr, dqp, dkr, dvh, p, gq, gk, rope, dp8):
    tm = 512

    def body(dqr_ref, dqp_ref, dkr_ref, dv_ref, qk_ref, gq_ref, gk_ref, cc_ref, cr_ref, sc_ref, sr_ref, dp_in_ref,
             dp_ref, ggq_ref, ggk_ref):
        dq_ref, dk_ref, dvo_ref = dp_ref.at[DP_Q], dp_ref.at[DP_K], dp_ref.at[DP_V]

        @pl.when(pl.program_id(0) == 0)
        def _():
            ggq_ref[...] = jnp.zeros_like(ggq_ref)
            ggk_ref[...] = jnp.zeros_like(ggk_ref)

        ones_bd = _head_ones()
        cs, sn = _rope_block(cc_ref, cr_ref, tm), _rope_block(sc_ref, sr_ref, tm)
        a = _merge_heads(dqr_ref)
        dyq = ((a * cs - _swap16(a) * sn) + _merge_heads(dqp_ref)) * QK_SCALE
        bk = _merge_heads(dkr_ref)
        dyk = bk * cs - _swap16(bk) * sn
        dq, gq_part = _head_norm_bwd(qk_ref[:, 0:DA], gq_ref[...], dyq, ones_bd)
        dk, gk_part = _head_norm_bwd(qk_ref[:, DA:2 * DA], gk_ref[...], dyk, ones_bd)
        dq_ref[...] = dq.astype(bf16)
        dk_ref[...] = dk.astype(bf16)
        dvo_ref[...] = _merge_heads(dv_ref).astype(bf16)
        ggq_ref[...] += gq_part
        ggk_ref[...] += gk_part

    hspec = pl.BlockSpec((H, tm, DH), lambda i: (0, i, 0))
    fixed = pl.BlockSpec((1, DA), lambda i: (0, 0))
    return _hbm_call(
        body, name="qk_bwd", out_shape=(SDS((8, S, DA), bf16), SDS((1, DA), f32), SDS((1, DA), f32)), grid=(S // tm,),
        in_specs=[hspec, hspec, hspec, hspec, pl.BlockSpec((tm, 2 * DA), lambda i: (i, 0)), fixed, fixed]
        + _rope_specs(tm) + [ANY_SPEC],
        out_specs=(pl.BlockSpec((3, tm, DA), lambda i: (0, i, 0)), fixed, fixed), input_output_aliases={11: 0},
        compiler_params=_cp(40, dimension_semantics=("arbitrary",)),
    )(dqr, dqp, dkr, dvh, p, gq, gk, *rope, dp8)


def _ctx_bwd(dkc, dvc, pc, gk):
    def body(dkc_ref, dvc_ref, p_ref, gk_ref, dk_ref, dv_ref, ggk_ref):
        ones_bd = _head_ones()
        dk, gk_part = _head_norm_bwd(p_ref[:, 0:DA], gk_ref[...], _merge_heads(dkc_ref), ones_bd)
        dk_ref[...] = dk.astype(bf16)
        dv_ref[...] = _merge_heads(dvc_ref).astype(bf16)
        ggk_ref[...] = gk_part

    piece = SDS((L, DA), bf16)
    return _hbm_call(
        body, name="ctx_bwd", out_shape=(piece, piece, SDS((1, DA), f32)), in_specs=[VMEM_SPEC] * 4,
        out_specs=(VMEM_SPEC,) * 3,
    )(dkc, dvc, pc, gk)


def _grad_w_in(h, dp8, hc, dkc_raw, dvc_m, half, name, after=None):
    def body(half_ref, h_ref, p_ref, hc_ref, dk_ref, dv_ref, after_ref, g_ref):
        j = pl.program_id(0)
        hv = h_ref[...]
        g_ref[:, 0:DA] = _tn(hv, p_ref[0])
        g_ref[:, DA:2 * DA] = _tn(hv, p_ref[1])

        @pl.when(j == 0)
        def _():
            g_ref[:, DA:2 * DA] += _tn(hc_ref[...], dk_ref[...])

        @pl.when(j == 1)
        def _():
            g_ref[:, 0:DA] += _tn(hc_ref[...], dv_ref[...])

    fixed = lambda j, s: (0, 0)
    hd = D // 2
    grid_spec = pltpu.PrefetchScalarGridSpec(
        num_scalar_prefetch=1, grid=(4,),
        in_specs=[pl.BlockSpec((S, hd), lambda j, s: (0, s[0])), pl.BlockSpec((2, S, DA), lambda j, s: (j, 0, 0)),
                  pl.BlockSpec((L, hd), lambda j, s: (0, s[0])), pl.BlockSpec((L, DA), fixed),
                  pl.BlockSpec((L, DA), fixed), ANY_SPEC],
        out_specs=pl.BlockSpec((None, hd, D), lambda j, s: (j, 0, 0)))
    return _hbm_call(
        body, name=name, out_shape=SDS((4, hd, D), f32), grid_spec=grid_spec, compiler_params=_cp(40),
    )(half, h, dp8, hc, dkc_raw, dvc_m, half if after is None else after)


def _grad_w_in_pair_sum(h, dp8, hc, dkc_raw, dvc_m, half, sent, landed, ssem, rsem):
    def body(half_ref, h_ref, p_ref, hc_ref, dk_ref, dv_ref, sent_ref, land_ref, ssem_ref, rsem_ref,
             t32_ref, tb_ref, buf, lsem):
        j = pl.program_id(0)
        hv = h_ref[...]
        t32_ref[:, 0:DA] = _tn(hv, p_ref[0])
        x, y, c = _my_pos()
        arrival = pltpu.make_async_remote_copy(src_ref=sent_ref.at[j], dst_ref=land_ref.at[j], send_sem=ssem_ref.at[j],
                                               recv_sem=rsem_ref.at[j], device_id=(x, y, 1 - c), device_id_type=MESH)
        arrival.wait_recv()
        arrival.wait_send()
        load = pltpu.make_async_copy(land_ref.at[j], buf, lsem)
        load.start()
        t32_ref[:, DA:2 * DA] = _tn(hv, p_ref[1])

        @pl.when(j == 0)
        def _():
            t32_ref[:, DA:2 * DA] += _tn(hc_ref[...], dk_ref[...])

        @pl.when(j == 1)
        def _():
            t32_ref[:, 0:DA] += _tn(hc_ref[...], dv_ref[...])

        load.wait()
        t = t32_ref[...] + buf[...]
        t32_ref[...] = t
        tb_ref[...] = t.astype(bf16)

    fixed = lambda j, s: (0, 0)
    hd = D // 2
    out_spec = pl.BlockSpec((None, hd, D), lambda j, s: (j, 0, 0))
    grid_spec = pltpu.PrefetchScalarGridSpec(
        num_scalar_prefetch=1, grid=(4,),
        in_specs=[pl.BlockSpec((S, hd), lambda j, s: (0, s[0])), pl.BlockSpec((2, S, DA), lambda j, s: (j, 0, 0)),
                  pl.BlockSpec((L, hd), lambda j, s: (0, s[0])), pl.BlockSpec((L, DA), fixed),
                  pl.BlockSpec((L, DA), fixed), HBM_SPEC, HBM_SPEC, SEM_SPEC, SEM_SPEC],
        out_specs=(out_spec, out_spec), scratch_shapes=[pltpu.VMEM((hd, D), f32), pltpu.SemaphoreType.DMA])
    return _hbm_call(
        body, name="grad_w_in_pair_sum", out_shape=(SDS((4, hd, D), f32), SDS((4, hd, D), bf16)), grid_spec=grid_spec,
        compiler_params=_cp(40, has_side_effects=DATAFLOW),
    )(half, h, dp8, hc, dkc_raw, dvc_m, sent, landed, ssem, rsem)


def _norm_mod_bwd(x, dh, g, scale):
    r = lax.rsqrt(jnp.mean(x * x, axis=-1, keepdims=True) + RMS_EPS)
    xh = x * r
    y = xh * g
    dshift = jnp.sum(dh, axis=0, keepdims=True)
    dscale = jnp.sum(dh * y, axis=0, keepdims=True)
    dyn = dh * (1.0 + scale)
    dg = jnp.sum(dyn * xh, axis=0, keepdims=True)
    gdy = dyn * g
    dx = r * (gdy - xh * jnp.mean(xh * gdy, axis=-1, keepdims=True))
    return dx, dshift, dscale, dg


def _dh_grad_x(dp8, w4, xx, dy, norm_g, mrow, b_ada, after=None):
    tm = 512

    def body(p_ref, w_ref, x_ref, dy_ref, g_ref, m_ref, b_ref, after_ref, gx_ref, dsh_ref, dsc_ref, dg_ref):
        @pl.when(pl.program_id(0) == 0)
        def _():
            dsh_ref[...] = jnp.zeros_like(dsh_ref)
            dsc_ref[...] = jnp.zeros_like(dsc_ref)
            dg_ref[...] = jnp.zeros_like(dg_ref)

        dh = None
        for j in range(4):
            for half in range(2):
                term = _tt(p_ref[2 * j + half], w_ref[j, :, half * DA:(half + 1) * DA])
                dh = term if dh is None else dh + term
        scale = m_ref[:, D:2 * D] + b_ref[:, D:2 * D]
        dx, dshift, dscale, dg = _norm_mod_bwd(x_ref[...], dh, g_ref[...], scale)
        gx_ref[...] = dy_ref[...] + dx
        dsh_ref[...] += dshift
        dsc_ref[...] += dscale
        dg_ref[...] += dg

    row = lambda i: (i, 0)
    fixed = lambda i: (0, 0)
    vec = SDS((1, D), f32)
    return _hbm_call(
        body, name="dh_grad_x", out_shape=(SDS((S, D), f32), vec, vec, vec), grid=(S // tm,),
        in_specs=[pl.BlockSpec((8, tm, DA), lambda i: (0, i, 0)), pl.BlockSpec((4, D, D), lambda i: (0, 0, 0)),
                  pl.BlockSpec((tm, D), row), pl.BlockSpec((tm, D), row), pl.BlockSpec((1, D), fixed),
                  pl.BlockSpec((1, 3 * D), fixed), pl.BlockSpec((1, 3 * D), fixed), ANY_SPEC],
        out_specs=(pl.BlockSpec((tm, D), row), pl.BlockSpec((1, D), fixed), pl.BlockSpec((1, D), fixed),
                   pl.BlockSpec((1, D), fixed)),
        compiler_params=_cp(56, dimension_semantics=("arbitrary",)),
    )(dp8, w4, xx, dy, norm_g, mrow, b_ada, b_ada if after is None else after)


def _dhc_sums(dkc_raw, dvc_m, w4, ctx2, norm_g, mrow_c, b_ada, after=None):
    def body(dk_ref, dv_ref, w0_ref, w1_ref, x_ref, g_ref, m_ref, b_ref, after_ref, dsh_ref, dsc_ref, dg_ref):
        dh = _tt(dk_ref[...], w0_ref[:, DA:2 * DA]) + _tt(dv_ref[...], w1_ref[:, 0:DA])
        scale = m_ref[:, D:2 * D] + b_ref[:, D:2 * D]
        _, dshift, dscale, dg = _norm_mod_bwd(x_ref[...], dh, g_ref[...], scale)
        dsh_ref[...] = dshift
        dsc_ref[...] = dscale
        dg_ref[...] = dg

    fixed = lambda i: (0, 0)
    vec = SDS((1, D), f32)
    vspec = pl.BlockSpec((1, D), fixed)
    return _hbm_call(
        body, name="dhc_sums", out_shape=(vec, vec, vec), grid=(1,),
        in_specs=[pl.BlockSpec((L, DA), fixed), pl.BlockSpec((L, DA), fixed),
                  pl.BlockSpec((None, D, D), lambda i: (0, 0, 0)), pl.BlockSpec((None, D, D), lambda i: (1, 0, 0)),
                  pl.BlockSpec((L, D), fixed), vspec, pl.BlockSpec((1, 3 * D), fixed), pl.BlockSpec((1, 3 * D), fixed),
                  ANY_SPEC],
        out_specs=(vspec, vspec, vspec), compiler_params=_cp(32),
    )(dkc_raw, dvc_m, w4, w4, ctx2, norm_g, mrow_c, b_ada, b_ada if after is None else after)


def _rope_tables():
    nf = DH // 4
    inv = np.float32(ROPE_THETA) ** (-np.arange(nf, dtype=np.float32) / np.float32(nf))
    ang_c = np.arange(GW, dtype=np.float32)[:, None] * inv
    ang_r = np.arange(ROWS, dtype=np.float32)[:, None] * inv
    zc, zr = np.zeros((GW, 2 * nf), np.float32), np.zeros((ROWS, 2 * nf), np.float32)
    ct_cos = np.tile(np.concatenate([zc, np.cos(ang_c), np.cos(ang_c)], axis=1), (1, H))
    ct_sin = np.tile(np.concatenate([zc, -np.sin(ang_c), np.sin(ang_c)], axis=1), (1, H))
    rt_cos = np.tile(np.concatenate([np.cos(ang_r), np.cos(ang_r), zr], axis=1), (1, H))
    rt_sin = np.tile(np.concatenate([-np.sin(ang_r), np.sin(ang_r), zr], axis=1), (1, H))
    rep8 = lambda t: np.ascontiguousarray(np.broadcast_to(t[:, None, :], (ROWS, 8, DA))).reshape(ROWS * 8, DA)
    return tuple(jnp.asarray(t, f32) for t in (ct_cos, rep8(rt_cos), ct_sin, rep8(rt_sin)))


def _local_step(xx, ctx2, tgt, mrow, mrow_c, b_ada, norm_g, weights, q_norm_g, k_norm_g, rpb2, conv_w_full, conv_b,
                hooks=None):
    hooks = hooks or {}
    gq = jnp.tile(q_norm_g, (1, H))
    gk = jnp.tile(k_norm_g, (1, H))
    rope = _rope_tables()

    h = _prenorm(xx, norm_g, mrow, b_ada, 256, "prenorm_x", after=weights.get("started"))
    jv = weights["jvec"]
    p = _in_proj_own(h, weights["own"], jv)
    btb = _bias_tiles(rpb2, after=p)
    w4, started = weights["near"](btb)
    p = _in_proj_block(h, w4, p, weights["first"], "in_proj_near", after=started)
    w4 = weights["near2"](w4, p)
    p = _in_proj_block(h, w4, p, weights["second"], "in_proj_near2")
    ctx_norm = {}

    def filler(token):
        ctx_norm["hc"] = _prenorm(ctx2, norm_g, mrow_c, b_ada, L, "prenorm_ctx", after=token)
        return ctx_norm["hc"]

    w4, started = weights["far"](w4, p, filler)
    hc = ctx_norm["hc"]
    p = _in_proj_block(h, w4, p, jv ^ 3, "in_proj_far", after=started)
    pc = _ctx_proj(hc, w4)
    qr, qp, kr, vh = _qk_prep(p, gq, gk, rope)
    kc, vc = _ctx_prep(pc, gk)
    o = _attn_fwd(qr, qp, kr, vh, kc, vc, btb)
    started = weights["out_arrived"](o) if "out_arrived" in weights else None
    conv_g = _conv_fwd(p, conv_w_full, conv_b, after=started)
    w_out_full = weights["out"](conv_g)
    dy, dconv, dp8, do, g_w_out, dgate, loss_sum = _out_proj_loss(o, p, conv_g, w_out_full, xx, tgt, mrow, b_ada)
    started = hooks["g_w_out"](g_w_out) if "g_w_out" in hooks else None
    dp8, g_conv_w, g_conv_b = _conv_bwd(dconv, p, conv_w_full, conv_b, dp8, after=started)
    started = hooks["after_conv"](dp8) if "after_conv" in hooks else None
    dqr, dqp, dkr, dvh, dkc, dvc, dbtb = _attn_bwd(qr, qp, kr, vh, kc, vc, btb, do, after=started)
    dp8, g_gq, g_gk = _qk_bwd(dqr, dqp, dkr, dvh, p, gq, gk, rope, dp8)
    dkc_raw, dvc_m, g_gk_c = _ctx_bwd(dkc, dvc, pc, gk)
    first = hooks.get("first_half", jnp.zeros((1,), i32))
    g_first = _grad_w_in(h, dp8, hc, dkc_raw, dvc_m, first, "grad_w_in_first")
    started = hooks["g_w_in_first"](g_first) if "g_w_in_first" in hooks else None
    if "w_in_pair_sum" in hooks:
        g_second = None
        started = hooks["w_in_pair_sum"](functools.partial(_grad_w_in_pair_sum, h, dp8, hc, dkc_raw, dvc_m, 1 - first))
    else:
        g_second = _grad_w_in(h, dp8, hc, dkc_raw, dvc_m, 1 - first, "grad_w_in_second", after=started)
    dshift_c, dscale_c, dng_c = _dhc_sums(dkc_raw, dvc_m, w4, ctx2, norm_g, mrow_c, b_ada, after=started)
    g_rpb = _bias_bwd(dbtb, after=dshift_c)
    grad_x, dshift, dscale, dng = _dh_grad_x(dp8, w4, xx, dy, norm_g, mrow, b_ada, after=g_rpb)
    return dict(loss_sum=loss_sum, grad_x=grad_x, g_w_in=(g_first, g_second), g_w_out=g_w_out, g_conv_w=g_conv_w,
                g_conv_b=g_conv_b, g_rpb=g_rpb, g_gq=g_gq, g_gk=g_gk, g_gk_c=g_gk_c, dshift=dshift, dscale=dscale,
                dgate=dgate, dng=dng, dshift_c=dshift_c, dscale_c=dscale_c, dng_c=dng_c)


def _pair_sum_w_out(g, r, cvec):
    hr = D // 8

    def body(c_ref, g0, g1, g2, g3, r_ref, t32_ref, tb_ref):
        for q, g_ref in enumerate((g0, g1, g2, g3)):
            t = g_ref[...] + r_ref[q]
            t32_ref[q] = t
            tb_ref[q] = t.astype(bf16)

    gspecs = [pl.BlockSpec((hr, D), lambda i, c, q=q: (2 * q + c[0], 0)) for q in range(4)]
    full = pl.BlockSpec((4, hr, D), lambda i, c: (0, 0, 0))
    grid_spec = pltpu.PrefetchScalarGridSpec(num_scalar_prefetch=1, grid=(1,), in_specs=gspecs + [full],
                                             out_specs=(full, full))
    return _hbm_call(body, name="pair_sum_w_out", out_shape=(SDS((4, hr, D), f32), SDS((4, hr, D), bf16)),
                          grid_spec=grid_spec)(cvec, g, g, g, g, r)


def _chip_sum(t32, r2, jvec, name):
    rows = t32.shape[1]
    tr = min(rows, 128)

    def body(j_ref, t_ref, r_ref, u_ref):
        u_ref[...] = ((t_ref[...] + r_ref[0].astype(f32)) + r_ref[1].astype(f32)) + r_ref[2].astype(f32)

    grid_spec = pltpu.PrefetchScalarGridSpec(
        num_scalar_prefetch=1, grid=(rows // tr,),
        in_specs=[pl.BlockSpec((None, tr, D), lambda i, j: (j[0], i, 0)),
                  pl.BlockSpec((3, tr, D), lambda i, j: (0, i, 0))],
        out_specs=pl.BlockSpec((tr, D), lambda i, j: (i, 0)))
    return _hbm_call(body, name=name, out_shape=SDS((rows, D), f32), grid_spec=grid_spec)(jvec, t32, r2)


_PK = {}
_off = 0
for _name, _rows in (("dm", 24), ("dmc", 24), ("dng", 8), ("dng_c", 8), ("gq", 8), ("gk", 8), ("gk_c", 8),
                     ("rpb", H * N_DR), ("conv_b", 8), ("conv_w", 16), ("loss", 8)):
    _PK[_name] = (_off, _off + _rows)
    _off += _rows
PK_ROWS = _off
RS_B_ADA, RS_NORM_G, RS_GQ, RS_GK, RS_RPB, RS_CONV_B, RS_CONV_W, RS_DMC, RS_LOSS, RS_ROWS = (
    0, 24, 32, 40, 48, 168, 176, 192, 216, 224)


def _small_reduce(gathered):
    def body(g_ref, o_ref, dm_ref):
        a0 = _PK["dm"][0]
        dm_ref[...] = jnp.zeros_like(dm_ref)
        for b in range(8):
            for i in range(24):
                dm_ref[b:b + 1, 128 * i:128 * (i + 1)] = g_ref[b, a0 + i:a0 + i + 1, :]
        tot = g_ref[0]
        for b in range(1, 8):
            tot = tot + g_ref[b]

        def rows(name):
            a, z = _PK[name]
            return tot[a:z]

        o_ref[RS_B_ADA:RS_B_ADA + 24] = rows("dm") + rows("dmc")
        o_ref[RS_NORM_G:RS_NORM_G + 8] = rows("dng") + rows("dng_c")
        gq = jnp.broadcast_to(jnp.sum(rows("gq"), axis=0, keepdims=True), (8, 128))
        gk = jnp.broadcast_to(jnp.sum(rows("gk") + rows("gk_c"), axis=0, keepdims=True), (8, 128))
        o_ref[RS_GQ:RS_GQ + 8] = gq + pltpu.roll(gq, DH, 1)
        o_ref[RS_GK:RS_GK + 8] = gk + pltpu.roll(gk, DH, 1)
        o_ref[RS_RPB:RS_RPB + H * N_DR] = rows("rpb")
        o_ref[RS_CONV_B:RS_CONV_B + 8] = rows("conv_b")
        o_ref[RS_CONV_W:RS_CONV_W + 16] = rows("conv_w")
        dmc = rows("dmc")
        o_ref[RS_DMC:RS_DMC + 24] = dmc
        o_ref[RS_LOSS:RS_LOSS + 8] = rows("loss")
        for i in range(24):
            dm_ref[8:9, 128 * i:128 * (i + 1)] = dmc[i:i + 1]

    return _hbm_call(body, name="small_reduce", out_shape=(SDS((RS_ROWS, 128), f32), SDS((16, 3 * D), f32)),
                     in_specs=[VMEM_SPEC], out_specs=(VMEM_SPEC, VMEM_SPEC))(gathered)


def _w_ada_grad(sc16, dm16, w_ada_shard, jvec):
    ncol = w_ada_shard.shape[1]

    def body(j_ref, sc_ref, dm_ref, w_ref, g_ref, part_ref):
        dm = dm_ref[...]
        g_ref[...] = lax.dot_general(sc_ref[...], dm, (((0,), (0,)), ((), ())), precision=HIGHEST,
                                     preferred_element_type=f32)
        part_ref[...] = lax.dot_general(dm[8:16], w_ref[...], (((1,), (1,)), ((), ())), precision=HIGHEST,
                                        preferred_element_type=f32)

    fixed = lambda i, j: (0, 0)
    grid_spec = pltpu.PrefetchScalarGridSpec(
        num_scalar_prefetch=1, grid=(1,),
        in_specs=[pl.BlockSpec((16, D), fixed), pl.BlockSpec((16, ncol), lambda i, j: (0, j[0])),
                  pl.BlockSpec((D, ncol), fixed)],
        out_specs=(pl.BlockSpec((D, ncol), fixed), pl.BlockSpec((8, D), fixed)))
    return _pallas_call(body, name="w_ada_grad", out_shape=(SDS((D, ncol), f32), SDS((8, D), f32)),
                        grid_spec=grid_spec, compiler_params=_cp(40))(jvec, sc16, dm16, w_ada_shard)


def _c_ctx_grad(parts4, c_ctx):
    def body(p_ref, c_ref, o_ref):
        tot = ((p_ref[0] + p_ref[1]) + p_ref[2]) + p_ref[3]
        o_ref[...] = tot[0:1] * _dsilu(c_ref[...].reshape(1, D))

    return _pallas_call(body, name="c_ctx_grad", out_shape=SDS((1, D), f32), in_specs=[VMEM_SPEC, VMEM_SPEC],
                        out_specs=VMEM_SPEC)(parts4, c_ctx)


def _adamw(w, g, m, v, name, after=None):
    return _adamw_stream(w, [g], m, v, None, name, after)


def _adamw_halves(w, g_mine, g_other, m, v, cvec, name, after=None):
    return _adamw_stream(w, [g_mine, g_other], m, v, cvec, name, after)


def _adamw_stream(w, g_parts, m, v, cvec, name, after):
    rows, cols = w.shape
    parts = len(g_parts)
    span = rows // parts
    chunk = min(128, span // 2)
    per = span // chunk
    outs_g = parts == 2
    pin = rows * cols * 4 > (1 << 20)

    def body(*refs):
        c_ref, w_ref = refs[0], refs[1]
        g_refs = refs[2:2 + parts]
        m_ref, v_ref = refs[2 + parts], refs[3 + parts]
        out_refs = refs[5 + parts:5 + parts + 3 + outs_g]
        bw, bg, bm, bv, bd, sem_in, sem_out = refs[5 + parts + 3 + outs_g:]
        c = c_ref[0]
        loads, stores = [], []
        for p in range(parts):
            first = (c if p == 0 else 1 - c) * span if parts == 2 else 0
            for kk in range(per):
                r = pl.ds(pl.multiple_of(first + kk * chunk, chunk), chunk)
                g_src = g_refs[p].at[kk * chunk:(kk + 1) * chunk]
                cps = [pltpu.make_async_copy(src, dst.at[r], sem_in.at[a, p, kk])
                       for a, (src, dst) in enumerate(((w_ref.at[r], bw), (g_src, bg), (m_ref.at[r], bm),
                                                       (v_ref.at[r], bv)))]
                for cp in cps:
                    cp.start()
                loads.append((r, p, kk, cps))
        for r, p, kk, cps in loads:
            for cp in cps:
                cp.wait()
            bd[r], bm[r], bv[r] = _adam_math(bw[r], bg[r], bm[r], bv[r])
            for a, (src, dst) in enumerate(zip(([bg] if outs_g else []) + [bd, bm, bv], out_refs)):
                cp = pltpu.make_async_copy(src.at[r], dst.at[r], sem_out.at[a, p, kk])
                cp.start()
                stores.append(cp)
        for cp in stores:
            cp.wait()

    shp = pltpu.HBM((rows, cols), f32) if pin else SDS((rows, cols), f32)
    spec = HBM_SPEC if pin else ANY_SPEC
    buf = pltpu.VMEM((rows, cols), f32)
    args = [pltpu.with_memory_space_constraint(a, pltpu.HBM) if pin else a for a in (w, *g_parts, m, v)]
    cvec = jnp.zeros((1,), i32) if cvec is None else cvec
    return _pallas_call(
        body, name=name, out_shape=(shp,) * (3 + outs_g), in_specs=[SMEM_SPEC] + [spec] * (3 + parts) + [ANY_SPEC],
        out_specs=(spec,) * (3 + outs_g),
        scratch_shapes=[buf] * 5 + [pltpu.SemaphoreType.DMA((4, parts, per)), pltpu.SemaphoreType.DMA((4, parts, per))],
        compiler_params=_cp(5 * rows * cols * 4 // (1 << 20) + 8),
    )(cvec, *args, cvec if after is None else after)


def _adam_math(w, g, m, v):
    m2 = ADAM_B1 * m + (1.0 - ADAM_B1) * g
    v2 = ADAM_B2 * v + (1.0 - ADAM_B2) * jnp.square(g)
    m_hat = m2 / (1.0 - ADAM_B1 ** ADAM_STEP)
    v_hat = v2 / (1.0 - ADAM_B2 ** ADAM_STEP)
    return -ADAM_LR * (m_hat / (jnp.sqrt(v_hat) + ADAM_EPS) + ADAM_WD * w), m2, v2


def _adamw_small(red, g_c_ctx, jvec, ws, ms, vs):
    n = len(ws)

    def body(*refs):
        red_ref, gc_ref, j_ref = refs[:3]
        w_refs, m_refs, v_refs = refs[3:3 + n], refs[3 + n:3 + 2 * n], refs[3 + 2 * n:3 + 3 * n]
        outs = refs[3 + 3 * n:]
        g_out, d_out, m_out, v_out = outs[:n], outs[n:2 * n], outs[2 * n:3 * n], outs[3 * n:]
        chip = j_ref[0]
        lanes = lambda i: (slice(None), slice(128 * i, 128 * (i + 1)))
        row = lambda r0, i: (lambda: red_ref[r0 + i:r0 + i + 1, :])
        whole = (slice(None), slice(None))
        chunks = [
            [((slice(None),), lambda: gc_ref[...].reshape(D))],
            [(lanes(i), row(RS_B_ADA, i)) for i in range(3 * D // 128)],
            [(lanes(i), row(RS_NORM_G, i)) for i in range(D // 128)],
            [(whole, lambda: red_ref[RS_GQ:RS_GQ + 1, 0:DH])],
            [(whole, lambda: red_ref[RS_GK:RS_GK + 1, 0:DH])],
            [((dr,), (lambda dr=dr: red_ref[pl.ds(RS_RPB + dr, H, stride=N_DR), 0:N_DC])) for dr in range(N_DR)],
            [((r,), (lambda r=r: red_ref[pl.ds(RS_CONV_W + 4 * r + chip, 1), :])) for r in range(3)],
            [(lanes(i), row(RS_CONV_B, i)) for i in range(DC // 128)],
        ]
        for a in range(n):
            for idx, grad in chunks[a]:
                g = grad()
                d, m2, v2 = _adam_math(w_refs[a][idx], g, m_refs[a][idx], v_refs[a][idx])
                g_out[a][idx] = g
                d_out[a][idx] = d
                m_out[a][idx] = m2
                v_out[a][idx] = v2

    shapes = [SDS(w.shape, f32) for w in ws]
    res = _pallas_call(body, name="adamw_small", out_shape=shapes * 4,
                       in_specs=[VMEM_SPEC, VMEM_SPEC, SMEM_SPEC] + [VMEM_SPEC] * (3 * n),
                       out_specs=[VMEM_SPEC] * (4 * n))(red, g_c_ctx, jvec, *ws, *ms, *vs)
    return [list(res[k * n:(k + 1) * n]) for k in range(4)]


def _rows128(a):
    return a.reshape(-1, 128)


def kernel(x, c, ctx, c_ctx, w_ada, b_ada, norm_g, w_in, q_norm_g, k_norm_g, rpb, conv_w, conv_b, w_out, loss_target, m_c_ctx, m_w_ada, m_b_ada, m_norm_g, m_w_in, m_q_norm_g, m_k_norm_g, m_rpb, m_conv_w, m_conv_b, m_w_out, v_c_ctx, v_w_ada, v_b_ada, v_norm_g, v_w_in, v_q_norm_g, v_k_norm_g, v_rpb, v_conv_w, v_conv_b, v_w_out):
    xi, yi, ci = lax.axis_index("x"), lax.axis_index("y"), lax.axis_index("c")
    dev = 4 * xi + 2 * yi + ci
    chip = 2 * xi + yi
    cvec = jnp.reshape(ci, (1,)).astype(i32)
    jvec = jnp.reshape(chip, (1,)).astype(i32)
    w_ada_s = w_ada[0]
    ncol = w_ada_s.shape[1]

    gc = _split_start([_to_slot(c.reshape(8, 128), 8, dev)], [], 7, _gather8_copies, "gather_c_start")
    wo4c = _cast_to_slot(w_out[0], jvec, "cast_w_out", after=gc[3])
    w4c = _cast_to_slot(w_in[0], jvec, "cast_w_in", after=wo4c)
    (c8,), _ = _split_wait(gc[0], gc[1], [gc[2]], [], w4c, _gather8_copies, "gather_c_wait")
    cc = jnp.concatenate([c8.reshape(8, D), c_ctx.reshape(1, D), jnp.zeros((7, D), f32)], axis=0)
    m_shard, sc16 = _adaln_shard(cc, w_ada_s)

    conv_w_pad = jnp.pad(conv_w[0], ((0, 5), (0, 0)))
    gm = _split_start([_to_slot(m_shard, 4, chip), _to_slot(conv_w_pad, 4, chip)], [], 6, _gather4_copies,
                      "gather_mod_start")

    all_k = [(0, 0, 0), (0, 0, 1), (0, 0, 2)]
    sem_a, rem_a, w4s, token = _split_start([w4c], [], 1, _near_copies, "weights_near_start", after=gm[4])
    (m4, cw4), _ = _split_wait(gm[0], gm[1], [gm[2], gm[3]], [], token, _gather4_copies, "gather_mod_wait")
    m_full = jnp.transpose(m4, (1, 0, 2)).reshape(16, 4 * ncol)
    mrow = lax.dynamic_slice(m_full, (dev, 0), (1, 3 * D))
    mrow_c = m_full[8:9]
    conv_w_full = jnp.transpose(cw4[:, 0:3, :], (1, 0, 2)).reshape(3, DC)
    waves = {}

    def near(after):
        (w4w,), _ = _split_wait(sem_a, rem_a, [w4s], [], after, _near_copies, "weights_near_wait")
        sem_b, rem_b, w4b, started = _split_start([w4w], [], 2, _pass_relay_copies, "weights_pass_start")
        waves["pass"] = (sem_b, rem_b)
        return w4b, started

    def near2(w4, after):
        (w4w,), _ = _split_wait(*waves["pass"], [w4], [], after, _pass_copy, "weights_pass_wait")
        return w4w

    def far(w4, after, filler):
        (w4w,), _ = _split_wait(*waves["pass"], [w4], [], after, _relay_copy, "weights_far_wait")
        w4x, sem_c, rem_c, wo4s, started = _forward_then_start(w4w, wo4c, all_k, "weights_far_forward_out_start")
        (w4f,), _ = _split_wait(sem_c, rem_c, [w4x], [], filler(started), _diag_forward_copy,
                                "weights_far_forward_wait")
        waves["out"] = (sem_c, rem_c, wo4s)
        return w4f, started

    def w_out_arrived(after):
        sem_c, rem_c, wo4s = waves["out"]
        (wow,) = _halves_wait(sem_c, rem_c, [wo4s], after, all_k, "weights_out_wait")
        sem_d, rem_d, wof, started = _split_start([wow], [], 3, _forward_copies, "weights_out_forward_start")
        waves["out_forward"] = (sem_d, rem_d, wof)
        return started

    def w_out_gathered(after):
        sem_d, rem_d, wof = waves["out_forward"]
        (wo,), _ = _split_wait(sem_d, rem_d, [wof], [], after, _forward_copies, "weights_out_forward_wait")
        return wo.reshape(D, D)

    weights = dict(own=w_in[0], jvec=jvec, started=token, first=jvec ^ (1 + cvec), second=jvec ^ (2 - cvec),
                   near=near, near2=near2, far=far, out_arrived=w_out_arrived, out=w_out_gathered)

    exchange = _exchange_copies
    pending = {}

    def on_g_w_out(g_w_out):
        out = _split_start([g_w_out], [SDS((4, D // 8, D), f32)], 4, exchange, "grad_out_pair_start")
        pending["ex_out"] = out
        return out[4]

    def after_conv(dp8):
        ssem_o, rsem_o, g_o, land_o, _ = pending["ex_out"]
        (g_o,), (ex_o,) = _split_wait(ssem_o, rsem_o, [g_o], [land_o], dp8, exchange, "grad_out_pair_wait")
        to32, tob = _pair_sum_w_out(g_o, ex_o, cvec)
        out = _split_start([tob], [SDS((3, D // 8, D), bf16)], 3, _scatter_copies, "grad_out_chip_start")
        pending["sc_out"] = (out, to32)
        return out[4]

    def on_g_w_in_first(g_first):
        out = _split_start([g_first], [SDS((4, D // 2, D), f32)], 4, _block_exchange_copies, "grad_pair_start")
        pending["ex"] = (out[0], out[1], [out[2]], [out[3]])
        return out[4]

    def on_w_in_pair_sum(pair_sum):
        ex_ssem, ex_rsem, ex_srcs, ex_lands = pending["ex"]
        t32, tb = pair_sum(ex_srcs[0], ex_lands[0], ex_ssem, ex_rsem)
        out = _split_start([tb], [SDS((3, D // 2, D), bf16)], 3, _scatter_copies, "grad_chip_start")
        pending["sc_in"] = (out, t32)
        return out[4]

    r = _local_step(x[0], ctx[0], loss_target[0], mrow, mrow_c, b_ada, norm_g, weights, q_norm_g, k_norm_g,
                    rpb[0], conv_w_full, conv_b,
                    dict(g_w_out=on_g_w_out, after_conv=after_conv, first_half=1 - cvec,
                         g_w_in_first=on_g_w_in_first, w_in_pair_sum=on_w_in_pair_sum))
    sc_in, t32 = pending["sc_in"]
    _, (r2,) = _split_wait(sc_in[0], sc_in[1], [sc_in[2]], [sc_in[3]], r["dng"], _scatter_copies, "grad_chip_wait")
    u_in = _chip_sum(t32, r2, jvec, "chip_sum_w_in")
    sc_o, to32 = pending["sc_out"]
    _, (ro2,) = _split_wait(sc_o[0], sc_o[1], [sc_o[2]], [sc_o[3]], u_in, _scatter_copies, "grad_out_chip_wait")
    u_out = _chip_sum(to32, ro2, jvec, "chip_sum_w_out")
    swap = _split_start([u_in, u_out], [SDS(u_in.shape, f32), SDS(u_out.shape, f32)], 2, _swap_copies,
                        "grad_pair_swap_start")

    dm = jnp.concatenate([r["dshift"], r["dscale"], r["dgate"]], axis=1)
    dmc = jnp.concatenate([r["dshift_c"], r["dscale_c"], jnp.zeros((1, D), f32)], axis=1)
    pack_parts = [_rows128(dm), _rows128(dmc), _rows128(r["dng"]), _rows128(r["dng_c"]), _rows128(r["g_gq"]),
                  _rows128(r["g_gk"]), _rows128(r["g_gk_c"]), r["g_rpb"].reshape(H * N_DR, 128),
                  _rows128(r["g_conv_b"]), _rows128(r["g_conv_w"][0:3]), jnp.pad(r["loss_sum"], ((0, 0), (0, 127)))]
    pack = jnp.concatenate([jnp.pad(p, ((0, -p.shape[0] % 8), (0, 0))) for p in pack_parts], axis=0)
    assert pack.shape[0] == PK_ROWS
    gs = _split_start([_to_slot(pack, 8, dev)], [], 7, _gather8_copies, "gather_small_start", after=swap[6])
    (u_in, u_out), (o_in, o_out) = _split_wait(swap[0], swap[1], swap[2:4], swap[4:6], gs[3], _swap_copies,
                                               "grad_pair_swap_wait")
    g_w_in_s, d_w_in, nm_w_in, nv_w_in = _adamw_halves(w_in[0], u_in, o_in, m_w_in[0], v_w_in[0], cvec, "adamw_w_in",
                                                       after=gs[3])
    g_w_out_s, d_w_out, nm_w_out, nv_w_out = _adamw_halves(w_out[0], u_out, o_out, m_w_out[0], v_w_out[0], cvec,
                                                           "adamw_w_out", after=nm_w_in)
    (gathered,), _ = _split_wait(gs[0], gs[1], [gs[2]], [], nm_w_out, _gather8_copies, "gather_small_wait")
    red, dm16 = _small_reduce(gathered)
    loss = red[RS_LOSS, 0] * (0.5 / D)

    g_w_ada_s, cpart = _w_ada_grad(sc16, dm16, w_ada_s, jvec)
    gcp = _split_start([_to_slot(cpart, 4, chip)], [], 3, _gather4_copies, "gather_c_ctx_parts_start")
    d_w_ada, nm_w_ada, nv_w_ada = _adamw(w_ada_s, g_w_ada_s, m_w_ada[0], v_w_ada[0], "adamw_w_ada", after=gcp[3])
    (cparts4,), _ = _split_wait(gcp[0], gcp[1], [gcp[2]], [], nm_w_ada, _gather4_copies, "gather_c_ctx_parts_wait")
    g_c_ctx = _c_ctx_grad(cparts4, c_ctx)

    t_rpb = lambda a: jnp.transpose(a, (0, 2, 1, 3)).reshape(N_DR, H, N_DC)
    t_cw = lambda a: jnp.transpose(a, (1, 0, 2))
    small = _adamw_small(
        red, g_c_ctx, jvec,
        [c_ctx, b_ada, norm_g, q_norm_g, k_norm_g, t_rpb(rpb), t_cw(conv_w), conv_b],
        [m_c_ctx, m_b_ada, m_norm_g, m_q_norm_g, m_k_norm_g, t_rpb(m_rpb), t_cw(m_conv_w), m_conv_b],
        [v_c_ctx, v_b_ada, v_norm_g, v_q_norm_g, v_k_norm_g, t_rpb(v_rpb), t_cw(v_conv_w), v_conv_b])
    for kind in small:
        kind[5] = jnp.transpose(kind[5].reshape(1, N_DR, H, N_DC), (0, 2, 1, 3))
        kind[6] = jnp.transpose(kind[6], (1, 0, 2))

    def ordered(kind, big_w_ada, big_w_in, big_w_out):
        s_c_ctx, s_b_ada, s_norm_g, s_q, s_k, s_rpb, s_conv_w, s_conv_b = small[kind]
        return [s_c_ctx, big_w_ada[None], s_b_ada, s_norm_g, big_w_in[None], s_q, s_k, s_rpb, s_conv_w,
                s_conv_b, big_w_out[None]]

    grads = ordered(0, g_w_ada_s, g_w_in_s, g_w_out_s)
    deltas = ordered(1, d_w_ada, d_w_in, d_w_out)
    new_m = ordered(2, nm_w_ada, nm_w_in, nm_w_out)
    new_v = ordered(3, nv_w_ada, nv_w_in, nv_w_out)
    return (loss, r["grad_x"][None], *grads, *deltas, *new_m, *new_v)
```

```python
import functools

import jax
import jax.numpy as jnp
import numpy as np
from jax import lax
from jax.experimental import pallas as pl
from jax.experimental.pallas import tpu as pltpu

f32, bf16, i32 = jnp.float32, jnp.bfloat16, jnp.int32
MESH = pl.DeviceIdType.MESH
HIGHEST = lax.Precision.HIGHEST

D = 1024
S = 2048
L = 256
GW = 64
ROWS = S // GW
H = 8
DH = 64
DA = H * DH
DC = 512
WIN_H, WIN_W = 8, 16
N_DR, N_DC = 2 * WIN_H - 1, 2 * WIN_W - 1
RMS_EPS = 1e-6
ROPE_THETA = 10000.0
QK_SCALE = DH ** -0.5
NEG = -1e30

QB = 128
NQB = S // QB
KR = 9
KB = KR * GW
TILE_GEOM = ((0, 0), (2, 0), (4, 0), (28, 23), (30, 23))
NT = len(TILE_GEOM)

ADAM_LR, ADAM_B1, ADAM_B2, ADAM_EPS, ADAM_WD, ADAM_STEP = 0.001, 0.9, 0.999, 1e-08, 0.01, 10

VMEM_SPEC = pl.BlockSpec(memory_space=pltpu.VMEM)
ANY_SPEC = pl.BlockSpec(memory_space=pl.ANY)
SMEM_SPEC = pl.BlockSpec(memory_space=pltpu.SMEM)
SDS = jax.ShapeDtypeStruct


_pallas_call = pl.pallas_call


def _hbm_call(body, *, out_shape, in_specs=None, out_specs=None, grid_spec=None, **kw):
    n_pre = 0
    if grid_spec is not None:
        ispecs, ospecs, n_pre = grid_spec.in_specs, grid_spec.out_specs, grid_spec.num_scalar_prefetch
        kw["grid_spec"] = grid_spec
    else:
        ispecs, ospecs = in_specs, out_specs
        kw.update(in_specs=in_specs, out_specs=out_specs)

    def blocked(spec):
        return isinstance(spec, pl.BlockSpec) and spec.block_shape is not None

    single = not isinstance(out_shape, (tuple, list))
    shapes = [out_shape] if single else list(out_shape)
    ospec_list = list(ospecs) if isinstance(ospecs, (tuple, list)) else [ospecs]
    shapes = [pltpu.HBM(s.shape, s.dtype) if blocked(sp) else s for s, sp in zip(shapes, ospec_list)]
    call = _pallas_call(body, out_shape=shapes[0] if single else tuple(shapes), **kw)

    def run(*args):
        arrays = [pltpu.with_memory_space_constraint(a, pltpu.HBM) if blocked(sp) else a
                  for a, sp in zip(args[n_pre:], ispecs)]
        return call(*args[:n_pre], *arrays)

    return run


def _cp(vmem_mb=None, **kw):
    if vmem_mb is not None:
        kw["vmem_limit_bytes"] = vmem_mb << 20
    return pltpu.CompilerParams(**kw)


def _silu(z):
    return z * jax.nn.sigmoid(z)


def _dsilu(z):
    sg = jax.nn.sigmoid(z)
    return sg * (1.0 + z * (1.0 - sg))


def _row_start(i):
    return min(max(i - WIN_H // 2, 0), ROWS - WIN_H)


def _my_pos():
    return lax.axis_index("x"), lax.axis_index("y"), lax.axis_index("c")


def _flip(v, bit):
    return 1 - v if bit else v


def _swap_copies(srcs, lands, ssem, rsem):
    x, y, c = _my_pos()
    return [pltpu.make_async_remote_copy(src_ref=srcs[a], dst_ref=lands[a], send_sem=ssem.at[a], recv_sem=rsem.at[a],
                                         device_id=(x, y, 1 - c), device_id_type=MESH) for a in range(len(srcs))]


HBM_SPEC = pl.BlockSpec(memory_space=pltpu.HBM)
SEM_SPEC = pl.BlockSpec(memory_space=pltpu.SEMAPHORE)
DATAFLOW = pltpu.SideEffectType.DATAFLOW_SIDE_EFFECTING


def _peer_chips(x, y, c):
    out = []
    for k in range(1, 4):
        px, py = _flip(x, (k >> 1) & 1), _flip(y, k & 1)
        out.append(((px, py, c), 2 * px + py))
    return out


def _half_copies(srcs, dsts, ssem, rsem, which):
    x, y, c = _my_pos()
    j = 2 * x + y
    peers = _peer_chips(x, y, c)
    pairs = []
    for pos, group, k in which:
        half = srcs[pos].shape[1] // 2
        mine = pl.ds(pl.multiple_of(c * half, 8), half)
        dev, pj = peers[k]
        sem = 3 * group + k
        send = pltpu.make_async_remote_copy(src_ref=srcs[pos].at[j, mine], dst_ref=dsts[pos].at[j, mine],
                                            send_sem=ssem.at[sem], recv_sem=rsem.at[sem], device_id=dev,
                                            device_id_type=MESH)
        arrive = pltpu.make_async_remote_copy(src_ref=srcs[pos].at[j, mine], dst_ref=dsts[pos].at[pj, mine],
                                              send_sem=ssem.at[sem], recv_sem=rsem.at[sem], device_id=dev,
                                              device_id_type=MESH)
        pairs.append((send, arrive))
    return pairs


def _halves_wait(ssem, rsem, bigs, after, which, name):
    nb = len(bigs)

    def body(*refs):
        b_in = refs[:nb]
        ssem_ref, rsem_ref = refs[nb], refs[nb + 1]
        for send, arrive in _half_copies(b_in, b_in, ssem_ref, rsem_ref, which):
            send.wait_send()
            arrive.wait_recv()

    return _hbm_call(
        body, name=name, out_shape=tuple(pltpu.HBM(b.shape, b.dtype) for b in bigs),
        in_specs=[HBM_SPEC] * nb + [SEM_SPEC, SEM_SPEC, ANY_SPEC], out_specs=tuple([HBM_SPEC] * nb),
        input_output_aliases={a: a for a in range(nb)}, compiler_params=_cp(has_side_effects=DATAFLOW),
    )(*bigs, ssem, rsem, after)


FORWARD_SEM = 3


def _diag_forward_copy(srcs, dsts, ssem, rsem):
    x, y, c = _my_pos()
    half = srcs[0].shape[1] // 2
    diag = 3 - (2 * x + y)
    mine = pl.ds(pl.multiple_of(c * half, 8), half)
    other = pl.ds(pl.multiple_of((1 - c) * half, 8), half)
    return [_Copy(srcs[0].at[diag, mine], dsts[0].at[diag, mine], dsts[0].at[diag, other], ssem.at[FORWARD_SEM],
                  rsem.at[FORWARD_SEM], (x, y, 1 - c))]


def _forward_then_start(fwd, big, order, name):
    def body(f_in, b_in, f_out, ssem, rsem, b_out, token):
        _diag_forward_copy([f_in], [f_out], ssem, rsem)[0].start()
        for send, _ in _half_copies([b_in], [b_out], ssem, rsem, order):
            send.start()
        token[...] = jnp.zeros_like(token)

    n_sem = FORWARD_SEM + 1
    out_shape = (pltpu.HBM(fwd.shape, fwd.dtype), pltpu.SemaphoreType.DMA((n_sem,)), pltpu.SemaphoreType.DMA((n_sem,)),
                 pltpu.HBM(big.shape, big.dtype), SDS((8, 128), f32))
    return _hbm_call(
        body, name=name, out_shape=out_shape, in_specs=[HBM_SPEC, HBM_SPEC],
        out_specs=(HBM_SPEC, SEM_SPEC, SEM_SPEC, HBM_SPEC, VMEM_SPEC), input_output_aliases={0: 0, 1: 3},
        compiler_params=_cp(has_side_effects=DATAFLOW),
    )(*[pltpu.with_memory_space_constraint(b, pltpu.HBM) for b in (fwd, big)])


def _cast_to_slot(w, jvec, name, after=None):
    rows, cols = w.shape
    tr = 256

    def body(j_ref, w_ref, after_ref, o_ref):
        o_ref[...] = w_ref[...].astype(bf16)

    grid_spec = pltpu.PrefetchScalarGridSpec(
        num_scalar_prefetch=1, grid=(rows // tr,),
        in_specs=[pl.BlockSpec((tr, cols), lambda i, j: (i, 0)), ANY_SPEC],
        out_specs=pl.BlockSpec((None, tr, cols), lambda i, j: (j[0], i, 0)))
    return _hbm_call(body, name=name, out_shape=SDS((4, rows, cols), bf16),
                     grid_spec=grid_spec)(jvec, w, jvec if after is None else after)


def _exchange_copies(srcs, lands, ssem, rsem):
    x, y, c = _my_pos()
    half = srcs[0].shape[0] // 8
    cps = []
    for jb in range(4):
        src = srcs[0].at[pl.ds(pl.multiple_of((2 * jb + 1 - c) * half, 8), half)]
        cps.append(pltpu.make_async_remote_copy(src_ref=src, dst_ref=lands[0].at[jb], send_sem=ssem.at[jb],
                                                recv_sem=rsem.at[jb], device_id=(x, y, 1 - c), device_id_type=MESH))
    return cps


def _block_exchange_copies(srcs, lands, ssem, rsem):
    x, y, c = _my_pos()
    return [pltpu.make_async_remote_copy(src_ref=srcs[0].at[jb], dst_ref=lands[0].at[jb], send_sem=ssem.at[jb],
                                         recv_sem=rsem.at[jb], device_id=(x, y, 1 - c), device_id_type=MESH)
            for jb in range(4)]


def _scatter_copies(srcs, lands, ssem, rsem):
    x, y, c = _my_pos()
    cps = []
    for a in range(len(srcs)):
        for k, (dev, pj) in enumerate(_peer_chips(x, y, c)):
            cps.append(pltpu.make_async_remote_copy(src_ref=srcs[a].at[pj], dst_ref=lands[a].at[k],
                                                    send_sem=ssem.at[3 * a + k], recv_sem=rsem.at[3 * a + k],
                                                    device_id=dev, device_id_type=MESH))
    return cps


class _Copy:
    def __init__(self, src, dst, arrive, ssem, rsem, dev):
        make = lambda to: pltpu.make_async_remote_copy(src_ref=src, dst_ref=to, send_sem=ssem, recv_sem=rsem,
                                                       device_id=dev, device_id_type=MESH)
        send, arrival = make(dst), make(arrive)
        self.start, self.wait_send, self.wait_recv = send.start, send.wait_send, arrival.wait_recv


def _toward(x, y, along_x):
    return x + along_x * (1 - 2 * x), y + (1 - along_x) * (1 - 2 * y)


def _near_copies(srcs, dsts, ssem, rsem):
    x, y, c = _my_pos()
    px, py = _toward(x, y, c)
    j = 2 * x + y
    return [_Copy(srcs[0].at[j], dsts[0].at[j], dsts[0].at[2 * px + py], ssem.at[0], rsem.at[0], (px, py, c))]


def _pass_copy(srcs, dsts, ssem, rsem):
    x, y, c = _my_pos()
    px, py = _toward(x, y, c)
    qx, qy = _toward(x, y, 1 - c)
    got = 2 * px + py
    return [_Copy(srcs[0].at[got], dsts[0].at[got], dsts[0].at[2 * qx + qy], ssem.at[0], rsem.at[0], (x, y, 1 - c))]


def _relay_copy(srcs, dsts, ssem, rsem):
    x, y, c = _my_pos()
    px, py = _toward(x, y, c)
    qx, qy = _toward(x, y, 1 - c)
    half = srcs[0].shape[1] // 2
    mine = pl.ds(pl.multiple_of(c * half, 8), half)
    got, diag = 2 * px + py, 3 - (2 * x + y)
    return [_Copy(srcs[0].at[got, mine], dsts[0].at[got, mine], dsts[0].at[diag, mine], ssem.at[1], rsem.at[1],
                  (qx, qy, c))]


def _forward_copies(srcs, dsts, ssem, rsem):
    x, y, c = _my_pos()
    half = srcs[0].shape[1] // 2
    mine = pl.ds(pl.multiple_of(c * half, 8), half)
    other = pl.ds(pl.multiple_of((1 - c) * half, 8), half)
    return [_Copy(srcs[0].at[pj, mine], dsts[0].at[pj, mine], dsts[0].at[pj, other], ssem.at[k], rsem.at[k],
                  (x, y, 1 - c)) for k, (_, pj) in enumerate(_peer_chips(x, y, c))]


def _pass_relay_copies(srcs, dsts, ssem, rsem):
    return _pass_copy(srcs, dsts, ssem, rsem) + _relay_copy(srcs, dsts, ssem, rsem)


def _gather8_copies(srcs, dsts, ssem, rsem):
    x, y, c = _my_pos()
    me = 4 * x + 2 * y + c
    cps = []
    for a in range(len(srcs)):
        for k in range(1, 8):
            tgt = (_flip(x, (k >> 2) & 1), _flip(y, (k >> 1) & 1), _flip(c, k & 1))
            cps.append(_Copy(srcs[a].at[me], dsts[a].at[me], dsts[a].at[4 * tgt[0] + 2 * tgt[1] + tgt[2]],
                             ssem.at[7 * a + k - 1], rsem.at[7 * a + k - 1], tgt))
    return cps


def _gather4_copies(srcs, dsts, ssem, rsem):
    x, y, c = _my_pos()
    j = 2 * x + y
    cps = []
    for a in range(len(srcs)):
        for k, (dev, pj) in enumerate(_peer_chips(x, y, c)):
            cps.append(_Copy(srcs[a].at[j], dsts[a].at[j], dsts[a].at[pj], ssem.at[3 * a + k], rsem.at[3 * a + k], dev))
    return cps


def _to_slot(a, n, i):
    return lax.dynamic_update_slice(jnp.zeros((n,) + a.shape, a.dtype), a[None], (i,) + (0,) * a.ndim)


def _split_start(srcs, land_shapes, n_cp, make, name, after=None):
    ns, nl = len(srcs), len(land_shapes)
    n_in = ns + nl + (after is not None)

    def body(*refs):
        s_in = refs[:ns]
        ssem, rsem = refs[n_in], refs[n_in + 1]
        s_out = refs[n_in + 2:n_in + 2 + ns]
        l_out = refs[n_in + 2 + ns:n_in + 2 + ns + nl]
        token = refs[n_in + 2 + ns + nl]
        for cp in make(s_in, l_out if nl else s_out, ssem, rsem):
            cp.start()
        token[...] = jnp.zeros_like(token)

    lands = [pltpu.with_memory_space_constraint(lax.empty(sh.shape, sh.dtype), pltpu.HBM) for sh in land_shapes]
    out_shape = (pltpu.SemaphoreType.DMA((n_cp,)), pltpu.SemaphoreType.DMA((n_cp,)),
                 *[pltpu.HBM(b.shape, b.dtype) for b in srcs], *[pltpu.HBM(b.shape, b.dtype) for b in land_shapes],
                 SDS((8, 128), f32))
    return _hbm_call(
        body, name=name, out_shape=out_shape, in_specs=[HBM_SPEC] * (ns + nl) + [ANY_SPEC] * (after is not None),
        out_specs=(SEM_SPEC, SEM_SPEC, *[HBM_SPEC] * (ns + nl), VMEM_SPEC),
        input_output_aliases={i: 2 + i for i in range(ns + nl)}, compiler_params=_cp(has_side_effects=DATAFLOW),
    )(*[pltpu.with_memory_space_constraint(b, pltpu.HBM) for b in srcs], *lands, *([] if after is None else [after]))


def _split_wait(ssem, rsem, srcs, lands, after, make, name):
    ns, nl = len(srcs), len(lands)

    def body(*refs):
        s_in, l_in = refs[:ns], refs[ns:ns + nl]
        ssem_ref, rsem_ref = refs[ns + nl], refs[ns + nl + 1]
        for cp in make(s_in, l_in if nl else s_in, ssem_ref, rsem_ref):
            cp.wait_send()
            cp.wait_recv()

    outs = _hbm_call(
        body, name=name, out_shape=tuple(pltpu.HBM(b.shape, b.dtype) for b in (*srcs, *lands)),
        in_specs=[HBM_SPEC] * (ns + nl) + [SEM_SPEC, SEM_SPEC, ANY_SPEC], out_specs=tuple([HBM_SPEC] * (ns + nl)),
        input_output_aliases={i: i for i in range(ns + nl)}, compiler_params=_cp(has_side_effects=DATAFLOW),
    )(*srcs, *lands, ssem, rsem, after)
    return list(outs[:ns]), list(outs[ns:])


def _adaln_shard(cc, w_ada_shard):
    def body(c_ref, w_ref, m_ref, sc_ref):
        sc = _silu(c_ref[...])
        sc_ref[...] = sc
        m_ref[...] = jnp.dot(sc, w_ref[...], precision=HIGHEST, preferred_element_type=f32)

    return _hbm_call(
        body, name="adaln_shard", out_shape=(SDS((16, w_ada_shard.shape[1]), f32), SDS((16, D), f32)),
        in_specs=[VMEM_SPEC, VMEM_SPEC], out_specs=(VMEM_SPEC, VMEM_SPEC), compiler_params=_cp(32),
    )(cc, w_ada_shard)


def _prenorm(xx, norm_g, mrow, b_ada, tm, name, after=None):
    n = xx.shape[0]

    def body(x_ref, g_ref, m_ref, b_ref, after_ref, h_ref):
        x = x_ref[...]
        shift = m_ref[:, 0:D] + b_ref[:, 0:D]
        scale = m_ref[:, D:2 * D] + b_ref[:, D:2 * D]
        r = lax.rsqrt(jnp.mean(x * x, axis=-1, keepdims=True) + RMS_EPS)
        y = (x * r) * g_ref[...]
        h_ref[...] = (y * (1.0 + scale) + shift).astype(bf16)

    row = lambda i: (i, 0)
    fixed = lambda i: (0, 0)
    return _hbm_call(
        body, name=name, out_shape=SDS((n, D), bf16), grid=(n // tm,),
        in_specs=[pl.BlockSpec((tm, D), row), pl.BlockSpec((1, D), fixed), pl.BlockSpec((1, 3 * D), fixed),
                  pl.BlockSpec((1, 3 * D), fixed), ANY_SPEC],
        out_specs=pl.BlockSpec((tm, D), row),
    )(xx, norm_g, mrow, b_ada, b_ada if after is None else after)


def _in_proj_own(h, w_own, jvec):
    tm = 512

    def body(j_ref, h_ref, w_ref, p_ref):
        p_ref[...] = jnp.dot(h_ref[...], w_ref[...].astype(bf16), preferred_element_type=f32)

    grid_spec = pltpu.PrefetchScalarGridSpec(
        num_scalar_prefetch=1, grid=(S // tm,),
        in_specs=[pl.BlockSpec((tm, D), lambda i, j: (i, 0)), pl.BlockSpec((D, D), lambda i, j: (0, 0))],
        out_specs=pl.BlockSpec((tm, D), lambda i, j: (i, j[0])))
    return _hbm_call(body, name="in_proj_own", out_shape=SDS((S, 4 * D), f32), grid_spec=grid_spec,
                     compiler_params=_cp(40))(jvec, h, w_own)


def _in_proj_block(h, w4, p, bvec, name, after=None):
    tm = 512

    def body(b_ref, h_ref, w_ref, p_in_ref, after_ref, p_ref):
        p_ref[...] = jnp.dot(h_ref[...], w_ref[...], preferred_element_type=f32)

    grid_spec = pltpu.PrefetchScalarGridSpec(
        num_scalar_prefetch=1, grid=(S // tm,),
        in_specs=[pl.BlockSpec((tm, D), lambda i, b: (i, 0)), pl.BlockSpec((None, D, D), lambda i, b: (b[0], 0, 0)),
                  ANY_SPEC, ANY_SPEC],
        out_specs=pl.BlockSpec((tm, D), lambda i, b: (i, b[0])))
    return _hbm_call(body, name=name, out_shape=SDS((S, 4 * D), f32), grid_spec=grid_spec,
                     input_output_aliases={3: 0})(bvec, h, w4, p, bvec if after is None else after)


def _ctx_proj(hc, w4):
    def body(h_ref, w0_ref, w1_ref, p_ref):
        hv = h_ref[...]
        p_ref[:, 0:DA] = jnp.dot(hv, w0_ref[:, DA:2 * DA], preferred_element_type=f32)
        p_ref[:, DA:2 * DA] = jnp.dot(hv, w1_ref[:, 0:DA], preferred_element_type=f32)

    return _hbm_call(
        body, name="ctx_proj", out_shape=SDS((L, 2 * DA), f32), grid=(1,),
        in_specs=[pl.BlockSpec((L, D), lambda i: (0, 0)), pl.BlockSpec((None, D, D), lambda i: (0, 0, 0)),
                  pl.BlockSpec((None, D, D), lambda i: (1, 0, 0))],
        out_specs=pl.BlockSpec((L, 2 * DA), lambda i: (0, 0)),
    )(hc, w4, w4)


def _head_ones():
    r = lax.broadcasted_iota(i32, (DA, DA), 0) // DH
    c = lax.broadcasted_iota(i32, (DA, DA), 1) // DH
    return (r == c).astype(bf16)


def _head_sum(v, ones_bd):
    hi = v.astype(bf16)
    lo = (v - hi.astype(f32)).astype(bf16)
    return jnp.dot(hi, ones_bd, preferred_element_type=f32) + jnp.dot(lo, ones_bd, preferred_element_type=f32)


def _swap16(v):
    lane = lax.broadcasted_iota(i32, v.shape, 1)
    return jnp.where((lane & 31) < 16, pltpu.roll(v, DA - 16, 1), pltpu.roll(v, 16, 1))


def _rope_block(ct_ref, rt_ref, tm):
    rows = [jnp.tile(rt_ref[8 * j:8 * j + 8, :], (GW // 8, 1)) for j in range(tm // GW)]
    return jnp.tile(ct_ref[...], (tm // GW, 1)) + jnp.concatenate(rows, axis=0)


def _rope_specs(tm):
    col = pl.BlockSpec((GW, DA), lambda i: (0, 0))
    row = pl.BlockSpec((8 * tm // GW, DA), lambda i: (i, 0))
    return [col, row, col, row]


def _qk_prep(p, gq, gk, rope):
    tm = 512

    def body(qk_ref, v_ref, gq_ref, gk_ref, cc_ref, cr_ref, sc_ref, sr_ref, qr_ref, qp_ref, kr_ref, vh_ref):
        ones_bd = _head_ones()
        cs, sn = _rope_block(cc_ref, cr_ref, tm), _rope_block(sc_ref, sr_ref, tm)
        q = qk_ref[:, 0:DA]
        k = qk_ref[:, DA:2 * DA]
        yq = (q * lax.rsqrt(_head_sum(q * q, ones_bd) * (1.0 / DH) + RMS_EPS)) * gq_ref[...]
        yk = (k * lax.rsqrt(_head_sum(k * k, ones_bd) * (1.0 / DH) + RMS_EPS)) * gk_ref[...]
        qr = (yq * cs + _swap16(yq) * sn) * QK_SCALE
        qp = yq * QK_SCALE
        kr = yk * cs + _swap16(yk) * sn
        vv = v_ref[...]
        for hh in range(H):
            sl = slice(hh * DH, (hh + 1) * DH)
            qr_ref[hh] = qr[:, sl].astype(bf16)
            qp_ref[hh] = qp[:, sl].astype(bf16)
            kr_ref[hh] = kr[:, sl].astype(bf16)
            vh_ref[hh] = vv[:, sl].astype(bf16)

    hm = SDS((H, S, DH), bf16)
    hspec = pl.BlockSpec((H, tm, DH), lambda i: (0, i, 0))
    fixed = lambda i: (0, 0)
    return _hbm_call(
        body, name="qk_prep", out_shape=(hm, hm, hm, hm), grid=(S // tm,),
        in_specs=[pl.BlockSpec((tm, 2 * DA), lambda i: (i, 0)), pl.BlockSpec((tm, DA), lambda i: (i, 2)),
                  pl.BlockSpec((1, DA), fixed), pl.BlockSpec((1, DA), fixed)] + _rope_specs(tm),
        out_specs=(hspec, hspec, hspec, hspec),
    )(p, p, gq, gk, *rope)


def _ctx_prep(pc, gk):
    def body(p_ref, gk_ref, kc_ref, vc_ref):
        ones_bd = _head_ones()
        k = p_ref[:, 0:DA]
        yk = (k * lax.rsqrt(_head_sum(k * k, ones_bd) * (1.0 / DH) + RMS_EPS)) * gk_ref[...]
        vv = p_ref[:, DA:2 * DA]
        for hh in range(H):
            sl = slice(hh * DH, (hh + 1) * DH)
            kc_ref[hh] = yk[:, sl].astype(bf16)
            vc_ref[hh] = vv[:, sl].astype(bf16)

    hm = SDS((H, L, DH), bf16)
    return _hbm_call(
        body, name="ctx_prep", out_shape=(hm, hm), in_specs=[VMEM_SPEC, VMEM_SPEC], out_specs=(VMEM_SPEC, VMEM_SPEC),
    )(pc, gk)


def _tile_pieces():
    out = []
    for (i0, u0) in TILE_GEOM:
        rows = []
        for j in range(2):
            i = i0 + j
            rs = _row_start(i)
            rows.append([(u0 + u - i + WIN_H - 1) if rs <= u0 + u < rs + WIN_H else None for u in range(KR)])
        out.append(rows)
    return out


def _bias_prep(rpb_rev_pad, after=None):
    pieces = _tile_pieces()

    def body(r_ref, after_ref, o_ref):
        rp = r_ref[...]
        xs = jnp.concatenate([pltpu.roll(jnp.broadcast_to(rp[dr:dr + 1, :], (GW, 128)), 128 - (WIN_W - 1), 1,
                                         stride=1, stride_axis=0) for dr in range(N_DR)], axis=0)
        row = lax.broadcasted_iota(i32, xs.shape, 0)
        lane = lax.broadcasted_iota(i32, xs.shape, 1)
        k = row & (GW - 1)
        c0 = jnp.clip(lane - WIN_W // 2, 0, GW - WIN_W)
        xs = jnp.where((k >= c0) & (k < c0 + WIN_W), xs, NEG)
        neg = jnp.full((GW, GW), NEG, f32)
        for t in range(NT):
            for j in range(2):
                for u in range(KR):
                    dr = pieces[t][j][u]
                    piece = neg if dr is None else xs[dr * GW:(dr + 1) * GW, 0:GW]
                    o_ref[t, u * GW:(u + 1) * GW, j * GW:(j + 1) * GW] = piece

    return _hbm_call(
        body, name="bias_prep", out_shape=SDS((H, NT, KB, QB), f32), grid=(H,),
        in_specs=[pl.BlockSpec((None, N_DR, 128), lambda h: (h, 0, 0)), ANY_SPEC],
        out_specs=pl.BlockSpec((None, NT, KB, QB), lambda h: (h, 0, 0, 0)),
    )(rpb_rev_pad, rpb_rev_pad if after is None else after)


def _bias_tiles(rpb2, after=None):
    return _bias_prep(jnp.pad(rpb2[:, :, ::-1], ((0, 0), (0, 0), (0, 128 - N_DC))), after)


def _block_geom(b):
    qs = b * QB
    ks = min(max(2 * b - 4, 0), ROWS - KR) * GW
    t = b if b < 2 else (b - (NQB - NT) if b > NQB - 3 else 2)
    return qs, ks, t


def _tt(a, b):
    return lax.dot_general(a, b, (((1,), (1,)), ((), ())), preferred_element_type=f32)


def _tn(a, b):
    return lax.dot_general(a, b, (((0,), (0,)), ((), ())), preferred_element_type=f32)


def _softmax_t(s_lat, s_ctx):
    m = jnp.maximum(jnp.max(s_lat, axis=0, keepdims=True), jnp.max(s_ctx, axis=0, keepdims=True))
    e_lat = jnp.exp(s_lat - m)
    e_ctx = jnp.exp(s_ctx - m)
    inv = 1.0 / (jnp.sum(e_lat, axis=0, keepdims=True) + jnp.sum(e_ctx, axis=0, keepdims=True))
    return e_lat * inv, e_ctx * inv


def _staged(n_blocks, stages):
    held = [dict() for _ in stages]
    for step in range(n_blocks + len(stages) - 1):
        for s, fn in enumerate(stages):
            b = step - s
            if 0 <= b < n_blocks:
                held[s][b] = fn(b) if s == 0 else fn(b, held[s - 1].pop(b))


def _attn_fwd(qr, qp, kr, vh, kc, vc, btt):
    def body(qr_ref, qp_ref, kr_ref, v_ref, kc_ref, vc_ref, bt_ref, o_ref):
        kcv, vcv = kc_ref[...], vc_ref[...]

        def scores(b):
            qs, ks, t = _block_geom(b)
            return (_tt(kr_ref[ks:ks + KB, :], qr_ref[qs:qs + QB, :]) + bt_ref[t], _tt(kcv, qp_ref[qs:qs + QB, :]))

        def probs(b, sc):
            p_lat, p_ctx = _softmax_t(*sc)
            return p_lat.astype(bf16), p_ctx.astype(bf16)

        def values(b, p):
            qs, ks, _ = _block_geom(b)
            o_ref[qs:qs + QB, :] = _tn(p[0], v_ref[ks:ks + KB, :]) + _tn(p[1], vcv)

        _staged(NQB, (scores, probs, values))

    sq = pl.BlockSpec((None, S, DH), lambda h: (h, 0, 0))
    sc = pl.BlockSpec((None, L, DH), lambda h: (h, 0, 0))
    return _hbm_call(
        body, name="attn_fwd", out_shape=SDS((H, S, DH), f32), grid=(H,),
        in_specs=[sq, sq, sq, sq, sc, sc, pl.BlockSpec((None, NT, KB, QB), lambda h: (h, 0, 0, 0))],
        out_specs=sq, compiler_params=_cp(48),
    )(qr, qp, kr, vh, kc, vc, btt)


def _shift_rows(v, down):
    n = v.shape[0]
    row = lax.broadcasted_iota(i32, v.shape, 0)
    if down:
        return jnp.where(row == 0, 0.0, pltpu.roll(v, 1, 0))
    return jnp.where(row == n - 1, 0.0, pltpu.roll(v, n - 1, 0))


def _conv_specs():
    col = lambda off: pl.BlockSpec((S, 128), lambda i, off=off: (0, off + i))
    return [col(16), col(20), col(24), col(28), pl.BlockSpec((3, 128), lambda i: (0, i)),
            pl.BlockSpec((1, 128), lambda i: (0, i))]


def _conv_fwd(p, conv_w, conv_b, after=None):
    def body(u_ref, bg_ref, cg_ref, zc_ref, w_ref, b_ref, after_ref, o_ref):
        cu = cg_ref[...] * u_ref[...]
        cv = b_ref[...] + _shift_rows(cu, True) * w_ref[0:1, :]
        cv = cv + cu * w_ref[1:2, :]
        cv = cv + _shift_rows(cu, False) * w_ref[2:3, :]
        o_ref[...] = ((bg_ref[...] * cv) * _silu(zc_ref[...])).astype(bf16)

    return _hbm_call(
        body, name="conv_fwd", out_shape=SDS((S, DC), bf16), grid=(DC // 128,),
        in_specs=_conv_specs() + [ANY_SPEC], out_specs=pl.BlockSpec((S, 128), lambda i: (0, i)),
        compiler_params=_cp(40),
    )(p, p, p, p, conv_w, conv_b, conv_b if after is None else after)


DP_Q, DP_K, DP_V, DP_ZA, DP_U, DP_BG, DP_CG, DP_ZC = range(8)


def _out_proj_loss(o, p, conv_g, w_out, xx, tgt, mrow, b_ada):
    tm = 512

    def body(o_ref, za_ref, c_ref, w_ref, x_ref, t_ref, m_ref, b_ref,
             dy_ref, dconv_ref, dp_ref, do_ref, gwo_ref, dgate_ref, loss_ref):
        k = pl.program_id(0)

        @pl.when(k == 0)
        def _():
            gwo_ref[...] = jnp.zeros_like(gwo_ref)
            dgate_ref[...] = jnp.zeros_like(dgate_ref)
            loss_ref[0, 0] = 0.0

        gate = m_ref[:, 2 * D:3 * D] + b_ref[:, 2 * D:3 * D]
        za = za_ref[...]
        sz = _silu(za)
        om = _merge_heads(o_ref)
        av, cv = (om * sz).astype(bf16), c_ref[...]
        mo = jnp.dot(av, w_ref[0:DA, :], preferred_element_type=f32)
        mo = mo + jnp.dot(cv, w_ref[DA:DA + DC, :], preferred_element_type=f32)
        y = x_ref[...] + gate * mo
        diff = y - t_ref[...]
        loss_ref[0, 0] += jnp.sum(diff * diff)
        dy = diff * (1.0 / D)
        dy_ref[...] = dy
        dgate_ref[...] += jnp.sum(dy * mo, axis=0, keepdims=True)
        dmo = (dy * gate).astype(bf16)
        dmix = _tt(dmo, w_ref[...])
        dattn = dmix[:, 0:DA]
        dconv_ref[...] = dmix[:, DA:DA + DC]
        a = dattn * sz
        for hh in range(H):
            do_ref[hh] = a[:, hh * DH:(hh + 1) * DH].astype(bf16)
        dp_ref[...] = ((dattn * _dsilu(za)) * om).astype(bf16)
        gwo_ref[0:DA, :] += _tn(av, dmo)
        gwo_ref[DA:DA + DC, :] += _tn(cv, dmo)

    row = lambda i: (i, 0)
    fixed = lambda i: (0, 0)
    hspec = pl.BlockSpec((H, tm, DH), lambda i: (0, i, 0))
    return _hbm_call(
        body, name="out_proj_loss",
        out_shape=(SDS((S, D), f32), SDS((S, DC), f32), SDS((8, S, DA), bf16), SDS((H, S, DH), bf16),
                   SDS((D, D), f32), SDS((1, D), f32), SDS((1, 1), f32)),
        grid=(S // tm,),
        in_specs=[hspec, pl.BlockSpec((tm, DA), lambda i: (i, 3)), pl.BlockSpec((tm, DC), row),
                  pl.BlockSpec((D, D), fixed), pl.BlockSpec((tm, D), row), pl.BlockSpec((tm, D), row),
                  pl.BlockSpec((1, 3 * D), fixed), pl.BlockSpec((1, 3 * D), fixed)],
        out_specs=(pl.BlockSpec((tm, D), row), pl.BlockSpec((tm, DC), row),
                   pl.BlockSpec((None, tm, DA), lambda i: (DP_ZA, i, 0)), hspec, pl.BlockSpec((D, D), fixed),
                   pl.BlockSpec((1, D), fixed), SMEM_SPEC),
        compiler_params=_cp(56, dimension_semantics=("arbitrary",)),
    )(o, p, conv_g, w_out, xx, tgt, mrow, b_ada)


def _conv_bwd(dconv, p, conv_w, conv_b, dp8, after=None):
    def body(d_ref, u_ref, bg_ref, cg_ref, zc_ref, w_ref, b_ref, dp_in_ref, after_ref, dp_ref, gw_ref, gb_ref):
        du_ref, dbg_ref, dcg_ref, dzc_ref = dp_ref.at[0], dp_ref.at[1], dp_ref.at[2], dp_ref.at[3]
        dconv = d_ref[...]
        u, bg, cg, zc = u_ref[...], bg_ref[...], cg_ref[...], zc_ref[...]
        w0, w1, w2 = w_ref[0:1, :], w_ref[1:2, :], w_ref[2:3, :]
        cu = cg * u
        cu_m, cu_p = _shift_rows(cu, True), _shift_rows(cu, False)
        cv = b_ref[...] + cu_m * w0
        cv = cv + cu * w1
        cv = cv + cu_p * w2
        sz = _silu(zc)
        dbg_ref[...] = ((dconv * sz) * cv).astype(bf16)
        dzc_ref[...] = ((dconv * (bg * cv)) * _dsilu(zc)).astype(bf16)
        dcv = (dconv * sz) * bg
        gb_ref[...] = jnp.sum(dcv, axis=0, keepdims=True)
        gw_ref[0:1, :] = jnp.sum(dcv * cu_m, axis=0, keepdims=True)
        gw_ref[1:2, :] = jnp.sum(dcv * cu, axis=0, keepdims=True)
        gw_ref[2:3, :] = jnp.sum(dcv * cu_p, axis=0, keepdims=True)
        gw_ref[3:8, :] = jnp.zeros((5, 128), f32)
        dcu = _shift_rows(dcv, False) * w0 + dcv * w1 + _shift_rows(dcv, True) * w2
        dcg_ref[...] = (dcu * u).astype(bf16)
        du_ref[...] = (dcu * cg).astype(bf16)

    return _hbm_call(
        body, name="conv_bwd", out_shape=(SDS((8, S, DC), bf16), SDS((8, DC), f32), SDS((1, DC), f32)),
        grid=(DC // 128,),
        in_specs=[pl.BlockSpec((S, 128), lambda i: (0, i))] + _conv_specs() + [ANY_SPEC, ANY_SPEC],
        out_specs=(pl.BlockSpec((4, S, 128), lambda i: (DP_U // 4, 0, i)), pl.BlockSpec((8, 128), lambda i: (0, i)),
                   pl.BlockSpec((1, 128), lambda i: (0, i))),
        input_output_aliases={7: 0}, compiler_params=_cp(48),
    )(dconv, p, p, p, p, conv_w, conv_b, dp8, conv_b if after is None else after)


def _attn_bwd(qr, qp, kr, vh, kc, vc, btt, do, after=None):
    def body(qr_ref, qp_ref, kr_ref, v_ref, kc_ref, vc_ref, bt_ref, do_ref, after_ref,
             dqr_ref, dqp_ref, dkr_ref, dv_ref, dkc_ref, dvc_ref, dbt_ref):
        kcv, vcv = kc_ref[...], vc_ref[...]
        dkr_ref[...] = jnp.zeros_like(dkr_ref)
        dv_ref[...] = jnp.zeros_like(dv_ref)
        dbt_ref[...] = jnp.zeros_like(dbt_ref)
        ctx_acc = {}

        def products(b):
            qs, ks, t = _block_geom(b)
            dob = do_ref[qs:qs + QB, :]
            s_lat = _tt(kr_ref[ks:ks + KB, :], qr_ref[qs:qs + QB, :]) + bt_ref[t]
            s_ctx = _tt(kcv, qp_ref[qs:qs + QB, :])
            return s_lat, s_ctx, _tt(v_ref[ks:ks + KB, :], dob), _tt(vcv, dob)

        def score_grads(b, x):
            s_lat, s_ctx, dp_lat, dp_ctx = x
            p_lat, p_ctx = _softmax_t(s_lat, s_ctx)
            delta = jnp.sum(p_lat * dp_lat, axis=0, keepdims=True) + jnp.sum(p_ctx * dp_ctx, axis=0, keepdims=True)
            ds_lat = p_lat * (dp_lat - delta)
            ds_ctx = p_ctx * (dp_ctx - delta)
            return ds_lat, ds_lat.astype(bf16), ds_ctx.astype(bf16), p_lat.astype(bf16), p_ctx.astype(bf16)

        def operand_grads(b, y):
            qs, ks, t = _block_geom(b)
            ds_lat, dsb_lat, dsb_ctx, pb_lat, pb_ctx = y
            qrb, qpb, dob = qr_ref[qs:qs + QB, :], qp_ref[qs:qs + QB, :], do_ref[qs:qs + QB, :]
            dbt_ref[t] += ds_lat
            dqr_ref[qs:qs + QB, :] = _tn(dsb_lat, kr_ref[ks:ks + KB, :])
            dqp_ref[qs:qs + QB, :] = _tn(dsb_ctx, kcv)
            dkr_ref[ks:ks + KB, :] += jnp.dot(dsb_lat, qrb, preferred_element_type=f32)
            dv_ref[ks:ks + KB, :] += jnp.dot(pb_lat, dob, preferred_element_type=f32)
            dkc = jnp.dot(dsb_ctx, qpb, preferred_element_type=f32)
            dvc = jnp.dot(pb_ctx, dob, preferred_element_type=f32)
            ctx_acc["k"] = dkc if b == 0 else ctx_acc["k"] + dkc
            ctx_acc["v"] = dvc if b == 0 else ctx_acc["v"] + dvc

        _staged(NQB, (products, score_grads, operand_grads))
        dkc_ref[...] = ctx_acc["k"]
        dvc_ref[...] = ctx_acc["v"]

    sq = pl.BlockSpec((None, S, DH), lambda h: (h, 0, 0))
    sc = pl.BlockSpec((None, L, DH), lambda h: (h, 0, 0))
    sb = pl.BlockSpec((None, NT, KB, QB), lambda h: (h, 0, 0, 0))
    big, ctxs = SDS((H, S, DH), f32), SDS((H, L, DH), f32)
    return _hbm_call(
        body, name="attn_bwd", out_shape=(big, big, big, big, ctxs, ctxs, SDS((H, NT, KB, QB), f32)), grid=(H,),
        in_specs=[sq, sq, sq, sq, sc, sc, sb, sq, ANY_SPEC], out_specs=(sq, sq, sq, sq, sc, sc, sb),
        compiler_params=_cp(56),
    )(qr, qp, kr, vh, kc, vc, btt, do, do if after is None else after)


def _bias_bwd(dbtt, after=None):
    pieces = _tile_pieces()

    def body(d_ref, after_ref, o_ref, scr):
        scr[...] = jnp.zeros_like(scr)
        acc = [None] * N_DR
        for t in range(NT):
            for j in range(2):
                for u in range(KR):
                    dr = pieces[t][j][u]
                    if dr is None:
                        continue
                    piece = d_ref[t, u * GW:(u + 1) * GW, j * GW:(j + 1) * GW]
                    acc[dr] = piece if acc[dr] is None else acc[dr] + piece
        a = lax.broadcasted_iota(i32, (GW, GW), 0)
        b = lax.broadcasted_iota(i32, (GW, GW), 1)
        flip = (a + b == GW - 1).astype(f32)
        for dr in range(N_DR):
            scr[dr * GW:(dr + 1) * GW, 0:GW] = jnp.dot(acc[dr], flip, precision=HIGHEST, preferred_element_type=f32)
        xs = jnp.concatenate([pltpu.roll(scr[dr * GW:(dr + 1) * GW, :], 128 + (WIN_W - 1) - (GW - 1), 1,
                                         stride=1, stride_axis=0) for dr in range(N_DR)], axis=0)
        tot = jnp.sum(xs.reshape(N_DR, GW, 128), axis=1)
        lane = lax.broadcasted_iota(i32, tot.shape, 1)
        o_ref[...] = jnp.where(lane < N_DC, tot, 0.0)

    return _hbm_call(
        body, name="bias_bwd", out_shape=SDS((H, N_DR, 128), f32), grid=(H,),
        in_specs=[pl.BlockSpec((None, NT, KB, QB), lambda h: (h, 0, 0, 0)), ANY_SPEC],
        out_specs=pl.BlockSpec((None, N_DR, 128), lambda h: (h, 0, 0)),
        scratch_shapes=[pltpu.VMEM((N_DR * GW, 128), f32)],
    )(dbtt, dbtt if after is None else after)


def _merge_heads(ref):
    return jnp.concatenate([ref[hh] for hh in range(H)], axis=1)


def _head_norm_bwd(xraw, gain, dy, ones_bd):
    r = lax.rsqrt(_head_sum(xraw * xraw, ones_bd) * (1.0 / DH) + RMS_EPS)
    xh = xraw * r
    gdy = dy * gain
    dx = r * (gdy - xh * (_head_sum(xh * gdy, ones_bd) * (1.0 / DH)))
    return dx, jnp.sum(dy * xh, axis=0, keepdims=True)


def _qk_bwd(dqr, dqp, dkr, dvh, p, gq, gk, rope, dp8):
    tm = 512

    def body(dqr_ref, dqp_ref, dkr_ref, dv_ref, qk_ref, gq_ref, gk_ref, cc_ref, cr_ref, sc_ref, sr_ref, dp_in_ref,
             dp_ref, ggq_ref, ggk_ref):
        dq_ref, dk_ref, dvo_ref = dp_ref.at[DP_Q], dp_ref.at[DP_K], dp_ref.at[DP_V]

        @pl.when(pl.program_id(0) == 0)
        def _():
            ggq_ref[...] = jnp.zeros_like(ggq_ref)
            ggk_ref[...] = jnp.zeros_like(ggk_ref)

        ones_bd = _head_ones()
        cs, sn = _rope_block(cc_ref, cr_ref, tm), _rope_block(sc_ref, sr_ref, tm)
        a = _merge_heads(dqr_ref)
        dyq = ((a * cs - _swap16(a) * sn) + _merge_heads(dqp_ref)) * QK_SCALE
        bk = _merge_heads(dkr_ref)
        dyk = bk * cs - _swap16(bk) * sn
        dq, gq_part = _head_norm_bwd(qk_ref[:, 0:DA], gq_ref[...], dyq, ones_bd)
        dk, gk_part = _head_norm_bwd(qk_ref[:, DA:2 * DA], gk_ref[...], dyk, ones_bd)
        dq_ref[...] = dq.astype(bf16)
        dk_ref[...] = dk.astype(bf16)
        dvo_ref[...] = _merge_heads(dv_ref).astype(bf16)
        ggq_ref[...] += gq_part
        ggk_ref[...] += gk_part

    hspec = pl.BlockSpec((H, tm, DH), lambda i: (0, i, 0))
    fixed = pl.BlockSpec((1, DA), lambda i: (0, 0))
    return _hbm_call(
        body, name="qk_bwd", out_shape=(SDS((8, S, DA), bf16), SDS((1, DA), f32), SDS((1, DA), f32)), grid=(S // tm,),
        in_specs=[hspec, hspec, hspec, hspec, pl.BlockSpec((tm, 2 * DA), lambda i: (i, 0)), fixed, fixed]
        + _rope_specs(tm) + [ANY_SPEC],
        out_specs=(pl.BlockSpec((3, tm, DA), lambda i: (0, i, 0)), fixed, fixed), input_output_aliases={11: 0},
        compiler_params=_cp(40, dimension_semantics=("arbitrary",)),
    )(dqr, dqp, dkr, dvh, p, gq, gk, *rope, dp8)


def _ctx_bwd(dkc, dvc, pc, gk):
    def body(dkc_ref, dvc_ref, p_ref, gk_ref, dk_ref, dv_ref, ggk_ref):
        ones_bd = _head_ones()
        dk, gk_part = _head_norm_bwd(p_ref[:, 0:DA], gk_ref[...], _merge_heads(dkc_ref), ones_bd)
        dk_ref[...] = dk.astype(bf16)
        dv_ref[...] = _merge_heads(dvc_ref).astype(bf16)
        ggk_ref[...] = gk_part

    piece = SDS((L, DA), bf16)
    return _hbm_call(
        body, name="ctx_bwd", out_shape=(piece, piece, SDS((1, DA), f32)), in_specs=[VMEM_SPEC] * 4,
        out_specs=(VMEM_SPEC,) * 3,
    )(dkc, dvc, pc, gk)


def _grad_w_in(h, dp8, hc, dkc_raw, dvc_m, half, name, after=None):
    def body(half_ref, h_ref, p_ref, hc_ref, dk_ref, dv_ref, after_ref, g_ref):
        j = pl.program_id(0)
        hv = h_ref[...]
        g_ref[:, 0:DA] = _tn(hv, p_ref[0])
        g_ref[:, DA:2 * DA] = _tn(hv, p_ref[1])

        @pl.when(j == 0)
        def _():
            g_ref[:, DA:2 * DA] += _tn(hc_ref[...], dk_ref[...])

        @pl.when(j == 1)
        def _():
            g_ref[:, 0:DA] += _tn(hc_ref[...], dv_ref[...])

    fixed = lambda j, s: (0, 0)
    hd = D // 2
    grid_spec = pltpu.PrefetchScalarGridSpec(
        num_scalar_prefetch=1, grid=(4,),
        in_specs=[pl.BlockSpec((S, hd), lambda j, s: (0, s[0])), pl.BlockSpec((2, S, DA), lambda j, s: (j, 0, 0)),
                  pl.BlockSpec((L, hd), lambda j, s: (0, s[0])), pl.BlockSpec((L, DA), fixed),
                  pl.BlockSpec((L, DA), fixed), ANY_SPEC],
        out_specs=pl.BlockSpec((None, hd, D), lambda j, s: (j, 0, 0)))
    return _hbm_call(
        body, name=name, out_shape=SDS((4, hd, D), f32), grid_spec=grid_spec, compiler_params=_cp(40),
    )(half, h, dp8, hc, dkc_raw, dvc_m, half if after is None else after)


def _grad_w_in_pair_sum(h, dp8, hc, dkc_raw, dvc_m, half, sent, landed, ssem, rsem):
    def body(half_ref, h_ref, p_ref, hc_ref, dk_ref, dv_ref, sent_ref, land_ref, ssem_ref, rsem_ref,
             t32_ref, tb_ref, buf, lsem):
        j = pl.program_id(0)
        hv = h_ref[...]
        t32_ref[:, 0:DA] = _tn(hv, p_ref[0])
        x, y, c = _my_pos()
        arrival = pltpu.make_async_remote_copy(src_ref=sent_ref.at[j], dst_ref=land_ref.at[j], send_sem=ssem_ref.at[j],
                                               recv_sem=rsem_ref.at[j], device_id=(x, y, 1 - c), device_id_type=MESH)
        arrival.wait_recv()
        arrival.wait_send()
        load = pltpu.make_async_copy(land_ref.at[j], buf, lsem)
        load.start()
        t32_ref[:, DA:2 * DA] = _tn(hv, p_ref[1])

        @pl.when(j == 0)
        def _():
            t32_ref[:, DA:2 * DA] += _tn(hc_ref[...], dk_ref[...])

        @pl.when(j == 1)
        def _():
            t32_ref[:, 0:DA] += _tn(hc_ref[...], dv_ref[...])

        load.wait()
        t = t32_ref[...] + buf[...]
        t32_ref[...] = t
        tb_ref[...] = t.astype(bf16)

    fixed = lambda j, s: (0, 0)
    hd = D // 2
    out_spec = pl.BlockSpec((None, hd, D), lambda j, s: (j, 0, 0))
    grid_spec = pltpu.PrefetchScalarGridSpec(
        num_scalar_prefetch=1, grid=(4,),
        in_specs=[pl.BlockSpec((S, hd), lambda j, s: (0, s[0])), pl.BlockSpec((2, S, DA), lambda j, s: (j, 0, 0)),
                  pl.BlockSpec((L, hd), lambda j, s: (0, s[0])), pl.BlockSpec((L, DA), fixed),
                  pl.BlockSpec((L, DA), fixed), HBM_SPEC, HBM_SPEC, SEM_SPEC, SEM_SPEC],
        out_specs=(out_spec, out_spec), scratch_shapes=[pltpu.VMEM((hd, D), f32), pltpu.SemaphoreType.DMA])
    return _hbm_call(
        body, name="grad_w_in_pair_sum", out_shape=(SDS((4, hd, D), f32), SDS((4, hd, D), bf16)), grid_spec=grid_spec,
        compiler_params=_cp(40, has_side_effects=DATAFLOW),
    )(half, h, dp8, hc, dkc_raw, dvc_m, sent, landed, ssem, rsem)


def _norm_mod_bwd(x, dh, g, scale):
    r = lax.rsqrt(jnp.mean(x * x, axis=-1, keepdims=True) + RMS_EPS)
    xh = x * r
    y = xh * g
    dshift = jnp.sum(dh, axis=0, keepdims=True)
    dscale = jnp.sum(dh * y, axis=0, keepdims=True)
    dyn = dh * (1.0 + scale)
    dg = jnp.sum(dyn * xh, axis=0, keepdims=True)
    gdy = dyn * g
    dx = r * (gdy - xh * jnp.mean(xh * gdy, axis=-1, keepdims=True))
    return dx, dshift, dscale, dg


def _dh_grad_x(dp8, w4, xx, dy, norm_g, mrow, b_ada, after=None):
    tm = 512

    def body(p_ref, w_ref, x_ref, dy_ref, g_ref, m_ref, b_ref, after_ref, gx_ref, dsh_ref, dsc_ref, dg_ref):
        @pl.when(pl.program_id(0) == 0)
        def _():
            dsh_ref[...] = jnp.zeros_like(dsh_ref)
            dsc_ref[...] = jnp.zeros_like(dsc_ref)
            dg_ref[...] = jnp.zeros_like(dg_ref)

        dh = None
        for j in range(4):
            for half in range(2):
                term = _tt(p_ref[2 * j + half], w_ref[j, :, half * DA:(half + 1) * DA])
                dh = term if dh is None else dh + term
        scale = m_ref[:, D:2 * D] + b_ref[:, D:2 * D]
        dx, dshift, dscale, dg = _norm_mod_bwd(x_ref[...], dh, g_ref[...], scale)
        gx_ref[...] = dy_ref[...] + dx
        dsh_ref[...] += dshift
        dsc_ref[...] += dscale
        dg_ref[...] += dg

    row = lambda i: (i, 0)
    fixed = lambda i: (0, 0)
    vec = SDS((1, D), f32)
    return _hbm_call(
        body, name="dh_grad_x", out_shape=(SDS((S, D), f32), vec, vec, vec), grid=(S // tm,),
        in_specs=[pl.BlockSpec((8, tm, DA), lambda i: (0, i, 0)), pl.BlockSpec((4, D, D), lambda i: (0, 0, 0)),
                  pl.BlockSpec((tm, D), row), pl.BlockSpec((tm, D), row), pl.BlockSpec((1, D), fixed),
                  pl.BlockSpec((1, 3 * D), fixed), pl.BlockSpec((1, 3 * D), fixed), ANY_SPEC],
        out_specs=(pl.BlockSpec((tm, D), row), pl.BlockSpec((1, D), fixed), pl.BlockSpec((1, D), fixed),
                   pl.BlockSpec((1, D), fixed)),
        compiler_params=_cp(56, dimension_semantics=("arbitrary",)),
    )(dp8, w4, xx, dy, norm_g, mrow, b_ada, b_ada if after is None else after)


def _dhc_sums(dkc_raw, dvc_m, w4, ctx2, norm_g, mrow_c, b_ada, after=None):
    def body(dk_ref, dv_ref, w0_ref, w1_ref, x_ref, g_ref, m_ref, b_ref, after_ref, dsh_ref, dsc_ref, dg_ref):
        dh = _tt(dk_ref[...], w0_ref[:, DA:2 * DA]) + _tt(dv_ref[...], w1_ref[:, 0:DA])
        scale = m_ref[:, D:2 * D] + b_ref[:, D:2 * D]
        _, dshift, dscale, dg = _norm_mod_bwd(x_ref[...], dh, g_ref[...], scale)
        dsh_ref[...] = dshift
        dsc_ref[...] = dscale
        dg_ref[...] = dg

    fixed = lambda i: (0, 0)
    vec = SDS((1, D), f32)
    vspec = pl.BlockSpec((1, D), fixed)
    return _hbm_call(
        body, name="dhc_sums", out_shape=(vec, vec, vec), grid=(1,),
        in_specs=[pl.BlockSpec((L, DA), fixed), pl.BlockSpec((L, DA), fixed),
                  pl.BlockSpec((None, D, D), lambda i: (0, 0, 0)), pl.BlockSpec((None, D, D), lambda i: (1, 0, 0)),
                  pl.BlockSpec((L, D), fixed), vspec, pl.BlockSpec((1, 3 * D), fixed), pl.BlockSpec((1, 3 * D), fixed),
                  ANY_SPEC],
        out_specs=(vspec, vspec, vspec), compiler_params=_cp(32),
    )(dkc_raw, dvc_m, w4, w4, ctx2, norm_g, mrow_c, b_ada, b_ada if after is None else after)


def _rope_tables():
    nf = DH // 4
    inv = np.float32(ROPE_THETA) ** (-np.arange(nf, dtype=np.float32) / np.float32(nf))
    ang_c = np.arange(GW, dtype=np.float32)[:, None] * inv
    ang_r = np.arange(ROWS, dtype=np.float32)[:, None] * inv
    zc, zr = np.zeros((GW, 2 * nf), np.float32), np.zeros((ROWS, 2 * nf), np.float32)
    ct_cos = np.tile(np.concatenate([zc, np.cos(ang_c), np.cos(ang_c)], axis=1), (1, H))
    ct_sin = np.tile(np.concatenate([zc, -np.sin(ang_c), np.sin(ang_c)], axis=1), (1, H))
    rt_cos = np.tile(np.concatenate([np.cos(ang_r), np.cos(ang_r), zr], axis=1), (1, H))
    rt_sin = np.tile(np.concatenate([-np.sin(ang_r), np.sin(ang_r), zr], axis=1), (1, H))
    rep8 = lambda t: np.ascontiguousarray(np.broadcast_to(t[:, None, :], (ROWS, 8, DA))).reshape(ROWS * 8, DA)
    return tuple(jnp.asarray(t, f32) for t in (ct_cos, rep8(rt_cos), ct_sin, rep8(rt_sin)))


def _local_step(xx, ctx2, tgt, mrow, mrow_c, b_ada, norm_g, weights, q_norm_g, k_norm_g, rpb2, conv_w_full, conv_b,
                hooks=None):
    hooks = hooks or {}
    gq = jnp.tile(q_norm_g, (1, H))
    gk = jnp.tile(k_norm_g, (1, H))
    rope = _rope_tables()

    h = _prenorm(xx, norm_g, mrow, b_ada, 256, "prenorm_x", after=weights.get("started"))
    jv = weights["jvec"]
    p = _in_proj_own(h, weights["own"], jv)
    btb = _bias_tiles(rpb2, after=p)
    w4, started = weights["near"](btb)
    p = _in_proj_block(h, w4, p, weights["first"], "in_proj_near", after=started)
    w4 = weights["near2"](w4, p)
    p = _in_proj_block(h, w4, p, weights["second"], "in_proj_near2")
    ctx_norm = {}

    def filler(token):
        ctx_norm["hc"] = _prenorm(ctx2, norm_g, mrow_c, b_ada, L, "prenorm_ctx", after=token)
        return ctx_norm["hc"]

    w4, started = weights["far"](w4, p, filler)
    hc = ctx_norm["hc"]
    p = _in_proj_block(h, w4, p, jv ^ 3, "in_proj_far", after=started)
    pc = _ctx_proj(hc, w4)
    qr, qp, kr, vh = _qk_prep(p, gq, gk, rope)
    kc, vc = _ctx_prep(pc, gk)
    o = _attn_fwd(qr, qp, kr, vh, kc, vc, btb)
    started = weights["out_arrived"](o) if "out_arrived" in weights else None
    conv_g = _conv_fwd(p, conv_w_full, conv_b, after=started)
    w_out_full = weights["out"](conv_g)
    dy, dconv, dp8, do, g_w_out, dgate, loss_sum = _out_proj_loss(o, p, conv_g, w_out_full, xx, tgt, mrow, b_ada)
    started = hooks["g_w_out"](g_w_out) if "g_w_out" in hooks else None
    dp8, g_conv_w, g_conv_b = _conv_bwd(dconv, p, conv_w_full, conv_b, dp8, after=started)
    started = hooks["after_conv"](dp8) if "after_conv" in hooks else None
    dqr, dqp, dkr, dvh, dkc, dvc, dbtb = _attn_bwd(qr, qp, kr, vh, kc, vc, btb, do, after=started)
    dp8, g_gq, g_gk = _qk_bwd(dqr, dqp, dkr, dvh, p, gq, gk, rope, dp8)
    dkc_raw, dvc_m, g_gk_c = _ctx_bwd(dkc, dvc, pc, gk)
    first = hooks.get("first_half", jnp.zeros((1,), i32))
    g_first = _grad_w_in(h, dp8, hc, dkc_raw, dvc_m, first, "grad_w_in_first")
    started = hooks["g_w_in_first"](g_first) if "g_w_in_first" in hooks else None
    if "w_in_pair_sum" in hooks:
        g_second = None
        started = hooks["w_in_pair_sum"](functools.partial(_grad_w_in_pair_sum, h, dp8, hc, dkc_raw, dvc_m, 1 - first))
    else:
        g_second = _grad_w_in(h, dp8, hc, dkc_raw, dvc_m, 1 - first, "grad_w_in_second", after=started)
    dshift_c, dscale_c, dng_c = _dhc_sums(dkc_raw, dvc_m, w4, ctx2, norm_g, mrow_c, b_ada, after=started)
    g_rpb = _bias_bwd(dbtb, after=dshift_c)
    grad_x, dshift, dscale, dng = _dh_grad_x(dp8, w4, xx, dy, norm_g, mrow, b_ada, after=g_rpb)
    return dict(loss_sum=loss_sum, grad_x=grad_x, g_w_in=(g_first, g_second), g_w_out=g_w_out, g_conv_w=g_conv_w,
                g_conv_b=g_conv_b, g_rpb=g_rpb, g_gq=g_gq, g_gk=g_gk, g_gk_c=g_gk_c, dshift=dshift, dscale=dscale,
                dgate=dgate, dng=dng, dshift_c=dshift_c, dscale_c=dscale_c, dng_c=dng_c)


def _pair_sum_w_out(g, r, cvec):
    hr = D // 8

    def body(c_ref, g0, g1, g2, g3, r_ref, t32_ref, tb_ref):
        for q, g_ref in enumerate((g0, g1, g2, g3)):
            t = g_ref[...] + r_ref[q]
            t32_ref[q] = t
            tb_ref[q] = t.astype(bf16)

    gspecs = [pl.BlockSpec((hr, D), lambda i, c, q=q: (2 * q + c[0], 0)) for q in range(4)]
    full = pl.BlockSpec((4, hr, D), lambda i, c: (0, 0, 0))
    grid_spec = pltpu.PrefetchScalarGridSpec(num_scalar_prefetch=1, grid=(1,), in_specs=gspecs + [full],
                                             out_specs=(full, full))
    return _hbm_call(body, name="pair_sum_w_out", out_shape=(SDS((4, hr, D), f32), SDS((4, hr, D), bf16)),
                          grid_spec=grid_spec)(cvec, g, g, g, g, r)


def _chip_sum(t32, r2, jvec, name):
    rows = t32.shape[1]
    tr = min(rows, 128)

    def body(j_ref, t_ref, r_ref, u_ref):
        u_ref[...] = ((t_ref[...] + r_ref[0].astype(f32)) + r_ref[1].astype(f32)) + r_ref[2].astype(f32)

    grid_spec = pltpu.PrefetchScalarGridSpec(
        num_scalar_prefetch=1, grid=(rows // tr,),
        in_specs=[pl.BlockSpec((None, tr, D), lambda i, j: (j[0], i, 0)),
                  pl.BlockSpec((3, tr, D), lambda i, j: (0, i, 0))],
        out_specs=pl.BlockSpec((tr, D), lambda i, j: (i, 0)))
    return _hbm_call(body, name=name, out_shape=SDS((rows, D), f32), grid_spec=grid_spec)(jvec, t32, r2)


_PK = {}
_off = 0
for _name, _rows in (("dm", 24), ("dmc", 24), ("dng", 8), ("dng_c", 8), ("gq", 8), ("gk", 8), ("gk_c", 8),
                     ("rpb", H * N_DR), ("conv_b", 8), ("conv_w", 16), ("loss", 8)):
    _PK[_name] = (_off, _off + _rows)
    _off += _rows
PK_ROWS = _off
RS_B_ADA, RS_NORM_G, RS_GQ, RS_GK, RS_RPB, RS_CONV_B, RS_CONV_W, RS_DMC, RS_LOSS, RS_ROWS = (
    0, 24, 32, 40, 48, 168, 176, 192, 216, 224)


def _small_reduce(gathered):
    def body(g_ref, o_ref, dm_ref):
        a0 = _PK["dm"][0]
        dm_ref[...] = jnp.zeros_like(dm_ref)
        for b in range(8):
            for i in range(24):
                dm_ref[b:b + 1, 128 * i:128 * (i + 1)] = g_ref[b, a0 + i:a0 + i + 1, :]
        tot = g_ref[0]
        for b in range(1, 8):
            tot = tot + g_ref[b]

        def rows(name):
            a, z = _PK[name]
            return tot[a:z]

        o_ref[RS_B_ADA:RS_B_ADA + 24] = rows("dm") + rows("dmc")
        o_ref[RS_NORM_G:RS_NORM_G + 8] = rows("dng") + rows("dng_c")
        gq = jnp.broadcast_to(jnp.sum(rows("gq"), axis=0, keepdims=True), (8, 128))
        gk = jnp.broadcast_to(jnp.sum(rows("gk") + rows("gk_c"), axis=0, keepdims=True), (8, 128))
        o_ref[RS_GQ:RS_GQ + 8] = gq + pltpu.roll(gq, DH, 1)
        o_ref[RS_GK:RS_GK + 8] = gk + pltpu.roll(gk, DH, 1)
        o_ref[RS_RPB:RS_RPB + H * N_DR] = rows("rpb")
        o_ref[RS_CONV_B:RS_CONV_B + 8] = rows("conv_b")
        o_ref[RS_CONV_W:RS_CONV_W + 16] = rows("conv_w")
        dmc = rows("dmc")
        o_ref[RS_DMC:RS_DMC + 24] = dmc
        o_ref[RS_LOSS:RS_LOSS + 8] = rows("loss")
        for i in range(24):
            dm_ref[8:9, 128 * i:128 * (i + 1)] = dmc[i:i + 1]

    return _hbm_call(body, name="small_reduce", out_shape=(SDS((RS_ROWS, 128), f32), SDS((16, 3 * D), f32)),
                     in_specs=[VMEM_SPEC], out_specs=(VMEM_SPEC, VMEM_SPEC))(gathered)


def _w_ada_grad(sc16, dm16, w_ada_shard, jvec):
    ncol = w_ada_shard.shape[1]

    def body(j_ref, sc_ref, dm_ref, w_ref, g_ref, part_ref):
        dm = dm_ref[...]
        g_ref[...] = lax.dot_general(sc_ref[...], dm, (((0,), (0,)), ((), ())), precision=HIGHEST,
                                     preferred_element_type=f32)
        part_ref[...] = lax.dot_general(dm[8:16], w_ref[...], (((1,), (1,)), ((), ())), precision=HIGHEST,
                                        preferred_element_type=f32)

    fixed = lambda i, j: (0, 0)
    grid_spec = pltpu.PrefetchScalarGridSpec(
        num_scalar_prefetch=1, grid=(1,),
        in_specs=[pl.BlockSpec((16, D), fixed), pl.BlockSpec((16, ncol), lambda i, j: (0, j[0])),
                  pl.BlockSpec((D, ncol), fixed)],
        out_specs=(pl.BlockSpec((D, ncol), fixed), pl.BlockSpec((8, D), fixed)))
    return _pallas_call(body, name="w_ada_grad", out_shape=(SDS((D, ncol), f32), SDS((8, D), f32)),
                        grid_spec=grid_spec, compiler_params=_cp(40))(jvec, sc16, dm16, w_ada_shard)


def _c_ctx_grad(parts4, c_ctx):
    def body(p_ref, c_ref, o_ref):
        tot = ((p_ref[0] + p_ref[1]) + p_ref[2]) + p_ref[3]
        o_ref[...] = tot[0:1] * _dsilu(c_ref[...].reshape(1, D))

    return _pallas_call(body, name="c_ctx_grad", out_shape=SDS((1, D), f32), in_specs=[VMEM_SPEC, VMEM_SPEC],
                        out_specs=VMEM_SPEC)(parts4, c_ctx)


def _adamw(w, g, m, v, name, after=None):
    return _adamw_stream(w, [g], m, v, None, name, after)


def _adamw_halves(w, g_mine, g_other, m, v, cvec, name, after=None):
    return _adamw_stream(w, [g_mine, g_other], m, v, cvec, name, after)


def _adamw_stream(w, g_parts, m, v, cvec, name, after):
    rows, cols = w.shape
    parts = len(g_parts)
    span = rows // parts
    chunk = min(128, span // 2)
    per = span // chunk
    outs_g = parts == 2
    pin = rows * cols * 4 > (1 << 20)

    def body(*refs):
        c_ref, w_ref = refs[0], refs[1]
        g_refs = refs[2:2 + parts]
        m_ref, v_ref = refs[2 + parts], refs[3 + parts]
        out_refs = refs[5 + parts:5 + parts + 3 + outs_g]
        bw, bg, bm, bv, bd, sem_in, sem_out = refs[5 + parts + 3 + outs_g:]
        c = c_ref[0]
        loads, stores = [], []
        for p in range(parts):
            first = (c if p == 0 else 1 - c) * span if parts == 2 else 0
            for kk in range(per):
                r = pl.ds(pl.multiple_of(first + kk * chunk, chunk), chunk)
                g_src = g_refs[p].at[kk * chunk:(kk + 1) * chunk]
                cps = [pltpu.make_async_copy(src, dst.at[r], sem_in.at[a, p, kk])
                       for a, (src, dst) in enumerate(((w_ref.at[r], bw), (g_src, bg), (m_ref.at[r], bm),
                                                       (v_ref.at[r], bv)))]
                for cp in cps:
                    cp.start()
                loads.append((r, p, kk, cps))
        for r, p, kk, cps in loads:
            for cp in cps:
                cp.wait()
            bd[r], bm[r], bv[r] = _adam_math(bw[r], bg[r], bm[r], bv[r])
            for a, (src, dst) in enumerate(zip(([bg] if outs_g else []) + [bd, bm, bv], out_refs)):
                cp = pltpu.make_async_copy(src.at[r], dst.at[r], sem_out.at[a, p, kk])
                cp.start()
                stores.append(cp)
        for cp in stores:
            cp.wait()

    shp = pltpu.HBM((rows, cols), f32) if pin else SDS((rows, cols), f32)
    spec = HBM_SPEC if pin else ANY_SPEC
    buf = pltpu.VMEM((rows, cols), f32)
    args = [pltpu.with_memory_space_constraint(a, pltpu.HBM) if pin else a for a in (w, *g_parts, m, v)]
    cvec = jnp.zeros((1,), i32) if cvec is None else cvec
    after_big = after is not None and after.size * 4 > (1 << 20)
    after = pltpu.with_memory_space_constraint(after, pltpu.HBM) if after_big else after
    return _pallas_call(
        body, name=name, out_shape=(shp,) * (3 + outs_g),
        in_specs=[SMEM_SPEC] + [spec] * (3 + parts) + [HBM_SPEC if after_big else ANY_SPEC],
        out_specs=(spec,) * (3 + outs_g),
        scratch_shapes=[buf] * 5 + [pltpu.SemaphoreType.DMA((4, parts, per)), pltpu.SemaphoreType.DMA((4, parts, per))],
        compiler_params=_cp(5 * rows * cols * 4 // (1 << 20) + 8),
    )(cvec, *args, cvec if after is None else after)


def _adam_math(w, g, m, v):
    m2 = ADAM_B1 * m + (1.0 - ADAM_B1) * g
    v2 = ADAM_B2 * v + (1.0 - ADAM_B2) * jnp.square(g)
    m_hat = m2 / (1.0 - ADAM_B1 ** ADAM_STEP)
    v_hat = v2 / (1.0 - ADAM_B2 ** ADAM_STEP)
    return -ADAM_LR * (m_hat / (jnp.sqrt(v_hat) + ADAM_EPS) + ADAM_WD * w), m2, v2


def _adamw_small(red, g_c_ctx, jvec, ws, ms, vs):
    n = len(ws)

    def body(*refs):
        red_ref, gc_ref, j_ref = refs[:3]
        w_refs, m_refs, v_refs = refs[3:3 + n], refs[3 + n:3 + 2 * n], refs[3 + 2 * n:3 + 3 * n]
        outs = refs[3 + 3 * n:]
        g_out, d_out, m_out, v_out = outs[:n], outs[n:2 * n], outs[2 * n:3 * n], outs[3 * n:]
        chip = j_ref[0]
        lanes = lambda i: (slice(None), slice(128 * i, 128 * (i + 1)))
        row = lambda r0, i: (lambda: red_ref[r0 + i:r0 + i + 1, :])
        whole = (slice(None), slice(None))
        chunks = [
            [((slice(None),), lambda: gc_ref[...].reshape(D))],
            [(lanes(i), row(RS_B_ADA, i)) for i in range(3 * D // 128)],
            [(lanes(i), row(RS_NORM_G, i)) for i in range(D // 128)],
            [(whole, lambda: red_ref[RS_GQ:RS_GQ + 1, 0:DH])],
            [(whole, lambda: red_ref[RS_GK:RS_GK + 1, 0:DH])],
            [((dr,), (lambda dr=dr: red_ref[pl.ds(RS_RPB + dr, H, stride=N_DR), 0:N_DC])) for dr in range(N_DR)],
            [((r,), (lambda r=r: red_ref[pl.ds(RS_CONV_W + 4 * r + chip, 1), :])) for r in range(3)],
            [(lanes(i), row(RS_CONV_B, i)) for i in range(DC // 128)],
        ]
        for a in range(n):
            for idx, grad in chunks[a]:
                g = grad()
                d, m2, v2 = _adam_math(w_refs[a][idx], g, m_refs[a][idx], v_refs[a][idx])
                g_out[a][idx] = g
                d_out[a][idx] = d
                m_out[a][idx] = m2
                v_out[a][idx] = v2

    shapes = [SDS(w.shape, f32) for w in ws]
    res = _pallas_call(body, name="adamw_small", out_shape=shapes * 4,
                       in_specs=[VMEM_SPEC, VMEM_SPEC, SMEM_SPEC] + [VMEM_SPEC] * (3 * n),
                       out_specs=[VMEM_SPEC] * (4 * n))(red, g_c_ctx, jvec, *ws, *ms, *vs)
    return [list(res[k * n:(k + 1) * n]) for k in range(4)]


def _rows128(a):
    return a.reshape(-1, 128)


def kernel(x, c, ctx, c_ctx, w_ada, b_ada, norm_g, w_in, q_norm_g, k_norm_g, rpb, conv_w, conv_b, w_out, loss_target, m_c_ctx, m_w_ada, m_b_ada, m_norm_g, m_w_in, m_q_norm_g, m_k_norm_g, m_rpb, m_conv_w, m_conv_b, m_w_out, v_c_ctx, v_w_ada, v_b_ada, v_norm_g, v_w_in, v_q_norm_g, v_k_norm_g, v_rpb, v_conv_w, v_conv_b, v_w_out):
    xi, yi, ci = lax.axis_index("x"), lax.axis_index("y"), lax.axis_index("c")
    dev = 4 * xi + 2 * yi + ci
    chip = 2 * xi + yi
    cvec = jnp.reshape(ci, (1,)).astype(i32)
    jvec = jnp.reshape(chip, (1,)).astype(i32)
    w_ada_s = w_ada[0]
    ncol = w_ada_s.shape[1]

    gc = _split_start([_to_slot(c.reshape(8, 128), 8, dev)], [], 7, _gather8_copies, "gather_c_start")
    wo4c = _cast_to_slot(w_out[0], jvec, "cast_w_out", after=gc[3])
    w4c = _cast_to_slot(w_in[0], jvec, "cast_w_in", after=wo4c)
    (c8,), _ = _split_wait(gc[0], gc[1], [gc[2]], [], w4c, _gather8_copies, "gather_c_wait")
    cc = jnp.concatenate([c8.reshape(8, D), c_ctx.reshape(1, D), jnp.zeros((7, D), f32)], axis=0)
    m_shard, sc16 = _adaln_shard(cc, w_ada_s)

    conv_w_pad = jnp.pad(conv_w[0], ((0, 5), (0, 0)))
    gm = _split_start([_to_slot(m_shard, 4, chip), _to_slot(conv_w_pad, 4, chip)], [], 6, _gather4_copies,
                      "gather_mod_start")

    all_k = [(0, 0, 0), (0, 0, 1), (0, 0, 2)]
    sem_a, rem_a, w4s, token = _split_start([w4c], [], 1, _near_copies, "weights_near_start", after=gm[4])
    (m4, cw4), _ = _split_wait(gm[0], gm[1], [gm[2], gm[3]], [], token, _gather4_copies, "gather_mod_wait")
    m_full = jnp.transpose(m4, (1, 0, 2)).reshape(16, 4 * ncol)
    mrow = lax.dynamic_slice(m_full, (dev, 0), (1, 3 * D))
    mrow_c = m_full[8:9]
    conv_w_full = jnp.transpose(cw4[:, 0:3, :], (1, 0, 2)).reshape(3, DC)
    waves = {}

    def near(after):
        (w4w,), _ = _split_wait(sem_a, rem_a, [w4s], [], after, _near_copies, "weights_near_wait")
        sem_b, rem_b, w4b, started = _split_start([w4w], [], 2, _pass_relay_copies, "weights_pass_start")
        waves["pass"] = (sem_b, rem_b)
        return w4b, started

    def near2(w4, after):
        (w4w,), _ = _split_wait(*waves["pass"], [w4], [], after, _pass_copy, "weights_pass_wait")
        return w4w

    def far(w4, after, filler):
        (w4w,), _ = _split_wait(*waves["pass"], [w4], [], after, _relay_copy, "weights_far_wait")
        w4x, sem_c, rem_c, wo4s, started = _forward_then_start(w4w, wo4c, all_k, "weights_far_forward_out_start")
        (w4f,), _ = _split_wait(sem_c, rem_c, [w4x], [], filler(started), _diag_forward_copy,
                                "weights_far_forward_wait")
        waves["out"] = (sem_c, rem_c, wo4s)
        return w4f, started

    def w_out_arrived(after):
        sem_c, rem_c, wo4s = waves["out"]
        (wow,) = _halves_wait(sem_c, rem_c, [wo4s], after, all_k, "weights_out_wait")
        sem_d, rem_d, wof, started = _split_start([wow], [], 3, _forward_copies, "weights_out_forward_start")
        waves["out_forward"] = (sem_d, rem_d, wof)
        return started

    def w_out_gathered(after):
        sem_d, rem_d, wof = waves["out_forward"]
        (wo,), _ = _split_wait(sem_d, rem_d, [wof], [], after, _forward_copies, "weights_out_forward_wait")
        return wo.reshape(D, D)

    weights = dict(own=w_in[0], jvec=jvec, started=token, first=jvec ^ (1 + cvec), second=jvec ^ (2 - cvec),
                   near=near, near2=near2, far=far, out_arrived=w_out_arrived, out=w_out_gathered)

    exchange = _exchange_copies
    pending = {}

    def on_g_w_out(g_w_out):
        out = _split_start([g_w_out], [SDS((4, D // 8, D), f32)], 4, exchange, "grad_out_pair_start")
        pending["ex_out"] = out
        return out[4]

    def after_conv(dp8):
        ssem_o, rsem_o, g_o, land_o, _ = pending["ex_out"]
        (g_o,), (ex_o,) = _split_wait(ssem_o, rsem_o, [g_o], [land_o], dp8, exchange, "grad_out_pair_wait")
        to32, tob = _pair_sum_w_out(g_o, ex_o, cvec)
        out = _split_start([tob], [SDS((3, D // 8, D), bf16)], 3, _scatter_copies, "grad_out_chip_start")
        pending["sc_out"] = (out, to32)
        return out[4]

    def on_g_w_in_first(g_first):
        out = _split_start([g_first], [SDS((4, D // 2, D), f32)], 4, _block_exchange_copies, "grad_pair_start")
        pending["ex"] = (out[0], out[1], [out[2]], [out[3]])
        return out[4]

    def on_w_in_pair_sum(pair_sum):
        ex_ssem, ex_rsem, ex_srcs, ex_lands = pending["ex"]
        t32, tb = pair_sum(ex_srcs[0], ex_lands[0], ex_ssem, ex_rsem)
        out = _split_start([tb], [SDS((3, D // 2, D), bf16)], 3, _scatter_copies, "grad_chip_start")
        pending["sc_in"] = (out, t32)
        return out[4]

    r = _local_step(x[0], ctx[0], loss_target[0], mrow, mrow_c, b_ada, norm_g, weights, q_norm_g, k_norm_g,
                    rpb[0], conv_w_full, conv_b,
                    dict(g_w_out=on_g_w_out, after_conv=after_conv, first_half=1 - cvec,
                         g_w_in_first=on_g_w_in_first, w_in_pair_sum=on_w_in_pair_sum))
    sc_in, t32 = pending["sc_in"]
    _, (r2,) = _split_wait(sc_in[0], sc_in[1], [sc_in[2]], [sc_in[3]], r["dng"], _scatter_copies, "grad_chip_wait")
    u_in = _chip_sum(t32, r2, jvec, "chip_sum_w_in")
    sc_o, to32 = pending["sc_out"]
    _, (ro2,) = _split_wait(sc_o[0], sc_o[1], [sc_o[2]], [sc_o[3]], u_in, _scatter_copies, "grad_out_chip_wait")
    u_out = _chip_sum(to32, ro2, jvec, "chip_sum_w_out")
    swap = _split_start([u_in, u_out], [SDS(u_in.shape, f32), SDS(u_out.shape, f32)], 2, _swap_copies,
                        "grad_pair_swap_start")

    dm = jnp.concatenate([r["dshift"], r["dscale"], r["dgate"]], axis=1)
    dmc = jnp.concatenate([r["dshift_c"], r["dscale_c"], jnp.zeros((1, D), f32)], axis=1)
    pack_parts = [_rows128(dm), _rows128(dmc), _rows128(r["dng"]), _rows128(r["dng_c"]), _rows128(r["g_gq"]),
                  _rows128(r["g_gk"]), _rows128(r["g_gk_c"]), r["g_rpb"].reshape(H * N_DR, 128),
                  _rows128(r["g_conv_b"]), _rows128(r["g_conv_w"][0:3]), jnp.pad(r["loss_sum"], ((0, 0), (0, 127)))]
    pack = jnp.concatenate([jnp.pad(p, ((0, -p.shape[0] % 8), (0, 0))) for p in pack_parts], axis=0)
    assert pack.shape[0] == PK_ROWS
    gs = _split_start([_to_slot(pack, 8, dev)], [], 7, _gather8_copies, "gather_small_start", after=swap[6])
    (u_in, u_out), (o_in, o_out) = _split_wait(swap[0], swap[1], swap[2:4], swap[4:6], gs[3], _swap_copies,
                                               "grad_pair_swap_wait")
    g_w_in_s, d_w_in, nm_w_in, nv_w_in = _adamw_halves(w_in[0], u_in, o_in, m_w_in[0], v_w_in[0], cvec, "adamw_w_in",
                                                       after=gs[3])
    g_w_out_s, d_w_out, nm_w_out, nv_w_out = _adamw_halves(w_out[0], u_out, o_out, m_w_out[0], v_w_out[0], cvec,
                                                           "adamw_w_out", after=nm_w_in)
    (gathered,), _ = _split_wait(gs[0], gs[1], [gs[2]], [], nm_w_out, _gather8_copies, "gather_small_wait")
    red, dm16 = _small_reduce(gathered)
    loss = red[RS_LOSS, 0] * (0.5 / D)

    g_w_ada_s, cpart = _w_ada_grad(sc16, dm16, w_ada_s, jvec)
    gcp = _split_start([_to_slot(cpart, 4, chip)], [], 3, _gather4_copies, "gather_c_ctx_parts_start")
    d_w_ada, nm_w_ada, nv_w_ada = _adamw(w_ada_s, g_w_ada_s, m_w_ada[0], v_w_ada[0], "adamw_w_ada", after=gcp[3])
    (cparts4,), _ = _split_wait(gcp[0], gcp[1], [gcp[2]], [], nm_w_ada, _gather4_copies, "gather_c_ctx_parts_wait")
    g_c_ctx = _c_ctx_grad(cparts4, c_ctx)

    t_rpb = lambda a: jnp.transpose(a, (0, 2, 1, 3)).reshape(N_DR, H, N_DC)
    t_cw = lambda a: jnp.transpose(a, (1, 0, 2))
    small = _adamw_small(
        red, g_c_ctx, jvec,
        [c_ctx, b_ada, norm_g, q_norm_g, k_norm_g, t_rpb(rpb), t_cw(conv_w), conv_b],
        [m_c_ctx, m_b_ada, m_norm_g, m_q_norm_g, m_k_norm_g, t_rpb(m_rpb), t_cw(m_conv_w), m_conv_b],
        [v_c_ctx, v_b_ada, v_norm_g, v_q_norm_g, v_k_norm_g, t_rpb(v_rpb), t_cw(v_conv_w), v_conv_b])
    for kind in small:
        kind[5] = jnp.transpose(kind[5].reshape(1, N_DR, H, N_DC), (0, 2, 1, 3))
        kind[6] = jnp.transpose(kind[6], (1, 0, 2))

    def ordered(kind, big_w_ada, big_w_in, big_w_out):
        s_c_ctx, s_b_ada, s_norm_g, s_q, s_k, s_rpb, s_conv_w, s_conv_b = small[kind]
        return [s_c_ctx, big_w_ada[None], s_b_ada, s_norm_g, big_w_in[None], s_q, s_k, s_rpb, s_conv_w,
                s_conv_b, big_w_out[None]]

    grads = ordered(0, g_w_ada_s, g_w_in_s, g_w_out_s)
    deltas = ordered(1, d_w_ada, d_w_in, d_w_out)
    new_m = ordered(2, nm_w_ada, nm_w_in, nm_w_out)
    new_v = ordered(3, nv_w_ada, nv_w_in, nv_w_out)
    return (loss, r["grad_x"][None], *grads, *deltas, *new_m, *new_v)
```

```python
import functools

import jax
import jax.numpy as jnp
import numpy as np
from jax import lax
from jax.experimental import pallas as pl
from jax.experimental.pallas import tpu as pltpu

f32, bf16, i32 = jnp.float32, jnp.bfloat16, jnp.int32
MESH = pl.DeviceIdType.MESH
HIGHEST = lax.Precision.HIGHEST

D = 1024
S = 2048
L = 256
GW = 64
ROWS = S // GW
H = 8
DH = 64
DA = H * DH
DC = 512
WIN_H, WIN_W = 8, 16
N_DR, N_DC = 2 * WIN_H - 1, 2 * WIN_W - 1
RMS_EPS = 1e-6
ROPE_THETA = 10000.0
QK_SCALE = DH ** -0.5
NEG = -1e30

QB = 128
NQB = S // QB
KR = 9
KB = KR * GW
TILE_GEOM = ((0, 0), (2, 0), (4, 0), (28, 23), (30, 23))
NT = len(TILE_GEOM)

ADAM_LR, ADAM_B1, ADAM_B2, ADAM_EPS, ADAM_WD, ADAM_STEP = 0.001, 0.9, 0.999, 1e-08, 0.01, 10

VMEM_SPEC = pl.BlockSpec(memory_space=pltpu.VMEM)
ANY_SPEC = pl.BlockSpec(memory_space=pl.ANY)
SMEM_SPEC = pl.BlockSpec(memory_space=pltpu.SMEM)
SDS = jax.ShapeDtypeStruct


_pallas_call = pl.pallas_call


def _hbm_call(body, *, out_shape, in_specs=None, out_specs=None, grid_spec=None, **kw):
    n_pre = 0
    if grid_spec is not None:
        ispecs, ospecs, n_pre = grid_spec.in_specs, grid_spec.out_specs, grid_spec.num_scalar_prefetch
        kw["grid_spec"] = grid_spec
    else:
        ispecs, ospecs = in_specs, out_specs
        kw.update(in_specs=in_specs, out_specs=out_specs)

    def blocked(spec):
        return isinstance(spec, pl.BlockSpec) and spec.block_shape is not None

    single = not isinstance(out_shape, (tuple, list))
    shapes = [out_shape] if single else list(out_shape)
    ospec_list = list(ospecs) if isinstance(ospecs, (tuple, list)) else [ospecs]
    shapes = [pltpu.HBM(s.shape, s.dtype) if blocked(sp) else s for s, sp in zip(shapes, ospec_list)]
    call = _pallas_call(body, out_shape=shapes[0] if single else tuple(shapes), **kw)

    def run(*args):
        arrays = [pltpu.with_memory_space_constraint(a, pltpu.HBM) if blocked(sp) else a
                  for a, sp in zip(args[n_pre:], ispecs)]
        return call(*args[:n_pre], *arrays)

    return run


def _cp(vmem_mb=None, **kw):
    if vmem_mb is not None:
        kw["vmem_limit_bytes"] = vmem_mb << 20
    return pltpu.CompilerParams(**kw)


def _silu(z):
    return z * jax.nn.sigmoid(z)


def _dsilu(z):
    sg = jax.nn.sigmoid(z)
    return sg * (1.0 + z * (1.0 - sg))


def _row_start(i):
    return min(max(i - WIN_H // 2, 0), ROWS - WIN_H)


def _my_pos():
    return lax.axis_index("x"), lax.axis_index("y"), lax.axis_index("c")


def _flip(v, bit):
    return 1 - v if bit else v


def _swap_copies(srcs, lands, ssem, rsem):
    x, y, c = _my_pos()
    return [pltpu.make_async_remote_copy(src_ref=srcs[a], dst_ref=lands[a], send_sem=ssem.at[a], recv_sem=rsem.at[a],
                                         device_id=(x, y, 1 - c), device_id_type=MESH) for a in range(len(srcs))]


HBM_SPEC = pl.BlockSpec(memory_space=pltpu.HBM)
SEM_SPEC = pl.BlockSpec(memory_space=pltpu.SEMAPHORE)
DATAFLOW = pltpu.SideEffectType.DATAFLOW_SIDE_EFFECTING


def _peer_chips(x, y, c):
    out = []
    for k in range(1, 4):
        px, py = _flip(x, (k >> 1) & 1), _flip(y, k & 1)
        out.append(((px, py, c), 2 * px + py))
    return out


def _half_copies(srcs, dsts, ssem, rsem, which):
    x, y, c = _my_pos()
    j = 2 * x + y
    peers = _peer_chips(x, y, c)
    pairs = []
    for pos, group, k in which:
        half = srcs[pos].shape[1] // 2
        mine = pl.ds(pl.multiple_of(c * half, 8), half)
        dev, pj = peers[k]
        sem = 3 * group + k
        send = pltpu.make_async_remote_copy(src_ref=srcs[pos].at[j, mine], dst_ref=dsts[pos].at[j, mine],
                                            send_sem=ssem.at[sem], recv_sem=rsem.at[sem], device_id=dev,
                                            device_id_type=MESH)
        arrive = pltpu.make_async_remote_copy(src_ref=srcs[pos].at[j, mine], dst_ref=dsts[pos].at[pj, mine],
                                              send_sem=ssem.at[sem], recv_sem=rsem.at[sem], device_id=dev,
                                              device_id_type=MESH)
        pairs.append((send, arrive))
    return pairs


def _halves_wait(ssem, rsem, bigs, after, which, name):
    nb = len(bigs)

    def body(*refs):
        b_in = refs[:nb]
        ssem_ref, rsem_ref = refs[nb], refs[nb + 1]
        for send, arrive in _half_copies(b_in, b_in, ssem_ref, rsem_ref, which):
            send.wait_send()
            arrive.wait_recv()

    return _hbm_call(
        body, name=name, out_shape=tuple(pltpu.HBM(b.shape, b.dtype) for b in bigs),
        in_specs=[HBM_SPEC] * nb + [SEM_SPEC, SEM_SPEC, ANY_SPEC], out_specs=tuple([HBM_SPEC] * nb),
        input_output_aliases={a: a for a in range(nb)}, compiler_params=_cp(has_side_effects=DATAFLOW),
    )(*bigs, ssem, rsem, after)


FORWARD_SEM = 3


def _diag_forward_copy(srcs, dsts, ssem, rsem):
    x, y, c = _my_pos()
    half = srcs[0].shape[1] // 2
    diag = 3 - (2 * x + y)
    mine = pl.ds(pl.multiple_of(c * half, 8), half)
    other = pl.ds(pl.multiple_of((1 - c) * half, 8), half)
    return [_Copy(srcs[0].at[diag, mine], dsts[0].at[diag, mine], dsts[0].at[diag, other], ssem.at[FORWARD_SEM],
                  rsem.at[FORWARD_SEM], (x, y, 1 - c))]


def _forward_then_start(fwd, big, order, name):
    def body(f_in, b_in, f_out, ssem, rsem, b_out, token):
        _diag_forward_copy([f_in], [f_out], ssem, rsem)[0].start()
        for send, _ in _half_copies([b_in], [b_out], ssem, rsem, order):
            send.start()
        token[...] = jnp.zeros_like(token)

    n_sem = FORWARD_SEM + 1
    out_shape = (pltpu.HBM(fwd.shape, fwd.dtype), pltpu.SemaphoreType.DMA((n_sem,)), pltpu.SemaphoreType.DMA((n_sem,)),
                 pltpu.HBM(big.shape, big.dtype), SDS((8, 128), f32))
    return _hbm_call(
        body, name=name, out_shape=out_shape, in_specs=[HBM_SPEC, HBM_SPEC],
        out_specs=(HBM_SPEC, SEM_SPEC, SEM_SPEC, HBM_SPEC, VMEM_SPEC), input_output_aliases={0: 0, 1: 3},
        compiler_params=_cp(has_side_effects=DATAFLOW),
    )(*[pltpu.with_memory_space_constraint(b, pltpu.HBM) for b in (fwd, big)])


def _cast_to_slot(w, jvec, name, after=None):
    rows, cols = w.shape
    tr = 128
    band = lambda k: slice(k * tr, (k + 1) * tr)
    return _stream_call(
        lambda x: (x.astype(bf16),), rows // tr, [(w, (tr, cols), lambda ref, k, j: ref.at[band(k)])],
        [((4, rows, cols), bf16, (tr, cols), lambda ref, k, j: ref.at[j, band(k)])], jvec, name, after)[0]


def _stream_call(fn, n_chunks, srcs, outs, vec, name, after=None):
    ns, no = len(srcs), len(outs)

    def body(*refs):
        s = refs[0][0]
        in_refs, out_refs = refs[1:1 + ns], refs[2 + ns:2 + ns + no]
        in_bufs, out_bufs = refs[2 + ns + no:2 + 2 * ns + no], refs[2 + 2 * ns + no:2 + 2 * ns + 2 * no]
        sem_in, sem_out = refs[2 + 2 * ns + 2 * no:]
        loads, stores = [], []
        for k in range(n_chunks):
            cps = [pltpu.make_async_copy(srcs[a][2](in_refs[a], k, s), in_bufs[a].at[k], sem_in.at[a, k])
                   for a in range(ns)]
            for cp in cps:
                cp.start()
            loads.append(cps)
        for k, cps in enumerate(loads):
            for cp in cps:
                cp.wait()
            for b, val in enumerate(fn(*[buf[k] for buf in in_bufs])):
                out_bufs[b][k] = val
                cp = pltpu.make_async_copy(out_bufs[b].at[k], outs[b][3](out_refs[b], k, s), sem_out.at[b, k])
                cp.start()
                stores.append(cp)
        for cp in stores:
            cp.wait()

    in_bufs = [pltpu.VMEM((n_chunks,) + tuple(shape), a.dtype) for a, shape, _ in srcs]
    out_bufs = [pltpu.VMEM((n_chunks,) + tuple(cshape), dtype) for _, dtype, cshape, _ in outs]
    n_bytes = sum(np.prod(b.shape) * np.dtype(b.dtype).itemsize for b in in_bufs + out_bufs)
    after_big = after is not None and after.size * after.dtype.itemsize > (1 << 20)
    after = pltpu.with_memory_space_constraint(after, pltpu.HBM) if after_big else after
    res = _pallas_call(
        body, name=name, out_shape=tuple(pltpu.HBM(shape, dtype) for shape, dtype, _, _ in outs),
        in_specs=[SMEM_SPEC] + [HBM_SPEC] * ns + [HBM_SPEC if after_big else ANY_SPEC], out_specs=(HBM_SPEC,) * no,
        scratch_shapes=in_bufs + out_bufs + [pltpu.SemaphoreType.DMA((ns, n_chunks)),
                                             pltpu.SemaphoreType.DMA((no, n_chunks))],
        compiler_params=_cp(int(n_bytes) // (1 << 20) + 8),
    )(vec, *[pltpu.with_memory_space_constraint(a, pltpu.HBM) for a, _, _ in srcs], vec if after is None else after)
    return res


def _exchange_copies(srcs, lands, ssem, rsem):
    x, y, c = _my_pos()
    half = srcs[0].shape[0] // 8
    cps = []
    for jb in range(4):
        src = srcs[0].at[pl.ds(pl.multiple_of((2 * jb + 1 - c) * half, 8), half)]
        cps.append(pltpu.make_async_remote_copy(src_ref=src, dst_ref=lands[0].at[jb], send_sem=ssem.at[jb],
                                                recv_sem=rsem.at[jb], device_id=(x, y, 1 - c), device_id_type=MESH))
    return cps


def _block_exchange_copies(srcs, lands, ssem, rsem):
    x, y, c = _my_pos()
    return [pltpu.make_async_remote_copy(src_ref=srcs[0].at[jb], dst_ref=lands[0].at[jb], send_sem=ssem.at[jb],
                                         recv_sem=rsem.at[jb], device_id=(x, y, 1 - c), device_id_type=MESH)
            for jb in range(4)]


def _scatter_copies(srcs, lands, ssem, rsem):
    x, y, c = _my_pos()
    cps = []
    for a in range(len(srcs)):
        for k, (dev, pj) in enumerate(_peer_chips(x, y, c)):
            cps.append(pltpu.make_async_remote_copy(src_ref=srcs[a].at[pj], dst_ref=lands[a].at[k],
                                                    send_sem=ssem.at[3 * a + k], recv_sem=rsem.at[3 * a + k],
                                                    device_id=dev, device_id_type=MESH))
    return cps


class _Copy:
    def __init__(self, src, dst, arrive, ssem, rsem, dev):
        make = lambda to: pltpu.make_async_remote_copy(src_ref=src, dst_ref=to, send_sem=ssem, recv_sem=rsem,
                                                       device_id=dev, device_id_type=MESH)
        send, arrival = make(dst), make(arrive)
        self.start, self.wait_send, self.wait_recv = send.start, send.wait_send, arrival.wait_recv


def _toward(x, y, along_x):
    return x + along_x * (1 - 2 * x), y + (1 - along_x) * (1 - 2 * y)


def _near_copies(srcs, dsts, ssem, rsem):
    x, y, c = _my_pos()
    px, py = _toward(x, y, c)
    j = 2 * x + y
    return [_Copy(srcs[0].at[j], dsts[0].at[j], dsts[0].at[2 * px + py], ssem.at[0], rsem.at[0], (px, py, c))]


def _pass_copy(srcs, dsts, ssem, rsem):
    x, y, c = _my_pos()
    px, py = _toward(x, y, c)
    qx, qy = _toward(x, y, 1 - c)
    got = 2 * px + py
    return [_Copy(srcs[0].at[got], dsts[0].at[got], dsts[0].at[2 * qx + qy], ssem.at[0], rsem.at[0], (x, y, 1 - c))]


def _relay_copy(srcs, dsts, ssem, rsem):
    x, y, c = _my_pos()
    px, py = _toward(x, y, c)
    qx, qy = _toward(x, y, 1 - c)
    half = srcs[0].shape[1] // 2
    mine = pl.ds(pl.multiple_of(c * half, 8), half)
    got, diag = 2 * px + py, 3 - (2 * x + y)
    return [_Copy(srcs[0].at[got, mine], dsts[0].at[got, mine], dsts[0].at[diag, mine], ssem.at[1], rsem.at[1],
                  (qx, qy, c))]


def _forward_copies(srcs, dsts, ssem, rsem):
    x, y, c = _my_pos()
    half = srcs[0].shape[1] // 2
    mine = pl.ds(pl.multiple_of(c * half, 8), half)
    other = pl.ds(pl.multiple_of((1 - c) * half, 8), half)
    return [_Copy(srcs[0].at[pj, mine], dsts[0].at[pj, mine], dsts[0].at[pj, other], ssem.at[k], rsem.at[k],
                  (x, y, 1 - c)) for k, (_, pj) in enumerate(_peer_chips(x, y, c))]


def _pass_relay_copies(srcs, dsts, ssem, rsem):
    return _pass_copy(srcs, dsts, ssem, rsem) + _relay_copy(srcs, dsts, ssem, rsem)


def _gather8_copies(srcs, dsts, ssem, rsem):
    x, y, c = _my_pos()
    me = 4 * x + 2 * y + c
    cps = []
    for a in range(len(srcs)):
        for k in range(1, 8):
            tgt = (_flip(x, (k >> 2) & 1), _flip(y, (k >> 1) & 1), _flip(c, k & 1))
            cps.append(_Copy(srcs[a].at[me], dsts[a].at[me], dsts[a].at[4 * tgt[0] + 2 * tgt[1] + tgt[2]],
                             ssem.at[7 * a + k - 1], rsem.at[7 * a + k - 1], tgt))
    return cps


def _gather4_copies(srcs, dsts, ssem, rsem):
    x, y, c = _my_pos()
    j = 2 * x + y
    cps = []
    for a in range(len(srcs)):
        for k, (dev, pj) in enumerate(_peer_chips(x, y, c)):
            cps.append(_Copy(srcs[a].at[j], dsts[a].at[j], dsts[a].at[pj], ssem.at[3 * a + k], rsem.at[3 * a + k], dev))
    return cps


def _to_slot(a, n, i):
    return lax.dynamic_update_slice(jnp.zeros((n,) + a.shape, a.dtype), a[None], (i,) + (0,) * a.ndim)


def _split_start(srcs, land_shapes, n_cp, make, name, after=None):
    ns, nl = len(srcs), len(land_shapes)
    n_in = ns + nl + (after is not None)

    def body(*refs):
        s_in = refs[:ns]
        ssem, rsem = refs[n_in], refs[n_in + 1]
        s_out = refs[n_in + 2:n_in + 2 + ns]
        l_out = refs[n_in + 2 + ns:n_in + 2 + ns + nl]
        token = refs[n_in + 2 + ns + nl]
        for cp in make(s_in, l_out if nl else s_out, ssem, rsem):
            cp.start()
        token[...] = jnp.zeros_like(token)

    lands = [pltpu.with_memory_space_constraint(lax.empty(sh.shape, sh.dtype), pltpu.HBM) for sh in land_shapes]
    out_shape = (pltpu.SemaphoreType.DMA((n_cp,)), pltpu.SemaphoreType.DMA((n_cp,)),
                 *[pltpu.HBM(b.shape, b.dtype) for b in srcs], *[pltpu.HBM(b.shape, b.dtype) for b in land_shapes],
                 SDS((8, 128), f32))
    return _hbm_call(
        body, name=name, out_shape=out_shape, in_specs=[HBM_SPEC] * (ns + nl) + [ANY_SPEC] * (after is not None),
        out_specs=(SEM_SPEC, SEM_SPEC, *[HBM_SPEC] * (ns + nl), VMEM_SPEC),
        input_output_aliases={i: 2 + i for i in range(ns + nl)}, compiler_params=_cp(has_side_effects=DATAFLOW),
    )(*[pltpu.with_memory_space_constraint(b, pltpu.HBM) for b in srcs], *lands, *([] if after is None else [after]))


def _split_wait(ssem, rsem, srcs, lands, after, make, name):
    ns, nl = len(srcs), len(lands)

    def body(*refs):
        s_in, l_in = refs[:ns], refs[ns:ns + nl]
        ssem_ref, rsem_ref = refs[ns + nl], refs[ns + nl + 1]
        for cp in make(s_in, l_in if nl else s_in, ssem_ref, rsem_ref):
            cp.wait_send()
            cp.wait_recv()

    outs = _hbm_call(
        body, name=name, out_shape=tuple(pltpu.HBM(b.shape, b.dtype) for b in (*srcs, *lands)),
        in_specs=[HBM_SPEC] * (ns + nl) + [SEM_SPEC, SEM_SPEC, ANY_SPEC], out_specs=tuple([HBM_SPEC] * (ns + nl)),
        input_output_aliases={i: i for i in range(ns + nl)}, compiler_params=_cp(has_side_effects=DATAFLOW),
    )(*srcs, *lands, ssem, rsem, after)
    return list(outs[:ns]), list(outs[ns:])


def _adaln_shard(cc, w_ada_shard):
    def body(c_ref, w_ref, m_ref, sc_ref):
        sc = _silu(c_ref[...])
        sc_ref[...] = sc
        m_ref[...] = jnp.dot(sc, w_ref[...], precision=HIGHEST, preferred_element_type=f32)

    return _hbm_call(
        body, name="adaln_shard", out_shape=(SDS((16, w_ada_shard.shape[1]), f32), SDS((16, D), f32)),
        in_specs=[VMEM_SPEC, VMEM_SPEC], out_specs=(VMEM_SPEC, VMEM_SPEC), compiler_params=_cp(32),
    )(cc, w_ada_shard)


def _prenorm(xx, norm_g, mrow, b_ada, tm, name, after=None):
    n = xx.shape[0]

    def body(x_ref, g_ref, m_ref, b_ref, after_ref, h_ref):
        x = x_ref[...]
        shift = m_ref[:, 0:D] + b_ref[:, 0:D]
        scale = m_ref[:, D:2 * D] + b_ref[:, D:2 * D]
        r = lax.rsqrt(jnp.mean(x * x, axis=-1, keepdims=True) + RMS_EPS)
        y = (x * r) * g_ref[...]
        h_ref[...] = (y * (1.0 + scale) + shift).astype(bf16)

    row = lambda i: (i, 0)
    fixed = lambda i: (0, 0)
    return _hbm_call(
        body, name=name, out_shape=SDS((n, D), bf16), grid=(n // tm,),
        in_specs=[pl.BlockSpec((tm, D), row), pl.BlockSpec((1, D), fixed), pl.BlockSpec((1, 3 * D), fixed),
                  pl.BlockSpec((1, 3 * D), fixed), ANY_SPEC],
        out_specs=pl.BlockSpec((tm, D), row),
    )(xx, norm_g, mrow, b_ada, b_ada if after is None else after)


def _in_proj_own(h, w_own, jvec):
    tm = 512

    def body(j_ref, h_ref, w_ref, p_ref):
        p_ref[...] = jnp.dot(h_ref[...], w_ref[...].astype(bf16), preferred_element_type=f32)

    grid_spec = pltpu.PrefetchScalarGridSpec(
        num_scalar_prefetch=1, grid=(S // tm,),
        in_specs=[pl.BlockSpec((tm, D), lambda i, j: (i, 0)), pl.BlockSpec((D, D), lambda i, j: (0, 0))],
        out_specs=pl.BlockSpec((tm, D), lambda i, j: (i, j[0])))
    return _hbm_call(body, name="in_proj_own", out_shape=SDS((S, 4 * D), f32), grid_spec=grid_spec,
                     compiler_params=_cp(40))(jvec, h, w_own)


def _in_proj_block(h, w4, p, bvec, name, after=None):
    tm = 512

    def body(b_ref, h_ref, w_ref, p_in_ref, after_ref, p_ref):
        p_ref[...] = jnp.dot(h_ref[...], w_ref[...], preferred_element_type=f32)

    grid_spec = pltpu.PrefetchScalarGridSpec(
        num_scalar_prefetch=1, grid=(S // tm,),
        in_specs=[pl.BlockSpec((tm, D), lambda i, b: (i, 0)), pl.BlockSpec((None, D, D), lambda i, b: (b[0], 0, 0)),
                  ANY_SPEC, ANY_SPEC],
        out_specs=pl.BlockSpec((tm, D), lambda i, b: (i, b[0])))
    return _hbm_call(body, name=name, out_shape=SDS((S, 4 * D), f32), grid_spec=grid_spec,
                     input_output_aliases={3: 0})(bvec, h, w4, p, bvec if after is None else after)


def _ctx_proj(hc, w4):
    def body(h_ref, w0_ref, w1_ref, p_ref):
        hv = h_ref[...]
        p_ref[:, 0:DA] = jnp.dot(hv, w0_ref[:, DA:2 * DA], preferred_element_type=f32)
        p_ref[:, DA:2 * DA] = jnp.dot(hv, w1_ref[:, 0:DA], preferred_element_type=f32)

    return _hbm_call(
        body, name="ctx_proj", out_shape=SDS((L, 2 * DA), f32), grid=(1,),
        in_specs=[pl.BlockSpec((L, D), lambda i: (0, 0)), pl.BlockSpec((None, D, D), lambda i: (0, 0, 0)),
                  pl.BlockSpec((None, D, D), lambda i: (1, 0, 0))],
        out_specs=pl.BlockSpec((L, 2 * DA), lambda i: (0, 0)),
    )(hc, w4, w4)


def _head_ones():
    r = lax.broadcasted_iota(i32, (DA, DA), 0) // DH
    c = lax.broadcasted_iota(i32, (DA, DA), 1) // DH
    return (r == c).astype(bf16)


def _head_sum(v, ones_bd):
    hi = v.astype(bf16)
    lo = (v - hi.astype(f32)).astype(bf16)
    return jnp.dot(hi, ones_bd, preferred_element_type=f32) + jnp.dot(lo, ones_bd, preferred_element_type=f32)


def _swap16(v):
    lane = lax.broadcasted_iota(i32, v.shape, 1)
    return jnp.where((lane & 31) < 16, pltpu.roll(v, DA - 16, 1), pltpu.roll(v, 16, 1))


def _rope_block(ct_ref, rt_ref, tm):
    rows = [jnp.tile(rt_ref[8 * j:8 * j + 8, :], (GW // 8, 1)) for j in range(tm // GW)]
    return jnp.tile(ct_ref[...], (tm // GW, 1)) + jnp.concatenate(rows, axis=0)


def _rope_specs(tm):
    col = pl.BlockSpec((GW, DA), lambda i: (0, 0))
    row = pl.BlockSpec((8 * tm // GW, DA), lambda i: (i, 0))
    return [col, row, col, row]


def _qk_prep(p, gq, gk, rope):
    tm = 512

    def body(qk_ref, v_ref, gq_ref, gk_ref, cc_ref, cr_ref, sc_ref, sr_ref, qr_ref, qp_ref, kr_ref, vh_ref):
        ones_bd = _head_ones()
        cs, sn = _rope_block(cc_ref, cr_ref, tm), _rope_block(sc_ref, sr_ref, tm)
        q = qk_ref[:, 0:DA]
        k = qk_ref[:, DA:2 * DA]
        yq = (q * lax.rsqrt(_head_sum(q * q, ones_bd) * (1.0 / DH) + RMS_EPS)) * gq_ref[...]
        yk = (k * lax.rsqrt(_head_sum(k * k, ones_bd) * (1.0 / DH) + RMS_EPS)) * gk_ref[...]
        qr = (yq * cs + _swap16(yq) * sn) * QK_SCALE
        qp = yq * QK_SCALE
        kr = yk * cs + _swap16(yk) * sn
        vv = v_ref[...]
        for hh in range(H):
            sl = slice(hh * DH, (hh + 1) * DH)
            qr_ref[hh] = qr[:, sl].astype(bf16)
            qp_ref[hh] = qp[:, sl].astype(bf16)
            kr_ref[hh] = kr[:, sl].astype(bf16)
            vh_ref[hh] = vv[:, sl].astype(bf16)

    hm = SDS((H, S, DH), bf16)
    hspec = pl.BlockSpec((H, tm, DH), lambda i: (0, i, 0))
    fixed = lambda i: (0, 0)
    return _hbm_call(
        body, name="qk_prep", out_shape=(hm, hm, hm, hm), grid=(S // tm,),
        in_specs=[pl.BlockSpec((tm, 2 * DA), lambda i: (i, 0)), pl.BlockSpec((tm, DA), lambda i: (i, 2)),
                  pl.BlockSpec((1, DA), fixed), pl.BlockSpec((1, DA), fixed)] + _rope_specs(tm),
        out_specs=(hspec, hspec, hspec, hspec),
    )(p, p, gq, gk, *rope)


def _ctx_prep(pc, gk):
    def body(p_ref, gk_ref, kc_ref, vc_ref):
        ones_bd = _head_ones()
        k = p_ref[:, 0:DA]
        yk = (k * lax.rsqrt(_head_sum(k * k, ones_bd) * (1.0 / DH) + RMS_EPS)) * gk_ref[...]
        vv = p_ref[:, DA:2 * DA]
        for hh in range(H):
            sl = slice(hh * DH, (hh + 1) * DH)
            kc_ref[hh] = yk[:, sl].astype(bf16)
            vc_ref[hh] = vv[:, sl].astype(bf16)

    hm = SDS((H, L, DH), bf16)
    return _hbm_call(
        body, name="ctx_prep", out_shape=(hm, hm), in_specs=[VMEM_SPEC, VMEM_SPEC], out_specs=(VMEM_SPEC, VMEM_SPEC),
    )(pc, gk)


def _tile_pieces():
    out = []
    for (i0, u0) in TILE_GEOM:
        rows = []
        for j in range(2):
            i = i0 + j
            rs = _row_start(i)
            rows.append([(u0 + u - i + WIN_H - 1) if rs <= u0 + u < rs + WIN_H else None for u in range(KR)])
        out.append(rows)
    return out


def _bias_prep(rpb_rev_pad, after=None):
    pieces = _tile_pieces()

    def body(r_ref, after_ref, o_ref):
        rp = r_ref[...]
        xs = jnp.concatenate([pltpu.roll(jnp.broadcast_to(rp[dr:dr + 1, :], (GW, 128)), 128 - (WIN_W - 1), 1,
                                         stride=1, stride_axis=0) for dr in range(N_DR)], axis=0)
        row = lax.broadcasted_iota(i32, xs.shape, 0)
        lane = lax.broadcasted_iota(i32, xs.shape, 1)
        k = row & (GW - 1)
        c0 = jnp.clip(lane - WIN_W // 2, 0, GW - WIN_W)
        xs = jnp.where((k >= c0) & (k < c0 + WIN_W), xs, NEG)
        neg = jnp.full((GW, GW), NEG, f32)
        for t in range(NT):
            for j in range(2):
                for u in range(KR):
                    dr = pieces[t][j][u]
                    piece = neg if dr is None else xs[dr * GW:(dr + 1) * GW, 0:GW]
                    o_ref[t, u * GW:(u + 1) * GW, j * GW:(j + 1) * GW] = piece

    return _hbm_call(
        body, name="bias_prep", out_shape=SDS((H, NT, KB, QB), f32), grid=(H,),
        in_specs=[pl.BlockSpec((None, N_DR, 128), lambda h: (h, 0, 0)), ANY_SPEC],
        out_specs=pl.BlockSpec((None, NT, KB, QB), lambda h: (h, 0, 0, 0)),
    )(rpb_rev_pad, rpb_rev_pad if after is None else after)


def _bias_tiles(rpb2, after=None):
    return _bias_prep(jnp.pad(rpb2[:, :, ::-1], ((0, 0), (0, 0), (0, 128 - N_DC))), after)


def _block_geom(b):
    qs = b * QB
    ks = min(max(2 * b - 4, 0), ROWS - KR) * GW
    t = b if b < 2 else (b - (NQB - NT) if b > NQB - 3 else 2)
    return qs, ks, t


def _tt(a, b):
    return lax.dot_general(a, b, (((1,), (1,)), ((), ())), preferred_element_type=f32)


def _tn(a, b):
    return lax.dot_general(a, b, (((0,), (0,)), ((), ())), preferred_element_type=f32)


def _softmax_t(s_lat, s_ctx):
    m = jnp.maximum(jnp.max(s_lat, axis=0, keepdims=True), jnp.max(s_ctx, axis=0, keepdims=True))
    e_lat = jnp.exp(s_lat - m)
    e_ctx = jnp.exp(s_ctx - m)
    inv = 1.0 / (jnp.sum(e_lat, axis=0, keepdims=True) + jnp.sum(e_ctx, axis=0, keepdims=True))
    return e_lat * inv, e_ctx * inv


def _staged(n_blocks, stages):
    held = [dict() for _ in stages]
    for step in range(n_blocks + len(stages) - 1):
        for s, fn in enumerate(stages):
            b = step - s
            if 0 <= b < n_blocks:
                held[s][b] = fn(b) if s == 0 else fn(b, held[s - 1].pop(b))


def _attn_fwd(qr, qp, kr, vh, kc, vc, btt):
    def body(qr_ref, qp_ref, kr_ref, v_ref, kc_ref, vc_ref, bt_ref, o_ref):
        kcv, vcv = kc_ref[...], vc_ref[...]

        def scores(b):
            qs, ks, t = _block_geom(b)
            return (_tt(kr_ref[ks:ks + KB, :], qr_ref[qs:qs + QB, :]) + bt_ref[t], _tt(kcv, qp_ref[qs:qs + QB, :]))

        def probs(b, sc):
            p_lat, p_ctx = _softmax_t(*sc)
            return p_lat.astype(bf16), p_ctx.astype(bf16)

        def values(b, p):
            qs, ks, _ = _block_geom(b)
            o_ref[qs:qs + QB, :] = _tn(p[0], v_ref[ks:ks + KB, :]) + _tn(p[1], vcv)

        _staged(NQB, (scores, probs, values))

    sq = pl.BlockSpec((None, S, DH), lambda h: (h, 0, 0))
    sc = pl.BlockSpec((None, L, DH), lambda h: (h, 0, 0))
    return _hbm_call(
        body, name="attn_fwd", out_shape=SDS((H, S, DH), f32), grid=(H,),
        in_specs=[sq, sq, sq, sq, sc, sc, pl.BlockSpec((None, NT, KB, QB), lambda h: (h, 0, 0, 0))],
        out_specs=sq, compiler_params=_cp(48),
    )(qr, qp, kr, vh, kc, vc, btt)


def _shift_rows(v, down):
    n = v.shape[0]
    row = lax.broadcasted_iota(i32, v.shape, 0)
    if down:
        return jnp.where(row == 0, 0.0, pltpu.roll(v, 1, 0))
    return jnp.where(row == n - 1, 0.0, pltpu.roll(v, n - 1, 0))


def _conv_specs():
    col = lambda off: pl.BlockSpec((S, 128), lambda i, off=off: (0, off + i))
    return [col(16), col(20), col(24), col(28), pl.BlockSpec((3, 128), lambda i: (0, i)),
            pl.BlockSpec((1, 128), lambda i: (0, i))]


def _conv_fwd(p, conv_w, conv_b, after=None):
    def body(u_ref, bg_ref, cg_ref, zc_ref, w_ref, b_ref, after_ref, o_ref):
        cu = cg_ref[...] * u_ref[...]
        cv = b_ref[...] + _shift_rows(cu, True) * w_ref[0:1, :]
        cv = cv + cu * w_ref[1:2, :]
        cv = cv + _shift_rows(cu, False) * w_ref[2:3, :]
        o_ref[...] = ((bg_ref[...] * cv) * _silu(zc_ref[...])).astype(bf16)

    return _hbm_call(
        body, name="conv_fwd", out_shape=SDS((S, DC), bf16), grid=(DC // 128,),
        in_specs=_conv_specs() + [ANY_SPEC], out_specs=pl.BlockSpec((S, 128), lambda i: (0, i)),
        compiler_params=_cp(40),
    )(p, p, p, p, conv_w, conv_b, conv_b if after is None else after)


DP_Q, DP_K, DP_V, DP_ZA, DP_U, DP_BG, DP_CG, DP_ZC = range(8)


def _out_proj_loss(o, p, conv_g, w_out, xx, tgt, mrow, b_ada):
    tm = 512

    def body(o_ref, za_ref, c_ref, w_ref, x_ref, t_ref, m_ref, b_ref,
             dy_ref, dconv_ref, dp_ref, do_ref, gwo_ref, dgate_ref, loss_ref):
        k = pl.program_id(0)

        @pl.when(k == 0)
        def _():
            gwo_ref[...] = jnp.zeros_like(gwo_ref)
            dgate_ref[...] = jnp.zeros_like(dgate_ref)
            loss_ref[0, 0] = 0.0

        gate = m_ref[:, 2 * D:3 * D] + b_ref[:, 2 * D:3 * D]
        za = za_ref[...]
        sz = _silu(za)
        om = _merge_heads(o_ref)
        av, cv = (om * sz).astype(bf16), c_ref[...]
        mo = jnp.dot(av, w_ref[0:DA, :], preferred_element_type=f32)
        mo = mo + jnp.dot(cv, w_ref[DA:DA + DC, :], preferred_element_type=f32)
        y = x_ref[...] + gate * mo
        diff = y - t_ref[...]
        loss_ref[0, 0] += jnp.sum(diff * diff)
        dy = diff * (1.0 / D)
        dy_ref[...] = dy
        dgate_ref[...] += jnp.sum(dy * mo, axis=0, keepdims=True)
        dmo = (dy * gate).astype(bf16)
        dmix = _tt(dmo, w_ref[...])
        dattn = dmix[:, 0:DA]
        dconv_ref[...] = dmix[:, DA:DA + DC]
        a = dattn * sz
        for hh in range(H):
            do_ref[hh] = a[:, hh * DH:(hh + 1) * DH].astype(bf16)
        dp_ref[...] = ((dattn * _dsilu(za)) * om).astype(bf16)
        gwo_ref[0:DA, :] += _tn(av, dmo)
        gwo_ref[DA:DA + DC, :] += _tn(cv, dmo)

    row = lambda i: (i, 0)
    fixed = lambda i: (0, 0)
    hspec = pl.BlockSpec((H, tm, DH), lambda i: (0, i, 0))
    return _hbm_call(
        body, name="out_proj_loss",
        out_shape=(SDS((S, D), f32), SDS((S, DC), f32), SDS((8, S, DA), bf16), SDS((H, S, DH), bf16),
                   SDS((D, D), f32), SDS((1, D), f32), SDS((1, 1), f32)),
        grid=(S // tm,),
        in_specs=[hspec, pl.BlockSpec((tm, DA), lambda i: (i, 3)), pl.BlockSpec((tm, DC), row),
                  pl.BlockSpec((D, D), fixed), pl.BlockSpec((tm, D), row), pl.BlockSpec((tm, D), row),
                  pl.BlockSpec((1, 3 * D), fixed), pl.BlockSpec((1, 3 * D), fixed)],
        out_specs=(pl.BlockSpec((tm, D), row), pl.BlockSpec((tm, DC), row),
                   pl.BlockSpec((None, tm, DA), lambda i: (DP_ZA, i, 0)), hspec, pl.BlockSpec((D, D), fixed),
                   pl.BlockSpec((1, D), fixed), SMEM_SPEC),
        compiler_params=_cp(56, dimension_semantics=("arbitrary",)),
    )(o, p, conv_g, w_out, xx, tgt, mrow, b_ada)


def _conv_bwd(dconv, p, conv_w, conv_b, dp8, after=None):
    def body(d_ref, u_ref, bg_ref, cg_ref, zc_ref, w_ref, b_ref, dp_in_ref, after_ref, dp_ref, gw_ref, gb_ref):
        du_ref, dbg_ref, dcg_ref, dzc_ref = dp_ref.at[0], dp_ref.at[1], dp_ref.at[2], dp_ref.at[3]
        dconv = d_ref[...]
        u, bg, cg, zc = u_ref[...], bg_ref[...], cg_ref[...], zc_ref[...]
        w0, w1, w2 = w_ref[0:1, :], w_ref[1:2, :], w_ref[2:3, :]
        cu = cg * u
        cu_m, cu_p = _shift_rows(cu, True), _shift_rows(cu, False)
        cv = b_ref[...] + cu_m * w0
        cv = cv + cu * w1
        cv = cv + cu_p * w2
        sz = _silu(zc)
        dbg_ref[...] = ((dconv * sz) * cv).astype(bf16)
        dzc_ref[...] = ((dconv * (bg * cv)) * _dsilu(zc)).astype(bf16)
        dcv = (dconv * sz) * bg
        gb_ref[...] = jnp.sum(dcv, axis=0, keepdims=True)
        gw_ref[0:1, :] = jnp.sum(dcv * cu_m, axis=0, keepdims=True)
        gw_ref[1:2, :] = jnp.sum(dcv * cu, axis=0, keepdims=True)
        gw_ref[2:3, :] = jnp.sum(dcv * cu_p, axis=0, keepdims=True)
        gw_ref[3:8, :] = jnp.zeros((5, 128), f32)
        dcu = _shift_rows(dcv, False) * w0 + dcv * w1 + _shift_rows(dcv, True) * w2
        dcg_ref[...] = (dcu * u).astype(bf16)
        du_ref[...] = (dcu * cg).astype(bf16)

    return _hbm_call(
        body, name="conv_bwd", out_shape=(SDS((8, S, DC), bf16), SDS((8, DC), f32), SDS((1, DC), f32)),
        grid=(DC // 128,),
        in_specs=[pl.BlockSpec((S, 128), lambda i: (0, i))] + _conv_specs() + [ANY_SPEC, ANY_SPEC],
        out_specs=(pl.BlockSpec((4, S, 128), lambda i: (DP_U // 4, 0, i)), pl.BlockSpec((8, 128), lambda i: (0, i)),
                   pl.BlockSpec((1, 128), lambda i: (0, i))),
        input_output_aliases={7: 0}, compiler_params=_cp(48),
    )(dconv, p, p, p, p, conv_w, conv_b, dp8, conv_b if after is None else after)


def _attn_bwd(qr, qp, kr, vh, kc, vc, btt, do, after=None):
    def body(qr_ref, qp_ref, kr_ref, v_ref, kc_ref, vc_ref, bt_ref, do_ref, after_ref,
             dqr_ref, dqp_ref, dkr_ref, dv_ref, dkc_ref, dvc_ref, dbt_ref):
        kcv, vcv = kc_ref[...], vc_ref[...]
        dkr_ref[...] = jnp.zeros_like(dkr_ref)
        dv_ref[...] = jnp.zeros_like(dv_ref)
        dbt_ref[...] = jnp.zeros_like(dbt_ref)
        ctx_acc = {}

        def products(b):
            qs, ks, t = _block_geom(b)
            dob = do_ref[qs:qs + QB, :]
            s_lat = _tt(kr_ref[ks:ks + KB, :], qr_ref[qs:qs + QB, :]) + bt_ref[t]
            s_ctx = _tt(kcv, qp_ref[qs:qs + QB, :])
            return s_lat, s_ctx, _tt(v_ref[ks:ks + KB, :], dob), _tt(vcv, dob)

        def score_grads(b, x):
            s_lat, s_ctx, dp_lat, dp_ctx = x
            p_lat, p_ctx = _softmax_t(s_lat, s_ctx)
            delta = jnp.sum(p_lat * dp_lat, axis=0, keepdims=True) + jnp.sum(p_ctx * dp_ctx, axis=0, keepdims=True)
            ds_lat = p_lat * (dp_lat - delta)
            ds_ctx = p_ctx * (dp_ctx - delta)
            return ds_lat, ds_lat.astype(bf16), ds_ctx.astype(bf16), p_lat.astype(bf16), p_ctx.astype(bf16)

        def operand_grads(b, y):
            qs, ks, t = _block_geom(b)
            ds_lat, dsb_lat, dsb_ctx, pb_lat, pb_ctx = y
            qrb, qpb, dob = qr_ref[qs:qs + QB, :], qp_ref[qs:qs + QB, :], do_ref[qs:qs + QB, :]
            dbt_ref[t] += ds_lat
            dqr_ref[qs:qs + QB, :] = _tn(dsb_lat, kr_ref[ks:ks + KB, :])
            dqp_ref[qs:qs + QB, :] = _tn(dsb_ctx, kcv)
            dkr_ref[ks:ks + KB, :] += jnp.dot(dsb_lat, qrb, preferred_element_type=f32)
            dv_ref[ks:ks + KB, :] += jnp.dot(pb_lat, dob, preferred_element_type=f32)
            dkc = jnp.dot(dsb_ctx, qpb, preferred_element_type=f32)
            dvc = jnp.dot(pb_ctx, dob, preferred_element_type=f32)
            ctx_acc["k"] = dkc if b == 0 else ctx_acc["k"] + dkc
            ctx_acc["v"] = dvc if b == 0 else ctx_acc["v"] + dvc

        _staged(NQB, (products, score_grads, operand_grads))
        dkc_ref[...] = ctx_acc["k"]
        dvc_ref[...] = ctx_acc["v"]

    sq = pl.BlockSpec((None, S, DH), lambda h: (h, 0, 0))
    sc = pl.BlockSpec((None, L, DH), lambda h: (h, 0, 0))
    sb = pl.BlockSpec((None, NT, KB, QB), lambda h: (h, 0, 0, 0))
    big, ctxs = SDS((H, S, DH), f32), SDS((H, L, DH), f32)
    return _hbm_call(
        body, name="attn_bwd", out_shape=(big, big, big, big, ctxs, ctxs, SDS((H, NT, KB, QB), f32)), grid=(H,),
        in_specs=[sq, sq, sq, sq, sc, sc, sb, sq, ANY_SPEC], out_specs=(sq, sq, sq, sq, sc, sc, sb),
        compiler_params=_cp(56),
    )(qr, qp, kr, vh, kc, vc, btt, do, do if after is None else after)


def _bias_bwd(dbtt, after=None):
    pieces = _tile_pieces()

    def body(d_ref, after_ref, o_ref, scr):
        scr[...] = jnp.zeros_like(scr)
        acc = [None] * N_DR
        for t in range(NT):
            for j in range(2):
                for u in range(KR):
                    dr = pieces[t][j][u]
                    if dr is None:
                        continue
                    piece = d_ref[t, u * GW:(u + 1) * GW, j * GW:(j + 1) * GW]
                    acc[dr] = piece if acc[dr] is None else acc[dr] + piece
        a = lax.broadcasted_iota(i32, (GW, GW), 0)
        b = lax.broadcasted_iota(i32, (GW, GW), 1)
        flip = (a + b == GW - 1).astype(f32)
        for dr in range(N_DR):
            scr[dr * GW:(dr + 1) * GW, 0:GW] = jnp.dot(acc[dr], flip, precision=HIGHEST, preferred_element_type=f32)
        xs = jnp.concatenate([pltpu.roll(scr[dr * GW:(dr + 1) * GW, :], 128 + (WIN_W - 1) - (GW - 1), 1,
                                         stride=1, stride_axis=0) for dr in range(N_DR)], axis=0)
        tot = jnp.sum(xs.reshape(N_DR, GW, 128), axis=1)
        lane = lax.broadcasted_iota(i32, tot.shape, 1)
        o_ref[...] = jnp.where(lane < N_DC, tot, 0.0)

    return _hbm_call(
        body, name="bias_bwd", out_shape=SDS((H, N_DR, 128), f32), grid=(H,),
        in_specs=[pl.BlockSpec((None, NT, KB, QB), lambda h: (h, 0, 0, 0)), ANY_SPEC],
        out_specs=pl.BlockSpec((None, N_DR, 128), lambda h: (h, 0, 0)),
        scratch_shapes=[pltpu.VMEM((N_DR * GW, 128), f32)],
    )(dbtt, dbtt if after is None else after)


def _merge_heads(ref):
    return jnp.concatenate([ref[hh] for hh in range(H)], axis=1)


def _head_norm_bwd(xraw, gain, dy, ones_bd):
    r = lax.rsqrt(_head_sum(xraw * xraw, ones_bd) * (1.0 / DH) + RMS_EPS)
    xh = xraw * r
    gdy = dy * gain
    dx = r * (gdy - xh * (_head_sum(xh * gdy, ones_bd) * (1.0 / DH)))
    return dx, jnp.sum(dy * xh, axis=0, keepdims=True)


def _qk_bwd(dqr, dqp, dkr, dvh, p, gq, gk, rope, dp8):
    tm = 512

    def body(dqr_ref, dqp_ref, dkr_ref, dv_ref, qk_ref, gq_ref, gk_ref, cc_ref, cr_ref, sc_ref, sr_ref, dp_in_ref,
             dp_ref, ggq_ref, ggk_ref):
        dq_ref, dk_ref, dvo_ref = dp_ref.at[DP_Q], dp_ref.at[DP_K], dp_ref.at[DP_V]

        @pl.when(pl.program_id(0) == 0)
        def _():
            ggq_ref[...] = jnp.zeros_like(ggq_ref)
            ggk_ref[...] = jnp.zeros_like(ggk_ref)

        ones_bd = _head_ones()
        cs, sn = _rope_block(cc_ref, cr_ref, tm), _rope_block(sc_ref, sr_ref, tm)
        a = _merge_heads(dqr_ref)
        dyq = ((a * cs - _swap16(a) * sn) + _merge_heads(dqp_ref)) * QK_SCALE
        bk = _merge_heads(dkr_ref)
        dyk = bk * cs - _swap16(bk) * sn
        dq, gq_part = _head_norm_bwd(qk_ref[:, 0:DA], gq_ref[...], dyq, ones_bd)
        dk, gk_part = _head_norm_bwd(qk_ref[:, DA:2 * DA], gk_ref[...], dyk, ones_bd)
        dq_ref[...] = dq.astype(bf16)
        dk_ref[...] = dk.astype(bf16)
        dvo_ref[...] = _merge_heads(dv_ref).astype(bf16)
        ggq_ref[...] += gq_part
        ggk_ref[...] += gk_part

    hspec = pl.BlockSpec((H, tm, DH), lambda i: (0, i, 0))
    fixed = pl.BlockSpec((1, DA), lambda i: (0, 0))
    return _hbm_call(
        body, name="qk_bwd", out_shape=(SDS((8, S, DA), bf16), SDS((1, DA), f32), SDS((1, DA), f32)), grid=(S // tm,),
        in_specs=[hspec, hspec, hspec, hspec, pl.BlockSpec((tm, 2 * DA), lambda i: (i, 0)), fixed, fixed]
        + _rope_specs(tm) + [ANY_SPEC],
        out_specs=(pl.BlockSpec((3, tm, DA), lambda i: (0, i, 0)), fixed, fixed), input_output_aliases={11: 0},
        compiler_params=_cp(40, dimension_semantics=("arbitrary",)),
    )(dqr, dqp, dkr, dvh, p, gq, gk, *rope, dp8)


def _ctx_bwd(dkc, dvc, pc, gk):
    def body(dkc_ref, dvc_ref, p_ref, gk_ref, dk_ref, dv_ref, ggk_ref):
        ones_bd = _head_ones()
        dk, gk_part = _head_norm_bwd(p_ref[:, 0:DA], gk_ref[...], _merge_heads(dkc_ref), ones_bd)
        dk_ref[...] = dk.astype(bf16)
        dv_ref[...] = _merge_heads(dvc_ref).astype(bf16)
        ggk_ref[...] = gk_part

    piece = SDS((L, DA), bf16)
    return _hbm_call(
        body, name="ctx_bwd", out_shape=(piece, piece, SDS((1, DA), f32)), in_specs=[VMEM_SPEC] * 4,
        out_specs=(VMEM_SPEC,) * 3,
    )(dkc, dvc, pc, gk)


def _grad_w_in(h, dp8, hc, dkc_raw, dvc_m, half, name, after=None):
    def body(half_ref, h_ref, p_ref, hc_ref, dk_ref, dv_ref, after_ref, g_ref):
        j = pl.program_id(0)
        hv = h_ref[...]
        g_ref[:, 0:DA] = _tn(hv, p_ref[0])
        g_ref[:, DA:2 * DA] = _tn(hv, p_ref[1])

        @pl.when(j == 0)
        def _():
            g_ref[:, DA:2 * DA] += _tn(hc_ref[...], dk_ref[...])

        @pl.when(j == 1)
        def _():
            g_ref[:, 0:DA] += _tn(hc_ref[...], dv_ref[...])

    fixed = lambda j, s: (0, 0)
    hd = D // 2
    grid_spec = pltpu.PrefetchScalarGridSpec(
        num_scalar_prefetch=1, grid=(4,),
        in_specs=[pl.BlockSpec((S, hd), lambda j, s: (0, s[0])), pl.BlockSpec((2, S, DA), lambda j, s: (j, 0, 0)),
                  pl.BlockSpec((L, hd), lambda j, s: (0, s[0])), pl.BlockSpec((L, DA), fixed),
                  pl.BlockSpec((L, DA), fixed), ANY_SPEC],
        out_specs=pl.BlockSpec((None, hd, D), lambda j, s: (j, 0, 0)))
    return _hbm_call(
        body, name=name, out_shape=SDS((4, hd, D), f32), grid_spec=grid_spec, compiler_params=_cp(40),
    )(half, h, dp8, hc, dkc_raw, dvc_m, half if after is None else after)


def _grad_w_in_pair_sum(h, dp8, hc, dkc_raw, dvc_m, half, sent, landed, ssem, rsem):
    def body(half_ref, h_ref, p_ref, hc_ref, dk_ref, dv_ref, sent_ref, land_ref, ssem_ref, rsem_ref,
             t32_ref, tb_ref, buf, lsem):
        j = pl.program_id(0)
        hv = h_ref[...]
        t32_ref[:, 0:DA] = _tn(hv, p_ref[0])
        x, y, c = _my_pos()
        arrival = pltpu.make_async_remote_copy(src_ref=sent_ref.at[j], dst_ref=land_ref.at[j], send_sem=ssem_ref.at[j],
                                               recv_sem=rsem_ref.at[j], device_id=(x, y, 1 - c), device_id_type=MESH)
        arrival.wait_recv()
        arrival.wait_send()
        load = pltpu.make_async_copy(land_ref.at[j], buf, lsem)
        load.start()
        t32_ref[:, DA:2 * DA] = _tn(hv, p_ref[1])

        @pl.when(j == 0)
        def _():
            t32_ref[:, DA:2 * DA] += _tn(hc_ref[...], dk_ref[...])

        @pl.when(j == 1)
        def _():
            t32_ref[:, 0:DA] += _tn(hc_ref[...], dv_ref[...])

        load.wait()
        t = t32_ref[...] + buf[...]
        t32_ref[...] = t
        tb_ref[...] = t.astype(bf16)

    fixed = lambda j, s: (0, 0)
    hd = D // 2
    out_spec = pl.BlockSpec((None, hd, D), lambda j, s: (j, 0, 0))
    grid_spec = pltpu.PrefetchScalarGridSpec(
        num_scalar_prefetch=1, grid=(4,),
        in_specs=[pl.BlockSpec((S, hd), lambda j, s: (0, s[0])), pl.BlockSpec((2, S, DA), lambda j, s: (j, 0, 0)),
                  pl.BlockSpec((L, hd), lambda j, s: (0, s[0])), pl.BlockSpec((L, DA), fixed),
                  pl.BlockSpec((L, DA), fixed), HBM_SPEC, HBM_SPEC, SEM_SPEC, SEM_SPEC],
        out_specs=(out_spec, out_spec), scratch_shapes=[pltpu.VMEM((hd, D), f32), pltpu.SemaphoreType.DMA])
    return _hbm_call(
        body, name="grad_w_in_pair_sum", out_shape=(SDS((4, hd, D), f32), SDS((4, hd, D), bf16)), grid_spec=grid_spec,
        compiler_params=_cp(40, has_side_effects=DATAFLOW),
    )(half, h, dp8, hc, dkc_raw, dvc_m, sent, landed, ssem, rsem)


def _norm_mod_bwd(x, dh, g, scale):
    r = lax.rsqrt(jnp.mean(x * x, axis=-1, keepdims=True) + RMS_EPS)
    xh = x * r
    y = xh * g
    dshift = jnp.sum(dh, axis=0, keepdims=True)
    dscale = jnp.sum(dh * y, axis=0, keepdims=True)
    dyn = dh * (1.0 + scale)
    dg = jnp.sum(dyn * xh, axis=0, keepdims=True)
    gdy = dyn * g
    dx = r * (gdy - xh * jnp.mean(xh * gdy, axis=-1, keepdims=True))
    return dx, dshift, dscale, dg


def _dh_grad_x(dp8, w4, xx, dy, norm_g, mrow, b_ada, after=None):
    tm = 512

    def body(p_ref, w_ref, x_ref, dy_ref, g_ref, m_ref, b_ref, after_ref, gx_ref, dsh_ref, dsc_ref, dg_ref):
        @pl.when(pl.program_id(0) == 0)
        def _():
            dsh_ref[...] = jnp.zeros_like(dsh_ref)
            dsc_ref[...] = jnp.zeros_like(dsc_ref)
            dg_ref[...] = jnp.zeros_like(dg_ref)

        dh = None
        for j in range(4):
            for half in range(2):
                term = _tt(p_ref[2 * j + half], w_ref[j, :, half * DA:(half + 1) * DA])
                dh = term if dh is None else dh + term
        scale = m_ref[:, D:2 * D] + b_ref[:, D:2 * D]
        dx, dshift, dscale, dg = _norm_mod_bwd(x_ref[...], dh, g_ref[...], scale)
        gx_ref[...] = dy_ref[...] + dx
        dsh_ref[...] += dshift
        dsc_ref[...] += dscale
        dg_ref[...] += dg

    row = lambda i: (i, 0)
    fixed = lambda i: (0, 0)
    vec = SDS((1, D), f32)
    return _hbm_call(
        body, name="dh_grad_x", out_shape=(SDS((S, D), f32), vec, vec, vec), grid=(S // tm,),
        in_specs=[pl.BlockSpec((8, tm, DA), lambda i: (0, i, 0)), pl.BlockSpec((4, D, D), lambda i: (0, 0, 0)),
                  pl.BlockSpec((tm, D), row), pl.BlockSpec((tm, D), row), pl.BlockSpec((1, D), fixed),
                  pl.BlockSpec((1, 3 * D), fixed), pl.BlockSpec((1, 3 * D), fixed), ANY_SPEC],
        out_specs=(pl.BlockSpec((tm, D), row), pl.BlockSpec((1, D), fixed), pl.BlockSpec((1, D), fixed),
                   pl.BlockSpec((1, D), fixed)),
        compiler_params=_cp(56, dimension_semantics=("arbitrary",)),
    )(dp8, w4, xx, dy, norm_g, mrow, b_ada, b_ada if after is None else after)


def _dhc_sums(dkc_raw, dvc_m, w4, ctx2, norm_g, mrow_c, b_ada, after=None):
    def body(dk_ref, dv_ref, w0_ref, w1_ref, x_ref, g_ref, m_ref, b_ref, after_ref, dsh_ref, dsc_ref, dg_ref):
        dh = _tt(dk_ref[...], w0_ref[:, DA:2 * DA]) + _tt(dv_ref[...], w1_ref[:, 0:DA])
        scale = m_ref[:, D:2 * D] + b_ref[:, D:2 * D]
        _, dshift, dscale, dg = _norm_mod_bwd(x_ref[...], dh, g_ref[...], scale)
        dsh_ref[...] = dshift
        dsc_ref[...] = dscale
        dg_ref[...] = dg

    fixed = lambda i: (0, 0)
    vec = SDS((1, D), f32)
    vspec = pl.BlockSpec((1, D), fixed)
    return _hbm_call(
        body, name="dhc_sums", out_shape=(vec, vec, vec), grid=(1,),
        in_specs=[pl.BlockSpec((L, DA), fixed), pl.BlockSpec((L, DA), fixed),
                  pl.BlockSpec((None, D, D), lambda i: (0, 0, 0)), pl.BlockSpec((None, D, D), lambda i: (1, 0, 0)),
                  pl.BlockSpec((L, D), fixed), vspec, pl.BlockSpec((1, 3 * D), fixed), pl.BlockSpec((1, 3 * D), fixed),
                  ANY_SPEC],
        out_specs=(vspec, vspec, vspec), compiler_params=_cp(32),
    )(dkc_raw, dvc_m, w4, w4, ctx2, norm_g, mrow_c, b_ada, b_ada if after is None else after)


def _rope_tables():
    nf = DH // 4
    inv = np.float32(ROPE_THETA) ** (-np.arange(nf, dtype=np.float32) / np.float32(nf))
    ang_c = np.arange(GW, dtype=np.float32)[:, None] * inv
    ang_r = np.arange(ROWS, dtype=np.float32)[:, None] * inv
    zc, zr = np.zeros((GW, 2 * nf), np.float32), np.zeros((ROWS, 2 * nf), np.float32)
    ct_cos = np.tile(np.concatenate([zc, np.cos(ang_c), np.cos(ang_c)], axis=1), (1, H))
    ct_sin = np.tile(np.concatenate([zc, -np.sin(ang_c), np.sin(ang_c)], axis=1), (1, H))
    rt_cos = np.tile(np.concatenate([np.cos(ang_r), np.cos(ang_r), zr], axis=1), (1, H))
    rt_sin = np.tile(np.concatenate([-np.sin(ang_r), np.sin(ang_r), zr], axis=1), (1, H))
    rep8 = lambda t: np.ascontiguousarray(np.broadcast_to(t[:, None, :], (ROWS, 8, DA))).reshape(ROWS * 8, DA)
    return tuple(jnp.asarray(t, f32) for t in (ct_cos, rep8(rt_cos), ct_sin, rep8(rt_sin)))


def _local_step(xx, ctx2, tgt, mrow, mrow_c, b_ada, norm_g, weights, q_norm_g, k_norm_g, rpb2, conv_w_full, conv_b,
                hooks=None):
    hooks = hooks or {}
    gq = jnp.tile(q_norm_g, (1, H))
    gk = jnp.tile(k_norm_g, (1, H))
    rope = _rope_tables()

    h = _prenorm(xx, norm_g, mrow, b_ada, 256, "prenorm_x", after=weights.get("started"))
    jv = weights["jvec"]
    p = _in_proj_own(h, weights["own"], jv)
    btb = _bias_tiles(rpb2, after=p)
    w4, started = weights["near"](btb)
    p = _in_proj_block(h, w4, p, weights["first"], "in_proj_near", after=started)
    w4 = weights["near2"](w4, p)
    p = _in_proj_block(h, w4, p, weights["second"], "in_proj_near2")
    ctx_norm = {}

    def filler(token):
        ctx_norm["hc"] = _prenorm(ctx2, norm_g, mrow_c, b_ada, L, "prenorm_ctx", after=token)
        return ctx_norm["hc"]

    w4, started = weights["far"](w4, p, filler)
    hc = ctx_norm["hc"]
    p = _in_proj_block(h, w4, p, jv ^ 3, "in_proj_far", after=started)
    pc = _ctx_proj(hc, w4)
    qr, qp, kr, vh = _qk_prep(p, gq, gk, rope)
    kc, vc = _ctx_prep(pc, gk)
    o = _attn_fwd(qr, qp, kr, vh, kc, vc, btb)
    started = weights["out_arrived"](o) if "out_arrived" in weights else None
    conv_g = _conv_fwd(p, conv_w_full, conv_b, after=started)
    w_out_full = weights["out"](conv_g)
    dy, dconv, dp8, do, g_w_out, dgate, loss_sum = _out_proj_loss(o, p, conv_g, w_out_full, xx, tgt, mrow, b_ada)
    started = hooks["g_w_out"](g_w_out) if "g_w_out" in hooks else None
    dp8, g_conv_w, g_conv_b = _conv_bwd(dconv, p, conv_w_full, conv_b, dp8, after=started)
    started = hooks["after_conv"](dp8) if "after_conv" in hooks else None
    dqr, dqp, dkr, dvh, dkc, dvc, dbtb = _attn_bwd(qr, qp, kr, vh, kc, vc, btb, do, after=started)
    dp8, g_gq, g_gk = _qk_bwd(dqr, dqp, dkr, dvh, p, gq, gk, rope, dp8)
    dkc_raw, dvc_m, g_gk_c = _ctx_bwd(dkc, dvc, pc, gk)
    first = hooks.get("first_half", jnp.zeros((1,), i32))
    g_first = _grad_w_in(h, dp8, hc, dkc_raw, dvc_m, first, "grad_w_in_first")
    started = hooks["g_w_in_first"](g_first) if "g_w_in_first" in hooks else None
    if "w_in_pair_sum" in hooks:
        g_second = None
        started = hooks["w_in_pair_sum"](functools.partial(_grad_w_in_pair_sum, h, dp8, hc, dkc_raw, dvc_m, 1 - first))
    else:
        g_second = _grad_w_in(h, dp8, hc, dkc_raw, dvc_m, 1 - first, "grad_w_in_second", after=started)
    dshift_c, dscale_c, dng_c = _dhc_sums(dkc_raw, dvc_m, w4, ctx2, norm_g, mrow_c, b_ada, after=started)
    g_rpb = _bias_bwd(dbtb, after=dshift_c)
    grad_x, dshift, dscale, dng = _dh_grad_x(dp8, w4, xx, dy, norm_g, mrow, b_ada, after=g_rpb)
    return dict(loss_sum=loss_sum, grad_x=grad_x, g_w_in=(g_first, g_second), g_w_out=g_w_out, g_conv_w=g_conv_w,
                g_conv_b=g_conv_b, g_rpb=g_rpb, g_gq=g_gq, g_gk=g_gk, g_gk_c=g_gk_c, dshift=dshift, dscale=dscale,
                dgate=dgate, dng=dng, dshift_c=dshift_c, dscale_c=dscale_c, dng_c=dng_c)


def _pair_sum_w_out(g, r, cvec):
    hr = D // 8

    def body(c_ref, g0, g1, g2, g3, r_ref, t32_ref, tb_ref):
        for q, g_ref in enumerate((g0, g1, g2, g3)):
            t = g_ref[...] + r_ref[q]
            t32_ref[q] = t
            tb_ref[q] = t.astype(bf16)

    gspecs = [pl.BlockSpec((hr, D), lambda i, c, q=q: (2 * q + c[0], 0)) for q in range(4)]
    full = pl.BlockSpec((4, hr, D), lambda i, c: (0, 0, 0))
    grid_spec = pltpu.PrefetchScalarGridSpec(num_scalar_prefetch=1, grid=(1,), in_specs=gspecs + [full],
                                             out_specs=(full, full))
    return _hbm_call(body, name="pair_sum_w_out", out_shape=(SDS((4, hr, D), f32), SDS((4, hr, D), bf16)),
                          grid_spec=grid_spec)(cvec, g, g, g, g, r)


def _chip_sum(t32, r2, jvec, name):
    rows = t32.shape[1]
    tr = min(rows, 128)
    band = lambda k: slice(k * tr, (k + 1) * tr)
    total = lambda t, r: (((t + r[0].astype(f32)) + r[1].astype(f32)) + r[2].astype(f32),)
    return _stream_call(
        total, rows // tr,
        [(t32, (tr, D), lambda ref, k, j: ref.at[j, band(k)]), (r2, (3, tr, D), lambda ref, k, j: ref.at[:, band(k)])],
        [((rows, D), f32, (tr, D), lambda ref, k, j: ref.at[band(k)])], jvec, name)[0]


_PK = {}
_off = 0
for _name, _rows in (("dm", 24), ("dmc", 24), ("dng", 8), ("dng_c", 8), ("gq", 8), ("gk", 8), ("gk_c", 8),
                     ("rpb", H * N_DR), ("conv_b", 8), ("conv_w", 16), ("loss", 8)):
    _PK[_name] = (_off, _off + _rows)
    _off += _rows
PK_ROWS = _off
RS_B_ADA, RS_NORM_G, RS_GQ, RS_GK, RS_RPB, RS_CONV_B, RS_CONV_W, RS_DMC, RS_LOSS, RS_ROWS = (
    0, 24, 32, 40, 48, 168, 176, 192, 216, 224)


def _small_reduce(gathered):
    def body(g_ref, o_ref, dm_ref):
        a0 = _PK["dm"][0]
        dm_ref[...] = jnp.zeros_like(dm_ref)
        for b in range(8):
            for i in range(24):
                dm_ref[b:b + 1, 128 * i:128 * (i + 1)] = g_ref[b, a0 + i:a0 + i + 1, :]
        tot = g_ref[0]
        for b in range(1, 8):
            tot = tot + g_ref[b]

        def rows(name):
            a, z = _PK[name]
            return tot[a:z]

        o_ref[RS_B_ADA:RS_B_ADA + 24] = rows("dm") + rows("dmc")
        o_ref[RS_NORM_G:RS_NORM_G + 8] = rows("dng") + rows("dng_c")
        gq = jnp.broadcast_to(jnp.sum(rows("gq"), axis=0, keepdims=True), (8, 128))
        gk = jnp.broadcast_to(jnp.sum(rows("gk") + rows("gk_c"), axis=0, keepdims=True), (8, 128))
        o_ref[RS_GQ:RS_GQ + 8] = gq + pltpu.roll(gq, DH, 1)
        o_ref[RS_GK:RS_GK + 8] = gk + pltpu.roll(gk, DH, 1)
        o_ref[RS_RPB:RS_RPB + H * N_DR] = rows("rpb")
        o_ref[RS_CONV_B:RS_CONV_B + 8] = rows("conv_b")
        o_ref[RS_CONV_W:RS_CONV_W + 16] = rows("conv_w")
        dmc = rows("dmc")
        o_ref[RS_DMC:RS_DMC + 24] = dmc
        o_ref[RS_LOSS:RS_LOSS + 8] = rows("loss")
        for i in range(24):
            dm_ref[8:9, 128 * i:128 * (i + 1)] = dmc[i:i + 1]

    return _hbm_call(body, name="small_reduce", out_shape=(SDS((RS_ROWS, 128), f32), SDS((16, 3 * D), f32)),
                     in_specs=[VMEM_SPEC], out_specs=(VMEM_SPEC, VMEM_SPEC))(gathered)


def _w_ada_grad(sc16, dm16, w_ada_shard, jvec):
    ncol = w_ada_shard.shape[1]

    def body(j_ref, sc_ref, dm_ref, w_ref, g_ref, part_ref):
        dm = dm_ref[...]
        g_ref[...] = lax.dot_general(sc_ref[...], dm, (((0,), (0,)), ((), ())), precision=HIGHEST,
                                     preferred_element_type=f32)
        part_ref[...] = lax.dot_general(dm[8:16], w_ref[...], (((1,), (1,)), ((), ())), precision=HIGHEST,
                                        preferred_element_type=f32)

    fixed = lambda i, j: (0, 0)
    grid_spec = pltpu.PrefetchScalarGridSpec(
        num_scalar_prefetch=1, grid=(1,),
        in_specs=[pl.BlockSpec((16, D), fixed), pl.BlockSpec((16, ncol), lambda i, j: (0, j[0])),
                  pl.BlockSpec((D, ncol), fixed)],
        out_specs=(pl.BlockSpec((D, ncol), fixed), pl.BlockSpec((8, D), fixed)))
    return _pallas_call(body, name="w_ada_grad", out_shape=(SDS((D, ncol), f32), SDS((8, D), f32)),
                        grid_spec=grid_spec, compiler_params=_cp(40))(jvec, sc16, dm16, w_ada_shard)


def _c_ctx_grad(parts4, c_ctx):
    def body(p_ref, c_ref, o_ref):
        tot = ((p_ref[0] + p_ref[1]) + p_ref[2]) + p_ref[3]
        o_ref[...] = tot[0:1] * _dsilu(c_ref[...].reshape(1, D))

    return _pallas_call(body, name="c_ctx_grad", out_shape=SDS((1, D), f32), in_specs=[VMEM_SPEC, VMEM_SPEC],
                        out_specs=VMEM_SPEC)(parts4, c_ctx)


def _adamw(w, g, m, v, name, after=None):
    return _adamw_stream(w, [g], m, v, None, name, after)


def _adamw_halves(w, g_mine, g_other, m, v, cvec, name, after=None):
    return _adamw_stream(w, [g_mine, g_other], m, v, cvec, name, after)


def _adamw_stream(w, g_parts, m, v, cvec, name, after):
    rows, cols = w.shape
    parts = len(g_parts)
    span = rows // parts
    chunk = min(128, span // 2)
    per = span // chunk
    outs_g = parts == 2
    pin = rows * cols * 4 > (1 << 20)

    def body(*refs):
        c_ref, w_ref = refs[0], refs[1]
        g_refs = refs[2:2 + parts]
        m_ref, v_ref = refs[2 + parts], refs[3 + parts]
        out_refs = refs[5 + parts:5 + parts + 3 + outs_g]
        bw, bg, bm, bv, bd, sem_in, sem_out = refs[5 + parts + 3 + outs_g:]
        c = c_ref[0]
        loads, stores = [], []
        for p in range(parts):
            first = (c if p == 0 else 1 - c) * span if parts == 2 else 0
            for kk in range(per):
                r = pl.ds(pl.multiple_of(first + kk * chunk, chunk), chunk)
                g_src = g_refs[p].at[kk * chunk:(kk + 1) * chunk]
                cps = [pltpu.make_async_copy(src, dst.at[r], sem_in.at[a, p, kk])
                       for a, (src, dst) in enumerate(((w_ref.at[r], bw), (g_src, bg), (m_ref.at[r], bm),
                                                       (v_ref.at[r], bv)))]
                for cp in cps:
                    cp.start()
                loads.append((r, p, kk, cps))
        for r, p, kk, cps in loads:
            for cp in cps:
                cp.wait()
            bd[r], bm[r], bv[r] = _adam_math(bw[r], bg[r], bm[r], bv[r])
            for a, (src, dst) in enumerate(zip(([bg] if outs_g else []) + [bd, bm, bv], out_refs)):
                cp = pltpu.make_async_copy(src.at[r], dst.at[r], sem_out.at[a, p, kk])
                cp.start()
                stores.append(cp)
        for cp in stores:
            cp.wait()

    shp = pltpu.HBM((rows, cols), f32) if pin else SDS((rows, cols), f32)
    spec = HBM_SPEC if pin else ANY_SPEC
    buf = pltpu.VMEM((rows, cols), f32)
    args = [pltpu.with_memory_space_constraint(a, pltpu.HBM) if pin else a for a in (w, *g_parts, m, v)]
    cvec = jnp.zeros((1,), i32) if cvec is None else cvec
    after_big = after is not None and after.size * 4 > (1 << 20)
    after = pltpu.with_memory_space_constraint(after, pltpu.HBM) if after_big else after
    return _pallas_call(
        body, name=name, out_shape=(shp,) * (3 + outs_g),
        in_specs=[SMEM_SPEC] + [spec] * (3 + parts) + [HBM_SPEC if after_big else ANY_SPEC],
        out_specs=(spec,) * (3 + outs_g),
        scratch_shapes=[buf] * 5 + [pltpu.SemaphoreType.DMA((4, parts, per)), pltpu.SemaphoreType.DMA((4, parts, per))],
        compiler_params=_cp(5 * rows * cols * 4 // (1 << 20) + 8),
    )(cvec, *args, cvec if after is None else after)


def _adam_math(w, g, m, v):
    m2 = ADAM_B1 * m + (1.0 - ADAM_B1) * g
    v2 = ADAM_B2 * v + (1.0 - ADAM_B2) * jnp.square(g)
    m_hat = m2 / (1.0 - ADAM_B1 ** ADAM_STEP)
    v_hat = v2 / (1.0 - ADAM_B2 ** ADAM_STEP)
    return -ADAM_LR * (m_hat / (jnp.sqrt(v_hat) + ADAM_EPS) + ADAM_WD * w), m2, v2


def _adamw_small(red, g_c_ctx, jvec, ws, ms, vs):
    n = len(ws)

    def body(*refs):
        red_ref, gc_ref, j_ref = refs[:3]
        w_refs, m_refs, v_refs = refs[3:3 + n], refs[3 + n:3 + 2 * n], refs[3 + 2 * n:3 + 3 * n]
        outs = refs[3 + 3 * n:]
        g_out, d_out, m_out, v_out = outs[:n], outs[n:2 * n], outs[2 * n:3 * n], outs[3 * n:]
        chip = j_ref[0]
        lanes = lambda i: (slice(None), slice(128 * i, 128 * (i + 1)))
        row = lambda r0, i: (lambda: red_ref[r0 + i:r0 + i + 1, :])
        whole = (slice(None), slice(None))
        chunks = [
            [((slice(None),), lambda: gc_ref[...].reshape(D))],
            [(lanes(i), row(RS_B_ADA, i)) for i in range(3 * D // 128)],
            [(lanes(i), row(RS_NORM_G, i)) for i in range(D // 128)],
            [(whole, lambda: red_ref[RS_GQ:RS_GQ + 1, 0:DH])],
            [(whole, lambda: red_ref[RS_GK:RS_GK + 1, 0:DH])],
            [((dr,), (lambda dr=dr: red_ref[pl.ds(RS_RPB + dr, H, stride=N_DR), 0:N_DC])) for dr in range(N_DR)],
            [((r,), (lambda r=r: red_ref[pl.ds(RS_CONV_W + 4 * r + chip, 1), :])) for r in range(3)],
            [(lanes(i), row(RS_CONV_B, i)) for i in range(DC // 128)],
        ]
        for a in range(n):
            for idx, grad in chunks[a]:
                g = grad()
                d, m2, v2 = _adam_math(w_refs[a][idx], g, m_refs[a][idx], v_refs[a][idx])
                g_out[a][idx] = g
                d_out[a][idx] = d
                m_out[a][idx] = m2
                v_out[a][idx] = v2

    shapes = [SDS(w.shape, f32) for w in ws]
    res = _pallas_call(body, name="adamw_small", out_shape=shapes * 4,
                       in_specs=[VMEM_SPEC, VMEM_SPEC, SMEM_SPEC] + [VMEM_SPEC] * (3 * n),
                       out_specs=[VMEM_SPEC] * (4 * n))(red, g_c_ctx, jvec, *ws, *ms, *vs)
    return [list(res[k * n:(k + 1) * n]) for k in range(4)]


def _rows128(a):
    return a.reshape(-1, 128)


def kernel(x, c, ctx, c_ctx, w_ada, b_ada, norm_g, w_in, q_norm_g, k_norm_g, rpb, conv_w, conv_b, w_out, loss_target, m_c_ctx, m_w_ada, m_b_ada, m_norm_g, m_w_in, m_q_norm_g, m_k_norm_g, m_rpb, m_conv_w, m_conv_b, m_w_out, v_c_ctx, v_w_ada, v_b_ada, v_norm_g, v_w_in, v_q_norm_g, v_k_norm_g, v_rpb, v_conv_w, v_conv_b, v_w_out):
    xi, yi, ci = lax.axis_index("x"), lax.axis_index("y"), lax.axis_index("c")
    dev = 4 * xi + 2 * yi + ci
    chip = 2 * xi + yi
    cvec = jnp.reshape(ci, (1,)).astype(i32)
    jvec = jnp.reshape(chip, (1,)).astype(i32)
    w_ada_s = w_ada[0]
    ncol = w_ada_s.shape[1]

    gc = _split_start([_to_slot(c.reshape(8, 128), 8, dev)], [], 7, _gather8_copies, "gather_c_start")
    wo4c = _cast_to_slot(w_out[0], jvec, "cast_w_out", after=gc[3])
    w4c = _cast_to_slot(w_in[0], jvec, "cast_w_in", after=wo4c)
    (c8,), _ = _split_wait(gc[0], gc[1], [gc[2]], [], w4c, _gather8_copies, "gather_c_wait")
    cc = jnp.concatenate([c8.reshape(8, D), c_ctx.reshape(1, D), jnp.zeros((7, D), f32)], axis=0)
    m_shard, sc16 = _adaln_shard(cc, w_ada_s)

    conv_w_pad = jnp.pad(conv_w[0], ((0, 5), (0, 0)))
    gm = _split_start([_to_slot(m_shard, 4, chip), _to_slot(conv_w_pad, 4, chip)], [], 6, _gather4_copies,
                      "gather_mod_start")

    all_k = [(0, 0, 0), (0, 0, 1), (0, 0, 2)]
    sem_a, rem_a, w4s, token = _split_start([w4c], [], 1, _near_copies, "weights_near_start", after=gm[4])
    (m4, cw4), _ = _split_wait(gm[0], gm[1], [gm[2], gm[3]], [], token, _gather4_copies, "gather_mod_wait")
    m_full = jnp.transpose(m4, (1, 0, 2)).reshape(16, 4 * ncol)
    mrow = lax.dynamic_slice(m_full, (dev, 0), (1, 3 * D))
    mrow_c = m_full[8:9]
    conv_w_full = jnp.transpose(cw4[:, 0:3, :], (1, 0, 2)).reshape(3, DC)
    waves = {}

    def near(after):
        (w4w,), _ = _split_wait(sem_a, rem_a, [w4s], [], after, _near_copies, "weights_near_wait")
        sem_b, rem_b, w4b, started = _split_start([w4w], [], 2, _pass_relay_copies, "weights_pass_start")
        waves["pass"] = (sem_b, rem_b)
        return w4b, started

    def near2(w4, after):
        (w4w,), _ = _split_wait(*waves["pass"], [w4], [], after, _pass_copy, "weights_pass_wait")
        return w4w

    def far(w4, after, filler):
        (w4w,), _ = _split_wait(*waves["pass"], [w4], [], after, _relay_copy, "weights_far_wait")
        w4x, sem_c, rem_c, wo4s, started = _forward_then_start(w4w, wo4c, all_k, "weights_far_forward_out_start")
        (w4f,), _ = _split_wait(sem_c, rem_c, [w4x], [], filler(started), _diag_forward_copy,
                                "weights_far_forward_wait")
        waves["out"] = (sem_c, rem_c, wo4s)
        return w4f, started

    def w_out_arrived(after):
        sem_c, rem_c, wo4s = waves["out"]
        (wow,) = _halves_wait(sem_c, rem_c, [wo4s], after, all_k, "weights_out_wait")
        sem_d, rem_d, wof, started = _split_start([wow], [], 3, _forward_copies, "weights_out_forward_start")
        waves["out_forward"] = (sem_d, rem_d, wof)
        return started

    def w_out_gathered(after):
        sem_d, rem_d, wof = waves["out_forward"]
        (wo,), _ = _split_wait(sem_d, rem_d, [wof], [], after, _forward_copies, "weights_out_forward_wait")
        return wo.reshape(D, D)

    weights = dict(own=w_in[0], jvec=jvec, started=token, first=jvec ^ (1 + cvec), second=jvec ^ (2 - cvec),
                   near=near, near2=near2, far=far, out_arrived=w_out_arrived, out=w_out_gathered)

    exchange = _exchange_copies
    pending = {}

    def on_g_w_out(g_w_out):
        out = _split_start([g_w_out], [SDS((4, D // 8, D), f32)], 4, exchange, "grad_out_pair_start")
        pending["ex_out"] = out
        return out[4]

    def after_conv(dp8):
        ssem_o, rsem_o, g_o, land_o, _ = pending["ex_out"]
        (g_o,), (ex_o,) = _split_wait(ssem_o, rsem_o, [g_o], [land_o], dp8, exchange, "grad_out_pair_wait")
        to32, tob = _pair_sum_w_out(g_o, ex_o, cvec)
        out = _split_start([tob], [SDS((3, D // 8, D), bf16)], 3, _scatter_copies, "grad_out_chip_start")
        pending["sc_out"] = (out, to32)
        return out[4]

    def on_g_w_in_first(g_first):
        out = _split_start([g_first], [SDS((4, D // 2, D), f32)], 4, _block_exchange_copies, "grad_pair_start")
        pending["ex"] = (out[0], out[1], [out[2]], [out[3]])
        return out[4]

    def on_w_in_pair_sum(pair_sum):
        ex_ssem, ex_rsem, ex_srcs, ex_lands = pending["ex"]
        t32, tb = pair_sum(ex_srcs[0], ex_lands[0], ex_ssem, ex_rsem)
        out = _split_start([tb], [SDS((3, D // 2, D), bf16)], 3, _scatter_copies, "grad_chip_start")
        pending["sc_in"] = (out, t32)
        return out[4]

    r = _local_step(x[0], ctx[0], loss_target[0], mrow, mrow_c, b_ada, norm_g, weights, q_norm_g, k_norm_g,
                    rpb[0], conv_w_full, conv_b,
                    dict(g_w_out=on_g_w_out, after_conv=after_conv, first_half=1 - cvec,
                         g_w_in_first=on_g_w_in_first, w_in_pair_sum=on_w_in_pair_sum))
    sc_in, t32 = pending["sc_in"]
    _, (r2,) = _split_wait(sc_in[0], sc_in[1], [sc_in[2]], [sc_in[3]], r["dng"], _scatter_copies, "grad_chip_wait")
    u_in = _chip_sum(t32, r2, jvec, "chip_sum_w_in")
    sc_o, to32 = pending["sc_out"]
    _, (ro2,) = _split_wait(sc_o[0], sc_o[1], [sc_o[2]], [sc_o[3]], u_in, _scatter_copies, "grad_out_chip_wait")
    u_out = _chip_sum(to32, ro2, jvec, "chip_sum_w_out")
    swap = _split_start([u_in, u_out], [SDS(u_in.shape, f32), SDS(u_out.shape, f32)], 2, _swap_copies,
                        "grad_pair_swap_start")

    dm = jnp.concatenate([r["dshift"], r["dscale"], r["dgate"]], axis=1)
    dmc = jnp.concatenate([r["dshift_c"], r["dscale_c"], jnp.zeros((1, D), f32)], axis=1)
    pack_parts = [_rows128(dm), _rows128(dmc), _rows128(r["dng"]), _rows128(r["dng_c"]), _rows128(r["g_gq"]),
                  _rows128(r["g_gk"]), _rows128(r["g_gk_c"]), r["g_rpb"].reshape(H * N_DR, 128),
                  _rows128(r["g_conv_b"]), _rows128(r["g_conv_w"][0:3]), jnp.pad(r["loss_sum"], ((0, 0), (0, 127)))]
    pack = jnp.concatenate([jnp.pad(p, ((0, -p.shape[0] % 8), (0, 0))) for p in pack_parts], axis=0)
    assert pack.shape[0] == PK_ROWS
    gs = _split_start([_to_slot(pack, 8, dev)], [], 7, _gather8_copies, "gather_small_start", after=swap[6])
    (u_in, u_out), (o_in, o_out) = _split_wait(swap[0], swap[1], swap[2:4], swap[4:6], gs[3], _swap_copies,
                                               "grad_pair_swap_wait")
    g_w_in_s, d_w_in, nm_w_in, nv_w_in = _adamw_halves(w_in[0], u_in, o_in, m_w_in[0], v_w_in[0], cvec, "adamw_w_in",
                                                       after=gs[3])
    g_w_out_s, d_w_out, nm_w_out, nv_w_out = _adamw_halves(w_out[0], u_out, o_out, m_w_out[0], v_w_out[0], cvec,
                                                           "adamw_w_out", after=nm_w_in)
    (gathered,), _ = _split_wait(gs[0], gs[1], [gs[2]], [], nm_w_out, _gather8_copies, "gather_small_wait")
    red, dm16 = _small_reduce(gathered)
    loss = red[RS_LOSS, 0] * (0.5 / D)

    g_w_ada_s, cpart = _w_ada_grad(sc16, dm16, w_ada_s, jvec)
    gcp = _split_start([_to_slot(cpart, 4, chip)], [], 3, _gather4_copies, "gather_c_ctx_parts_start")
    d_w_ada, nm_w_ada, nv_w_ada = _adamw(w_ada_s, g_w_ada_s, m_w_ada[0], v_w_ada[0], "adamw_w_ada", after=gcp[3])
    (cparts4,), _ = _split_wait(gcp[0], gcp[1], [gcp[2]], [], nm_w_ada, _gather4_copies, "gather_c_ctx_parts_wait")
    g_c_ctx = _c_ctx_grad(cparts4, c_ctx)

    t_rpb = lambda a: jnp.transpose(a, (0, 2, 1, 3)).reshape(N_DR, H, N_DC)
    t_cw = lambda a: jnp.transpose(a, (1, 0, 2))
    small = _adamw_small(
        red, g_c_ctx, jvec,
        [c_ctx, b_ada, norm_g, q_norm_g, k_norm_g, t_rpb(rpb), t_cw(conv_w), conv_b],
        [m_c_ctx, m_b_ada, m_norm_g, m_q_norm_g, m_k_norm_g, t_rpb(m_rpb), t_cw(m_conv_w), m_conv_b],
        [v_c_ctx, v_b_ada, v_norm_g, v_q_norm_g, v_k_norm_g, t_rpb(v_rpb), t_cw(v_conv_w), v_conv_b])
    for kind in small:
        kind[5] = jnp.transpose(kind[5].reshape(1, N_DR, H, N_DC), (0, 2, 1, 3))
        kind[6] = jnp.transpose(kind[6], (1, 0, 2))

    def ordered(kind, big_w_ada, big_w_in, big_w_out):
        s_c_ctx, s_b_ada, s_norm_g, s_q, s_k, s_rpb, s_conv_w, s_conv_b = small[kind]
        return [s_c_ctx, big_w_ada[None], s_b_ada, s_norm_g, big_w_in[None], s_q, s_k, s_rpb, s_conv_w,
                s_conv_b, big_w_out[None]]

    grads = ordered(0, g_w_ada_s, g_w_in_s, g_w_out_s)
    deltas = ordered(1, d_w_ada, d_w_in, d_w_out)
    new_m = ordered(2, nm_w_ada, nm_w_in, nm_w_out)
    new_v = ordered(3, nv_w_ada, nv_w_in, nv_w_out)
    return (loss, r["grad_x"][None], *grads, *deltas, *new_m, *new_v)
```

```python
import functools

import jax
import jax.numpy as jnp
import numpy as np
from jax import lax
from jax.experimental import pallas as pl
from jax.experimental.pallas import tpu as pltpu

f32, bf16, i32 = jnp.float32, jnp.bfloat16, jnp.int32
MESH = pl.DeviceIdType.MESH
HIGHEST = lax.Precision.HIGHEST

D = 1024
S = 2048
L = 256
GW = 64
ROWS = S // GW
H = 8
DH = 64
DA = H * DH
DC = 512
WIN_H, WIN_W = 8, 16
N_DR, N_DC = 2 * WIN_H - 1, 2 * WIN_W - 1
RMS_EPS = 1e-6
ROPE_THETA = 10000.0
QK_SCALE = DH ** -0.5
NEG = -1e30

QB = 128
NQB = S // QB
KR = 9
KB = KR * GW
TILE_GEOM = ((0, 0), (2, 0), (4, 0), (28, 23), (30, 23))
NT = len(TILE_GEOM)

ADAM_LR, ADAM_B1, ADAM_B2, ADAM_EPS, ADAM_WD, ADAM_STEP = 0.001, 0.9, 0.999, 1e-08, 0.01, 10

VMEM_SPEC = pl.BlockSpec(memory_space=pltpu.VMEM)
ANY_SPEC = pl.BlockSpec(memory_space=pl.ANY)
SMEM_SPEC = pl.BlockSpec(memory_space=pltpu.SMEM)
SDS = jax.ShapeDtypeStruct


_pallas_call = pl.pallas_call


def _hbm_call(body, *, out_shape, in_specs=None, out_specs=None, grid_spec=None, **kw):
    n_pre = 0
    if grid_spec is not None:
        ispecs, ospecs, n_pre = grid_spec.in_specs, grid_spec.out_specs, grid_spec.num_scalar_prefetch
        kw["grid_spec"] = grid_spec
    else:
        ispecs, ospecs = in_specs, out_specs
        kw.update(in_specs=in_specs, out_specs=out_specs)

    def blocked(spec):
        return isinstance(spec, pl.BlockSpec) and spec.block_shape is not None

    single = not isinstance(out_shape, (tuple, list))
    shapes = [out_shape] if single else list(out_shape)
    ospec_list = list(ospecs) if isinstance(ospecs, (tuple, list)) else [ospecs]
    shapes = [pltpu.HBM(s.shape, s.dtype) if blocked(sp) else s for s, sp in zip(shapes, ospec_list)]
    call = _pallas_call(body, out_shape=shapes[0] if single else tuple(shapes), **kw)

    def run(*args):
        arrays = [pltpu.with_memory_space_constraint(a, pltpu.HBM) if blocked(sp) else a
                  for a, sp in zip(args[n_pre:], ispecs)]
        return call(*args[:n_pre], *arrays)

    return run


def _cp(vmem_mb=None, **kw):
    if vmem_mb is not None:
        kw["vmem_limit_bytes"] = vmem_mb << 20
    return pltpu.CompilerParams(**kw)


def _silu(z):
    return z * jax.nn.sigmoid(z)


def _dsilu(z):
    sg = jax.nn.sigmoid(z)
    return sg * (1.0 + z * (1.0 - sg))


def _row_start(i):
    return min(max(i - WIN_H // 2, 0), ROWS - WIN_H)


def _my_pos():
    return lax.axis_index("x"), lax.axis_index("y"), lax.axis_index("c")


def _flip(v, bit):
    return 1 - v if bit else v


def _swap_copies(srcs, lands, ssem, rsem):
    x, y, c = _my_pos()
    return [pltpu.make_async_remote_copy(src_ref=srcs[a], dst_ref=lands[a], send_sem=ssem.at[a], recv_sem=rsem.at[a],
                                         device_id=(x, y, 1 - c), device_id_type=MESH) for a in range(len(srcs))]


HBM_SPEC = pl.BlockSpec(memory_space=pltpu.HBM)
SEM_SPEC = pl.BlockSpec(memory_space=pltpu.SEMAPHORE)
DATAFLOW = pltpu.SideEffectType.DATAFLOW_SIDE_EFFECTING


def _peer_chips(x, y, c):
    out = []
    for k in range(1, 4):
        px, py = _flip(x, (k >> 1) & 1), _flip(y, k & 1)
        out.append(((px, py, c), 2 * px + py))
    return out


def _half_copies(srcs, dsts, ssem, rsem, which):
    x, y, c = _my_pos()
    j = 2 * x + y
    peers = _peer_chips(x, y, c)
    pairs = []
    for pos, group, k in which:
        half = srcs[pos].shape[1] // 2
        mine = pl.ds(pl.multiple_of(c * half, 8), half)
        dev, pj = peers[k]
        sem = 3 * group + k
        send = pltpu.make_async_remote_copy(src_ref=srcs[pos].at[j, mine], dst_ref=dsts[pos].at[j, mine],
                                            send_sem=ssem.at[sem], recv_sem=rsem.at[sem], device_id=dev,
                                            device_id_type=MESH)
        arrive = pltpu.make_async_remote_copy(src_ref=srcs[pos].at[j, mine], dst_ref=dsts[pos].at[pj, mine],
                                              send_sem=ssem.at[sem], recv_sem=rsem.at[sem], device_id=dev,
                                              device_id_type=MESH)
        pairs.append((send, arrive))
    return pairs


def _halves_wait(ssem, rsem, bigs, after, which, name):
    nb = len(bigs)

    def body(*refs):
        b_in = refs[:nb]
        ssem_ref, rsem_ref = refs[nb], refs[nb + 1]
        for send, arrive in _half_copies(b_in, b_in, ssem_ref, rsem_ref, which):
            send.wait_send()
            arrive.wait_recv()

    return _hbm_call(
        body, name=name, out_shape=tuple(pltpu.HBM(b.shape, b.dtype) for b in bigs),
        in_specs=[HBM_SPEC] * nb + [SEM_SPEC, SEM_SPEC, ANY_SPEC], out_specs=tuple([HBM_SPEC] * nb),
        input_output_aliases={a: a for a in range(nb)}, compiler_params=_cp(has_side_effects=DATAFLOW),
    )(*bigs, ssem, rsem, after)


FORWARD_SEM = 3


def _diag_forward_copy(srcs, dsts, ssem, rsem):
    x, y, c = _my_pos()
    half = srcs[0].shape[1] // 2
    diag = 3 - (2 * x + y)
    mine = pl.ds(pl.multiple_of(c * half, 8), half)
    other = pl.ds(pl.multiple_of((1 - c) * half, 8), half)
    return [_Copy(srcs[0].at[diag, mine], dsts[0].at[diag, mine], dsts[0].at[diag, other], ssem.at[FORWARD_SEM],
                  rsem.at[FORWARD_SEM], (x, y, 1 - c))]


def _forward_then_start(fwd, big, order, name):
    def body(f_in, b_in, f_out, ssem, rsem, b_out, token):
        _diag_forward_copy([f_in], [f_out], ssem, rsem)[0].start()
        for send, _ in _half_copies([b_in], [b_out], ssem, rsem, order):
            send.start()
        token[...] = jnp.zeros_like(token)

    n_sem = FORWARD_SEM + 1
    out_shape = (pltpu.HBM(fwd.shape, fwd.dtype), pltpu.SemaphoreType.DMA((n_sem,)), pltpu.SemaphoreType.DMA((n_sem,)),
                 pltpu.HBM(big.shape, big.dtype), SDS((8, 128), f32))
    return _hbm_call(
        body, name=name, out_shape=out_shape, in_specs=[HBM_SPEC, HBM_SPEC],
        out_specs=(HBM_SPEC, SEM_SPEC, SEM_SPEC, HBM_SPEC, VMEM_SPEC), input_output_aliases={0: 0, 1: 3},
        compiler_params=_cp(has_side_effects=DATAFLOW),
    )(*[pltpu.with_memory_space_constraint(b, pltpu.HBM) for b in (fwd, big)])


def _cast_to_slot(w, jvec, name, after=None):
    rows, cols = w.shape
    tr = 128
    band = lambda k: slice(k * tr, (k + 1) * tr)
    return _stream_call(
        lambda x: (x.astype(bf16),), rows // tr, [(w, (tr, cols), lambda ref, k, j: ref.at[band(k)])],
        [((4, rows, cols), bf16, (tr, cols), lambda ref, k, j: ref.at[j, band(k)])], jvec, name, after)[0]


def _stream_call(fn, n_chunks, srcs, outs, vec, name, after=None):
    ns, no = len(srcs), len(outs)

    def body(*refs):
        s = refs[0][0]
        in_refs, out_refs = refs[1:1 + ns], refs[2 + ns:2 + ns + no]
        in_bufs, out_bufs = refs[2 + ns + no:2 + 2 * ns + no], refs[2 + 2 * ns + no:2 + 2 * ns + 2 * no]
        sem_in, sem_out = refs[2 + 2 * ns + 2 * no:]
        loads, stores = [], []
        for k in range(n_chunks):
            cps = [pltpu.make_async_copy(srcs[a][2](in_refs[a], k, s), in_bufs[a].at[k], sem_in.at[a, k])
                   for a in range(ns)]
            for cp in cps:
                cp.start()
            loads.append(cps)
        for k, cps in enumerate(loads):
            for cp in cps:
                cp.wait()
            for b, val in enumerate(fn(*[buf[k] for buf in in_bufs])):
                out_bufs[b][k] = val
                cp = pltpu.make_async_copy(out_bufs[b].at[k], outs[b][3](out_refs[b], k, s), sem_out.at[b, k])
                cp.start()
                stores.append(cp)
        for cp in stores:
            cp.wait()

    in_bufs = [pltpu.VMEM((n_chunks,) + tuple(shape), a.dtype) for a, shape, _ in srcs]
    out_bufs = [pltpu.VMEM((n_chunks,) + tuple(cshape), dtype) for _, dtype, cshape, _ in outs]
    n_bytes = sum(np.prod(b.shape) * np.dtype(b.dtype).itemsize for b in in_bufs + out_bufs)
    after_big = after is not None and after.size * after.dtype.itemsize > (1 << 20)
    after = pltpu.with_memory_space_constraint(after, pltpu.HBM) if after_big else after
    res = _pallas_call(
        body, name=name, out_shape=tuple(pltpu.HBM(shape, dtype) for shape, dtype, _, _ in outs),
        in_specs=[SMEM_SPEC] + [HBM_SPEC] * ns + [HBM_SPEC if after_big else ANY_SPEC], out_specs=(HBM_SPEC,) * no,
        scratch_shapes=in_bufs + out_bufs + [pltpu.SemaphoreType.DMA((ns, n_chunks)),
                                             pltpu.SemaphoreType.DMA((no, n_chunks))],
        compiler_params=_cp(int(n_bytes) // (1 << 20) + 8),
    )(vec, *[pltpu.with_memory_space_constraint(a, pltpu.HBM) for a, _, _ in srcs], vec if after is None else after)
    return res


def _exchange_copies(srcs, lands, ssem, rsem):
    x, y, c = _my_pos()
    half = srcs[0].shape[0] // 8
    cps = []
    for jb in range(4):
        src = srcs[0].at[pl.ds(pl.multiple_of((2 * jb + 1 - c) * half, 8), half)]
        cps.append(pltpu.make_async_remote_copy(src_ref=src, dst_ref=lands[0].at[jb], send_sem=ssem.at[jb],
                                                recv_sem=rsem.at[jb], device_id=(x, y, 1 - c), device_id_type=MESH))
    return cps


def _block_exchange_copies(srcs, lands, ssem, rsem):
    x, y, c = _my_pos()
    return [pltpu.make_async_remote_copy(src_ref=srcs[0].at[jb], dst_ref=lands[0].at[jb], send_sem=ssem.at[jb],
                                         recv_sem=rsem.at[jb], device_id=(x, y, 1 - c), device_id_type=MESH)
            for jb in range(4)]


def _scatter_copies(srcs, lands, ssem, rsem):
    x, y, c = _my_pos()
    cps = []
    for a in range(len(srcs)):
        for k, (dev, pj) in enumerate(_peer_chips(x, y, c)):
            cps.append(pltpu.make_async_remote_copy(src_ref=srcs[a].at[pj], dst_ref=lands[a].at[k],
                                                    send_sem=ssem.at[3 * a + k], recv_sem=rsem.at[3 * a + k],
                                                    device_id=dev, device_id_type=MESH))
    return cps


class _Copy:
    def __init__(self, src, dst, arrive, ssem, rsem, dev):
        make = lambda to: pltpu.make_async_remote_copy(src_ref=src, dst_ref=to, send_sem=ssem, recv_sem=rsem,
                                                       device_id=dev, device_id_type=MESH)
        send, arrival = make(dst), make(arrive)
        self.start, self.wait_send, self.wait_recv = send.start, send.wait_send, arrival.wait_recv


def _toward(x, y, along_x):
    return x + along_x * (1 - 2 * x), y + (1 - along_x) * (1 - 2 * y)


def _near_copies(srcs, dsts, ssem, rsem):
    x, y, c = _my_pos()
    px, py = _toward(x, y, c)
    j = 2 * x + y
    return [_Copy(srcs[0].at[j], dsts[0].at[j], dsts[0].at[2 * px + py], ssem.at[0], rsem.at[0], (px, py, c))]


def _pass_copy(srcs, dsts, ssem, rsem):
    x, y, c = _my_pos()
    px, py = _toward(x, y, c)
    qx, qy = _toward(x, y, 1 - c)
    got = 2 * px + py
    return [_Copy(srcs[0].at[got], dsts[0].at[got], dsts[0].at[2 * qx + qy], ssem.at[0], rsem.at[0], (x, y, 1 - c))]


def _relay_copy(srcs, dsts, ssem, rsem):
    x, y, c = _my_pos()
    px, py = _toward(x, y, c)
    qx, qy = _toward(x, y, 1 - c)
    half = srcs[0].shape[1] // 2
    mine = pl.ds(pl.multiple_of(c * half, 8), half)
    got, diag = 2 * px + py, 3 - (2 * x + y)
    return [_Copy(srcs[0].at[got, mine], dsts[0].at[got, mine], dsts[0].at[diag, mine], ssem.at[1], rsem.at[1],
                  (qx, qy, c))]


def _forward_copies(srcs, dsts, ssem, rsem):
    x, y, c = _my_pos()
    half = srcs[0].shape[1] // 2
    mine = pl.ds(pl.multiple_of(c * half, 8), half)
    other = pl.ds(pl.multiple_of((1 - c) * half, 8), half)
    return [_Copy(srcs[0].at[pj, mine], dsts[0].at[pj, mine], dsts[0].at[pj, other], ssem.at[k], rsem.at[k],
                  (x, y, 1 - c)) for k, (_, pj) in enumerate(_peer_chips(x, y, c))]


def _pass_relay_copies(srcs, dsts, ssem, rsem):
    return _pass_copy(srcs, dsts, ssem, rsem) + _relay_copy(srcs, dsts, ssem, rsem)


def _gather8_copies(srcs, dsts, ssem, rsem):
    x, y, c = _my_pos()
    me = 4 * x + 2 * y + c
    cps = []
    for a in range(len(srcs)):
        for k in range(1, 8):
            tgt = (_flip(x, (k >> 2) & 1), _flip(y, (k >> 1) & 1), _flip(c, k & 1))
            cps.append(_Copy(srcs[a].at[me], dsts[a].at[me], dsts[a].at[4 * tgt[0] + 2 * tgt[1] + tgt[2]],
                             ssem.at[7 * a + k - 1], rsem.at[7 * a + k - 1], tgt))
    return cps


def _gather4_copies(srcs, dsts, ssem, rsem):
    x, y, c = _my_pos()
    j = 2 * x + y
    cps = []
    for a in range(len(srcs)):
        for k, (dev, pj) in enumerate(_peer_chips(x, y, c)):
            cps.append(_Copy(srcs[a].at[j], dsts[a].at[j], dsts[a].at[pj], ssem.at[3 * a + k], rsem.at[3 * a + k], dev))
    return cps


def _to_slot(a, n, i):
    return lax.dynamic_update_slice(jnp.zeros((n,) + a.shape, a.dtype), a[None], (i,) + (0,) * a.ndim)


def _split_start(srcs, land_shapes, n_cp, make, name, after=None):
    ns, nl = len(srcs), len(land_shapes)
    n_in = ns + nl + (after is not None)

    def body(*refs):
        s_in = refs[:ns]
        ssem, rsem = refs[n_in], refs[n_in + 1]
        s_out = refs[n_in + 2:n_in + 2 + ns]
        l_out = refs[n_in + 2 + ns:n_in + 2 + ns + nl]
        token = refs[n_in + 2 + ns + nl]
        for cp in make(s_in, l_out if nl else s_out, ssem, rsem):
            cp.start()
        token[...] = jnp.zeros_like(token)

    lands = [pltpu.with_memory_space_constraint(lax.empty(sh.shape, sh.dtype), pltpu.HBM) for sh in land_shapes]
    out_shape = (pltpu.SemaphoreType.DMA((n_cp,)), pltpu.SemaphoreType.DMA((n_cp,)),
                 *[pltpu.HBM(b.shape, b.dtype) for b in srcs], *[pltpu.HBM(b.shape, b.dtype) for b in land_shapes],
                 SDS((8, 128), f32))
    return _hbm_call(
        body, name=name, out_shape=out_shape, in_specs=[HBM_SPEC] * (ns + nl) + [ANY_SPEC] * (after is not None),
        out_specs=(SEM_SPEC, SEM_SPEC, *[HBM_SPEC] * (ns + nl), VMEM_SPEC),
        input_output_aliases={i: 2 + i for i in range(ns + nl)}, compiler_params=_cp(has_side_effects=DATAFLOW),
    )(*[pltpu.with_memory_space_constraint(b, pltpu.HBM) for b in srcs], *lands, *([] if after is None else [after]))


def _split_wait(ssem, rsem, srcs, lands, after, make, name):
    ns, nl = len(srcs), len(lands)

    def body(*refs):
        s_in, l_in = refs[:ns], refs[ns:ns + nl]
        ssem_ref, rsem_ref = refs[ns + nl], refs[ns + nl + 1]
        for cp in make(s_in, l_in if nl else s_in, ssem_ref, rsem_ref):
            cp.wait_send()
            cp.wait_recv()

    outs = _hbm_call(
        body, name=name, out_shape=tuple(pltpu.HBM(b.shape, b.dtype) for b in (*srcs, *lands)),
        in_specs=[HBM_SPEC] * (ns + nl) + [SEM_SPEC, SEM_SPEC, ANY_SPEC], out_specs=tuple([HBM_SPEC] * (ns + nl)),
        input_output_aliases={i: i for i in range(ns + nl)}, compiler_params=_cp(has_side_effects=DATAFLOW),
    )(*srcs, *lands, ssem, rsem, after)
    return list(outs[:ns]), list(outs[ns:])


def _adaln_shard(cc, w_ada_shard):
    def body(c_ref, w_ref, m_ref, sc_ref):
        sc = _silu(c_ref[...])
        sc_ref[...] = sc
        m_ref[...] = jnp.dot(sc, w_ref[...], precision=HIGHEST, preferred_element_type=f32)

    return _hbm_call(
        body, name="adaln_shard", out_shape=(SDS((16, w_ada_shard.shape[1]), f32), SDS((16, D), f32)),
        in_specs=[VMEM_SPEC, VMEM_SPEC], out_specs=(VMEM_SPEC, VMEM_SPEC), compiler_params=_cp(32),
    )(cc, w_ada_shard)


def _prenorm(xx, norm_g, mrow, b_ada, tm, name, after=None):
    n = xx.shape[0]

    def norm(x, g, m, b):
        shift = m[:, 0:D] + b[:, 0:D]
        scale = m[:, D:2 * D] + b[:, D:2 * D]
        r = lax.rsqrt(jnp.mean(x * x, axis=-1, keepdims=True) + RMS_EPS)
        y = (x * r) * g
        return ((y * (1.0 + scale) + shift).astype(bf16),)

    band = lambda ref, k, s: ref.at[k * tm:(k + 1) * tm]
    whole = lambda ref, k, s: ref
    jvec = jnp.zeros((1,), i32)
    return _stream_call(
        norm, n // tm, [(xx, (tm, D), band), (norm_g, (1, D), whole), (mrow, (1, 3 * D), whole),
                        (b_ada, (1, 3 * D), whole)],
        [((n, D), bf16, (tm, D), band)], jvec, name, after)[0]


def _in_proj_own(h, w_own, jvec):
    tm = 512

    def body(j_ref, h_ref, w_ref, p_ref):
        p_ref[...] = jnp.dot(h_ref[...], w_ref[...].astype(bf16), preferred_element_type=f32)

    grid_spec = pltpu.PrefetchScalarGridSpec(
        num_scalar_prefetch=1, grid=(S // tm,),
        in_specs=[pl.BlockSpec((tm, D), lambda i, j: (i, 0)), pl.BlockSpec((D, D), lambda i, j: (0, 0))],
        out_specs=pl.BlockSpec((tm, D), lambda i, j: (i, j[0])))
    return _hbm_call(body, name="in_proj_own", out_shape=SDS((S, 4 * D), f32), grid_spec=grid_spec,
                     compiler_params=_cp(40))(jvec, h, w_own)


def _in_proj_block(h, w4, p, bvec, name, after=None):
    tm = 512

    def body(b_ref, h_ref, w_ref, p_in_ref, after_ref, p_ref):
        p_ref[...] = jnp.dot(h_ref[...], w_ref[...], preferred_element_type=f32)

    grid_spec = pltpu.PrefetchScalarGridSpec(
        num_scalar_prefetch=1, grid=(S // tm,),
        in_specs=[pl.BlockSpec((tm, D), lambda i, b: (i, 0)), pl.BlockSpec((None, D, D), lambda i, b: (b[0], 0, 0)),
                  ANY_SPEC, ANY_SPEC],
        out_specs=pl.BlockSpec((tm, D), lambda i, b: (i, b[0])))
    return _hbm_call(body, name=name, out_shape=SDS((S, 4 * D), f32), grid_spec=grid_spec,
                     input_output_aliases={3: 0})(bvec, h, w4, p, bvec if after is None else after)


def _ctx_proj(hc, w4):
    def body(h_ref, w0_ref, w1_ref, p_ref):
        hv = h_ref[...]
        p_ref[:, 0:DA] = jnp.dot(hv, w0_ref[:, DA:2 * DA], preferred_element_type=f32)
        p_ref[:, DA:2 * DA] = jnp.dot(hv, w1_ref[:, 0:DA], preferred_element_type=f32)

    return _hbm_call(
        body, name="ctx_proj", out_shape=SDS((L, 2 * DA), f32), grid=(1,),
        in_specs=[pl.BlockSpec((L, D), lambda i: (0, 0)), pl.BlockSpec((None, D, D), lambda i: (0, 0, 0)),
                  pl.BlockSpec((None, D, D), lambda i: (1, 0, 0))],
        out_specs=pl.BlockSpec((L, 2 * DA), lambda i: (0, 0)),
    )(hc, w4, w4)


def _head_ones():
    r = lax.broadcasted_iota(i32, (DA, DA), 0) // DH
    c = lax.broadcasted_iota(i32, (DA, DA), 1) // DH
    return (r == c).astype(bf16)


def _head_sum(v, ones_bd):
    hi = v.astype(bf16)
    lo = (v - hi.astype(f32)).astype(bf16)
    return jnp.dot(hi, ones_bd, preferred_element_type=f32) + jnp.dot(lo, ones_bd, preferred_element_type=f32)


def _swap16(v):
    lane = lax.broadcasted_iota(i32, v.shape, 1)
    return jnp.where((lane & 31) < 16, pltpu.roll(v, DA - 16, 1), pltpu.roll(v, 16, 1))


def _rope_block(ct_ref, rt_ref, tm):
    rows = [jnp.tile(rt_ref[8 * j:8 * j + 8, :], (GW // 8, 1)) for j in range(tm // GW)]
    return jnp.tile(ct_ref[...], (tm // GW, 1)) + jnp.concatenate(rows, axis=0)


def _rope_specs(tm):
    col = pl.BlockSpec((GW, DA), lambda i: (0, 0))
    row = pl.BlockSpec((8 * tm // GW, DA), lambda i: (i, 0))
    return [col, row, col, row]


def _qk_prep(p, gq, gk, rope):
    tm = 512

    def body(qk_ref, v_ref, gq_ref, gk_ref, cc_ref, cr_ref, sc_ref, sr_ref, qr_ref, qp_ref, kr_ref, vh_ref):
        ones_bd = _head_ones()
        cs, sn = _rope_block(cc_ref, cr_ref, tm), _rope_block(sc_ref, sr_ref, tm)
        q = qk_ref[:, 0:DA]
        k = qk_ref[:, DA:2 * DA]
        yq = (q * lax.rsqrt(_head_sum(q * q, ones_bd) * (1.0 / DH) + RMS_EPS)) * gq_ref[...]
        yk = (k * lax.rsqrt(_head_sum(k * k, ones_bd) * (1.0 / DH) + RMS_EPS)) * gk_ref[...]
        qr = (yq * cs + _swap16(yq) * sn) * QK_SCALE
        qp = yq * QK_SCALE
        kr = yk * cs + _swap16(yk) * sn
        vv = v_ref[...]
        for hh in range(H):
            sl = slice(hh * DH, (hh + 1) * DH)
            qr_ref[hh] = qr[:, sl].astype(bf16)
            qp_ref[hh] = qp[:, sl].astype(bf16)
            kr_ref[hh] = kr[:, sl].astype(bf16)
            vh_ref[hh] = vv[:, sl].astype(bf16)

    hm = SDS((H, S, DH), bf16)
    hspec = pl.BlockSpec((H, tm, DH), lambda i: (0, i, 0))
    fixed = lambda i: (0, 0)
    return _hbm_call(
        body, name="qk_prep", out_shape=(hm, hm, hm, hm), grid=(S // tm,),
        in_specs=[pl.BlockSpec((tm, 2 * DA), lambda i: (i, 0)), pl.BlockSpec((tm, DA), lambda i: (i, 2)),
                  pl.BlockSpec((1, DA), fixed), pl.BlockSpec((1, DA), fixed)] + _rope_specs(tm),
        out_specs=(hspec, hspec, hspec, hspec),
    )(p, p, gq, gk, *rope)


def _ctx_prep(pc, gk):
    def body(p_ref, gk_ref, kc_ref, vc_ref):
        ones_bd = _head_ones()
        k = p_ref[:, 0:DA]
        yk = (k * lax.rsqrt(_head_sum(k * k, ones_bd) * (1.0 / DH) + RMS_EPS)) * gk_ref[...]
        vv = p_ref[:, DA:2 * DA]
        for hh in range(H):
            sl = slice(hh * DH, (hh + 1) * DH)
            kc_ref[hh] = yk[:, sl].astype(bf16)
            vc_ref[hh] = vv[:, sl].astype(bf16)

    hm = SDS((H, L, DH), bf16)
    return _hbm_call(
        body, name="ctx_prep", out_shape=(hm, hm), in_specs=[VMEM_SPEC, VMEM_SPEC], out_specs=(VMEM_SPEC, VMEM_SPEC),
    )(pc, gk)


def _tile_pieces():
    out = []
    for (i0, u0) in TILE_GEOM:
        rows = []
        for j in range(2):
            i = i0 + j
            rs = _row_start(i)
            rows.append([(u0 + u - i + WIN_H - 1) if rs <= u0 + u < rs + WIN_H else None for u in range(KR)])
        out.append(rows)
    return out


def _bias_prep(rpb_rev_pad, after=None):
    pieces = _tile_pieces()

    def body(r_ref, after_ref, o_ref):
        rp = r_ref[...]
        xs = jnp.concatenate([pltpu.roll(jnp.broadcast_to(rp[dr:dr + 1, :], (GW, 128)), 128 - (WIN_W - 1), 1,
                                         stride=1, stride_axis=0) for dr in range(N_DR)], axis=0)
        row = lax.broadcasted_iota(i32, xs.shape, 0)
        lane = lax.broadcasted_iota(i32, xs.shape, 1)
        k = row & (GW - 1)
        c0 = jnp.clip(lane - WIN_W // 2, 0, GW - WIN_W)
        xs = jnp.where((k >= c0) & (k < c0 + WIN_W), xs, NEG)
        neg = jnp.full((GW, GW), NEG, f32)
        for t in range(NT):
            for j in range(2):
                for u in range(KR):
                    dr = pieces[t][j][u]
                    piece = neg if dr is None else xs[dr * GW:(dr + 1) * GW, 0:GW]
                    o_ref[t, u * GW:(u + 1) * GW, j * GW:(j + 1) * GW] = piece

    return _hbm_call(
        body, name="bias_prep", out_shape=SDS((H, NT, KB, QB), f32), grid=(H,),
        in_specs=[pl.BlockSpec((None, N_DR, 128), lambda h: (h, 0, 0)), ANY_SPEC],
        out_specs=pl.BlockSpec((None, NT, KB, QB), lambda h: (h, 0, 0, 0)),
    )(rpb_rev_pad, rpb_rev_pad if after is None else after)


def _bias_tiles(rpb2, after=None):
    return _bias_prep(jnp.pad(rpb2[:, :, ::-1], ((0, 0), (0, 0), (0, 128 - N_DC))), after)


def _block_geom(b):
    qs = b * QB
    ks = min(max(2 * b - 4, 0), ROWS - KR) * GW
    t = b if b < 2 else (b - (NQB - NT) if b > NQB - 3 else 2)
    return qs, ks, t


def _tt(a, b):
    return lax.dot_general(a, b, (((1,), (1,)), ((), ())), preferred_element_type=f32)


def _tn(a, b):
    return lax.dot_general(a, b, (((0,), (0,)), ((), ())), preferred_element_type=f32)


def _softmax_t(s_lat, s_ctx):
    m = jnp.maximum(jnp.max(s_lat, axis=0, keepdims=True), jnp.max(s_ctx, axis=0, keepdims=True))
    e_lat = jnp.exp(s_lat - m)
    e_ctx = jnp.exp(s_ctx - m)
    inv = 1.0 / (jnp.sum(e_lat, axis=0, keepdims=True) + jnp.sum(e_ctx, axis=0, keepdims=True))
    return e_lat * inv, e_ctx * inv


def _staged(n_blocks, stages):
    held = [dict() for _ in stages]
    for step in range(n_blocks + len(stages) - 1):
        for s, fn in enumerate(stages):
            b = step - s
            if 0 <= b < n_blocks:
                held[s][b] = fn(b) if s == 0 else fn(b, held[s - 1].pop(b))


def _attn_fwd(qr, qp, kr, vh, kc, vc, btt):
    def body(qr_ref, qp_ref, kr_ref, v_ref, kc_ref, vc_ref, bt_ref, o_ref):
        kcv, vcv = kc_ref[...], vc_ref[...]

        def scores(b):
            qs, ks, t = _block_geom(b)
            return (_tt(kr_ref[ks:ks + KB, :], qr_ref[qs:qs + QB, :]) + bt_ref[t], _tt(kcv, qp_ref[qs:qs + QB, :]))

        def probs(b, sc):
            p_lat, p_ctx = _softmax_t(*sc)
            return p_lat.astype(bf16), p_ctx.astype(bf16)

        def values(b, p):
            qs, ks, _ = _block_geom(b)
            o_ref[qs:qs + QB, :] = _tn(p[0], v_ref[ks:ks + KB, :]) + _tn(p[1], vcv)

        _staged(NQB, (scores, probs, values))

    sq = pl.BlockSpec((None, S, DH), lambda h: (h, 0, 0))
    sc = pl.BlockSpec((None, L, DH), lambda h: (h, 0, 0))
    return _hbm_call(
        body, name="attn_fwd", out_shape=SDS((H, S, DH), f32), grid=(H,),
        in_specs=[sq, sq, sq, sq, sc, sc, pl.BlockSpec((None, NT, KB, QB), lambda h: (h, 0, 0, 0))],
        out_specs=sq, compiler_params=_cp(48),
    )(qr, qp, kr, vh, kc, vc, btt)


def _shift_rows(v, down):
    n = v.shape[0]
    row = lax.broadcasted_iota(i32, v.shape, 0)
    if down:
        return jnp.where(row == 0, 0.0, pltpu.roll(v, 1, 0))
    return jnp.where(row == n - 1, 0.0, pltpu.roll(v, n - 1, 0))


def _conv_specs():
    col = lambda off: pl.BlockSpec((S, 128), lambda i, off=off: (0, off + i))
    return [col(16), col(20), col(24), col(28), pl.BlockSpec((3, 128), lambda i: (0, i)),
            pl.BlockSpec((1, 128), lambda i: (0, i))]


def _conv_fwd(p, conv_w, conv_b, after=None):
    def body(u_ref, bg_ref, cg_ref, zc_ref, w_ref, b_ref, after_ref, o_ref):
        cu = cg_ref[...] * u_ref[...]
        cv = b_ref[...] + _shift_rows(cu, True) * w_ref[0:1, :]
        cv = cv + cu * w_ref[1:2, :]
        cv = cv + _shift_rows(cu, False) * w_ref[2:3, :]
        o_ref[...] = ((bg_ref[...] * cv) * _silu(zc_ref[...])).astype(bf16)

    return _hbm_call(
        body, name="conv_fwd", out_shape=SDS((S, DC), bf16), grid=(DC // 128,),
        in_specs=_conv_specs() + [ANY_SPEC], out_specs=pl.BlockSpec((S, 128), lambda i: (0, i)),
        compiler_params=_cp(40),
    )(p, p, p, p, conv_w, conv_b, conv_b if after is None else after)


DP_Q, DP_K, DP_V, DP_ZA, DP_U, DP_BG, DP_CG, DP_ZC = range(8)


def _out_proj_loss(o, p, conv_g, w_out, xx, tgt, mrow, b_ada):
    tm = 512

    def body(o_ref, za_ref, c_ref, w_ref, x_ref, t_ref, m_ref, b_ref,
             dy_ref, dconv_ref, dp_ref, do_ref, gwo_ref, dgate_ref, loss_ref):
        k = pl.program_id(0)

        @pl.when(k == 0)
        def _():
            gwo_ref[...] = jnp.zeros_like(gwo_ref)
            dgate_ref[...] = jnp.zeros_like(dgate_ref)
            loss_ref[0, 0] = 0.0

        gate = m_ref[:, 2 * D:3 * D] + b_ref[:, 2 * D:3 * D]
        za = za_ref[...]
        sz = _silu(za)
        om = _merge_heads(o_ref)
        av, cv = (om * sz).astype(bf16), c_ref[...]
        mo = jnp.dot(av, w_ref[0:DA, :], preferred_element_type=f32)
        mo = mo + jnp.dot(cv, w_ref[DA:DA + DC, :], preferred_element_type=f32)
        y = x_ref[...] + gate * mo
        diff = y - t_ref[...]
        loss_ref[0, 0] += jnp.sum(diff * diff)
        dy = diff * (1.0 / D)
        dy_ref[...] = dy
        dgate_ref[...] += jnp.sum(dy * mo, axis=0, keepdims=True)
        dmo = (dy * gate).astype(bf16)
        dmix = _tt(dmo, w_ref[...])
        dattn = dmix[:, 0:DA]
        dconv_ref[...] = dmix[:, DA:DA + DC]
        a = dattn * sz
        for hh in range(H):
            do_ref[hh] = a[:, hh * DH:(hh + 1) * DH].astype(bf16)
        dp_ref[...] = ((dattn * _dsilu(za)) * om).astype(bf16)
        gwo_ref[0:DA, :] += _tn(av, dmo)
        gwo_ref[DA:DA + DC, :] += _tn(cv, dmo)

    row = lambda i: (i, 0)
    fixed = lambda i: (0, 0)
    hspec = pl.BlockSpec((H, tm, DH), lambda i: (0, i, 0))
    return _hbm_call(
        body, name="out_proj_loss",
        out_shape=(SDS((S, D), f32), SDS((S, DC), f32), SDS((8, S, DA), bf16), SDS((H, S, DH), bf16),
                   SDS((D, D), f32), SDS((1, D), f32), SDS((1, 1), f32)),
        grid=(S // tm,),
        in_specs=[hspec, pl.BlockSpec((tm, DA), lambda i: (i, 3)), pl.BlockSpec((tm, DC), row),
                  pl.BlockSpec((D, D), fixed), pl.BlockSpec((tm, D), row), pl.BlockSpec((tm, D), row),
                  pl.BlockSpec((1, 3 * D), fixed), pl.BlockSpec((1, 3 * D), fixed)],
        out_specs=(pl.BlockSpec((tm, D), row), pl.BlockSpec((tm, DC), row),
                   pl.BlockSpec((None, tm, DA), lambda i: (DP_ZA, i, 0)), hspec, pl.BlockSpec((D, D), fixed),
                   pl.BlockSpec((1, D), fixed), SMEM_SPEC),
        compiler_params=_cp(56, dimension_semantics=("arbitrary",)),
    )(o, p, conv_g, w_out, xx, tgt, mrow, b_ada)


def _conv_bwd(dconv, p, conv_w, conv_b, dp8, after=None):
    def body(d_ref, u_ref, bg_ref, cg_ref, zc_ref, w_ref, b_ref, dp_in_ref, after_ref, dp_ref, gw_ref, gb_ref):
        du_ref, dbg_ref, dcg_ref, dzc_ref = dp_ref.at[0], dp_ref.at[1], dp_ref.at[2], dp_ref.at[3]
        dconv = d_ref[...]
        u, bg, cg, zc = u_ref[...], bg_ref[...], cg_ref[...], zc_ref[...]
        w0, w1, w2 = w_ref[0:1, :], w_ref[1:2, :], w_ref[2:3, :]
        cu = cg * u
        cu_m, cu_p = _shift_rows(cu, True), _shift_rows(cu, False)
        cv = b_ref[...] + cu_m * w0
        cv = cv + cu * w1
        cv = cv + cu_p * w2
        sz = _silu(zc)
        dbg_ref[...] = ((dconv * sz) * cv).astype(bf16)
        dzc_ref[...] = ((dconv * (bg * cv)) * _dsilu(zc)).astype(bf16)
        dcv = (dconv * sz) * bg
        gb_ref[...] = jnp.sum(dcv, axis=0, keepdims=True)
        gw_ref[0:1, :] = jnp.sum(dcv * cu_m, axis=0, keepdims=True)
        gw_ref[1:2, :] = jnp.sum(dcv * cu, axis=0, keepdims=True)
        gw_ref[2:3, :] = jnp.sum(dcv * cu_p, axis=0, keepdims=True)
        gw_ref[3:8, :] = jnp.zeros((5, 128), f32)
        dcu = _shift_rows(dcv, False) * w0 + dcv * w1 + _shift_rows(dcv, True) * w2
        dcg_ref[...] = (dcu * u).astype(bf16)
        du_ref[...] = (dcu * cg).astype(bf16)

    return _hbm_call(
        body, name="conv_bwd", out_shape=(SDS((8, S, DC), bf16), SDS((8, DC), f32), SDS((1, DC), f32)),
        grid=(DC // 128,),
        in_specs=[pl.BlockSpec((S, 128), lambda i: (0, i))] + _conv_specs() + [ANY_SPEC, ANY_SPEC],
        out_specs=(pl.BlockSpec((4, S, 128), lambda i: (DP_U // 4, 0, i)), pl.BlockSpec((8, 128), lambda i: (0, i)),
                   pl.BlockSpec((1, 128), lambda i: (0, i))),
        input_output_aliases={7: 0}, compiler_params=_cp(48),
    )(dconv, p, p, p, p, conv_w, conv_b, dp8, conv_b if after is None else after)


def _attn_bwd(qr, qp, kr, vh, kc, vc, btt, do, after=None):
    def body(qr_ref, qp_ref, kr_ref, v_ref, kc_ref, vc_ref, bt_ref, do_ref, after_ref,
             dqr_ref, dqp_ref, dkr_ref, dv_ref, dkc_ref, dvc_ref, dbt_ref):
        kcv, vcv = kc_ref[...], vc_ref[...]
        dkr_ref[...] = jnp.zeros_like(dkr_ref)
        dv_ref[...] = jnp.zeros_like(dv_ref)
        dbt_ref[...] = jnp.zeros_like(dbt_ref)
        ctx_acc = {}

        def products(b):
            qs, ks, t = _block_geom(b)
            dob = do_ref[qs:qs + QB, :]
            s_lat = _tt(kr_ref[ks:ks + KB, :], qr_ref[qs:qs + QB, :]) + bt_ref[t]
            s_ctx = _tt(kcv, qp_ref[qs:qs + QB, :])
            return s_lat, s_ctx, _tt(v_ref[ks:ks + KB, :], dob), _tt(vcv, dob)

        def score_grads(b, x):
            s_lat, s_ctx, dp_lat, dp_ctx = x
            p_lat, p_ctx = _softmax_t(s_lat, s_ctx)
            delta = jnp.sum(p_lat * dp_lat, axis=0, keepdims=True) + jnp.sum(p_ctx * dp_ctx, axis=0, keepdims=True)
            ds_lat = p_lat * (dp_lat - delta)
            ds_ctx = p_ctx * (dp_ctx - delta)
            return ds_lat, ds_lat.astype(bf16), ds_ctx.astype(bf16), p_lat.astype(bf16), p_ctx.astype(bf16)

        def operand_grads(b, y):
            qs, ks, t = _block_geom(b)
            ds_lat, dsb_lat, dsb_ctx, pb_lat, pb_ctx = y
            qrb, qpb, dob = qr_ref[qs:qs + QB, :], qp_ref[qs:qs + QB, :], do_ref[qs:qs + QB, :]
            dbt_ref[t] += ds_lat
            dqr_ref[qs:qs + QB, :] = _tn(dsb_lat, kr_ref[ks:ks + KB, :])
            dqp_ref[qs:qs + QB, :] = _tn(dsb_ctx, kcv)
            dkr_ref[ks:ks + KB, :] += jnp.dot(dsb_lat, qrb, preferred_element_type=f32)
            dv_ref[ks:ks + KB, :] += jnp.dot(pb_lat, dob, preferred_element_type=f32)
            dkc = jnp.dot(dsb_ctx, qpb, preferred_element_type=f32)
            dvc = jnp.dot(pb_ctx, dob, preferred_element_type=f32)
            ctx_acc["k"] = dkc if b == 0 else ctx_acc["k"] + dkc
            ctx_acc["v"] = dvc if b == 0 else ctx_acc["v"] + dvc

        _staged(NQB, (products, score_grads, operand_grads))
        dkc_ref[...] = ctx_acc["k"]
        dvc_ref[...] = ctx_acc["v"]

    sq = pl.BlockSpec((None, S, DH), lambda h: (h, 0, 0))
    sc = pl.BlockSpec((None, L, DH), lambda h: (h, 0, 0))
    sb = pl.BlockSpec((None, NT, KB, QB), lambda h: (h, 0, 0, 0))
    big, ctxs = SDS((H, S, DH), f32), SDS((H, L, DH), f32)
    return _hbm_call(
        body, name="attn_bwd", out_shape=(big, big, big, big, ctxs, ctxs, SDS((H, NT, KB, QB), f32)), grid=(H,),
        in_specs=[sq, sq, sq, sq, sc, sc, sb, sq, ANY_SPEC], out_specs=(sq, sq, sq, sq, sc, sc, sb),
        compiler_params=_cp(56),
    )(qr, qp, kr, vh, kc, vc, btt, do, do if after is None else after)


def _bias_bwd(dbtt, after=None):
    pieces = _tile_pieces()

    def body(d_ref, after_ref, o_ref, scr):
        scr[...] = jnp.zeros_like(scr)
        acc = [None] * N_DR
        for t in range(NT):
            for j in range(2):
                for u in range(KR):
                    dr = pieces[t][j][u]
                    if dr is None:
                        continue
                    piece = d_ref[t, u * GW:(u + 1) * GW, j * GW:(j + 1) * GW]
                    acc[dr] = piece if acc[dr] is None else acc[dr] + piece
        a = lax.broadcasted_iota(i32, (GW, GW), 0)
        b = lax.broadcasted_iota(i32, (GW, GW), 1)
        flip = (a + b == GW - 1).astype(f32)
        for dr in range(N_DR):
            scr[dr * GW:(dr + 1) * GW, 0:GW] = jnp.dot(acc[dr], flip, precision=HIGHEST, preferred_element_type=f32)
        xs = jnp.concatenate([pltpu.roll(scr[dr * GW:(dr + 1) * GW, :], 128 + (WIN_W - 1) - (GW - 1), 1,
                                         stride=1, stride_axis=0) for dr in range(N_DR)], axis=0)
        tot = jnp.sum(xs.reshape(N_DR, GW, 128), axis=1)
        lane = lax.broadcasted_iota(i32, tot.shape, 1)
        o_ref[...] = jnp.where(lane < N_DC, tot, 0.0)

    return _hbm_call(
        body, name="bias_bwd", out_shape=SDS((H, N_DR, 128), f32), grid=(H,),
        in_specs=[pl.BlockSpec((None, NT, KB, QB), lambda h: (h, 0, 0, 0)), ANY_SPEC],
        out_specs=pl.BlockSpec((None, N_DR, 128), lambda h: (h, 0, 0)),
        scratch_shapes=[pltpu.VMEM((N_DR * GW, 128), f32)],
    )(dbtt, dbtt if after is None else after)


def _merge_heads(ref):
    return jnp.concatenate([ref[hh] for hh in range(H)], axis=1)


def _head_norm_bwd(xraw, gain, dy, ones_bd):
    r = lax.rsqrt(_head_sum(xraw * xraw, ones_bd) * (1.0 / DH) + RMS_EPS)
    xh = xraw * r
    gdy = dy * gain
    dx = r * (gdy - xh * (_head_sum(xh * gdy, ones_bd) * (1.0 / DH)))
    return dx, jnp.sum(dy * xh, axis=0, keepdims=True)


def _qk_bwd(dqr, dqp, dkr, dvh, p, gq, gk, rope, dp8):
    tm = 512

    def body(dqr_ref, dqp_ref, dkr_ref, dv_ref, qk_ref, gq_ref, gk_ref, cc_ref, cr_ref, sc_ref, sr_ref, dp_in_ref,
             dp_ref, ggq_ref, ggk_ref):
        dq_ref, dk_ref, dvo_ref = dp_ref.at[DP_Q], dp_ref.at[DP_K], dp_ref.at[DP_V]

        @pl.when(pl.program_id(0) == 0)
        def _():
            ggq_ref[...] = jnp.zeros_like(ggq_ref)
            ggk_ref[...] = jnp.zeros_like(ggk_ref)

        ones_bd = _head_ones()
        cs, sn = _rope_block(cc_ref, cr_ref, tm), _rope_block(sc_ref, sr_ref, tm)
        a = _merge_heads(dqr_ref)
        dyq = ((a * cs - _swap16(a) * sn) + _merge_heads(dqp_ref)) * QK_SCALE
        bk = _merge_heads(dkr_ref)
        dyk = bk * cs - _swap16(bk) * sn
        dq, gq_part = _head_norm_bwd(qk_ref[:, 0:DA], gq_ref[...], dyq, ones_bd)
        dk, gk_part = _head_norm_bwd(qk_ref[:, DA:2 * DA], gk_ref[...], dyk, ones_bd)
        dq_ref[...] = dq.astype(bf16)
        dk_ref[...] = dk.astype(bf16)
        dvo_ref[...] = _merge_heads(dv_ref).astype(bf16)
        ggq_ref[...] += gq_part
        ggk_ref[...] += gk_part

    hspec = pl.BlockSpec((H, tm, DH), lambda i: (0, i, 0))
    fixed = pl.BlockSpec((1, DA), lambda i: (0, 0))
    return _hbm_call(
        body, name="qk_bwd", out_shape=(SDS((8, S, DA), bf16), SDS((1, DA), f32), SDS((1, DA), f32)), grid=(S // tm,),
        in_specs=[hspec, hspec, hspec, hspec, pl.BlockSpec((tm, 2 * DA), lambda i: (i, 0)), fixed, fixed]
        + _rope_specs(tm) + [ANY_SPEC],
        out_specs=(pl.BlockSpec((3, tm, DA), lambda i: (0, i, 0)), fixed, fixed), input_output_aliases={11: 0},
        compiler_params=_cp(40, dimension_semantics=("arbitrary",)),
    )(dqr, dqp, dkr, dvh, p, gq, gk, *rope, dp8)


def _ctx_bwd(dkc, dvc, pc, gk):
    def body(dkc_ref, dvc_ref, p_ref, gk_ref, dk_ref, dv_ref, ggk_ref):
        ones_bd = _head_ones()
        dk, gk_part = _head_norm_bwd(p_ref[:, 0:DA], gk_ref[...], _merge_heads(dkc_ref), ones_bd)
        dk_ref[...] = dk.astype(bf16)
        dv_ref[...] = _merge_heads(dvc_ref).astype(bf16)
        ggk_ref[...] = gk_part

    piece = SDS((L, DA), bf16)
    return _hbm_call(
        body, name="ctx_bwd", out_shape=(piece, piece, SDS((1, DA), f32)), in_specs=[VMEM_SPEC] * 4,
        out_specs=(VMEM_SPEC,) * 3,
    )(dkc, dvc, pc, gk)


def _grad_w_in(h, dp8, hc, dkc_raw, dvc_m, half, name, after=None):
    def body(half_ref, h_ref, p_ref, hc_ref, dk_ref, dv_ref, after_ref, g_ref):
        j = pl.program_id(0)
        hv = h_ref[...]
        g_ref[:, 0:DA] = _tn(hv, p_ref[0])
        g_ref[:, DA:2 * DA] = _tn(hv, p_ref[1])

        @pl.when(j == 0)
        def _():
            g_ref[:, DA:2 * DA] += _tn(hc_ref[...], dk_ref[...])

        @pl.when(j == 1)
        def _():
            g_ref[:, 0:DA] += _tn(hc_ref[...], dv_ref[...])

    fixed = lambda j, s: (0, 0)
    hd = D // 2
    grid_spec = pltpu.PrefetchScalarGridSpec(
        num_scalar_prefetch=1, grid=(4,),
        in_specs=[pl.BlockSpec((S, hd), lambda j, s: (0, s[0])), pl.BlockSpec((2, S, DA), lambda j, s: (j, 0, 0)),
                  pl.BlockSpec((L, hd), lambda j, s: (0, s[0])), pl.BlockSpec((L, DA), fixed),
                  pl.BlockSpec((L, DA), fixed), ANY_SPEC],
        out_specs=pl.BlockSpec((None, hd, D), lambda j, s: (j, 0, 0)))
    return _hbm_call(
        body, name=name, out_shape=SDS((4, hd, D), f32), grid_spec=grid_spec, compiler_params=_cp(40),
    )(half, h, dp8, hc, dkc_raw, dvc_m, half if after is None else after)


def _grad_w_in_pair_sum(h, dp8, hc, dkc_raw, dvc_m, half, sent, landed, ssem, rsem):
    def body(half_ref, h_ref, p_ref, hc_ref, dk_ref, dv_ref, sent_ref, land_ref, ssem_ref, rsem_ref,
             t32_ref, tb_ref, buf, lsem):
        j = pl.program_id(0)
        hv = h_ref[...]
        t32_ref[:, 0:DA] = _tn(hv, p_ref[0])
        x, y, c = _my_pos()
        arrival = pltpu.make_async_remote_copy(src_ref=sent_ref.at[j], dst_ref=land_ref.at[j], send_sem=ssem_ref.at[j],
                                               recv_sem=rsem_ref.at[j], device_id=(x, y, 1 - c), device_id_type=MESH)
        arrival.wait_recv()
        arrival.wait_send()
        load = pltpu.make_async_copy(land_ref.at[j], buf, lsem)
        load.start()
        t32_ref[:, DA:2 * DA] = _tn(hv, p_ref[1])

        @pl.when(j == 0)
        def _():
            t32_ref[:, DA:2 * DA] += _tn(hc_ref[...], dk_ref[...])

        @pl.when(j == 1)
        def _():
            t32_ref[:, 0:DA] += _tn(hc_ref[...], dv_ref[...])

        load.wait()
        t = t32_ref[...] + buf[...]
        t32_ref[...] = t
        tb_ref[...] = t.astype(bf16)

    fixed = lambda j, s: (0, 0)
    hd = D // 2
    out_spec = pl.BlockSpec((None, hd, D), lambda j, s: (j, 0, 0))
    grid_spec = pltpu.PrefetchScalarGridSpec(
        num_scalar_prefetch=1, grid=(4,),
        in_specs=[pl.BlockSpec((S, hd), lambda j, s: (0, s[0])), pl.BlockSpec((2, S, DA), lambda j, s: (j, 0, 0)),
                  pl.BlockSpec((L, hd), lambda j, s: (0, s[0])), pl.BlockSpec((L, DA), fixed),
                  pl.BlockSpec((L, DA), fixed), HBM_SPEC, HBM_SPEC, SEM_SPEC, SEM_SPEC],
        out_specs=(out_spec, out_spec), scratch_shapes=[pltpu.VMEM((hd, D), f32), pltpu.SemaphoreType.DMA])
    return _hbm_call(
        body, name="grad_w_in_pair_sum", out_shape=(SDS((4, hd, D), f32), SDS((4, hd, D), bf16)), grid_spec=grid_spec,
        compiler_params=_cp(40, has_side_effects=DATAFLOW),
    )(half, h, dp8, hc, dkc_raw, dvc_m, sent, landed, ssem, rsem)


def _norm_mod_bwd(x, dh, g, scale):
    r = lax.rsqrt(jnp.mean(x * x, axis=-1, keepdims=True) + RMS_EPS)
    xh = x * r
    y = xh * g
    dshift = jnp.sum(dh, axis=0, keepdims=True)
    dscale = jnp.sum(dh * y, axis=0, keepdims=True)
    dyn = dh * (1.0 + scale)
    dg = jnp.sum(dyn * xh, axis=0, keepdims=True)
    gdy = dyn * g
    dx = r * (gdy - xh * jnp.mean(xh * gdy, axis=-1, keepdims=True))
    return dx, dshift, dscale, dg


def _dh_grad_x(dp8, w4, xx, dy, norm_g, mrow, b_ada, after=None):
    tm = 512

    def body(p_ref, w_ref, x_ref, dy_ref, g_ref, m_ref, b_ref, after_ref, gx_ref, dsh_ref, dsc_ref, dg_ref):
        @pl.when(pl.program_id(0) == 0)
        def _():
            dsh_ref[...] = jnp.zeros_like(dsh_ref)
            dsc_ref[...] = jnp.zeros_like(dsc_ref)
            dg_ref[...] = jnp.zeros_like(dg_ref)

        dh = None
        for j in range(4):
            for half in range(2):
                term = _tt(p_ref[2 * j + half], w_ref[j, :, half * DA:(half + 1) * DA])
                dh = term if dh is None else dh + term
        scale = m_ref[:, D:2 * D] + b_ref[:, D:2 * D]
        dx, dshift, dscale, dg = _norm_mod_bwd(x_ref[...], dh, g_ref[...], scale)
        gx_ref[...] = dy_ref[...] + dx
        dsh_ref[...] += dshift
        dsc_ref[...] += dscale
        dg_ref[...] += dg

    row = lambda i: (i, 0)
    fixed = lambda i: (0, 0)
    vec = SDS((1, D), f32)
    return _hbm_call(
        body, name="dh_grad_x", out_shape=(SDS((S, D), f32), vec, vec, vec), grid=(S // tm,),
        in_specs=[pl.BlockSpec((8, tm, DA), lambda i: (0, i, 0)), pl.BlockSpec((4, D, D), lambda i: (0, 0, 0)),
                  pl.BlockSpec((tm, D), row), pl.BlockSpec((tm, D), row), pl.BlockSpec((1, D), fixed),
                  pl.BlockSpec((1, 3 * D), fixed), pl.BlockSpec((1, 3 * D), fixed), ANY_SPEC],
        out_specs=(pl.BlockSpec((tm, D), row), pl.BlockSpec((1, D), fixed), pl.BlockSpec((1, D), fixed),
                   pl.BlockSpec((1, D), fixed)),
        compiler_params=_cp(56, dimension_semantics=("arbitrary",)),
    )(dp8, w4, xx, dy, norm_g, mrow, b_ada, b_ada if after is None else after)


def _dhc_sums(dkc_raw, dvc_m, w4, ctx2, norm_g, mrow_c, b_ada, after=None):
    def body(dk_ref, dv_ref, w0_ref, w1_ref, x_ref, g_ref, m_ref, b_ref, after_ref, dsh_ref, dsc_ref, dg_ref):
        dh = _tt(dk_ref[...], w0_ref[:, DA:2 * DA]) + _tt(dv_ref[...], w1_ref[:, 0:DA])
        scale = m_ref[:, D:2 * D] + b_ref[:, D:2 * D]
        _, dshift, dscale, dg = _norm_mod_bwd(x_ref[...], dh, g_ref[...], scale)
        dsh_ref[...] = dshift
        dsc_ref[...] = dscale
        dg_ref[...] = dg

    fixed = lambda i: (0, 0)
    vec = SDS((1, D), f32)
    vspec = pl.BlockSpec((1, D), fixed)
    return _hbm_call(
        body, name="dhc_sums", out_shape=(vec, vec, vec), grid=(1,),
        in_specs=[pl.BlockSpec((L, DA), fixed), pl.BlockSpec((L, DA), fixed),
                  pl.BlockSpec((None, D, D), lambda i: (0, 0, 0)), pl.BlockSpec((None, D, D), lambda i: (1, 0, 0)),
                  pl.BlockSpec((L, D), fixed), vspec, pl.BlockSpec((1, 3 * D), fixed), pl.BlockSpec((1, 3 * D), fixed),
                  ANY_SPEC],
        out_specs=(vspec, vspec, vspec), compiler_params=_cp(32),
    )(dkc_raw, dvc_m, w4, w4, ctx2, norm_g, mrow_c, b_ada, b_ada if after is None else after)


def _rope_tables():
    nf = DH // 4
    inv = np.float32(ROPE_THETA) ** (-np.arange(nf, dtype=np.float32) / np.float32(nf))
    ang_c = np.arange(GW, dtype=np.float32)[:, None] * inv
    ang_r = np.arange(ROWS, dtype=np.float32)[:, None] * inv
    zc, zr = np.zeros((GW, 2 * nf), np.float32), np.zeros((ROWS, 2 * nf), np.float32)
    ct_cos = np.tile(np.concatenate([zc, np.cos(ang_c), np.cos(ang_c)], axis=1), (1, H))
    ct_sin = np.tile(np.concatenate([zc, -np.sin(ang_c), np.sin(ang_c)], axis=1), (1, H))
    rt_cos = np.tile(np.concatenate([np.cos(ang_r), np.cos(ang_r), zr], axis=1), (1, H))
    rt_sin = np.tile(np.concatenate([-np.sin(ang_r), np.sin(ang_r), zr], axis=1), (1, H))
    rep8 = lambda t: np.ascontiguousarray(np.broadcast_to(t[:, None, :], (ROWS, 8, DA))).reshape(ROWS * 8, DA)
    return tuple(jnp.asarray(t, f32) for t in (ct_cos, rep8(rt_cos), ct_sin, rep8(rt_sin)))


def _local_step(xx, ctx2, tgt, mrow, mrow_c, b_ada, norm_g, weights, q_norm_g, k_norm_g, rpb2, conv_w_full, conv_b,
                hooks=None):
    hooks = hooks or {}
    gq = jnp.tile(q_norm_g, (1, H))
    gk = jnp.tile(k_norm_g, (1, H))
    rope = _rope_tables()

    h = _prenorm(xx, norm_g, mrow, b_ada, 256, "prenorm_x", after=weights.get("started"))
    jv = weights["jvec"]
    p = _in_proj_own(h, weights["own"], jv)
    btb = _bias_tiles(rpb2, after=p)
    w4, started = weights["near"](btb)
    p = _in_proj_block(h, w4, p, weights["first"], "in_proj_near", after=started)
    w4 = weights["near2"](w4, p)
    p = _in_proj_block(h, w4, p, weights["second"], "in_proj_near2")
    ctx_norm = {}

    def filler(token):
        ctx_norm["hc"] = _prenorm(ctx2, norm_g, mrow_c, b_ada, L, "prenorm_ctx", after=token)
        return ctx_norm["hc"]

    w4, started = weights["far"](w4, p, filler)
    hc = ctx_norm["hc"]
    p = _in_proj_block(h, w4, p, jv ^ 3, "in_proj_far", after=started)
    pc = _ctx_proj(hc, w4)
    qr, qp, kr, vh = _qk_prep(p, gq, gk, rope)
    kc, vc = _ctx_prep(pc, gk)
    o = _attn_fwd(qr, qp, kr, vh, kc, vc, btb)
    started = weights["out_arrived"](o) if "out_arrived" in weights else None
    conv_g = _conv_fwd(p, conv_w_full, conv_b, after=started)
    w_out_full = weights["out"](conv_g)
    dy, dconv, dp8, do, g_w_out, dgate, loss_sum = _out_proj_loss(o, p, conv_g, w_out_full, xx, tgt, mrow, b_ada)
    started = hooks["g_w_out"](g_w_out) if "g_w_out" in hooks else None
    dp8, g_conv_w, g_conv_b = _conv_bwd(dconv, p, conv_w_full, conv_b, dp8, after=started)
    started = hooks["after_conv"](dp8) if "after_conv" in hooks else None
    dqr, dqp, dkr, dvh, dkc, dvc, dbtb = _attn_bwd(qr, qp, kr, vh, kc, vc, btb, do, after=started)
    dp8, g_gq, g_gk = _qk_bwd(dqr, dqp, dkr, dvh, p, gq, gk, rope, dp8)
    dkc_raw, dvc_m, g_gk_c = _ctx_bwd(dkc, dvc, pc, gk)
    first = hooks.get("first_half", jnp.zeros((1,), i32))
    g_first = _grad_w_in(h, dp8, hc, dkc_raw, dvc_m, first, "grad_w_in_first")
    started = hooks["g_w_in_first"](g_first) if "g_w_in_first" in hooks else None
    if "w_in_pair_sum" in hooks:
        g_second = None
        started = hooks["w_in_pair_sum"](functools.partial(_grad_w_in_pair_sum, h, dp8, hc, dkc_raw, dvc_m, 1 - first))
    else:
        g_second = _grad_w_in(h, dp8, hc, dkc_raw, dvc_m, 1 - first, "grad_w_in_second", after=started)
    dshift_c, dscale_c, dng_c = _dhc_sums(dkc_raw, dvc_m, w4, ctx2, norm_g, mrow_c, b_ada, after=started)
    g_rpb = _bias_bwd(dbtb, after=dshift_c)
    grad_x, dshift, dscale, dng = _dh_grad_x(dp8, w4, xx, dy, norm_g, mrow, b_ada, after=g_rpb)
    return dict(loss_sum=loss_sum, grad_x=grad_x, g_w_in=(g_first, g_second), g_w_out=g_w_out, g_conv_w=g_conv_w,
                g_conv_b=g_conv_b, g_rpb=g_rpb, g_gq=g_gq, g_gk=g_gk, g_gk_c=g_gk_c, dshift=dshift, dscale=dscale,
                dgate=dgate, dng=dng, dshift_c=dshift_c, dscale_c=dscale_c, dng_c=dng_c)


def _pair_sum_w_out(g, r, cvec):
    hr = D // 8

    def add(own, other):
        t = own + other
        return t, t.astype(bf16)

    mine = lambda ref, q, c: ref.at[pl.ds(pl.multiple_of((2 * q + c) * hr, hr), hr)]
    block = lambda ref, q, c: ref.at[q]
    return _stream_call(add, 4, [(g, (hr, D), mine), (r, (hr, D), block)],
                        [((4, hr, D), f32, (hr, D), block), ((4, hr, D), bf16, (hr, D), block)], cvec, "pair_sum_w_out")


def _chip_sum(t32, r2, jvec, name):
    rows = t32.shape[1]
    tr = min(rows, 128)
    band = lambda k: slice(k * tr, (k + 1) * tr)
    total = lambda t, r: (((t + r[0].astype(f32)) + r[1].astype(f32)) + r[2].astype(f32),)
    return _stream_call(
        total, rows // tr,
        [(t32, (tr, D), lambda ref, k, j: ref.at[j, band(k)]), (r2, (3, tr, D), lambda ref, k, j: ref.at[:, band(k)])],
        [((rows, D), f32, (tr, D), lambda ref, k, j: ref.at[band(k)])], jvec, name)[0]


_PK = {}
_off = 0
for _name, _rows in (("dm", 24), ("dmc", 24), ("dng", 8), ("dng_c", 8), ("gq", 8), ("gk", 8), ("gk_c", 8),
                     ("rpb", H * N_DR), ("conv_b", 8), ("conv_w", 16), ("loss", 8)):
    _PK[_name] = (_off, _off + _rows)
    _off += _rows
PK_ROWS = _off
RS_B_ADA, RS_NORM_G, RS_GQ, RS_GK, RS_RPB, RS_CONV_B, RS_CONV_W, RS_DMC, RS_LOSS, RS_ROWS = (
    0, 24, 32, 40, 48, 168, 176, 192, 216, 224)


def _small_reduce(gathered):
    def body(g_ref, o_ref, dm_ref):
        a0 = _PK["dm"][0]
        dm_ref[...] = jnp.zeros_like(dm_ref)
        for b in range(8):
            for i in range(24):
                dm_ref[b:b + 1, 128 * i:128 * (i + 1)] = g_ref[b, a0 + i:a0 + i + 1, :]
        tot = g_ref[0]
        for b in range(1, 8):
            tot = tot + g_ref[b]

        def rows(name):
            a, z = _PK[name]
            return tot[a:z]

        o_ref[RS_B_ADA:RS_B_ADA + 24] = rows("dm") + rows("dmc")
        o_ref[RS_NORM_G:RS_NORM_G + 8] = rows("dng") + rows("dng_c")
        gq = jnp.broadcast_to(jnp.sum(rows("gq"), axis=0, keepdims=True), (8, 128))
        gk = jnp.broadcast_to(jnp.sum(rows("gk") + rows("gk_c"), axis=0, keepdims=True), (8, 128))
        o_ref[RS_GQ:RS_GQ + 8] = gq + pltpu.roll(gq, DH, 1)
        o_ref[RS_GK:RS_GK + 8] = gk + pltpu.roll(gk, DH, 1)
        o_ref[RS_RPB:RS_RPB + H * N_DR] = rows("rpb")
        o_ref[RS_CONV_B:RS_CONV_B + 8] = rows("conv_b")
        o_ref[RS_CONV_W:RS_CONV_W + 16] = rows("conv_w")
        dmc = rows("dmc")
        o_ref[RS_DMC:RS_DMC + 24] = dmc
        o_ref[RS_LOSS:RS_LOSS + 8] = rows("loss")
        for i in range(24):
            dm_ref[8:9, 128 * i:128 * (i + 1)] = dmc[i:i + 1]

    return _hbm_call(body, name="small_reduce", out_shape=(SDS((RS_ROWS, 128), f32), SDS((16, 3 * D), f32)),
                     in_specs=[VMEM_SPEC], out_specs=(VMEM_SPEC, VMEM_SPEC))(gathered)


def _w_ada_grad(sc16, dm16, w_ada_shard, jvec):
    ncol = w_ada_shard.shape[1]

    def body(j_ref, sc_ref, dm_ref, w_ref, g_ref, part_ref):
        dm = dm_ref[...]
        g_ref[...] = lax.dot_general(sc_ref[...], dm, (((0,), (0,)), ((), ())), precision=HIGHEST,
                                     preferred_element_type=f32)
        part_ref[...] = lax.dot_general(dm[8:16], w_ref[...], (((1,), (1,)), ((), ())), precision=HIGHEST,
                                        preferred_element_type=f32)

    fixed = lambda i, j: (0, 0)
    grid_spec = pltpu.PrefetchScalarGridSpec(
        num_scalar_prefetch=1, grid=(1,),
        in_specs=[pl.BlockSpec((16, D), fixed), pl.BlockSpec((16, ncol), lambda i, j: (0, j[0])),
                  pl.BlockSpec((D, ncol), fixed)],
        out_specs=(pl.BlockSpec((D, ncol), fixed), pl.BlockSpec((8, D), fixed)))
    return _pallas_call(body, name="w_ada_grad", out_shape=(SDS((D, ncol), f32), SDS((8, D), f32)),
                        grid_spec=grid_spec, compiler_params=_cp(40))(jvec, sc16, dm16, w_ada_shard)


def _c_ctx_grad(parts4, c_ctx):
    def body(p_ref, c_ref, o_ref):
        tot = ((p_ref[0] + p_ref[1]) + p_ref[2]) + p_ref[3]
        o_ref[...] = tot[0:1] * _dsilu(c_ref[...].reshape(1, D))

    return _pallas_call(body, name="c_ctx_grad", out_shape=SDS((1, D), f32), in_specs=[VMEM_SPEC, VMEM_SPEC],
                        out_specs=VMEM_SPEC)(parts4, c_ctx)


def _adamw(w, g, m, v, name, after=None):
    return _adamw_stream(w, [g], m, v, None, name, after)


def _adamw_halves(w, g_mine, g_other, m, v, cvec, name, after=None):
    return _adamw_stream(w, [g_mine, g_other], m, v, cvec, name, after)


def _adamw_stream(w, g_parts, m, v, cvec, name, after):
    rows, cols = w.shape
    parts = len(g_parts)
    span = rows // parts
    chunk = min(128, span // 2)
    per = span // chunk
    outs_g = parts == 2
    pin = rows * cols * 4 > (1 << 20)

    def body(*refs):
        c_ref, w_ref = refs[0], refs[1]
        g_refs = refs[2:2 + parts]
        m_ref, v_ref = refs[2 + parts], refs[3 + parts]
        out_refs = refs[5 + parts:5 + parts + 3 + outs_g]
        bw, bg, bm, bv, bd, sem_in, sem_out = refs[5 + parts + 3 + outs_g:]
        c = c_ref[0]
        loads, stores = [], []
        for p in range(parts):
            first = (c if p == 0 else 1 - c) * span if parts == 2 else 0
            for kk in range(per):
                r = pl.ds(pl.multiple_of(first + kk * chunk, chunk), chunk)
                g_src = g_refs[p].at[kk * chunk:(kk + 1) * chunk]
                cps = [pltpu.make_async_copy(src, dst.at[r], sem_in.at[a, p, kk])
                       for a, (src, dst) in enumerate(((w_ref.at[r], bw), (g_src, bg), (m_ref.at[r], bm),
                                                       (v_ref.at[r], bv)))]
                for cp in cps:
                    cp.start()
                loads.append((r, p, kk, cps))
        for r, p, kk, cps in loads:
            for cp in cps:
                cp.wait()
            bd[r], bm[r], bv[r] = _adam_math(bw[r], bg[r], bm[r], bv[r])
            for a, (src, dst) in enumerate(zip(([bg] if outs_g else []) + [bd, bm, bv], out_refs)):
                cp = pltpu.make_async_copy(src.at[r], dst.at[r], sem_out.at[a, p, kk])
                cp.start()
                stores.append(cp)
        for cp in stores:
            cp.wait()

    shp = pltpu.HBM((rows, cols), f32) if pin else SDS((rows, cols), f32)
    spec = HBM_SPEC if pin else ANY_SPEC
    buf = pltpu.VMEM((rows, cols), f32)
    args = [pltpu.with_memory_space_constraint(a, pltpu.HBM) if pin else a for a in (w, *g_parts, m, v)]
    cvec = jnp.zeros((1,), i32) if cvec is None else cvec
    after_big = after is not None and after.size * 4 > (1 << 20)
    after = pltpu.with_memory_space_constraint(after, pltpu.HBM) if after_big else after
    return _pallas_call(
        body, name=name, out_shape=(shp,) * (3 + outs_g),
        in_specs=[SMEM_SPEC] + [spec] * (3 + parts) + [HBM_SPEC if after_big else ANY_SPEC],
        out_specs=(spec,) * (3 + outs_g),
        scratch_shapes=[buf] * 5 + [pltpu.SemaphoreType.DMA((4, parts, per)), pltpu.SemaphoreType.DMA((4, parts, per))],
        compiler_params=_cp(5 * rows * cols * 4 // (1 << 20) + 8),
    )(cvec, *args, cvec if after is None else after)


def _adam_math(w, g, m, v):
    m2 = ADAM_B1 * m + (1.0 - ADAM_B1) * g
    v2 = ADAM_B2 * v + (1.0 - ADAM_B2) * jnp.square(g)
    m_hat = m2 / (1.0 - ADAM_B1 ** ADAM_STEP)
    v_hat = v2 / (1.0 - ADAM_B2 ** ADAM_STEP)
    return -ADAM_LR * (m_hat / (jnp.sqrt(v_hat) + ADAM_EPS) + ADAM_WD * w), m2, v2


def _adamw_small(red, g_c_ctx, jvec, ws, ms, vs):
    n = len(ws)

    def body(*refs):
        red_ref, gc_ref, j_ref = refs[:3]
        w_refs, m_refs, v_refs = refs[3:3 + n], refs[3 + n:3 + 2 * n], refs[3 + 2 * n:3 + 3 * n]
        outs = refs[3 + 3 * n:]
        g_out, d_out, m_out, v_out = outs[:n], outs[n:2 * n], outs[2 * n:3 * n], outs[3 * n:]
        chip = j_ref[0]
        lanes = lambda i: (slice(None), slice(128 * i, 128 * (i + 1)))
        row = lambda r0, i: (lambda: red_ref[r0 + i:r0 + i + 1, :])
        whole = (slice(None), slice(None))
        chunks = [
            [((slice(None),), lambda: gc_ref[...].reshape(D))],
            [(lanes(i), row(RS_B_ADA, i)) for i in range(3 * D // 128)],
            [(lanes(i), row(RS_NORM_G, i)) for i in range(D // 128)],
            [(whole, lambda: red_ref[RS_GQ:RS_GQ + 1, 0:DH])],
            [(whole, lambda: red_ref[RS_GK:RS_GK + 1, 0:DH])],
            [((dr,), (lambda dr=dr: red_ref[pl.ds(RS_RPB + dr, H, stride=N_DR), 0:N_DC])) for dr in range(N_DR)],
            [((r,), (lambda r=r: red_ref[pl.ds(RS_CONV_W + 4 * r + chip, 1), :])) for r in range(3)],
            [(lanes(i), row(RS_CONV_B, i)) for i in range(DC // 128)],
        ]
        for a in range(n):
            for idx, grad in chunks[a]:
                g = grad()
                d, m2, v2 = _adam_math(w_refs[a][idx], g, m_refs[a][idx], v_refs[a][idx])
                g_out[a][idx] = g
                d_out[a][idx] = d
                m_out[a][idx] = m2
                v_out[a][idx] = v2

    shapes = [SDS(w.shape, f32) for w in ws]
    res = _pallas_call(body, name="adamw_small", out_shape=shapes * 4,
                       in_specs=[VMEM_SPEC, VMEM_SPEC, SMEM_SPEC] + [VMEM_SPEC] * (3 * n),
                       out_specs=[VMEM_SPEC] * (4 * n))(red, g_c_ctx, jvec, *ws, *ms, *vs)
    return [list(res[k * n:(k + 1) * n]) for k in range(4)]


def _rows128(a):
    return a.reshape(-1, 128)


def kernel(x, c, ctx, c_ctx, w_ada, b_ada, norm_g, w_in, q_norm_g, k_norm_g, rpb, conv_w, conv_b, w_out, loss_target, m_c_ctx, m_w_ada, m_b_ada, m_norm_g, m_w_in, m_q_norm_g, m_k_norm_g, m_rpb, m_conv_w, m_conv_b, m_w_out, v_c_ctx, v_w_ada, v_b_ada, v_norm_g, v_w_in, v_q_norm_g, v_k_norm_g, v_rpb, v_conv_w, v_conv_b, v_w_out):
    xi, yi, ci = lax.axis_index("x"), lax.axis_index("y"), lax.axis_index("c")
    dev = 4 * xi + 2 * yi + ci
    chip = 2 * xi + yi
    cvec = jnp.reshape(ci, (1,)).astype(i32)
    jvec = jnp.reshape(chip, (1,)).astype(i32)
    w_ada_s = w_ada[0]
    ncol = w_ada_s.shape[1]

    gc = _split_start([_to_slot(c.reshape(8, 128), 8, dev)], [], 7, _gather8_copies, "gather_c_start")
    wo4c = _cast_to_slot(w_out[0], jvec, "cast_w_out", after=gc[3])
    w4c = _cast_to_slot(w_in[0], jvec, "cast_w_in", after=wo4c)
    (c8,), _ = _split_wait(gc[0], gc[1], [gc[2]], [], w4c, _gather8_copies, "gather_c_wait")
    cc = jnp.concatenate([c8.reshape(8, D), c_ctx.reshape(1, D), jnp.zeros((7, D), f32)], axis=0)
    m_shard, sc16 = _adaln_shard(cc, w_ada_s)

    conv_w_pad = jnp.pad(conv_w[0], ((0, 5), (0, 0)))
    gm = _split_start([_to_slot(m_shard, 4, chip), _to_slot(conv_w_pad, 4, chip)], [], 6, _gather4_copies,
                      "gather_mod_start")

    all_k = [(0, 0, 0), (0, 0, 1), (0, 0, 2)]
    sem_a, rem_a, w4s, token = _split_start([w4c], [], 1, _near_copies, "weights_near_start", after=gm[4])
    (m4, cw4), _ = _split_wait(gm[0], gm[1], [gm[2], gm[3]], [], token, _gather4_copies, "gather_mod_wait")
    m_full = jnp.transpose(m4, (1, 0, 2)).reshape(16, 4 * ncol)
    mrow = lax.dynamic_slice(m_full, (dev, 0), (1, 3 * D))
    mrow_c = m_full[8:9]
    conv_w_full = jnp.transpose(cw4[:, 0:3, :], (1, 0, 2)).reshape(3, DC)
    waves = {}

    def near(after):
        (w4w,), _ = _split_wait(sem_a, rem_a, [w4s], [], after, _near_copies, "weights_near_wait")
        sem_b, rem_b, w4b, started = _split_start([w4w], [], 2, _pass_relay_copies, "weights_pass_start")
        waves["pass"] = (sem_b, rem_b)
        return w4b, started

    def near2(w4, after):
        (w4w,), _ = _split_wait(*waves["pass"], [w4], [], after, _pass_copy, "weights_pass_wait")
        return w4w

    def far(w4, after, filler):
        (w4w,), _ = _split_wait(*waves["pass"], [w4], [], after, _relay_copy, "weights_far_wait")
        w4x, sem_c, rem_c, wo4s, started = _forward_then_start(w4w, wo4c, all_k, "weights_far_forward_out_start")
        (w4f,), _ = _split_wait(sem_c, rem_c, [w4x], [], filler(started), _diag_forward_copy,
                                "weights_far_forward_wait")
        waves["out"] = (sem_c, rem_c, wo4s)
        return w4f, started

    def w_out_arrived(after):
        sem_c, rem_c, wo4s = waves["out"]
        (wow,) = _halves_wait(sem_c, rem_c, [wo4s], after, all_k, "weights_out_wait")
        sem_d, rem_d, wof, started = _split_start([wow], [], 3, _forward_copies, "weights_out_forward_start")
        waves["out_forward"] = (sem_d, rem_d, wof)
        return started

    def w_out_gathered(after):
        sem_d, rem_d, wof = waves["out_forward"]
        (wo,), _ = _split_wait(sem_d, rem_d, [wof], [], after, _forward_copies, "weights_out_forward_wait")
        return wo.reshape(D, D)

    weights = dict(own=w_in[0], jvec=jvec, started=token, first=jvec ^ (1 + cvec), second=jvec ^ (2 - cvec),
                   near=near, near2=near2, far=far, out_arrived=w_out_arrived, out=w_out_gathered)

    exchange = _exchange_copies
    pending = {}

    def on_g_w_out(g_w_out):
        out = _split_start([g_w_out], [SDS((4, D // 8, D), f32)], 4, exchange, "grad_out_pair_start")
        pending["ex_out"] = out
        return out[4]

    def after_conv(dp8):
        ssem_o, rsem_o, g_o, land_o, _ = pending["ex_out"]
        (g_o,), (ex_o,) = _split_wait(ssem_o, rsem_o, [g_o], [land_o], dp8, exchange, "grad_out_pair_wait")
        to32, tob = _pair_sum_w_out(g_o, ex_o, cvec)
        out = _split_start([tob], [SDS((3, D // 8, D), bf16)], 3, _scatter_copies, "grad_out_chip_start")
        pending["sc_out"] = (out, to32)
        return out[4]

    def on_g_w_in_first(g_first):
        out = _split_start([g_first], [SDS((4, D // 2, D), f32)], 4, _block_exchange_copies, "grad_pair_start")
        pending["ex"] = (out[0], out[1], [out[2]], [out[3]])
        return out[4]

    def on_w_in_pair_sum(pair_sum):
        ex_ssem, ex_rsem, ex_srcs, ex_lands = pending["ex"]
        t32, tb = pair_sum(ex_srcs[0], ex_lands[0], ex_ssem, ex_rsem)
        out = _split_start([tb], [SDS((3, D // 2, D), bf16)], 3, _scatter_copies, "grad_chip_start")
        pending["sc_in"] = (out, t32)
        return out[4]

    r = _local_step(x[0], ctx[0], loss_target[0], mrow, mrow_c, b_ada, norm_g, weights, q_norm_g, k_norm_g,
                    rpb[0], conv_w_full, conv_b,
                    dict(g_w_out=on_g_w_out, after_conv=after_conv, first_half=1 - cvec,
                         g_w_in_first=on_g_w_in_first, w_in_pair_sum=on_w_in_pair_sum))
    sc_in, t32 = pending["sc_in"]
    _, (r2,) = _split_wait(sc_in[0], sc_in[1], [sc_in[2]], [sc_in[3]], r["dng"], _scatter_copies, "grad_chip_wait")
    u_in = _chip_sum(t32, r2, jvec, "chip_sum_w_in")
    sc_o, to32 = pending["sc_out"]
    _, (ro2,) = _split_wait(sc_o[0], sc_o[1], [sc_o[2]], [sc_o[3]], u_in, _scatter_copies, "grad_out_chip_wait")
    u_out = _chip_sum(to32, ro2, jvec, "chip_sum_w_out")
    swap = _split_start([u_in, u_out], [SDS(u_in.shape, f32), SDS(u_out.shape, f32)], 2, _swap_copies,
                        "grad_pair_swap_start")

    dm = jnp.concatenate([r["dshift"], r["dscale"], r["dgate"]], axis=1)
    dmc = jnp.concatenate([r["dshift_c"], r["dscale_c"], jnp.zeros((1, D), f32)], axis=1)
    pack_parts = [_rows128(dm), _rows128(dmc), _rows128(r["dng"]), _rows128(r["dng_c"]), _rows128(r["g_gq"]),
                  _rows128(r["g_gk"]), _rows128(r["g_gk_c"]), r["g_rpb"].reshape(H * N_DR, 128),
                  _rows128(r["g_conv_b"]), _rows128(r["g_conv_w"][0:3]), jnp.pad(r["loss_sum"], ((0, 0), (0, 127)))]
    pack = jnp.concatenate([jnp.pad(p, ((0, -p.shape[0] % 8), (0, 0))) for p in pack_parts], axis=0)
    assert pack.shape[0] == PK_ROWS
    gs = _split_start([_to_slot(pack, 8, dev)], [], 7, _gather8_copies, "gather_small_start", after=swap[6])
    (u_in, u_out), (o_in, o_out) = _split_wait(swap[0], swap[1], swap[2:4], swap[4:6], gs[3], _swap_copies,
                                               "grad_pair_swap_wait")
    g_w_in_s, d_w_in, nm_w_in, nv_w_in = _adamw_halves(w_in[0], u_in, o_in, m_w_in[0], v_w_in[0], cvec, "adamw_w_in",
                                                       after=gs[3])
    g_w_out_s, d_w_out, nm_w_out, nv_w_out = _adamw_halves(w_out[0], u_out, o_out, m_w_out[0], v_w_out[0], cvec,
                                                           "adamw_w_out", after=nm_w_in)
    (gathered,), _ = _split_wait(gs[0], gs[1], [gs[2]], [], nm_w_out, _gather8_copies, "gather_small_wait")
    red, dm16 = _small_reduce(gathered)
    loss = red[RS_LOSS, 0] * (0.5 / D)

    g_w_ada_s, cpart = _w_ada_grad(sc16, dm16, w_ada_s, jvec)
    gcp = _split_start([_to_slot(cpart, 4, chip)], [], 3, _gather4_copies, "gather_c_ctx_parts_start")
    d_w_ada, nm_w_ada, nv_w_ada = _adamw(w_ada_s, g_w_ada_s, m_w_ada[0], v_w_ada[0], "adamw_w_ada", after=gcp[3])
    (cparts4,), _ = _split_wait(gcp[0], gcp[1], [gcp[2]], [], nm_w_ada, _gather4_copies, "gather_c_ctx_parts_wait")
    g_c_ctx = _c_ctx_grad(cparts4, c_ctx)

    t_rpb = lambda a: jnp.transpose(a, (0, 2, 1, 3)).reshape(N_DR, H, N_DC)
    t_cw = lambda a: jnp.transpose(a, (1, 0, 2))
    small = _adamw_small(
        red, g_c_ctx, jvec,
        [c_ctx, b_ada, norm_g, q_norm_g, k_norm_g, t_rpb(rpb), t_cw(conv_w), conv_b],
        [m_c_ctx, m_b_ada, m_norm_g, m_q_norm_g, m_k_norm_g, t_rpb(m_rpb), t_cw(m_conv_w), m_conv_b],
        [v_c_ctx, v_b_ada, v_norm_g, v_q_norm_g, v_k_norm_g, t_rpb(v_rpb), t_cw(v_conv_w), v_conv_b])
    for kind in small:
        kind[5] = jnp.transpose(kind[5].reshape(1, N_DR, H, N_DC), (0, 2, 1, 3))
        kind[6] = jnp.transpose(kind[6], (1, 0, 2))

    def ordered(kind, big_w_ada, big_w_in, big_w_out):
        s_c_ctx, s_b_ada, s_norm_g, s_q, s_k, s_rpb, s_conv_w, s_conv_b = small[kind]
        return [s_c_ctx, big_w_ada[None], s_b_ada, s_norm_g, big_w_in[None], s_q, s_k, s_rpb, s_conv_w,
                s_conv_b, big_w_out[None]]

    grads = ordered(0, g_w_ada_s, g_w_in_s, g_w_out_s)
    deltas = ordered(1, d_w_ada, d_w_in, d_w_out)
    new_m = ordered(2, nm_w_ada, nm_w_in, nm_w_out)
    new_v = ordered(3, nv_w_ada, nv_w_in, nv_w_out)
    return (loss, r["grad_x"][None], *grads, *deltas, *new_m, *new_v)
```

```python
import functools

import jax
import jax.numpy as jnp
import numpy as np
from jax import lax
from jax.experimental import pallas as pl
from jax.experimental.pallas import tpu as pltpu

f32, bf16, i32 = jnp.float32, jnp.bfloat16, jnp.int32
MESH = pl.DeviceIdType.MESH
HIGHEST = lax.Precision.HIGHEST

D = 1024
S = 2048
L = 256
GW = 64
ROWS = S // GW
H = 8
DH = 64
DA = H * DH
DC = 512
WIN_H, WIN_W = 8, 16
N_DR, N_DC = 2 * WIN_H - 1, 2 * WIN_W - 1
RMS_EPS = 1e-6
ROPE_THETA = 10000.0
QK_SCALE = DH ** -0.5
NEG = -1e30

QB = 128
NQB = S // QB
KR = 9
KB = KR * GW
TILE_GEOM = ((0, 0), (2, 0), (4, 0), (28, 23), (30, 23))
NT = len(TILE_GEOM)

ADAM_LR, ADAM_B1, ADAM_B2, ADAM_EPS, ADAM_WD, ADAM_STEP = 0.001, 0.9, 0.999, 1e-08, 0.01, 10

VMEM_SPEC = pl.BlockSpec(memory_space=pltpu.VMEM)
ANY_SPEC = pl.BlockSpec(memory_space=pl.ANY)
SMEM_SPEC = pl.BlockSpec(memory_space=pltpu.SMEM)
SDS = jax.ShapeDtypeStruct


_pallas_call = pl.pallas_call


def _hbm_call(body, *, out_shape, in_specs=None, out_specs=None, grid_spec=None, **kw):
    n_pre = 0
    if grid_spec is not None:
        ispecs, ospecs, n_pre = grid_spec.in_specs, grid_spec.out_specs, grid_spec.num_scalar_prefetch
        kw["grid_spec"] = grid_spec
    else:
        ispecs, ospecs = in_specs, out_specs
        kw.update(in_specs=in_specs, out_specs=out_specs)

    def blocked(spec):
        return isinstance(spec, pl.BlockSpec) and spec.block_shape is not None

    single = not isinstance(out_shape, (tuple, list))
    shapes = [out_shape] if single else list(out_shape)
    ospec_list = list(ospecs) if isinstance(ospecs, (tuple, list)) else [ospecs]
    shapes = [pltpu.HBM(s.shape, s.dtype) if blocked(sp) else s for s, sp in zip(shapes, ospec_list)]
    call = _pallas_call(body, out_shape=shapes[0] if single else tuple(shapes), **kw)

    def run(*args):
        arrays = [pltpu.with_memory_space_constraint(a, pltpu.HBM) if blocked(sp) else a
                  for a, sp in zip(args[n_pre:], ispecs)]
        return call(*args[:n_pre], *arrays)

    return run


def _cp(vmem_mb=None, **kw):
    if vmem_mb is not None:
        kw["vmem_limit_bytes"] = vmem_mb << 20
    return pltpu.CompilerParams(**kw)


def _silu(z):
    return z * jax.nn.sigmoid(z)


def _dsilu(z):
    sg = jax.nn.sigmoid(z)
    return sg * (1.0 + z * (1.0 - sg))


def _row_start(i):
    return min(max(i - WIN_H // 2, 0), ROWS - WIN_H)


def _my_pos():
    return lax.axis_index("x"), lax.axis_index("y"), lax.axis_index("c")


def _flip(v, bit):
    return 1 - v if bit else v


def _swap_copies(srcs, lands, ssem, rsem):
    x, y, c = _my_pos()
    return [pltpu.make_async_remote_copy(src_ref=srcs[a], dst_ref=lands[a], send_sem=ssem.at[a], recv_sem=rsem.at[a],
                                         device_id=(x, y, 1 - c), device_id_type=MESH) for a in range(len(srcs))]


HBM_SPEC = pl.BlockSpec(memory_space=pltpu.HBM)
SEM_SPEC = pl.BlockSpec(memory_space=pltpu.SEMAPHORE)
DATAFLOW = pltpu.SideEffectType.DATAFLOW_SIDE_EFFECTING


def _peer_chips(x, y, c):
    out = []
    for k in range(1, 4):
        px, py = _flip(x, (k >> 1) & 1), _flip(y, k & 1)
        out.append(((px, py, c), 2 * px + py))
    return out


def _half_copies(srcs, dsts, ssem, rsem, which):
    x, y, c = _my_pos()
    j = 2 * x + y
    peers = _peer_chips(x, y, c)
    pairs = []
    for pos, group, k in which:
        half = srcs[pos].shape[1] // 2
        mine = pl.ds(pl.multiple_of(c * half, 8), half)
        dev, pj = peers[k]
        sem = 3 * group + k
        send = pltpu.make_async_remote_copy(src_ref=srcs[pos].at[j, mine], dst_ref=dsts[pos].at[j, mine],
                                            send_sem=ssem.at[sem], recv_sem=rsem.at[sem], device_id=dev,
                                            device_id_type=MESH)
        arrive = pltpu.make_async_remote_copy(src_ref=srcs[pos].at[j, mine], dst_ref=dsts[pos].at[pj, mine],
                                              send_sem=ssem.at[sem], recv_sem=rsem.at[sem], device_id=dev,
                                              device_id_type=MESH)
        pairs.append((send, arrive))
    return pairs


def _halves_wait(ssem, rsem, bigs, after, which, name):
    nb = len(bigs)

    def body(*refs):
        b_in = refs[:nb]
        ssem_ref, rsem_ref = refs[nb], refs[nb + 1]
        for send, arrive in _half_copies(b_in, b_in, ssem_ref, rsem_ref, which):
            send.wait_send()
            arrive.wait_recv()

    return _hbm_call(
        body, name=name, out_shape=tuple(pltpu.HBM(b.shape, b.dtype) for b in bigs),
        in_specs=[HBM_SPEC] * nb + [SEM_SPEC, SEM_SPEC, ANY_SPEC], out_specs=tuple([HBM_SPEC] * nb),
        input_output_aliases={a: a for a in range(nb)}, compiler_params=_cp(has_side_effects=DATAFLOW),
    )(*bigs, ssem, rsem, after)


FORWARD_SEM = 3


def _diag_forward_copy(srcs, dsts, ssem, rsem):
    x, y, c = _my_pos()
    half = srcs[0].shape[1] // 2
    diag = 3 - (2 * x + y)
    mine = pl.ds(pl.multiple_of(c * half, 8), half)
    other = pl.ds(pl.multiple_of((1 - c) * half, 8), half)
    return [_Copy(srcs[0].at[diag, mine], dsts[0].at[diag, mine], dsts[0].at[diag, other], ssem.at[FORWARD_SEM],
                  rsem.at[FORWARD_SEM], (x, y, 1 - c))]


def _forward_then_start(fwd, big, order, name):
    def body(f_in, b_in, f_out, ssem, rsem, b_out, token):
        _diag_forward_copy([f_in], [f_out], ssem, rsem)[0].start()
        for send, _ in _half_copies([b_in], [b_out], ssem, rsem, order):
            send.start()
        token[...] = jnp.zeros_like(token)

    n_sem = FORWARD_SEM + 1
    out_shape = (pltpu.HBM(fwd.shape, fwd.dtype), pltpu.SemaphoreType.DMA((n_sem,)), pltpu.SemaphoreType.DMA((n_sem,)),
                 pltpu.HBM(big.shape, big.dtype), SDS((8, 128), f32))
    return _hbm_call(
        body, name=name, out_shape=out_shape, in_specs=[HBM_SPEC, HBM_SPEC],
        out_specs=(HBM_SPEC, SEM_SPEC, SEM_SPEC, HBM_SPEC, VMEM_SPEC), input_output_aliases={0: 0, 1: 3},
        compiler_params=_cp(has_side_effects=DATAFLOW),
    )(*[pltpu.with_memory_space_constraint(b, pltpu.HBM) for b in (fwd, big)])


def _cast_to_slot(w, jvec, name, after=None):
    rows, cols = w.shape
    tr = min(128, rows // 4)
    band = lambda k: slice(k * tr, (k + 1) * tr)
    return _stream_call(
        lambda x: (x.astype(bf16),), rows // tr, [(w, (tr, cols), lambda ref, k, j: ref.at[band(k)])],
        [((4, rows, cols), bf16, (tr, cols), lambda ref, k, j: ref.at[j, band(k)])], jvec, name, after)[0]


def _stream_call(fn, n_chunks, srcs, outs, vec, name, after=None):
    ns, no = len(srcs), len(outs)

    def body(*refs):
        s = refs[0][0]
        in_refs, out_refs = refs[1:1 + ns], refs[2 + ns:2 + ns + no]
        in_bufs, out_bufs = refs[2 + ns + no:2 + 2 * ns + no], refs[2 + 2 * ns + no:2 + 2 * ns + 2 * no]
        sem_in, sem_out = refs[2 + 2 * ns + 2 * no:]
        loads, stores = [], []
        for k in range(n_chunks):
            cps = [pltpu.make_async_copy(srcs[a][2](in_refs[a], k, s), in_bufs[a].at[k], sem_in.at[a, k])
                   for a in range(ns)]
            for cp in cps:
                cp.start()
            loads.append(cps)
        for k, cps in enumerate(loads):
            for cp in cps:
                cp.wait()
            for b, val in enumerate(fn(*[buf[k] for buf in in_bufs])):
                out_bufs[b][k] = val
                cp = pltpu.make_async_copy(out_bufs[b].at[k], outs[b][3](out_refs[b], k, s), sem_out.at[b, k])
                cp.start()
                stores.append(cp)
        for cp in stores:
            cp.wait()

    in_bufs = [pltpu.VMEM((n_chunks,) + tuple(shape), a.dtype) for a, shape, _ in srcs]
    out_bufs = [pltpu.VMEM((n_chunks,) + tuple(cshape), dtype) for _, dtype, cshape, _ in outs]
    n_bytes = sum(np.prod(b.shape) * np.dtype(b.dtype).itemsize for b in in_bufs + out_bufs)
    after_big = after is not None and after.size * after.dtype.itemsize > (1 << 20)
    after = pltpu.with_memory_space_constraint(after, pltpu.HBM) if after_big else after
    res = _pallas_call(
        body, name=name, out_shape=tuple(pltpu.HBM(shape, dtype) for shape, dtype, _, _ in outs),
        in_specs=[SMEM_SPEC] + [HBM_SPEC] * ns + [HBM_SPEC if after_big else ANY_SPEC], out_specs=(HBM_SPEC,) * no,
        scratch_shapes=in_bufs + out_bufs + [pltpu.SemaphoreType.DMA((ns, n_chunks)),
                                             pltpu.SemaphoreType.DMA((no, n_chunks))],
        compiler_params=_cp(int(n_bytes) // (1 << 20) + 8),
    )(vec, *[pltpu.with_memory_space_constraint(a, pltpu.HBM) for a, _, _ in srcs], vec if after is None else after)
    return res


def _exchange_copies(srcs, lands, ssem, rsem):
    x, y, c = _my_pos()
    half = srcs[0].shape[0] // 8
    cps = []
    for jb in range(4):
        src = srcs[0].at[pl.ds(pl.multiple_of((2 * jb + 1 - c) * half, 8), half)]
        cps.append(pltpu.make_async_remote_copy(src_ref=src, dst_ref=lands[0].at[jb], send_sem=ssem.at[jb],
                                                recv_sem=rsem.at[jb], device_id=(x, y, 1 - c), device_id_type=MESH))
    return cps


def _block_exchange_copies(srcs, lands, ssem, rsem):
    x, y, c = _my_pos()
    return [pltpu.make_async_remote_copy(src_ref=srcs[0].at[jb], dst_ref=lands[0].at[jb], send_sem=ssem.at[jb],
                                         recv_sem=rsem.at[jb], device_id=(x, y, 1 - c), device_id_type=MESH)
            for jb in range(4)]


def _scatter_copies(srcs, lands, ssem, rsem):
    x, y, c = _my_pos()
    cps = []
    for a in range(len(srcs)):
        for k, (dev, pj) in enumerate(_peer_chips(x, y, c)):
            cps.append(pltpu.make_async_remote_copy(src_ref=srcs[a].at[pj], dst_ref=lands[a].at[k],
                                                    send_sem=ssem.at[3 * a + k], recv_sem=rsem.at[3 * a + k],
                                                    device_id=dev, device_id_type=MESH))
    return cps


class _Copy:
    def __init__(self, src, dst, arrive, ssem, rsem, dev):
        make = lambda to: pltpu.make_async_remote_copy(src_ref=src, dst_ref=to, send_sem=ssem, recv_sem=rsem,
                                                       device_id=dev, device_id_type=MESH)
        send, arrival = make(dst), make(arrive)
        self.start, self.wait_send, self.wait_recv = send.start, send.wait_send, arrival.wait_recv


def _toward(x, y, along_x):
    return x + along_x * (1 - 2 * x), y + (1 - along_x) * (1 - 2 * y)


def _near_copies(srcs, dsts, ssem, rsem):
    x, y, c = _my_pos()
    px, py = _toward(x, y, c)
    j = 2 * x + y
    return [_Copy(srcs[0].at[j], dsts[0].at[j], dsts[0].at[2 * px + py], ssem.at[0], rsem.at[0], (px, py, c))]


def _pass_copy(srcs, dsts, ssem, rsem):
    x, y, c = _my_pos()
    px, py = _toward(x, y, c)
    qx, qy = _toward(x, y, 1 - c)
    got = 2 * px + py
    return [_Copy(srcs[0].at[got], dsts[0].at[got], dsts[0].at[2 * qx + qy], ssem.at[0], rsem.at[0], (x, y, 1 - c))]


def _relay_copy(srcs, dsts, ssem, rsem):
    x, y, c = _my_pos()
    px, py = _toward(x, y, c)
    qx, qy = _toward(x, y, 1 - c)
    half = srcs[0].shape[1] // 2
    mine = pl.ds(pl.multiple_of(c * half, 8), half)
    got, diag = 2 * px + py, 3 - (2 * x + y)
    return [_Copy(srcs[0].at[got, mine], dsts[0].at[got, mine], dsts[0].at[diag, mine], ssem.at[1], rsem.at[1],
                  (qx, qy, c))]


def _forward_copies(srcs, dsts, ssem, rsem):
    x, y, c = _my_pos()
    half = srcs[0].shape[1] // 2
    mine = pl.ds(pl.multiple_of(c * half, 8), half)
    other = pl.ds(pl.multiple_of((1 - c) * half, 8), half)
    return [_Copy(srcs[0].at[pj, mine], dsts[0].at[pj, mine], dsts[0].at[pj, other], ssem.at[k], rsem.at[k],
                  (x, y, 1 - c)) for k, (_, pj) in enumerate(_peer_chips(x, y, c))]


def _pass_relay_copies(srcs, dsts, ssem, rsem):
    return _pass_copy(srcs, dsts, ssem, rsem) + _relay_copy(srcs, dsts, ssem, rsem)


def _gather8_copies(srcs, dsts, ssem, rsem):
    x, y, c = _my_pos()
    me = 4 * x + 2 * y + c
    cps = []
    for a in range(len(srcs)):
        for k in range(1, 8):
            tgt = (_flip(x, (k >> 2) & 1), _flip(y, (k >> 1) & 1), _flip(c, k & 1))
            cps.append(_Copy(srcs[a].at[me], dsts[a].at[me], dsts[a].at[4 * tgt[0] + 2 * tgt[1] + tgt[2]],
                             ssem.at[7 * a + k - 1], rsem.at[7 * a + k - 1], tgt))
    return cps


def _gather4_copies(srcs, dsts, ssem, rsem):
    x, y, c = _my_pos()
    j = 2 * x + y
    cps = []
    for a in range(len(srcs)):
        for k, (dev, pj) in enumerate(_peer_chips(x, y, c)):
            cps.append(_Copy(srcs[a].at[j], dsts[a].at[j], dsts[a].at[pj], ssem.at[3 * a + k], rsem.at[3 * a + k], dev))
    return cps


def _to_slot(a, n, i):
    return lax.dynamic_update_slice(jnp.zeros((n,) + a.shape, a.dtype), a[None], (i,) + (0,) * a.ndim)


def _split_start(srcs, land_shapes, n_cp, make, name, after=None):
    ns, nl = len(srcs), len(land_shapes)
    n_in = ns + nl + (after is not None)

    def body(*refs):
        s_in = refs[:ns]
        ssem, rsem = refs[n_in], refs[n_in + 1]
        s_out = refs[n_in + 2:n_in + 2 + ns]
        l_out = refs[n_in + 2 + ns:n_in + 2 + ns + nl]
        token = refs[n_in + 2 + ns + nl]
        for cp in make(s_in, l_out if nl else s_out, ssem, rsem):
            cp.start()
        token[...] = jnp.zeros_like(token)

    lands = [pltpu.with_memory_space_constraint(lax.empty(sh.shape, sh.dtype), pltpu.HBM) for sh in land_shapes]
    out_shape = (pltpu.SemaphoreType.DMA((n_cp,)), pltpu.SemaphoreType.DMA((n_cp,)),
                 *[pltpu.HBM(b.shape, b.dtype) for b in srcs], *[pltpu.HBM(b.shape, b.dtype) for b in land_shapes],
                 SDS((8, 128), f32))
    return _hbm_call(
        body, name=name, out_shape=out_shape, in_specs=[HBM_SPEC] * (ns + nl) + [ANY_SPEC] * (after is not None),
        out_specs=(SEM_SPEC, SEM_SPEC, *[HBM_SPEC] * (ns + nl), VMEM_SPEC),
        input_output_aliases={i: 2 + i for i in range(ns + nl)}, compiler_params=_cp(has_side_effects=DATAFLOW),
    )(*[pltpu.with_memory_space_constraint(b, pltpu.HBM) for b in srcs], *lands, *([] if after is None else [after]))


def _split_wait(ssem, rsem, srcs, lands, after, make, name):
    ns, nl = len(srcs), len(lands)

    def body(*refs):
        s_in, l_in = refs[:ns], refs[ns:ns + nl]
        ssem_ref, rsem_ref = refs[ns + nl], refs[ns + nl + 1]
        for cp in make(s_in, l_in if nl else s_in, ssem_ref, rsem_ref):
            cp.wait_send()
            cp.wait_recv()

    outs = _hbm_call(
        body, name=name, out_shape=tuple(pltpu.HBM(b.shape, b.dtype) for b in (*srcs, *lands)),
        in_specs=[HBM_SPEC] * (ns + nl) + [SEM_SPEC, SEM_SPEC, ANY_SPEC], out_specs=tuple([HBM_SPEC] * (ns + nl)),
        input_output_aliases={i: i for i in range(ns + nl)}, compiler_params=_cp(has_side_effects=DATAFLOW),
    )(*srcs, *lands, ssem, rsem, after)
    return list(outs[:ns]), list(outs[ns:])


def _adaln_shard(cc, w_ada_shard):
    def body(c_ref, w_ref, m_ref, sc_ref):
        sc = _silu(c_ref[...])
        sc_ref[...] = sc
        m_ref[...] = jnp.dot(sc, w_ref[...], precision=HIGHEST, preferred_element_type=f32)

    return _hbm_call(
        body, name="adaln_shard", out_shape=(SDS((16, w_ada_shard.shape[1]), f32), SDS((16, D), f32)),
        in_specs=[VMEM_SPEC, VMEM_SPEC], out_specs=(VMEM_SPEC, VMEM_SPEC), compiler_params=_cp(32),
    )(cc, w_ada_shard)


def _prenorm(xx, norm_g, mrow, b_ada, tm, name, after=None):
    n = xx.shape[0]

    def norm(x, g, m, b):
        shift = m[:, 0:D] + b[:, 0:D]
        scale = m[:, D:2 * D] + b[:, D:2 * D]
        r = lax.rsqrt(jnp.mean(x * x, axis=-1, keepdims=True) + RMS_EPS)
        y = (x * r) * g
        return ((y * (1.0 + scale) + shift).astype(bf16),)

    band = lambda ref, k, s: ref.at[k * tm:(k + 1) * tm]
    whole = lambda ref, k, s: ref
    jvec = jnp.zeros((1,), i32)
    return _stream_call(
        norm, n // tm, [(xx, (tm, D), band), (norm_g, (1, D), whole), (mrow, (1, 3 * D), whole),
                        (b_ada, (1, 3 * D), whole)],
        [((n, D), bf16, (tm, D), band)], jvec, name, after)[0]


def _in_proj_own(h, w_own, jvec):
    tm = 512

    def body(j_ref, h_ref, w_ref, p_ref):
        p_ref[...] = jnp.dot(h_ref[...], w_ref[...].astype(bf16), preferred_element_type=f32)

    grid_spec = pltpu.PrefetchScalarGridSpec(
        num_scalar_prefetch=1, grid=(S // tm,),
        in_specs=[pl.BlockSpec((tm, D), lambda i, j: (i, 0)), pl.BlockSpec((D, D), lambda i, j: (0, 0))],
        out_specs=pl.BlockSpec((tm, D), lambda i, j: (i, j[0])))
    return _hbm_call(body, name="in_proj_own", out_shape=SDS((S, 4 * D), f32), grid_spec=grid_spec,
                     compiler_params=_cp(40))(jvec, h, w_own)


def _in_proj_block(h, w4, p, bvec, name, after=None):
    tm = 512

    def body(b_ref, h_ref, w_ref, p_in_ref, after_ref, p_ref):
        p_ref[...] = jnp.dot(h_ref[...], w_ref[...], preferred_element_type=f32)

    grid_spec = pltpu.PrefetchScalarGridSpec(
        num_scalar_prefetch=1, grid=(S // tm,),
        in_specs=[pl.BlockSpec((tm, D), lambda i, b: (i, 0)), pl.BlockSpec((None, D, D), lambda i, b: (b[0], 0, 0)),
                  ANY_SPEC, ANY_SPEC],
        out_specs=pl.BlockSpec((tm, D), lambda i, b: (i, b[0])))
    return _hbm_call(body, name=name, out_shape=SDS((S, 4 * D), f32), grid_spec=grid_spec,
                     input_output_aliases={3: 0})(bvec, h, w4, p, bvec if after is None else after)


def _ctx_proj(hc, w4):
    def body(h_ref, w0_ref, w1_ref, p_ref):
        hv = h_ref[...]
        p_ref[:, 0:DA] = jnp.dot(hv, w0_ref[:, DA:2 * DA], preferred_element_type=f32)
        p_ref[:, DA:2 * DA] = jnp.dot(hv, w1_ref[:, 0:DA], preferred_element_type=f32)

    return _hbm_call(
        body, name="ctx_proj", out_shape=SDS((L, 2 * DA), f32), grid=(1,),
        in_specs=[pl.BlockSpec((L, D), lambda i: (0, 0)), pl.BlockSpec((None, D, D), lambda i: (0, 0, 0)),
                  pl.BlockSpec((None, D, D), lambda i: (1, 0, 0))],
        out_specs=pl.BlockSpec((L, 2 * DA), lambda i: (0, 0)),
    )(hc, w4, w4)


def _head_ones():
    r = lax.broadcasted_iota(i32, (DA, DA), 0) // DH
    c = lax.broadcasted_iota(i32, (DA, DA), 1) // DH
    return (r == c).astype(bf16)


def _head_sum(v, ones_bd):
    hi = v.astype(bf16)
    lo = (v - hi.astype(f32)).astype(bf16)
    return jnp.dot(hi, ones_bd, preferred_element_type=f32) + jnp.dot(lo, ones_bd, preferred_element_type=f32)


def _swap16(v):
    lane = lax.broadcasted_iota(i32, v.shape, 1)
    return jnp.where((lane & 31) < 16, pltpu.roll(v, DA - 16, 1), pltpu.roll(v, 16, 1))


def _rope_block(ct_ref, rt_ref, tm):
    rows = [jnp.tile(rt_ref[8 * j:8 * j + 8, :], (GW // 8, 1)) for j in range(tm // GW)]
    return jnp.tile(ct_ref[...], (tm // GW, 1)) + jnp.concatenate(rows, axis=0)


def _rope_specs(tm):
    col = pl.BlockSpec((GW, DA), lambda i: (0, 0))
    row = pl.BlockSpec((8 * tm // GW, DA), lambda i: (i, 0))
    return [col, row, col, row]


def _qk_prep(p, gq, gk, rope):
    tm = 512

    def body(qk_ref, v_ref, gq_ref, gk_ref, cc_ref, cr_ref, sc_ref, sr_ref, qr_ref, qp_ref, kr_ref, vh_ref):
        ones_bd = _head_ones()
        cs, sn = _rope_block(cc_ref, cr_ref, tm), _rope_block(sc_ref, sr_ref, tm)
        q = qk_ref[:, 0:DA]
        k = qk_ref[:, DA:2 * DA]
        yq = (q * lax.rsqrt(_head_sum(q * q, ones_bd) * (1.0 / DH) + RMS_EPS)) * gq_ref[...]
        yk = (k * lax.rsqrt(_head_sum(k * k, ones_bd) * (1.0 / DH) + RMS_EPS)) * gk_ref[...]
        qr = (yq * cs + _swap16(yq) * sn) * QK_SCALE
        qp = yq * QK_SCALE
        kr = yk * cs + _swap16(yk) * sn
        vv = v_ref[...]
        for hh in range(H):
            sl = slice(hh * DH, (hh + 1) * DH)
            qr_ref[hh] = qr[:, sl].astype(bf16)
            qp_ref[hh] = qp[:, sl].astype(bf16)
            kr_ref[hh] = kr[:, sl].astype(bf16)
            vh_ref[hh] = vv[:, sl].astype(bf16)

    hm = SDS((H, S, DH), bf16)
    hspec = pl.BlockSpec((H, tm, DH), lambda i: (0, i, 0))
    fixed = lambda i: (0, 0)
    return _hbm_call(
        body, name="qk_prep", out_shape=(hm, hm, hm, hm), grid=(S // tm,),
        in_specs=[pl.BlockSpec((tm, 2 * DA), lambda i: (i, 0)), pl.BlockSpec((tm, DA), lambda i: (i, 2)),
                  pl.BlockSpec((1, DA), fixed), pl.BlockSpec((1, DA), fixed)] + _rope_specs(tm),
        out_specs=(hspec, hspec, hspec, hspec),
    )(p, p, gq, gk, *rope)


def _ctx_prep(pc, gk):
    def body(p_ref, gk_ref, kc_ref, vc_ref):
        ones_bd = _head_ones()
        k = p_ref[:, 0:DA]
        yk = (k * lax.rsqrt(_head_sum(k * k, ones_bd) * (1.0 / DH) + RMS_EPS)) * gk_ref[...]
        vv = p_ref[:, DA:2 * DA]
        for hh in range(H):
            sl = slice(hh * DH, (hh + 1) * DH)
            kc_ref[hh] = yk[:, sl].astype(bf16)
            vc_ref[hh] = vv[:, sl].astype(bf16)

    hm = SDS((H, L, DH), bf16)
    return _hbm_call(
        body, name="ctx_prep", out_shape=(hm, hm), in_specs=[VMEM_SPEC, VMEM_SPEC], out_specs=(VMEM_SPEC, VMEM_SPEC),
    )(pc, gk)


def _tile_pieces():
    out = []
    for (i0, u0) in TILE_GEOM:
        rows = []
        for j in range(2):
            i = i0 + j
            rs = _row_start(i)
            rows.append([(u0 + u - i + WIN_H - 1) if rs <= u0 + u < rs + WIN_H else None for u in range(KR)])
        out.append(rows)
    return out


def _bias_prep(rpb_rev_pad, after=None):
    pieces = _tile_pieces()

    def body(r_ref, after_ref, o_ref):
        rp = r_ref[...]
        xs = jnp.concatenate([pltpu.roll(jnp.broadcast_to(rp[dr:dr + 1, :], (GW, 128)), 128 - (WIN_W - 1), 1,
                                         stride=1, stride_axis=0) for dr in range(N_DR)], axis=0)
        row = lax.broadcasted_iota(i32, xs.shape, 0)
        lane = lax.broadcasted_iota(i32, xs.shape, 1)
        k = row & (GW - 1)
        c0 = jnp.clip(lane - WIN_W // 2, 0, GW - WIN_W)
        xs = jnp.where((k >= c0) & (k < c0 + WIN_W), xs, NEG)
        neg = jnp.full((GW, GW), NEG, f32)
        for t in range(NT):
            for j in range(2):
                for u in range(KR):
                    dr = pieces[t][j][u]
                    piece = neg if dr is None else xs[dr * GW:(dr + 1) * GW, 0:GW]
                    o_ref[t, u * GW:(u + 1) * GW, j * GW:(j + 1) * GW] = piece

    return _hbm_call(
        body, name="bias_prep", out_shape=SDS((H, NT, KB, QB), f32), grid=(H,),
        in_specs=[pl.BlockSpec((None, N_DR, 128), lambda h: (h, 0, 0)), ANY_SPEC],
        out_specs=pl.BlockSpec((None, NT, KB, QB), lambda h: (h, 0, 0, 0)),
    )(rpb_rev_pad, rpb_rev_pad if after is None else after)


def _bias_tiles(rpb2, after=None):
    return _bias_prep(jnp.pad(rpb2[:, :, ::-1], ((0, 0), (0, 0), (0, 128 - N_DC))), after)


def _block_geom(b):
    qs = b * QB
    ks = min(max(2 * b - 4, 0), ROWS - KR) * GW
    t = b if b < 2 else (b - (NQB - NT) if b > NQB - 3 else 2)
    return qs, ks, t


def _tt(a, b):
    return lax.dot_general(a, b, (((1,), (1,)), ((), ())), preferred_element_type=f32)


def _tn(a, b):
    return lax.dot_general(a, b, (((0,), (0,)), ((), ())), preferred_element_type=f32)


def _softmax_t(s_lat, s_ctx):
    m = jnp.maximum(jnp.max(s_lat, axis=0, keepdims=True), jnp.max(s_ctx, axis=0, keepdims=True))
    e_lat = jnp.exp(s_lat - m)
    e_ctx = jnp.exp(s_ctx - m)
    inv = 1.0 / (jnp.sum(e_lat, axis=0, keepdims=True) + jnp.sum(e_ctx, axis=0, keepdims=True))
    return e_lat * inv, e_ctx * inv


def _staged(n_blocks, stages):
    held = [dict() for _ in stages]
    for step in range(n_blocks + len(stages) - 1):
        for s, fn in enumerate(stages):
            b = step - s
            if 0 <= b < n_blocks:
                held[s][b] = fn(b) if s == 0 else fn(b, held[s - 1].pop(b))


def _attn_fwd(qr, qp, kr, vh, kc, vc, btt):
    def body(qr_ref, qp_ref, kr_ref, v_ref, kc_ref, vc_ref, bt_ref, o_ref):
        kcv, vcv = kc_ref[...], vc_ref[...]

        def scores(b):
            qs, ks, t = _block_geom(b)
            return (_tt(kr_ref[ks:ks + KB, :], qr_ref[qs:qs + QB, :]) + bt_ref[t], _tt(kcv, qp_ref[qs:qs + QB, :]))

        def probs(b, sc):
            p_lat, p_ctx = _softmax_t(*sc)
            return p_lat.astype(bf16), p_ctx.astype(bf16)

        def values(b, p):
            qs, ks, _ = _block_geom(b)
            o_ref[qs:qs + QB, :] = _tn(p[0], v_ref[ks:ks + KB, :]) + _tn(p[1], vcv)

        _staged(NQB, (scores, probs, values))

    sq = pl.BlockSpec((None, S, DH), lambda h: (h, 0, 0))
    sc = pl.BlockSpec((None, L, DH), lambda h: (h, 0, 0))
    return _hbm_call(
        body, name="attn_fwd", out_shape=SDS((H, S, DH), f32), grid=(H,),
        in_specs=[sq, sq, sq, sq, sc, sc, pl.BlockSpec((None, NT, KB, QB), lambda h: (h, 0, 0, 0))],
        out_specs=sq, compiler_params=_cp(48),
    )(qr, qp, kr, vh, kc, vc, btt)


def _shift_rows(v, down):
    n = v.shape[0]
    row = lax.broadcasted_iota(i32, v.shape, 0)
    if down:
        return jnp.where(row == 0, 0.0, pltpu.roll(v, 1, 0))
    return jnp.where(row == n - 1, 0.0, pltpu.roll(v, n - 1, 0))


def _conv_specs():
    col = lambda off: pl.BlockSpec((S, 128), lambda i, off=off: (0, off + i))
    return [col(16), col(20), col(24), col(28), pl.BlockSpec((3, 128), lambda i: (0, i)),
            pl.BlockSpec((1, 128), lambda i: (0, i))]


def _conv_fwd(p, conv_w, conv_b, after=None):
    def body(u_ref, bg_ref, cg_ref, zc_ref, w_ref, b_ref, after_ref, o_ref):
        cu = cg_ref[...] * u_ref[...]
        cv = b_ref[...] + _shift_rows(cu, True) * w_ref[0:1, :]
        cv = cv + cu * w_ref[1:2, :]
        cv = cv + _shift_rows(cu, False) * w_ref[2:3, :]
        o_ref[...] = ((bg_ref[...] * cv) * _silu(zc_ref[...])).astype(bf16)

    return _hbm_call(
        body, name="conv_fwd", out_shape=SDS((S, DC), bf16), grid=(DC // 128,),
        in_specs=_conv_specs() + [ANY_SPEC], out_specs=pl.BlockSpec((S, 128), lambda i: (0, i)),
        compiler_params=_cp(40),
    )(p, p, p, p, conv_w, conv_b, conv_b if after is None else after)


DP_Q, DP_K, DP_V, DP_ZA, DP_U, DP_BG, DP_CG, DP_ZC = range(8)


def _out_proj_loss(o, p, conv_g, w_out, xx, tgt, mrow, b_ada):
    tm = 512

    def body(o_ref, za_ref, c_ref, w_ref, x_ref, t_ref, m_ref, b_ref,
             dy_ref, dconv_ref, dp_ref, do_ref, gwo_ref, dgate_ref, loss_ref):
        k = pl.program_id(0)

        @pl.when(k == 0)
        def _():
            gwo_ref[...] = jnp.zeros_like(gwo_ref)
            dgate_ref[...] = jnp.zeros_like(dgate_ref)
            loss_ref[0, 0] = 0.0

        gate = m_ref[:, 2 * D:3 * D] + b_ref[:, 2 * D:3 * D]
        za = za_ref[...]
        sz = _silu(za)
        om = _merge_heads(o_ref)
        av, cv = (om * sz).astype(bf16), c_ref[...]
        mo = jnp.dot(av, w_ref[0:DA, :], preferred_element_type=f32)
        mo = mo + jnp.dot(cv, w_ref[DA:DA + DC, :], preferred_element_type=f32)
        y = x_ref[...] + gate * mo
        diff = y - t_ref[...]
        loss_ref[0, 0] += jnp.sum(diff * diff)
        dy = diff * (1.0 / D)
        dy_ref[...] = dy
        dgate_ref[...] += jnp.sum(dy * mo, axis=0, keepdims=True)
        dmo = (dy * gate).astype(bf16)
        dmix = _tt(dmo, w_ref[...])
        dattn = dmix[:, 0:DA]
        dconv_ref[...] = dmix[:, DA:DA + DC]
        a = dattn * sz
        for hh in range(H):
            do_ref[hh] = a[:, hh * DH:(hh + 1) * DH].astype(bf16)
        dp_ref[...] = ((dattn * _dsilu(za)) * om).astype(bf16)
        gwo_ref[0:DA, :] += _tn(av, dmo)
        gwo_ref[DA:DA + DC, :] += _tn(cv, dmo)

    row = lambda i: (i, 0)
    fixed = lambda i: (0, 0)
    hspec = pl.BlockSpec((H, tm, DH), lambda i: (0, i, 0))
    return _hbm_call(
        body, name="out_proj_loss",
        out_shape=(SDS((S, D), f32), SDS((S, DC), f32), SDS((8, S, DA), bf16), SDS((H, S, DH), bf16),
                   SDS((D, D), f32), SDS((1, D), f32), SDS((1, 1), f32)),
        grid=(S // tm,),
        in_specs=[hspec, pl.BlockSpec((tm, DA), lambda i: (i, 3)), pl.BlockSpec((tm, DC), row),
                  pl.BlockSpec((D, D), fixed), pl.BlockSpec((tm, D), row), pl.BlockSpec((tm, D), row),
                  pl.BlockSpec((1, 3 * D), fixed), pl.BlockSpec((1, 3 * D), fixed)],
        out_specs=(pl.BlockSpec((tm, D), row), pl.BlockSpec((tm, DC), row),
                   pl.BlockSpec((None, tm, DA), lambda i: (DP_ZA, i, 0)), hspec, pl.BlockSpec((D, D), fixed),
                   pl.BlockSpec((1, D), fixed), SMEM_SPEC),
        compiler_params=_cp(56, dimension_semantics=("arbitrary",)),
    )(o, p, conv_g, w_out, xx, tgt, mrow, b_ada)


def _conv_bwd(dconv, p, conv_w, conv_b, dp8, after=None):
    def body(d_ref, u_ref, bg_ref, cg_ref, zc_ref, w_ref, b_ref, dp_in_ref, after_ref, dp_ref, gw_ref, gb_ref):
        du_ref, dbg_ref, dcg_ref, dzc_ref = dp_ref.at[0], dp_ref.at[1], dp_ref.at[2], dp_ref.at[3]
        dconv = d_ref[...]
        u, bg, cg, zc = u_ref[...], bg_ref[...], cg_ref[...], zc_ref[...]
        w0, w1, w2 = w_ref[0:1, :], w_ref[1:2, :], w_ref[2:3, :]
        cu = cg * u
        cu_m, cu_p = _shift_rows(cu, True), _shift_rows(cu, False)
        cv = b_ref[...] + cu_m * w0
        cv = cv + cu * w1
        cv = cv + cu_p * w2
        sz = _silu(zc)
        dbg_ref[...] = ((dconv * sz) * cv).astype(bf16)
        dzc_ref[...] = ((dconv * (bg * cv)) * _dsilu(zc)).astype(bf16)
        dcv = (dconv * sz) * bg
        gb_ref[...] = jnp.sum(dcv, axis=0, keepdims=True)
        gw_ref[0:1, :] = jnp.sum(dcv * cu_m, axis=0, keepdims=True)
        gw_ref[1:2, :] = jnp.sum(dcv * cu, axis=0, keepdims=True)
        gw_ref[2:3, :] = jnp.sum(dcv * cu_p, axis=0, keepdims=True)
        gw_ref[3:8, :] = jnp.zeros((5, 128), f32)
        dcu = _shift_rows(dcv, False) * w0 + dcv * w1 + _shift_rows(dcv, True) * w2
        dcg_ref[...] = (dcu * u).astype(bf16)
        du_ref[...] = (dcu * cg).astype(bf16)

    return _hbm_call(
        body, name="conv_bwd", out_shape=(SDS((8, S, DC), bf16), SDS((8, DC), f32), SDS((1, DC), f32)),
        grid=(DC // 128,),
        in_specs=[pl.BlockSpec((S, 128), lambda i: (0, i))] + _conv_specs() + [ANY_SPEC, ANY_SPEC],
        out_specs=(pl.BlockSpec((4, S, 128), lambda i: (DP_U // 4, 0, i)), pl.BlockSpec((8, 128), lambda i: (0, i)),
                   pl.BlockSpec((1, 128), lambda i: (0, i))),
        input_output_aliases={7: 0}, compiler_params=_cp(48),
    )(dconv, p, p, p, p, conv_w, conv_b, dp8, conv_b if after is None else after)


def _attn_bwd(qr, qp, kr, vh, kc, vc, btt, do, after=None):
    def body(qr_ref, qp_ref, kr_ref, v_ref, kc_ref, vc_ref, bt_ref, do_ref, after_ref,
             dqr_ref, dqp_ref, dkr_ref, dv_ref, dkc_ref, dvc_ref, dbt_ref):
        kcv, vcv = kc_ref[...], vc_ref[...]
        dkr_ref[...] = jnp.zeros_like(dkr_ref)
        dv_ref[...] = jnp.zeros_like(dv_ref)
        dbt_ref[...] = jnp.zeros_like(dbt_ref)
        ctx_acc = {}

        def products(b):
            qs, ks, t = _block_geom(b)
            dob = do_ref[qs:qs + QB, :]
            s_lat = _tt(kr_ref[ks:ks + KB, :], qr_ref[qs:qs + QB, :]) + bt_ref[t]
            s_ctx = _tt(kcv, qp_ref[qs:qs + QB, :])
            return s_lat, s_ctx, _tt(v_ref[ks:ks + KB, :], dob), _tt(vcv, dob)

        def score_grads(b, x):
            s_lat, s_ctx, dp_lat, dp_ctx = x
            p_lat, p_ctx = _softmax_t(s_lat, s_ctx)
            delta = jnp.sum(p_lat * dp_lat, axis=0, keepdims=True) + jnp.sum(p_ctx * dp_ctx, axis=0, keepdims=True)
            ds_lat = p_lat * (dp_lat - delta)
            ds_ctx = p_ctx * (dp_ctx - delta)
            return ds_lat, ds_lat.astype(bf16), ds_ctx.astype(bf16), p_lat.astype(bf16), p_ctx.astype(bf16)

        def operand_grads(b, y):
            qs, ks, t = _block_geom(b)
            ds_lat, dsb_lat, dsb_ctx, pb_lat, pb_ctx = y
            qrb, qpb, dob = qr_ref[qs:qs + QB, :], qp_ref[qs:qs + QB, :], do_ref[qs:qs + QB, :]
            dbt_ref[t] += ds_lat
            dqr_ref[qs:qs + QB, :] = _tn(dsb_lat, kr_ref[ks:ks + KB, :])
            dqp_ref[qs:qs + QB, :] = _tn(dsb_ctx, kcv)
            dkr_ref[ks:ks + KB, :] += jnp.dot(dsb_lat, qrb, preferred_element_type=f32)
            dv_ref[ks:ks + KB, :] += jnp.dot(pb_lat, dob, preferred_element_type=f32)
            dkc = jnp.dot(dsb_ctx, qpb, preferred_element_type=f32)
            dvc = jnp.dot(pb_ctx, dob, preferred_element_type=f32)
            ctx_acc["k"] = dkc if b == 0 else ctx_acc["k"] + dkc
            ctx_acc["v"] = dvc if b == 0 else ctx_acc["v"] + dvc

        _staged(NQB, (products, score_grads, operand_grads))
        dkc_ref[...] = ctx_acc["k"]
        dvc_ref[...] = ctx_acc["v"]

    sq = pl.BlockSpec((None, S, DH), lambda h: (h, 0, 0))
    sc = pl.BlockSpec((None, L, DH), lambda h: (h, 0, 0))
    sb = pl.BlockSpec((None, NT, KB, QB), lambda h: (h, 0, 0, 0))
    big, ctxs = SDS((H, S, DH), f32), SDS((H, L, DH), f32)
    return _hbm_call(
        body, name="attn_bwd", out_shape=(big, big, big, big, ctxs, ctxs, SDS((H, NT, KB, QB), f32)), grid=(H,),
        in_specs=[sq, sq, sq, sq, sc, sc, sb, sq, ANY_SPEC], out_specs=(sq, sq, sq, sq, sc, sc, sb),
        compiler_params=_cp(56),
    )(qr, qp, kr, vh, kc, vc, btt, do, do if after is None else after)


def _bias_bwd(dbtt, after=None):
    pieces = _tile_pieces()

    def body(d_ref, after_ref, o_ref, scr):
        scr[...] = jnp.zeros_like(scr)
        acc = [None] * N_DR
        for t in range(NT):
            for j in range(2):
                for u in range(KR):
                    dr = pieces[t][j][u]
                    if dr is None:
                        continue
                    piece = d_ref[t, u * GW:(u + 1) * GW, j * GW:(j + 1) * GW]
                    acc[dr] = piece if acc[dr] is None else acc[dr] + piece
        a = lax.broadcasted_iota(i32, (GW, GW), 0)
        b = lax.broadcasted_iota(i32, (GW, GW), 1)
        flip = (a + b == GW - 1).astype(f32)
        for dr in range(N_DR):
            scr[dr * GW:(dr + 1) * GW, 0:GW] = jnp.dot(acc[dr], flip, precision=HIGHEST, preferred_element_type=f32)
        xs = jnp.concatenate([pltpu.roll(scr[dr * GW:(dr + 1) * GW, :], 128 + (WIN_W - 1) - (GW - 1), 1,
                                         stride=1, stride_axis=0) for dr in range(N_DR)], axis=0)
        tot = jnp.sum(xs.reshape(N_DR, GW, 128), axis=1)
        lane = lax.broadcasted_iota(i32, tot.shape, 1)
        o_ref[...] = jnp.where(lane < N_DC, tot, 0.0)

    return _hbm_call(
        body, name="bias_bwd", out_shape=SDS((H, N_DR, 128), f32), grid=(H,),
        in_specs=[pl.BlockSpec((None, NT, KB, QB), lambda h: (h, 0, 0, 0)), ANY_SPEC],
        out_specs=pl.BlockSpec((None, N_DR, 128), lambda h: (h, 0, 0)),
        scratch_shapes=[pltpu.VMEM((N_DR * GW, 128), f32)],
    )(dbtt, dbtt if after is None else after)


def _merge_heads(ref):
    return jnp.concatenate([ref[hh] for hh in range(H)], axis=1)


def _head_norm_bwd(xraw, gain, dy, ones_bd):
    r = lax.rsqrt(_head_sum(xraw * xraw, ones_bd) * (1.0 / DH) + RMS_EPS)
    xh = xraw * r
    gdy = dy * gain
    dx = r * (gdy - xh * (_head_sum(xh * gdy, ones_bd) * (1.0 / DH)))
    return dx, jnp.sum(dy * xh, axis=0, keepdims=True)


def _qk_bwd(dqr, dqp, dkr, dvh, p, gq, gk, rope, dp8):
    tm = 512

    def body(dqr_ref, dqp_ref, dkr_ref, dv_ref, qk_ref, gq_ref, gk_ref, cc_ref, cr_ref, sc_ref, sr_ref, dp_in_ref,
             dp_ref, ggq_ref, ggk_ref):
        dq_ref, dk_ref, dvo_ref = dp_ref.at[DP_Q], dp_ref.at[DP_K], dp_ref.at[DP_V]

        @pl.when(pl.program_id(0) == 0)
        def _():
            ggq_ref[...] = jnp.zeros_like(ggq_ref)
            ggk_ref[...] = jnp.zeros_like(ggk_ref)

        ones_bd = _head_ones()
        cs, sn = _rope_block(cc_ref, cr_ref, tm), _rope_block(sc_ref, sr_ref, tm)
        a = _merge_heads(dqr_ref)
        dyq = ((a * cs - _swap16(a) * sn) + _merge_heads(dqp_ref)) * QK_SCALE
        bk = _merge_heads(dkr_ref)
        dyk = bk * cs - _swap16(bk) * sn
        dq, gq_part = _head_norm_bwd(qk_ref[:, 0:DA], gq_ref[...], dyq, ones_bd)
        dk, gk_part = _head_norm_bwd(qk_ref[:, DA:2 * DA], gk_ref[...], dyk, ones_bd)
        dq_ref[...] = dq.astype(bf16)
        dk_ref[...] = dk.astype(bf16)
        dvo_ref[...] = _merge_heads(dv_ref).astype(bf16)
        ggq_ref[...] += gq_part
        ggk_ref[...] += gk_part

    hspec = pl.BlockSpec((H, tm, DH), lambda i: (0, i, 0))
    fixed = pl.BlockSpec((1, DA), lambda i: (0, 0))
    return _hbm_call(
        body, name="qk_bwd", out_shape=(SDS((8, S, DA), bf16), SDS((1, DA), f32), SDS((1, DA), f32)), grid=(S // tm,),
        in_specs=[hspec, hspec, hspec, hspec, pl.BlockSpec((tm, 2 * DA), lambda i: (i, 0)), fixed, fixed]
        + _rope_specs(tm) + [ANY_SPEC],
        out_specs=(pl.BlockSpec((3, tm, DA), lambda i: (0, i, 0)), fixed, fixed), input_output_aliases={11: 0},
        compiler_params=_cp(40, dimension_semantics=("arbitrary",)),
    )(dqr, dqp, dkr, dvh, p, gq, gk, *rope, dp8)


def _ctx_bwd(dkc, dvc, pc, gk):
    def body(dkc_ref, dvc_ref, p_ref, gk_ref, dk_ref, dv_ref, ggk_ref):
        ones_bd = _head_ones()
        dk, gk_part = _head_norm_bwd(p_ref[:, 0:DA], gk_ref[...], _merge_heads(dkc_ref), ones_bd)
        dk_ref[...] = dk.astype(bf16)
        dv_ref[...] = _merge_heads(dvc_ref).astype(bf16)
        ggk_ref[...] = gk_part

    piece = SDS((L, DA), bf16)
    return _hbm_call(
        body, name="ctx_bwd", out_shape=(piece, piece, SDS((1, DA), f32)), in_specs=[VMEM_SPEC] * 4,
        out_specs=(VMEM_SPEC,) * 3,
    )(dkc, dvc, pc, gk)


def _grad_w_in(h, dp8, hc, dkc_raw, dvc_m, half, name, after=None):
    def body(half_ref, h_ref, p_ref, hc_ref, dk_ref, dv_ref, after_ref, g_ref):
        j = pl.program_id(0)
        hv = h_ref[...]
        g_ref[:, 0:DA] = _tn(hv, p_ref[0])
        g_ref[:, DA:2 * DA] = _tn(hv, p_ref[1])

        @pl.when(j == 0)
        def _():
            g_ref[:, DA:2 * DA] += _tn(hc_ref[...], dk_ref[...])

        @pl.when(j == 1)
        def _():
            g_ref[:, 0:DA] += _tn(hc_ref[...], dv_ref[...])

    fixed = lambda j, s: (0, 0)
    hd = D // 2
    grid_spec = pltpu.PrefetchScalarGridSpec(
        num_scalar_prefetch=1, grid=(4,),
        in_specs=[pl.BlockSpec((S, hd), lambda j, s: (0, s[0])), pl.BlockSpec((2, S, DA), lambda j, s: (j, 0, 0)),
                  pl.BlockSpec((L, hd), lambda j, s: (0, s[0])), pl.BlockSpec((L, DA), fixed),
                  pl.BlockSpec((L, DA), fixed), ANY_SPEC],
        out_specs=pl.BlockSpec((None, hd, D), lambda j, s: (j, 0, 0)))
    return _hbm_call(
        body, name=name, out_shape=SDS((4, hd, D), f32), grid_spec=grid_spec, compiler_params=_cp(40),
    )(half, h, dp8, hc, dkc_raw, dvc_m, half if after is None else after)


def _grad_w_in_pair_sum(h, dp8, hc, dkc_raw, dvc_m, half, sent, landed, ssem, rsem):
    def body(half_ref, h_ref, p_ref, hc_ref, dk_ref, dv_ref, sent_ref, land_ref, ssem_ref, rsem_ref,
             t32_ref, tb_ref, buf, lsem):
        j = pl.program_id(0)
        hv = h_ref[...]
        t32_ref[:, 0:DA] = _tn(hv, p_ref[0])
        x, y, c = _my_pos()
        arrival = pltpu.make_async_remote_copy(src_ref=sent_ref.at[j], dst_ref=land_ref.at[j], send_sem=ssem_ref.at[j],
                                               recv_sem=rsem_ref.at[j], device_id=(x, y, 1 - c), device_id_type=MESH)
        arrival.wait_recv()
        arrival.wait_send()
        load = pltpu.make_async_copy(land_ref.at[j], buf, lsem)
        load.start()
        t32_ref[:, DA:2 * DA] = _tn(hv, p_ref[1])

        @pl.when(j == 0)
        def _():
            t32_ref[:, DA:2 * DA] += _tn(hc_ref[...], dk_ref[...])

        @pl.when(j == 1)
        def _():
            t32_ref[:, 0:DA] += _tn(hc_ref[...], dv_ref[...])

        load.wait()
        t = t32_ref[...] + buf[...]
        t32_ref[...] = t
        tb_ref[...] = t.astype(bf16)

    fixed = lambda j, s: (0, 0)
    hd = D // 2
    out_spec = pl.BlockSpec((None, hd, D), lambda j, s: (j, 0, 0))
    grid_spec = pltpu.PrefetchScalarGridSpec(
        num_scalar_prefetch=1, grid=(4,),
        in_specs=[pl.BlockSpec((S, hd), lambda j, s: (0, s[0])), pl.BlockSpec((2, S, DA), lambda j, s: (j, 0, 0)),
                  pl.BlockSpec((L, hd), lambda j, s: (0, s[0])), pl.BlockSpec((L, DA), fixed),
                  pl.BlockSpec((L, DA), fixed), HBM_SPEC, HBM_SPEC, SEM_SPEC, SEM_SPEC],
        out_specs=(out_spec, out_spec), scratch_shapes=[pltpu.VMEM((hd, D), f32), pltpu.SemaphoreType.DMA])
    return _hbm_call(
        body, name="grad_w_in_pair_sum", out_shape=(SDS((4, hd, D), f32), SDS((4, hd, D), bf16)), grid_spec=grid_spec,
        compiler_params=_cp(40, has_side_effects=DATAFLOW),
    )(half, h, dp8, hc, dkc_raw, dvc_m, sent, landed, ssem, rsem)


def _norm_mod_bwd(x, dh, g, scale):
    r = lax.rsqrt(jnp.mean(x * x, axis=-1, keepdims=True) + RMS_EPS)
    xh = x * r
    y = xh * g
    dshift = jnp.sum(dh, axis=0, keepdims=True)
    dscale = jnp.sum(dh * y, axis=0, keepdims=True)
    dyn = dh * (1.0 + scale)
    dg = jnp.sum(dyn * xh, axis=0, keepdims=True)
    gdy = dyn * g
    dx = r * (gdy - xh * jnp.mean(xh * gdy, axis=-1, keepdims=True))
    return dx, dshift, dscale, dg


def _dh_grad_x(dp8, w4, xx, dy, norm_g, mrow, b_ada, after=None):
    tm = 512

    def body(p_ref, w_ref, x_ref, dy_ref, g_ref, m_ref, b_ref, after_ref, gx_ref, dsh_ref, dsc_ref, dg_ref):
        @pl.when(pl.program_id(0) == 0)
        def _():
            dsh_ref[...] = jnp.zeros_like(dsh_ref)
            dsc_ref[...] = jnp.zeros_like(dsc_ref)
            dg_ref[...] = jnp.zeros_like(dg_ref)

        dh = None
        for j in range(4):
            for half in range(2):
                term = _tt(p_ref[2 * j + half], w_ref[j, :, half * DA:(half + 1) * DA])
                dh = term if dh is None else dh + term
        scale = m_ref[:, D:2 * D] + b_ref[:, D:2 * D]
        dx, dshift, dscale, dg = _norm_mod_bwd(x_ref[...], dh, g_ref[...], scale)
        gx_ref[...] = dy_ref[...] + dx
        dsh_ref[...] += dshift
        dsc_ref[...] += dscale
        dg_ref[...] += dg

    row = lambda i: (i, 0)
    fixed = lambda i: (0, 0)
    vec = SDS((1, D), f32)
    return _hbm_call(
        body, name="dh_grad_x", out_shape=(SDS((S, D), f32), vec, vec, vec), grid=(S // tm,),
        in_specs=[pl.BlockSpec((8, tm, DA), lambda i: (0, i, 0)), pl.BlockSpec((4, D, D), lambda i: (0, 0, 0)),
                  pl.BlockSpec((tm, D), row), pl.BlockSpec((tm, D), row), pl.BlockSpec((1, D), fixed),
                  pl.BlockSpec((1, 3 * D), fixed), pl.BlockSpec((1, 3 * D), fixed), ANY_SPEC],
        out_specs=(pl.BlockSpec((tm, D), row), pl.BlockSpec((1, D), fixed), pl.BlockSpec((1, D), fixed),
                   pl.BlockSpec((1, D), fixed)),
        compiler_params=_cp(56, dimension_semantics=("arbitrary",)),
    )(dp8, w4, xx, dy, norm_g, mrow, b_ada, b_ada if after is None else after)


def _dhc_sums(dkc_raw, dvc_m, w4, ctx2, norm_g, mrow_c, b_ada, after=None):
    def body(dk_ref, dv_ref, w0_ref, w1_ref, x_ref, g_ref, m_ref, b_ref, after_ref, dsh_ref, dsc_ref, dg_ref):
        dh = _tt(dk_ref[...], w0_ref[:, DA:2 * DA]) + _tt(dv_ref[...], w1_ref[:, 0:DA])
        scale = m_ref[:, D:2 * D] + b_ref[:, D:2 * D]
        _, dshift, dscale, dg = _norm_mod_bwd(x_ref[...], dh, g_ref[...], scale)
        dsh_ref[...] = dshift
        dsc_ref[...] = dscale
        dg_ref[...] = dg

    fixed = lambda i: (0, 0)
    vec = SDS((1, D), f32)
    vspec = pl.BlockSpec((1, D), fixed)
    return _hbm_call(
        body, name="dhc_sums", out_shape=(vec, vec, vec), grid=(1,),
        in_specs=[pl.BlockSpec((L, DA), fixed), pl.BlockSpec((L, DA), fixed),
                  pl.BlockSpec((None, D, D), lambda i: (0, 0, 0)), pl.BlockSpec((None, D, D), lambda i: (1, 0, 0)),
                  pl.BlockSpec((L, D), fixed), vspec, pl.BlockSpec((1, 3 * D), fixed), pl.BlockSpec((1, 3 * D), fixed),
                  ANY_SPEC],
        out_specs=(vspec, vspec, vspec), compiler_params=_cp(32),
    )(dkc_raw, dvc_m, w4, w4, ctx2, norm_g, mrow_c, b_ada, b_ada if after is None else after)


def _rope_tables():
    nf = DH // 4
    inv = np.float32(ROPE_THETA) ** (-np.arange(nf, dtype=np.float32) / np.float32(nf))
    ang_c = np.arange(GW, dtype=np.float32)[:, None] * inv
    ang_r = np.arange(ROWS, dtype=np.float32)[:, None] * inv
    zc, zr = np.zeros((GW, 2 * nf), np.float32), np.zeros((ROWS, 2 * nf), np.float32)
    ct_cos = np.tile(np.concatenate([zc, np.cos(ang_c), np.cos(ang_c)], axis=1), (1, H))
    ct_sin = np.tile(np.concatenate([zc, -np.sin(ang_c), np.sin(ang_c)], axis=1), (1, H))
    rt_cos = np.tile(np.concatenate([np.cos(ang_r), np.cos(ang_r), zr], axis=1), (1, H))
    rt_sin = np.tile(np.concatenate([-np.sin(ang_r), np.sin(ang_r), zr], axis=1), (1, H))
    rep8 = lambda t: np.ascontiguousarray(np.broadcast_to(t[:, None, :], (ROWS, 8, DA))).reshape(ROWS * 8, DA)
    return tuple(jnp.asarray(t, f32) for t in (ct_cos, rep8(rt_cos), ct_sin, rep8(rt_sin)))


def _local_step(xx, ctx2, tgt, mrow, mrow_c, b_ada, norm_g, weights, q_norm_g, k_norm_g, rpb2, conv_w_full, conv_b,
                hooks=None):
    hooks = hooks or {}
    gq = jnp.tile(q_norm_g, (1, H))
    gk = jnp.tile(k_norm_g, (1, H))
    rope = _rope_tables()

    h = _prenorm(xx, norm_g, mrow, b_ada, 128, "prenorm_x", after=weights.get("started"))
    jv = weights["jvec"]
    p = _in_proj_own(h, weights["own"], jv)
    btb = _bias_tiles(rpb2, after=p)
    w4, started = weights["near"](btb)
    p = _in_proj_block(h, w4, p, weights["first"], "in_proj_near", after=started)
    w4 = weights["near2"](w4, p)
    p = _in_proj_block(h, w4, p, weights["second"], "in_proj_near2")
    ctx_norm = {}

    def filler(token):
        ctx_norm["hc"] = _prenorm(ctx2, norm_g, mrow_c, b_ada, L // 2, "prenorm_ctx", after=token)
        return ctx_norm["hc"]

    w4, started = weights["far"](w4, p, filler)
    hc = ctx_norm["hc"]
    p = _in_proj_block(h, w4, p, jv ^ 3, "in_proj_far", after=started)
    pc = _ctx_proj(hc, w4)
    qr, qp, kr, vh = _qk_prep(p, gq, gk, rope)
    kc, vc = _ctx_prep(pc, gk)
    o = _attn_fwd(qr, qp, kr, vh, kc, vc, btb)
    started = weights["out_arrived"](o) if "out_arrived" in weights else None
    conv_g = _conv_fwd(p, conv_w_full, conv_b, after=started)
    w_out_full = weights["out"](conv_g)
    dy, dconv, dp8, do, g_w_out, dgate, loss_sum = _out_proj_loss(o, p, conv_g, w_out_full, xx, tgt, mrow, b_ada)
    started = hooks["g_w_out"](g_w_out) if "g_w_out" in hooks else None
    dp8, g_conv_w, g_conv_b = _conv_bwd(dconv, p, conv_w_full, conv_b, dp8, after=started)
    started = hooks["after_conv"](dp8) if "after_conv" in hooks else None
    dqr, dqp, dkr, dvh, dkc, dvc, dbtb = _attn_bwd(qr, qp, kr, vh, kc, vc, btb, do, after=started)
    dp8, g_gq, g_gk = _qk_bwd(dqr, dqp, dkr, dvh, p, gq, gk, rope, dp8)
    dkc_raw, dvc_m, g_gk_c = _ctx_bwd(dkc, dvc, pc, gk)
    first = hooks.get("first_half", jnp.zeros((1,), i32))
    g_first = _grad_w_in(h, dp8, hc, dkc_raw, dvc_m, first, "grad_w_in_first")
    started = hooks["g_w_in_first"](g_first) if "g_w_in_first" in hooks else None
    if "w_in_pair_sum" in hooks:
        g_second = None
        started = hooks["w_in_pair_sum"](functools.partial(_grad_w_in_pair_sum, h, dp8, hc, dkc_raw, dvc_m, 1 - first))
    else:
        g_second = _grad_w_in(h, dp8, hc, dkc_raw, dvc_m, 1 - first, "grad_w_in_second", after=started)
    dshift_c, dscale_c, dng_c = _dhc_sums(dkc_raw, dvc_m, w4, ctx2, norm_g, mrow_c, b_ada, after=started)
    g_rpb = _bias_bwd(dbtb, after=dshift_c)
    grad_x, dshift, dscale, dng = _dh_grad_x(dp8, w4, xx, dy, norm_g, mrow, b_ada, after=g_rpb)
    return dict(loss_sum=loss_sum, grad_x=grad_x, g_w_in=(g_first, g_second), g_w_out=g_w_out, g_conv_w=g_conv_w,
                g_conv_b=g_conv_b, g_rpb=g_rpb, g_gq=g_gq, g_gk=g_gk, g_gk_c=g_gk_c, dshift=dshift, dscale=dscale,
                dgate=dgate, dng=dng, dshift_c=dshift_c, dscale_c=dscale_c, dng_c=dng_c)


def _pair_sum_w_out(g, r, cvec):
    hr = D // 8

    def add(own, other):
        t = own + other
        return t, t.astype(bf16)

    mine = lambda ref, q, c: ref.at[pl.ds(pl.multiple_of((2 * q + c) * hr, hr), hr)]
    block = lambda ref, q, c: ref.at[q]
    return _stream_call(add, 4, [(g, (hr, D), mine), (r, (hr, D), block)],
                        [((4, hr, D), f32, (hr, D), block), ((4, hr, D), bf16, (hr, D), block)], cvec, "pair_sum_w_out")


def _chip_sum(t32, r2, jvec, name):
    rows = t32.shape[1]
    tr = min(rows // 2, 128)
    band = lambda k: slice(k * tr, (k + 1) * tr)
    total =lambda t, r: (((t + r[0].astype(f32)) + r[1].astype(f32)) + r[2].astype(f32),)
    return _stream_call(
        total, rows // tr,
        [(t32, (tr, D), lambda ref, k, j: ref.at[j, band(k)]), (r2, (3, tr, D), lambda ref, k, j: ref.at[:, band(k)])],
        [((rows, D), f32, (tr, D), lambda ref, k, j: ref.at[band(k)])], jvec, name)[0]


_PK = {}
_off = 0
for _name, _rows in (("dm", 24), ("dmc", 24), ("dng", 8), ("dng_c", 8), ("gq", 8), ("gk", 8), ("gk_c", 8),
                     ("rpb", H * N_DR), ("conv_b", 8), ("conv_w", 16), ("loss", 8)):
    _PK[_name] = (_off, _off + _rows)
    _off += _rows
PK_ROWS = _off
RS_B_ADA, RS_NORM_G, RS_GQ, RS_GK, RS_RPB, RS_CONV_B, RS_CONV_W, RS_DMC, RS_LOSS, RS_ROWS = (
    0, 24, 32, 40, 48, 168, 176, 192, 216, 224)


def _small_reduce(gathered):
    def body(g_ref, o_ref, dm_ref):
        a0 = _PK["dm"][0]
        dm_ref[...] = jnp.zeros_like(dm_ref)
        for b in range(8):
            for i in range(24):
                dm_ref[b:b + 1, 128 * i:128 * (i + 1)] = g_ref[b, a0 + i:a0 + i + 1, :]
        tot = g_ref[0]
        for b in range(1, 8):
            tot = tot + g_ref[b]

        def rows(name):
            a, z = _PK[name]
            return tot[a:z]

        o_ref[RS_B_ADA:RS_B_ADA + 24] = rows("dm") + rows("dmc")
        o_ref[RS_NORM_G:RS_NORM_G + 8] = rows("dng") + rows("dng_c")
        gq = jnp.broadcast_to(jnp.sum(rows("gq"), axis=0, keepdims=True), (8, 128))
        gk = jnp.broadcast_to(jnp.sum(rows("gk") + rows("gk_c"), axis=0, keepdims=True), (8, 128))
        o_ref[RS_GQ:RS_GQ + 8] = gq + pltpu.roll(gq, DH, 1)
        o_ref[RS_GK:RS_GK + 8] = gk + pltpu.roll(gk, DH, 1)
        o_ref[RS_RPB:RS_RPB + H * N_DR] = rows("rpb")
        o_ref[RS_CONV_B:RS_CONV_B + 8] = rows("conv_b")
        o_ref[RS_CONV_W:RS_CONV_W + 16] = rows("conv_w")
        dmc = rows("dmc")
        o_ref[RS_DMC:RS_DMC + 24] = dmc
        o_ref[RS_LOSS:RS_LOSS + 8] = rows("loss")
        for i in range(24):
            dm_ref[8:9, 128 * i:128 * (i + 1)] = dmc[i:i + 1]

    return _hbm_call(body, name="small_reduce", out_shape=(SDS((RS_ROWS, 128), f32), SDS((16, 3 * D), f32)),
                     in_specs=[VMEM_SPEC], out_specs=(VMEM_SPEC, VMEM_SPEC))(gathered)


def _w_ada_grad(sc16, dm16, w_ada_shard, jvec):
    ncol = w_ada_shard.shape[1]

    def body(j_ref, sc_ref, dm_ref, w_ref, g_ref, part_ref):
        dm = dm_ref[...]
        g_ref[...] = lax.dot_general(sc_ref[...], dm, (((0,), (0,)), ((), ())), precision=HIGHEST,
                                     preferred_element_type=f32)
        part_ref[...] = lax.dot_general(dm[8:16], w_ref[...], (((1,), (1,)), ((), ())), precision=HIGHEST,
                                        preferred_element_type=f32)

    fixed = lambda i, j: (0, 0)
    grid_spec = pltpu.PrefetchScalarGridSpec(
        num_scalar_prefetch=1, grid=(1,),
        in_specs=[pl.BlockSpec((16, D), fixed), pl.BlockSpec((16, ncol), lambda i, j: (0, j[0])),
                  pl.BlockSpec((D, ncol), fixed)],
        out_specs=(pl.BlockSpec((D, ncol), fixed), pl.BlockSpec((8, D), fixed)))
    return _pallas_call(body, name="w_ada_grad", out_shape=(SDS((D, ncol), f32), SDS((8, D), f32)),
                        grid_spec=grid_spec, compiler_params=_cp(40))(jvec, sc16, dm16, w_ada_shard)


def _c_ctx_grad(parts4, c_ctx):
    def body(p_ref, c_ref, o_ref):
        tot = ((p_ref[0] + p_ref[1]) + p_ref[2]) + p_ref[3]
        o_ref[...] = tot[0:1] * _dsilu(c_ref[...].reshape(1, D))

    return _pallas_call(body, name="c_ctx_grad", out_shape=SDS((1, D), f32), in_specs=[VMEM_SPEC, VMEM_SPEC],
                        out_specs=VMEM_SPEC)(parts4, c_ctx)


def _adamw(w, g, m, v, name, after=None):
    return _adamw_stream(w, [g], m, v, None, name, after)


def _adamw_halves(w, g_mine, g_other, m, v, cvec, name, after=None):
    return _adamw_stream(w, [g_mine, g_other], m, v, cvec, name, after)


def _adamw_stream(w, g_parts, m, v, cvec, name, after):
    rows, cols = w.shape
    parts = len(g_parts)
    span = rows // parts
    chunk = min(128, span // 2)
    per = span // chunk
    outs_g = parts == 2
    pin = rows * cols * 4 > (1 << 20)

    def body(*refs):
        c_ref, w_ref = refs[0], refs[1]
        g_refs = refs[2:2 + parts]
        m_ref, v_ref = refs[2 + parts], refs[3 + parts]
        out_refs = refs[5 + parts:5 + parts + 3 + outs_g]
        bw, bg, bm, bv, bd, sem_in, sem_out = refs[5 + parts + 3 + outs_g:]
        c = c_ref[0]
        loads, stores = [], []
        for p in range(parts):
            first = (c if p == 0 else 1 - c) * span if parts == 2 else 0
            for kk in range(per):
                r = pl.ds(pl.multiple_of(first + kk * chunk, chunk), chunk)
                g_src = g_refs[p].at[kk * chunk:(kk + 1) * chunk]
                cps = [pltpu.make_async_copy(src, dst.at[r], sem_in.at[a, p, kk])
                       for a, (src, dst) in enumerate(((w_ref.at[r], bw), (g_src, bg), (m_ref.at[r], bm),
                                                       (v_ref.at[r], bv)))]
                for cp in cps:
                    cp.start()
                loads.append((r, p, kk, cps))
        for r, p, kk, cps in loads:
            for cp in cps:
                cp.wait()
            bd[r], bm[r], bv[r] = _adam_math(bw[r], bg[r], bm[r], bv[r])
            for a, (src, dst) in enumerate(zip(([bg] if outs_g else []) + [bd, bm, bv], out_refs)):
                cp = pltpu.make_async_copy(src.at[r], dst.at[r], sem_out.at[a, p, kk])
                cp.start()
                stores.append(cp)
        for cp in stores:
            cp.wait()

    shp = pltpu.HBM((rows, cols), f32) if pin else SDS((rows, cols), f32)
    spec = HBM_SPEC if pin else ANY_SPEC
    buf = pltpu.VMEM((rows, cols), f32)
    args = [pltpu.with_memory_space_constraint(a, pltpu.HBM) if pin else a for a in (w, *g_parts, m, v)]
    cvec = jnp.zeros((1,), i32) if cvec is None else cvec
    after_big = after is not None and after.size * 4 > (1 << 20)
    after = pltpu.with_memory_space_constraint(after, pltpu.HBM) if after_big else after
    return _pallas_call(
        body, name=name, out_shape=(shp,) * (3 + outs_g),
        in_specs=[SMEM_SPEC] + [spec] * (3 + parts) + [HBM_SPEC if after_big else ANY_SPEC],
        out_specs=(spec,) * (3 + outs_g),
        scratch_shapes=[buf] * 5 + [pltpu.SemaphoreType.DMA((4, parts, per)), pltpu.SemaphoreType.DMA((4, parts, per))],
        compiler_params=_cp(5 * rows * cols * 4 // (1 << 20) + 8),
    )(cvec, *args, cvec if after is None else after)


def _adam_math(w, g, m, v):
    m2 = ADAM_B1 * m + (1.0 - ADAM_B1) * g
    v2 = ADAM_B2 * v + (1.0 - ADAM_B2) * jnp.square(g)
    m_hat = m2 / (1.0 - ADAM_B1 ** ADAM_STEP)
    v_hat = v2 / (1.0 - ADAM_B2 ** ADAM_STEP)
    return -ADAM_LR * (m_hat / (jnp.sqrt(v_hat) + ADAM_EPS) + ADAM_WD * w), m2, v2


def _adamw_small(red, g_c_ctx, jvec, ws, ms, vs):
    n = len(ws)

    def body(*refs):
        red_ref, gc_ref, j_ref = refs[:3]
        w_refs, m_refs, v_refs = refs[3:3 + n], refs[3 + n:3 + 2 * n], refs[3 + 2 * n:3 + 3 * n]
        outs = refs[3 + 3 * n:]
        g_out, d_out, m_out, v_out = outs[:n], outs[n:2 * n], outs[2 * n:3 * n], outs[3 * n:]
        chip = j_ref[0]
        lanes = lambda i: (slice(None), slice(128 * i, 128 * (i + 1)))
        row = lambda r0, i: (lambda: red_ref[r0 + i:r0 + i + 1, :])
        whole = (slice(None), slice(None))
        chunks = [
            [((slice(None),), lambda: gc_ref[...].reshape(D))],
            [(lanes(i), row(RS_B_ADA, i)) for i in range(3 * D // 128)],
            [(lanes(i), row(RS_NORM_G, i)) for i in range(D // 128)],
            [(whole, lambda: red_ref[RS_GQ:RS_GQ + 1, 0:DH])],
            [(whole, lambda: red_ref[RS_GK:RS_GK + 1, 0:DH])],
            [((dr,), (lambda dr=dr: red_ref[pl.ds(RS_RPB + dr, H, stride=N_DR), 0:N_DC])) for dr in range(N_DR)],
            [((r,), (lambda r=r: red_ref[pl.ds(RS_CONV_W + 4 * r + chip, 1), :])) for r in range(3)],
            [(lanes(i), row(RS_CONV_B, i)) for i in range(DC // 128)],
        ]
        for a in range(n):
            for idx, grad in chunks[a]:
                g = grad()
                d, m2, v2 = _adam_math(w_refs[a][idx], g, m_refs[a][idx], v_refs[a][idx])
                g_out[a][idx] = g
                d_out[a][idx] = d
                m_out[a][idx] = m2
                v_out[a][idx] = v2

    shapes = [SDS(w.shape, f32) for w in ws]
    res = _pallas_call(body, name="adamw_small", out_shape=shapes * 4,
                       in_specs=[VMEM_SPEC, VMEM_SPEC, SMEM_SPEC] + [VMEM_SPEC] * (3 * n),
                       out_specs=[VMEM_SPEC] * (4 * n))(red, g_c_ctx, jvec, *ws, *ms, *vs)
    return [list(res[k * n:(k + 1) * n]) for k in range(4)]


def _rows128(a):
    return a.reshape(-1, 128)


def kernel(x, c, ctx, c_ctx, w_ada, b_ada, norm_g, w_in, q_norm_g, k_norm_g, rpb, conv_w, conv_b, w_out, loss_target, m_c_ctx, m_w_ada, m_b_ada, m_norm_g, m_w_in, m_q_norm_g, m_k_norm_g, m_rpb, m_conv_w, m_conv_b, m_w_out, v_c_ctx, v_w_ada, v_b_ada, v_norm_g, v_w_in, v_q_norm_g, v_k_norm_g, v_rpb, v_conv_w, v_conv_b, v_w_out):
    xi, yi, ci = lax.axis_index("x"), lax.axis_index("y"), lax.axis_index("c")
    dev = 4 * xi + 2 * yi + ci
    chip = 2 * xi + yi
    cvec = jnp.reshape(ci, (1,)).astype(i32)
    jvec = jnp.reshape(chip, (1,)).astype(i32)
    w_ada_s = w_ada[0]
    ncol = w_ada_s.shape[1]

    gc = _split_start([_to_slot(c.reshape(8, 128), 8, dev)], [], 7, _gather8_copies, "gather_c_start")
    wo4c = _cast_to_slot(w_out[0], jvec, "cast_w_out", after=gc[3])
    w4c = _cast_to_slot(w_in[0], jvec, "cast_w_in", after=wo4c)
    (c8,), _ = _split_wait(gc[0], gc[1], [gc[2]], [], w4c, _gather8_copies, "gather_c_wait")
    cc = jnp.concatenate([c8.reshape(8, D), c_ctx.reshape(1, D), jnp.zeros((7, D), f32)], axis=0)
    m_shard, sc16 = _adaln_shard(cc, w_ada_s)

    conv_w_pad = jnp.pad(conv_w[0], ((0, 5), (0, 0)))
    gm = _split_start([_to_slot(m_shard, 4, chip), _to_slot(conv_w_pad, 4, chip)], [], 6, _gather4_copies,
                      "gather_mod_start")

    all_k = [(0, 0, 0), (0, 0, 1), (0, 0, 2)]
    sem_a, rem_a, w4s, token = _split_start([w4c], [], 1, _near_copies, "weights_near_start", after=gm[4])
    (m4, cw4), _ = _split_wait(gm[0], gm[1], [gm[2], gm[3]], [], token, _gather4_copies, "gather_mod_wait")
    m_full = jnp.transpose(m4, (1, 0, 2)).reshape(16, 4 * ncol)
    mrow = lax.dynamic_slice(m_full, (dev, 0), (1, 3 * D))
    mrow_c = m_full[8:9]
    conv_w_full = jnp.transpose(cw4[:, 0:3, :], (1, 0, 2)).reshape(3, DC)
    waves = {}

    def near(after):
        (w4w,), _ = _split_wait(sem_a, rem_a, [w4s], [], after, _near_copies, "weights_near_wait")
        sem_b, rem_b, w4b, started = _split_start([w4w], [], 2, _pass_relay_copies, "weights_pass_start")
        waves["pass"] = (sem_b, rem_b)
        return w4b, started

    def near2(w4, after):
        (w4w,), _ = _split_wait(*waves["pass"], [w4], [], after, _pass_copy, "weights_pass_wait")
        return w4w

    def far(w4, after, filler):
        (w4w,), _ = _split_wait(*waves["pass"], [w4], [], after, _relay_copy, "weights_far_wait")
        w4x, sem_c, rem_c, wo4s, started = _forward_then_start(w4w, wo4c, all_k, "weights_far_forward_out_start")
        (w4f,), _ = _split_wait(sem_c, rem_c, [w4x], [], filler(started), _diag_forward_copy,
                                "weights_far_forward_wait")
        waves["out"] = (sem_c, rem_c, wo4s)
        return w4f, started

    def w_out_arrived(after):
        sem_c, rem_c, wo4s = waves["out"]
        (wow,) = _halves_wait(sem_c, rem_c, [wo4s], after, all_k, "weights_out_wait")
        sem_d, rem_d, wof, started = _split_start([wow], [], 3, _forward_copies, "weights_out_forward_start")
        waves["out_forward"] = (sem_d, rem_d, wof)
        return started

    def w_out_gathered(after):
        sem_d, rem_d, wof = waves["out_forward"]
        (wo,), _ = _split_wait(sem_d, rem_d, [wof], [], after, _forward_copies, "weights_out_forward_wait")
        return wo.reshape(D, D)

    weights = dict(own=w_in[0], jvec=jvec, started=token, first=jvec ^ (1 + cvec), second=jvec ^ (2 - cvec),
                   near=near, near2=near2, far=far, out_arrived=w_out_arrived, out=w_out_gathered)

    exchange = _exchange_copies
    pending = {}

    def on_g_w_out(g_w_out):
        out = _split_start([g_w_out], [SDS((4, D // 8, D), f32)], 4, exchange, "grad_out_pair_start")
        pending["ex_out"] = out
        return out[4]

    def after_conv(dp8):
        ssem_o, rsem_o, g_o, land_o, _ = pending["ex_out"]
        (g_o,), (ex_o,) = _split_wait(ssem_o, rsem_o, [g_o], [land_o], dp8, exchange, "grad_out_pair_wait")
        to32, tob = _pair_sum_w_out(g_o, ex_o, cvec)
        out = _split_start([tob], [SDS((3, D // 8, D), bf16)], 3, _scatter_copies, "grad_out_chip_start")
        pending["sc_out"] = (out, to32)
        return out[4]

    def on_g_w_in_first(g_first):
        out = _split_start([g_first], [SDS((4, D // 2, D), f32)], 4, _block_exchange_copies, "grad_pair_start")
        pending["ex"] = (out[0], out[1], [out[2]], [out[3]])
        return out[4]

    def on_w_in_pair_sum(pair_sum):
        ex_ssem, ex_rsem, ex_srcs, ex_lands = pending["ex"]
        t32, tb = pair_sum(ex_srcs[0], ex_lands[0], ex_ssem, ex_rsem)
        out = _split_start([tb], [SDS((3, D // 2, D), bf16)], 3, _scatter_copies, "grad_chip_start")
        pending["sc_in"] = (out, t32)
        return out[4]

    r = _local_step(x[0], ctx[0], loss_target[0], mrow, mrow_c, b_ada, norm_g, weights, q_norm_g, k_norm_g,
                    rpb[0], conv_w_full, conv_b,
                    dict(g_w_out=on_g_w_out, after_conv=after_conv, first_half=1 - cvec,
                         g_w_in_first=on_g_w_in_first, w_in_pair_sum=on_w_in_pair_sum))
    sc_in, t32 = pending["sc_in"]
    _, (r2,) = _split_wait(sc_in[0], sc_in[1], [sc_in[2]], [sc_in[3]], r["dng"], _scatter_copies, "grad_chip_wait")
    u_in = _chip_sum(t32, r2, jvec, "chip_sum_w_in")
    sc_o, to32 = pending["sc_out"]
    _, (ro2,) = _split_wait(sc_o[0], sc_o[1], [sc_o[2]], [sc_o[3]], u_in, _scatter_copies, "grad_out_chip_wait")
    u_out = _chip_sum(to32, ro2, jvec, "chip_sum_w_out")
    swap = _split_start([u_in, u_out], [SDS(u_in.shape, f32), SDS(u_out.shape, f32)], 2, _swap_copies,
                        "grad_pair_swap_start")

    dm = jnp.concatenate([r["dshift"], r["dscale"], r["dgate"]], axis=1)
    dmc = jnp.concatenate([r["dshift_c"], r["dscale_c"], jnp.zeros((1, D), f32)], axis=1)
    pack_parts = [_rows128(dm), _rows128(dmc), _rows128(r["dng"]), _rows128(r["dng_c"]), _rows128(r["g_gq"]),
                  _rows128(r["g_gk"]), _rows128(r["g_gk_c"]), r["g_rpb"].reshape(H * N_DR, 128),
                  _rows128(r["g_conv_b"]), _rows128(r["g_conv_w"][0:3]), jnp.pad(r["loss_sum"], ((0, 0), (0, 127)))]
    pack = jnp.concatenate([jnp.pad(p, ((0, -p.shape[0] % 8), (0, 0))) for p in pack_parts], axis=0)
    assert pack.shape[0] == PK_ROWS
    gs = _split_start([_to_slot(pack, 8, dev)], [], 7, _gather8_copies, "gather_small_start", after=swap[6])
    (u_in, u_out), (o_in, o_out) = _split_wait(swap[0], swap[1], swap[2:4], swap[4:6], gs[3], _swap_copies,
                                               "grad_pair_swap_wait")
    g_w_in_s, d_w_in, nm_w_in, nv_w_in = _adamw_halves(w_in[0], u_in, o_in, m_w_in[0], v_w_in[0], cvec, "adamw_w_in",
                                                       after=gs[3])
    g_w_out_s, d_w_out, nm_w_out, nv_w_out = _adamw_halves(w_out[0], u_out, o_out, m_w_out[0], v_w_out[0], cvec,
                                                           "adamw_w_out", after=nm_w_in)
    (gathered,), _ = _split_wait(gs[0], gs[1], [gs[2]], [], nm_w_out, _gather8_copies, "gather_small_wait")
    red, dm16 = _small_reduce(gathered)
    loss = red[RS_LOSS, 0] * (0.5 / D)

    g_w_ada_s, cpart = _w_ada_grad(sc16, dm16, w_ada_s, jvec)
    gcp = _split_start([_to_slot(cpart, 4, chip)], [], 3, _gather4_copies, "gather_c_ctx_parts_start")
    d_w_ada, nm_w_ada, nv_w_ada = _adamw(w_ada_s, g_w_ada_s, m_w_ada[0], v_w_ada[0], "adamw_w_ada", after=gcp[3])
    (cparts4,), _ = _split_wait(gcp[0], gcp[1], [gcp[2]], [], nm_w_ada, _gather4_copies, "gather_c_ctx_parts_wait")
    g_c_ctx = _c_ctx_grad(cparts4, c_ctx)

    t_rpb = lambda a: jnp.transpose(a, (0, 2, 1, 3)).reshape(N_DR, H, N_DC)
    t_cw = lambda a: jnp.transpose(a, (1, 0, 2))
    small = _adamw_small(
        red, g_c_ctx, jvec,
        [c_ctx, b_ada, norm_g, q_norm_g, k_norm_g, t_rpb(rpb), t_cw(conv_w), conv_b],
        [m_c_ctx, m_b_ada, m_norm_g, m_q_norm_g, m_k_norm_g, t_rpb(m_rpb), t_cw(m_conv_w), m_conv_b],
        [v_c_ctx, v_b_ada, v_norm_g, v_q_norm_g, v_k_norm_g, t_rpb(v_rpb), t_cw(v_conv_w), v_conv_b])
    for kind in small:
        kind[5] = jnp.transpose(kind[5].reshape(1, N_DR, H, N_DC), (0, 2, 1, 3))
        kind[6] = jnp.transpose(kind[6], (1, 0, 2))

    def ordered(kind, big_w_ada, big_w_in, big_w_out):
        s_c_ctx, s_b_ada, s_norm_g, s_q, s_k, s_rpb, s_conv_w, s_conv_b = small[kind]
        return [s_c_ctx, big_w_ada[None], s_b_ada, s_norm_g, big_w_in[None], s_q, s_k, s_rpb, s_conv_w,
                s_conv_b, big_w_out[None]]

    grads = ordered(0, g_w_ada_s, g_w_in_s, g_w_out_s)
    deltas = ordered(1, d_w_ada, d_w_in, d_w_out)
    new_m = ordered(2, nm_w_ada, nm_w_in, nm_w_out)
    new_v = ordered(3, nv_w_ada, nv_w_in, nv_w_out)
    return (loss, r["grad_x"][None], *grads, *deltas, *new_m, *new_v)
```

```python
import functools

import jax
import jax.numpy as jnp
import numpy as np
from jax import lax
from jax.experimental import pallas as pl
from jax.experimental.pallas import tpu as pltpu

f32, bf16, i32 = jnp.float32, jnp.bfloat16, jnp.int32
MESH = pl.DeviceIdType.MESH
HIGHEST = lax.Precision.HIGHEST

D = 1024
S = 2048
L = 256
GW = 64
ROWS = S // GW
H = 8
DH = 64
DA = H * DH
DC = 512
WIN_H, WIN_W = 8, 16
N_DR, N_DC = 2 * WIN_H - 1, 2 * WIN_W - 1
RMS_EPS = 1e-6
ROPE_THETA = 10000.0
QK_SCALE = DH ** -0.5
NEG = -1e30

QB = 128
NQB = S // QB
KR = 9
KB = KR * GW
TILE_GEOM = ((0, 0), (2, 0), (4, 0), (28, 23), (30, 23))
NT = len(TILE_GEOM)

ADAM_LR, ADAM_B1, ADAM_B2, ADAM_EPS, ADAM_WD, ADAM_STEP = 0.001, 0.9, 0.999, 1e-08, 0.01, 10

VMEM_SPEC = pl.BlockSpec(memory_space=pltpu.VMEM)
ANY_SPEC = pl.BlockSpec(memory_space=pl.ANY)
SMEM_SPEC = pl.BlockSpec(memory_space=pltpu.SMEM)
SDS = jax.ShapeDtypeStruct


_pallas_call = pl.pallas_call


def _hbm_call(body, *, out_shape, in_specs=None, out_specs=None, grid_spec=None, **kw):
    n_pre = 0
    if grid_spec is not None:
        ispecs, ospecs, n_pre = grid_spec.in_specs, grid_spec.out_specs, grid_spec.num_scalar_prefetch
        kw["grid_spec"] = grid_spec
    else:
        ispecs, ospecs = in_specs, out_specs
        kw.update(in_specs=in_specs, out_specs=out_specs)

    def blocked(spec):
        return isinstance(spec, pl.BlockSpec) and spec.block_shape is not None

    single = not isinstance(out_shape, (tuple, list))
    shapes = [out_shape] if single else list(out_shape)
    ospec_list = list(ospecs) if isinstance(ospecs, (tuple, list)) else [ospecs]
    shapes = [pltpu.HBM(s.shape, s.dtype) if blocked(sp) else s for s, sp in zip(shapes, ospec_list)]
    call = _pallas_call(body, out_shape=shapes[0] if single else tuple(shapes), **kw)

    def run(*args):
        arrays = [pltpu.with_memory_space_constraint(a, pltpu.HBM) if blocked(sp) else a
                  for a, sp in zip(args[n_pre:], ispecs)]
        return call(*args[:n_pre], *arrays)

    return run


def _cp(vmem_mb=None, **kw):
    if vmem_mb is not None:
        kw["vmem_limit_bytes"] = vmem_mb << 20
    return pltpu.CompilerParams(**kw)


def _silu(z):
    return z * jax.nn.sigmoid(z)


def _dsilu(z):
    sg = jax.nn.sigmoid(z)
    return sg * (1.0 + z * (1.0 - sg))


def _row_start(i):
    return min(max(i - WIN_H // 2, 0), ROWS - WIN_H)


def _my_pos():
    return lax.axis_index("x"), lax.axis_index("y"), lax.axis_index("c")


def _flip(v, bit):
    return 1 - v if bit else v


def _swap_copies(srcs, lands, ssem, rsem):
    x, y, c = _my_pos()
    return [pltpu.make_async_remote_copy(src_ref=srcs[a], dst_ref=lands[a], send_sem=ssem.at[a], recv_sem=rsem.at[a],
                                         device_id=(x, y, 1 - c), device_id_type=MESH) for a in range(len(srcs))]


HBM_SPEC = pl.BlockSpec(memory_space=pltpu.HBM)
SEM_SPEC = pl.BlockSpec(memory_space=pltpu.SEMAPHORE)
DATAFLOW = pltpu.SideEffectType.DATAFLOW_SIDE_EFFECTING


def _peer_chips(x, y, c):
    out = []
    for k in range(1, 4):
        px, py = _flip(x, (k >> 1) & 1), _flip(y, k & 1)
        out.append(((px, py, c), 2 * px + py))
    return out


def _half_copies(srcs, dsts, ssem, rsem, which):
    x, y, c = _my_pos()
    j = 2 * x + y
    peers = _peer_chips(x, y, c)
    pairs = []
    for pos, group, k in which:
        half = srcs[pos].shape[1] // 2
        mine = pl.ds(pl.multiple_of(c * half, 8), half)
        dev, pj = peers[k]
        sem = 3 * group + k
        send = pltpu.make_async_remote_copy(src_ref=srcs[pos].at[j, mine], dst_ref=dsts[pos].at[j, mine],
                                            send_sem=ssem.at[sem], recv_sem=rsem.at[sem], device_id=dev,
                                            device_id_type=MESH)
        arrive = pltpu.make_async_remote_copy(src_ref=srcs[pos].at[j, mine], dst_ref=dsts[pos].at[pj, mine],
                                              send_sem=ssem.at[sem], recv_sem=rsem.at[sem], device_id=dev,
                                              device_id_type=MESH)
        pairs.append((send, arrive))
    return pairs


def _halves_wait(ssem, rsem, bigs, after, which, name):
    nb = len(bigs)

    def body(*refs):
        b_in = refs[:nb]
        ssem_ref, rsem_ref = refs[nb], refs[nb + 1]
        for send, arrive in _half_copies(b_in, b_in, ssem_ref, rsem_ref, which):
            send.wait_send()
            arrive.wait_recv()

    return _hbm_call(
        body, name=name, out_shape=tuple(pltpu.HBM(b.shape, b.dtype) for b in bigs),
        in_specs=[HBM_SPEC] * nb + [SEM_SPEC, SEM_SPEC, ANY_SPEC], out_specs=tuple([HBM_SPEC] * nb),
        input_output_aliases={a: a for a in range(nb)}, compiler_params=_cp(has_side_effects=DATAFLOW),
    )(*bigs, ssem, rsem, after)


FORWARD_SEM = 3


def _diag_forward_copy(srcs, dsts, ssem, rsem):
    x, y, c = _my_pos()
    half = srcs[0].shape[1] // 2
    diag = 3 - (2 * x + y)
    mine = pl.ds(pl.multiple_of(c * half, 8), half)
    other = pl.ds(pl.multiple_of((1 - c) * half, 8), half)
    return [_Copy(srcs[0].at[diag, mine], dsts[0].at[diag, mine], dsts[0].at[diag, other], ssem.at[FORWARD_SEM],
                  rsem.at[FORWARD_SEM], (x, y, 1 - c))]


def _forward_then_start(fwd, big, order, name):
    def body(f_in, b_in, f_out, ssem, rsem, b_out, token):
        _diag_forward_copy([f_in], [f_out], ssem, rsem)[0].start()
        for send, _ in _half_copies([b_in], [b_out], ssem, rsem, order):
            send.start()
        token[...] = jnp.zeros_like(token)

    n_sem = FORWARD_SEM + 1
    out_shape = (pltpu.HBM(fwd.shape, fwd.dtype), pltpu.SemaphoreType.DMA((n_sem,)), pltpu.SemaphoreType.DMA((n_sem,)),
                 pltpu.HBM(big.shape, big.dtype), SDS((8, 128), f32))
    return _hbm_call(
        body, name=name, out_shape=out_shape, in_specs=[HBM_SPEC, HBM_SPEC],
        out_specs=(HBM_SPEC, SEM_SPEC, SEM_SPEC, HBM_SPEC, VMEM_SPEC), input_output_aliases={0: 0, 1: 3},
        compiler_params=_cp(has_side_effects=DATAFLOW),
    )(*[pltpu.with_memory_space_constraint(b, pltpu.HBM) for b in (fwd, big)])


def _cast_to_slot(w, jvec, name, after=None):
    rows, cols = w.shape
    tr = min(128, rows // 4)
    band = lambda k: slice(k * tr, (k + 1) * tr)
    return _stream_call(
        lambda x: (x.astype(bf16),), rows // tr, [(w, (tr, cols), lambda ref, k, j: ref.at[band(k)])],
        [((4, rows, cols), bf16, (tr, cols), lambda ref, k, j: ref.at[j, band(k)])], jvec, name, after)[0]


def _stream_call(fn, n_chunks, srcs, outs, vec, name, after=None):
    ns, no = len(srcs), len(outs)

    def body(*refs):
        s = refs[0][0]
        in_refs, out_refs = refs[1:1 + ns], refs[2 + ns:2 + ns + no]
        in_bufs, out_bufs = refs[2 + ns + no:2 + 2 * ns + no], refs[2 + 2 * ns + no:2 + 2 * ns + 2 * no]
        sem_in, sem_out = refs[2 + 2 * ns + 2 * no:]
        loads, stores = [], []
        for k in range(n_chunks):
            cps = [pltpu.make_async_copy(srcs[a][2](in_refs[a], k, s), in_bufs[a].at[k], sem_in.at[a, k])
                   for a in range(ns)]
            for cp in cps:
                cp.start()
            loads.append(cps)
        for k, cps in enumerate(loads):
            for cp in cps:
                cp.wait()
            for b, val in enumerate(fn(*[buf[k] for buf in in_bufs])):
                out_bufs[b][k] = val
                cp = pltpu.make_async_copy(out_bufs[b].at[k], outs[b][3](out_refs[b], k, s), sem_out.at[b, k])
                cp.start()
                stores.append(cp)
        for cp in stores:
            cp.wait()

    in_bufs = [pltpu.VMEM((n_chunks,) + tuple(shape), a.dtype) for a, shape, _ in srcs]
    out_bufs = [pltpu.VMEM((n_chunks,) + tuple(cshape), dtype) for _, dtype, cshape, _ in outs]
    n_bytes = sum(np.prod(b.shape) * np.dtype(b.dtype).itemsize for b in in_bufs + out_bufs)
    after_big = after is not None and after.size * after.dtype.itemsize > (1 << 20)
    after = pltpu.with_memory_space_constraint(after, pltpu.HBM) if after_big else after
    res = _pallas_call(
        body, name=name, out_shape=tuple(pltpu.HBM(shape, dtype) for shape, dtype, _, _ in outs),
        in_specs=[SMEM_SPEC] + [HBM_SPEC] * ns + [HBM_SPEC if after_big else ANY_SPEC], out_specs=(HBM_SPEC,) * no,
        scratch_shapes=in_bufs + out_bufs + [pltpu.SemaphoreType.DMA((ns, n_chunks)),
                                             pltpu.SemaphoreType.DMA((no, n_chunks))],
        compiler_params=_cp(int(n_bytes) // (1 << 20) + 8),
    )(vec, *[pltpu.with_memory_space_constraint(a, pltpu.HBM) for a, _, _ in srcs], vec if after is None else after)
    return res


def _exchange_copies(srcs, lands, ssem, rsem):
    x, y, c = _my_pos()
    half = srcs[0].shape[0] // 8
    cps = []
    for jb in range(4):
        src = srcs[0].at[pl.ds(pl.multiple_of((2 * jb + 1 - c) * half, 8), half)]
        cps.append(pltpu.make_async_remote_copy(src_ref=src, dst_ref=lands[0].at[jb], send_sem=ssem.at[jb],
                                                recv_sem=rsem.at[jb], device_id=(x, y, 1 - c), device_id_type=MESH))
    return cps


def _block_exchange_copies(srcs, lands, ssem, rsem):
    x, y, c = _my_pos()
    return [pltpu.make_async_remote_copy(src_ref=srcs[0].at[jb], dst_ref=lands[0].at[jb], send_sem=ssem.at[jb],
                                         recv_sem=rsem.at[jb], device_id=(x, y, 1 - c), device_id_type=MESH)
            for jb in range(4)]


def _scatter_copies(srcs, lands, ssem, rsem):
    x, y, c = _my_pos()
    cps = []
    for a in range(len(srcs)):
        for k, (dev, pj) in enumerate(_peer_chips(x, y, c)):
            cps.append(pltpu.make_async_remote_copy(src_ref=srcs[a].at[pj], dst_ref=lands[a].at[k],
                                                    send_sem=ssem.at[3 * a + k], recv_sem=rsem.at[3 * a + k],
                                                    device_id=dev, device_id_type=MESH))
    return cps


class _Copy:
    def __init__(self, src, dst, arrive, ssem, rsem, dev):
        make = lambda to: pltpu.make_async_remote_copy(src_ref=src, dst_ref=to, send_sem=ssem, recv_sem=rsem,
                                                       device_id=dev, device_id_type=MESH)
        send, arrival = make(dst), make(arrive)
        self.start, self.wait_send, self.wait_recv = send.start, send.wait_send, arrival.wait_recv


def _toward(x, y, along_x):
    return x + along_x * (1 - 2 * x), y + (1 - along_x) * (1 - 2 * y)


def _near_copies(srcs, dsts, ssem, rsem):
    x, y, c = _my_pos()
    px, py = _toward(x, y, c)
    j = 2 * x + y
    return [_Copy(srcs[0].at[j], dsts[0].at[j], dsts[0].at[2 * px + py], ssem.at[0], rsem.at[0], (px, py, c))]


def _pass_copy(srcs, dsts, ssem, rsem):
    x, y, c = _my_pos()
    px, py = _toward(x, y, c)
    qx, qy = _toward(x, y, 1 - c)
    got = 2 * px + py
    return [_Copy(srcs[0].at[got], dsts[0].at[got], dsts[0].at[2 * qx + qy], ssem.at[0], rsem.at[0], (x, y, 1 - c))]


def _relay_copy(srcs, dsts, ssem, rsem):
    x, y, c = _my_pos()
    px, py = _toward(x, y, c)
    qx, qy = _toward(x, y, 1 - c)
    half = srcs[0].shape[1] // 2
    mine = pl.ds(pl.multiple_of(c * half, 8), half)
    got, diag = 2 * px + py, 3 - (2 * x + y)
    return [_Copy(srcs[0].at[got, mine], dsts[0].at[got, mine], dsts[0].at[diag, mine], ssem.at[1], rsem.at[1],
                  (qx, qy, c))]


def _forward_copies(srcs, dsts, ssem, rsem):
    x, y, c = _my_pos()
    half = srcs[0].shape[1] // 2
    mine = pl.ds(pl.multiple_of(c * half, 8), half)
    other = pl.ds(pl.multiple_of((1 - c) * half, 8), half)
    return [_Copy(srcs[0].at[pj, mine], dsts[0].at[pj, mine], dsts[0].at[pj, other], ssem.at[k], rsem.at[k],
                  (x, y, 1 - c)) for k, (_, pj) in enumerate(_peer_chips(x, y, c))]


def _pass_relay_copies(srcs, dsts, ssem, rsem):
    return _pass_copy(srcs, dsts, ssem, rsem) + _relay_copy(srcs, dsts, ssem, rsem)


def _gather8_copies(srcs, dsts, ssem, rsem):
    x, y, c = _my_pos()
    me = 4 * x + 2 * y + c
    cps = []
    for a in range(len(srcs)):
        for k in range(1, 8):
            tgt = (_flip(x, (k >> 2) & 1), _flip(y, (k >> 1) & 1), _flip(c, k & 1))
            cps.append(_Copy(srcs[a].at[me], dsts[a].at[me], dsts[a].at[4 * tgt[0] + 2 * tgt[1] + tgt[2]],
                             ssem.at[7 * a + k - 1], rsem.at[7 * a + k - 1], tgt))
    return cps


def _gather4_copies(srcs, dsts, ssem, rsem):
    x, y, c = _my_pos()
    j = 2 * x + y
    cps = []
    for a in range(len(srcs)):
        for k, (dev, pj) in enumerate(_peer_chips(x, y, c)):
            cps.append(_Copy(srcs[a].at[j], dsts[a].at[j], dsts[a].at[pj], ssem.at[3 * a + k], rsem.at[3 * a + k], dev))
    return cps


def _to_slot(a, n, i):
    return lax.dynamic_update_slice(jnp.zeros((n,) + a.shape, a.dtype), a[None], (i,) + (0,) * a.ndim)


def _split_start(srcs, land_shapes, n_cp, make, name, after=None):
    ns, nl = len(srcs), len(land_shapes)
    n_in = ns + nl + (after is not None)

    def body(*refs):
        s_in = refs[:ns]
        ssem, rsem = refs[n_in], refs[n_in + 1]
        s_out = refs[n_in + 2:n_in + 2 + ns]
        l_out = refs[n_in + 2 + ns:n_in + 2 + ns + nl]
        token = refs[n_in + 2 + ns + nl]
        for cp in make(s_in, l_out if nl else s_out, ssem, rsem):
            cp.start()
        token[...] = jnp.zeros_like(token)

    lands = [pltpu.with_memory_space_constraint(lax.empty(sh.shape, sh.dtype), pltpu.HBM) for sh in land_shapes]
    out_shape = (pltpu.SemaphoreType.DMA((n_cp,)), pltpu.SemaphoreType.DMA((n_cp,)),
                 *[pltpu.HBM(b.shape, b.dtype) for b in srcs], *[pltpu.HBM(b.shape, b.dtype) for b in land_shapes],
                 SDS((8, 128), f32))
    return _hbm_call(
        body, name=name, out_shape=out_shape, in_specs=[HBM_SPEC] * (ns + nl) + [ANY_SPEC] * (after is not None),
        out_specs=(SEM_SPEC, SEM_SPEC, *[HBM_SPEC] * (ns + nl), VMEM_SPEC),
        input_output_aliases={i: 2 + i for i in range(ns + nl)}, compiler_params=_cp(has_side_effects=DATAFLOW),
    )(*[pltpu.with_memory_space_constraint(b, pltpu.HBM) for b in srcs], *lands, *([] if after is None else [after]))


def _split_wait(ssem, rsem, srcs, lands, after, make, name):
    ns, nl = len(srcs), len(lands)

    def body(*refs):
        s_in, l_in = refs[:ns], refs[ns:ns + nl]
        ssem_ref, rsem_ref = refs[ns + nl], refs[ns + nl + 1]
        for cp in make(s_in, l_in if nl else s_in, ssem_ref, rsem_ref):
            cp.wait_send()
            cp.wait_recv()

    outs = _hbm_call(
        body, name=name, out_shape=tuple(pltpu.HBM(b.shape, b.dtype) for b in (*srcs, *lands)),
        in_specs=[HBM_SPEC] * (ns + nl) + [SEM_SPEC, SEM_SPEC, ANY_SPEC], out_specs=tuple([HBM_SPEC] * (ns + nl)),
        input_output_aliases={i: i for i in range(ns + nl)}, compiler_params=_cp(has_side_effects=DATAFLOW),
    )(*srcs, *lands, ssem, rsem, after)
    return list(outs[:ns]), list(outs[ns:])


def _adaln_shard(cc, w_ada_shard):
    def body(c_ref, w_ref, m_ref, sc_ref):
        sc = _silu(c_ref[...])
        sc_ref[...] = sc
        m_ref[...] = jnp.dot(sc, w_ref[...], precision=HIGHEST, preferred_element_type=f32)

    return _hbm_call(
        body, name="adaln_shard", out_shape=(SDS((16, w_ada_shard.shape[1]), f32), SDS((16, D), f32)),
        in_specs=[VMEM_SPEC, VMEM_SPEC], out_specs=(VMEM_SPEC, VMEM_SPEC), compiler_params=_cp(32),
    )(cc, w_ada_shard)


def _prenorm(xx, norm_g, mrow, b_ada, tm, name, after=None):
    n = xx.shape[0]

    def norm(x, g, m, b):
        shift = m[:, 0:D] + b[:, 0:D]
        scale = m[:, D:2 * D] + b[:, D:2 * D]
        r = lax.rsqrt(jnp.mean(x * x, axis=-1, keepdims=True) + RMS_EPS)
        y = (x * r) * g
        return ((y * (1.0 + scale) + shift).astype(bf16),)

    band = lambda ref, k, s: ref.at[k * tm:(k + 1) * tm]
    whole = lambda ref, k, s: ref
    jvec = jnp.zeros((1,), i32)
    return _stream_call(
        norm, n // tm, [(xx, (tm, D), band), (norm_g, (1, D), whole), (mrow, (1, 3 * D), whole),
                        (b_ada, (1, 3 * D), whole)],
        [((n, D), bf16, (tm, D), band)], jvec, name, after)[0]


def _in_proj_own(h, w_own, jvec):
    tm = 512

    def body(j_ref, h_ref, w_ref, p_ref):
        p_ref[...] = jnp.dot(h_ref[...], w_ref[...].astype(bf16), preferred_element_type=f32)

    grid_spec = pltpu.PrefetchScalarGridSpec(
        num_scalar_prefetch=1, grid=(S // tm,),
        in_specs=[pl.BlockSpec((tm, D), lambda i, j: (i, 0)), pl.BlockSpec((D, D), lambda i, j: (0, 0))],
        out_specs=pl.BlockSpec((tm, D), lambda i, j: (i, j[0])))
    return _hbm_call(body, name="in_proj_own", out_shape=SDS((S, 4 * D), f32), grid_spec=grid_spec,
                     compiler_params=_cp(40))(jvec, h, w_own)


def _in_proj_block(h, w4, p, bvec, name, after=None):
    tm = 512

    def body(b_ref, h_ref, w_ref, p_in_ref, after_ref, p_ref):
        p_ref[...] = jnp.dot(h_ref[...], w_ref[...], preferred_element_type=f32)

    grid_spec = pltpu.PrefetchScalarGridSpec(
        num_scalar_prefetch=1, grid=(S // tm,),
        in_specs=[pl.BlockSpec((tm, D), lambda i, b: (i, 0)), pl.BlockSpec((None, D, D), lambda i, b: (b[0], 0, 0)),
                  ANY_SPEC, ANY_SPEC],
        out_specs=pl.BlockSpec((tm, D), lambda i, b: (i, b[0])))
    return _hbm_call(body, name=name, out_shape=SDS((S, 4 * D), f32), grid_spec=grid_spec,
                     input_output_aliases={3: 0})(bvec, h, w4, p, bvec if after is None else after)


def _ctx_proj(hc, w4):
    def body(h_ref, w0_ref, w1_ref, p_ref):
        hv = h_ref[...]
        p_ref[:, 0:DA] = jnp.dot(hv, w0_ref[:, DA:2 * DA], preferred_element_type=f32)
        p_ref[:, DA:2 * DA] = jnp.dot(hv, w1_ref[:, 0:DA], preferred_element_type=f32)

    return _hbm_call(
        body, name="ctx_proj", out_shape=SDS((L, 2 * DA), f32), grid=(1,),
        in_specs=[pl.BlockSpec((L, D), lambda i: (0, 0)), pl.BlockSpec((None, D, D), lambda i: (0, 0, 0)),
                  pl.BlockSpec((None, D, D), lambda i: (1, 0, 0))],
        out_specs=pl.BlockSpec((L, 2 * DA), lambda i: (0, 0)),
    )(hc, w4, w4)


def _head_ones():
    r = lax.broadcasted_iota(i32, (DA, DA), 0) // DH
    c = lax.broadcasted_iota(i32, (DA, DA), 1) // DH
    return (r == c).astype(bf16)


def _head_sum(v, ones_bd):
    hi = v.astype(bf16)
    lo = (v - hi.astype(f32)).astype(bf16)
    return jnp.dot(hi, ones_bd, preferred_element_type=f32) + jnp.dot(lo, ones_bd, preferred_element_type=f32)


def _swap16(v):
    lane = lax.broadcasted_iota(i32, v.shape, 1)
    return jnp.where((lane & 31) < 16, pltpu.roll(v, DA - 16, 1), pltpu.roll(v, 16, 1))


def _rope_block(ct_ref, rt_ref, tm):
    rows = [jnp.tile(rt_ref[8 * j:8 * j + 8, :], (GW // 8, 1)) for j in range(tm // GW)]
    return jnp.tile(ct_ref[...], (tm // GW, 1)) + jnp.concatenate(rows, axis=0)


def _rope_specs(tm):
    col = pl.BlockSpec((GW, DA), lambda i: (0, 0))
    row = pl.BlockSpec((8 * tm // GW, DA), lambda i: (i, 0))
    return [col, row, col, row]


def _qk_prep(p, gq, gk, rope):
    tm = 512

    def body(qk_ref, v_ref, gq_ref, gk_ref, cc_ref, cr_ref, sc_ref, sr_ref, qr_ref, qp_ref, kr_ref, vh_ref):
        ones_bd = _head_ones()
        cs, sn = _rope_block(cc_ref, cr_ref, tm), _rope_block(sc_ref, sr_ref, tm)
        q = qk_ref[:, 0:DA]
        k = qk_ref[:, DA:2 * DA]
        yq = (q * lax.rsqrt(_head_sum(q * q, ones_bd) * (1.0 / DH) + RMS_EPS)) * gq_ref[...]
        yk = (k * lax.rsqrt(_head_sum(k * k, ones_bd) * (1.0 / DH) + RMS_EPS)) * gk_ref[...]
        qr = (yq * cs + _swap16(yq) * sn) * QK_SCALE
        qp = yq * QK_SCALE
        kr = yk * cs + _swap16(yk) * sn
        vv = v_ref[...]
        for hh in range(H):
            sl = slice(hh * DH, (hh + 1) * DH)
            qr_ref[hh] = qr[:, sl].astype(bf16)
            qp_ref[hh] = qp[:, sl].astype(bf16)
            kr_ref[hh] = kr[:, sl].astype(bf16)
            vh_ref[hh] = vv[:, sl].astype(bf16)

    hm = SDS((H, S, DH), bf16)
    hspec = pl.BlockSpec((H, tm, DH), lambda i: (0, i, 0))
    fixed = lambda i: (0, 0)
    return _hbm_call(
        body, name="qk_prep", out_shape=(hm, hm, hm, hm), grid=(S // tm,),
        in_specs=[pl.BlockSpec((tm, 2 * DA), lambda i: (i, 0)), pl.BlockSpec((tm, DA), lambda i: (i, 2)),
                  pl.BlockSpec((1, DA), fixed), pl.BlockSpec((1, DA), fixed)] + _rope_specs(tm),
        out_specs=(hspec, hspec, hspec, hspec),
    )(p, p, gq, gk, *rope)


def _ctx_prep(pc, gk):
    def body(p_ref, gk_ref, kc_ref, vc_ref):
        ones_bd = _head_ones()
        k = p_ref[:, 0:DA]
        yk = (k * lax.rsqrt(_head_sum(k * k, ones_bd) * (1.0 / DH) + RMS_EPS)) * gk_ref[...]
        vv = p_ref[:, DA:2 * DA]
        for hh in range(H):
            sl = slice(hh * DH, (hh + 1) * DH)
            kc_ref[hh] = yk[:, sl].astype(bf16)
            vc_ref[hh] = vv[:, sl].astype(bf16)

    hm = SDS((H, L, DH), bf16)
    return _hbm_call(
        body, name="ctx_prep", out_shape=(hm, hm), in_specs=[VMEM_SPEC, VMEM_SPEC], out_specs=(VMEM_SPEC, VMEM_SPEC),
    )(pc, gk)


def _tile_pieces():
    out = []
    for (i0, u0) in TILE_GEOM:
        rows = []
        for j in range(2):
            i = i0 + j
            rs = _row_start(i)
            rows.append([(u0 + u - i + WIN_H - 1) if rs <= u0 + u < rs + WIN_H else None for u in range(KR)])
        out.append(rows)
    return out


def _bias_prep(rpb_rev_pad, after=None):
    pieces = _tile_pieces()

    def body(r_ref, after_ref, o_ref):
        rp = r_ref[...]
        xs = jnp.concatenate([pltpu.roll(jnp.broadcast_to(rp[dr:dr + 1, :], (GW, 128)), 128 - (WIN_W - 1), 1,
                                         stride=1, stride_axis=0) for dr in range(N_DR)], axis=0)
        row = lax.broadcasted_iota(i32, xs.shape, 0)
        lane = lax.broadcasted_iota(i32, xs.shape, 1)
        k = row & (GW - 1)
        c0 = jnp.clip(lane - WIN_W // 2, 0, GW - WIN_W)
        xs = jnp.where((k >= c0) & (k < c0 + WIN_W), xs, NEG)
        neg = jnp.full((GW, GW), NEG, f32)
        for t in range(NT):
            for j in range(2):
                for u in range(KR):
                    dr = pieces[t][j][u]
                    piece = neg if dr is None else xs[dr * GW:(dr + 1) * GW, 0:GW]
                    o_ref[t, u * GW:(u + 1) * GW, j * GW:(j + 1) * GW] = piece

    return _hbm_call(
        body, name="bias_prep", out_shape=SDS((H, NT, KB, QB), f32), grid=(H,),
        in_specs=[pl.BlockSpec((None, N_DR, 128), lambda h: (h, 0, 0)), ANY_SPEC],
        out_specs=pl.BlockSpec((None, NT, KB, QB), lambda h: (h, 0, 0, 0)),
    )(rpb_rev_pad, rpb_rev_pad if after is None else after)


def _bias_tiles(rpb2, after=None):
    return _bias_prep(jnp.pad(rpb2[:, :, ::-1], ((0, 0), (0, 0), (0, 128 - N_DC))), after)


def _block_geom(b):
    qs = b * QB
    ks = min(max(2 * b - 4, 0), ROWS - KR) * GW
    t = b if b < 2 else (b - (NQB - NT) if b > NQB - 3 else 2)
    return qs, ks, t


def _tt(a, b):
    return lax.dot_general(a, b, (((1,), (1,)), ((), ())), preferred_element_type=f32)


def _tn(a, b):
    return lax.dot_general(a, b, (((0,), (0,)), ((), ())), preferred_element_type=f32)


def _softmax_t(s_lat, s_ctx):
    m = jnp.maximum(jnp.max(s_lat, axis=0, keepdims=True), jnp.max(s_ctx, axis=0, keepdims=True))
    e_lat = jnp.exp(s_lat - m)
    e_ctx = jnp.exp(s_ctx - m)
    inv = 1.0 / (jnp.sum(e_lat, axis=0, keepdims=True) + jnp.sum(e_ctx, axis=0, keepdims=True))
    return e_lat * inv, e_ctx * inv


def _staged(n_blocks, stages):
    held = [dict() for _ in stages]
    for step in range(n_blocks + len(stages) - 1):
        for s, fn in enumerate(stages):
            b = step - s
            if 0 <= b < n_blocks:
                held[s][b] = fn(b) if s == 0 else fn(b, held[s - 1].pop(b))


def _attn_fwd(qr, qp, kr, vh, kc, vc, btt):
    def body(qr_ref, qp_ref, kr_ref, v_ref, kc_ref, vc_ref, bt_ref, o_ref):
        kcv, vcv = kc_ref[...], vc_ref[...]

        def scores(b):
            qs, ks, t = _block_geom(b)
            return (_tt(kr_ref[ks:ks + KB, :], qr_ref[qs:qs + QB, :]) + bt_ref[t], _tt(kcv, qp_ref[qs:qs + QB, :]))

        def probs(b, sc):
            p_lat, p_ctx = _softmax_t(*sc)
            return p_lat.astype(bf16), p_ctx.astype(bf16)

        def values(b, p):
            qs, ks, _ = _block_geom(b)
            o_ref[qs:qs + QB, :] = _tn(p[0], v_ref[ks:ks + KB, :]) + _tn(p[1], vcv)

        _staged(NQB, (scores, probs, values))

    sq = pl.BlockSpec((None, S, DH), lambda h: (h, 0, 0))
    sc = pl.BlockSpec((None, L, DH), lambda h: (h, 0, 0))
    return _hbm_call(
        body, name="attn_fwd", out_shape=SDS((H, S, DH), f32), grid=(H,),
        in_specs=[sq, sq, sq, sq, sc, sc, pl.BlockSpec((None, NT, KB, QB), lambda h: (h, 0, 0, 0))],
        out_specs=sq, compiler_params=_cp(48),
    )(qr, qp, kr, vh, kc, vc, btt)


def _shift_rows(v, down):
    n = v.shape[0]
    row = lax.broadcasted_iota(i32, v.shape, 0)
    if down:
        return jnp.where(row == 0, 0.0, pltpu.roll(v, 1, 0))
    return jnp.where(row == n - 1, 0.0, pltpu.roll(v, n - 1, 0))


def _conv_specs():
    col = lambda off: pl.BlockSpec((S, 128), lambda i, off=off: (0, off + i))
    return [col(16), col(20), col(24), col(28), pl.BlockSpec((3, 128), lambda i: (0, i)),
            pl.BlockSpec((1, 128), lambda i: (0, i))]


def _conv_fwd(p, conv_w, conv_b, after=None):
    def body(u_ref, bg_ref, cg_ref, zc_ref, w_ref, b_ref, after_ref, o_ref):
        cu = cg_ref[...] * u_ref[...]
        cv = b_ref[...] + _shift_rows(cu, True) * w_ref[0:1, :]
        cv = cv + cu * w_ref[1:2, :]
        cv = cv + _shift_rows(cu, False) * w_ref[2:3, :]
        o_ref[...] = ((bg_ref[...] * cv) * _silu(zc_ref[...])).astype(bf16)

    return _hbm_call(
        body, name="conv_fwd", out_shape=SDS((S, DC), bf16), grid=(DC // 128,),
        in_specs=_conv_specs() + [ANY_SPEC], out_specs=pl.BlockSpec((S, 128), lambda i: (0, i)),
        compiler_params=_cp(40),
    )(p, p, p, p, conv_w, conv_b, conv_b if after is None else after)


DP_Q, DP_K, DP_V, DP_ZA, DP_U, DP_BG, DP_CG, DP_ZC = range(8)


def _out_proj_loss(o, p, conv_g, w_out, xx, tgt, mrow, b_ada):
    tm = 512

    def body(o_ref, za_ref, c_ref, w_ref, x_ref, t_ref, m_ref, b_ref,
             dy_ref, dconv_ref, dp_ref, do_ref, gwo_ref, dgate_ref, loss_ref):
        k = pl.program_id(0)

        @pl.when(k == 0)
        def _():
            gwo_ref[...] = jnp.zeros_like(gwo_ref)
            dgate_ref[...] = jnp.zeros_like(dgate_ref)
            loss_ref[0, 0] = 0.0

        gate = m_ref[:, 2 * D:3 * D] + b_ref[:, 2 * D:3 * D]
        za = za_ref[...]
        sz = _silu(za)
        om = _merge_heads(o_ref)
        av, cv = (om * sz).astype(bf16), c_ref[...]
        mo = jnp.dot(av, w_ref[0:DA, :], preferred_element_type=f32)
        mo = mo + jnp.dot(cv, w_ref[DA:DA + DC, :], preferred_element_type=f32)
        y = x_ref[...] + gate * mo
        diff = y - t_ref[...]
        loss_ref[0, 0] += jnp.sum(diff * diff)
        dy = diff * (1.0 / D)
        dy_ref[...] = dy
        dgate_ref[...] += jnp.sum(dy * mo, axis=0, keepdims=True)
        dmo = (dy * gate).astype(bf16)
        dmix = _tt(dmo, w_ref[...])
        dattn = dmix[:, 0:DA]
        dconv_ref[...] = dmix[:, DA:DA + DC]
        a = dattn * sz
        for hh in range(H):
            do_ref[hh] = a[:, hh * DH:(hh + 1) * DH].astype(bf16)
        dp_ref[...] = ((dattn * _dsilu(za)) * om).astype(bf16)
        gwo_ref[0:DA, :] += _tn(av, dmo)
        gwo_ref[DA:DA + DC, :] += _tn(cv, dmo)

    row = lambda i: (i, 0)
    fixed = lambda i: (0, 0)
    hspec = pl.BlockSpec((H, tm, DH), lambda i: (0, i, 0))
    return _hbm_call(
        body, name="out_proj_loss",
        out_shape=(SDS((S, D), f32), SDS((S, DC), f32), SDS((8, S, DA), bf16), SDS((H, S, DH), bf16),
                   SDS((D, D), f32), SDS((1, D), f32), SDS((1, 1), f32)),
        grid=(S // tm,),
        in_specs=[hspec, pl.BlockSpec((tm, DA), lambda i: (i, 3)), pl.BlockSpec((tm, DC), row),
                  pl.BlockSpec((D, D), fixed), pl.BlockSpec((tm, D), row), pl.BlockSpec((tm, D), row),
                  pl.BlockSpec((1, 3 * D), fixed), pl.BlockSpec((1, 3 * D), fixed)],
        out_specs=(pl.BlockSpec((tm, D), row), pl.BlockSpec((tm, DC), row),
                   pl.BlockSpec((None, tm, DA), lambda i: (DP_ZA, i, 0)), hspec, pl.BlockSpec((D, D), fixed),
                   pl.BlockSpec((1, D), fixed), SMEM_SPEC),
        compiler_params=_cp(56, dimension_semantics=("arbitrary",)),
    )(o, p, conv_g, w_out, xx, tgt, mrow, b_ada)


def _conv_bwd(dconv, p, conv_w, conv_b, dp8, after=None):
    def body(d_ref, u_ref, bg_ref, cg_ref, zc_ref, w_ref, b_ref, dp_in_ref, after_ref, dp_ref, gw_ref, gb_ref):
        du_ref, dbg_ref, dcg_ref, dzc_ref = dp_ref.at[0], dp_ref.at[1], dp_ref.at[2], dp_ref.at[3]
        dconv = d_ref[...]
        u, bg, cg, zc = u_ref[...], bg_ref[...], cg_ref[...], zc_ref[...]
        w0, w1, w2 = w_ref[0:1, :], w_ref[1:2, :], w_ref[2:3, :]
        cu = cg * u
        cu_m, cu_p = _shift_rows(cu, True), _shift_rows(cu, False)
        cv = b_ref[...] + cu_m * w0
        cv = cv + cu * w1
        cv = cv + cu_p * w2
        sz = _silu(zc)
        dbg_ref[...] = ((dconv * sz) * cv).astype(bf16)
        dzc_ref[...] = ((dconv * (bg * cv)) * _dsilu(zc)).astype(bf16)
        dcv = (dconv * sz) * bg
        gb_ref[...] = jnp.sum(dcv, axis=0, keepdims=True)
        gw_ref[0:1, :] = jnp.sum(dcv * cu_m, axis=0, keepdims=True)
        gw_ref[1:2, :] = jnp.sum(dcv * cu, axis=0, keepdims=True)
        gw_ref[2:3, :] = jnp.sum(dcv * cu_p, axis=0, keepdims=True)
        gw_ref[3:8, :] = jnp.zeros((5, 128), f32)
        dcu = _shift_rows(dcv, False) * w0 + dcv * w1 + _shift_rows(dcv, True) * w2
        dcg_ref[...] = (dcu * u).astype(bf16)
        du_ref[...] = (dcu * cg).astype(bf16)

    return _hbm_call(
        body, name="conv_bwd", out_shape=(SDS((8, S, DC), bf16), SDS((8, DC), f32), SDS((1, DC), f32)),
        grid=(DC // 128,),
        in_specs=[pl.BlockSpec((S, 128), lambda i: (0, i))] + _conv_specs() + [ANY_SPEC, ANY_SPEC],
        out_specs=(pl.BlockSpec((4, S, 128), lambda i: (DP_U // 4, 0, i)), pl.BlockSpec((8, 128), lambda i: (0, i)),
                   pl.BlockSpec((1, 128), lambda i: (0, i))),
        input_output_aliases={7: 0}, compiler_params=_cp(48),
    )(dconv, p, p, p, p, conv_w, conv_b, dp8, conv_b if after is None else after)


def _attn_bwd(qr, qp, kr, vh, kc, vc, btt, do, after=None):
    def body(qr_ref, qp_ref, kr_ref, v_ref, kc_ref, vc_ref, bt_ref, do_ref, after_ref,
             dqr_ref, dqp_ref, dkr_ref, dv_ref, dkc_ref, dvc_ref, dbt_ref):
        kcv, vcv = kc_ref[...], vc_ref[...]
        dkr_ref[...] = jnp.zeros_like(dkr_ref)
        dv_ref[...] = jnp.zeros_like(dv_ref)
        dbt_ref[...] = jnp.zeros_like(dbt_ref)
        ctx_acc = {}

        def products(b):
            qs, ks, t = _block_geom(b)
            dob = do_ref[qs:qs + QB, :]
            s_lat = _tt(kr_ref[ks:ks + KB, :], qr_ref[qs:qs + QB, :]) + bt_ref[t]
            s_ctx = _tt(kcv, qp_ref[qs:qs + QB, :])
            return s_lat, s_ctx, _tt(v_ref[ks:ks + KB, :], dob), _tt(vcv, dob)

        def score_grads(b, x):
            s_lat, s_ctx, dp_lat, dp_ctx = x
            p_lat, p_ctx = _softmax_t(s_lat, s_ctx)
            delta = jnp.sum(p_lat * dp_lat, axis=0, keepdims=True) + jnp.sum(p_ctx * dp_ctx, axis=0, keepdims=True)
            ds_lat = p_lat * (dp_lat - delta)
            ds_ctx = p_ctx * (dp_ctx - delta)
            return ds_lat, ds_lat.astype(bf16), ds_ctx.astype(bf16), p_lat.astype(bf16), p_ctx.astype(bf16)

        def operand_grads(b, y):
            qs, ks, t = _block_geom(b)
            ds_lat, dsb_lat, dsb_ctx, pb_lat, pb_ctx = y
            qrb, qpb, dob = qr_ref[qs:qs + QB, :], qp_ref[qs:qs + QB, :], do_ref[qs:qs + QB, :]
            dbt_ref[t] += ds_lat
            dqr_ref[qs:qs + QB, :] = _tn(dsb_lat, kr_ref[ks:ks + KB, :])
            dqp_ref[qs:qs + QB, :] = _tn(dsb_ctx, kcv)
            dkr_ref[ks:ks + KB, :] += jnp.dot(dsb_lat, qrb, preferred_element_type=f32)
            dv_ref[ks:ks + KB, :] += jnp.dot(pb_lat, dob, preferred_element_type=f32)
            dkc = jnp.dot(dsb_ctx, qpb, preferred_element_type=f32)
            dvc = jnp.dot(pb_ctx, dob, preferred_element_type=f32)
            ctx_acc["k"] = dkc if b == 0 else ctx_acc["k"] + dkc
            ctx_acc["v"] = dvc if b == 0 else ctx_acc["v"] + dvc

        _staged(NQB, (products, score_grads, operand_grads))
        dkc_ref[...] = ctx_acc["k"]
        dvc_ref[...] = ctx_acc["v"]

    sq = pl.BlockSpec((None, S, DH), lambda h: (h, 0, 0))
    sc = pl.BlockSpec((None, L, DH), lambda h: (h, 0, 0))
    sb = pl.BlockSpec((None, NT, KB, QB), lambda h: (h, 0, 0, 0))
    big, ctxs = SDS((H, S, DH), f32), SDS((H, L, DH), f32)
    return _hbm_call(
        body, name="attn_bwd", out_shape=(big, big, big, big, ctxs, ctxs, SDS((H, NT, KB, QB), f32)), grid=(H,),
        in_specs=[sq, sq, sq, sq, sc, sc, sb, sq, ANY_SPEC], out_specs=(sq, sq, sq, sq, sc, sc, sb),
        compiler_params=_cp(56),
    )(qr, qp, kr, vh, kc, vc, btt, do, do if after is None else after)


def _bias_bwd(dbtt, after=None):
    pieces = _tile_pieces()

    def body(d_ref, after_ref, o_ref, scr):
        scr[...] = jnp.zeros_like(scr)
        acc = [None] * N_DR
        for t in range(NT):
            for j in range(2):
                for u in range(KR):
                    dr = pieces[t][j][u]
                    if dr is None:
                        continue
                    piece = d_ref[t, u * GW:(u + 1) * GW, j * GW:(j + 1) * GW]
                    acc[dr] = piece if acc[dr] is None else acc[dr] + piece
        a = lax.broadcasted_iota(i32, (GW, GW), 0)
        b = lax.broadcasted_iota(i32, (GW, GW), 1)
        flip = (a + b == GW - 1).astype(f32)
        for dr in range(N_DR):
            scr[dr * GW:(dr + 1) * GW, 0:GW] = jnp.dot(acc[dr], flip, precision=HIGHEST, preferred_element_type=f32)
        xs = jnp.concatenate([pltpu.roll(scr[dr * GW:(dr + 1) * GW, :], 128 + (WIN_W - 1) - (GW - 1), 1,
                                         stride=1, stride_axis=0) for dr in range(N_DR)], axis=0)
        tot = jnp.sum(xs.reshape(N_DR, GW, 128), axis=1)
        lane = lax.broadcasted_iota(i32, tot.shape, 1)
        o_ref[...] = jnp.where(lane < N_DC, tot, 0.0)

    return _hbm_call(
        body, name="bias_bwd", out_shape=SDS((H, N_DR, 128), f32), grid=(H,),
        in_specs=[pl.BlockSpec((None, NT, KB, QB), lambda h: (h, 0, 0, 0)), ANY_SPEC],
        out_specs=pl.BlockSpec((None, N_DR, 128), lambda h: (h, 0, 0)),
        scratch_shapes=[pltpu.VMEM((N_DR * GW, 128), f32)],
    )(dbtt, dbtt if after is None else after)


def _merge_heads(ref):
    return jnp.concatenate([ref[hh] for hh in range(H)], axis=1)


def _head_norm_bwd(xraw, gain, dy, ones_bd):
    r = lax.rsqrt(_head_sum(xraw * xraw, ones_bd) * (1.0 / DH) + RMS_EPS)
    xh = xraw * r
    gdy = dy * gain
    dx = r * (gdy - xh * (_head_sum(xh * gdy, ones_bd) * (1.0 / DH)))
    return dx, jnp.sum(dy * xh, axis=0, keepdims=True)


def _qk_bwd(dqr, dqp, dkr, dvh, p, gq, gk, rope, dp8):
    tm = 512

    def body(dqr_ref, dqp_ref, dkr_ref, dv_ref, qk_ref, gq_ref, gk_ref, cc_ref, cr_ref, sc_ref, sr_ref, dp_in_ref,
             dp_ref, ggq_ref, ggk_ref):
        dq_ref, dk_ref, dvo_ref = dp_ref.at[DP_Q], dp_ref.at[DP_K], dp_ref.at[DP_V]

        @pl.when(pl.program_id(0) == 0)
        def _():
            ggq_ref[...] = jnp.zeros_like(ggq_ref)
            ggk_ref[...] = jnp.zeros_like(ggk_ref)

        ones_bd = _head_ones()
        cs, sn = _rope_block(cc_ref, cr_ref, tm), _rope_block(sc_ref, sr_ref, tm)
        a = _merge_heads(dqr_ref)
        dyq = ((a * cs - _swap16(a) * sn) + _merge_heads(dqp_ref)) * QK_SCALE
        bk = _merge_heads(dkr_ref)
        dyk = bk * cs - _swap16(bk) * sn
        dq, gq_part = _head_norm_bwd(qk_ref[:, 0:DA], gq_ref[...], dyq, ones_bd)
        dk, gk_part = _head_norm_bwd(qk_ref[:, DA:2 * DA], gk_ref[...], dyk, ones_bd)
        dq_ref[...] = dq.astype(bf16)
        dk_ref[...] = dk.astype(bf16)
        dvo_ref[...] = _merge_heads(dv_ref).astype(bf16)
        ggq_ref[...] += gq_part
        ggk_ref[...] += gk_part

    hspec = pl.BlockSpec((H, tm, DH), lambda i: (0, i, 0))
    fixed = pl.BlockSpec((1, DA), lambda i: (0, 0))
    return _hbm_call(
        body, name="qk_bwd", out_shape=(SDS((8, S, DA), bf16), SDS((1, DA), f32), SDS((1, DA), f32)), grid=(S // tm,),
        in_specs=[hspec, hspec, hspec, hspec, pl.BlockSpec((tm, 2 * DA), lambda i: (i, 0)), fixed, fixed]
        + _rope_specs(tm) + [ANY_SPEC],
        out_specs=(pl.BlockSpec((3, tm, DA), lambda i: (0, i, 0)), fixed, fixed), input_output_aliases={11: 0},
        compiler_params=_cp(40, dimension_semantics=("arbitrary",)),
    )(dqr, dqp, dkr, dvh, p, gq, gk, *rope, dp8)


def _ctx_bwd(dkc, dvc, pc, gk):
    def body(dkc_ref, dvc_ref, p_ref, gk_ref, dk_ref, dv_ref, ggk_ref):
        ones_bd = _head_ones()
        dk, gk_part = _head_norm_bwd(p_ref[:, 0:DA], gk_ref[...], _merge_heads(dkc_ref), ones_bd)
        dk_ref[...] = dk.astype(bf16)
        dv_ref[...] = _merge_heads(dvc_ref).astype(bf16)
        ggk_ref[...] = gk_part

    piece = SDS((L, DA), bf16)
    return _hbm_call(
        body, name="ctx_bwd", out_shape=(piece, piece, SDS((1, DA), f32)), in_specs=[VMEM_SPEC] * 4,
        out_specs=(VMEM_SPEC,) * 3,
    )(dkc, dvc, pc, gk)


def _grad_w_in(h, dp8, hc, dkc_raw, dvc_m, half, name, after=None):
    def body(half_ref, h_ref, p_ref, hc_ref, dk_ref, dv_ref, after_ref, g_ref):
        j = pl.program_id(0)
        hv = h_ref[...]
        g_ref[:, 0:DA] = _tn(hv, p_ref[0])
        g_ref[:, DA:2 * DA] = _tn(hv, p_ref[1])

        @pl.when(j == 0)
        def _():
            g_ref[:, DA:2 * DA] += _tn(hc_ref[...], dk_ref[...])

        @pl.when(j == 1)
        def _():
            g_ref[:, 0:DA] += _tn(hc_ref[...], dv_ref[...])

    fixed = lambda j, s: (0, 0)
    hd = D // 2
    grid_spec = pltpu.PrefetchScalarGridSpec(
        num_scalar_prefetch=1, grid=(4,),
        in_specs=[pl.BlockSpec((S, hd), lambda j, s: (0, s[0])), pl.BlockSpec((2, S, DA), lambda j, s: (j, 0, 0)),
                  pl.BlockSpec((L, hd), lambda j, s: (0, s[0])), pl.BlockSpec((L, DA), fixed),
                  pl.BlockSpec((L, DA), fixed), ANY_SPEC],
        out_specs=pl.BlockSpec((None, hd, D), lambda j, s: (j, 0, 0)))
    return _hbm_call(
        body, name=name, out_shape=SDS((4, hd, D), f32), grid_spec=grid_spec, compiler_params=_cp(40),
    )(half, h, dp8, hc, dkc_raw, dvc_m, half if after is None else after)


def _grad_w_in_pair_sum(h, dp8, hc, dkc_raw, dvc_m, half, sent, landed, ssem, rsem):
    def body(half_ref, h_ref, p_ref, hc_ref, dk_ref, dv_ref, sent_ref, land_ref, ssem_ref, rsem_ref,
             t32_ref, tb_ref, buf, lsem):
        j = pl.program_id(0)
        hv = h_ref[...]
        t32_ref[:, 0:DA] = _tn(hv, p_ref[0])
        x, y, c = _my_pos()
        arrival = pltpu.make_async_remote_copy(src_ref=sent_ref.at[j], dst_ref=land_ref.at[j], send_sem=ssem_ref.at[j],
                                               recv_sem=rsem_ref.at[j], device_id=(x, y, 1 - c), device_id_type=MESH)
        arrival.wait_recv()
        arrival.wait_send()
        load = pltpu.make_async_copy(land_ref.at[j], buf, lsem)
        load.start()
        t32_ref[:, DA:2 * DA] = _tn(hv, p_ref[1])

        @pl.when(j == 0)
        def _():
            t32_ref[:, DA:2 * DA] += _tn(hc_ref[...], dk_ref[...])

        @pl.when(j == 1)
        def _():
            t32_ref[:, 0:DA] += _tn(hc_ref[...], dv_ref[...])

        load.wait()
        t = t32_ref[...] + buf[...]
        t32_ref[...] = t
        tb_ref[...] = t.astype(bf16)

    fixed = lambda j, s: (0, 0)
    hd = D // 2
    out_spec = pl.BlockSpec((None, hd, D), lambda j, s: (j, 0, 0))
    grid_spec = pltpu.PrefetchScalarGridSpec(
        num_scalar_prefetch=1, grid=(4,),
        in_specs=[pl.BlockSpec((S, hd), lambda j, s: (0, s[0])), pl.BlockSpec((2, S, DA), lambda j, s: (j, 0, 0)),
                  pl.BlockSpec((L, hd), lambda j, s: (0, s[0])), pl.BlockSpec((L, DA), fixed),
                  pl.BlockSpec((L, DA), fixed), HBM_SPEC, HBM_SPEC, SEM_SPEC, SEM_SPEC],
        out_specs=(out_spec, out_spec), scratch_shapes=[pltpu.VMEM((hd, D), f32), pltpu.SemaphoreType.DMA])
    return _hbm_call(
        body, name="grad_w_in_pair_sum", out_shape=(SDS((4, hd, D), f32), SDS((4, hd, D), bf16)), grid_spec=grid_spec,
        compiler_params=_cp(40, has_side_effects=DATAFLOW),
    )(half, h, dp8, hc, dkc_raw, dvc_m, sent, landed, ssem, rsem)


def _norm_mod_bwd(x, dh, g, scale):
    r = lax.rsqrt(jnp.mean(x * x, axis=-1, keepdims=True) + RMS_EPS)
    xh = x * r
    y = xh * g
    dshift = jnp.sum(dh, axis=0, keepdims=True)
    dscale = jnp.sum(dh * y, axis=0, keepdims=True)
    dyn = dh * (1.0 + scale)
    dg = jnp.sum(dyn * xh, axis=0, keepdims=True)
    gdy = dyn * g
    dx = r * (gdy - xh * jnp.mean(xh * gdy, axis=-1, keepdims=True))
    return dx, dshift, dscale, dg


def _dh_grad_x(dp8, w4, xx, dy, norm_g, mrow, b_ada, after=None):
    tm = 512

    def body(p_ref, w_ref, x_ref, dy_ref, g_ref, m_ref, b_ref, after_ref, gx_ref, dsh_ref, dsc_ref, dg_ref):
        @pl.when(pl.program_id(0) == 0)
        def _():
            dsh_ref[...] = jnp.zeros_like(dsh_ref)
            dsc_ref[...] = jnp.zeros_like(dsc_ref)
            dg_ref[...] = jnp.zeros_like(dg_ref)

        dh = None
        for j in range(4):
            for half in range(2):
                term = _tt(p_ref[2 * j + half], w_ref[j, :, half * DA:(half + 1) * DA])
                dh = term if dh is None else dh + term
        scale = m_ref[:, D:2 * D] + b_ref[:, D:2 * D]
        dx, dshift, dscale, dg = _norm_mod_bwd(x_ref[...], dh, g_ref[...], scale)
        gx_ref[...] = dy_ref[...] + dx
        dsh_ref[...] += dshift
        dsc_ref[...] += dscale
        dg_ref[...] += dg

    row = lambda i: (i, 0)
    fixed = lambda i: (0, 0)
    vec = SDS((1, D), f32)
    return _hbm_call(
        body, name="dh_grad_x", out_shape=(SDS((S, D), f32), vec, vec, vec), grid=(S // tm,),
        in_specs=[pl.BlockSpec((8, tm, DA), lambda i: (0, i, 0)), pl.BlockSpec((4, D, D), lambda i: (0, 0, 0)),
                  pl.BlockSpec((tm, D), row), pl.BlockSpec((tm, D), row), pl.BlockSpec((1, D), fixed),
                  pl.BlockSpec((1, 3 * D), fixed), pl.BlockSpec((1, 3 * D), fixed), ANY_SPEC],
        out_specs=(pl.BlockSpec((tm, D), row), pl.BlockSpec((1, D), fixed), pl.BlockSpec((1, D), fixed),
                   pl.BlockSpec((1, D), fixed)),
        compiler_params=_cp(56, dimension_semantics=("arbitrary",)),
    )(dp8, w4, xx, dy, norm_g, mrow, b_ada, b_ada if after is None else after)


def _dhc_sums(dkc_raw, dvc_m, w4, ctx2, norm_g, mrow_c, b_ada, after=None):
    def body(dk_ref, dv_ref, w0_ref, w1_ref, x_ref, g_ref, m_ref, b_ref, after_ref, dsh_ref, dsc_ref, dg_ref):
        dh = _tt(dk_ref[...], w0_ref[:, DA:2 * DA]) + _tt(dv_ref[...], w1_ref[:, 0:DA])
        scale = m_ref[:, D:2 * D] + b_ref[:, D:2 * D]
        _, dshift, dscale, dg = _norm_mod_bwd(x_ref[...], dh, g_ref[...], scale)
        dsh_ref[...] = dshift
        dsc_ref[...] = dscale
        dg_ref[...] = dg

    fixed = lambda i: (0, 0)
    vec = SDS((1, D), f32)
    vspec = pl.BlockSpec((1, D), fixed)
    return _hbm_call(
        body, name="dhc_sums", out_shape=(vec, vec, vec), grid=(1,),
        in_specs=[pl.BlockSpec((L, DA), fixed), pl.BlockSpec((L, DA), fixed),
                  pl.BlockSpec((None, D, D), lambda i: (0, 0, 0)), pl.BlockSpec((None, D, D), lambda i: (1, 0, 0)),
                  pl.BlockSpec((L, D), fixed), vspec, pl.BlockSpec((1, 3 * D), fixed), pl.BlockSpec((1, 3 * D), fixed),
                  ANY_SPEC],
        out_specs=(vspec, vspec, vspec), compiler_params=_cp(32),
    )(dkc_raw, dvc_m, w4, w4, ctx2, norm_g, mrow_c, b_ada, b_ada if after is None else after)


def _rope_tables():
    nf = DH // 4
    inv = np.float32(ROPE_THETA) ** (-np.arange(nf, dtype=np.float32) / np.float32(nf))
    ang_c = np.arange(GW, dtype=np.float32)[:, None] * inv
    ang_r = np.arange(ROWS, dtype=np.float32)[:, None] * inv
    zc, zr = np.zeros((GW, 2 * nf), np.float32), np.zeros((ROWS, 2 * nf), np.float32)
    ct_cos = np.tile(np.concatenate([zc, np.cos(ang_c), np.cos(ang_c)], axis=1), (1, H))
    ct_sin = np.tile(np.concatenate([zc, -np.sin(ang_c), np.sin(ang_c)], axis=1), (1, H))
    rt_cos = np.tile(np.concatenate([np.cos(ang_r), np.cos(ang_r), zr], axis=1), (1, H))
    rt_sin = np.tile(np.concatenate([-np.sin(ang_r), np.sin(ang_r), zr], axis=1), (1, H))
    rep8 = lambda t: np.ascontiguousarray(np.broadcast_to(t[:, None, :], (ROWS, 8, DA))).reshape(ROWS * 8, DA)
    return tuple(jnp.asarray(t, f32) for t in (ct_cos, rep8(rt_cos), ct_sin, rep8(rt_sin)))


def _local_step(xx, ctx2, tgt, mrow, mrow_c, b_ada, norm_g, weights, q_norm_g, k_norm_g, rpb2, conv_w_full, conv_b,
                hooks=None):
    hooks = hooks or {}
    gq = jnp.tile(q_norm_g, (1, H))
    gk = jnp.tile(k_norm_g, (1, H))
    rope = _rope_tables()

    h = _prenorm(xx, norm_g, mrow, b_ada, 256, "prenorm_x", after=weights.get("started"))
    jv = weights["jvec"]
    p = _in_proj_own(h, weights["own"], jv)
    btb = _bias_tiles(rpb2, after=p)
    w4, started = weights["near"](btb)
    p = _in_proj_block(h, w4, p, weights["first"], "in_proj_near", after=started)
    w4 = weights["near2"](w4, p)
    p = _in_proj_block(h, w4, p, weights["second"], "in_proj_near2")
    ctx_norm = {}

    def filler(token):
        ctx_norm["hc"] = _prenorm(ctx2, norm_g, mrow_c, b_ada, L // 2, "prenorm_ctx", after=token)
        return ctx_norm["hc"]

    w4, started = weights["far"](w4, p, filler)
    hc = ctx_norm["hc"]
    p = _in_proj_block(h, w4, p, jv ^ 3, "in_proj_far", after=started)
    pc = _ctx_proj(hc, w4)
    qr, qp, kr, vh = _qk_prep(p, gq, gk, rope)
    kc, vc = _ctx_prep(pc, gk)
    o = _attn_fwd(qr, qp, kr, vh, kc, vc, btb)
    started = weights["out_arrived"](o) if "out_arrived" in weights else None
    conv_g = _conv_fwd(p, conv_w_full, conv_b, after=started)
    w_out_full = weights["out"](conv_g)
    dy, dconv, dp8, do, g_w_out, dgate, loss_sum = _out_proj_loss(o, p, conv_g, w_out_full, xx, tgt, mrow, b_ada)
    started = hooks["g_w_out"](g_w_out) if "g_w_out" in hooks else None
    dp8, g_conv_w, g_conv_b = _conv_bwd(dconv, p, conv_w_full, conv_b, dp8, after=started)
    started = hooks["after_conv"](dp8) if "after_conv" in hooks else None
    dqr, dqp, dkr, dvh, dkc, dvc, dbtb = _attn_bwd(qr, qp, kr, vh, kc, vc, btb, do, after=started)
    dp8, g_gq, g_gk = _qk_bwd(dqr, dqp, dkr, dvh, p, gq, gk, rope, dp8)
    dkc_raw, dvc_m, g_gk_c = _ctx_bwd(dkc, dvc, pc, gk)
    first = hooks.get("first_half", jnp.zeros((1,), i32))
    g_first = _grad_w_in(h, dp8, hc, dkc_raw, dvc_m, first, "grad_w_in_first")
    started = hooks["g_w_in_first"](g_first) if "g_w_in_first" in hooks else None
    if "w_in_pair_sum" in hooks:
        g_second = None
        started = hooks["w_in_pair_sum"](functools.partial(_grad_w_in_pair_sum, h, dp8, hc, dkc_raw, dvc_m, 1 - first))
    else:
        g_second = _grad_w_in(h, dp8, hc, dkc_raw, dvc_m, 1 - first, "grad_w_in_second", after=started)
    dshift_c, dscale_c, dng_c = _dhc_sums(dkc_raw, dvc_m, w4, ctx2, norm_g, mrow_c, b_ada, after=started)
    g_rpb = _bias_bwd(dbtb, after=dshift_c)
    grad_x, dshift, dscale, dng = _dh_grad_x(dp8, w4, xx, dy, norm_g, mrow, b_ada, after=g_rpb)
    return dict(loss_sum=loss_sum, grad_x=grad_x, g_w_in=(g_first, g_second), g_w_out=g_w_out, g_conv_w=g_conv_w,
                g_conv_b=g_conv_b, g_rpb=g_rpb, g_gq=g_gq, g_gk=g_gk, g_gk_c=g_gk_c, dshift=dshift, dscale=dscale,
                dgate=dgate, dng=dng, dshift_c=dshift_c, dscale_c=dscale_c, dng_c=dng_c)


def _pair_sum_w_out(g, r, cvec):
    hr = D // 8

    def add(own, other):
        t = own + other
        return t, t.astype(bf16)

    mine = lambda ref, q, c: ref.at[pl.ds(pl.multiple_of((2 * q + c) * hr, hr), hr)]
    block = lambda ref, q, c: ref.at[q]
    return _stream_call(add, 4, [(g, (hr, D), mine), (r, (hr, D), block)],
                        [((4, hr, D), f32, (hr, D), block), ((4, hr, D), bf16, (hr, D), block)], cvec, "pair_sum_w_out")


def _chip_sum(t32, r2, jvec, name):
    rows = t32.shape[1]
    tr = min(rows // 2, 128)
    band = lambda k: slice(k * tr, (k + 1) * tr)
    total =lambda t, r: (((t + r[0].astype(f32)) + r[1].astype(f32)) + r[2].astype(f32),)
    return _stream_call(
        total, rows // tr,
        [(t32, (tr, D), lambda ref, k, j: ref.at[j, band(k)]), (r2, (3, tr, D), lambda ref, k, j: ref.at[:, band(k)])],
        [((rows, D), f32, (tr, D), lambda ref, k, j: ref.at[band(k)])], jvec, name)[0]


_PK = {}
_off = 0
for _name, _rows in (("dm", 24), ("dmc", 24), ("dng", 8), ("dng_c", 8), ("gq", 8), ("gk", 8), ("gk_c", 8),
                     ("rpb", H * N_DR), ("conv_b", 8), ("conv_w", 16), ("loss", 8)):
    _PK[_name] = (_off, _off + _rows)
    _off += _rows
PK_ROWS = _off
RS_B_ADA, RS_NORM_G, RS_GQ, RS_GK, RS_RPB, RS_CONV_B, RS_CONV_W, RS_DMC, RS_LOSS, RS_ROWS = (
    0, 24, 32, 40, 48, 168, 176, 192, 216, 224)


def _small_reduce(gathered):
    def body(g_ref, o_ref, dm_ref):
        a0 = _PK["dm"][0]
        dm_ref[...] = jnp.zeros_like(dm_ref)
        for b in range(8):
            for i in range(24):
                dm_ref[b:b + 1, 128 * i:128 * (i + 1)] = g_ref[b, a0 + i:a0 + i + 1, :]
        tot = g_ref[0]
        for b in range(1, 8):
            tot = tot + g_ref[b]

        def rows(name):
            a, z = _PK[name]
            return tot[a:z]

        o_ref[RS_B_ADA:RS_B_ADA + 24] = rows("dm") + rows("dmc")
        o_ref[RS_NORM_G:RS_NORM_G + 8] = rows("dng") + rows("dng_c")
        gq = jnp.broadcast_to(jnp.sum(rows("gq"), axis=0, keepdims=True), (8, 128))
        gk = jnp.broadcast_to(jnp.sum(rows("gk") + rows("gk_c"), axis=0, keepdims=True), (8, 128))
        o_ref[RS_GQ:RS_GQ + 8] = gq + pltpu.roll(gq, DH, 1)
        o_ref[RS_GK:RS_GK + 8] = gk + pltpu.roll(gk, DH, 1)
        o_ref[RS_RPB:RS_RPB + H * N_DR] = rows("rpb")
        o_ref[RS_CONV_B:RS_CONV_B + 8] = rows("conv_b")
        o_ref[RS_CONV_W:RS_CONV_W + 16] = rows("conv_w")
        dmc = rows("dmc")
        o_ref[RS_DMC:RS_DMC + 24] = dmc
        o_ref[RS_LOSS:RS_LOSS + 8] = rows("loss")
        for i in range(24):
            dm_ref[8:9, 128 * i:128 * (i + 1)] = dmc[i:i + 1]

    return _hbm_call(body, name="small_reduce", out_shape=(SDS((RS_ROWS, 128), f32), SDS((16, 3 * D), f32)),
                     in_specs=[VMEM_SPEC], out_specs=(VMEM_SPEC, VMEM_SPEC))(gathered)


def _w_ada_grad(sc16, dm16, w_ada_shard, jvec):
    ncol = w_ada_shard.shape[1]

    def body(j_ref, sc_ref, dm_ref, w_ref, g_ref, part_ref):
        dm = dm_ref[...]
        g_ref[...] = lax.dot_general(sc_ref[...], dm, (((0,), (0,)), ((), ())), precision=HIGHEST,
                                     preferred_element_type=f32)
        part_ref[...] = lax.dot_general(dm[8:16], w_ref[...], (((1,), (1,)), ((), ())), precision=HIGHEST,
                                        preferred_element_type=f32)

    fixed = lambda i, j: (0, 0)
    grid_spec = pltpu.PrefetchScalarGridSpec(
        num_scalar_prefetch=1, grid=(1,),
        in_specs=[pl.BlockSpec((16, D), fixed), pl.BlockSpec((16, ncol), lambda i, j: (0, j[0])),
                  pl.BlockSpec((D, ncol), fixed)],
        out_specs=(pl.BlockSpec((D, ncol), fixed), pl.BlockSpec((8, D), fixed)))
    return _pallas_call(body, name="w_ada_grad", out_shape=(SDS((D, ncol), f32), SDS((8, D), f32)),
                        grid_spec=grid_spec, compiler_params=_cp(40))(jvec, sc16, dm16, w_ada_shard)


def _c_ctx_grad(parts4, c_ctx):
    def body(p_ref, c_ref, o_ref):
        tot = ((p_ref[0] + p_ref[1]) + p_ref[2]) + p_ref[3]
        o_ref[...] = tot[0:1] * _dsilu(c_ref[...].reshape(1, D))

    return _pallas_call(body, name="c_ctx_grad", out_shape=SDS((1, D), f32), in_specs=[VMEM_SPEC, VMEM_SPEC],
                        out_specs=VMEM_SPEC)(parts4, c_ctx)


def _adamw(w, g, m, v, name, after=None):
    return _adamw_stream(w, [g], m, v, None, name, after)


def _adamw_halves(w, g_mine, g_other, m, v, cvec, name, after=None):
    return _adamw_stream(w, [g_mine, g_other], m, v, cvec, name, after)


def _adamw_stream(w, g_parts, m, v, cvec, name, after):
    rows, cols = w.shape
    parts = len(g_parts)
    span = rows // parts
    chunk = min(128, span // 2)
    per = span // chunk
    outs_g = parts == 2
    pin = rows * cols * 4 > (1 << 20)

    def body(*refs):
        c_ref, w_ref = refs[0], refs[1]
        g_refs = refs[2:2 + parts]
        m_ref, v_ref = refs[2 + parts], refs[3 + parts]
        out_refs = refs[5 + parts:5 + parts + 3 + outs_g]
        bw, bg, bm, bv, bd, sem_in, sem_out = refs[5 + parts + 3 + outs_g:]
        c = c_ref[0]
        loads, stores = [], []
        for p in range(parts):
            first = (c if p == 0 else 1 - c) * span if parts == 2 else 0
            for kk in range(per):
                r = pl.ds(pl.multiple_of(first + kk * chunk, chunk), chunk)
                g_src = g_refs[p].at[kk * chunk:(kk + 1) * chunk]
                cps = [pltpu.make_async_copy(src, dst.at[r], sem_in.at[a, p, kk])
                       for a, (src, dst) in enumerate(((w_ref.at[r], bw), (g_src, bg), (m_ref.at[r], bm),
                                                       (v_ref.at[r], bv)))]
                for cp in cps:
                    cp.start()
                loads.append((r, p, kk, cps))
        for r, p, kk, cps in loads:
            for cp in cps:
                cp.wait()
            bd[r], bm[r], bv[r] = _adam_math(bw[r], bg[r], bm[r], bv[r])
            for a, (src, dst) in enumerate(zip(([bg] if outs_g else []) + [bd, bm, bv], out_refs)):
                cp = pltpu.make_async_copy(src.at[r], dst.at[r], sem_out.at[a, p, kk])
                cp.start()
                stores.append(cp)
        for cp in stores:
            cp.wait()

    shp = pltpu.HBM((rows, cols), f32) if pin else SDS((rows, cols), f32)
    spec = HBM_SPEC if pin else ANY_SPEC
    buf = pltpu.VMEM((rows, cols), f32)
    args = [pltpu.with_memory_space_constraint(a, pltpu.HBM) if pin else a for a in (w, *g_parts, m, v)]
    cvec = jnp.zeros((1,), i32) if cvec is None else cvec
    after_big = after is not None and after.size * 4 > (1 << 20)
    after = pltpu.with_memory_space_constraint(after, pltpu.HBM) if after_big else after
    return _pallas_call(
        body, name=name, out_shape=(shp,) * (3 + outs_g),
        in_specs=[SMEM_SPEC] + [spec] * (3 + parts) + [HBM_SPEC if after_big else ANY_SPEC],
        out_specs=(spec,) * (3 + outs_g),
        scratch_shapes=[buf] * 5 + [pltpu.SemaphoreType.DMA((4, parts, per)), pltpu.SemaphoreType.DMA((4, parts, per))],
        compiler_params=_cp(5 * rows * cols * 4 // (1 << 20) + 8),
    )(cvec, *args, cvec if after is None else after)


def _adam_math(w, g, m, v):
    m2 = ADAM_B1 * m + (1.0 - ADAM_B1) * g
    v2 = ADAM_B2 * v + (1.0 - ADAM_B2) * jnp.square(g)
    m_hat = m2 / (1.0 - ADAM_B1 ** ADAM_STEP)
    v_hat = v2 / (1.0 - ADAM_B2 ** ADAM_STEP)
    return -ADAM_LR * (m_hat / (jnp.sqrt(v_hat) + ADAM_EPS) + ADAM_WD * w), m2, v2


def _adamw_small(red, g_c_ctx, jvec, ws, ms, vs):
    n = len(ws)

    def body(*refs):
        red_ref, gc_ref, j_ref = refs[:3]
        w_refs, m_refs, v_refs = refs[3:3 + n], refs[3 + n:3 + 2 * n], refs[3 + 2 * n:3 + 3 * n]
        outs = refs[3 + 3 * n:]
        g_out, d_out, m_out, v_out = outs[:n], outs[n:2 * n], outs[2 * n:3 * n], outs[3 * n:]
        chip = j_ref[0]
        lanes = lambda i: (slice(None), slice(128 * i, 128 * (i + 1)))
        row = lambda r0, i: (lambda: red_ref[r0 + i:r0 + i + 1, :])
        whole = (slice(None), slice(None))
        chunks = [
            [((slice(None),), lambda: gc_ref[...].reshape(D))],
            [(lanes(i), row(RS_B_ADA, i)) for i in range(3 * D // 128)],
            [(lanes(i), row(RS_NORM_G, i)) for i in range(D // 128)],
            [(whole, lambda: red_ref[RS_GQ:RS_GQ + 1, 0:DH])],
            [(whole, lambda: red_ref[RS_GK:RS_GK + 1, 0:DH])],
            [((dr,), (lambda dr=dr: red_ref[pl.ds(RS_RPB + dr, H, stride=N_DR), 0:N_DC])) for dr in range(N_DR)],
            [((r,), (lambda r=r: red_ref[pl.ds(RS_CONV_W + 4 * r + chip, 1), :])) for r in range(3)],
            [(lanes(i), row(RS_CONV_B, i)) for i in range(DC // 128)],
        ]
        for a in range(n):
            for idx, grad in chunks[a]:
                g = grad()
                d, m2, v2 = _adam_math(w_refs[a][idx], g, m_refs[a][idx], v_refs[a][idx])
                g_out[a][idx] = g
                d_out[a][idx] = d
                m_out[a][idx] = m2
                v_out[a][idx] = v2

    shapes = [SDS(w.shape, f32) for w in ws]
    res = _pallas_call(body, name="adamw_small", out_shape=shapes * 4,
                       in_specs=[VMEM_SPEC, VMEM_SPEC, SMEM_SPEC] + [VMEM_SPEC] * (3 * n),
                       out_specs=[VMEM_SPEC] * (4 * n))(red, g_c_ctx, jvec, *ws, *ms, *vs)
    return [list(res[k * n:(k + 1) * n]) for k in range(4)]


def _rows128(a):
    return a.reshape(-1, 128)


def kernel(x, c, ctx, c_ctx, w_ada, b_ada, norm_g, w_in, q_norm_g, k_norm_g, rpb, conv_w, conv_b, w_out, loss_target, m_c_ctx, m_w_ada, m_b_ada, m_norm_g, m_w_in, m_q_norm_g, m_k_norm_g, m_rpb, m_conv_w, m_conv_b, m_w_out, v_c_ctx, v_w_ada, v_b_ada, v_norm_g, v_w_in, v_q_norm_g, v_k_norm_g, v_rpb, v_conv_w, v_conv_b, v_w_out):
    xi, yi, ci = lax.axis_index("x"), lax.axis_index("y"), lax.axis_index("c")
    dev = 4 * xi + 2 * yi + ci
    chip = 2 * xi + yi
    cvec = jnp.reshape(ci, (1,)).astype(i32)
    jvec = jnp.reshape(chip, (1,)).astype(i32)
    w_ada_s = w_ada[0]
    ncol = w_ada_s.shape[1]

    gc = _split_start([_to_slot(c.reshape(8, 128), 8, dev)], [], 7, _gather8_copies, "gather_c_start")
    wo4c = _cast_to_slot(w_out[0], jvec, "cast_w_out", after=gc[3])
    w4c = _cast_to_slot(w_in[0], jvec, "cast_w_in", after=wo4c)
    (c8,), _ = _split_wait(gc[0], gc[1], [gc[2]], [], w4c, _gather8_copies, "gather_c_wait")
    cc = jnp.concatenate([c8.reshape(8, D), c_ctx.reshape(1, D), jnp.zeros((7, D), f32)], axis=0)
    m_shard, sc16 = _adaln_shard(cc, w_ada_s)

    conv_w_pad = jnp.pad(conv_w[0], ((0, 5), (0, 0)))
    gm = _split_start([_to_slot(m_shard, 4, chip), _to_slot(conv_w_pad, 4, chip)], [], 6, _gather4_copies,
                      "gather_mod_start")

    all_k = [(0, 0, 0), (0, 0, 1), (0, 0, 2)]
    sem_a, rem_a, w4s, token = _split_start([w4c], [], 1, _near_copies, "weights_near_start", after=gm[4])
    (m4, cw4), _ = _split_wait(gm[0], gm[1], [gm[2], gm[3]], [], token, _gather4_copies, "gather_mod_wait")
    m_full = jnp.transpose(m4, (1, 0, 2)).reshape(16, 4 * ncol)
    mrow = lax.dynamic_slice(m_full, (dev, 0), (1, 3 * D))
    mrow_c = m_full[8:9]
    conv_w_full = jnp.transpose(cw4[:, 0:3, :], (1, 0, 2)).reshape(3, DC)
    waves = {}

    def near(after):
        (w4w,), _ = _split_wait(sem_a, rem_a, [w4s], [], after, _near_copies, "weights_near_wait")
        sem_b, rem_b, w4b, started = _split_start([w4w], [], 2, _pass_relay_copies, "weights_pass_start")
        waves["pass"] = (sem_b, rem_b)
        return w4b, started

    def near2(w4, after):
        (w4w,), _ = _split_wait(*waves["pass"], [w4], [], after, _pass_copy, "weights_pass_wait")
        return w4w

    def far(w4, after, filler):
        (w4w,), _ = _split_wait(*waves["pass"], [w4], [], after, _relay_copy, "weights_far_wait")
        w4x, sem_c, rem_c, wo4s, started = _forward_then_start(w4w, wo4c, all_k, "weights_far_forward_out_start")
        (w4f,), _ = _split_wait(sem_c, rem_c, [w4x], [], filler(started), _diag_forward_copy,
                                "weights_far_forward_wait")
        waves["out"] = (sem_c, rem_c, wo4s)
        return w4f, started

    def w_out_arrived(after):
        sem_c, rem_c, wo4s = waves["out"]
        (wow,) = _halves_wait(sem_c, rem_c, [wo4s], after, all_k, "weights_out_wait")
        sem_d, rem_d, wof, started = _split_start([wow], [], 3, _forward_copies, "weights_out_forward_start")
        waves["out_forward"] = (sem_d, rem_d, wof)
        return started

    def w_out_gathered(after):
        sem_d, rem_d, wof = waves["out_forward"]
        (wo,), _ = _split_wait(sem_d, rem_d, [wof], [], after, _forward_copies, "weights_out_forward_wait")
        return wo.reshape(D, D)

    weights = dict(own=w_in[0], jvec=jvec, started=token, first=jvec ^ (1 + cvec), second=jvec ^ (2 - cvec),
                   near=near, near2=near2, far=far, out_arrived=w_out_arrived, out=w_out_gathered)

    exchange = _exchange_copies
    pending = {}

    def on_g_w_out(g_w_out):
        out = _split_start([g_w_out], [SDS((4, D // 8, D), f32)], 4, exchange, "grad_out_pair_start")
        pending["ex_out"] = out
        return out[4]

    def after_conv(dp8):
        ssem_o, rsem_o, g_o, land_o, _ = pending["ex_out"]
        (g_o,), (ex_o,) = _split_wait(ssem_o, rsem_o, [g_o], [land_o], dp8, exchange, "grad_out_pair_wait")
        to32, tob = _pair_sum_w_out(g_o, ex_o, cvec)
        out = _split_start([tob], [SDS((3, D // 8, D), bf16)], 3, _scatter_copies, "grad_out_chip_start")
        pending["sc_out"] = (out, to32)
        return out[4]

    def on_g_w_in_first(g_first):
        out = _split_start([g_first], [SDS((4, D // 2, D), f32)], 4, _block_exchange_copies, "grad_pair_start")
        pending["ex"] = (out[0], out[1], [out[2]], [out[3]])
        return out[4]

    def on_w_in_pair_sum(pair_sum):
        ex_ssem, ex_rsem, ex_srcs, ex_lands = pending["ex"]
        t32, tb = pair_sum(ex_srcs[0], ex_lands[0], ex_ssem, ex_rsem)
        out = _split_start([tb], [SDS((3, D // 2, D), bf16)], 3, _scatter_copies, "grad_chip_start")
        pending["sc_in"] = (out, t32)
        return out[4]

    r = _local_step(x[0], ctx[0], loss_target[0], mrow, mrow_c, b_ada, norm_g, weights, q_norm_g, k_norm_g,
                    rpb[0], conv_w_full, conv_b,
                    dict(g_w_out=on_g_w_out, after_conv=after_conv, first_half=1 - cvec,
                         g_w_in_first=on_g_w_in_first, w_in_pair_sum=on_w_in_pair_sum))
    sc_in, t32 = pending["sc_in"]
    _, (r2,) = _split_wait(sc_in[0], sc_in[1], [sc_in[2]], [sc_in[3]], r["dng"], _scatter_copies, "grad_chip_wait")
    u_in = _chip_sum(t32, r2, jvec, "chip_sum_w_in")
    sc_o, to32 = pending["sc_out"]
    _, (ro2,) = _split_wait(sc_o[0], sc_o[1], [sc_o[2]], [sc_o[3]], u_in, _scatter_copies, "grad_out_chip_wait")
    u_out = _chip_sum(to32, ro2, jvec, "chip_sum_w_out")
    swap = _split_start([u_in, u_out], [SDS(u_in.shape, f32), SDS(u_out.shape, f32)], 2, _swap_copies,
                        "grad_pair_swap_start")

    dm = jnp.concatenate([r["dshift"], r["dscale"], r["dgate"]], axis=1)
    dmc = jnp.concatenate([r["dshift_c"], r["dscale_c"], jnp.zeros((1, D), f32)], axis=1)
    pack_parts = [_rows128(dm), _rows128(dmc), _rows128(r["dng"]), _rows128(r["dng_c"]), _rows128(r["g_gq"]),
                  _rows128(r["g_gk"]), _rows128(r["g_gk_c"]), r["g_rpb"].reshape(H * N_DR, 128),
                  _rows128(r["g_conv_b"]), _rows128(r["g_conv_w"][0:3]), jnp.pad(r["loss_sum"], ((0, 0), (0, 127)))]
    pack = jnp.concatenate([jnp.pad(p, ((0, -p.shape[0] % 8), (0, 0))) for p in pack_parts], axis=0)
    assert pack.shape[0] == PK_ROWS
    gs = _split_start([_to_slot(pack, 8, dev)], [], 7, _gather8_copies, "gather_small_start", after=swap[6])
    (u_in, u_out), (o_in, o_out) = _split_wait(swap[0], swap[1], swap[2:4], swap[4:6], gs[3], _swap_copies,
                                               "grad_pair_swap_wait")
    g_w_in_s, d_w_in, nm_w_in, nv_w_in = _adamw_halves(w_in[0], u_in, o_in, m_w_in[0], v_w_in[0], cvec, "adamw_w_in",
                                                       after=gs[3])
    g_w_out_s, d_w_out, nm_w_out, nv_w_out = _adamw_halves(w_out[0], u_out, o_out, m_w_out[0], v_w_out[0], cvec,
                                                           "adamw_w_out", after=nm_w_in)
    (gathered,), _ = _split_wait(gs[0], gs[1], [gs[2]], [], nm_w_out, _gather8_copies, "gather_small_wait")
    red, dm16 = _small_reduce(gathered)
    loss = red[RS_LOSS, 0] * (0.5 / D)

    g_w_ada_s, cpart = _w_ada_grad(sc16, dm16, w_ada_s, jvec)
    gcp = _split_start([_to_slot(cpart, 4, chip)], [], 3, _gather4_copies, "gather_c_ctx_parts_start")
    d_w_ada, nm_w_ada, nv_w_ada = _adamw(w_ada_s, g_w_ada_s, m_w_ada[0], v_w_ada[0], "adamw_w_ada", after=gcp[3])
    (cparts4,), _ = _split_wait(gcp[0], gcp[1], [gcp[2]], [], nm_w_ada, _gather4_copies, "gather_c_ctx_parts_wait")
    g_c_ctx = _c_ctx_grad(cparts4, c_ctx)

    t_rpb = lambda a: jnp.transpose(a, (0, 2, 1, 3)).reshape(N_DR, H, N_DC)
    t_cw = lambda a: jnp.transpose(a, (1, 0, 2))
    small = _adamw_small(
        red, g_c_ctx, jvec,
        [c_ctx, b_ada, norm_g, q_norm_g, k_norm_g, t_rpb(rpb), t_cw(conv_w), conv_b],
        [m_c_ctx, m_b_ada, m_norm_g, m_q_norm_g, m_k_norm_g, t_rpb(m_rpb), t_cw(m_conv_w), m_conv_b],
        [v_c_ctx, v_b_ada, v_norm_g, v_q_norm_g, v_k_norm_g, t_rpb(v_rpb), t_cw(v_conv_w), v_conv_b])
    for kind in small:
        kind[5] = jnp.transpose(kind[5].reshape(1, N_DR, H, N_DC), (0, 2, 1, 3))
        kind[6] = jnp.transpose(kind[6], (1, 0, 2))

    def ordered(kind, big_w_ada, big_w_in, big_w_out):
        s_c_ctx, s_b_ada, s_norm_g, s_q, s_k, s_rpb, s_conv_w, s_conv_b = small[kind]
        return [s_c_ctx, big_w_ada[None], s_b_ada, s_norm_g, big_w_in[None], s_q, s_k, s_rpb, s_conv_w,
                s_conv_b, big_w_out[None]]

    grads = ordered(0, g_w_ada_s, g_w_in_s, g_w_out_s)
    deltas = ordered(1, d_w_ada, d_w_in, d_w_out)
    new_m = ordered(2, nm_w_ada, nm_w_in, nm_w_out)
    new_v = ordered(3, nv_w_ada, nv_w_in, nv_w_out)
    return (loss, r["grad_x"][None], *grads, *deltas, *new_m, *new_v)
```

```python
import functools

import jax
import jax.numpy as jnp
import numpy as np
from jax import lax
from jax.experimental import pallas as pl
from jax.experimental.pallas import tpu as pltpu

f32, bf16, i32 = jnp.float32, jnp.bfloat16, jnp.int32
MESH = pl.DeviceIdType.MESH
HIGHEST = lax.Precision.HIGHEST

D = 1024
S = 2048
L = 256
GW = 64
ROWS = S // GW
H = 8
DH = 64
DA = H * DH
DC = 512
WIN_H, WIN_W = 8, 16
N_DR, N_DC = 2 * WIN_H - 1, 2 * WIN_W - 1
RMS_EPS = 1e-6
ROPE_THETA = 10000.0
QK_SCALE = DH ** -0.5
NEG = -1e30

QB = 128
NQB = S // QB
KR = 9
KB = KR * GW
TILE_GEOM = ((0, 0), (2, 0), (4, 0), (28, 23), (30, 23))
NT = len(TILE_GEOM)

ADAM_LR, ADAM_B1, ADAM_B2, ADAM_EPS, ADAM_WD, ADAM_STEP = 0.001, 0.9, 0.999, 1e-08, 0.01, 10

VMEM_SPEC = pl.BlockSpec(memory_space=pltpu.VMEM)
ANY_SPEC = pl.BlockSpec(memory_space=pl.ANY)
SMEM_SPEC = pl.BlockSpec(memory_space=pltpu.SMEM)
SDS = jax.ShapeDtypeStruct


_pallas_call = pl.pallas_call


def _hbm_call(body, *, out_shape, in_specs=None, out_specs=None, grid_spec=None, **kw):
    n_pre = 0
    if grid_spec is not None:
        ispecs, ospecs, n_pre = grid_spec.in_specs, grid_spec.out_specs, grid_spec.num_scalar_prefetch
        kw["grid_spec"] = grid_spec
    else:
        ispecs, ospecs = in_specs, out_specs
        kw.update(in_specs=in_specs, out_specs=out_specs)

    def blocked(spec):
        return isinstance(spec, pl.BlockSpec) and spec.block_shape is not None

    single = not isinstance(out_shape, (tuple, list))
    shapes = [out_shape] if single else list(out_shape)
    ospec_list = list(ospecs) if isinstance(ospecs, (tuple, list)) else [ospecs]
    shapes = [pltpu.HBM(s.shape, s.dtype) if blocked(sp) else s for s, sp in zip(shapes, ospec_list)]
    call = _pallas_call(body, out_shape=shapes[0] if single else tuple(shapes), **kw)

    def run(*args):
        arrays = [pltpu.with_memory_space_constraint(a, pltpu.HBM) if blocked(sp) else a
                  for a, sp in zip(args[n_pre:], ispecs)]
        return call(*args[:n_pre], *arrays)

    return run


def _cp(vmem_mb=None, **kw):
    if vmem_mb is not None:
        kw["vmem_limit_bytes"] = vmem_mb << 20
    return pltpu.CompilerParams(**kw)


def _silu(z):
    return z * jax.nn.sigmoid(z)


def _dsilu(z):
    sg = jax.nn.sigmoid(z)
    return sg * (1.0 + z * (1.0 - sg))


def _row_start(i):
    return min(max(i - WIN_H // 2, 0), ROWS - WIN_H)


def _my_pos():
    return lax.axis_index("x"), lax.axis_index("y"), lax.axis_index("c")


def _flip(v, bit):
    return 1 - v if bit else v


def _swap_copies(srcs, lands, ssem, rsem):
    x, y, c = _my_pos()
    return [pltpu.make_async_remote_copy(src_ref=srcs[a], dst_ref=lands[a], send_sem=ssem.at[a], recv_sem=rsem.at[a],
                                         device_id=(x, y, 1 - c), device_id_type=MESH) for a in range(len(srcs))]


HBM_SPEC = pl.BlockSpec(memory_space=pltpu.HBM)
SEM_SPEC = pl.BlockSpec(memory_space=pltpu.SEMAPHORE)
DATAFLOW = pltpu.SideEffectType.DATAFLOW_SIDE_EFFECTING


def _peer_chips(x, y, c):
    out = []
    for k in range(1, 4):
        px, py = _flip(x, (k >> 1) & 1), _flip(y, k & 1)
        out.append(((px, py, c), 2 * px + py))
    return out


def _half_copies(srcs, dsts, ssem, rsem, which):
    x, y, c = _my_pos()
    j = 2 * x + y
    peers = _peer_chips(x, y, c)
    pairs = []
    for pos, group, k in which:
        half = srcs[pos].shape[1] // 2
        mine = pl.ds(pl.multiple_of(c * half, 8), half)
        dev, pj = peers[k]
        sem = 3 * group + k
        send = pltpu.make_async_remote_copy(src_ref=srcs[pos].at[j, mine], dst_ref=dsts[pos].at[j, mine],
                                            send_sem=ssem.at[sem], recv_sem=rsem.at[sem], device_id=dev,
                                            device_id_type=MESH)
        arrive = pltpu.make_async_remote_copy(src_ref=srcs[pos].at[j, mine], dst_ref=dsts[pos].at[pj, mine],
                                              send_sem=ssem.at[sem], recv_sem=rsem.at[sem], device_id=dev,
                                              device_id_type=MESH)
        pairs.append((send, arrive))
    return pairs


def _halves_wait(ssem, rsem, bigs, after, which, name):
    nb = len(bigs)

    def body(*refs):
        b_in = refs[:nb]
        ssem_ref, rsem_ref = refs[nb], refs[nb + 1]
        for send, arrive in _half_copies(b_in, b_in, ssem_ref, rsem_ref, which):
            send.wait_send()
            arrive.wait_recv()

    return _hbm_call(
        body, name=name, out_shape=tuple(pltpu.HBM(b.shape, b.dtype) for b in bigs),
        in_specs=[HBM_SPEC] * nb + [SEM_SPEC, SEM_SPEC, ANY_SPEC], out_specs=tuple([HBM_SPEC] * nb),
        input_output_aliases={a: a for a in range(nb)}, compiler_params=_cp(has_side_effects=DATAFLOW),
    )(*bigs, ssem, rsem, after)


FORWARD_SEM = 3


def _diag_forward_copy(srcs, dsts, ssem, rsem):
    x, y, c = _my_pos()
    half = srcs[0].shape[1] // 2
    diag = 3 - (2 * x + y)
    mine = pl.ds(pl.multiple_of(c * half, 8), half)
    other = pl.ds(pl.multiple_of((1 - c) * half, 8), half)
    return [_Copy(srcs[0].at[diag, mine], dsts[0].at[diag, mine], dsts[0].at[diag, other], ssem.at[FORWARD_SEM],
                  rsem.at[FORWARD_SEM], (x, y, 1 - c))]


def _forward_then_start(fwd, big, order, name):
    def body(f_in, b_in, f_out, ssem, rsem, b_out, token):
        _diag_forward_copy([f_in], [f_out], ssem, rsem)[0].start()
        for send, _ in _half_copies([b_in], [b_out], ssem, rsem, order):
            send.start()
        token[...] = jnp.zeros_like(token)

    n_sem = FORWARD_SEM + 1
    out_shape = (pltpu.HBM(fwd.shape, fwd.dtype), pltpu.SemaphoreType.DMA((n_sem,)), pltpu.SemaphoreType.DMA((n_sem,)),
                 pltpu.HBM(big.shape, big.dtype), SDS((8, 128), f32))
    return _hbm_call(
        body, name=name, out_shape=out_shape, in_specs=[HBM_SPEC, HBM_SPEC],
        out_specs=(HBM_SPEC, SEM_SPEC, SEM_SPEC, HBM_SPEC, VMEM_SPEC), input_output_aliases={0: 0, 1: 3},
        compiler_params=_cp(has_side_effects=DATAFLOW),
    )(*[pltpu.with_memory_space_constraint(b, pltpu.HBM) for b in (fwd, big)])


def _cast_to_slot(w, jvec, name, after=None):
    rows, cols = w.shape
    tr = min(128, rows // 4)
    band = lambda k: slice(k * tr, (k + 1) * tr)
    return _stream_call(
        lambda x: (x.astype(bf16),), rows // tr, [(w, (tr, cols), lambda ref, k, j: ref.at[band(k)])],
        [((4, rows, cols), bf16, (tr, cols), lambda ref, k, j: ref.at[j, band(k)])], jvec, name, after)[0]


def _stream_call(fn, n_chunks, srcs, outs, vec, name, after=None):
    ns, no = len(srcs), len(outs)

    def body(*refs):
        s = refs[0][0]
        in_refs, out_refs = refs[1:1 + ns], refs[2 + ns:2 + ns + no]
        in_bufs, out_bufs = refs[2 + ns + no:2 + 2 * ns + no], refs[2 + 2 * ns + no:2 + 2 * ns + 2 * no]
        sem_in, sem_out = refs[2 + 2 * ns + 2 * no:]
        loads, stores = [], []
        for k in range(n_chunks):
            cps = [pltpu.make_async_copy(srcs[a][2](in_refs[a], k, s), in_bufs[a].at[k], sem_in.at[a, k])
                   for a in range(ns)]
            for cp in cps:
                cp.start(priority=k % 2)
            loads.append(cps)
        for k, cps in enumerate(loads):
            for cp in cps:
                cp.wait()
            for b, val in enumerate(fn(*[buf[k] for buf in in_bufs])):
                out_bufs[b][k] = val
                cp = pltpu.make_async_copy(out_bufs[b].at[k], outs[b][3](out_refs[b], k, s), sem_out.at[b, k])
                cp.start(priority=k % 2)
                stores.append(cp)
        for cp in stores:
            cp.wait()

    in_bufs = [pltpu.VMEM((n_chunks,) + tuple(shape), a.dtype) for a, shape, _ in srcs]
    out_bufs = [pltpu.VMEM((n_chunks,) + tuple(cshape), dtype) for _, dtype, cshape, _ in outs]
    n_bytes = sum(np.prod(b.shape) * np.dtype(b.dtype).itemsize for b in in_bufs + out_bufs)
    after_big = after is not None and after.size * after.dtype.itemsize > (1 << 20)
    after = pltpu.with_memory_space_constraint(after, pltpu.HBM) if after_big else after
    res = _pallas_call(
        body, name=name, out_shape=tuple(pltpu.HBM(shape, dtype) for shape, dtype, _, _ in outs),
        in_specs=[SMEM_SPEC] + [HBM_SPEC] * ns + [HBM_SPEC if after_big else ANY_SPEC], out_specs=(HBM_SPEC,) * no,
        scratch_shapes=in_bufs + out_bufs + [pltpu.SemaphoreType.DMA((ns, n_chunks)),
                                             pltpu.SemaphoreType.DMA((no, n_chunks))],
        compiler_params=_cp(int(n_bytes) // (1 << 20) + 8),
    )(vec, *[pltpu.with_memory_space_constraint(a, pltpu.HBM) for a, _, _ in srcs], vec if after is None else after)
    return res


def _exchange_copies(srcs, lands, ssem, rsem):
    x, y, c = _my_pos()
    half = srcs[0].shape[0] // 8
    cps = []
    for jb in range(4):
        src = srcs[0].at[pl.ds(pl.multiple_of((2 * jb + 1 - c) * half, 8), half)]
        cps.append(pltpu.make_async_remote_copy(src_ref=src, dst_ref=lands[0].at[jb], send_sem=ssem.at[jb],
                                                recv_sem=rsem.at[jb], device_id=(x, y, 1 - c), device_id_type=MESH))
    return cps


def _block_exchange_copies(srcs, lands, ssem, rsem):
    x, y, c = _my_pos()
    return [pltpu.make_async_remote_copy(src_ref=srcs[0].at[jb], dst_ref=lands[0].at[jb], send_sem=ssem.at[jb],
                                         recv_sem=rsem.at[jb], device_id=(x, y, 1 - c), device_id_type=MESH)
            for jb in range(4)]


def _scatter_copies(srcs, lands, ssem, rsem):
    x, y, c = _my_pos()
    cps = []
    for a in range(len(srcs)):
        for k, (dev, pj) in enumerate(_peer_chips(x, y, c)):
            cps.append(pltpu.make_async_remote_copy(src_ref=srcs[a].at[pj], dst_ref=lands[a].at[k],
                                                    send_sem=ssem.at[3 * a + k], recv_sem=rsem.at[3 * a + k],
                                                    device_id=dev, device_id_type=MESH))
    return cps


class _Copy:
    def __init__(self, src, dst, arrive, ssem, rsem, dev):
        make = lambda to: pltpu.make_async_remote_copy(src_ref=src, dst_ref=to, send_sem=ssem, recv_sem=rsem,
                                                       device_id=dev, device_id_type=MESH)
        send, arrival = make(dst), make(arrive)
        self.start, self.wait_send, self.wait_recv = send.start, send.wait_send, arrival.wait_recv


def _toward(x, y, along_x):
    return x + along_x * (1 - 2 * x), y + (1 - along_x) * (1 - 2 * y)


def _near_copies(srcs, dsts, ssem, rsem):
    x, y, c = _my_pos()
    px, py = _toward(x, y, c)
    j = 2 * x + y
    return [_Copy(srcs[0].at[j], dsts[0].at[j], dsts[0].at[2 * px + py], ssem.at[0], rsem.at[0], (px, py, c))]


def _pass_copy(srcs, dsts, ssem, rsem):
    x, y, c = _my_pos()
    px, py = _toward(x, y, c)
    qx, qy = _toward(x, y, 1 - c)
    got = 2 * px + py
    return [_Copy(srcs[0].at[got], dsts[0].at[got], dsts[0].at[2 * qx + qy], ssem.at[0], rsem.at[0], (x, y, 1 - c))]


def _relay_copy(srcs, dsts, ssem, rsem):
    x, y, c = _my_pos()
    px, py = _toward(x, y, c)
    qx, qy = _toward(x, y, 1 - c)
    half = srcs[0].shape[1] // 2
    mine = pl.ds(pl.multiple_of(c * half, 8), half)
    got, diag = 2 * px + py, 3 - (2 * x + y)
    return [_Copy(srcs[0].at[got, mine], dsts[0].at[got, mine], dsts[0].at[diag, mine], ssem.at[1], rsem.at[1],
                  (qx, qy, c))]


def _forward_copies(srcs, dsts, ssem, rsem):
    x, y, c = _my_pos()
    half = srcs[0].shape[1] // 2
    mine = pl.ds(pl.multiple_of(c * half, 8), half)
    other = pl.ds(pl.multiple_of((1 - c) * half, 8), half)
    return [_Copy(srcs[0].at[pj, mine], dsts[0].at[pj, mine], dsts[0].at[pj, other], ssem.at[k], rsem.at[k],
                  (x, y, 1 - c)) for k, (_, pj) in enumerate(_peer_chips(x, y, c))]


def _pass_relay_copies(srcs, dsts, ssem, rsem):
    return _pass_copy(srcs, dsts, ssem, rsem) + _relay_copy(srcs, dsts, ssem, rsem)


def _gather8_copies(srcs, dsts, ssem, rsem):
    x, y, c = _my_pos()
    me = 4 * x + 2 * y + c
    cps = []
    for a in range(len(srcs)):
        for k in range(1, 8):
            tgt = (_flip(x, (k >> 2) & 1), _flip(y, (k >> 1) & 1), _flip(c, k & 1))
            cps.append(_Copy(srcs[a].at[me], dsts[a].at[me], dsts[a].at[4 * tgt[0] + 2 * tgt[1] + tgt[2]],
                             ssem.at[7 * a + k - 1], rsem.at[7 * a + k - 1], tgt))
    return cps


def _gather4_copies(srcs, dsts, ssem, rsem):
    x, y, c = _my_pos()
    j = 2 * x + y
    cps = []
    for a in range(len(srcs)):
        for k, (dev, pj) in enumerate(_peer_chips(x, y, c)):
            cps.append(_Copy(srcs[a].at[j], dsts[a].at[j], dsts[a].at[pj], ssem.at[3 * a + k], rsem.at[3 * a + k], dev))
    return cps


def _to_slot(a, n, i):
    return lax.dynamic_update_slice(jnp.zeros((n,) + a.shape, a.dtype), a[None], (i,) + (0,) * a.ndim)


def _split_start(srcs, land_shapes, n_cp, make, name, after=None):
    ns, nl = len(srcs), len(land_shapes)
    n_in = ns + nl + (after is not None)

    def body(*refs):
        s_in = refs[:ns]
        ssem, rsem = refs[n_in], refs[n_in + 1]
        s_out = refs[n_in + 2:n_in + 2 + ns]
        l_out = refs[n_in + 2 + ns:n_in + 2 + ns + nl]
        token = refs[n_in + 2 + ns + nl]
        for cp in make(s_in, l_out if nl else s_out, ssem, rsem):
            cp.start()
        token[...] = jnp.zeros_like(token)

    lands = [pltpu.with_memory_space_constraint(lax.empty(sh.shape, sh.dtype), pltpu.HBM) for sh in land_shapes]
    out_shape = (pltpu.SemaphoreType.DMA((n_cp,)), pltpu.SemaphoreType.DMA((n_cp,)),
                 *[pltpu.HBM(b.shape, b.dtype) for b in srcs], *[pltpu.HBM(b.shape, b.dtype) for b in land_shapes],
                 SDS((8, 128), f32))
    return _hbm_call(
        body, name=name, out_shape=out_shape, in_specs=[HBM_SPEC] * (ns + nl) + [ANY_SPEC] * (after is not None),
        out_specs=(SEM_SPEC, SEM_SPEC, *[HBM_SPEC] * (ns + nl), VMEM_SPEC),
        input_output_aliases={i: 2 + i for i in range(ns + nl)}, compiler_params=_cp(has_side_effects=DATAFLOW),
    )(*[pltpu.with_memory_space_constraint(b, pltpu.HBM) for b in srcs], *lands, *([] if after is None else [after]))


def _split_wait(ssem, rsem, srcs, lands, after, make, name):
    ns, nl = len(srcs), len(lands)

    def body(*refs):
        s_in, l_in = refs[:ns], refs[ns:ns + nl]
        ssem_ref, rsem_ref = refs[ns + nl], refs[ns + nl + 1]
        for cp in make(s_in, l_in if nl else s_in, ssem_ref, rsem_ref):
            cp.wait_send()
            cp.wait_recv()

    outs = _hbm_call(
        body, name=name, out_shape=tuple(pltpu.HBM(b.shape, b.dtype) for b in (*srcs, *lands)),
        in_specs=[HBM_SPEC] * (ns + nl) + [SEM_SPEC, SEM_SPEC, ANY_SPEC], out_specs=tuple([HBM_SPEC] * (ns + nl)),
        input_output_aliases={i: i for i in range(ns + nl)}, compiler_params=_cp(has_side_effects=DATAFLOW),
    )(*srcs, *lands, ssem, rsem, after)
    return list(outs[:ns]), list(outs[ns:])


def _adaln_shard(cc, w_ada_shard):
    def body(c_ref, w_ref, m_ref, sc_ref):
        sc = _silu(c_ref[...])
        sc_ref[...] = sc
        m_ref[...] = jnp.dot(sc, w_ref[...], precision=HIGHEST, preferred_element_type=f32)

    return _hbm_call(
        body, name="adaln_shard", out_shape=(SDS((16, w_ada_shard.shape[1]), f32), SDS((16, D), f32)),
        in_specs=[VMEM_SPEC, VMEM_SPEC], out_specs=(VMEM_SPEC, VMEM_SPEC), compiler_params=_cp(32),
    )(cc, w_ada_shard)


def _prenorm(xx, norm_g, mrow, b_ada, tm, name, after=None):
    n = xx.shape[0]

    def norm(x, g, m, b):
        shift = m[:, 0:D] + b[:, 0:D]
        scale = m[:, D:2 * D] + b[:, D:2 * D]
        r = lax.rsqrt(jnp.mean(x * x, axis=-1, keepdims=True) + RMS_EPS)
        y = (x * r) * g
        return ((y * (1.0 + scale) + shift).astype(bf16),)

    band = lambda ref, k, s: ref.at[k * tm:(k + 1) * tm]
    whole = lambda ref, k, s: ref
    jvec = jnp.zeros((1,), i32)
    return _stream_call(
        norm, n // tm, [(xx, (tm, D), band), (norm_g, (1, D), whole), (mrow, (1, 3 * D), whole),
                        (b_ada, (1, 3 * D), whole)],
        [((n, D), bf16, (tm, D), band)], jvec, name, after)[0]


def _in_proj_own(h, w_own, jvec):
    tm = 512

    def body(j_ref, h_ref, w_ref, p_ref):
        p_ref[...] = jnp.dot(h_ref[...], w_ref[...].astype(bf16), preferred_element_type=f32)

    grid_spec = pltpu.PrefetchScalarGridSpec(
        num_scalar_prefetch=1, grid=(S // tm,),
        in_specs=[pl.BlockSpec((tm, D), lambda i, j: (i, 0)), pl.BlockSpec((D, D), lambda i, j: (0, 0))],
        out_specs=pl.BlockSpec((tm, D), lambda i, j: (i, j[0])))
    return _hbm_call(body, name="in_proj_own", out_shape=SDS((S, 4 * D), f32), grid_spec=grid_spec,
                     compiler_params=_cp(40))(jvec, h, w_own)


def _in_proj_block(h, w4, p, bvec, name, after=None):
    tm = 512

    def body(b_ref, h_ref, w_ref, p_in_ref, after_ref, p_ref):
        p_ref[...] = jnp.dot(h_ref[...], w_ref[...], preferred_element_type=f32)

    grid_spec = pltpu.PrefetchScalarGridSpec(
        num_scalar_prefetch=1, grid=(S // tm,),
        in_specs=[pl.BlockSpec((tm, D), lambda i, b: (i, 0)), pl.BlockSpec((None, D, D), lambda i, b: (b[0], 0, 0)),
                  ANY_SPEC, ANY_SPEC],
        out_specs=pl.BlockSpec((tm, D), lambda i, b: (i, b[0])))
    return _hbm_call(body, name=name, out_shape=SDS((S, 4 * D), f32), grid_spec=grid_spec,
                     input_output_aliases={3: 0})(bvec, h, w4, p, bvec if after is None else after)


def _ctx_proj(hc, w4):
    def body(h_ref, w0_ref, w1_ref, p_ref):
        hv = h_ref[...]
        p_ref[:, 0:DA] = jnp.dot(hv, w0_ref[:, DA:2 * DA], preferred_element_type=f32)
        p_ref[:, DA:2 * DA] = jnp.dot(hv, w1_ref[:, 0:DA], preferred_element_type=f32)

    return _hbm_call(
        body, name="ctx_proj", out_shape=SDS((L, 2 * DA), f32), grid=(1,),
        in_specs=[pl.BlockSpec((L, D), lambda i: (0, 0)), pl.BlockSpec((None, D, D), lambda i: (0, 0, 0)),
                  pl.BlockSpec((None, D, D), lambda i: (1, 0, 0))],
        out_specs=pl.BlockSpec((L, 2 * DA), lambda i: (0, 0)),
    )(hc, w4, w4)


def _head_ones():
    r = lax.broadcasted_iota(i32, (DA, DA), 0) // DH
    c = lax.broadcasted_iota(i32, (DA, DA), 1) // DH
    return (r == c).astype(bf16)


def _head_sum(v, ones_bd):
    hi = v.astype(bf16)
    lo = (v - hi.astype(f32)).astype(bf16)
    return jnp.dot(hi, ones_bd, preferred_element_type=f32) + jnp.dot(lo, ones_bd, preferred_element_type=f32)


def _swap16(v):
    lane = lax.broadcasted_iota(i32, v.shape, 1)
    return jnp.where((lane & 31) < 16, pltpu.roll(v, DA - 16, 1), pltpu.roll(v, 16, 1))


def _rope_block(ct_ref, rt_ref, tm):
    rows = [jnp.tile(rt_ref[8 * j:8 * j + 8, :], (GW // 8, 1)) for j in range(tm // GW)]
    return jnp.tile(ct_ref[...], (tm // GW, 1)) + jnp.concatenate(rows, axis=0)


def _rope_specs(tm):
    col = pl.BlockSpec((GW, DA), lambda i: (0, 0))
    row = pl.BlockSpec((8 * tm // GW, DA), lambda i: (i, 0))
    return [col, row, col, row]


def _qk_prep(p, gq, gk, rope):
    tm = 512

    def body(qk_ref, v_ref, gq_ref, gk_ref, cc_ref, cr_ref, sc_ref, sr_ref, qr_ref, qp_ref, kr_ref, vh_ref):
        ones_bd = _head_ones()
        cs, sn = _rope_block(cc_ref, cr_ref, tm), _rope_block(sc_ref, sr_ref, tm)
        q = qk_ref[:, 0:DA]
        k = qk_ref[:, DA:2 * DA]
        yq = (q * lax.rsqrt(_head_sum(q * q, ones_bd) * (1.0 / DH) + RMS_EPS)) * gq_ref[...]
        yk = (k * lax.rsqrt(_head_sum(k * k, ones_bd) * (1.0 / DH) + RMS_EPS)) * gk_ref[...]
        qr = (yq * cs + _swap16(yq) * sn) * QK_SCALE
        qp = yq * QK_SCALE
        kr = yk * cs + _swap16(yk) * sn
        vv = v_ref[...]
        for hh in range(H):
            sl = slice(hh * DH, (hh + 1) * DH)
            qr_ref[hh] = qr[:, sl].astype(bf16)
            qp_ref[hh] = qp[:, sl].astype(bf16)
            kr_ref[hh] = kr[:, sl].astype(bf16)
            vh_ref[hh] = vv[:, sl].astype(bf16)

    hm = SDS((H, S, DH), bf16)
    hspec = pl.BlockSpec((H, tm, DH), lambda i: (0, i, 0))
    fixed = lambda i: (0, 0)
    return _hbm_call(
        body, name="qk_prep", out_shape=(hm, hm, hm, hm), grid=(S // tm,),
        in_specs=[pl.BlockSpec((tm, 2 * DA), lambda i: (i, 0)), pl.BlockSpec((tm, DA), lambda i: (i, 2)),
                  pl.BlockSpec((1, DA), fixed), pl.BlockSpec((1, DA), fixed)] + _rope_specs(tm),
        out_specs=(hspec, hspec, hspec, hspec),
    )(p, p, gq, gk, *rope)


def _ctx_prep(pc, gk):
    def body(p_ref, gk_ref, kc_ref, vc_ref):
        ones_bd = _head_ones()
        k = p_ref[:, 0:DA]
        yk = (k * lax.rsqrt(_head_sum(k * k, ones_bd) * (1.0 / DH) + RMS_EPS)) * gk_ref[...]
        vv = p_ref[:, DA:2 * DA]
        for hh in range(H):
            sl = slice(hh * DH, (hh + 1) * DH)
            kc_ref[hh] = yk[:, sl].astype(bf16)
            vc_ref[hh] = vv[:, sl].astype(bf16)

    hm = SDS((H, L, DH), bf16)
    return _hbm_call(
        body, name="ctx_prep", out_shape=(hm, hm), in_specs=[VMEM_SPEC, VMEM_SPEC], out_specs=(VMEM_SPEC, VMEM_SPEC),
    )(pc, gk)


def _tile_pieces():
    out = []
    for (i0, u0) in TILE_GEOM:
        rows = []
        for j in range(2):
            i = i0 + j
            rs = _row_start(i)
            rows.append([(u0 + u - i + WIN_H - 1) if rs <= u0 + u < rs + WIN_H else None for u in range(KR)])
        out.append(rows)
    return out


def _bias_prep(rpb_rev_pad, after=None):
    pieces = _tile_pieces()

    def body(r_ref, after_ref, o_ref):
        rp = r_ref[...]
        xs = jnp.concatenate([pltpu.roll(jnp.broadcast_to(rp[dr:dr + 1, :], (GW, 128)), 128 - (WIN_W - 1), 1,
                                         stride=1, stride_axis=0) for dr in range(N_DR)], axis=0)
        row = lax.broadcasted_iota(i32, xs.shape, 0)
        lane = lax.broadcasted_iota(i32, xs.shape, 1)
        k = row & (GW - 1)
        c0 = jnp.clip(lane - WIN_W // 2, 0, GW - WIN_W)
        xs = jnp.where((k >= c0) & (k < c0 + WIN_W), xs, NEG)
        neg = jnp.full((GW, GW), NEG, f32)
        for t in range(NT):
            for j in range(2):
                for u in range(KR):
                    dr = pieces[t][j][u]
                    piece = neg if dr is None else xs[dr * GW:(dr + 1) * GW, 0:GW]
                    o_ref[t, u * GW:(u + 1) * GW, j * GW:(j + 1) * GW] = piece

    return _hbm_call(
        body, name="bias_prep", out_shape=SDS((H, NT, KB, QB), f32), grid=(H,),
        in_specs=[pl.BlockSpec((None, N_DR, 128), lambda h: (h, 0, 0)), ANY_SPEC],
        out_specs=pl.BlockSpec((None, NT, KB, QB), lambda h: (h, 0, 0, 0)),
    )(rpb_rev_pad, rpb_rev_pad if after is None else after)


def _bias_tiles(rpb2, after=None):
    return _bias_prep(jnp.pad(rpb2[:, :, ::-1], ((0, 0), (0, 0), (0, 128 - N_DC))), after)


def _block_geom(b):
    qs = b * QB
    ks = min(max(2 * b - 4, 0), ROWS - KR) * GW
    t = b if b < 2 else (b - (NQB - NT) if b > NQB - 3 else 2)
    return qs, ks, t


def _tt(a, b):
    return lax.dot_general(a, b, (((1,), (1,)), ((), ())), preferred_element_type=f32)


def _tn(a, b):
    return lax.dot_general(a, b, (((0,), (0,)), ((), ())), preferred_element_type=f32)


def _softmax_t(s_lat, s_ctx):
    m = jnp.maximum(jnp.max(s_lat, axis=0, keepdims=True), jnp.max(s_ctx, axis=0, keepdims=True))
    e_lat = jnp.exp(s_lat - m)
    e_ctx = jnp.exp(s_ctx - m)
    inv = 1.0 / (jnp.sum(e_lat, axis=0, keepdims=True) + jnp.sum(e_ctx, axis=0, keepdims=True))
    return e_lat * inv, e_ctx * inv


def _staged(n_blocks, stages):
    held = [dict() for _ in stages]
    for step in range(n_blocks + len(stages) - 1):
        for s, fn in enumerate(stages):
            b = step - s
            if 0 <= b < n_blocks:
                held[s][b] = fn(b) if s == 0 else fn(b, held[s - 1].pop(b))


def _attn_fwd(qr, qp, kr, vh, kc, vc, btt):
    def body(qr_ref, qp_ref, kr_ref, v_ref, kc_ref, vc_ref, bt_ref, o_ref):
        kcv, vcv = kc_ref[...], vc_ref[...]

        def scores(b):
            qs, ks, t = _block_geom(b)
            return (_tt(kr_ref[ks:ks + KB, :], qr_ref[qs:qs + QB, :]) + bt_ref[t], _tt(kcv, qp_ref[qs:qs + QB, :]))

        def probs(b, sc):
            p_lat, p_ctx = _softmax_t(*sc)
            return p_lat.astype(bf16), p_ctx.astype(bf16)

        def values(b, p):
            qs, ks, _ = _block_geom(b)
            o_ref[qs:qs + QB, :] = _tn(p[0], v_ref[ks:ks + KB, :]) + _tn(p[1], vcv)

        _staged(NQB, (scores, probs, values))

    sq = pl.BlockSpec((None, S, DH), lambda h: (h, 0, 0))
    sc = pl.BlockSpec((None, L, DH), lambda h: (h, 0, 0))
    return _hbm_call(
        body, name="attn_fwd", out_shape=SDS((H, S, DH), f32), grid=(H,),
        in_specs=[sq, sq, sq, sq, sc, sc, pl.BlockSpec((None, NT, KB, QB), lambda h: (h, 0, 0, 0))],
        out_specs=sq, compiler_params=_cp(48),
    )(qr, qp, kr, vh, kc, vc, btt)


def _shift_rows(v, down):
    n = v.shape[0]
    row = lax.broadcasted_iota(i32, v.shape, 0)
    if down:
        return jnp.where(row == 0, 0.0, pltpu.roll(v, 1, 0))
    return jnp.where(row == n - 1, 0.0, pltpu.roll(v, n - 1, 0))


def _conv_specs():
    col = lambda off: pl.BlockSpec((S, 128), lambda i, off=off: (0, off + i))
    return [col(16), col(20), col(24), col(28), pl.BlockSpec((3, 128), lambda i: (0, i)),
            pl.BlockSpec((1, 128), lambda i: (0, i))]


def _conv_fwd(p, conv_w, conv_b, after=None):
    def body(u_ref, bg_ref, cg_ref, zc_ref, w_ref, b_ref, after_ref, o_ref):
        cu = cg_ref[...] * u_ref[...]
        cv = b_ref[...] + _shift_rows(cu, True) * w_ref[0:1, :]
        cv = cv + cu * w_ref[1:2, :]
        cv = cv + _shift_rows(cu, False) * w_ref[2:3, :]
        o_ref[...] = ((bg_ref[...] * cv) * _silu(zc_ref[...])).astype(bf16)

    return _hbm_call(
        body, name="conv_fwd", out_shape=SDS((S, DC), bf16), grid=(DC // 128,),
        in_specs=_conv_specs() + [ANY_SPEC], out_specs=pl.BlockSpec((S, 128), lambda i: (0, i)),
        compiler_params=_cp(40),
    )(p, p, p, p, conv_w, conv_b, conv_b if after is None else after)


DP_Q, DP_K, DP_V, DP_ZA, DP_U, DP_BG, DP_CG, DP_ZC = range(8)


def _out_proj_loss(o, p, conv_g, w_out, xx, tgt, mrow, b_ada):
    tm = 512

    def body(o_ref, za_ref, c_ref, w_ref, x_ref, t_ref, m_ref, b_ref,
             dy_ref, dconv_ref, dp_ref, do_ref, gwo_ref, dgate_ref, loss_ref):
        k = pl.program_id(0)

        @pl.when(k == 0)
        def _():
            gwo_ref[...] = jnp.zeros_like(gwo_ref)
            dgate_ref[...] = jnp.zeros_like(dgate_ref)
            loss_ref[0, 0] = 0.0

        gate = m_ref[:, 2 * D:3 * D] + b_ref[:, 2 * D:3 * D]
        za = za_ref[...]
        sz = _silu(za)
        om = _merge_heads(o_ref)
        av, cv = (om * sz).astype(bf16), c_ref[...]
        mo = jnp.dot(av, w_ref[0:DA, :], preferred_element_type=f32)
        mo = mo + jnp.dot(cv, w_ref[DA:DA + DC, :], preferred_element_type=f32)
        y = x_ref[...] + gate * mo
        diff = y - t_ref[...]
        loss_ref[0, 0] += jnp.sum(diff * diff)
        dy = diff * (1.0 / D)
        dy_ref[...] = dy
        dgate_ref[...] += jnp.sum(dy * mo, axis=0, keepdims=True)
        dmo = (dy * gate).astype(bf16)
        dmix = _tt(dmo, w_ref[...])
        dattn = dmix[:, 0:DA]
        dconv_ref[...] = dmix[:, DA:DA + DC]
        a = dattn * sz
        for hh in range(H):
            do_ref[hh] = a[:, hh * DH:(hh + 1) * DH].astype(bf16)
        dp_ref[...] = ((dattn * _dsilu(za)) * om).astype(bf16)
        gwo_ref[0:DA, :] += _tn(av, dmo)
        gwo_ref[DA:DA + DC, :] += _tn(cv, dmo)

    row = lambda i: (i, 0)
    fixed = lambda i: (0, 0)
    hspec = pl.BlockSpec((H, tm, DH), lambda i: (0, i, 0))
    return _hbm_call(
        body, name="out_proj_loss",
        out_shape=(SDS((S, D), f32), SDS((S, DC), f32), SDS((8, S, DA), bf16), SDS((H, S, DH), bf16),
                   SDS((D, D), f32), SDS((1, D), f32), SDS((1, 1), f32)),
        grid=(S // tm,),
        in_specs=[hspec, pl.BlockSpec((tm, DA), lambda i: (i, 3)), pl.BlockSpec((tm, DC), row),
                  pl.BlockSpec((D, D), fixed), pl.BlockSpec((tm, D), row), pl.BlockSpec((tm, D), row),
                  pl.BlockSpec((1, 3 * D), fixed), pl.BlockSpec((1, 3 * D), fixed)],
        out_specs=(pl.BlockSpec((tm, D), row), pl.BlockSpec((tm, DC), row),
                   pl.BlockSpec((None, tm, DA), lambda i: (DP_ZA, i, 0)), hspec, pl.BlockSpec((D, D), fixed),
                   pl.BlockSpec((1, D), fixed), SMEM_SPEC),
        compiler_params=_cp(56, dimension_semantics=("arbitrary",)),
    )(o, p, conv_g, w_out, xx, tgt, mrow, b_ada)


def _conv_bwd(dconv, p, conv_w, conv_b, dp8, after=None):
    def body(d_ref, u_ref, bg_ref, cg_ref, zc_ref, w_ref, b_ref, dp_in_ref, after_ref, dp_ref, gw_ref, gb_ref):
        du_ref, dbg_ref, dcg_ref, dzc_ref = dp_ref.at[0], dp_ref.at[1], dp_ref.at[2], dp_ref.at[3]
        dconv = d_ref[...]
        u, bg, cg, zc = u_ref[...], bg_ref[...], cg_ref[...], zc_ref[...]
        w0, w1, w2 = w_ref[0:1, :], w_ref[1:2, :], w_ref[2:3, :]
        cu = cg * u
        cu_m, cu_p = _shift_rows(cu, True), _shift_rows(cu, False)
        cv = b_ref[...] + cu_m * w0
        cv = cv + cu * w1
        cv = cv + cu_p * w2
        sz = _silu(zc)
        dbg_ref[...] = ((dconv * sz) * cv).astype(bf16)
        dzc_ref[...] = ((dconv * (bg * cv)) * _dsilu(zc)).astype(bf16)
        dcv = (dconv * sz) * bg
        gb_ref[...] = jnp.sum(dcv, axis=0, keepdims=True)
        gw_ref[0:1, :] = jnp.sum(dcv * cu_m, axis=0, keepdims=True)
        gw_ref[1:2, :] = jnp.sum(dcv * cu, axis=0, keepdims=True)
        gw_ref[2:3, :] = jnp.sum(dcv * cu_p, axis=0, keepdims=True)
        gw_ref[3:8, :] = jnp.zeros((5, 128), f32)
        dcu = _shift_rows(dcv, False) * w0 + dcv * w1 + _shift_rows(dcv, True) * w2
        dcg_ref[...] = (dcu * u).astype(bf16)
        du_ref[...] = (dcu * cg).astype(bf16)

    return _hbm_call(
        body, name="conv_bwd", out_shape=(SDS((8, S, DC), bf16), SDS((8, DC), f32), SDS((1, DC), f32)),
        grid=(DC // 128,),
        in_specs=[pl.BlockSpec((S, 128), lambda i: (0, i))] + _conv_specs() + [ANY_SPEC, ANY_SPEC],
        out_specs=(pl.BlockSpec((4, S, 128), lambda i: (DP_U // 4, 0, i)), pl.BlockSpec((8, 128), lambda i: (0, i)),
                   pl.BlockSpec((1, 128), lambda i: (0, i))),
        input_output_aliases={7: 0}, compiler_params=_cp(48),
    )(dconv, p, p, p, p, conv_w, conv_b, dp8, conv_b if after is None else after)


def _attn_bwd(qr, qp, kr, vh, kc, vc, btt, do, after=None):
    def body(qr_ref, qp_ref, kr_ref, v_ref, kc_ref, vc_ref, bt_ref, do_ref, after_ref,
             dqr_ref, dqp_ref, dkr_ref, dv_ref, dkc_ref, dvc_ref, dbt_ref):
        kcv, vcv = kc_ref[...], vc_ref[...]
        dkr_ref[...] = jnp.zeros_like(dkr_ref)
        dv_ref[...] = jnp.zeros_like(dv_ref)
        dbt_ref[...] = jnp.zeros_like(dbt_ref)
        ctx_acc = {}

        def products(b):
            qs, ks, t = _block_geom(b)
            dob = do_ref[qs:qs + QB, :]
            s_lat = _tt(kr_ref[ks:ks + KB, :], qr_ref[qs:qs + QB, :]) + bt_ref[t]
            s_ctx = _tt(kcv, qp_ref[qs:qs + QB, :])
            return s_lat, s_ctx, _tt(v_ref[ks:ks + KB, :], dob), _tt(vcv, dob)

        def score_grads(b, x):
            s_lat, s_ctx, dp_lat, dp_ctx = x
            p_lat, p_ctx = _softmax_t(s_lat, s_ctx)
            delta = jnp.sum(p_lat * dp_lat, axis=0, keepdims=True) + jnp.sum(p_ctx * dp_ctx, axis=0, keepdims=True)
            ds_lat = p_lat * (dp_lat - delta)
            ds_ctx = p_ctx * (dp_ctx - delta)
            return ds_lat, ds_lat.astype(bf16), ds_ctx.astype(bf16), p_lat.astype(bf16), p_ctx.astype(bf16)

        def operand_grads(b, y):
            qs, ks, t = _block_geom(b)
            ds_lat, dsb_lat, dsb_ctx, pb_lat, pb_ctx = y
            qrb, qpb, dob = qr_ref[qs:qs + QB, :], qp_ref[qs:qs + QB, :], do_ref[qs:qs + QB, :]
            dbt_ref[t] += ds_lat
            dqr_ref[qs:qs + QB, :] = _tn(dsb_lat, kr_ref[ks:ks + KB, :])
            dqp_ref[qs:qs + QB, :] = _tn(dsb_ctx, kcv)
            dkr_ref[ks:ks + KB, :] += jnp.dot(dsb_lat, qrb, preferred_element_type=f32)
            dv_ref[ks:ks + KB, :] += jnp.dot(pb_lat, dob, preferred_element_type=f32)
            dkc = jnp.dot(dsb_ctx, qpb, preferred_element_type=f32)
            dvc = jnp.dot(pb_ctx, dob, preferred_element_type=f32)
            ctx_acc["k"] = dkc if b == 0 else ctx_acc["k"] + dkc
            ctx_acc["v"] = dvc if b == 0 else ctx_acc["v"] + dvc

        _staged(NQB, (products, score_grads, operand_grads))
        dkc_ref[...] = ctx_acc["k"]
        dvc_ref[...] = ctx_acc["v"]

    sq = pl.BlockSpec((None, S, DH), lambda h: (h, 0, 0))
    sc = pl.BlockSpec((None, L, DH), lambda h: (h, 0, 0))
    sb = pl.BlockSpec((None, NT, KB, QB), lambda h: (h, 0, 0, 0))
    big, ctxs = SDS((H, S, DH), f32), SDS((H, L, DH), f32)
    return _hbm_call(
        body, name="attn_bwd", out_shape=(big, big, big, big, ctxs, ctxs, SDS((H, NT, KB, QB), f32)), grid=(H,),
        in_specs=[sq, sq, sq, sq, sc, sc, sb, sq, ANY_SPEC], out_specs=(sq, sq, sq, sq, sc, sc, sb),
        compiler_params=_cp(56),
    )(qr, qp, kr, vh, kc, vc, btt, do, do if after is None else after)


def _bias_bwd(dbtt, after=None):
    pieces = _tile_pieces()

    def body(d_ref, after_ref, o_ref, scr):
        scr[...] = jnp.zeros_like(scr)
        acc = [None] * N_DR
        for t in range(NT):
            for j in range(2):
                for u in range(KR):
                    dr = pieces[t][j][u]
                    if dr is None:
                        continue
                    piece = d_ref[t, u * GW:(u + 1) * GW, j * GW:(j + 1) * GW]
                    acc[dr] = piece if acc[dr] is None else acc[dr] + piece
        a = lax.broadcasted_iota(i32, (GW, GW), 0)
        b = lax.broadcasted_iota(i32, (GW, GW), 1)
        flip = (a + b == GW - 1).astype(f32)
        for dr in range(N_DR):
            scr[dr * GW:(dr + 1) * GW, 0:GW] = jnp.dot(acc[dr], flip, precision=HIGHEST, preferred_element_type=f32)
        xs = jnp.concatenate([pltpu.roll(scr[dr * GW:(dr + 1) * GW, :], 128 + (WIN_W - 1) - (GW - 1), 1,
                                         stride=1, stride_axis=0) for dr in range(N_DR)], axis=0)
        tot = jnp.sum(xs.reshape(N_DR, GW, 128), axis=1)
        lane = lax.broadcasted_iota(i32, tot.shape, 1)
        o_ref[...] = jnp.where(lane < N_DC, tot, 0.0)

    return _hbm_call(
        body, name="bias_bwd", out_shape=SDS((H, N_DR, 128), f32), grid=(H,),
        in_specs=[pl.BlockSpec((None, NT, KB, QB), lambda h: (h, 0, 0, 0)), ANY_SPEC],
        out_specs=pl.BlockSpec((None, N_DR, 128), lambda h: (h, 0, 0)),
        scratch_shapes=[pltpu.VMEM((N_DR * GW, 128), f32)],
    )(dbtt, dbtt if after is None else after)


def _merge_heads(ref):
    return jnp.concatenate([ref[hh] for hh in range(H)], axis=1)


def _head_norm_bwd(xraw, gain, dy, ones_bd):
    r = lax.rsqrt(_head_sum(xraw * xraw, ones_bd) * (1.0 / DH) + RMS_EPS)
    xh = xraw * r
    gdy = dy * gain
    dx = r * (gdy - xh * (_head_sum(xh * gdy, ones_bd) * (1.0 / DH)))
    return dx, jnp.sum(dy * xh, axis=0, keepdims=True)


def _qk_bwd(dqr, dqp, dkr, dvh, p, gq, gk, rope, dp8):
    tm = 512

    def body(dqr_ref, dqp_ref, dkr_ref, dv_ref, qk_ref, gq_ref, gk_ref, cc_ref, cr_ref, sc_ref, sr_ref, dp_in_ref,
             dp_ref, ggq_ref, ggk_ref):
        dq_ref, dk_ref, dvo_ref = dp_ref.at[DP_Q], dp_ref.at[DP_K], dp_ref.at[DP_V]

        @pl.when(pl.program_id(0) == 0)
        def _():
            ggq_ref[...] = jnp.zeros_like(ggq_ref)
            ggk_ref[...] = jnp.zeros_like(ggk_ref)

        ones_bd = _head_ones()
        cs, sn = _rope_block(cc_ref, cr_ref, tm), _rope_block(sc_ref, sr_ref, tm)
        a = _merge_heads(dqr_ref)
        dyq = ((a * cs - _swap16(a) * sn) + _merge_heads(dqp_ref)) * QK_SCALE
        bk = _merge_heads(dkr_ref)
        dyk = bk * cs - _swap16(bk) * sn
        dq, gq_part = _head_norm_bwd(qk_ref[:, 0:DA], gq_ref[...], dyq, ones_bd)
        dk, gk_part = _head_norm_bwd(qk_ref[:, DA:2 * DA], gk_ref[...], dyk, ones_bd)
        dq_ref[...] = dq.astype(bf16)
        dk_ref[...] = dk.astype(bf16)
        dvo_ref[...] = _merge_heads(dv_ref).astype(bf16)
        ggq_ref[...] += gq_part
        ggk_ref[...] += gk_part

    hspec = pl.BlockSpec((H, tm, DH), lambda i: (0, i, 0))
    fixed = pl.BlockSpec((1, DA), lambda i: (0, 0))
    return _hbm_call(
        body, name="qk_bwd", out_shape=(SDS((8, S, DA), bf16), SDS((1, DA), f32), SDS((1, DA), f32)), grid=(S // tm,),
        in_specs=[hspec, hspec, hspec, hspec, pl.BlockSpec((tm, 2 * DA), lambda i: (i, 0)), fixed, fixed]
        + _rope_specs(tm) + [ANY_SPEC],
        out_specs=(pl.BlockSpec((3, tm, DA), lambda i: (0, i, 0)), fixed, fixed), input_output_aliases={11: 0},
        compiler_params=_cp(40, dimension_semantics=("arbitrary",)),
    )(dqr, dqp, dkr, dvh, p, gq, gk, *rope, dp8)


def _ctx_bwd(dkc, dvc, pc, gk):
    def body(dkc_ref, dvc_ref, p_ref, gk_ref, dk_ref, dv_ref, ggk_ref):
        ones_bd = _head_ones()
        dk, gk_part = _head_norm_bwd(p_ref[:, 0:DA], gk_ref[...], _merge_heads(dkc_ref), ones_bd)
        dk_ref[...] = dk.astype(bf16)
        dv_ref[...] = _merge_heads(dvc_ref).astype(bf16)
        ggk_ref[...] = gk_part

    piece = SDS((L, DA), bf16)
    return _hbm_call(
        body, name="ctx_bwd", out_shape=(piece, piece, SDS((1, DA), f32)), in_specs=[VMEM_SPEC] * 4,
        out_specs=(VMEM_SPEC,) * 3,
    )(dkc, dvc, pc, gk)


def _grad_w_in(h, dp8, hc, dkc_raw, dvc_m, half, name, after=None):
    def body(half_ref, h_ref, p_ref, hc_ref, dk_ref, dv_ref, after_ref, g_ref):
        j = pl.program_id(0)
        hv = h_ref[...]
        g_ref[:, 0:DA] = _tn(hv, p_ref[0])
        g_ref[:, DA:2 * DA] = _tn(hv, p_ref[1])

        @pl.when(j == 0)
        def _():
            g_ref[:, DA:2 * DA] += _tn(hc_ref[...], dk_ref[...])

        @pl.when(j == 1)
        def _():
            g_ref[:, 0:DA] += _tn(hc_ref[...], dv_ref[...])

    fixed = lambda j, s: (0, 0)
    hd = D // 2
    grid_spec = pltpu.PrefetchScalarGridSpec(
        num_scalar_prefetch=1, grid=(4,),
        in_specs=[pl.BlockSpec((S, hd), lambda j, s: (0, s[0])), pl.BlockSpec((2, S, DA), lambda j, s: (j, 0, 0)),
                  pl.BlockSpec((L, hd), lambda j, s: (0, s[0])), pl.BlockSpec((L, DA), fixed),
                  pl.BlockSpec((L, DA), fixed), ANY_SPEC],
        out_specs=pl.BlockSpec((None, hd, D), lambda j, s: (j, 0, 0)))
    return _hbm_call(
        body, name=name, out_shape=SDS((4, hd, D), f32), grid_spec=grid_spec, compiler_params=_cp(40),
    )(half, h, dp8, hc, dkc_raw, dvc_m, half if after is None else after)


def _grad_w_in_pair_sum(h, dp8, hc, dkc_raw, dvc_m, half, sent, landed, ssem, rsem):
    def body(half_ref, h_ref, p_ref, hc_ref, dk_ref, dv_ref, sent_ref, land_ref, ssem_ref, rsem_ref,
             t32_ref, tb_ref, buf, lsem):
        j = pl.program_id(0)
        hv = h_ref[...]
        t32_ref[:, 0:DA] = _tn(hv, p_ref[0])
        x, y, c = _my_pos()
        arrival = pltpu.make_async_remote_copy(src_ref=sent_ref.at[j], dst_ref=land_ref.at[j], send_sem=ssem_ref.at[j],
                                               recv_sem=rsem_ref.at[j], device_id=(x, y, 1 - c), device_id_type=MESH)
        arrival.wait_recv()
        arrival.wait_send()
        load = pltpu.make_async_copy(land_ref.at[j], buf, lsem)
        load.start()
        t32_ref[:, DA:2 * DA] = _tn(hv, p_ref[1])

        @pl.when(j == 0)
        def _():
            t32_ref[:, DA:2 * DA] += _tn(hc_ref[...], dk_ref[...])

        @pl.when(j == 1)
        def _():
            t32_ref[:, 0:DA] += _tn(hc_ref[...], dv_ref[...])

        load.wait()
        t = t32_ref[...] + buf[...]
        t32_ref[...] = t
        tb_ref[...] = t.astype(bf16)

    fixed = lambda j, s: (0, 0)
    hd = D // 2
    out_spec = pl.BlockSpec((None, hd, D), lambda j, s: (j, 0, 0))
    grid_spec = pltpu.PrefetchScalarGridSpec(
        num_scalar_prefetch=1, grid=(4,),
        in_specs=[pl.BlockSpec((S, hd), lambda j, s: (0, s[0])), pl.BlockSpec((2, S, DA), lambda j, s: (j, 0, 0)),
                  pl.BlockSpec((L, hd), lambda j, s: (0, s[0])), pl.BlockSpec((L, DA), fixed),
                  pl.BlockSpec((L, DA), fixed), HBM_SPEC, HBM_SPEC, SEM_SPEC, SEM_SPEC],
        out_specs=(out_spec, out_spec), scratch_shapes=[pltpu.VMEM((hd, D), f32), pltpu.SemaphoreType.DMA])
    return _hbm_call(
        body, name="grad_w_in_pair_sum", out_shape=(SDS((4, hd, D), f32), SDS((4, hd, D), bf16)), grid_spec=grid_spec,
        compiler_params=_cp(40, has_side_effects=DATAFLOW),
    )(half, h, dp8, hc, dkc_raw, dvc_m, sent, landed, ssem, rsem)


def _norm_mod_bwd(x, dh, g, scale):
    r = lax.rsqrt(jnp.mean(x * x, axis=-1, keepdims=True) + RMS_EPS)
    xh = x * r
    y = xh * g
    dshift = jnp.sum(dh, axis=0, keepdims=True)
    dscale = jnp.sum(dh * y, axis=0, keepdims=True)
    dyn = dh * (1.0 + scale)
    dg = jnp.sum(dyn * xh, axis=0, keepdims=True)
    gdy = dyn * g
    dx = r * (gdy - xh * jnp.mean(xh * gdy, axis=-1, keepdims=True))
    return dx, dshift, dscale, dg


def _dh_grad_x(dp8, w4, xx, dy, norm_g, mrow, b_ada, after=None):
    tm = 512

    def body(p_ref, w_ref, x_ref, dy_ref, g_ref, m_ref, b_ref, after_ref, gx_ref, dsh_ref, dsc_ref, dg_ref):
        @pl.when(pl.program_id(0) == 0)
        def _():
            dsh_ref[...] = jnp.zeros_like(dsh_ref)
            dsc_ref[...] = jnp.zeros_like(dsc_ref)
            dg_ref[...] = jnp.zeros_like(dg_ref)

        dh = None
        for j in range(4):
            for half in range(2):
                term = _tt(p_ref[2 * j + half], w_ref[j, :, half * DA:(half + 1) * DA])
                dh = term if dh is None else dh + term
        scale = m_ref[:, D:2 * D] + b_ref[:, D:2 * D]
        dx, dshift, dscale, dg = _norm_mod_bwd(x_ref[...], dh, g_ref[...], scale)
        gx_ref[...] = dy_ref[...] + dx
        dsh_ref[...] += dshift
        dsc_ref[...] += dscale
        dg_ref[...] += dg

    row = lambda i: (i, 0)
    fixed = lambda i: (0, 0)
    vec = SDS((1, D), f32)
    return _hbm_call(
        body, name="dh_grad_x", out_shape=(SDS((S, D), f32), vec, vec, vec), grid=(S // tm,),
        in_specs=[pl.BlockSpec((8, tm, DA), lambda i: (0, i, 0)), pl.BlockSpec((4, D, D), lambda i: (0, 0, 0)),
                  pl.BlockSpec((tm, D), row), pl.BlockSpec((tm, D), row), pl.BlockSpec((1, D), fixed),
                  pl.BlockSpec((1, 3 * D), fixed), pl.BlockSpec((1, 3 * D), fixed), ANY_SPEC],
        out_specs=(pl.BlockSpec((tm, D), row), pl.BlockSpec((1, D), fixed), pl.BlockSpec((1, D), fixed),
                   pl.BlockSpec((1, D), fixed)),
        compiler_params=_cp(56, dimension_semantics=("arbitrary",)),
    )(dp8, w4, xx, dy, norm_g, mrow, b_ada, b_ada if after is None else after)


def _dhc_sums(dkc_raw, dvc_m, w4, ctx2, norm_g, mrow_c, b_ada, after=None):
    def body(dk_ref, dv_ref, w0_ref, w1_ref, x_ref, g_ref, m_ref, b_ref, after_ref, dsh_ref, dsc_ref, dg_ref):
        dh = _tt(dk_ref[...], w0_ref[:, DA:2 * DA]) + _tt(dv_ref[...], w1_ref[:, 0:DA])
        scale = m_ref[:, D:2 * D] + b_ref[:, D:2 * D]
        _, dshift, dscale, dg = _norm_mod_bwd(x_ref[...], dh, g_ref[...], scale)
        dsh_ref[...] = dshift
        dsc_ref[...] = dscale
        dg_ref[...] = dg

    fixed = lambda i: (0, 0)
    vec = SDS((1, D), f32)
    vspec = pl.BlockSpec((1, D), fixed)
    return _hbm_call(
        body, name="dhc_sums", out_shape=(vec, vec, vec), grid=(1,),
        in_specs=[pl.BlockSpec((L, DA), fixed), pl.BlockSpec((L, DA), fixed),
                  pl.BlockSpec((None, D, D), lambda i: (0, 0, 0)), pl.BlockSpec((None, D, D), lambda i: (1, 0, 0)),
                  pl.BlockSpec((L, D), fixed), vspec, pl.BlockSpec((1, 3 * D), fixed), pl.BlockSpec((1, 3 * D), fixed),
                  ANY_SPEC],
        out_specs=(vspec, vspec, vspec), compiler_params=_cp(32),
    )(dkc_raw, dvc_m, w4, w4, ctx2, norm_g, mrow_c, b_ada, b_ada if after is None else after)


def _rope_tables():
    nf = DH // 4
    inv = np.float32(ROPE_THETA) ** (-np.arange(nf, dtype=np.float32) / np.float32(nf))
    ang_c = np.arange(GW, dtype=np.float32)[:, None] * inv
    ang_r = np.arange(ROWS, dtype=np.float32)[:, None] * inv
    zc, zr = np.zeros((GW, 2 * nf), np.float32), np.zeros((ROWS, 2 * nf), np.float32)
    ct_cos = np.tile(np.concatenate([zc, np.cos(ang_c), np.cos(ang_c)], axis=1), (1, H))
    ct_sin = np.tile(np.concatenate([zc, -np.sin(ang_c), np.sin(ang_c)], axis=1), (1, H))
    rt_cos = np.tile(np.concatenate([np.cos(ang_r), np.cos(ang_r), zr], axis=1), (1, H))
    rt_sin = np.tile(np.concatenate([-np.sin(ang_r), np.sin(ang_r), zr], axis=1), (1, H))
    rep8 = lambda t: np.ascontiguousarray(np.broadcast_to(t[:, None, :], (ROWS, 8, DA))).reshape(ROWS * 8, DA)
    return tuple(jnp.asarray(t, f32) for t in (ct_cos, rep8(rt_cos), ct_sin, rep8(rt_sin)))


def _local_step(xx, ctx2, tgt, mrow, mrow_c, b_ada, norm_g, weights, q_norm_g, k_norm_g, rpb2, conv_w_full, conv_b,
                hooks=None):
    hooks = hooks or {}
    gq = jnp.tile(q_norm_g, (1, H))
    gk = jnp.tile(k_norm_g, (1, H))
    rope = _rope_tables()

    h = _prenorm(xx, norm_g, mrow, b_ada, 256, "prenorm_x", after=weights.get("started"))
    jv = weights["jvec"]
    p = _in_proj_own(h, weights["own"], jv)
    btb = _bias_tiles(rpb2, after=p)
    w4, started = weights["near"](btb)
    p = _in_proj_block(h, w4, p, weights["first"], "in_proj_near", after=started)
    w4 = weights["near2"](w4, p)
    p = _in_proj_block(h, w4, p, weights["second"], "in_proj_near2")
    ctx_norm = {}

    def filler(token):
        ctx_norm["hc"] = _prenorm(ctx2, norm_g, mrow_c, b_ada, L // 2, "prenorm_ctx", after=token)
        return ctx_norm["hc"]

    w4, started = weights["far"](w4, p, filler)
    hc = ctx_norm["hc"]
    p = _in_proj_block(h, w4, p, jv ^ 3, "in_proj_far", after=started)
    pc = _ctx_proj(hc, w4)
    qr, qp, kr, vh = _qk_prep(p, gq, gk, rope)
    kc, vc = _ctx_prep(pc, gk)
    o = _attn_fwd(qr, qp, kr, vh, kc, vc, btb)
    started = weights["out_arrived"](o) if "out_arrived" in weights else None
    conv_g = _conv_fwd(p, conv_w_full, conv_b, after=started)
    w_out_full = weights["out"](conv_g)
    dy, dconv, dp8, do, g_w_out, dgate, loss_sum = _out_proj_loss(o, p, conv_g, w_out_full, xx, tgt, mrow, b_ada)
    started = hooks["g_w_out"](g_w_out) if "g_w_out" in hooks else None
    dp8, g_conv_w, g_conv_b = _conv_bwd(dconv, p, conv_w_full, conv_b, dp8, after=started)
    started = hooks["after_conv"](dp8) if "after_conv" in hooks else None
    dqr, dqp, dkr, dvh, dkc, dvc, dbtb = _attn_bwd(qr, qp, kr, vh, kc, vc, btb, do, after=started)
    dp8, g_gq, g_gk = _qk_bwd(dqr, dqp, dkr, dvh, p, gq, gk, rope, dp8)
    dkc_raw, dvc_m, g_gk_c = _ctx_bwd(dkc, dvc, pc, gk)
    first = hooks.get("first_half", jnp.zeros((1,), i32))
    g_first = _grad_w_in(h, dp8, hc, dkc_raw, dvc_m, first, "grad_w_in_first")
    started = hooks["g_w_in_first"](g_first) if "g_w_in_first" in hooks else None
    if "w_in_pair_sum" in hooks:
        g_second = None
        started = hooks["w_in_pair_sum"](functools.partial(_grad_w_in_pair_sum, h, dp8, hc, dkc_raw, dvc_m, 1 - first))
    else:
        g_second = _grad_w_in(h, dp8, hc, dkc_raw, dvc_m, 1 - first, "grad_w_in_second", after=started)
    dshift_c, dscale_c, dng_c = _dhc_sums(dkc_raw, dvc_m, w4, ctx2, norm_g, mrow_c, b_ada, after=started)
    g_rpb = _bias_bwd(dbtb, after=dshift_c)
    grad_x, dshift, dscale, dng = _dh_grad_x(dp8, w4, xx, dy, norm_g, mrow, b_ada, after=g_rpb)
    return dict(loss_sum=loss_sum, grad_x=grad_x, g_w_in=(g_first, g_second), g_w_out=g_w_out, g_conv_w=g_conv_w,
                g_conv_b=g_conv_b, g_rpb=g_rpb, g_gq=g_gq, g_gk=g_gk, g_gk_c=g_gk_c, dshift=dshift, dscale=dscale,
                dgate=dgate, dng=dng, dshift_c=dshift_c, dscale_c=dscale_c, dng_c=dng_c)


def _pair_sum_w_out(g, r, cvec):
    hr = D // 8

    def add(own, other):
        t = own + other
        return t, t.astype(bf16)

    mine = lambda ref, q, c: ref.at[pl.ds(pl.multiple_of((2 * q + c) * hr, hr), hr)]
    block = lambda ref, q, c: ref.at[q]
    return _stream_call(add, 4, [(g, (hr, D), mine), (r, (hr, D), block)],
                        [((4, hr, D), f32, (hr, D), block), ((4, hr, D), bf16, (hr, D), block)], cvec, "pair_sum_w_out")


def _chip_sum(t32, r2, jvec, name):
    rows = t32.shape[1]
    tr = min(rows // 2, 128)
    band = lambda k: slice(k * tr, (k + 1) * tr)
    total =lambda t, r: (((t + r[0].astype(f32)) + r[1].astype(f32)) + r[2].astype(f32),)
    return _stream_call(
        total, rows // tr,
        [(t32, (tr, D), lambda ref, k, j: ref.at[j, band(k)]), (r2, (3, tr, D), lambda ref, k, j: ref.at[:, band(k)])],
        [((rows, D), f32, (tr, D), lambda ref, k, j: ref.at[band(k)])], jvec, name)[0]


_PK = {}
_off = 0
for _name, _rows in (("dm", 24), ("dmc", 24), ("dng", 8), ("dng_c", 8), ("gq", 8), ("gk", 8), ("gk_c", 8),
                     ("rpb", H * N_DR), ("conv_b", 8), ("conv_w", 16), ("loss", 8)):
    _PK[_name] = (_off, _off + _rows)
    _off += _rows
PK_ROWS = _off
RS_B_ADA, RS_NORM_G, RS_GQ, RS_GK, RS_RPB, RS_CONV_B, RS_CONV_W, RS_DMC, RS_LOSS, RS_ROWS = (
    0, 24, 32, 40, 48, 168, 176, 192, 216, 224)


def _small_reduce(gathered):
    def body(g_ref, o_ref, dm_ref):
        a0 = _PK["dm"][0]
        dm_ref[...] = jnp.zeros_like(dm_ref)
        for b in range(8):
            for i in range(24):
                dm_ref[b:b + 1, 128 * i:128 * (i + 1)] = g_ref[b, a0 + i:a0 + i + 1, :]
        tot = g_ref[0]
        for b in range(1, 8):
            tot = tot + g_ref[b]

        def rows(name):
            a, z = _PK[name]
            return tot[a:z]

        o_ref[RS_B_ADA:RS_B_ADA + 24] = rows("dm") + rows("dmc")
        o_ref[RS_NORM_G:RS_NORM_G + 8] = rows("dng") + rows("dng_c")
        gq = jnp.broadcast_to(jnp.sum(rows("gq"), axis=0, keepdims=True), (8, 128))
        gk = jnp.broadcast_to(jnp.sum(rows("gk") + rows("gk_c"), axis=0, keepdims=True), (8, 128))
        o_ref[RS_GQ:RS_GQ + 8] = gq + pltpu.roll(gq, DH, 1)
        o_ref[RS_GK:RS_GK + 8] = gk + pltpu.roll(gk, DH, 1)
        o_ref[RS_RPB:RS_RPB + H * N_DR] = rows("rpb")
        o_ref[RS_CONV_B:RS_CONV_B + 8] = rows("conv_b")
        o_ref[RS_CONV_W:RS_CONV_W + 16] = rows("conv_w")
        dmc = rows("dmc")
        o_ref[RS_DMC:RS_DMC + 24] = dmc
        o_ref[RS_LOSS:RS_LOSS + 8] = rows("loss")
        for i in range(24):
            dm_ref[8:9, 128 * i:128 * (i + 1)] = dmc[i:i + 1]

    return _hbm_call(body, name="small_reduce", out_shape=(SDS((RS_ROWS, 128), f32), SDS((16, 3 * D), f32)),
                     in_specs=[VMEM_SPEC], out_specs=(VMEM_SPEC, VMEM_SPEC))(gathered)


def _w_ada_grad(sc16, dm16, w_ada_shard, jvec):
    ncol = w_ada_shard.shape[1]

    def body(j_ref, sc_ref, dm_ref, w_ref, g_ref, part_ref):
        dm = dm_ref[...]
        g_ref[...] = lax.dot_general(sc_ref[...], dm, (((0,), (0,)), ((), ())), precision=HIGHEST,
                                     preferred_element_type=f32)
        part_ref[...] = lax.dot_general(dm[8:16], w_ref[...], (((1,), (1,)), ((), ())), precision=HIGHEST,
                                        preferred_element_type=f32)

    fixed = lambda i, j: (0, 0)
    grid_spec = pltpu.PrefetchScalarGridSpec(
        num_scalar_prefetch=1, grid=(1,),
        in_specs=[pl.BlockSpec((16, D), fixed), pl.BlockSpec((16, ncol), lambda i, j: (0, j[0])),
                  pl.BlockSpec((D, ncol), fixed)],
        out_specs=(pl.BlockSpec((D, ncol), fixed), pl.BlockSpec((8, D), fixed)))
    return _pallas_call(body, name="w_ada_grad", out_shape=(SDS((D, ncol), f32), SDS((8, D), f32)),
                        grid_spec=grid_spec, compiler_params=_cp(40))(jvec, sc16, dm16, w_ada_shard)


def _c_ctx_grad(parts4, c_ctx):
    def body(p_ref, c_ref, o_ref):
        tot = ((p_ref[0] + p_ref[1]) + p_ref[2]) + p_ref[3]
        o_ref[...] = tot[0:1] * _dsilu(c_ref[...].reshape(1, D))

    return _pallas_call(body, name="c_ctx_grad", out_shape=SDS((1, D), f32), in_specs=[VMEM_SPEC, VMEM_SPEC],
                        out_specs=VMEM_SPEC)(parts4, c_ctx)


def _adamw(w, g, m, v, name, after=None):
    return _adamw_stream(w, [g], m, v, None, name, after)


def _adamw_halves(w, g_mine, g_other, m, v, cvec, name, after=None):
    return _adamw_stream(w, [g_mine, g_other], m, v, cvec, name, after)


def _adamw_stream(w, g_parts, m, v, cvec, name, after):
    rows, cols = w.shape
    parts = len(g_parts)
    span = rows // parts
    chunk = min(128, span // 2)
    per = span // chunk
    outs_g = parts == 2
    pin = rows * cols * 4 > (1 << 20)

    def body(*refs):
        c_ref, w_ref = refs[0], refs[1]
        g_refs = refs[2:2 + parts]
        m_ref, v_ref = refs[2 + parts], refs[3 + parts]
        out_refs = refs[5 + parts:5 + parts + 3 + outs_g]
        bw, bg, bm, bv, bd, sem_in, sem_out = refs[5 + parts + 3 + outs_g:]
        c = c_ref[0]
        loads, stores = [], []
        for p in range(parts):
            first = (c if p == 0 else 1 - c) * span if parts == 2 else 0
            for kk in range(per):
                r = pl.ds(pl.multiple_of(first + kk * chunk, chunk), chunk)
                g_src = g_refs[p].at[kk * chunk:(kk + 1) * chunk]
                cps = [pltpu.make_async_copy(src, dst.at[r], sem_in.at[a, p, kk])
                       for a, (src, dst) in enumerate(((w_ref.at[r], bw), (g_src, bg), (m_ref.at[r], bm),
                                                       (v_ref.at[r], bv)))]
                for cp in cps:
                    cp.start(priority=kk % 2)
                loads.append((r, p, kk, cps))
        for r, p, kk, cps in loads:
            for cp in cps:
                cp.wait()
            bd[r], bm[r], bv[r] = _adam_math(bw[r], bg[r], bm[r], bv[r])
            for a, (src, dst) in enumerate(zip(([bg] if outs_g else []) + [bd, bm, bv], out_refs)):
                cp = pltpu.make_async_copy(src.at[r], dst.at[r], sem_out.at[a, p, kk])
                cp.start(priority=kk % 2)
                stores.append(cp)
        for cp in stores:
            cp.wait()

    shp = pltpu.HBM((rows, cols), f32) if pin else SDS((rows, cols), f32)
    spec = HBM_SPEC if pin else ANY_SPEC
    buf = pltpu.VMEM((rows, cols), f32)
    args = [pltpu.with_memory_space_constraint(a, pltpu.HBM) if pin else a for a in (w, *g_parts, m, v)]
    cvec = jnp.zeros((1,), i32) if cvec is None else cvec
    after_big = after is not None and after.size * 4 > (1 << 20)
    after = pltpu.with_memory_space_constraint(after, pltpu.HBM) if after_big else after
    return _pallas_call(
        body, name=name, out_shape=(shp,) * (3 + outs_g),
        in_specs=[SMEM_SPEC] + [spec] * (3 + parts) + [HBM_SPEC if after_big else ANY_SPEC],
        out_specs=(spec,) * (3 + outs_g),
        scratch_shapes=[buf] * 5 + [pltpu.SemaphoreType.DMA((4, parts, per)), pltpu.SemaphoreType.DMA((4, parts, per))],
        compiler_params=_cp(5 * rows * cols * 4 // (1 << 20) + 8),
    )(cvec, *args, cvec if after is None else after)


def _adam_math(w, g, m, v):
    m2 = ADAM_B1 * m + (1.0 - ADAM_B1) * g
    v2 = ADAM_B2 * v + (1.0 - ADAM_B2) * jnp.square(g)
    m_hat = m2 / (1.0 - ADAM_B1 ** ADAM_STEP)
    v_hat = v2 / (1.0 - ADAM_B2 ** ADAM_STEP)
    return -ADAM_LR * (m_hat / (jnp.sqrt(v_hat) + ADAM_EPS) + ADAM_WD * w), m2, v2


def _adamw_small(red, g_c_ctx, jvec, ws, ms, vs):
    n = len(ws)

    def body(*refs):
        red_ref, gc_ref, j_ref = refs[:3]
        w_refs, m_refs, v_refs = refs[3:3 + n], refs[3 + n:3 + 2 * n], refs[3 + 2 * n:3 + 3 * n]
        outs = refs[3 + 3 * n:]
        g_out, d_out, m_out, v_out = outs[:n], outs[n:2 * n], outs[2 * n:3 * n], outs[3 * n:]
        chip = j_ref[0]
        lanes = lambda i: (slice(None), slice(128 * i, 128 * (i + 1)))
        row = lambda r0, i: (lambda: red_ref[r0 + i:r0 + i + 1, :])
        whole = (slice(None), slice(None))
        chunks = [
            [((slice(None),), lambda: gc_ref[...].reshape(D))],
            [(lanes(i), row(RS_B_ADA, i)) for i in range(3 * D // 128)],
            [(lanes(i), row(RS_NORM_G, i)) for i in range(D // 128)],
            [(whole, lambda: red_ref[RS_GQ:RS_GQ + 1, 0:DH])],
            [(whole, lambda: red_ref[RS_GK:RS_GK + 1, 0:DH])],
            [((dr,), (lambda dr=dr: red_ref[pl.ds(RS_RPB + dr, H, stride=N_DR), 0:N_DC])) for dr in range(N_DR)],
            [((r,), (lambda r=r: red_ref[pl.ds(RS_CONV_W + 4 * r + chip, 1), :])) for r in range(3)],
            [(lanes(i), row(RS_CONV_B, i)) for i in range(DC // 128)],
        ]
        for a in range(n):
            for idx, grad in chunks[a]:
                g = grad()
                d, m2, v2 = _adam_math(w_refs[a][idx], g, m_refs[a][idx], v_refs[a][idx])
                g_out[a][idx] = g
                d_out[a][idx] = d
                m_out[a][idx] = m2
                v_out[a][idx] = v2

    shapes = [SDS(w.shape, f32) for w in ws]
    res = _pallas_call(body, name="adamw_small", out_shape=shapes * 4,
                       in_specs=[VMEM_SPEC, VMEM_SPEC, SMEM_SPEC] + [VMEM_SPEC] * (3 * n),
                       out_specs=[VMEM_SPEC] * (4 * n))(red, g_c_ctx, jvec, *ws, *ms, *vs)
    return [list(res[k * n:(k + 1) * n]) for k in range(4)]


def _rows128(a):
    return a.reshape(-1, 128)


def kernel(x, c, ctx, c_ctx, w_ada, b_ada, norm_g, w_in, q_norm_g, k_norm_g, rpb, conv_w, conv_b, w_out, loss_target, m_c_ctx, m_w_ada, m_b_ada, m_norm_g, m_w_in, m_q_norm_g, m_k_norm_g, m_rpb, m_conv_w, m_conv_b, m_w_out, v_c_ctx, v_w_ada, v_b_ada, v_norm_g, v_w_in, v_q_norm_g, v_k_norm_g, v_rpb, v_conv_w, v_conv_b, v_w_out):
    xi, yi, ci = lax.axis_index("x"), lax.axis_index("y"), lax.axis_index("c")
    dev = 4 * xi + 2 * yi + ci
    chip = 2 * xi + yi
    cvec = jnp.reshape(ci, (1,)).astype(i32)
    jvec = jnp.reshape(chip, (1,)).astype(i32)
    w_ada_s = w_ada[0]
    ncol = w_ada_s.shape[1]

    gc = _split_start([_to_slot(c.reshape(8, 128), 8, dev)], [], 7, _gather8_copies, "gather_c_start")
    wo4c = _cast_to_slot(w_out[0], jvec, "cast_w_out", after=gc[3])
    w4c = _cast_to_slot(w_in[0], jvec, "cast_w_in", after=wo4c)
    (c8,), _ = _split_wait(gc[0], gc[1], [gc[2]], [], w4c, _gather8_copies, "gather_c_wait")
    cc = jnp.concatenate([c8.reshape(8, D), c_ctx.reshape(1, D), jnp.zeros((7, D), f32)], axis=0)
    m_shard, sc16 = _adaln_shard(cc, w_ada_s)

    conv_w_pad = jnp.pad(conv_w[0], ((0, 5), (0, 0)))
    gm = _split_start([_to_slot(m_shard, 4, chip), _to_slot(conv_w_pad, 4, chip)], [], 6, _gather4_copies,
                      "gather_mod_start")

    all_k = [(0, 0, 0), (0, 0, 1), (0, 0, 2)]
    sem_a, rem_a, w4s, token = _split_start([w4c], [], 1, _near_copies, "weights_near_start", after=gm[4])
    (m4, cw4), _ = _split_wait(gm[0], gm[1], [gm[2], gm[3]], [], token, _gather4_copies, "gather_mod_wait")
    m_full = jnp.transpose(m4, (1, 0, 2)).reshape(16, 4 * ncol)
    mrow = lax.dynamic_slice(m_full, (dev, 0), (1, 3 * D))
    mrow_c = m_full[8:9]
    conv_w_full = jnp.transpose(cw4[:, 0:3, :], (1, 0, 2)).reshape(3, DC)
    waves = {}

    def near(after):
        (w4w,), _ = _split_wait(sem_a, rem_a, [w4s], [], after, _near_copies, "weights_near_wait")
        sem_b, rem_b, w4b, started = _split_start([w4w], [], 2, _pass_relay_copies, "weights_pass_start")
        waves["pass"] = (sem_b, rem_b)
        return w4b, started

    def near2(w4, after):
        (w4w,), _ = _split_wait(*waves["pass"], [w4], [], after, _pass_copy, "weights_pass_wait")
        return w4w

    def far(w4, after, filler):
        (w4w,), _ = _split_wait(*waves["pass"], [w4], [], after, _relay_copy, "weights_far_wait")
        w4x, sem_c, rem_c, wo4s, started = _forward_then_start(w4w, wo4c, all_k, "weights_far_forward_out_start")
        (w4f,), _ = _split_wait(sem_c, rem_c, [w4x], [], filler(started), _diag_forward_copy,
                                "weights_far_forward_wait")
        waves["out"] = (sem_c, rem_c, wo4s)
        return w4f, started

    def w_out_arrived(after):
        sem_c, rem_c, wo4s = waves["out"]
        (wow,) = _halves_wait(sem_c, rem_c, [wo4s], after, all_k, "weights_out_wait")
        sem_d, rem_d, wof, started = _split_start([wow], [], 3, _forward_copies, "weights_out_forward_start")
        waves["out_forward"] = (sem_d, rem_d, wof)
        return started

    def w_out_gathered(after):
        sem_d, rem_d, wof = waves["out_forward"]
        (wo,), _ = _split_wait(sem_d, rem_d, [wof], [], after, _forward_copies, "weights_out_forward_wait")
        return wo.reshape(D, D)

    weights = dict(own=w_in[0], jvec=jvec, started=token, first=jvec ^ (1 + cvec), second=jvec ^ (2 - cvec),
                   near=near, near2=near2, far=far, out_arrived=w_out_arrived, out=w_out_gathered)

    exchange = _exchange_copies
    pending = {}

    def on_g_w_out(g_w_out):
        out = _split_start([g_w_out], [SDS((4, D // 8, D), f32)], 4, exchange, "grad_out_pair_start")
        pending["ex_out"] = out
        return out[4]

    def after_conv(dp8):
        ssem_o, rsem_o, g_o, land_o, _ = pending["ex_out"]
        (g_o,), (ex_o,) = _split_wait(ssem_o, rsem_o, [g_o], [land_o], dp8, exchange, "grad_out_pair_wait")
        to32, tob = _pair_sum_w_out(g_o, ex_o, cvec)
        out = _split_start([tob], [SDS((3, D // 8, D), bf16)], 3, _scatter_copies, "grad_out_chip_start")
        pending["sc_out"] = (out, to32)
        return out[4]

    def on_g_w_in_first(g_first):
        out = _split_start([g_first], [SDS((4, D // 2, D), f32)], 4, _block_exchange_copies, "grad_pair_start")
        pending["ex"] = (out[0], out[1], [out[2]], [out[3]])
        return out[4]

    def on_w_in_pair_sum(pair_sum):
        ex_ssem, ex_rsem, ex_srcs, ex_lands = pending["ex"]
        t32, tb = pair_sum(ex_srcs[0], ex_lands[0], ex_ssem, ex_rsem)
        out = _split_start([tb], [SDS((3, D // 2, D), bf16)], 3, _scatter_copies, "grad_chip_start")
        pending["sc_in"] = (out, t32)
        return out[4]

    r = _local_step(x[0], ctx[0], loss_target[0], mrow, mrow_c, b_ada, norm_g, weights, q_norm_g, k_norm_g,
                    rpb[0], conv_w_full, conv_b,
                    dict(g_w_out=on_g_w_out, after_conv=after_conv, first_half=1 - cvec,
                         g_w_in_first=on_g_w_in_first, w_in_pair_sum=on_w_in_pair_sum))
    sc_in, t32 = pending["sc_in"]
    _, (r2,) = _split_wait(sc_in[0], sc_in[1], [sc_in[2]], [sc_in[3]], r["dng"], _scatter_copies, "grad_chip_wait")
    u_in = _chip_sum(t32, r2, jvec, "chip_sum_w_in")
    sc_o, to32 = pending["sc_out"]
    _, (ro2,) = _split_wait(sc_o[0], sc_o[1], [sc_o[2]], [sc_o[3]], u_in, _scatter_copies, "grad_out_chip_wait")
    u_out = _chip_sum(to32, ro2, jvec, "chip_sum_w_out")
    swap = _split_start([u_in, u_out], [SDS(u_in.shape, f32), SDS(u_out.shape, f32)], 2, _swap_copies,
                        "grad_pair_swap_start")

    dm = jnp.concatenate([r["dshift"], r["dscale"], r["dgate"]], axis=1)
    dmc = jnp.concatenate([r["dshift_c"], r["dscale_c"], jnp.zeros((1, D), f32)], axis=1)
    pack_parts = [_rows128(dm), _rows128(dmc), _rows128(r["dng"]), _rows128(r["dng_c"]), _rows128(r["g_gq"]),
                  _rows128(r["g_gk"]), _rows128(r["g_gk_c"]), r["g_rpb"].reshape(H * N_DR, 128),
                  _rows128(r["g_conv_b"]), _rows128(r["g_conv_w"][0:3]), jnp.pad(r["loss_sum"], ((0, 0), (0, 127)))]
    pack = jnp.concatenate([jnp.pad(p, ((0, -p.shape[0] % 8), (0, 0))) for p in pack_parts], axis=0)
    assert pack.shape[0] == PK_ROWS
    gs = _split_start([_to_slot(pack, 8, dev)], [], 7, _gather8_copies, "gather_small_start", after=swap[6])
    (u_in, u_out), (o_in, o_out) = _split_wait(swap[0], swap[1], swap[2:4], swap[4:6], gs[3], _swap_copies,
                                               "grad_pair_swap_wait")
    g_w_in_s, d_w_in, nm_w_in, nv_w_in = _adamw_halves(w_in[0], u_in, o_in, m_w_in[0], v_w_in[0], cvec, "adamw_w_in",
                                                       after=gs[3])
    g_w_out_s, d_w_out, nm_w_out, nv_w_out = _adamw_halves(w_out[0], u_out, o_out, m_w_out[0], v_w_out[0], cvec,
                                                           "adamw_w_out", after=nm_w_in)
    (gathered,), _ = _split_wait(gs[0], gs[1], [gs[2]], [], nm_w_out, _gather8_copies, "gather_small_wait")
    red, dm16 = _small_reduce(gathered)
    loss = red[RS_LOSS, 0] * (0.5 / D)

    g_w_ada_s, cpart = _w_ada_grad(sc16, dm16, w_ada_s, jvec)
    gcp = _split_start([_to_slot(cpart, 4, chip)], [], 3, _gather4_copies, "gather_c_ctx_parts_start")
    d_w_ada, nm_w_ada, nv_w_ada = _adamw(w_ada_s, g_w_ada_s, m_w_ada[0], v_w_ada[0], "adamw_w_ada", after=gcp[3])
    (cparts4,), _ = _split_wait(gcp[0], gcp[1], [gcp[2]], [], nm_w_ada, _gather4_copies, "gather_c_ctx_parts_wait")
    g_c_ctx = _c_ctx_grad(cparts4, c_ctx)

    t_rpb = lambda a: jnp.transpose(a, (0, 2, 1, 3)).reshape(N_DR, H, N_DC)
    t_cw = lambda a: jnp.transpose(a, (1, 0, 2))
    small = _adamw_small(
        red, g_c_ctx, jvec,
        [c_ctx, b_ada, norm_g, q_norm_g, k_norm_g, t_rpb(rpb), t_cw(conv_w), conv_b],
        [m_c_ctx, m_b_ada, m_norm_g, m_q_norm_g, m_k_norm_g, t_rpb(m_rpb), t_cw(m_conv_w), m_conv_b],
        [v_c_ctx, v_b_ada, v_norm_g, v_q_norm_g, v_k_norm_g, t_rpb(v_rpb), t_cw(v_conv_w), v_conv_b])
    for kind in small:
        kind[5] = jnp.transpose(kind[5].reshape(1, N_DR, H, N_DC), (0, 2, 1, 3))
        kind[6] = jnp.transpose(kind[6], (1, 0, 2))

    def ordered(kind, big_w_ada, big_w_in, big_w_out):
        s_c_ctx, s_b_ada, s_norm_g, s_q, s_k, s_rpb, s_conv_w, s_conv_b = small[kind]
        return [s_c_ctx, big_w_ada[None], s_b_ada, s_norm_g, big_w_in[None], s_q, s_k, s_rpb, s_conv_w,
                s_conv_b, big_w_out[None]]

    grads = ordered(0, g_w_ada_s, g_w_in_s, g_w_out_s)
    deltas = ordered(1, d_w_ada, d_w_in, d_w_out)
    new_m = ordered(2, nm_w_ada, nm_w_in, nm_w_out)
    new_v = ordered(3, nv_w_ada, nv_w_in, nv_w_out)
    return (loss, r["grad_x"][None], *grads, *deltas, *new_m, *new_v)
```

```python
import functools

import jax
import jax.numpy as jnp
import numpy as np
from jax import lax
from jax.experimental import pallas as pl
from jax.experimental.pallas import tpu as pltpu

f32, bf16, i32 = jnp.float32, jnp.bfloat16, jnp.int32
MESH = pl.DeviceIdType.MESH
HIGHEST = lax.Precision.HIGHEST

D = 1024
S = 2048
L = 256
GW = 64
ROWS = S // GW
H = 8
DH = 64
DA = H * DH
DC = 512
WIN_H, WIN_W = 8, 16
N_DR, N_DC = 2 * WIN_H - 1, 2 * WIN_W - 1
RMS_EPS = 1e-6
ROPE_THETA = 10000.0
QK_SCALE = DH ** -0.5
NEG = -1e30

QB = 128
NQB = S // QB
KR = 9
KB = KR * GW
TILE_GEOM = ((0, 0), (2, 0), (4, 0), (28, 23), (30, 23))
NT = len(TILE_GEOM)

ADAM_LR, ADAM_B1, ADAM_B2, ADAM_EPS, ADAM_WD, ADAM_STEP = 0.001, 0.9, 0.999, 1e-08, 0.01, 10

VMEM_SPEC = pl.BlockSpec(memory_space=pltpu.VMEM)
ANY_SPEC = pl.BlockSpec(memory_space=pl.ANY)
SMEM_SPEC = pl.BlockSpec(memory_space=pltpu.SMEM)
SDS = jax.ShapeDtypeStruct


_pallas_call = pl.pallas_call


def _hbm_call(body, *, out_shape, in_specs=None, out_specs=None, grid_spec=None, **kw):
    n_pre = 0
    if grid_spec is not None:
        ispecs, ospecs, n_pre = grid_spec.in_specs, grid_spec.out_specs, grid_spec.num_scalar_prefetch
        kw["grid_spec"] = grid_spec
    else:
        ispecs, ospecs = in_specs, out_specs
        kw.update(in_specs=in_specs, out_specs=out_specs)

    def blocked(spec):
        return isinstance(spec, pl.BlockSpec) and spec.block_shape is not None

    single = not isinstance(out_shape, (tuple, list))
    shapes = [out_shape] if single else list(out_shape)
    ospec_list = list(ospecs) if isinstance(ospecs, (tuple, list)) else [ospecs]
    shapes = [pltpu.HBM(s.shape, s.dtype) if blocked(sp) else s for s, sp in zip(shapes, ospec_list)]
    call = _pallas_call(body, out_shape=shapes[0] if single else tuple(shapes), **kw)

    def run(*args):
        arrays = [pltpu.with_memory_space_constraint(a, pltpu.HBM) if blocked(sp) else a
                  for a, sp in zip(args[n_pre:], ispecs)]
        return call(*args[:n_pre], *arrays)

    return run


def _cp(vmem_mb=None, **kw):
    if vmem_mb is not None:
        kw["vmem_limit_bytes"] = vmem_mb << 20
    return pltpu.CompilerParams(**kw)


def _silu(z):
    return z * jax.nn.sigmoid(z)


def _dsilu(z):
    sg = jax.nn.sigmoid(z)
    return sg * (1.0 + z * (1.0 - sg))


def _row_start(i):
    return min(max(i - WIN_H // 2, 0), ROWS - WIN_H)


def _my_pos():
    return lax.axis_index("x"), lax.axis_index("y"), lax.axis_index("c")


def _flip(v, bit):
    return 1 - v if bit else v


def _swap_copies(srcs, lands, ssem, rsem):
    x, y, c = _my_pos()
    return [pltpu.make_async_remote_copy(src_ref=srcs[a], dst_ref=lands[a], send_sem=ssem.at[a], recv_sem=rsem.at[a],
                                         device_id=(x, y, 1 - c), device_id_type=MESH) for a in range(len(srcs))]


HBM_SPEC = pl.BlockSpec(memory_space=pltpu.HBM)
SEM_SPEC = pl.BlockSpec(memory_space=pltpu.SEMAPHORE)
DATAFLOW = pltpu.SideEffectType.DATAFLOW_SIDE_EFFECTING


def _peer_chips(x, y, c):
    out = []
    for k in range(1, 4):
        px, py = _flip(x, (k >> 1) & 1), _flip(y, k & 1)
        out.append(((px, py, c), 2 * px + py))
    return out


def _half_copies(srcs, dsts, ssem, rsem, which):
    x, y, c = _my_pos()
    j = 2 * x + y
    peers = _peer_chips(x, y, c)
    pairs = []
    for pos, group, k in which:
        half = srcs[pos].shape[1] // 2
        mine = pl.ds(pl.multiple_of(c * half, 8), half)
        dev, pj = peers[k]
        sem = 3 * group + k
        send = pltpu.make_async_remote_copy(src_ref=srcs[pos].at[j, mine], dst_ref=dsts[pos].at[j, mine],
                                            send_sem=ssem.at[sem], recv_sem=rsem.at[sem], device_id=dev,
                                            device_id_type=MESH)
        arrive = pltpu.make_async_remote_copy(src_ref=srcs[pos].at[j, mine], dst_ref=dsts[pos].at[pj, mine],
                                              send_sem=ssem.at[sem], recv_sem=rsem.at[sem], device_id=dev,
                                              device_id_type=MESH)
        pairs.append((send, arrive))
    return pairs


def _halves_wait(ssem, rsem, bigs, after, which, name):
    nb = len(bigs)

    def body(*refs):
        b_in = refs[:nb]
        ssem_ref, rsem_ref = refs[nb], refs[nb + 1]
        for send, arrive in _half_copies(b_in, b_in, ssem_ref, rsem_ref, which):
            send.wait_send()
            arrive.wait_recv()

    return _hbm_call(
        body, name=name, out_shape=tuple(pltpu.HBM(b.shape, b.dtype) for b in bigs),
        in_specs=[HBM_SPEC] * nb + [SEM_SPEC, SEM_SPEC, ANY_SPEC], out_specs=tuple([HBM_SPEC] * nb),
        input_output_aliases={a: a for a in range(nb)}, compiler_params=_cp(has_side_effects=DATAFLOW),
    )(*bigs, ssem, rsem, after)


FORWARD_SEM = 3


def _diag_forward_copy(srcs, dsts, ssem, rsem):
    x, y, c = _my_pos()
    half = srcs[0].shape[1] // 2
    diag = 3 - (2 * x + y)
    mine = pl.ds(pl.multiple_of(c * half, 8), half)
    other = pl.ds(pl.multiple_of((1 - c) * half, 8), half)
    return [_Copy(srcs[0].at[diag, mine], dsts[0].at[diag, mine], dsts[0].at[diag, other], ssem.at[FORWARD_SEM],
                  rsem.at[FORWARD_SEM], (x, y, 1 - c))]


def _forward_then_start(fwd, big, order, name):
    def body(f_in, b_in, f_out, ssem, rsem, b_out, token):
        _diag_forward_copy([f_in], [f_out], ssem, rsem)[0].start()
        for send, _ in _half_copies([b_in], [b_out], ssem, rsem, order):
            send.start()
        token[...] = jnp.zeros_like(token)

    n_sem = FORWARD_SEM + 1
    out_shape = (pltpu.HBM(fwd.shape, fwd.dtype), pltpu.SemaphoreType.DMA((n_sem,)), pltpu.SemaphoreType.DMA((n_sem,)),
                 pltpu.HBM(big.shape, big.dtype), SDS((8, 128), f32))
    return _hbm_call(
        body, name=name, out_shape=out_shape, in_specs=[HBM_SPEC, HBM_SPEC],
        out_specs=(HBM_SPEC, SEM_SPEC, SEM_SPEC, HBM_SPEC, VMEM_SPEC), input_output_aliases={0: 0, 1: 3},
        compiler_params=_cp(has_side_effects=DATAFLOW),
    )(*[pltpu.with_memory_space_constraint(b, pltpu.HBM) for b in (fwd, big)])


def _cast_to_slot(w, jvec, name, after=None):
    rows, cols = w.shape
    tr = min(128, rows // 4)
    band = lambda k: slice(k * tr, (k + 1) * tr)
    return _stream_call(
        lambda x: (x.astype(bf16),), rows // tr, [(w, (tr, cols), lambda ref, k, j: ref.at[band(k)])],
        [((4, rows, cols), bf16, (tr, cols), lambda ref, k, j: ref.at[j, band(k)])], jvec, name, after)[0]


def _stream_call(fn, n_chunks, srcs, outs, vec, name, after=None):
    ns, no = len(srcs), len(outs)

    def body(*refs):
        s = refs[0][0]
        in_refs, out_refs = refs[1:1 + ns], refs[2 + ns:2 + ns + no]
        in_bufs, out_bufs = refs[2 + ns + no:2 + 2 * ns + no], refs[2 + 2 * ns + no:2 + 2 * ns + 2 * no]
        sem_in, sem_out = refs[2 + 2 * ns + 2 * no:]
        loads, stores = [], []
        for k in range(n_chunks):
            cps = [pltpu.make_async_copy(srcs[a][2](in_refs[a], k, s), in_bufs[a].at[k], sem_in.at[a, k])
                   for a in range(ns)]
            for cp in cps:
                cp.start()
            loads.append(cps)
        for k, cps in enumerate(loads):
            for cp in cps:
                cp.wait()
            for b, val in enumerate(fn(*[buf[k] for buf in in_bufs])):
                out_bufs[b][k] = val
                cp = pltpu.make_async_copy(out_bufs[b].at[k], outs[b][3](out_refs[b], k, s), sem_out.at[b, k])
                cp.start()
                stores.append(cp)
        for cp in stores:
            cp.wait()

    in_bufs = [pltpu.VMEM((n_chunks,) + tuple(shape), a.dtype) for a, shape, _ in srcs]
    out_bufs = [pltpu.VMEM((n_chunks,) + tuple(cshape), dtype) for _, dtype, cshape, _ in outs]
    n_bytes = sum(np.prod(b.shape) * np.dtype(b.dtype).itemsize for b in in_bufs + out_bufs)
    after_big = after is not None and after.size * after.dtype.itemsize > (1 << 20)
    after = pltpu.with_memory_space_constraint(after, pltpu.HBM) if after_big else after
    res = _pallas_call(
        body, name=name, out_shape=tuple(pltpu.HBM(shape, dtype) for shape, dtype, _, _ in outs),
        in_specs=[SMEM_SPEC] + [HBM_SPEC] * ns + [HBM_SPEC if after_big else ANY_SPEC], out_specs=(HBM_SPEC,) * no,
        scratch_shapes=in_bufs + out_bufs + [pltpu.SemaphoreType.DMA((ns, n_chunks)),
                                             pltpu.SemaphoreType.DMA((no, n_chunks))],
        compiler_params=_cp(int(n_bytes) // (1 << 20) + 8),
    )(vec, *[pltpu.with_memory_space_constraint(a, pltpu.HBM) for a, _, _ in srcs], vec if after is None else after)
    return res


def _exchange_copies(srcs, lands, ssem, rsem):
    x, y, c = _my_pos()
    half = srcs[0].shape[0] // 8
    cps = []
    for jb in range(4):
        src = srcs[0].at[pl.ds(pl.multiple_of((2 * jb + 1 - c) * half, 8), half)]
        cps.append(pltpu.make_async_remote_copy(src_ref=src, dst_ref=lands[0].at[jb], send_sem=ssem.at[jb],
                                                recv_sem=rsem.at[jb], device_id=(x, y, 1 - c), device_id_type=MESH))
    return cps


def _block_exchange_copies(srcs, lands, ssem, rsem):
    x, y, c = _my_pos()
    return [pltpu.make_async_remote_copy(src_ref=srcs[0].at[jb], dst_ref=lands[0].at[jb], send_sem=ssem.at[jb],
                                         recv_sem=rsem.at[jb], device_id=(x, y, 1 - c), device_id_type=MESH)
            for jb in range(4)]


def _scatter_copies(srcs, lands, ssem, rsem):
    x, y, c = _my_pos()
    cps = []
    for a in range(len(srcs)):
        for k, (dev, pj) in enumerate(_peer_chips(x, y, c)):
            cps.append(pltpu.make_async_remote_copy(src_ref=srcs[a].at[pj], dst_ref=lands[a].at[k],
                                                    send_sem=ssem.at[3 * a + k], recv_sem=rsem.at[3 * a + k],
                                                    device_id=dev, device_id_type=MESH))
    return cps


class _Copy:
    def __init__(self, src, dst, arrive, ssem, rsem, dev):
        make = lambda to: pltpu.make_async_remote_copy(src_ref=src, dst_ref=to, send_sem=ssem, recv_sem=rsem,
                                                       device_id=dev, device_id_type=MESH)
        send, arrival = make(dst), make(arrive)
        self.start, self.wait_send, self.wait_recv = send.start, send.wait_send, arrival.wait_recv


def _toward(x, y, along_x):
    return x + along_x * (1 - 2 * x), y + (1 - along_x) * (1 - 2 * y)


def _near_copies(srcs, dsts, ssem, rsem):
    x, y, c = _my_pos()
    px, py = _toward(x, y, c)
    j = 2 * x + y
    return [_Copy(srcs[0].at[j], dsts[0].at[j], dsts[0].at[2 * px + py], ssem.at[0], rsem.at[0], (px, py, c))]


def _pass_copy(srcs, dsts, ssem, rsem):
    x, y, c = _my_pos()
    px, py = _toward(x, y, c)
    qx, qy = _toward(x, y, 1 - c)
    got = 2 * px + py
    return [_Copy(srcs[0].at[got], dsts[0].at[got], dsts[0].at[2 * qx + qy], ssem.at[0], rsem.at[0], (x, y, 1 - c))]


def _relay_copy(srcs, dsts, ssem, rsem):
    x, y, c = _my_pos()
    px, py = _toward(x, y, c)
    qx, qy = _toward(x, y, 1 - c)
    half = srcs[0].shape[1] // 2
    mine = pl.ds(pl.multiple_of(c * half, 8), half)
    got, diag = 2 * px + py, 3 - (2 * x + y)
    return [_Copy(srcs[0].at[got, mine], dsts[0].at[got, mine], dsts[0].at[diag, mine], ssem.at[1], rsem.at[1],
                  (qx, qy, c))]


def _forward_copies(srcs, dsts, ssem, rsem):
    x, y, c = _my_pos()
    half = srcs[0].shape[1] // 2
    mine = pl.ds(pl.multiple_of(c * half, 8), half)
    other = pl.ds(pl.multiple_of((1 - c) * half, 8), half)
    return [_Copy(srcs[0].at[pj, mine], dsts[0].at[pj, mine], dsts[0].at[pj, other], ssem.at[k], rsem.at[k],
                  (x, y, 1 - c)) for k, (_, pj) in enumerate(_peer_chips(x, y, c))]


def _pass_relay_copies(srcs, dsts, ssem, rsem):
    return _pass_copy(srcs, dsts, ssem, rsem) + _relay_copy(srcs, dsts, ssem, rsem)


def _gather8_copies(srcs, dsts, ssem, rsem):
    x, y, c = _my_pos()
    me = 4 * x + 2 * y + c
    cps = []
    for a in range(len(srcs)):
        for k in range(1, 8):
            tgt = (_flip(x, (k >> 2) & 1), _flip(y, (k >> 1) & 1), _flip(c, k & 1))
            cps.append(_Copy(srcs[a].at[me], dsts[a].at[me], dsts[a].at[4 * tgt[0] + 2 * tgt[1] + tgt[2]],
                             ssem.at[7 * a + k - 1], rsem.at[7 * a + k - 1], tgt))
    return cps


def _gather4_copies(srcs, dsts, ssem, rsem):
    x, y, c = _my_pos()
    j = 2 * x + y
    cps = []
    for a in range(len(srcs)):
        for k, (dev, pj) in enumerate(_peer_chips(x, y, c)):
            cps.append(_Copy(srcs[a].at[j], dsts[a].at[j], dsts[a].at[pj], ssem.at[3 * a + k], rsem.at[3 * a + k], dev))
    return cps


def _to_slot(a, n, i):
    return lax.dynamic_update_slice(jnp.zeros((n,) + a.shape, a.dtype), a[None], (i,) + (0,) * a.ndim)


def _split_start(srcs, land_shapes, n_cp, make, name, after=None):
    ns, nl = len(srcs), len(land_shapes)
    n_in = ns + nl + (after is not None)

    def body(*refs):
        s_in = refs[:ns]
        ssem, rsem = refs[n_in], refs[n_in + 1]
        s_out = refs[n_in + 2:n_in + 2 + ns]
        l_out = refs[n_in + 2 + ns:n_in + 2 + ns + nl]
        token = refs[n_in + 2 + ns + nl]
        for cp in make(s_in, l_out if nl else s_out, ssem, rsem):
            cp.start()
        token[...] = jnp.zeros_like(token)

    lands = [pltpu.with_memory_space_constraint(lax.empty(sh.shape, sh.dtype), pltpu.HBM) for sh in land_shapes]
    out_shape = (pltpu.SemaphoreType.DMA((n_cp,)), pltpu.SemaphoreType.DMA((n_cp,)),
                 *[pltpu.HBM(b.shape, b.dtype) for b in srcs], *[pltpu.HBM(b.shape, b.dtype) for b in land_shapes],
                 SDS((8, 128), f32))
    return _hbm_call(
        body, name=name, out_shape=out_shape, in_specs=[HBM_SPEC] * (ns + nl) + [ANY_SPEC] * (after is not None),
        out_specs=(SEM_SPEC, SEM_SPEC, *[HBM_SPEC] * (ns + nl), VMEM_SPEC),
        input_output_aliases={i: 2 + i for i in range(ns + nl)}, compiler_params=_cp(has_side_effects=DATAFLOW),
    )(*[pltpu.with_memory_space_constraint(b, pltpu.HBM) for b in srcs], *lands, *([] if after is None else [after]))


def _split_wait(ssem, rsem, srcs, lands, after, make, name):
    ns, nl = len(srcs), len(lands)

    def body(*refs):
        s_in, l_in = refs[:ns], refs[ns:ns + nl]
        ssem_ref, rsem_ref = refs[ns + nl], refs[ns + nl + 1]
        for cp in make(s_in, l_in if nl else s_in, ssem_ref, rsem_ref):
            cp.wait_send()
            cp.wait_recv()

    outs = _hbm_call(
        body, name=name, out_shape=tuple(pltpu.HBM(b.shape, b.dtype) for b in (*srcs, *lands)),
        in_specs=[HBM_SPEC] * (ns + nl) + [SEM_SPEC, SEM_SPEC, ANY_SPEC], out_specs=tuple([HBM_SPEC] * (ns + nl)),
        input_output_aliases={i: i for i in range(ns + nl)}, compiler_params=_cp(has_side_effects=DATAFLOW),
    )(*srcs, *lands, ssem, rsem, after)
    return list(outs[:ns]), list(outs[ns:])


def _adaln_shard(cc, w_ada_shard):
    def body(c_ref, w_ref, m_ref, sc_ref):
        sc = _silu(c_ref[...])
        sc_ref[...] = sc
        m_ref[...] = jnp.dot(sc, w_ref[...], precision=HIGHEST, preferred_element_type=f32)

    return _hbm_call(
        body, name="adaln_shard", out_shape=(SDS((16, w_ada_shard.shape[1]), f32), SDS((16, D), f32)),
        in_specs=[VMEM_SPEC, VMEM_SPEC], out_specs=(VMEM_SPEC, VMEM_SPEC), compiler_params=_cp(32),
    )(cc, w_ada_shard)


def _prenorm(xx, norm_g, mrow, b_ada, tm, name, after=None):
    n = xx.shape[0]

    def norm(x, g, m, b):
        shift = m[:, 0:D] + b[:, 0:D]
        scale = m[:, D:2 * D] + b[:, D:2 * D]
        r = lax.rsqrt(jnp.mean(x * x, axis=-1, keepdims=True) + RMS_EPS)
        y = (x * r) * g
        return ((y * (1.0 + scale) + shift).astype(bf16),)

    band = lambda ref, k, s: ref.at[k * tm:(k + 1) * tm]
    whole = lambda ref, k, s: ref
    jvec = jnp.zeros((1,), i32)
    return _stream_call(
        norm, n // tm, [(xx, (tm, D), band), (norm_g, (1, D), whole), (mrow, (1, 3 * D), whole),
                        (b_ada, (1, 3 * D), whole)],
        [((n, D), bf16, (tm, D), band)], jvec, name, after)[0]


def _in_proj_own(h, w_own, jvec):
    tm = 512

    def body(j_ref, h_ref, w_ref, p_ref):
        p_ref[...] = jnp.dot(h_ref[...], w_ref[...], preferred_element_type=f32)

    grid_spec = pltpu.PrefetchScalarGridSpec(
        num_scalar_prefetch=1, grid=(S // tm,),
        in_specs=[pl.BlockSpec((tm, D), lambda i, j: (i, 0)), pl.BlockSpec((None, D, D), lambda i, j: (j[0], 0, 0))],
        out_specs=pl.BlockSpec((tm, D), lambda i, j: (i, j[0])))
    return _hbm_call(body, name="in_proj_own", out_shape=SDS((S, 4 * D), f32), grid_spec=grid_spec,
                     compiler_params=_cp(40))(jvec, h, w_own)


def _in_proj_block(h, w4, p, bvec, name, after=None):
    tm = 512

    def body(b_ref, h_ref, w_ref, p_in_ref, after_ref, p_ref):
        p_ref[...] = jnp.dot(h_ref[...], w_ref[...], preferred_element_type=f32)

    grid_spec = pltpu.PrefetchScalarGridSpec(
        num_scalar_prefetch=1, grid=(S // tm,),
        in_specs=[pl.BlockSpec((tm, D), lambda i, b: (i, 0)), pl.BlockSpec((None, D, D), lambda i, b: (b[0], 0, 0)),
                  ANY_SPEC, ANY_SPEC],
        out_specs=pl.BlockSpec((tm, D), lambda i, b: (i, b[0])))
    return _hbm_call(body, name=name, out_shape=SDS((S, 4 * D), f32), grid_spec=grid_spec,
                     input_output_aliases={3: 0})(bvec, h, w4, p, bvec if after is None else after)


def _ctx_proj(hc, w4):
    def body(h_ref, w0_ref, w1_ref, p_ref):
        hv = h_ref[...]
        p_ref[:, 0:DA] = jnp.dot(hv, w0_ref[:, DA:2 * DA], preferred_element_type=f32)
        p_ref[:, DA:2 * DA] = jnp.dot(hv, w1_ref[:, 0:DA], preferred_element_type=f32)

    return _hbm_call(
        body, name="ctx_proj", out_shape=SDS((L, 2 * DA), f32), grid=(1,),
        in_specs=[pl.BlockSpec((L, D), lambda i: (0, 0)), pl.BlockSpec((None, D, D), lambda i: (0, 0, 0)),
                  pl.BlockSpec((None, D, D), lambda i: (1, 0, 0))],
        out_specs=pl.BlockSpec((L, 2 * DA), lambda i: (0, 0)),
    )(hc, w4, w4)


def _head_ones():
    r = lax.broadcasted_iota(i32, (DA, DA), 0) // DH
    c = lax.broadcasted_iota(i32, (DA, DA), 1) // DH
    return (r == c).astype(bf16)


def _head_sum(v, ones_bd):
    hi = v.astype(bf16)
    lo = (v - hi.astype(f32)).astype(bf16)
    return jnp.dot(hi, ones_bd, preferred_element_type=f32) + jnp.dot(lo, ones_bd, preferred_element_type=f32)


def _swap16(v):
    lane = lax.broadcasted_iota(i32, v.shape, 1)
    return jnp.where((lane & 31) < 16, pltpu.roll(v, DA - 16, 1), pltpu.roll(v, 16, 1))


def _rope_block(ct_ref, rt_ref, tm):
    rows = [jnp.tile(rt_ref[8 * j:8 * j + 8, :], (GW // 8, 1)) for j in range(tm // GW)]
    return jnp.tile(ct_ref[...], (tm // GW, 1)) + jnp.concatenate(rows, axis=0)


def _rope_specs(tm):
    col = pl.BlockSpec((GW, DA), lambda i: (0, 0))
    row = pl.BlockSpec((8 * tm // GW, DA), lambda i: (i, 0))
    return [col, row, col, row]


def _qk_prep(p, gq, gk, rope):
    tm = 512

    def body(qk_ref, v_ref, gq_ref, gk_ref, cc_ref, cr_ref, sc_ref, sr_ref, qr_ref, qp_ref, kr_ref, vh_ref):
        ones_bd = _head_ones()
        cs, sn = _rope_block(cc_ref, cr_ref, tm), _rope_block(sc_ref, sr_ref, tm)
        q = qk_ref[:, 0:DA]
        k = qk_ref[:, DA:2 * DA]
        yq = (q * lax.rsqrt(_head_sum(q * q, ones_bd) * (1.0 / DH) + RMS_EPS)) * gq_ref[...]
        yk = (k * lax.rsqrt(_head_sum(k * k, ones_bd) * (1.0 / DH) + RMS_EPS)) * gk_ref[...]
        qr = (yq * cs + _swap16(yq) * sn) * QK_SCALE
        qp = yq * QK_SCALE
        kr = yk * cs + _swap16(yk) * sn
        vv = v_ref[...]
        for hh in range(H):
            sl = slice(hh * DH, (hh + 1) * DH)
            qr_ref[hh] = qr[:, sl].astype(bf16)
            qp_ref[hh] = qp[:, sl].astype(bf16)
            kr_ref[hh] = kr[:, sl].astype(bf16)
            vh_ref[hh] = vv[:, sl].astype(bf16)

    hm = SDS((H, S, DH), bf16)
    hspec = pl.BlockSpec((H, tm, DH), lambda i: (0, i, 0))
    fixed = lambda i: (0, 0)
    return _hbm_call(
        body, name="qk_prep", out_shape=(hm, hm, hm, hm), grid=(S // tm,),
        in_specs=[pl.BlockSpec((tm, 2 * DA), lambda i: (i, 0)), pl.BlockSpec((tm, DA), lambda i: (i, 2)),
                  pl.BlockSpec((1, DA), fixed), pl.BlockSpec((1, DA), fixed)] + _rope_specs(tm),
        out_specs=(hspec, hspec, hspec, hspec),
    )(p, p, gq, gk, *rope)


def _ctx_prep(pc, gk):
    def body(p_ref, gk_ref, kc_ref, vc_ref):
        ones_bd = _head_ones()
        k = p_ref[:, 0:DA]
        yk = (k * lax.rsqrt(_head_sum(k * k, ones_bd) * (1.0 / DH) + RMS_EPS)) * gk_ref[...]
        vv = p_ref[:, DA:2 * DA]
        for hh in range(H):
            sl = slice(hh * DH, (hh + 1) * DH)
            kc_ref[hh] = yk[:, sl].astype(bf16)
            vc_ref[hh] = vv[:, sl].astype(bf16)

    hm = SDS((H, L, DH), bf16)
    return _hbm_call(
        body, name="ctx_prep", out_shape=(hm, hm), in_specs=[VMEM_SPEC, VMEM_SPEC], out_specs=(VMEM_SPEC, VMEM_SPEC),
    )(pc, gk)


def _tile_pieces():
    out = []
    for (i0, u0) in TILE_GEOM:
        rows = []
        for j in range(2):
            i = i0 + j
            rs = _row_start(i)
            rows.append([(u0 + u - i + WIN_H - 1) if rs <= u0 + u < rs + WIN_H else None for u in range(KR)])
        out.append(rows)
    return out


def _bias_prep(rpb_rev_pad, after=None):
    pieces = _tile_pieces()

    def body(r_ref, after_ref, o_ref):
        rp = r_ref[...]
        xs = jnp.concatenate([pltpu.roll(jnp.broadcast_to(rp[dr:dr + 1, :], (GW, 128)), 128 - (WIN_W - 1), 1,
                                         stride=1, stride_axis=0) for dr in range(N_DR)], axis=0)
        row = lax.broadcasted_iota(i32, xs.shape, 0)
        lane = lax.broadcasted_iota(i32, xs.shape, 1)
        k = row & (GW - 1)
        c0 = jnp.clip(lane - WIN_W // 2, 0, GW - WIN_W)
        xs = jnp.where((k >= c0) & (k < c0 + WIN_W), xs, NEG)
        neg = jnp.full((GW, GW), NEG, f32)
        for t in range(NT):
            for j in range(2):
                for u in range(KR):
                    dr = pieces[t][j][u]
                    piece = neg if dr is None else xs[dr * GW:(dr + 1) * GW, 0:GW]
                    o_ref[t, u * GW:(u + 1) * GW, j * GW:(j + 1) * GW] = piece

    return _hbm_call(
        body, name="bias_prep", out_shape=SDS((H, NT, KB, QB), f32), grid=(H,),
        in_specs=[pl.BlockSpec((None, N_DR, 128), lambda h: (h, 0, 0)), ANY_SPEC],
        out_specs=pl.BlockSpec((None, NT, KB, QB), lambda h: (h, 0, 0, 0)),
    )(rpb_rev_pad, rpb_rev_pad if after is None else after)


def _bias_tiles(rpb2, after=None):
    return _bias_prep(jnp.pad(rpb2[:, :, ::-1], ((0, 0), (0, 0), (0, 128 - N_DC))), after)


def _block_geom(b):
    qs = b * QB
    ks = min(max(2 * b - 4, 0), ROWS - KR) * GW
    t = b if b < 2 else (b - (NQB - NT) if b > NQB - 3 else 2)
    return qs, ks, t


def _tt(a, b):
    return lax.dot_general(a, b, (((1,), (1,)), ((), ())), preferred_element_type=f32)


def _tn(a, b):
    return lax.dot_general(a, b, (((0,), (0,)), ((), ())), preferred_element_type=f32)


def _softmax_t(s_lat, s_ctx):
    m = jnp.maximum(jnp.max(s_lat, axis=0, keepdims=True), jnp.max(s_ctx, axis=0, keepdims=True))
    e_lat = jnp.exp(s_lat - m)
    e_ctx = jnp.exp(s_ctx - m)
    inv = 1.0 / (jnp.sum(e_lat, axis=0, keepdims=True) + jnp.sum(e_ctx, axis=0, keepdims=True))
    return e_lat * inv, e_ctx * inv


def _staged(n_blocks, stages):
    held = [dict() for _ in stages]
    for step in range(n_blocks + len(stages) - 1):
        for s, fn in enumerate(stages):
            b = step - s
            if 0 <= b < n_blocks:
                held[s][b] = fn(b) if s == 0 else fn(b, held[s - 1].pop(b))


def _attn_fwd(qr, qp, kr, vh, kc, vc, btt):
    def body(qr_ref, qp_ref, kr_ref, v_ref, kc_ref, vc_ref, bt_ref, o_ref):
        kcv, vcv = kc_ref[...], vc_ref[...]

        def scores(b):
            qs, ks, t = _block_geom(b)
            return (_tt(kr_ref[ks:ks + KB, :], qr_ref[qs:qs + QB, :]) + bt_ref[t], _tt(kcv, qp_ref[qs:qs + QB, :]))

        def probs(b, sc):
            p_lat, p_ctx = _softmax_t(*sc)
            return p_lat.astype(bf16), p_ctx.astype(bf16)

        def values(b, p):
            qs, ks, _ = _block_geom(b)
            o_ref[qs:qs + QB, :] = _tn(p[0], v_ref[ks:ks + KB, :]) + _tn(p[1], vcv)

        _staged(NQB, (scores, probs, values))

    sq = pl.BlockSpec((None, S, DH), lambda h: (h, 0, 0))
    sc = pl.BlockSpec((None, L, DH), lambda h: (h, 0, 0))
    return _hbm_call(
        body, name="attn_fwd", out_shape=SDS((H, S, DH), f32), grid=(H,),
        in_specs=[sq, sq, sq, sq, sc, sc, pl.BlockSpec((None, NT, KB, QB), lambda h: (h, 0, 0, 0))],
        out_specs=sq, compiler_params=_cp(48),
    )(qr, qp, kr, vh, kc, vc, btt)


def _shift_rows(v, down):
    n = v.shape[0]
    row = lax.broadcasted_iota(i32, v.shape, 0)
    if down:
        return jnp.where(row == 0, 0.0, pltpu.roll(v, 1, 0))
    return jnp.where(row == n - 1, 0.0, pltpu.roll(v, n - 1, 0))


def _conv_specs():
    col = lambda off: pl.BlockSpec((S, 128), lambda i, off=off: (0, off + i))
    return [col(16), col(20), col(24), col(28), pl.BlockSpec((3, 128), lambda i: (0, i)),
            pl.BlockSpec((1, 128), lambda i: (0, i))]


def _conv_fwd(p, conv_w, conv_b, after=None):
    def body(u_ref, bg_ref, cg_ref, zc_ref, w_ref, b_ref, after_ref, o_ref):
        cu = cg_ref[...] * u_ref[...]
        cv = b_ref[...] + _shift_rows(cu, True) * w_ref[0:1, :]
        cv = cv + cu * w_ref[1:2, :]
        cv = cv + _shift_rows(cu, False) * w_ref[2:3, :]
        o_ref[...] = ((bg_ref[...] * cv) * _silu(zc_ref[...])).astype(bf16)

    return _hbm_call(
        body, name="conv_fwd", out_shape=SDS((S, DC), bf16), grid=(DC // 128,),
        in_specs=_conv_specs() + [ANY_SPEC], out_specs=pl.BlockSpec((S, 128), lambda i: (0, i)),
        compiler_params=_cp(40),
    )(p, p, p, p, conv_w, conv_b, conv_b if after is None else after)


DP_Q, DP_K, DP_V, DP_ZA, DP_U, DP_BG, DP_CG, DP_ZC = range(8)


def _out_proj_loss(o, p, conv_g, w_out, xx, tgt, mrow, b_ada):
    tm = 512

    def body(o_ref, za_ref, c_ref, w_ref, x_ref, t_ref, m_ref, b_ref,
             dy_ref, dconv_ref, dp_ref, do_ref, gwo_ref, dgate_ref, loss_ref):
        k = pl.program_id(0)

        @pl.when(k == 0)
        def _():
            gwo_ref[...] = jnp.zeros_like(gwo_ref)
            dgate_ref[...] = jnp.zeros_like(dgate_ref)
            loss_ref[0, 0] = 0.0

        gate = m_ref[:, 2 * D:3 * D] + b_ref[:, 2 * D:3 * D]
        za = za_ref[...]
        sz = _silu(za)
        om = _merge_heads(o_ref)
        av, cv = (om * sz).astype(bf16), c_ref[...]
        mo = jnp.dot(av, w_ref[0:DA, :], preferred_element_type=f32)
        mo = mo + jnp.dot(cv, w_ref[DA:DA + DC, :], preferred_element_type=f32)
        y = x_ref[...] + gate * mo
        diff = y - t_ref[...]
        loss_ref[0, 0] += jnp.sum(diff * diff)
        dy = diff * (1.0 / D)
        dy_ref[...] = dy
        dgate_ref[...] += jnp.sum(dy * mo, axis=0, keepdims=True)
        dmo = (dy * gate).astype(bf16)
        dmix = _tt(dmo, w_ref[...])
        dattn = dmix[:, 0:DA]
        dconv_ref[...] = dmix[:, DA:DA + DC]
        a = dattn * sz
        for hh in range(H):
            do_ref[hh] = a[:, hh * DH:(hh + 1) * DH].astype(bf16)
        dp_ref[...] = ((dattn * _dsilu(za)) * om).astype(bf16)
        gwo_ref[0:DA, :] += _tn(av, dmo)
        gwo_ref[DA:DA + DC, :] += _tn(cv, dmo)

    row = lambda i: (i, 0)
    fixed = lambda i: (0, 0)
    hspec = pl.BlockSpec((H, tm, DH), lambda i: (0, i, 0))
    return _hbm_call(
        body, name="out_proj_loss",
        out_shape=(SDS((S, D), f32), SDS((S, DC), f32), SDS((8, S, DA), bf16), SDS((H, S, DH), bf16),
                   SDS((D, D), f32), SDS((1, D), f32), SDS((1, 1), f32)),
        grid=(S // tm,),
        in_specs=[hspec, pl.BlockSpec((tm, DA), lambda i: (i, 3)), pl.BlockSpec((tm, DC), row),
                  pl.BlockSpec((D, D), fixed), pl.BlockSpec((tm, D), row), pl.BlockSpec((tm, D), row),
                  pl.BlockSpec((1, 3 * D), fixed), pl.BlockSpec((1, 3 * D), fixed)],
        out_specs=(pl.BlockSpec((tm, D), row), pl.BlockSpec((tm, DC), row),
                   pl.BlockSpec((None, tm, DA), lambda i: (DP_ZA, i, 0)), hspec, pl.BlockSpec((D, D), fixed),
                   pl.BlockSpec((1, D), fixed), SMEM_SPEC),
        compiler_params=_cp(56, dimension_semantics=("arbitrary",)),
    )(o, p, conv_g, w_out, xx, tgt, mrow, b_ada)


def _conv_bwd(dconv, p, conv_w, conv_b, dp8, after=None):
    def body(d_ref, u_ref, bg_ref, cg_ref, zc_ref, w_ref, b_ref, dp_in_ref, after_ref, dp_ref, gw_ref, gb_ref):
        du_ref, dbg_ref, dcg_ref, dzc_ref = dp_ref.at[0], dp_ref.at[1], dp_ref.at[2], dp_ref.at[3]
        dconv = d_ref[...]
        u, bg, cg, zc = u_ref[...], bg_ref[...], cg_ref[...], zc_ref[...]
        w0, w1, w2 = w_ref[0:1, :], w_ref[1:2, :], w_ref[2:3, :]
        cu = cg * u
        cu_m, cu_p = _shift_rows(cu, True), _shift_rows(cu, False)
        cv = b_ref[...] + cu_m * w0
        cv = cv + cu * w1
        cv = cv + cu_p * w2
        sz = _silu(zc)
        dbg_ref[...] = ((dconv * sz) * cv).astype(bf16)
        dzc_ref[...] = ((dconv * (bg * cv)) * _dsilu(zc)).astype(bf16)
        dcv = (dconv * sz) * bg
        gb_ref[...] = jnp.sum(dcv, axis=0, keepdims=True)
        gw_ref[0:1, :] = jnp.sum(dcv * cu_m, axis=0, keepdims=True)
        gw_ref[1:2, :] = jnp.sum(dcv * cu, axis=0, keepdims=True)
        gw_ref[2:3, :] = jnp.sum(dcv * cu_p, axis=0, keepdims=True)
        gw_ref[3:8, :] = jnp.zeros((5, 128), f32)
        dcu = _shift_rows(dcv, False) * w0 + dcv * w1 + _shift_rows(dcv, True) * w2
        dcg_ref[...] = (dcu * u).astype(bf16)
        du_ref[...] = (dcu * cg).astype(bf16)

    return _hbm_call(
        body, name="conv_bwd", out_shape=(SDS((8, S, DC), bf16), SDS((8, DC), f32), SDS((1, DC), f32)),
        grid=(DC // 128,),
        in_specs=[pl.BlockSpec((S, 128), lambda i: (0, i))] + _conv_specs() + [ANY_SPEC, ANY_SPEC],
        out_specs=(pl.BlockSpec((4, S, 128), lambda i: (DP_U // 4, 0, i)), pl.BlockSpec((8, 128), lambda i: (0, i)),
                   pl.BlockSpec((1, 128), lambda i: (0, i))),
        input_output_aliases={7: 0}, compiler_params=_cp(48),
    )(dconv, p, p, p, p, conv_w, conv_b, dp8, conv_b if after is None else after)


def _attn_bwd(qr, qp, kr, vh, kc, vc, btt, do, after=None):
    def body(qr_ref, qp_ref, kr_ref, v_ref, kc_ref, vc_ref, bt_ref, do_ref, after_ref,
             dqr_ref, dqp_ref, dkr_ref, dv_ref, dkc_ref, dvc_ref, dbt_ref):
        kcv, vcv = kc_ref[...], vc_ref[...]
        dkr_ref[...] = jnp.zeros_like(dkr_ref)
        dv_ref[...] = jnp.zeros_like(dv_ref)
        dbt_ref[...] = jnp.zeros_like(dbt_ref)
        ctx_acc = {}

        def products(b):
            qs, ks, t = _block_geom(b)
            dob = do_ref[qs:qs + QB, :]
            s_lat = _tt(kr_ref[ks:ks + KB, :], qr_ref[qs:qs + QB, :]) + bt_ref[t]
            s_ctx = _tt(kcv, qp_ref[qs:qs + QB, :])
            return s_lat, s_ctx, _tt(v_ref[ks:ks + KB, :], dob), _tt(vcv, dob)

        def score_grads(b, x):
            s_lat, s_ctx, dp_lat, dp_ctx = x
            p_lat, p_ctx = _softmax_t(s_lat, s_ctx)
            delta = jnp.sum(p_lat * dp_lat, axis=0, keepdims=True) + jnp.sum(p_ctx * dp_ctx, axis=0, keepdims=True)
            ds_lat = p_lat * (dp_lat - delta)
            ds_ctx = p_ctx * (dp_ctx - delta)
            return ds_lat, ds_lat.astype(bf16), ds_ctx.astype(bf16), p_lat.astype(bf16), p_ctx.astype(bf16)

        def operand_grads(b, y):
            qs, ks, t = _block_geom(b)
            ds_lat, dsb_lat, dsb_ctx, pb_lat, pb_ctx = y
            qrb, qpb, dob = qr_ref[qs:qs + QB, :], qp_ref[qs:qs + QB, :], do_ref[qs:qs + QB, :]
            dbt_ref[t] += ds_lat
            dqr_ref[qs:qs + QB, :] = _tn(dsb_lat, kr_ref[ks:ks + KB, :])
            dqp_ref[qs:qs + QB, :] = _tn(dsb_ctx, kcv)
            dkr_ref[ks:ks + KB, :] += jnp.dot(dsb_lat, qrb, preferred_element_type=f32)
            dv_ref[ks:ks + KB, :] += jnp.dot(pb_lat, dob, preferred_element_type=f32)
            dkc = jnp.dot(dsb_ctx, qpb, preferred_element_type=f32)
            dvc = jnp.dot(pb_ctx, dob, preferred_element_type=f32)
            ctx_acc["k"] = dkc if b == 0 else ctx_acc["k"] + dkc
            ctx_acc["v"] = dvc if b == 0 else ctx_acc["v"] + dvc

        _staged(NQB, (products, score_grads, operand_grads))
        dkc_ref[...] = ctx_acc["k"]
        dvc_ref[...] = ctx_acc["v"]

    sq = pl.BlockSpec((None, S, DH), lambda h: (h, 0, 0))
    sc = pl.BlockSpec((None, L, DH), lambda h: (h, 0, 0))
    sb = pl.BlockSpec((None, NT, KB, QB), lambda h: (h, 0, 0, 0))
    big, ctxs = SDS((H, S, DH), f32), SDS((H, L, DH), f32)
    return _hbm_call(
        body, name="attn_bwd", out_shape=(big, big, big, big, ctxs, ctxs, SDS((H, NT, KB, QB), f32)), grid=(H,),
        in_specs=[sq, sq, sq, sq, sc, sc, sb, sq, ANY_SPEC], out_specs=(sq, sq, sq, sq, sc, sc, sb),
        compiler_params=_cp(56),
    )(qr, qp, kr, vh, kc, vc, btt, do, do if after is None else after)


def _bias_bwd(dbtt, after=None):
    pieces = _tile_pieces()

    def body(d_ref, after_ref, o_ref, scr):
        scr[...] = jnp.zeros_like(scr)
        acc = [None] * N_DR
        for t in range(NT):
            for j in range(2):
                for u in range(KR):
                    dr = pieces[t][j][u]
                    if dr is None:
                        continue
                    piece = d_ref[t, u * GW:(u + 1) * GW, j * GW:(j + 1) * GW]
                    acc[dr] = piece if acc[dr] is None else acc[dr] + piece
        a = lax.broadcasted_iota(i32, (GW, GW), 0)
        b = lax.broadcasted_iota(i32, (GW, GW), 1)
        flip = (a + b == GW - 1).astype(f32)
        for dr in range(N_DR):
            scr[dr * GW:(dr + 1) * GW, 0:GW] = jnp.dot(acc[dr], flip, precision=HIGHEST, preferred_element_type=f32)
        xs = jnp.concatenate([pltpu.roll(scr[dr * GW:(dr + 1) * GW, :], 128 + (WIN_W - 1) - (GW - 1), 1,
                                         stride=1, stride_axis=0) for dr in range(N_DR)], axis=0)
        tot = jnp.sum(xs.reshape(N_DR, GW, 128), axis=1)
        lane = lax.broadcasted_iota(i32, tot.shape, 1)
        o_ref[...] = jnp.where(lane < N_DC, tot, 0.0)

    return _hbm_call(
        body, name="bias_bwd", out_shape=SDS((H, N_DR, 128), f32), grid=(H,),
        in_specs=[pl.BlockSpec((None, NT, KB, QB), lambda h: (h, 0, 0, 0)), ANY_SPEC],
        out_specs=pl.BlockSpec((None, N_DR, 128), lambda h: (h, 0, 0)),
        scratch_shapes=[pltpu.VMEM((N_DR * GW, 128), f32)],
    )(dbtt, dbtt if after is None else after)


def _merge_heads(ref):
    return jnp.concatenate([ref[hh] for hh in range(H)], axis=1)


def _head_norm_bwd(xraw, gain, dy, ones_bd):
    r = lax.rsqrt(_head_sum(xraw * xraw, ones_bd) * (1.0 / DH) + RMS_EPS)
    xh = xraw * r
    gdy = dy * gain
    dx = r * (gdy - xh * (_head_sum(xh * gdy, ones_bd) * (1.0 / DH)))
    return dx, jnp.sum(dy * xh, axis=0, keepdims=True)


def _qk_bwd(dqr, dqp, dkr, dvh, p, gq, gk, rope, dp8):
    tm = 512

    def body(dqr_ref, dqp_ref, dkr_ref, dv_ref, qk_ref, gq_ref, gk_ref, cc_ref, cr_ref, sc_ref, sr_ref, dp_in_ref,
             dp_ref, ggq_ref, ggk_ref):
        dq_ref, dk_ref, dvo_ref = dp_ref.at[DP_Q], dp_ref.at[DP_K], dp_ref.at[DP_V]

        @pl.when(pl.program_id(0) == 0)
        def _():
            ggq_ref[...] = jnp.zeros_like(ggq_ref)
            ggk_ref[...] = jnp.zeros_like(ggk_ref)

        ones_bd = _head_ones()
        cs, sn = _rope_block(cc_ref, cr_ref, tm), _rope_block(sc_ref, sr_ref, tm)
        a = _merge_heads(dqr_ref)
        dyq = ((a * cs - _swap16(a) * sn) + _merge_heads(dqp_ref)) * QK_SCALE
        bk = _merge_heads(dkr_ref)
        dyk = bk * cs - _swap16(bk) * sn
        dq, gq_part = _head_norm_bwd(qk_ref[:, 0:DA], gq_ref[...], dyq, ones_bd)
        dk, gk_part = _head_norm_bwd(qk_ref[:, DA:2 * DA], gk_ref[...], dyk, ones_bd)
        dq_ref[...] = dq.astype(bf16)
        dk_ref[...] = dk.astype(bf16)
        dvo_ref[...] = _merge_heads(dv_ref).astype(bf16)
        ggq_ref[...] += gq_part
        ggk_ref[...] += gk_part

    hspec = pl.BlockSpec((H, tm, DH), lambda i: (0, i, 0))
    fixed = pl.BlockSpec((1, DA), lambda i: (0, 0))
    return _hbm_call(
        body, name="qk_bwd", out_shape=(SDS((8, S, DA), bf16), SDS((1, DA), f32), SDS((1, DA), f32)), grid=(S // tm,),
        in_specs=[hspec, hspec, hspec, hspec, pl.BlockSpec((tm, 2 * DA), lambda i: (i, 0)), fixed, fixed]
        + _rope_specs(tm) + [ANY_SPEC],
        out_specs=(pl.BlockSpec((3, tm, DA), lambda i: (0, i, 0)), fixed, fixed), input_output_aliases={11: 0},
        compiler_params=_cp(40, dimension_semantics=("arbitrary",)),
    )(dqr, dqp, dkr, dvh, p, gq, gk, *rope, dp8)


def _ctx_bwd(dkc, dvc, pc, gk):
    def body(dkc_ref, dvc_ref, p_ref, gk_ref, dk_ref, dv_ref, ggk_ref):
        ones_bd = _head_ones()
        dk, gk_part = _head_norm_bwd(p_ref[:, 0:DA], gk_ref[...], _merge_heads(dkc_ref), ones_bd)
        dk_ref[...] = dk.astype(bf16)
        dv_ref[...] = _merge_heads(dvc_ref).astype(bf16)
        ggk_ref[...] = gk_part

    piece = SDS((L, DA), bf16)
    return _hbm_call(
        body, name="ctx_bwd", out_shape=(piece, piece, SDS((1, DA), f32)), in_specs=[VMEM_SPEC] * 4,
        out_specs=(VMEM_SPEC,) * 3,
    )(dkc, dvc, pc, gk)


def _grad_w_in(h, dp8, hc, dkc_raw, dvc_m, half, name, after=None):
    def body(half_ref, h_ref, p_ref, hc_ref, dk_ref, dv_ref, after_ref, g_ref):
        j = pl.program_id(0)
        hv = h_ref[...]
        g_ref[:, 0:DA] = _tn(hv, p_ref[0])
        g_ref[:, DA:2 * DA] = _tn(hv, p_ref[1])

        @pl.when(j == 0)
        def _():
            g_ref[:, DA:2 * DA] += _tn(hc_ref[...], dk_ref[...])

        @pl.when(j == 1)
        def _():
            g_ref[:, 0:DA] += _tn(hc_ref[...], dv_ref[...])

    fixed = lambda j, s: (0, 0)
    hd = D // 2
    grid_spec = pltpu.PrefetchScalarGridSpec(
        num_scalar_prefetch=1, grid=(4,),
        in_specs=[pl.BlockSpec((S, hd), lambda j, s: (0, s[0])), pl.BlockSpec((2, S, DA), lambda j, s: (j, 0, 0)),
                  pl.BlockSpec((L, hd), lambda j, s: (0, s[0])), pl.BlockSpec((L, DA), fixed),
                  pl.BlockSpec((L, DA), fixed), ANY_SPEC],
        out_specs=pl.BlockSpec((None, hd, D), lambda j, s: (j, 0, 0)))
    return _hbm_call(
        body, name=name, out_shape=SDS((4, hd, D), f32), grid_spec=grid_spec, compiler_params=_cp(40),
    )(half, h, dp8, hc, dkc_raw, dvc_m, half if after is None else after)


def _grad_w_in_pair_sum(h, dp8, hc, dkc_raw, dvc_m, half, sent, landed, ssem, rsem):
    def body(half_ref, h_ref, p_ref, hc_ref, dk_ref, dv_ref, sent_ref, land_ref, ssem_ref, rsem_ref,
             t32_ref, tb_ref, buf, lsem):
        j = pl.program_id(0)
        hv = h_ref[...]
        t32_ref[:, 0:DA] = _tn(hv, p_ref[0])
        x, y, c = _my_pos()
        arrival = pltpu.make_async_remote_copy(src_ref=sent_ref.at[j], dst_ref=land_ref.at[j], send_sem=ssem_ref.at[j],
                                               recv_sem=rsem_ref.at[j], device_id=(x, y, 1 - c), device_id_type=MESH)
        arrival.wait_recv()
        arrival.wait_send()
        load = pltpu.make_async_copy(land_ref.at[j], buf, lsem)
        load.start()
        t32_ref[:, DA:2 * DA] = _tn(hv, p_ref[1])

        @pl.when(j == 0)
        def _():
            t32_ref[:, DA:2 * DA] += _tn(hc_ref[...], dk_ref[...])

        @pl.when(j == 1)
        def _():
            t32_ref[:, 0:DA] += _tn(hc_ref[...], dv_ref[...])

        load.wait()
        t = t32_ref[...] + buf[...]
        t32_ref[...] = t
        tb_ref[...] = t.astype(bf16)

    fixed = lambda j, s: (0, 0)
    hd = D // 2
    out_spec = pl.BlockSpec((None, hd, D), lambda j, s: (j, 0, 0))
    grid_spec = pltpu.PrefetchScalarGridSpec(
        num_scalar_prefetch=1, grid=(4,),
        in_specs=[pl.BlockSpec((S, hd), lambda j, s: (0, s[0])), pl.BlockSpec((2, S, DA), lambda j, s: (j, 0, 0)),
                  pl.BlockSpec((L, hd), lambda j, s: (0, s[0])), pl.BlockSpec((L, DA), fixed),
                  pl.BlockSpec((L, DA), fixed), HBM_SPEC, HBM_SPEC, SEM_SPEC, SEM_SPEC],
        out_specs=(out_spec, out_spec), scratch_shapes=[pltpu.VMEM((hd, D), f32), pltpu.SemaphoreType.DMA])
    return _hbm_call(
        body, name="grad_w_in_pair_sum", out_shape=(SDS((4, hd, D), f32), SDS((4, hd, D), bf16)), grid_spec=grid_spec,
        compiler_params=_cp(40, has_side_effects=DATAFLOW),
    )(half, h, dp8, hc, dkc_raw, dvc_m, sent, landed, ssem, rsem)


def _norm_mod_bwd(x, dh, g, scale):
    r = lax.rsqrt(jnp.mean(x * x, axis=-1, keepdims=True) + RMS_EPS)
    xh = x * r
    y = xh * g
    dshift = jnp.sum(dh, axis=0, keepdims=True)
    dscale = jnp.sum(dh * y, axis=0, keepdims=True)
    dyn = dh * (1.0 + scale)
    dg = jnp.sum(dyn * xh, axis=0, keepdims=True)
    gdy = dyn * g
    dx = r * (gdy - xh * jnp.mean(xh * gdy, axis=-1, keepdims=True))
    return dx, dshift, dscale, dg


def _dh_grad_x(dp8, w4, xx, dy, norm_g, mrow, b_ada, after=None):
    tm = 512

    def body(p_ref, w_ref, x_ref, dy_ref, g_ref, m_ref, b_ref, after_ref, gx_ref, dsh_ref, dsc_ref, dg_ref):
        @pl.when(pl.program_id(0) == 0)
        def _():
            dsh_ref[...] = jnp.zeros_like(dsh_ref)
            dsc_ref[...] = jnp.zeros_like(dsc_ref)
            dg_ref[...] = jnp.zeros_like(dg_ref)

        dh = None
        for j in range(4):
            for half in range(2):
                term = _tt(p_ref[2 * j + half], w_ref[j, :, half * DA:(half + 1) * DA])
                dh = term if dh is None else dh + term
        scale = m_ref[:, D:2 * D] + b_ref[:, D:2 * D]
        dx, dshift, dscale, dg = _norm_mod_bwd(x_ref[...], dh, g_ref[...], scale)
        gx_ref[...] = dy_ref[...] + dx
        dsh_ref[...] += dshift
        dsc_ref[...] += dscale
        dg_ref[...] += dg

    row = lambda i: (i, 0)
    fixed = lambda i: (0, 0)
    vec = SDS((1, D), f32)
    return _hbm_call(
        body, name="dh_grad_x", out_shape=(SDS((S, D), f32), vec, vec, vec), grid=(S // tm,),
        in_specs=[pl.BlockSpec((8, tm, DA), lambda i: (0, i, 0)), pl.BlockSpec((4, D, D), lambda i: (0, 0, 0)),
                  pl.BlockSpec((tm, D), row), pl.BlockSpec((tm, D), row), pl.BlockSpec((1, D), fixed),
                  pl.BlockSpec((1, 3 * D), fixed), pl.BlockSpec((1, 3 * D), fixed), ANY_SPEC],
        out_specs=(pl.BlockSpec((tm, D), row), pl.BlockSpec((1, D), fixed), pl.BlockSpec((1, D), fixed),
                   pl.BlockSpec((1, D), fixed)),
        compiler_params=_cp(56, dimension_semantics=("arbitrary",)),
    )(dp8, w4, xx, dy, norm_g, mrow, b_ada, b_ada if after is None else after)


def _dhc_sums(dkc_raw, dvc_m, w4, ctx2, norm_g, mrow_c, b_ada, after=None):
    def body(dk_ref, dv_ref, w0_ref, w1_ref, x_ref, g_ref, m_ref, b_ref, after_ref, dsh_ref, dsc_ref, dg_ref):
        dh = _tt(dk_ref[...], w0_ref[:, DA:2 * DA]) + _tt(dv_ref[...], w1_ref[:, 0:DA])
        scale = m_ref[:, D:2 * D] + b_ref[:, D:2 * D]
        _, dshift, dscale, dg = _norm_mod_bwd(x_ref[...], dh, g_ref[...], scale)
        dsh_ref[...] = dshift
        dsc_ref[...] = dscale
        dg_ref[...] = dg

    fixed = lambda i: (0, 0)
    vec = SDS((1, D), f32)
    vspec = pl.BlockSpec((1, D), fixed)
    return _hbm_call(
        body, name="dhc_sums", out_shape=(vec, vec, vec), grid=(1,),
        in_specs=[pl.BlockSpec((L, DA), fixed), pl.BlockSpec((L, DA), fixed),
                  pl.BlockSpec((None, D, D), lambda i: (0, 0, 0)), pl.BlockSpec((None, D, D), lambda i: (1, 0, 0)),
                  pl.BlockSpec((L, D), fixed), vspec, pl.BlockSpec((1, 3 * D), fixed), pl.BlockSpec((1, 3 * D), fixed),
                  ANY_SPEC],
        out_specs=(vspec, vspec, vspec), compiler_params=_cp(32),
    )(dkc_raw, dvc_m, w4, w4, ctx2, norm_g, mrow_c, b_ada, b_ada if after is None else after)


def _rope_tables():
    nf = DH // 4
    inv = np.float32(ROPE_THETA) ** (-np.arange(nf, dtype=np.float32) / np.float32(nf))
    ang_c = np.arange(GW, dtype=np.float32)[:, None] * inv
    ang_r = np.arange(ROWS, dtype=np.float32)[:, None] * inv
    zc, zr = np.zeros((GW, 2 * nf), np.float32), np.zeros((ROWS, 2 * nf), np.float32)
    ct_cos = np.tile(np.concatenate([zc, np.cos(ang_c), np.cos(ang_c)], axis=1), (1, H))
    ct_sin = np.tile(np.concatenate([zc, -np.sin(ang_c), np.sin(ang_c)], axis=1), (1, H))
    rt_cos = np.tile(np.concatenate([np.cos(ang_r), np.cos(ang_r), zr], axis=1), (1, H))
    rt_sin = np.tile(np.concatenate([-np.sin(ang_r), np.sin(ang_r), zr], axis=1), (1, H))
    rep8 = lambda t: np.ascontiguousarray(np.broadcast_to(t[:, None, :], (ROWS, 8, DA))).reshape(ROWS * 8, DA)
    return tuple(jnp.asarray(t, f32) for t in (ct_cos, rep8(rt_cos), ct_sin, rep8(rt_sin)))


def _local_step(xx, ctx2, tgt, mrow, mrow_c, b_ada, norm_g, weights, q_norm_g, k_norm_g, rpb2, conv_w_full, conv_b,
                hooks=None):
    hooks = hooks or {}
    gq = jnp.tile(q_norm_g, (1, H))
    gk = jnp.tile(k_norm_g, (1, H))
    rope = _rope_tables()

    h = _prenorm(xx, norm_g, mrow, b_ada, 256, "prenorm_x", after=weights.get("started"))
    jv = weights["jvec"]
    p = _in_proj_own(h, weights["own"], jv)
    btb = _bias_tiles(rpb2, after=p)
    w4, started = weights["near"](btb)
    p = _in_proj_block(h, w4, p, weights["first"], "in_proj_near", after=started)
    w4 = weights["near2"](w4, p)
    p = _in_proj_block(h, w4, p, weights["second"], "in_proj_near2")
    ctx_norm = {}

    def filler(token):
        ctx_norm["hc"] = _prenorm(ctx2, norm_g, mrow_c, b_ada, L // 2, "prenorm_ctx", after=token)
        return ctx_norm["hc"]

    w4, started = weights["far"](w4, p, filler)
    hc = ctx_norm["hc"]
    p = _in_proj_block(h, w4, p, jv ^ 3, "in_proj_far", after=started)
    pc = _ctx_proj(hc, w4)
    qr, qp, kr, vh = _qk_prep(p, gq, gk, rope)
    kc, vc = _ctx_prep(pc, gk)
    o = _attn_fwd(qr, qp, kr, vh, kc, vc, btb)
    started = weights["out_arrived"](o) if "out_arrived" in weights else None
    conv_g = _conv_fwd(p, conv_w_full, conv_b, after=started)
    w_out_full = weights["out"](conv_g)
    dy, dconv, dp8, do, g_w_out, dgate, loss_sum = _out_proj_loss(o, p, conv_g, w_out_full, xx, tgt, mrow, b_ada)
    started = hooks["g_w_out"](g_w_out) if "g_w_out" in hooks else None
    dp8, g_conv_w, g_conv_b = _conv_bwd(dconv, p, conv_w_full, conv_b, dp8, after=started)
    started = hooks["after_conv"](dp8) if "after_conv" in hooks else None
    dqr, dqp, dkr, dvh, dkc, dvc, dbtb = _attn_bwd(qr, qp, kr, vh, kc, vc, btb, do, after=started)
    dp8, g_gq, g_gk = _qk_bwd(dqr, dqp, dkr, dvh, p, gq, gk, rope, dp8)
    dkc_raw, dvc_m, g_gk_c = _ctx_bwd(dkc, dvc, pc, gk)
    first = hooks.get("first_half", jnp.zeros((1,), i32))
    g_first = _grad_w_in(h, dp8, hc, dkc_raw, dvc_m, first, "grad_w_in_first")
    started = hooks["g_w_in_first"](g_first) if "g_w_in_first" in hooks else None
    if "w_in_pair_sum" in hooks:
        g_second = None
        started = hooks["w_in_pair_sum"](functools.partial(_grad_w_in_pair_sum, h, dp8, hc, dkc_raw, dvc_m, 1 - first))
    else:
        g_second = _grad_w_in(h, dp8, hc, dkc_raw, dvc_m, 1 - first, "grad_w_in_second", after=started)
    dshift_c, dscale_c, dng_c = _dhc_sums(dkc_raw, dvc_m, w4, ctx2, norm_g, mrow_c, b_ada, after=started)
    g_rpb = _bias_bwd(dbtb, after=dshift_c)
    grad_x, dshift, dscale, dng = _dh_grad_x(dp8, w4, xx, dy, norm_g, mrow, b_ada, after=g_rpb)
    return dict(loss_sum=loss_sum, grad_x=grad_x, g_w_in=(g_first, g_second), g_w_out=g_w_out, g_conv_w=g_conv_w,
                g_conv_b=g_conv_b, g_rpb=g_rpb, g_gq=g_gq, g_gk=g_gk, g_gk_c=g_gk_c, dshift=dshift, dscale=dscale,
                dgate=dgate, dng=dng, dshift_c=dshift_c, dscale_c=dscale_c, dng_c=dng_c)


def _pair_sum_w_out(g, r, cvec):
    hr = D // 8

    def add(own, other):
        t = own + other
        return t, t.astype(bf16)

    mine = lambda ref, q, c: ref.at[pl.ds(pl.multiple_of((2 * q + c) * hr, hr), hr)]
    block = lambda ref, q, c: ref.at[q]
    return _stream_call(add, 4, [(g, (hr, D), mine), (r, (hr, D), block)],
                        [((4, hr, D), f32, (hr, D), block), ((4, hr, D), bf16, (hr, D), block)], cvec, "pair_sum_w_out")


def _chip_sum(t32, r2, jvec, name):
    rows = t32.shape[1]
    tr = min(rows // 2, 128)
    band = lambda k: slice(k * tr, (k + 1) * tr)
    total =lambda t, r: (((t + r[0].astype(f32)) + r[1].astype(f32)) + r[2].astype(f32),)
    return _stream_call(
        total, rows // tr,
        [(t32, (tr, D), lambda ref, k, j: ref.at[j, band(k)]), (r2, (3, tr, D), lambda ref, k, j: ref.at[:, band(k)])],
        [((rows, D), f32, (tr, D), lambda ref, k, j: ref.at[band(k)])], jvec, name)[0]


_PK = {}
_off = 0
for _name, _rows in (("dm", 24), ("dmc", 24), ("dng", 8), ("dng_c", 8), ("gq", 8), ("gk", 8), ("gk_c", 8),
                     ("rpb", H * N_DR), ("conv_b", 8), ("conv_w", 16), ("loss", 8)):
    _PK[_name] = (_off, _off + _rows)
    _off += _rows
PK_ROWS = _off
RS_B_ADA, RS_NORM_G, RS_GQ, RS_GK, RS_RPB, RS_CONV_B, RS_CONV_W, RS_DMC, RS_LOSS, RS_ROWS = (
    0, 24, 32, 40, 48, 168, 176, 192, 216, 224)


def _small_reduce(gathered):
    def body(g_ref, o_ref, dm_ref):
        a0 = _PK["dm"][0]
        dm_ref[...] = jnp.zeros_like(dm_ref)
        for b in range(8):
            for i in range(24):
                dm_ref[b:b + 1, 128 * i:128 * (i + 1)] = g_ref[b, a0 + i:a0 + i + 1, :]
        tot = g_ref[0]
        for b in range(1, 8):
            tot = tot + g_ref[b]

        def rows(name):
            a, z = _PK[name]
            return tot[a:z]

        o_ref[RS_B_ADA:RS_B_ADA + 24] = rows("dm") + rows("dmc")
        o_ref[RS_NORM_G:RS_NORM_G + 8] = rows("dng") + rows("dng_c")
        gq = jnp.broadcast_to(jnp.sum(rows("gq"), axis=0, keepdims=True), (8, 128))
        gk = jnp.broadcast_to(jnp.sum(rows("gk") + rows("gk_c"), axis=0, keepdims=True), (8, 128))
        o_ref[RS_GQ:RS_GQ + 8] = gq + pltpu.roll(gq, DH, 1)
        o_ref[RS_GK:RS_GK + 8] = gk + pltpu.roll(gk, DH, 1)
        o_ref[RS_RPB:RS_RPB + H * N_DR] = rows("rpb")
        o_ref[RS_CONV_B:RS_CONV_B + 8] = rows("conv_b")
        o_ref[RS_CONV_W:RS_CONV_W + 16] = rows("conv_w")
        dmc = rows("dmc")
        o_ref[RS_DMC:RS_DMC + 24] = dmc
        o_ref[RS_LOSS:RS_LOSS + 8] = rows("loss")
        for i in range(24):
            dm_ref[8:9, 128 * i:128 * (i + 1)] = dmc[i:i + 1]

    return _hbm_call(body, name="small_reduce", out_shape=(SDS((RS_ROWS, 128), f32), SDS((16, 3 * D), f32)),
                     in_specs=[VMEM_SPEC], out_specs=(VMEM_SPEC, VMEM_SPEC))(gathered)


def _w_ada_grad(sc16, dm16, w_ada_shard, jvec):
    ncol = w_ada_shard.shape[1]

    def body(j_ref, sc_ref, dm_ref, w_ref, g_ref, part_ref):
        dm = dm_ref[...]
        g_ref[...] = lax.dot_general(sc_ref[...], dm, (((0,), (0,)), ((), ())), precision=HIGHEST,
                                     preferred_element_type=f32)
        part_ref[...] = lax.dot_general(dm[8:16], w_ref[...], (((1,), (1,)), ((), ())), precision=HIGHEST,
                                        preferred_element_type=f32)

    fixed = lambda i, j: (0, 0)
    grid_spec = pltpu.PrefetchScalarGridSpec(
        num_scalar_prefetch=1, grid=(1,),
        in_specs=[pl.BlockSpec((16, D), fixed), pl.BlockSpec((16, ncol), lambda i, j: (0, j[0])),
                  pl.BlockSpec((D, ncol), fixed)],
        out_specs=(pl.BlockSpec((D, ncol), fixed), pl.BlockSpec((8, D), fixed)))
    return _pallas_call(body, name="w_ada_grad", out_shape=(SDS((D, ncol), f32), SDS((8, D), f32)),
                        grid_spec=grid_spec, compiler_params=_cp(40))(jvec, sc16, dm16, w_ada_shard)


def _c_ctx_grad(parts4, c_ctx):
    def body(p_ref, c_ref, o_ref):
        tot = ((p_ref[0] + p_ref[1]) + p_ref[2]) + p_ref[3]
        o_ref[...] = tot[0:1] * _dsilu(c_ref[...].reshape(1, D))

    return _pallas_call(body, name="c_ctx_grad", out_shape=SDS((1, D), f32), in_specs=[VMEM_SPEC, VMEM_SPEC],
                        out_specs=VMEM_SPEC)(parts4, c_ctx)


def _adamw(w, g, m, v, name, after=None):
    return _adamw_stream(w, [g], m, v, None, name, after)


def _adamw_halves(w, g_mine, g_other, m, v, cvec, name, after=None):
    return _adamw_stream(w, [g_mine, g_other], m, v, cvec, name, after)


def _adamw_stream(w, g_parts, m, v, cvec, name, after):
    rows, cols = w.shape
    parts = len(g_parts)
    span = rows // parts
    chunk = min(128, span // 2)
    per = span // chunk
    outs_g = parts == 2
    pin = rows * cols * 4 > (1 << 20)

    def body(*refs):
        c_ref, w_ref = refs[0], refs[1]
        g_refs = refs[2:2 + parts]
        m_ref, v_ref = refs[2 + parts], refs[3 + parts]
        out_refs = refs[5 + parts:5 + parts + 3 + outs_g]
        bw, bg, bm, bv, bd, sem_in, sem_out = refs[5 + parts + 3 + outs_g:]
        c = c_ref[0]
        loads, stores = [], []
        for p in range(parts):
            first = (c if p == 0 else 1 - c) * span if parts == 2 else 0
            for kk in range(per):
                r = pl.ds(pl.multiple_of(first + kk * chunk, chunk), chunk)
                g_src = g_refs[p].at[kk * chunk:(kk + 1) * chunk]
                cps = [pltpu.make_async_copy(src, dst.at[r], sem_in.at[a, p, kk])
                       for a, (src, dst) in enumerate(((w_ref.at[r], bw), (g_src, bg), (m_ref.at[r], bm),
                                                       (v_ref.at[r], bv)))]
                for cp in cps:
                    cp.start()
                loads.append((r, p, kk, cps))
        for r, p, kk, cps in loads:
            for cp in cps:
                cp.wait()
            bd[r], bm[r], bv[r] = _adam_math(bw[r], bg[r], bm[r], bv[r])
            for a, (src, dst) in enumerate(zip(([bg] if outs_g else []) + [bd, bm, bv], out_refs)):
                cp = pltpu.make_async_copy(src.at[r], dst.at[r], sem_out.at[a, p, kk])
                cp.start()
                stores.append(cp)
        for cp in stores:
            cp.wait()

    shp = pltpu.HBM((rows, cols), f32) if pin else SDS((rows, cols), f32)
    spec = HBM_SPEC if pin else ANY_SPEC
    buf = pltpu.VMEM((rows, cols), f32)
    args = [pltpu.with_memory_space_constraint(a, pltpu.HBM) if pin else a for a in (w, *g_parts, m, v)]
    cvec = jnp.zeros((1,), i32) if cvec is None else cvec
    after_big = after is not None and after.size * 4 > (1 << 20)
    after = pltpu.with_memory_space_constraint(after, pltpu.HBM) if after_big else after
    return _pallas_call(
        body, name=name, out_shape=(shp,) * (3 + outs_g),
        in_specs=[SMEM_SPEC] + [spec] * (3 + parts) + [HBM_SPEC if after_big else ANY_SPEC],
        out_specs=(spec,) * (3 + outs_g),
        scratch_shapes=[buf] * 5 + [pltpu.SemaphoreType.DMA((4, parts, per)), pltpu.SemaphoreType.DMA((4, parts, per))],
        compiler_params=_cp(5 * rows * cols * 4 // (1 << 20) + 8),
    )(cvec, *args, cvec if after is None else after)


def _adam_math(w, g, m, v):
    m2 = ADAM_B1 * m + (1.0 - ADAM_B1) * g
    v2 = ADAM_B2 * v + (1.0 - ADAM_B2) * jnp.square(g)
    m_hat = m2 / (1.0 - ADAM_B1 ** ADAM_STEP)
    v_hat = v2 / (1.0 - ADAM_B2 ** ADAM_STEP)
    return -ADAM_LR * (m_hat / (jnp.sqrt(v_hat) + ADAM_EPS) + ADAM_WD * w), m2, v2


def _adamw_small(red, g_c_ctx, jvec, ws, ms, vs):
    n = len(ws)

    def body(*refs):
        red_ref, gc_ref, j_ref = refs[:3]
        w_refs, m_refs, v_refs = refs[3:3 + n], refs[3 + n:3 + 2 * n], refs[3 + 2 * n:3 + 3 * n]
        outs = refs[3 + 3 * n:]
        g_out, d_out, m_out, v_out = outs[:n], outs[n:2 * n], outs[2 * n:3 * n], outs[3 * n:]
        chip = j_ref[0]
        lanes = lambda i: (slice(None), slice(128 * i, 128 * (i + 1)))
        row = lambda r0, i: (lambda: red_ref[r0 + i:r0 + i + 1, :])
        whole = (slice(None), slice(None))
        chunks = [
            [((slice(None),), lambda: gc_ref[...].reshape(D))],
            [(lanes(i), row(RS_B_ADA, i)) for i in range(3 * D // 128)],
            [(lanes(i), row(RS_NORM_G, i)) for i in range(D // 128)],
            [(whole, lambda: red_ref[RS_GQ:RS_GQ + 1, 0:DH])],
            [(whole, lambda: red_ref[RS_GK:RS_GK + 1, 0:DH])],
            [((dr,), (lambda dr=dr: red_ref[pl.ds(RS_RPB + dr, H, stride=N_DR), 0:N_DC])) for dr in range(N_DR)],
            [((r,), (lambda r=r: red_ref[pl.ds(RS_CONV_W + 4 * r + chip, 1), :])) for r in range(3)],
            [(lanes(i), row(RS_CONV_B, i)) for i in range(DC // 128)],
        ]
        for a in range(n):
            for idx, grad in chunks[a]:
                g = grad()
                d, m2, v2 = _adam_math(w_refs[a][idx], g, m_refs[a][idx], v_refs[a][idx])
                g_out[a][idx] = g
                d_out[a][idx] = d
                m_out[a][idx] = m2
                v_out[a][idx] = v2

    shapes = [SDS(w.shape, f32) for w in ws]
    res = _pallas_call(body, name="adamw_small", out_shape=shapes * 4,
                       in_specs=[VMEM_SPEC, VMEM_SPEC, SMEM_SPEC] + [VMEM_SPEC] * (3 * n),
                       out_specs=[VMEM_SPEC] * (4 * n))(red, g_c_ctx, jvec, *ws, *ms, *vs)
    return [list(res[k * n:(k + 1) * n]) for k in range(4)]


def _rows128(a):
    return a.reshape(-1, 128)


def kernel(x, c, ctx, c_ctx, w_ada, b_ada, norm_g, w_in, q_norm_g, k_norm_g, rpb, conv_w, conv_b, w_out, loss_target, m_c_ctx, m_w_ada, m_b_ada, m_norm_g, m_w_in, m_q_norm_g, m_k_norm_g, m_rpb, m_conv_w, m_conv_b, m_w_out, v_c_ctx, v_w_ada, v_b_ada, v_norm_g, v_w_in, v_q_norm_g, v_k_norm_g, v_rpb, v_conv_w, v_conv_b, v_w_out):
    xi, yi, ci = lax.axis_index("x"), lax.axis_index("y"), lax.axis_index("c")
    dev = 4 * xi + 2 * yi + ci
    chip = 2 * xi + yi
    cvec = jnp.reshape(ci, (1,)).astype(i32)
    jvec = jnp.reshape(chip, (1,)).astype(i32)
    w_ada_s = w_ada[0]
    ncol = w_ada_s.shape[1]

    gc = _split_start([_to_slot(c.reshape(8, 128), 8, dev)], [], 7, _gather8_copies, "gather_c_start")
    wo4c = _cast_to_slot(w_out[0], jvec, "cast_w_out", after=gc[3])
    w4c = _cast_to_slot(w_in[0], jvec, "cast_w_in", after=wo4c)
    (c8,), _ = _split_wait(gc[0], gc[1], [gc[2]], [], w4c, _gather8_copies, "gather_c_wait")
    cc = jnp.concatenate([c8.reshape(8, D), c_ctx.reshape(1, D), jnp.zeros((7, D), f32)], axis=0)
    m_shard, sc16 = _adaln_shard(cc, w_ada_s)

    conv_w_pad = jnp.pad(conv_w[0], ((0, 5), (0, 0)))
    gm = _split_start([_to_slot(m_shard, 4, chip), _to_slot(conv_w_pad, 4, chip)], [], 6, _gather4_copies,
                      "gather_mod_start")

    all_k = [(0, 0, 0), (0, 0, 1), (0, 0, 2)]
    sem_a, rem_a, w4s, token = _split_start([w4c], [], 1, _near_copies, "weights_near_start", after=gm[4])
    (m4, cw4), _ = _split_wait(gm[0], gm[1], [gm[2], gm[3]], [], token, _gather4_copies, "gather_mod_wait")
    m_full = jnp.transpose(m4, (1, 0, 2)).reshape(16, 4 * ncol)
    mrow = lax.dynamic_slice(m_full, (dev, 0), (1, 3 * D))
    mrow_c = m_full[8:9]
    conv_w_full = jnp.transpose(cw4[:, 0:3, :], (1, 0, 2)).reshape(3, DC)
    waves = {}

    def near(after):
        (w4w,), _ = _split_wait(sem_a, rem_a, [w4s], [], after, _near_copies, "weights_near_wait")
        sem_b, rem_b, w4b, started = _split_start([w4w], [], 2, _pass_relay_copies, "weights_pass_start")
        waves["pass"] = (sem_b, rem_b)
        return w4b, started

    def near2(w4, after):
        (w4w,), _ = _split_wait(*waves["pass"], [w4], [], after, _pass_copy, "weights_pass_wait")
        return w4w

    def far(w4, after, filler):
        (w4w,), _ = _split_wait(*waves["pass"], [w4], [], after, _relay_copy, "weights_far_wait")
        w4x, sem_c, rem_c, wo4s, started = _forward_then_start(w4w, wo4c, all_k, "weights_far_forward_out_start")
        (w4f,), _ = _split_wait(sem_c, rem_c, [w4x], [], filler(started), _diag_forward_copy,
                                "weights_far_forward_wait")
        waves["out"] = (sem_c, rem_c, wo4s)
        return w4f, started

    def w_out_arrived(after):
        sem_c, rem_c, wo4s = waves["out"]
        (wow,) = _halves_wait(sem_c, rem_c, [wo4s], after, all_k, "weights_out_wait")
        sem_d, rem_d, wof, started = _split_start([wow], [], 3, _forward_copies, "weights_out_forward_start")
        waves["out_forward"] = (sem_d, rem_d, wof)
        return started

    def w_out_gathered(after):
        sem_d, rem_d, wof = waves["out_forward"]
        (wo,), _ = _split_wait(sem_d, rem_d, [wof], [], after, _forward_copies, "weights_out_forward_wait")
        return wo.reshape(D, D)

    weights = dict(own=w4s, jvec=jvec, started=token, first=jvec ^ (1 + cvec), second=jvec ^ (2 - cvec),
                   near=near, near2=near2, far=far, out_arrived=w_out_arrived, out=w_out_gathered)

    exchange = _exchange_copies
    pending = {}

    def on_g_w_out(g_w_out):
        out = _split_start([g_w_out], [SDS((4, D // 8, D), f32)], 4, exchange, "grad_out_pair_start")
        pending["ex_out"] = out
        return out[4]

    def after_conv(dp8):
        ssem_o, rsem_o, g_o, land_o, _ = pending["ex_out"]
        (g_o,), (ex_o,) = _split_wait(ssem_o, rsem_o, [g_o], [land_o], dp8, exchange, "grad_out_pair_wait")
        to32, tob = _pair_sum_w_out(g_o, ex_o, cvec)
        out = _split_start([tob], [SDS((3, D // 8, D), bf16)], 3, _scatter_copies, "grad_out_chip_start")
        pending["sc_out"] = (out, to32)
        return out[4]

    def on_g_w_in_first(g_first):
        out = _split_start([g_first], [SDS((4, D // 2, D), f32)], 4, _block_exchange_copies, "grad_pair_start")
        pending["ex"] = (out[0], out[1], [out[2]], [out[3]])
        return out[4]

    def on_w_in_pair_sum(pair_sum):
        ex_ssem, ex_rsem, ex_srcs, ex_lands = pending["ex"]
        t32, tb = pair_sum(ex_srcs[0], ex_lands[0], ex_ssem, ex_rsem)
        out = _split_start([tb], [SDS((3, D // 2, D), bf16)], 3, _scatter_copies, "grad_chip_start")
        pending["sc_in"] = (out, t32)
        return out[4]

    r = _local_step(x[0], ctx[0], loss_target[0], mrow, mrow_c, b_ada, norm_g, weights, q_norm_g, k_norm_g,
                    rpb[0], conv_w_full, conv_b,
                    dict(g_w_out=on_g_w_out, after_conv=after_conv, first_half=1 - cvec,
                         g_w_in_first=on_g_w_in_first, w_in_pair_sum=on_w_in_pair_sum))
    sc_in, t32 = pending["sc_in"]
    _, (r2,) = _split_wait(sc_in[0], sc_in[1], [sc_in[2]], [sc_in[3]], r["dng"], _scatter_copies, "grad_chip_wait")
    u_in = _chip_sum(t32, r2, jvec, "chip_sum_w_in")
    sc_o, to32 = pending["sc_out"]
    _, (ro2,) = _split_wait(sc_o[0], sc_o[1], [sc_o[2]], [sc_o[3]], u_in, _scatter_copies, "grad_out_chip_wait")
    u_out = _chip_sum(to32, ro2, jvec, "chip_sum_w_out")
    swap = _split_start([u_in, u_out], [SDS(u_in.shape, f32), SDS(u_out.shape, f32)], 2, _swap_copies,
                        "grad_pair_swap_start")

    dm = jnp.concatenate([r["dshift"], r["dscale"], r["dgate"]], axis=1)
    dmc = jnp.concatenate([r["dshift_c"], r["dscale_c"], jnp.zeros((1, D), f32)], axis=1)
    pack_parts = [_rows128(dm), _rows128(dmc), _rows128(r["dng"]), _rows128(r["dng_c"]), _rows128(r["g_gq"]),
                  _rows128(r["g_gk"]), _rows128(r["g_gk_c"]), r["g_rpb"].reshape(H * N_DR, 128),
                  _rows128(r["g_conv_b"]), _rows128(r["g_conv_w"][0:3]), jnp.pad(r["loss_sum"], ((0, 0), (0, 127)))]
    pack = jnp.concatenate([jnp.pad(p, ((0, -p.shape[0] % 8), (0, 0))) for p in pack_parts], axis=0)
    assert pack.shape[0] == PK_ROWS
    gs = _split_start([_to_slot(pack, 8, dev)], [], 7, _gather8_copies, "gather_small_start", after=swap[6])
    (u_in, u_out), (o_in, o_out) = _split_wait(swap[0], swap[1], swap[2:4], swap[4:6], gs[3], _swap_copies,
                                               "grad_pair_swap_wait")
    g_w_in_s, d_w_in, nm_w_in, nv_w_in = _adamw_halves(w_in[0], u_in, o_in, m_w_in[0], v_w_in[0], cvec, "adamw_w_in",
                                                       after=gs[3])
    g_w_out_s, d_w_out, nm_w_out, nv_w_out = _adamw_halves(w_out[0], u_out, o_out, m_w_out[0], v_w_out[0], cvec,
                                                           "adamw_w_out", after=nm_w_in)
    (gathered,), _ = _split_wait(gs[0], gs[1], [gs[2]], [], nm_w_out, _gather8_copies, "gather_small_wait")
    red, dm16 = _small_reduce(gathered)
    loss = red[RS_LOSS, 0] * (0.5 / D)

    g_w_ada_s, cpart = _w_ada_grad(sc16, dm16, w_ada_s, jvec)
    gcp = _split_start([_to_slot(cpart, 4, chip)], [], 3, _gather4_copies, "gather_c_ctx_parts_start")
    d_w_ada, nm_w_ada, nv_w_ada = _adamw(w_ada_s, g_w_ada_s, m_w_ada[0], v_w_ada[0], "adamw_w_ada", after=gcp[3])
    (cparts4,), _ = _split_wait(gcp[0], gcp[1], [gcp[2]], [], nm_w_ada, _gather4_copies, "gather_c_ctx_parts_wait")
    g_c_ctx = _c_ctx_grad(cparts4, c_ctx)

    t_rpb = lambda a: jnp.transpose(a, (0, 2, 1, 3)).reshape(N_DR, H, N_DC)
    t_cw = lambda a: jnp.transpose(a, (1, 0, 2))
    small = _adamw_small(
        red, g_c_ctx, jvec,
        [c_ctx, b_ada, norm_g, q_norm_g, k_norm_g, t_rpb(rpb), t_cw(conv_w), conv_b],
        [m_c_ctx, m_b_ada, m_norm_g, m_q_norm_g, m_k_norm_g, t_rpb(m_rpb), t_cw(m_conv_w), m_conv_b],
        [v_c_ctx, v_b_ada, v_norm_g, v_q_norm_g, v_k_norm_g, t_rpb(v_rpb), t_cw(v_conv_w), v_conv_b])
    for kind in small:
        kind[5] = jnp.transpose(kind[5].reshape(1, N_DR, H, N_DC), (0, 2, 1, 3))
        kind[6] = jnp.transpose(kind[6], (1, 0, 2))

    def ordered(kind, big_w_ada, big_w_in, big_w_out):
        s_c_ctx, s_b_ada, s_norm_g, s_q, s_k, s_rpb, s_conv_w, s_conv_b = small[kind]
        return [s_c_ctx, big_w_ada[None], s_b_ada, s_norm_g, big_w_in[None], s_q, s_k, s_rpb, s_conv_w,
                s_conv_b, big_w_out[None]]

    grads = ordered(0, g_w_ada_s, g_w_in_s, g_w_out_s)
    deltas = ordered(1, d_w_ada, d_w_in, d_w_out)
    new_m = ordered(2, nm_w_ada, nm_w_in, nm_w_out)
    new_v = ordered(3, nv_w_ada, nv_w_in, nv_w_out)
    return (loss, r["grad_x"][None], *grads, *deltas, *new_m, *new_v)
```
